```python
import jax, jax.numpy as jnp
from jax import lax
import numpy as np

D_MODEL = 1024
BATCH = 8
SEQ = 8192
DEPTH = 1

HEAD_DIM = 64
N_Q_HEADS = D_MODEL // HEAD_DIM
N_KV_HEADS = N_Q_HEADS // 8
GROUP = N_Q_HEADS // N_KV_HEADS
WINDOW = 128
BLOCK = 128
ATTN_WIDTH = N_Q_HEADS * HEAD_DIM
KV_WIDTH = N_KV_HEADS * HEAD_DIM
CONV_WIDTH = D_MODEL
CONV_K = 3
N_BRANCH = 2
D_FF = -(-8 * D_MODEL // (3 * 256)) * 256
IN_WIDTH = ATTN_WIDTH + 2 * KV_WIDTH + 3 * CONV_WIDTH + N_BRANCH * D_MODEL
N_MOD = 6
EPS = 1e-6

kernel_name = "hybrid_swa_sink_shortconv_gated_merge_adaln_block"


def rms_norm(x, g):
    xf = x.astype(jnp.float32)
    y = xf * lax.rsqrt(jnp.mean(xf * xf, axis=-1, keepdims=True) + EPS)
    return (y * g.astype(jnp.float32)).astype(x.dtype)


def with_prev_block(t):
    prev = jnp.pad(t, ((0, 0), (1, 0), (0, 0), (0, 0), (0, 0)))[:, :-1]
    return jnp.concatenate([prev, t], axis=2)


def sliding_window_sink_attention(q, k, v, sinks):
    B, T = q.shape[0], q.shape[1]
    nb = T // BLOCK
    qb = q.reshape(B, nb, BLOCK, N_KV_HEADS, GROUP, HEAD_DIM)
    kx = with_prev_block(k.reshape(B, nb, BLOCK, N_KV_HEADS, HEAD_DIM))
    vx = with_prev_block(v.reshape(B, nb, BLOCK, N_KV_HEADS, HEAD_DIM))
    s = jnp.einsum('bnqhgd,bnkhd->bnhgqk', qb, kx).astype(jnp.float32) * (HEAD_DIM ** -0.5)
    qpos = jnp.arange(BLOCK)[:, None] + BLOCK
    kpos = jnp.arange(2 * BLOCK)[None, :]
    rel = qpos - kpos
    band = (rel >= 0) & (rel < WINDOW)
    blk = jnp.arange(nb)[:, None, None]
    valid = band[None] & ((blk > 0) | (kpos[None] >= BLOCK))
    s = jnp.where(valid[None, :, None, None], s, -jnp.inf)
    sink = sinks.astype(jnp.float32).reshape(1, 1, N_KV_HEADS, GROUP, 1, 1)
    m = jnp.maximum(jnp.max(s, axis=-1, keepdims=True), sink)
    p = jnp.exp(s - m)
    denom = jnp.sum(p, axis=-1, keepdims=True) + jnp.exp(sink - m)
    p = (p / denom).astype(v.dtype)
    o = jnp.einsum('bnhgqk,bnkhd->bnqhgd', p, vx)
    return o.reshape(B, T, ATTN_WIDTH)


def causal_depthwise_conv(u, w):
    C = u.shape[-1]
    return lax.conv_general_dilated(
        u, w[:, None, :].astype(u.dtype), window_strides=(1,),
        padding=[(CONV_K - 1, 0)], dimension_numbers=('NWC', 'WIO', 'NWC'),
        feature_group_count=C)


def _fwd_setup_inputs(seed: int = 0) -> dict:
    key = jax.random.key(seed)
    ks = jax.random.split(key, 16)
    f32 = jnp.float32
    nrm = lambda k, shape, s: jax.random.normal(k, shape, f32) * s
    return {
        "x": nrm(ks[0], (BATCH, SEQ, D_MODEL), 1.0),
        "c": nrm(ks[1], (BATCH, D_MODEL), 1.0),
        "w_ada": nrm(ks[2], (DEPTH, D_MODEL, N_MOD * D_MODEL), D_MODEL ** -0.5),
        "b_ada": nrm(ks[3], (DEPTH, N_MOD * D_MODEL), 0.02),
        "g_mix": 1.0 + nrm(ks[4], (DEPTH, D_MODEL), 0.02),
        "w_in": nrm(ks[5], (DEPTH, D_MODEL, IN_WIDTH), D_MODEL ** -0.5),
        "b_in": nrm(ks[6], (DEPTH, IN_WIDTH), 0.02),
        "sinks": nrm(ks[7], (DEPTH, N_Q_HEADS), 0.5),
        "conv_w": nrm(ks[8], (DEPTH, CONV_K, CONV_WIDTH), CONV_K ** -0.5),
        "w_out": nrm(ks[9], (DEPTH, D_MODEL, D_MODEL), D_MODEL ** -0.5),
        "g_ffn": 1.0 + nrm(ks[10], (DEPTH, D_MODEL), 0.02),
        "w_ffn_in": nrm(ks[11], (DEPTH, D_MODEL, 2 * D_FF), D_MODEL ** -0.5),
        "w_ffn_out": nrm(ks[12], (DEPTH, D_FF, D_MODEL), D_FF ** -0.5),
        "g_final": 1.0 + nrm(ks[13], (D_MODEL,), 0.02),
    }


def _fwd_reference(x, c, w_ada, b_ada, g_mix, w_in, b_in, sinks, conv_w, w_out,
              g_ffn, w_ffn_in, w_ffn_out, g_final):
    splits = [ATTN_WIDTH,
              ATTN_WIDTH + KV_WIDTH,
              ATTN_WIDTH + 2 * KV_WIDTH,
              ATTN_WIDTH + 2 * KV_WIDTH + CONV_WIDTH,
              ATTN_WIDTH + 2 * KV_WIDTH + 2 * CONV_WIDTH,
              ATTN_WIDTH + 2 * KV_WIDTH + 3 * CONV_WIDTH,
              ATTN_WIDTH + 2 * KV_WIDTH + 3 * CONV_WIDTH + D_MODEL]
    for l in range(DEPTH):
        mod = (jax.nn.silu(c) @ w_ada[l] + b_ada[l])[:, None, :]
        sh1, sc1, ga1, sh2, sc2, ga2 = jnp.split(mod, N_MOD, axis=-1)

        h = rms_norm(x, g_mix[l]) * (1 + sc1) + sh1
        z = h @ w_in[l] + b_in[l]
        q, k, v, conv_b, conv_c, conv_x, gate_a, gate_c = jnp.split(z, splits, axis=-1)
        attn = sliding_window_sink_attention(q, k, v, sinks[l])
        conv = conv_b * causal_depthwise_conv(conv_c * conv_x, conv_w[l])
        merged = jax.nn.sigmoid(gate_a) * attn + jax.nn.sigmoid(gate_c) * conv
        x = x + ga1 * (merged @ w_out[l])

        h = rms_norm(x, g_ffn[l]) * (1 + sc2) + sh2
        gu = h @ w_ffn_in[l]
        g_part, u_part = jnp.split(gu, 2, axis=-1)
        x = x + ga2 * ((jax.nn.silu(g_part) * u_part) @ w_ffn_out[l])
    return rms_norm(x, g_final)


import jax as _jax
import jax.numpy as _jnp

TWIN_FORMAT = 'train_step'
FWD_PARAMS = ['x', 'c', 'w_ada', 'b_ada', 'g_mix', 'w_in', 'b_in', 'sinks', 'conv_w', 'w_out', 'g_ffn', 'w_ffn_in', 'w_ffn_out', 'g_final']
TWIN_WEIGHTS = ['w_ada', 'b_ada', 'g_mix', 'w_in', 'b_in', 'sinks', 'conv_w', 'w_out', 'g_ffn', 'w_ffn_in', 'w_ffn_out', 'g_final']
TWIN_DIFF_INPUT = 'x'
TWIN_INPUTS = ['x', 'c', 'w_ada', 'b_ada', 'g_mix', 'w_in', 'b_in', 'sinks', 'conv_w', 'w_out', 'g_ffn', 'w_ffn_in', 'w_ffn_out', 'g_final', 'loss_target', 'm_w_ada', 'm_b_ada', 'm_g_mix', 'm_w_in', 'm_b_in', 'm_sinks', 'm_conv_w', 'm_w_out', 'm_g_ffn', 'm_w_ffn_in', 'm_w_ffn_out', 'm_g_final', 'v_w_ada', 'v_b_ada', 'v_g_mix', 'v_w_in', 'v_b_in', 'v_sinks', 'v_conv_w', 'v_w_out', 'v_g_ffn', 'v_w_ffn_in', 'v_w_ffn_out', 'v_g_final']
TWIN_OUTPUTS = ['loss', 'grad_x', 'grad_w_ada', 'grad_b_ada', 'grad_g_mix', 'grad_w_in', 'grad_b_in', 'grad_sinks', 'grad_conv_w', 'grad_w_out', 'grad_g_ffn', 'grad_w_ffn_in', 'grad_w_ffn_out', 'grad_g_final', 'delta_w_ada', 'delta_b_ada', 'delta_g_mix', 'delta_w_in', 'delta_b_in', 'delta_sinks', 'delta_conv_w', 'delta_w_out', 'delta_g_ffn', 'delta_w_ffn_in', 'delta_w_ffn_out', 'delta_g_final', 'new_m_w_ada', 'new_m_b_ada', 'new_m_g_mix', 'new_m_w_in', 'new_m_b_in', 'new_m_sinks', 'new_m_conv_w', 'new_m_w_out', 'new_m_g_ffn', 'new_m_w_ffn_in', 'new_m_w_ffn_out', 'new_m_g_final', 'new_v_w_ada', 'new_v_b_ada', 'new_v_g_mix', 'new_v_w_in', 'new_v_b_in', 'new_v_sinks', 'new_v_conv_w', 'new_v_w_out', 'new_v_g_ffn', 'new_v_w_ffn_in', 'new_v_w_ffn_out', 'new_v_g_final']
TWIN_LEAF_KINDS = {'loss': 'loss', 'grad_x': 'grad_x', 'grad_w_ada': 'grad_w', 'grad_b_ada': 'grad_w', 'grad_g_mix': 'grad_w', 'grad_w_in': 'grad_w', 'grad_b_in': 'grad_w', 'grad_sinks': 'grad_w', 'grad_conv_w': 'grad_w', 'grad_w_out': 'grad_w', 'grad_g_ffn': 'grad_w', 'grad_w_ffn_in': 'grad_w', 'grad_w_ffn_out': 'grad_w', 'grad_g_final': 'grad_w', 'delta_w_ada': 'delta_w', 'delta_b_ada': 'delta_w', 'delta_g_mix': 'delta_w', 'delta_w_in': 'delta_w', 'delta_b_in': 'delta_w', 'delta_sinks': 'delta_w', 'delta_conv_w': 'delta_w', 'delta_w_out': 'delta_w', 'delta_g_ffn': 'delta_w', 'delta_w_ffn_in': 'delta_w', 'delta_w_ffn_out': 'delta_w', 'delta_g_final': 'delta_w', 'new_m_w_ada': 'new_m', 'new_m_b_ada': 'new_m', 'new_m_g_mix': 'new_m', 'new_m_w_in': 'new_m', 'new_m_b_in': 'new_m', 'new_m_sinks': 'new_m', 'new_m_conv_w': 'new_m', 'new_m_w_out': 'new_m', 'new_m_g_ffn': 'new_m', 'new_m_w_ffn_in': 'new_m', 'new_m_w_ffn_out': 'new_m', 'new_m_g_final': 'new_m', 'new_v_w_ada': 'new_v', 'new_v_b_ada': 'new_v', 'new_v_g_mix': 'new_v', 'new_v_w_in': 'new_v', 'new_v_b_in': 'new_v', 'new_v_sinks': 'new_v', 'new_v_conv_w': 'new_v', 'new_v_w_out': 'new_v', 'new_v_g_ffn': 'new_v', 'new_v_w_ffn_in': 'new_v', 'new_v_w_ffn_out': 'new_v', 'new_v_g_final': 'new_v'}


def _forward(args):
    return _fwd_reference(*[args[k] for k in FWD_PARAMS])


def _output_shape():
    def fwd():
        inp = _fwd_setup_inputs(0)
        return _fwd_reference(*[inp[k] for k in FWD_PARAMS])
    out = _jax.eval_shape(fwd)
    return out.shape, out.dtype

N_MICROBATCH = 1
ADAM_LR = 0.001
ADAM_B1 = 0.9
ADAM_B2 = 0.999
ADAM_EPS = 1e-08
ADAM_WD = 0.01
ADAM_STEP = 10
PER_EXAMPLE_BATCH_AXIS = {'x': 0, 'c': 0, 'loss_target': 0}
SHARED_INPUTS = []
_WEIGHT_DTYPES = {'w_ada': _jnp.float32, 'b_ada': _jnp.float32, 'g_mix': _jnp.float32, 'w_in': _jnp.float32, 'b_in': _jnp.float32, 'sinks': _jnp.float32, 'conv_w': _jnp.float32, 'w_out': _jnp.float32, 'g_ffn': _jnp.float32, 'w_ffn_in': _jnp.float32, 'w_ffn_out': _jnp.float32, 'g_final': _jnp.float32}
MOMENT_SCALE = {'w_ada': 1.459222e-01, 'b_ada': 2.727871e-01, 'g_mix': 2.646596e-01, 'w_in': 1.179949e-01, 'b_in': 9.607622e-02, 'sinks': 1.139910e-02, 'conv_w': 1.707568e-01, 'w_out': 1.679595e-01, 'g_ffn': 1.597470e-01, 'w_ffn_in': 7.097046e-02, 'w_ffn_out': 1.165017e-01, 'g_final': 6.533154e+01}


def _to_microbatches(a, axis):
    t = _jnp.moveaxis(a, axis, 0)
    t = t.reshape((N_MICROBATCH, t.shape[0] // N_MICROBATCH) + t.shape[1:])
    return _jnp.moveaxis(t, 1, axis + 1)


def setup_inputs(seed: int = 0) -> dict:
    inp = _fwd_setup_inputs(seed)
    key = _jax.random.fold_in(_jax.random.key(seed), 7919)
    shape, _ = _output_shape()
    out = dict(inp)
    out["loss_target"] = _jax.random.normal(_jax.random.fold_in(key, 0), shape, _jnp.float32)
    for i, name in enumerate(TWIN_WEIGHTS):
        w = inp[name].astype(_jnp.float32)
        if MOMENT_SCALE is None:
            s = _jnp.sqrt(_jnp.mean(_jnp.square(w)) + 1e-30)
        else:
            s = MOMENT_SCALE[name]
        km, kv = _jax.random.split(_jax.random.fold_in(key, i + 1))
        out[name] = w
        out["m_" + name] = s * _jax.random.normal(km, w.shape, _jnp.float32)
        out["v_" + name] = (s * s) * _jax.random.uniform(kv, w.shape, _jnp.float32, 0.5, 1.5)
    if N_MICROBATCH > 1:
        for name, axis in PER_EXAMPLE_BATCH_AXIS.items():
            out[name] = _to_microbatches(out[name], axis)
    return {'x': out['x'], 'c': out['c'], 'w_ada': out['w_ada'], 'b_ada': out['b_ada'], 'g_mix': out['g_mix'], 'w_in': out['w_in'], 'b_in': out['b_in'], 'sinks': out['sinks'], 'conv_w': out['conv_w'], 'w_out': out['w_out'], 'g_ffn': out['g_ffn'], 'w_ffn_in': out['w_ffn_in'], 'w_ffn_out': out['w_ffn_out'], 'g_final': out['g_final'], 'loss_target': out['loss_target'], 'm_w_ada': out['m_w_ada'], 'm_b_ada': out['m_b_ada'], 'm_g_mix': out['m_g_mix'], 'm_w_in': out['m_w_in'], 'm_b_in': out['m_b_in'], 'm_sinks': out['m_sinks'], 'm_conv_w': out['m_conv_w'], 'm_w_out': out['m_w_out'], 'm_g_ffn': out['m_g_ffn'], 'm_w_ffn_in': out['m_w_ffn_in'], 'm_w_ffn_out': out['m_w_ffn_out'], 'm_g_final': out['m_g_final'], 'v_w_ada': out['v_w_ada'], 'v_b_ada': out['v_b_ada'], 'v_g_mix': out['v_g_mix'], 'v_w_in': out['v_w_in'], 'v_b_in': out['v_b_in'], 'v_sinks': out['v_sinks'], 'v_conv_w': out['v_conv_w'], 'v_w_out': out['v_w_out'], 'v_g_ffn': out['v_g_ffn'], 'v_w_ffn_in': out['v_w_ffn_in'], 'v_w_ffn_out': out['v_w_ffn_out'], 'v_g_final': out['v_g_final']}


def _loss(weights, diff, rest, loss_target):
    with _jax.named_scope("forward"):
        args = {**rest, TWIN_DIFF_INPUT: diff, **{k: w.astype(_WEIGHT_DTYPES[k]) for k, w in weights.items()}}
        y = _forward(args)
    with _jax.named_scope("loss_head"):
        err = _jnp.square(y.astype(_jnp.float32) - loss_target)
        return 0.5 * _jnp.sum(_jnp.mean(err, axis=-1)) if err.ndim else 0.5 * err


def _adamw(w, g, m, v):
    m = ADAM_B1 * m + (1.0 - ADAM_B1) * g
    v = ADAM_B2 * v + (1.0 - ADAM_B2) * _jnp.square(g)
    m_hat = m / (1.0 - ADAM_B1 ** ADAM_STEP)
    v_hat = v / (1.0 - ADAM_B2 ** ADAM_STEP)
    delta = -ADAM_LR * (m_hat / (_jnp.sqrt(v_hat) + ADAM_EPS) + ADAM_WD * w)
    return delta, m, v


def reference(x, c, w_ada, b_ada, g_mix, w_in, b_in, sinks, conv_w, w_out, g_ffn, w_ffn_in, w_ffn_out, g_final, loss_target, m_w_ada, m_b_ada, m_g_mix, m_w_in, m_b_in, m_sinks, m_conv_w, m_w_out, m_g_ffn, m_w_ffn_in, m_w_ffn_out, m_g_final, v_w_ada, v_b_ada, v_g_mix, v_w_in, v_b_in, v_sinks, v_conv_w, v_w_out, v_g_ffn, v_w_ffn_in, v_w_ffn_out, v_g_final):
    given = dict(x=x, c=c, w_ada=w_ada, b_ada=b_ada, g_mix=g_mix, w_in=w_in, b_in=b_in, sinks=sinks, conv_w=conv_w, w_out=w_out, g_ffn=g_ffn, w_ffn_in=w_ffn_in, w_ffn_out=w_ffn_out, g_final=g_final, loss_target=loss_target, m_w_ada=m_w_ada, m_b_ada=m_b_ada, m_g_mix=m_g_mix, m_w_in=m_w_in, m_b_in=m_b_in, m_sinks=m_sinks, m_conv_w=m_conv_w, m_w_out=m_w_out, m_g_ffn=m_g_ffn, m_w_ffn_in=m_w_ffn_in, m_w_ffn_out=m_w_ffn_out, m_g_final=m_g_final, v_w_ada=v_w_ada, v_b_ada=v_b_ada, v_g_mix=v_g_mix, v_w_in=v_w_in, v_b_in=v_b_in, v_sinks=v_sinks, v_conv_w=v_conv_w, v_w_out=v_w_out, v_g_ffn=v_g_ffn, v_w_ffn_in=v_w_ffn_in, v_w_ffn_out=v_w_ffn_out, v_g_final=v_g_final)
    weights = {n: given[n] for n in TWIN_WEIGHTS}
    shared = {n: given[n] for n in SHARED_INPUTS}
    per_example = {n: given[n] for n in ['x', 'c']}
    grad_fn = _jax.value_and_grad(_loss, argnums=(0, 1))

    def one_microbatch(ex, loss_target):
        ex = dict(ex)
        diff = ex.pop(TWIN_DIFF_INPUT)
        return grad_fn(weights, diff, {**shared, **ex}, loss_target)

    if N_MICROBATCH == 1:
        loss, (grad_w, grad_x) = one_microbatch(per_example, given["loss_target"])
    else:
        def body(carry, xs):
            loss_sum, grad_sum = carry
            l_k, (gw_k, gx_k) = one_microbatch(xs[0], xs[1])
            with _jax.named_scope("update"):
                return (loss_sum + l_k, _jax.tree.map(_jnp.add, grad_sum, gw_k)), gx_k

        init = (_jnp.zeros((), _jnp.float32), _jax.tree.map(_jnp.zeros_like, weights))
        (loss, grad_w), grad_x = _jax.lax.scan(body, init, (per_example, given["loss_target"]))
    with _jax.named_scope("update"):
        delta_w, new_m, new_v = {}, {}, {}
        for n in TWIN_WEIGHTS:
            delta_w[n], new_m[n], new_v[n] = _adamw(weights[n], grad_w[n], given["m_" + n], given["v_" + n])
    return (loss, grad_x, *[grad_w[n] for n in TWIN_WEIGHTS], *[delta_w[n] for n in TWIN_WEIGHTS],
            *[new_m[n] for n in TWIN_WEIGHTS], *[new_v[n] for n in TWIN_WEIGHTS])
```

```python
import functools

import jax
import jax.numpy as jnp
from jax import lax
from jax.experimental import pallas as pl
from jax.experimental.pallas import tpu as pltpu

F32 = jnp.float32
BF16 = jnp.bfloat16

D_MODEL = 1024
HEAD_DIM = 64
N_Q_HEADS = 16
N_KV_HEADS = 2
GROUP = 8
WINDOW = 128
KV_WIDTH = N_KV_HEADS * HEAD_DIM
D_FF = 2816
IN_WIDTH = 6400
N_MOD = 6
EPS = 1e-6
N_DEV = 8
REST_WIDTH = 5 * D_MODEL
KV_COL = D_MODEL + REST_WIDTH
ATTN_SCALE = HEAD_DIM ** -0.5

ADAM_LR = 0.001
ADAM_B1 = 0.9
ADAM_B2 = 0.999
ADAM_EPS = 1e-08
ADAM_WD = 0.01
ADAM_STEP = 10

LANES = 128
SUBLANES = 8
VMEM_LIMIT = 56 * 1024 * 1024
TOKEN_TILE = 512
MESH = pl.DeviceIdType.MESH

NT_DIMS = (((1,), (1,)), ((), ()))
TN_DIMS = (((0,), (0,)), ((), ()))


def _params(sem=None):
    return pltpu.CompilerParams(dimension_semantics=sem, vmem_limit_bytes=VMEM_LIMIT)


def _full(shape):
    return pl.BlockSpec(shape, lambda *_: (0,) * len(shape))


def _my_place():
    return lax.axis_index("x"), lax.axis_index("y"), lax.axis_index("c")


def _flip(v, bit):
    return 1 - v if bit else v


def _sigmoid(v):
    return 1.0 / (1.0 + jnp.exp(-v))


def _small_allgather(v, name):
    rows = v.shape[0]

    def body(v_ref, out_ref, send_sems, recv_sems, local_sem):
        x, y, c = _my_place()
        me = 4 * x + 2 * y + c
        mine = pltpu.make_async_copy(v_ref, out_ref.at[me], local_sem)
        mine.start()
        sends = []
        for k in range(1, N_DEV):
            px, py, pc = _flip(x, k & 4), _flip(y, k & 2), _flip(c, k & 1)
            cp = pltpu.make_async_remote_copy(
                src_ref=v_ref, dst_ref=out_ref.at[me], send_sem=send_sems.at[k - 1], recv_sem=recv_sems.at[k - 1],
                device_id=(px, py, pc), device_id_type=MESH)
            cp.start()
            sends.append(cp)
        for k in range(1, N_DEV):
            px, py, pc = _flip(x, k & 4), _flip(y, k & 2), _flip(c, k & 1)
            pltpu.make_async_remote_copy(
                src_ref=v_ref, dst_ref=out_ref.at[4 * px + 2 * py + pc], send_sem=send_sems.at[k - 1],
                recv_sem=recv_sems.at[k - 1], device_id=(px, py, pc), device_id_type=MESH).wait_recv()
        for cp in sends:
            cp.wait_send()
        mine.wait()

    return pl.pallas_call(
        body, name=name,
        out_shape=jax.ShapeDtypeStruct((N_DEV, rows, LANES), F32),
        in_specs=[pl.BlockSpec(memory_space=pltpu.VMEM)],
        out_specs=pl.BlockSpec(memory_space=pltpu.VMEM),
        scratch_shapes=[pltpu.SemaphoreType.DMA((N_DEV - 1,)), pltpu.SemaphoreType.DMA((N_DEV - 1,)),
                        pltpu.SemaphoreType.DMA],
        compiler_params=pltpu.CompilerParams(vmem_limit_bytes=VMEM_LIMIT),
    )(v)


def _gather_weights(shards):
    n = len(shards)

    def body(*refs):
        in_refs, out_refs = refs[:n], refs[n:2 * n]
        cast_refs = refs[2 * n:3 * n]
        send_sems, recv_sems, local_sems = refs[3 * n:]
        x, y, c = _my_place()
        me, sibling = (x, y, c), (x, y, 1 - c)
        chips = [(1 - x, y), (x, 1 - y), (1 - x, 1 - y)]

        def block(ref, place):
            return ref.at[4 * place[0] + 2 * place[1] + place[2]]

        def copy(w, k, place, to, src=None):
            return pltpu.make_async_remote_copy(
                src_ref=block(out_refs[w], place) if src is None else src, dst_ref=block(out_refs[w], place),
                send_sem=send_sems.at[w, k], recv_sem=recv_sems.at[w, k], device_id=to, device_id_type=MESH)

        started = []
        local = []
        for w in range(n):
            cast_refs[w][...] = in_refs[w][...].astype(BF16)
            mine = pltpu.make_async_copy(cast_refs[w], block(out_refs[w], me), local_sems.at[w])
            mine.start()
            local.append(mine)
            first = [copy(w, 0, me, sibling, src=cast_refs[w])]
            first += [copy(w, 1 + j, me, (*chip, c), src=cast_refs[w]) for j, chip in enumerate(chips)]
            for cp in first:
                cp.start()
            started += first
        for w in range(n):
            for j, chip in enumerate(chips):
                copy(w, 1 + j, (*chip, c), me).wait_recv()
                passed = copy(w, 4 + j, (*chip, c), sibling)
                passed.start()
                started.append(passed)
        for w in range(n):
            copy(w, 0, sibling, me).wait_recv()
            for j, chip in enumerate(chips):
                copy(w, 4 + j, (*chip, 1 - c), me).wait_recv()
        for cp in started:
            cp.wait_send()
        for mine in local:
            mine.wait()

    return pl.pallas_call(
        body, name="gather_weights",
        out_shape=[jax.ShapeDtypeStruct((N_DEV,) + s.shape, BF16) for s in shards],
        in_specs=[pl.BlockSpec(memory_space=pltpu.VMEM)] * n,
        out_specs=[pl.BlockSpec(memory_space=pl.ANY)] * n,
        scratch_shapes=[pltpu.VMEM(s.shape, BF16) for s in shards]
        + [pltpu.SemaphoreType.DMA((n, N_DEV - 1)), pltpu.SemaphoreType.DMA((n, N_DEV - 1)),
           pltpu.SemaphoreType.DMA((n,))],
        compiler_params=pltpu.CompilerParams(vmem_limit_bytes=VMEM_LIMIT),
    )(*shards)


def _reduce_scatter(gblocks, name):
    _, r, cdim = gblocks.shape
    flips = [(0, 0), (1, 0), (0, 1), (1, 1)]

    def body(g_ref, out_ref, own_buf, sib_buf, sum_buf, ici_buf, own_sems, p1_send, p1_recv, p2_send, p2_recv):
        x, y, c = _my_place()

        def blk(fx, fy, core):
            return 4 * _flip(x, fx) + 2 * _flip(y, fy) + core

        own = []
        for f, (fx, fy) in enumerate(flips):
            cp = pltpu.make_async_copy(g_ref.at[blk(fx, fy, c)], own_buf.at[f], own_sems.at[f])
            cp.start()
            own.append(cp)
        p1 = []
        for f, (fx, fy) in enumerate(flips):
            cp = pltpu.make_async_remote_copy(
                src_ref=g_ref.at[blk(fx, fy, 1 - c)], dst_ref=sib_buf.at[f], send_sem=p1_send.at[f],
                recv_sem=p1_recv.at[f], device_id=(x, y, 1 - c), device_id_type=MESH)
            cp.start()
            p1.append(cp)
        p2 = []
        for f in (1, 2, 3):
            fx, fy = flips[f]
            own[f].wait()
            p1[f].wait_recv()
            sum_buf[f - 1] = (own_buf[f].astype(F32) + sib_buf[f].astype(F32)).astype(BF16)
            cp = pltpu.make_async_remote_copy(
                src_ref=sum_buf.at[f - 1], dst_ref=ici_buf.at[f - 1], send_sem=p2_send.at[f - 1],
                recv_sem=p2_recv.at[f - 1], device_id=(_flip(x, fx), _flip(y, fy), c), device_id_type=MESH)
            cp.start()
            p2.append(cp)
        own[0].wait()
        p1[0].wait_recv()
        total = own_buf[0].astype(F32) + sib_buf[0].astype(F32)
        for f in (1, 2, 3):
            p2[f - 1].wait_recv()
            total = total + ici_buf[f - 1].astype(F32)
        out_ref[...] = total
        for cp in p1 + p2:
            cp.wait_send()

    return pl.pallas_call(
        body, name=name,
        out_shape=jax.ShapeDtypeStruct((r, cdim), F32),
        in_specs=[pl.BlockSpec(memory_space=pl.ANY)],
        out_specs=pl.BlockSpec(memory_space=pltpu.VMEM),
        scratch_shapes=[pltpu.VMEM((4, r, cdim), BF16), pltpu.VMEM((4, r, cdim), BF16),
                        pltpu.VMEM((3, r, cdim), BF16), pltpu.VMEM((3, r, cdim), BF16),
                        pltpu.SemaphoreType.DMA((4,)), pltpu.SemaphoreType.DMA((4,)), pltpu.SemaphoreType.DMA((4,)),
                        pltpu.SemaphoreType.DMA((3,)), pltpu.SemaphoreType.DMA((3,))],
        compiler_params=pltpu.CompilerParams(vmem_limit_bytes=VMEM_LIMIT),
    )(gblocks)


def _ada_forward(c_all, w_ada, b_cols):
    cols = w_ada.shape[1]

    def body(c_ref, w_ref, b_ref, out_ref):
        cf = c_ref[...]
        act = (cf * _sigmoid(cf)).astype(BF16)
        out_ref[...] = jnp.dot(act, w_ref[...].astype(BF16), preferred_element_type=F32) + b_ref[...]

    return pl.pallas_call(
        body, name="ada_forward",
        out_shape=jax.ShapeDtypeStruct((N_DEV, cols), F32),
        in_specs=[pl.BlockSpec(memory_space=pltpu.VMEM)] * 3,
        out_specs=pl.BlockSpec(memory_space=pltpu.VMEM),
        compiler_params=pltpu.CompilerParams(vmem_limit_bytes=VMEM_LIMIT),
    )(c_all, w_ada, b_cols)


def _ada_weight_grad(c_all, dmod_cols):
    cols = dmod_cols.shape[1]

    def body(c_ref, d_ref, out_ref):
        cf = c_ref[...]
        act = (cf * _sigmoid(cf)).astype(BF16)
        out_ref[...] = lax.dot_general(act, d_ref[...].astype(BF16), TN_DIMS, preferred_element_type=F32)

    return pl.pallas_call(
        body, name="ada_weight_grad",
        out_shape=jax.ShapeDtypeStruct((D_MODEL, cols), F32),
        in_specs=[pl.BlockSpec(memory_space=pltpu.VMEM)] * 2,
        out_specs=pl.BlockSpec(memory_space=pltpu.VMEM),
        compiler_params=pltpu.CompilerParams(vmem_limit_bytes=VMEM_LIMIT),
    )(c_all, dmod_cols)


def _sum_devices(packed):
    def body(p_ref, out_ref):
        total = p_ref[0]
        for d in range(1, N_DEV):
            total = total + p_ref[d]
        out_ref[...] = total

    return pl.pallas_call(
        body, name="sum_devices",
        out_shape=jax.ShapeDtypeStruct(packed.shape[1:], F32),
        in_specs=[pl.BlockSpec(memory_space=pltpu.VMEM)],
        out_specs=pl.BlockSpec(memory_space=pltpu.VMEM),
        compiler_params=pltpu.CompilerParams(vmem_limit_bytes=VMEM_LIMIT),
    )(packed)


def _adamw(w, g, m, v, name):
    rows, cols = w.shape
    tile = rows
    for cand in (256, 128, 64, 32, 16, 8):
        if rows % cand == 0:
            tile = cand
            break
    c1 = 1.0 / (1.0 - ADAM_B1 ** ADAM_STEP)
    c2 = 1.0 / (1.0 - ADAM_B2 ** ADAM_STEP)

    def body(w_ref, g_ref, m_ref, v_ref, d_ref, nm_ref, nv_ref):
        gf = g_ref[...]
        nm = ADAM_B1 * m_ref[...] + (1.0 - ADAM_B1) * gf
        nv = ADAM_B2 * v_ref[...] + (1.0 - ADAM_B2) * (gf * gf)
        nm_ref[...] = nm
        nv_ref[...] = nv
        d_ref[...] = -ADAM_LR * ((nm * c1) / (jnp.sqrt(nv * c2) + ADAM_EPS) + ADAM_WD * w_ref[...])

    spec = pl.BlockSpec((tile, cols), lambda i: (i, 0))
    return pl.pallas_call(
        body, name=name, grid=(rows // tile,),
        out_shape=[jax.ShapeDtypeStruct((rows, cols), F32)] * 3,
        in_specs=[spec] * 4, out_specs=[spec] * 3,
        compiler_params=_params(("arbitrary",)),
    )(w, g, m, v)


def _inproj_fwd(x, vec, w_in, b_in):
    t = x.shape[0]
    tm = min(TOKEN_TILE, t)
    chunk = 1280

    def body(x_ref, vec_ref, w_ref, b_ref, z_ref, h_ref):
        xf = x_ref[...]
        r = lax.rsqrt(jnp.mean(xf * xf, axis=-1, keepdims=True) + EPS)
        h = (xf * r) * vec_ref[0:1, :] * (1.0 + vec_ref[1:2, :]) + vec_ref[2:3, :]
        hb = h.astype(BF16)
        h_ref[...] = hb
        for n in range(IN_WIDTH // chunk):
            sl = slice(n * chunk, (n + 1) * chunk)
            z_ref[:, sl] = (jnp.dot(hb, w_ref[:, sl], preferred_element_type=F32) + b_ref[:, sl]).astype(BF16)

    return pl.pallas_call(
        body, name="inproj_fwd", grid=(t // tm,),
        out_shape=[jax.ShapeDtypeStruct((t, IN_WIDTH), BF16), jax.ShapeDtypeStruct((t, D_MODEL), BF16)],
        in_specs=[pl.BlockSpec((tm, D_MODEL), lambda i: (i, 0)), _full((SUBLANES, D_MODEL)),
                  _full((D_MODEL, IN_WIDTH)), _full((1, IN_WIDTH))],
        out_specs=[pl.BlockSpec((tm, IN_WIDTH), lambda i: (i, 0)), pl.BlockSpec((tm, D_MODEL), lambda i: (i, 0))],
        compiler_params=_params(("arbitrary",)),
    )(x, vec, w_in, b_in)


def _window_mask(has_prev):
    qi = lax.broadcasted_iota(jnp.int32, (WINDOW, 2 * WINDOW), 0)
    kj = lax.broadcasted_iota(jnp.int32, (WINDOW, 2 * WINDOW), 1)
    off = jnp.where(has_prev, 0, 4 * WINDOW)
    in_prev = jnp.logical_and(kj < WINDOW, kj > qi + off)
    in_cur = jnp.logical_and(kj >= WINDOW, (kj - WINDOW) <= qi)
    return jnp.logical_or(in_prev, in_cur)


def _attn_fwd(z, sinks):
    t = z.shape[0]
    tq = min(TOKEN_TILE, t)
    nblk = tq // WINDOW

    def body(q_ref, kv_ref, sink_ref, o_ref, lse_ref):
        i = pl.program_id(0)
        lane = lax.broadcasted_iota(jnp.int32, (WINDOW, N_Q_HEADS), 1)

        def one_block(b, carry):
            row0 = pl.multiple_of(b * WINDOW, WINDOW)
            start = i * tq + b * WINDOW
            prev = pl.multiple_of(jnp.maximum(start - WINDOW, 0), WINDOW)
            cur = pl.multiple_of(start, WINDOW)
            kvw = jnp.concatenate([kv_ref[pl.ds(prev, WINDOW), :], kv_ref[pl.ds(cur, WINDOW), :]], axis=0)
            valid = _window_mask(start > 0)
            lse_blk = jnp.zeros((WINDOW, N_Q_HEADS), F32)
            for h in range(N_Q_HEADS):
                j = h // GROUP
                k = kvw[:, j * HEAD_DIM:(j + 1) * HEAD_DIM]
                v = kvw[:, KV_WIDTH + j * HEAD_DIM:KV_WIDTH + (j + 1) * HEAD_DIM]
                qh = q_ref[pl.ds(row0, WINDOW), h * HEAD_DIM:(h + 1) * HEAD_DIM]
                s = lax.dot_general(qh, k, NT_DIMS, preferred_element_type=F32) * ATTN_SCALE
                s = jnp.where(valid, s, -jnp.inf)
                sink = sink_ref[h]
                m = jnp.maximum(jnp.max(s, axis=-1, keepdims=True), sink)
                p = jnp.exp(s - m)
                denom = jnp.sum(p, axis=-1, keepdims=True) + jnp.exp(sink - m)
                o = jnp.dot(p.astype(BF16), v, preferred_element_type=F32) / denom
                o_ref[pl.ds(row0, WINDOW), h * HEAD_DIM:(h + 1) * HEAD_DIM] = o.astype(BF16)
                lse_blk = jnp.where(lane == h, m + jnp.log(denom), lse_blk)
            lse_ref[pl.ds(row0, WINDOW), :] = lse_blk
            return carry

        lax.fori_loop(0, nblk, one_block, 0)

    return pl.pallas_call(
        body, name="attn_fwd", grid=(t // tq,),
        out_shape=[jax.ShapeDtypeStruct((t, D_MODEL), BF16), jax.ShapeDtypeStruct((t, N_Q_HEADS), F32)],
        in_specs=[pl.BlockSpec((tq, D_MODEL), lambda i: (i, 0)),
                  pl.BlockSpec((t, 2 * KV_WIDTH), lambda i: (0, KV_COL // (2 * KV_WIDTH))),
                  pl.BlockSpec(memory_space=pltpu.SMEM)],
        out_specs=[pl.BlockSpec((tq, D_MODEL), lambda i: (i, 0)), pl.BlockSpec((tq, N_Q_HEADS), lambda i: (i, 0))],
        compiler_params=_params(("arbitrary",)),
    )(z, z, sinks)


HALO = 16


def _shift_down(u, uh, k):
    row = lax.broadcasted_iota(jnp.int32, u.shape, 0)
    out = pltpu.roll(u, k, 0)
    for j in range(k):
        out = jnp.where(row == j, uh[HALO - k + j:HALO - k + j + 1, :], out)
    return out


def _shift_up(u, nxt, k):
    n = u.shape[0]
    row = lax.broadcasted_iota(jnp.int32, u.shape, 0)
    out = pltpu.roll(u, n - k, 0)
    for j in range(k):
        out = jnp.where(row == n - k + j, nxt[j:j + 1, :], out)
    return out


def _conv_inputs(cc_ref, cx_ref, hc_ref, hx_ref, first_tile):
    cc = cc_ref[...].astype(F32)
    cx = cx_ref[...].astype(F32)
    u = cc * cx
    uh = jnp.where(first_tile, 0.0, hc_ref[...].astype(F32) * hx_ref[...].astype(F32))
    return cc, cx, u, _shift_down(u, uh, 1), _shift_down(u, uh, 2)


def _z_specs(tm, order):
    per_tile = tm // HALO
    cols = [pl.BlockSpec((tm, D_MODEL), functools.partial(lambda i, j: (order(i), j), j=j)) for j in range(1, 6)]
    halos = [pl.BlockSpec((HALO, D_MODEL),
                          functools.partial(lambda i, j: (jnp.maximum(order(i) * per_tile - 1, 0), j), j=j))
             for j in (2, 3)]
    return cols + halos


def _mix_fwd(x, attn, z, vec, w_out):
    t = x.shape[0]
    tm = min(TOKEN_TILE, t)

    def body(x_ref, a_ref, cb_ref, cc_ref, cx_ref, ga_ref, gc_ref, hc_ref, hx_ref, vec_ref, w_ref,
             m_ref, x2_ref, h2_ref):
        i = pl.program_id(0)
        _, _, u, u1, u2 = _conv_inputs(cc_ref, cx_ref, hc_ref, hx_ref, i == 0)
        cv = vec_ref[4:5, :] * u2 + vec_ref[5:6, :] * u1 + vec_ref[6:7, :] * u
        conv = cb_ref[...].astype(F32) * cv
        merged = (_sigmoid(ga_ref[...].astype(F32)) * a_ref[...].astype(F32)
                  + _sigmoid(gc_ref[...].astype(F32)) * conv)
        mb = merged.astype(BF16)
        m_ref[...] = mb
        o = jnp.dot(mb, w_ref[...], preferred_element_type=F32)
        x2 = x_ref[...] + vec_ref[0:1, :] * o
        x2_ref[...] = x2
        r = lax.rsqrt(jnp.mean(x2 * x2, axis=-1, keepdims=True) + EPS)
        h2 = (x2 * r) * vec_ref[1:2, :] * (1.0 + vec_ref[2:3, :]) + vec_ref[3:4, :]
        h2_ref[...] = h2.astype(BF16)

    tok = pl.BlockSpec((tm, D_MODEL), lambda i: (i, 0))
    return pl.pallas_call(
        body, name="mix_fwd", grid=(t // tm,),
        out_shape=[jax.ShapeDtypeStruct((t, D_MODEL), BF16), jax.ShapeDtypeStruct((t, D_MODEL), F32),
                   jax.ShapeDtypeStruct((t, D_MODEL), BF16)],
        in_specs=[tok, tok] + _z_specs(tm, lambda i: i) + [_full((SUBLANES, D_MODEL)), _full((D_MODEL, D_MODEL))],
        out_specs=[tok, tok, tok],
        compiler_params=_params(("arbitrary",)),
    )(x, attn, z, z, z, z, z, z, z, vec, w_out)


def _ffn_fwd(h2, w_ffn_in):
    t = h2.shape[0]
    tm = min(TOKEN_TILE, t)

    def body(h_ref, w_ref, gu_ref, a_ref):
        hb = h_ref[...]
        g = jnp.dot(hb, w_ref[:, :D_FF], preferred_element_type=F32)
        u = jnp.dot(hb, w_ref[:, D_FF:], preferred_element_type=F32)
        gu_ref[:, :D_FF] = g.astype(BF16)
        gu_ref[:, D_FF:] = u.astype(BF16)
        a_ref[...] = (g * _sigmoid(g) * u).astype(BF16)

    return pl.pallas_call(
        body, name="ffn_fwd", grid=(t // tm,),
        out_shape=[jax.ShapeDtypeStruct((t, 2 * D_FF), BF16), jax.ShapeDtypeStruct((t, D_FF), BF16)],
        in_specs=[pl.BlockSpec((tm, D_MODEL), lambda i: (i, 0)), _full((D_MODEL, 2 * D_FF))],
        out_specs=[pl.BlockSpec((tm, 2 * D_FF), lambda i: (i, 0)), pl.BlockSpec((tm, D_FF), lambda i: (i, 0))],
        compiler_params=_params(("arbitrary",)),
    )(h2, w_ffn_in)


def _ffn_out_loss(a, gu, x2, target, vec, w_ffn_out):
    t = a.shape[0]
    tm = min(TOKEN_TILE, t)

    def body(a_ref, gu_ref, x2_ref, t_ref, vec_ref, w_ref, dx3_ref, df_ref, dgu_ref, acc_ref):
        @pl.when(pl.program_id(0) == 0)
        def _():
            acc_ref[...] = jnp.zeros_like(acc_ref)

        ga2 = vec_ref[0:1, :]
        gf = vec_ref[1:2, :]
        f = jnp.dot(a_ref[...], w_ref[...], preferred_element_type=F32)
        x3 = x2_ref[...] + ga2 * f
        r = lax.rsqrt(jnp.mean(x3 * x3, axis=-1, keepdims=True) + EPS)
        xn = x3 * r
        err = xn * gf - t_ref[...]
        dy = err * (1.0 / D_MODEL)
        dxn = dy * gf
        dx3 = r * (dxn - xn * jnp.mean(dxn * xn, axis=-1, keepdims=True))
        dx3_ref[...] = dx3
        acc_ref[0:1, :] += jnp.sum(err * err, axis=0, keepdims=True)
        acc_ref[1:2, :] += jnp.sum(dy * xn, axis=0, keepdims=True)
        acc_ref[2:3, :] += jnp.sum(dx3 * f, axis=0, keepdims=True)
        df = (dx3 * ga2).astype(BF16)
        df_ref[...] = df
        da = lax.dot_general(df, w_ref[...], NT_DIMS, preferred_element_type=F32)
        g = gu_ref[:, :D_FF].astype(F32)
        u = gu_ref[:, D_FF:].astype(F32)
        sg = _sigmoid(g)
        dgu_ref[:, :D_FF] = (da * u * (sg * (1.0 + g * (1.0 - sg)))).astype(BF16)
        dgu_ref[:, D_FF:] = (da * (g * sg)).astype(BF16)

    tok = pl.BlockSpec((tm, D_MODEL), lambda i: (i, 0))
    return pl.pallas_call(
        body, name="ffn_out_loss", grid=(t // tm,),
        out_shape=[jax.ShapeDtypeStruct((t, D_MODEL), F32), jax.ShapeDtypeStruct((t, D_MODEL), BF16),
                   jax.ShapeDtypeStruct((t, 2 * D_FF), BF16), jax.ShapeDtypeStruct((SUBLANES, D_MODEL), F32)],
        in_specs=[pl.BlockSpec((tm, D_FF), lambda i: (i, 0)), pl.BlockSpec((tm, 2 * D_FF), lambda i: (i, 0)),
                  tok, tok, _full((SUBLANES, D_MODEL)), _full((D_FF, D_MODEL))],
        out_specs=[tok, tok, pl.BlockSpec((tm, 2 * D_FF), lambda i: (i, 0)), _full((SUBLANES, D_MODEL))],
        compiler_params=_params(("arbitrary",)),
    )(a, gu, x2, target, vec, w_ffn_out)


def _ffn_in_bwd(dgu, x2, dx3, vec, w_ffn_in):
    t = x2.shape[0]
    tm = min(TOKEN_TILE, t)

    def body(dgu_ref, x2_ref, dx3_ref, vec_ref, wf_ref, dx2_ref, acc_ref):
        @pl.when(pl.program_id(0) == 0)
        def _():
            acc_ref[...] = jnp.zeros_like(acc_ref)

        gffn = vec_ref[0:1, :]
        sc2 = vec_ref[1:2, :]
        dh2 = lax.dot_general(dgu_ref[...], wf_ref[...], NT_DIMS, preferred_element_type=F32)
        x2 = x2_ref[...]
        r = lax.rsqrt(jnp.mean(x2 * x2, axis=-1, keepdims=True) + EPS)
        xn = x2 * r
        acc_ref[0:1, :] += jnp.sum(dh2, axis=0, keepdims=True)
        acc_ref[1:2, :] += jnp.sum(dh2 * xn * gffn, axis=0, keepdims=True)
        acc_ref[2:3, :] += jnp.sum(dh2 * xn * (1.0 + sc2), axis=0, keepdims=True)
        dxn = dh2 * gffn * (1.0 + sc2)
        dx2_ref[...] = dx3_ref[...] + r * (dxn - xn * jnp.mean(dxn * xn, axis=-1, keepdims=True))

    tok = pl.BlockSpec((tm, D_MODEL), lambda i: (i, 0))
    return pl.pallas_call(
        body, name="ffn_in_bwd", grid=(t // tm,),
        out_shape=[jax.ShapeDtypeStruct((t, D_MODEL), F32), jax.ShapeDtypeStruct((SUBLANES, D_MODEL), F32)],
        in_specs=[pl.BlockSpec((tm, 2 * D_FF), lambda i: (i, 0)), tok, tok, _full((SUBLANES, D_MODEL)),
                  _full((D_MODEL, 2 * D_FF))],
        out_specs=[tok, _full((SUBLANES, D_MODEL))],
        compiler_params=_params(("arbitrary",)),
    )(dgu, x2, dx3, vec, w_ffn_in)


def _mix_bwd(dx2, merged, attn, z, vec, w_out):
    t = dx2.shape[0]
    tm = min(TOKEN_TILE, t)
    nt = t // tm
    rev = lambda i: nt - 1 - i

    def body(dx2_ref, m_ref, a_ref, cb_ref, cc_ref, cx_ref, ga_ref, gc_ref, hc_ref, hx_ref,
             vec_ref, wo_ref, do_ref, da_ref, dr_ref, acc_ref, carry_ref):
        i = pl.program_id(0)

        @pl.when(i == 0)
        def _():
            acc_ref[...] = jnp.zeros_like(acc_ref)
            carry_ref[...] = jnp.zeros_like(carry_ref)

        ga1 = vec_ref[0:1, :]
        w0, w1, w2 = vec_ref[1:2, :], vec_ref[2:3, :], vec_ref[3:4, :]
        dx2 = dx2_ref[...]
        o = jnp.dot(m_ref[...], wo_ref[...], preferred_element_type=F32)
        acc_ref[0:1, :] += jnp.sum(dx2 * o, axis=0, keepdims=True)
        do = (dx2 * ga1).astype(BF16)
        do_ref[...] = do
        dm = lax.dot_general(do, wo_ref[...], NT_DIMS, preferred_element_type=F32)

        cc, cx, u, u1, u2 = _conv_inputs(cc_ref, cx_ref, hc_ref, hx_ref, i == nt - 1)
        cv = w0 * u2 + w1 * u1 + w2 * u
        cb = cb_ref[...].astype(F32)
        sa = _sigmoid(ga_ref[...].astype(F32))
        sc = _sigmoid(gc_ref[...].astype(F32))
        attn = a_ref[...].astype(F32)
        da_ref[...] = (dm * sa).astype(BF16)
        dconv = dm * sc
        dr_ref[:, 3 * D_MODEL:4 * D_MODEL] = (dm * attn * sa * (1.0 - sa)).astype(BF16)
        dr_ref[:, 4 * D_MODEL:5 * D_MODEL] = (dconv * (cb * cv) * (1.0 - sc)).astype(BF16)
        dr_ref[:, 0:D_MODEL] = (dconv * cv).astype(BF16)
        dcv = dconv * cb
        acc_ref[1:2, :] += jnp.sum(dcv * u2, axis=0, keepdims=True)
        acc_ref[2:3, :] += jnp.sum(dcv * u1, axis=0, keepdims=True)
        acc_ref[3:4, :] += jnp.sum(dcv * u, axis=0, keepdims=True)
        nxt = carry_ref[...]
        du = w2 * dcv + w1 * _shift_up(dcv, nxt, 1) + w0 * _shift_up(dcv, nxt, 2)
        carry_ref[...] = dcv[0:SUBLANES, :]
        dr_ref[:, D_MODEL:2 * D_MODEL] = (du * cx).astype(BF16)
        dr_ref[:, 2 * D_MODEL:3 * D_MODEL] = (du * cc).astype(BF16)

    tok = pl.BlockSpec((tm, D_MODEL), lambda i: (rev(i), 0))
    return pl.pallas_call(
        body, name="mix_bwd", grid=(nt,),
        out_shape=[jax.ShapeDtypeStruct((t, D_MODEL), BF16), jax.ShapeDtypeStruct((t, D_MODEL), BF16),
                   jax.ShapeDtypeStruct((t, REST_WIDTH), BF16), jax.ShapeDtypeStruct((SUBLANES, D_MODEL), F32)],
        in_specs=[tok, tok, tok] + _z_specs(tm, rev) + [_full((SUBLANES, D_MODEL)), _full((D_MODEL, D_MODEL))],
        out_specs=[tok, tok, pl.BlockSpec((tm, REST_WIDTH), lambda i: (rev(i), 0)), _full((SUBLANES, D_MODEL))],
        scratch_shapes=[pltpu.VMEM((SUBLANES, D_MODEL), F32)],
        compiler_params=_params(("arbitrary",)),
    )(dx2, merged, attn, z, z, z, z, z, z, z, vec, w_out)


def _attn_bwd(z, dattn, attn, lse, sinks):
    t = z.shape[0]
    tq = min(TOKEN_TILE, t)
    nblk = tq // WINDOW
    nt = t // tq

    def body(q_ref, kv_ref, do_ref, o_ref, lse_ref, sink_ref, dq_ref, dkv_ref, ds_ref, acc_ref):
        i = pl.program_id(0)

        @pl.when(i == 0)
        def _():
            acc_ref[...] = jnp.zeros_like(acc_ref)
            ds_ref[...] = jnp.zeros_like(ds_ref)

        lane = lax.broadcasted_iota(jnp.int32, (1, LANES), 1)

        def one_block(b, dsink):
            row0 = pl.multiple_of(b * WINDOW, WINDOW)
            start = i * tq + b * WINDOW
            prev = pl.multiple_of(jnp.maximum(start - WINDOW, 0), WINDOW)
            cur = pl.multiple_of(start, WINDOW)
            kvw = jnp.concatenate([kv_ref[pl.ds(prev, WINDOW), :], kv_ref[pl.ds(cur, WINDOW), :]], axis=0)
            valid = _window_mask(start > 0)
            lse_blk = lse_ref[pl.ds(row0, WINDOW), :]
            parts = []
            for j in range(N_KV_HEADS):
                k = kvw[:, j * HEAD_DIM:(j + 1) * HEAD_DIM]
                v = kvw[:, KV_WIDTH + j * HEAD_DIM:KV_WIDTH + (j + 1) * HEAD_DIM]
                dk = jnp.zeros((2 * WINDOW, HEAD_DIM), F32)
                dv = jnp.zeros((2 * WINDOW, HEAD_DIM), F32)
                for g in range(GROUP):
                    h = j * GROUP + g
                    cols = slice(h * HEAD_DIM, (h + 1) * HEAD_DIM)
                    qh = q_ref[pl.ds(row0, WINDOW), cols]
                    doh = do_ref[pl.ds(row0, WINDOW), cols]
                    oh = o_ref[pl.ds(row0, WINDOW), cols]
                    lse_h = lse_blk[:, h:h + 1]
                    s = lax.dot_general(qh, k, NT_DIMS, preferred_element_type=F32) * ATTN_SCALE
                    p = jnp.where(valid, jnp.exp(s - lse_h), 0.0)
                    delta = jnp.sum(doh.astype(F32) * oh.astype(F32), axis=-1, keepdims=True)
                    dp = lax.dot_general(doh, v, NT_DIMS, preferred_element_type=F32)
                    dsb = (p * (dp - delta)).astype(BF16)
                    dq = jnp.dot(dsb, k, preferred_element_type=F32) * ATTN_SCALE
                    dq_ref[pl.ds(row0, WINDOW), cols] = dq.astype(BF16)
                    dk = dk + lax.dot_general(dsb, qh, TN_DIMS, preferred_element_type=F32)
                    dv = dv + lax.dot_general(p.astype(BF16), doh, TN_DIMS, preferred_element_type=F32)
                    psink = jnp.exp(sink_ref[h] - lse_h)
                    dsink = dsink - jnp.where(lane == h, jnp.sum(psink * delta), 0.0)
                parts.append((dk * ATTN_SCALE, dv))
            blk = jnp.concatenate([parts[0][0], parts[1][0], parts[0][1], parts[1][1]], axis=1)
            acc_ref[pl.ds(prev, WINDOW), :] += blk[:WINDOW, :]
            acc_ref[pl.ds(cur, WINDOW), :] += blk[WINDOW:, :]
            return dsink

        dsink = lax.fori_loop(0, nblk, one_block, jnp.zeros((1, LANES), F32))
        ds_ref[0:1, :] += dsink

        @pl.when(i == nt - 1)
        def _():
            dkv_ref[...] = acc_ref[...].astype(BF16)

    tok = pl.BlockSpec((tq, D_MODEL), lambda i: (i, 0))
    return pl.pallas_call(
        body, name="attn_bwd", grid=(nt,),
        out_shape=[jax.ShapeDtypeStruct((t, D_MODEL), BF16), jax.ShapeDtypeStruct((t, 2 * KV_WIDTH), BF16),
                   jax.ShapeDtypeStruct((SUBLANES, LANES), F32)],
        in_specs=[tok, pl.BlockSpec((t, 2 * KV_WIDTH), lambda i: (0, KV_COL // (2 * KV_WIDTH))), tok, tok,
                  pl.BlockSpec((tq, N_Q_HEADS), lambda i: (i, 0)), pl.BlockSpec(memory_space=pltpu.SMEM)],
        out_specs=[tok, _full((t, 2 * KV_WIDTH)), _full((SUBLANES, LANES))],
        scratch_shapes=[pltpu.VMEM((t, 2 * KV_WIDTH), F32)],
        compiler_params=_params(("arbitrary",)),
    )(z, z, dattn, attn, lse, sinks)


def _inproj_bwd(dq, drest, dkv, x, dx2, vec, w_in):
    t = x.shape[0]
    tm = min(TOKEN_TILE, t)

    def body(dq_ref, dr_ref, dkv_ref, x_ref, dx2_ref, vec_ref, w_ref, gx_ref, acc_ref, db_ref):
        @pl.when(pl.program_id(0) == 0)
        def _():
            acc_ref[...] = jnp.zeros_like(acc_ref)
            db_ref[...] = jnp.zeros_like(db_ref)

        g = vec_ref[0:1, :]
        sc1 = vec_ref[1:2, :]
        dqb, drb, dkvb = dq_ref[...], dr_ref[...], dkv_ref[...]
        dh = lax.dot_general(dqb, w_ref[:, :D_MODEL], NT_DIMS, preferred_element_type=F32)
        dh = dh + lax.dot_general(drb, w_ref[:, D_MODEL:KV_COL], NT_DIMS, preferred_element_type=F32)
        dh = dh + lax.dot_general(dkvb, w_ref[:, KV_COL:], NT_DIMS, preferred_element_type=F32)
        db_ref[:, :D_MODEL] += jnp.sum(dqb.astype(F32), axis=0, keepdims=True)
        db_ref[:, D_MODEL:KV_COL] += jnp.sum(drb.astype(F32), axis=0, keepdims=True)
        db_ref[:, KV_COL:] += jnp.sum(dkvb.astype(F32), axis=0, keepdims=True)
        xf = x_ref[...]
        r = lax.rsqrt(jnp.mean(xf * xf, axis=-1, keepdims=True) + EPS)
        xn = xf * r
        acc_ref[0:1, :] += jnp.sum(dh, axis=0, keepdims=True)
        acc_ref[1:2, :] += jnp.sum(dh * xn * g, axis=0, keepdims=True)
        acc_ref[2:3, :] += jnp.sum(dh * xn * (1.0 + sc1), axis=0, keepdims=True)
        dxn = dh * g * (1.0 + sc1)
        gx_ref[...] = dx2_ref[...] + r * (dxn - xn * jnp.mean(dxn * xn, axis=-1, keepdims=True))

    tok = pl.BlockSpec((tm, D_MODEL), lambda i: (i, 0))
    return pl.pallas_call(
        body, name="inproj_bwd", grid=(t // tm,),
        out_shape=[jax.ShapeDtypeStruct((t, D_MODEL), F32), jax.ShapeDtypeStruct((SUBLANES, D_MODEL), F32),
                   jax.ShapeDtypeStruct((1, IN_WIDTH), F32)],
        in_specs=[tok, pl.BlockSpec((tm, REST_WIDTH), lambda i: (i, 0)),
                  pl.BlockSpec((tm, 2 * KV_WIDTH), lambda i: (i, 0)), tok, tok,
                  _full((SUBLANES, D_MODEL)), _full((D_MODEL, IN_WIDTH))],
        out_specs=[tok, _full((SUBLANES, D_MODEL)), _full((1, IN_WIDTH))],
        compiler_params=_params(("arbitrary",)),
    )(dq, drest, dkv, x, dx2, vec, w_in)


def _weight_grad(a, b, name, bn, tk=TOKEN_TILE):
    t, m = a.shape
    n = b.shape[1]
    tk = min(tk, t)
    nk = t // tk

    def body(a_ref, b_ref, out_ref, acc_ref):
        k = pl.program_id(1)

        @pl.when(k == 0)
        def _():
            acc_ref[...] = jnp.zeros_like(acc_ref)

        acc_ref[...] += lax.dot_general(a_ref[...], b_ref[...], TN_DIMS, preferred_element_type=F32)

        @pl.when(k == nk - 1)
        def _():
            out_ref[...] = acc_ref[...].astype(BF16)

    return pl.pallas_call(
        body, name=name, grid=(n // bn, nk),
        out_shape=jax.ShapeDtypeStruct((m, n), BF16),
        in_specs=[pl.BlockSpec((tk, m), lambda j, k: (k, 0)), pl.BlockSpec((tk, bn), lambda j, k: (k, j))],
        out_specs=pl.BlockSpec((m, bn), lambda j, k: (0, j)),
        scratch_shapes=[pltpu.VMEM((m, bn), F32)],
        compiler_params=_params(("arbitrary", "arbitrary")),
    )(a, b)


def _to_rows(v):
    n = v.shape[0]
    padded = -(-n // (SUBLANES * LANES)) * SUBLANES * LANES
    return jnp.pad(v, (0, padded - n)).reshape(padded // LANES, LANES)


def _vec_rows(*rows):
    stacked = jnp.concatenate([r.reshape(1, D_MODEL) for r in rows], axis=0)
    return jnp.pad(stacked, ((0, SUBLANES - len(rows)), (0, 0)))


def _col_blocks_to_matrix(g):
    return jnp.transpose(g, (1, 0, 2)).reshape(g.shape[1], N_DEV * g.shape[2])


def _matrix_to_col_blocks(w):
    k, n = w.shape
    return jnp.transpose(w.reshape(k, N_DEV, n // N_DEV), (1, 0, 2))


def _permute_in_cols(w):
    return jnp.concatenate([w[..., :D_MODEL], w[..., D_MODEL + 2 * KV_WIDTH:], w[..., D_MODEL:D_MODEL + 2 * KV_WIDTH]],
                           axis=-1)


def _unpermute_in_cols(w):
    return jnp.concatenate([w[..., :D_MODEL], w[..., KV_COL:], w[..., D_MODEL:KV_COL]], axis=-1)


def kernel(x, c, w_ada, b_ada, g_mix, w_in, b_in, sinks, conv_w, w_out, g_ffn, w_ffn_in, w_ffn_out, g_final, loss_target, m_w_ada, m_b_ada, m_g_mix, m_w_in, m_b_in, m_sinks, m_conv_w, m_w_out, m_g_ffn, m_w_ffn_in, m_w_ffn_out, m_g_final, v_w_ada, v_b_ada, v_g_mix, v_w_in, v_b_in, v_sinks, v_conv_w, v_w_out, v_g_ffn, v_w_ffn_in, v_w_ffn_out, v_g_final):
    ix, iy, ic = _my_place()
    me = 4 * ix + 2 * iy + ic
    xs = x[0]
    target = loss_target[0]
    ada_cols = w_ada.shape[2]
    conv_cols = conv_w.shape[2]

    first = _small_allgather(_to_rows(jnp.concatenate([c[0], conv_w[0].reshape(-1)])), "gather_c_conv")
    first = first.reshape(N_DEV, -1)
    c_all = first[:, :D_MODEL]
    conv_full = jnp.transpose(first[:, D_MODEL:D_MODEL + 3 * conv_cols].reshape(N_DEV, 3, conv_cols), (1, 0, 2))
    conv_full = conv_full.reshape(3, D_MODEL)
    b_cols = lax.dynamic_slice_in_dim(b_ada, me * ada_cols, ada_cols, axis=1)
    mod_part = _ada_forward(c_all, w_ada[0], b_cols)
    mod_all = _small_allgather(mod_part.reshape(-1, LANES), "gather_mod").reshape(N_DEV, N_DEV, ada_cols)
    mod = lax.dynamic_index_in_dim(mod_all, me, axis=1, keepdims=False).reshape(N_MOD, D_MODEL)
    sh1, sc1, ga1, sh2, sc2, ga2 = [mod[i:i + 1] for i in range(N_MOD)]

    g_in, g_fi, g_out, g_fo = _gather_weights([w_in[0], w_ffn_in[0], w_out[0], w_ffn_out[0]])
    w_in_full = _permute_in_cols(_col_blocks_to_matrix(g_in))
    w_fi_full = _col_blocks_to_matrix(g_fi)
    w_out_full = g_out.reshape(D_MODEL, D_MODEL)
    w_fo_full = g_fo.reshape(D_FF, D_MODEL)
    b_in_p = _permute_in_cols(b_in)

    z, h1 = _inproj_fwd(xs, _vec_rows(g_mix, sc1, sh1), w_in_full, b_in_p)
    attn, lse = _attn_fwd(z, sinks[0])
    merged, x2, h2 = _mix_fwd(xs, attn, z, _vec_rows(ga1, g_ffn, sc2, sh2, conv_full[0], conv_full[1], conv_full[2]),
                              w_out_full)
    gu, act = _ffn_fwd(h2, w_fi_full)
    dx3, df, dgu, acc_l = _ffn_out_loss(act, gu, x2, target, _vec_rows(ga2, g_final), w_fo_full)

    dx2, acc_f = _ffn_in_bwd(dgu, x2, dx3, _vec_rows(g_ffn, sc2), w_fi_full)
    dout, dattn, drest, acc_m = _mix_bwd(dx2, merged, attn, z,
                                         _vec_rows(ga1, conv_full[0], conv_full[1], conv_full[2]), w_out_full)
    dq, dkv, dsink = _attn_bwd(z, dattn, attn, lse, sinks[0])
    grad_x, acc_i, db_in_p = _inproj_bwd(dq, drest, dkv, xs, dx2, _vec_rows(g_mix, sc1), w_in_full)

    gw_in_p = jnp.concatenate([_weight_grad(h1, dq, "wgrad_in_q", 1024),
                               _weight_grad(h1, drest, "wgrad_in_rest", 1280),
                               _weight_grad(h1, dkv, "wgrad_in_kv", 256)], axis=1)
    gw_in = _unpermute_in_cols(gw_in_p)
    gw_fi = _weight_grad(h2, dgu, "wgrad_ffn_in", 2816)
    gw_out = _weight_grad(merged, dout, "wgrad_out", 1024)
    gw_fo = _weight_grad(act, df, "wgrad_ffn_out", 1024)

    grad_w_in = _reduce_scatter(_matrix_to_col_blocks(gw_in), "rs_w_in")
    grad_w_fi = _reduce_scatter(_matrix_to_col_blocks(gw_fi), "rs_w_ffn_in")
    grad_w_out = _reduce_scatter(gw_out.reshape(N_DEV, D_MODEL // N_DEV, D_MODEL), "rs_w_out")
    grad_w_fo = _reduce_scatter(gw_fo.reshape(N_DEV, D_FF // N_DEV, D_MODEL), "rs_w_ffn_out")

    pieces = [acc_i[0], acc_i[1], acc_m[0], acc_f[0], acc_f[1], acc_l[2],
              acc_i[2], _unpermute_in_cols(db_in_p)[0], acc_f[2], acc_l[1],
              acc_m[1], acc_m[2], acc_m[3], dsink[0], acc_l[0]]
    offsets = [0]
    for p in pieces:
        offsets.append(offsets[-1] + p.shape[0])
    packed = _small_allgather(_to_rows(jnp.concatenate(pieces)), "gather_small")
    dmod_all = packed.reshape(N_DEV, -1)[:, :N_MOD * D_MODEL]
    total = _sum_devices(packed).reshape(-1)
    part = lambda i: total[offsets[i]:offsets[i + 1]]
    g_b_ada = total[:N_MOD * D_MODEL].reshape(1, -1)
    g_g_mix, g_b_in, g_g_ffn, g_g_final = part(6).reshape(1, -1), part(7).reshape(1, -1), part(8).reshape(1, -1), part(9)
    g_conv_full = jnp.stack([part(10), part(11), part(12)])
    g_conv = lax.dynamic_slice_in_dim(g_conv_full, me * conv_cols, conv_cols, axis=1)[None]
    g_sinks = part(13)[:N_Q_HEADS].reshape(1, -1)
    loss = (0.5 / D_MODEL) * jnp.sum(part(14))
    dmod_cols = lax.dynamic_slice_in_dim(dmod_all, me * ada_cols, ada_cols, axis=1)
    g_w_ada = _ada_weight_grad(c_all, dmod_cols)

    def big(w, g, m, v, name):
        d, nm, nv = _adamw(w[0], g, m[0], v[0], name)
        return g[None], d[None], nm[None], nv[None]

    small_names = ["b_ada", "g_mix", "b_in", "sinks", "conv_w", "g_ffn", "g_final"]
    small_w = [b_ada, g_mix, b_in, sinks, conv_w, g_ffn, g_final]
    small_m = [m_b_ada, m_g_mix, m_b_in, m_sinks, m_conv_w, m_g_ffn, m_g_final]
    small_v = [v_b_ada, v_g_mix, v_b_in, v_sinks, v_conv_w, v_g_ffn, v_g_final]
    small_g = [g_b_ada, g_g_mix, g_b_in, g_sinks, g_conv, g_g_ffn, g_g_final]
    small_g = [g.reshape(w.shape) for g, w in zip(small_g, small_w)]
    flat = lambda arrs: _to_rows(jnp.concatenate([a.reshape(-1) for a in arrs]))
    sd, snm, snv = _adamw(flat(small_w), flat(small_g), flat(small_m), flat(small_v), "adamw_small")
    sizes = [w.size for w in small_w]
    starts = [sum(sizes[:i]) for i in range(len(sizes))]
    unflat = lambda a: {n: a.reshape(-1)[s:s + z_].reshape(w.shape)
                        for n, s, z_, w in zip(small_names, starts, sizes, small_w)}
    sd, snm, snv = unflat(sd), unflat(snm), unflat(snv)
    sg = dict(zip(small_names, small_g))

    res = {
        "w_ada": big(w_ada, g_w_ada, m_w_ada, v_w_ada, "adamw_w_ada"),
        "w_in": big(w_in, grad_w_in, m_w_in, v_w_in, "adamw_w_in"),
        "w_out": big(w_out, grad_w_out, m_w_out, v_w_out, "adamw_w_out"),
        "w_ffn_in": big(w_ffn_in, grad_w_fi, m_w_ffn_in, v_w_ffn_in, "adamw_w_ffn_in"),
        "w_ffn_out": big(w_ffn_out, grad_w_fo, m_w_ffn_out, v_w_ffn_out, "adamw_w_ffn_out"),
    }
    for n in small_names:
        res[n] = (sg[n], sd[n], snm[n], snv[n])
    order = ["w_ada", "b_ada", "g_mix", "w_in", "b_in", "sinks", "conv_w", "w_out", "g_ffn", "w_ffn_in", "w_ffn_out",
             "g_final"]
    outs = [loss, grad_x[None]]
    for k in range(4):
        outs += [res[n][k] for n in order]
    return tuple(outs)
```

```python
import functools
import math

import jax
import jax.numpy as jnp
from jax import lax
from jax.experimental import pallas as pl
from jax.experimental.pallas import tpu as pltpu

F32 = jnp.float32
BF16 = jnp.bfloat16

D_MODEL = 1024
HEAD_DIM = 64
N_Q_HEADS = 16
N_KV_HEADS = 2
GROUP = 8
WINDOW = 128
KV_WIDTH = N_KV_HEADS * HEAD_DIM
D_FF = 2816
IN_WIDTH = 6400
N_MOD = 6
EPS = 1e-6
N_DEV = 8
REST_WIDTH = 5 * D_MODEL
KV_COL = D_MODEL + REST_WIDTH
ATTN_SCALE = HEAD_DIM ** -0.5

ADAM_LR = 0.001
ADAM_B1 = 0.9
ADAM_B2 = 0.999
ADAM_EPS = 1e-08
ADAM_WD = 0.01
ADAM_STEP = 10

LANES = 128
SUBLANES = 8
BF16_ROWS = 16
VMEM_LIMIT = 56 * 1024 * 1024
TOKEN_TILE = 512
MESH = pl.DeviceIdType.MESH
ANY = pl.BlockSpec(memory_space=pl.ANY)

NT_DIMS = (((1,), (1,)), ((), ()))
TN_DIMS = (((0,), (0,)), ((), ()))
CHIP_FLIPS = [(0, 0), (1, 0), (0, 1), (1, 1)]


def _full(shape):
    return pl.BlockSpec(shape, lambda *_: (0,) * len(shape))


def _my_place():
    return lax.axis_index("x"), lax.axis_index("y"), lax.axis_index("c")


def _flip(v, bit):
    return 1 - v if bit else v


def _sigmoid(v):
    return 1.0 / (1.0 + jnp.exp(-v))


class _Rider:
    def __init__(self, ins, out_shapes, sem_shapes, first=None, mid=None, last=None):
        self.ins, self.out_shapes, self.sem_shapes = list(ins), list(out_shapes), list(sem_shapes)
        self.hooks = [(when, fn) for when, fn in (("first", first), ("mid", mid), ("last", last)) if fn is not None]


def _call(body, name, grid, args, in_specs, out_shape, out_specs, scratch=(), rider=None):
    n_in, n_out, n_scr = len(args), len(out_shape), len(scratch)
    r_in = rider.ins if rider else []
    r_out = rider.out_shapes if rider else []
    r_sem = rider.sem_shapes if rider else []
    nsteps = math.prod(grid)

    def full_body(*refs):
        pos = 0
        groups = []
        for size in (n_in, len(r_in), n_out, len(r_out), n_scr, len(r_sem)):
            groups.append(refs[pos:pos + size])
            pos += size
        ins, rins, outs, routs, scr, rsems = groups
        step = pl.program_id(0)
        for axis in range(1, len(grid)):
            step = step * grid[axis] + pl.program_id(axis)
        at = {"first": 0, "mid": nsteps // 2, "last": nsteps - 1}
        hooks = rider.hooks if rider else []
        for when, fn in hooks:
            if when != "last":
                pl.when(step == at[when])(functools.partial(fn, rins, routs, rsems))
        body(*ins, *outs, *scr)
        for when, fn in hooks:
            if when == "last":
                pl.when(step == at[when])(functools.partial(fn, rins, routs, rsems))

    outs = pl.pallas_call(
        full_body, name=name, grid=grid,
        out_shape=list(out_shape) + list(r_out),
        in_specs=list(in_specs) + [ANY] * len(r_in),
        out_specs=list(out_specs) + [ANY] * len(r_out),
        scratch_shapes=list(scratch) + list(r_sem),
        compiler_params=pltpu.CompilerParams(dimension_semantics=("arbitrary",) * len(grid),
                                             vmem_limit_bytes=VMEM_LIMIT),
    )(*args, *r_in)
    return list(outs[:n_out]), list(outs[n_out:])


def _gather_rider(shards):
    n = len(shards)

    def setup(outs, sems):
        x, y, c = _my_place()
        send_sems, recv_sems, _ = sems
        chips = [(1 - x, y), (x, 1 - y), (1 - x, 1 - y)]

        def block(w, place):
            return outs[w].at[4 * place[0] + 2 * place[1] + place[2]]

        def copy(w, k, place, to, src=None):
            return pltpu.make_async_remote_copy(
                src_ref=block(w, place) if src is None else src, dst_ref=block(w, place),
                send_sem=send_sems.at[w, k], recv_sem=recv_sems.at[w, k], device_id=to, device_id_type=MESH)

        return (x, y, c), (x, y, 1 - c), chips, block, copy

    def first(ins, outs, sems):
        me, sibling, chips, block, copy = setup(outs, sems)
        for w in range(n):
            pltpu.make_async_copy(ins[w], block(w, me), sems[2].at[w]).start()
            copy(w, 0, me, sibling, src=ins[w]).start()
            for j, chip in enumerate(chips):
                copy(w, 1 + j, me, (*chip, me[2]), src=ins[w]).start()

    def mid(ins, outs, sems):
        me, sibling, chips, block, copy = setup(outs, sems)
        for w in range(n):
            for j, chip in enumerate(chips):
                copy(w, 1 + j, (*chip, me[2]), me).wait_recv()
                copy(w, 4 + j, (*chip, me[2]), sibling).start()

    def last(ins, outs, sems):
        me, sibling, chips, block, copy = setup(outs, sems)
        for w in range(n):
            copy(w, 0, sibling, me).wait_recv()
            for j, chip in enumerate(chips):
                copy(w, 4 + j, (*chip, 1 - me[2]), me).wait_recv()
            copy(w, 0, me, sibling, src=ins[w]).wait_send()
            for j, chip in enumerate(chips):
                copy(w, 1 + j, me, (*chip, me[2]), src=ins[w]).wait_send()
                copy(w, 4 + j, (*chip, me[2]), sibling).wait_send()
            pltpu.make_async_copy(ins[w], block(w, me), sems[2].at[w]).wait()

    return _Rider(
        shards, [jax.ShapeDtypeStruct((N_DEV,) + s.shape, BF16) for s in shards],
        [pltpu.SemaphoreType.DMA((n, N_DEV - 1)), pltpu.SemaphoreType.DMA((n, N_DEV - 1)),
         pltpu.SemaphoreType.DMA((n,))],
        first=first, mid=mid, last=last)


def _sibling_rider(gblocks):
    n = len(gblocks)

    def copies(ins, outs, sems):
        x, y, c = _my_place()
        own_sems, send_sems, recv_sems = sems
        made = []
        for w in range(n):
            own, sib = outs[2 * w], outs[2 * w + 1]
            for f, (fx, fy) in enumerate(CHIP_FLIPS):
                chip = 4 * _flip(x, fx) + 2 * _flip(y, fy)
                made.append((
                    pltpu.make_async_copy(ins[w].at[chip + c], own.at[f], own_sems.at[w, f]),
                    pltpu.make_async_remote_copy(
                        src_ref=ins[w].at[chip + 1 - c], dst_ref=sib.at[f], send_sem=send_sems.at[w, f],
                        recv_sem=recv_sems.at[w, f], device_id=(x, y, 1 - c), device_id_type=MESH)))
        return made

    def first(ins, outs, sems):
        for local, remote in copies(ins, outs, sems):
            local.start()
            remote.start()

    def last(ins, outs, sems):
        for local, remote in copies(ins, outs, sems):
            local.wait()
            remote.wait_recv()
            remote.wait_send()

    out_shapes = []
    for g in gblocks:
        out_shapes += [jax.ShapeDtypeStruct((4,) + g.shape[1:], BF16)] * 2
    return _Rider(gblocks, out_shapes, [pltpu.SemaphoreType.DMA((n, 4))] * 3, first=first, last=last)


def _chip_rider(sums):
    n = len(sums)

    def copies(ins, outs, sems):
        x, y, c = _my_place()
        send_sems, recv_sems = sems
        made = []
        for w in range(n):
            for f in (1, 2, 3):
                fx, fy = CHIP_FLIPS[f]
                made.append(pltpu.make_async_remote_copy(
                    src_ref=ins[w].at[f - 1], dst_ref=outs[w].at[f - 1], send_sem=send_sems.at[w, f - 1],
                    recv_sem=recv_sems.at[w, f - 1], device_id=(_flip(x, fx), _flip(y, fy), c), device_id_type=MESH))
        return made

    def first(ins, outs, sems):
        for cp in copies(ins, outs, sems):
            cp.start()

    def last(ins, outs, sems):
        for cp in copies(ins, outs, sems):
            cp.wait_recv()
            cp.wait_send()

    return _Rider(sums, [jax.ShapeDtypeStruct(s.shape, BF16) for s in sums],
                  [pltpu.SemaphoreType.DMA((n, 3))] * 2, first=first, last=last)


def _small_allgather(v, name):
    rows = v.shape[0]

    def body(v_ref, out_ref, send_sems, recv_sems, local_sem):
        x, y, c = _my_place()
        me = 4 * x + 2 * y + c
        mine = pltpu.make_async_copy(v_ref, out_ref.at[me], local_sem)
        mine.start()
        sends = []
        for k in range(1, N_DEV):
            px, py, pc = _flip(x, k & 4), _flip(y, k & 2), _flip(c, k & 1)
            cp = pltpu.make_async_remote_copy(
                src_ref=v_ref, dst_ref=out_ref.at[me], send_sem=send_sems.at[k - 1], recv_sem=recv_sems.at[k - 1],
                device_id=(px, py, pc), device_id_type=MESH)
            cp.start()
            sends.append(cp)
        for k in range(1, N_DEV):
            px, py, pc = _flip(x, k & 4), _flip(y, k & 2), _flip(c, k & 1)
            pltpu.make_async_remote_copy(
                src_ref=v_ref, dst_ref=out_ref.at[4 * px + 2 * py + pc], send_sem=send_sems.at[k - 1],
                recv_sem=recv_sems.at[k - 1], device_id=(px, py, pc), device_id_type=MESH).wait_recv()
        for cp in sends:
            cp.wait_send()
        mine.wait()

    return pl.pallas_call(
        body, name=name,
        out_shape=jax.ShapeDtypeStruct((N_DEV, rows, LANES), F32),
        in_specs=[pl.BlockSpec(memory_space=pltpu.VMEM)],
        out_specs=pl.BlockSpec(memory_space=pltpu.VMEM),
        scratch_shapes=[pltpu.SemaphoreType.DMA((N_DEV - 1,)), pltpu.SemaphoreType.DMA((N_DEV - 1,)),
                        pltpu.SemaphoreType.DMA],
        compiler_params=pltpu.CompilerParams(vmem_limit_bytes=VMEM_LIMIT),
    )(v)


def _gather_first_weight(shard, others):
    n = len(others)

    def body(*refs):
        w_ref, other_refs = refs[0], refs[1:1 + n]
        out_ref, cast_refs = refs[1 + n], refs[2 + n:2 + 2 * n]
        mine_ref, send_sems, recv_sems, local_sem = refs[2 + 2 * n:]
        x, y, c = _my_place()
        me, sibling = (x, y, c), (x, y, 1 - c)
        chips = [(1 - x, y), (x, 1 - y), (1 - x, 1 - y)]

        def block(place):
            return out_ref.at[4 * place[0] + 2 * place[1] + place[2]]

        def copy(k, place, to, src=None):
            return pltpu.make_async_remote_copy(
                src_ref=block(place) if src is None else src, dst_ref=block(place),
                send_sem=send_sems.at[k], recv_sem=recv_sems.at[k], device_id=to, device_id_type=MESH)

        mine_ref[...] = w_ref[...].astype(BF16)
        local = pltpu.make_async_copy(mine_ref, block(me), local_sem)
        local.start()
        started = [copy(0, me, sibling, src=mine_ref)]
        started += [copy(1 + j, me, (*chip, c), src=mine_ref) for j, chip in enumerate(chips)]
        for cp in started:
            cp.start()
        for o_ref, c_ref in zip(other_refs, cast_refs):
            c_ref[...] = o_ref[...].astype(BF16)
        for j, chip in enumerate(chips):
            copy(1 + j, (*chip, c), me).wait_recv()
            passed = copy(4 + j, (*chip, c), sibling)
            passed.start()
            started.append(passed)
        copy(0, sibling, me).wait_recv()
        for j, chip in enumerate(chips):
            copy(4 + j, (*chip, 1 - c), me).wait_recv()
        for cp in started:
            cp.wait_send()
        local.wait()

    vmem = pl.BlockSpec(memory_space=pltpu.VMEM)
    outs = pl.pallas_call(
        body, name="gather_w_in",
        out_shape=[jax.ShapeDtypeStruct((N_DEV,) + shard.shape, BF16)]
        + [jax.ShapeDtypeStruct(o.shape, BF16) for o in others],
        in_specs=[vmem] * (1 + n),
        out_specs=[ANY] + [vmem] * n,
        scratch_shapes=[pltpu.VMEM(shard.shape, BF16), pltpu.SemaphoreType.DMA((N_DEV - 1,)),
                        pltpu.SemaphoreType.DMA((N_DEV - 1,)), pltpu.SemaphoreType.DMA],
        compiler_params=pltpu.CompilerParams(vmem_limit_bytes=VMEM_LIMIT),
    )(shard, *others)
    return outs[0], list(outs[1:])


def _carry(rider, name):
    def body(token_ref):
        token_ref[...] = jnp.zeros_like(token_ref)

    _, routs = _call(body, name, (1,), [], [], [jax.ShapeDtypeStruct((SUBLANES, LANES), F32)],
                     [_full((SUBLANES, LANES))], rider=rider)
    return routs


def _ada_forward(c_all, w_ada, b_cols):
    cols = w_ada.shape[1]

    def body(c_ref, w_ref, b_ref, out_ref):
        cf = c_ref[...]
        act = (cf * _sigmoid(cf)).astype(BF16)
        out_ref[...] = jnp.dot(act, w_ref[...].astype(BF16), preferred_element_type=F32) + b_ref[...]

    return pl.pallas_call(
        body, name="ada_forward",
        out_shape=jax.ShapeDtypeStruct((N_DEV, cols), F32),
        in_specs=[pl.BlockSpec(memory_space=pltpu.VMEM)] * 3,
        out_specs=pl.BlockSpec(memory_space=pltpu.VMEM),
        compiler_params=pltpu.CompilerParams(vmem_limit_bytes=VMEM_LIMIT),
    )(c_all, w_ada, b_cols)


def _ada_weight_grad(c_all, dmod_cols):
    cols = dmod_cols.shape[1]

    def body(c_ref, d_ref, out_ref):
        cf = c_ref[...]
        act = (cf * _sigmoid(cf)).astype(BF16)
        out_ref[...] = lax.dot_general(act, d_ref[...].astype(BF16), TN_DIMS, preferred_element_type=F32)

    return pl.pallas_call(
        body, name="ada_weight_grad",
        out_shape=jax.ShapeDtypeStruct((D_MODEL, cols), F32),
        in_specs=[pl.BlockSpec(memory_space=pltpu.VMEM)] * 2,
        out_specs=pl.BlockSpec(memory_space=pltpu.VMEM),
        compiler_params=pltpu.CompilerParams(vmem_limit_bytes=VMEM_LIMIT),
    )(c_all, dmod_cols)


def _sum_devices(packed):
    def body(p_ref, out_ref):
        total = p_ref[0]
        for d in range(1, N_DEV):
            total = total + p_ref[d]
        out_ref[...] = total

    return pl.pallas_call(
        body, name="sum_devices",
        out_shape=jax.ShapeDtypeStruct(packed.shape[1:], F32),
        in_specs=[pl.BlockSpec(memory_space=pltpu.VMEM)],
        out_specs=pl.BlockSpec(memory_space=pltpu.VMEM),
        compiler_params=pltpu.CompilerParams(vmem_limit_bytes=VMEM_LIMIT),
    )(packed)


def _row_tile(rows, multiple):
    for cand in range(min(rows, 256), 0, -1):
        if rows % cand == 0 and cand % multiple == 0:
            return cand
    return rows


def _adamw_update(w, g, m, v):
    c1 = 1.0 / (1.0 - ADAM_B1 ** ADAM_STEP)
    c2 = 1.0 / (1.0 - ADAM_B2 ** ADAM_STEP)
    nm = ADAM_B1 * m + (1.0 - ADAM_B1) * g
    nv = ADAM_B2 * v + (1.0 - ADAM_B2) * (g * g)
    delta = -ADAM_LR * ((nm * c1) / (jnp.sqrt(nv * c2) + ADAM_EPS) + ADAM_WD * w)
    return delta, nm, nv


def _adamw(w, g, m, v, name):
    rows, cols = w.shape
    tile = _row_tile(rows, SUBLANES)

    def body(w_ref, g_ref, m_ref, v_ref, d_ref, nm_ref, nv_ref):
        d_ref[...], nm_ref[...], nv_ref[...] = _adamw_update(w_ref[...], g_ref[...], m_ref[...], v_ref[...])

    spec = pl.BlockSpec((tile, cols), lambda i: (i, 0))
    outs, _ = _call(body, name, (rows // tile,), [w, g, m, v], [spec] * 4,
                    [jax.ShapeDtypeStruct((rows, cols), F32)] * 3, [spec] * 3)
    return outs


def _sibling_sum(own, sib, name):
    _, r, cdim = own.shape
    tile = _row_tile(r, BF16_ROWS)

    def body(own_ref, sib_ref, sums_ref, mine_ref):
        mine_ref[...] = own_ref[0].astype(F32) + sib_ref[0].astype(F32)
        for f in (1, 2, 3):
            sums_ref[f - 1] = (own_ref[f].astype(F32) + sib_ref[f].astype(F32)).astype(BF16)

    outs, _ = _call(
        body, name, (r // tile,), [own, sib], [pl.BlockSpec((4, tile, cdim), lambda i: (0, i, 0))] * 2,
        [jax.ShapeDtypeStruct((3, r, cdim), BF16), jax.ShapeDtypeStruct((r, cdim), F32)],
        [pl.BlockSpec((3, tile, cdim), lambda i: (0, i, 0)), pl.BlockSpec((tile, cdim), lambda i: (i, 0))])
    return outs


def _chip_sum_adamw(mine, ici, w, m, v, name):
    r, cdim = mine.shape
    tile = _row_tile(r, BF16_ROWS)

    def body(mine_ref, ici_ref, w_ref, m_ref, v_ref, g_ref, d_ref, nm_ref, nv_ref):
        g = mine_ref[...]
        for f in range(3):
            g = g + ici_ref[f].astype(F32)
        g_ref[...] = g
        d_ref[...], nm_ref[...], nv_ref[...] = _adamw_update(w_ref[...], g, m_ref[...], v_ref[...])

    spec = pl.BlockSpec((tile, cdim), lambda i: (i, 0))
    outs, _ = _call(
        body, name, (r // tile,), [mine, ici, w, m, v],
        [spec, pl.BlockSpec((3, tile, cdim), lambda i: (0, i, 0)), spec, spec, spec],
        [jax.ShapeDtypeStruct((r, cdim), F32)] * 4, [spec] * 4)
    return outs


def _inproj_fwd(x, vec, w_in, b_in, rider):
    t = x.shape[0]
    tm = min(TOKEN_TILE, t)
    chunk = 1280

    def body(x_ref, vec_ref, w_ref, b_ref, z_ref, h_ref):
        xf = x_ref[...]
        r = lax.rsqrt(jnp.mean(xf * xf, axis=-1, keepdims=True) + EPS)
        h = (xf * r) * vec_ref[0:1, :] * (1.0 + vec_ref[1:2, :]) + vec_ref[2:3, :]
        hb = h.astype(BF16)
        h_ref[...] = hb
        for n in range(IN_WIDTH // chunk):
            sl = slice(n * chunk, (n + 1) * chunk)
            z_ref[:, sl] = (jnp.dot(hb, w_ref[:, sl], preferred_element_type=F32) + b_ref[:, sl]).astype(BF16)

    return _call(
        body, "inproj_fwd", (t // tm,), [x, vec, w_in, b_in],
        [pl.BlockSpec((tm, D_MODEL), lambda i: (i, 0)), _full((SUBLANES, D_MODEL)),
         _full((D_MODEL, IN_WIDTH)), _full((1, IN_WIDTH))],
        [jax.ShapeDtypeStruct((t, IN_WIDTH), BF16), jax.ShapeDtypeStruct((t, D_MODEL), BF16)],
        [pl.BlockSpec((tm, IN_WIDTH), lambda i: (i, 0)), pl.BlockSpec((tm, D_MODEL), lambda i: (i, 0))],
        rider=rider)


def _window_mask(has_prev):
    qi = lax.broadcasted_iota(jnp.int32, (WINDOW, 2 * WINDOW), 0)
    kj = lax.broadcasted_iota(jnp.int32, (WINDOW, 2 * WINDOW), 1)
    off = jnp.where(has_prev, 0, 4 * WINDOW)
    in_prev = jnp.logical_and(kj < WINDOW, kj > qi + off)
    in_cur = jnp.logical_and(kj >= WINDOW, (kj - WINDOW) <= qi)
    return jnp.logical_or(in_prev, in_cur)


def _attn_fwd(z, sinks):
    t = z.shape[0]
    tq = min(TOKEN_TILE, t)
    nblk = tq // WINDOW

    def body(q_ref, kv_ref, sink_ref, o_ref, lse_ref):
        i = pl.program_id(0)
        lane = lax.broadcasted_iota(jnp.int32, (WINDOW, N_Q_HEADS), 1)

        def one_block(b, carry):
            row0 = pl.multiple_of(b * WINDOW, WINDOW)
            start = i * tq + b * WINDOW
            prev = pl.multiple_of(jnp.maximum(start - WINDOW, 0), WINDOW)
            cur = pl.multiple_of(start, WINDOW)
            kvw = jnp.concatenate([kv_ref[pl.ds(prev, WINDOW), :], kv_ref[pl.ds(cur, WINDOW), :]], axis=0)
            valid = _window_mask(start > 0)
            lse_blk = jnp.zeros((WINDOW, N_Q_HEADS), F32)
            for h in range(N_Q_HEADS):
                j = h // GROUP
                k = kvw[:, j * HEAD_DIM:(j + 1) * HEAD_DIM]
                v = kvw[:, KV_WIDTH + j * HEAD_DIM:KV_WIDTH + (j + 1) * HEAD_DIM]
                qh = q_ref[pl.ds(row0, WINDOW), h * HEAD_DIM:(h + 1) * HEAD_DIM]
                s = lax.dot_general(qh, k, NT_DIMS, preferred_element_type=F32) * ATTN_SCALE
                s = jnp.where(valid, s, -jnp.inf)
                sink = sink_ref[h]
                m = jnp.maximum(jnp.max(s, axis=-1, keepdims=True), sink)
                p = jnp.exp(s - m)
                denom = jnp.sum(p, axis=-1, keepdims=True) + jnp.exp(sink - m)
                o = jnp.dot(p.astype(BF16), v, preferred_element_type=F32) / denom
                o_ref[pl.ds(row0, WINDOW), h * HEAD_DIM:(h + 1) * HEAD_DIM] = o.astype(BF16)
                lse_blk = jnp.where(lane == h, m + jnp.log(denom), lse_blk)
            lse_ref[pl.ds(row0, WINDOW), :] = lse_blk
            return carry

        lax.fori_loop(0, nblk, one_block, 0)

    outs, _ = _call(
        body, "attn_fwd", (t // tq,), [z, z, sinks],
        [pl.BlockSpec((tq, D_MODEL), lambda i: (i, 0)),
         pl.BlockSpec((t, 2 * KV_WIDTH), lambda i: (0, KV_COL // (2 * KV_WIDTH))),
         pl.BlockSpec(memory_space=pltpu.SMEM)],
        [jax.ShapeDtypeStruct((t, D_MODEL), BF16), jax.ShapeDtypeStruct((t, N_Q_HEADS), F32)],
        [pl.BlockSpec((tq, D_MODEL), lambda i: (i, 0)), pl.BlockSpec((tq, N_Q_HEADS), lambda i: (i, 0))])
    return outs


HALO = BF16_ROWS


def _shift_down(u, uh, k):
    row = lax.broadcasted_iota(jnp.int32, u.shape, 0)
    out = pltpu.roll(u, k, 0)
    for j in range(k):
        out = jnp.where(row == j, uh[HALO - k + j:HALO - k + j + 1, :], out)
    return out


def _shift_up(u, nxt, k):
    n = u.shape[0]
    row = lax.broadcasted_iota(jnp.int32, u.shape, 0)
    out = pltpu.roll(u, n - k, 0)
    for j in range(k):
        out = jnp.where(row == n - k + j, nxt[j:j + 1, :], out)
    return out


def _conv_inputs(cc_ref, cx_ref, hc_ref, hx_ref, first_tile):
    cc = cc_ref[...].astype(F32)
    cx = cx_ref[...].astype(F32)
    u = cc * cx
    uh = jnp.where(first_tile, 0.0, hc_ref[...].astype(F32) * hx_ref[...].astype(F32))
    return cc, cx, u, _shift_down(u, uh, 1), _shift_down(u, uh, 2)


def _z_specs(tm, order):
    per_tile = tm // HALO
    cols = [pl.BlockSpec((tm, D_MODEL), functools.partial(lambda i, j: (order(i), j), j=j)) for j in range(1, 6)]
    halos = [pl.BlockSpec((HALO, D_MODEL),
                          functools.partial(lambda i, j: (jnp.maximum(order(i) * per_tile - 1, 0), j), j=j))
             for j in (2, 3)]
    return cols + halos


def _mix_fwd(x, attn, z, vec, w_out):
    t = x.shape[0]
    tm = min(TOKEN_TILE, t)

    def body(x_ref, a_ref, cb_ref, cc_ref, cx_ref, ga_ref, gc_ref, hc_ref, hx_ref, vec_ref, w_ref,
             m_ref, x2_ref, h2_ref):
        i = pl.program_id(0)
        _, _, u, u1, u2 = _conv_inputs(cc_ref, cx_ref, hc_ref, hx_ref, i == 0)
        cv = vec_ref[4:5, :] * u2 + vec_ref[5:6, :] * u1 + vec_ref[6:7, :] * u
        conv = cb_ref[...].astype(F32) * cv
        merged = (_sigmoid(ga_ref[...].astype(F32)) * a_ref[...].astype(F32)
                  + _sigmoid(gc_ref[...].astype(F32)) * conv)
        mb = merged.astype(BF16)
        m_ref[...] = mb
        o = jnp.dot(mb, w_ref[...], preferred_element_type=F32)
        x2 = x_ref[...] + vec_ref[0:1, :] * o
        x2_ref[...] = x2
        r = lax.rsqrt(jnp.mean(x2 * x2, axis=-1, keepdims=True) + EPS)
        h2 = (x2 * r) * vec_ref[1:2, :] * (1.0 + vec_ref[2:3, :]) + vec_ref[3:4, :]
        h2_ref[...] = h2.astype(BF16)

    tok = pl.BlockSpec((tm, D_MODEL), lambda i: (i, 0))
    outs, _ = _call(
        body, "mix_fwd", (t // tm,), [x, attn, z, z, z, z, z, z, z, vec, w_out],
        [tok, tok] + _z_specs(tm, lambda i: i) + [_full((SUBLANES, D_MODEL)), _full((D_MODEL, D_MODEL))],
        [jax.ShapeDtypeStruct((t, D_MODEL), BF16), jax.ShapeDtypeStruct((t, D_MODEL), F32),
         jax.ShapeDtypeStruct((t, D_MODEL), BF16)],
        [tok, tok, tok])
    return outs


def _ffn_fwd(h2, w_ffn_in):
    t = h2.shape[0]
    tm = min(TOKEN_TILE, t)

    def body(h_ref, w_ref, gu_ref, a_ref):
        hb = h_ref[...]
        g = jnp.dot(hb, w_ref[:, :D_FF], preferred_element_type=F32)
        u = jnp.dot(hb, w_ref[:, D_FF:], preferred_element_type=F32)
        gu_ref[:, :D_FF] = g.astype(BF16)
        gu_ref[:, D_FF:] = u.astype(BF16)
        a_ref[...] = (g * _sigmoid(g) * u).astype(BF16)

    outs, _ = _call(
        body, "ffn_fwd", (t // tm,), [h2, w_ffn_in],
        [pl.BlockSpec((tm, D_MODEL), lambda i: (i, 0)), _full((D_MODEL, 2 * D_FF))],
        [jax.ShapeDtypeStruct((t, 2 * D_FF), BF16), jax.ShapeDtypeStruct((t, D_FF), BF16)],
        [pl.BlockSpec((tm, 2 * D_FF), lambda i: (i, 0)), pl.BlockSpec((tm, D_FF), lambda i: (i, 0))])
    return outs


def _ffn_out_loss(a, gu, x2, target, vec, w_ffn_out):
    t = a.shape[0]
    tm = min(TOKEN_TILE, t)

    def body(a_ref, gu_ref, x2_ref, t_ref, vec_ref, w_ref, dx3_ref, df_ref, dgu_ref, acc_ref):
        @pl.when(pl.program_id(0) == 0)
        def _():
            acc_ref[...] = jnp.zeros_like(acc_ref)

        ga2 = vec_ref[0:1, :]
        gf = vec_ref[1:2, :]
        f = jnp.dot(a_ref[...], w_ref[...], preferred_element_type=F32)
        x3 = x2_ref[...] + ga2 * f
        r = lax.rsqrt(jnp.mean(x3 * x3, axis=-1, keepdims=True) + EPS)
        xn = x3 * r
        err = xn * gf - t_ref[...]
        dy = err * (1.0 / D_MODEL)
        dxn = dy * gf
        dx3 = r * (dxn - xn * jnp.mean(dxn * xn, axis=-1, keepdims=True))
        dx3_ref[...] = dx3
        acc_ref[0:1, :] += jnp.sum(err * err, axis=0, keepdims=True)
        acc_ref[1:2, :] += jnp.sum(dy * xn, axis=0, keepdims=True)
        acc_ref[2:3, :] += jnp.sum(dx3 * f, axis=0, keepdims=True)
        df = (dx3 * ga2).astype(BF16)
        df_ref[...] = df
        da = lax.dot_general(df, w_ref[...], NT_DIMS, preferred_element_type=F32)
        g = gu_ref[:, :D_FF].astype(F32)
        u = gu_ref[:, D_FF:].astype(F32)
        sg = _sigmoid(g)
        dgu_ref[:, :D_FF] = (da * u * (sg * (1.0 + g * (1.0 - sg)))).astype(BF16)
        dgu_ref[:, D_FF:] = (da * (g * sg)).astype(BF16)

    tok = pl.BlockSpec((tm, D_MODEL), lambda i: (i, 0))
    outs, _ = _call(
        body, "ffn_out_loss", (t // tm,), [a, gu, x2, target, vec, w_ffn_out],
        [pl.BlockSpec((tm, D_FF), lambda i: (i, 0)), pl.BlockSpec((tm, 2 * D_FF), lambda i: (i, 0)),
         tok, tok, _full((SUBLANES, D_MODEL)), _full((D_FF, D_MODEL))],
        [jax.ShapeDtypeStruct((t, D_MODEL), F32), jax.ShapeDtypeStruct((t, D_MODEL), BF16),
         jax.ShapeDtypeStruct((t, 2 * D_FF), BF16), jax.ShapeDtypeStruct((SUBLANES, D_MODEL), F32)],
        [tok, tok, pl.BlockSpec((tm, 2 * D_FF), lambda i: (i, 0)), _full((SUBLANES, D_MODEL))])
    return outs


def _ffn_in_bwd(dgu, x2, dx3, vec, w_ffn_in, rider):
    t = x2.shape[0]
    tm = min(TOKEN_TILE, t)

    def body(dgu_ref, x2_ref, dx3_ref, vec_ref, wf_ref, dx2_ref, acc_ref):
        @pl.when(pl.program_id(0) == 0)
        def _():
            acc_ref[...] = jnp.zeros_like(acc_ref)

        gffn = vec_ref[0:1, :]
        sc2 = vec_ref[1:2, :]
        dh2 = lax.dot_general(dgu_ref[...], wf_ref[...], NT_DIMS, preferred_element_type=F32)
        x2 = x2_ref[...]
        r = lax.rsqrt(jnp.mean(x2 * x2, axis=-1, keepdims=True) + EPS)
        xn = x2 * r
        acc_ref[0:1, :] += jnp.sum(dh2, axis=0, keepdims=True)
        acc_ref[1:2, :] += jnp.sum(dh2 * xn * gffn, axis=0, keepdims=True)
        acc_ref[2:3, :] += jnp.sum(dh2 * xn * (1.0 + sc2), axis=0, keepdims=True)
        dxn = dh2 * gffn * (1.0 + sc2)
        dx2_ref[...] = dx3_ref[...] + r * (dxn - xn * jnp.mean(dxn * xn, axis=-1, keepdims=True))

    tok = pl.BlockSpec((tm, D_MODEL), lambda i: (i, 0))
    return _call(
        body, "ffn_in_bwd", (t // tm,), [dgu, x2, dx3, vec, w_ffn_in],
        [pl.BlockSpec((tm, 2 * D_FF), lambda i: (i, 0)), tok, tok, _full((SUBLANES, D_MODEL)),
         _full((D_MODEL, 2 * D_FF))],
        [jax.ShapeDtypeStruct((t, D_MODEL), F32), jax.ShapeDtypeStruct((SUBLANES, D_MODEL), F32)],
        [tok, _full((SUBLANES, D_MODEL))], rider=rider)


def _mix_bwd(dx2, merged, attn, z, vec, w_out, rider):
    t = dx2.shape[0]
    tm = min(TOKEN_TILE, t)
    nt = t // tm
    rev = lambda i: nt - 1 - i

    def body(dx2_ref, m_ref, a_ref, cb_ref, cc_ref, cx_ref, ga_ref, gc_ref, hc_ref, hx_ref,
             vec_ref, wo_ref, do_ref, da_ref, dr_ref, acc_ref, carry_ref):
        i = pl.program_id(0)

        @pl.when(i == 0)
        def _():
            acc_ref[...] = jnp.zeros_like(acc_ref)
            carry_ref[...] = jnp.zeros_like(carry_ref)

        ga1 = vec_ref[0:1, :]
        w0, w1, w2 = vec_ref[1:2, :], vec_ref[2:3, :], vec_ref[3:4, :]
        dx2 = dx2_ref[...]
        o = jnp.dot(m_ref[...], wo_ref[...], preferred_element_type=F32)
        acc_ref[0:1, :] += jnp.sum(dx2 * o, axis=0, keepdims=True)
        do = (dx2 * ga1).astype(BF16)
        do_ref[...] = do
        dm = lax.dot_general(do, wo_ref[...], NT_DIMS, preferred_element_type=F32)

        cc, cx, u, u1, u2 = _conv_inputs(cc_ref, cx_ref, hc_ref, hx_ref, i == nt - 1)
        cv = w0 * u2 + w1 * u1 + w2 * u
        cb = cb_ref[...].astype(F32)
        sa = _sigmoid(ga_ref[...].astype(F32))
        sc = _sigmoid(gc_ref[...].astype(F32))
        attn = a_ref[...].astype(F32)
        da_ref[...] = (dm * sa).astype(BF16)
        dconv = dm * sc
        dr_ref[:, 3 * D_MODEL:4 * D_MODEL] = (dm * attn * sa * (1.0 - sa)).astype(BF16)
        dr_ref[:, 4 * D_MODEL:5 * D_MODEL] = (dconv * (cb * cv) * (1.0 - sc)).astype(BF16)
        dr_ref[:, 0:D_MODEL] = (dconv * cv).astype(BF16)
        dcv = dconv * cb
        acc_ref[1:2, :] += jnp.sum(dcv * u2, axis=0, keepdims=True)
        acc_ref[2:3, :] += jnp.sum(dcv * u1, axis=0, keepdims=True)
        acc_ref[3:4, :] += jnp.sum(dcv * u, axis=0, keepdims=True)
        nxt = carry_ref[...]
        du = w2 * dcv + w1 * _shift_up(dcv, nxt, 1) + w0 * _shift_up(dcv, nxt, 2)
        carry_ref[...] = dcv[0:SUBLANES, :]
        dr_ref[:, D_MODEL:2 * D_MODEL] = (du * cx).astype(BF16)
        dr_ref[:, 2 * D_MODEL:3 * D_MODEL] = (du * cc).astype(BF16)

    tok = pl.BlockSpec((tm, D_MODEL), lambda i: (rev(i), 0))
    return _call(
        body, "mix_bwd", (nt,), [dx2, merged, attn, z, z, z, z, z, z, z, vec, w_out],
        [tok, tok, tok] + _z_specs(tm, rev) + [_full((SUBLANES, D_MODEL)), _full((D_MODEL, D_MODEL))],
        [jax.ShapeDtypeStruct((t, D_MODEL), BF16), jax.ShapeDtypeStruct((t, D_MODEL), BF16),
         jax.ShapeDtypeStruct((t, REST_WIDTH), BF16), jax.ShapeDtypeStruct((SUBLANES, D_MODEL), F32)],
        [tok, tok, pl.BlockSpec((tm, REST_WIDTH), lambda i: (rev(i), 0)), _full((SUBLANES, D_MODEL))],
        scratch=[pltpu.VMEM((SUBLANES, D_MODEL), F32)], rider=rider)


def _attn_bwd(z, dattn, attn, lse, sinks, rider):
    t = z.shape[0]
    tq = min(TOKEN_TILE, t)
    nblk = tq // WINDOW
    nt = t // tq

    def body(q_ref, kv_ref, do_ref, o_ref, lse_ref, sink_ref, dq_ref, dkv_ref, ds_ref, acc_ref):
        i = pl.program_id(0)

        @pl.when(i == 0)
        def _():
            acc_ref[...] = jnp.zeros_like(acc_ref)
            ds_ref[...] = jnp.zeros_like(ds_ref)

        lane = lax.broadcasted_iota(jnp.int32, (1, LANES), 1)

        def one_block(b, dsink):
            row0 = pl.multiple_of(b * WINDOW, WINDOW)
            start = i * tq + b * WINDOW
            prev = pl.multiple_of(jnp.maximum(start - WINDOW, 0), WINDOW)
            cur = pl.multiple_of(start, WINDOW)
            kvw = jnp.concatenate([kv_ref[pl.ds(prev, WINDOW), :], kv_ref[pl.ds(cur, WINDOW), :]], axis=0)
            valid = _window_mask(start > 0)
            lse_blk = lse_ref[pl.ds(row0, WINDOW), :]
            parts = []
            for j in range(N_KV_HEADS):
                k = kvw[:, j * HEAD_DIM:(j + 1) * HEAD_DIM]
                v = kvw[:, KV_WIDTH + j * HEAD_DIM:KV_WIDTH + (j + 1) * HEAD_DIM]
                dk = jnp.zeros((2 * WINDOW, HEAD_DIM), F32)
                dv = jnp.zeros((2 * WINDOW, HEAD_DIM), F32)
                for g in range(GROUP):
                    h = j * GROUP + g
                    cols = slice(h * HEAD_DIM, (h + 1) * HEAD_DIM)
                    qh = q_ref[pl.ds(row0, WINDOW), cols]
                    doh = do_ref[pl.ds(row0, WINDOW), cols]
                    oh = o_ref[pl.ds(row0, WINDOW), cols]
                    lse_h = lse_blk[:, h:h + 1]
                    s = lax.dot_general(qh, k, NT_DIMS, preferred_element_type=F32) * ATTN_SCALE
                    p = jnp.where(valid, jnp.exp(s - lse_h), 0.0)
                    delta = jnp.sum(doh.astype(F32) * oh.astype(F32), axis=-1, keepdims=True)
                    dp = lax.dot_general(doh, v, NT_DIMS, preferred_element_type=F32)
                    dsb = (p * (dp - delta)).astype(BF16)
                    dq = jnp.dot(dsb, k, preferred_element_type=F32) * ATTN_SCALE
                    dq_ref[pl.ds(row0, WINDOW), cols] = dq.astype(BF16)
                    dk = dk + lax.dot_general(dsb, qh, TN_DIMS, preferred_element_type=F32)
                    dv = dv + lax.dot_general(p.astype(BF16), doh, TN_DIMS, preferred_element_type=F32)
                    psink = jnp.exp(sink_ref[h] - lse_h)
                    dsink = dsink - jnp.where(lane == h, jnp.sum(psink * delta), 0.0)
                parts.append((dk * ATTN_SCALE, dv))
            blk = jnp.concatenate([parts[0][0], parts[1][0], parts[0][1], parts[1][1]], axis=1)
            acc_ref[pl.ds(prev, WINDOW), :] += blk[:WINDOW, :]
            acc_ref[pl.ds(cur, WINDOW), :] += blk[WINDOW:, :]
            return dsink

        dsink = lax.fori_loop(0, nblk, one_block, jnp.zeros((1, LANES), F32))
        ds_ref[0:1, :] += dsink

        @pl.when(i == nt - 1)
        def _():
            dkv_ref[...] = acc_ref[...].astype(BF16)

    tok = pl.BlockSpec((tq, D_MODEL), lambda i: (i, 0))
    return _call(
        body, "attn_bwd", (nt,), [z, z, dattn, attn, lse, sinks],
        [tok, pl.BlockSpec((t, 2 * KV_WIDTH), lambda i: (0, KV_COL // (2 * KV_WIDTH))), tok, tok,
         pl.BlockSpec((tq, N_Q_HEADS), lambda i: (i, 0)), pl.BlockSpec(memory_space=pltpu.SMEM)],
        [jax.ShapeDtypeStruct((t, D_MODEL), BF16), jax.ShapeDtypeStruct((t, 2 * KV_WIDTH), BF16),
         jax.ShapeDtypeStruct((SUBLANES, LANES), F32)],
        [tok, _full((t, 2 * KV_WIDTH)), _full((SUBLANES, LANES))],
        scratch=[pltpu.VMEM((t, 2 * KV_WIDTH), F32)], rider=rider)


def _inproj_bwd(dq, drest, dkv, x, dx2, vec, w_in, rider):
    t = x.shape[0]
    tm = min(TOKEN_TILE, t)

    def body(dq_ref, dr_ref, dkv_ref, x_ref, dx2_ref, vec_ref, w_ref, gx_ref, acc_ref, db_ref):
        @pl.when(pl.program_id(0) == 0)
        def _():
            acc_ref[...] = jnp.zeros_like(acc_ref)
            db_ref[...] = jnp.zeros_like(db_ref)

        g = vec_ref[0:1, :]
        sc1 = vec_ref[1:2, :]
        dqb, drb, dkvb = dq_ref[...], dr_ref[...], dkv_ref[...]
        dh = lax.dot_general(dqb, w_ref[:, :D_MODEL], NT_DIMS, preferred_element_type=F32)
        dh = dh + lax.dot_general(drb, w_ref[:, D_MODEL:KV_COL], NT_DIMS, preferred_element_type=F32)
        dh = dh + lax.dot_general(dkvb, w_ref[:, KV_COL:], NT_DIMS, preferred_element_type=F32)
        db_ref[:, :D_MODEL] += jnp.sum(dqb.astype(F32), axis=0, keepdims=True)
        db_ref[:, D_MODEL:KV_COL] += jnp.sum(drb.astype(F32), axis=0, keepdims=True)
        db_ref[:, KV_COL:] += jnp.sum(dkvb.astype(F32), axis=0, keepdims=True)
        xf = x_ref[...]
        r = lax.rsqrt(jnp.mean(xf * xf, axis=-1, keepdims=True) + EPS)
        xn = xf * r
        acc_ref[0:1, :] += jnp.sum(dh, axis=0, keepdims=True)
        acc_ref[1:2, :] += jnp.sum(dh * xn * g, axis=0, keepdims=True)
        acc_ref[2:3, :] += jnp.sum(dh * xn * (1.0 + sc1), axis=0, keepdims=True)
        dxn = dh * g * (1.0 + sc1)
        gx_ref[...] = dx2_ref[...] + r * (dxn - xn * jnp.mean(dxn * xn, axis=-1, keepdims=True))

    tok = pl.BlockSpec((tm, D_MODEL), lambda i: (i, 0))
    return _call(
        body, "inproj_bwd", (t // tm,), [dq, drest, dkv, x, dx2, vec, w_in],
        [tok, pl.BlockSpec((tm, REST_WIDTH), lambda i: (i, 0)),
         pl.BlockSpec((tm, 2 * KV_WIDTH), lambda i: (i, 0)), tok, tok,
         _full((SUBLANES, D_MODEL)), _full((D_MODEL, IN_WIDTH))],
        [jax.ShapeDtypeStruct((t, D_MODEL), F32), jax.ShapeDtypeStruct((SUBLANES, D_MODEL), F32),
         jax.ShapeDtypeStruct((1, IN_WIDTH), F32)],
        [tok, _full((SUBLANES, D_MODEL)), _full((1, IN_WIDTH))], rider=rider)


def _weight_grad(a, b, name, bn, rider=None):
    t, m = a.shape
    n = b.shape[1]
    tk = min(TOKEN_TILE, t)
    nk = t // tk

    def body(a_ref, b_ref, out_ref, acc_ref):
        k = pl.program_id(1)

        @pl.when(k == 0)
        def _():
            acc_ref[...] = jnp.zeros_like(acc_ref)

        acc_ref[...] += lax.dot_general(a_ref[...], b_ref[...], TN_DIMS, preferred_element_type=F32)

        @pl.when(k == nk - 1)
        def _():
            out_ref[...] = acc_ref[...].astype(BF16)

    outs, routs = _call(
        body, name, (n // bn, nk), [a, b],
        [pl.BlockSpec((tk, m), lambda j, k: (k, 0)), pl.BlockSpec((tk, bn), lambda j, k: (k, j))],
        [jax.ShapeDtypeStruct((m, n), BF16)], [pl.BlockSpec((m, bn), lambda j, k: (0, j))],
        scratch=[pltpu.VMEM((m, bn), F32)], rider=rider)
    return outs[0], routs


def _to_rows(v):
    n = v.shape[0]
    padded = -(-n // (SUBLANES * LANES)) * SUBLANES * LANES
    return jnp.pad(v, (0, padded - n)).reshape(padded // LANES, LANES)


def _vec_rows(*rows):
    stacked = jnp.concatenate([r.reshape(1, D_MODEL) for r in rows], axis=0)
    return jnp.pad(stacked, ((0, SUBLANES - len(rows)), (0, 0)))


def _col_blocks_to_matrix(g):
    return jnp.transpose(g, (1, 0, 2)).reshape(g.shape[1], N_DEV * g.shape[2])


def _matrix_to_col_blocks(w):
    k, n = w.shape
    return jnp.transpose(w.reshape(k, N_DEV, n // N_DEV), (1, 0, 2))


def _permute_in_cols(w):
    return jnp.concatenate([w[..., :D_MODEL], w[..., D_MODEL + 2 * KV_WIDTH:], w[..., D_MODEL:D_MODEL + 2 * KV_WIDTH]],
                           axis=-1)


def _unpermute_in_cols(w):
    return jnp.concatenate([w[..., :D_MODEL], w[..., KV_COL:], w[..., D_MODEL:KV_COL]], axis=-1)


def kernel(x, c, w_ada, b_ada, g_mix, w_in, b_in, sinks, conv_w, w_out, g_ffn, w_ffn_in, w_ffn_out, g_final, loss_target, m_w_ada, m_b_ada, m_g_mix, m_w_in, m_b_in, m_sinks, m_conv_w, m_w_out, m_g_ffn, m_w_ffn_in, m_w_ffn_out, m_g_final, v_w_ada, v_b_ada, v_g_mix, v_w_in, v_b_in, v_sinks, v_conv_w, v_w_out, v_g_ffn, v_w_ffn_in, v_w_ffn_out, v_g_final):
    ix, iy, ic = _my_place()
    me = 4 * ix + 2 * iy + ic
    xs = x[0]
    target = loss_target[0]
    ada_cols = w_ada.shape[2]
    conv_cols = conv_w.shape[2]

    first = _small_allgather(_to_rows(jnp.concatenate([c[0], conv_w[0].reshape(-1)])), "gather_c_conv")
    first = first.reshape(N_DEV, -1)
    c_all = first[:, :D_MODEL]
    conv_full = jnp.transpose(first[:, D_MODEL:D_MODEL + 3 * conv_cols].reshape(N_DEV, 3, conv_cols), (1, 0, 2))
    conv_full = conv_full.reshape(3, D_MODEL)
    b_cols = lax.dynamic_slice_in_dim(b_ada, me * ada_cols, ada_cols, axis=1)
    mod_part = _ada_forward(c_all, w_ada[0], b_cols)
    mod_all = _small_allgather(mod_part.reshape(-1, LANES), "gather_mod").reshape(N_DEV, N_DEV, ada_cols)
    mod = lax.dynamic_index_in_dim(mod_all, me, axis=1, keepdims=False).reshape(N_MOD, D_MODEL)
    sh1, sc1, ga1, sh2, sc2, ga2 = [mod[i:i + 1] for i in range(N_MOD)]

    g_in, later = _gather_first_weight(w_in[0], [w_ffn_in[0], w_out[0], w_ffn_out[0]])
    w_in_full = _permute_in_cols(_col_blocks_to_matrix(g_in))
    b_in_p = _permute_in_cols(b_in)
    (z, h1), (g_fi, g_out, g_fo) = _inproj_fwd(xs, _vec_rows(g_mix, sc1, sh1), w_in_full, b_in_p,
                                              _gather_rider(later))
    w_fi_full = _col_blocks_to_matrix(g_fi)
    w_out_full = g_out.reshape(D_MODEL, D_MODEL)
    w_fo_full = g_fo.reshape(D_FF, D_MODEL)
    attn, lse = _attn_fwd(z, sinks[0])
    merged, x2, h2 = _mix_fwd(xs, attn, z, _vec_rows(ga1, g_ffn, sc2, sh2, conv_full[0], conv_full[1], conv_full[2]),
                              w_out_full)
    gu, act = _ffn_fwd(h2, w_fi_full)
    dx3, df, dgu, acc_l = _ffn_out_loss(act, gu, x2, target, _vec_rows(ga2, g_final), w_fo_full)

    gw_fo, _ = _weight_grad(act, df, "wgrad_ffn_out", 1024)
    gw_fi, _ = _weight_grad(h2, dgu, "wgrad_ffn_in", 2816)
    (dx2, acc_f), p1 = _ffn_in_bwd(
        dgu, x2, dx3, _vec_rows(g_ffn, sc2), w_fi_full,
        _sibling_rider([gw_fo.reshape(N_DEV, D_FF // N_DEV, D_MODEL), _matrix_to_col_blocks(gw_fi)]))
    sums_fo, mine_fo = _sibling_sum(p1[0], p1[1], "sibling_sum_ffn_out")
    sums_fi, mine_fi = _sibling_sum(p1[2], p1[3], "sibling_sum_ffn_in")
    (dout, dattn, drest, acc_m), (ici_fo, ici_fi) = _mix_bwd(
        dx2, merged, attn, z, _vec_rows(ga1, conv_full[0], conv_full[1], conv_full[2]), w_out_full,
        _chip_rider([sums_fo, sums_fi]))
    gw_out, _ = _weight_grad(merged, dout, "wgrad_out", 1024)
    (dq, dkv, dsink), p1 = _attn_bwd(z, dattn, attn, lse, sinks[0],
                                     _sibling_rider([gw_out.reshape(N_DEV, D_MODEL // N_DEV, D_MODEL)]))
    sums_out, mine_out = _sibling_sum(p1[0], p1[1], "sibling_sum_out")
    gw_rest, (ici_out,) = _weight_grad(h1, drest, "wgrad_in_rest", 1280, _chip_rider([sums_out]))
    gw_q, _ = _weight_grad(h1, dq, "wgrad_in_q", 1024)
    gw_kv, _ = _weight_grad(h1, dkv, "wgrad_in_kv", 256)
    gw_in = _unpermute_in_cols(jnp.concatenate([gw_q, gw_rest, gw_kv], axis=1))
    p1 = _carry(_sibling_rider([_matrix_to_col_blocks(gw_in)]), "sibling_w_in")
    sums_in, mine_in = _sibling_sum(p1[0], p1[1], "sibling_sum_in")
    (grad_x, acc_i, db_in_p), (ici_in,) = _inproj_bwd(dq, drest, dkv, xs, dx2, _vec_rows(g_mix, sc1), w_in_full,
                                                      _chip_rider([sums_in]))

    pieces = [acc_i[0], acc_i[1], acc_m[0], acc_f[0], acc_f[1], acc_l[2],
              acc_i[2], _unpermute_in_cols(db_in_p)[0], acc_f[2], acc_l[1],
              acc_m[1], acc_m[2], acc_m[3], dsink[0], acc_l[0]]
    offsets = [0]
    for p in pieces:
        offsets.append(offsets[-1] + p.shape[0])
    packed = _small_allgather(_to_rows(jnp.concatenate(pieces)), "gather_small")
    dmod_all = packed.reshape(N_DEV, -1)[:, :N_MOD * D_MODEL]
    total = _sum_devices(packed).reshape(-1)
    part = lambda i: total[offsets[i]:offsets[i + 1]]
    g_b_ada = total[:N_MOD * D_MODEL].reshape(1, -1)
    g_g_mix, g_b_in, g_g_ffn, g_g_final = part(6).reshape(1, -1), part(7).reshape(1, -1), part(8).reshape(1, -1), part(9)
    g_conv_full = jnp.stack([part(10), part(11), part(12)])
    g_conv = lax.dynamic_slice_in_dim(g_conv_full, me * conv_cols, conv_cols, axis=1)[None]
    g_sinks = part(13)[:N_Q_HEADS].reshape(1, -1)
    loss = (0.5 / D_MODEL) * jnp.sum(part(14))
    dmod_cols = lax.dynamic_slice_in_dim(dmod_all, me * ada_cols, ada_cols, axis=1)
    g_w_ada = _ada_weight_grad(c_all, dmod_cols)

    def reduced(mine, ici, w, m, v, name):
        return tuple(o[None] for o in _chip_sum_adamw(mine, ici, w[0], m[0], v[0], name))

    d_ada, nm_ada, nv_ada = _adamw(w_ada[0], g_w_ada, m_w_ada[0], v_w_ada[0], "adamw_w_ada")
    small_names = ["b_ada", "g_mix", "b_in", "sinks", "conv_w", "g_ffn", "g_final"]
    small_w = [b_ada, g_mix, b_in, sinks, conv_w, g_ffn, g_final]
    small_m = [m_b_ada, m_g_mix, m_b_in, m_sinks, m_conv_w, m_g_ffn, m_g_final]
    small_v = [v_b_ada, v_g_mix, v_b_in, v_sinks, v_conv_w, v_g_ffn, v_g_final]
    small_g = [g_b_ada, g_g_mix, g_b_in, g_sinks, g_conv, g_g_ffn, g_g_final]
    small_g = [g.reshape(w.shape) for g, w in zip(small_g, small_w)]
    flat = lambda arrs: _to_rows(jnp.concatenate([a.reshape(-1) for a in arrs]))
    sd, snm, snv = _adamw(flat(small_w), flat(small_g), flat(small_m), flat(small_v), "adamw_small")
    sizes = [w.size for w in small_w]
    starts = [sum(sizes[:i]) for i in range(len(sizes))]
    unflat = lambda a: {n: a.reshape(-1)[s:s + z_].reshape(w.shape)
                        for n, s, z_, w in zip(small_names, starts, sizes, small_w)}
    sd, snm, snv = unflat(sd), unflat(snm), unflat(snv)
    sg = dict(zip(small_names, small_g))

    res = {
        "w_ada": (g_w_ada[None], d_ada[None], nm_ada[None], nv_ada[None]),
        "w_in": reduced(mine_in, ici_in, w_in, m_w_in, v_w_in, "adamw_w_in"),
        "w_out": reduced(mine_out, ici_out, w_out, m_w_out, v_w_out, "adamw_w_out"),
        "w_ffn_in": reduced(mine_fi, ici_fi, w_ffn_in, m_w_ffn_in, v_w_ffn_in, "adamw_w_ffn_in"),
        "w_ffn_out": reduced(mine_fo, ici_fo, w_ffn_out, m_w_ffn_out, v_w_ffn_out, "adamw_w_ffn_out"),
    }
    for n in small_names:
        res[n] = (sg[n], sd[n], snm[n], snv[n])
    order = ["w_ada", "b_ada", "g_mix", "w_in", "b_in", "sinks", "conv_w", "w_out", "g_ffn", "w_ffn_in", "w_ffn_out",
             "g_final"]
    outs = [loss, grad_x[None]]
    for k in range(4):
        outs += [res[n][k] for n in order]
    return tuple(outs)
```

```python
import functools
import math

import jax
import jax.numpy as jnp
from jax import lax
from jax.experimental import pallas as pl
from jax.experimental.pallas import tpu as pltpu

F32 = jnp.float32
BF16 = jnp.bfloat16

D_MODEL = 1024
HEAD_DIM = 64
N_Q_HEADS = 16
N_KV_HEADS = 2
GROUP = 8
WINDOW = 128
KV_WIDTH = N_KV_HEADS * HEAD_DIM
D_FF = 2816
IN_WIDTH = 6400
N_MOD = 6
EPS = 1e-6
N_DEV = 8
REST_WIDTH = 5 * D_MODEL
KV_COL = D_MODEL + REST_WIDTH
ATTN_SCALE = HEAD_DIM ** -0.5

ADAM_LR = 0.001
ADAM_B1 = 0.9
ADAM_B2 = 0.999
ADAM_EPS = 1e-08
ADAM_WD = 0.01
ADAM_STEP = 10

LANES = 128
SUBLANES = 8
BF16_ROWS = 16
VMEM_LIMIT = 56 * 1024 * 1024
TOKEN_TILE = 512
MESH = pl.DeviceIdType.MESH
ANY = pl.BlockSpec(memory_space=pl.ANY)

NT_DIMS = (((1,), (1,)), ((), ()))
TN_DIMS = (((0,), (0,)), ((), ()))
CHIP_FLIPS = [(0, 0), (1, 0), (0, 1), (1, 1)]


def _full(shape):
    return pl.BlockSpec(shape, lambda *_: (0,) * len(shape))


def _my_place():
    return lax.axis_index("x"), lax.axis_index("y"), lax.axis_index("c")


def _flip(v, bit):
    return 1 - v if bit else v


def _sigmoid(v):
    return 1.0 / (1.0 + jnp.exp(-v))


class _Rider:
    def __init__(self, ins, out_shapes, sem_shapes, first=None, mid=None, last=None, ins_in_vmem=False):
        self.ins, self.out_shapes, self.sem_shapes = list(ins), list(out_shapes), list(sem_shapes)
        self.in_specs = [_full(a.shape) if ins_in_vmem else ANY for a in self.ins]
        self.hooks = [(when, fn) for when, fn in (("first", first), ("mid", mid), ("last", last)) if fn is not None]


def _call(body, name, grid, args, in_specs, out_shape, out_specs, scratch=(), rider=None):
    n_in, n_out, n_scr = len(args), len(out_shape), len(scratch)
    r_in = rider.ins if rider else []
    r_out = rider.out_shapes if rider else []
    r_sem = rider.sem_shapes if rider else []
    nsteps = math.prod(grid)

    def full_body(*refs):
        pos = 0
        groups = []
        for size in (n_in, len(r_in), n_out, len(r_out), n_scr, len(r_sem)):
            groups.append(refs[pos:pos + size])
            pos += size
        ins, rins, outs, routs, scr, rsems = groups
        step = pl.program_id(0)
        for axis in range(1, len(grid)):
            step = step * grid[axis] + pl.program_id(axis)
        at = {"first": 0, "mid": nsteps // 2, "last": nsteps - 1}
        hooks = rider.hooks if rider else []
        for when, fn in hooks:
            if when != "last":
                pl.when(step == at[when])(functools.partial(fn, rins, routs, rsems))
        body(*ins, *outs, *scr)
        for when, fn in hooks:
            if when == "last":
                pl.when(step == at[when])(functools.partial(fn, rins, routs, rsems))

    outs = pl.pallas_call(
        full_body, name=name, grid=grid,
        out_shape=list(out_shape) + list(r_out),
        in_specs=list(in_specs) + (rider.in_specs if rider else []),
        out_specs=list(out_specs) + [ANY] * len(r_out),
        scratch_shapes=list(scratch) + list(r_sem),
        compiler_params=pltpu.CompilerParams(dimension_semantics=("arbitrary",) * len(grid),
                                             vmem_limit_bytes=VMEM_LIMIT),
    )(*args, *r_in)
    return list(outs[:n_out]), list(outs[n_out:])


def _gather_rider(shards):
    n = len(shards)

    def setup(outs, sems):
        x, y, c = _my_place()
        send_sems, recv_sems, _ = sems
        chips = [(1 - x, y), (x, 1 - y), (1 - x, 1 - y)]

        def block(w, place):
            return outs[w].at[4 * place[0] + 2 * place[1] + place[2]]

        def copy(w, k, place, to, src=None):
            return pltpu.make_async_remote_copy(
                src_ref=block(w, place) if src is None else src, dst_ref=block(w, place),
                send_sem=send_sems.at[w, k], recv_sem=recv_sems.at[w, k], device_id=to, device_id_type=MESH)

        return (x, y, c), (x, y, 1 - c), chips, block, copy

    def first(ins, outs, sems):
        me, sibling, chips, block, copy = setup(outs, sems)
        for w in range(n):
            pltpu.make_async_copy(ins[w], block(w, me), sems[2].at[w]).start()
            copy(w, 0, me, sibling, src=ins[w]).start()
            for j, chip in enumerate(chips):
                copy(w, 1 + j, me, (*chip, me[2]), src=ins[w]).start()

    def mid(ins, outs, sems):
        me, sibling, chips, block, copy = setup(outs, sems)
        for w in range(n):
            for j, chip in enumerate(chips):
                copy(w, 1 + j, (*chip, me[2]), me).wait_recv()
                copy(w, 4 + j, (*chip, me[2]), sibling).start()

    def last(ins, outs, sems):
        me, sibling, chips, block, copy = setup(outs, sems)
        for w in range(n):
            copy(w, 0, sibling, me).wait_recv()
            for j, chip in enumerate(chips):
                copy(w, 4 + j, (*chip, 1 - me[2]), me).wait_recv()
            copy(w, 0, me, sibling, src=ins[w]).wait_send()
            for j, chip in enumerate(chips):
                copy(w, 1 + j, me, (*chip, me[2]), src=ins[w]).wait_send()
                copy(w, 4 + j, (*chip, me[2]), sibling).wait_send()
            pltpu.make_async_copy(ins[w], block(w, me), sems[2].at[w]).wait()

    return _Rider(
        shards, [jax.ShapeDtypeStruct((N_DEV,) + s.shape, BF16) for s in shards],
        [pltpu.SemaphoreType.DMA((n, N_DEV - 1)), pltpu.SemaphoreType.DMA((n, N_DEV - 1)),
         pltpu.SemaphoreType.DMA((n,))],
        first=first, mid=mid, last=last, ins_in_vmem=True)


def _sibling_rider(gblocks):
    n = len(gblocks)

    def copies(ins, outs, sems):
        x, y, c = _my_place()
        send_sems, recv_sems = sems
        made = []
        for w in range(n):
            for f, (fx, fy) in enumerate(CHIP_FLIPS):
                chip = 4 * _flip(x, fx) + 2 * _flip(y, fy)
                made.append(pltpu.make_async_remote_copy(
                    src_ref=ins[w].at[chip + 1 - c], dst_ref=outs[w].at[f], send_sem=send_sems.at[w, f],
                    recv_sem=recv_sems.at[w, f], device_id=(x, y, 1 - c), device_id_type=MESH))
        return made

    def first(ins, outs, sems):
        for cp in copies(ins, outs, sems):
            cp.start()

    def last(ins, outs, sems):
        for cp in copies(ins, outs, sems):
            cp.wait_recv()
            cp.wait_send()

    return _Rider(gblocks, [jax.ShapeDtypeStruct((4,) + g.shape[1:], BF16) for g in gblocks],
                  [pltpu.SemaphoreType.DMA((n, 4))] * 2, first=first, last=last)


def _own_blocks(gblocks):
    x, y, c = _my_place()
    return jnp.stack([lax.dynamic_index_in_dim(gblocks, 4 * _flip(x, fx) + 2 * _flip(y, fy) + c, 0, keepdims=False)
                      for fx, fy in CHIP_FLIPS])


def _chip_rider(sums):
    n = len(sums)

    def copies(ins, outs, sems):
        x, y, c = _my_place()
        send_sems, recv_sems = sems
        made = []
        for w in range(n):
            for f in (1, 2, 3):
                fx, fy = CHIP_FLIPS[f]
                made.append(pltpu.make_async_remote_copy(
                    src_ref=ins[w].at[f - 1], dst_ref=outs[w].at[f - 1], send_sem=send_sems.at[w, f - 1],
                    recv_sem=recv_sems.at[w, f - 1], device_id=(_flip(x, fx), _flip(y, fy), c), device_id_type=MESH))
        return made

    def first(ins, outs, sems):
        for cp in copies(ins, outs, sems):
            cp.start()

    def last(ins, outs, sems):
        for cp in copies(ins, outs, sems):
            cp.wait_recv()
            cp.wait_send()

    return _Rider(sums, [jax.ShapeDtypeStruct(s.shape, BF16) for s in sums],
                  [pltpu.SemaphoreType.DMA((n, 3))] * 2, first=first, last=last)


def _small_allgather(v, name):
    rows = v.shape[0]

    def body(v_ref, out_ref, send_sems, recv_sems, local_sem):
        x, y, c = _my_place()
        me = 4 * x + 2 * y + c
        mine = pltpu.make_async_copy(v_ref, out_ref.at[me], local_sem)
        mine.start()
        sends = []
        for k in range(1, N_DEV):
            px, py, pc = _flip(x, k & 4), _flip(y, k & 2), _flip(c, k & 1)
            cp = pltpu.make_async_remote_copy(
                src_ref=v_ref, dst_ref=out_ref.at[me], send_sem=send_sems.at[k - 1], recv_sem=recv_sems.at[k - 1],
                device_id=(px, py, pc), device_id_type=MESH)
            cp.start()
            sends.append(cp)
        for k in range(1, N_DEV):
            px, py, pc = _flip(x, k & 4), _flip(y, k & 2), _flip(c, k & 1)
            pltpu.make_async_remote_copy(
                src_ref=v_ref, dst_ref=out_ref.at[4 * px + 2 * py + pc], send_sem=send_sems.at[k - 1],
                recv_sem=recv_sems.at[k - 1], device_id=(px, py, pc), device_id_type=MESH).wait_recv()
        for cp in sends:
            cp.wait_send()
        mine.wait()

    return pl.pallas_call(
        body, name=name,
        out_shape=jax.ShapeDtypeStruct((N_DEV, rows, LANES), F32),
        in_specs=[pl.BlockSpec(memory_space=pltpu.VMEM)],
        out_specs=pl.BlockSpec(memory_space=pltpu.VMEM),
        scratch_shapes=[pltpu.SemaphoreType.DMA((N_DEV - 1,)), pltpu.SemaphoreType.DMA((N_DEV - 1,)),
                        pltpu.SemaphoreType.DMA],
        compiler_params=pltpu.CompilerParams(vmem_limit_bytes=VMEM_LIMIT),
    )(v)


def _gather_first_weight(shard, others):
    n = len(others)

    def body(*refs):
        w_ref, other_refs = refs[0], refs[1:1 + n]
        out_ref, cast_refs = refs[1 + n], refs[2 + n:2 + 2 * n]
        mine_ref, send_sems, recv_sems, local_sem = refs[2 + 2 * n:]
        x, y, c = _my_place()
        me, sibling = (x, y, c), (x, y, 1 - c)
        chips = [(1 - x, y), (x, 1 - y), (1 - x, 1 - y)]

        def block(place):
            return out_ref.at[4 * place[0] + 2 * place[1] + place[2]]

        def copy(k, place, to, src=None):
            return pltpu.make_async_remote_copy(
                src_ref=block(place) if src is None else src, dst_ref=block(place),
                send_sem=send_sems.at[k], recv_sem=recv_sems.at[k], device_id=to, device_id_type=MESH)

        mine_ref[...] = w_ref[...].astype(BF16)
        local = pltpu.make_async_copy(mine_ref, block(me), local_sem)
        local.start()
        started = [copy(0, me, sibling, src=mine_ref)]
        started += [copy(1 + j, me, (*chip, c), src=mine_ref) for j, chip in enumerate(chips)]
        for cp in started:
            cp.start()
        for o_ref, c_ref in zip(other_refs, cast_refs):
            c_ref[...] = o_ref[...].astype(BF16)
        for j, chip in enumerate(chips):
            copy(1 + j, (*chip, c), me).wait_recv()
            passed = copy(4 + j, (*chip, c), sibling)
            passed.start()
            started.append(passed)
        copy(0, sibling, me).wait_recv()
        for j, chip in enumerate(chips):
            copy(4 + j, (*chip, 1 - c), me).wait_recv()
        for cp in started:
            cp.wait_send()
        local.wait()

    vmem = pl.BlockSpec(memory_space=pltpu.VMEM)
    outs = pl.pallas_call(
        body, name="gather_w_in",
        out_shape=[jax.ShapeDtypeStruct((N_DEV,) + shard.shape, BF16)]
        + [jax.ShapeDtypeStruct(o.shape, BF16) for o in others],
        in_specs=[vmem] * (1 + n),
        out_specs=[ANY] + [vmem] * n,
        scratch_shapes=[pltpu.VMEM(shard.shape, BF16), pltpu.SemaphoreType.DMA((N_DEV - 1,)),
                        pltpu.SemaphoreType.DMA((N_DEV - 1,)), pltpu.SemaphoreType.DMA],
        compiler_params=pltpu.CompilerParams(vmem_limit_bytes=VMEM_LIMIT),
    )(shard, *others)
    return outs[0], list(outs[1:])


def _carry(rider, name):
    def body(token_ref):
        token_ref[...] = jnp.zeros_like(token_ref)

    _, routs = _call(body, name, (1,), [], [], [jax.ShapeDtypeStruct((SUBLANES, LANES), F32)],
                     [_full((SUBLANES, LANES))], rider=rider)
    return routs


def _ada_forward(c_all, w_ada, b_cols):
    cols = w_ada.shape[1]

    def body(c_ref, w_ref, b_ref, out_ref):
        cf = c_ref[...]
        act = (cf * _sigmoid(cf)).astype(BF16)
        out_ref[...] = jnp.dot(act, w_ref[...].astype(BF16), preferred_element_type=F32) + b_ref[...]

    return pl.pallas_call(
        body, name="ada_forward",
        out_shape=jax.ShapeDtypeStruct((N_DEV, cols), F32),
        in_specs=[pl.BlockSpec(memory_space=pltpu.VMEM)] * 3,
        out_specs=pl.BlockSpec(memory_space=pltpu.VMEM),
        compiler_params=pltpu.CompilerParams(vmem_limit_bytes=VMEM_LIMIT),
    )(c_all, w_ada, b_cols)


def _ada_weight_grad(c_all, dmod_cols):
    cols = dmod_cols.shape[1]

    def body(c_ref, d_ref, out_ref):
        cf = c_ref[...]
        act = (cf * _sigmoid(cf)).astype(BF16)
        out_ref[...] = lax.dot_general(act, d_ref[...].astype(BF16), TN_DIMS, preferred_element_type=F32)

    return pl.pallas_call(
        body, name="ada_weight_grad",
        out_shape=jax.ShapeDtypeStruct((D_MODEL, cols), F32),
        in_specs=[pl.BlockSpec(memory_space=pltpu.VMEM)] * 2,
        out_specs=pl.BlockSpec(memory_space=pltpu.VMEM),
        compiler_params=pltpu.CompilerParams(vmem_limit_bytes=VMEM_LIMIT),
    )(c_all, dmod_cols)


def _sum_devices(packed):
    def body(p_ref, out_ref):
        total = p_ref[0]
        for d in range(1, N_DEV):
            total = total + p_ref[d]
        out_ref[...] = total

    return pl.pallas_call(
        body, name="sum_devices",
        out_shape=jax.ShapeDtypeStruct(packed.shape[1:], F32),
        in_specs=[pl.BlockSpec(memory_space=pltpu.VMEM)],
        out_specs=pl.BlockSpec(memory_space=pltpu.VMEM),
        compiler_params=pltpu.CompilerParams(vmem_limit_bytes=VMEM_LIMIT),
    )(packed)


def _row_tile(rows, multiple):
    for cand in range(min(rows, 256), 0, -1):
        if rows % cand == 0 and cand % multiple == 0:
            return cand
    return rows


def _adamw_update(w, g, m, v):
    c1 = 1.0 / (1.0 - ADAM_B1 ** ADAM_STEP)
    c2 = 1.0 / (1.0 - ADAM_B2 ** ADAM_STEP)
    nm = ADAM_B1 * m + (1.0 - ADAM_B1) * g
    nv = ADAM_B2 * v + (1.0 - ADAM_B2) * (g * g)
    delta = -ADAM_LR * ((nm * c1) / (jnp.sqrt(nv * c2) + ADAM_EPS) + ADAM_WD * w)
    return delta, nm, nv


def _adamw(w, g, m, v, name):
    rows, cols = w.shape
    tile = _row_tile(rows, SUBLANES)

    def body(w_ref, g_ref, m_ref, v_ref, d_ref, nm_ref, nv_ref):
        d_ref[...], nm_ref[...], nv_ref[...] = _adamw_update(w_ref[...], g_ref[...], m_ref[...], v_ref[...])

    spec = pl.BlockSpec((tile, cols), lambda i: (i, 0))
    outs, _ = _call(body, name, (rows // tile,), [w, g, m, v], [spec] * 4,
                    [jax.ShapeDtypeStruct((rows, cols), F32)] * 3, [spec] * 3)
    return outs


def _sibling_sum(own, sib, name):
    _, r, cdim = own.shape
    tile = _row_tile(r, BF16_ROWS)

    def body(own_ref, sib_ref, sums_ref, mine_ref):
        mine_ref[...] = own_ref[0].astype(F32) + sib_ref[0].astype(F32)
        for f in (1, 2, 3):
            sums_ref[f - 1] = (own_ref[f].astype(F32) + sib_ref[f].astype(F32)).astype(BF16)

    outs, _ = _call(
        body, name, (r // tile,), [own, sib], [pl.BlockSpec((4, tile, cdim), lambda i: (0, i, 0))] * 2,
        [jax.ShapeDtypeStruct((3, r, cdim), BF16), jax.ShapeDtypeStruct((r, cdim), F32)],
        [pl.BlockSpec((3, tile, cdim), lambda i: (0, i, 0)), pl.BlockSpec((tile, cdim), lambda i: (i, 0))])
    return outs


def _chip_sum_adamw(mine, ici, w, m, v, name):
    r, cdim = mine.shape
    tile = _row_tile(r, BF16_ROWS)

    def body(mine_ref, ici_ref, w_ref, m_ref, v_ref, g_ref, d_ref, nm_ref, nv_ref):
        g = mine_ref[...]
        for f in range(3):
            g = g + ici_ref[f].astype(F32)
        g_ref[...] = g
        d_ref[...], nm_ref[...], nv_ref[...] = _adamw_update(w_ref[...], g, m_ref[...], v_ref[...])

    spec = pl.BlockSpec((tile, cdim), lambda i: (i, 0))
    outs, _ = _call(
        body, name, (r // tile,), [mine, ici, w, m, v],
        [spec, pl.BlockSpec((3, tile, cdim), lambda i: (0, i, 0)), spec, spec, spec],
        [jax.ShapeDtypeStruct((r, cdim), F32)] * 4, [spec] * 4)
    return outs


def _inproj_fwd(x, vec, w_in, b_in, rider):
    t = x.shape[0]
    tm = min(TOKEN_TILE, t)
    chunk = 1280

    def body(x_ref, vec_ref, w_ref, b_ref, z_ref, h_ref):
        xf = x_ref[...]
        r = lax.rsqrt(jnp.mean(xf * xf, axis=-1, keepdims=True) + EPS)
        h = (xf * r) * vec_ref[0:1, :] * (1.0 + vec_ref[1:2, :]) + vec_ref[2:3, :]
        hb = h.astype(BF16)
        h_ref[...] = hb
        for n in range(IN_WIDTH // chunk):
            sl = slice(n * chunk, (n + 1) * chunk)
            z_ref[:, sl] = (jnp.dot(hb, w_ref[:, sl], preferred_element_type=F32) + b_ref[:, sl]).astype(BF16)

    return _call(
        body, "inproj_fwd", (t // tm,), [x, vec, w_in, b_in],
        [pl.BlockSpec((tm, D_MODEL), lambda i: (i, 0)), _full((SUBLANES, D_MODEL)),
         _full((D_MODEL, IN_WIDTH)), _full((1, IN_WIDTH))],
        [jax.ShapeDtypeStruct((t, IN_WIDTH), BF16), jax.ShapeDtypeStruct((t, D_MODEL), BF16)],
        [pl.BlockSpec((tm, IN_WIDTH), lambda i: (i, 0)), pl.BlockSpec((tm, D_MODEL), lambda i: (i, 0))],
        rider=rider)


def _window_mask(has_prev):
    qi = lax.broadcasted_iota(jnp.int32, (WINDOW, 2 * WINDOW), 0)
    kj = lax.broadcasted_iota(jnp.int32, (WINDOW, 2 * WINDOW), 1)
    off = jnp.where(has_prev, 0, 4 * WINDOW)
    in_prev = jnp.logical_and(kj < WINDOW, kj > qi + off)
    in_cur = jnp.logical_and(kj >= WINDOW, (kj - WINDOW) <= qi)
    return jnp.logical_or(in_prev, in_cur)


def _attn_fwd(z, sinks):
    t = z.shape[0]
    tq = min(TOKEN_TILE, t)
    nblk = tq // WINDOW

    def body(q_ref, kv_ref, sink_ref, o_ref, lse_ref):
        i = pl.program_id(0)
        lane = lax.broadcasted_iota(jnp.int32, (WINDOW, N_Q_HEADS), 1)

        def one_block(b, carry):
            row0 = pl.multiple_of(b * WINDOW, WINDOW)
            start = i * tq + b * WINDOW
            prev = pl.multiple_of(jnp.maximum(start - WINDOW, 0), WINDOW)
            cur = pl.multiple_of(start, WINDOW)
            kvw = jnp.concatenate([kv_ref[pl.ds(prev, WINDOW), :], kv_ref[pl.ds(cur, WINDOW), :]], axis=0)
            valid = _window_mask(start > 0)
            lse_blk = jnp.zeros((WINDOW, N_Q_HEADS), F32)
            for h in range(N_Q_HEADS):
                j = h // GROUP
                k = kvw[:, j * HEAD_DIM:(j + 1) * HEAD_DIM]
                v = kvw[:, KV_WIDTH + j * HEAD_DIM:KV_WIDTH + (j + 1) * HEAD_DIM]
                qh = q_ref[pl.ds(row0, WINDOW), h * HEAD_DIM:(h + 1) * HEAD_DIM]
                s = lax.dot_general(qh, k, NT_DIMS, preferred_element_type=F32) * ATTN_SCALE
                s = jnp.where(valid, s, -jnp.inf)
                sink = sink_ref[h]
                m = jnp.maximum(jnp.max(s, axis=-1, keepdims=True), sink)
                p = jnp.exp(s - m)
                denom = jnp.sum(p, axis=-1, keepdims=True) + jnp.exp(sink - m)
                o = jnp.dot(p.astype(BF16), v, preferred_element_type=F32) / denom
                o_ref[pl.ds(row0, WINDOW), h * HEAD_DIM:(h + 1) * HEAD_DIM] = o.astype(BF16)
                lse_blk = jnp.where(lane == h, m + jnp.log(denom), lse_blk)
            lse_ref[pl.ds(row0, WINDOW), :] = lse_blk
            return carry

        lax.fori_loop(0, nblk, one_block, 0)

    outs, _ = _call(
        body, "attn_fwd", (t // tq,), [z, z, sinks],
        [pl.BlockSpec((tq, D_MODEL), lambda i: (i, 0)),
         pl.BlockSpec((t, 2 * KV_WIDTH), lambda i: (0, KV_COL // (2 * KV_WIDTH))),
         pl.BlockSpec(memory_space=pltpu.SMEM)],
        [jax.ShapeDtypeStruct((t, D_MODEL), BF16), jax.ShapeDtypeStruct((t, N_Q_HEADS), F32)],
        [pl.BlockSpec((tq, D_MODEL), lambda i: (i, 0)), pl.BlockSpec((tq, N_Q_HEADS), lambda i: (i, 0))])
    return outs


HALO = BF16_ROWS


def _shift_down(u, uh, k):
    row = lax.broadcasted_iota(jnp.int32, u.shape, 0)
    out = pltpu.roll(u, k, 0)
    for j in range(k):
        out = jnp.where(row == j, uh[HALO - k + j:HALO - k + j + 1, :], out)
    return out


def _shift_up(u, nxt, k):
    n = u.shape[0]
    row = lax.broadcasted_iota(jnp.int32, u.shape, 0)
    out = pltpu.roll(u, n - k, 0)
    for j in range(k):
        out = jnp.where(row == n - k + j, nxt[j:j + 1, :], out)
    return out


def _conv_inputs(cc_ref, cx_ref, hc_ref, hx_ref, first_tile):
    cc = cc_ref[...].astype(F32)
    cx = cx_ref[...].astype(F32)
    u = cc * cx
    uh = jnp.where(first_tile, 0.0, hc_ref[...].astype(F32) * hx_ref[...].astype(F32))
    return cc, cx, u, _shift_down(u, uh, 1), _shift_down(u, uh, 2)


def _z_specs(tm, order):
    per_tile = tm // HALO
    cols = [pl.BlockSpec((tm, D_MODEL), functools.partial(lambda i, j: (order(i), j), j=j)) for j in range(1, 6)]
    halos = [pl.BlockSpec((HALO, D_MODEL),
                          functools.partial(lambda i, j: (jnp.maximum(order(i) * per_tile - 1, 0), j), j=j))
             for j in (2, 3)]
    return cols + halos


def _mix_fwd(x, attn, z, vec, w_out):
    t = x.shape[0]
    tm = min(TOKEN_TILE, t)

    def body(x_ref, a_ref, cb_ref, cc_ref, cx_ref, ga_ref, gc_ref, hc_ref, hx_ref, vec_ref, w_ref,
             m_ref, x2_ref, h2_ref):
        i = pl.program_id(0)
        _, _, u, u1, u2 = _conv_inputs(cc_ref, cx_ref, hc_ref, hx_ref, i == 0)
        cv = vec_ref[4:5, :] * u2 + vec_ref[5:6, :] * u1 + vec_ref[6:7, :] * u
        conv = cb_ref[...].astype(F32) * cv
        merged = (_sigmoid(ga_ref[...].astype(F32)) * a_ref[...].astype(F32)
                  + _sigmoid(gc_ref[...].astype(F32)) * conv)
        mb = merged.astype(BF16)
        m_ref[...] = mb
        o = jnp.dot(mb, w_ref[...], preferred_element_type=F32)
        x2 = x_ref[...] + vec_ref[0:1, :] * o
        x2_ref[...] = x2
        r = lax.rsqrt(jnp.mean(x2 * x2, axis=-1, keepdims=True) + EPS)
        h2 = (x2 * r) * vec_ref[1:2, :] * (1.0 + vec_ref[2:3, :]) + vec_ref[3:4, :]
        h2_ref[...] = h2.astype(BF16)

    tok = pl.BlockSpec((tm, D_MODEL), lambda i: (i, 0))
    outs, _ = _call(
        body, "mix_fwd", (t // tm,), [x, attn, z, z, z, z, z, z, z, vec, w_out],
        [tok, tok] + _z_specs(tm, lambda i: i) + [_full((SUBLANES, D_MODEL)), _full((D_MODEL, D_MODEL))],
        [jax.ShapeDtypeStruct((t, D_MODEL), BF16), jax.ShapeDtypeStruct((t, D_MODEL), F32),
         jax.ShapeDtypeStruct((t, D_MODEL), BF16)],
        [tok, tok, tok])
    return outs


def _ffn_fwd(h2, w_ffn_in):
    t = h2.shape[0]
    tm = min(TOKEN_TILE, t)

    def body(h_ref, w_ref, gu_ref, a_ref):
        hb = h_ref[...]
        g = jnp.dot(hb, w_ref[:, :D_FF], preferred_element_type=F32)
        u = jnp.dot(hb, w_ref[:, D_FF:], preferred_element_type=F32)
        gu_ref[:, :D_FF] = g.astype(BF16)
        gu_ref[:, D_FF:] = u.astype(BF16)
        a_ref[...] = (g * _sigmoid(g) * u).astype(BF16)

    outs, _ = _call(
        body, "ffn_fwd", (t // tm,), [h2, w_ffn_in],
        [pl.BlockSpec((tm, D_MODEL), lambda i: (i, 0)), _full((D_MODEL, 2 * D_FF))],
        [jax.ShapeDtypeStruct((t, 2 * D_FF), BF16), jax.ShapeDtypeStruct((t, D_FF), BF16)],
        [pl.BlockSpec((tm, 2 * D_FF), lambda i: (i, 0)), pl.BlockSpec((tm, D_FF), lambda i: (i, 0))])
    return outs


def _ffn_out_loss(a, gu, x2, target, vec, w_ffn_out):
    t = a.shape[0]
    tm = min(TOKEN_TILE, t)

    def body(a_ref, gu_ref, x2_ref, t_ref, vec_ref, w_ref, dx3_ref, df_ref, dgu_ref, acc_ref):
        @pl.when(pl.program_id(0) == 0)
        def _():
            acc_ref[...] = jnp.zeros_like(acc_ref)

        ga2 = vec_ref[0:1, :]
        gf = vec_ref[1:2, :]
        f = jnp.dot(a_ref[...], w_ref[...], preferred_element_type=F32)
        x3 = x2_ref[...] + ga2 * f
        r = lax.rsqrt(jnp.mean(x3 * x3, axis=-1, keepdims=True) + EPS)
        xn = x3 * r
        err = xn * gf - t_ref[...]
        dy = err * (1.0 / D_MODEL)
        dxn = dy * gf
        dx3 = r * (dxn - xn * jnp.mean(dxn * xn, axis=-1, keepdims=True))
        dx3_ref[...] = dx3
        acc_ref[0:1, :] += jnp.sum(err * err, axis=0, keepdims=True)
        acc_ref[1:2, :] += jnp.sum(dy * xn, axis=0, keepdims=True)
        acc_ref[2:3, :] += jnp.sum(dx3 * f, axis=0, keepdims=True)
        df = (dx3 * ga2).astype(BF16)
        df_ref[...] = df
        da = lax.dot_general(df, w_ref[...], NT_DIMS, preferred_element_type=F32)
        g = gu_ref[:, :D_FF].astype(F32)
        u = gu_ref[:, D_FF:].astype(F32)
        sg = _sigmoid(g)
        dgu_ref[:, :D_FF] = (da * u * (sg * (1.0 + g * (1.0 - sg)))).astype(BF16)
        dgu_ref[:, D_FF:] = (da * (g * sg)).astype(BF16)

    tok = pl.BlockSpec((tm, D_MODEL), lambda i: (i, 0))
    outs, _ = _call(
        body, "ffn_out_loss", (t // tm,), [a, gu, x2, target, vec, w_ffn_out],
        [pl.BlockSpec((tm, D_FF), lambda i: (i, 0)), pl.BlockSpec((tm, 2 * D_FF), lambda i: (i, 0)),
         tok, tok, _full((SUBLANES, D_MODEL)), _full((D_FF, D_MODEL))],
        [jax.ShapeDtypeStruct((t, D_MODEL), F32), jax.ShapeDtypeStruct((t, D_MODEL), BF16),
         jax.ShapeDtypeStruct((t, 2 * D_FF), BF16), jax.ShapeDtypeStruct((SUBLANES, D_MODEL), F32)],
        [tok, tok, pl.BlockSpec((tm, 2 * D_FF), lambda i: (i, 0)), _full((SUBLANES, D_MODEL))])
    return outs


def _ffn_in_bwd(dgu, x2, dx3, vec, w_ffn_in, rider):
    t = x2.shape[0]
    tm = min(TOKEN_TILE, t)

    def body(dgu_ref, x2_ref, dx3_ref, vec_ref, wf_ref, dx2_ref, acc_ref):
        @pl.when(pl.program_id(0) == 0)
        def _():
            acc_ref[...] = jnp.zeros_like(acc_ref)

        gffn = vec_ref[0:1, :]
        sc2 = vec_ref[1:2, :]
        dh2 = lax.dot_general(dgu_ref[...], wf_ref[...], NT_DIMS, preferred_element_type=F32)
        x2 = x2_ref[...]
        r = lax.rsqrt(jnp.mean(x2 * x2, axis=-1, keepdims=True) + EPS)
        xn = x2 * r
        acc_ref[0:1, :] += jnp.sum(dh2, axis=0, keepdims=True)
        acc_ref[1:2, :] += jnp.sum(dh2 * xn * gffn, axis=0, keepdims=True)
        acc_ref[2:3, :] += jnp.sum(dh2 * xn * (1.0 + sc2), axis=0, keepdims=True)
        dxn = dh2 * gffn * (1.0 + sc2)
        dx2_ref[...] = dx3_ref[...] + r * (dxn - xn * jnp.mean(dxn * xn, axis=-1, keepdims=True))

    tok = pl.BlockSpec((tm, D_MODEL), lambda i: (i, 0))
    return _call(
        body, "ffn_in_bwd", (t // tm,), [dgu, x2, dx3, vec, w_ffn_in],
        [pl.BlockSpec((tm, 2 * D_FF), lambda i: (i, 0)), tok, tok, _full((SUBLANES, D_MODEL)),
         _full((D_MODEL, 2 * D_FF))],
        [jax.ShapeDtypeStruct((t, D_MODEL), F32), jax.ShapeDtypeStruct((SUBLANES, D_MODEL), F32)],
        [tok, _full((SUBLANES, D_MODEL))], rider=rider)


def _mix_bwd(dx2, merged, attn, z, vec, w_out, rider):
    t = dx2.shape[0]
    tm = min(TOKEN_TILE, t)
    nt = t // tm
    rev = lambda i: nt - 1 - i

    def body(dx2_ref, m_ref, a_ref, cb_ref, cc_ref, cx_ref, ga_ref, gc_ref, hc_ref, hx_ref,
             vec_ref, wo_ref, do_ref, da_ref, dr_ref, acc_ref, carry_ref):
        i = pl.program_id(0)

        @pl.when(i == 0)
        def _():
            acc_ref[...] = jnp.zeros_like(acc_ref)
            carry_ref[...] = jnp.zeros_like(carry_ref)

        ga1 = vec_ref[0:1, :]
        w0, w1, w2 = vec_ref[1:2, :], vec_ref[2:3, :], vec_ref[3:4, :]
        dx2 = dx2_ref[...]
        o = jnp.dot(m_ref[...], wo_ref[...], preferred_element_type=F32)
        acc_ref[0:1, :] += jnp.sum(dx2 * o, axis=0, keepdims=True)
        do = (dx2 * ga1).astype(BF16)
        do_ref[...] = do
        dm = lax.dot_general(do, wo_ref[...], NT_DIMS, preferred_element_type=F32)

        cc, cx, u, u1, u2 = _conv_inputs(cc_ref, cx_ref, hc_ref, hx_ref, i == nt - 1)
        cv = w0 * u2 + w1 * u1 + w2 * u
        cb = cb_ref[...].astype(F32)
        sa = _sigmoid(ga_ref[...].astype(F32))
        sc = _sigmoid(gc_ref[...].astype(F32))
        attn = a_ref[...].astype(F32)
        da_ref[...] = (dm * sa).astype(BF16)
        dconv = dm * sc
        dr_ref[:, 3 * D_MODEL:4 * D_MODEL] = (dm * attn * sa * (1.0 - sa)).astype(BF16)
        dr_ref[:, 4 * D_MODEL:5 * D_MODEL] = (dconv * (cb * cv) * (1.0 - sc)).astype(BF16)
        dr_ref[:, 0:D_MODEL] = (dconv * cv).astype(BF16)
        dcv = dconv * cb
        acc_ref[1:2, :] += jnp.sum(dcv * u2, axis=0, keepdims=True)
        acc_ref[2:3, :] += jnp.sum(dcv * u1, axis=0, keepdims=True)
        acc_ref[3:4, :] += jnp.sum(dcv * u, axis=0, keepdims=True)
        nxt = carry_ref[...]
        du = w2 * dcv + w1 * _shift_up(dcv, nxt, 1) + w0 * _shift_up(dcv, nxt, 2)
        carry_ref[...] = dcv[0:SUBLANES, :]
        dr_ref[:, D_MODEL:2 * D_MODEL] = (du * cx).astype(BF16)
        dr_ref[:, 2 * D_MODEL:3 * D_MODEL] = (du * cc).astype(BF16)

    tok = pl.BlockSpec((tm, D_MODEL), lambda i: (rev(i), 0))
    return _call(
        body, "mix_bwd", (nt,), [dx2, merged, attn, z, z, z, z, z, z, z, vec, w_out],
        [tok, tok, tok] + _z_specs(tm, rev) + [_full((SUBLANES, D_MODEL)), _full((D_MODEL, D_MODEL))],
        [jax.ShapeDtypeStruct((t, D_MODEL), BF16), jax.ShapeDtypeStruct((t, D_MODEL), BF16),
         jax.ShapeDtypeStruct((t, REST_WIDTH), BF16), jax.ShapeDtypeStruct((SUBLANES, D_MODEL), F32)],
        [tok, tok, pl.BlockSpec((tm, REST_WIDTH), lambda i: (rev(i), 0)), _full((SUBLANES, D_MODEL))],
        scratch=[pltpu.VMEM((SUBLANES, D_MODEL), F32)], rider=rider)


def _attn_bwd(z, dattn, attn, lse, sinks, rider):
    t = z.shape[0]
    tq = min(TOKEN_TILE, t)
    nblk = tq // WINDOW
    nt = t // tq

    def body(q_ref, kv_ref, do_ref, o_ref, lse_ref, sink_ref, dq_ref, dkv_ref, ds_ref, acc_ref):
        i = pl.program_id(0)

        @pl.when(i == 0)
        def _():
            acc_ref[...] = jnp.zeros_like(acc_ref)
            ds_ref[...] = jnp.zeros_like(ds_ref)

        lane = lax.broadcasted_iota(jnp.int32, (1, LANES), 1)

        def one_block(b, dsink):
            row0 = pl.multiple_of(b * WINDOW, WINDOW)
            start = i * tq + b * WINDOW
            prev = pl.multiple_of(jnp.maximum(start - WINDOW, 0), WINDOW)
            cur = pl.multiple_of(start, WINDOW)
            kvw = jnp.concatenate([kv_ref[pl.ds(prev, WINDOW), :], kv_ref[pl.ds(cur, WINDOW), :]], axis=0)
            valid = _window_mask(start > 0)
            lse_blk = lse_ref[pl.ds(row0, WINDOW), :]
            parts = []
            for j in range(N_KV_HEADS):
                k = kvw[:, j * HEAD_DIM:(j + 1) * HEAD_DIM]
                v = kvw[:, KV_WIDTH + j * HEAD_DIM:KV_WIDTH + (j + 1) * HEAD_DIM]
                dk = jnp.zeros((2 * WINDOW, HEAD_DIM), F32)
                dv = jnp.zeros((2 * WINDOW, HEAD_DIM), F32)
                for g in range(GROUP):
                    h = j * GROUP + g
                    cols = slice(h * HEAD_DIM, (h + 1) * HEAD_DIM)
                    qh = q_ref[pl.ds(row0, WINDOW), cols]
                    doh = do_ref[pl.ds(row0, WINDOW), cols]
                    oh = o_ref[pl.ds(row0, WINDOW), cols]
                    lse_h = lse_blk[:, h:h + 1]
                    s = lax.dot_general(qh, k, NT_DIMS, preferred_element_type=F32) * ATTN_SCALE
                    p = jnp.where(valid, jnp.exp(s - lse_h), 0.0)
                    delta = jnp.sum(doh.astype(F32) * oh.astype(F32), axis=-1, keepdims=True)
                    dp = lax.dot_general(doh, v, NT_DIMS, preferred_element_type=F32)
                    dsb = (p * (dp - delta)).astype(BF16)
                    dq = jnp.dot(dsb, k, preferred_element_type=F32) * ATTN_SCALE
                    dq_ref[pl.ds(row0, WINDOW), cols] = dq.astype(BF16)
                    dk = dk + lax.dot_general(dsb, qh, TN_DIMS, preferred_element_type=F32)
                    dv = dv + lax.dot_general(p.astype(BF16), doh, TN_DIMS, preferred_element_type=F32)
                    psink = jnp.exp(sink_ref[h] - lse_h)
                    dsink = dsink - jnp.where(lane == h, jnp.sum(psink * delta), 0.0)
                parts.append((dk * ATTN_SCALE, dv))
            blk = jnp.concatenate([parts[0][0], parts[1][0], parts[0][1], parts[1][1]], axis=1)
            acc_ref[pl.ds(prev, WINDOW), :] += blk[:WINDOW, :]
            acc_ref[pl.ds(cur, WINDOW), :] += blk[WINDOW:, :]
            return dsink

        dsink = lax.fori_loop(0, nblk, one_block, jnp.zeros((1, LANES), F32))
        ds_ref[0:1, :] += dsink

        @pl.when(i == nt - 1)
        def _():
            dkv_ref[...] = acc_ref[...].astype(BF16)

    tok = pl.BlockSpec((tq, D_MODEL), lambda i: (i, 0))
    return _call(
        body, "attn_bwd", (nt,), [z, z, dattn, attn, lse, sinks],
        [tok, pl.BlockSpec((t, 2 * KV_WIDTH), lambda i: (0, KV_COL // (2 * KV_WIDTH))), tok, tok,
         pl.BlockSpec((tq, N_Q_HEADS), lambda i: (i, 0)), pl.BlockSpec(memory_space=pltpu.SMEM)],
        [jax.ShapeDtypeStruct((t, D_MODEL), BF16), jax.ShapeDtypeStruct((t, 2 * KV_WIDTH), BF16),
         jax.ShapeDtypeStruct((SUBLANES, LANES), F32)],
        [tok, _full((t, 2 * KV_WIDTH)), _full((SUBLANES, LANES))],
        scratch=[pltpu.VMEM((t, 2 * KV_WIDTH), F32)], rider=rider)


def _inproj_bwd(dq, drest, dkv, x, dx2, vec, w_in, rider):
    t = x.shape[0]
    tm = min(TOKEN_TILE, t)

    def body(dq_ref, dr_ref, dkv_ref, x_ref, dx2_ref, vec_ref, w_ref, gx_ref, acc_ref, db_ref):
        @pl.when(pl.program_id(0) == 0)
        def _():
            acc_ref[...] = jnp.zeros_like(acc_ref)
            db_ref[...] = jnp.zeros_like(db_ref)

        g = vec_ref[0:1, :]
        sc1 = vec_ref[1:2, :]
        dqb, drb, dkvb = dq_ref[...], dr_ref[...], dkv_ref[...]
        dh = lax.dot_general(dqb, w_ref[:, :D_MODEL], NT_DIMS, preferred_element_type=F32)
        dh = dh + lax.dot_general(drb, w_ref[:, D_MODEL:KV_COL], NT_DIMS, preferred_element_type=F32)
        dh = dh + lax.dot_general(dkvb, w_ref[:, KV_COL:], NT_DIMS, preferred_element_type=F32)
        db_ref[:, :D_MODEL] += jnp.sum(dqb.astype(F32), axis=0, keepdims=True)
        db_ref[:, D_MODEL:KV_COL] += jnp.sum(drb.astype(F32), axis=0, keepdims=True)
        db_ref[:, KV_COL:] += jnp.sum(dkvb.astype(F32), axis=0, keepdims=True)
        xf = x_ref[...]
        r = lax.rsqrt(jnp.mean(xf * xf, axis=-1, keepdims=True) + EPS)
        xn = xf * r
        acc_ref[0:1, :] += jnp.sum(dh, axis=0, keepdims=True)
        acc_ref[1:2, :] += jnp.sum(dh * xn * g, axis=0, keepdims=True)
        acc_ref[2:3, :] += jnp.sum(dh * xn * (1.0 + sc1), axis=0, keepdims=True)
        dxn = dh * g * (1.0 + sc1)
        gx_ref[...] = dx2_ref[...] + r * (dxn - xn * jnp.mean(dxn * xn, axis=-1, keepdims=True))

    tok = pl.BlockSpec((tm, D_MODEL), lambda i: (i, 0))
    return _call(
        body, "inproj_bwd", (t // tm,), [dq, drest, dkv, x, dx2, vec, w_in],
        [tok, pl.BlockSpec((tm, REST_WIDTH), lambda i: (i, 0)),
         pl.BlockSpec((tm, 2 * KV_WIDTH), lambda i: (i, 0)), tok, tok,
         _full((SUBLANES, D_MODEL)), _full((D_MODEL, IN_WIDTH))],
        [jax.ShapeDtypeStruct((t, D_MODEL), F32), jax.ShapeDtypeStruct((SUBLANES, D_MODEL), F32),
         jax.ShapeDtypeStruct((1, IN_WIDTH), F32)],
        [tok, _full((SUBLANES, D_MODEL)), _full((1, IN_WIDTH))], rider=rider)


def _weight_grad(a, b, name, bn, rider=None):
    t, m = a.shape
    n = b.shape[1]
    tk = min(TOKEN_TILE, t)
    nk = t // tk

    def body(a_ref, b_ref, out_ref, acc_ref):
        k = pl.program_id(1)

        @pl.when(k == 0)
        def _():
            acc_ref[...] = jnp.zeros_like(acc_ref)

        acc_ref[...] += lax.dot_general(a_ref[...], b_ref[...], TN_DIMS, preferred_element_type=F32)

        @pl.when(k == nk - 1)
        def _():
            out_ref[...] = acc_ref[...].astype(BF16)

    outs, routs = _call(
        body, name, (n // bn, nk), [a, b],
        [pl.BlockSpec((tk, m), lambda j, k: (k, 0)), pl.BlockSpec((tk, bn), lambda j, k: (k, j))],
        [jax.ShapeDtypeStruct((m, n), BF16)], [pl.BlockSpec((m, bn), lambda j, k: (0, j))],
        scratch=[pltpu.VMEM((m, bn), F32)], rider=rider)
    return outs[0], routs


def _to_rows(v):
    n = v.shape[0]
    padded = -(-n // (SUBLANES * LANES)) * SUBLANES * LANES
    return jnp.pad(v, (0, padded - n)).reshape(padded // LANES, LANES)


def _vec_rows(*rows):
    stacked = jnp.concatenate([r.reshape(1, D_MODEL) for r in rows], axis=0)
    return jnp.pad(stacked, ((0, SUBLANES - len(rows)), (0, 0)))


def _col_blocks_to_matrix(g):
    return jnp.transpose(g, (1, 0, 2)).reshape(g.shape[1], N_DEV * g.shape[2])


def _matrix_to_col_blocks(w):
    k, n = w.shape
    return jnp.transpose(w.reshape(k, N_DEV, n // N_DEV), (1, 0, 2))


def _permute_in_cols(w):
    return jnp.concatenate([w[..., :D_MODEL], w[..., D_MODEL + 2 * KV_WIDTH:], w[..., D_MODEL:D_MODEL + 2 * KV_WIDTH]],
                           axis=-1)


def _unpermute_in_cols(w):
    return jnp.concatenate([w[..., :D_MODEL], w[..., KV_COL:], w[..., D_MODEL:KV_COL]], axis=-1)


def kernel(x, c, w_ada, b_ada, g_mix, w_in, b_in, sinks, conv_w, w_out, g_ffn, w_ffn_in, w_ffn_out, g_final, loss_target, m_w_ada, m_b_ada, m_g_mix, m_w_in, m_b_in, m_sinks, m_conv_w, m_w_out, m_g_ffn, m_w_ffn_in, m_w_ffn_out, m_g_final, v_w_ada, v_b_ada, v_g_mix, v_w_in, v_b_in, v_sinks, v_conv_w, v_w_out, v_g_ffn, v_w_ffn_in, v_w_ffn_out, v_g_final):
    ix, iy, ic = _my_place()
    me = 4 * ix + 2 * iy + ic
    xs = x[0]
    target = loss_target[0]
    ada_cols = w_ada.shape[2]
    conv_cols = conv_w.shape[2]

    first = _small_allgather(_to_rows(jnp.concatenate([c[0], conv_w[0].reshape(-1)])), "gather_c_conv")
    first = first.reshape(N_DEV, -1)
    c_all = first[:, :D_MODEL]
    conv_full = jnp.transpose(first[:, D_MODEL:D_MODEL + 3 * conv_cols].reshape(N_DEV, 3, conv_cols), (1, 0, 2))
    conv_full = conv_full.reshape(3, D_MODEL)
    b_cols = lax.dynamic_slice_in_dim(b_ada, me * ada_cols, ada_cols, axis=1)
    mod_part = _ada_forward(c_all, w_ada[0], b_cols)
    mod_all = _small_allgather(mod_part.reshape(-1, LANES), "gather_mod").reshape(N_DEV, N_DEV, ada_cols)
    mod = lax.dynamic_index_in_dim(mod_all, me, axis=1, keepdims=False).reshape(N_MOD, D_MODEL)
    sh1, sc1, ga1, sh2, sc2, ga2 = [mod[i:i + 1] for i in range(N_MOD)]

    g_in, later = _gather_first_weight(w_in[0], [w_ffn_in[0], w_out[0], w_ffn_out[0]])
    w_in_full = _permute_in_cols(_col_blocks_to_matrix(g_in))
    b_in_p = _permute_in_cols(b_in)
    (z, h1), (g_fi, g_out, g_fo) = _inproj_fwd(xs, _vec_rows(g_mix, sc1, sh1), w_in_full, b_in_p,
                                              _gather_rider(later))
    w_fi_full = _col_blocks_to_matrix(g_fi)
    w_out_full = g_out.reshape(D_MODEL, D_MODEL)
    w_fo_full = g_fo.reshape(D_FF, D_MODEL)
    attn, lse = _attn_fwd(z, sinks[0])
    merged, x2, h2 = _mix_fwd(xs, attn, z, _vec_rows(ga1, g_ffn, sc2, sh2, conv_full[0], conv_full[1], conv_full[2]),
                              w_out_full)
    gu, act = _ffn_fwd(h2, w_fi_full)
    dx3, df, dgu, acc_l = _ffn_out_loss(act, gu, x2, target, _vec_rows(ga2, g_final), w_fo_full)

    gw_fo, _ = _weight_grad(act, df, "wgrad_ffn_out", 1024)
    gw_fi, _ = _weight_grad(h2, dgu, "wgrad_ffn_in", 2816)
    blocks_fo = gw_fo.reshape(N_DEV, D_FF // N_DEV, D_MODEL)
    blocks_fi = _matrix_to_col_blocks(gw_fi)
    (dx2, acc_f), (sib_fo, sib_fi) = _ffn_in_bwd(dgu, x2, dx3, _vec_rows(g_ffn, sc2), w_fi_full,
                                                 _sibling_rider([blocks_fo, blocks_fi]))
    sums_fo, mine_fo = _sibling_sum(_own_blocks(blocks_fo), sib_fo, "sibling_sum_ffn_out")
    sums_fi, mine_fi = _sibling_sum(_own_blocks(blocks_fi), sib_fi, "sibling_sum_ffn_in")
    (dout, dattn, drest, acc_m), (ici_fo, ici_fi) = _mix_bwd(
        dx2, merged, attn, z, _vec_rows(ga1, conv_full[0], conv_full[1], conv_full[2]), w_out_full,
        _chip_rider([sums_fo, sums_fi]))
    gw_out, _ = _weight_grad(merged, dout, "wgrad_out", 1024)
    blocks_out = gw_out.reshape(N_DEV, D_MODEL // N_DEV, D_MODEL)
    (dq, dkv, dsink), (sib_out,) = _attn_bwd(z, dattn, attn, lse, sinks[0], _sibling_rider([blocks_out]))
    sums_out, mine_out = _sibling_sum(_own_blocks(blocks_out), sib_out, "sibling_sum_out")
    gw_rest, (ici_out,) = _weight_grad(h1, drest, "wgrad_in_rest", 1280, _chip_rider([sums_out]))
    gw_q, _ = _weight_grad(h1, dq, "wgrad_in_q", 1024)
    gw_kv, _ = _weight_grad(h1, dkv, "wgrad_in_kv", 256)
    gw_in = _unpermute_in_cols(jnp.concatenate([gw_q, gw_rest, gw_kv], axis=1))
    blocks_in = _matrix_to_col_blocks(gw_in)
    (sib_in,) = _carry(_sibling_rider([blocks_in]), "sibling_w_in")
    sums_in, mine_in = _sibling_sum(_own_blocks(blocks_in), sib_in, "sibling_sum_in")
    (grad_x, acc_i, db_in_p), (ici_in,) = _inproj_bwd(dq, drest, dkv, xs, dx2, _vec_rows(g_mix, sc1), w_in_full,
                                                      _chip_rider([sums_in]))

    pieces = [acc_i[0], acc_i[1], acc_m[0], acc_f[0], acc_f[1], acc_l[2],
              acc_i[2], _unpermute_in_cols(db_in_p)[0], acc_f[2], acc_l[1],
              acc_m[1], acc_m[2], acc_m[3], dsink[0], acc_l[0]]
    offsets = [0]
    for p in pieces:
        offsets.append(offsets[-1] + p.shape[0])
    packed = _small_allgather(_to_rows(jnp.concatenate(pieces)), "gather_small")
    dmod_all = packed.reshape(N_DEV, -1)[:, :N_MOD * D_MODEL]
    total = _sum_devices(packed).reshape(-1)
    part = lambda i: total[offsets[i]:offsets[i + 1]]
    g_b_ada = total[:N_MOD * D_MODEL].reshape(1, -1)
    g_g_mix, g_b_in, g_g_ffn, g_g_final = part(6).reshape(1, -1), part(7).reshape(1, -1), part(8).reshape(1, -1), part(9)
    g_conv_full = jnp.stack([part(10), part(11), part(12)])
    g_conv = lax.dynamic_slice_in_dim(g_conv_full, me * conv_cols, conv_cols, axis=1)[None]
    g_sinks = part(13)[:N_Q_HEADS].reshape(1, -1)
    loss = (0.5 / D_MODEL) * jnp.sum(part(14))
    dmod_cols = lax.dynamic_slice_in_dim(dmod_all, me * ada_cols, ada_cols, axis=1)
    g_w_ada = _ada_weight_grad(c_all, dmod_cols)

    def reduced(mine, ici, w, m, v, name):
        return tuple(o[None] for o in _chip_sum_adamw(mine, ici, w[0], m[0], v[0], name))

    d_ada, nm_ada, nv_ada = _adamw(w_ada[0], g_w_ada, m_w_ada[0], v_w_ada[0], "adamw_w_ada")
    small_names = ["b_ada", "g_mix", "b_in", "sinks", "conv_w", "g_ffn", "g_final"]
    small_w = [b_ada, g_mix, b_in, sinks, conv_w, g_ffn, g_final]
    small_m = [m_b_ada, m_g_mix, m_b_in, m_sinks, m_conv_w, m_g_ffn, m_g_final]
    small_v = [v_b_ada, v_g_mix, v_b_in, v_sinks, v_conv_w, v_g_ffn, v_g_final]
    small_g = [g_b_ada, g_g_mix, g_b_in, g_sinks, g_conv, g_g_ffn, g_g_final]
    small_g = [g.reshape(w.shape) for g, w in zip(small_g, small_w)]
    flat = lambda arrs: _to_rows(jnp.concatenate([a.reshape(-1) for a in arrs]))
    sd, snm, snv = _adamw(flat(small_w), flat(small_g), flat(small_m), flat(small_v), "adamw_small")
    sizes = [w.size for w in small_w]
    starts = [sum(sizes[:i]) for i in range(len(sizes))]
    unflat = lambda a: {n: a.reshape(-1)[s:s + z_].reshape(w.shape)
                        for n, s, z_, w in zip(small_names, starts, sizes, small_w)}
    sd, snm, snv = unflat(sd), unflat(snm), unflat(snv)
    sg = dict(zip(small_names, small_g))

    res = {
        "w_ada": (g_w_ada[None], d_ada[None], nm_ada[None], nv_ada[None]),
        "w_in": reduced(mine_in, ici_in, w_in, m_w_in, v_w_in, "adamw_w_in"),
        "w_out": reduced(mine_out, ici_out, w_out, m_w_out, v_w_out, "adamw_w_out"),
        "w_ffn_in": reduced(mine_fi, ici_fi, w_ffn_in, m_w_ffn_in, v_w_ffn_in, "adamw_w_ffn_in"),
        "w_ffn_out": reduced(mine_fo, ici_fo, w_ffn_out, m_w_ffn_out, v_w_ffn_out, "adamw_w_ffn_out"),
    }
    for n in small_names:
        res[n] = (sg[n], sd[n], snm[n], snv[n])
    order = ["w_ada", "b_ada", "g_mix", "w_in", "b_in", "sinks", "conv_w", "w_out", "g_ffn", "w_ffn_in", "w_ffn_out",
             "g_final"]
    outs = [loss, grad_x[None]]
    for k in range(4):
        outs += [res[n][k] for n in order]
    return tuple(outs)
```

```python
import functools
import math

import jax
import jax.numpy as jnp
from jax import lax
from jax.experimental import pallas as pl
from jax.experimental.pallas import tpu as pltpu

F32 = jnp.float32
BF16 = jnp.bfloat16

D_MODEL = 1024
HEAD_DIM = 64
N_Q_HEADS = 16
N_KV_HEADS = 2
GROUP = 8
WINDOW = 128
KV_WIDTH = N_KV_HEADS * HEAD_DIM
D_FF = 2816
IN_WIDTH = 6400
N_MOD = 6
EPS = 1e-6
N_DEV = 8
REST_WIDTH = 5 * D_MODEL
KV_COL = D_MODEL + REST_WIDTH
ATTN_SCALE = HEAD_DIM ** -0.5

ADAM_LR = 0.001
ADAM_B1 = 0.9
ADAM_B2 = 0.999
ADAM_EPS = 1e-08
ADAM_WD = 0.01
ADAM_STEP = 10

LANES = 128
SUBLANES = 8
BF16_ROWS = 16
VMEM_LIMIT = 56 * 1024 * 1024
TOKEN_TILE = 512
MESH = pl.DeviceIdType.MESH
ANY = pl.BlockSpec(memory_space=pl.ANY)

NT_DIMS = (((1,), (1,)), ((), ()))
TN_DIMS = (((0,), (0,)), ((), ()))
CHIP_FLIPS = [(0, 0), (1, 0), (0, 1), (1, 1)]


def _full(shape):
    return pl.BlockSpec(shape, lambda *_: (0,) * len(shape))


def _my_place():
    return lax.axis_index("x"), lax.axis_index("y"), lax.axis_index("c")


def _flip(v, bit):
    return 1 - v if bit else v


def _sigmoid(v):
    return 1.0 / (1.0 + jnp.exp(-v))


class _Rider:
    def __init__(self, ins, out_shapes, sem_shapes, first=None, mid=None, last=None, ins_in_vmem=False):
        self.ins, self.out_shapes, self.sem_shapes = list(ins), list(out_shapes), list(sem_shapes)
        self.in_specs = [_full(a.shape) if ins_in_vmem else ANY for a in self.ins]
        self.hooks = [(when, fn) for when, fn in (("first", first), ("mid", mid), ("last", last)) if fn is not None]


def _call(body, name, grid, args, in_specs, out_shape, out_specs, scratch=(), rider=None, aliases=None):
    n_in, n_out, n_scr = len(args), len(out_shape), len(scratch)
    r_in = rider.ins if rider else []
    r_out = rider.out_shapes if rider else []
    r_sem = rider.sem_shapes if rider else []
    nsteps = math.prod(grid)

    def full_body(*refs):
        pos = 0
        groups = []
        for size in (n_in, len(r_in), n_out, len(r_out), n_scr, len(r_sem)):
            groups.append(refs[pos:pos + size])
            pos += size
        ins, rins, outs, routs, scr, rsems = groups
        step = pl.program_id(0)
        for axis in range(1, len(grid)):
            step = step * grid[axis] + pl.program_id(axis)
        at = {"first": 0, "mid": (3 * nsteps) // 4, "last": nsteps - 1}
        hooks = rider.hooks if rider else []
        for when, fn in hooks:
            if when != "last":
                pl.when(step == at[when])(functools.partial(fn, rins, routs, rsems))
        body(*ins, *outs, *scr)
        for when, fn in hooks:
            if when == "last":
                pl.when(step == at[when])(functools.partial(fn, rins, routs, rsems))

    outs = pl.pallas_call(
        full_body, name=name, grid=grid,
        out_shape=list(out_shape) + list(r_out),
        in_specs=list(in_specs) + (rider.in_specs if rider else []),
        out_specs=list(out_specs) + [ANY] * len(r_out),
        scratch_shapes=list(scratch) + list(r_sem),
        input_output_aliases=dict(aliases or {}),
        compiler_params=pltpu.CompilerParams(dimension_semantics=("arbitrary",) * len(grid),
                                             vmem_limit_bytes=VMEM_LIMIT),
    )(*args, *r_in)
    return list(outs[:n_out]), list(outs[n_out:])


def _gather_rider(shards):
    n = len(shards)

    def setup(outs, sems):
        x, y, c = _my_place()
        send_sems, recv_sems, _ = sems
        chips = [(1 - x, y), (x, 1 - y), (1 - x, 1 - y)]

        def block(w, place):
            return outs[w].at[4 * place[0] + 2 * place[1] + place[2]]

        def copy(w, k, place, to, src=None):
            return pltpu.make_async_remote_copy(
                src_ref=block(w, place) if src is None else src, dst_ref=block(w, place),
                send_sem=send_sems.at[w, k], recv_sem=recv_sems.at[w, k], device_id=to, device_id_type=MESH)

        return (x, y, c), (x, y, 1 - c), chips, block, copy

    def first(ins, outs, sems):
        me, sibling, chips, block, copy = setup(outs, sems)
        for w in range(n):
            pltpu.make_async_copy(ins[w], block(w, me), sems[2].at[w]).start()
            copy(w, 0, me, sibling, src=ins[w]).start()
            for j, chip in enumerate(chips):
                copy(w, 1 + j, me, (*chip, me[2]), src=ins[w]).start()

    def mid(ins, outs, sems):
        me, sibling, chips, block, copy = setup(outs, sems)
        for w in range(n):
            for j, chip in enumerate(chips):
                copy(w, 1 + j, (*chip, me[2]), me).wait_recv()
                copy(w, 4 + j, (*chip, me[2]), sibling).start()

    def last(ins, outs, sems):
        me, sibling, chips, block, copy = setup(outs, sems)
        for w in range(n):
            copy(w, 0, sibling, me).wait_recv()
            for j, chip in enumerate(chips):
                copy(w, 4 + j, (*chip, 1 - me[2]), me).wait_recv()
            copy(w, 0, me, sibling, src=ins[w]).wait_send()
            for j, chip in enumerate(chips):
                copy(w, 1 + j, me, (*chip, me[2]), src=ins[w]).wait_send()
                copy(w, 4 + j, (*chip, me[2]), sibling).wait_send()
            pltpu.make_async_copy(ins[w], block(w, me), sems[2].at[w]).wait()

    return _Rider(
        shards, [jax.ShapeDtypeStruct((N_DEV,) + s.shape, BF16) for s in shards],
        [pltpu.SemaphoreType.DMA((n, N_DEV - 1)), pltpu.SemaphoreType.DMA((n, N_DEV - 1)),
         pltpu.SemaphoreType.DMA((n,))],
        first=first, mid=mid, last=last, ins_in_vmem=True)


def _sibling_rider(gblocks):
    n = len(gblocks)

    def copies(ins, outs, sems):
        x, y, c = _my_place()
        send_sems, recv_sems = sems
        made = []
        for w in range(n):
            for f, (fx, fy) in enumerate(CHIP_FLIPS):
                chip = 4 * _flip(x, fx) + 2 * _flip(y, fy)
                made.append(pltpu.make_async_remote_copy(
                    src_ref=ins[w].at[chip + 1 - c], dst_ref=outs[w].at[f], send_sem=send_sems.at[w, f],
                    recv_sem=recv_sems.at[w, f], device_id=(x, y, 1 - c), device_id_type=MESH))
        return made

    def first(ins, outs, sems):
        for cp in copies(ins, outs, sems):
            cp.start()

    def last(ins, outs, sems):
        for cp in copies(ins, outs, sems):
            cp.wait_recv()
            cp.wait_send()

    return _Rider(gblocks, [jax.ShapeDtypeStruct((4,) + g.shape[1:], BF16) for g in gblocks],
                  [pltpu.SemaphoreType.DMA((n, 4))] * 2, first=first, last=last)


def _own_blocks(gblocks):
    x, y, c = _my_place()
    return jnp.stack([lax.dynamic_index_in_dim(gblocks, 4 * _flip(x, fx) + 2 * _flip(y, fy) + c, 0, keepdims=False)
                      for fx, fy in CHIP_FLIPS])


def _chip_rider(sums):
    n = len(sums)

    def copies(ins, outs, sems):
        x, y, c = _my_place()
        send_sems, recv_sems = sems
        made = []
        for w in range(n):
            for f in (1, 2, 3):
                fx, fy = CHIP_FLIPS[f]
                made.append(pltpu.make_async_remote_copy(
                    src_ref=ins[w].at[f - 1], dst_ref=outs[w].at[f - 1], send_sem=send_sems.at[w, f - 1],
                    recv_sem=recv_sems.at[w, f - 1], device_id=(_flip(x, fx), _flip(y, fy), c), device_id_type=MESH))
        return made

    def first(ins, outs, sems):
        for cp in copies(ins, outs, sems):
            cp.start()

    def last(ins, outs, sems):
        for cp in copies(ins, outs, sems):
            cp.wait_recv()
            cp.wait_send()

    return _Rider(sums, [jax.ShapeDtypeStruct(s.shape, BF16) for s in sums],
                  [pltpu.SemaphoreType.DMA((n, 3))] * 2, first=first, last=last)


def _small_allgather(v, name):
    rows = v.shape[0]

    def body(v_ref, out_ref, send_sems, recv_sems, local_sem):
        x, y, c = _my_place()
        me = 4 * x + 2 * y + c
        mine = pltpu.make_async_copy(v_ref, out_ref.at[me], local_sem)
        mine.start()
        sends = []
        for k in range(1, N_DEV):
            px, py, pc = _flip(x, k & 4), _flip(y, k & 2), _flip(c, k & 1)
            cp = pltpu.make_async_remote_copy(
                src_ref=v_ref, dst_ref=out_ref.at[me], send_sem=send_sems.at[k - 1], recv_sem=recv_sems.at[k - 1],
                device_id=(px, py, pc), device_id_type=MESH)
            cp.start()
            sends.append(cp)
        for k in range(1, N_DEV):
            px, py, pc = _flip(x, k & 4), _flip(y, k & 2), _flip(c, k & 1)
            pltpu.make_async_remote_copy(
                src_ref=v_ref, dst_ref=out_ref.at[4 * px + 2 * py + pc], send_sem=send_sems.at[k - 1],
                recv_sem=recv_sems.at[k - 1], device_id=(px, py, pc), device_id_type=MESH).wait_recv()
        for cp in sends:
            cp.wait_send()
        mine.wait()

    return pl.pallas_call(
        body, name=name,
        out_shape=jax.ShapeDtypeStruct((N_DEV, rows, LANES), F32),
        in_specs=[pl.BlockSpec(memory_space=pltpu.VMEM)],
        out_specs=pl.BlockSpec(memory_space=pltpu.VMEM),
        scratch_shapes=[pltpu.SemaphoreType.DMA((N_DEV - 1,)), pltpu.SemaphoreType.DMA((N_DEV - 1,)),
                        pltpu.SemaphoreType.DMA],
        compiler_params=pltpu.CompilerParams(vmem_limit_bytes=VMEM_LIMIT),
    )(v)


def _gather_first_weight(shard, others):
    n = len(others)

    def body(*refs):
        w_ref, other_refs = refs[0], refs[1:1 + n]
        out_ref, cast_refs = refs[1 + n], refs[2 + n:2 + 2 * n]
        mine_ref, send_sems, recv_sems, local_sem = refs[2 + 2 * n:]
        x, y, c = _my_place()
        me, sibling = (x, y, c), (x, y, 1 - c)
        chips = [(1 - x, y), (x, 1 - y), (1 - x, 1 - y)]

        def block(place):
            return out_ref.at[4 * place[0] + 2 * place[1] + place[2]]

        def copy(k, place, to, src=None):
            return pltpu.make_async_remote_copy(
                src_ref=block(place) if src is None else src, dst_ref=block(place),
                send_sem=send_sems.at[k], recv_sem=recv_sems.at[k], device_id=to, device_id_type=MESH)

        mine_ref[...] = w_ref[...].astype(BF16)
        local = pltpu.make_async_copy(mine_ref, block(me), local_sem)
        local.start()
        started = [copy(0, me, sibling, src=mine_ref)]
        started += [copy(1 + j, me, (*chip, c), src=mine_ref) for j, chip in enumerate(chips)]
        for cp in started:
            cp.start()
        for o_ref, c_ref in zip(other_refs, cast_refs):
            c_ref[...] = o_ref[...].astype(BF16)
        for j, chip in enumerate(chips):
            copy(1 + j, (*chip, c), me).wait_recv()
            passed = copy(4 + j, (*chip, c), sibling)
            passed.start()
            started.append(passed)
        copy(0, sibling, me).wait_recv()
        for j, chip in enumerate(chips):
            copy(4 + j, (*chip, 1 - c), me).wait_recv()
        for cp in started:
            cp.wait_send()
        local.wait()

    vmem = pl.BlockSpec(memory_space=pltpu.VMEM)
    outs = pl.pallas_call(
        body, name="gather_w_in",
        out_shape=[jax.ShapeDtypeStruct((N_DEV,) + shard.shape, BF16)]
        + [jax.ShapeDtypeStruct(o.shape, BF16) for o in others],
        in_specs=[vmem] * (1 + n),
        out_specs=[ANY] + [vmem] * n,
        scratch_shapes=[pltpu.VMEM(shard.shape, BF16), pltpu.SemaphoreType.DMA((N_DEV - 1,)),
                        pltpu.SemaphoreType.DMA((N_DEV - 1,)), pltpu.SemaphoreType.DMA],
        compiler_params=pltpu.CompilerParams(vmem_limit_bytes=VMEM_LIMIT),
    )(shard, *others)
    return outs[0], list(outs[1:])


def _carry(rider, name):
    def body(token_ref):
        token_ref[...] = jnp.zeros_like(token_ref)

    _, routs = _call(body, name, (1,), [], [], [jax.ShapeDtypeStruct((SUBLANES, LANES), F32)],
                     [_full((SUBLANES, LANES))], rider=rider)
    return routs


def _ada_forward(c_all, w_ada, b_cols):
    cols = w_ada.shape[1]

    def body(c_ref, w_ref, b_ref, out_ref):
        cf = c_ref[...]
        act = (cf * _sigmoid(cf)).astype(BF16)
        out_ref[...] = jnp.dot(act, w_ref[...].astype(BF16), preferred_element_type=F32) + b_ref[...]

    return pl.pallas_call(
        body, name="ada_forward",
        out_shape=jax.ShapeDtypeStruct((N_DEV, cols), F32),
        in_specs=[pl.BlockSpec(memory_space=pltpu.VMEM)] * 3,
        out_specs=pl.BlockSpec(memory_space=pltpu.VMEM),
        compiler_params=pltpu.CompilerParams(vmem_limit_bytes=VMEM_LIMIT),
    )(c_all, w_ada, b_cols)


def _ada_weight_grad(c_all, dmod_cols):
    cols = dmod_cols.shape[1]

    def body(c_ref, d_ref, out_ref):
        cf = c_ref[...]
        act = (cf * _sigmoid(cf)).astype(BF16)
        out_ref[...] = lax.dot_general(act, d_ref[...].astype(BF16), TN_DIMS, preferred_element_type=F32)

    return pl.pallas_call(
        body, name="ada_weight_grad",
        out_shape=jax.ShapeDtypeStruct((D_MODEL, cols), F32),
        in_specs=[pl.BlockSpec(memory_space=pltpu.VMEM)] * 2,
        out_specs=pl.BlockSpec(memory_space=pltpu.VMEM),
        compiler_params=pltpu.CompilerParams(vmem_limit_bytes=VMEM_LIMIT),
    )(c_all, dmod_cols)


def _sum_devices(packed):
    def body(p_ref, out_ref):
        total = p_ref[0]
        for d in range(1, N_DEV):
            total = total + p_ref[d]
        out_ref[...] = total

    return pl.pallas_call(
        body, name="sum_devices",
        out_shape=jax.ShapeDtypeStruct(packed.shape[1:], F32),
        in_specs=[pl.BlockSpec(memory_space=pltpu.VMEM)],
        out_specs=pl.BlockSpec(memory_space=pltpu.VMEM),
        compiler_params=pltpu.CompilerParams(vmem_limit_bytes=VMEM_LIMIT),
    )(packed)


def _row_tile(rows, multiple):
    for cand in range(min(rows, 256), 0, -1):
        if rows % cand == 0 and cand % multiple == 0:
            return cand
    return rows


def _adamw_update(w, g, m, v):
    c1 = 1.0 / (1.0 - ADAM_B1 ** ADAM_STEP)
    c2 = 1.0 / (1.0 - ADAM_B2 ** ADAM_STEP)
    nm = ADAM_B1 * m + (1.0 - ADAM_B1) * g
    nv = ADAM_B2 * v + (1.0 - ADAM_B2) * (g * g)
    delta = -ADAM_LR * ((nm * c1) / (jnp.sqrt(nv * c2) + ADAM_EPS) + ADAM_WD * w)
    return delta, nm, nv


def _adamw(w, g, m, v, name):
    rows, cols = w.shape
    tile = _row_tile(rows, SUBLANES)

    def body(w_ref, g_ref, m_ref, v_ref, d_ref, nm_ref, nv_ref):
        d_ref[...], nm_ref[...], nv_ref[...] = _adamw_update(w_ref[...], g_ref[...], m_ref[...], v_ref[...])

    spec = pl.BlockSpec((tile, cols), lambda i: (i, 0))
    outs, _ = _call(body, name, (rows // tile,), [w, g, m, v], [spec] * 4,
                    [jax.ShapeDtypeStruct((rows, cols), F32)] * 3, [spec] * 3)
    return outs


def _sibling_sum(own, sib, name):
    _, r, cdim = own.shape
    tile = _row_tile(r, BF16_ROWS)

    def body(own_ref, sib_ref, sums_ref, mine_ref):
        mine_ref[...] = own_ref[0].astype(F32) + sib_ref[0].astype(F32)
        for f in (1, 2, 3):
            sums_ref[f - 1] = (own_ref[f].astype(F32) + sib_ref[f].astype(F32)).astype(BF16)

    outs, _ = _call(
        body, name, (r // tile,), [own, sib], [pl.BlockSpec((4, tile, cdim), lambda i: (0, i, 0))] * 2,
        [jax.ShapeDtypeStruct((3, r, cdim), BF16), jax.ShapeDtypeStruct((r, cdim), F32)],
        [pl.BlockSpec((3, tile, cdim), lambda i: (0, i, 0)), pl.BlockSpec((tile, cdim), lambda i: (i, 0))])
    return outs


def _chip_sum_adamw(mine, ici, w, m, v, name):
    r, cdim = mine.shape
    tile = _row_tile(r, BF16_ROWS)

    def body(mine_ref, ici_ref, w_ref, m_ref, v_ref, g_ref, d_ref, nm_ref, nv_ref):
        g = mine_ref[...]
        for f in range(3):
            g = g + ici_ref[f].astype(F32)
        g_ref[...] = g
        d_ref[...], nm_ref[...], nv_ref[...] = _adamw_update(w_ref[...], g, m_ref[...], v_ref[...])

    spec = pl.BlockSpec((tile, cdim), lambda i: (i, 0))
    outs, _ = _call(
        body, name, (r // tile,), [mine, ici, w, m, v],
        [spec, pl.BlockSpec((3, tile, cdim), lambda i: (0, i, 0)), spec, spec, spec],
        [jax.ShapeDtypeStruct((r, cdim), F32)] * 4, [spec] * 4)
    return outs


REF_KV_COL = D_MODEL
REF_REST_COL = D_MODEL + 2 * KV_WIDTH
IN_CHUNK = 1280
IN_PIECES = ([(0, 0, D_MODEL)]
             + [(D_MODEL + n * IN_CHUNK, REF_REST_COL + n * IN_CHUNK, IN_CHUNK) for n in range(REST_WIDTH // IN_CHUNK)]
             + [(KV_COL, REF_KV_COL, 2 * KV_WIDTH)])


def _inproj_fwd(x, vec, w_t, b_in, rider):
    t = x.shape[0]
    tm = min(TOKEN_TILE, t)

    def body(x_ref, vec_ref, w_ref, b_ref, z_ref, h_ref):
        xf = x_ref[...]
        r = lax.rsqrt(jnp.mean(xf * xf, axis=-1, keepdims=True) + EPS)
        h = (xf * r) * vec_ref[0:1, :] * (1.0 + vec_ref[1:2, :]) + vec_ref[2:3, :]
        hb = h.astype(BF16)
        h_ref[...] = hb
        for mine, ref, width in IN_PIECES:
            zc = lax.dot_general(hb, w_ref[ref:ref + width, :], NT_DIMS, preferred_element_type=F32)
            z_ref[:, mine:mine + width] = (zc + b_ref[:, ref:ref + width]).astype(BF16)

    return _call(
        body, "inproj_fwd", (t // tm,), [x, vec, w_t, b_in],
        [pl.BlockSpec((tm, D_MODEL), lambda i: (i, 0)), _full((SUBLANES, D_MODEL)),
         _full((IN_WIDTH, D_MODEL)), _full((1, IN_WIDTH))],
        [jax.ShapeDtypeStruct((t, IN_WIDTH), BF16), jax.ShapeDtypeStruct((t, D_MODEL), BF16)],
        [pl.BlockSpec((tm, IN_WIDTH), lambda i: (i, 0)), pl.BlockSpec((tm, D_MODEL), lambda i: (i, 0))],
        rider=rider)


def _window_mask(has_prev):
    qi = lax.broadcasted_iota(jnp.int32, (WINDOW, 2 * WINDOW), 0)
    kj = lax.broadcasted_iota(jnp.int32, (WINDOW, 2 * WINDOW), 1)
    off = jnp.where(has_prev, 0, 4 * WINDOW)
    in_prev = jnp.logical_and(kj < WINDOW, kj > qi + off)
    in_cur = jnp.logical_and(kj >= WINDOW, (kj - WINDOW) <= qi)
    return jnp.logical_or(in_prev, in_cur)


def _attn_fwd(z, sinks, rider):
    t = z.shape[0]
    tq = min(TOKEN_TILE, t)
    nblk = tq // WINDOW

    def body(q_ref, kv_ref, sink_ref, o_ref, lse_ref):
        i = pl.program_id(0)
        lane = lax.broadcasted_iota(jnp.int32, (WINDOW, N_Q_HEADS), 1)

        def one_block(b, carry):
            row0 = pl.multiple_of(b * WINDOW, WINDOW)
            start = i * tq + b * WINDOW
            prev = pl.multiple_of(jnp.maximum(start - WINDOW, 0), WINDOW)
            cur = pl.multiple_of(start, WINDOW)
            kvw = jnp.concatenate([kv_ref[pl.ds(prev, WINDOW), :], kv_ref[pl.ds(cur, WINDOW), :]], axis=0)
            valid = _window_mask(start > 0)
            lse_blk = jnp.zeros((WINDOW, N_Q_HEADS), F32)
            for h in range(N_Q_HEADS):
                j = h // GROUP
                k = kvw[:, j * HEAD_DIM:(j + 1) * HEAD_DIM]
                v = kvw[:, KV_WIDTH + j * HEAD_DIM:KV_WIDTH + (j + 1) * HEAD_DIM]
                qh = q_ref[pl.ds(row0, WINDOW), h * HEAD_DIM:(h + 1) * HEAD_DIM]
                s = lax.dot_general(qh, k, NT_DIMS, preferred_element_type=F32) * ATTN_SCALE
                s = jnp.where(valid, s, -jnp.inf)
                sink = sink_ref[h]
                m = jnp.maximum(jnp.max(s, axis=-1, keepdims=True), sink)
                p = jnp.exp(s - m)
                denom = jnp.sum(p, axis=-1, keepdims=True) + jnp.exp(sink - m)
                o = jnp.dot(p.astype(BF16), v, preferred_element_type=F32) / denom
                o_ref[pl.ds(row0, WINDOW), h * HEAD_DIM:(h + 1) * HEAD_DIM] = o.astype(BF16)
                lse_blk = jnp.where(lane == h, m + jnp.log(denom), lse_blk)
            lse_ref[pl.ds(row0, WINDOW), :] = lse_blk
            return carry

        lax.fori_loop(0, nblk, one_block, 0)

    return _call(
        body, "attn_fwd", (t // tq,), [z, z, sinks],
        [pl.BlockSpec((tq, D_MODEL), lambda i: (i, 0)),
         pl.BlockSpec((t, 2 * KV_WIDTH), lambda i: (0, KV_COL // (2 * KV_WIDTH))),
         pl.BlockSpec(memory_space=pltpu.SMEM)],
        [jax.ShapeDtypeStruct((t, D_MODEL), BF16), jax.ShapeDtypeStruct((t, N_Q_HEADS), F32)],
        [pl.BlockSpec((tq, D_MODEL), lambda i: (i, 0)), pl.BlockSpec((tq, N_Q_HEADS), lambda i: (i, 0))],
        rider=rider)


HALO = BF16_ROWS


def _shift_down(u, uh, k):
    row = lax.broadcasted_iota(jnp.int32, u.shape, 0)
    out = pltpu.roll(u, k, 0)
    for j in range(k):
        out = jnp.where(row == j, uh[HALO - k + j:HALO - k + j + 1, :], out)
    return out


def _shift_up(u, nxt, k):
    n = u.shape[0]
    row = lax.broadcasted_iota(jnp.int32, u.shape, 0)
    out = pltpu.roll(u, n - k, 0)
    for j in range(k):
        out = jnp.where(row == n - k + j, nxt[j:j + 1, :], out)
    return out


def _conv_inputs(cc_ref, cx_ref, hc_ref, hx_ref, first_tile):
    cc = cc_ref[...].astype(F32)
    cx = cx_ref[...].astype(F32)
    u = cc * cx
    uh = jnp.where(first_tile, 0.0, hc_ref[...].astype(F32) * hx_ref[...].astype(F32))
    return cc, cx, u, _shift_down(u, uh, 1), _shift_down(u, uh, 2)


def _z_specs(tm, order):
    per_tile = tm // HALO
    cols = [pl.BlockSpec((tm, D_MODEL), functools.partial(lambda i, j: (order(i), j), j=j)) for j in range(1, 6)]
    halos = [pl.BlockSpec((HALO, D_MODEL),
                          functools.partial(lambda i, j: (jnp.maximum(order(i) * per_tile - 1, 0), j), j=j))
             for j in (2, 3)]
    return cols + halos


def _mix_fwd(x, attn, z, vec, w_out):
    t = x.shape[0]
    tm = min(TOKEN_TILE, t)

    def body(x_ref, a_ref, cb_ref, cc_ref, cx_ref, ga_ref, gc_ref, hc_ref, hx_ref, vec_ref, w_ref,
             m_ref, x2_ref, h2_ref):
        i = pl.program_id(0)
        _, _, u, u1, u2 = _conv_inputs(cc_ref, cx_ref, hc_ref, hx_ref, i == 0)
        cv = vec_ref[4:5, :] * u2 + vec_ref[5:6, :] * u1 + vec_ref[6:7, :] * u
        conv = cb_ref[...].astype(F32) * cv
        merged = (_sigmoid(ga_ref[...].astype(F32)) * a_ref[...].astype(F32)
                  + _sigmoid(gc_ref[...].astype(F32)) * conv)
        mb = merged.astype(BF16)
        m_ref[...] = mb
        o = jnp.dot(mb, w_ref[...], preferred_element_type=F32)
        x2 = x_ref[...] + vec_ref[0:1, :] * o
        x2_ref[...] = x2
        r = lax.rsqrt(jnp.mean(x2 * x2, axis=-1, keepdims=True) + EPS)
        h2 = (x2 * r) * vec_ref[1:2, :] * (1.0 + vec_ref[2:3, :]) + vec_ref[3:4, :]
        h2_ref[...] = h2.astype(BF16)

    tok = pl.BlockSpec((tm, D_MODEL), lambda i: (i, 0))
    outs, _ = _call(
        body, "mix_fwd", (t // tm,), [x, attn, z, z, z, z, z, z, z, vec, w_out],
        [tok, tok] + _z_specs(tm, lambda i: i) + [_full((SUBLANES, D_MODEL)), _full((D_MODEL, D_MODEL))],
        [jax.ShapeDtypeStruct((t, D_MODEL), BF16), jax.ShapeDtypeStruct((t, D_MODEL), F32),
         jax.ShapeDtypeStruct((t, D_MODEL), BF16)],
        [tok, tok, tok])
    return outs


def _ffn_fwd(h2, w_t):
    t = h2.shape[0]
    tm = min(TOKEN_TILE, t)

    def body(h_ref, w_ref, gu_ref, a_ref):
        hb = h_ref[...]
        g = lax.dot_general(hb, w_ref[:D_FF, :], NT_DIMS, preferred_element_type=F32)
        u = lax.dot_general(hb, w_ref[D_FF:, :], NT_DIMS, preferred_element_type=F32)
        gu_ref[:, :D_FF] = g.astype(BF16)
        gu_ref[:, D_FF:] = u.astype(BF16)
        a_ref[...] = (g * _sigmoid(g) * u).astype(BF16)

    outs, _ = _call(
        body, "ffn_fwd", (t // tm,), [h2, w_t],
        [pl.BlockSpec((tm, D_MODEL), lambda i: (i, 0)), _full((2 * D_FF, D_MODEL))],
        [jax.ShapeDtypeStruct((t, 2 * D_FF), BF16), jax.ShapeDtypeStruct((t, D_FF), BF16)],
        [pl.BlockSpec((tm, 2 * D_FF), lambda i: (i, 0)), pl.BlockSpec((tm, D_FF), lambda i: (i, 0))])
    return outs


def _ffn_out_loss(a, gu, x2, target, vec, w_ffn_out):
    t = a.shape[0]
    tm = min(TOKEN_TILE, t)

    def body(a_ref, gu_ref, x2_ref, t_ref, vec_ref, w_ref, dx3_ref, df_ref, dgu_ref, acc_ref):
        @pl.when(pl.program_id(0) == 0)
        def _():
            acc_ref[...] = jnp.zeros_like(acc_ref)

        ga2 = vec_ref[0:1, :]
        gf = vec_ref[1:2, :]
        f = jnp.dot(a_ref[...], w_ref[...], preferred_element_type=F32)
        x3 = x2_ref[...] + ga2 * f
        r = lax.rsqrt(jnp.mean(x3 * x3, axis=-1, keepdims=True) + EPS)
        xn = x3 * r
        err = xn * gf - t_ref[...]
        dy = err * (1.0 / D_MODEL)
        dxn = dy * gf
        dx3 = r * (dxn - xn * jnp.mean(dxn * xn, axis=-1, keepdims=True))
        dx3_ref[...] = dx3
        acc_ref[0:1, :] += jnp.sum(err * err, axis=0, keepdims=True)
        acc_ref[1:2, :] += jnp.sum(dy * xn, axis=0, keepdims=True)
        acc_ref[2:3, :] += jnp.sum(dx3 * f, axis=0, keepdims=True)
        df = (dx3 * ga2).astype(BF16)
        df_ref[...] = df
        da = lax.dot_general(df, w_ref[...], NT_DIMS, preferred_element_type=F32)
        g = gu_ref[:, :D_FF].astype(F32)
        u = gu_ref[:, D_FF:].astype(F32)
        sg = _sigmoid(g)
        dgu_ref[:, :D_FF] = (da * u * (sg * (1.0 + g * (1.0 - sg)))).astype(BF16)
        dgu_ref[:, D_FF:] = (da * (g * sg)).astype(BF16)

    tok = pl.BlockSpec((tm, D_MODEL), lambda i: (i, 0))
    outs, _ = _call(
        body, "ffn_out_loss", (t // tm,), [a, gu, x2, target, vec, w_ffn_out],
        [pl.BlockSpec((tm, D_FF), lambda i: (i, 0)), pl.BlockSpec((tm, 2 * D_FF), lambda i: (i, 0)),
         tok, tok, _full((SUBLANES, D_MODEL)), _full((D_FF, D_MODEL))],
        [jax.ShapeDtypeStruct((t, D_MODEL), F32), jax.ShapeDtypeStruct((t, D_MODEL), BF16),
         jax.ShapeDtypeStruct((t, 2 * D_FF), BF16), jax.ShapeDtypeStruct((SUBLANES, D_MODEL), F32)],
        [tok, tok, pl.BlockSpec((tm, 2 * D_FF), lambda i: (i, 0)), _full((SUBLANES, D_MODEL))])
    return outs


def _ffn_in_bwd(dgu, x2, dx3, vec, w_t, rider):
    t = x2.shape[0]
    tm = min(TOKEN_TILE, t)

    def body(dgu_ref, x2_ref, dx3_ref, vec_ref, wf_ref, dx2_ref, acc_ref):
        @pl.when(pl.program_id(0) == 0)
        def _():
            acc_ref[...] = jnp.zeros_like(acc_ref)

        gffn = vec_ref[0:1, :]
        sc2 = vec_ref[1:2, :]
        dh2 = jnp.dot(dgu_ref[...], wf_ref[...], preferred_element_type=F32)
        x2 = x2_ref[...]
        r = lax.rsqrt(jnp.mean(x2 * x2, axis=-1, keepdims=True) + EPS)
        xn = x2 * r
        acc_ref[0:1, :] += jnp.sum(dh2, axis=0, keepdims=True)
        acc_ref[1:2, :] += jnp.sum(dh2 * xn * gffn, axis=0, keepdims=True)
        acc_ref[2:3, :] += jnp.sum(dh2 * xn * (1.0 + sc2), axis=0, keepdims=True)
        dxn = dh2 * gffn * (1.0 + sc2)
        dx2_ref[...] = dx3_ref[...] + r * (dxn - xn * jnp.mean(dxn * xn, axis=-1, keepdims=True))

    tok = pl.BlockSpec((tm, D_MODEL), lambda i: (i, 0))
    return _call(
        body, "ffn_in_bwd", (t // tm,), [dgu, x2, dx3, vec, w_t],
        [pl.BlockSpec((tm, 2 * D_FF), lambda i: (i, 0)), tok, tok, _full((SUBLANES, D_MODEL)),
         _full((2 * D_FF, D_MODEL))],
        [jax.ShapeDtypeStruct((t, D_MODEL), F32), jax.ShapeDtypeStruct((SUBLANES, D_MODEL), F32)],
        [tok, _full((SUBLANES, D_MODEL))], rider=rider)


def _mix_bwd(dx2, merged, attn, z, vec, w_out, rider):
    t = dx2.shape[0]
    tm = min(TOKEN_TILE, t)
    nt = t // tm
    rev = lambda i: nt - 1 - i

    def body(dx2_ref, m_ref, a_ref, cb_ref, cc_ref, cx_ref, ga_ref, gc_ref, hc_ref, hx_ref,
             vec_ref, wo_ref, do_ref, da_ref, dr_ref, acc_ref, carry_ref):
        i = pl.program_id(0)

        @pl.when(i == 0)
        def _():
            acc_ref[...] = jnp.zeros_like(acc_ref)
            carry_ref[...] = jnp.zeros_like(carry_ref)

        ga1 = vec_ref[0:1, :]
        w0, w1, w2 = vec_ref[1:2, :], vec_ref[2:3, :], vec_ref[3:4, :]
        dx2 = dx2_ref[...]
        o = jnp.dot(m_ref[...], wo_ref[...], preferred_element_type=F32)
        acc_ref[0:1, :] += jnp.sum(dx2 * o, axis=0, keepdims=True)
        do = (dx2 * ga1).astype(BF16)
        do_ref[...] = do
        dm = lax.dot_general(do, wo_ref[...], NT_DIMS, preferred_element_type=F32)

        cc, cx, u, u1, u2 = _conv_inputs(cc_ref, cx_ref, hc_ref, hx_ref, i == nt - 1)
        cv = w0 * u2 + w1 * u1 + w2 * u
        cb = cb_ref[...].astype(F32)
        sa = _sigmoid(ga_ref[...].astype(F32))
        sc = _sigmoid(gc_ref[...].astype(F32))
        attn = a_ref[...].astype(F32)
        da_ref[...] = (dm * sa).astype(BF16)
        dconv = dm * sc
        dr_ref[:, 3 * D_MODEL:4 * D_MODEL] = (dm * attn * sa * (1.0 - sa)).astype(BF16)
        dr_ref[:, 4 * D_MODEL:5 * D_MODEL] = (dconv * (cb * cv) * (1.0 - sc)).astype(BF16)
        dr_ref[:, 0:D_MODEL] = (dconv * cv).astype(BF16)
        dcv = dconv * cb
        acc_ref[1:2, :] += jnp.sum(dcv * u2, axis=0, keepdims=True)
        acc_ref[2:3, :] += jnp.sum(dcv * u1, axis=0, keepdims=True)
        acc_ref[3:4, :] += jnp.sum(dcv * u, axis=0, keepdims=True)
        nxt = carry_ref[...]
        du = w2 * dcv + w1 * _shift_up(dcv, nxt, 1) + w0 * _shift_up(dcv, nxt, 2)
        carry_ref[...] = dcv[0:SUBLANES, :]
        dr_ref[:, D_MODEL:2 * D_MODEL] = (du * cx).astype(BF16)
        dr_ref[:, 2 * D_MODEL:3 * D_MODEL] = (du * cc).astype(BF16)

    tok = pl.BlockSpec((tm, D_MODEL), lambda i: (rev(i), 0))
    return _call(
        body, "mix_bwd", (nt,), [dx2, merged, attn, z, z, z, z, z, z, z, vec, w_out],
        [tok, tok, tok] + _z_specs(tm, rev) + [_full((SUBLANES, D_MODEL)), _full((D_MODEL, D_MODEL))],
        [jax.ShapeDtypeStruct((t, D_MODEL), BF16), jax.ShapeDtypeStruct((t, D_MODEL), BF16),
         jax.ShapeDtypeStruct((t, REST_WIDTH), BF16), jax.ShapeDtypeStruct((SUBLANES, D_MODEL), F32)],
        [tok, tok, pl.BlockSpec((tm, REST_WIDTH), lambda i: (rev(i), 0)), _full((SUBLANES, D_MODEL))],
        scratch=[pltpu.VMEM((SUBLANES, D_MODEL), F32)], rider=rider)


def _attn_bwd(z, dattn, attn, lse, sinks, rider):
    t = z.shape[0]
    tq = min(TOKEN_TILE, t)
    nblk = tq // WINDOW
    nt = t // tq

    def body(q_ref, kv_ref, do_ref, o_ref, lse_ref, sink_ref, dq_ref, dkv_ref, ds_ref, acc_ref):
        i = pl.program_id(0)

        @pl.when(i == 0)
        def _():
            acc_ref[...] = jnp.zeros_like(acc_ref)
            ds_ref[...] = jnp.zeros_like(ds_ref)

        lane = lax.broadcasted_iota(jnp.int32, (1, LANES), 1)

        def one_block(b, dsink):
            row0 = pl.multiple_of(b * WINDOW, WINDOW)
            start = i * tq + b * WINDOW
            prev = pl.multiple_of(jnp.maximum(start - WINDOW, 0), WINDOW)
            cur = pl.multiple_of(start, WINDOW)
            kvw = jnp.concatenate([kv_ref[pl.ds(prev, WINDOW), :], kv_ref[pl.ds(cur, WINDOW), :]], axis=0)
            valid = _window_mask(start > 0)
            lse_blk = lse_ref[pl.ds(row0, WINDOW), :]
            parts = []
            for j in range(N_KV_HEADS):
                k = kvw[:, j * HEAD_DIM:(j + 1) * HEAD_DIM]
                v = kvw[:, KV_WIDTH + j * HEAD_DIM:KV_WIDTH + (j + 1) * HEAD_DIM]
                dk = jnp.zeros((2 * WINDOW, HEAD_DIM), F32)
                dv = jnp.zeros((2 * WINDOW, HEAD_DIM), F32)
                for g in range(GROUP):
                    h = j * GROUP + g
                    cols = slice(h * HEAD_DIM, (h + 1) * HEAD_DIM)
                    qh = q_ref[pl.ds(row0, WINDOW), cols]
                    doh = do_ref[pl.ds(row0, WINDOW), cols]
                    oh = o_ref[pl.ds(row0, WINDOW), cols]
                    lse_h = lse_blk[:, h:h + 1]
                    s = lax.dot_general(qh, k, NT_DIMS, preferred_element_type=F32) * ATTN_SCALE
                    p = jnp.where(valid, jnp.exp(s - lse_h), 0.0)
                    delta = jnp.sum(doh.astype(F32) * oh.astype(F32), axis=-1, keepdims=True)
                    dp = lax.dot_general(doh, v, NT_DIMS, preferred_element_type=F32)
                    dsb = (p * (dp - delta)).astype(BF16)
                    dq = jnp.dot(dsb, k, preferred_element_type=F32) * ATTN_SCALE
                    dq_ref[pl.ds(row0, WINDOW), cols] = dq.astype(BF16)
                    dk = dk + lax.dot_general(dsb, qh, TN_DIMS, preferred_element_type=F32)
                    dv = dv + lax.dot_general(p.astype(BF16), doh, TN_DIMS, preferred_element_type=F32)
                    psink = jnp.exp(sink_ref[h] - lse_h)
                    dsink = dsink - jnp.where(lane == h, jnp.sum(psink * delta), 0.0)
                parts.append((dk * ATTN_SCALE, dv))
            blk = jnp.concatenate([parts[0][0], parts[1][0], parts[0][1], parts[1][1]], axis=1)
            acc_ref[pl.ds(prev, WINDOW), :] += blk[:WINDOW, :]
            acc_ref[pl.ds(cur, WINDOW), :] += blk[WINDOW:, :]
            return dsink

        dsink = lax.fori_loop(0, nblk, one_block, jnp.zeros((1, LANES), F32))
        ds_ref[0:1, :] += dsink

        @pl.when(i == nt - 1)
        def _():
            dkv_ref[...] = acc_ref[...].astype(BF16)

    tok = pl.BlockSpec((tq, D_MODEL), lambda i: (i, 0))
    return _call(
        body, "attn_bwd", (nt,), [z, z, dattn, attn, lse, sinks],
        [tok, pl.BlockSpec((t, 2 * KV_WIDTH), lambda i: (0, KV_COL // (2 * KV_WIDTH))), tok, tok,
         pl.BlockSpec((tq, N_Q_HEADS), lambda i: (i, 0)), pl.BlockSpec(memory_space=pltpu.SMEM)],
        [jax.ShapeDtypeStruct((t, D_MODEL), BF16), jax.ShapeDtypeStruct((t, 2 * KV_WIDTH), BF16),
         jax.ShapeDtypeStruct((SUBLANES, LANES), F32)],
        [tok, _full((t, 2 * KV_WIDTH)), _full((SUBLANES, LANES))],
        scratch=[pltpu.VMEM((t, 2 * KV_WIDTH), F32)], rider=rider)


def _inproj_bwd(dq, drest, dkv, x, dx2, vec, w_t, rider):
    t = x.shape[0]
    tm = min(TOKEN_TILE, t)

    def body(dq_ref, dr_ref, dkv_ref, x_ref, dx2_ref, vec_ref, w_ref, gx_ref, acc_ref, db_ref):
        @pl.when(pl.program_id(0) == 0)
        def _():
            acc_ref[...] = jnp.zeros_like(acc_ref)
            db_ref[...] = jnp.zeros_like(db_ref)

        g = vec_ref[0:1, :]
        sc1 = vec_ref[1:2, :]
        dqb, drb, dkvb = dq_ref[...], dr_ref[...], dkv_ref[...]
        dh = jnp.dot(dqb, w_ref[:REF_KV_COL, :], preferred_element_type=F32)
        dh = dh + jnp.dot(drb, w_ref[REF_REST_COL:, :], preferred_element_type=F32)
        dh = dh + jnp.dot(dkvb, w_ref[REF_KV_COL:REF_REST_COL, :], preferred_element_type=F32)
        db_ref[:, :REF_KV_COL] += jnp.sum(dqb.astype(F32), axis=0, keepdims=True)
        db_ref[:, REF_REST_COL:] += jnp.sum(drb.astype(F32), axis=0, keepdims=True)
        db_ref[:, REF_KV_COL:REF_REST_COL] += jnp.sum(dkvb.astype(F32), axis=0, keepdims=True)
        xf = x_ref[...]
        r = lax.rsqrt(jnp.mean(xf * xf, axis=-1, keepdims=True) + EPS)
        xn = xf * r
        acc_ref[0:1, :] += jnp.sum(dh, axis=0, keepdims=True)
        acc_ref[1:2, :] += jnp.sum(dh * xn * g, axis=0, keepdims=True)
        acc_ref[2:3, :] += jnp.sum(dh * xn * (1.0 + sc1), axis=0, keepdims=True)
        dxn = dh * g * (1.0 + sc1)
        gx_ref[...] = dx2_ref[...] + r * (dxn - xn * jnp.mean(dxn * xn, axis=-1, keepdims=True))

    tok = pl.BlockSpec((tm, D_MODEL), lambda i: (i, 0))
    return _call(
        body, "inproj_bwd", (t // tm,), [dq, drest, dkv, x, dx2, vec, w_t],
        [tok, pl.BlockSpec((tm, REST_WIDTH), lambda i: (i, 0)),
         pl.BlockSpec((tm, 2 * KV_WIDTH), lambda i: (i, 0)), tok, tok,
         _full((SUBLANES, D_MODEL)), _full((IN_WIDTH, D_MODEL))],
        [jax.ShapeDtypeStruct((t, D_MODEL), F32), jax.ShapeDtypeStruct((SUBLANES, D_MODEL), F32),
         jax.ShapeDtypeStruct((1, IN_WIDTH), F32)],
        [tok, _full((SUBLANES, D_MODEL)), _full((1, IN_WIDTH))], rider=rider)


def _weight_grad(b, a, name, bn, rows=None, row0=0, into=None, rider=None):
    t, n = b.shape
    m = a.shape[1]
    rows = n if rows is None else rows
    tk = min(TOKEN_TILE, t)
    nk = t // tk
    block0 = row0 // bn

    def body(b_ref, a_ref, *rest):
        out_ref, acc_ref = rest[-2:]
        k = pl.program_id(1)

        @pl.when(k == 0)
        def _():
            acc_ref[...] = jnp.zeros_like(acc_ref)

        acc_ref[...] += lax.dot_general(b_ref[...], a_ref[...], TN_DIMS, preferred_element_type=F32)

        @pl.when(k == nk - 1)
        def _():
            out_ref[...] = acc_ref[...].astype(BF16)

    outs, routs = _call(
        body, name, (n // bn, nk), [b, a] + ([] if into is None else [into]),
        [pl.BlockSpec((tk, bn), lambda j, k: (k, j)), pl.BlockSpec((tk, m), lambda j, k: (k, 0))]
        + ([] if into is None else [ANY]),
        [jax.ShapeDtypeStruct((rows, m), BF16)], [pl.BlockSpec((bn, m), lambda j, k: (block0 + j, 0))],
        scratch=[pltpu.VMEM((bn, m), F32)], rider=rider, aliases=None if into is None else {2: 0})
    return outs[0], routs


def _to_rows(v):
    n = v.shape[0]
    padded = -(-n // (SUBLANES * LANES)) * SUBLANES * LANES
    return jnp.pad(v, (0, padded - n)).reshape(padded // LANES, LANES)


def _vec_rows(*rows):
    stacked = jnp.concatenate([r.reshape(1, D_MODEL) for r in rows], axis=0)
    return jnp.pad(stacked, ((0, SUBLANES - len(rows)), (0, 0)))


def kernel(x, c, w_ada, b_ada, g_mix, w_in, b_in, sinks, conv_w, w_out, g_ffn, w_ffn_in, w_ffn_out, g_final, loss_target, m_w_ada, m_b_ada, m_g_mix, m_w_in, m_b_in, m_sinks, m_conv_w, m_w_out, m_g_ffn, m_w_ffn_in, m_w_ffn_out, m_g_final, v_w_ada, v_b_ada, v_g_mix, v_w_in, v_b_in, v_sinks, v_conv_w, v_w_out, v_g_ffn, v_w_ffn_in, v_w_ffn_out, v_g_final):
    ix, iy, ic = _my_place()
    me = 4 * ix + 2 * iy + ic
    xs = x[0]
    target = loss_target[0]
    ada_cols = w_ada.shape[2]
    conv_cols = conv_w.shape[2]

    first = _small_allgather(_to_rows(jnp.concatenate([c[0], conv_w[0].reshape(-1)])), "gather_c_conv")
    first = first.reshape(N_DEV, -1)
    c_all = first[:, :D_MODEL]
    conv_full = jnp.transpose(first[:, D_MODEL:D_MODEL + 3 * conv_cols].reshape(N_DEV, 3, conv_cols), (1, 0, 2))
    conv_full = conv_full.reshape(3, D_MODEL)
    b_cols = lax.dynamic_slice_in_dim(b_ada, me * ada_cols, ada_cols, axis=1)
    mod_part = _ada_forward(c_all, w_ada[0], b_cols)
    mod_all = _small_allgather(mod_part.reshape(-1, LANES), "gather_mod").reshape(N_DEV, N_DEV, ada_cols)
    mod = lax.dynamic_index_in_dim(mod_all, me, axis=1, keepdims=False).reshape(N_MOD, D_MODEL)
    sh1, sc1, ga1, sh2, sc2, ga2 = [mod[i:i + 1] for i in range(N_MOD)]

    wt_in, wt_fi = jnp.transpose(w_in[0]), jnp.transpose(w_ffn_in[0])
    g_in, (cast_fi, cast_out, cast_fo) = _gather_first_weight(wt_in, [wt_fi, w_out[0], w_ffn_out[0]])
    w_in_t = g_in.reshape(IN_WIDTH, D_MODEL)
    (z, h1), (g_fi, g_out) = _inproj_fwd(xs, _vec_rows(g_mix, sc1, sh1), w_in_t, b_in,
                                         _gather_rider([cast_fi, cast_out]))
    w_fi_t = g_fi.reshape(2 * D_FF, D_MODEL)
    w_out_full = g_out.reshape(D_MODEL, D_MODEL)
    (attn, lse), (g_fo,) = _attn_fwd(z, sinks[0], _gather_rider([cast_fo]))
    w_fo_full = g_fo.reshape(D_FF, D_MODEL)
    merged, x2, h2 = _mix_fwd(xs, attn, z, _vec_rows(ga1, g_ffn, sc2, sh2, conv_full[0], conv_full[1], conv_full[2]),
                              w_out_full)
    gu, act = _ffn_fwd(h2, w_fi_t)
    dx3, df, dgu, acc_l = _ffn_out_loss(act, gu, x2, target, _vec_rows(ga2, g_final), w_fo_full)

    gw_fo, _ = _weight_grad(act, df, "wgrad_ffn_out", D_FF)
    gw_fi, _ = _weight_grad(dgu, h2, "wgrad_ffn_in", D_FF)
    blocks_fo = gw_fo.reshape(N_DEV, D_FF // N_DEV, D_MODEL)
    blocks_fi = gw_fi.reshape(N_DEV, 2 * D_FF // N_DEV, D_MODEL)
    (dx2, acc_f), (sib_fo, sib_fi) = _ffn_in_bwd(dgu, x2, dx3, _vec_rows(g_ffn, sc2), w_fi_t,
                                                 _sibling_rider([blocks_fo, blocks_fi]))
    sums_fo, mine_fo = _sibling_sum(_own_blocks(blocks_fo), sib_fo, "sibling_sum_ffn_out")
    sums_fi, mine_fi = _sibling_sum(_own_blocks(blocks_fi), sib_fi, "sibling_sum_ffn_in")
    (dout, dattn, drest, acc_m), (ici_fo, ici_fi) = _mix_bwd(
        dx2, merged, attn, z, _vec_rows(ga1, conv_full[0], conv_full[1], conv_full[2]), w_out_full,
        _chip_rider([sums_fo, sums_fi]))
    gw_out, _ = _weight_grad(merged, dout, "wgrad_out", D_MODEL)
    blocks_out = gw_out.reshape(N_DEV, D_MODEL // N_DEV, D_MODEL)
    (dq, dkv, dsink), (sib_out,) = _attn_bwd(z, dattn, attn, lse, sinks[0], _sibling_rider([blocks_out]))
    sums_out, mine_out = _sibling_sum(_own_blocks(blocks_out), sib_out, "sibling_sum_out")
    gw_in, (ici_out,) = _weight_grad(drest, h1, "wgrad_in_rest", IN_CHUNK, rows=IN_WIDTH, row0=REF_REST_COL,
                                     rider=_chip_rider([sums_out]))
    gw_in, _ = _weight_grad(dq, h1, "wgrad_in_q", D_MODEL, rows=IN_WIDTH, row0=0, into=gw_in)
    gw_in, _ = _weight_grad(dkv, h1, "wgrad_in_kv", 2 * KV_WIDTH, rows=IN_WIDTH, row0=REF_KV_COL, into=gw_in)
    blocks_in = gw_in.reshape(N_DEV, IN_WIDTH // N_DEV, D_MODEL)
    (sib_in,) = _carry(_sibling_rider([blocks_in]), "sibling_w_in")
    sums_in, mine_in = _sibling_sum(_own_blocks(blocks_in), sib_in, "sibling_sum_in")
    (grad_x, acc_i, db_in), (ici_in,) = _inproj_bwd(dq, drest, dkv, xs, dx2, _vec_rows(g_mix, sc1), w_in_t,
                                                    _chip_rider([sums_in]))

    pieces = [acc_i[0], acc_i[1], acc_m[0], acc_f[0], acc_f[1], acc_l[2],
              acc_i[2], db_in[0], acc_f[2], acc_l[1],
              acc_m[1], acc_m[2], acc_m[3], dsink[0], acc_l[0]]
    offsets = [0]
    for p in pieces:
        offsets.append(offsets[-1] + p.shape[0])
    packed = _small_allgather(_to_rows(jnp.concatenate(pieces)), "gather_small")
    dmod_all = packed.reshape(N_DEV, -1)[:, :N_MOD * D_MODEL]
    total = _sum_devices(packed).reshape(-1)
    part = lambda i: total[offsets[i]:offsets[i + 1]]
    g_b_ada = total[:N_MOD * D_MODEL].reshape(1, -1)
    g_g_mix, g_b_in, g_g_ffn, g_g_final = part(6).reshape(1, -1), part(7).reshape(1, -1), part(8).reshape(1, -1), part(9)
    g_conv_full = jnp.stack([part(10), part(11), part(12)])
    g_conv = lax.dynamic_slice_in_dim(g_conv_full, me * conv_cols, conv_cols, axis=1)[None]
    g_sinks = part(13)[:N_Q_HEADS].reshape(1, -1)
    loss = (0.5 / D_MODEL) * jnp.sum(part(14))
    dmod_cols = lax.dynamic_slice_in_dim(dmod_all, me * ada_cols, ada_cols, axis=1)
    g_w_ada = _ada_weight_grad(c_all, dmod_cols)

    def reduced(mine, ici, w, m, v, name, transposed=False):
        turn = jnp.transpose if transposed else (lambda a: a)
        return tuple(turn(o)[None] for o in _chip_sum_adamw(mine, ici, turn(w[0]), turn(m[0]), turn(v[0]), name))

    d_ada, nm_ada, nv_ada = _adamw(w_ada[0], g_w_ada, m_w_ada[0], v_w_ada[0], "adamw_w_ada")
    small_names = ["b_ada", "g_mix", "b_in", "sinks", "conv_w", "g_ffn", "g_final"]
    small_w = [b_ada, g_mix, b_in, sinks, conv_w, g_ffn, g_final]
    small_m = [m_b_ada, m_g_mix, m_b_in, m_sinks, m_conv_w, m_g_ffn, m_g_final]
    small_v = [v_b_ada, v_g_mix, v_b_in, v_sinks, v_conv_w, v_g_ffn, v_g_final]
    small_g = [g_b_ada, g_g_mix, g_b_in, g_sinks, g_conv, g_g_ffn, g_g_final]
    small_g = [g.reshape(w.shape) for g, w in zip(small_g, small_w)]
    flat = lambda arrs: _to_rows(jnp.concatenate([a.reshape(-1) for a in arrs]))
    sd, snm, snv = _adamw(flat(small_w), flat(small_g), flat(small_m), flat(small_v), "adamw_small")
    sizes = [w.size for w in small_w]
    starts = [sum(sizes[:i]) for i in range(len(sizes))]
    unflat = lambda a: {n: a.reshape(-1)[s:s + z_].reshape(w.shape)
                        for n, s, z_, w in zip(small_names, starts, sizes, small_w)}
    sd, snm, snv = unflat(sd), unflat(snm), unflat(snv)
    sg = dict(zip(small_names, small_g))

    res = {
        "w_ada": (g_w_ada[None], d_ada[None], nm_ada[None], nv_ada[None]),
        "w_in": reduced(mine_in, ici_in, w_in, m_w_in, v_w_in, "adamw_w_in", transposed=True),
        "w_out": reduced(mine_out, ici_out, w_out, m_w_out, v_w_out, "adamw_w_out"),
        "w_ffn_in": reduced(mine_fi, ici_fi, w_ffn_in, m_w_ffn_in, v_w_ffn_in, "adamw_w_ffn_in", transposed=True),
        "w_ffn_out": reduced(mine_fo, ici_fo, w_ffn_out, m_w_ffn_out, v_w_ffn_out, "adamw_w_ffn_out"),
    }
    for n in small_names:
        res[n] = (sg[n], sd[n], snm[n], snv[n])
    order = ["w_ada", "b_ada", "g_mix", "w_in", "b_in", "sinks", "conv_w", "w_out", "g_ffn", "w_ffn_in", "w_ffn_out",
             "g_final"]
    outs = [loss, grad_x[None]]
    for k in range(4):
        outs += [res[n][k] for n in order]
    return tuple(outs)
```

```python
import functools
import math

import jax
import jax.numpy as jnp
from jax import lax
from jax.experimental import pallas as pl
from jax.experimental.pallas import tpu as pltpu

F32 = jnp.float32
BF16 = jnp.bfloat16

D_MODEL = 1024
HEAD_DIM = 64
N_Q_HEADS = 16
N_KV_HEADS = 2
GROUP = 8
WINDOW = 128
KV_WIDTH = N_KV_HEADS * HEAD_DIM
D_FF = 2816
IN_WIDTH = 6400
N_MOD = 6
EPS = 1e-6
N_DEV = 8
REST_WIDTH = 5 * D_MODEL
KV_COL = D_MODEL + REST_WIDTH
ATTN_SCALE = HEAD_DIM ** -0.5

ADAM_LR = 0.001
ADAM_B1 = 0.9
ADAM_B2 = 0.999
ADAM_EPS = 1e-08
ADAM_WD = 0.01
ADAM_STEP = 10

LANES = 128
SUBLANES = 8
BF16_ROWS = 16
VMEM_LIMIT = 56 * 1024 * 1024
TOKEN_TILE = 512
FF_CHUNK = 256
WGRAD_VMEM = 40 * 1024 * 1024
MESH = pl.DeviceIdType.MESH
ANY = pl.BlockSpec(memory_space=pl.ANY)

NT_DIMS = (((1,), (1,)), ((), ()))
TN_DIMS = (((0,), (0,)), ((), ()))
CHIP_FLIPS = [(0, 0), (1, 0), (0, 1), (1, 1)]


def _full(shape):
    return pl.BlockSpec(shape, lambda *_: (0,) * len(shape))


def _my_place():
    return lax.axis_index("x"), lax.axis_index("y"), lax.axis_index("c")


def _flip(v, bit):
    return 1 - v if bit else v


def _sigmoid(v):
    return 1.0 / (1.0 + jnp.exp(-v))


class _Rider:
    def __init__(self, ins, out_shapes, sem_shapes, first=None, mid=None, last=None, ins_in_vmem=False):
        self.ins, self.out_shapes, self.sem_shapes = list(ins), list(out_shapes), list(sem_shapes)
        self.in_specs = [_full(a.shape) if ins_in_vmem else ANY for a in self.ins]
        self.hooks = [(when, fn) for when, fn in (("first", first), ("mid", mid), ("last", last)) if fn is not None]


def _call(body, name, grid, args, in_specs, out_shape, out_specs, scratch=(), rider=None, aliases=None):
    n_in, n_out, n_scr = len(args), len(out_shape), len(scratch)
    r_in = rider.ins if rider else []
    r_out = rider.out_shapes if rider else []
    r_sem = rider.sem_shapes if rider else []
    nsteps = math.prod(grid)

    def full_body(*refs):
        pos = 0
        groups = []
        for size in (n_in, len(r_in), n_out, len(r_out), n_scr, len(r_sem)):
            groups.append(refs[pos:pos + size])
            pos += size
        ins, rins, outs, routs, scr, rsems = groups
        step = pl.program_id(0)
        for axis in range(1, len(grid)):
            step = step * grid[axis] + pl.program_id(axis)
        at = {"first": 0, "mid": (3 * nsteps) // 4, "last": nsteps - 1}
        hooks = rider.hooks if rider else []
        for when, fn in hooks:
            if when != "last":
                pl.when(step == at[when])(functools.partial(fn, rins, routs, rsems))
        body(*ins, *outs, *scr)
        for when, fn in hooks:
            if when == "last":
                pl.when(step == at[when])(functools.partial(fn, rins, routs, rsems))

    outs = pl.pallas_call(
        full_body, name=name, grid=grid,
        out_shape=list(out_shape) + list(r_out),
        in_specs=list(in_specs) + (rider.in_specs if rider else []),
        out_specs=list(out_specs) + [ANY] * len(r_out),
        scratch_shapes=list(scratch) + list(r_sem),
        input_output_aliases=dict(aliases or {}),
        compiler_params=pltpu.CompilerParams(dimension_semantics=("arbitrary",) * len(grid),
                                             vmem_limit_bytes=VMEM_LIMIT),
    )(*args, *r_in)
    return list(outs[:n_out]), list(outs[n_out:])


def _gather_rider(shards):
    n = len(shards)

    def setup(outs, sems):
        x, y, c = _my_place()
        send_sems, recv_sems, _ = sems
        chips = [(1 - x, y), (x, 1 - y), (1 - x, 1 - y)]

        def block(w, place):
            return outs[w].at[4 * place[0] + 2 * place[1] + place[2]]

        def copy(w, k, place, to, src=None):
            return pltpu.make_async_remote_copy(
                src_ref=block(w, place) if src is None else src, dst_ref=block(w, place),
                send_sem=send_sems.at[w, k], recv_sem=recv_sems.at[w, k], device_id=to, device_id_type=MESH)

        return (x, y, c), (x, y, 1 - c), chips, block, copy

    def first(ins, outs, sems):
        me, sibling, chips, block, copy = setup(outs, sems)
        for w in range(n):
            pltpu.make_async_copy(ins[w], block(w, me), sems[2].at[w]).start()
            copy(w, 0, me, sibling, src=ins[w]).start()
            for j, chip in enumerate(chips):
                copy(w, 1 + j, me, (*chip, me[2]), src=ins[w]).start()

    def mid(ins, outs, sems):
        me, sibling, chips, block, copy = setup(outs, sems)
        for w in range(n):
            for j, chip in enumerate(chips):
                copy(w, 1 + j, (*chip, me[2]), me).wait_recv()
                copy(w, 4 + j, (*chip, me[2]), sibling).start()

    def last(ins, outs, sems):
        me, sibling, chips, block, copy = setup(outs, sems)
        for w in range(n):
            copy(w, 0, sibling, me).wait_recv()
            for j, chip in enumerate(chips):
                copy(w, 4 + j, (*chip, 1 - me[2]), me).wait_recv()
            copy(w, 0, me, sibling, src=ins[w]).wait_send()
            for j, chip in enumerate(chips):
                copy(w, 1 + j, me, (*chip, me[2]), src=ins[w]).wait_send()
                copy(w, 4 + j, (*chip, me[2]), sibling).wait_send()
            pltpu.make_async_copy(ins[w], block(w, me), sems[2].at[w]).wait()

    return _Rider(
        shards, [jax.ShapeDtypeStruct((N_DEV,) + s.shape, BF16) for s in shards],
        [pltpu.SemaphoreType.DMA((n, N_DEV - 1)), pltpu.SemaphoreType.DMA((n, N_DEV - 1)),
         pltpu.SemaphoreType.DMA((n,))],
        first=first, mid=mid, last=last, ins_in_vmem=True)


def _sibling_rider(gblocks):
    n = len(gblocks)

    def copies(ins, outs, sems):
        x, y, c = _my_place()
        send_sems, recv_sems = sems
        made = []
        for w in range(n):
            for f, (fx, fy) in enumerate(CHIP_FLIPS):
                chip = 4 * _flip(x, fx) + 2 * _flip(y, fy)
                made.append(pltpu.make_async_remote_copy(
                    src_ref=ins[w].at[chip + 1 - c], dst_ref=outs[w].at[f], send_sem=send_sems.at[w, f],
                    recv_sem=recv_sems.at[w, f], device_id=(x, y, 1 - c), device_id_type=MESH))
        return made

    def first(ins, outs, sems):
        for cp in copies(ins, outs, sems):
            cp.start()

    def last(ins, outs, sems):
        for cp in copies(ins, outs, sems):
            cp.wait_recv()
            cp.wait_send()

    return _Rider(gblocks, [jax.ShapeDtypeStruct((4,) + g.shape[1:], BF16) for g in gblocks],
                  [pltpu.SemaphoreType.DMA((n, 4))] * 2, first=first, last=last)


def _own_blocks(gblocks):
    x, y, c = _my_place()
    return jnp.stack([lax.dynamic_index_in_dim(gblocks, 4 * _flip(x, fx) + 2 * _flip(y, fy) + c, 0, keepdims=False)
                      for fx, fy in CHIP_FLIPS])


def _chip_rider(sums):
    n = len(sums)

    def copies(ins, outs, sems):
        x, y, c = _my_place()
        send_sems, recv_sems = sems
        made = []
        for w in range(n):
            for f in (1, 2, 3):
                fx, fy = CHIP_FLIPS[f]
                made.append(pltpu.make_async_remote_copy(
                    src_ref=ins[w].at[f - 1], dst_ref=outs[w].at[f - 1], send_sem=send_sems.at[w, f - 1],
                    recv_sem=recv_sems.at[w, f - 1], device_id=(_flip(x, fx), _flip(y, fy), c), device_id_type=MESH))
        return made

    def first(ins, outs, sems):
        for cp in copies(ins, outs, sems):
            cp.start()

    def last(ins, outs, sems):
        for cp in copies(ins, outs, sems):
            cp.wait_recv()
            cp.wait_send()

    return _Rider(sums, [jax.ShapeDtypeStruct(s.shape, BF16) for s in sums],
                  [pltpu.SemaphoreType.DMA((n, 3))] * 2, first=first, last=last)


def _small_allgather(v, name):
    rows = v.shape[0]

    def body(v_ref, out_ref, send_sems, recv_sems, local_sem):
        x, y, c = _my_place()
        me = 4 * x + 2 * y + c
        mine = pltpu.make_async_copy(v_ref, out_ref.at[me], local_sem)
        mine.start()
        sends = []
        for k in range(1, N_DEV):
            px, py, pc = _flip(x, k & 4), _flip(y, k & 2), _flip(c, k & 1)
            cp = pltpu.make_async_remote_copy(
                src_ref=v_ref, dst_ref=out_ref.at[me], send_sem=send_sems.at[k - 1], recv_sem=recv_sems.at[k - 1],
                device_id=(px, py, pc), device_id_type=MESH)
            cp.start()
            sends.append(cp)
        for k in range(1, N_DEV):
            px, py, pc = _flip(x, k & 4), _flip(y, k & 2), _flip(c, k & 1)
            pltpu.make_async_remote_copy(
                src_ref=v_ref, dst_ref=out_ref.at[4 * px + 2 * py + pc], send_sem=send_sems.at[k - 1],
                recv_sem=recv_sems.at[k - 1], device_id=(px, py, pc), device_id_type=MESH).wait_recv()
        for cp in sends:
            cp.wait_send()
        mine.wait()

    return pl.pallas_call(
        body, name=name,
        out_shape=jax.ShapeDtypeStruct((N_DEV, rows, LANES), F32),
        in_specs=[pl.BlockSpec(memory_space=pltpu.VMEM)],
        out_specs=pl.BlockSpec(memory_space=pltpu.VMEM),
        scratch_shapes=[pltpu.SemaphoreType.DMA((N_DEV - 1,)), pltpu.SemaphoreType.DMA((N_DEV - 1,)),
                        pltpu.SemaphoreType.DMA],
        compiler_params=pltpu.CompilerParams(vmem_limit_bytes=VMEM_LIMIT),
    )(v)


def _gather_first_weight(shard, others):
    n = len(others)

    def body(*refs):
        w_ref, other_refs = refs[0], refs[1:1 + n]
        out_ref, cast_refs = refs[1 + n], refs[2 + n:2 + 2 * n]
        mine_ref, send_sems, recv_sems, local_sem = refs[2 + 2 * n:]
        x, y, c = _my_place()
        me, sibling = (x, y, c), (x, y, 1 - c)
        chips = [(1 - x, y), (x, 1 - y), (1 - x, 1 - y)]

        def block(place):
            return out_ref.at[4 * place[0] + 2 * place[1] + place[2]]

        def copy(k, place, to, src=None):
            return pltpu.make_async_remote_copy(
                src_ref=block(place) if src is None else src, dst_ref=block(place),
                send_sem=send_sems.at[k], recv_sem=recv_sems.at[k], device_id=to, device_id_type=MESH)

        mine_ref[...] = w_ref[...].astype(BF16)
        local = pltpu.make_async_copy(mine_ref, block(me), local_sem)
        local.start()
        started = [copy(0, me, sibling, src=mine_ref)]
        started += [copy(1 + j, me, (*chip, c), src=mine_ref) for j, chip in enumerate(chips)]
        for cp in started:
            cp.start()
        for o_ref, c_ref in zip(other_refs, cast_refs):
            c_ref[...] = o_ref[...].astype(BF16)
        for j, chip in enumerate(chips):
            copy(1 + j, (*chip, c), me).wait_recv()
            passed = copy(4 + j, (*chip, c), sibling)
            passed.start()
            started.append(passed)
        copy(0, sibling, me).wait_recv()
        for j, chip in enumerate(chips):
            copy(4 + j, (*chip, 1 - c), me).wait_recv()
        for cp in started:
            cp.wait_send()
        local.wait()

    vmem = pl.BlockSpec(memory_space=pltpu.VMEM)
    outs = pl.pallas_call(
        body, name="gather_w_in",
        out_shape=[jax.ShapeDtypeStruct((N_DEV,) + shard.shape, BF16)]
        + [jax.ShapeDtypeStruct(o.shape, BF16) for o in others],
        in_specs=[vmem] * (1 + n),
        out_specs=[ANY] + [vmem] * n,
        scratch_shapes=[pltpu.VMEM(shard.shape, BF16), pltpu.SemaphoreType.DMA((N_DEV - 1,)),
                        pltpu.SemaphoreType.DMA((N_DEV - 1,)), pltpu.SemaphoreType.DMA],
        compiler_params=pltpu.CompilerParams(vmem_limit_bytes=VMEM_LIMIT),
    )(shard, *others)
    return outs[0], list(outs[1:])


def _carry(rider, name):
    def body(token_ref):
        token_ref[...] = jnp.zeros_like(token_ref)

    _, routs = _call(body, name, (1,), [], [], [jax.ShapeDtypeStruct((SUBLANES, LANES), F32)],
                     [_full((SUBLANES, LANES))], rider=rider)
    return routs


def _ada_forward(c_all, w_ada, b_cols):
    cols = w_ada.shape[1]

    def body(c_ref, w_ref, b_ref, out_ref):
        cf = c_ref[...]
        act = (cf * _sigmoid(cf)).astype(BF16)
        out_ref[...] = jnp.dot(act, w_ref[...].astype(BF16), preferred_element_type=F32) + b_ref[...]

    return pl.pallas_call(
        body, name="ada_forward",
        out_shape=jax.ShapeDtypeStruct((N_DEV, cols), F32),
        in_specs=[pl.BlockSpec(memory_space=pltpu.VMEM)] * 3,
        out_specs=pl.BlockSpec(memory_space=pltpu.VMEM),
        compiler_params=pltpu.CompilerParams(vmem_limit_bytes=VMEM_LIMIT),
    )(c_all, w_ada, b_cols)


def _ada_weight_grad(c_all, dmod_cols):
    cols = dmod_cols.shape[1]

    def body(c_ref, d_ref, out_ref):
        cf = c_ref[...]
        act = (cf * _sigmoid(cf)).astype(BF16)
        out_ref[...] = lax.dot_general(act, d_ref[...].astype(BF16), TN_DIMS, preferred_element_type=F32)

    return pl.pallas_call(
        body, name="ada_weight_grad",
        out_shape=jax.ShapeDtypeStruct((D_MODEL, cols), F32),
        in_specs=[pl.BlockSpec(memory_space=pltpu.VMEM)] * 2,
        out_specs=pl.BlockSpec(memory_space=pltpu.VMEM),
        compiler_params=pltpu.CompilerParams(vmem_limit_bytes=VMEM_LIMIT),
    )(c_all, dmod_cols)


def _sum_devices(packed):
    def body(p_ref, out_ref):
        total = p_ref[0]
        for d in range(1, N_DEV):
            total = total + p_ref[d]
        out_ref[...] = total

    return pl.pallas_call(
        body, name="sum_devices",
        out_shape=jax.ShapeDtypeStruct(packed.shape[1:], F32),
        in_specs=[pl.BlockSpec(memory_space=pltpu.VMEM)],
        out_specs=pl.BlockSpec(memory_space=pltpu.VMEM),
        compiler_params=pltpu.CompilerParams(vmem_limit_bytes=VMEM_LIMIT),
    )(packed)


def _row_tile(rows, multiple):
    for cand in range(min(rows, 256), 0, -1):
        if rows % cand == 0 and cand % multiple == 0:
            return cand
    return rows


def _adamw_update(w, g, m, v):
    c1 = 1.0 / (1.0 - ADAM_B1 ** ADAM_STEP)
    c2 = 1.0 / (1.0 - ADAM_B2 ** ADAM_STEP)
    nm = ADAM_B1 * m + (1.0 - ADAM_B1) * g
    nv = ADAM_B2 * v + (1.0 - ADAM_B2) * (g * g)
    delta = -ADAM_LR * ((nm * c1) / (jnp.sqrt(nv * c2) + ADAM_EPS) + ADAM_WD * w)
    return delta, nm, nv


def _adamw(w, g, m, v, name):
    rows, cols = w.shape
    tile = _row_tile(rows, SUBLANES)

    def body(w_ref, g_ref, m_ref, v_ref, d_ref, nm_ref, nv_ref):
        d_ref[...], nm_ref[...], nv_ref[...] = _adamw_update(w_ref[...], g_ref[...], m_ref[...], v_ref[...])

    spec = pl.BlockSpec((tile, cols), lambda i: (i, 0))
    outs, _ = _call(body, name, (rows // tile,), [w, g, m, v], [spec] * 4,
                    [jax.ShapeDtypeStruct((rows, cols), F32)] * 3, [spec] * 3)
    return outs


def _sibling_sum(own, sib, name):
    _, r, cdim = own.shape
    tile = _row_tile(r, BF16_ROWS)

    def body(own_ref, sib_ref, sums_ref, mine_ref):
        mine_ref[...] = own_ref[0].astype(F32) + sib_ref[0].astype(F32)
        for f in (1, 2, 3):
            sums_ref[f - 1] = (own_ref[f].astype(F32) + sib_ref[f].astype(F32)).astype(BF16)

    outs, _ = _call(
        body, name, (r // tile,), [own, sib], [pl.BlockSpec((4, tile, cdim), lambda i: (0, i, 0))] * 2,
        [jax.ShapeDtypeStruct((3, r, cdim), BF16), jax.ShapeDtypeStruct((r, cdim), F32)],
        [pl.BlockSpec((3, tile, cdim), lambda i: (0, i, 0)), pl.BlockSpec((tile, cdim), lambda i: (i, 0))])
    return outs


def _chip_sum_adamw(mine, ici, w, m, v, name):
    r, cdim = mine.shape
    tile = _row_tile(r, BF16_ROWS)

    def body(mine_ref, ici_ref, w_ref, m_ref, v_ref, g_ref, d_ref, nm_ref, nv_ref):
        g = mine_ref[...]
        for f in range(3):
            g = g + ici_ref[f].astype(F32)
        g_ref[...] = g
        d_ref[...], nm_ref[...], nv_ref[...] = _adamw_update(w_ref[...], g, m_ref[...], v_ref[...])

    spec = pl.BlockSpec((tile, cdim), lambda i: (i, 0))
    outs, _ = _call(
        body, name, (r // tile,), [mine, ici, w, m, v],
        [spec, pl.BlockSpec((3, tile, cdim), lambda i: (0, i, 0)), spec, spec, spec],
        [jax.ShapeDtypeStruct((r, cdim), F32)] * 4, [spec] * 4)
    return outs


REF_KV_COL = D_MODEL
REF_REST_COL = D_MODEL + 2 * KV_WIDTH
IN_CHUNK = 1280
IN_PIECES = ([(0, 0, D_MODEL)]
             + [(D_MODEL + n * IN_CHUNK, REF_REST_COL + n * IN_CHUNK, IN_CHUNK) for n in range(REST_WIDTH // IN_CHUNK)]
             + [(KV_COL, REF_KV_COL, 2 * KV_WIDTH)])


def _inproj_fwd(x, vec, w_t, b_in, rider):
    t = x.shape[0]
    tm = min(TOKEN_TILE, t)

    def body(x_ref, vec_ref, w_ref, b_ref, z_ref, h_ref):
        xf = x_ref[...]
        r = lax.rsqrt(jnp.mean(xf * xf, axis=-1, keepdims=True) + EPS)
        h = (xf * r) * vec_ref[0:1, :] * (1.0 + vec_ref[1:2, :]) + vec_ref[2:3, :]
        hb = h.astype(BF16)
        h_ref[...] = hb
        for mine, ref, width in IN_PIECES:
            zc = lax.dot_general(hb, w_ref[ref:ref + width, :], NT_DIMS, preferred_element_type=F32)
            z_ref[:, mine:mine + width] = (zc + b_ref[:, ref:ref + width]).astype(BF16)

    return _call(
        body, "inproj_fwd", (t // tm,), [x, vec, w_t, b_in],
        [pl.BlockSpec((tm, D_MODEL), lambda i: (i, 0)), _full((SUBLANES, D_MODEL)),
         _full((IN_WIDTH, D_MODEL)), _full((1, IN_WIDTH))],
        [jax.ShapeDtypeStruct((t, IN_WIDTH), BF16), jax.ShapeDtypeStruct((t, D_MODEL), BF16)],
        [pl.BlockSpec((tm, IN_WIDTH), lambda i: (i, 0)), pl.BlockSpec((tm, D_MODEL), lambda i: (i, 0))],
        rider=rider)


def _window_mask(has_prev):
    qi = lax.broadcasted_iota(jnp.int32, (WINDOW, 2 * WINDOW), 0)
    kj = lax.broadcasted_iota(jnp.int32, (WINDOW, 2 * WINDOW), 1)
    off = jnp.where(has_prev, 0, 4 * WINDOW)
    in_prev = jnp.logical_and(kj < WINDOW, kj > qi + off)
    in_cur = jnp.logical_and(kj >= WINDOW, (kj - WINDOW) <= qi)
    return jnp.logical_or(in_prev, in_cur)


def _attn_fwd(z, sinks, rider):
    t = z.shape[0]
    tq = min(TOKEN_TILE, t)
    nblk = tq // WINDOW

    def body(q_ref, kv_ref, sink_ref, o_ref, lse_ref):
        i = pl.program_id(0)
        lane = lax.broadcasted_iota(jnp.int32, (WINDOW, N_Q_HEADS), 1)

        def one_block(b, carry):
            row0 = pl.multiple_of(b * WINDOW, WINDOW)
            start = i * tq + b * WINDOW
            prev = pl.multiple_of(jnp.maximum(start - WINDOW, 0), WINDOW)
            cur = pl.multiple_of(start, WINDOW)
            kvw = jnp.concatenate([kv_ref[pl.ds(prev, WINDOW), :], kv_ref[pl.ds(cur, WINDOW), :]], axis=0)
            valid = _window_mask(start > 0)
            lse_blk = jnp.zeros((WINDOW, N_Q_HEADS), F32)
            for h in range(N_Q_HEADS):
                j = h // GROUP
                k = kvw[:, j * HEAD_DIM:(j + 1) * HEAD_DIM]
                v = kvw[:, KV_WIDTH + j * HEAD_DIM:KV_WIDTH + (j + 1) * HEAD_DIM]
                qh = q_ref[pl.ds(row0, WINDOW), h * HEAD_DIM:(h + 1) * HEAD_DIM]
                s = lax.dot_general(qh, k, NT_DIMS, preferred_element_type=F32) * ATTN_SCALE
                s = jnp.where(valid, s, -jnp.inf)
                sink = sink_ref[h]
                m = jnp.maximum(jnp.max(s, axis=-1, keepdims=True), sink)
                p = jnp.exp(s - m)
                denom = jnp.sum(p, axis=-1, keepdims=True) + jnp.exp(sink - m)
                o = jnp.dot(p.astype(BF16), v, preferred_element_type=F32) / denom
                o_ref[pl.ds(row0, WINDOW), h * HEAD_DIM:(h + 1) * HEAD_DIM] = o.astype(BF16)
                lse_blk = jnp.where(lane == h, m + jnp.log(denom), lse_blk)
            lse_ref[pl.ds(row0, WINDOW), :] = lse_blk
            return carry

        lax.fori_loop(0, nblk, one_block, 0)

    return _call(
        body, "attn_fwd", (t // tq,), [z, z, sinks],
        [pl.BlockSpec((tq, D_MODEL), lambda i: (i, 0)),
         pl.BlockSpec((t, 2 * KV_WIDTH), lambda i: (0, KV_COL // (2 * KV_WIDTH))),
         pl.BlockSpec(memory_space=pltpu.SMEM)],
        [jax.ShapeDtypeStruct((t, D_MODEL), BF16), jax.ShapeDtypeStruct((t, N_Q_HEADS), F32)],
        [pl.BlockSpec((tq, D_MODEL), lambda i: (i, 0)), pl.BlockSpec((tq, N_Q_HEADS), lambda i: (i, 0))],
        rider=rider)


HALO = BF16_ROWS


def _shift_down(u, uh, k):
    row = lax.broadcasted_iota(jnp.int32, u.shape, 0)
    out = pltpu.roll(u, k, 0)
    for j in range(k):
        out = jnp.where(row == j, uh[HALO - k + j:HALO - k + j + 1, :], out)
    return out


def _shift_up(u, nxt, k):
    n = u.shape[0]
    row = lax.broadcasted_iota(jnp.int32, u.shape, 0)
    out = pltpu.roll(u, n - k, 0)
    for j in range(k):
        out = jnp.where(row == n - k + j, nxt[j:j + 1, :], out)
    return out


def _conv_inputs(cc_ref, cx_ref, hc_ref, hx_ref, first_tile):
    cc = cc_ref[...].astype(F32)
    cx = cx_ref[...].astype(F32)
    u = cc * cx
    uh = jnp.where(first_tile, 0.0, hc_ref[...].astype(F32) * hx_ref[...].astype(F32))
    return cc, cx, u, _shift_down(u, uh, 1), _shift_down(u, uh, 2)


def _z_specs(tm, order):
    per_tile = tm // HALO
    cols = [pl.BlockSpec((tm, D_MODEL), functools.partial(lambda i, j: (order(i), j), j=j)) for j in range(1, 6)]
    halos = [pl.BlockSpec((HALO, D_MODEL),
                          functools.partial(lambda i, j: (jnp.maximum(order(i) * per_tile - 1, 0), j), j=j))
             for j in (2, 3)]
    return cols + halos


def _mix_fwd(x, attn, z, vec, w_out):
    t = x.shape[0]
    tm = min(TOKEN_TILE, t)

    def body(x_ref, a_ref, cb_ref, cc_ref, cx_ref, ga_ref, gc_ref, hc_ref, hx_ref, vec_ref, w_ref,
             m_ref, x2_ref, h2_ref):
        i = pl.program_id(0)
        _, _, u, u1, u2 = _conv_inputs(cc_ref, cx_ref, hc_ref, hx_ref, i == 0)
        cv = vec_ref[4:5, :] * u2 + vec_ref[5:6, :] * u1 + vec_ref[6:7, :] * u
        conv = cb_ref[...].astype(F32) * cv
        merged = (_sigmoid(ga_ref[...].astype(F32)) * a_ref[...].astype(F32)
                  + _sigmoid(gc_ref[...].astype(F32)) * conv)
        mb = merged.astype(BF16)
        m_ref[...] = mb
        o = jnp.dot(mb, w_ref[...], preferred_element_type=F32)
        x2 = x_ref[...] + vec_ref[0:1, :] * o
        x2_ref[...] = x2
        r = lax.rsqrt(jnp.mean(x2 * x2, axis=-1, keepdims=True) + EPS)
        h2 = (x2 * r) * vec_ref[1:2, :] * (1.0 + vec_ref[2:3, :]) + vec_ref[3:4, :]
        h2_ref[...] = h2.astype(BF16)

    tok = pl.BlockSpec((tm, D_MODEL), lambda i: (i, 0))
    outs, _ = _call(
        body, "mix_fwd", (t // tm,), [x, attn, z, z, z, z, z, z, z, vec, w_out],
        [tok, tok] + _z_specs(tm, lambda i: i) + [_full((SUBLANES, D_MODEL)), _full((D_MODEL, D_MODEL))],
        [jax.ShapeDtypeStruct((t, D_MODEL), BF16), jax.ShapeDtypeStruct((t, D_MODEL), F32),
         jax.ShapeDtypeStruct((t, D_MODEL), BF16)],
        [tok, tok, tok])
    return outs


def _ffn_fwd(h2, w_t):
    t = h2.shape[0]
    tm = min(TOKEN_TILE, t)

    def body(h_ref, w_ref, gu_ref, a_ref):
        hb = h_ref[...]
        for n in range(D_FF // FF_CHUNK):
            lo, hi = n * FF_CHUNK, (n + 1) * FF_CHUNK
            g = lax.dot_general(hb, w_ref[lo:hi, :], NT_DIMS, preferred_element_type=F32)
            u = lax.dot_general(hb, w_ref[D_FF + lo:D_FF + hi, :], NT_DIMS, preferred_element_type=F32)
            gu_ref[:, lo:hi] = g.astype(BF16)
            gu_ref[:, D_FF + lo:D_FF + hi] = u.astype(BF16)
            a_ref[:, lo:hi] = (g * _sigmoid(g) * u).astype(BF16)

    outs, _ = _call(
        body, "ffn_fwd", (t // tm,), [h2, w_t],
        [pl.BlockSpec((tm, D_MODEL), lambda i: (i, 0)), _full((2 * D_FF, D_MODEL))],
        [jax.ShapeDtypeStruct((t, 2 * D_FF), BF16), jax.ShapeDtypeStruct((t, D_FF), BF16)],
        [pl.BlockSpec((tm, 2 * D_FF), lambda i: (i, 0)), pl.BlockSpec((tm, D_FF), lambda i: (i, 0))])
    return outs


def _ffn_out_loss(a, gu, x2, target, vec, w_ffn_out):
    t = a.shape[0]
    tm = min(TOKEN_TILE, t)

    def body(a_ref, gu_ref, x2_ref, t_ref, vec_ref, w_ref, dx3_ref, df_ref, dgu_ref, acc_ref):
        @pl.when(pl.program_id(0) == 0)
        def _():
            acc_ref[...] = jnp.zeros_like(acc_ref)

        ga2 = vec_ref[0:1, :]
        gf = vec_ref[1:2, :]
        f = jnp.dot(a_ref[...], w_ref[...], preferred_element_type=F32)
        x3 = x2_ref[...] + ga2 * f
        r = lax.rsqrt(jnp.mean(x3 * x3, axis=-1, keepdims=True) + EPS)
        xn = x3 * r
        err = xn * gf - t_ref[...]
        dy = err * (1.0 / D_MODEL)
        dxn = dy * gf
        dx3 = r * (dxn - xn * jnp.mean(dxn * xn, axis=-1, keepdims=True))
        dx3_ref[...] = dx3
        acc_ref[0:1, :] += jnp.sum(err * err, axis=0, keepdims=True)
        acc_ref[1:2, :] += jnp.sum(dy * xn, axis=0, keepdims=True)
        acc_ref[2:3, :] += jnp.sum(dx3 * f, axis=0, keepdims=True)
        df = (dx3 * ga2).astype(BF16)
        df_ref[...] = df
        for n in range(D_FF // FF_CHUNK):
            lo, hi = n * FF_CHUNK, (n + 1) * FF_CHUNK
            da = lax.dot_general(df, w_ref[lo:hi, :], NT_DIMS, preferred_element_type=F32)
            g = gu_ref[:, lo:hi].astype(F32)
            u = gu_ref[:, D_FF + lo:D_FF + hi].astype(F32)
            sg = _sigmoid(g)
            dgu_ref[:, lo:hi] = (da * u * (sg * (1.0 + g * (1.0 - sg)))).astype(BF16)
            dgu_ref[:, D_FF + lo:D_FF + hi] = (da * (g * sg)).astype(BF16)

    tok = pl.BlockSpec((tm, D_MODEL), lambda i: (i, 0))
    outs, _ = _call(
        body, "ffn_out_loss", (t // tm,), [a, gu, x2, target, vec, w_ffn_out],
        [pl.BlockSpec((tm, D_FF), lambda i: (i, 0)), pl.BlockSpec((tm, 2 * D_FF), lambda i: (i, 0)),
         tok, tok, _full((SUBLANES, D_MODEL)), _full((D_FF, D_MODEL))],
        [jax.ShapeDtypeStruct((t, D_MODEL), F32), jax.ShapeDtypeStruct((t, D_MODEL), BF16),
         jax.ShapeDtypeStruct((t, 2 * D_FF), BF16), jax.ShapeDtypeStruct((SUBLANES, D_MODEL), F32)],
        [tok, tok, pl.BlockSpec((tm, 2 * D_FF), lambda i: (i, 0)), _full((SUBLANES, D_MODEL))])
    return outs


def _ffn_in_bwd(dgu, x2, dx3, vec, w_t, rider):
    t = x2.shape[0]
    tm = min(TOKEN_TILE, t)

    def body(dgu_ref, x2_ref, dx3_ref, vec_ref, wf_ref, dx2_ref, acc_ref):
        @pl.when(pl.program_id(0) == 0)
        def _():
            acc_ref[...] = jnp.zeros_like(acc_ref)

        gffn = vec_ref[0:1, :]
        sc2 = vec_ref[1:2, :]
        dh2 = jnp.dot(dgu_ref[...], wf_ref[...], preferred_element_type=F32)
        x2 = x2_ref[...]
        r = lax.rsqrt(jnp.mean(x2 * x2, axis=-1, keepdims=True) + EPS)
        xn = x2 * r
        acc_ref[0:1, :] += jnp.sum(dh2, axis=0, keepdims=True)
        acc_ref[1:2, :] += jnp.sum(dh2 * xn * gffn, axis=0, keepdims=True)
        acc_ref[2:3, :] += jnp.sum(dh2 * xn * (1.0 + sc2), axis=0, keepdims=True)
        dxn = dh2 * gffn * (1.0 + sc2)
        dx2_ref[...] = dx3_ref[...] + r * (dxn - xn * jnp.mean(dxn * xn, axis=-1, keepdims=True))

    tok = pl.BlockSpec((tm, D_MODEL), lambda i: (i, 0))
    return _call(
        body, "ffn_in_bwd", (t // tm,), [dgu, x2, dx3, vec, w_t],
        [pl.BlockSpec((tm, 2 * D_FF), lambda i: (i, 0)), tok, tok, _full((SUBLANES, D_MODEL)),
         _full((2 * D_FF, D_MODEL))],
        [jax.ShapeDtypeStruct((t, D_MODEL), F32), jax.ShapeDtypeStruct((SUBLANES, D_MODEL), F32)],
        [tok, _full((SUBLANES, D_MODEL))], rider=rider)


def _mix_bwd(dx2, merged, attn, z, vec, w_out, rider):
    t = dx2.shape[0]
    tm = min(TOKEN_TILE, t)
    nt = t // tm
    rev = lambda i: nt - 1 - i

    def body(dx2_ref, m_ref, a_ref, cb_ref, cc_ref, cx_ref, ga_ref, gc_ref, hc_ref, hx_ref,
             vec_ref, wo_ref, do_ref, da_ref, dr_ref, acc_ref, carry_ref):
        i = pl.program_id(0)

        @pl.when(i == 0)
        def _():
            acc_ref[...] = jnp.zeros_like(acc_ref)
            carry_ref[...] = jnp.zeros_like(carry_ref)

        ga1 = vec_ref[0:1, :]
        w0, w1, w2 = vec_ref[1:2, :], vec_ref[2:3, :], vec_ref[3:4, :]
        dx2 = dx2_ref[...]
        o = jnp.dot(m_ref[...], wo_ref[...], preferred_element_type=F32)
        acc_ref[0:1, :] += jnp.sum(dx2 * o, axis=0, keepdims=True)
        do = (dx2 * ga1).astype(BF16)
        do_ref[...] = do
        dm = lax.dot_general(do, wo_ref[...], NT_DIMS, preferred_element_type=F32)

        cc, cx, u, u1, u2 = _conv_inputs(cc_ref, cx_ref, hc_ref, hx_ref, i == nt - 1)
        cv = w0 * u2 + w1 * u1 + w2 * u
        cb = cb_ref[...].astype(F32)
        sa = _sigmoid(ga_ref[...].astype(F32))
        sc = _sigmoid(gc_ref[...].astype(F32))
        attn = a_ref[...].astype(F32)
        da_ref[...] = (dm * sa).astype(BF16)
        dconv = dm * sc
        dr_ref[:, 3 * D_MODEL:4 * D_MODEL] = (dm * attn * sa * (1.0 - sa)).astype(BF16)
        dr_ref[:, 4 * D_MODEL:5 * D_MODEL] = (dconv * (cb * cv) * (1.0 - sc)).astype(BF16)
        dr_ref[:, 0:D_MODEL] = (dconv * cv).astype(BF16)
        dcv = dconv * cb
        acc_ref[1:2, :] += jnp.sum(dcv * u2, axis=0, keepdims=True)
        acc_ref[2:3, :] += jnp.sum(dcv * u1, axis=0, keepdims=True)
        acc_ref[3:4, :] += jnp.sum(dcv * u, axis=0, keepdims=True)
        nxt = carry_ref[...]
        du = w2 * dcv + w1 * _shift_up(dcv, nxt, 1) + w0 * _shift_up(dcv, nxt, 2)
        carry_ref[...] = dcv[0:SUBLANES, :]
        dr_ref[:, D_MODEL:2 * D_MODEL] = (du * cx).astype(BF16)
        dr_ref[:, 2 * D_MODEL:3 * D_MODEL] = (du * cc).astype(BF16)

    tok = pl.BlockSpec((tm, D_MODEL), lambda i: (rev(i), 0))
    return _call(
        body, "mix_bwd", (nt,), [dx2, merged, attn, z, z, z, z, z, z, z, vec, w_out],
        [tok, tok, tok] + _z_specs(tm, rev) + [_full((SUBLANES, D_MODEL)), _full((D_MODEL, D_MODEL))],
        [jax.ShapeDtypeStruct((t, D_MODEL), BF16), jax.ShapeDtypeStruct((t, D_MODEL), BF16),
         jax.ShapeDtypeStruct((t, REST_WIDTH), BF16), jax.ShapeDtypeStruct((SUBLANES, D_MODEL), F32)],
        [tok, tok, pl.BlockSpec((tm, REST_WIDTH), lambda i: (rev(i), 0)), _full((SUBLANES, D_MODEL))],
        scratch=[pltpu.VMEM((SUBLANES, D_MODEL), F32)], rider=rider)


def _attn_bwd(z, dattn, attn, lse, sinks, rider):
    t = z.shape[0]
    tq = min(TOKEN_TILE, t)
    nblk = tq // WINDOW
    nt = t // tq

    def body(q_ref, kv_ref, do_ref, o_ref, lse_ref, sink_ref, dq_ref, dkv_ref, ds_ref, acc_ref):
        i = pl.program_id(0)

        @pl.when(i == 0)
        def _():
            acc_ref[...] = jnp.zeros_like(acc_ref)
            ds_ref[...] = jnp.zeros_like(ds_ref)

        lane = lax.broadcasted_iota(jnp.int32, (1, LANES), 1)

        def one_block(b, dsink):
            row0 = pl.multiple_of(b * WINDOW, WINDOW)
            start = i * tq + b * WINDOW
            prev = pl.multiple_of(jnp.maximum(start - WINDOW, 0), WINDOW)
            cur = pl.multiple_of(start, WINDOW)
            kvw = jnp.concatenate([kv_ref[pl.ds(prev, WINDOW), :], kv_ref[pl.ds(cur, WINDOW), :]], axis=0)
            valid = _window_mask(start > 0)
            lse_blk = lse_ref[pl.ds(row0, WINDOW), :]
            parts = []
            for j in range(N_KV_HEADS):
                k = kvw[:, j * HEAD_DIM:(j + 1) * HEAD_DIM]
                v = kvw[:, KV_WIDTH + j * HEAD_DIM:KV_WIDTH + (j + 1) * HEAD_DIM]
                dk = jnp.zeros((2 * WINDOW, HEAD_DIM), F32)
                dv = jnp.zeros((2 * WINDOW, HEAD_DIM), F32)
                for g in range(GROUP):
                    h = j * GROUP + g
                    cols = slice(h * HEAD_DIM, (h + 1) * HEAD_DIM)
                    qh = q_ref[pl.ds(row0, WINDOW), cols]
                    doh = do_ref[pl.ds(row0, WINDOW), cols]
                    oh = o_ref[pl.ds(row0, WINDOW), cols]
                    lse_h = lse_blk[:, h:h + 1]
                    s = lax.dot_general(qh, k, NT_DIMS, preferred_element_type=F32) * ATTN_SCALE
                    p = jnp.where(valid, jnp.exp(s - lse_h), 0.0)
                    delta = jnp.sum(doh.astype(F32) * oh.astype(F32), axis=-1, keepdims=True)
                    dp = lax.dot_general(doh, v, NT_DIMS, preferred_element_type=F32)
                    dsb = (p * (dp - delta)).astype(BF16)
                    dq = jnp.dot(dsb, k, preferred_element_type=F32) * ATTN_SCALE
                    dq_ref[pl.ds(row0, WINDOW), cols] = dq.astype(BF16)
                    dk = dk + lax.dot_general(dsb, qh, TN_DIMS, preferred_element_type=F32)
                    dv = dv + lax.dot_general(p.astype(BF16), doh, TN_DIMS, preferred_element_type=F32)
                    psink = jnp.exp(sink_ref[h] - lse_h)
                    dsink = dsink - jnp.where(lane == h, jnp.sum(psink * delta), 0.0)
                parts.append((dk * ATTN_SCALE, dv))
            blk = jnp.concatenate([parts[0][0], parts[1][0], parts[0][1], parts[1][1]], axis=1)
            acc_ref[pl.ds(prev, WINDOW), :] += blk[:WINDOW, :]
            acc_ref[pl.ds(cur, WINDOW), :] += blk[WINDOW:, :]
            return dsink

        dsink = lax.fori_loop(0, nblk, one_block, jnp.zeros((1, LANES), F32))
        ds_ref[0:1, :] += dsink

        @pl.when(i == nt - 1)
        def _():
            dkv_ref[...] = acc_ref[...].astype(BF16)

    tok = pl.BlockSpec((tq, D_MODEL), lambda i: (i, 0))
    return _call(
        body, "attn_bwd", (nt,), [z, z, dattn, attn, lse, sinks],
        [tok, pl.BlockSpec((t, 2 * KV_WIDTH), lambda i: (0, KV_COL // (2 * KV_WIDTH))), tok, tok,
         pl.BlockSpec((tq, N_Q_HEADS), lambda i: (i, 0)), pl.BlockSpec(memory_space=pltpu.SMEM)],
        [jax.ShapeDtypeStruct((t, D_MODEL), BF16), jax.ShapeDtypeStruct((t, 2 * KV_WIDTH), BF16),
         jax.ShapeDtypeStruct((SUBLANES, LANES), F32)],
        [tok, _full((t, 2 * KV_WIDTH)), _full((SUBLANES, LANES))],
        scratch=[pltpu.VMEM((t, 2 * KV_WIDTH), F32)], rider=rider)


def _inproj_bwd(dq, drest, dkv, x, dx2, vec, w_t, rider):
    t = x.shape[0]
    tm = min(TOKEN_TILE, t)

    def body(dq_ref, dr_ref, dkv_ref, x_ref, dx2_ref, vec_ref, w_ref, gx_ref, acc_ref, db_ref):
        @pl.when(pl.program_id(0) == 0)
        def _():
            acc_ref[...] = jnp.zeros_like(acc_ref)
            db_ref[...] = jnp.zeros_like(db_ref)

        g = vec_ref[0:1, :]
        sc1 = vec_ref[1:2, :]
        dqb, drb, dkvb = dq_ref[...], dr_ref[...], dkv_ref[...]
        dh = jnp.dot(dqb, w_ref[:REF_KV_COL, :], preferred_element_type=F32)
        dh = dh + jnp.dot(drb, w_ref[REF_REST_COL:, :], preferred_element_type=F32)
        dh = dh + jnp.dot(dkvb, w_ref[REF_KV_COL:REF_REST_COL, :], preferred_element_type=F32)
        db_ref[:, :REF_KV_COL] += jnp.sum(dqb.astype(F32), axis=0, keepdims=True)
        db_ref[:, REF_REST_COL:] += jnp.sum(drb.astype(F32), axis=0, keepdims=True)
        db_ref[:, REF_KV_COL:REF_REST_COL] += jnp.sum(dkvb.astype(F32), axis=0, keepdims=True)
        xf = x_ref[...]
        r = lax.rsqrt(jnp.mean(xf * xf, axis=-1, keepdims=True) + EPS)
        xn = xf * r
        acc_ref[0:1, :] += jnp.sum(dh, axis=0, keepdims=True)
        acc_ref[1:2, :] += jnp.sum(dh * xn * g, axis=0, keepdims=True)
        acc_ref[2:3, :] += jnp.sum(dh * xn * (1.0 + sc1), axis=0, keepdims=True)
        dxn = dh * g * (1.0 + sc1)
        gx_ref[...] = dx2_ref[...] + r * (dxn - xn * jnp.mean(dxn * xn, axis=-1, keepdims=True))

    tok = pl.BlockSpec((tm, D_MODEL), lambda i: (i, 0))
    return _call(
        body, "inproj_bwd", (t // tm,), [dq, drest, dkv, x, dx2, vec, w_t],
        [tok, pl.BlockSpec((tm, REST_WIDTH), lambda i: (i, 0)),
         pl.BlockSpec((tm, 2 * KV_WIDTH), lambda i: (i, 0)), tok, tok,
         _full((SUBLANES, D_MODEL)), _full((IN_WIDTH, D_MODEL))],
        [jax.ShapeDtypeStruct((t, D_MODEL), F32), jax.ShapeDtypeStruct((SUBLANES, D_MODEL), F32),
         jax.ShapeDtypeStruct((1, IN_WIDTH), F32)],
        [tok, _full((SUBLANES, D_MODEL)), _full((1, IN_WIDTH))], rider=rider)


def _weight_grad(b, a, name, bn, rows=None, row0=0, into=None, rider=None):
    t, n = b.shape
    m = a.shape[1]
    rows = n if rows is None else rows
    tk = min(TOKEN_TILE, t)
    for cand in (4 * TOKEN_TILE, 2 * TOKEN_TILE):
        if t % cand == 0 and 2 * cand * (bn + m) * 2 + bn * m * 4 <= WGRAD_VMEM:
            tk = cand
            break
    nk = t // tk
    block0 = row0 // bn

    def body(b_ref, a_ref, *rest):
        out_ref, acc_ref = rest[-2:]
        k = pl.program_id(1)

        @pl.when(k == 0)
        def _():
            acc_ref[...] = jnp.zeros_like(acc_ref)

        acc_ref[...] += lax.dot_general(b_ref[...], a_ref[...], TN_DIMS, preferred_element_type=F32)

        @pl.when(k == nk - 1)
        def _():
            out_ref[...] = acc_ref[...].astype(BF16)

    outs, routs = _call(
        body, name, (n // bn, nk), [b, a] + ([] if into is None else [into]),
        [pl.BlockSpec((tk, bn), lambda j, k: (k, j)), pl.BlockSpec((tk, m), lambda j, k: (k, 0))]
        + ([] if into is None else [ANY]),
        [jax.ShapeDtypeStruct((rows, m), BF16)], [pl.BlockSpec((bn, m), lambda j, k: (block0 + j, 0))],
        scratch=[pltpu.VMEM((bn, m), F32)], rider=rider, aliases=None if into is None else {2: 0})
    return outs[0], routs


def _to_rows(v):
    n = v.shape[0]
    padded = -(-n // (SUBLANES * LANES)) * SUBLANES * LANES
    return jnp.pad(v, (0, padded - n)).reshape(padded // LANES, LANES)


def _vec_rows(*rows):
    stacked = jnp.concatenate([r.reshape(1, D_MODEL) for r in rows], axis=0)
    return jnp.pad(stacked, ((0, SUBLANES - len(rows)), (0, 0)))


def kernel(x, c, w_ada, b_ada, g_mix, w_in, b_in, sinks, conv_w, w_out, g_ffn, w_ffn_in, w_ffn_out, g_final, loss_target, m_w_ada, m_b_ada, m_g_mix, m_w_in, m_b_in, m_sinks, m_conv_w, m_w_out, m_g_ffn, m_w_ffn_in, m_w_ffn_out, m_g_final, v_w_ada, v_b_ada, v_g_mix, v_w_in, v_b_in, v_sinks, v_conv_w, v_w_out, v_g_ffn, v_w_ffn_in, v_w_ffn_out, v_g_final):
    ix, iy, ic = _my_place()
    me = 4 * ix + 2 * iy + ic
    xs = x[0]
    target = loss_target[0]
    ada_cols = w_ada.shape[2]
    conv_cols = conv_w.shape[2]

    first = _small_allgather(_to_rows(jnp.concatenate([c[0], conv_w[0].reshape(-1)])), "gather_c_conv")
    first = first.reshape(N_DEV, -1)
    c_all = first[:, :D_MODEL]
    conv_full = jnp.transpose(first[:, D_MODEL:D_MODEL + 3 * conv_cols].reshape(N_DEV, 3, conv_cols), (1, 0, 2))
    conv_full = conv_full.reshape(3, D_MODEL)
    b_cols = lax.dynamic_slice_in_dim(b_ada, me * ada_cols, ada_cols, axis=1)
    mod_part = _ada_forward(c_all, w_ada[0], b_cols)
    mod_all = _small_allgather(mod_part.reshape(-1, LANES), "gather_mod").reshape(N_DEV, N_DEV, ada_cols)
    mod = lax.dynamic_index_in_dim(mod_all, me, axis=1, keepdims=False).reshape(N_MOD, D_MODEL)
    sh1, sc1, ga1, sh2, sc2, ga2 = [mod[i:i + 1] for i in range(N_MOD)]

    wt_in, wt_fi = jnp.transpose(w_in[0]), jnp.transpose(w_ffn_in[0])
    g_in, (cast_fi, cast_out, cast_fo) = _gather_first_weight(wt_in, [wt_fi, w_out[0], w_ffn_out[0]])
    w_in_t = g_in.reshape(IN_WIDTH, D_MODEL)
    (z, h1), (g_fi, g_out) = _inproj_fwd(xs, _vec_rows(g_mix, sc1, sh1), w_in_t, b_in,
                                         _gather_rider([cast_fi, cast_out]))
    w_fi_t = g_fi.reshape(2 * D_FF, D_MODEL)
    w_out_full = g_out.reshape(D_MODEL, D_MODEL)
    (attn, lse), (g_fo,) = _attn_fwd(z, sinks[0], _gather_rider([cast_fo]))
    w_fo_full = g_fo.reshape(D_FF, D_MODEL)
    merged, x2, h2 = _mix_fwd(xs, attn, z, _vec_rows(ga1, g_ffn, sc2, sh2, conv_full[0], conv_full[1], conv_full[2]),
                              w_out_full)
    gu, act = _ffn_fwd(h2, w_fi_t)
    dx3, df, dgu, acc_l = _ffn_out_loss(act, gu, x2, target, _vec_rows(ga2, g_final), w_fo_full)

    gw_fo, _ = _weight_grad(act, df, "wgrad_ffn_out", D_FF)
    gw_fi, _ = _weight_grad(dgu, h2, "wgrad_ffn_in", D_FF)
    blocks_fo = gw_fo.reshape(N_DEV, D_FF // N_DEV, D_MODEL)
    blocks_fi = gw_fi.reshape(N_DEV, 2 * D_FF // N_DEV, D_MODEL)
    (dx2, acc_f), (sib_fo, sib_fi) = _ffn_in_bwd(dgu, x2, dx3, _vec_rows(g_ffn, sc2), w_fi_t,
                                                 _sibling_rider([blocks_fo, blocks_fi]))
    sums_fo, mine_fo = _sibling_sum(_own_blocks(blocks_fo), sib_fo, "sibling_sum_ffn_out")
    sums_fi, mine_fi = _sibling_sum(_own_blocks(blocks_fi), sib_fi, "sibling_sum_ffn_in")
    (dout, dattn, drest, acc_m), (ici_fo, ici_fi) = _mix_bwd(
        dx2, merged, attn, z, _vec_rows(ga1, conv_full[0], conv_full[1], conv_full[2]), w_out_full,
        _chip_rider([sums_fo, sums_fi]))
    gw_out, _ = _weight_grad(merged, dout, "wgrad_out", D_MODEL)
    blocks_out = gw_out.reshape(N_DEV, D_MODEL // N_DEV, D_MODEL)
    (dq, dkv, dsink), (sib_out,) = _attn_bwd(z, dattn, attn, lse, sinks[0], _sibling_rider([blocks_out]))
    sums_out, mine_out = _sibling_sum(_own_blocks(blocks_out), sib_out, "sibling_sum_out")
    gw_in, (ici_out,) = _weight_grad(drest, h1, "wgrad_in_rest", IN_CHUNK, rows=IN_WIDTH, row0=REF_REST_COL,
                                     rider=_chip_rider([sums_out]))
    gw_in, _ = _weight_grad(dq, h1, "wgrad_in_q", D_MODEL, rows=IN_WIDTH, row0=0, into=gw_in)
    gw_in, _ = _weight_grad(dkv, h1, "wgrad_in_kv", 2 * KV_WIDTH, rows=IN_WIDTH, row0=REF_KV_COL, into=gw_in)
    blocks_in = gw_in.reshape(N_DEV, IN_WIDTH // N_DEV, D_MODEL)
    (sib_in,) = _carry(_sibling_rider([blocks_in]), "sibling_w_in")
    sums_in, mine_in = _sibling_sum(_own_blocks(blocks_in), sib_in, "sibling_sum_in")
    (grad_x, acc_i, db_in), (ici_in,) = _inproj_bwd(dq, drest, dkv, xs, dx2, _vec_rows(g_mix, sc1), w_in_t,
                                                    _chip_rider([sums_in]))

    pieces = [acc_i[0], acc_i[1], acc_m[0], acc_f[0], acc_f[1], acc_l[2],
              acc_i[2], db_in[0], acc_f[2], acc_l[1],
              acc_m[1], acc_m[2], acc_m[3], dsink[0], acc_l[0]]
    offsets = [0]
    for p in pieces:
        offsets.append(offsets[-1] + p.shape[0])
    packed = _small_allgather(_to_rows(jnp.concatenate(pieces)), "gather_small")
    dmod_all = packed.reshape(N_DEV, -1)[:, :N_MOD * D_MODEL]
    total = _sum_devices(packed).reshape(-1)
    part = lambda i: total[offsets[i]:offsets[i + 1]]
    g_b_ada = total[:N_MOD * D_MODEL].reshape(1, -1)
    g_g_mix, g_b_in, g_g_ffn, g_g_final = part(6).reshape(1, -1), part(7).reshape(1, -1), part(8).reshape(1, -1), part(9)
    g_conv_full = jnp.stack([part(10), part(11), part(12)])
    g_conv = lax.dynamic_slice_in_dim(g_conv_full, me * conv_cols, conv_cols, axis=1)[None]
    g_sinks = part(13)[:N_Q_HEADS].reshape(1, -1)
    loss = (0.5 / D_MODEL) * jnp.sum(part(14))
    dmod_cols = lax.dynamic_slice_in_dim(dmod_all, me * ada_cols, ada_cols, axis=1)
    g_w_ada = _ada_weight_grad(c_all, dmod_cols)

    def reduced(mine, ici, w, m, v, name, transposed=False):
        turn = jnp.transpose if transposed else (lambda a: a)
        return tuple(turn(o)[None] for o in _chip_sum_adamw(mine, ici, turn(w[0]), turn(m[0]), turn(v[0]), name))

    d_ada, nm_ada, nv_ada = _adamw(w_ada[0], g_w_ada, m_w_ada[0], v_w_ada[0], "adamw_w_ada")
    small_names = ["b_ada", "g_mix", "b_in", "sinks", "conv_w", "g_ffn", "g_final"]
    small_w = [b_ada, g_mix, b_in, sinks, conv_w, g_ffn, g_final]
    small_m = [m_b_ada, m_g_mix, m_b_in, m_sinks, m_conv_w, m_g_ffn, m_g_final]
    small_v = [v_b_ada, v_g_mix, v_b_in, v_sinks, v_conv_w, v_g_ffn, v_g_final]
    small_g = [g_b_ada, g_g_mix, g_b_in, g_sinks, g_conv, g_g_ffn, g_g_final]
    small_g = [g.reshape(w.shape) for g, w in zip(small_g, small_w)]
    flat = lambda arrs: _to_rows(jnp.concatenate([a.reshape(-1) for a in arrs]))
    sd, snm, snv = _adamw(flat(small_w), flat(small_g), flat(small_m), flat(small_v), "adamw_small")
    sizes = [w.size for w in small_w]
    starts = [sum(sizes[:i]) for i in range(len(sizes))]
    unflat = lambda a: {n: a.reshape(-1)[s:s + z_].reshape(w.shape)
                        for n, s, z_, w in zip(small_names, starts, sizes, small_w)}
    sd, snm, snv = unflat(sd), unflat(snm), unflat(snv)
    sg = dict(zip(small_names, small_g))

    res = {
        "w_ada": (g_w_ada[None], d_ada[None], nm_ada[None], nv_ada[None]),
        "w_in": reduced(mine_in, ici_in, w_in, m_w_in, v_w_in, "adamw_w_in", transposed=True),
        "w_out": reduced(mine_out, ici_out, w_out, m_w_out, v_w_out, "adamw_w_out"),
        "w_ffn_in": reduced(mine_fi, ici_fi, w_ffn_in, m_w_ffn_in, v_w_ffn_in, "adamw_w_ffn_in", transposed=True),
        "w_ffn_out": reduced(mine_fo, ici_fo, w_ffn_out, m_w_ffn_out, v_w_ffn_out, "adamw_w_ffn_out"),
    }
    for n in small_names:
        res[n] = (sg[n], sd[n], snm[n], snv[n])
    order = ["w_ada", "b_ada", "g_mix", "w_in", "b_in", "sinks", "conv_w", "w_out", "g_ffn", "w_ffn_in", "w_ffn_out",
             "g_final"]
    outs = [loss, grad_x[None]]
    for k in range(4):
        outs += [res[n][k] for n in order]
    return tuple(outs)
```

```python
import functools
import math

import jax
import jax.numpy as jnp
from jax import lax
from jax.experimental import pallas as pl
from jax.experimental.pallas import tpu as pltpu

F32 = jnp.float32
BF16 = jnp.bfloat16

D_MODEL = 1024
HEAD_DIM = 64
N_Q_HEADS = 16
N_KV_HEADS = 2
GROUP = 8
WINDOW = 128
KV_WIDTH = N_KV_HEADS * HEAD_DIM
D_FF = 2816
IN_WIDTH = 6400
N_MOD = 6
EPS = 1e-6
N_DEV = 8
REST_WIDTH = 5 * D_MODEL
KV_COL = D_MODEL + REST_WIDTH
ATTN_SCALE = HEAD_DIM ** -0.5

ADAM_LR = 0.001
ADAM_B1 = 0.9
ADAM_B2 = 0.999
ADAM_EPS = 1e-08
ADAM_WD = 0.01
ADAM_STEP = 10

LANES = 128
SUBLANES = 8
BF16_ROWS = 16
VMEM_LIMIT = 56 * 1024 * 1024
TOKEN_TILE = 512
FF_CHUNK = 256
WGRAD_VMEM = 40 * 1024 * 1024
MESH = pl.DeviceIdType.MESH
ANY = pl.BlockSpec(memory_space=pl.ANY)

NT_DIMS = (((1,), (1,)), ((), ()))
TN_DIMS = (((0,), (0,)), ((), ()))
CHIP_FLIPS = [(0, 0), (1, 0), (0, 1), (1, 1)]


def _full(shape):
    return pl.BlockSpec(shape, lambda *_: (0,) * len(shape))


def _my_place():
    return lax.axis_index("x"), lax.axis_index("y"), lax.axis_index("c")


def _flip(v, bit):
    return 1 - v if bit else v


def _sigmoid(v):
    return 1.0 / (1.0 + jnp.exp(-v))


class _Rider:
    def __init__(self, ins, out_shapes, sem_shapes, first=None, mid=None, last=None, ins_in_vmem=False):
        self.ins, self.out_shapes, self.sem_shapes = list(ins), list(out_shapes), list(sem_shapes)
        self.in_specs = [_full(a.shape) if ins_in_vmem else ANY for a in self.ins]
        self.hooks = [(when, fn) for when, fn in (("first", first), ("mid", mid), ("last", last)) if fn is not None]


def _call(body, name, grid, args, in_specs, out_shape, out_specs, scratch=(), rider=None, aliases=None):
    n_in, n_out, n_scr = len(args), len(out_shape), len(scratch)
    r_in = rider.ins if rider else []
    r_out = rider.out_shapes if rider else []
    r_sem = rider.sem_shapes if rider else []
    nsteps = math.prod(grid)

    def full_body(*refs):
        pos = 0
        groups = []
        for size in (n_in, len(r_in), n_out, len(r_out), n_scr, len(r_sem)):
            groups.append(refs[pos:pos + size])
            pos += size
        ins, rins, outs, routs, scr, rsems = groups
        step = pl.program_id(0)
        for axis in range(1, len(grid)):
            step = step * grid[axis] + pl.program_id(axis)
        at = {"first": 0, "mid": (3 * nsteps) // 4, "last": nsteps - 1}
        hooks = rider.hooks if rider else []
        for when, fn in hooks:
            if when != "last":
                pl.when(step == at[when])(functools.partial(fn, rins, routs, rsems))
        body(*ins, *outs, *scr)
        for when, fn in hooks:
            if when == "last":
                pl.when(step == at[when])(functools.partial(fn, rins, routs, rsems))

    outs = pl.pallas_call(
        full_body, name=name, grid=grid,
        out_shape=list(out_shape) + list(r_out),
        in_specs=list(in_specs) + (rider.in_specs if rider else []),
        out_specs=list(out_specs) + [ANY] * len(r_out),
        scratch_shapes=list(scratch) + list(r_sem),
        input_output_aliases=dict(aliases or {}),
        compiler_params=pltpu.CompilerParams(dimension_semantics=("arbitrary",) * len(grid),
                                             vmem_limit_bytes=VMEM_LIMIT),
    )(*args, *r_in)
    return list(outs[:n_out]), list(outs[n_out:])


def _gather_rider(shards):
    n = len(shards)

    def setup(outs, sems):
        x, y, c = _my_place()
        send_sems, recv_sems, _ = sems
        chips = [(1 - x, y), (x, 1 - y), (1 - x, 1 - y)]

        def block(w, place):
            return outs[w].at[4 * place[0] + 2 * place[1] + place[2]]

        def copy(w, k, place, to, src=None):
            return pltpu.make_async_remote_copy(
                src_ref=block(w, place) if src is None else src, dst_ref=block(w, place),
                send_sem=send_sems.at[w, k], recv_sem=recv_sems.at[w, k], device_id=to, device_id_type=MESH)

        return (x, y, c), (x, y, 1 - c), chips, block, copy

    def first(ins, outs, sems):
        me, sibling, chips, block, copy = setup(outs, sems)
        for w in range(n):
            pltpu.make_async_copy(ins[w], block(w, me), sems[2].at[w]).start()
            copy(w, 0, me, sibling, src=ins[w]).start()
            for j, chip in enumerate(chips):
                copy(w, 1 + j, me, (*chip, me[2]), src=ins[w]).start()

    def mid(ins, outs, sems):
        me, sibling, chips, block, copy = setup(outs, sems)
        for w in range(n):
            for j, chip in enumerate(chips):
                copy(w, 1 + j, (*chip, me[2]), me).wait_recv()
                copy(w, 4 + j, (*chip, me[2]), sibling).start()

    def last(ins, outs, sems):
        me, sibling, chips, block, copy = setup(outs, sems)
        for w in range(n):
            copy(w, 0, sibling, me).wait_recv()
            for j, chip in enumerate(chips):
                copy(w, 4 + j, (*chip, 1 - me[2]), me).wait_recv()
            copy(w, 0, me, sibling, src=ins[w]).wait_send()
            for j, chip in enumerate(chips):
                copy(w, 1 + j, me, (*chip, me[2]), src=ins[w]).wait_send()
                copy(w, 4 + j, (*chip, me[2]), sibling).wait_send()
            pltpu.make_async_copy(ins[w], block(w, me), sems[2].at[w]).wait()

    return _Rider(
        shards, [jax.ShapeDtypeStruct((N_DEV,) + s.shape, BF16) for s in shards],
        [pltpu.SemaphoreType.DMA((n, N_DEV - 1)), pltpu.SemaphoreType.DMA((n, N_DEV - 1)),
         pltpu.SemaphoreType.DMA((n,))],
        first=first, mid=mid, last=last, ins_in_vmem=True)


def _sibling_rider(gblocks):
    n = len(gblocks)

    def copies(ins, outs, sems):
        x, y, c = _my_place()
        send_sems, recv_sems = sems
        made = []
        for w in range(n):
            for f, (fx, fy) in enumerate(CHIP_FLIPS):
                chip = 4 * _flip(x, fx) + 2 * _flip(y, fy)
                made.append(pltpu.make_async_remote_copy(
                    src_ref=ins[w].at[chip + 1 - c], dst_ref=outs[w].at[f], send_sem=send_sems.at[w, f],
                    recv_sem=recv_sems.at[w, f], device_id=(x, y, 1 - c), device_id_type=MESH))
        return made

    def first(ins, outs, sems):
        for cp in copies(ins, outs, sems):
            cp.start()

    def last(ins, outs, sems):
        for cp in copies(ins, outs, sems):
            cp.wait_recv()
            cp.wait_send()

    return _Rider(gblocks, [jax.ShapeDtypeStruct((4,) + g.shape[1:], BF16) for g in gblocks],
                  [pltpu.SemaphoreType.DMA((n, 4))] * 2, first=first, last=last)


def _own_blocks(gblocks):
    x, y, c = _my_place()
    return jnp.stack([lax.dynamic_index_in_dim(gblocks, 4 * _flip(x, fx) + 2 * _flip(y, fy) + c, 0, keepdims=False)
                      for fx, fy in CHIP_FLIPS])


def _chip_rider(sums):
    n = len(sums)

    def copies(ins, outs, sems):
        x, y, c = _my_place()
        send_sems, recv_sems = sems
        made = []
        for w in range(n):
            for f in (1, 2, 3):
                fx, fy = CHIP_FLIPS[f]
                made.append(pltpu.make_async_remote_copy(
                    src_ref=ins[w].at[f - 1], dst_ref=outs[w].at[f - 1], send_sem=send_sems.at[w, f - 1],
                    recv_sem=recv_sems.at[w, f - 1], device_id=(_flip(x, fx), _flip(y, fy), c), device_id_type=MESH))
        return made

    def first(ins, outs, sems):
        for cp in copies(ins, outs, sems):
            cp.start()

    def last(ins, outs, sems):
        for cp in copies(ins, outs, sems):
            cp.wait_recv()
            cp.wait_send()

    return _Rider(sums, [jax.ShapeDtypeStruct(s.shape, BF16) for s in sums],
                  [pltpu.SemaphoreType.DMA((n, 3))] * 2, first=first, last=last)


def _small_allgather(v, name):
    rows = v.shape[0]

    def body(v_ref, out_ref, send_sems, recv_sems, local_sem):
        x, y, c = _my_place()
        me = 4 * x + 2 * y + c
        mine = pltpu.make_async_copy(v_ref, out_ref.at[me], local_sem)
        mine.start()
        sends = []
        for k in range(1, N_DEV):
            px, py, pc = _flip(x, k & 4), _flip(y, k & 2), _flip(c, k & 1)
            cp = pltpu.make_async_remote_copy(
                src_ref=v_ref, dst_ref=out_ref.at[me], send_sem=send_sems.at[k - 1], recv_sem=recv_sems.at[k - 1],
                device_id=(px, py, pc), device_id_type=MESH)
            cp.start()
            sends.append(cp)
        for k in range(1, N_DEV):
            px, py, pc = _flip(x, k & 4), _flip(y, k & 2), _flip(c, k & 1)
            pltpu.make_async_remote_copy(
                src_ref=v_ref, dst_ref=out_ref.at[4 * px + 2 * py + pc], send_sem=send_sems.at[k - 1],
                recv_sem=recv_sems.at[k - 1], device_id=(px, py, pc), device_id_type=MESH).wait_recv()
        for cp in sends:
            cp.wait_send()
        mine.wait()

    return pl.pallas_call(
        body, name=name,
        out_shape=jax.ShapeDtypeStruct((N_DEV, rows, LANES), F32),
        in_specs=[pl.BlockSpec(memory_space=pltpu.VMEM)],
        out_specs=pl.BlockSpec(memory_space=pltpu.VMEM),
        scratch_shapes=[pltpu.SemaphoreType.DMA((N_DEV - 1,)), pltpu.SemaphoreType.DMA((N_DEV - 1,)),
                        pltpu.SemaphoreType.DMA],
        compiler_params=pltpu.CompilerParams(vmem_limit_bytes=VMEM_LIMIT),
    )(v)


def _gather_first_weight(shard, others):
    n = len(others)

    def body(*refs):
        w_ref, other_refs = refs[0], refs[1:1 + n]
        out_ref, cast_refs = refs[1 + n], refs[2 + n:2 + 2 * n]
        mine_ref, send_sems, recv_sems, local_sem = refs[2 + 2 * n:]
        x, y, c = _my_place()
        me, sibling = (x, y, c), (x, y, 1 - c)
        chips = [(1 - x, y), (x, 1 - y), (1 - x, 1 - y)]

        def block(place):
            return out_ref.at[4 * place[0] + 2 * place[1] + place[2]]

        def copy(k, place, to, src=None):
            return pltpu.make_async_remote_copy(
                src_ref=block(place) if src is None else src, dst_ref=block(place),
                send_sem=send_sems.at[k], recv_sem=recv_sems.at[k], device_id=to, device_id_type=MESH)

        mine_ref[...] = w_ref[...].astype(BF16)
        local = pltpu.make_async_copy(mine_ref, block(me), local_sem)
        local.start()
        started = [copy(0, me, sibling, src=mine_ref)]
        started += [copy(1 + j, me, (*chip, c), src=mine_ref) for j, chip in enumerate(chips)]
        for cp in started:
            cp.start()
        for o_ref, c_ref in zip(other_refs, cast_refs):
            c_ref[...] = o_ref[...].astype(BF16)
        for j, chip in enumerate(chips):
            copy(1 + j, (*chip, c), me).wait_recv()
            passed = copy(4 + j, (*chip, c), sibling)
            passed.start()
            started.append(passed)
        copy(0, sibling, me).wait_recv()
        for j, chip in enumerate(chips):
            copy(4 + j, (*chip, 1 - c), me).wait_recv()
        for cp in started:
            cp.wait_send()
        local.wait()

    vmem = pl.BlockSpec(memory_space=pltpu.VMEM)
    outs = pl.pallas_call(
        body, name="gather_w_in",
        out_shape=[jax.ShapeDtypeStruct((N_DEV,) + shard.shape, BF16)]
        + [jax.ShapeDtypeStruct(o.shape, BF16) for o in others],
        in_specs=[vmem] * (1 + n),
        out_specs=[ANY] + [vmem] * n,
        scratch_shapes=[pltpu.VMEM(shard.shape, BF16), pltpu.SemaphoreType.DMA((N_DEV - 1,)),
                        pltpu.SemaphoreType.DMA((N_DEV - 1,)), pltpu.SemaphoreType.DMA],
        compiler_params=pltpu.CompilerParams(vmem_limit_bytes=VMEM_LIMIT),
    )(shard, *others)
    return outs[0], list(outs[1:])


def _carry(rider, name):
    def body(token_ref):
        token_ref[...] = jnp.zeros_like(token_ref)

    _, routs = _call(body, name, (1,), [], [], [jax.ShapeDtypeStruct((SUBLANES, LANES), F32)],
                     [_full((SUBLANES, LANES))], rider=rider)
    return routs


def _ada_forward(c_all, w_ada, b_cols):
    cols = w_ada.shape[1]

    def body(c_ref, w_ref, b_ref, out_ref):
        cf = c_ref[...]
        act = (cf * _sigmoid(cf)).astype(BF16)
        out_ref[...] = jnp.dot(act, w_ref[...].astype(BF16), preferred_element_type=F32) + b_ref[...]

    return pl.pallas_call(
        body, name="ada_forward",
        out_shape=jax.ShapeDtypeStruct((N_DEV, cols), F32),
        in_specs=[pl.BlockSpec(memory_space=pltpu.VMEM)] * 3,
        out_specs=pl.BlockSpec(memory_space=pltpu.VMEM),
        compiler_params=pltpu.CompilerParams(vmem_limit_bytes=VMEM_LIMIT),
    )(c_all, w_ada, b_cols)


def _ada_weight_grad(c_all, dmod_cols):
    cols = dmod_cols.shape[1]

    def body(c_ref, d_ref, out_ref):
        cf = c_ref[...]
        act = (cf * _sigmoid(cf)).astype(BF16)
        out_ref[...] = lax.dot_general(act, d_ref[...].astype(BF16), TN_DIMS, preferred_element_type=F32)

    return pl.pallas_call(
        body, name="ada_weight_grad",
        out_shape=jax.ShapeDtypeStruct((D_MODEL, cols), F32),
        in_specs=[pl.BlockSpec(memory_space=pltpu.VMEM)] * 2,
        out_specs=pl.BlockSpec(memory_space=pltpu.VMEM),
        compiler_params=pltpu.CompilerParams(vmem_limit_bytes=VMEM_LIMIT),
    )(c_all, dmod_cols)


def _sum_devices(packed):
    def body(p_ref, out_ref):
        total = p_ref[0]
        for d in range(1, N_DEV):
            total = total + p_ref[d]
        out_ref[...] = total

    return pl.pallas_call(
        body, name="sum_devices",
        out_shape=jax.ShapeDtypeStruct(packed.shape[1:], F32),
        in_specs=[pl.BlockSpec(memory_space=pltpu.VMEM)],
        out_specs=pl.BlockSpec(memory_space=pltpu.VMEM),
        compiler_params=pltpu.CompilerParams(vmem_limit_bytes=VMEM_LIMIT),
    )(packed)


def _row_tile(rows, multiple):
    for cand in range(min(rows, 256), 0, -1):
        if rows % cand == 0 and cand % multiple == 0:
            return cand
    return rows


def _adamw_update(w, g, m, v):
    c1 = 1.0 / (1.0 - ADAM_B1 ** ADAM_STEP)
    c2 = 1.0 / (1.0 - ADAM_B2 ** ADAM_STEP)
    nm = ADAM_B1 * m + (1.0 - ADAM_B1) * g
    nv = ADAM_B2 * v + (1.0 - ADAM_B2) * (g * g)
    delta = -ADAM_LR * ((nm * c1) / (jnp.sqrt(nv * c2) + ADAM_EPS) + ADAM_WD * w)
    return delta, nm, nv


def _adamw(w, g, m, v, name):
    rows, cols = w.shape
    tile = _row_tile(rows, SUBLANES)

    def body(w_ref, g_ref, m_ref, v_ref, d_ref, nm_ref, nv_ref):
        d_ref[...], nm_ref[...], nv_ref[...] = _adamw_update(w_ref[...], g_ref[...], m_ref[...], v_ref[...])

    spec = pl.BlockSpec((tile, cols), lambda i: (i, 0))
    outs, _ = _call(body, name, (rows // tile,), [w, g, m, v], [spec] * 4,
                    [jax.ShapeDtypeStruct((rows, cols), F32)] * 3, [spec] * 3)
    return outs


def _sibling_sum(own, sib, name):
    _, r, cdim = own.shape
    tile = _row_tile(r, BF16_ROWS)

    def body(own_ref, sib_ref, sums_ref, mine_ref):
        mine_ref[...] = own_ref[0].astype(F32) + sib_ref[0].astype(F32)
        for f in (1, 2, 3):
            sums_ref[f - 1] = (own_ref[f].astype(F32) + sib_ref[f].astype(F32)).astype(BF16)

    outs, _ = _call(
        body, name, (r // tile,), [own, sib], [pl.BlockSpec((4, tile, cdim), lambda i: (0, i, 0))] * 2,
        [jax.ShapeDtypeStruct((3, r, cdim), BF16), jax.ShapeDtypeStruct((r, cdim), F32)],
        [pl.BlockSpec((3, tile, cdim), lambda i: (0, i, 0)), pl.BlockSpec((tile, cdim), lambda i: (i, 0))])
    return outs


def _chip_sum_adamw(mine, ici, w, m, v, name):
    r, cdim = mine.shape
    tile = _row_tile(r, BF16_ROWS)

    def body(mine_ref, ici_ref, w_ref, m_ref, v_ref, g_ref, d_ref, nm_ref, nv_ref):
        g = mine_ref[...]
        for f in range(3):
            g = g + ici_ref[f].astype(F32)
        g_ref[...] = g
        d_ref[...], nm_ref[...], nv_ref[...] = _adamw_update(w_ref[...], g, m_ref[...], v_ref[...])

    spec = pl.BlockSpec((tile, cdim), lambda i: (i, 0))
    outs, _ = _call(
        body, name, (r // tile,), [mine, ici, w, m, v],
        [spec, pl.BlockSpec((3, tile, cdim), lambda i: (0, i, 0)), spec, spec, spec],
        [jax.ShapeDtypeStruct((r, cdim), F32)] * 4, [spec] * 4)
    return outs


REF_KV_COL = D_MODEL
REF_REST_COL = D_MODEL + 2 * KV_WIDTH
IN_CHUNK = 1280
IN_PIECES = ([(0, 0, D_MODEL)]
             + [(D_MODEL + n * IN_CHUNK, REF_REST_COL + n * IN_CHUNK, IN_CHUNK) for n in range(REST_WIDTH // IN_CHUNK)]
             + [(KV_COL, REF_KV_COL, 2 * KV_WIDTH)])


def _inproj_fwd(x, vec, w_t, b_in, rider):
    t = x.shape[0]
    tm = min(TOKEN_TILE, t)

    def body(x_ref, vec_ref, w_ref, b_ref, z_ref, h_ref):
        xf = x_ref[...]
        r = lax.rsqrt(jnp.mean(xf * xf, axis=-1, keepdims=True) + EPS)
        h = (xf * r) * vec_ref[0:1, :] * (1.0 + vec_ref[1:2, :]) + vec_ref[2:3, :]
        hb = h.astype(BF16)
        h_ref[...] = hb
        for mine, ref, width in IN_PIECES:
            zc = lax.dot_general(hb, w_ref[ref:ref + width, :], NT_DIMS, preferred_element_type=F32)
            z_ref[:, mine:mine + width] = (zc + b_ref[:, ref:ref + width]).astype(BF16)

    return _call(
        body, "inproj_fwd", (t // tm,), [x, vec, w_t, b_in],
        [pl.BlockSpec((tm, D_MODEL), lambda i: (i, 0)), _full((SUBLANES, D_MODEL)),
         _full((IN_WIDTH, D_MODEL)), _full((1, IN_WIDTH))],
        [jax.ShapeDtypeStruct((t, IN_WIDTH), BF16), jax.ShapeDtypeStruct((t, D_MODEL), BF16)],
        [pl.BlockSpec((tm, IN_WIDTH), lambda i: (i, 0)), pl.BlockSpec((tm, D_MODEL), lambda i: (i, 0))],
        rider=rider)


def _window_mask(has_prev):
    qi = lax.broadcasted_iota(jnp.int32, (WINDOW, 2 * WINDOW), 0)
    kj = lax.broadcasted_iota(jnp.int32, (WINDOW, 2 * WINDOW), 1)
    off = jnp.where(has_prev, 0, 4 * WINDOW)
    in_prev = jnp.logical_and(kj < WINDOW, kj > qi + off)
    in_cur = jnp.logical_and(kj >= WINDOW, (kj - WINDOW) <= qi)
    return jnp.logical_or(in_prev, in_cur)


PAIRS = GROUP // 2
STACK = PAIRS * WINDOW


LOG2E = 1.4426950408889634
LN2 = 0.6931471805599453
SCORE_SCALE = ATTN_SCALE * LOG2E


def _stacked_mask(has_prev, width=STACK):
    kj = lax.broadcasted_iota(jnp.int32, (2 * WINDOW, width), 0)
    qi = jnp.bitwise_and(lax.broadcasted_iota(jnp.int32, (2 * WINDOW, width), 1), WINDOW - 1)
    off = jnp.where(has_prev, 0, 4 * WINDOW)
    in_prev = jnp.logical_and(kj < WINDOW, kj > qi + off)
    in_cur = jnp.logical_and(kj >= WINDOW, (kj - WINDOW) <= qi)
    return jnp.logical_or(in_prev, in_cur)


def _half_tiles(tile):
    low = lax.broadcasted_iota(jnp.int32, tile.shape, 1) < HEAD_DIM
    swapped = jnp.concatenate([tile[:, HEAD_DIM:], tile[:, :HEAD_DIM]], axis=1)
    zero = jnp.zeros_like(tile)
    return ((jnp.where(low, tile, zero), jnp.where(low, zero, swapped)),
            (jnp.where(low, swapped, zero), jnp.where(low, zero, tile)))


def _stack_pairs(ref, row0, j):
    return jnp.concatenate(
        [ref[pl.ds(row0, WINDOW), (j * PAIRS + p) * LANES:(j * PAIRS + p + 1) * LANES] for p in range(PAIRS)], axis=0)


def _per_pair_row(values):
    pair = lax.broadcasted_iota(jnp.int32, (1, STACK), 1) // WINDOW
    row = jnp.full((1, STACK), values[PAIRS - 1], F32)
    for p in range(PAIRS - 2, -1, -1):
        row = jnp.where(pair == p, values[p], row)
    return row


def _attn_fwd(z, sinks, rider):
    t = z.shape[0]
    tq = min(TOKEN_TILE, t)
    nblk = tq // WINDOW

    def body(q_ref, kv_ref, sink_ref, o_ref, lse_ref):
        i = pl.program_id(0)

        def one_block(b, carry):
            row0 = pl.multiple_of(b * WINDOW, WINDOW)
            start = i * tq + b * WINDOW
            prev = pl.multiple_of(jnp.maximum(start - WINDOW, 0), WINDOW)
            cur = pl.multiple_of(start, WINDOW)
            kvw = jnp.concatenate([kv_ref[pl.ds(prev, WINDOW), :], kv_ref[pl.ds(cur, WINDOW), :]], axis=0)
            k_halves = _half_tiles(kvw[:, :KV_WIDTH])
            v_halves = _half_tiles(kvw[:, KV_WIDTH:])
            valid = _stacked_mask(start > 0, WINDOW)
            for j in range(N_KV_HEADS):
                for pr in range(PAIRS):
                    cols = slice((j * PAIRS + pr) * LANES, (j * PAIRS + pr + 1) * LANES)
                    qp = q_ref[pl.ds(row0, WINDOW), cols]
                    o_t = jnp.zeros((LANES, WINDOW), F32)
                    for parity in range(2):
                        h = j * GROUP + 2 * pr + parity
                        s = lax.dot_general(k_halves[j][parity], qp, NT_DIMS, preferred_element_type=F32)
                        s = jnp.where(valid, s * SCORE_SCALE, -jnp.inf)
                        sink = sink_ref[h] * LOG2E
                        m = jnp.maximum(jnp.max(s, axis=0, keepdims=True), sink)
                        p = jnp.exp2(s - m)
                        denom = jnp.sum(p, axis=0, keepdims=True) + jnp.exp2(sink - m)
                        pv = lax.dot_general(v_halves[j][parity], p.astype(BF16), TN_DIMS,
                                             preferred_element_type=F32)
                        o_t = o_t + pv * (1.0 / denom)
                        lse_ref[h:h + 1, pl.ds(row0, WINDOW)] = m + jnp.log2(denom)
                    o_ref[pl.ds(row0, WINDOW), cols] = jnp.transpose(o_t.astype(BF16))
            return carry

        lax.fori_loop(0, nblk, one_block, 0)

    return _call(
        body, "attn_fwd", (t // tq,), [z, z, sinks],
        [pl.BlockSpec((tq, D_MODEL), lambda i: (i, 0)),
         pl.BlockSpec((t, 2 * KV_WIDTH), lambda i: (0, KV_COL // (2 * KV_WIDTH))),
         pl.BlockSpec(memory_space=pltpu.SMEM)],
        [jax.ShapeDtypeStruct((t, D_MODEL), BF16), jax.ShapeDtypeStruct((N_Q_HEADS, t), F32)],
        [pl.BlockSpec((tq, D_MODEL), lambda i: (i, 0)), pl.BlockSpec((N_Q_HEADS, tq), lambda i: (0, i))],
        rider=rider)


HALO = BF16_ROWS


def _shift_down(u, uh, k):
    row = lax.broadcasted_iota(jnp.int32, u.shape, 0)
    out = pltpu.roll(u, k, 0)
    for j in range(k):
        out = jnp.where(row == j, uh[HALO - k + j:HALO - k + j + 1, :], out)
    return out


def _shift_up(u, nxt, k):
    n = u.shape[0]
    row = lax.broadcasted_iota(jnp.int32, u.shape, 0)
    out = pltpu.roll(u, n - k, 0)
    for j in range(k):
        out = jnp.where(row == n - k + j, nxt[j:j + 1, :], out)
    return out


def _conv_inputs(cc_ref, cx_ref, hc_ref, hx_ref, first_tile):
    cc = cc_ref[...].astype(F32)
    cx = cx_ref[...].astype(F32)
    u = cc * cx
    uh = jnp.where(first_tile, 0.0, hc_ref[...].astype(F32) * hx_ref[...].astype(F32))
    return cc, cx, u, _shift_down(u, uh, 1), _shift_down(u, uh, 2)


def _z_specs(tm, order):
    per_tile = tm // HALO
    cols = [pl.BlockSpec((tm, D_MODEL), functools.partial(lambda i, j: (order(i), j), j=j)) for j in range(1, 6)]
    halos = [pl.BlockSpec((HALO, D_MODEL),
                          functools.partial(lambda i, j: (jnp.maximum(order(i) * per_tile - 1, 0), j), j=j))
             for j in (2, 3)]
    return cols + halos


def _mix_fwd(x, attn, z, vec, w_out):
    t = x.shape[0]
    tm = min(TOKEN_TILE, t)

    def body(x_ref, a_ref, cb_ref, cc_ref, cx_ref, ga_ref, gc_ref, hc_ref, hx_ref, vec_ref, w_ref,
             m_ref, x2_ref, h2_ref):
        i = pl.program_id(0)
        _, _, u, u1, u2 = _conv_inputs(cc_ref, cx_ref, hc_ref, hx_ref, i == 0)
        cv = vec_ref[4:5, :] * u2 + vec_ref[5:6, :] * u1 + vec_ref[6:7, :] * u
        conv = cb_ref[...].astype(F32) * cv
        merged = (_sigmoid(ga_ref[...].astype(F32)) * a_ref[...].astype(F32)
                  + _sigmoid(gc_ref[...].astype(F32)) * conv)
        mb = merged.astype(BF16)
        m_ref[...] = mb
        o = jnp.dot(mb, w_ref[...], preferred_element_type=F32)
        x2 = x_ref[...] + vec_ref[0:1, :] * o
        x2_ref[...] = x2
        r = lax.rsqrt(jnp.mean(x2 * x2, axis=-1, keepdims=True) + EPS)
        h2 = (x2 * r) * vec_ref[1:2, :] * (1.0 + vec_ref[2:3, :]) + vec_ref[3:4, :]
        h2_ref[...] = h2.astype(BF16)

    tok = pl.BlockSpec((tm, D_MODEL), lambda i: (i, 0))
    outs, _ = _call(
        body, "mix_fwd", (t // tm,), [x, attn, z, z, z, z, z, z, z, vec, w_out],
        [tok, tok] + _z_specs(tm, lambda i: i) + [_full((SUBLANES, D_MODEL)), _full((D_MODEL, D_MODEL))],
        [jax.ShapeDtypeStruct((t, D_MODEL), BF16), jax.ShapeDtypeStruct((t, D_MODEL), F32),
         jax.ShapeDtypeStruct((t, D_MODEL), BF16)],
        [tok, tok, tok])
    return outs


def _ffn_fwd(h2, w_t):
    t = h2.shape[0]
    tm = min(TOKEN_TILE, t)

    def body(h_ref, w_ref, gu_ref, a_ref):
        hb = h_ref[...]
        for n in range(D_FF // FF_CHUNK):
            lo, hi = n * FF_CHUNK, (n + 1) * FF_CHUNK
            g = lax.dot_general(hb, w_ref[lo:hi, :], NT_DIMS, preferred_element_type=F32)
            u = lax.dot_general(hb, w_ref[D_FF + lo:D_FF + hi, :], NT_DIMS, preferred_element_type=F32)
            gu_ref[:, lo:hi] = g.astype(BF16)
            gu_ref[:, D_FF + lo:D_FF + hi] = u.astype(BF16)
            a_ref[:, lo:hi] = (g * _sigmoid(g) * u).astype(BF16)

    outs, _ = _call(
        body, "ffn_fwd", (t // tm,), [h2, w_t],
        [pl.BlockSpec((tm, D_MODEL), lambda i: (i, 0)), _full((2 * D_FF, D_MODEL))],
        [jax.ShapeDtypeStruct((t, 2 * D_FF), BF16), jax.ShapeDtypeStruct((t, D_FF), BF16)],
        [pl.BlockSpec((tm, 2 * D_FF), lambda i: (i, 0)), pl.BlockSpec((tm, D_FF), lambda i: (i, 0))])
    return outs


def _ffn_out_loss(a, gu, x2, target, vec, w_ffn_out):
    t = a.shape[0]
    tm = min(TOKEN_TILE, t)

    def body(a_ref, gu_ref, x2_ref, t_ref, vec_ref, w_ref, dx3_ref, df_ref, dgu_ref, acc_ref):
        @pl.when(pl.program_id(0) == 0)
        def _():
            acc_ref[...] = jnp.zeros_like(acc_ref)

        ga2 = vec_ref[0:1, :]
        gf = vec_ref[1:2, :]
        f = jnp.dot(a_ref[...], w_ref[...], preferred_element_type=F32)
        x3 = x2_ref[...] + ga2 * f
        r = lax.rsqrt(jnp.mean(x3 * x3, axis=-1, keepdims=True) + EPS)
        xn = x3 * r
        err = xn * gf - t_ref[...]
        dy = err * (1.0 / D_MODEL)
        dxn = dy * gf
        dx3 = r * (dxn - xn * jnp.mean(dxn * xn, axis=-1, keepdims=True))
        dx3_ref[...] = dx3
        acc_ref[0:1, :] += jnp.sum(err * err, axis=0, keepdims=True)
        acc_ref[1:2, :] += jnp.sum(dy * xn, axis=0, keepdims=True)
        acc_ref[2:3, :] += jnp.sum(dx3 * f, axis=0, keepdims=True)
        df = (dx3 * ga2).astype(BF16)
        df_ref[...] = df
        for n in range(D_FF // FF_CHUNK):
            lo, hi = n * FF_CHUNK, (n + 1) * FF_CHUNK
            da = lax.dot_general(df, w_ref[lo:hi, :], NT_DIMS, preferred_element_type=F32)
            g = gu_ref[:, lo:hi].astype(F32)
            u = gu_ref[:, D_FF + lo:D_FF + hi].astype(F32)
            sg = _sigmoid(g)
            dgu_ref[:, lo:hi] = (da * u * (sg * (1.0 + g * (1.0 - sg)))).astype(BF16)
            dgu_ref[:, D_FF + lo:D_FF + hi] = (da * (g * sg)).astype(BF16)

    tok = pl.BlockSpec((tm, D_MODEL), lambda i: (i, 0))
    outs, _ = _call(
        body, "ffn_out_loss", (t // tm,), [a, gu, x2, target, vec, w_ffn_out],
        [pl.BlockSpec((tm, D_FF), lambda i: (i, 0)), pl.BlockSpec((tm, 2 * D_FF), lambda i: (i, 0)),
         tok, tok, _full((SUBLANES, D_MODEL)), _full((D_FF, D_MODEL))],
        [jax.ShapeDtypeStruct((t, D_MODEL), F32), jax.ShapeDtypeStruct((t, D_MODEL), BF16),
         jax.ShapeDtypeStruct((t, 2 * D_FF), BF16), jax.ShapeDtypeStruct((SUBLANES, D_MODEL), F32)],
        [tok, tok, pl.BlockSpec((tm, 2 * D_FF), lambda i: (i, 0)), _full((SUBLANES, D_MODEL))])
    return outs


def _ffn_in_bwd(dgu, x2, dx3, vec, w_t, rider):
    t = x2.shape[0]
    tm = min(TOKEN_TILE, t)

    def body(dgu_ref, x2_ref, dx3_ref, vec_ref, wf_ref, dx2_ref, acc_ref):
        @pl.when(pl.program_id(0) == 0)
        def _():
            acc_ref[...] = jnp.zeros_like(acc_ref)

        gffn = vec_ref[0:1, :]
        sc2 = vec_ref[1:2, :]
        dh2 = jnp.dot(dgu_ref[...], wf_ref[...], preferred_element_type=F32)
        x2 = x2_ref[...]
        r = lax.rsqrt(jnp.mean(x2 * x2, axis=-1, keepdims=True) + EPS)
        xn = x2 * r
        acc_ref[0:1, :] += jnp.sum(dh2, axis=0, keepdims=True)
        acc_ref[1:2, :] += jnp.sum(dh2 * xn * gffn, axis=0, keepdims=True)
        acc_ref[2:3, :] += jnp.sum(dh2 * xn * (1.0 + sc2), axis=0, keepdims=True)
        dxn = dh2 * gffn * (1.0 + sc2)
        dx2_ref[...] = dx3_ref[...] + r * (dxn - xn * jnp.mean(dxn * xn, axis=-1, keepdims=True))

    tok = pl.BlockSpec((tm, D_MODEL), lambda i: (i, 0))
    return _call(
        body, "ffn_in_bwd", (t // tm,), [dgu, x2, dx3, vec, w_t],
        [pl.BlockSpec((tm, 2 * D_FF), lambda i: (i, 0)), tok, tok, _full((SUBLANES, D_MODEL)),
         _full((2 * D_FF, D_MODEL))],
        [jax.ShapeDtypeStruct((t, D_MODEL), F32), jax.ShapeDtypeStruct((SUBLANES, D_MODEL), F32)],
        [tok, _full((SUBLANES, D_MODEL))], rider=rider)


def _mix_bwd(dx2, merged, attn, z, vec, w_out, rider):
    t = dx2.shape[0]
    tm = min(TOKEN_TILE, t)
    nt = t // tm
    rev = lambda i: nt - 1 - i

    def body(dx2_ref, m_ref, a_ref, cb_ref, cc_ref, cx_ref, ga_ref, gc_ref, hc_ref, hx_ref,
             vec_ref, wo_ref, do_ref, da_ref, dr_ref, acc_ref, carry_ref):
        i = pl.program_id(0)

        @pl.when(i == 0)
        def _():
            acc_ref[...] = jnp.zeros_like(acc_ref)
            carry_ref[...] = jnp.zeros_like(carry_ref)

        ga1 = vec_ref[0:1, :]
        w0, w1, w2 = vec_ref[1:2, :], vec_ref[2:3, :], vec_ref[3:4, :]
        dx2 = dx2_ref[...]
        o = jnp.dot(m_ref[...], wo_ref[...], preferred_element_type=F32)
        acc_ref[0:1, :] += jnp.sum(dx2 * o, axis=0, keepdims=True)
        do = (dx2 * ga1).astype(BF16)
        do_ref[...] = do
        dm = lax.dot_general(do, wo_ref[...], NT_DIMS, preferred_element_type=F32)

        cc, cx, u, u1, u2 = _conv_inputs(cc_ref, cx_ref, hc_ref, hx_ref, i == nt - 1)
        cv = w0 * u2 + w1 * u1 + w2 * u
        cb = cb_ref[...].astype(F32)
        sa = _sigmoid(ga_ref[...].astype(F32))
        sc = _sigmoid(gc_ref[...].astype(F32))
        attn = a_ref[...].astype(F32)
        da_ref[...] = (dm * sa).astype(BF16)
        dconv = dm * sc
        dr_ref[:, 3 * D_MODEL:4 * D_MODEL] = (dm * attn * sa * (1.0 - sa)).astype(BF16)
        dr_ref[:, 4 * D_MODEL:5 * D_MODEL] = (dconv * (cb * cv) * (1.0 - sc)).astype(BF16)
        dr_ref[:, 0:D_MODEL] = (dconv * cv).astype(BF16)
        dcv = dconv * cb
        acc_ref[1:2, :] += jnp.sum(dcv * u2, axis=0, keepdims=True)
        acc_ref[2:3, :] += jnp.sum(dcv * u1, axis=0, keepdims=True)
        acc_ref[3:4, :] += jnp.sum(dcv * u, axis=0, keepdims=True)
        nxt = carry_ref[...]
        du = w2 * dcv + w1 * _shift_up(dcv, nxt, 1) + w0 * _shift_up(dcv, nxt, 2)
        carry_ref[...] = dcv[0:SUBLANES, :]
        dr_ref[:, D_MODEL:2 * D_MODEL] = (du * cx).astype(BF16)
        dr_ref[:, 2 * D_MODEL:3 * D_MODEL] = (du * cc).astype(BF16)

    tok = pl.BlockSpec((tm, D_MODEL), lambda i: (rev(i), 0))
    return _call(
        body, "mix_bwd", (nt,), [dx2, merged, attn, z, z, z, z, z, z, z, vec, w_out],
        [tok, tok, tok] + _z_specs(tm, rev) + [_full((SUBLANES, D_MODEL)), _full((D_MODEL, D_MODEL))],
        [jax.ShapeDtypeStruct((t, D_MODEL), BF16), jax.ShapeDtypeStruct((t, D_MODEL), BF16),
         jax.ShapeDtypeStruct((t, REST_WIDTH), BF16), jax.ShapeDtypeStruct((SUBLANES, D_MODEL), F32)],
        [tok, tok, pl.BlockSpec((tm, REST_WIDTH), lambda i: (rev(i), 0)), _full((SUBLANES, D_MODEL))],
        scratch=[pltpu.VMEM((SUBLANES, D_MODEL), F32)], rider=rider)


def _attn_bwd(z, dattn, attn, lse, sinks, rider):
    t = z.shape[0]
    tq = min(TOKEN_TILE, t)
    nblk = tq // WINDOW
    nt = t // tq

    def body(q_ref, kv_ref, do_ref, o_ref, lse_ref, sink_ref, dq_ref, dkv_ref, ds_ref, acc_ref):
        i = pl.program_id(0)

        @pl.when(i == 0)
        def _():
            acc_ref[...] = jnp.zeros_like(acc_ref)
            ds_ref[...] = jnp.zeros_like(ds_ref)

        lane = lax.broadcasted_iota(jnp.int32, (1, LANES), 1)
        ind_row = lax.broadcasted_iota(jnp.int32, (SUBLANES, LANES), 0)
        ind_low = lax.broadcasted_iota(jnp.int32, (SUBLANES, LANES), 1) < HEAD_DIM
        indicator = jnp.where(jnp.logical_or(jnp.logical_and(ind_row == 0, ind_low),
                                             jnp.logical_and(ind_row == 1, jnp.logical_not(ind_low))),
                              1.0, 0.0).astype(BF16)
        low = lax.broadcasted_iota(jnp.int32, (2 * WINDOW, LANES), 1) < HEAD_DIM

        def both_heads(even, odd):
            picked = jnp.where(low, even, odd)
            return picked + jnp.concatenate([picked[:, HEAD_DIM:], picked[:, :HEAD_DIM]], axis=1)

        def one_block(b, dsink):
            row0 = pl.multiple_of(b * WINDOW, WINDOW)
            start = i * tq + b * WINDOW
            prev = pl.multiple_of(jnp.maximum(start - WINDOW, 0), WINDOW)
            cur = pl.multiple_of(start, WINDOW)
            kvw = jnp.concatenate([kv_ref[pl.ds(prev, WINDOW), :], kv_ref[pl.ds(cur, WINDOW), :]], axis=0)
            k_halves = _half_tiles(kvw[:, :KV_WIDTH])
            v_halves = _half_tiles(kvw[:, KV_WIDTH:])
            valid = _stacked_mask(start > 0)
            dk_groups, dv_groups = [], []
            for j in range(N_KV_HEADS):
                qst = _stack_pairs(q_ref, row0, j)
                dost = _stack_pairs(do_ref, row0, j)
                prod = dost.astype(F32) * _stack_pairs(o_ref, row0, j).astype(F32)
                prod_hi = prod.astype(BF16)
                prod_lo = (prod - prod_hi.astype(F32)).astype(BF16)
                deltas = (lax.dot_general(indicator, prod_hi, NT_DIMS, preferred_element_type=F32)
                          + lax.dot_general(indicator, prod_lo, NT_DIMS, preferred_element_type=F32))
                dq_t = jnp.zeros((LANES, STACK), F32)
                dk_par, dv_par = [], []
                for parity in range(2):
                    heads = [j * GROUP + 2 * p + parity for p in range(PAIRS)]
                    kk, vv = k_halves[j][parity], v_halves[j][parity]
                    s = lax.dot_general(kk, qst, NT_DIMS, preferred_element_type=F32) * SCORE_SCALE
                    lse = jnp.concatenate([lse_ref[h:h + 1, pl.ds(row0, WINDOW)] for h in heads], axis=1)
                    p = jnp.where(valid, jnp.exp2(s - lse), 0.0)
                    dp = lax.dot_general(vv, dost, NT_DIMS, preferred_element_type=F32)
                    delta = deltas[parity:parity + 1, :]
                    dsb = (p * (dp - delta)).astype(BF16)
                    dq_t = dq_t + lax.dot_general(kk, dsb, TN_DIMS, preferred_element_type=F32)
                    dk_par.append(jnp.dot(dsb, qst, preferred_element_type=F32))
                    dv_par.append(jnp.dot(p.astype(BF16), dost, preferred_element_type=F32))
                    sink = _per_pair_row([sink_ref[h] * LOG2E for h in heads])
                    weighted = jnp.exp2(sink - lse) * delta
                    for pr, h in enumerate(heads):
                        dsink = dsink - jnp.where(lane == h, jnp.sum(weighted[:, pr * WINDOW:(pr + 1) * WINDOW]), 0.0)
                dq_st = jnp.transpose((dq_t * ATTN_SCALE).astype(BF16))
                for pr in range(PAIRS):
                    dq_ref[pl.ds(row0, WINDOW), (j * PAIRS + pr) * LANES:(j * PAIRS + pr + 1) * LANES] = (
                        dq_st[pr * WINDOW:(pr + 1) * WINDOW, :])
                dk_groups.append(both_heads(dk_par[0], dk_par[1]))
                dv_groups.append(both_heads(dv_par[0], dv_par[1]))
            blk = jnp.concatenate([jnp.where(low, dk_groups[0], dk_groups[1]) * ATTN_SCALE,
                                   jnp.where(low, dv_groups[0], dv_groups[1])], axis=1)
            acc_ref[pl.ds(prev, WINDOW), :] += blk[:WINDOW, :]
            acc_ref[pl.ds(cur, WINDOW), :] += blk[WINDOW:, :]
            return dsink

        dsink = lax.fori_loop(0, nblk, one_block, jnp.zeros((1, LANES), F32))
        ds_ref[0:1, :] += dsink

        @pl.when(i == nt - 1)
        def _():
            dkv_ref[...] = acc_ref[...].astype(BF16)

    tok = pl.BlockSpec((tq, D_MODEL), lambda i: (i, 0))
    return _call(
        body, "attn_bwd", (nt,), [z, z, dattn, attn, lse, sinks],
        [tok, pl.BlockSpec((t, 2 * KV_WIDTH), lambda i: (0, KV_COL // (2 * KV_WIDTH))), tok, tok,
         pl.BlockSpec((N_Q_HEADS, tq), lambda i: (0, i)), pl.BlockSpec(memory_space=pltpu.SMEM)],
        [jax.ShapeDtypeStruct((t, D_MODEL), BF16), jax.ShapeDtypeStruct((t, 2 * KV_WIDTH), BF16),
         jax.ShapeDtypeStruct((SUBLANES, LANES), F32)],
        [tok, _full((t, 2 * KV_WIDTH)), _full((SUBLANES, LANES))],
        scratch=[pltpu.VMEM((t, 2 * KV_WIDTH), F32)], rider=rider)


def _inproj_bwd(dq, drest, dkv, x, dx2, vec, w_t, rider):
    t = x.shape[0]
    tm = min(TOKEN_TILE, t)

    def body(dq_ref, dr_ref, dkv_ref, x_ref, dx2_ref, vec_ref, w_ref, gx_ref, acc_ref, db_ref):
        @pl.when(pl.program_id(0) == 0)
        def _():
            acc_ref[...] = jnp.zeros_like(acc_ref)
            db_ref[...] = jnp.zeros_like(db_ref)

        g = vec_ref[0:1, :]
        sc1 = vec_ref[1:2, :]
        dqb, drb, dkvb = dq_ref[...], dr_ref[...], dkv_ref[...]
        dh = jnp.dot(dqb, w_ref[:REF_KV_COL, :], preferred_element_type=F32)
        dh = dh + jnp.dot(drb, w_ref[REF_REST_COL:, :], preferred_element_type=F32)
        dh = dh + jnp.dot(dkvb, w_ref[REF_KV_COL:REF_REST_COL, :], preferred_element_type=F32)
        db_ref[:, :REF_KV_COL] += jnp.sum(dqb.astype(F32), axis=0, keepdims=True)
        db_ref[:, REF_REST_COL:] += jnp.sum(drb.astype(F32), axis=0, keepdims=True)
        db_ref[:, REF_KV_COL:REF_REST_COL] += jnp.sum(dkvb.astype(F32), axis=0, keepdims=True)
        xf = x_ref[...]
        r = lax.rsqrt(jnp.mean(xf * xf, axis=-1, keepdims=True) + EPS)
        xn = xf * r
        acc_ref[0:1, :] += jnp.sum(dh, axis=0, keepdims=True)
        acc_ref[1:2, :] += jnp.sum(dh * xn * g, axis=0, keepdims=True)
        acc_ref[2:3, :] += jnp.sum(dh * xn * (1.0 + sc1), axis=0, keepdims=True)
        dxn = dh * g * (1.0 + sc1)
        gx_ref[...] = dx2_ref[...] + r * (dxn - xn * jnp.mean(dxn * xn, axis=-1, keepdims=True))

    tok = pl.BlockSpec((tm, D_MODEL), lambda i: (i, 0))
    return _call(
        body, "inproj_bwd", (t // tm,), [dq, drest, dkv, x, dx2, vec, w_t],
        [tok, pl.BlockSpec((tm, REST_WIDTH), lambda i: (i, 0)),
         pl.BlockSpec((tm, 2 * KV_WIDTH), lambda i: (i, 0)), tok, tok,
         _full((SUBLANES, D_MODEL)), _full((IN_WIDTH, D_MODEL))],
        [jax.ShapeDtypeStruct((t, D_MODEL), F32), jax.ShapeDtypeStruct((SUBLANES, D_MODEL), F32),
         jax.ShapeDtypeStruct((1, IN_WIDTH), F32)],
        [tok, _full((SUBLANES, D_MODEL)), _full((1, IN_WIDTH))], rider=rider)


def _weight_grad(b, a, name, bn, rows=None, row0=0, into=None, rider=None):
    t, n = b.shape
    m = a.shape[1]
    rows = n if rows is None else rows
    tk = min(TOKEN_TILE, t)
    for cand in (4 * TOKEN_TILE, 2 * TOKEN_TILE):
        if t % cand == 0 and 2 * cand * (bn + m) * 2 + bn * m * 4 <= WGRAD_VMEM:
            tk = cand
            break
    nk = t // tk
    block0 = row0 // bn

    def body(b_ref, a_ref, *rest):
        out_ref, acc_ref = rest[-2:]
        k = pl.program_id(1)

        @pl.when(k == 0)
        def _():
            acc_ref[...] = jnp.zeros_like(acc_ref)

        acc_ref[...] += lax.dot_general(b_ref[...], a_ref[...], TN_DIMS, preferred_element_type=F32)

        @pl.when(k == nk - 1)
        def _():
            out_ref[...] = acc_ref[...].astype(BF16)

    outs, routs = _call(
        body, name, (n // bn, nk), [b, a] + ([] if into is None else [into]),
        [pl.BlockSpec((tk, bn), lambda j, k: (k, j)), pl.BlockSpec((tk, m), lambda j, k: (k, 0))]
        + ([] if into is None else [ANY]),
        [jax.ShapeDtypeStruct((rows, m), BF16)], [pl.BlockSpec((bn, m), lambda j, k: (block0 + j, 0))],
        scratch=[pltpu.VMEM((bn, m), F32)], rider=rider, aliases=None if into is None else {2: 0})
    return outs[0], routs


def _to_rows(v):
    n = v.shape[0]
    padded = -(-n // (SUBLANES * LANES)) * SUBLANES * LANES
    return jnp.pad(v, (0, padded - n)).reshape(padded // LANES, LANES)


def _vec_rows(*rows):
    stacked = jnp.concatenate([r.reshape(1, D_MODEL) for r in rows], axis=0)
    return jnp.pad(stacked, ((0, SUBLANES - len(rows)), (0, 0)))


def kernel(x, c, w_ada, b_ada, g_mix, w_in, b_in, sinks, conv_w, w_out, g_ffn, w_ffn_in, w_ffn_out, g_final, loss_target, m_w_ada, m_b_ada, m_g_mix, m_w_in, m_b_in, m_sinks, m_conv_w, m_w_out, m_g_ffn, m_w_ffn_in, m_w_ffn_out, m_g_final, v_w_ada, v_b_ada, v_g_mix, v_w_in, v_b_in, v_sinks, v_conv_w, v_w_out, v_g_ffn, v_w_ffn_in, v_w_ffn_out, v_g_final):
    ix, iy, ic = _my_place()
    me = 4 * ix + 2 * iy + ic
    xs = x[0]
    target = loss_target[0]
    ada_cols = w_ada.shape[2]
    conv_cols = conv_w.shape[2]

    first = _small_allgather(_to_rows(jnp.concatenate([c[0], conv_w[0].reshape(-1)])), "gather_c_conv")
    first = first.reshape(N_DEV, -1)
    c_all = first[:, :D_MODEL]
    conv_full = jnp.transpose(first[:, D_MODEL:D_MODEL + 3 * conv_cols].reshape(N_DEV, 3, conv_cols), (1, 0, 2))
    conv_full = conv_full.reshape(3, D_MODEL)
    b_cols = lax.dynamic_slice_in_dim(b_ada, me * ada_cols, ada_cols, axis=1)
    mod_part = _ada_forward(c_all, w_ada[0], b_cols)
    mod_all = _small_allgather(mod_part.reshape(-1, LANES), "gather_mod").reshape(N_DEV, N_DEV, ada_cols)
    mod = lax.dynamic_index_in_dim(mod_all, me, axis=1, keepdims=False).reshape(N_MOD, D_MODEL)
    sh1, sc1, ga1, sh2, sc2, ga2 = [mod[i:i + 1] for i in range(N_MOD)]

    wt_in, wt_fi = jnp.transpose(w_in[0]), jnp.transpose(w_ffn_in[0])
    g_in, (cast_fi, cast_out, cast_fo) = _gather_first_weight(wt_in, [wt_fi, w_out[0], w_ffn_out[0]])
    w_in_t = g_in.reshape(IN_WIDTH, D_MODEL)
    (z, h1), (g_fi, g_out) = _inproj_fwd(xs, _vec_rows(g_mix, sc1, sh1), w_in_t, b_in,
                                         _gather_rider([cast_fi, cast_out]))
    w_fi_t = g_fi.reshape(2 * D_FF, D_MODEL)
    w_out_full = g_out.reshape(D_MODEL, D_MODEL)
    (attn, lse), (g_fo,) = _attn_fwd(z, sinks[0], _gather_rider([cast_fo]))
    w_fo_full = g_fo.reshape(D_FF, D_MODEL)
    merged, x2, h2 = _mix_fwd(xs, attn, z, _vec_rows(ga1, g_ffn, sc2, sh2, conv_full[0], conv_full[1], conv_full[2]),
                              w_out_full)
    gu, act = _ffn_fwd(h2, w_fi_t)
    dx3, df, dgu, acc_l = _ffn_out_loss(act, gu, x2, target, _vec_rows(ga2, g_final), w_fo_full)

    gw_fo, _ = _weight_grad(act, df, "wgrad_ffn_out", D_FF)
    gw_fi, _ = _weight_grad(dgu, h2, "wgrad_ffn_in", D_FF)
    blocks_fo = gw_fo.reshape(N_DEV, D_FF // N_DEV, D_MODEL)
    blocks_fi = gw_fi.reshape(N_DEV, 2 * D_FF // N_DEV, D_MODEL)
    (dx2, acc_f), (sib_fo, sib_fi) = _ffn_in_bwd(dgu, x2, dx3, _vec_rows(g_ffn, sc2), w_fi_t,
                                                 _sibling_rider([blocks_fo, blocks_fi]))
    sums_fo, mine_fo = _sibling_sum(_own_blocks(blocks_fo), sib_fo, "sibling_sum_ffn_out")
    sums_fi, mine_fi = _sibling_sum(_own_blocks(blocks_fi), sib_fi, "sibling_sum_ffn_in")
    (dout, dattn, drest, acc_m), (ici_fo, ici_fi) = _mix_bwd(
        dx2, merged, attn, z, _vec_rows(ga1, conv_full[0], conv_full[1], conv_full[2]), w_out_full,
        _chip_rider([sums_fo, sums_fi]))
    gw_out, _ = _weight_grad(merged, dout, "wgrad_out", D_MODEL)
    blocks_out = gw_out.reshape(N_DEV, D_MODEL // N_DEV, D_MODEL)
    (dq, dkv, dsink), (sib_out,) = _attn_bwd(z, dattn, attn, lse, sinks[0], _sibling_rider([blocks_out]))
    sums_out, mine_out = _sibling_sum(_own_blocks(blocks_out), sib_out, "sibling_sum_out")
    gw_in, (ici_out,) = _weight_grad(drest, h1, "wgrad_in_rest", IN_CHUNK, rows=IN_WIDTH, row0=REF_REST_COL,
                                     rider=_chip_rider([sums_out]))
    gw_in, _ = _weight_grad(dq, h1, "wgrad_in_q", D_MODEL, rows=IN_WIDTH, row0=0, into=gw_in)
    gw_in, _ = _weight_grad(dkv, h1, "wgrad_in_kv", 2 * KV_WIDTH, rows=IN_WIDTH, row0=REF_KV_COL, into=gw_in)
    blocks_in = gw_in.reshape(N_DEV, IN_WIDTH // N_DEV, D_MODEL)
    (sib_in,) = _carry(_sibling_rider([blocks_in]), "sibling_w_in")
    sums_in, mine_in = _sibling_sum(_own_blocks(blocks_in), sib_in, "sibling_sum_in")
    (grad_x, acc_i, db_in), (ici_in,) = _inproj_bwd(dq, drest, dkv, xs, dx2, _vec_rows(g_mix, sc1), w_in_t,
                                                    _chip_rider([sums_in]))

    pieces = [acc_i[0], acc_i[1], acc_m[0], acc_f[0], acc_f[1], acc_l[2],
              acc_i[2], db_in[0], acc_f[2], acc_l[1],
              acc_m[1], acc_m[2], acc_m[3], dsink[0], acc_l[0]]
    offsets = [0]
    for p in pieces:
        offsets.append(offsets[-1] + p.shape[0])
    packed = _small_allgather(_to_rows(jnp.concatenate(pieces)), "gather_small")
    dmod_all = packed.reshape(N_DEV, -1)[:, :N_MOD * D_MODEL]
    total = _sum_devices(packed).reshape(-1)
    part = lambda i: total[offsets[i]:offsets[i + 1]]
    g_b_ada = total[:N_MOD * D_MODEL].reshape(1, -1)
    g_g_mix, g_b_in, g_g_ffn, g_g_final = part(6).reshape(1, -1), part(7).reshape(1, -1), part(8).reshape(1, -1), part(9)
    g_conv_full = jnp.stack([part(10), part(11), part(12)])
    g_conv = lax.dynamic_slice_in_dim(g_conv_full, me * conv_cols, conv_cols, axis=1)[None]
    g_sinks = part(13)[:N_Q_HEADS].reshape(1, -1)
    loss = (0.5 / D_MODEL) * jnp.sum(part(14))
    dmod_cols = lax.dynamic_slice_in_dim(dmod_all, me * ada_cols, ada_cols, axis=1)
    g_w_ada = _ada_weight_grad(c_all, dmod_cols)

    def reduced(mine, ici, w, m, v, name, transposed=False):
        turn = jnp.transpose if transposed else (lambda a: a)
        return tuple(turn(o)[None] for o in _chip_sum_adamw(mine, ici, turn(w[0]), turn(m[0]), turn(v[0]), name))

    d_ada, nm_ada, nv_ada = _adamw(w_ada[0], g_w_ada, m_w_ada[0], v_w_ada[0], "adamw_w_ada")
    small_names = ["b_ada", "g_mix", "b_in", "sinks", "conv_w", "g_ffn", "g_final"]
    small_w = [b_ada, g_mix, b_in, sinks, conv_w, g_ffn, g_final]
    small_m = [m_b_ada, m_g_mix, m_b_in, m_sinks, m_conv_w, m_g_ffn, m_g_final]
    small_v = [v_b_ada, v_g_mix, v_b_in, v_sinks, v_conv_w, v_g_ffn, v_g_final]
    small_g = [g_b_ada, g_g_mix, g_b_in, g_sinks, g_conv, g_g_ffn, g_g_final]
    small_g = [g.reshape(w.shape) for g, w in zip(small_g, small_w)]
    flat = lambda arrs: _to_rows(jnp.concatenate([a.reshape(-1) for a in arrs]))
    sd, snm, snv = _adamw(flat(small_w), flat(small_g), flat(small_m), flat(small_v), "adamw_small")
    sizes = [w.size for w in small_w]
    starts = [sum(sizes[:i]) for i in range(len(sizes))]
    unflat = lambda a: {n: a.reshape(-1)[s:s + z_].reshape(w.shape)
                        for n, s, z_, w in zip(small_names, starts, sizes, small_w)}
    sd, snm, snv = unflat(sd), unflat(snm), unflat(snv)
    sg = dict(zip(small_names, small_g))

    res = {
        "w_ada": (g_w_ada[None], d_ada[None], nm_ada[None], nv_ada[None]),
        "w_in": reduced(mine_in, ici_in, w_in, m_w_in, v_w_in, "adamw_w_in", transposed=True),
        "w_out": reduced(mine_out, ici_out, w_out, m_w_out, v_w_out, "adamw_w_out"),
        "w_ffn_in": reduced(mine_fi, ici_fi, w_ffn_in, m_w_ffn_in, v_w_ffn_in, "adamw_w_ffn_in", transposed=True),
        "w_ffn_out": reduced(mine_fo, ici_fo, w_ffn_out, m_w_ffn_out, v_w_ffn_out, "adamw_w_ffn_out"),
    }
    for n in small_names:
        res[n] = (sg[n], sd[n], snm[n], snv[n])
    order = ["w_ada", "b_ada", "g_mix", "w_in", "b_in", "sinks", "conv_w", "w_out", "g_ffn", "w_ffn_in", "w_ffn_out",
             "g_final"]
    outs = [loss, grad_x[None]]
    for k in range(4):
        outs += [res[n][k] for n in order]
    return tuple(outs)
```

```python
import functools
import math

import jax
import jax.numpy as jnp
from jax import lax
from jax.experimental import pallas as pl
from jax.experimental.pallas import tpu as pltpu

F32 = jnp.float32
BF16 = jnp.bfloat16

D_MODEL = 1024
HEAD_DIM = 64
N_Q_HEADS = 16
N_KV_HEADS = 2
GROUP = 8
WINDOW = 128
KV_WIDTH = N_KV_HEADS * HEAD_DIM
D_FF = 2816
IN_WIDTH = 6400
N_MOD = 6
EPS = 1e-6
N_DEV = 8
REST_WIDTH = 5 * D_MODEL
KV_COL = D_MODEL + REST_WIDTH
ATTN_SCALE = HEAD_DIM ** -0.5

ADAM_LR = 0.001
ADAM_B1 = 0.9
ADAM_B2 = 0.999
ADAM_EPS = 1e-08
ADAM_WD = 0.01
ADAM_STEP = 10

LANES = 128
SUBLANES = 8
BF16_ROWS = 16
VMEM_LIMIT = 56 * 1024 * 1024
TOKEN_TILE = 512
FF_CHUNK = 256
WGRAD_VMEM = 40 * 1024 * 1024
MESH = pl.DeviceIdType.MESH
ANY = pl.BlockSpec(memory_space=pl.ANY)

NT_DIMS = (((1,), (1,)), ((), ()))
TN_DIMS = (((0,), (0,)), ((), ()))
CHIP_FLIPS = [(0, 0), (1, 0), (0, 1), (1, 1)]


def _full(shape):
    return pl.BlockSpec(shape, lambda *_: (0,) * len(shape))


def _my_place():
    return lax.axis_index("x"), lax.axis_index("y"), lax.axis_index("c")


def _flip(v, bit):
    return 1 - v if bit else v


def _sigmoid(v):
    return 1.0 / (1.0 + jnp.exp(-v))


class _Rider:
    def __init__(self, ins, out_shapes, sem_shapes, first=None, mid=None, last=None, ins_in_vmem=False):
        self.ins, self.out_shapes, self.sem_shapes = list(ins), list(out_shapes), list(sem_shapes)
        self.in_specs = [_full(a.shape) if ins_in_vmem else ANY for a in self.ins]
        self.hooks = [(when, fn) for when, fn in (("first", first), ("mid", mid), ("last", last)) if fn is not None]


def _call(body, name, grid, args, in_specs, out_shape, out_specs, scratch=(), rider=None, aliases=None):
    n_in, n_out, n_scr = len(args), len(out_shape), len(scratch)
    r_in = rider.ins if rider else []
    r_out = rider.out_shapes if rider else []
    r_sem = rider.sem_shapes if rider else []
    nsteps = math.prod(grid)

    def full_body(*refs):
        pos = 0
        groups = []
        for size in (n_in, len(r_in), n_out, len(r_out), n_scr, len(r_sem)):
            groups.append(refs[pos:pos + size])
            pos += size
        ins, rins, outs, routs, scr, rsems = groups
        step = pl.program_id(0)
        for axis in range(1, len(grid)):
            step = step * grid[axis] + pl.program_id(axis)
        at = {"first": 0, "mid": (3 * nsteps) // 4, "last": nsteps - 1}
        hooks = rider.hooks if rider else []
        for when, fn in hooks:
            if when != "last":
                pl.when(step == at[when])(functools.partial(fn, rins, routs, rsems))
        body(*ins, *outs, *scr)
        for when, fn in hooks:
            if when == "last":
                pl.when(step == at[when])(functools.partial(fn, rins, routs, rsems))

    outs = pl.pallas_call(
        full_body, name=name, grid=grid,
        out_shape=list(out_shape) + list(r_out),
        in_specs=list(in_specs) + (rider.in_specs if rider else []),
        out_specs=list(out_specs) + [ANY] * len(r_out),
        scratch_shapes=list(scratch) + list(r_sem),
        input_output_aliases=dict(aliases or {}),
        compiler_params=pltpu.CompilerParams(dimension_semantics=("arbitrary",) * len(grid),
                                             vmem_limit_bytes=VMEM_LIMIT),
    )(*args, *r_in)
    return list(outs[:n_out]), list(outs[n_out:])


def _gather_rider(shards):
    n = len(shards)

    def setup(outs, sems):
        x, y, c = _my_place()
        send_sems, recv_sems, _ = sems
        chips = [(1 - x, y), (x, 1 - y), (1 - x, 1 - y)]

        def block(w, place):
            return outs[w].at[4 * place[0] + 2 * place[1] + place[2]]

        def copy(w, k, place, to, src=None):
            return pltpu.make_async_remote_copy(
                src_ref=block(w, place) if src is None else src, dst_ref=block(w, place),
                send_sem=send_sems.at[w, k], recv_sem=recv_sems.at[w, k], device_id=to, device_id_type=MESH)

        return (x, y, c), (x, y, 1 - c), chips, block, copy

    def first(ins, outs, sems):
        me, sibling, chips, block, copy = setup(outs, sems)
        for w in range(n):
            pltpu.make_async_copy(ins[w], block(w, me), sems[2].at[w]).start()
            copy(w, 0, me, sibling, src=ins[w]).start()
            for j, chip in enumerate(chips):
                copy(w, 1 + j, me, (*chip, me[2]), src=ins[w]).start()

    def mid(ins, outs, sems):
        me, sibling, chips, block, copy = setup(outs, sems)
        for w in range(n):
            for j, chip in enumerate(chips):
                copy(w, 1 + j, (*chip, me[2]), me).wait_recv()
                copy(w, 4 + j, (*chip, me[2]), sibling).start()

    def last(ins, outs, sems):
        me, sibling, chips, block, copy = setup(outs, sems)
        for w in range(n):
            copy(w, 0, sibling, me).wait_recv()
            for j, chip in enumerate(chips):
                copy(w, 4 + j, (*chip, 1 - me[2]), me).wait_recv()
            copy(w, 0, me, sibling, src=ins[w]).wait_send()
            for j, chip in enumerate(chips):
                copy(w, 1 + j, me, (*chip, me[2]), src=ins[w]).wait_send()
                copy(w, 4 + j, (*chip, me[2]), sibling).wait_send()
            pltpu.make_async_copy(ins[w], block(w, me), sems[2].at[w]).wait()

    return _Rider(
        shards, [jax.ShapeDtypeStruct((N_DEV,) + s.shape, BF16) for s in shards],
        [pltpu.SemaphoreType.DMA((n, N_DEV - 1)), pltpu.SemaphoreType.DMA((n, N_DEV - 1)),
         pltpu.SemaphoreType.DMA((n,))],
        first=first, mid=mid, last=last, ins_in_vmem=True)


def _sibling_rider(gblocks):
    n = len(gblocks)

    def copies(ins, outs, sems):
        x, y, c = _my_place()
        send_sems, recv_sems = sems
        made = []
        for w in range(n):
            for f, (fx, fy) in enumerate(CHIP_FLIPS):
                chip = 4 * _flip(x, fx) + 2 * _flip(y, fy)
                made.append(pltpu.make_async_remote_copy(
                    src_ref=ins[w].at[chip + 1 - c], dst_ref=outs[w].at[f], send_sem=send_sems.at[w, f],
                    recv_sem=recv_sems.at[w, f], device_id=(x, y, 1 - c), device_id_type=MESH))
        return made

    def first(ins, outs, sems):
        for cp in copies(ins, outs, sems):
            cp.start()

    def last(ins, outs, sems):
        for cp in copies(ins, outs, sems):
            cp.wait_recv()
            cp.wait_send()

    return _Rider(gblocks, [jax.ShapeDtypeStruct((4,) + g.shape[1:], BF16) for g in gblocks],
                  [pltpu.SemaphoreType.DMA((n, 4))] * 2, first=first, last=last)


def _own_blocks(gblocks):
    x, y, c = _my_place()
    return jnp.stack([lax.dynamic_index_in_dim(gblocks, 4 * _flip(x, fx) + 2 * _flip(y, fy) + c, 0, keepdims=False)
                      for fx, fy in CHIP_FLIPS])


def _chip_rider(sums):
    n = len(sums)

    def copies(ins, outs, sems):
        x, y, c = _my_place()
        send_sems, recv_sems = sems
        made = []
        for w in range(n):
            for f in (1, 2, 3):
                fx, fy = CHIP_FLIPS[f]
                made.append(pltpu.make_async_remote_copy(
                    src_ref=ins[w].at[f - 1], dst_ref=outs[w].at[f - 1], send_sem=send_sems.at[w, f - 1],
                    recv_sem=recv_sems.at[w, f - 1], device_id=(_flip(x, fx), _flip(y, fy), c), device_id_type=MESH))
        return made

    def first(ins, outs, sems):
        for cp in copies(ins, outs, sems):
            cp.start()

    def last(ins, outs, sems):
        for cp in copies(ins, outs, sems):
            cp.wait_recv()
            cp.wait_send()

    return _Rider(sums, [jax.ShapeDtypeStruct(s.shape, BF16) for s in sums],
                  [pltpu.SemaphoreType.DMA((n, 3))] * 2, first=first, last=last)


def _small_allgather(v, name):
    rows = v.shape[0]

    def body(v_ref, out_ref, send_sems, recv_sems, local_sem):
        x, y, c = _my_place()
        me = 4 * x + 2 * y + c
        mine = pltpu.make_async_copy(v_ref, out_ref.at[me], local_sem)
        mine.start()
        sends = []
        for k in range(1, N_DEV):
            px, py, pc = _flip(x, k & 4), _flip(y, k & 2), _flip(c, k & 1)
            cp = pltpu.make_async_remote_copy(
                src_ref=v_ref, dst_ref=out_ref.at[me], send_sem=send_sems.at[k - 1], recv_sem=recv_sems.at[k - 1],
                device_id=(px, py, pc), device_id_type=MESH)
            cp.start()
            sends.append(cp)
        for k in range(1, N_DEV):
            px, py, pc = _flip(x, k & 4), _flip(y, k & 2), _flip(c, k & 1)
            pltpu.make_async_remote_copy(
                src_ref=v_ref, dst_ref=out_ref.at[4 * px + 2 * py + pc], send_sem=send_sems.at[k - 1],
                recv_sem=recv_sems.at[k - 1], device_id=(px, py, pc), device_id_type=MESH).wait_recv()
        for cp in sends:
            cp.wait_send()
        mine.wait()

    return pl.pallas_call(
        body, name=name,
        out_shape=jax.ShapeDtypeStruct((N_DEV, rows, LANES), F32),
        in_specs=[pl.BlockSpec(memory_space=pltpu.VMEM)],
        out_specs=pl.BlockSpec(memory_space=pltpu.VMEM),
        scratch_shapes=[pltpu.SemaphoreType.DMA((N_DEV - 1,)), pltpu.SemaphoreType.DMA((N_DEV - 1,)),
                        pltpu.SemaphoreType.DMA],
        compiler_params=pltpu.CompilerParams(vmem_limit_bytes=VMEM_LIMIT),
    )(v)


def _gather_first_weight(shard, others):
    n = len(others)

    def body(*refs):
        w_ref, other_refs = refs[0], refs[1:1 + n]
        out_ref, cast_refs = refs[1 + n], refs[2 + n:2 + 2 * n]
        mine_ref, send_sems, recv_sems, local_sem = refs[2 + 2 * n:]
        x, y, c = _my_place()
        me, sibling = (x, y, c), (x, y, 1 - c)
        chips = [(1 - x, y), (x, 1 - y), (1 - x, 1 - y)]

        def block(place):
            return out_ref.at[4 * place[0] + 2 * place[1] + place[2]]

        def copy(k, place, to, src=None):
            return pltpu.make_async_remote_copy(
                src_ref=block(place) if src is None else src, dst_ref=block(place),
                send_sem=send_sems.at[k], recv_sem=recv_sems.at[k], device_id=to, device_id_type=MESH)

        mine_ref[...] = w_ref[...].astype(BF16)
        local = pltpu.make_async_copy(mine_ref, block(me), local_sem)
        local.start()
        started = [copy(0, me, sibling, src=mine_ref)]
        started += [copy(1 + j, me, (*chip, c), src=mine_ref) for j, chip in enumerate(chips)]
        for cp in started:
            cp.start()
        for o_ref, c_ref in zip(other_refs, cast_refs):
            c_ref[...] = o_ref[...].astype(BF16)
        for j, chip in enumerate(chips):
            copy(1 + j, (*chip, c), me).wait_recv()
            passed = copy(4 + j, (*chip, c), sibling)
            passed.start()
            started.append(passed)
        copy(0, sibling, me).wait_recv()
        for j, chip in enumerate(chips):
            copy(4 + j, (*chip, 1 - c), me).wait_recv()
        for cp in started:
            cp.wait_send()
        local.wait()

    vmem = pl.BlockSpec(memory_space=pltpu.VMEM)
    outs = pl.pallas_call(
        body, name="gather_w_in",
        out_shape=[jax.ShapeDtypeStruct((N_DEV,) + shard.shape, BF16)]
        + [jax.ShapeDtypeStruct(o.shape, BF16) for o in others],
        in_specs=[vmem] * (1 + n),
        out_specs=[ANY] + [vmem] * n,
        scratch_shapes=[pltpu.VMEM(shard.shape, BF16), pltpu.SemaphoreType.DMA((N_DEV - 1,)),
                        pltpu.SemaphoreType.DMA((N_DEV - 1,)), pltpu.SemaphoreType.DMA],
        compiler_params=pltpu.CompilerParams(vmem_limit_bytes=VMEM_LIMIT),
    )(shard, *others)
    return outs[0], list(outs[1:])


def _carry(rider, name):
    def body(token_ref):
        token_ref[...] = jnp.zeros_like(token_ref)

    _, routs = _call(body, name, (1,), [], [], [jax.ShapeDtypeStruct((SUBLANES, LANES), F32)],
                     [_full((SUBLANES, LANES))], rider=rider)
    return routs


def _ada_forward(c_all, w_ada, b_cols):
    cols = w_ada.shape[1]

    def body(c_ref, w_ref, b_ref, out_ref):
        cf = c_ref[...]
        act = (cf * _sigmoid(cf)).astype(BF16)
        out_ref[...] = jnp.dot(act, w_ref[...].astype(BF16), preferred_element_type=F32) + b_ref[...]

    return pl.pallas_call(
        body, name="ada_forward",
        out_shape=jax.ShapeDtypeStruct((N_DEV, cols), F32),
        in_specs=[pl.BlockSpec(memory_space=pltpu.VMEM)] * 3,
        out_specs=pl.BlockSpec(memory_space=pltpu.VMEM),
        compiler_params=pltpu.CompilerParams(vmem_limit_bytes=VMEM_LIMIT),
    )(c_all, w_ada, b_cols)


def _ada_weight_grad(c_all, dmod_cols):
    cols = dmod_cols.shape[1]

    def body(c_ref, d_ref, out_ref):
        cf = c_ref[...]
        act = (cf * _sigmoid(cf)).astype(BF16)
        out_ref[...] = lax.dot_general(act, d_ref[...].astype(BF16), TN_DIMS, preferred_element_type=F32)

    return pl.pallas_call(
        body, name="ada_weight_grad",
        out_shape=jax.ShapeDtypeStruct((D_MODEL, cols), F32),
        in_specs=[pl.BlockSpec(memory_space=pltpu.VMEM)] * 2,
        out_specs=pl.BlockSpec(memory_space=pltpu.VMEM),
        compiler_params=pltpu.CompilerParams(vmem_limit_bytes=VMEM_LIMIT),
    )(c_all, dmod_cols)


def _sum_devices(packed):
    def body(p_ref, out_ref):
        total = p_ref[0]
        for d in range(1, N_DEV):
            total = total + p_ref[d]
        out_ref[...] = total

    return pl.pallas_call(
        body, name="sum_devices",
        out_shape=jax.ShapeDtypeStruct(packed.shape[1:], F32),
        in_specs=[pl.BlockSpec(memory_space=pltpu.VMEM)],
        out_specs=pl.BlockSpec(memory_space=pltpu.VMEM),
        compiler_params=pltpu.CompilerParams(vmem_limit_bytes=VMEM_LIMIT),
    )(packed)


def _row_tile(rows, multiple):
    for cand in range(min(rows, 256), 0, -1):
        if rows % cand == 0 and cand % multiple == 0:
            return cand
    return rows


def _adamw_update(w, g, m, v):
    c1 = 1.0 / (1.0 - ADAM_B1 ** ADAM_STEP)
    c2 = 1.0 / (1.0 - ADAM_B2 ** ADAM_STEP)
    nm = ADAM_B1 * m + (1.0 - ADAM_B1) * g
    nv = ADAM_B2 * v + (1.0 - ADAM_B2) * (g * g)
    delta = -ADAM_LR * ((nm * c1) / (jnp.sqrt(nv * c2) + ADAM_EPS) + ADAM_WD * w)
    return delta, nm, nv


def _adamw(w, g, m, v, name):
    rows, cols = w.shape
    tile = _row_tile(rows, SUBLANES)

    def body(w_ref, g_ref, m_ref, v_ref, d_ref, nm_ref, nv_ref):
        d_ref[...], nm_ref[...], nv_ref[...] = _adamw_update(w_ref[...], g_ref[...], m_ref[...], v_ref[...])

    spec = pl.BlockSpec((tile, cols), lambda i: (i, 0))
    outs, _ = _call(body, name, (rows // tile,), [w, g, m, v], [spec] * 4,
                    [jax.ShapeDtypeStruct((rows, cols), F32)] * 3, [spec] * 3)
    return outs


def _sibling_sum(own, sib, name):
    _, r, cdim = own.shape
    tile = _row_tile(r, BF16_ROWS)

    def body(own_ref, sib_ref, sums_ref, mine_ref):
        mine_ref[...] = own_ref[0].astype(F32) + sib_ref[0].astype(F32)
        for f in (1, 2, 3):
            sums_ref[f - 1] = (own_ref[f].astype(F32) + sib_ref[f].astype(F32)).astype(BF16)

    outs, _ = _call(
        body, name, (r // tile,), [own, sib], [pl.BlockSpec((4, tile, cdim), lambda i: (0, i, 0))] * 2,
        [jax.ShapeDtypeStruct((3, r, cdim), BF16), jax.ShapeDtypeStruct((r, cdim), F32)],
        [pl.BlockSpec((3, tile, cdim), lambda i: (0, i, 0)), pl.BlockSpec((tile, cdim), lambda i: (i, 0))])
    return outs


def _chip_sum_adamw(mine, ici, w, m, v, name):
    r, cdim = mine.shape
    tile = _row_tile(r, BF16_ROWS)

    def body(mine_ref, ici_ref, w_ref, m_ref, v_ref, g_ref, d_ref, nm_ref, nv_ref):
        g = mine_ref[...]
        for f in range(3):
            g = g + ici_ref[f].astype(F32)
        g_ref[...] = g
        d_ref[...], nm_ref[...], nv_ref[...] = _adamw_update(w_ref[...], g, m_ref[...], v_ref[...])

    spec = pl.BlockSpec((tile, cdim), lambda i: (i, 0))
    outs, _ = _call(
        body, name, (r // tile,), [mine, ici, w, m, v],
        [spec, pl.BlockSpec((3, tile, cdim), lambda i: (0, i, 0)), spec, spec, spec],
        [jax.ShapeDtypeStruct((r, cdim), F32)] * 4, [spec] * 4)
    return outs


REF_KV_COL = D_MODEL
REF_REST_COL = D_MODEL + 2 * KV_WIDTH
IN_CHUNK = 1280
IN_PIECES = ([(0, 0, D_MODEL)]
             + [(D_MODEL + n * IN_CHUNK, REF_REST_COL + n * IN_CHUNK, IN_CHUNK) for n in range(REST_WIDTH // IN_CHUNK)]
             + [(KV_COL, REF_KV_COL, 2 * KV_WIDTH)])


def _inproj_fwd(x, vec, w_t, b_in, rider):
    t = x.shape[0]
    tm = min(TOKEN_TILE, t)

    def body(x_ref, vec_ref, w_ref, b_ref, z_ref, h_ref):
        xf = x_ref[...]
        r = lax.rsqrt(jnp.mean(xf * xf, axis=-1, keepdims=True) + EPS)
        h = (xf * r) * vec_ref[0:1, :] * (1.0 + vec_ref[1:2, :]) + vec_ref[2:3, :]
        hb = h.astype(BF16)
        h_ref[...] = hb
        for mine, ref, width in IN_PIECES:
            zc = lax.dot_general(hb, w_ref[ref:ref + width, :], NT_DIMS, preferred_element_type=F32)
            z_ref[:, mine:mine + width] = (zc + b_ref[:, ref:ref + width]).astype(BF16)

    return _call(
        body, "inproj_fwd", (t // tm,), [x, vec, w_t, b_in],
        [pl.BlockSpec((tm, D_MODEL), lambda i: (i, 0)), _full((SUBLANES, D_MODEL)),
         _full((IN_WIDTH, D_MODEL)), _full((1, IN_WIDTH))],
        [jax.ShapeDtypeStruct((t, IN_WIDTH), BF16), jax.ShapeDtypeStruct((t, D_MODEL), BF16)],
        [pl.BlockSpec((tm, IN_WIDTH), lambda i: (i, 0)), pl.BlockSpec((tm, D_MODEL), lambda i: (i, 0))],
        rider=rider)


def _window_mask(has_prev):
    qi = lax.broadcasted_iota(jnp.int32, (WINDOW, 2 * WINDOW), 0)
    kj = lax.broadcasted_iota(jnp.int32, (WINDOW, 2 * WINDOW), 1)
    off = jnp.where(has_prev, 0, 4 * WINDOW)
    in_prev = jnp.logical_and(kj < WINDOW, kj > qi + off)
    in_cur = jnp.logical_and(kj >= WINDOW, (kj - WINDOW) <= qi)
    return jnp.logical_or(in_prev, in_cur)


PAIRS = GROUP // 2
STACK = PAIRS * WINDOW


LOG2E = 1.4426950408889634
LN2 = 0.6931471805599453
SCORE_SCALE = ATTN_SCALE * LOG2E


def _fill_window_bias(bias_ref):
    shape = bias_ref.shape[1:]
    kj = lax.broadcasted_iota(jnp.int32, shape, 0)
    qi = jnp.bitwise_and(lax.broadcasted_iota(jnp.int32, shape, 1), WINDOW - 1)
    in_prev = jnp.logical_and(kj < WINDOW, kj > qi)
    in_cur = jnp.logical_and(kj >= WINDOW, (kj - WINDOW) <= qi)
    bias_ref[0] = jnp.where(in_cur, 0.0, -jnp.inf)
    bias_ref[1] = jnp.where(jnp.logical_or(in_prev, in_cur), 0.0, -jnp.inf)


def _half_tiles(tile):
    low = lax.broadcasted_iota(jnp.int32, tile.shape, 1) < HEAD_DIM
    swapped = jnp.concatenate([tile[:, HEAD_DIM:], tile[:, :HEAD_DIM]], axis=1)
    zero = jnp.zeros_like(tile)
    return ((jnp.where(low, tile, zero), jnp.where(low, zero, swapped)),
            (jnp.where(low, swapped, zero), jnp.where(low, zero, tile)))


def _stack_pairs(ref, row0, j):
    return jnp.concatenate(
        [ref[pl.ds(row0, WINDOW), (j * PAIRS + p) * LANES:(j * PAIRS + p + 1) * LANES] for p in range(PAIRS)], axis=0)


def _per_pair_row(values):
    pair = lax.broadcasted_iota(jnp.int32, (1, STACK), 1) // WINDOW
    row = jnp.full((1, STACK), values[PAIRS - 1], F32)
    for p in range(PAIRS - 2, -1, -1):
        row = jnp.where(pair == p, values[p], row)
    return row


def _attn_fwd(z, sinks, rider):
    t = z.shape[0]
    tq = min(TOKEN_TILE, t)
    nblk = tq // WINDOW

    def body(q_ref, kv_ref, sink_ref, o_ref, lse_ref, bias_ref):
        i = pl.program_id(0)

        @pl.when(i == 0)
        def _():
            _fill_window_bias(bias_ref)

        def one_block(b, carry):
            row0 = pl.multiple_of(b * WINDOW, WINDOW)
            start = i * tq + b * WINDOW
            prev = pl.multiple_of(jnp.maximum(start - WINDOW, 0), WINDOW)
            cur = pl.multiple_of(start, WINDOW)
            kvw = jnp.concatenate([kv_ref[pl.ds(prev, WINDOW), :], kv_ref[pl.ds(cur, WINDOW), :]], axis=0)
            k_halves = _half_tiles(kvw[:, :KV_WIDTH])
            v_halves = _half_tiles(kvw[:, KV_WIDTH:])
            bias = bias_ref[jnp.minimum(start, 1)]
            for j in range(N_KV_HEADS):
                for pr in range(PAIRS):
                    cols = slice((j * PAIRS + pr) * LANES, (j * PAIRS + pr + 1) * LANES)
                    qp = q_ref[pl.ds(row0, WINDOW), cols]
                    o_t = jnp.zeros((LANES, WINDOW), F32)
                    for parity in range(2):
                        h = j * GROUP + 2 * pr + parity
                        s = lax.dot_general(k_halves[j][parity], qp, NT_DIMS, preferred_element_type=F32)
                        s = s * SCORE_SCALE + bias
                        sink = sink_ref[h] * LOG2E
                        m = jnp.maximum(jnp.max(s, axis=0, keepdims=True), sink)
                        p = jnp.exp2(s - m)
                        denom = jnp.sum(p, axis=0, keepdims=True) + jnp.exp2(sink - m)
                        pv = lax.dot_general(v_halves[j][parity], p.astype(BF16), TN_DIMS,
                                             preferred_element_type=F32)
                        o_t = o_t + pv * (1.0 / denom)
                        lse_ref[h:h + 1, pl.ds(row0, WINDOW)] = m + jnp.log2(denom)
                    o_ref[pl.ds(row0, WINDOW), cols] = jnp.transpose(o_t.astype(BF16))
            return carry

        lax.fori_loop(0, nblk, one_block, 0)

    return _call(
        body, "attn_fwd", (t // tq,), [z, z, sinks],
        [pl.BlockSpec((tq, D_MODEL), lambda i: (i, 0)),
         pl.BlockSpec((t, 2 * KV_WIDTH), lambda i: (0, KV_COL // (2 * KV_WIDTH))),
         pl.BlockSpec(memory_space=pltpu.SMEM)],
        [jax.ShapeDtypeStruct((t, D_MODEL), BF16), jax.ShapeDtypeStruct((N_Q_HEADS, t), F32)],
        [pl.BlockSpec((tq, D_MODEL), lambda i: (i, 0)), pl.BlockSpec((N_Q_HEADS, tq), lambda i: (0, i))],
        scratch=[pltpu.VMEM((2, 2 * WINDOW, WINDOW), F32)], rider=rider)


HALO = BF16_ROWS


def _shift_down(u, uh, k):
    row = lax.broadcasted_iota(jnp.int32, u.shape, 0)
    out = pltpu.roll(u, k, 0)
    for j in range(k):
        out = jnp.where(row == j, uh[HALO - k + j:HALO - k + j + 1, :], out)
    return out


def _shift_up(u, nxt, k):
    n = u.shape[0]
    row = lax.broadcasted_iota(jnp.int32, u.shape, 0)
    out = pltpu.roll(u, n - k, 0)
    for j in range(k):
        out = jnp.where(row == n - k + j, nxt[j:j + 1, :], out)
    return out


def _conv_inputs(cc_ref, cx_ref, hc_ref, hx_ref, first_tile):
    cc = cc_ref[...].astype(F32)
    cx = cx_ref[...].astype(F32)
    u = cc * cx
    uh = jnp.where(first_tile, 0.0, hc_ref[...].astype(F32) * hx_ref[...].astype(F32))
    return cc, cx, u, _shift_down(u, uh, 1), _shift_down(u, uh, 2)


def _z_specs(tm, order):
    per_tile = tm // HALO
    cols = [pl.BlockSpec((tm, D_MODEL), functools.partial(lambda i, j: (order(i), j), j=j)) for j in range(1, 6)]
    halos = [pl.BlockSpec((HALO, D_MODEL),
                          functools.partial(lambda i, j: (jnp.maximum(order(i) * per_tile - 1, 0), j), j=j))
             for j in (2, 3)]
    return cols + halos


def _mix_fwd(x, attn, z, vec, w_out):
    t = x.shape[0]
    tm = min(TOKEN_TILE, t)

    def body(x_ref, a_ref, cb_ref, cc_ref, cx_ref, ga_ref, gc_ref, hc_ref, hx_ref, vec_ref, w_ref,
             m_ref, x2_ref, h2_ref):
        i = pl.program_id(0)
        _, _, u, u1, u2 = _conv_inputs(cc_ref, cx_ref, hc_ref, hx_ref, i == 0)
        cv = vec_ref[4:5, :] * u2 + vec_ref[5:6, :] * u1 + vec_ref[6:7, :] * u
        conv = cb_ref[...].astype(F32) * cv
        merged = (_sigmoid(ga_ref[...].astype(F32)) * a_ref[...].astype(F32)
                  + _sigmoid(gc_ref[...].astype(F32)) * conv)
        mb = merged.astype(BF16)
        m_ref[...] = mb
        o = jnp.dot(mb, w_ref[...], preferred_element_type=F32)
        x2 = x_ref[...] + vec_ref[0:1, :] * o
        x2_ref[...] = x2
        r = lax.rsqrt(jnp.mean(x2 * x2, axis=-1, keepdims=True) + EPS)
        h2 = (x2 * r) * vec_ref[1:2, :] * (1.0 + vec_ref[2:3, :]) + vec_ref[3:4, :]
        h2_ref[...] = h2.astype(BF16)

    tok = pl.BlockSpec((tm, D_MODEL), lambda i: (i, 0))
    outs, _ = _call(
        body, "mix_fwd", (t // tm,), [x, attn, z, z, z, z, z, z, z, vec, w_out],
        [tok, tok] + _z_specs(tm, lambda i: i) + [_full((SUBLANES, D_MODEL)), _full((D_MODEL, D_MODEL))],
        [jax.ShapeDtypeStruct((t, D_MODEL), BF16), jax.ShapeDtypeStruct((t, D_MODEL), F32),
         jax.ShapeDtypeStruct((t, D_MODEL), BF16)],
        [tok, tok, tok])
    return outs


def _ffn_fwd(h2, w_t):
    t = h2.shape[0]
    tm = min(TOKEN_TILE, t)

    def body(h_ref, w_ref, gu_ref, a_ref):
        hb = h_ref[...]
        for n in range(D_FF // FF_CHUNK):
            lo, hi = n * FF_CHUNK, (n + 1) * FF_CHUNK
            g = lax.dot_general(hb, w_ref[lo:hi, :], NT_DIMS, preferred_element_type=F32)
            u = lax.dot_general(hb, w_ref[D_FF + lo:D_FF + hi, :], NT_DIMS, preferred_element_type=F32)
            sg = _sigmoid(g)
            silu = g * sg
            gu_ref[:, lo:hi] = (u * (sg * (1.0 + g * (1.0 - sg)))).astype(BF16)
            gu_ref[:, D_FF + lo:D_FF + hi] = silu.astype(BF16)
            a_ref[:, lo:hi] = (silu * u).astype(BF16)

    outs, _ = _call(
        body, "ffn_fwd", (t // tm,), [h2, w_t],
        [pl.BlockSpec((tm, D_MODEL), lambda i: (i, 0)), _full((2 * D_FF, D_MODEL))],
        [jax.ShapeDtypeStruct((t, 2 * D_FF), BF16), jax.ShapeDtypeStruct((t, D_FF), BF16)],
        [pl.BlockSpec((tm, 2 * D_FF), lambda i: (i, 0)), pl.BlockSpec((tm, D_FF), lambda i: (i, 0))])
    return outs


def _ffn_out_loss(a, gu, x2, target, vec, w_ffn_out):
    t = a.shape[0]
    tm = min(TOKEN_TILE, t)

    def body(a_ref, gu_ref, x2_ref, t_ref, vec_ref, w_ref, dx3_ref, df_ref, dgu_ref, acc_ref):
        @pl.when(pl.program_id(0) == 0)
        def _():
            acc_ref[...] = jnp.zeros_like(acc_ref)

        ga2 = vec_ref[0:1, :]
        gf = vec_ref[1:2, :]
        f = jnp.dot(a_ref[...], w_ref[...], preferred_element_type=F32)
        x3 = x2_ref[...] + ga2 * f
        r = lax.rsqrt(jnp.mean(x3 * x3, axis=-1, keepdims=True) + EPS)
        xn = x3 * r
        err = xn * gf - t_ref[...]
        dy = err * (1.0 / D_MODEL)
        dxn = dy * gf
        dx3 = r * (dxn - xn * jnp.mean(dxn * xn, axis=-1, keepdims=True))
        dx3_ref[...] = dx3
        acc_ref[0:1, :] += jnp.sum(err * err, axis=0, keepdims=True)
        acc_ref[1:2, :] += jnp.sum(dy * xn, axis=0, keepdims=True)
        acc_ref[2:3, :] += jnp.sum(dx3 * f, axis=0, keepdims=True)
        df = (dx3 * ga2).astype(BF16)
        df_ref[...] = df
        for n in range(D_FF // FF_CHUNK):
            lo, hi = n * FF_CHUNK, (n + 1) * FF_CHUNK
            da = lax.dot_general(df, w_ref[lo:hi, :], NT_DIMS, preferred_element_type=F32)
            dgu_ref[:, lo:hi] = (da * gu_ref[:, lo:hi].astype(F32)).astype(BF16)
            dgu_ref[:, D_FF + lo:D_FF + hi] = (da * gu_ref[:, D_FF + lo:D_FF + hi].astype(F32)).astype(BF16)

    tok = pl.BlockSpec((tm, D_MODEL), lambda i: (i, 0))
    outs, _ = _call(
        body, "ffn_out_loss", (t // tm,), [a, gu, x2, target, vec, w_ffn_out],
        [pl.BlockSpec((tm, D_FF), lambda i: (i, 0)), pl.BlockSpec((tm, 2 * D_FF), lambda i: (i, 0)),
         tok, tok, _full((SUBLANES, D_MODEL)), _full((D_FF, D_MODEL))],
        [jax.ShapeDtypeStruct((t, D_MODEL), F32), jax.ShapeDtypeStruct((t, D_MODEL), BF16),
         jax.ShapeDtypeStruct((t, 2 * D_FF), BF16), jax.ShapeDtypeStruct((SUBLANES, D_MODEL), F32)],
        [tok, tok, pl.BlockSpec((tm, 2 * D_FF), lambda i: (i, 0)), _full((SUBLANES, D_MODEL))])
    return outs


def _ffn_in_bwd(dgu, x2, dx3, vec, w_t, rider):
    t = x2.shape[0]
    tm = min(TOKEN_TILE, t)

    def body(dgu_ref, x2_ref, dx3_ref, vec_ref, wf_ref, dx2_ref, acc_ref):
        @pl.when(pl.program_id(0) == 0)
        def _():
            acc_ref[...] = jnp.zeros_like(acc_ref)

        gffn = vec_ref[0:1, :]
        sc2 = vec_ref[1:2, :]
        dh2 = jnp.dot(dgu_ref[...], wf_ref[...], preferred_element_type=F32)
        x2 = x2_ref[...]
        r = lax.rsqrt(jnp.mean(x2 * x2, axis=-1, keepdims=True) + EPS)
        xn = x2 * r
        acc_ref[0:1, :] += jnp.sum(dh2, axis=0, keepdims=True)
        acc_ref[1:2, :] += jnp.sum(dh2 * xn * gffn, axis=0, keepdims=True)
        acc_ref[2:3, :] += jnp.sum(dh2 * xn * (1.0 + sc2), axis=0, keepdims=True)
        dxn = dh2 * gffn * (1.0 + sc2)
        dx2_ref[...] = dx3_ref[...] + r * (dxn - xn * jnp.mean(dxn * xn, axis=-1, keepdims=True))

    tok = pl.BlockSpec((tm, D_MODEL), lambda i: (i, 0))
    return _call(
        body, "ffn_in_bwd", (t // tm,), [dgu, x2, dx3, vec, w_t],
        [pl.BlockSpec((tm, 2 * D_FF), lambda i: (i, 0)), tok, tok, _full((SUBLANES, D_MODEL)),
         _full((2 * D_FF, D_MODEL))],
        [jax.ShapeDtypeStruct((t, D_MODEL), F32), jax.ShapeDtypeStruct((SUBLANES, D_MODEL), F32)],
        [tok, _full((SUBLANES, D_MODEL))], rider=rider)


def _mix_bwd(dx2, merged, attn, z, vec, w_out, rider):
    t = dx2.shape[0]
    tm = min(TOKEN_TILE, t)
    nt = t // tm
    rev = lambda i: nt - 1 - i

    def body(dx2_ref, m_ref, a_ref, cb_ref, cc_ref, cx_ref, ga_ref, gc_ref, hc_ref, hx_ref,
             vec_ref, wo_ref, do_ref, da_ref, dr_ref, acc_ref, carry_ref):
        i = pl.program_id(0)

        @pl.when(i == 0)
        def _():
            acc_ref[...] = jnp.zeros_like(acc_ref)
            carry_ref[...] = jnp.zeros_like(carry_ref)

        ga1 = vec_ref[0:1, :]
        w0, w1, w2 = vec_ref[1:2, :], vec_ref[2:3, :], vec_ref[3:4, :]
        dx2 = dx2_ref[...]
        o = jnp.dot(m_ref[...], wo_ref[...], preferred_element_type=F32)
        acc_ref[0:1, :] += jnp.sum(dx2 * o, axis=0, keepdims=True)
        do = (dx2 * ga1).astype(BF16)
        do_ref[...] = do
        dm = lax.dot_general(do, wo_ref[...], NT_DIMS, preferred_element_type=F32)

        cc, cx, u, u1, u2 = _conv_inputs(cc_ref, cx_ref, hc_ref, hx_ref, i == nt - 1)
        cv = w0 * u2 + w1 * u1 + w2 * u
        cb = cb_ref[...].astype(F32)
        sa = _sigmoid(ga_ref[...].astype(F32))
        sc = _sigmoid(gc_ref[...].astype(F32))
        attn = a_ref[...].astype(F32)
        da_ref[...] = (dm * sa).astype(BF16)
        dconv = dm * sc
        dr_ref[:, 3 * D_MODEL:4 * D_MODEL] = (dm * attn * sa * (1.0 - sa)).astype(BF16)
        dr_ref[:, 4 * D_MODEL:5 * D_MODEL] = (dconv * (cb * cv) * (1.0 - sc)).astype(BF16)
        dr_ref[:, 0:D_MODEL] = (dconv * cv).astype(BF16)
        dcv = dconv * cb
        acc_ref[1:2, :] += jnp.sum(dcv * u2, axis=0, keepdims=True)
        acc_ref[2:3, :] += jnp.sum(dcv * u1, axis=0, keepdims=True)
        acc_ref[3:4, :] += jnp.sum(dcv * u, axis=0, keepdims=True)
        nxt = carry_ref[...]
        du = w2 * dcv + w1 * _shift_up(dcv, nxt, 1) + w0 * _shift_up(dcv, nxt, 2)
        carry_ref[...] = dcv[0:SUBLANES, :]
        dr_ref[:, D_MODEL:2 * D_MODEL] = (du * cx).astype(BF16)
        dr_ref[:, 2 * D_MODEL:3 * D_MODEL] = (du * cc).astype(BF16)

    tok = pl.BlockSpec((tm, D_MODEL), lambda i: (rev(i), 0))
    return _call(
        body, "mix_bwd", (nt,), [dx2, merged, attn, z, z, z, z, z, z, z, vec, w_out],
        [tok, tok, tok] + _z_specs(tm, rev) + [_full((SUBLANES, D_MODEL)), _full((D_MODEL, D_MODEL))],
        [jax.ShapeDtypeStruct((t, D_MODEL), BF16), jax.ShapeDtypeStruct((t, D_MODEL), BF16),
         jax.ShapeDtypeStruct((t, REST_WIDTH), BF16), jax.ShapeDtypeStruct((SUBLANES, D_MODEL), F32)],
        [tok, tok, pl.BlockSpec((tm, REST_WIDTH), lambda i: (rev(i), 0)), _full((SUBLANES, D_MODEL))],
        scratch=[pltpu.VMEM((SUBLANES, D_MODEL), F32)], rider=rider)


def _attn_bwd(z, dattn, attn, lse, sinks, rider):
    t = z.shape[0]
    tq = min(TOKEN_TILE, t)
    nblk = tq // WINDOW
    nt = t // tq

    def body(q_ref, kv_ref, do_ref, o_ref, lse_ref, sink_ref, dq_ref, dkv_ref, ds_ref, acc_ref, bias_ref):
        i = pl.program_id(0)

        @pl.when(i == 0)
        def _():
            acc_ref[...] = jnp.zeros_like(acc_ref)
            ds_ref[...] = jnp.zeros_like(ds_ref)
            _fill_window_bias(bias_ref)

        lane = lax.broadcasted_iota(jnp.int32, (1, LANES), 1)
        ind_row = lax.broadcasted_iota(jnp.int32, (SUBLANES, LANES), 0)
        ind_low = lax.broadcasted_iota(jnp.int32, (SUBLANES, LANES), 1) < HEAD_DIM
        indicator = jnp.where(jnp.logical_or(jnp.logical_and(ind_row == 0, ind_low),
                                             jnp.logical_and(ind_row == 1, jnp.logical_not(ind_low))),
                              1.0, 0.0).astype(BF16)
        low = lax.broadcasted_iota(jnp.int32, (2 * WINDOW, LANES), 1) < HEAD_DIM

        def both_heads(even, odd):
            picked = jnp.where(low, even, odd)
            return picked + jnp.concatenate([picked[:, HEAD_DIM:], picked[:, :HEAD_DIM]], axis=1)

        def one_block(b, dsink):
            row0 = pl.multiple_of(b * WINDOW, WINDOW)
            start = i * tq + b * WINDOW
            prev = pl.multiple_of(jnp.maximum(start - WINDOW, 0), WINDOW)
            cur = pl.multiple_of(start, WINDOW)
            kvw = jnp.concatenate([kv_ref[pl.ds(prev, WINDOW), :], kv_ref[pl.ds(cur, WINDOW), :]], axis=0)
            k_halves = _half_tiles(kvw[:, :KV_WIDTH])
            v_halves = _half_tiles(kvw[:, KV_WIDTH:])
            bias = bias_ref[jnp.minimum(start, 1)]
            dk_groups, dv_groups = [], []
            for j in range(N_KV_HEADS):
                qst = _stack_pairs(q_ref, row0, j)
                dost = _stack_pairs(do_ref, row0, j)
                prod = dost.astype(F32) * _stack_pairs(o_ref, row0, j).astype(F32)
                prod_hi = prod.astype(BF16)
                prod_lo = (prod - prod_hi.astype(F32)).astype(BF16)
                deltas = (lax.dot_general(indicator, prod_hi, NT_DIMS, preferred_element_type=F32)
                          + lax.dot_general(indicator, prod_lo, NT_DIMS, preferred_element_type=F32))
                dq_t = jnp.zeros((LANES, STACK), F32)
                dk_par, dv_par = [], []
                for parity in range(2):
                    heads = [j * GROUP + 2 * p + parity for p in range(PAIRS)]
                    kk, vv = k_halves[j][parity], v_halves[j][parity]
                    s = lax.dot_general(kk, qst, NT_DIMS, preferred_element_type=F32) * SCORE_SCALE + bias
                    lse = jnp.concatenate([lse_ref[h:h + 1, pl.ds(row0, WINDOW)] for h in heads], axis=1)
                    p = jnp.exp2(s - lse)
                    dp = lax.dot_general(vv, dost, NT_DIMS, preferred_element_type=F32)
                    delta = deltas[parity:parity + 1, :]
                    dsb = (p * (dp - delta)).astype(BF16)
                    dq_t = dq_t + lax.dot_general(kk, dsb, TN_DIMS, preferred_element_type=F32)
                    dk_par.append(jnp.dot(dsb, qst, preferred_element_type=F32))
                    dv_par.append(jnp.dot(p.astype(BF16), dost, preferred_element_type=F32))
                    sink = _per_pair_row([sink_ref[h] * LOG2E for h in heads])
                    weighted = jnp.exp2(sink - lse) * delta
                    for pr, h in enumerate(heads):
                        dsink = dsink - jnp.where(lane == h, jnp.sum(weighted[:, pr * WINDOW:(pr + 1) * WINDOW]), 0.0)
                dq_st = jnp.transpose((dq_t * ATTN_SCALE).astype(BF16))
                for pr in range(PAIRS):
                    dq_ref[pl.ds(row0, WINDOW), (j * PAIRS + pr) * LANES:(j * PAIRS + pr + 1) * LANES] = (
                        dq_st[pr * WINDOW:(pr + 1) * WINDOW, :])
                dk_groups.append(both_heads(dk_par[0], dk_par[1]))
                dv_groups.append(both_heads(dv_par[0], dv_par[1]))
            blk = jnp.concatenate([jnp.where(low, dk_groups[0], dk_groups[1]) * ATTN_SCALE,
                                   jnp.where(low, dv_groups[0], dv_groups[1])], axis=1)
            acc_ref[pl.ds(prev, WINDOW), :] += blk[:WINDOW, :]
            acc_ref[pl.ds(cur, WINDOW), :] += blk[WINDOW:, :]
            return dsink

        dsink = lax.fori_loop(0, nblk, one_block, jnp.zeros((1, LANES), F32))
        ds_ref[0:1, :] += dsink

        @pl.when(i == nt - 1)
        def _():
            dkv_ref[...] = acc_ref[...].astype(BF16)

    tok = pl.BlockSpec((tq, D_MODEL), lambda i: (i, 0))
    return _call(
        body, "attn_bwd", (nt,), [z, z, dattn, attn, lse, sinks],
        [tok, pl.BlockSpec((t, 2 * KV_WIDTH), lambda i: (0, KV_COL // (2 * KV_WIDTH))), tok, tok,
         pl.BlockSpec((N_Q_HEADS, tq), lambda i: (0, i)), pl.BlockSpec(memory_space=pltpu.SMEM)],
        [jax.ShapeDtypeStruct((t, D_MODEL), BF16), jax.ShapeDtypeStruct((t, 2 * KV_WIDTH), BF16),
         jax.ShapeDtypeStruct((SUBLANES, LANES), F32)],
        [tok, _full((t, 2 * KV_WIDTH)), _full((SUBLANES, LANES))],
        scratch=[pltpu.VMEM((t, 2 * KV_WIDTH), F32), pltpu.VMEM((2, 2 * WINDOW, STACK), F32)], rider=rider)


def _inproj_bwd(dq, drest, dkv, x, dx2, vec, w_t, rider):
    t = x.shape[0]
    tm = min(TOKEN_TILE, t)

    def body(dq_ref, dr_ref, dkv_ref, x_ref, dx2_ref, vec_ref, w_ref, gx_ref, acc_ref, db_ref):
        @pl.when(pl.program_id(0) == 0)
        def _():
            acc_ref[...] = jnp.zeros_like(acc_ref)
            db_ref[...] = jnp.zeros_like(db_ref)

        g = vec_ref[0:1, :]
        sc1 = vec_ref[1:2, :]
        dqb, drb, dkvb = dq_ref[...], dr_ref[...], dkv_ref[...]
        dh = jnp.dot(dqb, w_ref[:REF_KV_COL, :], preferred_element_type=F32)
        dh = dh + jnp.dot(drb, w_ref[REF_REST_COL:, :], preferred_element_type=F32)
        dh = dh + jnp.dot(dkvb, w_ref[REF_KV_COL:REF_REST_COL, :], preferred_element_type=F32)
        db_ref[:, :REF_KV_COL] += jnp.sum(dqb.astype(F32), axis=0, keepdims=True)
        db_ref[:, REF_REST_COL:] += jnp.sum(drb.astype(F32), axis=0, keepdims=True)
        db_ref[:, REF_KV_COL:REF_REST_COL] += jnp.sum(dkvb.astype(F32), axis=0, keepdims=True)
        xf = x_ref[...]
        r = lax.rsqrt(jnp.mean(xf * xf, axis=-1, keepdims=True) + EPS)
        xn = xf * r
        acc_ref[0:1, :] += jnp.sum(dh, axis=0, keepdims=True)
        acc_ref[1:2, :] += jnp.sum(dh * xn * g, axis=0, keepdims=True)
        acc_ref[2:3, :] += jnp.sum(dh * xn * (1.0 + sc1), axis=0, keepdims=True)
        dxn = dh * g * (1.0 + sc1)
        gx_ref[...] = dx2_ref[...] + r * (dxn - xn * jnp.mean(dxn * xn, axis=-1, keepdims=True))

    tok = pl.BlockSpec((tm, D_MODEL), lambda i: (i, 0))
    return _call(
        body, "inproj_bwd", (t // tm,), [dq, drest, dkv, x, dx2, vec, w_t],
        [tok, pl.BlockSpec((tm, REST_WIDTH), lambda i: (i, 0)),
         pl.BlockSpec((tm, 2 * KV_WIDTH), lambda i: (i, 0)), tok, tok,
         _full((SUBLANES, D_MODEL)), _full((IN_WIDTH, D_MODEL))],
        [jax.ShapeDtypeStruct((t, D_MODEL), F32), jax.ShapeDtypeStruct((SUBLANES, D_MODEL), F32),
         jax.ShapeDtypeStruct((1, IN_WIDTH), F32)],
        [tok, _full((SUBLANES, D_MODEL)), _full((1, IN_WIDTH))], rider=rider)


def _weight_grad(b, a, name, bn, rows=None, row0=0, into=None, rider=None):
    t, n = b.shape
    m = a.shape[1]
    rows = n if rows is None else rows
    tk = min(TOKEN_TILE, t)
    for cand in (4 * TOKEN_TILE, 2 * TOKEN_TILE):
        if t % cand == 0 and 2 * cand * (bn + m) * 2 + bn * m * 4 <= WGRAD_VMEM:
            tk = cand
            break
    nk = t // tk
    block0 = row0 // bn

    def body(b_ref, a_ref, *rest):
        out_ref, acc_ref = rest[-2:]
        k = pl.program_id(1)

        @pl.when(k == 0)
        def _():
            acc_ref[...] = jnp.zeros_like(acc_ref)

        acc_ref[...] += lax.dot_general(b_ref[...], a_ref[...], TN_DIMS, preferred_element_type=F32)

        @pl.when(k == nk - 1)
        def _():
            out_ref[...] = acc_ref[...].astype(BF16)

    outs, routs = _call(
        body, name, (n // bn, nk), [b, a] + ([] if into is None else [into]),
        [pl.BlockSpec((tk, bn), lambda j, k: (k, j)), pl.BlockSpec((tk, m), lambda j, k: (k, 0))]
        + ([] if into is None else [ANY]),
        [jax.ShapeDtypeStruct((rows, m), BF16)], [pl.BlockSpec((bn, m), lambda j, k: (block0 + j, 0))],
        scratch=[pltpu.VMEM((bn, m), F32)], rider=rider, aliases=None if into is None else {2: 0})
    return outs[0], routs


def _to_rows(v):
    n = v.shape[0]
    padded = -(-n // (SUBLANES * LANES)) * SUBLANES * LANES
    return jnp.pad(v, (0, padded - n)).reshape(padded // LANES, LANES)


def _vec_rows(*rows):
    stacked = jnp.concatenate([r.reshape(1, D_MODEL) for r in rows], axis=0)
    return jnp.pad(stacked, ((0, SUBLANES - len(rows)), (0, 0)))


def kernel(x, c, w_ada, b_ada, g_mix, w_in, b_in, sinks, conv_w, w_out, g_ffn, w_ffn_in, w_ffn_out, g_final, loss_target, m_w_ada, m_b_ada, m_g_mix, m_w_in, m_b_in, m_sinks, m_conv_w, m_w_out, m_g_ffn, m_w_ffn_in, m_w_ffn_out, m_g_final, v_w_ada, v_b_ada, v_g_mix, v_w_in, v_b_in, v_sinks, v_conv_w, v_w_out, v_g_ffn, v_w_ffn_in, v_w_ffn_out, v_g_final):
    ix, iy, ic = _my_place()
    me = 4 * ix + 2 * iy + ic
    xs = x[0]
    target = loss_target[0]
    ada_cols = w_ada.shape[2]
    conv_cols = conv_w.shape[2]

    first = _small_allgather(_to_rows(jnp.concatenate([c[0], conv_w[0].reshape(-1)])), "gather_c_conv")
    first = first.reshape(N_DEV, -1)
    c_all = first[:, :D_MODEL]
    conv_full = jnp.transpose(first[:, D_MODEL:D_MODEL + 3 * conv_cols].reshape(N_DEV, 3, conv_cols), (1, 0, 2))
    conv_full = conv_full.reshape(3, D_MODEL)
    b_cols = lax.dynamic_slice_in_dim(b_ada, me * ada_cols, ada_cols, axis=1)
    mod_part = _ada_forward(c_all, w_ada[0], b_cols)
    mod_all = _small_allgather(mod_part.reshape(-1, LANES), "gather_mod").reshape(N_DEV, N_DEV, ada_cols)
    mod = lax.dynamic_index_in_dim(mod_all, me, axis=1, keepdims=False).reshape(N_MOD, D_MODEL)
    sh1, sc1, ga1, sh2, sc2, ga2 = [mod[i:i + 1] for i in range(N_MOD)]

    wt_in, wt_fi = jnp.transpose(w_in[0]), jnp.transpose(w_ffn_in[0])
    g_in, (cast_fi, cast_out, cast_fo) = _gather_first_weight(wt_in, [wt_fi, w_out[0], w_ffn_out[0]])
    w_in_t = g_in.reshape(IN_WIDTH, D_MODEL)
    (z, h1), (g_fi, g_out) = _inproj_fwd(xs, _vec_rows(g_mix, sc1, sh1), w_in_t, b_in,
                                         _gather_rider([cast_fi, cast_out]))
    w_fi_t = g_fi.reshape(2 * D_FF, D_MODEL)
    w_out_full = g_out.reshape(D_MODEL, D_MODEL)
    (attn, lse), (g_fo,) = _attn_fwd(z, sinks[0], _gather_rider([cast_fo]))
    w_fo_full = g_fo.reshape(D_FF, D_MODEL)
    merged, x2, h2 = _mix_fwd(xs, attn, z, _vec_rows(ga1, g_ffn, sc2, sh2, conv_full[0], conv_full[1], conv_full[2]),
                              w_out_full)
    gu, act = _ffn_fwd(h2, w_fi_t)
    dx3, df, dgu, acc_l = _ffn_out_loss(act, gu, x2, target, _vec_rows(ga2, g_final), w_fo_full)

    gw_fo, _ = _weight_grad(act, df, "wgrad_ffn_out", D_FF)
    gw_fi, _ = _weight_grad(dgu, h2, "wgrad_ffn_in", D_FF)
    blocks_fo = gw_fo.reshape(N_DEV, D_FF // N_DEV, D_MODEL)
    blocks_fi = gw_fi.reshape(N_DEV, 2 * D_FF // N_DEV, D_MODEL)
    (dx2, acc_f), (sib_fo, sib_fi) = _ffn_in_bwd(dgu, x2, dx3, _vec_rows(g_ffn, sc2), w_fi_t,
                                                 _sibling_rider([blocks_fo, blocks_fi]))
    sums_fo, mine_fo = _sibling_sum(_own_blocks(blocks_fo), sib_fo, "sibling_sum_ffn_out")
    sums_fi, mine_fi = _sibling_sum(_own_blocks(blocks_fi), sib_fi, "sibling_sum_ffn_in")
    (dout, dattn, drest, acc_m), (ici_fo, ici_fi) = _mix_bwd(
        dx2, merged, attn, z, _vec_rows(ga1, conv_full[0], conv_full[1], conv_full[2]), w_out_full,
        _chip_rider([sums_fo, sums_fi]))
    gw_out, _ = _weight_grad(merged, dout, "wgrad_out", D_MODEL)
    blocks_out = gw_out.reshape(N_DEV, D_MODEL // N_DEV, D_MODEL)
    (dq, dkv, dsink), (sib_out,) = _attn_bwd(z, dattn, attn, lse, sinks[0], _sibling_rider([blocks_out]))
    sums_out, mine_out = _sibling_sum(_own_blocks(blocks_out), sib_out, "sibling_sum_out")
    gw_in, (ici_out,) = _weight_grad(drest, h1, "wgrad_in_rest", IN_CHUNK, rows=IN_WIDTH, row0=REF_REST_COL,
                                     rider=_chip_rider([sums_out]))
    gw_in, _ = _weight_grad(dq, h1, "wgrad_in_q", D_MODEL, rows=IN_WIDTH, row0=0, into=gw_in)
    gw_in, _ = _weight_grad(dkv, h1, "wgrad_in_kv", 2 * KV_WIDTH, rows=IN_WIDTH, row0=REF_KV_COL, into=gw_in)
    blocks_in = gw_in.reshape(N_DEV, IN_WIDTH // N_DEV, D_MODEL)
    (sib_in,) = _carry(_sibling_rider([blocks_in]), "sibling_w_in")
    sums_in, mine_in = _sibling_sum(_own_blocks(blocks_in), sib_in, "sibling_sum_in")
    (grad_x, acc_i, db_in), (ici_in,) = _inproj_bwd(dq, drest, dkv, xs, dx2, _vec_rows(g_mix, sc1), w_in_t,
                                                    _chip_rider([sums_in]))

    pieces = [acc_i[0], acc_i[1], acc_m[0], acc_f[0], acc_f[1], acc_l[2],
              acc_i[2], db_in[0], acc_f[2], acc_l[1],
              acc_m[1], acc_m[2], acc_m[3], dsink[0], acc_l[0]]
    offsets = [0]
    for p in pieces:
        offsets.append(offsets[-1] + p.shape[0])
    packed = _small_allgather(_to_rows(jnp.concatenate(pieces)), "gather_small")
    dmod_all = packed.reshape(N_DEV, -1)[:, :N_MOD * D_MODEL]
    total = _sum_devices(packed).reshape(-1)
    part = lambda i: total[offsets[i]:offsets[i + 1]]
    g_b_ada = total[:N_MOD * D_MODEL].reshape(1, -1)
    g_g_mix, g_b_in, g_g_ffn, g_g_final = part(6).reshape(1, -1), part(7).reshape(1, -1), part(8).reshape(1, -1), part(9)
    g_conv_full = jnp.stack([part(10), part(11), part(12)])
    g_conv = lax.dynamic_slice_in_dim(g_conv_full, me * conv_cols, conv_cols, axis=1)[None]
    g_sinks = part(13)[:N_Q_HEADS].reshape(1, -1)
    loss = (0.5 / D_MODEL) * jnp.sum(part(14))
    dmod_cols = lax.dynamic_slice_in_dim(dmod_all, me * ada_cols, ada_cols, axis=1)
    g_w_ada = _ada_weight_grad(c_all, dmod_cols)

    def reduced(mine, ici, w, m, v, name, transposed=False):
        turn = jnp.transpose if transposed else (lambda a: a)
        return tuple(turn(o)[None] for o in _chip_sum_adamw(mine, ici, turn(w[0]), turn(m[0]), turn(v[0]), name))

    d_ada, nm_ada, nv_ada = _adamw(w_ada[0], g_w_ada, m_w_ada[0], v_w_ada[0], "adamw_w_ada")
    small_names = ["b_ada", "g_mix", "b_in", "sinks", "conv_w", "g_ffn", "g_final"]
    small_w = [b_ada, g_mix, b_in, sinks, conv_w, g_ffn, g_final]
    small_m = [m_b_ada, m_g_mix, m_b_in, m_sinks, m_conv_w, m_g_ffn, m_g_final]
    small_v = [v_b_ada, v_g_mix, v_b_in, v_sinks, v_conv_w, v_g_ffn, v_g_final]
    small_g = [g_b_ada, g_g_mix, g_b_in, g_sinks, g_conv, g_g_ffn, g_g_final]
    small_g = [g.reshape(w.shape) for g, w in zip(small_g, small_w)]
    flat = lambda arrs: _to_rows(jnp.concatenate([a.reshape(-1) for a in arrs]))
    sd, snm, snv = _adamw(flat(small_w), flat(small_g), flat(small_m), flat(small_v), "adamw_small")
    sizes = [w.size for w in small_w]
    starts = [sum(sizes[:i]) for i in range(len(sizes))]
    unflat = lambda a: {n: a.reshape(-1)[s:s + z_].reshape(w.shape)
                        for n, s, z_, w in zip(small_names, starts, sizes, small_w)}
    sd, snm, snv = unflat(sd), unflat(snm), unflat(snv)
    sg = dict(zip(small_names, small_g))

    res = {
        "w_ada": (g_w_ada[None], d_ada[None], nm_ada[None], nv_ada[None]),
        "w_in": reduced(mine_in, ici_in, w_in, m_w_in, v_w_in, "adamw_w_in", transposed=True),
        "w_out": reduced(mine_out, ici_out, w_out, m_w_out, v_w_out, "adamw_w_out"),
        "w_ffn_in": reduced(mine_fi, ici_fi, w_ffn_in, m_w_ffn_in, v_w_ffn_in, "adamw_w_ffn_in", transposed=True),
        "w_ffn_out": reduced(mine_fo, ici_fo, w_ffn_out, m_w_ffn_out, v_w_ffn_out, "adamw_w_ffn_out"),
    }
    for n in small_names:
        res[n] = (sg[n], sd[n], snm[n], snv[n])
    order = ["w_ada", "b_ada", "g_mix", "w_in", "b_in", "sinks", "conv_w", "w_out", "g_ffn", "w_ffn_in", "w_ffn_out",
             "g_final"]
    outs = [loss, grad_x[None]]
    for k in range(4):
        outs += [res[n][k] for n in order]
    return tuple(outs)
```

```python
import functools
import math

import jax
import jax.numpy as jnp
from jax import lax
from jax.experimental import pallas as pl
from jax.experimental.pallas import tpu as pltpu

F32 = jnp.float32
BF16 = jnp.bfloat16

D_MODEL = 1024
HEAD_DIM = 64
N_Q_HEADS = 16
N_KV_HEADS = 2
GROUP = 8
WINDOW = 128
KV_WIDTH = N_KV_HEADS * HEAD_DIM
D_FF = 2816
IN_WIDTH = 6400
N_MOD = 6
EPS = 1e-6
N_DEV = 8
REST_WIDTH = 5 * D_MODEL
KV_COL = D_MODEL + REST_WIDTH
ATTN_SCALE = HEAD_DIM ** -0.5

ADAM_LR = 0.001
ADAM_B1 = 0.9
ADAM_B2 = 0.999
ADAM_EPS = 1e-08
ADAM_WD = 0.01
ADAM_STEP = 10

LANES = 128
SUBLANES = 8
BF16_ROWS = 16
VMEM_LIMIT = 56 * 1024 * 1024
TOKEN_TILE = 512
FF_CHUNK = 256
WGRAD_VMEM = 40 * 1024 * 1024
MESH = pl.DeviceIdType.MESH
ANY = pl.BlockSpec(memory_space=pl.ANY)

NT_DIMS = (((1,), (1,)), ((), ()))
TN_DIMS = (((0,), (0,)), ((), ()))
CHIP_FLIPS = [(0, 0), (1, 0), (0, 1), (1, 1)]


def _full(shape):
    return pl.BlockSpec(shape, lambda *_: (0,) * len(shape))


def _my_place():
    return lax.axis_index("x"), lax.axis_index("y"), lax.axis_index("c")


def _flip(v, bit):
    return 1 - v if bit else v


def _sigmoid(v):
    return 1.0 / (1.0 + jnp.exp(-v))


class _Rider:
    def __init__(self, ins, out_shapes, sem_shapes, first=None, mid=None, last=None, ins_in_vmem=False):
        self.ins, self.out_shapes, self.sem_shapes = list(ins), list(out_shapes), list(sem_shapes)
        self.in_specs = [_full(a.shape) if ins_in_vmem else ANY for a in self.ins]
        self.hooks = [(when, fn) for when, fn in (("first", first), ("mid", mid), ("last", last)) if fn is not None]


def _call(body, name, grid, args, in_specs, out_shape, out_specs, scratch=(), rider=None, aliases=None):
    n_in, n_out, n_scr = len(args), len(out_shape), len(scratch)
    r_in = rider.ins if rider else []
    r_out = rider.out_shapes if rider else []
    r_sem = rider.sem_shapes if rider else []
    nsteps = math.prod(grid)

    def full_body(*refs):
        pos = 0
        groups = []
        for size in (n_in, len(r_in), n_out, len(r_out), n_scr, len(r_sem)):
            groups.append(refs[pos:pos + size])
            pos += size
        ins, rins, outs, routs, scr, rsems = groups
        step = pl.program_id(0)
        for axis in range(1, len(grid)):
            step = step * grid[axis] + pl.program_id(axis)
        at = {"first": 0, "mid": (3 * nsteps) // 4, "last": nsteps - 1}
        hooks = rider.hooks if rider else []
        for when, fn in hooks:
            if when != "last":
                pl.when(step == at[when])(functools.partial(fn, rins, routs, rsems))
        body(*ins, *outs, *scr)
        for when, fn in hooks:
            if when == "last":
                pl.when(step == at[when])(functools.partial(fn, rins, routs, rsems))

    outs = pl.pallas_call(
        full_body, name=name, grid=grid,
        out_shape=list(out_shape) + list(r_out),
        in_specs=list(in_specs) + (rider.in_specs if rider else []),
        out_specs=list(out_specs) + [ANY] * len(r_out),
        scratch_shapes=list(scratch) + list(r_sem),
        input_output_aliases=dict(aliases or {}),
        compiler_params=pltpu.CompilerParams(dimension_semantics=("arbitrary",) * len(grid),
                                             vmem_limit_bytes=VMEM_LIMIT),
    )(*args, *r_in)
    return list(outs[:n_out]), list(outs[n_out:])


def _gather_rider(shards):
    n = len(shards)

    def setup(outs, sems):
        x, y, c = _my_place()
        send_sems, recv_sems, _ = sems
        chips = [(1 - x, y), (x, 1 - y), (1 - x, 1 - y)]

        def block(w, place):
            return outs[w].at[4 * place[0] + 2 * place[1] + place[2]]

        def copy(w, k, place, to, src=None):
            return pltpu.make_async_remote_copy(
                src_ref=block(w, place) if src is None else src, dst_ref=block(w, place),
                send_sem=send_sems.at[w, k], recv_sem=recv_sems.at[w, k], device_id=to, device_id_type=MESH)

        return (x, y, c), (x, y, 1 - c), chips, block, copy

    def first(ins, outs, sems):
        me, sibling, chips, block, copy = setup(outs, sems)
        for w in range(n):
            pltpu.make_async_copy(ins[w], block(w, me), sems[2].at[w]).start()
            copy(w, 0, me, sibling, src=ins[w]).start()
            for j, chip in enumerate(chips):
                copy(w, 1 + j, me, (*chip, me[2]), src=ins[w]).start()

    def mid(ins, outs, sems):
        me, sibling, chips, block, copy = setup(outs, sems)
        for w in range(n):
            for j, chip in enumerate(chips):
                copy(w, 1 + j, (*chip, me[2]), me).wait_recv()
                copy(w, 4 + j, (*chip, me[2]), sibling).start()

    def last(ins, outs, sems):
        me, sibling, chips, block, copy = setup(outs, sems)
        for w in range(n):
            copy(w, 0, sibling, me).wait_recv()
            for j, chip in enumerate(chips):
                copy(w, 4 + j, (*chip, 1 - me[2]), me).wait_recv()
            copy(w, 0, me, sibling, src=ins[w]).wait_send()
            for j, chip in enumerate(chips):
                copy(w, 1 + j, me, (*chip, me[2]), src=ins[w]).wait_send()
                copy(w, 4 + j, (*chip, me[2]), sibling).wait_send()
            pltpu.make_async_copy(ins[w], block(w, me), sems[2].at[w]).wait()

    return _Rider(
        shards, [jax.ShapeDtypeStruct((N_DEV,) + s.shape, BF16) for s in shards],
        [pltpu.SemaphoreType.DMA((n, N_DEV - 1)), pltpu.SemaphoreType.DMA((n, N_DEV - 1)),
         pltpu.SemaphoreType.DMA((n,))],
        first=first, mid=mid, last=last, ins_in_vmem=True)


def _sibling_rider(gblocks):
    n = len(gblocks)

    def copies(ins, outs, sems):
        x, y, c = _my_place()
        send_sems, recv_sems = sems
        made = []
        for w in range(n):
            for f, (fx, fy) in enumerate(CHIP_FLIPS):
                chip = 4 * _flip(x, fx) + 2 * _flip(y, fy)
                made.append(pltpu.make_async_remote_copy(
                    src_ref=ins[w].at[chip + 1 - c], dst_ref=outs[w].at[f], send_sem=send_sems.at[w, f],
                    recv_sem=recv_sems.at[w, f], device_id=(x, y, 1 - c), device_id_type=MESH))
        return made

    def first(ins, outs, sems):
        for cp in copies(ins, outs, sems):
            cp.start()

    def last(ins, outs, sems):
        for cp in copies(ins, outs, sems):
            cp.wait_recv()
            cp.wait_send()

    return _Rider(gblocks, [jax.ShapeDtypeStruct((4,) + g.shape[1:], BF16) for g in gblocks],
                  [pltpu.SemaphoreType.DMA((n, 4))] * 2, first=first, last=last)


def _own_blocks(gblocks):
    x, y, c = _my_place()
    return jnp.stack([lax.dynamic_index_in_dim(gblocks, 4 * _flip(x, fx) + 2 * _flip(y, fy) + c, 0, keepdims=False)
                      for fx, fy in CHIP_FLIPS])


def _chip_rider(sums):
    n = len(sums)

    def copies(ins, outs, sems):
        x, y, c = _my_place()
        send_sems, recv_sems = sems
        made = []
        for w in range(n):
            for f in (1, 2, 3):
                fx, fy = CHIP_FLIPS[f]
                made.append(pltpu.make_async_remote_copy(
                    src_ref=ins[w].at[f - 1], dst_ref=outs[w].at[f - 1], send_sem=send_sems.at[w, f - 1],
                    recv_sem=recv_sems.at[w, f - 1], device_id=(_flip(x, fx), _flip(y, fy), c), device_id_type=MESH))
        return made

    def first(ins, outs, sems):
        for cp in copies(ins, outs, sems):
            cp.start()

    def last(ins, outs, sems):
        for cp in copies(ins, outs, sems):
            cp.wait_recv()
            cp.wait_send()

    return _Rider(sums, [jax.ShapeDtypeStruct(s.shape, BF16) for s in sums],
                  [pltpu.SemaphoreType.DMA((n, 3))] * 2, first=first, last=last)


def _push_to_all(v_ref, out_ref, send_sems, recv_sems, local_sem):
    x, y, c = _my_place()
    me = 4 * x + 2 * y + c
    mine = pltpu.make_async_copy(v_ref, out_ref.at[me], local_sem)
    mine.start()
    sends = []
    for k in range(1, N_DEV):
        px, py, pc = _flip(x, k & 4), _flip(y, k & 2), _flip(c, k & 1)
        cp = pltpu.make_async_remote_copy(
            src_ref=v_ref, dst_ref=out_ref.at[me], send_sem=send_sems.at[k - 1], recv_sem=recv_sems.at[k - 1],
            device_id=(px, py, pc), device_id_type=MESH)
        cp.start()
        sends.append(cp)
    for k in range(1, N_DEV):
        px, py, pc = _flip(x, k & 4), _flip(y, k & 2), _flip(c, k & 1)
        pltpu.make_async_remote_copy(
            src_ref=v_ref, dst_ref=out_ref.at[4 * px + 2 * py + pc], send_sem=send_sems.at[k - 1],
            recv_sem=recv_sems.at[k - 1], device_id=(px, py, pc), device_id_type=MESH).wait_recv()
    for cp in sends:
        cp.wait_send()
    mine.wait()


def _small_allgather(v, name):
    rows = v.shape[0]

    def body(v_ref, out_ref, send_sems, recv_sems, local_sem):
        _push_to_all(v_ref, out_ref, send_sems, recv_sems, local_sem)

    return pl.pallas_call(
        body, name=name,
        out_shape=jax.ShapeDtypeStruct((N_DEV, rows, LANES), F32),
        in_specs=[pl.BlockSpec(memory_space=pltpu.VMEM)],
        out_specs=pl.BlockSpec(memory_space=pltpu.VMEM),
        scratch_shapes=[pltpu.SemaphoreType.DMA((N_DEV - 1,)), pltpu.SemaphoreType.DMA((N_DEV - 1,)),
                        pltpu.SemaphoreType.DMA],
        compiler_params=pltpu.CompilerParams(vmem_limit_bytes=VMEM_LIMIT),
    )(v)


def _gather_first_weight(shard, others, cond_rows, w_ada, b_cols):
    n = len(others)
    ada_cols = w_ada.shape[1]
    c_rows = D_MODEL // LANES

    def body(*refs):
        w_ref, other_refs = refs[0], refs[1:1 + n]
        cond_ref, wada_ref, bcols_ref = refs[1 + n:4 + n]
        out_ref, cast_refs = refs[4 + n], refs[5 + n:5 + 2 * n]
        cond_all_ref, mod_all_ref = refs[5 + 2 * n:7 + 2 * n]
        mine_ref, mod_ref, send_sems, recv_sems, local_sem, small_send, small_recv, small_local = refs[7 + 2 * n:]
        x, y, c = _my_place()
        me, sibling = (x, y, c), (x, y, 1 - c)
        chips = [(1 - x, y), (x, 1 - y), (1 - x, 1 - y)]

        def block(place):
            return out_ref.at[4 * place[0] + 2 * place[1] + place[2]]

        def copy(k, place, to, src=None):
            return pltpu.make_async_remote_copy(
                src_ref=block(place) if src is None else src, dst_ref=block(place),
                send_sem=send_sems.at[k], recv_sem=recv_sems.at[k], device_id=to, device_id_type=MESH)

        mine_ref[...] = w_ref[...].astype(BF16)
        local = pltpu.make_async_copy(mine_ref, block(me), local_sem)
        local.start()
        started = [copy(0, me, sibling, src=mine_ref)]
        started += [copy(1 + j, me, (*chip, c), src=mine_ref) for j, chip in enumerate(chips)]
        for cp in started:
            cp.start()
        _push_to_all(cond_ref, cond_all_ref, small_send.at[0], small_recv.at[0], small_local.at[0])
        mod = jnp.zeros((N_DEV, ada_cols), F32) + bcols_ref[...]
        for r in range(c_rows):
            cf = cond_all_ref[:, r, :]
            act = (cf * _sigmoid(cf)).astype(BF16)
            mod = mod + jnp.dot(act, wada_ref[r * LANES:(r + 1) * LANES, :].astype(BF16),
                                preferred_element_type=F32)
        mod_ref[...] = mod
        _push_to_all(mod_ref, mod_all_ref, small_send.at[1], small_recv.at[1], small_local.at[1])
        for o_ref, c_ref in zip(other_refs, cast_refs):
            c_ref[...] = o_ref[...].astype(BF16)
        for j, chip in enumerate(chips):
            copy(1 + j, (*chip, c), me).wait_recv()
            passed = copy(4 + j, (*chip, c), sibling)
            passed.start()
            started.append(passed)
        copy(0, sibling, me).wait_recv()
        for j, chip in enumerate(chips):
            copy(4 + j, (*chip, 1 - c), me).wait_recv()
        for cp in started:
            cp.wait_send()
        local.wait()

    vmem = pl.BlockSpec(memory_space=pltpu.VMEM)
    outs = pl.pallas_call(
        body, name="gather_w_in",
        out_shape=[jax.ShapeDtypeStruct((N_DEV,) + shard.shape, BF16)]
        + [jax.ShapeDtypeStruct(o.shape, BF16) for o in others]
        + [jax.ShapeDtypeStruct((N_DEV,) + cond_rows.shape, F32), jax.ShapeDtypeStruct((N_DEV, N_DEV, ada_cols), F32)],
        in_specs=[vmem] * (4 + n),
        out_specs=[ANY] + [vmem] * (n + 2),
        scratch_shapes=[pltpu.VMEM(shard.shape, BF16), pltpu.VMEM((N_DEV, ada_cols), F32),
                        pltpu.SemaphoreType.DMA((N_DEV - 1,)), pltpu.SemaphoreType.DMA((N_DEV - 1,)),
                        pltpu.SemaphoreType.DMA,
                        pltpu.SemaphoreType.DMA((2, N_DEV - 1)), pltpu.SemaphoreType.DMA((2, N_DEV - 1)),
                        pltpu.SemaphoreType.DMA((2,))],
        compiler_params=pltpu.CompilerParams(vmem_limit_bytes=VMEM_LIMIT),
    )(shard, *others, cond_rows, w_ada, b_cols)
    return outs[0], list(outs[1:1 + n]), outs[1 + n], outs[2 + n]


def _carry(rider, name):
    def body(token_ref):
        token_ref[...] = jnp.zeros_like(token_ref)

    _, routs = _call(body, name, (1,), [], [], [jax.ShapeDtypeStruct((SUBLANES, LANES), F32)],
                     [_full((SUBLANES, LANES))], rider=rider)
    return routs


def _ada_weight_grad(c_all, dmod_cols):
    cols = dmod_cols.shape[1]

    def body(c_ref, d_ref, out_ref):
        cf = c_ref[...]
        act = (cf * _sigmoid(cf)).astype(BF16)
        out_ref[...] = lax.dot_general(act, d_ref[...].astype(BF16), TN_DIMS, preferred_element_type=F32)

    return pl.pallas_call(
        body, name="ada_weight_grad",
        out_shape=jax.ShapeDtypeStruct((D_MODEL, cols), F32),
        in_specs=[pl.BlockSpec(memory_space=pltpu.VMEM)] * 2,
        out_specs=pl.BlockSpec(memory_space=pltpu.VMEM),
        compiler_params=pltpu.CompilerParams(vmem_limit_bytes=VMEM_LIMIT),
    )(c_all, dmod_cols)


def _sum_devices(packed):
    def body(p_ref, out_ref):
        total = p_ref[0]
        for d in range(1, N_DEV):
            total = total + p_ref[d]
        out_ref[...] = total

    return pl.pallas_call(
        body, name="sum_devices",
        out_shape=jax.ShapeDtypeStruct(packed.shape[1:], F32),
        in_specs=[pl.BlockSpec(memory_space=pltpu.VMEM)],
        out_specs=pl.BlockSpec(memory_space=pltpu.VMEM),
        compiler_params=pltpu.CompilerParams(vmem_limit_bytes=VMEM_LIMIT),
    )(packed)


def _row_tile(rows, multiple):
    for cand in range(min(rows, 256), 0, -1):
        if rows % cand == 0 and cand % multiple == 0:
            return cand
    return rows


def _adamw_update(w, g, m, v):
    c1 = 1.0 / (1.0 - ADAM_B1 ** ADAM_STEP)
    c2 = 1.0 / (1.0 - ADAM_B2 ** ADAM_STEP)
    nm = ADAM_B1 * m + (1.0 - ADAM_B1) * g
    nv = ADAM_B2 * v + (1.0 - ADAM_B2) * (g * g)
    delta = -ADAM_LR * ((nm * c1) / (jnp.sqrt(nv * c2) + ADAM_EPS) + ADAM_WD * w)
    return delta, nm, nv


def _adamw(w, g, m, v, name):
    rows, cols = w.shape
    tile = _row_tile(rows, SUBLANES)

    def body(w_ref, g_ref, m_ref, v_ref, d_ref, nm_ref, nv_ref):
        d_ref[...], nm_ref[...], nv_ref[...] = _adamw_update(w_ref[...], g_ref[...], m_ref[...], v_ref[...])

    spec = pl.BlockSpec((tile, cols), lambda i: (i, 0))
    outs, _ = _call(body, name, (rows // tile,), [w, g, m, v], [spec] * 4,
                    [jax.ShapeDtypeStruct((rows, cols), F32)] * 3, [spec] * 3)
    return outs


def _sibling_sum(own, sib, name):
    _, r, cdim = own.shape
    tile = _row_tile(r, BF16_ROWS)

    def body(own_ref, sib_ref, sums_ref, mine_ref):
        mine_ref[...] = own_ref[0].astype(F32) + sib_ref[0].astype(F32)
        for f in (1, 2, 3):
            sums_ref[f - 1] = (own_ref[f].astype(F32) + sib_ref[f].astype(F32)).astype(BF16)

    outs, _ = _call(
        body, name, (r // tile,), [own, sib], [pl.BlockSpec((4, tile, cdim), lambda i: (0, i, 0))] * 2,
        [jax.ShapeDtypeStruct((3, r, cdim), BF16), jax.ShapeDtypeStruct((r, cdim), F32)],
        [pl.BlockSpec((3, tile, cdim), lambda i: (0, i, 0)), pl.BlockSpec((tile, cdim), lambda i: (i, 0))])
    return outs


def _chip_sum_adamw(mine, ici, w, m, v, name):
    r, cdim = mine.shape
    tile = _row_tile(r, BF16_ROWS)

    def body(mine_ref, ici_ref, w_ref, m_ref, v_ref, g_ref, d_ref, nm_ref, nv_ref):
        g = mine_ref[...]
        for f in range(3):
            g = g + ici_ref[f].astype(F32)
        g_ref[...] = g
        d_ref[...], nm_ref[...], nv_ref[...] = _adamw_update(w_ref[...], g, m_ref[...], v_ref[...])

    spec = pl.BlockSpec((tile, cdim), lambda i: (i, 0))
    outs, _ = _call(
        body, name, (r // tile,), [mine, ici, w, m, v],
        [spec, pl.BlockSpec((3, tile, cdim), lambda i: (0, i, 0)), spec, spec, spec],
        [jax.ShapeDtypeStruct((r, cdim), F32)] * 4, [spec] * 4)
    return outs


REF_KV_COL = D_MODEL
REF_REST_COL = D_MODEL + 2 * KV_WIDTH
IN_CHUNK = 1280
IN_PIECES = ([(0, 0, D_MODEL)]
             + [(D_MODEL + n * IN_CHUNK, REF_REST_COL + n * IN_CHUNK, IN_CHUNK) for n in range(REST_WIDTH // IN_CHUNK)]
             + [(KV_COL, REF_KV_COL, 2 * KV_WIDTH)])


def _inproj_fwd(x, vec, w_t, b_in, rider):
    t = x.shape[0]
    tm = min(TOKEN_TILE, t)

    def body(x_ref, vec_ref, w_ref, b_ref, z_ref, h_ref):
        xf = x_ref[...]
        r = lax.rsqrt(jnp.mean(xf * xf, axis=-1, keepdims=True) + EPS)
        h = (xf * r) * vec_ref[0:1, :] * (1.0 + vec_ref[1:2, :]) + vec_ref[2:3, :]
        hb = h.astype(BF16)
        h_ref[...] = hb
        for mine, ref, width in IN_PIECES:
            zc = lax.dot_general(hb, w_ref[ref:ref + width, :], NT_DIMS, preferred_element_type=F32)
            z_ref[:, mine:mine + width] = (zc + b_ref[:, ref:ref + width]).astype(BF16)

    return _call(
        body, "inproj_fwd", (t // tm,), [x, vec, w_t, b_in],
        [pl.BlockSpec((tm, D_MODEL), lambda i: (i, 0)), _full((SUBLANES, D_MODEL)),
         _full((IN_WIDTH, D_MODEL)), _full((1, IN_WIDTH))],
        [jax.ShapeDtypeStruct((t, IN_WIDTH), BF16), jax.ShapeDtypeStruct((t, D_MODEL), BF16)],
        [pl.BlockSpec((tm, IN_WIDTH), lambda i: (i, 0)), pl.BlockSpec((tm, D_MODEL), lambda i: (i, 0))],
        rider=rider)


def _window_mask(has_prev):
    qi = lax.broadcasted_iota(jnp.int32, (WINDOW, 2 * WINDOW), 0)
    kj = lax.broadcasted_iota(jnp.int32, (WINDOW, 2 * WINDOW), 1)
    off = jnp.where(has_prev, 0, 4 * WINDOW)
    in_prev = jnp.logical_and(kj < WINDOW, kj > qi + off)
    in_cur = jnp.logical_and(kj >= WINDOW, (kj - WINDOW) <= qi)
    return jnp.logical_or(in_prev, in_cur)


PAIRS = GROUP // 2
STACK = PAIRS * WINDOW


LOG2E = 1.4426950408889634
LN2 = 0.6931471805599453
SCORE_SCALE = ATTN_SCALE * LOG2E


def _fill_window_bias(bias_ref):
    shape = bias_ref.shape[1:]
    kj = lax.broadcasted_iota(jnp.int32, shape, 0)
    qi = jnp.bitwise_and(lax.broadcasted_iota(jnp.int32, shape, 1), WINDOW - 1)
    in_prev = jnp.logical_and(kj < WINDOW, kj > qi)
    in_cur = jnp.logical_and(kj >= WINDOW, (kj - WINDOW) <= qi)
    bias_ref[0] = jnp.where(in_cur, 0.0, -jnp.inf)
    bias_ref[1] = jnp.where(jnp.logical_or(in_prev, in_cur), 0.0, -jnp.inf)


def _half_tiles(tile):
    low = lax.broadcasted_iota(jnp.int32, tile.shape, 1) < HEAD_DIM
    swapped = jnp.concatenate([tile[:, HEAD_DIM:], tile[:, :HEAD_DIM]], axis=1)
    zero = jnp.zeros_like(tile)
    return ((jnp.where(low, tile, zero), jnp.where(low, zero, swapped)),
            (jnp.where(low, swapped, zero), jnp.where(low, zero, tile)))


def _stack_pairs(ref, row0, j):
    return jnp.concatenate(
        [ref[pl.ds(row0, WINDOW), (j * PAIRS + p) * LANES:(j * PAIRS + p + 1) * LANES] for p in range(PAIRS)], axis=0)


def _per_pair_row(values):
    pair = lax.broadcasted_iota(jnp.int32, (1, STACK), 1) // WINDOW
    row = jnp.full((1, STACK), values[PAIRS - 1], F32)
    for p in range(PAIRS - 2, -1, -1):
        row = jnp.where(pair == p, values[p], row)
    return row


def _attn_fwd(z, sinks, rider):
    t = z.shape[0]
    tq = min(TOKEN_TILE, t)
    nblk = tq // WINDOW

    def body(q_ref, kv_ref, sink_ref, o_ref, lse_ref, bias_ref):
        i = pl.program_id(0)

        @pl.when(i == 0)
        def _():
            _fill_window_bias(bias_ref)

        def one_block(b, carry):
            row0 = pl.multiple_of(b * WINDOW, WINDOW)
            start = i * tq + b * WINDOW
            prev = pl.multiple_of(jnp.maximum(start - WINDOW, 0), WINDOW)
            cur = pl.multiple_of(start, WINDOW)
            kvw = jnp.concatenate([kv_ref[pl.ds(prev, WINDOW), :], kv_ref[pl.ds(cur, WINDOW), :]], axis=0)
            k_halves = _half_tiles(kvw[:, :KV_WIDTH])
            v_halves = _half_tiles(kvw[:, KV_WIDTH:])
            bias = bias_ref[jnp.minimum(start, 1)]
            for j in range(N_KV_HEADS):
                for pr in range(PAIRS):
                    cols = slice((j * PAIRS + pr) * LANES, (j * PAIRS + pr + 1) * LANES)
                    qp = q_ref[pl.ds(row0, WINDOW), cols]
                    o_t = jnp.zeros((LANES, WINDOW), F32)
                    for parity in range(2):
                        h = j * GROUP + 2 * pr + parity
                        s = lax.dot_general(k_halves[j][parity], qp, NT_DIMS, preferred_element_type=F32)
                        s = s * SCORE_SCALE + bias
                        sink = sink_ref[h] * LOG2E
                        m = jnp.maximum(jnp.max(s, axis=0, keepdims=True), sink)
                        p = jnp.exp2(s - m)
                        denom = jnp.sum(p, axis=0, keepdims=True) + jnp.exp2(sink - m)
                        pv = lax.dot_general(v_halves[j][parity], p.astype(BF16), TN_DIMS,
                                             preferred_element_type=F32)
                        o_t = o_t + pv * (1.0 / denom)
                        lse_ref[h:h + 1, pl.ds(row0, WINDOW)] = m + jnp.log2(denom)
                    o_ref[pl.ds(row0, WINDOW), cols] = jnp.transpose(o_t.astype(BF16))
            return carry

        lax.fori_loop(0, nblk, one_block, 0)

    return _call(
        body, "attn_fwd", (t // tq,), [z, z, sinks],
        [pl.BlockSpec((tq, D_MODEL), lambda i: (i, 0)),
         pl.BlockSpec((t, 2 * KV_WIDTH), lambda i: (0, KV_COL // (2 * KV_WIDTH))),
         pl.BlockSpec(memory_space=pltpu.SMEM)],
        [jax.ShapeDtypeStruct((t, D_MODEL), BF16), jax.ShapeDtypeStruct((N_Q_HEADS, t), F32)],
        [pl.BlockSpec((tq, D_MODEL), lambda i: (i, 0)), pl.BlockSpec((N_Q_HEADS, tq), lambda i: (0, i))],
        scratch=[pltpu.VMEM((2, 2 * WINDOW, WINDOW), F32)], rider=rider)


HALO = BF16_ROWS


def _shift_down(u, uh, k):
    row = lax.broadcasted_iota(jnp.int32, u.shape, 0)
    out = pltpu.roll(u, k, 0)
    for j in range(k):
        out = jnp.where(row == j, uh[HALO - k + j:HALO - k + j + 1, :], out)
    return out


def _shift_up(u, nxt, k):
    n = u.shape[0]
    row = lax.broadcasted_iota(jnp.int32, u.shape, 0)
    out = pltpu.roll(u, n - k, 0)
    for j in range(k):
        out = jnp.where(row == n - k + j, nxt[j:j + 1, :], out)
    return out


def _conv_inputs(cc_ref, cx_ref, hc_ref, hx_ref, first_tile):
    cc = cc_ref[...].astype(F32)
    cx = cx_ref[...].astype(F32)
    u = cc * cx
    uh = jnp.where(first_tile, 0.0, hc_ref[...].astype(F32) * hx_ref[...].astype(F32))
    return cc, cx, u, _shift_down(u, uh, 1), _shift_down(u, uh, 2)


def _z_specs(tm, order):
    per_tile = tm // HALO
    cols = [pl.BlockSpec((tm, D_MODEL), functools.partial(lambda i, j: (order(i), j), j=j)) for j in range(1, 6)]
    halos = [pl.BlockSpec((HALO, D_MODEL),
                          functools.partial(lambda i, j: (jnp.maximum(order(i) * per_tile - 1, 0), j), j=j))
             for j in (2, 3)]
    return cols + halos


def _mix_fwd(x, attn, z, vec, w_out):
    t = x.shape[0]
    tm = min(TOKEN_TILE, t)

    def body(x_ref, a_ref, cb_ref, cc_ref, cx_ref, ga_ref, gc_ref, hc_ref, hx_ref, vec_ref, w_ref,
             m_ref, x2_ref, h2_ref, o_ref):
        i = pl.program_id(0)
        _, _, u, u1, u2 = _conv_inputs(cc_ref, cx_ref, hc_ref, hx_ref, i == 0)
        cv = vec_ref[4:5, :] * u2 + vec_ref[5:6, :] * u1 + vec_ref[6:7, :] * u
        conv = cb_ref[...].astype(F32) * cv
        merged = (_sigmoid(ga_ref[...].astype(F32)) * a_ref[...].astype(F32)
                  + _sigmoid(gc_ref[...].astype(F32)) * conv)
        mb = merged.astype(BF16)
        m_ref[...] = mb
        o = jnp.dot(mb, w_ref[...], preferred_element_type=F32)
        o_ref[...] = o.astype(BF16)
        x2 = x_ref[...] + vec_ref[0:1, :] * o
        x2_ref[...] = x2
        r = lax.rsqrt(jnp.mean(x2 * x2, axis=-1, keepdims=True) + EPS)
        h2 = (x2 * r) * vec_ref[1:2, :] * (1.0 + vec_ref[2:3, :]) + vec_ref[3:4, :]
        h2_ref[...] = h2.astype(BF16)

    tok = pl.BlockSpec((tm, D_MODEL), lambda i: (i, 0))
    outs, _ = _call(
        body, "mix_fwd", (t // tm,), [x, attn, z, z, z, z, z, z, z, vec, w_out],
        [tok, tok] + _z_specs(tm, lambda i: i) + [_full((SUBLANES, D_MODEL)), _full((D_MODEL, D_MODEL))],
        [jax.ShapeDtypeStruct((t, D_MODEL), BF16), jax.ShapeDtypeStruct((t, D_MODEL), F32),
         jax.ShapeDtypeStruct((t, D_MODEL), BF16), jax.ShapeDtypeStruct((t, D_MODEL), BF16)],
        [tok, tok, tok, tok])
    return outs


def _ffn_fwd(h2, w_t):
    t = h2.shape[0]
    tm = min(TOKEN_TILE, t)

    def body(h_ref, w_ref, gu_ref, a_ref):
        hb = h_ref[...]
        for n in range(D_FF // FF_CHUNK):
            lo, hi = n * FF_CHUNK, (n + 1) * FF_CHUNK
            g = lax.dot_general(hb, w_ref[lo:hi, :], NT_DIMS, preferred_element_type=F32)
            u = lax.dot_general(hb, w_ref[D_FF + lo:D_FF + hi, :], NT_DIMS, preferred_element_type=F32)
            sg = _sigmoid(g)
            silu = g * sg
            gu_ref[:, lo:hi] = (u * (sg * (1.0 + g * (1.0 - sg)))).astype(BF16)
            gu_ref[:, D_FF + lo:D_FF + hi] = silu.astype(BF16)
            a_ref[:, lo:hi] = (silu * u).astype(BF16)

    outs, _ = _call(
        body, "ffn_fwd", (t // tm,), [h2, w_t],
        [pl.BlockSpec((tm, D_MODEL), lambda i: (i, 0)), _full((2 * D_FF, D_MODEL))],
        [jax.ShapeDtypeStruct((t, 2 * D_FF), BF16), jax.ShapeDtypeStruct((t, D_FF), BF16)],
        [pl.BlockSpec((tm, 2 * D_FF), lambda i: (i, 0)), pl.BlockSpec((tm, D_FF), lambda i: (i, 0))])
    return outs


def _ffn_out_loss(a, gu, x2, target, vec, w_ffn_out):
    t = a.shape[0]
    tm = min(TOKEN_TILE, t)

    def body(a_ref, gu_ref, x2_ref, t_ref, vec_ref, w_ref, dx3_ref, df_ref, dgu_ref, acc_ref):
        @pl.when(pl.program_id(0) == 0)
        def _():
            acc_ref[...] = jnp.zeros_like(acc_ref)

        ga2 = vec_ref[0:1, :]
        gf = vec_ref[1:2, :]
        f = jnp.dot(a_ref[...], w_ref[...], preferred_element_type=F32)
        x3 = x2_ref[...] + ga2 * f
        r = lax.rsqrt(jnp.mean(x3 * x3, axis=-1, keepdims=True) + EPS)
        xn = x3 * r
        err = xn * gf - t_ref[...]
        dy = err * (1.0 / D_MODEL)
        dxn = dy * gf
        dx3 = r * (dxn - xn * jnp.mean(dxn * xn, axis=-1, keepdims=True))
        dx3_ref[...] = dx3
        acc_ref[0:1, :] += jnp.sum(err * err, axis=0, keepdims=True)
        acc_ref[1:2, :] += jnp.sum(dy * xn, axis=0, keepdims=True)
        acc_ref[2:3, :] += jnp.sum(dx3 * f, axis=0, keepdims=True)
        df = (dx3 * ga2).astype(BF16)
        df_ref[...] = df
        for n in range(D_FF // FF_CHUNK):
            lo, hi = n * FF_CHUNK, (n + 1) * FF_CHUNK
            da = lax.dot_general(df, w_ref[lo:hi, :], NT_DIMS, preferred_element_type=F32)
            dgu_ref[:, lo:hi] = (da * gu_ref[:, lo:hi].astype(F32)).astype(BF16)
            dgu_ref[:, D_FF + lo:D_FF + hi] = (da * gu_ref[:, D_FF + lo:D_FF + hi].astype(F32)).astype(BF16)

    tok = pl.BlockSpec((tm, D_MODEL), lambda i: (i, 0))
    outs, _ = _call(
        body, "ffn_out_loss", (t // tm,), [a, gu, x2, target, vec, w_ffn_out],
        [pl.BlockSpec((tm, D_FF), lambda i: (i, 0)), pl.BlockSpec((tm, 2 * D_FF), lambda i: (i, 0)),
         tok, tok, _full((SUBLANES, D_MODEL)), _full((D_FF, D_MODEL))],
        [jax.ShapeDtypeStruct((t, D_MODEL), F32), jax.ShapeDtypeStruct((t, D_MODEL), BF16),
         jax.ShapeDtypeStruct((t, 2 * D_FF), BF16), jax.ShapeDtypeStruct((SUBLANES, D_MODEL), F32)],
        [tok, tok, pl.BlockSpec((tm, 2 * D_FF), lambda i: (i, 0)), _full((SUBLANES, D_MODEL))])
    return outs


def _ffn_in_bwd(dgu, x2, dx3, vec, w_t, rider):
    t = x2.shape[0]
    tm = min(TOKEN_TILE, t)

    def body(dgu_ref, x2_ref, dx3_ref, vec_ref, wf_ref, dx2_ref, acc_ref):
        @pl.when(pl.program_id(0) == 0)
        def _():
            acc_ref[...] = jnp.zeros_like(acc_ref)

        gffn = vec_ref[0:1, :]
        sc2 = vec_ref[1:2, :]
        dh2 = jnp.dot(dgu_ref[...], wf_ref[...], preferred_element_type=F32)
        x2 = x2_ref[...]
        r = lax.rsqrt(jnp.mean(x2 * x2, axis=-1, keepdims=True) + EPS)
        xn = x2 * r
        acc_ref[0:1, :] += jnp.sum(dh2, axis=0, keepdims=True)
        acc_ref[1:2, :] += jnp.sum(dh2 * xn * gffn, axis=0, keepdims=True)
        acc_ref[2:3, :] += jnp.sum(dh2 * xn * (1.0 + sc2), axis=0, keepdims=True)
        dxn = dh2 * gffn * (1.0 + sc2)
        dx2_ref[...] = dx3_ref[...] + r * (dxn - xn * jnp.mean(dxn * xn, axis=-1, keepdims=True))

    tok = pl.BlockSpec((tm, D_MODEL), lambda i: (i, 0))
    return _call(
        body, "ffn_in_bwd", (t // tm,), [dgu, x2, dx3, vec, w_t],
        [pl.BlockSpec((tm, 2 * D_FF), lambda i: (i, 0)), tok, tok, _full((SUBLANES, D_MODEL)),
         _full((2 * D_FF, D_MODEL))],
        [jax.ShapeDtypeStruct((t, D_MODEL), F32), jax.ShapeDtypeStruct((SUBLANES, D_MODEL), F32)],
        [tok, _full((SUBLANES, D_MODEL))], rider=rider)


def _mix_bwd(dx2, oproj, attn, z, vec, w_out, rider):
    t = dx2.shape[0]
    tm = min(TOKEN_TILE, t)
    nt = t // tm
    rev = lambda i: nt - 1 - i

    def body(dx2_ref, m_ref, a_ref, cb_ref, cc_ref, cx_ref, ga_ref, gc_ref, hc_ref, hx_ref,
             vec_ref, wo_ref, do_ref, da_ref, dr_ref, acc_ref, carry_ref):
        i = pl.program_id(0)

        @pl.when(i == 0)
        def _():
            acc_ref[...] = jnp.zeros_like(acc_ref)
            carry_ref[...] = jnp.zeros_like(carry_ref)

        ga1 = vec_ref[0:1, :]
        w0, w1, w2 = vec_ref[1:2, :], vec_ref[2:3, :], vec_ref[3:4, :]
        dx2 = dx2_ref[...]
        acc_ref[0:1, :] += jnp.sum(dx2 * m_ref[...].astype(F32), axis=0, keepdims=True)
        do = (dx2 * ga1).astype(BF16)
        do_ref[...] = do
        dm = lax.dot_general(do, wo_ref[...], NT_DIMS, preferred_element_type=F32)

        cc, cx, u, u1, u2 = _conv_inputs(cc_ref, cx_ref, hc_ref, hx_ref, i == nt - 1)
        cv = w0 * u2 + w1 * u1 + w2 * u
        cb = cb_ref[...].astype(F32)
        sa = _sigmoid(ga_ref[...].astype(F32))
        sc = _sigmoid(gc_ref[...].astype(F32))
        attn = a_ref[...].astype(F32)
        da_ref[...] = (dm * sa).astype(BF16)
        dconv = dm * sc
        dr_ref[:, 3 * D_MODEL:4 * D_MODEL] = (dm * attn * sa * (1.0 - sa)).astype(BF16)
        dr_ref[:, 4 * D_MODEL:5 * D_MODEL] = (dconv * (cb * cv) * (1.0 - sc)).astype(BF16)
        dr_ref[:, 0:D_MODEL] = (dconv * cv).astype(BF16)
        dcv = dconv * cb
        acc_ref[1:2, :] += jnp.sum(dcv * u2, axis=0, keepdims=True)
        acc_ref[2:3, :] += jnp.sum(dcv * u1, axis=0, keepdims=True)
        acc_ref[3:4, :] += jnp.sum(dcv * u, axis=0, keepdims=True)
        nxt = carry_ref[...]
        du = w2 * dcv + w1 * _shift_up(dcv, nxt, 1) + w0 * _shift_up(dcv, nxt, 2)
        carry_ref[...] = dcv[0:SUBLANES, :]
        dr_ref[:, D_MODEL:2 * D_MODEL] = (du * cx).astype(BF16)
        dr_ref[:, 2 * D_MODEL:3 * D_MODEL] = (du * cc).astype(BF16)

    tok = pl.BlockSpec((tm, D_MODEL), lambda i: (rev(i), 0))
    return _call(
        body, "mix_bwd", (nt,), [dx2, oproj, attn, z, z, z, z, z, z, z, vec, w_out],
        [tok, tok, tok] + _z_specs(tm, rev) + [_full((SUBLANES, D_MODEL)), _full((D_MODEL, D_MODEL))],
        [jax.ShapeDtypeStruct((t, D_MODEL), BF16), jax.ShapeDtypeStruct((t, D_MODEL), BF16),
         jax.ShapeDtypeStruct((t, REST_WIDTH), BF16), jax.ShapeDtypeStruct((SUBLANES, D_MODEL), F32)],
        [tok, tok, pl.BlockSpec((tm, REST_WIDTH), lambda i: (rev(i), 0)), _full((SUBLANES, D_MODEL))],
        scratch=[pltpu.VMEM((SUBLANES, D_MODEL), F32)], rider=rider)


def _attn_bwd(z, dattn, attn, lse, sinks, rider):
    t = z.shape[0]
    tq = min(TOKEN_TILE, t)
    nblk = tq // WINDOW
    nt = t // tq

    def body(q_ref, kv_ref, do_ref, o_ref, lse_ref, sink_ref, dq_ref, dkv_ref, ds_ref, acc_ref, bias_ref):
        i = pl.program_id(0)

        @pl.when(i == 0)
        def _():
            acc_ref[...] = jnp.zeros_like(acc_ref)
            ds_ref[...] = jnp.zeros_like(ds_ref)
            _fill_window_bias(bias_ref)

        lane = lax.broadcasted_iota(jnp.int32, (1, LANES), 1)
        ind_row = lax.broadcasted_iota(jnp.int32, (SUBLANES, LANES), 0)
        ind_low = lax.broadcasted_iota(jnp.int32, (SUBLANES, LANES), 1) < HEAD_DIM
        indicator = jnp.where(jnp.logical_or(jnp.logical_and(ind_row == 0, ind_low),
                                             jnp.logical_and(ind_row == 1, jnp.logical_not(ind_low))),
                              1.0, 0.0).astype(BF16)
        low = lax.broadcasted_iota(jnp.int32, (2 * WINDOW, LANES), 1) < HEAD_DIM

        def both_heads(even, odd):
            picked = jnp.where(low, even, odd)
            return picked + jnp.concatenate([picked[:, HEAD_DIM:], picked[:, :HEAD_DIM]], axis=1)

        def one_block(b, dsink):
            row0 = pl.multiple_of(b * WINDOW, WINDOW)
            start = i * tq + b * WINDOW
            prev = pl.multiple_of(jnp.maximum(start - WINDOW, 0), WINDOW)
            cur = pl.multiple_of(start, WINDOW)
            kvw = jnp.concatenate([kv_ref[pl.ds(prev, WINDOW), :], kv_ref[pl.ds(cur, WINDOW), :]], axis=0)
            k_halves = _half_tiles(kvw[:, :KV_WIDTH])
            v_halves = _half_tiles(kvw[:, KV_WIDTH:])
            bias = bias_ref[jnp.minimum(start, 1)]
            dk_groups, dv_groups = [], []
            for j in range(N_KV_HEADS):
                qst = _stack_pairs(q_ref, row0, j)
                dost = _stack_pairs(do_ref, row0, j)
                prod = dost.astype(F32) * _stack_pairs(o_ref, row0, j).astype(F32)
                prod_hi = prod.astype(BF16)
                prod_lo = (prod - prod_hi.astype(F32)).astype(BF16)
                deltas = (lax.dot_general(indicator, prod_hi, NT_DIMS, preferred_element_type=F32)
                          + lax.dot_general(indicator, prod_lo, NT_DIMS, preferred_element_type=F32))
                dq_t = jnp.zeros((LANES, STACK), F32)
                dk_par, dv_par = [], []
                for parity in range(2):
                    heads = [j * GROUP + 2 * p + parity for p in range(PAIRS)]
                    kk, vv = k_halves[j][parity], v_halves[j][parity]
                    s = lax.dot_general(kk, qst, NT_DIMS, preferred_element_type=F32) * SCORE_SCALE + bias
                    lse = jnp.concatenate([lse_ref[h:h + 1, pl.ds(row0, WINDOW)] for h in heads], axis=1)
                    p = jnp.exp2(s - lse)
                    dp = lax.dot_general(vv, dost, NT_DIMS, preferred_element_type=F32)
                    delta = deltas[parity:parity + 1, :]
                    dsb = (p * (dp - delta)).astype(BF16)
                    dq_t = dq_t + lax.dot_general(kk, dsb, TN_DIMS, preferred_element_type=F32)
                    dk_par.append(jnp.dot(dsb, qst, preferred_element_type=F32))
                    dv_par.append(jnp.dot(p.astype(BF16), dost, preferred_element_type=F32))
                    sink = _per_pair_row([sink_ref[h] * LOG2E for h in heads])
                    weighted = jnp.exp2(sink - lse) * delta
                    for pr, h in enumerate(heads):
                        dsink = dsink - jnp.where(lane == h, jnp.sum(weighted[:, pr * WINDOW:(pr + 1) * WINDOW]), 0.0)
                dq_st = jnp.transpose((dq_t * ATTN_SCALE).astype(BF16))
                for pr in range(PAIRS):
                    dq_ref[pl.ds(row0, WINDOW), (j * PAIRS + pr) * LANES:(j * PAIRS + pr + 1) * LANES] = (
                        dq_st[pr * WINDOW:(pr + 1) * WINDOW, :])
                dk_groups.append(both_heads(dk_par[0], dk_par[1]))
                dv_groups.append(both_heads(dv_par[0], dv_par[1]))
            blk = jnp.concatenate([jnp.where(low, dk_groups[0], dk_groups[1]) * ATTN_SCALE,
                                   jnp.where(low, dv_groups[0], dv_groups[1])], axis=1)
            acc_ref[pl.ds(prev, WINDOW), :] += blk[:WINDOW, :]
            acc_ref[pl.ds(cur, WINDOW), :] += blk[WINDOW:, :]
            return dsink

        dsink = lax.fori_loop(0, nblk, one_block, jnp.zeros((1, LANES), F32))
        ds_ref[0:1, :] += dsink

        @pl.when(i == nt - 1)
        def _():
            dkv_ref[...] = acc_ref[...].astype(BF16)

    tok = pl.BlockSpec((tq, D_MODEL), lambda i: (i, 0))
    return _call(
        body, "attn_bwd", (nt,), [z, z, dattn, attn, lse, sinks],
        [tok, pl.BlockSpec((t, 2 * KV_WIDTH), lambda i: (0, KV_COL // (2 * KV_WIDTH))), tok, tok,
         pl.BlockSpec((N_Q_HEADS, tq), lambda i: (0, i)), pl.BlockSpec(memory_space=pltpu.SMEM)],
        [jax.ShapeDtypeStruct((t, D_MODEL), BF16), jax.ShapeDtypeStruct((t, 2 * KV_WIDTH), BF16),
         jax.ShapeDtypeStruct((SUBLANES, LANES), F32)],
        [tok, _full((t, 2 * KV_WIDTH)), _full((SUBLANES, LANES))],
        scratch=[pltpu.VMEM((t, 2 * KV_WIDTH), F32), pltpu.VMEM((2, 2 * WINDOW, STACK), F32)], rider=rider)


def _inproj_bwd(dq, drest, dkv, x, dx2, vec, w_t, rider):
    t = x.shape[0]
    tm = min(TOKEN_TILE, t)

    def body(dq_ref, dr_ref, dkv_ref, x_ref, dx2_ref, vec_ref, w_ref, gx_ref, acc_ref, db_ref):
        @pl.when(pl.program_id(0) == 0)
        def _():
            acc_ref[...] = jnp.zeros_like(acc_ref)
            db_ref[...] = jnp.zeros_like(db_ref)

        g = vec_ref[0:1, :]
        sc1 = vec_ref[1:2, :]
        dqb, drb, dkvb = dq_ref[...], dr_ref[...], dkv_ref[...]
        dh = jnp.dot(dqb, w_ref[:REF_KV_COL, :], preferred_element_type=F32)
        dh = dh + jnp.dot(drb, w_ref[REF_REST_COL:, :], preferred_element_type=F32)
        dh = dh + jnp.dot(dkvb, w_ref[REF_KV_COL:REF_REST_COL, :], preferred_element_type=F32)
        db_ref[:, :REF_KV_COL] += jnp.sum(dqb.astype(F32), axis=0, keepdims=True)
        db_ref[:, REF_REST_COL:] += jnp.sum(drb.astype(F32), axis=0, keepdims=True)
        db_ref[:, REF_KV_COL:REF_REST_COL] += jnp.sum(dkvb.astype(F32), axis=0, keepdims=True)
        xf = x_ref[...]
        r = lax.rsqrt(jnp.mean(xf * xf, axis=-1, keepdims=True) + EPS)
        xn = xf * r
        acc_ref[0:1, :] += jnp.sum(dh, axis=0, keepdims=True)
        acc_ref[1:2, :] += jnp.sum(dh * xn * g, axis=0, keepdims=True)
        acc_ref[2:3, :] += jnp.sum(dh * xn * (1.0 + sc1), axis=0, keepdims=True)
        dxn = dh * g * (1.0 + sc1)
        gx_ref[...] = dx2_ref[...] + r * (dxn - xn * jnp.mean(dxn * xn, axis=-1, keepdims=True))

    tok = pl.BlockSpec((tm, D_MODEL), lambda i: (i, 0))
    return _call(
        body, "inproj_bwd", (t // tm,), [dq, drest, dkv, x, dx2, vec, w_t],
        [tok, pl.BlockSpec((tm, REST_WIDTH), lambda i: (i, 0)),
         pl.BlockSpec((tm, 2 * KV_WIDTH), lambda i: (i, 0)), tok, tok,
         _full((SUBLANES, D_MODEL)), _full((IN_WIDTH, D_MODEL))],
        [jax.ShapeDtypeStruct((t, D_MODEL), F32), jax.ShapeDtypeStruct((SUBLANES, D_MODEL), F32),
         jax.ShapeDtypeStruct((1, IN_WIDTH), F32)],
        [tok, _full((SUBLANES, D_MODEL)), _full((1, IN_WIDTH))], rider=rider)


def _weight_grad(b, a, name, bn, rows=None, row0=0, into=None, rider=None):
    t, n = b.shape
    m = a.shape[1]
    rows = n if rows is None else rows
    tk = min(TOKEN_TILE, t)
    for cand in (4 * TOKEN_TILE, 2 * TOKEN_TILE):
        if t % cand == 0 and 2 * cand * (bn + m) * 2 + bn * m * 4 <= WGRAD_VMEM:
            tk = cand
            break
    nk = t // tk
    block0 = row0 // bn

    def body(b_ref, a_ref, *rest):
        out_ref, acc_ref = rest[-2:]
        k = pl.program_id(1)

        @pl.when(k == 0)
        def _():
            acc_ref[...] = jnp.zeros_like(acc_ref)

        acc_ref[...] += lax.dot_general(b_ref[...], a_ref[...], TN_DIMS, preferred_element_type=F32)

        @pl.when(k == nk - 1)
        def _():
            out_ref[...] = acc_ref[...].astype(BF16)

    outs, routs = _call(
        body, name, (n // bn, nk), [b, a] + ([] if into is None else [into]),
        [pl.BlockSpec((tk, bn), lambda j, k: (k, j)), pl.BlockSpec((tk, m), lambda j, k: (k, 0))]
        + ([] if into is None else [ANY]),
        [jax.ShapeDtypeStruct((rows, m), BF16)], [pl.BlockSpec((bn, m), lambda j, k: (block0 + j, 0))],
        scratch=[pltpu.VMEM((bn, m), F32)], rider=rider, aliases=None if into is None else {2: 0})
    return outs[0], routs


def _to_rows(v):
    n = v.shape[0]
    padded = -(-n // (SUBLANES * LANES)) * SUBLANES * LANES
    return jnp.pad(v, (0, padded - n)).reshape(padded // LANES, LANES)


def _vec_rows(*rows):
    stacked = jnp.concatenate([r.reshape(1, D_MODEL) for r in rows], axis=0)
    return jnp.pad(stacked, ((0, SUBLANES - len(rows)), (0, 0)))


def kernel(x, c, w_ada, b_ada, g_mix, w_in, b_in, sinks, conv_w, w_out, g_ffn, w_ffn_in, w_ffn_out, g_final, loss_target, m_w_ada, m_b_ada, m_g_mix, m_w_in, m_b_in, m_sinks, m_conv_w, m_w_out, m_g_ffn, m_w_ffn_in, m_w_ffn_out, m_g_final, v_w_ada, v_b_ada, v_g_mix, v_w_in, v_b_in, v_sinks, v_conv_w, v_w_out, v_g_ffn, v_w_ffn_in, v_w_ffn_out, v_g_final):
    ix, iy, ic = _my_place()
    me = 4 * ix + 2 * iy + ic
    xs = x[0]
    target = loss_target[0]
    ada_cols = w_ada.shape[2]
    conv_cols = conv_w.shape[2]

    wt_in, wt_fi = jnp.transpose(w_in[0]), jnp.transpose(w_ffn_in[0])
    b_cols = lax.dynamic_slice_in_dim(b_ada, me * ada_cols, ada_cols, axis=1)
    g_in, (cast_fi, cast_out, cast_fo), first, mod_all = _gather_first_weight(
        wt_in, [wt_fi, w_out[0], w_ffn_out[0]], _to_rows(jnp.concatenate([c[0], conv_w[0].reshape(-1)])),
        w_ada[0], b_cols)
    first = first.reshape(N_DEV, -1)
    c_all = first[:, :D_MODEL]
    conv_full = jnp.transpose(first[:, D_MODEL:D_MODEL + 3 * conv_cols].reshape(N_DEV, 3, conv_cols), (1, 0, 2))
    conv_full = conv_full.reshape(3, D_MODEL)
    mod = lax.dynamic_index_in_dim(mod_all, me, axis=1, keepdims=False).reshape(N_MOD, D_MODEL)
    sh1, sc1, ga1, sh2, sc2, ga2 = [mod[i:i + 1] for i in range(N_MOD)]
    w_in_t = g_in.reshape(IN_WIDTH, D_MODEL)
    (z, h1), (g_fi, g_out) = _inproj_fwd(xs, _vec_rows(g_mix, sc1, sh1), w_in_t, b_in,
                                         _gather_rider([cast_fi, cast_out]))
    w_fi_t = g_fi.reshape(2 * D_FF, D_MODEL)
    w_out_full = g_out.reshape(D_MODEL, D_MODEL)
    (attn, lse), (g_fo,) = _attn_fwd(z, sinks[0], _gather_rider([cast_fo]))
    w_fo_full = g_fo.reshape(D_FF, D_MODEL)
    merged, x2, h2, oproj = _mix_fwd(
        xs, attn, z, _vec_rows(ga1, g_ffn, sc2, sh2, conv_full[0], conv_full[1], conv_full[2]), w_out_full)
    gu, act = _ffn_fwd(h2, w_fi_t)
    dx3, df, dgu, acc_l = _ffn_out_loss(act, gu, x2, target, _vec_rows(ga2, g_final), w_fo_full)

    gw_fo, _ = _weight_grad(act, df, "wgrad_ffn_out", D_FF)
    gw_fi, _ = _weight_grad(dgu, h2, "wgrad_ffn_in", D_FF)
    blocks_fo = gw_fo.reshape(N_DEV, D_FF // N_DEV, D_MODEL)
    blocks_fi = gw_fi.reshape(N_DEV, 2 * D_FF // N_DEV, D_MODEL)
    (dx2, acc_f), (sib_fo, sib_fi) = _ffn_in_bwd(dgu, x2, dx3, _vec_rows(g_ffn, sc2), w_fi_t,
                                                 _sibling_rider([blocks_fo, blocks_fi]))
    sums_fo, mine_fo = _sibling_sum(_own_blocks(blocks_fo), sib_fo, "sibling_sum_ffn_out")
    sums_fi, mine_fi = _sibling_sum(_own_blocks(blocks_fi), sib_fi, "sibling_sum_ffn_in")
    (dout, dattn, drest, acc_m), (ici_fo, ici_fi) = _mix_bwd(
        dx2, oproj, attn, z, _vec_rows(ga1, conv_full[0], conv_full[1], conv_full[2]), w_out_full,
        _chip_rider([sums_fo, sums_fi]))
    gw_out, _ = _weight_grad(merged, dout, "wgrad_out", D_MODEL)
    blocks_out = gw_out.reshape(N_DEV, D_MODEL // N_DEV, D_MODEL)
    (dq, dkv, dsink), (sib_out,) = _attn_bwd(z, dattn, attn, lse, sinks[0], _sibling_rider([blocks_out]))
    sums_out, mine_out = _sibling_sum(_own_blocks(blocks_out), sib_out, "sibling_sum_out")
    gw_in, (ici_out,) = _weight_grad(drest, h1, "wgrad_in_rest", IN_CHUNK, rows=IN_WIDTH, row0=REF_REST_COL,
                                     rider=_chip_rider([sums_out]))
    gw_in, _ = _weight_grad(dq, h1, "wgrad_in_q", D_MODEL, rows=IN_WIDTH, row0=0, into=gw_in)
    gw_in, _ = _weight_grad(dkv, h1, "wgrad_in_kv", 2 * KV_WIDTH, rows=IN_WIDTH, row0=REF_KV_COL, into=gw_in)
    blocks_in = gw_in.reshape(N_DEV, IN_WIDTH // N_DEV, D_MODEL)
    (sib_in,) = _carry(_sibling_rider([blocks_in]), "sibling_w_in")
    sums_in, mine_in = _sibling_sum(_own_blocks(blocks_in), sib_in, "sibling_sum_in")
    (grad_x, acc_i, db_in), (ici_in,) = _inproj_bwd(dq, drest, dkv, xs, dx2, _vec_rows(g_mix, sc1), w_in_t,
                                                    _chip_rider([sums_in]))

    pieces = [acc_i[0], acc_i[1], acc_m[0], acc_f[0], acc_f[1], acc_l[2],
              acc_i[2], db_in[0], acc_f[2], acc_l[1],
              acc_m[1], acc_m[2], acc_m[3], dsink[0], acc_l[0]]
    offsets = [0]
    for p in pieces:
        offsets.append(offsets[-1] + p.shape[0])
    packed = _small_allgather(_to_rows(jnp.concatenate(pieces)), "gather_small")
    dmod_all = packed.reshape(N_DEV, -1)[:, :N_MOD * D_MODEL]
    total = _sum_devices(packed).reshape(-1)
    part = lambda i: total[offsets[i]:offsets[i + 1]]
    g_b_ada = total[:N_MOD * D_MODEL].reshape(1, -1)
    g_g_mix, g_b_in, g_g_ffn, g_g_final = part(6).reshape(1, -1), part(7).reshape(1, -1), part(8).reshape(1, -1), part(9)
    g_conv_full = jnp.stack([part(10), part(11), part(12)])
    g_conv = lax.dynamic_slice_in_dim(g_conv_full, me * conv_cols, conv_cols, axis=1)[None]
    g_sinks = part(13)[:N_Q_HEADS].reshape(1, -1)
    loss = (0.5 / D_MODEL) * jnp.sum(part(14))
    dmod_cols = lax.dynamic_slice_in_dim(dmod_all, me * ada_cols, ada_cols, axis=1)
    g_w_ada = _ada_weight_grad(c_all, dmod_cols)

    def reduced(mine, ici, w, m, v, name, transposed=False):
        turn = jnp.transpose if transposed else (lambda a: a)
        return tuple(turn(o)[None] for o in _chip_sum_adamw(mine, ici, turn(w[0]), turn(m[0]), turn(v[0]), name))

    d_ada, nm_ada, nv_ada = _adamw(w_ada[0], g_w_ada, m_w_ada[0], v_w_ada[0], "adamw_w_ada")
    small_names = ["b_ada", "g_mix", "b_in", "sinks", "conv_w", "g_ffn", "g_final"]
    small_w = [b_ada, g_mix, b_in, sinks, conv_w, g_ffn, g_final]
    small_m = [m_b_ada, m_g_mix, m_b_in, m_sinks, m_conv_w, m_g_ffn, m_g_final]
    small_v = [v_b_ada, v_g_mix, v_b_in, v_sinks, v_conv_w, v_g_ffn, v_g_final]
    small_g = [g_b_ada, g_g_mix, g_b_in, g_sinks, g_conv, g_g_ffn, g_g_final]
    small_g = [g.reshape(w.shape) for g, w in zip(small_g, small_w)]
    flat = lambda arrs: _to_rows(jnp.concatenate([a.reshape(-1) for a in arrs]))
    sd, snm, snv = _adamw(flat(small_w), flat(small_g), flat(small_m), flat(small_v), "adamw_small")
    sizes = [w.size for w in small_w]
    starts = [sum(sizes[:i]) for i in range(len(sizes))]
    unflat = lambda a: {n: a.reshape(-1)[s:s + z_].reshape(w.shape)
                        for n, s, z_, w in zip(small_names, starts, sizes, small_w)}
    sd, snm, snv = unflat(sd), unflat(snm), unflat(snv)
    sg = dict(zip(small_names, small_g))

    res = {
        "w_ada": (g_w_ada[None], d_ada[None], nm_ada[None], nv_ada[None]),
        "w_in": reduced(mine_in, ici_in, w_in, m_w_in, v_w_in, "adamw_w_in", transposed=True),
        "w_out": reduced(mine_out, ici_out, w_out, m_w_out, v_w_out, "adamw_w_out"),
        "w_ffn_in": reduced(mine_fi, ici_fi, w_ffn_in, m_w_ffn_in, v_w_ffn_in, "adamw_w_ffn_in", transposed=True),
        "w_ffn_out": reduced(mine_fo, ici_fo, w_ffn_out, m_w_ffn_out, v_w_ffn_out, "adamw_w_ffn_out"),
    }
    for n in small_names:
        res[n] = (sg[n], sd[n], snm[n], snv[n])
    order = ["w_ada", "b_ada", "g_mix", "w_in", "b_in", "sinks", "conv_w", "w_out", "g_ffn", "w_ffn_in", "w_ffn_out",
             "g_final"]
    outs = [loss, grad_x[None]]
    for k in range(4):
        outs += [res[n][k] for n in order]
    return tuple(outs)
```

```python
import functools
import math

import jax
import jax.numpy as jnp
from jax import lax
from jax.experimental import pallas as pl
from jax.experimental.pallas import tpu as pltpu

F32 = jnp.float32
BF16 = jnp.bfloat16
GRAD_STREAM = BF16

D_MODEL = 1024
HEAD_DIM = 64
N_Q_HEADS = 16
N_KV_HEADS = 2
GROUP = 8
WINDOW = 128
KV_WIDTH = N_KV_HEADS * HEAD_DIM
D_FF = 2816
IN_WIDTH = 6400
N_MOD = 6
EPS = 1e-6
N_DEV = 8
REST_WIDTH = 5 * D_MODEL
KV_COL = D_MODEL + REST_WIDTH
ATTN_SCALE = HEAD_DIM ** -0.5

ADAM_LR = 0.001
ADAM_B1 = 0.9
ADAM_B2 = 0.999
ADAM_EPS = 1e-08
ADAM_WD = 0.01
ADAM_STEP = 10

LANES = 128
SUBLANES = 8
BF16_ROWS = 16
VMEM_LIMIT = 56 * 1024 * 1024
TOKEN_TILE = 512
FF_CHUNK = 256
WGRAD_VMEM = 40 * 1024 * 1024
MESH = pl.DeviceIdType.MESH
ANY = pl.BlockSpec(memory_space=pl.ANY)

NT_DIMS = (((1,), (1,)), ((), ()))
TN_DIMS = (((0,), (0,)), ((), ()))
CHIP_FLIPS = [(0, 0), (1, 0), (0, 1), (1, 1)]


def _full(shape):
    return pl.BlockSpec(shape, lambda *_: (0,) * len(shape))


def _my_place():
    return lax.axis_index("x"), lax.axis_index("y"), lax.axis_index("c")


def _flip(v, bit):
    return 1 - v if bit else v


def _sigmoid(v):
    return 1.0 / (1.0 + jnp.exp(-v))


class _Rider:
    def __init__(self, ins, out_shapes, sem_shapes, first=None, mid=None, last=None, ins_in_vmem=False):
        self.ins, self.out_shapes, self.sem_shapes = list(ins), list(out_shapes), list(sem_shapes)
        self.in_specs = [_full(a.shape) if ins_in_vmem else ANY for a in self.ins]
        self.hooks = [(when, fn) for when, fn in (("first", first), ("mid", mid), ("last", last)) if fn is not None]


def _call(body, name, grid, args, in_specs, out_shape, out_specs, scratch=(), rider=None, aliases=None):
    n_in, n_out, n_scr = len(args), len(out_shape), len(scratch)
    r_in = rider.ins if rider else []
    r_out = rider.out_shapes if rider else []
    r_sem = rider.sem_shapes if rider else []
    nsteps = math.prod(grid)

    def full_body(*refs):
        pos = 0
        groups = []
        for size in (n_in, len(r_in), n_out, len(r_out), n_scr, len(r_sem)):
            groups.append(refs[pos:pos + size])
            pos += size
        ins, rins, outs, routs, scr, rsems = groups
        step = pl.program_id(0)
        for axis in range(1, len(grid)):
            step = step * grid[axis] + pl.program_id(axis)
        at = {"first": 0, "mid": (3 * nsteps) // 4, "last": nsteps - 1}
        hooks = rider.hooks if rider else []
        for when, fn in hooks:
            if when != "last":
                pl.when(step == at[when])(functools.partial(fn, rins, routs, rsems))
        body(*ins, *outs, *scr)
        for when, fn in hooks:
            if when == "last":
                pl.when(step == at[when])(functools.partial(fn, rins, routs, rsems))

    outs = pl.pallas_call(
        full_body, name=name, grid=grid,
        out_shape=list(out_shape) + list(r_out),
        in_specs=list(in_specs) + (rider.in_specs if rider else []),
        out_specs=list(out_specs) + [ANY] * len(r_out),
        scratch_shapes=list(scratch) + list(r_sem),
        input_output_aliases=dict(aliases or {}),
        compiler_params=pltpu.CompilerParams(dimension_semantics=("arbitrary",) * len(grid),
                                             vmem_limit_bytes=VMEM_LIMIT),
    )(*args, *r_in)
    return list(outs[:n_out]), list(outs[n_out:])


def _gather_rider(shards):
    n = len(shards)

    def setup(outs, sems):
        x, y, c = _my_place()
        send_sems, recv_sems, _ = sems
        chips = [(1 - x, y), (x, 1 - y), (1 - x, 1 - y)]

        def block(w, place):
            return outs[w].at[4 * place[0] + 2 * place[1] + place[2]]

        def copy(w, k, place, to, src=None):
            return pltpu.make_async_remote_copy(
                src_ref=block(w, place) if src is None else src, dst_ref=block(w, place),
                send_sem=send_sems.at[w, k], recv_sem=recv_sems.at[w, k], device_id=to, device_id_type=MESH)

        return (x, y, c), (x, y, 1 - c), chips, block, copy

    def first(ins, outs, sems):
        me, sibling, chips, block, copy = setup(outs, sems)
        for w in range(n):
            pltpu.make_async_copy(ins[w], block(w, me), sems[2].at[w]).start()
            copy(w, 0, me, sibling, src=ins[w]).start()
            for j, chip in enumerate(chips):
                copy(w, 1 + j, me, (*chip, me[2]), src=ins[w]).start()

    def mid(ins, outs, sems):
        me, sibling, chips, block, copy = setup(outs, sems)
        for w in range(n):
            for j, chip in enumerate(chips):
                copy(w, 1 + j, (*chip, me[2]), me).wait_recv()
                copy(w, 4 + j, (*chip, me[2]), sibling).start()

    def last(ins, outs, sems):
        me, sibling, chips, block, copy = setup(outs, sems)
        for w in range(n):
            copy(w, 0, sibling, me).wait_recv()
            for j, chip in enumerate(chips):
                copy(w, 4 + j, (*chip, 1 - me[2]), me).wait_recv()
            copy(w, 0, me, sibling, src=ins[w]).wait_send()
            for j, chip in enumerate(chips):
                copy(w, 1 + j, me, (*chip, me[2]), src=ins[w]).wait_send()
                copy(w, 4 + j, (*chip, me[2]), sibling).wait_send()
            pltpu.make_async_copy(ins[w], block(w, me), sems[2].at[w]).wait()

    return _Rider(
        shards, [jax.ShapeDtypeStruct((N_DEV,) + s.shape, BF16) for s in shards],
        [pltpu.SemaphoreType.DMA((n, N_DEV - 1)), pltpu.SemaphoreType.DMA((n, N_DEV - 1)),
         pltpu.SemaphoreType.DMA((n,))],
        first=first, mid=mid, last=last, ins_in_vmem=True)


def _sibling_rider(gblocks):
    n = len(gblocks)

    def copies(ins, outs, sems):
        x, y, c = _my_place()
        send_sems, recv_sems = sems
        made = []
        for w in range(n):
            for f, (fx, fy) in enumerate(CHIP_FLIPS):
                chip = 4 * _flip(x, fx) + 2 * _flip(y, fy)
                made.append(pltpu.make_async_remote_copy(
                    src_ref=ins[w].at[chip + 1 - c], dst_ref=outs[w].at[f], send_sem=send_sems.at[w, f],
                    recv_sem=recv_sems.at[w, f], device_id=(x, y, 1 - c), device_id_type=MESH))
        return made

    def first(ins, outs, sems):
        for cp in copies(ins, outs, sems):
            cp.start()

    def last(ins, outs, sems):
        for cp in copies(ins, outs, sems):
            cp.wait_recv()
            cp.wait_send()

    return _Rider(gblocks, [jax.ShapeDtypeStruct((4,) + g.shape[1:], BF16) for g in gblocks],
                  [pltpu.SemaphoreType.DMA((n, 4))] * 2, first=first, last=last)


def _own_blocks(gblocks):
    x, y, c = _my_place()
    return jnp.stack([lax.dynamic_index_in_dim(gblocks, 4 * _flip(x, fx) + 2 * _flip(y, fy) + c, 0, keepdims=False)
                      for fx, fy in CHIP_FLIPS])


def _chip_rider(sums):
    n = len(sums)

    def copies(ins, outs, sems):
        x, y, c = _my_place()
        send_sems, recv_sems = sems
        made = []
        for w in range(n):
            for f in (1, 2, 3):
                fx, fy = CHIP_FLIPS[f]
                made.append(pltpu.make_async_remote_copy(
                    src_ref=ins[w].at[f - 1], dst_ref=outs[w].at[f - 1], send_sem=send_sems.at[w, f - 1],
                    recv_sem=recv_sems.at[w, f - 1], device_id=(_flip(x, fx), _flip(y, fy), c), device_id_type=MESH))
        return made

    def first(ins, outs, sems):
        for cp in copies(ins, outs, sems):
            cp.start()

    def last(ins, outs, sems):
        for cp in copies(ins, outs, sems):
            cp.wait_recv()
            cp.wait_send()

    return _Rider(sums, [jax.ShapeDtypeStruct(s.shape, BF16) for s in sums],
                  [pltpu.SemaphoreType.DMA((n, 3))] * 2, first=first, last=last)


def _push_to_all(v_ref, out_ref, send_sems, recv_sems, local_sem, wait=True):
    x, y, c = _my_place()
    me = 4 * x + 2 * y + c
    mine = pltpu.make_async_copy(v_ref, out_ref.at[me], local_sem)
    mine.start()
    sends = []
    for k in range(1, N_DEV):
        px, py, pc = _flip(x, k & 4), _flip(y, k & 2), _flip(c, k & 1)
        cp = pltpu.make_async_remote_copy(
            src_ref=v_ref, dst_ref=out_ref.at[me], send_sem=send_sems.at[k - 1], recv_sem=recv_sems.at[k - 1],
            device_id=(px, py, pc), device_id_type=MESH)
        cp.start()
        sends.append(cp)

    def finish():
        for k in range(1, N_DEV):
            px, py, pc = _flip(x, k & 4), _flip(y, k & 2), _flip(c, k & 1)
            pltpu.make_async_remote_copy(
                src_ref=v_ref, dst_ref=out_ref.at[4 * px + 2 * py + pc], send_sem=send_sems.at[k - 1],
                recv_sem=recv_sems.at[k - 1], device_id=(px, py, pc), device_id_type=MESH).wait_recv()
        for cp in sends:
            cp.wait_send()
        mine.wait()

    if wait:
        finish()
    return finish


def _small_allgather(v, name):
    rows = v.shape[0]

    def body(v_ref, out_ref, send_sems, recv_sems, local_sem):
        _push_to_all(v_ref, out_ref, send_sems, recv_sems, local_sem)

    return pl.pallas_call(
        body, name=name,
        out_shape=jax.ShapeDtypeStruct((N_DEV, rows, LANES), F32),
        in_specs=[pl.BlockSpec(memory_space=pltpu.VMEM)],
        out_specs=pl.BlockSpec(memory_space=pltpu.VMEM),
        scratch_shapes=[pltpu.SemaphoreType.DMA((N_DEV - 1,)), pltpu.SemaphoreType.DMA((N_DEV - 1,)),
                        pltpu.SemaphoreType.DMA],
        compiler_params=pltpu.CompilerParams(vmem_limit_bytes=VMEM_LIMIT),
    )(v)


def _gather_first_weight(shard, others, cond_rows, w_ada, b_cols):
    n = len(others)
    ada_cols = w_ada.shape[1]
    c_rows = D_MODEL // LANES

    def body(*refs):
        w_ref, other_refs = refs[0], refs[1:1 + n]
        cond_ref, wada_ref, bcols_ref = refs[1 + n:4 + n]
        out_ref, cast_refs = refs[4 + n], refs[5 + n:5 + 2 * n]
        cond_all_ref, mod_all_ref = refs[5 + 2 * n:7 + 2 * n]
        mine_ref, mod_ref, send_sems, recv_sems, local_sem, small_send, small_recv, small_local = refs[7 + 2 * n:]
        x, y, c = _my_place()
        me, sibling = (x, y, c), (x, y, 1 - c)
        chips = [(1 - x, y), (x, 1 - y), (1 - x, 1 - y)]

        def block(place):
            return out_ref.at[4 * place[0] + 2 * place[1] + place[2]]

        def copy(k, place, to, src=None):
            return pltpu.make_async_remote_copy(
                src_ref=block(place) if src is None else src, dst_ref=block(place),
                send_sem=send_sems.at[k], recv_sem=recv_sems.at[k], device_id=to, device_id_type=MESH)

        finish_cond = _push_to_all(cond_ref, cond_all_ref, small_send.at[0], small_recv.at[0], small_local.at[0],
                                   wait=False)
        mine_ref[...] = w_ref[...].astype(BF16)
        finish_cond()
        local = pltpu.make_async_copy(mine_ref, block(me), local_sem)
        local.start()
        started = [copy(0, me, sibling, src=mine_ref)]
        started += [copy(1 + j, me, (*chip, c), src=mine_ref) for j, chip in enumerate(chips)]
        for cp in started:
            cp.start()
        mod = jnp.zeros((N_DEV, ada_cols), F32) + bcols_ref[...]
        for r in range(c_rows):
            cf = cond_all_ref[:, r, :]
            act = (cf * _sigmoid(cf)).astype(BF16)
            mod = mod + jnp.dot(act, wada_ref[r * LANES:(r + 1) * LANES, :].astype(BF16),
                                preferred_element_type=F32)
        mod_ref[...] = mod
        finish_mod = _push_to_all(mod_ref, mod_all_ref, small_send.at[1], small_recv.at[1], small_local.at[1],
                                  wait=False)
        for o_ref, c_ref in zip(other_refs, cast_refs):
            c_ref[...] = o_ref[...].astype(BF16)
        for j, chip in enumerate(chips):
            copy(1 + j, (*chip, c), me).wait_recv()
            passed = copy(4 + j, (*chip, c), sibling)
            passed.start()
            started.append(passed)
        copy(0, sibling, me).wait_recv()
        for j, chip in enumerate(chips):
            copy(4 + j, (*chip, 1 - c), me).wait_recv()
        finish_mod()
        for cp in started:
            cp.wait_send()
        local.wait()

    vmem = pl.BlockSpec(memory_space=pltpu.VMEM)
    outs = pl.pallas_call(
        body, name="gather_w_in",
        out_shape=[jax.ShapeDtypeStruct((N_DEV,) + shard.shape, BF16)]
        + [jax.ShapeDtypeStruct(o.shape, BF16) for o in others]
        + [jax.ShapeDtypeStruct((N_DEV,) + cond_rows.shape, F32), jax.ShapeDtypeStruct((N_DEV, N_DEV, ada_cols), F32)],
        in_specs=[vmem] * (4 + n),
        out_specs=[ANY] + [vmem] * (n + 2),
        scratch_shapes=[pltpu.VMEM(shard.shape, BF16), pltpu.VMEM((N_DEV, ada_cols), F32),
                        pltpu.SemaphoreType.DMA((N_DEV - 1,)), pltpu.SemaphoreType.DMA((N_DEV - 1,)),
                        pltpu.SemaphoreType.DMA,
                        pltpu.SemaphoreType.DMA((2, N_DEV - 1)), pltpu.SemaphoreType.DMA((2, N_DEV - 1)),
                        pltpu.SemaphoreType.DMA((2,))],
        compiler_params=pltpu.CompilerParams(vmem_limit_bytes=VMEM_LIMIT),
    )(shard, *others, cond_rows, w_ada, b_cols)
    return outs[0], list(outs[1:1 + n]), outs[1 + n], outs[2 + n]


def _carry(rider, name):
    def body(token_ref):
        token_ref[...] = jnp.zeros_like(token_ref)

    _, routs = _call(body, name, (1,), [], [], [jax.ShapeDtypeStruct((SUBLANES, LANES), F32)],
                     [_full((SUBLANES, LANES))], rider=rider)
    return routs


def _ada_weight_grad(c_all, dmod_cols):
    cols = dmod_cols.shape[1]

    def body(c_ref, d_ref, out_ref):
        cf = c_ref[...]
        act = (cf * _sigmoid(cf)).astype(BF16)
        out_ref[...] = lax.dot_general(act, d_ref[...].astype(BF16), TN_DIMS, preferred_element_type=F32)

    return pl.pallas_call(
        body, name="ada_weight_grad",
        out_shape=jax.ShapeDtypeStruct((D_MODEL, cols), F32),
        in_specs=[pl.BlockSpec(memory_space=pltpu.VMEM)] * 2,
        out_specs=pl.BlockSpec(memory_space=pltpu.VMEM),
        compiler_params=pltpu.CompilerParams(vmem_limit_bytes=VMEM_LIMIT),
    )(c_all, dmod_cols)


def _sum_devices(packed):
    def body(p_ref, out_ref):
        total = p_ref[0]
        for d in range(1, N_DEV):
            total = total + p_ref[d]
        out_ref[...] = total

    return pl.pallas_call(
        body, name="sum_devices",
        out_shape=jax.ShapeDtypeStruct(packed.shape[1:], F32),
        in_specs=[pl.BlockSpec(memory_space=pltpu.VMEM)],
        out_specs=pl.BlockSpec(memory_space=pltpu.VMEM),
        compiler_params=pltpu.CompilerParams(vmem_limit_bytes=VMEM_LIMIT),
    )(packed)


def _row_tile(rows, multiple):
    for cand in range(min(rows, 256), 0, -1):
        if rows % cand == 0 and cand % multiple == 0:
            return cand
    return rows


def _adamw_update(w, g, m, v):
    c1 = 1.0 / (1.0 - ADAM_B1 ** ADAM_STEP)
    c2 = 1.0 / (1.0 - ADAM_B2 ** ADAM_STEP)
    nm = ADAM_B1 * m + (1.0 - ADAM_B1) * g
    nv = ADAM_B2 * v + (1.0 - ADAM_B2) * (g * g)
    delta = -ADAM_LR * ((nm * c1) / (jnp.sqrt(nv * c2) + ADAM_EPS) + ADAM_WD * w)
    return delta, nm, nv


def _adamw(w, g, m, v, name):
    rows, cols = w.shape
    tile = _row_tile(rows, SUBLANES)

    def body(w_ref, g_ref, m_ref, v_ref, d_ref, nm_ref, nv_ref):
        d_ref[...], nm_ref[...], nv_ref[...] = _adamw_update(w_ref[...], g_ref[...], m_ref[...], v_ref[...])

    spec = pl.BlockSpec((tile, cols), lambda i: (i, 0))
    outs, _ = _call(body, name, (rows // tile,), [w, g, m, v], [spec] * 4,
                    [jax.ShapeDtypeStruct((rows, cols), F32)] * 3, [spec] * 3)
    return outs


def _sibling_sum(own, sib, name):
    _, r, cdim = own.shape
    tile = _row_tile(r, BF16_ROWS)

    def body(own_ref, sib_ref, sums_ref, mine_ref):
        mine_ref[...] = own_ref[0].astype(F32) + sib_ref[0].astype(F32)
        for f in (1, 2, 3):
            sums_ref[f - 1] = (own_ref[f].astype(F32) + sib_ref[f].astype(F32)).astype(BF16)

    outs, _ = _call(
        body, name, (r // tile,), [own, sib], [pl.BlockSpec((4, tile, cdim), lambda i: (0, i, 0))] * 2,
        [jax.ShapeDtypeStruct((3, r, cdim), BF16), jax.ShapeDtypeStruct((r, cdim), F32)],
        [pl.BlockSpec((3, tile, cdim), lambda i: (0, i, 0)), pl.BlockSpec((tile, cdim), lambda i: (i, 0))])
    return outs


def _chip_sum_adamw(mine, ici, w, m, v, name):
    r, cdim = mine.shape
    tile = _row_tile(r, BF16_ROWS)

    def body(mine_ref, ici_ref, w_ref, m_ref, v_ref, g_ref, d_ref, nm_ref, nv_ref):
        g = mine_ref[...]
        for f in range(3):
            g = g + ici_ref[f].astype(F32)
        g_ref[...] = g
        d_ref[...], nm_ref[...], nv_ref[...] = _adamw_update(w_ref[...], g, m_ref[...], v_ref[...])

    spec = pl.BlockSpec((tile, cdim), lambda i: (i, 0))
    outs, _ = _call(
        body, name, (r // tile,), [mine, ici, w, m, v],
        [spec, pl.BlockSpec((3, tile, cdim), lambda i: (0, i, 0)), spec, spec, spec],
        [jax.ShapeDtypeStruct((r, cdim), F32)] * 4, [spec] * 4)
    return outs


REF_KV_COL = D_MODEL
REF_REST_COL = D_MODEL + 2 * KV_WIDTH
IN_CHUNK = 1280
IN_PIECES = ([(0, 0, D_MODEL)]
             + [(D_MODEL + n * IN_CHUNK, REF_REST_COL + n * IN_CHUNK, IN_CHUNK) for n in range(REST_WIDTH // IN_CHUNK)]
             + [(KV_COL, REF_KV_COL, 2 * KV_WIDTH)])


def _inproj_fwd(x, vec, w_t, b_in, rider):
    t = x.shape[0]
    tm = min(TOKEN_TILE, t)

    def body(x_ref, vec_ref, w_ref, b_ref, z_ref, h_ref):
        xf = x_ref[...]
        r = lax.rsqrt(jnp.mean(xf * xf, axis=-1, keepdims=True) + EPS)
        h = (xf * r) * vec_ref[0:1, :] * (1.0 + vec_ref[1:2, :]) + vec_ref[2:3, :]
        hb = h.astype(BF16)
        h_ref[...] = hb
        for mine, ref, width in IN_PIECES:
            zc = lax.dot_general(hb, w_ref[ref:ref + width, :], NT_DIMS, preferred_element_type=F32)
            z_ref[:, mine:mine + width] = (zc + b_ref[:, ref:ref + width]).astype(BF16)

    return _call(
        body, "inproj_fwd", (t // tm,), [x, vec, w_t, b_in],
        [pl.BlockSpec((tm, D_MODEL), lambda i: (i, 0)), _full((SUBLANES, D_MODEL)),
         _full((IN_WIDTH, D_MODEL)), _full((1, IN_WIDTH))],
        [jax.ShapeDtypeStruct((t, IN_WIDTH), BF16), jax.ShapeDtypeStruct((t, D_MODEL), BF16)],
        [pl.BlockSpec((tm, IN_WIDTH), lambda i: (i, 0)), pl.BlockSpec((tm, D_MODEL), lambda i: (i, 0))],
        rider=rider)


def _window_mask(has_prev):
    qi = lax.broadcasted_iota(jnp.int32, (WINDOW, 2 * WINDOW), 0)
    kj = lax.broadcasted_iota(jnp.int32, (WINDOW, 2 * WINDOW), 1)
    off = jnp.where(has_prev, 0, 4 * WINDOW)
    in_prev = jnp.logical_and(kj < WINDOW, kj > qi + off)
    in_cur = jnp.logical_and(kj >= WINDOW, (kj - WINDOW) <= qi)
    return jnp.logical_or(in_prev, in_cur)


PAIRS = GROUP // 2
STACK = PAIRS * WINDOW


LOG2E = 1.4426950408889634
LN2 = 0.6931471805599453
SCORE_SCALE = ATTN_SCALE * LOG2E


def _fill_window_bias(bias_ref):
    shape = bias_ref.shape[1:]
    kj = lax.broadcasted_iota(jnp.int32, shape, 0)
    qi = jnp.bitwise_and(lax.broadcasted_iota(jnp.int32, shape, 1), WINDOW - 1)
    in_prev = jnp.logical_and(kj < WINDOW, kj > qi)
    in_cur = jnp.logical_and(kj >= WINDOW, (kj - WINDOW) <= qi)
    bias_ref[0] = jnp.where(in_cur, 0.0, -jnp.inf)
    bias_ref[1] = jnp.where(jnp.logical_or(in_prev, in_cur), 0.0, -jnp.inf)


def _half_tiles(tile):
    low = lax.broadcasted_iota(jnp.int32, tile.shape, 1) < HEAD_DIM
    swapped = jnp.concatenate([tile[:, HEAD_DIM:], tile[:, :HEAD_DIM]], axis=1)
    zero = jnp.zeros_like(tile)
    return ((jnp.where(low, tile, zero), jnp.where(low, zero, swapped)),
            (jnp.where(low, swapped, zero), jnp.where(low, zero, tile)))


def _stack_pairs(ref, row0, j):
    return jnp.concatenate(
        [ref[pl.ds(row0, WINDOW), (j * PAIRS + p) * LANES:(j * PAIRS + p + 1) * LANES] for p in range(PAIRS)], axis=0)


def _per_pair_row(values):
    pair = lax.broadcasted_iota(jnp.int32, (1, STACK), 1) // WINDOW
    row = jnp.full((1, STACK), values[PAIRS - 1], F32)
    for p in range(PAIRS - 2, -1, -1):
        row = jnp.where(pair == p, values[p], row)
    return row


def _attn_fwd(z, sinks, rider):
    t = z.shape[0]
    tq = min(TOKEN_TILE, t)
    nblk = tq // WINDOW

    def body(q_ref, kv_ref, sink_ref, o_ref, lse_ref, bias_ref):
        i = pl.program_id(0)

        @pl.when(i == 0)
        def _():
            _fill_window_bias(bias_ref)

        def one_block(b, carry):
            row0 = pl.multiple_of(b * WINDOW, WINDOW)
            start = i * tq + b * WINDOW
            prev = pl.multiple_of(jnp.maximum(start - WINDOW, 0), WINDOW)
            cur = pl.multiple_of(start, WINDOW)
            kvw = jnp.concatenate([kv_ref[pl.ds(prev, WINDOW), :], kv_ref[pl.ds(cur, WINDOW), :]], axis=0)
            k_halves = _half_tiles(kvw[:, :KV_WIDTH])
            v_halves = _half_tiles(kvw[:, KV_WIDTH:])
            bias = bias_ref[jnp.minimum(start, 1)]
            for j in range(N_KV_HEADS):
                for pr in range(PAIRS):
                    cols = slice((j * PAIRS + pr) * LANES, (j * PAIRS + pr + 1) * LANES)
                    qp = q_ref[pl.ds(row0, WINDOW), cols]
                    o_t = jnp.zeros((LANES, WINDOW), F32)
                    for parity in range(2):
                        h = j * GROUP + 2 * pr + parity
                        s = lax.dot_general(k_halves[j][parity], qp, NT_DIMS, preferred_element_type=F32)
                        s = s * SCORE_SCALE + bias
                        sink = sink_ref[h] * LOG2E
                        m = jnp.maximum(jnp.max(s, axis=0, keepdims=True), sink)
                        p = jnp.exp2(s - m)
                        denom = jnp.sum(p, axis=0, keepdims=True) + jnp.exp2(sink - m)
                        pv = lax.dot_general(v_halves[j][parity], p.astype(BF16), TN_DIMS,
                                             preferred_element_type=F32)
                        o_t = o_t + pv * (1.0 / denom)
                        lse_ref[h:h + 1, pl.ds(row0, WINDOW)] = m + jnp.log2(denom)
                    o_ref[pl.ds(row0, WINDOW), cols] = jnp.transpose(o_t.astype(BF16))
            return carry

        lax.fori_loop(0, nblk, one_block, 0)

    return _call(
        body, "attn_fwd", (t // tq,), [z, z, sinks],
        [pl.BlockSpec((tq, D_MODEL), lambda i: (i, 0)),
         pl.BlockSpec((t, 2 * KV_WIDTH), lambda i: (0, KV_COL // (2 * KV_WIDTH))),
         pl.BlockSpec(memory_space=pltpu.SMEM)],
        [jax.ShapeDtypeStruct((t, D_MODEL), BF16), jax.ShapeDtypeStruct((N_Q_HEADS, t), F32)],
        [pl.BlockSpec((tq, D_MODEL), lambda i: (i, 0)), pl.BlockSpec((N_Q_HEADS, tq), lambda i: (0, i))],
        scratch=[pltpu.VMEM((2, 2 * WINDOW, WINDOW), F32)], rider=rider)


HALO = BF16_ROWS


def _shift_down(u, uh, k):
    row = lax.broadcasted_iota(jnp.int32, u.shape, 0)
    out = pltpu.roll(u, k, 0)
    for j in range(k):
        out = jnp.where(row == j, uh[HALO - k + j:HALO - k + j + 1, :], out)
    return out


def _shift_up(u, nxt, k):
    n = u.shape[0]
    row = lax.broadcasted_iota(jnp.int32, u.shape, 0)
    out = pltpu.roll(u, n - k, 0)
    for j in range(k):
        out = jnp.where(row == n - k + j, nxt[j:j + 1, :], out)
    return out


def _conv_inputs(cc_ref, cx_ref, hc_ref, hx_ref, first_tile):
    cc = cc_ref[...].astype(F32)
    cx = cx_ref[...].astype(F32)
    u = cc * cx
    uh = jnp.where(first_tile, 0.0, hc_ref[...].astype(F32) * hx_ref[...].astype(F32))
    return cc, cx, u, _shift_down(u, uh, 1), _shift_down(u, uh, 2)


def _z_specs(tm, order):
    per_tile = tm // HALO
    cols = [pl.BlockSpec((tm, D_MODEL), functools.partial(lambda i, j: (order(i), j), j=j)) for j in range(1, 6)]
    halos = [pl.BlockSpec((HALO, D_MODEL),
                          functools.partial(lambda i, j: (jnp.maximum(order(i) * per_tile - 1, 0), j), j=j))
             for j in (2, 3)]
    return cols + halos


def _mix_fwd(x, attn, z, vec, w_out):
    t = x.shape[0]
    tm = min(TOKEN_TILE, t)

    def body(x_ref, a_ref, cb_ref, cc_ref, cx_ref, ga_ref, gc_ref, hc_ref, hx_ref, vec_ref, w_ref,
             m_ref, x2_ref, h2_ref, o_ref):
        i = pl.program_id(0)
        _, _, u, u1, u2 = _conv_inputs(cc_ref, cx_ref, hc_ref, hx_ref, i == 0)
        cv = vec_ref[4:5, :] * u2 + vec_ref[5:6, :] * u1 + vec_ref[6:7, :] * u
        conv = cb_ref[...].astype(F32) * cv
        merged = (_sigmoid(ga_ref[...].astype(F32)) * a_ref[...].astype(F32)
                  + _sigmoid(gc_ref[...].astype(F32)) * conv)
        mb = merged.astype(BF16)
        m_ref[...] = mb
        o = jnp.dot(mb, w_ref[...], preferred_element_type=F32)
        o_ref[...] = o.astype(BF16)
        x2 = x_ref[...] + vec_ref[0:1, :] * o
        x2_ref[...] = x2
        r = lax.rsqrt(jnp.mean(x2 * x2, axis=-1, keepdims=True) + EPS)
        h2 = (x2 * r) * vec_ref[1:2, :] * (1.0 + vec_ref[2:3, :]) + vec_ref[3:4, :]
        h2_ref[...] = h2.astype(BF16)

    tok = pl.BlockSpec((tm, D_MODEL), lambda i: (i, 0))
    outs, _ = _call(
        body, "mix_fwd", (t // tm,), [x, attn, z, z, z, z, z, z, z, vec, w_out],
        [tok, tok] + _z_specs(tm, lambda i: i) + [_full((SUBLANES, D_MODEL)), _full((D_MODEL, D_MODEL))],
        [jax.ShapeDtypeStruct((t, D_MODEL), BF16), jax.ShapeDtypeStruct((t, D_MODEL), F32),
         jax.ShapeDtypeStruct((t, D_MODEL), BF16), jax.ShapeDtypeStruct((t, D_MODEL), BF16)],
        [tok, tok, tok, tok])
    return outs


def _ffn_fwd(h2, w_t):
    t = h2.shape[0]
    tm = min(TOKEN_TILE, t)

    def body(h_ref, w_ref, gu_ref, a_ref):
        hb = h_ref[...]
        for n in range(D_FF // FF_CHUNK):
            lo, hi = n * FF_CHUNK, (n + 1) * FF_CHUNK
            g = lax.dot_general(hb, w_ref[lo:hi, :], NT_DIMS, preferred_element_type=F32)
            u = lax.dot_general(hb, w_ref[D_FF + lo:D_FF + hi, :], NT_DIMS, preferred_element_type=F32)
            sg = _sigmoid(g)
            silu = g * sg
            gu_ref[:, lo:hi] = (u * (sg * (1.0 + g * (1.0 - sg)))).astype(BF16)
            gu_ref[:, D_FF + lo:D_FF + hi] = silu.astype(BF16)
            a_ref[:, lo:hi] = (silu * u).astype(BF16)

    outs, _ = _call(
        body, "ffn_fwd", (t // tm,), [h2, w_t],
        [pl.BlockSpec((tm, D_MODEL), lambda i: (i, 0)), _full((2 * D_FF, D_MODEL))],
        [jax.ShapeDtypeStruct((t, 2 * D_FF), BF16), jax.ShapeDtypeStruct((t, D_FF), BF16)],
        [pl.BlockSpec((tm, 2 * D_FF), lambda i: (i, 0)), pl.BlockSpec((tm, D_FF), lambda i: (i, 0))])
    return outs


def _ffn_out_loss(a, gu, x2, target, vec, w_ffn_out):
    t = a.shape[0]
    tm = min(TOKEN_TILE, t)

    def body(a_ref, gu_ref, x2_ref, t_ref, vec_ref, w_ref, dx3_ref, df_ref, dgu_ref, acc_ref):
        @pl.when(pl.program_id(0) == 0)
        def _():
            acc_ref[...] = jnp.zeros_like(acc_ref)

        ga2 = vec_ref[0:1, :]
        gf = vec_ref[1:2, :]
        f = jnp.dot(a_ref[...], w_ref[...], preferred_element_type=F32)
        x3 = x2_ref[...] + ga2 * f
        r = lax.rsqrt(jnp.mean(x3 * x3, axis=-1, keepdims=True) + EPS)
        xn = x3 * r
        err = xn * gf - t_ref[...]
        dy = err * (1.0 / D_MODEL)
        dxn = dy * gf
        dx3 = r * (dxn - xn * jnp.mean(dxn * xn, axis=-1, keepdims=True))
        dx3_ref[...] = dx3.astype(GRAD_STREAM)
        acc_ref[0:1, :] += jnp.sum(err * err, axis=0, keepdims=True)
        acc_ref[1:2, :] += jnp.sum(dy * xn, axis=0, keepdims=True)
        acc_ref[2:3, :] += jnp.sum(dx3 * f, axis=0, keepdims=True)
        df = (dx3 * ga2).astype(BF16)
        df_ref[...] = df
        for n in range(D_FF // FF_CHUNK):
            lo, hi = n * FF_CHUNK, (n + 1) * FF_CHUNK
            da = lax.dot_general(df, w_ref[lo:hi, :], NT_DIMS, preferred_element_type=F32)
            dgu_ref[:, lo:hi] = (da * gu_ref[:, lo:hi].astype(F32)).astype(BF16)
            dgu_ref[:, D_FF + lo:D_FF + hi] = (da * gu_ref[:, D_FF + lo:D_FF + hi].astype(F32)).astype(BF16)

    tok = pl.BlockSpec((tm, D_MODEL), lambda i: (i, 0))
    outs, _ = _call(
        body, "ffn_out_loss", (t // tm,), [a, gu, x2, target, vec, w_ffn_out],
        [pl.BlockSpec((tm, D_FF), lambda i: (i, 0)), pl.BlockSpec((tm, 2 * D_FF), lambda i: (i, 0)),
         tok, tok, _full((SUBLANES, D_MODEL)), _full((D_FF, D_MODEL))],
        [jax.ShapeDtypeStruct((t, D_MODEL), GRAD_STREAM), jax.ShapeDtypeStruct((t, D_MODEL), BF16),
         jax.ShapeDtypeStruct((t, 2 * D_FF), BF16), jax.ShapeDtypeStruct((SUBLANES, D_MODEL), F32)],
        [tok, tok, pl.BlockSpec((tm, 2 * D_FF), lambda i: (i, 0)), _full((SUBLANES, D_MODEL))])
    return outs


def _ffn_in_bwd(dgu, x2, dx3, vec, w_t, rider):
    t = x2.shape[0]
    tm = min(TOKEN_TILE, t)

    def body(dgu_ref, x2_ref, dx3_ref, vec_ref, wf_ref, dx2_ref, acc_ref):
        @pl.when(pl.program_id(0) == 0)
        def _():
            acc_ref[...] = jnp.zeros_like(acc_ref)

        gffn = vec_ref[0:1, :]
        sc2 = vec_ref[1:2, :]
        dh2 = jnp.dot(dgu_ref[...], wf_ref[...], preferred_element_type=F32)
        x2 = x2_ref[...]
        r = lax.rsqrt(jnp.mean(x2 * x2, axis=-1, keepdims=True) + EPS)
        xn = x2 * r
        acc_ref[0:1, :] += jnp.sum(dh2, axis=0, keepdims=True)
        acc_ref[1:2, :] += jnp.sum(dh2 * xn * gffn, axis=0, keepdims=True)
        acc_ref[2:3, :] += jnp.sum(dh2 * xn * (1.0 + sc2), axis=0, keepdims=True)
        dxn = dh2 * gffn * (1.0 + sc2)
        dx2 = dx3_ref[...].astype(F32) + r * (dxn - xn * jnp.mean(dxn * xn, axis=-1, keepdims=True))
        dx2_ref[...] = dx2.astype(GRAD_STREAM)

    tok = pl.BlockSpec((tm, D_MODEL), lambda i: (i, 0))
    return _call(
        body, "ffn_in_bwd", (t // tm,), [dgu, x2, dx3, vec, w_t],
        [pl.BlockSpec((tm, 2 * D_FF), lambda i: (i, 0)), tok, tok, _full((SUBLANES, D_MODEL)),
         _full((2 * D_FF, D_MODEL))],
        [jax.ShapeDtypeStruct((t, D_MODEL), GRAD_STREAM), jax.ShapeDtypeStruct((SUBLANES, D_MODEL), F32)],
        [tok, _full((SUBLANES, D_MODEL))], rider=rider)


def _mix_bwd(dx2, oproj, attn, z, vec, w_out, rider):
    t = dx2.shape[0]
    tm = min(TOKEN_TILE, t)
    nt = t // tm
    rev = lambda i: nt - 1 - i

    def body(dx2_ref, m_ref, a_ref, cb_ref, cc_ref, cx_ref, ga_ref, gc_ref, hc_ref, hx_ref,
             vec_ref, wo_ref, do_ref, da_ref, dr_ref, acc_ref, carry_ref):
        i = pl.program_id(0)

        @pl.when(i == 0)
        def _():
            acc_ref[...] = jnp.zeros_like(acc_ref)
            carry_ref[...] = jnp.zeros_like(carry_ref)

        ga1 = vec_ref[0:1, :]
        w0, w1, w2 = vec_ref[1:2, :], vec_ref[2:3, :], vec_ref[3:4, :]
        dx2 = dx2_ref[...].astype(F32)
        acc_ref[0:1, :] += jnp.sum(dx2 * m_ref[...].astype(F32), axis=0, keepdims=True)
        do = (dx2 * ga1).astype(BF16)
        do_ref[...] = do
        dm = lax.dot_general(do, wo_ref[...], NT_DIMS, preferred_element_type=F32)

        cc, cx, u, u1, u2 = _conv_inputs(cc_ref, cx_ref, hc_ref, hx_ref, i == nt - 1)
        cv = w0 * u2 + w1 * u1 + w2 * u
        cb = cb_ref[...].astype(F32)
        sa = _sigmoid(ga_ref[...].astype(F32))
        sc = _sigmoid(gc_ref[...].astype(F32))
        attn = a_ref[...].astype(F32)
        da_ref[...] = (dm * sa).astype(BF16)
        dconv = dm * sc
        dr_ref[:, 3 * D_MODEL:4 * D_MODEL] = (dm * attn * sa * (1.0 - sa)).astype(BF16)
        dr_ref[:, 4 * D_MODEL:5 * D_MODEL] = (dconv * (cb * cv) * (1.0 - sc)).astype(BF16)
        dr_ref[:, 0:D_MODEL] = (dconv * cv).astype(BF16)
        dcv = dconv * cb
        acc_ref[1:2, :] += jnp.sum(dcv * u2, axis=0, keepdims=True)
        acc_ref[2:3, :] += jnp.sum(dcv * u1, axis=0, keepdims=True)
        acc_ref[3:4, :] += jnp.sum(dcv * u, axis=0, keepdims=True)
        nxt = carry_ref[...]
        du = w2 * dcv + w1 * _shift_up(dcv, nxt, 1) + w0 * _shift_up(dcv, nxt, 2)
        carry_ref[...] = dcv[0:SUBLANES, :]
        dr_ref[:, D_MODEL:2 * D_MODEL] = (du * cx).astype(BF16)
        dr_ref[:, 2 * D_MODEL:3 * D_MODEL] = (du * cc).astype(BF16)

    tok = pl.BlockSpec((tm, D_MODEL), lambda i: (rev(i), 0))
    return _call(
        body, "mix_bwd", (nt,), [dx2, oproj, attn, z, z, z, z, z, z, z, vec, w_out],
        [tok, tok, tok] + _z_specs(tm, rev) + [_full((SUBLANES, D_MODEL)), _full((D_MODEL, D_MODEL))],
        [jax.ShapeDtypeStruct((t, D_MODEL), BF16), jax.ShapeDtypeStruct((t, D_MODEL), BF16),
         jax.ShapeDtypeStruct((t, REST_WIDTH), BF16), jax.ShapeDtypeStruct((SUBLANES, D_MODEL), F32)],
        [tok, tok, pl.BlockSpec((tm, REST_WIDTH), lambda i: (rev(i), 0)), _full((SUBLANES, D_MODEL))],
        scratch=[pltpu.VMEM((SUBLANES, D_MODEL), F32)], rider=rider)


def _attn_bwd(z, dattn, attn, lse, sinks, rider):
    t = z.shape[0]
    tq = min(TOKEN_TILE, t)
    nblk = tq // WINDOW
    nt = t // tq

    def body(q_ref, kv_ref, do_ref, o_ref, lse_ref, sink_ref, dq_ref, dkv_ref, ds_ref, acc_ref, bias_ref):
        i = pl.program_id(0)

        @pl.when(i == 0)
        def _():
            acc_ref[...] = jnp.zeros_like(acc_ref)
            ds_ref[...] = jnp.zeros_like(ds_ref)
            _fill_window_bias(bias_ref)

        lane = lax.broadcasted_iota(jnp.int32, (1, LANES), 1)
        ind_row = lax.broadcasted_iota(jnp.int32, (SUBLANES, LANES), 0)
        ind_low = lax.broadcasted_iota(jnp.int32, (SUBLANES, LANES), 1) < HEAD_DIM
        indicator = jnp.where(jnp.logical_or(jnp.logical_and(ind_row == 0, ind_low),
                                             jnp.logical_and(ind_row == 1, jnp.logical_not(ind_low))),
                              1.0, 0.0).astype(BF16)
        low = lax.broadcasted_iota(jnp.int32, (2 * WINDOW, LANES), 1) < HEAD_DIM

        def both_heads(even, odd):
            picked = jnp.where(low, even, odd)
            return picked + jnp.concatenate([picked[:, HEAD_DIM:], picked[:, :HEAD_DIM]], axis=1)

        def one_block(b, dsink):
            row0 = pl.multiple_of(b * WINDOW, WINDOW)
            start = i * tq + b * WINDOW
            prev = pl.multiple_of(jnp.maximum(start - WINDOW, 0), WINDOW)
            cur = pl.multiple_of(start, WINDOW)
            kvw = jnp.concatenate([kv_ref[pl.ds(prev, WINDOW), :], kv_ref[pl.ds(cur, WINDOW), :]], axis=0)
            k_halves = _half_tiles(kvw[:, :KV_WIDTH])
            v_halves = _half_tiles(kvw[:, KV_WIDTH:])
            bias = bias_ref[jnp.minimum(start, 1)]
            dk_groups, dv_groups = [], []
            for j in range(N_KV_HEADS):
                qst = _stack_pairs(q_ref, row0, j)
                dost = _stack_pairs(do_ref, row0, j)
                prod = dost.astype(F32) * _stack_pairs(o_ref, row0, j).astype(F32)
                prod_hi = prod.astype(BF16)
                prod_lo = (prod - prod_hi.astype(F32)).astype(BF16)
                deltas = (lax.dot_general(indicator, prod_hi, NT_DIMS, preferred_element_type=F32)
                          + lax.dot_general(indicator, prod_lo, NT_DIMS, preferred_element_type=F32))
                dq_t = jnp.zeros((LANES, STACK), F32)
                dk_par, dv_par = [], []
                for parity in range(2):
                    heads = [j * GROUP + 2 * p + parity for p in range(PAIRS)]
                    kk, vv = k_halves[j][parity], v_halves[j][parity]
                    s = lax.dot_general(kk, qst, NT_DIMS, preferred_element_type=F32) * SCORE_SCALE + bias
                    lse = jnp.concatenate([lse_ref[h:h + 1, pl.ds(row0, WINDOW)] for h in heads], axis=1)
                    p = jnp.exp2(s - lse)
                    dp = lax.dot_general(vv, dost, NT_DIMS, preferred_element_type=F32)
                    delta = deltas[parity:parity + 1, :]
                    dsb = (p * (dp - delta)).astype(BF16)
                    dq_t = dq_t + lax.dot_general(kk, dsb, TN_DIMS, preferred_element_type=F32)
                    dk_par.append(jnp.dot(dsb, qst, preferred_element_type=F32))
                    dv_par.append(jnp.dot(p.astype(BF16), dost, preferred_element_type=F32))
                    sink = _per_pair_row([sink_ref[h] * LOG2E for h in heads])
                    weighted = jnp.exp2(sink - lse) * delta
                    for pr, h in enumerate(heads):
                        dsink = dsink - jnp.where(lane == h, jnp.sum(weighted[:, pr * WINDOW:(pr + 1) * WINDOW]), 0.0)
                dq_st = jnp.transpose((dq_t * ATTN_SCALE).astype(BF16))
                for pr in range(PAIRS):
                    dq_ref[pl.ds(row0, WINDOW), (j * PAIRS + pr) * LANES:(j * PAIRS + pr + 1) * LANES] = (
                        dq_st[pr * WINDOW:(pr + 1) * WINDOW, :])
                dk_groups.append(both_heads(dk_par[0], dk_par[1]))
                dv_groups.append(both_heads(dv_par[0], dv_par[1]))
            blk = jnp.concatenate([jnp.where(low, dk_groups[0], dk_groups[1]) * ATTN_SCALE,
                                   jnp.where(low, dv_groups[0], dv_groups[1])], axis=1)
            acc_ref[pl.ds(prev, WINDOW), :] += blk[:WINDOW, :]
            acc_ref[pl.ds(cur, WINDOW), :] += blk[WINDOW:, :]
            return dsink

        dsink = lax.fori_loop(0, nblk, one_block, jnp.zeros((1, LANES), F32))
        ds_ref[0:1, :] += dsink

        @pl.when(i == nt - 1)
        def _():
            dkv_ref[...] = acc_ref[...].astype(BF16)

    tok = pl.BlockSpec((tq, D_MODEL), lambda i: (i, 0))
    return _call(
        body, "attn_bwd", (nt,), [z, z, dattn, attn, lse, sinks],
        [tok, pl.BlockSpec((t, 2 * KV_WIDTH), lambda i: (0, KV_COL // (2 * KV_WIDTH))), tok, tok,
         pl.BlockSpec((N_Q_HEADS, tq), lambda i: (0, i)), pl.BlockSpec(memory_space=pltpu.SMEM)],
        [jax.ShapeDtypeStruct((t, D_MODEL), BF16), jax.ShapeDtypeStruct((t, 2 * KV_WIDTH), BF16),
         jax.ShapeDtypeStruct((SUBLANES, LANES), F32)],
        [tok, _full((t, 2 * KV_WIDTH)), _full((SUBLANES, LANES))],
        scratch=[pltpu.VMEM((t, 2 * KV_WIDTH), F32), pltpu.VMEM((2, 2 * WINDOW, STACK), F32)], rider=rider)


def _inproj_bwd(dq, drest, dkv, x, dx2, vec, w_t, rider):
    t = x.shape[0]
    tm = min(TOKEN_TILE, t)

    def body(dq_ref, dr_ref, dkv_ref, x_ref, dx2_ref, vec_ref, w_ref, gx_ref, acc_ref, db_ref):
        @pl.when(pl.program_id(0) == 0)
        def _():
            acc_ref[...] = jnp.zeros_like(acc_ref)
            db_ref[...] = jnp.zeros_like(db_ref)

        g = vec_ref[0:1, :]
        sc1 = vec_ref[1:2, :]
        dqb, drb, dkvb = dq_ref[...], dr_ref[...], dkv_ref[...]
        dh = jnp.dot(dqb, w_ref[:REF_KV_COL, :], preferred_element_type=F32)
        dh = dh + jnp.dot(drb, w_ref[REF_REST_COL:, :], preferred_element_type=F32)
        dh = dh + jnp.dot(dkvb, w_ref[REF_KV_COL:REF_REST_COL, :], preferred_element_type=F32)
        db_ref[:, :REF_KV_COL] += jnp.sum(dqb.astype(F32), axis=0, keepdims=True)
        db_ref[:, REF_REST_COL:] += jnp.sum(drb.astype(F32), axis=0, keepdims=True)
        db_ref[:, REF_KV_COL:REF_REST_COL] += jnp.sum(dkvb.astype(F32), axis=0, keepdims=True)
        xf = x_ref[...]
        r = lax.rsqrt(jnp.mean(xf * xf, axis=-1, keepdims=True) + EPS)
        xn = xf * r
        acc_ref[0:1, :] += jnp.sum(dh, axis=0, keepdims=True)
        acc_ref[1:2, :] += jnp.sum(dh * xn * g, axis=0, keepdims=True)
        acc_ref[2:3, :] += jnp.sum(dh * xn * (1.0 + sc1), axis=0, keepdims=True)
        dxn = dh * g * (1.0 + sc1)
        gx_ref[...] = dx2_ref[...].astype(F32) + r * (dxn - xn * jnp.mean(dxn * xn, axis=-1, keepdims=True))

    tok = pl.BlockSpec((tm, D_MODEL), lambda i: (i, 0))
    return _call(
        body, "inproj_bwd", (t // tm,), [dq, drest, dkv, x, dx2, vec, w_t],
        [tok, pl.BlockSpec((tm, REST_WIDTH), lambda i: (i, 0)),
         pl.BlockSpec((tm, 2 * KV_WIDTH), lambda i: (i, 0)), tok, tok,
         _full((SUBLANES, D_MODEL)), _full((IN_WIDTH, D_MODEL))],
        [jax.ShapeDtypeStruct((t, D_MODEL), F32), jax.ShapeDtypeStruct((SUBLANES, D_MODEL), F32),
         jax.ShapeDtypeStruct((1, IN_WIDTH), F32)],
        [tok, _full((SUBLANES, D_MODEL)), _full((1, IN_WIDTH))], rider=rider)


def _weight_grad(b, a, name, bn, rows=None, row0=0, into=None, rider=None):
    t, n = b.shape
    m = a.shape[1]
    rows = n if rows is None else rows
    tk = min(TOKEN_TILE, t)
    for cand in (4 * TOKEN_TILE, 2 * TOKEN_TILE):
        if t % cand == 0 and 2 * cand * (bn + m) * 2 + bn * m * 4 <= WGRAD_VMEM:
            tk = cand
            break
    nk = t // tk
    block0 = row0 // bn

    def body(b_ref, a_ref, *rest):
        out_ref, acc_ref = rest[-2:]
        k = pl.program_id(1)

        @pl.when(k == 0)
        def _():
            acc_ref[...] = jnp.zeros_like(acc_ref)

        acc_ref[...] += lax.dot_general(b_ref[...], a_ref[...], TN_DIMS, preferred_element_type=F32)

        @pl.when(k == nk - 1)
        def _():
            out_ref[...] = acc_ref[...].astype(BF16)

    outs, routs = _call(
        body, name, (n // bn, nk), [b, a] + ([] if into is None else [into]),
        [pl.BlockSpec((tk, bn), lambda j, k: (k, j)), pl.BlockSpec((tk, m), lambda j, k: (k, 0))]
        + ([] if into is None else [ANY]),
        [jax.ShapeDtypeStruct((rows, m), BF16)], [pl.BlockSpec((bn, m), lambda j, k: (block0 + j, 0))],
        scratch=[pltpu.VMEM((bn, m), F32)], rider=rider, aliases=None if into is None else {2: 0})
    return outs[0], routs


def _to_rows(v):
    n = v.shape[0]
    padded = -(-n // (SUBLANES * LANES)) * SUBLANES * LANES
    return jnp.pad(v, (0, padded - n)).reshape(padded // LANES, LANES)


def _vec_rows(*rows):
    stacked = jnp.concatenate([r.reshape(1, D_MODEL) for r in rows], axis=0)
    return jnp.pad(stacked, ((0, SUBLANES - len(rows)), (0, 0)))


def kernel(x, c, w_ada, b_ada, g_mix, w_in, b_in, sinks, conv_w, w_out, g_ffn, w_ffn_in, w_ffn_out, g_final, loss_target, m_w_ada, m_b_ada, m_g_mix, m_w_in, m_b_in, m_sinks, m_conv_w, m_w_out, m_g_ffn, m_w_ffn_in, m_w_ffn_out, m_g_final, v_w_ada, v_b_ada, v_g_mix, v_w_in, v_b_in, v_sinks, v_conv_w, v_w_out, v_g_ffn, v_w_ffn_in, v_w_ffn_out, v_g_final):
    ix, iy, ic = _my_place()
    me = 4 * ix + 2 * iy + ic
    xs = x[0]
    target = loss_target[0]
    ada_cols = w_ada.shape[2]
    conv_cols = conv_w.shape[2]

    wt_in, wt_fi = jnp.transpose(w_in[0]), jnp.transpose(w_ffn_in[0])
    b_cols = lax.dynamic_slice_in_dim(b_ada, me * ada_cols, ada_cols, axis=1)
    g_in, (cast_fi, cast_out, cast_fo), first, mod_all = _gather_first_weight(
        wt_in, [wt_fi, w_out[0], w_ffn_out[0]], _to_rows(jnp.concatenate([c[0], conv_w[0].reshape(-1)])),
        w_ada[0], b_cols)
    first = first.reshape(N_DEV, -1)
    c_all = first[:, :D_MODEL]
    conv_full = jnp.transpose(first[:, D_MODEL:D_MODEL + 3 * conv_cols].reshape(N_DEV, 3, conv_cols), (1, 0, 2))
    conv_full = conv_full.reshape(3, D_MODEL)
    mod = lax.dynamic_index_in_dim(mod_all, me, axis=1, keepdims=False).reshape(N_MOD, D_MODEL)
    sh1, sc1, ga1, sh2, sc2, ga2 = [mod[i:i + 1] for i in range(N_MOD)]
    w_in_t = g_in.reshape(IN_WIDTH, D_MODEL)
    (z, h1), (g_fi, g_out) = _inproj_fwd(xs, _vec_rows(g_mix, sc1, sh1), w_in_t, b_in,
                                         _gather_rider([cast_fi, cast_out]))
    w_fi_t = g_fi.reshape(2 * D_FF, D_MODEL)
    w_out_full = g_out.reshape(D_MODEL, D_MODEL)
    (attn, lse), (g_fo,) = _attn_fwd(z, sinks[0], _gather_rider([cast_fo]))
    w_fo_full = g_fo.reshape(D_FF, D_MODEL)
    merged, x2, h2, oproj = _mix_fwd(
        xs, attn, z, _vec_rows(ga1, g_ffn, sc2, sh2, conv_full[0], conv_full[1], conv_full[2]), w_out_full)
    gu, act = _ffn_fwd(h2, w_fi_t)
    dx3, df, dgu, acc_l = _ffn_out_loss(act, gu, x2, target, _vec_rows(ga2, g_final), w_fo_full)

    gw_fo, _ = _weight_grad(act, df, "wgrad_ffn_out", D_FF)
    gw_fi, _ = _weight_grad(dgu, h2, "wgrad_ffn_in", D_FF)
    blocks_fo = gw_fo.reshape(N_DEV, D_FF // N_DEV, D_MODEL)
    blocks_fi = gw_fi.reshape(N_DEV, 2 * D_FF // N_DEV, D_MODEL)
    (dx2, acc_f), (sib_fo, sib_fi) = _ffn_in_bwd(dgu, x2, dx3, _vec_rows(g_ffn, sc2), w_fi_t,
                                                 _sibling_rider([blocks_fo, blocks_fi]))
    sums_fo, mine_fo = _sibling_sum(_own_blocks(blocks_fo), sib_fo, "sibling_sum_ffn_out")
    sums_fi, mine_fi = _sibling_sum(_own_blocks(blocks_fi), sib_fi, "sibling_sum_ffn_in")
    (dout, dattn, drest, acc_m), (ici_fo, ici_fi) = _mix_bwd(
        dx2, oproj, attn, z, _vec_rows(ga1, conv_full[0], conv_full[1], conv_full[2]), w_out_full,
        _chip_rider([sums_fo, sums_fi]))
    gw_out, _ = _weight_grad(merged, dout, "wgrad_out", D_MODEL)
    blocks_out = gw_out.reshape(N_DEV, D_MODEL // N_DEV, D_MODEL)
    (dq, dkv, dsink), (sib_out,) = _attn_bwd(z, dattn, attn, lse, sinks[0], _sibling_rider([blocks_out]))
    sums_out, mine_out = _sibling_sum(_own_blocks(blocks_out), sib_out, "sibling_sum_out")
    gw_in, (ici_out,) = _weight_grad(drest, h1, "wgrad_in_rest", IN_CHUNK, rows=IN_WIDTH, row0=REF_REST_COL,
                                     rider=_chip_rider([sums_out]))
    gw_in, _ = _weight_grad(dq, h1, "wgrad_in_q", D_MODEL, rows=IN_WIDTH, row0=0, into=gw_in)
    gw_in, _ = _weight_grad(dkv, h1, "wgrad_in_kv", 2 * KV_WIDTH, rows=IN_WIDTH, row0=REF_KV_COL, into=gw_in)
    blocks_in = gw_in.reshape(N_DEV, IN_WIDTH // N_DEV, D_MODEL)
    (sib_in,) = _carry(_sibling_rider([blocks_in]), "sibling_w_in")
    sums_in, mine_in = _sibling_sum(_own_blocks(blocks_in), sib_in, "sibling_sum_in")
    (grad_x, acc_i, db_in), (ici_in,) = _inproj_bwd(dq, drest, dkv, xs, dx2, _vec_rows(g_mix, sc1), w_in_t,
                                                    _chip_rider([sums_in]))

    pieces = [acc_i[0], acc_i[1], acc_m[0], acc_f[0], acc_f[1], acc_l[2],
              acc_i[2], db_in[0], acc_f[2], acc_l[1],
              acc_m[1], acc_m[2], acc_m[3], dsink[0], acc_l[0]]
    offsets = [0]
    for p in pieces:
        offsets.append(offsets[-1] + p.shape[0])
    packed = _small_allgather(_to_rows(jnp.concatenate(pieces)), "gather_small")
    dmod_all = packed.reshape(N_DEV, -1)[:, :N_MOD * D_MODEL]
    total = _sum_devices(packed).reshape(-1)
    part = lambda i: total[offsets[i]:offsets[i + 1]]
    g_b_ada = total[:N_MOD * D_MODEL].reshape(1, -1)
    g_g_mix, g_b_in, g_g_ffn, g_g_final = part(6).reshape(1, -1), part(7).reshape(1, -1), part(8).reshape(1, -1), part(9)
    g_conv_full = jnp.stack([part(10), part(11), part(12)])
    g_conv = lax.dynamic_slice_in_dim(g_conv_full, me * conv_cols, conv_cols, axis=1)[None]
    g_sinks = part(13)[:N_Q_HEADS].reshape(1, -1)
    loss = (0.5 / D_MODEL) * jnp.sum(part(14))
    dmod_cols = lax.dynamic_slice_in_dim(dmod_all, me * ada_cols, ada_cols, axis=1)
    g_w_ada = _ada_weight_grad(c_all, dmod_cols)

    def reduced(mine, ici, w, m, v, name, transposed=False):
        turn = jnp.transpose if transposed else (lambda a: a)
        return tuple(turn(o)[None] for o in _chip_sum_adamw(mine, ici, turn(w[0]), turn(m[0]), turn(v[0]), name))

    d_ada, nm_ada, nv_ada = _adamw(w_ada[0], g_w_ada, m_w_ada[0], v_w_ada[0], "adamw_w_ada")
    small_names = ["b_ada", "g_mix", "b_in", "sinks", "conv_w", "g_ffn", "g_final"]
    small_w = [b_ada, g_mix, b_in, sinks, conv_w, g_ffn, g_final]
    small_m = [m_b_ada, m_g_mix, m_b_in, m_sinks, m_conv_w, m_g_ffn, m_g_final]
    small_v = [v_b_ada, v_g_mix, v_b_in, v_sinks, v_conv_w, v_g_ffn, v_g_final]
    small_g = [g_b_ada, g_g_mix, g_b_in, g_sinks, g_conv, g_g_ffn, g_g_final]
    small_g = [g.reshape(w.shape) for g, w in zip(small_g, small_w)]
    flat = lambda arrs: _to_rows(jnp.concatenate([a.reshape(-1) for a in arrs]))
    sd, snm, snv = _adamw(flat(small_w), flat(small_g), flat(small_m), flat(small_v), "adamw_small")
    sizes = [w.size for w in small_w]
    starts = [sum(sizes[:i]) for i in range(len(sizes))]
    unflat = lambda a: {n: a.reshape(-1)[s:s + z_].reshape(w.shape)
                        for n, s, z_, w in zip(small_names, starts, sizes, small_w)}
    sd, snm, snv = unflat(sd), unflat(snm), unflat(snv)
    sg = dict(zip(small_names, small_g))

    res = {
        "w_ada": (g_w_ada[None], d_ada[None], nm_ada[None], nv_ada[None]),
        "w_in": reduced(mine_in, ici_in, w_in, m_w_in, v_w_in, "adamw_w_in", transposed=True),
        "w_out": reduced(mine_out, ici_out, w_out, m_w_out, v_w_out, "adamw_w_out"),
        "w_ffn_in": reduced(mine_fi, ici_fi, w_ffn_in, m_w_ffn_in, v_w_ffn_in, "adamw_w_ffn_in", transposed=True),
        "w_ffn_out": reduced(mine_fo, ici_fo, w_ffn_out, m_w_ffn_out, v_w_ffn_out, "adamw_w_ffn_out"),
    }
    for n in small_names:
        res[n] = (sg[n], sd[n], snm[n], snv[n])
    order = ["w_ada", "b_ada", "g_mix", "w_in", "b_in", "sinks", "conv_w", "w_out", "g_ffn", "w_ffn_in", "w_ffn_out",
             "g_final"]
    outs = [loss, grad_x[None]]
    for k in range(4):
        outs += [res[n][k] for n in order]
    return tuple(outs)
```

```python
import functools
import math

import jax
import jax.numpy as jnp
from jax import lax
from jax.experimental import pallas as pl
from jax.experimental.pallas import tpu as pltpu

F32 = jnp.float32
BF16 = jnp.bfloat16
GRAD_STREAM = F32

D_MODEL = 1024
HEAD_DIM = 64
N_Q_HEADS = 16
N_KV_HEADS = 2
GROUP = 8
WINDOW = 128
KV_WIDTH = N_KV_HEADS * HEAD_DIM
D_FF = 2816
IN_WIDTH = 6400
N_MOD = 6
EPS = 1e-6
N_DEV = 8
REST_WIDTH = 5 * D_MODEL
KV_COL = D_MODEL + REST_WIDTH
ATTN_SCALE = HEAD_DIM ** -0.5

ADAM_LR = 0.001
ADAM_B1 = 0.9
ADAM_B2 = 0.999
ADAM_EPS = 1e-08
ADAM_WD = 0.01
ADAM_STEP = 10

LANES = 128
SUBLANES = 8
BF16_ROWS = 16
VMEM_LIMIT = 56 * 1024 * 1024
TOKEN_TILE = 512
FF_CHUNK = 256
WGRAD_VMEM = 40 * 1024 * 1024
MESH = pl.DeviceIdType.MESH
ANY = pl.BlockSpec(memory_space=pl.ANY)

NT_DIMS = (((1,), (1,)), ((), ()))
TN_DIMS = (((0,), (0,)), ((), ()))
CHIP_FLIPS = [(0, 0), (1, 0), (0, 1), (1, 1)]


def _full(shape):
    return pl.BlockSpec(shape, lambda *_: (0,) * len(shape))


def _my_place():
    return lax.axis_index("x"), lax.axis_index("y"), lax.axis_index("c")


def _flip(v, bit):
    return 1 - v if bit else v


def _sigmoid(v):
    return 1.0 / (1.0 + jnp.exp2(v * (-1.4426950408889634)))


class _Rider:
    def __init__(self, ins, out_shapes, sem_shapes, first=None, mid=None, last=None, ins_in_vmem=False):
        self.ins, self.out_shapes, self.sem_shapes = list(ins), list(out_shapes), list(sem_shapes)
        self.in_specs = [_full(a.shape) if ins_in_vmem else ANY for a in self.ins]
        self.hooks = [(when, fn) for when, fn in (("first", first), ("mid", mid), ("last", last)) if fn is not None]


def _call(body, name, grid, args, in_specs, out_shape, out_specs, scratch=(), rider=None, aliases=None):
    n_in, n_out, n_scr = len(args), len(out_shape), len(scratch)
    r_in = rider.ins if rider else []
    r_out = rider.out_shapes if rider else []
    r_sem = rider.sem_shapes if rider else []
    nsteps = math.prod(grid)

    def full_body(*refs):
        pos = 0
        groups = []
        for size in (n_in, len(r_in), n_out, len(r_out), n_scr, len(r_sem)):
            groups.append(refs[pos:pos + size])
            pos += size
        ins, rins, outs, routs, scr, rsems = groups
        step = pl.program_id(0)
        for axis in range(1, len(grid)):
            step = step * grid[axis] + pl.program_id(axis)
        at = {"first": 0, "mid": (3 * nsteps) // 4, "last": nsteps - 1}
        hooks = rider.hooks if rider else []
        for when, fn in hooks:
            if when != "last":
                pl.when(step == at[when])(functools.partial(fn, rins, routs, rsems))
        body(*ins, *outs, *scr)
        for when, fn in hooks:
            if when == "last":
                pl.when(step == at[when])(functools.partial(fn, rins, routs, rsems))

    outs = pl.pallas_call(
        full_body, name=name, grid=grid,
        out_shape=list(out_shape) + list(r_out),
        in_specs=list(in_specs) + (rider.in_specs if rider else []),
        out_specs=list(out_specs) + [ANY] * len(r_out),
        scratch_shapes=list(scratch) + list(r_sem),
        input_output_aliases=dict(aliases or {}),
        compiler_params=pltpu.CompilerParams(dimension_semantics=("arbitrary",) * len(grid),
                                             vmem_limit_bytes=VMEM_LIMIT),
    )(*args, *r_in)
    return list(outs[:n_out]), list(outs[n_out:])


def _gather_rider(shards):
    n = len(shards)

    def setup(outs, sems):
        x, y, c = _my_place()
        send_sems, recv_sems, _ = sems
        chips = [(1 - x, y), (x, 1 - y), (1 - x, 1 - y)]

        def block(w, place):
            return outs[w].at[4 * place[0] + 2 * place[1] + place[2]]

        def copy(w, k, place, to, src=None):
            return pltpu.make_async_remote_copy(
                src_ref=block(w, place) if src is None else src, dst_ref=block(w, place),
                send_sem=send_sems.at[w, k], recv_sem=recv_sems.at[w, k], device_id=to, device_id_type=MESH)

        return (x, y, c), (x, y, 1 - c), chips, block, copy

    def first(ins, outs, sems):
        me, sibling, chips, block, copy = setup(outs, sems)
        for w in range(n):
            pltpu.make_async_copy(ins[w], block(w, me), sems[2].at[w]).start()
            copy(w, 0, me, sibling, src=ins[w]).start()
            for j, chip in enumerate(chips):
                copy(w, 1 + j, me, (*chip, me[2]), src=ins[w]).start()

    def mid(ins, outs, sems):
        me, sibling, chips, block, copy = setup(outs, sems)
        for w in range(n):
            for j, chip in enumerate(chips):
                copy(w, 1 + j, (*chip, me[2]), me).wait_recv()
                copy(w, 4 + j, (*chip, me[2]), sibling).start()

    def last(ins, outs, sems):
        me, sibling, chips, block, copy = setup(outs, sems)
        for w in range(n):
            copy(w, 0, sibling, me).wait_recv()
            for j, chip in enumerate(chips):
                copy(w, 4 + j, (*chip, 1 - me[2]), me).wait_recv()
            copy(w, 0, me, sibling, src=ins[w]).wait_send()
            for j, chip in enumerate(chips):
                copy(w, 1 + j, me, (*chip, me[2]), src=ins[w]).wait_send()
                copy(w, 4 + j, (*chip, me[2]), sibling).wait_send()
            pltpu.make_async_copy(ins[w], block(w, me), sems[2].at[w]).wait()

    return _Rider(
        shards, [jax.ShapeDtypeStruct((N_DEV,) + s.shape, BF16) for s in shards],
        [pltpu.SemaphoreType.DMA((n, N_DEV - 1)), pltpu.SemaphoreType.DMA((n, N_DEV - 1)),
         pltpu.SemaphoreType.DMA((n,))],
        first=first, mid=mid, last=last, ins_in_vmem=True)


def _sibling_rider(gblocks):
    n = len(gblocks)

    def copies(ins, outs, sems):
        x, y, c = _my_place()
        send_sems, recv_sems = sems
        made = []
        for w in range(n):
            for f, (fx, fy) in enumerate(CHIP_FLIPS):
                chip = 4 * _flip(x, fx) + 2 * _flip(y, fy)
                made.append(pltpu.make_async_remote_copy(
                    src_ref=ins[w].at[chip + 1 - c], dst_ref=outs[w].at[f], send_sem=send_sems.at[w, f],
                    recv_sem=recv_sems.at[w, f], device_id=(x, y, 1 - c), device_id_type=MESH))
        return made

    def first(ins, outs, sems):
        for cp in copies(ins, outs, sems):
            cp.start()

    def last(ins, outs, sems):
        for cp in copies(ins, outs, sems):
            cp.wait_recv()
            cp.wait_send()

    return _Rider(gblocks, [jax.ShapeDtypeStruct((4,) + g.shape[1:], BF16) for g in gblocks],
                  [pltpu.SemaphoreType.DMA((n, 4))] * 2, first=first, last=last)


def _own_blocks(gblocks):
    x, y, c = _my_place()
    return jnp.stack([lax.dynamic_index_in_dim(gblocks, 4 * _flip(x, fx) + 2 * _flip(y, fy) + c, 0, keepdims=False)
                      for fx, fy in CHIP_FLIPS])


def _chip_rider(sums):
    n = len(sums)

    def copies(ins, outs, sems):
        x, y, c = _my_place()
        send_sems, recv_sems = sems
        made = []
        for w in range(n):
            for f in (1, 2, 3):
                fx, fy = CHIP_FLIPS[f]
                made.append(pltpu.make_async_remote_copy(
                    src_ref=ins[w].at[f - 1], dst_ref=outs[w].at[f - 1], send_sem=send_sems.at[w, f - 1],
                    recv_sem=recv_sems.at[w, f - 1], device_id=(_flip(x, fx), _flip(y, fy), c), device_id_type=MESH))
        return made

    def first(ins, outs, sems):
        for cp in copies(ins, outs, sems):
            cp.start()

    def last(ins, outs, sems):
        for cp in copies(ins, outs, sems):
            cp.wait_recv()
            cp.wait_send()

    return _Rider(sums, [jax.ShapeDtypeStruct(s.shape, BF16) for s in sums],
                  [pltpu.SemaphoreType.DMA((n, 3))] * 2, first=first, last=last)


def _push_to_all(v_ref, out_ref, send_sems, recv_sems, local_sem, wait=True):
    x, y, c = _my_place()
    me = 4 * x + 2 * y + c
    mine = pltpu.make_async_copy(v_ref, out_ref.at[me], local_sem)
    mine.start()
    sends = []
    for k in range(1, N_DEV):
        px, py, pc = _flip(x, k & 4), _flip(y, k & 2), _flip(c, k & 1)
        cp = pltpu.make_async_remote_copy(
            src_ref=v_ref, dst_ref=out_ref.at[me], send_sem=send_sems.at[k - 1], recv_sem=recv_sems.at[k - 1],
            device_id=(px, py, pc), device_id_type=MESH)
        cp.start()
        sends.append(cp)

    def finish():
        for k in range(1, N_DEV):
            px, py, pc = _flip(x, k & 4), _flip(y, k & 2), _flip(c, k & 1)
            pltpu.make_async_remote_copy(
                src_ref=v_ref, dst_ref=out_ref.at[4 * px + 2 * py + pc], send_sem=send_sems.at[k - 1],
                recv_sem=recv_sems.at[k - 1], device_id=(px, py, pc), device_id_type=MESH).wait_recv()
        for cp in sends:
            cp.wait_send()
        mine.wait()

    if wait:
        finish()
    return finish


def _small_allgather(v, name):
    rows = v.shape[0]

    def body(v_ref, out_ref, send_sems, recv_sems, local_sem):
        _push_to_all(v_ref, out_ref, send_sems, recv_sems, local_sem)

    return pl.pallas_call(
        body, name=name,
        out_shape=jax.ShapeDtypeStruct((N_DEV, rows, LANES), F32),
        in_specs=[pl.BlockSpec(memory_space=pltpu.VMEM)],
        out_specs=pl.BlockSpec(memory_space=pltpu.VMEM),
        scratch_shapes=[pltpu.SemaphoreType.DMA((N_DEV - 1,)), pltpu.SemaphoreType.DMA((N_DEV - 1,)),
                        pltpu.SemaphoreType.DMA],
        compiler_params=pltpu.CompilerParams(vmem_limit_bytes=VMEM_LIMIT),
    )(v)


def _gather_first_weight(shard, others, cond_rows, w_ada, b_cols):
    n = len(others)
    ada_cols = w_ada.shape[1]
    c_rows = D_MODEL // LANES

    def body(*refs):
        w_ref, other_refs = refs[0], refs[1:1 + n]
        cond_ref, wada_ref, bcols_ref = refs[1 + n:4 + n]
        out_ref, cast_refs = refs[4 + n], refs[5 + n:5 + 2 * n]
        cond_all_ref, mod_all_ref = refs[5 + 2 * n:7 + 2 * n]
        mine_ref, mod_ref, send_sems, recv_sems, local_sem, small_send, small_recv, small_local = refs[7 + 2 * n:]
        x, y, c = _my_place()
        me, sibling = (x, y, c), (x, y, 1 - c)
        chips = [(1 - x, y), (x, 1 - y), (1 - x, 1 - y)]

        def block(place):
            return out_ref.at[4 * place[0] + 2 * place[1] + place[2]]

        def copy(k, place, to, src=None):
            return pltpu.make_async_remote_copy(
                src_ref=block(place) if src is None else src, dst_ref=block(place),
                send_sem=send_sems.at[k], recv_sem=recv_sems.at[k], device_id=to, device_id_type=MESH)

        finish_cond = _push_to_all(cond_ref, cond_all_ref, small_send.at[0], small_recv.at[0], small_local.at[0],
                                   wait=False)
        mine_ref[...] = w_ref[...].astype(BF16)
        finish_cond()
        local = pltpu.make_async_copy(mine_ref, block(me), local_sem)
        local.start()
        started = [copy(0, me, sibling, src=mine_ref)]
        started += [copy(1 + j, me, (*chip, c), src=mine_ref) for j, chip in enumerate(chips)]
        for cp in started:
            cp.start()
        mod = jnp.zeros((N_DEV, ada_cols), F32) + bcols_ref[...]
        for r in range(c_rows):
            cf = cond_all_ref[:, r, :]
            act = (cf * _sigmoid(cf)).astype(BF16)
            mod = mod + jnp.dot(act, wada_ref[r * LANES:(r + 1) * LANES, :].astype(BF16),
                                preferred_element_type=F32)
        mod_ref[...] = mod
        finish_mod = _push_to_all(mod_ref, mod_all_ref, small_send.at[1], small_recv.at[1], small_local.at[1],
                                  wait=False)
        for o_ref, c_ref in zip(other_refs, cast_refs):
            c_ref[...] = o_ref[...].astype(BF16)
        for j, chip in enumerate(chips):
            copy(1 + j, (*chip, c), me).wait_recv()
            passed = copy(4 + j, (*chip, c), sibling)
            passed.start()
            started.append(passed)
        copy(0, sibling, me).wait_recv()
        for j, chip in enumerate(chips):
            copy(4 + j, (*chip, 1 - c), me).wait_recv()
        finish_mod()
        for cp in started:
            cp.wait_send()
        local.wait()

    vmem = pl.BlockSpec(memory_space=pltpu.VMEM)
    outs = pl.pallas_call(
        body, name="gather_w_in",
        out_shape=[jax.ShapeDtypeStruct((N_DEV,) + shard.shape, BF16)]
        + [jax.ShapeDtypeStruct(o.shape, BF16) for o in others]
        + [jax.ShapeDtypeStruct((N_DEV,) + cond_rows.shape, F32), jax.ShapeDtypeStruct((N_DEV, N_DEV, ada_cols), F32)],
        in_specs=[vmem] * (4 + n),
        out_specs=[ANY] + [vmem] * (n + 2),
        scratch_shapes=[pltpu.VMEM(shard.shape, BF16), pltpu.VMEM((N_DEV, ada_cols), F32),
                        pltpu.SemaphoreType.DMA((N_DEV - 1,)), pltpu.SemaphoreType.DMA((N_DEV - 1,)),
                        pltpu.SemaphoreType.DMA,
                        pltpu.SemaphoreType.DMA((2, N_DEV - 1)), pltpu.SemaphoreType.DMA((2, N_DEV - 1)),
                        pltpu.SemaphoreType.DMA((2,))],
        compiler_params=pltpu.CompilerParams(vmem_limit_bytes=VMEM_LIMIT),
    )(shard, *others, cond_rows, w_ada, b_cols)
    return outs[0], list(outs[1:1 + n]), outs[1 + n], outs[2 + n]


def _carry(rider, name):
    def body(token_ref):
        token_ref[...] = jnp.zeros_like(token_ref)

    _, routs = _call(body, name, (1,), [], [], [jax.ShapeDtypeStruct((SUBLANES, LANES), F32)],
                     [_full((SUBLANES, LANES))], rider=rider)
    return routs


def _ada_weight_grad(c_all, dmod_cols):
    cols = dmod_cols.shape[1]

    def body(c_ref, d_ref, out_ref):
        cf = c_ref[...]
        act = (cf * _sigmoid(cf)).astype(BF16)
        out_ref[...] = lax.dot_general(act, d_ref[...].astype(BF16), TN_DIMS, preferred_element_type=F32)

    return pl.pallas_call(
        body, name="ada_weight_grad",
        out_shape=jax.ShapeDtypeStruct((D_MODEL, cols), F32),
        in_specs=[pl.BlockSpec(memory_space=pltpu.VMEM)] * 2,
        out_specs=pl.BlockSpec(memory_space=pltpu.VMEM),
        compiler_params=pltpu.CompilerParams(vmem_limit_bytes=VMEM_LIMIT),
    )(c_all, dmod_cols)


def _sum_devices(packed):
    def body(p_ref, out_ref):
        total = p_ref[0]
        for d in range(1, N_DEV):
            total = total + p_ref[d]
        out_ref[...] = total

    return pl.pallas_call(
        body, name="sum_devices",
        out_shape=jax.ShapeDtypeStruct(packed.shape[1:], F32),
        in_specs=[pl.BlockSpec(memory_space=pltpu.VMEM)],
        out_specs=pl.BlockSpec(memory_space=pltpu.VMEM),
        compiler_params=pltpu.CompilerParams(vmem_limit_bytes=VMEM_LIMIT),
    )(packed)


def _row_tile(rows, multiple):
    for cand in range(min(rows, 256), 0, -1):
        if rows % cand == 0 and cand % multiple == 0:
            return cand
    return rows


def _adamw_update(w, g, m, v):
    c1 = 1.0 / (1.0 - ADAM_B1 ** ADAM_STEP)
    c2 = 1.0 / (1.0 - ADAM_B2 ** ADAM_STEP)
    nm = ADAM_B1 * m + (1.0 - ADAM_B1) * g
    nv = ADAM_B2 * v + (1.0 - ADAM_B2) * (g * g)
    delta = -ADAM_LR * ((nm * c1) / (jnp.sqrt(nv * c2) + ADAM_EPS) + ADAM_WD * w)
    return delta, nm, nv


def _adamw(w, g, m, v, name):
    rows, cols = w.shape
    tile = _row_tile(rows, SUBLANES)

    def body(w_ref, g_ref, m_ref, v_ref, d_ref, nm_ref, nv_ref):
        d_ref[...], nm_ref[...], nv_ref[...] = _adamw_update(w_ref[...], g_ref[...], m_ref[...], v_ref[...])

    spec = pl.BlockSpec((tile, cols), lambda i: (i, 0))
    outs, _ = _call(body, name, (rows // tile,), [w, g, m, v], [spec] * 4,
                    [jax.ShapeDtypeStruct((rows, cols), F32)] * 3, [spec] * 3)
    return outs


def _sibling_sum(own, sib, name):
    _, r, cdim = own.shape
    tile = _row_tile(r, BF16_ROWS)

    def body(own_ref, sib_ref, sums_ref, mine_ref):
        mine_ref[...] = own_ref[0].astype(F32) + sib_ref[0].astype(F32)
        for f in (1, 2, 3):
            sums_ref[f - 1] = (own_ref[f].astype(F32) + sib_ref[f].astype(F32)).astype(BF16)

    outs, _ = _call(
        body, name, (r // tile,), [own, sib], [pl.BlockSpec((4, tile, cdim), lambda i: (0, i, 0))] * 2,
        [jax.ShapeDtypeStruct((3, r, cdim), BF16), jax.ShapeDtypeStruct((r, cdim), F32)],
        [pl.BlockSpec((3, tile, cdim), lambda i: (0, i, 0)), pl.BlockSpec((tile, cdim), lambda i: (i, 0))])
    return outs


def _chip_sum_adamw(mine, ici, w, m, v, name):
    r, cdim = mine.shape
    tile = _row_tile(r, BF16_ROWS)

    def body(mine_ref, ici_ref, w_ref, m_ref, v_ref, g_ref, d_ref, nm_ref, nv_ref):
        g = mine_ref[...]
        for f in range(3):
            g = g + ici_ref[f].astype(F32)
        g_ref[...] = g
        d_ref[...], nm_ref[...], nv_ref[...] = _adamw_update(w_ref[...], g, m_ref[...], v_ref[...])

    spec = pl.BlockSpec((tile, cdim), lambda i: (i, 0))
    outs, _ = _call(
        body, name, (r // tile,), [mine, ici, w, m, v],
        [spec, pl.BlockSpec((3, tile, cdim), lambda i: (0, i, 0)), spec, spec, spec],
        [jax.ShapeDtypeStruct((r, cdim), F32)] * 4, [spec] * 4)
    return outs


REF_KV_COL = D_MODEL
REF_REST_COL = D_MODEL + 2 * KV_WIDTH
IN_CHUNK = 1280
IN_PIECES = ([(0, 0, D_MODEL)]
             + [(D_MODEL + n * IN_CHUNK, REF_REST_COL + n * IN_CHUNK, IN_CHUNK) for n in range(REST_WIDTH // IN_CHUNK)]
             + [(KV_COL, REF_KV_COL, 2 * KV_WIDTH)])


def _inproj_fwd(x, vec, w_t, b_in, rider):
    t = x.shape[0]
    tm = min(TOKEN_TILE, t)

    def body(x_ref, vec_ref, w_ref, b_ref, z_ref, h_ref):
        xf = x_ref[...]
        r = lax.rsqrt(jnp.mean(xf * xf, axis=-1, keepdims=True) + EPS)
        h = (xf * r) * vec_ref[0:1, :] * (1.0 + vec_ref[1:2, :]) + vec_ref[2:3, :]
        hb = h.astype(BF16)
        h_ref[...] = hb
        for mine, ref, width in IN_PIECES:
            zc = lax.dot_general(hb, w_ref[ref:ref + width, :], NT_DIMS, preferred_element_type=F32)
            z_ref[:, mine:mine + width] = (zc + b_ref[:, ref:ref + width]).astype(BF16)

    return _call(
        body, "inproj_fwd", (t // tm,), [x, vec, w_t, b_in],
        [pl.BlockSpec((tm, D_MODEL), lambda i: (i, 0)), _full((SUBLANES, D_MODEL)),
         _full((IN_WIDTH, D_MODEL)), _full((1, IN_WIDTH))],
        [jax.ShapeDtypeStruct((t, IN_WIDTH), BF16), jax.ShapeDtypeStruct((t, D_MODEL), BF16)],
        [pl.BlockSpec((tm, IN_WIDTH), lambda i: (i, 0)), pl.BlockSpec((tm, D_MODEL), lambda i: (i, 0))],
        rider=rider)


def _window_mask(has_prev):
    qi = lax.broadcasted_iota(jnp.int32, (WINDOW, 2 * WINDOW), 0)
    kj = lax.broadcasted_iota(jnp.int32, (WINDOW, 2 * WINDOW), 1)
    off = jnp.where(has_prev, 0, 4 * WINDOW)
    in_prev = jnp.logical_and(kj < WINDOW, kj > qi + off)
    in_cur = jnp.logical_and(kj >= WINDOW, (kj - WINDOW) <= qi)
    return jnp.logical_or(in_prev, in_cur)


PAIRS = GROUP // 2
STACK = PAIRS * WINDOW


ATTN_BLOCKS = 4
ATTN_BWD_BLOCKS = 2
LOG2E = 1.4426950408889634
LN2 = 0.6931471805599453
SCORE_SCALE = ATTN_SCALE * LOG2E


def _fill_window_bias(bias_ref):
    shape = bias_ref.shape[1:]
    kj = lax.broadcasted_iota(jnp.int32, shape, 0)
    qi = jnp.bitwise_and(lax.broadcasted_iota(jnp.int32, shape, 1), WINDOW - 1)
    in_prev = jnp.logical_and(kj < WINDOW, kj > qi)
    in_cur = jnp.logical_and(kj >= WINDOW, (kj - WINDOW) <= qi)
    bias_ref[0] = jnp.where(in_cur, 0.0, -jnp.inf)
    bias_ref[1] = jnp.where(jnp.logical_or(in_prev, in_cur), 0.0, -jnp.inf)


def _half_tiles(tile):
    low = lax.broadcasted_iota(jnp.int32, tile.shape, 1) < HEAD_DIM
    swapped = jnp.concatenate([tile[:, HEAD_DIM:], tile[:, :HEAD_DIM]], axis=1)
    zero = jnp.zeros_like(tile)
    return ((jnp.where(low, tile, zero), jnp.where(low, zero, swapped)),
            (jnp.where(low, swapped, zero), jnp.where(low, zero, tile)))


def _stack_pairs(ref, row0, j):
    return jnp.concatenate(
        [ref[pl.ds(row0, WINDOW), (j * PAIRS + p) * LANES:(j * PAIRS + p + 1) * LANES] for p in range(PAIRS)], axis=0)


def _per_pair_row(values):
    pair = lax.broadcasted_iota(jnp.int32, (1, STACK), 1) // WINDOW
    row = jnp.full((1, STACK), values[PAIRS - 1], F32)
    for p in range(PAIRS - 2, -1, -1):
        row = jnp.where(pair == p, values[p], row)
    return row


def _attn_fwd(z, sinks, rider):
    t = z.shape[0]
    tq = min(TOKEN_TILE, t)
    nblk = tq // WINDOW

    def body(q_ref, kv_ref, sink_ref, o_ref, lse_ref, bias_ref):
        i = pl.program_id(0)

        @pl.when(i == 0)
        def _():
            _fill_window_bias(bias_ref)

        def window(b):
            row0 = pl.multiple_of(b * WINDOW, WINDOW)
            start = i * tq + b * WINDOW
            prev = pl.multiple_of(jnp.maximum(start - WINDOW, 0), WINDOW)
            cur = pl.multiple_of(start, WINDOW)
            kvw = jnp.concatenate([kv_ref[pl.ds(prev, WINDOW), :], kv_ref[pl.ds(cur, WINDOW), :]], axis=0)
            return row0, _half_tiles(kvw[:, :KV_WIDTH]), _half_tiles(kvw[:, KV_WIDTH:]), bias_ref[jnp.minimum(start, 1)]

        def block_group(bb, carry):
            windows = [window(bb * ATTN_BLOCKS + n) for n in range(ATTN_BLOCKS)]
            for j in range(N_KV_HEADS):
                for pr in range(PAIRS):
                    cols = slice((j * PAIRS + pr) * LANES, (j * PAIRS + pr + 1) * LANES)
                    o_ts = [jnp.zeros((LANES, WINDOW), F32) for _ in windows]
                    for parity in range(2):
                        h = j * GROUP + 2 * pr + parity
                        sink = sink_ref[h] * LOG2E
                        for n, (row0, k_halves, v_halves, bias) in enumerate(windows):
                            qp = q_ref[pl.ds(row0, WINDOW), cols]
                            s = lax.dot_general(k_halves[j][parity], qp, NT_DIMS, preferred_element_type=F32)
                            s = s * SCORE_SCALE + bias
                            m = jnp.maximum(jnp.max(s, axis=0, keepdims=True), sink)
                            p = jnp.exp2(s - m)
                            denom = jnp.sum(p, axis=0, keepdims=True) + jnp.exp2(sink - m)
                            pv = lax.dot_general(v_halves[j][parity], p.astype(BF16), TN_DIMS,
                                                 preferred_element_type=F32)
                            o_ts[n] = o_ts[n] + pv * (1.0 / denom)
                            lse_ref[h:h + 1, pl.ds(row0, WINDOW)] = m + jnp.log2(denom)
                    for n, (row0, _, _, _) in enumerate(windows):
                        o_ref[pl.ds(row0, WINDOW), cols] = jnp.transpose(o_ts[n].astype(BF16))
            return carry

        lax.fori_loop(0, nblk // ATTN_BLOCKS, block_group, 0)

    return _call(
        body, "attn_fwd", (t // tq,), [z, z, sinks],
        [pl.BlockSpec((tq, D_MODEL), lambda i: (i, 0)),
         pl.BlockSpec((t, 2 * KV_WIDTH), lambda i: (0, KV_COL // (2 * KV_WIDTH))),
         pl.BlockSpec(memory_space=pltpu.SMEM)],
        [jax.ShapeDtypeStruct((t, D_MODEL), BF16), jax.ShapeDtypeStruct((N_Q_HEADS, t), F32)],
        [pl.BlockSpec((tq, D_MODEL), lambda i: (i, 0)), pl.BlockSpec((N_Q_HEADS, tq), lambda i: (0, i))],
        scratch=[pltpu.VMEM((2, 2 * WINDOW, WINDOW), F32)], rider=rider)


HALO = BF16_ROWS


def _shift_down(u, uh, k):
    row = lax.broadcasted_iota(jnp.int32, u.shape, 0)
    out = pltpu.roll(u, k, 0)
    for j in range(k):
        out = jnp.where(row == j, uh[HALO - k + j:HALO - k + j + 1, :], out)
    return out


def _shift_up(u, nxt, k):
    n = u.shape[0]
    row = lax.broadcasted_iota(jnp.int32, u.shape, 0)
    out = pltpu.roll(u, n - k, 0)
    for j in range(k):
        out = jnp.where(row == n - k + j, nxt[j:j + 1, :], out)
    return out


def _conv_inputs(cc_ref, cx_ref, hc_ref, hx_ref, first_tile):
    cc = cc_ref[...].astype(F32)
    cx = cx_ref[...].astype(F32)
    u = cc * cx
    uh = jnp.where(first_tile, 0.0, hc_ref[...].astype(F32) * hx_ref[...].astype(F32))
    return cc, cx, u, _shift_down(u, uh, 1), _shift_down(u, uh, 2)


def _z_specs(tm, order):
    per_tile = tm // HALO
    cols = [pl.BlockSpec((tm, D_MODEL), functools.partial(lambda i, j: (order(i), j), j=j)) for j in range(1, 6)]
    halos = [pl.BlockSpec((HALO, D_MODEL),
                          functools.partial(lambda i, j: (jnp.maximum(order(i) * per_tile - 1, 0), j), j=j))
             for j in (2, 3)]
    return cols + halos


def _mix_fwd(x, attn, z, vec, w_out):
    t = x.shape[0]
    tm = min(TOKEN_TILE, t)

    def body(x_ref, a_ref, cb_ref, cc_ref, cx_ref, ga_ref, gc_ref, hc_ref, hx_ref, vec_ref, w_ref,
             m_ref, x2_ref, h2_ref, o_ref):
        i = pl.program_id(0)
        _, _, u, u1, u2 = _conv_inputs(cc_ref, cx_ref, hc_ref, hx_ref, i == 0)
        cv = vec_ref[4:5, :] * u2 + vec_ref[5:6, :] * u1 + vec_ref[6:7, :] * u
        conv = cb_ref[...].astype(F32) * cv
        merged = (_sigmoid(ga_ref[...].astype(F32)) * a_ref[...].astype(F32)
                  + _sigmoid(gc_ref[...].astype(F32)) * conv)
        mb = merged.astype(BF16)
        m_ref[...] = mb
        o = jnp.dot(mb, w_ref[...], preferred_element_type=F32)
        o_ref[...] = o.astype(BF16)
        x2 = x_ref[...] + vec_ref[0:1, :] * o
        x2_ref[...] = x2
        r = lax.rsqrt(jnp.mean(x2 * x2, axis=-1, keepdims=True) + EPS)
        h2 = (x2 * r) * vec_ref[1:2, :] * (1.0 + vec_ref[2:3, :]) + vec_ref[3:4, :]
        h2_ref[...] = h2.astype(BF16)

    tok = pl.BlockSpec((tm, D_MODEL), lambda i: (i, 0))
    outs, _ = _call(
        body, "mix_fwd", (t // tm,), [x, attn, z, z, z, z, z, z, z, vec, w_out],
        [tok, tok] + _z_specs(tm, lambda i: i) + [_full((SUBLANES, D_MODEL)), _full((D_MODEL, D_MODEL))],
        [jax.ShapeDtypeStruct((t, D_MODEL), BF16), jax.ShapeDtypeStruct((t, D_MODEL), F32),
         jax.ShapeDtypeStruct((t, D_MODEL), BF16), jax.ShapeDtypeStruct((t, D_MODEL), BF16)],
        [tok, tok, tok, tok])
    return outs


def _ffn_fwd(h2, w_t):
    t = h2.shape[0]
    tm = min(TOKEN_TILE, t)

    def body(h_ref, w_ref, gu_ref, a_ref):
        hb = h_ref[...]
        for n in range(D_FF // FF_CHUNK):
            lo, hi = n * FF_CHUNK, (n + 1) * FF_CHUNK
            g = lax.dot_general(hb, w_ref[lo:hi, :], NT_DIMS, preferred_element_type=F32)
            u = lax.dot_general(hb, w_ref[D_FF + lo:D_FF + hi, :], NT_DIMS, preferred_element_type=F32)
            sg = _sigmoid(g)
            silu = g * sg
            gu_ref[:, lo:hi] = (u * (sg * (1.0 + g * (1.0 - sg)))).astype(BF16)
            gu_ref[:, D_FF + lo:D_FF + hi] = silu.astype(BF16)
            a_ref[:, lo:hi] = (silu * u).astype(BF16)

    outs, _ = _call(
        body, "ffn_fwd", (t // tm,), [h2, w_t],
        [pl.BlockSpec((tm, D_MODEL), lambda i: (i, 0)), _full((2 * D_FF, D_MODEL))],
        [jax.ShapeDtypeStruct((t, 2 * D_FF), BF16), jax.ShapeDtypeStruct((t, D_FF), BF16)],
        [pl.BlockSpec((tm, 2 * D_FF), lambda i: (i, 0)), pl.BlockSpec((tm, D_FF), lambda i: (i, 0))])
    return outs


def _ffn_out_loss(a, gu, x2, target, vec, w_ffn_out):
    t = a.shape[0]
    tm = min(TOKEN_TILE, t)

    def body(a_ref, gu_ref, x2_ref, t_ref, vec_ref, w_ref, dx3_ref, df_ref, dgu_ref, acc_ref):
        @pl.when(pl.program_id(0) == 0)
        def _():
            acc_ref[...] = jnp.zeros_like(acc_ref)

        ga2 = vec_ref[0:1, :]
        gf = vec_ref[1:2, :]
        f = jnp.dot(a_ref[...], w_ref[...], preferred_element_type=F32)
        x3 = x2_ref[...] + ga2 * f
        r = lax.rsqrt(jnp.mean(x3 * x3, axis=-1, keepdims=True) + EPS)
        xn = x3 * r
        err = xn * gf - t_ref[...]
        dy = err * (1.0 / D_MODEL)
        dxn = dy * gf
        dx3 = r * (dxn - xn * jnp.mean(dxn * xn, axis=-1, keepdims=True))
        dx3_ref[...] = dx3.astype(GRAD_STREAM)
        acc_ref[0:1, :] += jnp.sum(err * err, axis=0, keepdims=True)
        acc_ref[1:2, :] += jnp.sum(dy * xn, axis=0, keepdims=True)
        acc_ref[2:3, :] += jnp.sum(dx3 * f, axis=0, keepdims=True)
        df = (dx3 * ga2).astype(BF16)
        df_ref[...] = df
        for n in range(D_FF // FF_CHUNK):
            lo, hi = n * FF_CHUNK, (n + 1) * FF_CHUNK
            da = lax.dot_general(df, w_ref[lo:hi, :], NT_DIMS, preferred_element_type=F32)
            dgu_ref[:, lo:hi] = (da * gu_ref[:, lo:hi].astype(F32)).astype(BF16)
            dgu_ref[:, D_FF + lo:D_FF + hi] = (da * gu_ref[:, D_FF + lo:D_FF + hi].astype(F32)).astype(BF16)

    tok = pl.BlockSpec((tm, D_MODEL), lambda i: (i, 0))
    outs, _ = _call(
        body, "ffn_out_loss", (t // tm,), [a, gu, x2, target, vec, w_ffn_out],
        [pl.BlockSpec((tm, D_FF), lambda i: (i, 0)), pl.BlockSpec((tm, 2 * D_FF), lambda i: (i, 0)),
         tok, tok, _full((SUBLANES, D_MODEL)), _full((D_FF, D_MODEL))],
        [jax.ShapeDtypeStruct((t, D_MODEL), GRAD_STREAM), jax.ShapeDtypeStruct((t, D_MODEL), BF16),
         jax.ShapeDtypeStruct((t, 2 * D_FF), BF16), jax.ShapeDtypeStruct((SUBLANES, D_MODEL), F32)],
        [tok, tok, pl.BlockSpec((tm, 2 * D_FF), lambda i: (i, 0)), _full((SUBLANES, D_MODEL))])
    return outs


def _ffn_in_bwd(dgu, x2, dx3, vec, w_t, rider):
    t = x2.shape[0]
    tm = min(TOKEN_TILE, t)

    def body(dgu_ref, x2_ref, dx3_ref, vec_ref, wf_ref, dx2_ref, acc_ref):
        @pl.when(pl.program_id(0) == 0)
        def _():
            acc_ref[...] = jnp.zeros_like(acc_ref)

        gffn = vec_ref[0:1, :]
        sc2 = vec_ref[1:2, :]
        dh2 = jnp.dot(dgu_ref[...], wf_ref[...], preferred_element_type=F32)
        x2 = x2_ref[...]
        r = lax.rsqrt(jnp.mean(x2 * x2, axis=-1, keepdims=True) + EPS)
        xn = x2 * r
        acc_ref[0:1, :] += jnp.sum(dh2, axis=0, keepdims=True)
        acc_ref[1:2, :] += jnp.sum(dh2 * xn * gffn, axis=0, keepdims=True)
        acc_ref[2:3, :] += jnp.sum(dh2 * xn * (1.0 + sc2), axis=0, keepdims=True)
        dxn = dh2 * gffn * (1.0 + sc2)
        dx2 = dx3_ref[...].astype(F32) + r * (dxn - xn * jnp.mean(dxn * xn, axis=-1, keepdims=True))
        dx2_ref[...] = dx2.astype(GRAD_STREAM)

    tok = pl.BlockSpec((tm, D_MODEL), lambda i: (i, 0))
    return _call(
        body, "ffn_in_bwd", (t // tm,), [dgu, x2, dx3, vec, w_t],
        [pl.BlockSpec((tm, 2 * D_FF), lambda i: (i, 0)), tok, tok, _full((SUBLANES, D_MODEL)),
         _full((2 * D_FF, D_MODEL))],
        [jax.ShapeDtypeStruct((t, D_MODEL), GRAD_STREAM), jax.ShapeDtypeStruct((SUBLANES, D_MODEL), F32)],
        [tok, _full((SUBLANES, D_MODEL))], rider=rider)


def _mix_bwd(dx2, oproj, attn, z, vec, w_out, rider):
    t = dx2.shape[0]
    tm = min(TOKEN_TILE, t)
    nt = t // tm
    rev = lambda i: nt - 1 - i

    def body(dx2_ref, m_ref, a_ref, cb_ref, cc_ref, cx_ref, ga_ref, gc_ref, hc_ref, hx_ref,
             vec_ref, wo_ref, do_ref, da_ref, dr_ref, acc_ref, carry_ref):
        i = pl.program_id(0)

        @pl.when(i == 0)
        def _():
            acc_ref[...] = jnp.zeros_like(acc_ref)
            carry_ref[...] = jnp.zeros_like(carry_ref)

        ga1 = vec_ref[0:1, :]
        w0, w1, w2 = vec_ref[1:2, :], vec_ref[2:3, :], vec_ref[3:4, :]
        dx2 = dx2_ref[...].astype(F32)
        acc_ref[0:1, :] += jnp.sum(dx2 * m_ref[...].astype(F32), axis=0, keepdims=True)
        do = (dx2 * ga1).astype(BF16)
        do_ref[...] = do
        dm = lax.dot_general(do, wo_ref[...], NT_DIMS, preferred_element_type=F32)

        cc, cx, u, u1, u2 = _conv_inputs(cc_ref, cx_ref, hc_ref, hx_ref, i == nt - 1)
        cv = w0 * u2 + w1 * u1 + w2 * u
        cb = cb_ref[...].astype(F32)
        sa = _sigmoid(ga_ref[...].astype(F32))
        sc = _sigmoid(gc_ref[...].astype(F32))
        attn = a_ref[...].astype(F32)
        da_ref[...] = (dm * sa).astype(BF16)
        dconv = dm * sc
        dr_ref[:, 3 * D_MODEL:4 * D_MODEL] = (dm * attn * sa * (1.0 - sa)).astype(BF16)
        dr_ref[:, 4 * D_MODEL:5 * D_MODEL] = (dconv * (cb * cv) * (1.0 - sc)).astype(BF16)
        dr_ref[:, 0:D_MODEL] = (dconv * cv).astype(BF16)
        dcv = dconv * cb
        acc_ref[1:2, :] += jnp.sum(dcv * u2, axis=0, keepdims=True)
        acc_ref[2:3, :] += jnp.sum(dcv * u1, axis=0, keepdims=True)
        acc_ref[3:4, :] += jnp.sum(dcv * u, axis=0, keepdims=True)
        nxt = carry_ref[...]
        du = w2 * dcv + w1 * _shift_up(dcv, nxt, 1) + w0 * _shift_up(dcv, nxt, 2)
        carry_ref[...] = dcv[0:SUBLANES, :]
        dr_ref[:, D_MODEL:2 * D_MODEL] = (du * cx).astype(BF16)
        dr_ref[:, 2 * D_MODEL:3 * D_MODEL] = (du * cc).astype(BF16)

    tok = pl.BlockSpec((tm, D_MODEL), lambda i: (rev(i), 0))
    return _call(
        body, "mix_bwd", (nt,), [dx2, oproj, attn, z, z, z, z, z, z, z, vec, w_out],
        [tok, tok, tok] + _z_specs(tm, rev) + [_full((SUBLANES, D_MODEL)), _full((D_MODEL, D_MODEL))],
        [jax.ShapeDtypeStruct((t, D_MODEL), BF16), jax.ShapeDtypeStruct((t, D_MODEL), BF16),
         jax.ShapeDtypeStruct((t, REST_WIDTH), BF16), jax.ShapeDtypeStruct((SUBLANES, D_MODEL), F32)],
        [tok, tok, pl.BlockSpec((tm, REST_WIDTH), lambda i: (rev(i), 0)), _full((SUBLANES, D_MODEL))],
        scratch=[pltpu.VMEM((SUBLANES, D_MODEL), F32)], rider=rider)


def _attn_bwd(z, dattn, attn, lse, sinks, rider):
    t = z.shape[0]
    tq = min(TOKEN_TILE, t)
    nblk = tq // WINDOW
    nt = t // tq

    def body(q_ref, kv_ref, do_ref, o_ref, lse_ref, sink_ref, dq_ref, dkv_ref, ds_ref, acc_ref, bias_ref):
        i = pl.program_id(0)

        @pl.when(i == 0)
        def _():
            acc_ref[...] = jnp.zeros_like(acc_ref)
            ds_ref[...] = jnp.zeros_like(ds_ref)
            _fill_window_bias(bias_ref)

        lane = lax.broadcasted_iota(jnp.int32, (1, LANES), 1)
        ind_row = lax.broadcasted_iota(jnp.int32, (SUBLANES, LANES), 0)
        ind_low = lax.broadcasted_iota(jnp.int32, (SUBLANES, LANES), 1) < HEAD_DIM
        indicator = jnp.where(jnp.logical_or(jnp.logical_and(ind_row == 0, ind_low),
                                             jnp.logical_and(ind_row == 1, jnp.logical_not(ind_low))),
                              1.0, 0.0).astype(BF16)
        low = lax.broadcasted_iota(jnp.int32, (2 * WINDOW, LANES), 1) < HEAD_DIM

        def both_heads(even, odd):
            picked = jnp.where(low, even, odd)
            return picked + jnp.concatenate([picked[:, HEAD_DIM:], picked[:, :HEAD_DIM]], axis=1)

        def window(b):
            row0 = pl.multiple_of(b * WINDOW, WINDOW)
            start = i * tq + b * WINDOW
            prev = pl.multiple_of(jnp.maximum(start - WINDOW, 0), WINDOW)
            cur = pl.multiple_of(start, WINDOW)
            kvw = jnp.concatenate([kv_ref[pl.ds(prev, WINDOW), :], kv_ref[pl.ds(cur, WINDOW), :]], axis=0)
            return (row0, prev, cur, _half_tiles(kvw[:, :KV_WIDTH]), _half_tiles(kvw[:, KV_WIDTH:]),
                    bias_ref[jnp.minimum(start, 1)])

        def block_group(bb, dsink):
            windows = [window(bb * ATTN_BWD_BLOCKS + n) for n in range(ATTN_BWD_BLOCKS)]
            dk_groups = [[] for _ in windows]
            dv_groups = [[] for _ in windows]
            for j in range(N_KV_HEADS):
                stacks, deltas, dq_ts = [], [], []
                for row0, _, _, _, _, _ in windows:
                    qst = _stack_pairs(q_ref, row0, j)
                    dost = _stack_pairs(do_ref, row0, j)
                    prod = dost.astype(F32) * _stack_pairs(o_ref, row0, j).astype(F32)
                    prod_hi = prod.astype(BF16)
                    prod_lo = (prod - prod_hi.astype(F32)).astype(BF16)
                    stacks.append((qst, dost))
                    deltas.append(lax.dot_general(indicator, prod_hi, NT_DIMS, preferred_element_type=F32)
                                  + lax.dot_general(indicator, prod_lo, NT_DIMS, preferred_element_type=F32))
                    dq_ts.append(jnp.zeros((LANES, STACK), F32))
                dk_par = [[] for _ in windows]
                dv_par = [[] for _ in windows]
                for parity in range(2):
                    heads = [j * GROUP + 2 * p + parity for p in range(PAIRS)]
                    sink = _per_pair_row([sink_ref[h] * LOG2E for h in heads])
                    for n, (row0, _, _, k_halves, v_halves, bias) in enumerate(windows):
                        qst, dost = stacks[n]
                        kk, vv = k_halves[j][parity], v_halves[j][parity]
                        s = lax.dot_general(kk, qst, NT_DIMS, preferred_element_type=F32) * SCORE_SCALE + bias
                        lse = jnp.concatenate([lse_ref[h:h + 1, pl.ds(row0, WINDOW)] for h in heads], axis=1)
                        p = jnp.exp2(s - lse)
                        dp = lax.dot_general(vv, dost, NT_DIMS, preferred_element_type=F32)
                        delta = deltas[n][parity:parity + 1, :]
                        dsb = (p * (dp - delta)).astype(BF16)
                        dq_ts[n] = dq_ts[n] + lax.dot_general(kk, dsb, TN_DIMS, preferred_element_type=F32)
                        dk_par[n].append(jnp.dot(dsb, qst, preferred_element_type=F32))
                        dv_par[n].append(jnp.dot(p.astype(BF16), dost, preferred_element_type=F32))
                        weighted = jnp.exp2(sink - lse) * delta
                        for pr, h in enumerate(heads):
                            dsink = dsink - jnp.where(
                                lane == h, jnp.sum(weighted[:, pr * WINDOW:(pr + 1) * WINDOW]), 0.0)
                for n, (row0, _, _, _, _, _) in enumerate(windows):
                    dq_st = jnp.transpose((dq_ts[n] * ATTN_SCALE).astype(BF16))
                    for pr in range(PAIRS):
                        dq_ref[pl.ds(row0, WINDOW), (j * PAIRS + pr) * LANES:(j * PAIRS + pr + 1) * LANES] = (
                            dq_st[pr * WINDOW:(pr + 1) * WINDOW, :])
                    dk_groups[n].append(both_heads(dk_par[n][0], dk_par[n][1]))
                    dv_groups[n].append(both_heads(dv_par[n][0], dv_par[n][1]))
            for n, (_, prev, cur, _, _, _) in enumerate(windows):
                blk = jnp.concatenate([jnp.where(low, dk_groups[n][0], dk_groups[n][1]) * ATTN_SCALE,
                                       jnp.where(low, dv_groups[n][0], dv_groups[n][1])], axis=1)
                acc_ref[pl.ds(prev, WINDOW), :] += blk[:WINDOW, :]
                acc_ref[pl.ds(cur, WINDOW), :] += blk[WINDOW:, :]
            return dsink

        dsink = lax.fori_loop(0, nblk // ATTN_BWD_BLOCKS, block_group, jnp.zeros((1, LANES), F32))
        ds_ref[0:1, :] += dsink

        @pl.when(i == nt - 1)
        def _():
            dkv_ref[...] = acc_ref[...].astype(BF16)

    tok = pl.BlockSpec((tq, D_MODEL), lambda i: (i, 0))
    return _call(
        body, "attn_bwd", (nt,), [z, z, dattn, attn, lse, sinks],
        [tok, pl.BlockSpec((t, 2 * KV_WIDTH), lambda i: (0, KV_COL // (2 * KV_WIDTH))), tok, tok,
         pl.BlockSpec((N_Q_HEADS, tq), lambda i: (0, i)), pl.BlockSpec(memory_space=pltpu.SMEM)],
        [jax.ShapeDtypeStruct((t, D_MODEL), BF16), jax.ShapeDtypeStruct((t, 2 * KV_WIDTH), BF16),
         jax.ShapeDtypeStruct((SUBLANES, LANES), F32)],
        [tok, _full((t, 2 * KV_WIDTH)), _full((SUBLANES, LANES))],
        scratch=[pltpu.VMEM((t, 2 * KV_WIDTH), F32), pltpu.VMEM((2, 2 * WINDOW, STACK), F32)], rider=rider)


def _inproj_bwd(dq, drest, dkv, x, dx2, vec, w_t, rider):
    t = x.shape[0]
    tm = min(TOKEN_TILE, t)

    def body(dq_ref, dr_ref, dkv_ref, x_ref, dx2_ref, vec_ref, w_ref, gx_ref, acc_ref, db_ref):
        @pl.when(pl.program_id(0) == 0)
        def _():
            acc_ref[...] = jnp.zeros_like(acc_ref)
            db_ref[...] = jnp.zeros_like(db_ref)

        g = vec_ref[0:1, :]
        sc1 = vec_ref[1:2, :]
        dqb, drb, dkvb = dq_ref[...], dr_ref[...], dkv_ref[...]
        dh = jnp.dot(dqb, w_ref[:REF_KV_COL, :], preferred_element_type=F32)
        dh = dh + jnp.dot(drb, w_ref[REF_REST_COL:, :], preferred_element_type=F32)
        dh = dh + jnp.dot(dkvb, w_ref[REF_KV_COL:REF_REST_COL, :], preferred_element_type=F32)
        db_ref[:, :REF_KV_COL] += jnp.sum(dqb.astype(F32), axis=0, keepdims=True)
        db_ref[:, REF_REST_COL:] += jnp.sum(drb.astype(F32), axis=0, keepdims=True)
        db_ref[:, REF_KV_COL:REF_REST_COL] += jnp.sum(dkvb.astype(F32), axis=0, keepdims=True)
        xf = x_ref[...]
        r = lax.rsqrt(jnp.mean(xf * xf, axis=-1, keepdims=True) + EPS)
        xn = xf * r
        acc_ref[0:1, :] += jnp.sum(dh, axis=0, keepdims=True)
        acc_ref[1:2, :] += jnp.sum(dh * xn * g, axis=0, keepdims=True)
        acc_ref[2:3, :] += jnp.sum(dh * xn * (1.0 + sc1), axis=0, keepdims=True)
        dxn = dh * g * (1.0 + sc1)
        gx_ref[...] = dx2_ref[...].astype(F32) + r * (dxn - xn * jnp.mean(dxn * xn, axis=-1, keepdims=True))

    tok = pl.BlockSpec((tm, D_MODEL), lambda i: (i, 0))
    return _call(
        body, "inproj_bwd", (t // tm,), [dq, drest, dkv, x, dx2, vec, w_t],
        [tok, pl.BlockSpec((tm, REST_WIDTH), lambda i: (i, 0)),
         pl.BlockSpec((tm, 2 * KV_WIDTH), lambda i: (i, 0)), tok, tok,
         _full((SUBLANES, D_MODEL)), _full((IN_WIDTH, D_MODEL))],
        [jax.ShapeDtypeStruct((t, D_MODEL), F32), jax.ShapeDtypeStruct((SUBLANES, D_MODEL), F32),
         jax.ShapeDtypeStruct((1, IN_WIDTH), F32)],
        [tok, _full((SUBLANES, D_MODEL)), _full((1, IN_WIDTH))], rider=rider)


def _weight_grad(b, a, name, bn, rows=None, row0=0, into=None, rider=None):
    t, n = b.shape
    m = a.shape[1]
    rows = n if rows is None else rows
    tk = min(TOKEN_TILE, t)
    for cand in (4 * TOKEN_TILE, 2 * TOKEN_TILE):
        if t % cand == 0 and 2 * cand * (bn + m) * 2 + bn * m * 4 <= WGRAD_VMEM:
            tk = cand
            break
    nk = t // tk
    block0 = row0 // bn

    def body(b_ref, a_ref, *rest):
        out_ref, acc_ref = rest[-2:]
        k = pl.program_id(1)

        @pl.when(k == 0)
        def _():
            acc_ref[...] = jnp.zeros_like(acc_ref)

        acc_ref[...] += lax.dot_general(b_ref[...], a_ref[...], TN_DIMS, preferred_element_type=F32)

        @pl.when(k == nk - 1)
        def _():
            out_ref[...] = acc_ref[...].astype(BF16)

    outs, routs = _call(
        body, name, (n // bn, nk), [b, a] + ([] if into is None else [into]),
        [pl.BlockSpec((tk, bn), lambda j, k: (k, j)), pl.BlockSpec((tk, m), lambda j, k: (k, 0))]
        + ([] if into is None else [ANY]),
        [jax.ShapeDtypeStruct((rows, m), BF16)], [pl.BlockSpec((bn, m), lambda j, k: (block0 + j, 0))],
        scratch=[pltpu.VMEM((bn, m), F32)], rider=rider, aliases=None if into is None else {2: 0})
    return outs[0], routs


def _to_rows(v):
    n = v.shape[0]
    padded = -(-n // (SUBLANES * LANES)) * SUBLANES * LANES
    return jnp.pad(v, (0, padded - n)).reshape(padded // LANES, LANES)


def _vec_rows(*rows):
    stacked = jnp.concatenate([r.reshape(1, D_MODEL) for r in rows], axis=0)
    return jnp.pad(stacked, ((0, SUBLANES - len(rows)), (0, 0)))


def kernel(x, c, w_ada, b_ada, g_mix, w_in, b_in, sinks, conv_w, w_out, g_ffn, w_ffn_in, w_ffn_out, g_final, loss_target, m_w_ada, m_b_ada, m_g_mix, m_w_in, m_b_in, m_sinks, m_conv_w, m_w_out, m_g_ffn, m_w_ffn_in, m_w_ffn_out, m_g_final, v_w_ada, v_b_ada, v_g_mix, v_w_in, v_b_in, v_sinks, v_conv_w, v_w_out, v_g_ffn, v_w_ffn_in, v_w_ffn_out, v_g_final):
    ix, iy, ic = _my_place()
    me = 4 * ix + 2 * iy + ic
    xs = x[0]
    target = loss_target[0]
    ada_cols = w_ada.shape[2]
    conv_cols = conv_w.shape[2]

    wt_in, wt_fi = jnp.transpose(w_in[0]), jnp.transpose(w_ffn_in[0])
    b_cols = lax.dynamic_slice_in_dim(b_ada, me * ada_cols, ada_cols, axis=1)
    g_in, (cast_fi, cast_out, cast_fo), first, mod_all = _gather_first_weight(
        wt_in, [wt_fi, w_out[0], w_ffn_out[0]], _to_rows(jnp.concatenate([c[0], conv_w[0].reshape(-1)])),
        w_ada[0], b_cols)
    first = first.reshape(N_DEV, -1)
    c_all = first[:, :D_MODEL]
    conv_full = jnp.transpose(first[:, D_MODEL:D_MODEL + 3 * conv_cols].reshape(N_DEV, 3, conv_cols), (1, 0, 2))
    conv_full = conv_full.reshape(3, D_MODEL)
    mod = lax.dynamic_index_in_dim(mod_all, me, axis=1, keepdims=False).reshape(N_MOD, D_MODEL)
    sh1, sc1, ga1, sh2, sc2, ga2 = [mod[i:i + 1] for i in range(N_MOD)]
    w_in_t = g_in.reshape(IN_WIDTH, D_MODEL)
    (z, h1), (g_fi, g_out) = _inproj_fwd(xs, _vec_rows(g_mix, sc1, sh1), w_in_t, b_in,
                                         _gather_rider([cast_fi, cast_out]))
    w_fi_t = g_fi.reshape(2 * D_FF, D_MODEL)
    w_out_full = g_out.reshape(D_MODEL, D_MODEL)
    (attn, lse), (g_fo,) = _attn_fwd(z, sinks[0], _gather_rider([cast_fo]))
    w_fo_full = g_fo.reshape(D_FF, D_MODEL)
    merged, x2, h2, oproj = _mix_fwd(
        xs, attn, z, _vec_rows(ga1, g_ffn, sc2, sh2, conv_full[0], conv_full[1], conv_full[2]), w_out_full)
    gu, act = _ffn_fwd(h2, w_fi_t)
    dx3, df, dgu, acc_l = _ffn_out_loss(act, gu, x2, target, _vec_rows(ga2, g_final), w_fo_full)

    gw_fo, _ = _weight_grad(act, df, "wgrad_ffn_out", D_FF)
    gw_fi, _ = _weight_grad(dgu, h2, "wgrad_ffn_in", D_FF)
    blocks_fo = gw_fo.reshape(N_DEV, D_FF // N_DEV, D_MODEL)
    blocks_fi = gw_fi.reshape(N_DEV, 2 * D_FF // N_DEV, D_MODEL)
    (dx2, acc_f), (sib_fo, sib_fi) = _ffn_in_bwd(dgu, x2, dx3, _vec_rows(g_ffn, sc2), w_fi_t,
                                                 _sibling_rider([blocks_fo, blocks_fi]))
    sums_fo, mine_fo = _sibling_sum(_own_blocks(blocks_fo), sib_fo, "sibling_sum_ffn_out")
    sums_fi, mine_fi = _sibling_sum(_own_blocks(blocks_fi), sib_fi, "sibling_sum_ffn_in")
    (dout, dattn, drest, acc_m), (ici_fo, ici_fi) = _mix_bwd(
        dx2, oproj, attn, z, _vec_rows(ga1, conv_full[0], conv_full[1], conv_full[2]), w_out_full,
        _chip_rider([sums_fo, sums_fi]))
    gw_out, _ = _weight_grad(merged, dout, "wgrad_out", D_MODEL)
    blocks_out = gw_out.reshape(N_DEV, D_MODEL // N_DEV, D_MODEL)
    (dq, dkv, dsink), (sib_out,) = _attn_bwd(z, dattn, attn, lse, sinks[0], _sibling_rider([blocks_out]))
    sums_out, mine_out = _sibling_sum(_own_blocks(blocks_out), sib_out, "sibling_sum_out")
    gw_in, (ici_out,) = _weight_grad(drest, h1, "wgrad_in_rest", IN_CHUNK, rows=IN_WIDTH, row0=REF_REST_COL,
                                     rider=_chip_rider([sums_out]))
    gw_in, _ = _weight_grad(dq, h1, "wgrad_in_q", D_MODEL, rows=IN_WIDTH, row0=0, into=gw_in)
    gw_in, _ = _weight_grad(dkv, h1, "wgrad_in_kv", 2 * KV_WIDTH, rows=IN_WIDTH, row0=REF_KV_COL, into=gw_in)
    blocks_in = gw_in.reshape(N_DEV, IN_WIDTH // N_DEV, D_MODEL)
    (sib_in,) = _carry(_sibling_rider([blocks_in]), "sibling_w_in")
    sums_in, mine_in = _sibling_sum(_own_blocks(blocks_in), sib_in, "sibling_sum_in")
    (grad_x, acc_i, db_in), (ici_in,) = _inproj_bwd(dq, drest, dkv, xs, dx2, _vec_rows(g_mix, sc1), w_in_t,
                                                    _chip_rider([sums_in]))

    pieces = [acc_i[0], acc_i[1], acc_m[0], acc_f[0], acc_f[1], acc_l[2],
              acc_i[2], db_in[0], acc_f[2], acc_l[1],
              acc_m[1], acc_m[2], acc_m[3], dsink[0], acc_l[0]]
    offsets = [0]
    for p in pieces:
        offsets.append(offsets[-1] + p.shape[0])
    packed = _small_allgather(_to_rows(jnp.concatenate(pieces)), "gather_small")
    dmod_all = packed.reshape(N_DEV, -1)[:, :N_MOD * D_MODEL]
    total = _sum_devices(packed).reshape(-1)
    part = lambda i: total[offsets[i]:offsets[i + 1]]
    g_b_ada = total[:N_MOD * D_MODEL].reshape(1, -1)
    g_g_mix, g_b_in, g_g_ffn, g_g_final = part(6).reshape(1, -1), part(7).reshape(1, -1), part(8).reshape(1, -1), part(9)
    g_conv_full = jnp.stack([part(10), part(11), part(12)])
    g_conv = lax.dynamic_slice_in_dim(g_conv_full, me * conv_cols, conv_cols, axis=1)[None]
    g_sinks = part(13)[:N_Q_HEADS].reshape(1, -1)
    loss = (0.5 / D_MODEL) * jnp.sum(part(14))
    dmod_cols = lax.dynamic_slice_in_dim(dmod_all, me * ada_cols, ada_cols, axis=1)
    g_w_ada = _ada_weight_grad(c_all, dmod_cols)

    def reduced(mine, ici, w, m, v, name, transposed=False):
        turn = jnp.transpose if transposed else (lambda a: a)
        return tuple(turn(o)[None] for o in _chip_sum_adamw(mine, ici, turn(w[0]), turn(m[0]), turn(v[0]), name))

    d_ada, nm_ada, nv_ada = _adamw(w_ada[0], g_w_ada, m_w_ada[0], v_w_ada[0], "adamw_w_ada")
    small_names = ["b_ada", "g_mix", "b_in", "sinks", "conv_w", "g_ffn", "g_final"]
    small_w = [b_ada, g_mix, b_in, sinks, conv_w, g_ffn, g_final]
    small_m = [m_b_ada, m_g_mix, m_b_in, m_sinks, m_conv_w, m_g_ffn, m_g_final]
    small_v = [v_b_ada, v_g_mix, v_b_in, v_sinks, v_conv_w, v_g_ffn, v_g_final]
    small_g = [g_b_ada, g_g_mix, g_b_in, g_sinks, g_conv, g_g_ffn, g_g_final]
    small_g = [g.reshape(w.shape) for g, w in zip(small_g, small_w)]
    flat = lambda arrs: _to_rows(jnp.concatenate([a.reshape(-1) for a in arrs]))
    sd, snm, snv = _adamw(flat(small_w), flat(small_g), flat(small_m), flat(small_v), "adamw_small")
    sizes = [w.size for w in small_w]
    starts = [sum(sizes[:i]) for i in range(len(sizes))]
    unflat = lambda a: {n: a.reshape(-1)[s:s + z_].reshape(w.shape)
                        for n, s, z_, w in zip(small_names, starts, sizes, small_w)}
    sd, snm, snv = unflat(sd), unflat(snm), unflat(snv)
    sg = dict(zip(small_names, small_g))

    res = {
        "w_ada": (g_w_ada[None], d_ada[None], nm_ada[None], nv_ada[None]),
        "w_in": reduced(mine_in, ici_in, w_in, m_w_in, v_w_in, "adamw_w_in", transposed=True),
        "w_out": reduced(mine_out, ici_out, w_out, m_w_out, v_w_out, "adamw_w_out"),
        "w_ffn_in": reduced(mine_fi, ici_fi, w_ffn_in, m_w_ffn_in, v_w_ffn_in, "adamw_w_ffn_in", transposed=True),
        "w_ffn_out": reduced(mine_fo, ici_fo, w_ffn_out, m_w_ffn_out, v_w_ffn_out, "adamw_w_ffn_out"),
    }
    for n in small_names:
        res[n] = (sg[n], sd[n], snm[n], snv[n])
    order = ["w_ada", "b_ada", "g_mix", "w_in", "b_in", "sinks", "conv_w", "w_out", "g_ffn", "w_ffn_in", "w_ffn_out",
             "g_final"]
    outs = [loss, grad_x[None]]
    for k in range(4):
        outs += [res[n][k] for n in order]
    return tuple(outs)
```

```python
import functools
import math

import jax
import jax.numpy as jnp
from jax import lax
from jax.experimental import pallas as pl
from jax.experimental.pallas import tpu as pltpu

F32 = jnp.float32
BF16 = jnp.bfloat16
GRAD_STREAM = F32

D_MODEL = 1024
HEAD_DIM = 64
N_Q_HEADS = 16
N_KV_HEADS = 2
GROUP = 8
WINDOW = 128
KV_WIDTH = N_KV_HEADS * HEAD_DIM
D_FF = 2816
IN_WIDTH = 6400
N_MOD = 6
EPS = 1e-6
N_DEV = 8
REST_WIDTH = 5 * D_MODEL
KV_COL = D_MODEL + REST_WIDTH
ATTN_SCALE = HEAD_DIM ** -0.5

ADAM_LR = 0.001
ADAM_B1 = 0.9
ADAM_B2 = 0.999
ADAM_EPS = 1e-08
ADAM_WD = 0.01
ADAM_STEP = 10

LANES = 128
SUBLANES = 8
BF16_ROWS = 16
VMEM_LIMIT = 56 * 1024 * 1024
TOKEN_TILE = 512
FF_CHUNK = 256
ROW_PARTS = 2
WGRAD_VMEM = 40 * 1024 * 1024
MESH = pl.DeviceIdType.MESH
ANY = pl.BlockSpec(memory_space=pl.ANY)

NT_DIMS = (((1,), (1,)), ((), ()))
TN_DIMS = (((0,), (0,)), ((), ()))
CHIP_FLIPS = [(0, 0), (1, 0), (0, 1), (1, 1)]


def _full(shape):
    return pl.BlockSpec(shape, lambda *_: (0,) * len(shape))


def _my_place():
    return lax.axis_index("x"), lax.axis_index("y"), lax.axis_index("c")


def _flip(v, bit):
    return 1 - v if bit else v


def _sigmoid(v):
    return 1.0 / (1.0 + jnp.exp2(v * (-1.4426950408889634)))


class _Rider:
    def __init__(self, ins, out_shapes, sem_shapes, first=None, mid=None, last=None, ins_in_vmem=False):
        self.ins, self.out_shapes, self.sem_shapes = list(ins), list(out_shapes), list(sem_shapes)
        self.in_specs = [_full(a.shape) if ins_in_vmem else ANY for a in self.ins]
        self.hooks = [(when, fn) for when, fn in (("first", first), ("mid", mid), ("last", last)) if fn is not None]


def _call(body, name, grid, args, in_specs, out_shape, out_specs, scratch=(), rider=None, aliases=None):
    n_in, n_out, n_scr = len(args), len(out_shape), len(scratch)
    r_in = rider.ins if rider else []
    r_out = rider.out_shapes if rider else []
    r_sem = rider.sem_shapes if rider else []
    nsteps = math.prod(grid)

    def full_body(*refs):
        pos = 0
        groups = []
        for size in (n_in, len(r_in), n_out, len(r_out), n_scr, len(r_sem)):
            groups.append(refs[pos:pos + size])
            pos += size
        ins, rins, outs, routs, scr, rsems = groups
        step = pl.program_id(0)
        for axis in range(1, len(grid)):
            step = step * grid[axis] + pl.program_id(axis)
        at = {"first": 0, "mid": (3 * nsteps) // 4, "last": nsteps - 1}
        hooks = rider.hooks if rider else []
        for when, fn in hooks:
            if when != "last":
                pl.when(step == at[when])(functools.partial(fn, rins, routs, rsems))
        body(*ins, *outs, *scr)
        for when, fn in hooks:
            if when == "last":
                pl.when(step == at[when])(functools.partial(fn, rins, routs, rsems))

    outs = pl.pallas_call(
        full_body, name=name, grid=grid,
        out_shape=list(out_shape) + list(r_out),
        in_specs=list(in_specs) + (rider.in_specs if rider else []),
        out_specs=list(out_specs) + [ANY] * len(r_out),
        scratch_shapes=list(scratch) + list(r_sem),
        input_output_aliases=dict(aliases or {}),
        compiler_params=pltpu.CompilerParams(dimension_semantics=("arbitrary",) * len(grid),
                                             vmem_limit_bytes=VMEM_LIMIT),
    )(*args, *r_in)
    return list(outs[:n_out]), list(outs[n_out:])


def _gather_rider(shards):
    n = len(shards)

    def setup(outs, sems):
        x, y, c = _my_place()
        send_sems, recv_sems, _ = sems
        chips = [(1 - x, y), (x, 1 - y), (1 - x, 1 - y)]

        def block(w, place):
            return outs[w].at[4 * place[0] + 2 * place[1] + place[2]]

        def copy(w, k, place, to, src=None):
            return pltpu.make_async_remote_copy(
                src_ref=block(w, place) if src is None else src, dst_ref=block(w, place),
                send_sem=send_sems.at[w, k], recv_sem=recv_sems.at[w, k], device_id=to, device_id_type=MESH)

        return (x, y, c), (x, y, 1 - c), chips, block, copy

    def first(ins, outs, sems):
        me, sibling, chips, block, copy = setup(outs, sems)
        for w in range(n):
            pltpu.make_async_copy(ins[w], block(w, me), sems[2].at[w]).start()
            copy(w, 0, me, sibling, src=ins[w]).start()
            for j, chip in enumerate(chips):
                copy(w, 1 + j, me, (*chip, me[2]), src=ins[w]).start()

    def mid(ins, outs, sems):
        me, sibling, chips, block, copy = setup(outs, sems)
        for w in range(n):
            for j, chip in enumerate(chips):
                copy(w, 1 + j, (*chip, me[2]), me).wait_recv()
                copy(w, 4 + j, (*chip, me[2]), sibling).start()

    def last(ins, outs, sems):
        me, sibling, chips, block, copy = setup(outs, sems)
        for w in range(n):
            copy(w, 0, sibling, me).wait_recv()
            for j, chip in enumerate(chips):
                copy(w, 4 + j, (*chip, 1 - me[2]), me).wait_recv()
            copy(w, 0, me, sibling, src=ins[w]).wait_send()
            for j, chip in enumerate(chips):
                copy(w, 1 + j, me, (*chip, me[2]), src=ins[w]).wait_send()
                copy(w, 4 + j, (*chip, me[2]), sibling).wait_send()
            pltpu.make_async_copy(ins[w], block(w, me), sems[2].at[w]).wait()

    return _Rider(
        shards, [jax.ShapeDtypeStruct((N_DEV,) + s.shape, BF16) for s in shards],
        [pltpu.SemaphoreType.DMA((n, N_DEV - 1)), pltpu.SemaphoreType.DMA((n, N_DEV - 1)),
         pltpu.SemaphoreType.DMA((n,))],
        first=first, mid=mid, last=last, ins_in_vmem=True)


def _sibling_rider(gblocks):
    n = len(gblocks)

    def copies(ins, outs, sems):
        x, y, c = _my_place()
        send_sems, recv_sems = sems
        made = []
        for w in range(n):
            for f, (fx, fy) in enumerate(CHIP_FLIPS):
                chip = 4 * _flip(x, fx) + 2 * _flip(y, fy)
                made.append(pltpu.make_async_remote_copy(
                    src_ref=ins[w].at[chip + 1 - c], dst_ref=outs[w].at[f], send_sem=send_sems.at[w, f],
                    recv_sem=recv_sems.at[w, f], device_id=(x, y, 1 - c), device_id_type=MESH))
        return made

    def first(ins, outs, sems):
        for cp in copies(ins, outs, sems):
            cp.start()

    def last(ins, outs, sems):
        for cp in copies(ins, outs, sems):
            cp.wait_recv()
            cp.wait_send()

    return _Rider(gblocks, [jax.ShapeDtypeStruct((4,) + g.shape[1:], BF16) for g in gblocks],
                  [pltpu.SemaphoreType.DMA((n, 4))] * 2, first=first, last=last)


def _own_blocks(gblocks):
    x, y, c = _my_place()
    return jnp.stack([lax.dynamic_index_in_dim(gblocks, 4 * _flip(x, fx) + 2 * _flip(y, fy) + c, 0, keepdims=False)
                      for fx, fy in CHIP_FLIPS])


def _chip_rider(sums):
    n = len(sums)

    def copies(ins, outs, sems):
        x, y, c = _my_place()
        send_sems, recv_sems = sems
        made = []
        for w in range(n):
            for f in (1, 2, 3):
                fx, fy = CHIP_FLIPS[f]
                made.append(pltpu.make_async_remote_copy(
                    src_ref=ins[w].at[f - 1], dst_ref=outs[w].at[f - 1], send_sem=send_sems.at[w, f - 1],
                    recv_sem=recv_sems.at[w, f - 1], device_id=(_flip(x, fx), _flip(y, fy), c), device_id_type=MESH))
        return made

    def first(ins, outs, sems):
        for cp in copies(ins, outs, sems):
            cp.start()

    def last(ins, outs, sems):
        for cp in copies(ins, outs, sems):
            cp.wait_recv()
            cp.wait_send()

    return _Rider(sums, [jax.ShapeDtypeStruct(s.shape, BF16) for s in sums],
                  [pltpu.SemaphoreType.DMA((n, 3))] * 2, first=first, last=last)


def _push_to_all(v_ref, out_ref, send_sems, recv_sems, local_sem, wait=True):
    x, y, c = _my_place()
    me = 4 * x + 2 * y + c
    mine = pltpu.make_async_copy(v_ref, out_ref.at[me], local_sem)
    mine.start()
    sends = []
    for k in range(1, N_DEV):
        px, py, pc = _flip(x, k & 4), _flip(y, k & 2), _flip(c, k & 1)
        cp = pltpu.make_async_remote_copy(
            src_ref=v_ref, dst_ref=out_ref.at[me], send_sem=send_sems.at[k - 1], recv_sem=recv_sems.at[k - 1],
            device_id=(px, py, pc), device_id_type=MESH)
        cp.start()
        sends.append(cp)

    def finish():
        for k in range(1, N_DEV):
            px, py, pc = _flip(x, k & 4), _flip(y, k & 2), _flip(c, k & 1)
            pltpu.make_async_remote_copy(
                src_ref=v_ref, dst_ref=out_ref.at[4 * px + 2 * py + pc], send_sem=send_sems.at[k - 1],
                recv_sem=recv_sems.at[k - 1], device_id=(px, py, pc), device_id_type=MESH).wait_recv()
        for cp in sends:
            cp.wait_send()
        mine.wait()

    if wait:
        finish()
    return finish


def _small_allgather(v, name):
    rows = v.shape[0]

    def body(v_ref, out_ref, send_sems, recv_sems, local_sem):
        _push_to_all(v_ref, out_ref, send_sems, recv_sems, local_sem)

    return pl.pallas_call(
        body, name=name,
        out_shape=jax.ShapeDtypeStruct((N_DEV, rows, LANES), F32),
        in_specs=[pl.BlockSpec(memory_space=pltpu.VMEM)],
        out_specs=pl.BlockSpec(memory_space=pltpu.VMEM),
        scratch_shapes=[pltpu.SemaphoreType.DMA((N_DEV - 1,)), pltpu.SemaphoreType.DMA((N_DEV - 1,)),
                        pltpu.SemaphoreType.DMA],
        compiler_params=pltpu.CompilerParams(vmem_limit_bytes=VMEM_LIMIT),
    )(v)


def _gather_first_weight(shard, others, cond_rows, w_ada, b_cols):
    n = len(others)
    ada_cols = w_ada.shape[1]
    c_rows = D_MODEL // LANES

    def body(*refs):
        w_ref, other_refs = refs[0], refs[1:1 + n]
        cond_ref, wada_ref, bcols_ref = refs[1 + n:4 + n]
        out_ref, cast_refs = refs[4 + n], refs[5 + n:5 + 2 * n]
        cond_all_ref, mod_all_ref = refs[5 + 2 * n:7 + 2 * n]
        mine_ref, mod_ref, send_sems, recv_sems, local_sem, small_send, small_recv, small_local = refs[7 + 2 * n:]
        x, y, c = _my_place()
        me, sibling = (x, y, c), (x, y, 1 - c)
        chips = [(1 - x, y), (x, 1 - y), (1 - x, 1 - y)]

        def block(place):
            return out_ref.at[4 * place[0] + 2 * place[1] + place[2]]

        def copy(k, place, to, src=None):
            return pltpu.make_async_remote_copy(
                src_ref=block(place) if src is None else src, dst_ref=block(place),
                send_sem=send_sems.at[k], recv_sem=recv_sems.at[k], device_id=to, device_id_type=MESH)

        finish_cond = _push_to_all(cond_ref, cond_all_ref, small_send.at[0], small_recv.at[0], small_local.at[0],
                                   wait=False)
        mine_ref[...] = w_ref[...].astype(BF16)
        finish_cond()
        local = pltpu.make_async_copy(mine_ref, block(me), local_sem)
        local.start()
        started = [copy(0, me, sibling, src=mine_ref)]
        started += [copy(1 + j, me, (*chip, c), src=mine_ref) for j, chip in enumerate(chips)]
        for cp in started:
            cp.start()
        mod = jnp.zeros((N_DEV, ada_cols), F32) + bcols_ref[...]
        for r in range(c_rows):
            cf = cond_all_ref[:, r, :]
            act = (cf * _sigmoid(cf)).astype(BF16)
            mod = mod + jnp.dot(act, wada_ref[r * LANES:(r + 1) * LANES, :].astype(BF16),
                                preferred_element_type=F32)
        mod_ref[...] = mod
        finish_mod = _push_to_all(mod_ref, mod_all_ref, small_send.at[1], small_recv.at[1], small_local.at[1],
                                  wait=False)
        for o_ref, c_ref in zip(other_refs, cast_refs):
            c_ref[...] = o_ref[...].astype(BF16)
        for j, chip in enumerate(chips):
            copy(1 + j, (*chip, c), me).wait_recv()
            passed = copy(4 + j, (*chip, c), sibling)
            passed.start()
            started.append(passed)
        copy(0, sibling, me).wait_recv()
        for j, chip in enumerate(chips):
            copy(4 + j, (*chip, 1 - c), me).wait_recv()
        finish_mod()
        for cp in started:
            cp.wait_send()
        local.wait()

    vmem = pl.BlockSpec(memory_space=pltpu.VMEM)
    outs = pl.pallas_call(
        body, name="gather_w_in",
        out_shape=[jax.ShapeDtypeStruct((N_DEV,) + shard.shape, BF16)]
        + [jax.ShapeDtypeStruct(o.shape, BF16) for o in others]
        + [jax.ShapeDtypeStruct((N_DEV,) + cond_rows.shape, F32), jax.ShapeDtypeStruct((N_DEV, N_DEV, ada_cols), F32)],
        in_specs=[vmem] * (4 + n),
        out_specs=[ANY] + [vmem] * (n + 2),
        scratch_shapes=[pltpu.VMEM(shard.shape, BF16), pltpu.VMEM((N_DEV, ada_cols), F32),
                        pltpu.SemaphoreType.DMA((N_DEV - 1,)), pltpu.SemaphoreType.DMA((N_DEV - 1,)),
                        pltpu.SemaphoreType.DMA,
                        pltpu.SemaphoreType.DMA((2, N_DEV - 1)), pltpu.SemaphoreType.DMA((2, N_DEV - 1)),
                        pltpu.SemaphoreType.DMA((2,))],
        compiler_params=pltpu.CompilerParams(vmem_limit_bytes=VMEM_LIMIT),
    )(shard, *others, cond_rows, w_ada, b_cols)
    return outs[0], list(outs[1:1 + n]), outs[1 + n], outs[2 + n]


def _carry(rider, name):
    def body(token_ref):
        token_ref[...] = jnp.zeros_like(token_ref)

    _, routs = _call(body, name, (1,), [], [], [jax.ShapeDtypeStruct((SUBLANES, LANES), F32)],
                     [_full((SUBLANES, LANES))], rider=rider)
    return routs


def _ada_weight_grad(c_all, dmod_cols):
    cols = dmod_cols.shape[1]

    def body(c_ref, d_ref, out_ref):
        cf = c_ref[...]
        act = (cf * _sigmoid(cf)).astype(BF16)
        out_ref[...] = lax.dot_general(act, d_ref[...].astype(BF16), TN_DIMS, preferred_element_type=F32)

    return pl.pallas_call(
        body, name="ada_weight_grad",
        out_shape=jax.ShapeDtypeStruct((D_MODEL, cols), F32),
        in_specs=[pl.BlockSpec(memory_space=pltpu.VMEM)] * 2,
        out_specs=pl.BlockSpec(memory_space=pltpu.VMEM),
        compiler_params=pltpu.CompilerParams(vmem_limit_bytes=VMEM_LIMIT),
    )(c_all, dmod_cols)


def _sum_devices(packed):
    def body(p_ref, out_ref):
        total = p_ref[0]
        for d in range(1, N_DEV):
            total = total + p_ref[d]
        out_ref[...] = total

    return pl.pallas_call(
        body, name="sum_devices",
        out_shape=jax.ShapeDtypeStruct(packed.shape[1:], F32),
        in_specs=[pl.BlockSpec(memory_space=pltpu.VMEM)],
        out_specs=pl.BlockSpec(memory_space=pltpu.VMEM),
        compiler_params=pltpu.CompilerParams(vmem_limit_bytes=VMEM_LIMIT),
    )(packed)


def _row_tile(rows, multiple):
    for cand in range(min(rows, 256), 0, -1):
        if rows % cand == 0 and cand % multiple == 0:
            return cand
    return rows


def _adamw_update(w, g, m, v):
    c1 = 1.0 / (1.0 - ADAM_B1 ** ADAM_STEP)
    c2 = 1.0 / (1.0 - ADAM_B2 ** ADAM_STEP)
    nm = ADAM_B1 * m + (1.0 - ADAM_B1) * g
    nv = ADAM_B2 * v + (1.0 - ADAM_B2) * (g * g)
    delta = -ADAM_LR * ((nm * c1) / (jnp.sqrt(nv * c2) + ADAM_EPS) + ADAM_WD * w)
    return delta, nm, nv


def _adamw(w, g, m, v, name):
    rows, cols = w.shape
    tile = _row_tile(rows, SUBLANES)

    def body(w_ref, g_ref, m_ref, v_ref, d_ref, nm_ref, nv_ref):
        d_ref[...], nm_ref[...], nv_ref[...] = _adamw_update(w_ref[...], g_ref[...], m_ref[...], v_ref[...])

    spec = pl.BlockSpec((tile, cols), lambda i: (i, 0))
    outs, _ = _call(body, name, (rows // tile,), [w, g, m, v], [spec] * 4,
                    [jax.ShapeDtypeStruct((rows, cols), F32)] * 3, [spec] * 3)
    return outs


def _sibling_sum(own, sib, name):
    _, r, cdim = own.shape
    tile = _row_tile(r, BF16_ROWS)

    def body(own_ref, sib_ref, sums_ref, mine_ref):
        mine_ref[...] = own_ref[0].astype(F32) + sib_ref[0].astype(F32)
        for f in (1, 2, 3):
            sums_ref[f - 1] = (own_ref[f].astype(F32) + sib_ref[f].astype(F32)).astype(BF16)

    outs, _ = _call(
        body, name, (r // tile,), [own, sib], [pl.BlockSpec((4, tile, cdim), lambda i: (0, i, 0))] * 2,
        [jax.ShapeDtypeStruct((3, r, cdim), BF16), jax.ShapeDtypeStruct((r, cdim), F32)],
        [pl.BlockSpec((3, tile, cdim), lambda i: (0, i, 0)), pl.BlockSpec((tile, cdim), lambda i: (i, 0))])
    return outs


def _chip_sum_adamw(mine, ici, w, m, v, name):
    r, cdim = mine.shape
    tile = _row_tile(r, BF16_ROWS)

    def body(mine_ref, ici_ref, w_ref, m_ref, v_ref, g_ref, d_ref, nm_ref, nv_ref):
        g = mine_ref[...]
        for f in range(3):
            g = g + ici_ref[f].astype(F32)
        g_ref[...] = g
        d_ref[...], nm_ref[...], nv_ref[...] = _adamw_update(w_ref[...], g, m_ref[...], v_ref[...])

    spec = pl.BlockSpec((tile, cdim), lambda i: (i, 0))
    outs, _ = _call(
        body, name, (r // tile,), [mine, ici, w, m, v],
        [spec, pl.BlockSpec((3, tile, cdim), lambda i: (0, i, 0)), spec, spec, spec],
        [jax.ShapeDtypeStruct((r, cdim), F32)] * 4, [spec] * 4)
    return outs


REF_KV_COL = D_MODEL
REF_REST_COL = D_MODEL + 2 * KV_WIDTH
IN_CHUNK = 1280
IN_PIECES = ([(0, 0, D_MODEL)]
             + [(D_MODEL + n * IN_CHUNK, REF_REST_COL + n * IN_CHUNK, IN_CHUNK) for n in range(REST_WIDTH // IN_CHUNK)]
             + [(KV_COL, REF_KV_COL, 2 * KV_WIDTH)])


def _inproj_fwd(x, vec, w_t, b_in, rider):
    t = x.shape[0]
    tm = min(TOKEN_TILE, t)

    def body(x_ref, vec_ref, w_ref, b_ref, z_ref, h_ref):
        xf = x_ref[...]
        r = lax.rsqrt(jnp.mean(xf * xf, axis=-1, keepdims=True) + EPS)
        h = (xf * r) * vec_ref[0:1, :] * (1.0 + vec_ref[1:2, :]) + vec_ref[2:3, :]
        hb = h.astype(BF16)
        h_ref[...] = hb
        for mine, ref, width in IN_PIECES:
            zc = lax.dot_general(hb, w_ref[ref:ref + width, :], NT_DIMS, preferred_element_type=F32)
            z_ref[:, mine:mine + width] = (zc + b_ref[:, ref:ref + width]).astype(BF16)

    return _call(
        body, "inproj_fwd", (t // tm,), [x, vec, w_t, b_in],
        [pl.BlockSpec((tm, D_MODEL), lambda i: (i, 0)), _full((SUBLANES, D_MODEL)),
         _full((IN_WIDTH, D_MODEL)), _full((1, IN_WIDTH))],
        [jax.ShapeDtypeStruct((t, IN_WIDTH), BF16), jax.ShapeDtypeStruct((t, D_MODEL), BF16)],
        [pl.BlockSpec((tm, IN_WIDTH), lambda i: (i, 0)), pl.BlockSpec((tm, D_MODEL), lambda i: (i, 0))],
        rider=rider)


def _window_mask(has_prev):
    qi = lax.broadcasted_iota(jnp.int32, (WINDOW, 2 * WINDOW), 0)
    kj = lax.broadcasted_iota(jnp.int32, (WINDOW, 2 * WINDOW), 1)
    off = jnp.where(has_prev, 0, 4 * WINDOW)
    in_prev = jnp.logical_and(kj < WINDOW, kj > qi + off)
    in_cur = jnp.logical_and(kj >= WINDOW, (kj - WINDOW) <= qi)
    return jnp.logical_or(in_prev, in_cur)


PAIRS = GROUP // 2
STACK = PAIRS * WINDOW


ATTN_BLOCKS = 4
ATTN_BWD_BLOCKS = 1
LOG2E = 1.4426950408889634
LN2 = 0.6931471805599453
SCORE_SCALE = ATTN_SCALE * LOG2E


def _fill_window_bias(bias_ref):
    shape = bias_ref.shape[1:]
    kj = lax.broadcasted_iota(jnp.int32, shape, 0)
    qi = jnp.bitwise_and(lax.broadcasted_iota(jnp.int32, shape, 1), WINDOW - 1)
    in_prev = jnp.logical_and(kj < WINDOW, kj > qi)
    in_cur = jnp.logical_and(kj >= WINDOW, (kj - WINDOW) <= qi)
    bias_ref[0] = jnp.where(in_cur, 0.0, -jnp.inf)
    bias_ref[1] = jnp.where(jnp.logical_or(in_prev, in_cur), 0.0, -jnp.inf)


def _half_tiles(tile):
    low = lax.broadcasted_iota(jnp.int32, tile.shape, 1) < HEAD_DIM
    swapped = jnp.concatenate([tile[:, HEAD_DIM:], tile[:, :HEAD_DIM]], axis=1)
    zero = jnp.zeros_like(tile)
    return ((jnp.where(low, tile, zero), jnp.where(low, zero, swapped)),
            (jnp.where(low, swapped, zero), jnp.where(low, zero, tile)))


def _stack_pairs(ref, row0, j):
    return jnp.concatenate(
        [ref[pl.ds(row0, WINDOW), (j * PAIRS + p) * LANES:(j * PAIRS + p + 1) * LANES] for p in range(PAIRS)], axis=0)


def _per_pair_row(values):
    pair = lax.broadcasted_iota(jnp.int32, (1, STACK), 1) // WINDOW
    row = jnp.full((1, STACK), values[PAIRS - 1], F32)
    for p in range(PAIRS - 2, -1, -1):
        row = jnp.where(pair == p, values[p], row)
    return row


def _attn_fwd(z, sinks, rider):
    t = z.shape[0]
    tq = min(TOKEN_TILE, t)
    nblk = tq // WINDOW

    def body(q_ref, kv_ref, sink_ref, o_ref, lse_ref, bias_ref):
        i = pl.program_id(0)

        @pl.when(i == 0)
        def _():
            _fill_window_bias(bias_ref)

        def window(b):
            row0 = pl.multiple_of(b * WINDOW, WINDOW)
            start = i * tq + b * WINDOW
            prev = pl.multiple_of(jnp.maximum(start - WINDOW, 0), WINDOW)
            cur = pl.multiple_of(start, WINDOW)
            kvw = jnp.concatenate([kv_ref[pl.ds(prev, WINDOW), :], kv_ref[pl.ds(cur, WINDOW), :]], axis=0)
            return row0, _half_tiles(kvw[:, :KV_WIDTH]), _half_tiles(kvw[:, KV_WIDTH:]), bias_ref[jnp.minimum(start, 1)]

        def block_group(bb, carry):
            windows = [window(bb * ATTN_BLOCKS + n) for n in range(ATTN_BLOCKS)]
            for j in range(N_KV_HEADS):
                for pr in range(PAIRS):
                    cols = slice((j * PAIRS + pr) * LANES, (j * PAIRS + pr + 1) * LANES)
                    o_ts = [jnp.zeros((LANES, WINDOW), F32) for _ in windows]
                    for parity in range(2):
                        h = j * GROUP + 2 * pr + parity
                        sink = sink_ref[h] * LOG2E
                        for n, (row0, k_halves, v_halves, bias) in enumerate(windows):
                            qp = q_ref[pl.ds(row0, WINDOW), cols]
                            s = lax.dot_general(k_halves[j][parity], qp, NT_DIMS, preferred_element_type=F32)
                            s = s * SCORE_SCALE + bias
                            m = jnp.maximum(jnp.max(s, axis=0, keepdims=True), sink)
                            p = jnp.exp2(s - m)
                            denom = jnp.sum(p, axis=0, keepdims=True) + jnp.exp2(sink - m)
                            pv = lax.dot_general(v_halves[j][parity], p.astype(BF16), TN_DIMS,
                                                 preferred_element_type=F32)
                            o_ts[n] = o_ts[n] + pv * (1.0 / denom)
                            lse_ref[h:h + 1, pl.ds(row0, WINDOW)] = m + jnp.log2(denom)
                    for n, (row0, _, _, _) in enumerate(windows):
                        o_ref[pl.ds(row0, WINDOW), cols] = jnp.transpose(o_ts[n].astype(BF16))
            return carry

        lax.fori_loop(0, nblk // ATTN_BLOCKS, block_group, 0)

    return _call(
        body, "attn_fwd", (t // tq,), [z, z, sinks],
        [pl.BlockSpec((tq, D_MODEL), lambda i: (i, 0)),
         pl.BlockSpec((t, 2 * KV_WIDTH), lambda i: (0, KV_COL // (2 * KV_WIDTH))),
         pl.BlockSpec(memory_space=pltpu.SMEM)],
        [jax.ShapeDtypeStruct((t, D_MODEL), BF16), jax.ShapeDtypeStruct((N_Q_HEADS, t), F32)],
        [pl.BlockSpec((tq, D_MODEL), lambda i: (i, 0)), pl.BlockSpec((N_Q_HEADS, tq), lambda i: (0, i))],
        scratch=[pltpu.VMEM((2, 2 * WINDOW, WINDOW), F32)], rider=rider)


HALO = BF16_ROWS


def _shift_down(u, uh, k):
    row = lax.broadcasted_iota(jnp.int32, u.shape, 0)
    out = pltpu.roll(u, k, 0)
    for j in range(k):
        out = jnp.where(row == j, uh[HALO - k + j:HALO - k + j + 1, :], out)
    return out


def _shift_up(u, nxt, k):
    n = u.shape[0]
    row = lax.broadcasted_iota(jnp.int32, u.shape, 0)
    out = pltpu.roll(u, n - k, 0)
    for j in range(k):
        out = jnp.where(row == n - k + j, nxt[j:j + 1, :], out)
    return out


def _conv_inputs(cc_ref, cx_ref, hc_ref, hx_ref, first_tile):
    cc = cc_ref[...].astype(F32)
    cx = cx_ref[...].astype(F32)
    u = cc * cx
    uh = jnp.where(first_tile, 0.0, hc_ref[...].astype(F32) * hx_ref[...].astype(F32))
    return cc, cx, u, _shift_down(u, uh, 1), _shift_down(u, uh, 2)


def _z_specs(tm, order):
    per_tile = tm // HALO
    cols = [pl.BlockSpec((tm, D_MODEL), functools.partial(lambda i, j: (order(i), j), j=j)) for j in range(1, 6)]
    halos = [pl.BlockSpec((HALO, D_MODEL),
                          functools.partial(lambda i, j: (jnp.maximum(order(i) * per_tile - 1, 0), j), j=j))
             for j in (2, 3)]
    return cols + halos


def _mix_fwd(x, attn, z, vec, w_out):
    t = x.shape[0]
    tm = min(TOKEN_TILE, t)

    def body(x_ref, a_ref, cb_ref, cc_ref, cx_ref, ga_ref, gc_ref, hc_ref, hx_ref, vec_ref, w_ref,
             m_ref, x2_ref, h2_ref, o_ref):
        i = pl.program_id(0)
        _, _, u, u1, u2 = _conv_inputs(cc_ref, cx_ref, hc_ref, hx_ref, i == 0)
        cv = vec_ref[4:5, :] * u2 + vec_ref[5:6, :] * u1 + vec_ref[6:7, :] * u
        conv = cb_ref[...].astype(F32) * cv
        merged = (_sigmoid(ga_ref[...].astype(F32)) * a_ref[...].astype(F32)
                  + _sigmoid(gc_ref[...].astype(F32)) * conv)
        mb = merged.astype(BF16)
        m_ref[...] = mb
        o = jnp.dot(mb, w_ref[...], preferred_element_type=F32)
        o_ref[...] = o.astype(BF16)
        x2 = x_ref[...] + vec_ref[0:1, :] * o
        x2_ref[...] = x2
        r = lax.rsqrt(jnp.mean(x2 * x2, axis=-1, keepdims=True) + EPS)
        h2 = (x2 * r) * vec_ref[1:2, :] * (1.0 + vec_ref[2:3, :]) + vec_ref[3:4, :]
        h2_ref[...] = h2.astype(BF16)

    tok = pl.BlockSpec((tm, D_MODEL), lambda i: (i, 0))
    outs, _ = _call(
        body, "mix_fwd", (t // tm,), [x, attn, z, z, z, z, z, z, z, vec, w_out],
        [tok, tok] + _z_specs(tm, lambda i: i) + [_full((SUBLANES, D_MODEL)), _full((D_MODEL, D_MODEL))],
        [jax.ShapeDtypeStruct((t, D_MODEL), BF16), jax.ShapeDtypeStruct((t, D_MODEL), F32),
         jax.ShapeDtypeStruct((t, D_MODEL), BF16), jax.ShapeDtypeStruct((t, D_MODEL), BF16)],
        [tok, tok, tok, tok])
    return outs


def _ffn_fwd(h2, w_t):
    t = h2.shape[0]
    tm = min(TOKEN_TILE, t)

    def body(h_ref, w_ref, gu_ref, a_ref):
        hb = h_ref[...]
        for n in range(D_FF // FF_CHUNK):
            lo, hi = n * FF_CHUNK, (n + 1) * FF_CHUNK
            g = lax.dot_general(hb, w_ref[lo:hi, :], NT_DIMS, preferred_element_type=F32)
            u = lax.dot_general(hb, w_ref[D_FF + lo:D_FF + hi, :], NT_DIMS, preferred_element_type=F32)
            sg = _sigmoid(g)
            silu = g * sg
            gu_ref[:, lo:hi] = (u * (sg * (1.0 + g * (1.0 - sg)))).astype(BF16)
            gu_ref[:, D_FF + lo:D_FF + hi] = silu.astype(BF16)
            a_ref[:, lo:hi] = (silu * u).astype(BF16)

    outs, _ = _call(
        body, "ffn_fwd", (t // tm,), [h2, w_t],
        [pl.BlockSpec((tm, D_MODEL), lambda i: (i, 0)), _full((2 * D_FF, D_MODEL))],
        [jax.ShapeDtypeStruct((t, 2 * D_FF), BF16), jax.ShapeDtypeStruct((t, D_FF), BF16)],
        [pl.BlockSpec((tm, 2 * D_FF), lambda i: (i, 0)), pl.BlockSpec((tm, D_FF), lambda i: (i, 0))])
    return outs


def _ffn_out_loss(a, gu, x2, target, vec, w_ffn_out):
    t = a.shape[0]
    tm = min(TOKEN_TILE, t)

    def body(a_ref, gu_ref, x2_ref, t_ref, vec_ref, w_ref, dx3_ref, df_ref, dgu_ref, acc_ref):
        @pl.when(pl.program_id(0) == 0)
        def _():
            acc_ref[...] = jnp.zeros_like(acc_ref)

        ga2 = vec_ref[0:1, :]
        gf = vec_ref[1:2, :]
        parts = min(ROW_PARTS, tm // LANES)
        part_rows = [slice(n * (tm // parts), (n + 1) * (tm // parts)) for n in range(parts)]

        def head(rows, f):
            x3 = x2_ref[rows, :] + ga2 * f
            r = lax.rsqrt(jnp.mean(x3 * x3, axis=-1, keepdims=True) + EPS)
            xn = x3 * r
            err = xn * gf - t_ref[rows, :]
            dy = err * (1.0 / D_MODEL)
            dxn = dy * gf
            dx3 = r * (dxn - xn * jnp.mean(dxn * xn, axis=-1, keepdims=True))
            dx3_ref[rows, :] = dx3.astype(GRAD_STREAM)
            sums = (jnp.sum(err * err, axis=0, keepdims=True), jnp.sum(dy * xn, axis=0, keepdims=True),
                    jnp.sum(dx3 * f, axis=0, keepdims=True))
            df = (dx3 * ga2).astype(BF16)
            df_ref[rows, :] = df
            return df, sums

        def tail(rows, df):
            for n in range(D_FF // FF_CHUNK):
                lo, hi = n * FF_CHUNK, (n + 1) * FF_CHUNK
                da = lax.dot_general(df, w_ref[lo:hi, :], NT_DIMS, preferred_element_type=F32)
                dgu_ref[rows, lo:hi] = (da * gu_ref[rows, lo:hi].astype(F32)).astype(BF16)
                dgu_ref[rows, D_FF + lo:D_FF + hi] = (da * gu_ref[rows, D_FF + lo:D_FF + hi].astype(F32)).astype(BF16)

        fs = [jnp.dot(a_ref[rows, :], w_ref[...], preferred_element_type=F32) for rows in part_rows]
        heads = [head(rows, f) for rows, f in zip(part_rows, fs)]
        for rows, (df, _) in zip(part_rows, heads):
            tail(rows, df)
        for k in range(3):
            total = heads[0][1][k]
            for _, sums in heads[1:]:
                total = total + sums[k]
            acc_ref[k:k + 1, :] += total

    tok = pl.BlockSpec((tm, D_MODEL), lambda i: (i, 0))
    outs, _ = _call(
        body, "ffn_out_loss", (t // tm,), [a, gu, x2, target, vec, w_ffn_out],
        [pl.BlockSpec((tm, D_FF), lambda i: (i, 0)), pl.BlockSpec((tm, 2 * D_FF), lambda i: (i, 0)),
         tok, tok, _full((SUBLANES, D_MODEL)), _full((D_FF, D_MODEL))],
        [jax.ShapeDtypeStruct((t, D_MODEL), GRAD_STREAM), jax.ShapeDtypeStruct((t, D_MODEL), BF16),
         jax.ShapeDtypeStruct((t, 2 * D_FF), BF16), jax.ShapeDtypeStruct((SUBLANES, D_MODEL), F32)],
        [tok, tok, pl.BlockSpec((tm, 2 * D_FF), lambda i: (i, 0)), _full((SUBLANES, D_MODEL))])
    return outs


def _ffn_in_bwd(dgu, x2, dx3, vec, w_t, rider):
    t = x2.shape[0]
    tm = min(TOKEN_TILE, t)

    def body(dgu_ref, x2_ref, dx3_ref, vec_ref, wf_ref, dx2_ref, acc_ref):
        @pl.when(pl.program_id(0) == 0)
        def _():
            acc_ref[...] = jnp.zeros_like(acc_ref)

        gffn = vec_ref[0:1, :]
        sc2 = vec_ref[1:2, :]
        parts = min(ROW_PARTS, tm // LANES)
        part_rows = [slice(n * (tm // parts), (n + 1) * (tm // parts)) for n in range(parts)]
        dhs = [jnp.dot(dgu_ref[rows, :], wf_ref[...], preferred_element_type=F32) for rows in part_rows]
        sums = []
        for rows, dh2 in zip(part_rows, dhs):
            x2 = x2_ref[rows, :]
            r = lax.rsqrt(jnp.mean(x2 * x2, axis=-1, keepdims=True) + EPS)
            xn = x2 * r
            sums.append((jnp.sum(dh2, axis=0, keepdims=True), jnp.sum(dh2 * xn * gffn, axis=0, keepdims=True),
                         jnp.sum(dh2 * xn * (1.0 + sc2), axis=0, keepdims=True)))
            dxn = dh2 * gffn * (1.0 + sc2)
            dx2 = dx3_ref[rows, :].astype(F32) + r * (dxn - xn * jnp.mean(dxn * xn, axis=-1, keepdims=True))
            dx2_ref[rows, :] = dx2.astype(GRAD_STREAM)
        for k in range(3):
            total = sums[0][k]
            for part in sums[1:]:
                total = total + part[k]
            acc_ref[k:k + 1, :] += total

    tok = pl.BlockSpec((tm, D_MODEL), lambda i: (i, 0))
    return _call(
        body, "ffn_in_bwd", (t // tm,), [dgu, x2, dx3, vec, w_t],
        [pl.BlockSpec((tm, 2 * D_FF), lambda i: (i, 0)), tok, tok, _full((SUBLANES, D_MODEL)),
         _full((2 * D_FF, D_MODEL))],
        [jax.ShapeDtypeStruct((t, D_MODEL), GRAD_STREAM), jax.ShapeDtypeStruct((SUBLANES, D_MODEL), F32)],
        [tok, _full((SUBLANES, D_MODEL))], rider=rider)


def _mix_bwd(dx2, oproj, attn, z, vec, w_out, rider):
    t = dx2.shape[0]
    tm = min(TOKEN_TILE, t)
    nt = t // tm
    rev = lambda i: nt - 1 - i

    def body(dx2_ref, m_ref, a_ref, cb_ref, cc_ref, cx_ref, ga_ref, gc_ref, hc_ref, hx_ref,
             vec_ref, wo_ref, do_ref, da_ref, dr_ref, acc_ref, carry_ref):
        i = pl.program_id(0)

        @pl.when(i == 0)
        def _():
            acc_ref[...] = jnp.zeros_like(acc_ref)
            carry_ref[...] = jnp.zeros_like(carry_ref)

        ga1 = vec_ref[0:1, :]
        w0, w1, w2 = vec_ref[1:2, :], vec_ref[2:3, :], vec_ref[3:4, :]
        dx2 = dx2_ref[...].astype(F32)
        acc_ref[0:1, :] += jnp.sum(dx2 * m_ref[...].astype(F32), axis=0, keepdims=True)
        do = (dx2 * ga1).astype(BF16)
        do_ref[...] = do
        dm = lax.dot_general(do, wo_ref[...], NT_DIMS, preferred_element_type=F32)

        cc, cx, u, u1, u2 = _conv_inputs(cc_ref, cx_ref, hc_ref, hx_ref, i == nt - 1)
        cv = w0 * u2 + w1 * u1 + w2 * u
        cb = cb_ref[...].astype(F32)
        sa = _sigmoid(ga_ref[...].astype(F32))
        sc = _sigmoid(gc_ref[...].astype(F32))
        attn = a_ref[...].astype(F32)
        da_ref[...] = (dm * sa).astype(BF16)
        dconv = dm * sc
        dr_ref[:, 3 * D_MODEL:4 * D_MODEL] = (dm * attn * sa * (1.0 - sa)).astype(BF16)
        dr_ref[:, 4 * D_MODEL:5 * D_MODEL] = (dconv * (cb * cv) * (1.0 - sc)).astype(BF16)
        dr_ref[:, 0:D_MODEL] = (dconv * cv).astype(BF16)
        dcv = dconv * cb
        acc_ref[1:2, :] += jnp.sum(dcv * u2, axis=0, keepdims=True)
        acc_ref[2:3, :] += jnp.sum(dcv * u1, axis=0, keepdims=True)
        acc_ref[3:4, :] += jnp.sum(dcv * u, axis=0, keepdims=True)
        nxt = carry_ref[...]
        du = w2 * dcv + w1 * _shift_up(dcv, nxt, 1) + w0 * _shift_up(dcv, nxt, 2)
        carry_ref[...] = dcv[0:SUBLANES, :]
        dr_ref[:, D_MODEL:2 * D_MODEL] = (du * cx).astype(BF16)
        dr_ref[:, 2 * D_MODEL:3 * D_MODEL] = (du * cc).astype(BF16)

    tok = pl.BlockSpec((tm, D_MODEL), lambda i: (rev(i), 0))
    return _call(
        body, "mix_bwd", (nt,), [dx2, oproj, attn, z, z, z, z, z, z, z, vec, w_out],
        [tok, tok, tok] + _z_specs(tm, rev) + [_full((SUBLANES, D_MODEL)), _full((D_MODEL, D_MODEL))],
        [jax.ShapeDtypeStruct((t, D_MODEL), BF16), jax.ShapeDtypeStruct((t, D_MODEL), BF16),
         jax.ShapeDtypeStruct((t, REST_WIDTH), BF16), jax.ShapeDtypeStruct((SUBLANES, D_MODEL), F32)],
        [tok, tok, pl.BlockSpec((tm, REST_WIDTH), lambda i: (rev(i), 0)), _full((SUBLANES, D_MODEL))],
        scratch=[pltpu.VMEM((SUBLANES, D_MODEL), F32)], rider=rider)


def _attn_bwd(z, dattn, attn, lse, sinks, rider):
    t = z.shape[0]
    tq = min(TOKEN_TILE, t)
    nblk = tq // WINDOW
    nt = t // tq

    def body(q_ref, kv_ref, do_ref, o_ref, lse_ref, sink_ref, dq_ref, dkv_ref, ds_ref, acc_ref, bias_ref):
        i = pl.program_id(0)

        @pl.when(i == 0)
        def _():
            acc_ref[...] = jnp.zeros_like(acc_ref)
            ds_ref[...] = jnp.zeros_like(ds_ref)
            _fill_window_bias(bias_ref)

        lane = lax.broadcasted_iota(jnp.int32, (1, LANES), 1)
        ind_row = lax.broadcasted_iota(jnp.int32, (SUBLANES, LANES), 0)
        ind_low = lax.broadcasted_iota(jnp.int32, (SUBLANES, LANES), 1) < HEAD_DIM
        indicator = jnp.where(jnp.logical_or(jnp.logical_and(ind_row == 0, ind_low),
                                             jnp.logical_and(ind_row == 1, jnp.logical_not(ind_low))),
                              1.0, 0.0).astype(BF16)
        low = lax.broadcasted_iota(jnp.int32, (2 * WINDOW, LANES), 1) < HEAD_DIM

        def both_heads(even, odd):
            picked = jnp.where(low, even, odd)
            return picked + jnp.concatenate([picked[:, HEAD_DIM:], picked[:, :HEAD_DIM]], axis=1)

        def window(b):
            row0 = pl.multiple_of(b * WINDOW, WINDOW)
            start = i * tq + b * WINDOW
            prev = pl.multiple_of(jnp.maximum(start - WINDOW, 0), WINDOW)
            cur = pl.multiple_of(start, WINDOW)
            kvw = jnp.concatenate([kv_ref[pl.ds(prev, WINDOW), :], kv_ref[pl.ds(cur, WINDOW), :]], axis=0)
            return (row0, prev, cur, _half_tiles(kvw[:, :KV_WIDTH]), _half_tiles(kvw[:, KV_WIDTH:]),
                    bias_ref[jnp.minimum(start, 1)])

        def block_group(bb, dsink):
            windows = [window(bb * ATTN_BWD_BLOCKS + n) for n in range(ATTN_BWD_BLOCKS)]
            dk_groups = [[] for _ in windows]
            dv_groups = [[] for _ in windows]
            for j in range(N_KV_HEADS):
                stacks, deltas, dq_ts = [], [], []
                for row0, _, _, _, _, _ in windows:
                    qst = _stack_pairs(q_ref, row0, j)
                    dost = _stack_pairs(do_ref, row0, j)
                    prod = dost.astype(F32) * _stack_pairs(o_ref, row0, j).astype(F32)
                    prod_hi = prod.astype(BF16)
                    prod_lo = (prod - prod_hi.astype(F32)).astype(BF16)
                    stacks.append((qst, dost))
                    deltas.append(lax.dot_general(indicator, prod_hi, NT_DIMS, preferred_element_type=F32)
                                  + lax.dot_general(indicator, prod_lo, NT_DIMS, preferred_element_type=F32))
                    dq_ts.append(jnp.zeros((LANES, STACK), F32))
                dk_par = [[] for _ in windows]
                dv_par = [[] for _ in windows]
                for parity in range(2):
                    heads = [j * GROUP + 2 * p + parity for p in range(PAIRS)]
                    sink = _per_pair_row([sink_ref[h] * LOG2E for h in heads])
                    for n, (row0, _, _, k_halves, v_halves, bias) in enumerate(windows):
                        qst, dost = stacks[n]
                        kk, vv = k_halves[j][parity], v_halves[j][parity]
                        s = lax.dot_general(kk, qst, NT_DIMS, preferred_element_type=F32) * SCORE_SCALE + bias
                        lse = jnp.concatenate([lse_ref[h:h + 1, pl.ds(row0, WINDOW)] for h in heads], axis=1)
                        p = jnp.exp2(s - lse)
                        dp = lax.dot_general(vv, dost, NT_DIMS, preferred_element_type=F32)
                        delta = deltas[n][parity:parity + 1, :]
                        dsb = (p * (dp - delta)).astype(BF16)
                        dq_ts[n] = dq_ts[n] + lax.dot_general(kk, dsb, TN_DIMS, preferred_element_type=F32)
                        dk_par[n].append(jnp.dot(dsb, qst, preferred_element_type=F32))
                        dv_par[n].append(jnp.dot(p.astype(BF16), dost, preferred_element_type=F32))
                        weighted = jnp.exp2(sink - lse) * delta
                        for pr, h in enumerate(heads):
                            dsink = dsink - jnp.where(
                                lane == h, jnp.sum(weighted[:, pr * WINDOW:(pr + 1) * WINDOW]), 0.0)
                for n, (row0, _, _, _, _, _) in enumerate(windows):
                    dq_st = jnp.transpose((dq_ts[n] * ATTN_SCALE).astype(BF16))
                    for pr in range(PAIRS):
                        dq_ref[pl.ds(row0, WINDOW), (j * PAIRS + pr) * LANES:(j * PAIRS + pr + 1) * LANES] = (
                            dq_st[pr * WINDOW:(pr + 1) * WINDOW, :])
                    dk_groups[n].append(both_heads(dk_par[n][0], dk_par[n][1]))
                    dv_groups[n].append(both_heads(dv_par[n][0], dv_par[n][1]))
            for n, (_, prev, cur, _, _, _) in enumerate(windows):
                blk = jnp.concatenate([jnp.where(low, dk_groups[n][0], dk_groups[n][1]) * ATTN_SCALE,
                                       jnp.where(low, dv_groups[n][0], dv_groups[n][1])], axis=1)
                acc_ref[pl.ds(prev, WINDOW), :] += blk[:WINDOW, :]
                acc_ref[pl.ds(cur, WINDOW), :] += blk[WINDOW:, :]
            return dsink

        dsink = lax.fori_loop(0, nblk // ATTN_BWD_BLOCKS, block_group, jnp.zeros((1, LANES), F32))
        ds_ref[0:1, :] += dsink

        @pl.when(i == nt - 1)
        def _():
            dkv_ref[...] = acc_ref[...].astype(BF16)

    tok = pl.BlockSpec((tq, D_MODEL), lambda i: (i, 0))
    return _call(
        body, "attn_bwd", (nt,), [z, z, dattn, attn, lse, sinks],
        [tok, pl.BlockSpec((t, 2 * KV_WIDTH), lambda i: (0, KV_COL // (2 * KV_WIDTH))), tok, tok,
         pl.BlockSpec((N_Q_HEADS, tq), lambda i: (0, i)), pl.BlockSpec(memory_space=pltpu.SMEM)],
        [jax.ShapeDtypeStruct((t, D_MODEL), BF16), jax.ShapeDtypeStruct((t, 2 * KV_WIDTH), BF16),
         jax.ShapeDtypeStruct((SUBLANES, LANES), F32)],
        [tok, _full((t, 2 * KV_WIDTH)), _full((SUBLANES, LANES))],
        scratch=[pltpu.VMEM((t, 2 * KV_WIDTH), F32), pltpu.VMEM((2, 2 * WINDOW, STACK), F32)], rider=rider)


def _inproj_bwd(dq, drest, dkv, x, dx2, vec, w_t, rider):
    t = x.shape[0]
    tm = min(TOKEN_TILE, t)

    def body(dq_ref, dr_ref, dkv_ref, x_ref, dx2_ref, vec_ref, w_ref, gx_ref, acc_ref, db_ref):
        @pl.when(pl.program_id(0) == 0)
        def _():
            acc_ref[...] = jnp.zeros_like(acc_ref)
            db_ref[...] = jnp.zeros_like(db_ref)

        g = vec_ref[0:1, :]
        sc1 = vec_ref[1:2, :]
        dqb, drb, dkvb = dq_ref[...], dr_ref[...], dkv_ref[...]
        dh = jnp.dot(dqb, w_ref[:REF_KV_COL, :], preferred_element_type=F32)
        dh = dh + jnp.dot(drb, w_ref[REF_REST_COL:, :], preferred_element_type=F32)
        dh = dh + jnp.dot(dkvb, w_ref[REF_KV_COL:REF_REST_COL, :], preferred_element_type=F32)
        db_ref[:, :REF_KV_COL] += jnp.sum(dqb.astype(F32), axis=0, keepdims=True)
        db_ref[:, REF_REST_COL:] += jnp.sum(drb.astype(F32), axis=0, keepdims=True)
        db_ref[:, REF_KV_COL:REF_REST_COL] += jnp.sum(dkvb.astype(F32), axis=0, keepdims=True)
        xf = x_ref[...]
        r = lax.rsqrt(jnp.mean(xf * xf, axis=-1, keepdims=True) + EPS)
        xn = xf * r
        acc_ref[0:1, :] += jnp.sum(dh, axis=0, keepdims=True)
        acc_ref[1:2, :] += jnp.sum(dh * xn * g, axis=0, keepdims=True)
        acc_ref[2:3, :] += jnp.sum(dh * xn * (1.0 + sc1), axis=0, keepdims=True)
        dxn = dh * g * (1.0 + sc1)
        gx_ref[...] = dx2_ref[...].astype(F32) + r * (dxn - xn * jnp.mean(dxn * xn, axis=-1, keepdims=True))

    tok = pl.BlockSpec((tm, D_MODEL), lambda i: (i, 0))
    return _call(
        body, "inproj_bwd", (t // tm,), [dq, drest, dkv, x, dx2, vec, w_t],
        [tok, pl.BlockSpec((tm, REST_WIDTH), lambda i: (i, 0)),
         pl.BlockSpec((tm, 2 * KV_WIDTH), lambda i: (i, 0)), tok, tok,
         _full((SUBLANES, D_MODEL)), _full((IN_WIDTH, D_MODEL))],
        [jax.ShapeDtypeStruct((t, D_MODEL), F32), jax.ShapeDtypeStruct((SUBLANES, D_MODEL), F32),
         jax.ShapeDtypeStruct((1, IN_WIDTH), F32)],
        [tok, _full((SUBLANES, D_MODEL)), _full((1, IN_WIDTH))], rider=rider)


def _weight_grad(b, a, name, bn, rows=None, row0=0, into=None, rider=None):
    t, n = b.shape
    m = a.shape[1]
    rows = n if rows is None else rows
    tk = min(TOKEN_TILE, t)
    for cand in (4 * TOKEN_TILE, 2 * TOKEN_TILE):
        if t % cand == 0 and 2 * cand * (bn + m) * 2 + bn * m * 4 <= WGRAD_VMEM:
            tk = cand
            break
    nk = t // tk
    block0 = row0 // bn

    def body(b_ref, a_ref, *rest):
        out_ref, acc_ref = rest[-2:]
        k = pl.program_id(1)

        @pl.when(k == 0)
        def _():
            acc_ref[...] = jnp.zeros_like(acc_ref)

        acc_ref[...] += lax.dot_general(b_ref[...], a_ref[...], TN_DIMS, preferred_element_type=F32)

        @pl.when(k == nk - 1)
        def _():
            out_ref[...] = acc_ref[...].astype(BF16)

    outs, routs = _call(
        body, name, (n // bn, nk), [b, a] + ([] if into is None else [into]),
        [pl.BlockSpec((tk, bn), lambda j, k: (k, j)), pl.BlockSpec((tk, m), lambda j, k: (k, 0))]
        + ([] if into is None else [ANY]),
        [jax.ShapeDtypeStruct((rows, m), BF16)], [pl.BlockSpec((bn, m), lambda j, k: (block0 + j, 0))],
        scratch=[pltpu.VMEM((bn, m), F32)], rider=rider, aliases=None if into is None else {2: 0})
    return outs[0], routs


def _to_rows(v):
    n = v.shape[0]
    padded = -(-n // (SUBLANES * LANES)) * SUBLANES * LANES
    return jnp.pad(v, (0, padded - n)).reshape(padded // LANES, LANES)


def _vec_rows(*rows):
    stacked = jnp.concatenate([r.reshape(1, D_MODEL) for r in rows], axis=0)
    return jnp.pad(stacked, ((0, SUBLANES - len(rows)), (0, 0)))


def kernel(x, c, w_ada, b_ada, g_mix, w_in, b_in, sinks, conv_w, w_out, g_ffn, w_ffn_in, w_ffn_out, g_final, loss_target, m_w_ada, m_b_ada, m_g_mix, m_w_in, m_b_in, m_sinks, m_conv_w, m_w_out, m_g_ffn, m_w_ffn_in, m_w_ffn_out, m_g_final, v_w_ada, v_b_ada, v_g_mix, v_w_in, v_b_in, v_sinks, v_conv_w, v_w_out, v_g_ffn, v_w_ffn_in, v_w_ffn_out, v_g_final):
    ix, iy, ic = _my_place()
    me = 4 * ix + 2 * iy + ic
    xs = x[0]
    target = loss_target[0]
    ada_cols = w_ada.shape[2]
    conv_cols = conv_w.shape[2]

    wt_in, wt_fi = jnp.transpose(w_in[0]), jnp.transpose(w_ffn_in[0])
    b_cols = lax.dynamic_slice_in_dim(b_ada, me * ada_cols, ada_cols, axis=1)
    g_in, (cast_fi, cast_out, cast_fo), first, mod_all = _gather_first_weight(
        wt_in, [wt_fi, w_out[0], w_ffn_out[0]], _to_rows(jnp.concatenate([c[0], conv_w[0].reshape(-1)])),
        w_ada[0], b_cols)
    first = first.reshape(N_DEV, -1)
    c_all = first[:, :D_MODEL]
    conv_full = jnp.transpose(first[:, D_MODEL:D_MODEL + 3 * conv_cols].reshape(N_DEV, 3, conv_cols), (1, 0, 2))
    conv_full = conv_full.reshape(3, D_MODEL)
    mod = lax.dynamic_index_in_dim(mod_all, me, axis=1, keepdims=False).reshape(N_MOD, D_MODEL)
    sh1, sc1, ga1, sh2, sc2, ga2 = [mod[i:i + 1] for i in range(N_MOD)]
    w_in_t = g_in.reshape(IN_WIDTH, D_MODEL)
    (z, h1), (g_fi, g_out) = _inproj_fwd(xs, _vec_rows(g_mix, sc1, sh1), w_in_t, b_in,
                                         _gather_rider([cast_fi, cast_out]))
    w_fi_t = g_fi.reshape(2 * D_FF, D_MODEL)
    w_out_full = g_out.reshape(D_MODEL, D_MODEL)
    (attn, lse), (g_fo,) = _attn_fwd(z, sinks[0], _gather_rider([cast_fo]))
    w_fo_full = g_fo.reshape(D_FF, D_MODEL)
    merged, x2, h2, oproj = _mix_fwd(
        xs, attn, z, _vec_rows(ga1, g_ffn, sc2, sh2, conv_full[0], conv_full[1], conv_full[2]), w_out_full)
    gu, act = _ffn_fwd(h2, w_fi_t)
    dx3, df, dgu, acc_l = _ffn_out_loss(act, gu, x2, target, _vec_rows(ga2, g_final), w_fo_full)

    gw_fo, _ = _weight_grad(act, df, "wgrad_ffn_out", D_FF)
    gw_fi, _ = _weight_grad(dgu, h2, "wgrad_ffn_in", D_FF)
    blocks_fo = gw_fo.reshape(N_DEV, D_FF // N_DEV, D_MODEL)
    blocks_fi = gw_fi.reshape(N_DEV, 2 * D_FF // N_DEV, D_MODEL)
    (dx2, acc_f), (sib_fo, sib_fi) = _ffn_in_bwd(dgu, x2, dx3, _vec_rows(g_ffn, sc2), w_fi_t,
                                                 _sibling_rider([blocks_fo, blocks_fi]))
    sums_fo, mine_fo = _sibling_sum(_own_blocks(blocks_fo), sib_fo, "sibling_sum_ffn_out")
    sums_fi, mine_fi = _sibling_sum(_own_blocks(blocks_fi), sib_fi, "sibling_sum_ffn_in")
    (dout, dattn, drest, acc_m), (ici_fo, ici_fi) = _mix_bwd(
        dx2, oproj, attn, z, _vec_rows(ga1, conv_full[0], conv_full[1], conv_full[2]), w_out_full,
        _chip_rider([sums_fo, sums_fi]))
    gw_out, _ = _weight_grad(merged, dout, "wgrad_out", D_MODEL)
    blocks_out = gw_out.reshape(N_DEV, D_MODEL // N_DEV, D_MODEL)
    (dq, dkv, dsink), (sib_out,) = _attn_bwd(z, dattn, attn, lse, sinks[0], _sibling_rider([blocks_out]))
    sums_out, mine_out = _sibling_sum(_own_blocks(blocks_out), sib_out, "sibling_sum_out")
    gw_in, (ici_out,) = _weight_grad(drest, h1, "wgrad_in_rest", IN_CHUNK, rows=IN_WIDTH, row0=REF_REST_COL,
                                     rider=_chip_rider([sums_out]))
    gw_in, _ = _weight_grad(dq, h1, "wgrad_in_q", D_MODEL, rows=IN_WIDTH, row0=0, into=gw_in)
    gw_in, _ = _weight_grad(dkv, h1, "wgrad_in_kv", 2 * KV_WIDTH, rows=IN_WIDTH, row0=REF_KV_COL, into=gw_in)
    blocks_in = gw_in.reshape(N_DEV, IN_WIDTH // N_DEV, D_MODEL)
    (sib_in,) = _carry(_sibling_rider([blocks_in]), "sibling_w_in")
    sums_in, mine_in = _sibling_sum(_own_blocks(blocks_in), sib_in, "sibling_sum_in")
    (grad_x, acc_i, db_in), (ici_in,) = _inproj_bwd(dq, drest, dkv, xs, dx2, _vec_rows(g_mix, sc1), w_in_t,
                                                    _chip_rider([sums_in]))

    pieces = [acc_i[0], acc_i[1], acc_m[0], acc_f[0], acc_f[1], acc_l[2],
              acc_i[2], db_in[0], acc_f[2], acc_l[1],
              acc_m[1], acc_m[2], acc_m[3], dsink[0], acc_l[0]]
    offsets = [0]
    for p in pieces:
        offsets.append(offsets[-1] + p.shape[0])
    packed = _small_allgather(_to_rows(jnp.concatenate(pieces)), "gather_small")
    dmod_all = packed.reshape(N_DEV, -1)[:, :N_MOD * D_MODEL]
    total = _sum_devices(packed).reshape(-1)
    part = lambda i: total[offsets[i]:offsets[i + 1]]
    g_b_ada = total[:N_MOD * D_MODEL].reshape(1, -1)
    g_g_mix, g_b_in, g_g_ffn, g_g_final = part(6).reshape(1, -1), part(7).reshape(1, -1), part(8).reshape(1, -1), part(9)
    g_conv_full = jnp.stack([part(10), part(11), part(12)])
    g_conv = lax.dynamic_slice_in_dim(g_conv_full, me * conv_cols, conv_cols, axis=1)[None]
    g_sinks = part(13)[:N_Q_HEADS].reshape(1, -1)
    loss = (0.5 / D_MODEL) * jnp.sum(part(14))
    dmod_cols = lax.dynamic_slice_in_dim(dmod_all, me * ada_cols, ada_cols, axis=1)
    g_w_ada = _ada_weight_grad(c_all, dmod_cols)

    def reduced(mine, ici, w, m, v, name, transposed=False):
        turn = jnp.transpose if transposed else (lambda a: a)
        return tuple(turn(o)[None] for o in _chip_sum_adamw(mine, ici, turn(w[0]), turn(m[0]), turn(v[0]), name))

    d_ada, nm_ada, nv_ada = _adamw(w_ada[0], g_w_ada, m_w_ada[0], v_w_ada[0], "adamw_w_ada")
    small_names = ["b_ada", "g_mix", "b_in", "sinks", "conv_w", "g_ffn", "g_final"]
    small_w = [b_ada, g_mix, b_in, sinks, conv_w, g_ffn, g_final]
    small_m = [m_b_ada, m_g_mix, m_b_in, m_sinks, m_conv_w, m_g_ffn, m_g_final]
    small_v = [v_b_ada, v_g_mix, v_b_in, v_sinks, v_conv_w, v_g_ffn, v_g_final]
    small_g = [g_b_ada, g_g_mix, g_b_in, g_sinks, g_conv, g_g_ffn, g_g_final]
    small_g = [g.reshape(w.shape) for g, w in zip(small_g, small_w)]
    flat = lambda arrs: _to_rows(jnp.concatenate([a.reshape(-1) for a in arrs]))
    sd, snm, snv = _adamw(flat(small_w), flat(small_g), flat(small_m), flat(small_v), "adamw_small")
    sizes = [w.size for w in small_w]
    starts = [sum(sizes[:i]) for i in range(len(sizes))]
    unflat = lambda a: {n: a.reshape(-1)[s:s + z_].reshape(w.shape)
                        for n, s, z_, w in zip(small_names, starts, sizes, small_w)}
    sd, snm, snv = unflat(sd), unflat(snm), unflat(snv)
    sg = dict(zip(small_names, small_g))

    res = {
        "w_ada": (g_w_ada[None], d_ada[None], nm_ada[None], nv_ada[None]),
        "w_in": reduced(mine_in, ici_in, w_in, m_w_in, v_w_in, "adamw_w_in", transposed=True),
        "w_out": reduced(mine_out, ici_out, w_out, m_w_out, v_w_out, "adamw_w_out"),
        "w_ffn_in": reduced(mine_fi, ici_fi, w_ffn_in, m_w_ffn_in, v_w_ffn_in, "adamw_w_ffn_in", transposed=True),
        "w_ffn_out": reduced(mine_fo, ici_fo, w_ffn_out, m_w_ffn_out, v_w_ffn_out, "adamw_w_ffn_out"),
    }
    for n in small_names:
        res[n] = (sg[n], sd[n], snm[n], snv[n])
    order = ["w_ada", "b_ada", "g_mix", "w_in", "b_in", "sinks", "conv_w", "w_out", "g_ffn", "w_ffn_in", "w_ffn_out",
             "g_final"]
    outs = [loss, grad_x[None]]
    for k in range(4):
        outs += [res[n][k] for n in order]
    return tuple(outs)
```

```python
import functools
import math

import jax
import jax.numpy as jnp
from jax import lax
from jax.experimental import pallas as pl
from jax.experimental.pallas import tpu as pltpu

F32 = jnp.float32
BF16 = jnp.bfloat16
GRAD_STREAM = F32

D_MODEL = 1024
HEAD_DIM = 64
N_Q_HEADS = 16
N_KV_HEADS = 2
GROUP = 8
WINDOW = 128
KV_WIDTH = N_KV_HEADS * HEAD_DIM
D_FF = 2816
IN_WIDTH = 6400
N_MOD = 6
EPS = 1e-6
N_DEV = 8
REST_WIDTH = 5 * D_MODEL
KV_COL = D_MODEL + REST_WIDTH
ATTN_SCALE = HEAD_DIM ** -0.5

ADAM_LR = 0.001
ADAM_B1 = 0.9
ADAM_B2 = 0.999
ADAM_EPS = 1e-08
ADAM_WD = 0.01
ADAM_STEP = 10

LANES = 128
SUBLANES = 8
BF16_ROWS = 16
VMEM_LIMIT = 56 * 1024 * 1024
TOKEN_TILE = 512
FF_CHUNK = 256
ROW_PARTS = 2
WGRAD_VMEM = 40 * 1024 * 1024
MESH = pl.DeviceIdType.MESH
ANY = pl.BlockSpec(memory_space=pl.ANY)

NT_DIMS = (((1,), (1,)), ((), ()))
TN_DIMS = (((0,), (0,)), ((), ()))
CHIP_FLIPS = [(0, 0), (1, 0), (0, 1), (1, 1)]


def _full(shape):
    return pl.BlockSpec(shape, lambda *_: (0,) * len(shape))


def _my_place():
    return lax.axis_index("x"), lax.axis_index("y"), lax.axis_index("c")


def _flip(v, bit):
    return 1 - v if bit else v


def _sigmoid(v):
    return 1.0 / (1.0 + jnp.exp2(v * (-1.4426950408889634)))


class _Rider:
    def __init__(self, ins, out_shapes, sem_shapes, first=None, mid=None, last=None, ins_in_vmem=False):
        self.ins, self.out_shapes, self.sem_shapes = list(ins), list(out_shapes), list(sem_shapes)
        self.in_specs = [_full(a.shape) if ins_in_vmem else ANY for a in self.ins]
        self.hooks = [(when, fn) for when, fn in (("first", first), ("mid", mid), ("last", last)) if fn is not None]


def _call(body, name, grid, args, in_specs, out_shape, out_specs, scratch=(), rider=None, aliases=None):
    n_in, n_out, n_scr = len(args), len(out_shape), len(scratch)
    r_in = rider.ins if rider else []
    r_out = rider.out_shapes if rider else []
    r_sem = rider.sem_shapes if rider else []
    nsteps = math.prod(grid)

    def full_body(*refs):
        pos = 0
        groups = []
        for size in (n_in, len(r_in), n_out, len(r_out), n_scr, len(r_sem)):
            groups.append(refs[pos:pos + size])
            pos += size
        ins, rins, outs, routs, scr, rsems = groups
        step = pl.program_id(0)
        for axis in range(1, len(grid)):
            step = step * grid[axis] + pl.program_id(axis)
        at = {"first": 0, "mid": (3 * nsteps) // 4, "last": nsteps - 1}
        hooks = rider.hooks if rider else []
        for when, fn in hooks:
            if when != "last":
                pl.when(step == at[when])(functools.partial(fn, rins, routs, rsems))
        body(*ins, *outs, *scr)
        for when, fn in hooks:
            if when == "last":
                pl.when(step == at[when])(functools.partial(fn, rins, routs, rsems))

    outs = pl.pallas_call(
        full_body, name=name, grid=grid,
        out_shape=list(out_shape) + list(r_out),
        in_specs=list(in_specs) + (rider.in_specs if rider else []),
        out_specs=list(out_specs) + [ANY] * len(r_out),
        scratch_shapes=list(scratch) + list(r_sem),
        input_output_aliases=dict(aliases or {}),
        compiler_params=pltpu.CompilerParams(dimension_semantics=("arbitrary",) * len(grid),
                                             vmem_limit_bytes=VMEM_LIMIT),
    )(*args, *r_in)
    return list(outs[:n_out]), list(outs[n_out:])


def _gather_rider(shards):
    n = len(shards)

    def setup(outs, sems):
        x, y, c = _my_place()
        send_sems, recv_sems, _ = sems
        chips = [(1 - x, y), (x, 1 - y), (1 - x, 1 - y)]

        def block(w, place):
            return outs[w].at[4 * place[0] + 2 * place[1] + place[2]]

        def copy(w, k, place, to, src=None):
            return pltpu.make_async_remote_copy(
                src_ref=block(w, place) if src is None else src, dst_ref=block(w, place),
                send_sem=send_sems.at[w, k], recv_sem=recv_sems.at[w, k], device_id=to, device_id_type=MESH)

        return (x, y, c), (x, y, 1 - c), chips, block, copy

    def first(ins, outs, sems):
        me, sibling, chips, block, copy = setup(outs, sems)
        for w in range(n):
            pltpu.make_async_copy(ins[w], block(w, me), sems[2].at[w]).start()
            copy(w, 0, me, sibling, src=ins[w]).start()
            for j, chip in enumerate(chips):
                copy(w, 1 + j, me, (*chip, me[2]), src=ins[w]).start()

    def mid(ins, outs, sems):
        me, sibling, chips, block, copy = setup(outs, sems)
        for w in range(n):
            for j, chip in enumerate(chips):
                copy(w, 1 + j, (*chip, me[2]), me).wait_recv()
                copy(w, 4 + j, (*chip, me[2]), sibling).start()

    def last(ins, outs, sems):
        me, sibling, chips, block, copy = setup(outs, sems)
        for w in range(n):
            copy(w, 0, sibling, me).wait_recv()
            for j, chip in enumerate(chips):
                copy(w, 4 + j, (*chip, 1 - me[2]), me).wait_recv()
            copy(w, 0, me, sibling, src=ins[w]).wait_send()
            for j, chip in enumerate(chips):
                copy(w, 1 + j, me, (*chip, me[2]), src=ins[w]).wait_send()
                copy(w, 4 + j, (*chip, me[2]), sibling).wait_send()
            pltpu.make_async_copy(ins[w], block(w, me), sems[2].at[w]).wait()

    return _Rider(
        shards, [jax.ShapeDtypeStruct((N_DEV,) + s.shape, BF16) for s in shards],
        [pltpu.SemaphoreType.DMA((n, N_DEV - 1)), pltpu.SemaphoreType.DMA((n, N_DEV - 1)),
         pltpu.SemaphoreType.DMA((n,))],
        first=first, mid=mid, last=last, ins_in_vmem=True)


def _sibling_rider(gblocks):
    n = len(gblocks)

    def copies(ins, outs, sems):
        x, y, c = _my_place()
        send_sems, recv_sems = sems
        made = []
        for w in range(n):
            for f, (fx, fy) in enumerate(CHIP_FLIPS):
                chip = 4 * _flip(x, fx) + 2 * _flip(y, fy)
                made.append(pltpu.make_async_remote_copy(
                    src_ref=ins[w].at[chip + 1 - c], dst_ref=outs[w].at[f], send_sem=send_sems.at[w, f],
                    recv_sem=recv_sems.at[w, f], device_id=(x, y, 1 - c), device_id_type=MESH))
        return made

    def first(ins, outs, sems):
        for cp in copies(ins, outs, sems):
            cp.start()

    def last(ins, outs, sems):
        for cp in copies(ins, outs, sems):
            cp.wait_recv()
            cp.wait_send()

    return _Rider(gblocks, [jax.ShapeDtypeStruct((4,) + g.shape[1:], BF16) for g in gblocks],
                  [pltpu.SemaphoreType.DMA((n, 4))] * 2, first=first, last=last)


def _chip_rider(sums):
    n = len(sums)

    def copies(ins, outs, sems):
        x, y, c = _my_place()
        send_sems, recv_sems = sems
        made = []
        for w in range(n):
            for f in (1, 2, 3):
                fx, fy = CHIP_FLIPS[f]
                made.append(pltpu.make_async_remote_copy(
                    src_ref=ins[w].at[f - 1], dst_ref=outs[w].at[f - 1], send_sem=send_sems.at[w, f - 1],
                    recv_sem=recv_sems.at[w, f - 1], device_id=(_flip(x, fx), _flip(y, fy), c), device_id_type=MESH))
        return made

    def first(ins, outs, sems):
        for cp in copies(ins, outs, sems):
            cp.start()

    def last(ins, outs, sems):
        for cp in copies(ins, outs, sems):
            cp.wait_recv()
            cp.wait_send()

    return _Rider(sums, [jax.ShapeDtypeStruct(s.shape, BF16) for s in sums],
                  [pltpu.SemaphoreType.DMA((n, 3))] * 2, first=first, last=last)


def _push_to_all(v_ref, out_ref, send_sems, recv_sems, local_sem, wait=True):
    x, y, c = _my_place()
    me = 4 * x + 2 * y + c
    mine = pltpu.make_async_copy(v_ref, out_ref.at[me], local_sem)
    mine.start()
    sends = []
    for k in range(1, N_DEV):
        px, py, pc = _flip(x, k & 4), _flip(y, k & 2), _flip(c, k & 1)
        cp = pltpu.make_async_remote_copy(
            src_ref=v_ref, dst_ref=out_ref.at[me], send_sem=send_sems.at[k - 1], recv_sem=recv_sems.at[k - 1],
            device_id=(px, py, pc), device_id_type=MESH)
        cp.start()
        sends.append(cp)

    def finish():
        for k in range(1, N_DEV):
            px, py, pc = _flip(x, k & 4), _flip(y, k & 2), _flip(c, k & 1)
            pltpu.make_async_remote_copy(
                src_ref=v_ref, dst_ref=out_ref.at[4 * px + 2 * py + pc], send_sem=send_sems.at[k - 1],
                recv_sem=recv_sems.at[k - 1], device_id=(px, py, pc), device_id_type=MESH).wait_recv()
        for cp in sends:
            cp.wait_send()
        mine.wait()

    if wait:
        finish()
    return finish


def _small_allgather(v, name):
    def body(v_ref, out_ref, send_sems, recv_sems, local_sem):
        _push_to_all(v_ref, out_ref, send_sems, recv_sems, local_sem)

    return pl.pallas_call(
        body, name=name,
        out_shape=jax.ShapeDtypeStruct((N_DEV,) + v.shape, F32),
        in_specs=[pl.BlockSpec(memory_space=pltpu.VMEM)],
        out_specs=pl.BlockSpec(memory_space=pltpu.VMEM),
        scratch_shapes=[pltpu.SemaphoreType.DMA((N_DEV - 1,)), pltpu.SemaphoreType.DMA((N_DEV - 1,)),
                        pltpu.SemaphoreType.DMA],
        compiler_params=pltpu.CompilerParams(vmem_limit_bytes=VMEM_LIMIT),
    )(v)


def _gather_first_weight(shard, others, cond_rows, w_ada, b_cols):
    n = len(others)
    ada_cols = w_ada.shape[1]
    c_rows = D_MODEL // LANES

    def body(*refs):
        w_ref, other_refs = refs[0], refs[1:1 + n]
        cond_ref, wada_ref, bcols_ref = refs[1 + n:4 + n]
        out_ref, cast_refs = refs[4 + n], refs[5 + n:5 + 2 * n]
        cond_all_ref, mod_all_ref = refs[5 + 2 * n:7 + 2 * n]
        mine_ref, mod_ref, send_sems, recv_sems, local_sem, small_send, small_recv, small_local = refs[7 + 2 * n:]
        x, y, c = _my_place()
        me, sibling = (x, y, c), (x, y, 1 - c)
        chips = [(1 - x, y), (x, 1 - y), (1 - x, 1 - y)]

        def block(place):
            return out_ref.at[4 * place[0] + 2 * place[1] + place[2]]

        def copy(k, place, to, src=None):
            return pltpu.make_async_remote_copy(
                src_ref=block(place) if src is None else src, dst_ref=block(place),
                send_sem=send_sems.at[k], recv_sem=recv_sems.at[k], device_id=to, device_id_type=MESH)

        finish_cond = _push_to_all(cond_ref, cond_all_ref, small_send.at[0], small_recv.at[0], small_local.at[0],
                                   wait=False)
        mine_ref[...] = w_ref[...].astype(BF16)
        finish_cond()
        local = pltpu.make_async_copy(mine_ref, block(me), local_sem)
        local.start()
        started = [copy(0, me, sibling, src=mine_ref)]
        started += [copy(1 + j, me, (*chip, c), src=mine_ref) for j, chip in enumerate(chips)]
        for cp in started:
            cp.start()
        mod = jnp.zeros((N_DEV, ada_cols), F32) + bcols_ref[...]
        for r in range(c_rows):
            cf = cond_all_ref[:, r, :]
            act = (cf * _sigmoid(cf)).astype(BF16)
            mod = mod + jnp.dot(act, wada_ref[r * LANES:(r + 1) * LANES, :].astype(BF16),
                                preferred_element_type=F32)
        mod_ref[...] = mod
        finish_mod = _push_to_all(mod_ref, mod_all_ref, small_send.at[1], small_recv.at[1], small_local.at[1],
                                  wait=False)
        for o_ref, c_ref in zip(other_refs, cast_refs):
            c_ref[...] = o_ref[...].astype(BF16)
        for j, chip in enumerate(chips):
            copy(1 + j, (*chip, c), me).wait_recv()
            passed = copy(4 + j, (*chip, c), sibling)
            passed.start()
            started.append(passed)
        copy(0, sibling, me).wait_recv()
        for j, chip in enumerate(chips):
            copy(4 + j, (*chip, 1 - c), me).wait_recv()
        finish_mod()
        for cp in started:
            cp.wait_send()
        local.wait()

    vmem = pl.BlockSpec(memory_space=pltpu.VMEM)
    outs = pl.pallas_call(
        body, name="gather_w_in",
        out_shape=[jax.ShapeDtypeStruct((N_DEV,) + shard.shape, BF16)]
        + [jax.ShapeDtypeStruct(o.shape, BF16) for o in others]
        + [jax.ShapeDtypeStruct((N_DEV,) + cond_rows.shape, F32), jax.ShapeDtypeStruct((N_DEV, N_DEV, ada_cols), F32)],
        in_specs=[vmem] * (4 + n),
        out_specs=[ANY] + [vmem] * (n + 2),
        scratch_shapes=[pltpu.VMEM(shard.shape, BF16), pltpu.VMEM((N_DEV, ada_cols), F32),
                        pltpu.SemaphoreType.DMA((N_DEV - 1,)), pltpu.SemaphoreType.DMA((N_DEV - 1,)),
                        pltpu.SemaphoreType.DMA,
                        pltpu.SemaphoreType.DMA((2, N_DEV - 1)), pltpu.SemaphoreType.DMA((2, N_DEV - 1)),
                        pltpu.SemaphoreType.DMA((2,))],
        compiler_params=pltpu.CompilerParams(vmem_limit_bytes=VMEM_LIMIT),
    )(shard, *others, cond_rows, w_ada, b_cols)
    return outs[0], list(outs[1:1 + n]), outs[1 + n], outs[2 + n]


def _carry(rider, name):
    def body(token_ref):
        token_ref[...] = jnp.zeros_like(token_ref)

    _, routs = _call(body, name, (1,), [], [], [jax.ShapeDtypeStruct((SUBLANES, LANES), F32)],
                     [_full((SUBLANES, LANES))], rider=rider)
    return routs


def _ada_weight_grad(c_all, dmod_cols):
    cols = dmod_cols.shape[1]

    def body(c_ref, d_ref, out_ref):
        cf = c_ref[...]
        act = (cf * _sigmoid(cf)).astype(BF16)
        out_ref[...] = lax.dot_general(act, d_ref[...].astype(BF16), TN_DIMS, preferred_element_type=F32)

    return pl.pallas_call(
        body, name="ada_weight_grad",
        out_shape=jax.ShapeDtypeStruct((D_MODEL, cols), F32),
        in_specs=[pl.BlockSpec(memory_space=pltpu.VMEM)] * 2,
        out_specs=pl.BlockSpec(memory_space=pltpu.VMEM),
        compiler_params=pltpu.CompilerParams(vmem_limit_bytes=VMEM_LIMIT),
    )(c_all, dmod_cols)


PACK_ROWS = 24
PACK_DMOD = 0
PACK_PARAMS = {"g_mix": (6, D_MODEL), "b_in": (7, IN_WIDTH), "g_ffn": (14, D_MODEL), "g_final": (15, D_MODEL),
               "sinks": (19, N_Q_HEADS)}
PACK_CONV = 16
PACK_SQERR = 20


def _small_finalize(packed_all, params):
    names = ["b_ada"] + list(PACK_PARAMS)
    layout = dict(PACK_PARAMS, b_ada=(PACK_DMOD, N_MOD * D_MODEL))
    n = len(names)

    def body(*refs):
        p_ref = refs[0]
        ins = refs[1:1 + 3 * n]
        outs = refs[1 + 3 * n:1 + 7 * n]
        conv_ref, loss_ref = refs[1 + 7 * n:]
        total = p_ref[0]
        for d in range(1, N_DEV):
            total = total + p_ref[d]
        for k, name in enumerate(names):
            row0, width = layout[name]
            w_ref, m_ref, v_ref = ins[3 * k:3 * k + 3]
            g_ref, d_ref, nm_ref, nv_ref = outs[4 * k:4 * k + 4]
            for chunk in range(-(-width // D_MODEL)):
                lo = chunk * D_MODEL
                hi = min(lo + D_MODEL, width)
                g = total[row0 + chunk:row0 + chunk + 1, :hi - lo]
                g_ref[:, lo:hi] = g
                d_ref[:, lo:hi], nm_ref[:, lo:hi], nv_ref[:, lo:hi] = _adamw_update(
                    w_ref[:, lo:hi], g, m_ref[:, lo:hi], v_ref[:, lo:hi])
        conv_ref[...] = total[PACK_CONV:PACK_CONV + 3, :]
        loss_ref[...] = (0.5 / D_MODEL) * jnp.sum(total[PACK_SQERR:PACK_SQERR + 1, :], keepdims=True)

    vmem = pl.BlockSpec(memory_space=pltpu.VMEM)
    flat = [a for name in names for a in params[name]]
    out_shape = [jax.ShapeDtypeStruct(params[name][0].shape, F32) for name in names for _ in range(4)]
    outs = pl.pallas_call(
        body, name="small_finalize",
        out_shape=out_shape + [jax.ShapeDtypeStruct((3, D_MODEL), F32), jax.ShapeDtypeStruct((1, 1), F32)],
        in_specs=[vmem] * (1 + 3 * n),
        out_specs=[vmem] * (4 * n + 2),
        compiler_params=pltpu.CompilerParams(vmem_limit_bytes=VMEM_LIMIT),
    )(packed_all, *flat)
    return {name: tuple(outs[4 * k:4 * k + 4]) for k, name in enumerate(names)}, outs[4 * n], outs[4 * n + 1]


def _row_tile(rows, multiple):
    for cand in range(min(rows, 256), 0, -1):
        if rows % cand == 0 and cand % multiple == 0:
            return cand
    return rows


def _adamw_update(w, g, m, v):
    c1 = 1.0 / (1.0 - ADAM_B1 ** ADAM_STEP)
    c2 = 1.0 / (1.0 - ADAM_B2 ** ADAM_STEP)
    nm = ADAM_B1 * m + (1.0 - ADAM_B1) * g
    nv = ADAM_B2 * v + (1.0 - ADAM_B2) * (g * g)
    delta = -ADAM_LR * ((nm * c1) / (jnp.sqrt(nv * c2) + ADAM_EPS) + ADAM_WD * w)
    return delta, nm, nv


def _adamw(w, g, m, v, name):
    rows, cols = w.shape
    tile = _row_tile(rows, SUBLANES)

    def body(w_ref, g_ref, m_ref, v_ref, d_ref, nm_ref, nv_ref):
        d_ref[...], nm_ref[...], nv_ref[...] = _adamw_update(w_ref[...], g_ref[...], m_ref[...], v_ref[...])

    spec = pl.BlockSpec((tile, cols), lambda i: (i, 0))
    outs, _ = _call(body, name, (rows // tile,), [w, g, m, v], [spec] * 4,
                    [jax.ShapeDtypeStruct((rows, cols), F32)] * 3, [spec] * 3)
    return outs


def _sibling_sum(gblocks, sib, name):
    _, r, cdim = gblocks.shape
    tile = _row_tile(r, BF16_ROWS)
    x, y, c = _my_place()
    table = jnp.stack([4 * _flip(x, fx) + 2 * _flip(y, fy) + c for fx, fy in CHIP_FLIPS]).astype(jnp.int32)

    def body(table_ref, own0, own1, own2, own3, sib_ref, sums_ref, mine_ref):
        mine_ref[...] = own0[...].astype(F32) + sib_ref[0].astype(F32)
        for f, own in ((1, own1), (2, own2), (3, own3)):
            sums_ref[f - 1] = (own[...].astype(F32) + sib_ref[f].astype(F32)).astype(BF16)

    own_specs = [pl.BlockSpec((None, tile, cdim), functools.partial(lambda i, tab, f: (tab[f], i, 0), f=f))
                 for f in range(4)]
    return pl.pallas_call(
        body, name=name,
        grid_spec=pltpu.PrefetchScalarGridSpec(
            num_scalar_prefetch=1, grid=(r // tile,),
            in_specs=own_specs + [pl.BlockSpec((4, tile, cdim), lambda i, tab: (0, i, 0))],
            out_specs=[pl.BlockSpec((3, tile, cdim), lambda i, tab: (0, i, 0)),
                       pl.BlockSpec((tile, cdim), lambda i, tab: (i, 0))]),
        out_shape=[jax.ShapeDtypeStruct((3, r, cdim), BF16), jax.ShapeDtypeStruct((r, cdim), F32)],
        compiler_params=pltpu.CompilerParams(dimension_semantics=("arbitrary",), vmem_limit_bytes=VMEM_LIMIT),
    )(table, gblocks, gblocks, gblocks, gblocks, sib)


def _chip_sum_adamw(mine, ici, w, m, v, name):
    r, cdim = mine.shape
    tile = _row_tile(r, BF16_ROWS)

    def body(mine_ref, ici_ref, w_ref, m_ref, v_ref, g_ref, d_ref, nm_ref, nv_ref):
        g = mine_ref[...]
        for f in range(3):
            g = g + ici_ref[f].astype(F32)
        g_ref[...] = g
        d_ref[...], nm_ref[...], nv_ref[...] = _adamw_update(w_ref[...], g, m_ref[...], v_ref[...])

    spec = pl.BlockSpec((tile, cdim), lambda i: (i, 0))
    outs, _ = _call(
        body, name, (r // tile,), [mine, ici, w, m, v],
        [spec, pl.BlockSpec((3, tile, cdim), lambda i: (0, i, 0)), spec, spec, spec],
        [jax.ShapeDtypeStruct((r, cdim), F32)] * 4, [spec] * 4)
    return outs


REF_KV_COL = D_MODEL
REF_REST_COL = D_MODEL + 2 * KV_WIDTH
IN_CHUNK = 1280
IN_PIECES = ([(0, 0, D_MODEL)]
             + [(D_MODEL + n * IN_CHUNK, REF_REST_COL + n * IN_CHUNK, IN_CHUNK) for n in range(REST_WIDTH // IN_CHUNK)]
             + [(KV_COL, REF_KV_COL, 2 * KV_WIDTH)])


def _inproj_fwd(x, vec, w_t, b_in, rider):
    t = x.shape[0]
    tm = min(TOKEN_TILE, t)

    def body(x_ref, vec_ref, w_ref, b_ref, z_ref, h_ref):
        xf = x_ref[...]
        r = lax.rsqrt(jnp.mean(xf * xf, axis=-1, keepdims=True) + EPS)
        h = (xf * r) * vec_ref[0:1, :] * (1.0 + vec_ref[1:2, :]) + vec_ref[2:3, :]
        hb = h.astype(BF16)
        h_ref[...] = hb
        for mine, ref, width in IN_PIECES:
            zc = lax.dot_general(hb, w_ref[ref:ref + width, :], NT_DIMS, preferred_element_type=F32)
            z_ref[:, mine:mine + width] = (zc + b_ref[:, ref:ref + width]).astype(BF16)

    return _call(
        body, "inproj_fwd", (t // tm,), [x, vec, w_t, b_in],
        [pl.BlockSpec((tm, D_MODEL), lambda i: (i, 0)), _full((SUBLANES, D_MODEL)),
         _full((IN_WIDTH, D_MODEL)), _full((1, IN_WIDTH))],
        [jax.ShapeDtypeStruct((t, IN_WIDTH), BF16), jax.ShapeDtypeStruct((t, D_MODEL), BF16)],
        [pl.BlockSpec((tm, IN_WIDTH), lambda i: (i, 0)), pl.BlockSpec((tm, D_MODEL), lambda i: (i, 0))],
        rider=rider)


def _window_mask(has_prev):
    qi = lax.broadcasted_iota(jnp.int32, (WINDOW, 2 * WINDOW), 0)
    kj = lax.broadcasted_iota(jnp.int32, (WINDOW, 2 * WINDOW), 1)
    off = jnp.where(has_prev, 0, 4 * WINDOW)
    in_prev = jnp.logical_and(kj < WINDOW, kj > qi + off)
    in_cur = jnp.logical_and(kj >= WINDOW, (kj - WINDOW) <= qi)
    return jnp.logical_or(in_prev, in_cur)


PAIRS = GROUP // 2
STACK = PAIRS * WINDOW


ATTN_BLOCKS = 4
ATTN_BWD_BLOCKS = 1
LOG2E = 1.4426950408889634
LN2 = 0.6931471805599453
SCORE_SCALE = ATTN_SCALE * LOG2E


def _fill_window_bias(bias_ref):
    shape = bias_ref.shape[1:]
    kj = lax.broadcasted_iota(jnp.int32, shape, 0)
    qi = jnp.bitwise_and(lax.broadcasted_iota(jnp.int32, shape, 1), WINDOW - 1)
    in_prev = jnp.logical_and(kj < WINDOW, kj > qi)
    in_cur = jnp.logical_and(kj >= WINDOW, (kj - WINDOW) <= qi)
    bias_ref[0] = jnp.where(in_cur, 0.0, -jnp.inf)
    bias_ref[1] = jnp.where(jnp.logical_or(in_prev, in_cur), 0.0, -jnp.inf)


def _half_tiles(tile):
    low = lax.broadcasted_iota(jnp.int32, tile.shape, 1) < HEAD_DIM
    swapped = jnp.concatenate([tile[:, HEAD_DIM:], tile[:, :HEAD_DIM]], axis=1)
    zero = jnp.zeros_like(tile)
    return ((jnp.where(low, tile, zero), jnp.where(low, zero, swapped)),
            (jnp.where(low, swapped, zero), jnp.where(low, zero, tile)))


def _stack_pairs(ref, row0, j):
    return jnp.concatenate(
        [ref[pl.ds(row0, WINDOW), (j * PAIRS + p) * LANES:(j * PAIRS + p + 1) * LANES] for p in range(PAIRS)], axis=0)


def _per_pair_row(values):
    pair = lax.broadcasted_iota(jnp.int32, (1, STACK), 1) // WINDOW
    row = jnp.full((1, STACK), values[PAIRS - 1], F32)
    for p in range(PAIRS - 2, -1, -1):
        row = jnp.where(pair == p, values[p], row)
    return row


def _attn_fwd(z, sinks, rider):
    t = z.shape[0]
    tq = min(TOKEN_TILE, t)
    nblk = tq // WINDOW

    def body(q_ref, kv_ref, sink_ref, o_ref, lse_ref, bias_ref):
        i = pl.program_id(0)

        @pl.when(i == 0)
        def _():
            _fill_window_bias(bias_ref)

        def window(b):
            row0 = pl.multiple_of(b * WINDOW, WINDOW)
            start = i * tq + b * WINDOW
            prev = pl.multiple_of(jnp.maximum(start - WINDOW, 0), WINDOW)
            cur = pl.multiple_of(start, WINDOW)
            kvw = jnp.concatenate([kv_ref[pl.ds(prev, WINDOW), :], kv_ref[pl.ds(cur, WINDOW), :]], axis=0)
            return row0, _half_tiles(kvw[:, :KV_WIDTH]), _half_tiles(kvw[:, KV_WIDTH:]), bias_ref[jnp.minimum(start, 1)]

        def block_group(bb, carry):
            windows = [window(bb * ATTN_BLOCKS + n) for n in range(ATTN_BLOCKS)]
            for j in range(N_KV_HEADS):
                for pr in range(PAIRS):
                    cols = slice((j * PAIRS + pr) * LANES, (j * PAIRS + pr + 1) * LANES)
                    o_ts = [jnp.zeros((LANES, WINDOW), F32) for _ in windows]
                    for parity in range(2):
                        h = j * GROUP + 2 * pr + parity
                        sink = sink_ref[h] * LOG2E
                        for n, (row0, k_halves, v_halves, bias) in enumerate(windows):
                            qp = q_ref[pl.ds(row0, WINDOW), cols]
                            s = lax.dot_general(k_halves[j][parity], qp, NT_DIMS, preferred_element_type=F32)
                            s = s * SCORE_SCALE + bias
                            m = jnp.maximum(jnp.max(s, axis=0, keepdims=True), sink)
                            p = jnp.exp2(s - m)
                            denom = jnp.sum(p, axis=0, keepdims=True) + jnp.exp2(sink - m)
                            pv = lax.dot_general(v_halves[j][parity], p.astype(BF16), TN_DIMS,
                                                 preferred_element_type=F32)
                            o_ts[n] = o_ts[n] + pv * (1.0 / denom)
                            lse_ref[h:h + 1, pl.ds(row0, WINDOW)] = m + jnp.log2(denom)
                    for n, (row0, _, _, _) in enumerate(windows):
                        o_ref[pl.ds(row0, WINDOW), cols] = jnp.transpose(o_ts[n].astype(BF16))
            return carry

        lax.fori_loop(0, nblk // ATTN_BLOCKS, block_group, 0)

    return _call(
        body, "attn_fwd", (t // tq,), [z, z, sinks],
        [pl.BlockSpec((tq, D_MODEL), lambda i: (i, 0)),
         pl.BlockSpec((t, 2 * KV_WIDTH), lambda i: (0, KV_COL // (2 * KV_WIDTH))),
         pl.BlockSpec(memory_space=pltpu.SMEM)],
        [jax.ShapeDtypeStruct((t, D_MODEL), BF16), jax.ShapeDtypeStruct((N_Q_HEADS, t), F32)],
        [pl.BlockSpec((tq, D_MODEL), lambda i: (i, 0)), pl.BlockSpec((N_Q_HEADS, tq), lambda i: (0, i))],
        scratch=[pltpu.VMEM((2, 2 * WINDOW, WINDOW), F32)], rider=rider)


HALO = BF16_ROWS


def _shift_down(u, uh, k):
    row = lax.broadcasted_iota(jnp.int32, u.shape, 0)
    out = pltpu.roll(u, k, 0)
    for j in range(k):
        out = jnp.where(row == j, uh[HALO - k + j:HALO - k + j + 1, :], out)
    return out


def _shift_up(u, nxt, k):
    n = u.shape[0]
    row = lax.broadcasted_iota(jnp.int32, u.shape, 0)
    out = pltpu.roll(u, n - k, 0)
    for j in range(k):
        out = jnp.where(row == n - k + j, nxt[j:j + 1, :], out)
    return out


def _conv_inputs(cc_ref, cx_ref, hc_ref, hx_ref, first_tile):
    cc = cc_ref[...].astype(F32)
    cx = cx_ref[...].astype(F32)
    u = cc * cx
    uh = jnp.where(first_tile, 0.0, hc_ref[...].astype(F32) * hx_ref[...].astype(F32))
    return cc, cx, u, _shift_down(u, uh, 1), _shift_down(u, uh, 2)


def _z_specs(tm, order):
    per_tile = tm // HALO
    cols = [pl.BlockSpec((tm, D_MODEL), functools.partial(lambda i, j: (order(i), j), j=j)) for j in range(1, 6)]
    halos = [pl.BlockSpec((HALO, D_MODEL),
                          functools.partial(lambda i, j: (jnp.maximum(order(i) * per_tile - 1, 0), j), j=j))
             for j in (2, 3)]
    return cols + halos


def _mix_fwd(x, attn, z, vec, w_out):
    t = x.shape[0]
    tm = min(TOKEN_TILE, t)

    def body(x_ref, a_ref, cb_ref, cc_ref, cx_ref, ga_ref, gc_ref, hc_ref, hx_ref, vec_ref, w_ref,
             m_ref, x2_ref, h2_ref, o_ref):
        i = pl.program_id(0)
        _, _, u, u1, u2 = _conv_inputs(cc_ref, cx_ref, hc_ref, hx_ref, i == 0)
        cv = vec_ref[4:5, :] * u2 + vec_ref[5:6, :] * u1 + vec_ref[6:7, :] * u
        conv = cb_ref[...].astype(F32) * cv
        merged = (_sigmoid(ga_ref[...].astype(F32)) * a_ref[...].astype(F32)
                  + _sigmoid(gc_ref[...].astype(F32)) * conv)
        mb = merged.astype(BF16)
        m_ref[...] = mb
        o = jnp.dot(mb, w_ref[...], preferred_element_type=F32)
        o_ref[...] = o.astype(BF16)
        x2 = x_ref[...] + vec_ref[0:1, :] * o
        x2_ref[...] = x2
        r = lax.rsqrt(jnp.mean(x2 * x2, axis=-1, keepdims=True) + EPS)
        h2 = (x2 * r) * vec_ref[1:2, :] * (1.0 + vec_ref[2:3, :]) + vec_ref[3:4, :]
        h2_ref[...] = h2.astype(BF16)

    tok = pl.BlockSpec((tm, D_MODEL), lambda i: (i, 0))
    outs, _ = _call(
        body, "mix_fwd", (t // tm,), [x, attn, z, z, z, z, z, z, z, vec, w_out],
        [tok, tok] + _z_specs(tm, lambda i: i) + [_full((SUBLANES, D_MODEL)), _full((D_MODEL, D_MODEL))],
        [jax.ShapeDtypeStruct((t, D_MODEL), BF16), jax.ShapeDtypeStruct((t, D_MODEL), F32),
         jax.ShapeDtypeStruct((t, D_MODEL), BF16), jax.ShapeDtypeStruct((t, D_MODEL), BF16)],
        [tok, tok, tok, tok])
    return outs


def _ffn_fwd(h2, w_t):
    t = h2.shape[0]
    tm = min(TOKEN_TILE, t)

    def body(h_ref, w_ref, gu_ref, a_ref):
        hb = h_ref[...]
        for n in range(D_FF // FF_CHUNK):
            lo, hi = n * FF_CHUNK, (n + 1) * FF_CHUNK
            g = lax.dot_general(hb, w_ref[lo:hi, :], NT_DIMS, preferred_element_type=F32)
            u = lax.dot_general(hb, w_ref[D_FF + lo:D_FF + hi, :], NT_DIMS, preferred_element_type=F32)
            sg = _sigmoid(g)
            silu = g * sg
            gu_ref[:, lo:hi] = (u * (sg * (1.0 + g * (1.0 - sg)))).astype(BF16)
            gu_ref[:, D_FF + lo:D_FF + hi] = silu.astype(BF16)
            a_ref[:, lo:hi] = (silu * u).astype(BF16)

    outs, _ = _call(
        body, "ffn_fwd", (t // tm,), [h2, w_t],
        [pl.BlockSpec((tm, D_MODEL), lambda i: (i, 0)), _full((2 * D_FF, D_MODEL))],
        [jax.ShapeDtypeStruct((t, 2 * D_FF), BF16), jax.ShapeDtypeStruct((t, D_FF), BF16)],
        [pl.BlockSpec((tm, 2 * D_FF), lambda i: (i, 0)), pl.BlockSpec((tm, D_FF), lambda i: (i, 0))])
    return outs


def _ffn_out_loss(a, gu, x2, target, vec, w_ffn_out):
    t = a.shape[0]
    tm = min(TOKEN_TILE, t)

    def body(a_ref, gu_ref, x2_ref, t_ref, vec_ref, w_ref, dx3_ref, df_ref, dgu_ref, acc_ref):
        @pl.when(pl.program_id(0) == 0)
        def _():
            acc_ref[...] = jnp.zeros_like(acc_ref)

        ga2 = vec_ref[0:1, :]
        gf = vec_ref[1:2, :]
        parts = min(ROW_PARTS, tm // LANES)
        part_rows = [slice(n * (tm // parts), (n + 1) * (tm // parts)) for n in range(parts)]

        def head(rows, f):
            x3 = x2_ref[rows, :] + ga2 * f
            r = lax.rsqrt(jnp.mean(x3 * x3, axis=-1, keepdims=True) + EPS)
            xn = x3 * r
            err = xn * gf - t_ref[rows, :]
            dy = err * (1.0 / D_MODEL)
            dxn = dy * gf
            dx3 = r * (dxn - xn * jnp.mean(dxn * xn, axis=-1, keepdims=True))
            dx3_ref[rows, :] = dx3.astype(GRAD_STREAM)
            sums = (jnp.sum(err * err, axis=0, keepdims=True), jnp.sum(dy * xn, axis=0, keepdims=True),
                    jnp.sum(dx3 * f, axis=0, keepdims=True))
            df = (dx3 * ga2).astype(BF16)
            df_ref[rows, :] = df
            return df, sums

        def tail(rows, df):
            for n in range(D_FF // FF_CHUNK):
                lo, hi = n * FF_CHUNK, (n + 1) * FF_CHUNK
                da = lax.dot_general(df, w_ref[lo:hi, :], NT_DIMS, preferred_element_type=F32)
                dgu_ref[rows, lo:hi] = (da * gu_ref[rows, lo:hi].astype(F32)).astype(BF16)
                dgu_ref[rows, D_FF + lo:D_FF + hi] = (da * gu_ref[rows, D_FF + lo:D_FF + hi].astype(F32)).astype(BF16)

        fs = [jnp.dot(a_ref[rows, :], w_ref[...], preferred_element_type=F32) for rows in part_rows]
        heads = [head(rows, f) for rows, f in zip(part_rows, fs)]
        for rows, (df, _) in zip(part_rows, heads):
            tail(rows, df)
        for k in range(3):
            total = heads[0][1][k]
            for _, sums in heads[1:]:
                total = total + sums[k]
            acc_ref[k:k + 1, :] += total

    tok = pl.BlockSpec((tm, D_MODEL), lambda i: (i, 0))
    outs, _ = _call(
        body, "ffn_out_loss", (t // tm,), [a, gu, x2, target, vec, w_ffn_out],
        [pl.BlockSpec((tm, D_FF), lambda i: (i, 0)), pl.BlockSpec((tm, 2 * D_FF), lambda i: (i, 0)),
         tok, tok, _full((SUBLANES, D_MODEL)), _full((D_FF, D_MODEL))],
        [jax.ShapeDtypeStruct((t, D_MODEL), GRAD_STREAM), jax.ShapeDtypeStruct((t, D_MODEL), BF16),
         jax.ShapeDtypeStruct((t, 2 * D_FF), BF16), jax.ShapeDtypeStruct((SUBLANES, D_MODEL), F32)],
        [tok, tok, pl.BlockSpec((tm, 2 * D_FF), lambda i: (i, 0)), _full((SUBLANES, D_MODEL))])
    return outs


def _ffn_in_bwd(dgu, x2, dx3, vec, w_t, rider):
    t = x2.shape[0]
    tm = min(TOKEN_TILE, t)

    def body(dgu_ref, x2_ref, dx3_ref, vec_ref, wf_ref, dx2_ref, acc_ref):
        @pl.when(pl.program_id(0) == 0)
        def _():
            acc_ref[...] = jnp.zeros_like(acc_ref)

        gffn = vec_ref[0:1, :]
        sc2 = vec_ref[1:2, :]
        parts = min(ROW_PARTS, tm // LANES)
        part_rows = [slice(n * (tm // parts), (n + 1) * (tm // parts)) for n in range(parts)]
        dhs = [jnp.dot(dgu_ref[rows, :], wf_ref[...], preferred_element_type=F32) for rows in part_rows]
        sums = []
        for rows, dh2 in zip(part_rows, dhs):
            x2 = x2_ref[rows, :]
            r = lax.rsqrt(jnp.mean(x2 * x2, axis=-1, keepdims=True) + EPS)
            xn = x2 * r
            sums.append((jnp.sum(dh2, axis=0, keepdims=True), jnp.sum(dh2 * xn * gffn, axis=0, keepdims=True),
                         jnp.sum(dh2 * xn * (1.0 + sc2), axis=0, keepdims=True)))
            dxn = dh2 * gffn * (1.0 + sc2)
            dx2 = dx3_ref[rows, :].astype(F32) + r * (dxn - xn * jnp.mean(dxn * xn, axis=-1, keepdims=True))
            dx2_ref[rows, :] = dx2.astype(GRAD_STREAM)
        for k in range(3):
            total = sums[0][k]
            for part in sums[1:]:
                total = total + part[k]
            acc_ref[k:k + 1, :] += total

    tok = pl.BlockSpec((tm, D_MODEL), lambda i: (i, 0))
    return _call(
        body, "ffn_in_bwd", (t // tm,), [dgu, x2, dx3, vec, w_t],
        [pl.BlockSpec((tm, 2 * D_FF), lambda i: (i, 0)), tok, tok, _full((SUBLANES, D_MODEL)),
         _full((2 * D_FF, D_MODEL))],
        [jax.ShapeDtypeStruct((t, D_MODEL), GRAD_STREAM), jax.ShapeDtypeStruct((SUBLANES, D_MODEL), F32)],
        [tok, _full((SUBLANES, D_MODEL))], rider=rider)


def _mix_bwd(dx2, oproj, attn, z, vec, w_out, rider):
    t = dx2.shape[0]
    tm = min(TOKEN_TILE, t)
    nt = t // tm
    rev = lambda i: nt - 1 - i

    def body(dx2_ref, m_ref, a_ref, cb_ref, cc_ref, cx_ref, ga_ref, gc_ref, hc_ref, hx_ref,
             vec_ref, wo_ref, do_ref, da_ref, dr_ref, acc_ref, carry_ref):
        i = pl.program_id(0)

        @pl.when(i == 0)
        def _():
            acc_ref[...] = jnp.zeros_like(acc_ref)
            carry_ref[...] = jnp.zeros_like(carry_ref)

        ga1 = vec_ref[0:1, :]
        w0, w1, w2 = vec_ref[1:2, :], vec_ref[2:3, :], vec_ref[3:4, :]
        dx2 = dx2_ref[...].astype(F32)
        acc_ref[0:1, :] += jnp.sum(dx2 * m_ref[...].astype(F32), axis=0, keepdims=True)
        do = (dx2 * ga1).astype(BF16)
        do_ref[...] = do
        dm = lax.dot_general(do, wo_ref[...], NT_DIMS, preferred_element_type=F32)

        cc, cx, u, u1, u2 = _conv_inputs(cc_ref, cx_ref, hc_ref, hx_ref, i == nt - 1)
        cv = w0 * u2 + w1 * u1 + w2 * u
        cb = cb_ref[...].astype(F32)
        sa = _sigmoid(ga_ref[...].astype(F32))
        sc = _sigmoid(gc_ref[...].astype(F32))
        attn = a_ref[...].astype(F32)
        da_ref[...] = (dm * sa).astype(BF16)
        dconv = dm * sc
        dr_ref[:, 3 * D_MODEL:4 * D_MODEL] = (dm * attn * sa * (1.0 - sa)).astype(BF16)
        dr_ref[:, 4 * D_MODEL:5 * D_MODEL] = (dconv * (cb * cv) * (1.0 - sc)).astype(BF16)
        dr_ref[:, 0:D_MODEL] = (dconv * cv).astype(BF16)
        dcv = dconv * cb
        acc_ref[1:2, :] += jnp.sum(dcv * u2, axis=0, keepdims=True)
        acc_ref[2:3, :] += jnp.sum(dcv * u1, axis=0, keepdims=True)
        acc_ref[3:4, :] += jnp.sum(dcv * u, axis=0, keepdims=True)
        nxt = carry_ref[...]
        du = w2 * dcv + w1 * _shift_up(dcv, nxt, 1) + w0 * _shift_up(dcv, nxt, 2)
        carry_ref[...] = dcv[0:SUBLANES, :]
        dr_ref[:, D_MODEL:2 * D_MODEL] = (du * cx).astype(BF16)
        dr_ref[:, 2 * D_MODEL:3 * D_MODEL] = (du * cc).astype(BF16)

    tok = pl.BlockSpec((tm, D_MODEL), lambda i: (rev(i), 0))
    return _call(
        body, "mix_bwd", (nt,), [dx2, oproj, attn, z, z, z, z, z, z, z, vec, w_out],
        [tok, tok, tok] + _z_specs(tm, rev) + [_full((SUBLANES, D_MODEL)), _full((D_MODEL, D_MODEL))],
        [jax.ShapeDtypeStruct((t, D_MODEL), BF16), jax.ShapeDtypeStruct((t, D_MODEL), BF16),
         jax.ShapeDtypeStruct((t, REST_WIDTH), BF16), jax.ShapeDtypeStruct((SUBLANES, D_MODEL), F32)],
        [tok, tok, pl.BlockSpec((tm, REST_WIDTH), lambda i: (rev(i), 0)), _full((SUBLANES, D_MODEL))],
        scratch=[pltpu.VMEM((SUBLANES, D_MODEL), F32)], rider=rider)


def _attn_bwd(z, dattn, attn, lse, sinks, rider):
    t = z.shape[0]
    tq = min(TOKEN_TILE, t)
    nblk = tq // WINDOW
    nt = t // tq

    def body(q_ref, kv_ref, do_ref, o_ref, lse_ref, sink_ref, dq_ref, dkv_ref, ds_ref, acc_ref, bias_ref):
        i = pl.program_id(0)

        @pl.when(i == 0)
        def _():
            acc_ref[...] = jnp.zeros_like(acc_ref)
            ds_ref[...] = jnp.zeros_like(ds_ref)
            _fill_window_bias(bias_ref)

        lane = lax.broadcasted_iota(jnp.int32, (1, LANES), 1)
        ind_row = lax.broadcasted_iota(jnp.int32, (SUBLANES, LANES), 0)
        ind_low = lax.broadcasted_iota(jnp.int32, (SUBLANES, LANES), 1) < HEAD_DIM
        indicator = jnp.where(jnp.logical_or(jnp.logical_and(ind_row == 0, ind_low),
                                             jnp.logical_and(ind_row == 1, jnp.logical_not(ind_low))),
                              1.0, 0.0).astype(BF16)
        low = lax.broadcasted_iota(jnp.int32, (2 * WINDOW, LANES), 1) < HEAD_DIM

        def both_heads(even, odd):
            picked = jnp.where(low, even, odd)
            return picked + jnp.concatenate([picked[:, HEAD_DIM:], picked[:, :HEAD_DIM]], axis=1)

        def window(b):
            row0 = pl.multiple_of(b * WINDOW, WINDOW)
            start = i * tq + b * WINDOW
            prev = pl.multiple_of(jnp.maximum(start - WINDOW, 0), WINDOW)
            cur = pl.multiple_of(start, WINDOW)
            kvw = jnp.concatenate([kv_ref[pl.ds(prev, WINDOW), :], kv_ref[pl.ds(cur, WINDOW), :]], axis=0)
            return (row0, prev, cur, _half_tiles(kvw[:, :KV_WIDTH]), _half_tiles(kvw[:, KV_WIDTH:]),
                    bias_ref[jnp.minimum(start, 1)])

        def block_group(bb, dsink):
            windows = [window(bb * ATTN_BWD_BLOCKS + n) for n in range(ATTN_BWD_BLOCKS)]
            dk_groups = [[] for _ in windows]
            dv_groups = [[] for _ in windows]
            for j in range(N_KV_HEADS):
                stacks, deltas, dq_ts = [], [], []
                for row0, _, _, _, _, _ in windows:
                    qst = _stack_pairs(q_ref, row0, j)
                    dost = _stack_pairs(do_ref, row0, j)
                    prod = dost.astype(F32) * _stack_pairs(o_ref, row0, j).astype(F32)
                    prod_hi = prod.astype(BF16)
                    prod_lo = (prod - prod_hi.astype(F32)).astype(BF16)
                    stacks.append((qst, dost))
                    deltas.append(lax.dot_general(indicator, prod_hi, NT_DIMS, preferred_element_type=F32)
                                  + lax.dot_general(indicator, prod_lo, NT_DIMS, preferred_element_type=F32))
                    dq_ts.append(jnp.zeros((LANES, STACK), F32))
                dk_par = [[] for _ in windows]
                dv_par = [[] for _ in windows]
                for parity in range(2):
                    heads = [j * GROUP + 2 * p + parity for p in range(PAIRS)]
                    sink = _per_pair_row([sink_ref[h] * LOG2E for h in heads])
                    for n, (row0, _, _, k_halves, v_halves, bias) in enumerate(windows):
                        qst, dost = stacks[n]
                        kk, vv = k_halves[j][parity], v_halves[j][parity]
                        s = lax.dot_general(kk, qst, NT_DIMS, preferred_element_type=F32) * SCORE_SCALE + bias
                        lse = jnp.concatenate([lse_ref[h:h + 1, pl.ds(row0, WINDOW)] for h in heads], axis=1)
                        p = jnp.exp2(s - lse)
                        dp = lax.dot_general(vv, dost, NT_DIMS, preferred_element_type=F32)
                        delta = deltas[n][parity:parity + 1, :]
                        dsb = (p * (dp - delta)).astype(BF16)
                        dq_ts[n] = dq_ts[n] + lax.dot_general(kk, dsb, TN_DIMS, preferred_element_type=F32)
                        dk_par[n].append(jnp.dot(dsb, qst, preferred_element_type=F32))
                        dv_par[n].append(jnp.dot(p.astype(BF16), dost, preferred_element_type=F32))
                        weighted = jnp.exp2(sink - lse) * delta
                        for pr, h in enumerate(heads):
                            dsink = dsink - jnp.where(
                                lane == h, jnp.sum(weighted[:, pr * WINDOW:(pr + 1) * WINDOW]), 0.0)
                for n, (row0, _, _, _, _, _) in enumerate(windows):
                    dq_st = jnp.transpose((dq_ts[n] * ATTN_SCALE).astype(BF16))
                    for pr in range(PAIRS):
                        dq_ref[pl.ds(row0, WINDOW), (j * PAIRS + pr) * LANES:(j * PAIRS + pr + 1) * LANES] = (
                            dq_st[pr * WINDOW:(pr + 1) * WINDOW, :])
                    dk_groups[n].append(both_heads(dk_par[n][0], dk_par[n][1]))
                    dv_groups[n].append(both_heads(dv_par[n][0], dv_par[n][1]))
            for n, (_, prev, cur, _, _, _) in enumerate(windows):
                blk = jnp.concatenate([jnp.where(low, dk_groups[n][0], dk_groups[n][1]) * ATTN_SCALE,
                                       jnp.where(low, dv_groups[n][0], dv_groups[n][1])], axis=1)
                acc_ref[pl.ds(prev, WINDOW), :] += blk[:WINDOW, :]
                acc_ref[pl.ds(cur, WINDOW), :] += blk[WINDOW:, :]
            return dsink

        dsink = lax.fori_loop(0, nblk // ATTN_BWD_BLOCKS, block_group, jnp.zeros((1, LANES), F32))
        ds_ref[0:1, :] += dsink

        @pl.when(i == nt - 1)
        def _():
            dkv_ref[...] = acc_ref[...].astype(BF16)

    tok = pl.BlockSpec((tq, D_MODEL), lambda i: (i, 0))
    return _call(
        body, "attn_bwd", (nt,), [z, z, dattn, attn, lse, sinks],
        [tok, pl.BlockSpec((t, 2 * KV_WIDTH), lambda i: (0, KV_COL // (2 * KV_WIDTH))), tok, tok,
         pl.BlockSpec((N_Q_HEADS, tq), lambda i: (0, i)), pl.BlockSpec(memory_space=pltpu.SMEM)],
        [jax.ShapeDtypeStruct((t, D_MODEL), BF16), jax.ShapeDtypeStruct((t, 2 * KV_WIDTH), BF16),
         jax.ShapeDtypeStruct((SUBLANES, LANES), F32)],
        [tok, _full((t, 2 * KV_WIDTH)), _full((SUBLANES, LANES))],
        scratch=[pltpu.VMEM((t, 2 * KV_WIDTH), F32), pltpu.VMEM((2, 2 * WINDOW, STACK), F32)], rider=rider)


def _inproj_bwd(dq, drest, dkv, x, dx2, vec, w_t, rider):
    t = x.shape[0]
    tm = min(TOKEN_TILE, t)

    def body(dq_ref, dr_ref, dkv_ref, x_ref, dx2_ref, vec_ref, w_ref, gx_ref, acc_ref, db_ref):
        @pl.when(pl.program_id(0) == 0)
        def _():
            acc_ref[...] = jnp.zeros_like(acc_ref)
            db_ref[...] = jnp.zeros_like(db_ref)

        g = vec_ref[0:1, :]
        sc1 = vec_ref[1:2, :]
        dqb, drb, dkvb = dq_ref[...], dr_ref[...], dkv_ref[...]
        dh = jnp.dot(dqb, w_ref[:REF_KV_COL, :], preferred_element_type=F32)
        dh = dh + jnp.dot(drb, w_ref[REF_REST_COL:, :], preferred_element_type=F32)
        dh = dh + jnp.dot(dkvb, w_ref[REF_KV_COL:REF_REST_COL, :], preferred_element_type=F32)
        db_ref[:, :REF_KV_COL] += jnp.sum(dqb.astype(F32), axis=0, keepdims=True)
        db_ref[:, REF_REST_COL:] += jnp.sum(drb.astype(F32), axis=0, keepdims=True)
        db_ref[:, REF_KV_COL:REF_REST_COL] += jnp.sum(dkvb.astype(F32), axis=0, keepdims=True)
        xf = x_ref[...]
        r = lax.rsqrt(jnp.mean(xf * xf, axis=-1, keepdims=True) + EPS)
        xn = xf * r
        acc_ref[0:1, :] += jnp.sum(dh, axis=0, keepdims=True)
        acc_ref[1:2, :] += jnp.sum(dh * xn * g, axis=0, keepdims=True)
        acc_ref[2:3, :] += jnp.sum(dh * xn * (1.0 + sc1), axis=0, keepdims=True)
        dxn = dh * g * (1.0 + sc1)
        gx_ref[...] = dx2_ref[...].astype(F32) + r * (dxn - xn * jnp.mean(dxn * xn, axis=-1, keepdims=True))

    tok = pl.BlockSpec((tm, D_MODEL), lambda i: (i, 0))
    return _call(
        body, "inproj_bwd", (t // tm,), [dq, drest, dkv, x, dx2, vec, w_t],
        [tok, pl.BlockSpec((tm, REST_WIDTH), lambda i: (i, 0)),
         pl.BlockSpec((tm, 2 * KV_WIDTH), lambda i: (i, 0)), tok, tok,
         _full((SUBLANES, D_MODEL)), _full((IN_WIDTH, D_MODEL))],
        [jax.ShapeDtypeStruct((t, D_MODEL), F32), jax.ShapeDtypeStruct((SUBLANES, D_MODEL), F32),
         jax.ShapeDtypeStruct((1, IN_WIDTH), F32)],
        [tok, _full((SUBLANES, D_MODEL)), _full((1, IN_WIDTH))], rider=rider)


def _weight_grad(b, a, name, bn, rows=None, row0=0, into=None, rider=None):
    t, n = b.shape
    m = a.shape[1]
    rows = n if rows is None else rows
    tk = min(TOKEN_TILE, t)
    for cand in (4 * TOKEN_TILE, 2 * TOKEN_TILE):
        if t % cand == 0 and 2 * cand * (bn + m) * 2 + bn * m * 4 <= WGRAD_VMEM:
            tk = cand
            break
    nk = t // tk
    block0 = row0 // bn

    def body(b_ref, a_ref, *rest):
        out_ref, acc_ref = rest[-2:]
        k = pl.program_id(1)

        @pl.when(k == 0)
        def _():
            acc_ref[...] = jnp.zeros_like(acc_ref)

        acc_ref[...] += lax.dot_general(b_ref[...], a_ref[...], TN_DIMS, preferred_element_type=F32)

        @pl.when(k == nk - 1)
        def _():
            out_ref[...] = acc_ref[...].astype(BF16)

    outs, routs = _call(
        body, name, (n // bn, nk), [b, a] + ([] if into is None else [into]),
        [pl.BlockSpec((tk, bn), lambda j, k: (k, j)), pl.BlockSpec((tk, m), lambda j, k: (k, 0))]
        + ([] if into is None else [ANY]),
        [jax.ShapeDtypeStruct((rows, m), BF16)], [pl.BlockSpec((bn, m), lambda j, k: (block0 + j, 0))],
        scratch=[pltpu.VMEM((bn, m), F32)], rider=rider, aliases=None if into is None else {2: 0})
    return outs[0], routs


def _to_rows(v):
    n = v.shape[0]
    padded = -(-n // (SUBLANES * LANES)) * SUBLANES * LANES
    return jnp.pad(v, (0, padded - n)).reshape(padded // LANES, LANES)


def _vec_rows(*rows):
    stacked = jnp.concatenate([r.reshape(1, D_MODEL) for r in rows], axis=0)
    return jnp.pad(stacked, ((0, SUBLANES - len(rows)), (0, 0)))


def kernel(x, c, w_ada, b_ada, g_mix, w_in, b_in, sinks, conv_w, w_out, g_ffn, w_ffn_in, w_ffn_out, g_final, loss_target, m_w_ada, m_b_ada, m_g_mix, m_w_in, m_b_in, m_sinks, m_conv_w, m_w_out, m_g_ffn, m_w_ffn_in, m_w_ffn_out, m_g_final, v_w_ada, v_b_ada, v_g_mix, v_w_in, v_b_in, v_sinks, v_conv_w, v_w_out, v_g_ffn, v_w_ffn_in, v_w_ffn_out, v_g_final):
    ix, iy, ic = _my_place()
    me = 4 * ix + 2 * iy + ic
    xs = x[0]
    target = loss_target[0]
    ada_cols = w_ada.shape[2]
    conv_cols = conv_w.shape[2]

    wt_in, wt_fi = jnp.transpose(w_in[0]), jnp.transpose(w_ffn_in[0])
    b_cols = lax.dynamic_slice_in_dim(b_ada, me * ada_cols, ada_cols, axis=1)
    g_in, (cast_fi, cast_out, cast_fo), first, mod_all = _gather_first_weight(
        wt_in, [wt_fi, w_out[0], w_ffn_out[0]], _to_rows(jnp.concatenate([c[0], conv_w[0].reshape(-1)])),
        w_ada[0], b_cols)
    first = first.reshape(N_DEV, -1)
    c_all = first[:, :D_MODEL]
    conv_full = jnp.transpose(first[:, D_MODEL:D_MODEL + 3 * conv_cols].reshape(N_DEV, 3, conv_cols), (1, 0, 2))
    conv_full = conv_full.reshape(3, D_MODEL)
    mod = lax.dynamic_index_in_dim(mod_all, me, axis=1, keepdims=False).reshape(N_MOD, D_MODEL)
    sh1, sc1, ga1, sh2, sc2, ga2 = [mod[i:i + 1] for i in range(N_MOD)]
    w_in_t = g_in.reshape(IN_WIDTH, D_MODEL)
    (z, h1), (g_fi, g_out) = _inproj_fwd(xs, _vec_rows(g_mix, sc1, sh1), w_in_t, b_in,
                                         _gather_rider([cast_fi, cast_out]))
    w_fi_t = g_fi.reshape(2 * D_FF, D_MODEL)
    w_out_full = g_out.reshape(D_MODEL, D_MODEL)
    (attn, lse), (g_fo,) = _attn_fwd(z, sinks[0], _gather_rider([cast_fo]))
    w_fo_full = g_fo.reshape(D_FF, D_MODEL)
    merged, x2, h2, oproj = _mix_fwd(
        xs, attn, z, _vec_rows(ga1, g_ffn, sc2, sh2, conv_full[0], conv_full[1], conv_full[2]), w_out_full)
    gu, act = _ffn_fwd(h2, w_fi_t)
    dx3, df, dgu, acc_l = _ffn_out_loss(act, gu, x2, target, _vec_rows(ga2, g_final), w_fo_full)

    gw_fo, _ = _weight_grad(act, df, "wgrad_ffn_out", D_FF)
    gw_fi, _ = _weight_grad(dgu, h2, "wgrad_ffn_in", D_FF)
    blocks_fo = gw_fo.reshape(N_DEV, D_FF // N_DEV, D_MODEL)
    blocks_fi = gw_fi.reshape(N_DEV, 2 * D_FF // N_DEV, D_MODEL)
    (dx2, acc_f), (sib_fo, sib_fi) = _ffn_in_bwd(dgu, x2, dx3, _vec_rows(g_ffn, sc2), w_fi_t,
                                                 _sibling_rider([blocks_fo, blocks_fi]))
    sums_fo, mine_fo = _sibling_sum(blocks_fo, sib_fo, "sibling_sum_ffn_out")
    sums_fi, mine_fi = _sibling_sum(blocks_fi, sib_fi, "sibling_sum_ffn_in")
    (dout, dattn, drest, acc_m), (ici_fo, ici_fi) = _mix_bwd(
        dx2, oproj, attn, z, _vec_rows(ga1, conv_full[0], conv_full[1], conv_full[2]), w_out_full,
        _chip_rider([sums_fo, sums_fi]))
    gw_out, _ = _weight_grad(merged, dout, "wgrad_out", D_MODEL)
    blocks_out = gw_out.reshape(N_DEV, D_MODEL // N_DEV, D_MODEL)
    (dq, dkv, dsink), (sib_out,) = _attn_bwd(z, dattn, attn, lse, sinks[0], _sibling_rider([blocks_out]))
    sums_out, mine_out = _sibling_sum(blocks_out, sib_out, "sibling_sum_out")
    gw_in, (ici_out,) = _weight_grad(drest, h1, "wgrad_in_rest", IN_CHUNK, rows=IN_WIDTH, row0=REF_REST_COL,
                                     rider=_chip_rider([sums_out]))
    gw_in, _ = _weight_grad(dq, h1, "wgrad_in_q", D_MODEL, rows=IN_WIDTH, row0=0, into=gw_in)
    gw_in, _ = _weight_grad(dkv, h1, "wgrad_in_kv", 2 * KV_WIDTH, rows=IN_WIDTH, row0=REF_KV_COL, into=gw_in)
    blocks_in = gw_in.reshape(N_DEV, IN_WIDTH // N_DEV, D_MODEL)
    (sib_in,) = _carry(_sibling_rider([blocks_in]), "sibling_w_in")
    sums_in, mine_in = _sibling_sum(blocks_in, sib_in, "sibling_sum_in")
    (grad_x, acc_i, db_in), (ici_in,) = _inproj_bwd(dq, drest, dkv, xs, dx2, _vec_rows(g_mix, sc1), w_in_t,
                                                    _chip_rider([sums_in]))

    widen = lambda vec: jnp.pad(vec, (0, -vec.shape[0] % D_MODEL))
    packed = jnp.concatenate([
        acc_i[0], acc_i[1], acc_m[0], acc_f[0], acc_f[1], acc_l[2],
        acc_i[2], widen(db_in[0]), acc_f[2], acc_l[1],
        acc_m[1], acc_m[2], acc_m[3], widen(dsink[0]), acc_l[0],
        jnp.zeros(((PACK_ROWS - PACK_SQERR - 1) * D_MODEL,), F32)]).reshape(PACK_ROWS, D_MODEL)
    packed_all = _small_allgather(packed, "gather_small")
    dmod_all = packed_all[:, PACK_DMOD:PACK_DMOD + N_MOD, :].reshape(N_DEV, N_MOD * D_MODEL)
    dmod_cols = lax.dynamic_slice_in_dim(dmod_all, me * ada_cols, ada_cols, axis=1)
    g_w_ada = _ada_weight_grad(c_all, dmod_cols)
    row_of = lambda a: a.reshape(1, -1)
    small, g_conv_full, loss = _small_finalize(packed_all, {
        "b_ada": (b_ada, m_b_ada, v_b_ada), "g_mix": (g_mix, m_g_mix, v_g_mix), "b_in": (b_in, m_b_in, v_b_in),
        "g_ffn": (g_ffn, m_g_ffn, v_g_ffn), "sinks": (sinks, m_sinks, v_sinks),
        "g_final": (row_of(g_final), row_of(m_g_final), row_of(v_g_final))})
    small["g_final"] = tuple(o.reshape(g_final.shape) for o in small["g_final"])
    g_conv = lax.dynamic_slice_in_dim(g_conv_full, me * conv_cols, conv_cols, axis=1)
    d_conv, nm_conv, nv_conv = _adamw(conv_w[0], g_conv, m_conv_w[0], v_conv_w[0], "adamw_conv_w")
    small["conv_w"] = (g_conv[None], d_conv[None], nm_conv[None], nv_conv[None])

    def reduced(mine, ici, w, m, v, name, transposed=False):
        turn = jnp.transpose if transposed else (lambda a: a)
        return tuple(turn(o)[None] for o in _chip_sum_adamw(mine, ici, turn(w[0]), turn(m[0]), turn(v[0]), name))

    d_ada, nm_ada, nv_ada = _adamw(w_ada[0], g_w_ada, m_w_ada[0], v_w_ada[0], "adamw_w_ada")
    res = {
        "w_ada": (g_w_ada[None], d_ada[None], nm_ada[None], nv_ada[None]),
        "w_in": reduced(mine_in, ici_in, w_in, m_w_in, v_w_in, "adamw_w_in", transposed=True),
        "w_out": reduced(mine_out, ici_out, w_out, m_w_out, v_w_out, "adamw_w_out"),
        "w_ffn_in": reduced(mine_fi, ici_fi, w_ffn_in, m_w_ffn_in, v_w_ffn_in, "adamw_w_ffn_in", transposed=True),
        "w_ffn_out": reduced(mine_fo, ici_fo, w_ffn_out, m_w_ffn_out, v_w_ffn_out, "adamw_w_ffn_out"),
    }
    res.update(small)
    order = ["w_ada", "b_ada", "g_mix", "w_in", "b_in", "sinks", "conv_w", "w_out", "g_ffn", "w_ffn_in", "w_ffn_out",
             "g_final"]
    outs = [loss.reshape(()), grad_x[None]]
    for k in range(4):
        outs += [res[n][k] for n in order]
    return tuple(outs)
```

```python
import functools
import math

import jax
import jax.numpy as jnp
from jax import lax
from jax.experimental import pallas as pl
from jax.experimental.pallas import tpu as pltpu

F32 = jnp.float32
BF16 = jnp.bfloat16
GRAD_STREAM = F32

D_MODEL = 1024
HEAD_DIM = 64
N_Q_HEADS = 16
N_KV_HEADS = 2
GROUP = 8
WINDOW = 128
KV_WIDTH = N_KV_HEADS * HEAD_DIM
D_FF = 2816
IN_WIDTH = 6400
N_MOD = 6
EPS = 1e-6
N_DEV = 8
REST_WIDTH = 5 * D_MODEL
KV_COL = D_MODEL + REST_WIDTH
ATTN_SCALE = HEAD_DIM ** -0.5

ADAM_LR = 0.001
ADAM_B1 = 0.9
ADAM_B2 = 0.999
ADAM_EPS = 1e-08
ADAM_WD = 0.01
ADAM_STEP = 10

LANES = 128
SUBLANES = 8
BF16_ROWS = 16
VMEM_LIMIT = 56 * 1024 * 1024
TOKEN_TILE = 512
FF_CHUNK = 256
ROW_PARTS = 2
WGRAD_VMEM = 40 * 1024 * 1024
MESH = pl.DeviceIdType.MESH
ANY = pl.BlockSpec(memory_space=pl.ANY)

NT_DIMS = (((1,), (1,)), ((), ()))
TN_DIMS = (((0,), (0,)), ((), ()))
CHIP_FLIPS = [(0, 0), (1, 0), (0, 1), (1, 1)]


def _full(shape):
    return pl.BlockSpec(shape, lambda *_: (0,) * len(shape))


def _my_place():
    return lax.axis_index("x"), lax.axis_index("y"), lax.axis_index("c")


def _flip(v, bit):
    return 1 - v if bit else v


def _sigmoid(v):
    return 1.0 / (1.0 + jnp.exp2(v * (-1.4426950408889634)))


class _Rider:
    def __init__(self, ins, out_shapes, sem_shapes, first=None, mid=None, last=None, ins_in_vmem=False):
        self.ins, self.out_shapes, self.sem_shapes = list(ins), list(out_shapes), list(sem_shapes)
        self.in_specs = [_full(a.shape) if ins_in_vmem else ANY for a in self.ins]
        self.hooks = [(when, fn) for when, fn in (("first", first), ("mid", mid), ("last", last)) if fn is not None]


def _call(body, name, grid, args, in_specs, out_shape, out_specs, scratch=(), rider=None, aliases=None):
    n_in, n_out, n_scr = len(args), len(out_shape), len(scratch)
    r_in = rider.ins if rider else []
    r_out = rider.out_shapes if rider else []
    r_sem = rider.sem_shapes if rider else []
    nsteps = math.prod(grid)

    def full_body(*refs):
        pos = 0
        groups = []
        for size in (n_in, len(r_in), n_out, len(r_out), n_scr, len(r_sem)):
            groups.append(refs[pos:pos + size])
            pos += size
        ins, rins, outs, routs, scr, rsems = groups
        step = pl.program_id(0)
        for axis in range(1, len(grid)):
            step = step * grid[axis] + pl.program_id(axis)
        at = {"first": 0, "mid": (3 * nsteps) // 4, "last": nsteps - 1}
        hooks = rider.hooks if rider else []
        for when, fn in hooks:
            if when != "last":
                pl.when(step == at[when])(functools.partial(fn, rins, routs, rsems))
        body(*ins, *outs, *scr)
        for when, fn in hooks:
            if when == "last":
                pl.when(step == at[when])(functools.partial(fn, rins, routs, rsems))

    outs = pl.pallas_call(
        full_body, name=name, grid=grid,
        out_shape=list(out_shape) + list(r_out),
        in_specs=list(in_specs) + (rider.in_specs if rider else []),
        out_specs=list(out_specs) + [ANY] * len(r_out),
        scratch_shapes=list(scratch) + list(r_sem),
        input_output_aliases=dict(aliases or {}),
        compiler_params=pltpu.CompilerParams(dimension_semantics=("arbitrary",) * len(grid),
                                             vmem_limit_bytes=VMEM_LIMIT),
    )(*args, *r_in)
    return list(outs[:n_out]), list(outs[n_out:])


def _gather_rider(shards):
    n = len(shards)

    def setup(outs, sems):
        x, y, c = _my_place()
        send_sems, recv_sems, _ = sems
        chips = [(1 - x, y), (x, 1 - y), (1 - x, 1 - y)]

        def block(w, place):
            return outs[w].at[4 * place[0] + 2 * place[1] + place[2]]

        def copy(w, k, place, to, src=None):
            return pltpu.make_async_remote_copy(
                src_ref=block(w, place) if src is None else src, dst_ref=block(w, place),
                send_sem=send_sems.at[w, k], recv_sem=recv_sems.at[w, k], device_id=to, device_id_type=MESH)

        return (x, y, c), (x, y, 1 - c), chips, block, copy

    def first(ins, outs, sems):
        me, sibling, chips, block, copy = setup(outs, sems)
        for w in range(n):
            pltpu.make_async_copy(ins[w], block(w, me), sems[2].at[w]).start()
            copy(w, 0, me, sibling, src=ins[w]).start()
            for j, chip in enumerate(chips):
                copy(w, 1 + j, me, (*chip, me[2]), src=ins[w]).start()

    def mid(ins, outs, sems):
        me, sibling, chips, block, copy = setup(outs, sems)
        for w in range(n):
            for j, chip in enumerate(chips):
                copy(w, 1 + j, (*chip, me[2]), me).wait_recv()
                copy(w, 4 + j, (*chip, me[2]), sibling).start()

    def last(ins, outs, sems):
        me, sibling, chips, block, copy = setup(outs, sems)
        for w in range(n):
            copy(w, 0, sibling, me).wait_recv()
            for j, chip in enumerate(chips):
                copy(w, 4 + j, (*chip, 1 - me[2]), me).wait_recv()
            copy(w, 0, me, sibling, src=ins[w]).wait_send()
            for j, chip in enumerate(chips):
                copy(w, 1 + j, me, (*chip, me[2]), src=ins[w]).wait_send()
                copy(w, 4 + j, (*chip, me[2]), sibling).wait_send()
            pltpu.make_async_copy(ins[w], block(w, me), sems[2].at[w]).wait()

    return _Rider(
        shards, [jax.ShapeDtypeStruct((N_DEV,) + s.shape, BF16) for s in shards],
        [pltpu.SemaphoreType.DMA((n, N_DEV - 1)), pltpu.SemaphoreType.DMA((n, N_DEV - 1)),
         pltpu.SemaphoreType.DMA((n,))],
        first=first, mid=mid, last=last, ins_in_vmem=True)


def _sibling_rider(gblocks):
    n = len(gblocks)

    def copies(ins, outs, sems):
        x, y, c = _my_place()
        send_sems, recv_sems = sems
        made = []
        for w in range(n):
            for f, (fx, fy) in enumerate(CHIP_FLIPS):
                chip = 4 * _flip(x, fx) + 2 * _flip(y, fy)
                made.append(pltpu.make_async_remote_copy(
                    src_ref=ins[w].at[chip + 1 - c], dst_ref=outs[w].at[f], send_sem=send_sems.at[w, f],
                    recv_sem=recv_sems.at[w, f], device_id=(x, y, 1 - c), device_id_type=MESH))
        return made

    def first(ins, outs, sems):
        for cp in copies(ins, outs, sems):
            cp.start()

    def last(ins, outs, sems):
        for cp in copies(ins, outs, sems):
            cp.wait_recv()
            cp.wait_send()

    return _Rider(gblocks, [jax.ShapeDtypeStruct((4,) + g.shape[1:], BF16) for g in gblocks],
                  [pltpu.SemaphoreType.DMA((n, 4))] * 2, first=first, last=last)


def _chip_rider(sums):
    n = len(sums)

    def copies(ins, outs, sems):
        x, y, c = _my_place()
        send_sems, recv_sems = sems
        made = []
        for w in range(n):
            for f in (1, 2, 3):
                fx, fy = CHIP_FLIPS[f]
                made.append(pltpu.make_async_remote_copy(
                    src_ref=ins[w].at[f - 1], dst_ref=outs[w].at[f - 1], send_sem=send_sems.at[w, f - 1],
                    recv_sem=recv_sems.at[w, f - 1], device_id=(_flip(x, fx), _flip(y, fy), c), device_id_type=MESH))
        return made

    def first(ins, outs, sems):
        for cp in copies(ins, outs, sems):
            cp.start()

    def last(ins, outs, sems):
        for cp in copies(ins, outs, sems):
            cp.wait_recv()
            cp.wait_send()

    return _Rider(sums, [jax.ShapeDtypeStruct(s.shape, BF16) for s in sums],
                  [pltpu.SemaphoreType.DMA((n, 3))] * 2, first=first, last=last)


def _push_to_all(v_ref, out_ref, send_sems, recv_sems, local_sem, wait=True):
    x, y, c = _my_place()
    me = 4 * x + 2 * y + c
    mine = pltpu.make_async_copy(v_ref, out_ref.at[me], local_sem)
    mine.start()
    sends = []
    for k in range(1, N_DEV):
        px, py, pc = _flip(x, k & 4), _flip(y, k & 2), _flip(c, k & 1)
        cp = pltpu.make_async_remote_copy(
            src_ref=v_ref, dst_ref=out_ref.at[me], send_sem=send_sems.at[k - 1], recv_sem=recv_sems.at[k - 1],
            device_id=(px, py, pc), device_id_type=MESH)
        cp.start()
        sends.append(cp)

    def finish():
        for k in range(1, N_DEV):
            px, py, pc = _flip(x, k & 4), _flip(y, k & 2), _flip(c, k & 1)
            pltpu.make_async_remote_copy(
                src_ref=v_ref, dst_ref=out_ref.at[4 * px + 2 * py + pc], send_sem=send_sems.at[k - 1],
                recv_sem=recv_sems.at[k - 1], device_id=(px, py, pc), device_id_type=MESH).wait_recv()
        for cp in sends:
            cp.wait_send()
        mine.wait()

    if wait:
        finish()
    return finish


def _small_allgather(v, name):
    def body(v_ref, out_ref, send_sems, recv_sems, local_sem):
        _push_to_all(v_ref, out_ref, send_sems, recv_sems, local_sem)

    return pl.pallas_call(
        body, name=name,
        out_shape=jax.ShapeDtypeStruct((N_DEV,) + v.shape, F32),
        in_specs=[pl.BlockSpec(memory_space=pltpu.VMEM)],
        out_specs=pl.BlockSpec(memory_space=pltpu.VMEM),
        scratch_shapes=[pltpu.SemaphoreType.DMA((N_DEV - 1,)), pltpu.SemaphoreType.DMA((N_DEV - 1,)),
                        pltpu.SemaphoreType.DMA],
        compiler_params=pltpu.CompilerParams(vmem_limit_bytes=VMEM_LIMIT),
    )(v)


def _gather_first_weight(shard, others, cond_rows, w_ada, b_cols):
    n = len(others)
    ada_cols = w_ada.shape[1]
    c_rows = D_MODEL // LANES

    def body(*refs):
        w_ref, other_refs = refs[0], refs[1:1 + n]
        cond_ref, wada_ref, bcols_ref = refs[1 + n:4 + n]
        out_ref, cast_refs = refs[4 + n], refs[5 + n:5 + 2 * n]
        cond_all_ref, mod_all_ref = refs[5 + 2 * n:7 + 2 * n]
        mine_ref, mod_ref, send_sems, recv_sems, local_sem, small_send, small_recv, small_local = refs[7 + 2 * n:]
        x, y, c = _my_place()
        me, sibling = (x, y, c), (x, y, 1 - c)
        xnb, ynb, diag = (1 - x, y), (x, 1 - y), (1 - x, 1 - y)
        half = shard.shape[0] // 2

        def block(place, part=None):
            ref = out_ref.at[4 * place[0] + 2 * place[1] + place[2]]
            return ref if part is None else ref.at[pl.ds(part * half, half)]

        def copy(k, place, to, part=None, src=None):
            return pltpu.make_async_remote_copy(
                src_ref=block(place, part) if src is None else src, dst_ref=block(place, part),
                send_sem=send_sems.at[k], recv_sem=recv_sems.at[k], device_id=to, device_id_type=MESH)

        finish_cond = _push_to_all(cond_ref, cond_all_ref, small_send.at[0], small_recv.at[0], small_local.at[0],
                                   wait=False)
        mine_ref[...] = w_ref[...].astype(BF16)
        finish_cond()
        local = pltpu.make_async_copy(mine_ref, block(me), local_sem)
        local.start()
        started = [copy(0, me, sibling, src=mine_ref), copy(1, me, (*xnb, c), src=mine_ref),
                   copy(2, me, (*ynb, c), src=mine_ref)]
        for cp in started:
            cp.start()
        mod = jnp.zeros((N_DEV, ada_cols), F32) + bcols_ref[...]
        for r in range(c_rows):
            cf = cond_all_ref[:, r, :]
            act = (cf * _sigmoid(cf)).astype(BF16)
            mod = mod + jnp.dot(act, wada_ref[r * LANES:(r + 1) * LANES, :].astype(BF16),
                                preferred_element_type=F32)
        mod_ref[...] = mod
        finish_mod = _push_to_all(mod_ref, mod_all_ref, small_send.at[1], small_recv.at[1], small_local.at[1],
                                  wait=False)
        for o_ref, c_ref in zip(other_refs, cast_refs):
            c_ref[...] = o_ref[...].astype(BF16)
        def start(cp):
            cp.start()
            started.append(cp)

        copy(1, (*xnb, c), me).wait_recv()
        start(copy(3, (*xnb, c), (*ynb, c), part=0))
        start(copy(5, (*xnb, c), sibling))
        copy(2, (*ynb, c), me).wait_recv()
        start(copy(4, (*ynb, c), (*xnb, c), part=1))
        start(copy(6, (*ynb, c), sibling))
        copy(3, (*diag, c), me, part=0).wait_recv()
        start(copy(7, (*diag, c), sibling, part=0))
        copy(4, (*diag, c), me, part=1).wait_recv()
        start(copy(8, (*diag, c), sibling, part=1))
        copy(0, sibling, me).wait_recv()
        copy(5, (*xnb, 1 - c), me).wait_recv()
        copy(6, (*ynb, 1 - c), me).wait_recv()
        copy(7, (*diag, 1 - c), me, part=0).wait_recv()
        copy(8, (*diag, 1 - c), me, part=1).wait_recv()
        finish_mod()
        for cp in started:
            cp.wait_send()
        local.wait()

    vmem = pl.BlockSpec(memory_space=pltpu.VMEM)
    outs = pl.pallas_call(
        body, name="gather_w_in",
        out_shape=[jax.ShapeDtypeStruct((N_DEV,) + shard.shape, BF16)]
        + [jax.ShapeDtypeStruct(o.shape, BF16) for o in others]
        + [jax.ShapeDtypeStruct((N_DEV,) + cond_rows.shape, F32), jax.ShapeDtypeStruct((N_DEV, N_DEV, ada_cols), F32)],
        in_specs=[vmem] * (4 + n),
        out_specs=[ANY] + [vmem] * (n + 2),
        scratch_shapes=[pltpu.VMEM(shard.shape, BF16), pltpu.VMEM((N_DEV, ada_cols), F32),
                        pltpu.SemaphoreType.DMA((9,)), pltpu.SemaphoreType.DMA((9,)),
                        pltpu.SemaphoreType.DMA,
                        pltpu.SemaphoreType.DMA((2, N_DEV - 1)), pltpu.SemaphoreType.DMA((2, N_DEV - 1)),
                        pltpu.SemaphoreType.DMA((2,))],
        compiler_params=pltpu.CompilerParams(vmem_limit_bytes=VMEM_LIMIT),
    )(shard, *others, cond_rows, w_ada, b_cols)
    return outs[0], list(outs[1:1 + n]), outs[1 + n], outs[2 + n]


def _carry(rider, name):
    def body(token_ref):
        token_ref[...] = jnp.zeros_like(token_ref)

    _, routs = _call(body, name, (1,), [], [], [jax.ShapeDtypeStruct((SUBLANES, LANES), F32)],
                     [_full((SUBLANES, LANES))], rider=rider)
    return routs


def _ada_weight_grad(c_all, dmod_cols):
    cols = dmod_cols.shape[1]

    def body(c_ref, d_ref, out_ref):
        cf = c_ref[...]
        act = (cf * _sigmoid(cf)).astype(BF16)
        out_ref[...] = lax.dot_general(act, d_ref[...].astype(BF16), TN_DIMS, preferred_element_type=F32)

    return pl.pallas_call(
        body, name="ada_weight_grad",
        out_shape=jax.ShapeDtypeStruct((D_MODEL, cols), F32),
        in_specs=[pl.BlockSpec(memory_space=pltpu.VMEM)] * 2,
        out_specs=pl.BlockSpec(memory_space=pltpu.VMEM),
        compiler_params=pltpu.CompilerParams(vmem_limit_bytes=VMEM_LIMIT),
    )(c_all, dmod_cols)


PACK_ROWS = 24
PACK_DMOD = 0
PACK_PARAMS = {"g_mix": (6, D_MODEL), "b_in": (7, IN_WIDTH), "g_ffn": (14, D_MODEL), "g_final": (15, D_MODEL),
               "sinks": (19, N_Q_HEADS)}
PACK_CONV = 16
PACK_SQERR = 20


def _small_finalize(packed_all, params):
    names = ["b_ada"] + list(PACK_PARAMS)
    layout = dict(PACK_PARAMS, b_ada=(PACK_DMOD, N_MOD * D_MODEL))
    n = len(names)

    def body(*refs):
        p_ref = refs[0]
        ins = refs[1:1 + 3 * n]
        outs = refs[1 + 3 * n:1 + 7 * n]
        conv_ref, loss_ref = refs[1 + 7 * n:]
        total = p_ref[0]
        for d in range(1, N_DEV):
            total = total + p_ref[d]
        for k, name in enumerate(names):
            row0, width = layout[name]
            w_ref, m_ref, v_ref = ins[3 * k:3 * k + 3]
            g_ref, d_ref, nm_ref, nv_ref = outs[4 * k:4 * k + 4]
            for chunk in range(-(-width // D_MODEL)):
                lo = chunk * D_MODEL
                hi = min(lo + D_MODEL, width)
                g = total[row0 + chunk:row0 + chunk + 1, :hi - lo]
                g_ref[:, lo:hi] = g
                d_ref[:, lo:hi], nm_ref[:, lo:hi], nv_ref[:, lo:hi] = _adamw_update(
                    w_ref[:, lo:hi], g, m_ref[:, lo:hi], v_ref[:, lo:hi])
        conv_ref[...] = total[PACK_CONV:PACK_CONV + 3, :]
        loss_ref[...] = (0.5 / D_MODEL) * jnp.sum(total[PACK_SQERR:PACK_SQERR + 1, :], keepdims=True)

    vmem = pl.BlockSpec(memory_space=pltpu.VMEM)
    flat = [a for name in names for a in params[name]]
    out_shape = [jax.ShapeDtypeStruct(params[name][0].shape, F32) for name in names for _ in range(4)]
    outs = pl.pallas_call(
        body, name="small_finalize",
        out_shape=out_shape + [jax.ShapeDtypeStruct((3, D_MODEL), F32), jax.ShapeDtypeStruct((1, 1), F32)],
        in_specs=[vmem] * (1 + 3 * n),
        out_specs=[vmem] * (4 * n + 2),
        compiler_params=pltpu.CompilerParams(vmem_limit_bytes=VMEM_LIMIT),
    )(packed_all, *flat)
    return {name: tuple(outs[4 * k:4 * k + 4]) for k, name in enumerate(names)}, outs[4 * n], outs[4 * n + 1]


def _row_tile(rows, multiple):
    for cand in range(min(rows, 256), 0, -1):
        if rows % cand == 0 and cand % multiple == 0:
            return cand
    return rows


def _adamw_update(w, g, m, v):
    c1 = 1.0 / (1.0 - ADAM_B1 ** ADAM_STEP)
    c2 = 1.0 / (1.0 - ADAM_B2 ** ADAM_STEP)
    nm = ADAM_B1 * m + (1.0 - ADAM_B1) * g
    nv = ADAM_B2 * v + (1.0 - ADAM_B2) * (g * g)
    delta = -ADAM_LR * ((nm * c1) / (jnp.sqrt(nv * c2) + ADAM_EPS) + ADAM_WD * w)
    return delta, nm, nv


def _adamw(w, g, m, v, name):
    rows, cols = w.shape
    tile = _row_tile(rows, SUBLANES)

    def body(w_ref, g_ref, m_ref, v_ref, d_ref, nm_ref, nv_ref):
        d_ref[...], nm_ref[...], nv_ref[...] = _adamw_update(w_ref[...], g_ref[...], m_ref[...], v_ref[...])

    spec = pl.BlockSpec((tile, cols), lambda i: (i, 0))
    outs, _ = _call(body, name, (rows // tile,), [w, g, m, v], [spec] * 4,
                    [jax.ShapeDtypeStruct((rows, cols), F32)] * 3, [spec] * 3)
    return outs


def _sibling_sum(gblocks, sib, name):
    _, r, cdim = gblocks.shape
    tile = _row_tile(r, BF16_ROWS)
    x, y, c = _my_place()
    table = jnp.stack([4 * _flip(x, fx) + 2 * _flip(y, fy) + c for fx, fy in CHIP_FLIPS]).astype(jnp.int32)

    def body(table_ref, own0, own1, own2, own3, sib_ref, sums_ref, mine_ref):
        mine_ref[...] = own0[...].astype(F32) + sib_ref[0].astype(F32)
        for f, own in ((1, own1), (2, own2), (3, own3)):
            sums_ref[f - 1] = (own[...].astype(F32) + sib_ref[f].astype(F32)).astype(BF16)

    own_specs = [pl.BlockSpec((None, tile, cdim), functools.partial(lambda i, tab, f: (tab[f], i, 0), f=f))
                 for f in range(4)]
    return pl.pallas_call(
        body, name=name,
        grid_spec=pltpu.PrefetchScalarGridSpec(
            num_scalar_prefetch=1, grid=(r // tile,),
            in_specs=own_specs + [pl.BlockSpec((4, tile, cdim), lambda i, tab: (0, i, 0))],
            out_specs=[pl.BlockSpec((3, tile, cdim), lambda i, tab: (0, i, 0)),
                       pl.BlockSpec((tile, cdim), lambda i, tab: (i, 0))]),
        out_shape=[jax.ShapeDtypeStruct((3, r, cdim), BF16), jax.ShapeDtypeStruct((r, cdim), F32)],
        compiler_params=pltpu.CompilerParams(dimension_semantics=("arbitrary",), vmem_limit_bytes=VMEM_LIMIT),
    )(table, gblocks, gblocks, gblocks, gblocks, sib)


def _chip_sum_adamw(mine, ici, w, m, v, name):
    r, cdim = mine.shape
    tile = _row_tile(r, BF16_ROWS)

    def body(mine_ref, ici_ref, w_ref, m_ref, v_ref, g_ref, d_ref, nm_ref, nv_ref):
        g = mine_ref[...]
        for f in range(3):
            g = g + ici_ref[f].astype(F32)
        g_ref[...] = g
        d_ref[...], nm_ref[...], nv_ref[...] = _adamw_update(w_ref[...], g, m_ref[...], v_ref[...])

    spec = pl.BlockSpec((tile, cdim), lambda i: (i, 0))
    outs, _ = _call(
        body, name, (r // tile,), [mine, ici, w, m, v],
        [spec, pl.BlockSpec((3, tile, cdim), lambda i: (0, i, 0)), spec, spec, spec],
        [jax.ShapeDtypeStruct((r, cdim), F32)] * 4, [spec] * 4)
    return outs


REF_KV_COL = D_MODEL
REF_REST_COL = D_MODEL + 2 * KV_WIDTH
IN_CHUNK = 1280
IN_PIECES = ([(0, 0, D_MODEL)]
             + [(D_MODEL + n * IN_CHUNK, REF_REST_COL + n * IN_CHUNK, IN_CHUNK) for n in range(REST_WIDTH // IN_CHUNK)]
             + [(KV_COL, REF_KV_COL, 2 * KV_WIDTH)])


def _inproj_fwd(x, vec, w_t, b_in, rider):
    t = x.shape[0]
    tm = min(TOKEN_TILE, t)

    def body(x_ref, vec_ref, w_ref, b_ref, z_ref, h_ref):
        xf = x_ref[...]
        r = lax.rsqrt(jnp.mean(xf * xf, axis=-1, keepdims=True) + EPS)
        h = (xf * r) * vec_ref[0:1, :] * (1.0 + vec_ref[1:2, :]) + vec_ref[2:3, :]
        hb = h.astype(BF16)
        h_ref[...] = hb
        for mine, ref, width in IN_PIECES:
            zc = lax.dot_general(hb, w_ref[ref:ref + width, :], NT_DIMS, preferred_element_type=F32)
            z_ref[:, mine:mine + width] = (zc + b_ref[:, ref:ref + width]).astype(BF16)

    return _call(
        body, "inproj_fwd", (t // tm,), [x, vec, w_t, b_in],
        [pl.BlockSpec((tm, D_MODEL), lambda i: (i, 0)), _full((SUBLANES, D_MODEL)),
         _full((IN_WIDTH, D_MODEL)), _full((1, IN_WIDTH))],
        [jax.ShapeDtypeStruct((t, IN_WIDTH), BF16), jax.ShapeDtypeStruct((t, D_MODEL), BF16)],
        [pl.BlockSpec((tm, IN_WIDTH), lambda i: (i, 0)), pl.BlockSpec((tm, D_MODEL), lambda i: (i, 0))],
        rider=rider)


def _window_mask(has_prev):
    qi = lax.broadcasted_iota(jnp.int32, (WINDOW, 2 * WINDOW), 0)
    kj = lax.broadcasted_iota(jnp.int32, (WINDOW, 2 * WINDOW), 1)
    off = jnp.where(has_prev, 0, 4 * WINDOW)
    in_prev = jnp.logical_and(kj < WINDOW, kj > qi + off)
    in_cur = jnp.logical_and(kj >= WINDOW, (kj - WINDOW) <= qi)
    return jnp.logical_or(in_prev, in_cur)


PAIRS = GROUP // 2
STACK = PAIRS * WINDOW


ATTN_BLOCKS = 4
ATTN_BWD_BLOCKS = 1
LOG2E = 1.4426950408889634
LN2 = 0.6931471805599453
SCORE_SCALE = ATTN_SCALE * LOG2E


def _fill_window_bias(bias_ref):
    shape = bias_ref.shape[1:]
    kj = lax.broadcasted_iota(jnp.int32, shape, 0)
    qi = jnp.bitwise_and(lax.broadcasted_iota(jnp.int32, shape, 1), WINDOW - 1)
    in_prev = jnp.logical_and(kj < WINDOW, kj > qi)
    in_cur = jnp.logical_and(kj >= WINDOW, (kj - WINDOW) <= qi)
    bias_ref[0] = jnp.where(in_cur, 0.0, -jnp.inf)
    bias_ref[1] = jnp.where(jnp.logical_or(in_prev, in_cur), 0.0, -jnp.inf)


def _half_tiles(tile):
    low = lax.broadcasted_iota(jnp.int32, tile.shape, 1) < HEAD_DIM
    swapped = jnp.concatenate([tile[:, HEAD_DIM:], tile[:, :HEAD_DIM]], axis=1)
    zero = jnp.zeros_like(tile)
    return ((jnp.where(low, tile, zero), jnp.where(low, zero, swapped)),
            (jnp.where(low, swapped, zero), jnp.where(low, zero, tile)))


def _stack_pairs(ref, row0, j):
    return jnp.concatenate(
        [ref[pl.ds(row0, WINDOW), (j * PAIRS + p) * LANES:(j * PAIRS + p + 1) * LANES] for p in range(PAIRS)], axis=0)


def _per_pair_row(values):
    pair = lax.broadcasted_iota(jnp.int32, (1, STACK), 1) // WINDOW
    row = jnp.full((1, STACK), values[PAIRS - 1], F32)
    for p in range(PAIRS - 2, -1, -1):
        row = jnp.where(pair == p, values[p], row)
    return row


def _attn_fwd(z, sinks, rider):
    t = z.shape[0]
    tq = min(TOKEN_TILE, t)
    nblk = tq // WINDOW

    def body(q_ref, kv_ref, sink_ref, o_ref, lse_ref, bias_ref):
        i = pl.program_id(0)

        @pl.when(i == 0)
        def _():
            _fill_window_bias(bias_ref)

        def window(b):
            row0 = pl.multiple_of(b * WINDOW, WINDOW)
            start = i * tq + b * WINDOW
            prev = pl.multiple_of(jnp.maximum(start - WINDOW, 0), WINDOW)
            cur = pl.multiple_of(start, WINDOW)
            kvw = jnp.concatenate([kv_ref[pl.ds(prev, WINDOW), :], kv_ref[pl.ds(cur, WINDOW), :]], axis=0)
            return row0, _half_tiles(kvw[:, :KV_WIDTH]), _half_tiles(kvw[:, KV_WIDTH:]), bias_ref[jnp.minimum(start, 1)]

        def block_group(bb, carry):
            windows = [window(bb * ATTN_BLOCKS + n) for n in range(ATTN_BLOCKS)]
            for j in range(N_KV_HEADS):
                for pr in range(PAIRS):
                    cols = slice((j * PAIRS + pr) * LANES, (j * PAIRS + pr + 1) * LANES)
                    o_ts = [jnp.zeros((LANES, WINDOW), F32) for _ in windows]
                    for parity in range(2):
                        h = j * GROUP + 2 * pr + parity
                        sink = sink_ref[h] * LOG2E
                        for n, (row0, k_halves, v_halves, bias) in enumerate(windows):
                            qp = q_ref[pl.ds(row0, WINDOW), cols]
                            s = lax.dot_general(k_halves[j][parity], qp, NT_DIMS, preferred_element_type=F32)
                            s = s * SCORE_SCALE + bias
                            m = jnp.maximum(jnp.max(s, axis=0, keepdims=True), sink)
                            p = jnp.exp2(s - m)
                            denom = jnp.sum(p, axis=0, keepdims=True) + jnp.exp2(sink - m)
                            pv = lax.dot_general(v_halves[j][parity], p.astype(BF16), TN_DIMS,
                                                 preferred_element_type=F32)
                            o_ts[n] = o_ts[n] + pv * (1.0 / denom)
                            lse_ref[h:h + 1, pl.ds(row0, WINDOW)] = m + jnp.log2(denom)
                    for n, (row0, _, _, _) in enumerate(windows):
                        o_ref[pl.ds(row0, WINDOW), cols] = jnp.transpose(o_ts[n].astype(BF16))
            return carry

        lax.fori_loop(0, nblk // ATTN_BLOCKS, block_group, 0)

    return _call(
        body, "attn_fwd", (t // tq,), [z, z, sinks],
        [pl.BlockSpec((tq, D_MODEL), lambda i: (i, 0)),
         pl.BlockSpec((t, 2 * KV_WIDTH), lambda i: (0, KV_COL // (2 * KV_WIDTH))),
         pl.BlockSpec(memory_space=pltpu.SMEM)],
        [jax.ShapeDtypeStruct((t, D_MODEL), BF16), jax.ShapeDtypeStruct((N_Q_HEADS, t), F32)],
        [pl.BlockSpec((tq, D_MODEL), lambda i: (i, 0)), pl.BlockSpec((N_Q_HEADS, tq), lambda i: (0, i))],
        scratch=[pltpu.VMEM((2, 2 * WINDOW, WINDOW), F32)], rider=rider)


HALO = BF16_ROWS


def _shift_down(u, uh, k):
    row = lax.broadcasted_iota(jnp.int32, u.shape, 0)
    out = pltpu.roll(u, k, 0)
    for j in range(k):
        out = jnp.where(row == j, uh[HALO - k + j:HALO - k + j + 1, :], out)
    return out


def _shift_up(u, nxt, k):
    n = u.shape[0]
    row = lax.broadcasted_iota(jnp.int32, u.shape, 0)
    out = pltpu.roll(u, n - k, 0)
    for j in range(k):
        out = jnp.where(row == n - k + j, nxt[j:j + 1, :], out)
    return out


def _conv_inputs(cc_ref, cx_ref, hc_ref, hx_ref, first_tile):
    cc = cc_ref[...].astype(F32)
    cx = cx_ref[...].astype(F32)
    u = cc * cx
    uh = jnp.where(first_tile, 0.0, hc_ref[...].astype(F32) * hx_ref[...].astype(F32))
    return cc, cx, u, _shift_down(u, uh, 1), _shift_down(u, uh, 2)


def _z_specs(tm, order):
    per_tile = tm // HALO
    cols = [pl.BlockSpec((tm, D_MODEL), functools.partial(lambda i, j: (order(i), j), j=j)) for j in range(1, 6)]
    halos = [pl.BlockSpec((HALO, D_MODEL),
                          functools.partial(lambda i, j: (jnp.maximum(order(i) * per_tile - 1, 0), j), j=j))
             for j in (2, 3)]
    return cols + halos


def _mix_fwd(x, attn, z, vec, w_out):
    t = x.shape[0]
    tm = min(TOKEN_TILE, t)

    def body(x_ref, a_ref, cb_ref, cc_ref, cx_ref, ga_ref, gc_ref, hc_ref, hx_ref, vec_ref, w_ref,
             m_ref, x2_ref, h2_ref, o_ref):
        i = pl.program_id(0)
        _, _, u, u1, u2 = _conv_inputs(cc_ref, cx_ref, hc_ref, hx_ref, i == 0)
        cv = vec_ref[4:5, :] * u2 + vec_ref[5:6, :] * u1 + vec_ref[6:7, :] * u
        conv = cb_ref[...].astype(F32) * cv
        merged = (_sigmoid(ga_ref[...].astype(F32)) * a_ref[...].astype(F32)
                  + _sigmoid(gc_ref[...].astype(F32)) * conv)
        mb = merged.astype(BF16)
        m_ref[...] = mb
        o = jnp.dot(mb, w_ref[...], preferred_element_type=F32)
        o_ref[...] = o.astype(BF16)
        x2 = x_ref[...] + vec_ref[0:1, :] * o
        x2_ref[...] = x2
        r = lax.rsqrt(jnp.mean(x2 * x2, axis=-1, keepdims=True) + EPS)
        h2 = (x2 * r) * vec_ref[1:2, :] * (1.0 + vec_ref[2:3, :]) + vec_ref[3:4, :]
        h2_ref[...] = h2.astype(BF16)

    tok = pl.BlockSpec((tm, D_MODEL), lambda i: (i, 0))
    outs, _ = _call(
        body, "mix_fwd", (t // tm,), [x, attn, z, z, z, z, z, z, z, vec, w_out],
        [tok, tok] + _z_specs(tm, lambda i: i) + [_full((SUBLANES, D_MODEL)), _full((D_MODEL, D_MODEL))],
        [jax.ShapeDtypeStruct((t, D_MODEL), BF16), jax.ShapeDtypeStruct((t, D_MODEL), F32),
         jax.ShapeDtypeStruct((t, D_MODEL), BF16), jax.ShapeDtypeStruct((t, D_MODEL), BF16)],
        [tok, tok, tok, tok])
    return outs


def _ffn_fwd(h2, w_t):
    t = h2.shape[0]
    tm = min(TOKEN_TILE, t)

    def body(h_ref, w_ref, gu_ref, a_ref):
        hb = h_ref[...]
        for n in range(D_FF // FF_CHUNK):
            lo, hi = n * FF_CHUNK, (n + 1) * FF_CHUNK
            g = lax.dot_general(hb, w_ref[lo:hi, :], NT_DIMS, preferred_element_type=F32)
            u = lax.dot_general(hb, w_ref[D_FF + lo:D_FF + hi, :], NT_DIMS, preferred_element_type=F32)
            sg = _sigmoid(g)
            silu = g * sg
            gu_ref[:, lo:hi] = (u * (sg * (1.0 + g * (1.0 - sg)))).astype(BF16)
            gu_ref[:, D_FF + lo:D_FF + hi] = silu.astype(BF16)
            a_ref[:, lo:hi] = (silu * u).astype(BF16)

    outs, _ = _call(
        body, "ffn_fwd", (t // tm,), [h2, w_t],
        [pl.BlockSpec((tm, D_MODEL), lambda i: (i, 0)), _full((2 * D_FF, D_MODEL))],
        [jax.ShapeDtypeStruct((t, 2 * D_FF), BF16), jax.ShapeDtypeStruct((t, D_FF), BF16)],
        [pl.BlockSpec((tm, 2 * D_FF), lambda i: (i, 0)), pl.BlockSpec((tm, D_FF), lambda i: (i, 0))])
    return outs


def _ffn_out_loss(a, gu, x2, target, vec, w_ffn_out):
    t = a.shape[0]
    tm = min(TOKEN_TILE, t)

    def body(a_ref, gu_ref, x2_ref, t_ref, vec_ref, w_ref, dx3_ref, df_ref, dgu_ref, acc_ref):
        @pl.when(pl.program_id(0) == 0)
        def _():
            acc_ref[...] = jnp.zeros_like(acc_ref)

        ga2 = vec_ref[0:1, :]
        gf = vec_ref[1:2, :]
        parts = min(ROW_PARTS, tm // LANES)
        part_rows = [slice(n * (tm // parts), (n + 1) * (tm // parts)) for n in range(parts)]

        def head(rows, f):
            x3 = x2_ref[rows, :] + ga2 * f
            r = lax.rsqrt(jnp.mean(x3 * x3, axis=-1, keepdims=True) + EPS)
            xn = x3 * r
            err = xn * gf - t_ref[rows, :]
            dy = err * (1.0 / D_MODEL)
            dxn = dy * gf
            dx3 = r * (dxn - xn * jnp.mean(dxn * xn, axis=-1, keepdims=True))
            dx3_ref[rows, :] = dx3.astype(GRAD_STREAM)
            sums = (jnp.sum(err * err, axis=0, keepdims=True), jnp.sum(dy * xn, axis=0, keepdims=True),
                    jnp.sum(dx3 * f, axis=0, keepdims=True))
            df = (dx3 * ga2).astype(BF16)
            df_ref[rows, :] = df
            return df, sums

        def tail(rows, df):
            for n in range(D_FF // FF_CHUNK):
                lo, hi = n * FF_CHUNK, (n + 1) * FF_CHUNK
                da = lax.dot_general(df, w_ref[lo:hi, :], NT_DIMS, preferred_element_type=F32)
                dgu_ref[rows, lo:hi] = (da * gu_ref[rows, lo:hi].astype(F32)).astype(BF16)
                dgu_ref[rows, D_FF + lo:D_FF + hi] = (da * gu_ref[rows, D_FF + lo:D_FF + hi].astype(F32)).astype(BF16)

        fs = [jnp.dot(a_ref[rows, :], w_ref[...], preferred_element_type=F32) for rows in part_rows]
        heads = [head(rows, f) for rows, f in zip(part_rows, fs)]
        for rows, (df, _) in zip(part_rows, heads):
            tail(rows, df)
        for k in range(3):
            total = heads[0][1][k]
            for _, sums in heads[1:]:
                total = total + sums[k]
            acc_ref[k:k + 1, :] += total

    tok = pl.BlockSpec((tm, D_MODEL), lambda i: (i, 0))
    outs, _ = _call(
        body, "ffn_out_loss", (t // tm,), [a, gu, x2, target, vec, w_ffn_out],
        [pl.BlockSpec((tm, D_FF), lambda i: (i, 0)), pl.BlockSpec((tm, 2 * D_FF), lambda i: (i, 0)),
         tok, tok, _full((SUBLANES, D_MODEL)), _full((D_FF, D_MODEL))],
        [jax.ShapeDtypeStruct((t, D_MODEL), GRAD_STREAM), jax.ShapeDtypeStruct((t, D_MODEL), BF16),
         jax.ShapeDtypeStruct((t, 2 * D_FF), BF16), jax.ShapeDtypeStruct((SUBLANES, D_MODEL), F32)],
        [tok, tok, pl.BlockSpec((tm, 2 * D_FF), lambda i: (i, 0)), _full((SUBLANES, D_MODEL))])
    return outs


def _ffn_in_bwd(dgu, x2, dx3, vec, w_t, rider):
    t = x2.shape[0]
    tm = min(TOKEN_TILE, t)

    def body(dgu_ref, x2_ref, dx3_ref, vec_ref, wf_ref, dx2_ref, acc_ref):
        @pl.when(pl.program_id(0) == 0)
        def _():
            acc_ref[...] = jnp.zeros_like(acc_ref)

        gffn = vec_ref[0:1, :]
        sc2 = vec_ref[1:2, :]
        parts = min(ROW_PARTS, tm // LANES)
        part_rows = [slice(n * (tm // parts), (n + 1) * (tm // parts)) for n in range(parts)]
        dhs = [jnp.dot(dgu_ref[rows, :], wf_ref[...], preferred_element_type=F32) for rows in part_rows]
        sums = []
        for rows, dh2 in zip(part_rows, dhs):
            x2 = x2_ref[rows, :]
            r = lax.rsqrt(jnp.mean(x2 * x2, axis=-1, keepdims=True) + EPS)
            xn = x2 * r
            sums.append((jnp.sum(dh2, axis=0, keepdims=True), jnp.sum(dh2 * xn * gffn, axis=0, keepdims=True),
                         jnp.sum(dh2 * xn * (1.0 + sc2), axis=0, keepdims=True)))
            dxn = dh2 * gffn * (1.0 + sc2)
            dx2 = dx3_ref[rows, :].astype(F32) + r * (dxn - xn * jnp.mean(dxn * xn, axis=-1, keepdims=True))
            dx2_ref[rows, :] = dx2.astype(GRAD_STREAM)
        for k in range(3):
            total = sums[0][k]
            for part in sums[1:]:
                total = total + part[k]
            acc_ref[k:k + 1, :] += total

    tok = pl.BlockSpec((tm, D_MODEL), lambda i: (i, 0))
    return _call(
        body, "ffn_in_bwd", (t // tm,), [dgu, x2, dx3, vec, w_t],
        [pl.BlockSpec((tm, 2 * D_FF), lambda i: (i, 0)), tok, tok, _full((SUBLANES, D_MODEL)),
         _full((2 * D_FF, D_MODEL))],
        [jax.ShapeDtypeStruct((t, D_MODEL), GRAD_STREAM), jax.ShapeDtypeStruct((SUBLANES, D_MODEL), F32)],
        [tok, _full((SUBLANES, D_MODEL))], rider=rider)


def _mix_bwd(dx2, oproj, attn, z, vec, w_out, rider):
    t = dx2.shape[0]
    tm = min(TOKEN_TILE, t)
    nt = t // tm
    rev = lambda i: nt - 1 - i

    def body(dx2_ref, m_ref, a_ref, cb_ref, cc_ref, cx_ref, ga_ref, gc_ref, hc_ref, hx_ref,
             vec_ref, wo_ref, do_ref, da_ref, dr_ref, acc_ref, carry_ref):
        i = pl.program_id(0)

        @pl.when(i == 0)
        def _():
            acc_ref[...] = jnp.zeros_like(acc_ref)
            carry_ref[...] = jnp.zeros_like(carry_ref)

        ga1 = vec_ref[0:1, :]
        w0, w1, w2 = vec_ref[1:2, :], vec_ref[2:3, :], vec_ref[3:4, :]
        dx2 = dx2_ref[...].astype(F32)
        acc_ref[0:1, :] += jnp.sum(dx2 * m_ref[...].astype(F32), axis=0, keepdims=True)
        do = (dx2 * ga1).astype(BF16)
        do_ref[...] = do
        dm = lax.dot_general(do, wo_ref[...], NT_DIMS, preferred_element_type=F32)

        cc, cx, u, u1, u2 = _conv_inputs(cc_ref, cx_ref, hc_ref, hx_ref, i == nt - 1)
        cv = w0 * u2 + w1 * u1 + w2 * u
        cb = cb_ref[...].astype(F32)
        sa = _sigmoid(ga_ref[...].astype(F32))
        sc = _sigmoid(gc_ref[...].astype(F32))
        attn = a_ref[...].astype(F32)
        da_ref[...] = (dm * sa).astype(BF16)
        dconv = dm * sc
        dr_ref[:, 3 * D_MODEL:4 * D_MODEL] = (dm * attn * sa * (1.0 - sa)).astype(BF16)
        dr_ref[:, 4 * D_MODEL:5 * D_MODEL] = (dconv * (cb * cv) * (1.0 - sc)).astype(BF16)
        dr_ref[:, 0:D_MODEL] = (dconv * cv).astype(BF16)
        dcv = dconv * cb
        acc_ref[1:2, :] += jnp.sum(dcv * u2, axis=0, keepdims=True)
        acc_ref[2:3, :] += jnp.sum(dcv * u1, axis=0, keepdims=True)
        acc_ref[3:4, :] += jnp.sum(dcv * u, axis=0, keepdims=True)
        nxt = carry_ref[...]
        du = w2 * dcv + w1 * _shift_up(dcv, nxt, 1) + w0 * _shift_up(dcv, nxt, 2)
        carry_ref[...] = dcv[0:SUBLANES, :]
        dr_ref[:, D_MODEL:2 * D_MODEL] = (du * cx).astype(BF16)
        dr_ref[:, 2 * D_MODEL:3 * D_MODEL] = (du * cc).astype(BF16)

    tok = pl.BlockSpec((tm, D_MODEL), lambda i: (rev(i), 0))
    return _call(
        body, "mix_bwd", (nt,), [dx2, oproj, attn, z, z, z, z, z, z, z, vec, w_out],
        [tok, tok, tok] + _z_specs(tm, rev) + [_full((SUBLANES, D_MODEL)), _full((D_MODEL, D_MODEL))],
        [jax.ShapeDtypeStruct((t, D_MODEL), BF16), jax.ShapeDtypeStruct((t, D_MODEL), BF16),
         jax.ShapeDtypeStruct((t, REST_WIDTH), BF16), jax.ShapeDtypeStruct((SUBLANES, D_MODEL), F32)],
        [tok, tok, pl.BlockSpec((tm, REST_WIDTH), lambda i: (rev(i), 0)), _full((SUBLANES, D_MODEL))],
        scratch=[pltpu.VMEM((SUBLANES, D_MODEL), F32)], rider=rider)


def _attn_bwd(z, dattn, attn, lse, sinks, rider):
    t = z.shape[0]
    tq = min(TOKEN_TILE, t)
    nblk = tq // WINDOW
    nt = t // tq

    def body(q_ref, kv_ref, do_ref, o_ref, lse_ref, sink_ref, dq_ref, dkv_ref, ds_ref, acc_ref, bias_ref):
        i = pl.program_id(0)

        @pl.when(i == 0)
        def _():
            acc_ref[...] = jnp.zeros_like(acc_ref)
            ds_ref[...] = jnp.zeros_like(ds_ref)
            _fill_window_bias(bias_ref)

        lane = lax.broadcasted_iota(jnp.int32, (1, LANES), 1)
        ind_row = lax.broadcasted_iota(jnp.int32, (SUBLANES, LANES), 0)
        ind_low = lax.broadcasted_iota(jnp.int32, (SUBLANES, LANES), 1) < HEAD_DIM
        indicator = jnp.where(jnp.logical_or(jnp.logical_and(ind_row == 0, ind_low),
                                             jnp.logical_and(ind_row == 1, jnp.logical_not(ind_low))),
                              1.0, 0.0).astype(BF16)
        low = lax.broadcasted_iota(jnp.int32, (2 * WINDOW, LANES), 1) < HEAD_DIM

        def both_heads(even, odd):
            picked = jnp.where(low, even, odd)
            return picked + jnp.concatenate([picked[:, HEAD_DIM:], picked[:, :HEAD_DIM]], axis=1)

        def window(b):
            row0 = pl.multiple_of(b * WINDOW, WINDOW)
            start = i * tq + b * WINDOW
            prev = pl.multiple_of(jnp.maximum(start - WINDOW, 0), WINDOW)
            cur = pl.multiple_of(start, WINDOW)
            kvw = jnp.concatenate([kv_ref[pl.ds(prev, WINDOW), :], kv_ref[pl.ds(cur, WINDOW), :]], axis=0)
            return (row0, prev, cur, _half_tiles(kvw[:, :KV_WIDTH]), _half_tiles(kvw[:, KV_WIDTH:]),
                    bias_ref[jnp.minimum(start, 1)])

        def block_group(bb, dsink):
            windows = [window(bb * ATTN_BWD_BLOCKS + n) for n in range(ATTN_BWD_BLOCKS)]
            dk_groups = [[] for _ in windows]
            dv_groups = [[] for _ in windows]
            for j in range(N_KV_HEADS):
                stacks, deltas, dq_ts = [], [], []
                for row0, _, _, _, _, _ in windows:
                    qst = _stack_pairs(q_ref, row0, j)
                    dost = _stack_pairs(do_ref, row0, j)
                    prod = dost.astype(F32) * _stack_pairs(o_ref, row0, j).astype(F32)
                    prod_hi = prod.astype(BF16)
                    prod_lo = (prod - prod_hi.astype(F32)).astype(BF16)
                    stacks.append((qst, dost))
                    deltas.append(lax.dot_general(indicator, prod_hi, NT_DIMS, preferred_element_type=F32)
                                  + lax.dot_general(indicator, prod_lo, NT_DIMS, preferred_element_type=F32))
                    dq_ts.append(jnp.zeros((LANES, STACK), F32))
                dk_par = [[] for _ in windows]
                dv_par = [[] for _ in windows]
                for parity in range(2):
                    heads = [j * GROUP + 2 * p + parity for p in range(PAIRS)]
                    sink = _per_pair_row([sink_ref[h] * LOG2E for h in heads])
                    for n, (row0, _, _, k_halves, v_halves, bias) in enumerate(windows):
                        qst, dost = stacks[n]
                        kk, vv = k_halves[j][parity], v_halves[j][parity]
                        s = lax.dot_general(kk, qst, NT_DIMS, preferred_element_type=F32) * SCORE_SCALE + bias
                        lse = jnp.concatenate([lse_ref[h:h + 1, pl.ds(row0, WINDOW)] for h in heads], axis=1)
                        p = jnp.exp2(s - lse)
                        dp = lax.dot_general(vv, dost, NT_DIMS, preferred_element_type=F32)
                        delta = deltas[n][parity:parity + 1, :]
                        dsb = (p * (dp - delta)).astype(BF16)
                        dq_ts[n] = dq_ts[n] + lax.dot_general(kk, dsb, TN_DIMS, preferred_element_type=F32)
                        dk_par[n].append(jnp.dot(dsb, qst, preferred_element_type=F32))
                        dv_par[n].append(jnp.dot(p.astype(BF16), dost, preferred_element_type=F32))
                        weighted = jnp.exp2(sink - lse) * delta
                        for pr, h in enumerate(heads):
                            dsink = dsink - jnp.where(
                                lane == h, jnp.sum(weighted[:, pr * WINDOW:(pr + 1) * WINDOW]), 0.0)
                for n, (row0, _, _, _, _, _) in enumerate(windows):
                    dq_st = jnp.transpose((dq_ts[n] * ATTN_SCALE).astype(BF16))
                    for pr in range(PAIRS):
                        dq_ref[pl.ds(row0, WINDOW), (j * PAIRS + pr) * LANES:(j * PAIRS + pr + 1) * LANES] = (
                            dq_st[pr * WINDOW:(pr + 1) * WINDOW, :])
                    dk_groups[n].append(both_heads(dk_par[n][0], dk_par[n][1]))
                    dv_groups[n].append(both_heads(dv_par[n][0], dv_par[n][1]))
            for n, (_, prev, cur, _, _, _) in enumerate(windows):
                blk = jnp.concatenate([jnp.where(low, dk_groups[n][0], dk_groups[n][1]) * ATTN_SCALE,
                                       jnp.where(low, dv_groups[n][0], dv_groups[n][1])], axis=1)
                acc_ref[pl.ds(prev, WINDOW), :] += blk[:WINDOW, :]
                acc_ref[pl.ds(cur, WINDOW), :] += blk[WINDOW:, :]
            return dsink

        dsink = lax.fori_loop(0, nblk // ATTN_BWD_BLOCKS, block_group, jnp.zeros((1, LANES), F32))
        ds_ref[0:1, :] += dsink

        @pl.when(i == nt - 1)
        def _():
            dkv_ref[...] = acc_ref[...].astype(BF16)

    tok = pl.BlockSpec((tq, D_MODEL), lambda i: (i, 0))
    return _call(
        body, "attn_bwd", (nt,), [z, z, dattn, attn, lse, sinks],
        [tok, pl.BlockSpec((t, 2 * KV_WIDTH), lambda i: (0, KV_COL // (2 * KV_WIDTH))), tok, tok,
         pl.BlockSpec((N_Q_HEADS, tq), lambda i: (0, i)), pl.BlockSpec(memory_space=pltpu.SMEM)],
        [jax.ShapeDtypeStruct((t, D_MODEL), BF16), jax.ShapeDtypeStruct((t, 2 * KV_WIDTH), BF16),
         jax.ShapeDtypeStruct((SUBLANES, LANES), F32)],
        [tok, _full((t, 2 * KV_WIDTH)), _full((SUBLANES, LANES))],
        scratch=[pltpu.VMEM((t, 2 * KV_WIDTH), F32), pltpu.VMEM((2, 2 * WINDOW, STACK), F32)], rider=rider)


def _inproj_bwd(dq, drest, dkv, x, dx2, vec, w_t, rider):
    t = x.shape[0]
    tm = min(TOKEN_TILE, t)

    def body(dq_ref, dr_ref, dkv_ref, x_ref, dx2_ref, vec_ref, w_ref, gx_ref, acc_ref, db_ref):
        @pl.when(pl.program_id(0) == 0)
        def _():
            acc_ref[...] = jnp.zeros_like(acc_ref)
            db_ref[...] = jnp.zeros_like(db_ref)

        g = vec_ref[0:1, :]
        sc1 = vec_ref[1:2, :]
        dqb, drb, dkvb = dq_ref[...], dr_ref[...], dkv_ref[...]
        dh = jnp.dot(dqb, w_ref[:REF_KV_COL, :], preferred_element_type=F32)
        dh = dh + jnp.dot(drb, w_ref[REF_REST_COL:, :], preferred_element_type=F32)
        dh = dh + jnp.dot(dkvb, w_ref[REF_KV_COL:REF_REST_COL, :], preferred_element_type=F32)
        db_ref[:, :REF_KV_COL] += jnp.sum(dqb.astype(F32), axis=0, keepdims=True)
        db_ref[:, REF_REST_COL:] += jnp.sum(drb.astype(F32), axis=0, keepdims=True)
        db_ref[:, REF_KV_COL:REF_REST_COL] += jnp.sum(dkvb.astype(F32), axis=0, keepdims=True)
        xf = x_ref[...]
        r = lax.rsqrt(jnp.mean(xf * xf, axis=-1, keepdims=True) + EPS)
        xn = xf * r
        acc_ref[0:1, :] += jnp.sum(dh, axis=0, keepdims=True)
        acc_ref[1:2, :] += jnp.sum(dh * xn * g, axis=0, keepdims=True)
        acc_ref[2:3, :] += jnp.sum(dh * xn * (1.0 + sc1), axis=0, keepdims=True)
        dxn = dh * g * (1.0 + sc1)
        gx_ref[...] = dx2_ref[...].astype(F32) + r * (dxn - xn * jnp.mean(dxn * xn, axis=-1, keepdims=True))

    tok = pl.BlockSpec((tm, D_MODEL), lambda i: (i, 0))
    return _call(
        body, "inproj_bwd", (t // tm,), [dq, drest, dkv, x, dx2, vec, w_t],
        [tok, pl.BlockSpec((tm, REST_WIDTH), lambda i: (i, 0)),
         pl.BlockSpec((tm, 2 * KV_WIDTH), lambda i: (i, 0)), tok, tok,
         _full((SUBLANES, D_MODEL)), _full((IN_WIDTH, D_MODEL))],
        [jax.ShapeDtypeStruct((t, D_MODEL), F32), jax.ShapeDtypeStruct((SUBLANES, D_MODEL), F32),
         jax.ShapeDtypeStruct((1, IN_WIDTH), F32)],
        [tok, _full((SUBLANES, D_MODEL)), _full((1, IN_WIDTH))], rider=rider)


def _weight_grad(b, a, name, bn, rows=None, row0=0, into=None, rider=None):
    t, n = b.shape
    m = a.shape[1]
    rows = n if rows is None else rows
    tk = min(TOKEN_TILE, t)
    for cand in (4 * TOKEN_TILE, 2 * TOKEN_TILE):
        if t % cand == 0 and 2 * cand * (bn + m) * 2 + bn * m * 4 <= WGRAD_VMEM:
            tk = cand
            break
    nk = t // tk
    block0 = row0 // bn

    def body(b_ref, a_ref, *rest):
        out_ref, acc_ref = rest[-2:]
        k = pl.program_id(1)

        @pl.when(k == 0)
        def _():
            acc_ref[...] = jnp.zeros_like(acc_ref)

        acc_ref[...] += lax.dot_general(b_ref[...], a_ref[...], TN_DIMS, preferred_element_type=F32)

        @pl.when(k == nk - 1)
        def _():
            out_ref[...] = acc_ref[...].astype(BF16)

    outs, routs = _call(
        body, name, (n // bn, nk), [b, a] + ([] if into is None else [into]),
        [pl.BlockSpec((tk, bn), lambda j, k: (k, j)), pl.BlockSpec((tk, m), lambda j, k: (k, 0))]
        + ([] if into is None else [ANY]),
        [jax.ShapeDtypeStruct((rows, m), BF16)], [pl.BlockSpec((bn, m), lambda j, k: (block0 + j, 0))],
        scratch=[pltpu.VMEM((bn, m), F32)], rider=rider, aliases=None if into is None else {2: 0})
    return outs[0], routs


def _to_rows(v):
    n = v.shape[0]
    padded = -(-n // (SUBLANES * LANES)) * SUBLANES * LANES
    return jnp.pad(v, (0, padded - n)).reshape(padded // LANES, LANES)


def _vec_rows(*rows):
    stacked = jnp.concatenate([r.reshape(1, D_MODEL) for r in rows], axis=0)
    return jnp.pad(stacked, ((0, SUBLANES - len(rows)), (0, 0)))


def kernel(x, c, w_ada, b_ada, g_mix, w_in, b_in, sinks, conv_w, w_out, g_ffn, w_ffn_in, w_ffn_out, g_final, loss_target, m_w_ada, m_b_ada, m_g_mix, m_w_in, m_b_in, m_sinks, m_conv_w, m_w_out, m_g_ffn, m_w_ffn_in, m_w_ffn_out, m_g_final, v_w_ada, v_b_ada, v_g_mix, v_w_in, v_b_in, v_sinks, v_conv_w, v_w_out, v_g_ffn, v_w_ffn_in, v_w_ffn_out, v_g_final):
    ix, iy, ic = _my_place()
    me = 4 * ix + 2 * iy + ic
    xs = x[0]
    target = loss_target[0]
    ada_cols = w_ada.shape[2]
    conv_cols = conv_w.shape[2]

    wt_in, wt_fi = jnp.transpose(w_in[0]), jnp.transpose(w_ffn_in[0])
    b_cols = lax.dynamic_slice_in_dim(b_ada, me * ada_cols, ada_cols, axis=1)
    g_in, (cast_fi, cast_out, cast_fo), first, mod_all = _gather_first_weight(
        wt_in, [wt_fi, w_out[0], w_ffn_out[0]], _to_rows(jnp.concatenate([c[0], conv_w[0].reshape(-1)])),
        w_ada[0], b_cols)
    first = first.reshape(N_DEV, -1)
    c_all = first[:, :D_MODEL]
    conv_full = jnp.transpose(first[:, D_MODEL:D_MODEL + 3 * conv_cols].reshape(N_DEV, 3, conv_cols), (1, 0, 2))
    conv_full = conv_full.reshape(3, D_MODEL)
    mod = lax.dynamic_index_in_dim(mod_all, me, axis=1, keepdims=False).reshape(N_MOD, D_MODEL)
    sh1, sc1, ga1, sh2, sc2, ga2 = [mod[i:i + 1] for i in range(N_MOD)]
    w_in_t = g_in.reshape(IN_WIDTH, D_MODEL)
    (z, h1), (g_fi, g_out) = _inproj_fwd(xs, _vec_rows(g_mix, sc1, sh1), w_in_t, b_in,
                                         _gather_rider([cast_fi, cast_out]))
    w_fi_t = g_fi.reshape(2 * D_FF, D_MODEL)
    w_out_full = g_out.reshape(D_MODEL, D_MODEL)
    (attn, lse), (g_fo,) = _attn_fwd(z, sinks[0], _gather_rider([cast_fo]))
    w_fo_full = g_fo.reshape(D_FF, D_MODEL)
    merged, x2, h2, oproj = _mix_fwd(
        xs, attn, z, _vec_rows(ga1, g_ffn, sc2, sh2, conv_full[0], conv_full[1], conv_full[2]), w_out_full)
    gu, act = _ffn_fwd(h2, w_fi_t)
    dx3, df, dgu, acc_l = _ffn_out_loss(act, gu, x2, target, _vec_rows(ga2, g_final), w_fo_full)

    gw_fo, _ = _weight_grad(act, df, "wgrad_ffn_out", D_FF)
    gw_fi, _ = _weight_grad(dgu, h2, "wgrad_ffn_in", D_FF)
    blocks_fo = gw_fo.reshape(N_DEV, D_FF // N_DEV, D_MODEL)
    blocks_fi = gw_fi.reshape(N_DEV, 2 * D_FF // N_DEV, D_MODEL)
    (dx2, acc_f), (sib_fo, sib_fi) = _ffn_in_bwd(dgu, x2, dx3, _vec_rows(g_ffn, sc2), w_fi_t,
                                                 _sibling_rider([blocks_fo, blocks_fi]))
    sums_fo, mine_fo = _sibling_sum(blocks_fo, sib_fo, "sibling_sum_ffn_out")
    sums_fi, mine_fi = _sibling_sum(blocks_fi, sib_fi, "sibling_sum_ffn_in")
    (dout, dattn, drest, acc_m), (ici_fo, ici_fi) = _mix_bwd(
        dx2, oproj, attn, z, _vec_rows(ga1, conv_full[0], conv_full[1], conv_full[2]), w_out_full,
        _chip_rider([sums_fo, sums_fi]))
    gw_out, _ = _weight_grad(merged, dout, "wgrad_out", D_MODEL)
    blocks_out = gw_out.reshape(N_DEV, D_MODEL // N_DEV, D_MODEL)
    (dq, dkv, dsink), (sib_out,) = _attn_bwd(z, dattn, attn, lse, sinks[0], _sibling_rider([blocks_out]))
    sums_out, mine_out = _sibling_sum(blocks_out, sib_out, "sibling_sum_out")
    gw_in, (ici_out,) = _weight_grad(drest, h1, "wgrad_in_rest", IN_CHUNK, rows=IN_WIDTH, row0=REF_REST_COL,
                                     rider=_chip_rider([sums_out]))
    gw_in, _ = _weight_grad(dq, h1, "wgrad_in_q", D_MODEL, rows=IN_WIDTH, row0=0, into=gw_in)
    gw_in, _ = _weight_grad(dkv, h1, "wgrad_in_kv", 2 * KV_WIDTH, rows=IN_WIDTH, row0=REF_KV_COL, into=gw_in)
    blocks_in = gw_in.reshape(N_DEV, IN_WIDTH // N_DEV, D_MODEL)
    (sib_in,) = _carry(_sibling_rider([blocks_in]), "sibling_w_in")
    sums_in, mine_in = _sibling_sum(blocks_in, sib_in, "sibling_sum_in")
    (grad_x, acc_i, db_in), (ici_in,) = _inproj_bwd(dq, drest, dkv, xs, dx2, _vec_rows(g_mix, sc1), w_in_t,
                                                    _chip_rider([sums_in]))

    widen = lambda vec: jnp.pad(vec, (0, -vec.shape[0] % D_MODEL))
    packed = jnp.concatenate([
        acc_i[0], acc_i[1], acc_m[0], acc_f[0], acc_f[1], acc_l[2],
        acc_i[2], widen(db_in[0]), acc_f[2], acc_l[1],
        acc_m[1], acc_m[2], acc_m[3], widen(dsink[0]), acc_l[0],
        jnp.zeros(((PACK_ROWS - PACK_SQERR - 1) * D_MODEL,), F32)]).reshape(PACK_ROWS, D_MODEL)
    packed_all = _small_allgather(packed, "gather_small")
    dmod_all = packed_all[:, PACK_DMOD:PACK_DMOD + N_MOD, :].reshape(N_DEV, N_MOD * D_MODEL)
    dmod_cols = lax.dynamic_slice_in_dim(dmod_all, me * ada_cols, ada_cols, axis=1)
    g_w_ada = _ada_weight_grad(c_all, dmod_cols)
    row_of = lambda a: a.reshape(1, -1)
    small, g_conv_full, loss = _small_finalize(packed_all, {
        "b_ada": (b_ada, m_b_ada, v_b_ada), "g_mix": (g_mix, m_g_mix, v_g_mix), "b_in": (b_in, m_b_in, v_b_in),
        "g_ffn": (g_ffn, m_g_ffn, v_g_ffn), "sinks": (sinks, m_sinks, v_sinks),
        "g_final": (row_of(g_final), row_of(m_g_final), row_of(v_g_final))})
    small["g_final"] = tuple(o.reshape(g_final.shape) for o in small["g_final"])
    g_conv = lax.dynamic_slice_in_dim(g_conv_full, me * conv_cols, conv_cols, axis=1)
    d_conv, nm_conv, nv_conv = _adamw(conv_w[0], g_conv, m_conv_w[0], v_conv_w[0], "adamw_conv_w")
    small["conv_w"] = (g_conv[None], d_conv[None], nm_conv[None], nv_conv[None])

    def reduced(mine, ici, w, m, v, name, transposed=False):
        turn = jnp.transpose if transposed else (lambda a: a)
        return tuple(turn(o)[None] for o in _chip_sum_adamw(mine, ici, turn(w[0]), turn(m[0]), turn(v[0]), name))

    d_ada, nm_ada, nv_ada = _adamw(w_ada[0], g_w_ada, m_w_ada[0], v_w_ada[0], "adamw_w_ada")
    res = {
        "w_ada": (g_w_ada[None], d_ada[None], nm_ada[None], nv_ada[None]),
        "w_in": reduced(mine_in, ici_in, w_in, m_w_in, v_w_in, "adamw_w_in", transposed=True),
        "w_out": reduced(mine_out, ici_out, w_out, m_w_out, v_w_out, "adamw_w_out"),
        "w_ffn_in": reduced(mine_fi, ici_fi, w_ffn_in, m_w_ffn_in, v_w_ffn_in, "adamw_w_ffn_in", transposed=True),
        "w_ffn_out": reduced(mine_fo, ici_fo, w_ffn_out, m_w_ffn_out, v_w_ffn_out, "adamw_w_ffn_out"),
    }
    res.update(small)
    order = ["w_ada", "b_ada", "g_mix", "w_in", "b_in", "sinks", "conv_w", "w_out", "g_ffn", "w_ffn_in", "w_ffn_out",
             "g_final"]
    outs = [loss.reshape(()), grad_x[None]]
    for k in range(4):
        outs += [res[n][k] for n in order]
    return tuple(outs)
```

```python
import functools
import math

import jax
import jax.numpy as jnp
from jax import lax
from jax.experimental import pallas as pl
from jax.experimental.pallas import tpu as pltpu

F32 = jnp.float32
BF16 = jnp.bfloat16
GRAD_STREAM = F32

D_MODEL = 1024
HEAD_DIM = 64
N_Q_HEADS = 16
N_KV_HEADS = 2
GROUP = 8
WINDOW = 128
KV_WIDTH = N_KV_HEADS * HEAD_DIM
D_FF = 2816
IN_WIDTH = 6400
N_MOD = 6
EPS = 1e-6
N_DEV = 8
REST_WIDTH = 5 * D_MODEL
KV_COL = D_MODEL + REST_WIDTH
ATTN_SCALE = HEAD_DIM ** -0.5

ADAM_LR = 0.001
ADAM_B1 = 0.9
ADAM_B2 = 0.999
ADAM_EPS = 1e-08
ADAM_WD = 0.01
ADAM_STEP = 10

LANES = 128
SUBLANES = 8
BF16_ROWS = 16
VMEM_LIMIT = 56 * 1024 * 1024
TOKEN_TILE = 512
FF_CHUNK = 256
ROW_PARTS = 2
WGRAD_VMEM = 40 * 1024 * 1024
MESH = pl.DeviceIdType.MESH
ANY = pl.BlockSpec(memory_space=pl.ANY)

NT_DIMS = (((1,), (1,)), ((), ()))
TN_DIMS = (((0,), (0,)), ((), ()))
CHIP_FLIPS = [(0, 0), (1, 0), (0, 1), (1, 1)]


def _full(shape):
    return pl.BlockSpec(shape, lambda *_: (0,) * len(shape))


def _my_place():
    return lax.axis_index("x"), lax.axis_index("y"), lax.axis_index("c")


def _flip(v, bit):
    return 1 - v if bit else v


def _sigmoid(v):
    return 1.0 / (1.0 + jnp.exp2(v * (-1.4426950408889634)))


class _Rider:
    def __init__(self, ins, out_shapes, sem_shapes, first=None, mid=None, last=None, ins_in_vmem=False):
        self.ins, self.out_shapes, self.sem_shapes = list(ins), list(out_shapes), list(sem_shapes)
        self.in_specs = [_full(a.shape) if ins_in_vmem else ANY for a in self.ins]
        self.hooks = [(when, fn) for when, fn in (("first", first), ("mid", mid), ("last", last)) if fn is not None]


def _call(body, name, grid, args, in_specs, out_shape, out_specs, scratch=(), rider=None, aliases=None):
    n_in, n_out, n_scr = len(args), len(out_shape), len(scratch)
    r_in = rider.ins if rider else []
    r_out = rider.out_shapes if rider else []
    r_sem = rider.sem_shapes if rider else []
    nsteps = math.prod(grid)

    def full_body(*refs):
        pos = 0
        groups = []
        for size in (n_in, len(r_in), n_out, len(r_out), n_scr, len(r_sem)):
            groups.append(refs[pos:pos + size])
            pos += size
        ins, rins, outs, routs, scr, rsems = groups
        step = pl.program_id(0)
        for axis in range(1, len(grid)):
            step = step * grid[axis] + pl.program_id(axis)
        at = {"first": 0, "mid": (3 * nsteps) // 4, "last": nsteps - 1}
        hooks = rider.hooks if rider else []
        for when, fn in hooks:
            if when != "last":
                pl.when(step == at[when])(functools.partial(fn, rins, routs, rsems))
        body(*ins, *outs, *scr)
        for when, fn in hooks:
            if when == "last":
                pl.when(step == at[when])(functools.partial(fn, rins, routs, rsems))

    outs = pl.pallas_call(
        full_body, name=name, grid=grid,
        out_shape=list(out_shape) + list(r_out),
        in_specs=list(in_specs) + (rider.in_specs if rider else []),
        out_specs=list(out_specs) + [ANY] * len(r_out),
        scratch_shapes=list(scratch) + list(r_sem),
        input_output_aliases=dict(aliases or {}),
        compiler_params=pltpu.CompilerParams(dimension_semantics=("arbitrary",) * len(grid),
                                             vmem_limit_bytes=VMEM_LIMIT),
    )(*args, *r_in)
    return list(outs[:n_out]), list(outs[n_out:])


def _gather_rider(shards):
    n = len(shards)

    def setup(outs, sems):
        x, y, c = _my_place()
        send_sems, recv_sems, _ = sems
        chips = [(1 - x, y), (x, 1 - y), (1 - x, 1 - y)]

        def block(w, place):
            return outs[w].at[4 * place[0] + 2 * place[1] + place[2]]

        def copy(w, k, place, to, src=None):
            return pltpu.make_async_remote_copy(
                src_ref=block(w, place) if src is None else src, dst_ref=block(w, place),
                send_sem=send_sems.at[w, k], recv_sem=recv_sems.at[w, k], device_id=to, device_id_type=MESH)

        return (x, y, c), (x, y, 1 - c), chips, block, copy

    def first(ins, outs, sems):
        me, sibling, chips, block, copy = setup(outs, sems)
        for w in range(n):
            pltpu.make_async_copy(ins[w], block(w, me), sems[2].at[w]).start()
            copy(w, 0, me, sibling, src=ins[w]).start()
            for j, chip in enumerate(chips):
                copy(w, 1 + j, me, (*chip, me[2]), src=ins[w]).start()

    def mid(ins, outs, sems):
        me, sibling, chips, block, copy = setup(outs, sems)
        for w in range(n):
            for j, chip in enumerate(chips):
                copy(w, 1 + j, (*chip, me[2]), me).wait_recv()
                copy(w, 4 + j, (*chip, me[2]), sibling).start()

    def last(ins, outs, sems):
        me, sibling, chips, block, copy = setup(outs, sems)
        for w in range(n):
            copy(w, 0, sibling, me).wait_recv()
            for j, chip in enumerate(chips):
                copy(w, 4 + j, (*chip, 1 - me[2]), me).wait_recv()
            copy(w, 0, me, sibling, src=ins[w]).wait_send()
            for j, chip in enumerate(chips):
                copy(w, 1 + j, me, (*chip, me[2]), src=ins[w]).wait_send()
                copy(w, 4 + j, (*chip, me[2]), sibling).wait_send()
            pltpu.make_async_copy(ins[w], block(w, me), sems[2].at[w]).wait()

    return _Rider(
        shards, [jax.ShapeDtypeStruct((N_DEV,) + s.shape, BF16) for s in shards],
        [pltpu.SemaphoreType.DMA((n, N_DEV - 1)), pltpu.SemaphoreType.DMA((n, N_DEV - 1)),
         pltpu.SemaphoreType.DMA((n,))],
        first=first, mid=mid, last=last, ins_in_vmem=True)


def _sibling_rider(gblocks):
    n = len(gblocks)

    def copies(ins, outs, sems):
        x, y, c = _my_place()
        send_sems, recv_sems = sems
        made = []
        for w in range(n):
            for f, (fx, fy) in enumerate(CHIP_FLIPS):
                chip = 4 * _flip(x, fx) + 2 * _flip(y, fy)
                made.append(pltpu.make_async_remote_copy(
                    src_ref=ins[w].at[chip + 1 - c], dst_ref=outs[w].at[f], send_sem=send_sems.at[w, f],
                    recv_sem=recv_sems.at[w, f], device_id=(x, y, 1 - c), device_id_type=MESH))
        return made

    def first(ins, outs, sems):
        for cp in copies(ins, outs, sems):
            cp.start()

    def last(ins, outs, sems):
        for cp in copies(ins, outs, sems):
            cp.wait_recv()
            cp.wait_send()

    return _Rider(gblocks, [jax.ShapeDtypeStruct((4,) + g.shape[1:], BF16) for g in gblocks],
                  [pltpu.SemaphoreType.DMA((n, 4))] * 2, first=first, last=last)


def _chip_rider(sums):
    n = len(sums)

    def copies(ins, outs, sems):
        x, y, c = _my_place()
        send_sems, recv_sems = sems
        made = []
        for w in range(n):
            for f in (1, 2, 3):
                fx, fy = CHIP_FLIPS[f]
                made.append(pltpu.make_async_remote_copy(
                    src_ref=ins[w].at[f - 1], dst_ref=outs[w].at[f - 1], send_sem=send_sems.at[w, f - 1],
                    recv_sem=recv_sems.at[w, f - 1], device_id=(_flip(x, fx), _flip(y, fy), c), device_id_type=MESH))
        return made

    def first(ins, outs, sems):
        for cp in copies(ins, outs, sems):
            cp.start()

    def last(ins, outs, sems):
        for cp in copies(ins, outs, sems):
            cp.wait_recv()
            cp.wait_send()

    return _Rider(sums, [jax.ShapeDtypeStruct(s.shape, BF16) for s in sums],
                  [pltpu.SemaphoreType.DMA((n, 3))] * 2, first=first, last=last)


def _push_to_all(v_ref, out_ref, send_sems, recv_sems, local_sem, wait=True):
    x, y, c = _my_place()
    me = 4 * x + 2 * y + c
    mine = pltpu.make_async_copy(v_ref, out_ref.at[me], local_sem)
    mine.start()
    sends = []
    for k in range(1, N_DEV):
        px, py, pc = _flip(x, k & 4), _flip(y, k & 2), _flip(c, k & 1)
        cp = pltpu.make_async_remote_copy(
            src_ref=v_ref, dst_ref=out_ref.at[me], send_sem=send_sems.at[k - 1], recv_sem=recv_sems.at[k - 1],
            device_id=(px, py, pc), device_id_type=MESH)
        cp.start()
        sends.append(cp)

    def finish():
        for k in range(1, N_DEV):
            px, py, pc = _flip(x, k & 4), _flip(y, k & 2), _flip(c, k & 1)
            pltpu.make_async_remote_copy(
                src_ref=v_ref, dst_ref=out_ref.at[4 * px + 2 * py + pc], send_sem=send_sems.at[k - 1],
                recv_sem=recv_sems.at[k - 1], device_id=(px, py, pc), device_id_type=MESH).wait_recv()
        for cp in sends:
            cp.wait_send()
        mine.wait()

    if wait:
        finish()
    return finish


def _small_allgather(v, name):
    def body(v_ref, out_ref, send_sems, recv_sems, local_sem):
        _push_to_all(v_ref, out_ref, send_sems, recv_sems, local_sem)

    return pl.pallas_call(
        body, name=name,
        out_shape=jax.ShapeDtypeStruct((N_DEV,) + v.shape, F32),
        in_specs=[pl.BlockSpec(memory_space=pltpu.VMEM)],
        out_specs=pl.BlockSpec(memory_space=pltpu.VMEM),
        scratch_shapes=[pltpu.SemaphoreType.DMA((N_DEV - 1,)), pltpu.SemaphoreType.DMA((N_DEV - 1,)),
                        pltpu.SemaphoreType.DMA],
        compiler_params=pltpu.CompilerParams(vmem_limit_bytes=VMEM_LIMIT),
    )(v)


def _gather_first_weight(shard, others, cond_rows, w_ada, b_cols):
    n = len(others)
    ada_cols = w_ada.shape[1]
    c_rows = D_MODEL // LANES

    def body(*refs):
        w_ref, other_refs = refs[0], refs[1:1 + n]
        cond_ref, wada_ref, bcols_ref = refs[1 + n:4 + n]
        out_ref, cast_refs = refs[4 + n], refs[5 + n:5 + 2 * n]
        cond_all_ref, mod_all_ref = refs[5 + 2 * n:7 + 2 * n]
        mine_ref, mod_ref, send_sems, recv_sems, local_sem, small_send, small_recv, small_local = refs[7 + 2 * n:]
        x, y, c = _my_place()
        me, sibling = (x, y, c), (x, y, 1 - c)
        xnb, ynb, diag = (1 - x, y), (x, 1 - y), (1 - x, 1 - y)
        half = shard.shape[0] // 2

        def block(place, part=None):
            ref = out_ref.at[4 * place[0] + 2 * place[1] + place[2]]
            return ref if part is None else ref.at[pl.ds(part * half, half)]

        def copy(k, place, to, part=None, src=None):
            return pltpu.make_async_remote_copy(
                src_ref=block(place, part) if src is None else src, dst_ref=block(place, part),
                send_sem=send_sems.at[k], recv_sem=recv_sems.at[k], device_id=to, device_id_type=MESH)

        finish_cond = _push_to_all(cond_ref, cond_all_ref, small_send.at[0], small_recv.at[0], small_local.at[0],
                                   wait=False)
        mine_ref[...] = w_ref[...].astype(BF16)
        finish_cond()
        local = pltpu.make_async_copy(mine_ref, block(me), local_sem)
        local.start()
        started = [copy(0, me, sibling, src=mine_ref), copy(1, me, (*xnb, c), src=mine_ref),
                   copy(2, me, (*ynb, c), src=mine_ref)]
        for cp in started:
            cp.start()
        mod = jnp.zeros((N_DEV, ada_cols), F32) + bcols_ref[...]
        for r in range(c_rows):
            cf = cond_all_ref[:, r, :]
            act = (cf * _sigmoid(cf)).astype(BF16)
            mod = mod + jnp.dot(act, wada_ref[r * LANES:(r + 1) * LANES, :].astype(BF16),
                                preferred_element_type=F32)
        mod_ref[...] = mod
        finish_mod = _push_to_all(mod_ref, mod_all_ref, small_send.at[1], small_recv.at[1], small_local.at[1],
                                  wait=False)
        for o_ref, c_ref in zip(other_refs, cast_refs):
            c_ref[...] = o_ref[...].astype(BF16)
        def start(cp):
            cp.start()
            started.append(cp)

        copy(1, (*xnb, c), me).wait_recv()
        start(copy(3, (*xnb, c), (*ynb, c), part=0))
        start(copy(5, (*xnb, c), sibling))
        copy(2, (*ynb, c), me).wait_recv()
        start(copy(4, (*ynb, c), (*xnb, c), part=1))
        start(copy(6, (*ynb, c), sibling))
        copy(3, (*diag, c), me, part=0).wait_recv()
        start(copy(7, (*diag, c), sibling, part=0))
        copy(4, (*diag, c), me, part=1).wait_recv()
        start(copy(8, (*diag, c), sibling, part=1))
        copy(0, sibling, me).wait_recv()
        copy(5, (*xnb, 1 - c), me).wait_recv()
        copy(6, (*ynb, 1 - c), me).wait_recv()
        copy(7, (*diag, 1 - c), me, part=0).wait_recv()
        copy(8, (*diag, 1 - c), me, part=1).wait_recv()
        finish_mod()
        for cp in started:
            cp.wait_send()
        local.wait()

    vmem = pl.BlockSpec(memory_space=pltpu.VMEM)
    outs = pl.pallas_call(
        body, name="gather_w_in",
        out_shape=[jax.ShapeDtypeStruct((N_DEV,) + shard.shape, BF16)]
        + [jax.ShapeDtypeStruct(o.shape, BF16) for o in others]
        + [jax.ShapeDtypeStruct((N_DEV,) + cond_rows.shape, F32), jax.ShapeDtypeStruct((N_DEV, N_DEV, ada_cols), F32)],
        in_specs=[vmem] * (4 + n),
        out_specs=[ANY] + [vmem] * (n + 2),
        scratch_shapes=[pltpu.VMEM(shard.shape, BF16), pltpu.VMEM((N_DEV, ada_cols), F32),
                        pltpu.SemaphoreType.DMA((9,)), pltpu.SemaphoreType.DMA((9,)),
                        pltpu.SemaphoreType.DMA,
                        pltpu.SemaphoreType.DMA((2, N_DEV - 1)), pltpu.SemaphoreType.DMA((2, N_DEV - 1)),
                        pltpu.SemaphoreType.DMA((2,))],
        compiler_params=pltpu.CompilerParams(vmem_limit_bytes=VMEM_LIMIT),
    )(shard, *others, cond_rows, w_ada, b_cols)
    return outs[0], list(outs[1:1 + n]), outs[1 + n], outs[2 + n]


def _carry(rider, name):
    def body(token_ref):
        token_ref[...] = jnp.zeros_like(token_ref)

    _, routs = _call(body, name, (1,), [], [], [jax.ShapeDtypeStruct((SUBLANES, LANES), F32)],
                     [_full((SUBLANES, LANES))], rider=rider)
    return routs


def _ada_weight_grad(c_all, dmod_cols):
    cols = dmod_cols.shape[1]

    def body(c_ref, d_ref, out_ref):
        cf = c_ref[...]
        act = (cf * _sigmoid(cf)).astype(BF16)
        out_ref[...] = lax.dot_general(act, d_ref[...].astype(BF16), TN_DIMS, preferred_element_type=F32)

    return pl.pallas_call(
        body, name="ada_weight_grad",
        out_shape=jax.ShapeDtypeStruct((D_MODEL, cols), F32),
        in_specs=[pl.BlockSpec(memory_space=pltpu.VMEM)] * 2,
        out_specs=pl.BlockSpec(memory_space=pltpu.VMEM),
        compiler_params=pltpu.CompilerParams(vmem_limit_bytes=VMEM_LIMIT),
    )(c_all, dmod_cols)


PACK_ROWS = 24
PACK_DMOD = 0
PACK_PARAMS = {"g_mix": (6, D_MODEL), "b_in": (7, IN_WIDTH), "g_ffn": (14, D_MODEL), "g_final": (15, D_MODEL),
               "sinks": (19, N_Q_HEADS)}
PACK_CONV = 16
PACK_SQERR = 20


def _small_finalize(packed_all, params):
    names = ["b_ada"] + list(PACK_PARAMS)
    layout = dict(PACK_PARAMS, b_ada=(PACK_DMOD, N_MOD * D_MODEL))
    n = len(names)

    def body(*refs):
        p_ref = refs[0]
        ins = refs[1:1 + 3 * n]
        outs = refs[1 + 3 * n:1 + 7 * n]
        conv_ref, loss_ref = refs[1 + 7 * n:]
        total = p_ref[0]
        for d in range(1, N_DEV):
            total = total + p_ref[d]
        for k, name in enumerate(names):
            row0, width = layout[name]
            w_ref, m_ref, v_ref = ins[3 * k:3 * k + 3]
            g_ref, d_ref, nm_ref, nv_ref = outs[4 * k:4 * k + 4]
            for chunk in range(-(-width // D_MODEL)):
                lo = chunk * D_MODEL
                hi = min(lo + D_MODEL, width)
                g = total[row0 + chunk:row0 + chunk + 1, :hi - lo]
                g_ref[:, lo:hi] = g
                d_ref[:, lo:hi], nm_ref[:, lo:hi], nv_ref[:, lo:hi] = _adamw_update(
                    w_ref[:, lo:hi], g, m_ref[:, lo:hi], v_ref[:, lo:hi])
        conv_ref[...] = total[PACK_CONV:PACK_CONV + 3, :]
        loss_ref[...] = (0.5 / D_MODEL) * jnp.sum(total[PACK_SQERR:PACK_SQERR + 1, :], keepdims=True)

    vmem = pl.BlockSpec(memory_space=pltpu.VMEM)
    flat = [a for name in names for a in params[name]]
    out_shape = [jax.ShapeDtypeStruct(params[name][0].shape, F32) for name in names for _ in range(4)]
    outs = pl.pallas_call(
        body, name="small_finalize",
        out_shape=out_shape + [jax.ShapeDtypeStruct((3, D_MODEL), F32), jax.ShapeDtypeStruct((1, 1), F32)],
        in_specs=[vmem] * (1 + 3 * n),
        out_specs=[vmem] * (4 * n + 2),
        compiler_params=pltpu.CompilerParams(vmem_limit_bytes=VMEM_LIMIT),
    )(packed_all, *flat)
    return {name: tuple(outs[4 * k:4 * k + 4]) for k, name in enumerate(names)}, outs[4 * n], outs[4 * n + 1]


def _row_tile(rows, multiple):
    for cand in range(min(rows, 256), 0, -1):
        if rows % cand == 0 and cand % multiple == 0:
            return cand
    return rows


def _adamw_update(w, g, m, v):
    c1 = 1.0 / (1.0 - ADAM_B1 ** ADAM_STEP)
    c2 = 1.0 / (1.0 - ADAM_B2 ** ADAM_STEP)
    nm = ADAM_B1 * m + (1.0 - ADAM_B1) * g
    nv = ADAM_B2 * v + (1.0 - ADAM_B2) * (g * g)
    delta = -ADAM_LR * ((nm * c1) / (jnp.sqrt(nv * c2) + ADAM_EPS) + ADAM_WD * w)
    return delta, nm, nv


def _adamw(w, g, m, v, name):
    rows, cols = w.shape
    tile = _row_tile(rows, SUBLANES)

    def body(w_ref, g_ref, m_ref, v_ref, d_ref, nm_ref, nv_ref):
        d_ref[...], nm_ref[...], nv_ref[...] = _adamw_update(w_ref[...], g_ref[...], m_ref[...], v_ref[...])

    spec = pl.BlockSpec((tile, cols), lambda i: (i, 0))
    outs, _ = _call(body, name, (rows // tile,), [w, g, m, v], [spec] * 4,
                    [jax.ShapeDtypeStruct((rows, cols), F32)] * 3, [spec] * 3)
    return outs


def _sibling_sum(gblocks, sib, name):
    _, r, cdim = gblocks.shape
    tile = _row_tile(r, BF16_ROWS)
    x, y, c = _my_place()
    table = jnp.stack([4 * _flip(x, fx) + 2 * _flip(y, fy) + c for fx, fy in CHIP_FLIPS]).astype(jnp.int32)

    def body(table_ref, own0, own1, own2, own3, sib_ref, sums_ref, mine_ref):
        mine_ref[...] = own0[...].astype(F32) + sib_ref[0].astype(F32)
        for f, own in ((1, own1), (2, own2), (3, own3)):
            sums_ref[f - 1] = (own[...].astype(F32) + sib_ref[f].astype(F32)).astype(BF16)

    own_specs = [pl.BlockSpec((None, tile, cdim), functools.partial(lambda i, tab, f: (tab[f], i, 0), f=f))
                 for f in range(4)]
    return pl.pallas_call(
        body, name=name,
        grid_spec=pltpu.PrefetchScalarGridSpec(
            num_scalar_prefetch=1, grid=(r // tile,),
            in_specs=own_specs + [pl.BlockSpec((4, tile, cdim), lambda i, tab: (0, i, 0))],
            out_specs=[pl.BlockSpec((3, tile, cdim), lambda i, tab: (0, i, 0)),
                       pl.BlockSpec((tile, cdim), lambda i, tab: (i, 0))]),
        out_shape=[jax.ShapeDtypeStruct((3, r, cdim), BF16), jax.ShapeDtypeStruct((r, cdim), F32)],
        compiler_params=pltpu.CompilerParams(dimension_semantics=("arbitrary",), vmem_limit_bytes=VMEM_LIMIT),
    )(table, gblocks, gblocks, gblocks, gblocks, sib)


def _chip_sum_adamw(mine, ici, w, m, v, name):
    r, cdim = mine.shape
    tile = _row_tile(r, BF16_ROWS)

    def body(mine_ref, ici_ref, w_ref, m_ref, v_ref, g_ref, d_ref, nm_ref, nv_ref):
        g = mine_ref[...]
        for f in range(3):
            g = g + ici_ref[f].astype(F32)
        g_ref[...] = g
        d_ref[...], nm_ref[...], nv_ref[...] = _adamw_update(w_ref[...], g, m_ref[...], v_ref[...])

    spec = pl.BlockSpec((tile, cdim), lambda i: (i, 0))
    outs, _ = _call(
        body, name, (r // tile,), [mine, ici, w, m, v],
        [spec, pl.BlockSpec((3, tile, cdim), lambda i: (0, i, 0)), spec, spec, spec],
        [jax.ShapeDtypeStruct((r, cdim), F32)] * 4, [spec] * 4)
    return outs


REF_KV_COL = D_MODEL
REF_REST_COL = D_MODEL + 2 * KV_WIDTH
IN_CHUNK = 1280
IN_PIECES = ([(0, 0, D_MODEL)]
             + [(D_MODEL + n * IN_CHUNK, REF_REST_COL + n * IN_CHUNK, IN_CHUNK) for n in range(REST_WIDTH // IN_CHUNK)]
             + [(KV_COL, REF_KV_COL, 2 * KV_WIDTH)])


def _inproj_fwd(x, vec, w_t, b_in, rider):
    t = x.shape[0]
    tm = min(TOKEN_TILE, t)

    def body(x_ref, vec_ref, w_ref, b_ref, z_ref, h_ref):
        xf = x_ref[...]
        r = lax.rsqrt(jnp.mean(xf * xf, axis=-1, keepdims=True) + EPS)
        h = (xf * r) * (vec_ref[0:1, :] * (1.0 + vec_ref[1:2, :])) + vec_ref[2:3, :]
        hb = h.astype(BF16)
        h_ref[...] = hb
        for mine, ref, width in IN_PIECES:
            zc = lax.dot_general(hb, w_ref[ref:ref + width, :], NT_DIMS, preferred_element_type=F32)
            z_ref[:, mine:mine + width] = (zc + b_ref[:, ref:ref + width]).astype(BF16)

    return _call(
        body, "inproj_fwd", (t // tm,), [x, vec, w_t, b_in],
        [pl.BlockSpec((tm, D_MODEL), lambda i: (i, 0)), _full((SUBLANES, D_MODEL)),
         _full((IN_WIDTH, D_MODEL)), _full((1, IN_WIDTH))],
        [jax.ShapeDtypeStruct((t, IN_WIDTH), BF16), jax.ShapeDtypeStruct((t, D_MODEL), BF16)],
        [pl.BlockSpec((tm, IN_WIDTH), lambda i: (i, 0)), pl.BlockSpec((tm, D_MODEL), lambda i: (i, 0))],
        rider=rider)


def _window_mask(has_prev):
    qi = lax.broadcasted_iota(jnp.int32, (WINDOW, 2 * WINDOW), 0)
    kj = lax.broadcasted_iota(jnp.int32, (WINDOW, 2 * WINDOW), 1)
    off = jnp.where(has_prev, 0, 4 * WINDOW)
    in_prev = jnp.logical_and(kj < WINDOW, kj > qi + off)
    in_cur = jnp.logical_and(kj >= WINDOW, (kj - WINDOW) <= qi)
    return jnp.logical_or(in_prev, in_cur)


PAIRS = GROUP // 2
STACK = PAIRS * WINDOW


ATTN_BLOCKS = 4
ATTN_BWD_BLOCKS = 1
LOG2E = 1.4426950408889634
LN2 = 0.6931471805599453
SCORE_SCALE = ATTN_SCALE * LOG2E


def _fill_window_bias(bias_ref):
    shape = bias_ref.shape[1:]
    kj = lax.broadcasted_iota(jnp.int32, shape, 0)
    qi = jnp.bitwise_and(lax.broadcasted_iota(jnp.int32, shape, 1), WINDOW - 1)
    in_prev = jnp.logical_and(kj < WINDOW, kj > qi)
    in_cur = jnp.logical_and(kj >= WINDOW, (kj - WINDOW) <= qi)
    bias_ref[0] = jnp.where(in_cur, 0.0, -jnp.inf)
    bias_ref[1] = jnp.where(jnp.logical_or(in_prev, in_cur), 0.0, -jnp.inf)


def _half_tiles(tile):
    low = lax.broadcasted_iota(jnp.int32, tile.shape, 1) < HEAD_DIM
    swapped = jnp.concatenate([tile[:, HEAD_DIM:], tile[:, :HEAD_DIM]], axis=1)
    zero = jnp.zeros_like(tile)
    return ((jnp.where(low, tile, zero), jnp.where(low, zero, swapped)),
            (jnp.where(low, swapped, zero), jnp.where(low, zero, tile)))


def _stack_pairs(ref, row0, j):
    return jnp.concatenate(
        [ref[pl.ds(row0, WINDOW), (j * PAIRS + p) * LANES:(j * PAIRS + p + 1) * LANES] for p in range(PAIRS)], axis=0)


def _per_pair_row(values):
    pair = lax.broadcasted_iota(jnp.int32, (1, STACK), 1) // WINDOW
    row = jnp.full((1, STACK), values[PAIRS - 1], F32)
    for p in range(PAIRS - 2, -1, -1):
        row = jnp.where(pair == p, values[p], row)
    return row


def _attn_fwd(z, sinks, rider):
    t = z.shape[0]
    tq = min(TOKEN_TILE, t)
    nblk = tq // WINDOW

    def body(q_ref, kv_ref, sink_ref, o_ref, lse_ref, bias_ref):
        i = pl.program_id(0)

        @pl.when(i == 0)
        def _():
            _fill_window_bias(bias_ref)

        def window(b):
            row0 = pl.multiple_of(b * WINDOW, WINDOW)
            start = i * tq + b * WINDOW
            prev = pl.multiple_of(jnp.maximum(start - WINDOW, 0), WINDOW)
            cur = pl.multiple_of(start, WINDOW)
            kvw = jnp.concatenate([kv_ref[pl.ds(prev, WINDOW), :], kv_ref[pl.ds(cur, WINDOW), :]], axis=0)
            return row0, _half_tiles(kvw[:, :KV_WIDTH]), _half_tiles(kvw[:, KV_WIDTH:]), bias_ref[jnp.minimum(start, 1)]

        def block_group(bb, carry):
            windows = [window(bb * ATTN_BLOCKS + n) for n in range(ATTN_BLOCKS)]
            for j in range(N_KV_HEADS):
                for pr in range(PAIRS):
                    cols = slice((j * PAIRS + pr) * LANES, (j * PAIRS + pr + 1) * LANES)
                    o_ts = [jnp.zeros((LANES, WINDOW), F32) for _ in windows]
                    for parity in range(2):
                        h = j * GROUP + 2 * pr + parity
                        sink = sink_ref[h] * LOG2E
                        for n, (row0, k_halves, v_halves, bias) in enumerate(windows):
                            qp = q_ref[pl.ds(row0, WINDOW), cols]
                            s = lax.dot_general(k_halves[j][parity], qp, NT_DIMS, preferred_element_type=F32)
                            s = s * SCORE_SCALE + bias
                            m = jnp.maximum(jnp.max(s, axis=0, keepdims=True), sink)
                            p = jnp.exp2(s - m)
                            denom = jnp.sum(p, axis=0, keepdims=True) + jnp.exp2(sink - m)
                            pv = lax.dot_general(v_halves[j][parity], p.astype(BF16), TN_DIMS,
                                                 preferred_element_type=F32)
                            o_ts[n] = o_ts[n] + pv * (1.0 / denom)
                            lse_ref[h:h + 1, pl.ds(row0, WINDOW)] = m + jnp.log2(denom)
                    for n, (row0, _, _, _) in enumerate(windows):
                        o_ref[pl.ds(row0, WINDOW), cols] = jnp.transpose(o_ts[n].astype(BF16))
            return carry

        lax.fori_loop(0, nblk // ATTN_BLOCKS, block_group, 0)

    return _call(
        body, "attn_fwd", (t // tq,), [z, z, sinks],
        [pl.BlockSpec((tq, D_MODEL), lambda i: (i, 0)),
         pl.BlockSpec((t, 2 * KV_WIDTH), lambda i: (0, KV_COL // (2 * KV_WIDTH))),
         pl.BlockSpec(memory_space=pltpu.SMEM)],
        [jax.ShapeDtypeStruct((t, D_MODEL), BF16), jax.ShapeDtypeStruct((N_Q_HEADS, t), F32)],
        [pl.BlockSpec((tq, D_MODEL), lambda i: (i, 0)), pl.BlockSpec((N_Q_HEADS, tq), lambda i: (0, i))],
        scratch=[pltpu.VMEM((2, 2 * WINDOW, WINDOW), F32)], rider=rider)


HALO = BF16_ROWS


def _shift_down(u, uh, k):
    rolled = pltpu.roll(u, k, 0)
    row = lax.broadcasted_iota(jnp.int32, (SUBLANES, u.shape[1]), 0)
    top = rolled[:SUBLANES, :]
    for j in range(k):
        top = jnp.where(row == j, uh[HALO - k + j:HALO - k + j + 1, :], top)
    return jnp.concatenate([top, rolled[SUBLANES:, :]], axis=0)


def _shift_up(u, nxt, k):
    n = u.shape[0]
    rolled = pltpu.roll(u, n - k, 0)
    row = lax.broadcasted_iota(jnp.int32, (SUBLANES, u.shape[1]), 0)
    bottom = rolled[n - SUBLANES:, :]
    for j in range(k):
        bottom = jnp.where(row == SUBLANES - k + j, nxt[j:j + 1, :], bottom)
    return jnp.concatenate([rolled[:n - SUBLANES, :], bottom], axis=0)


def _conv_inputs(cc_ref, cx_ref, hc_ref, hx_ref, first_tile):
    cc = cc_ref[...].astype(F32)
    cx = cx_ref[...].astype(F32)
    u = cc * cx
    uh = jnp.where(first_tile, 0.0, hc_ref[...].astype(F32) * hx_ref[...].astype(F32))
    return cc, cx, u, _shift_down(u, uh, 1), _shift_down(u, uh, 2)


def _z_specs(tm, order):
    per_tile = tm // HALO
    cols = [pl.BlockSpec((tm, D_MODEL), functools.partial(lambda i, j: (order(i), j), j=j)) for j in range(1, 6)]
    halos = [pl.BlockSpec((HALO, D_MODEL),
                          functools.partial(lambda i, j: (jnp.maximum(order(i) * per_tile - 1, 0), j), j=j))
             for j in (2, 3)]
    return cols + halos


def _mix_fwd(x, attn, z, vec, w_out):
    t = x.shape[0]
    tm = min(TOKEN_TILE, t)

    def body(x_ref, a_ref, cb_ref, cc_ref, cx_ref, ga_ref, gc_ref, hc_ref, hx_ref, vec_ref, w_ref,
             m_ref, x2_ref, h2_ref, o_ref):
        i = pl.program_id(0)
        _, _, u, u1, u2 = _conv_inputs(cc_ref, cx_ref, hc_ref, hx_ref, i == 0)
        cv = vec_ref[4:5, :] * u2 + vec_ref[5:6, :] * u1 + vec_ref[6:7, :] * u
        conv = cb_ref[...].astype(F32) * cv
        merged = (_sigmoid(ga_ref[...].astype(F32)) * a_ref[...].astype(F32)
                  + _sigmoid(gc_ref[...].astype(F32)) * conv)
        mb = merged.astype(BF16)
        m_ref[...] = mb
        o = jnp.dot(mb, w_ref[...], preferred_element_type=F32)
        o_ref[...] = o.astype(BF16)
        x2 = x_ref[...] + vec_ref[0:1, :] * o
        x2_ref[...] = x2
        r = lax.rsqrt(jnp.mean(x2 * x2, axis=-1, keepdims=True) + EPS)
        h2 = (x2 * r) * (vec_ref[1:2, :] * (1.0 + vec_ref[2:3, :])) + vec_ref[3:4, :]
        h2_ref[...] = h2.astype(BF16)

    tok = pl.BlockSpec((tm, D_MODEL), lambda i: (i, 0))
    outs, _ = _call(
        body, "mix_fwd", (t // tm,), [x, attn, z, z, z, z, z, z, z, vec, w_out],
        [tok, tok] + _z_specs(tm, lambda i: i) + [_full((SUBLANES, D_MODEL)), _full((D_MODEL, D_MODEL))],
        [jax.ShapeDtypeStruct((t, D_MODEL), BF16), jax.ShapeDtypeStruct((t, D_MODEL), F32),
         jax.ShapeDtypeStruct((t, D_MODEL), BF16), jax.ShapeDtypeStruct((t, D_MODEL), BF16)],
        [tok, tok, tok, tok])
    return outs


def _ffn_fwd(h2, w_t):
    t = h2.shape[0]
    tm = min(TOKEN_TILE, t)

    def body(h_ref, w_ref, gu_ref, a_ref):
        hb = h_ref[...]
        for n in range(D_FF // FF_CHUNK):
            lo, hi = n * FF_CHUNK, (n + 1) * FF_CHUNK
            g = lax.dot_general(hb, w_ref[lo:hi, :], NT_DIMS, preferred_element_type=F32)
            u = lax.dot_general(hb, w_ref[D_FF + lo:D_FF + hi, :], NT_DIMS, preferred_element_type=F32)
            sg = _sigmoid(g)
            silu = g * sg
            gu_ref[:, lo:hi] = (u * (sg * (1.0 + g * (1.0 - sg)))).astype(BF16)
            gu_ref[:, D_FF + lo:D_FF + hi] = silu.astype(BF16)
            a_ref[:, lo:hi] = (silu * u).astype(BF16)

    outs, _ = _call(
        body, "ffn_fwd", (t // tm,), [h2, w_t],
        [pl.BlockSpec((tm, D_MODEL), lambda i: (i, 0)), _full((2 * D_FF, D_MODEL))],
        [jax.ShapeDtypeStruct((t, 2 * D_FF), BF16), jax.ShapeDtypeStruct((t, D_FF), BF16)],
        [pl.BlockSpec((tm, 2 * D_FF), lambda i: (i, 0)), pl.BlockSpec((tm, D_FF), lambda i: (i, 0))])
    return outs


def _ffn_out_loss(a, gu, x2, target, vec, w_ffn_out):
    t = a.shape[0]
    tm = min(TOKEN_TILE, t)

    def body(a_ref, gu_ref, x2_ref, t_ref, vec_ref, w_ref, dx3_ref, df_ref, dgu_ref, acc_ref):
        @pl.when(pl.program_id(0) == 0)
        def _():
            acc_ref[...] = jnp.zeros_like(acc_ref)

        ga2 = vec_ref[0:1, :]
        gf = vec_ref[1:2, :]
        parts = min(ROW_PARTS, tm // LANES)
        part_rows = [slice(n * (tm // parts), (n + 1) * (tm // parts)) for n in range(parts)]

        def head(rows, f):
            x3 = x2_ref[rows, :] + ga2 * f
            r = lax.rsqrt(jnp.mean(x3 * x3, axis=-1, keepdims=True) + EPS)
            xn = x3 * r
            err = xn * gf - t_ref[rows, :]
            dxn = err * (gf * (1.0 / D_MODEL))
            dx3 = r * (dxn - xn * jnp.mean(dxn * xn, axis=-1, keepdims=True))
            dx3_ref[rows, :] = dx3.astype(GRAD_STREAM)
            sums = (jnp.sum(err * err, axis=0, keepdims=True),
                    jnp.sum(err * xn, axis=0, keepdims=True) * (1.0 / D_MODEL),
                    jnp.sum(dx3 * f, axis=0, keepdims=True))
            df = (dx3 * ga2).astype(BF16)
            df_ref[rows, :] = df
            return df, sums

        def tail(rows, df):
            for n in range(D_FF // FF_CHUNK):
                lo, hi = n * FF_CHUNK, (n + 1) * FF_CHUNK
                da = lax.dot_general(df, w_ref[lo:hi, :], NT_DIMS, preferred_element_type=F32)
                dgu_ref[rows, lo:hi] = (da * gu_ref[rows, lo:hi].astype(F32)).astype(BF16)
                dgu_ref[rows, D_FF + lo:D_FF + hi] = (da * gu_ref[rows, D_FF + lo:D_FF + hi].astype(F32)).astype(BF16)

        fs = [jnp.dot(a_ref[rows, :], w_ref[...], preferred_element_type=F32) for rows in part_rows]
        heads = [head(rows, f) for rows, f in zip(part_rows, fs)]
        for rows, (df, _) in zip(part_rows, heads):
            tail(rows, df)
        for k in range(3):
            total = heads[0][1][k]
            for _, sums in heads[1:]:
                total = total + sums[k]
            acc_ref[k:k + 1, :] += total

    tok = pl.BlockSpec((tm, D_MODEL), lambda i: (i, 0))
    outs, _ = _call(
        body, "ffn_out_loss", (t // tm,), [a, gu, x2, target, vec, w_ffn_out],
        [pl.BlockSpec((tm, D_FF), lambda i: (i, 0)), pl.BlockSpec((tm, 2 * D_FF), lambda i: (i, 0)),
         tok, tok, _full((SUBLANES, D_MODEL)), _full((D_FF, D_MODEL))],
        [jax.ShapeDtypeStruct((t, D_MODEL), GRAD_STREAM), jax.ShapeDtypeStruct((t, D_MODEL), BF16),
         jax.ShapeDtypeStruct((t, 2 * D_FF), BF16), jax.ShapeDtypeStruct((SUBLANES, D_MODEL), F32)],
        [tok, tok, pl.BlockSpec((tm, 2 * D_FF), lambda i: (i, 0)), _full((SUBLANES, D_MODEL))])
    return outs


def _ffn_in_bwd(dgu, x2, dx3, vec, w_t, rider):
    t = x2.shape[0]
    tm = min(TOKEN_TILE, t)

    def body(dgu_ref, x2_ref, dx3_ref, vec_ref, wf_ref, dx2_ref, acc_ref):
        @pl.when(pl.program_id(0) == 0)
        def _():
            acc_ref[...] = jnp.zeros_like(acc_ref)

        gffn = vec_ref[0:1, :]
        sc2 = vec_ref[1:2, :]
        parts = min(ROW_PARTS, tm // LANES)
        part_rows = [slice(n * (tm // parts), (n + 1) * (tm // parts)) for n in range(parts)]
        dhs = [jnp.dot(dgu_ref[rows, :], wf_ref[...], preferred_element_type=F32) for rows in part_rows]
        gs = gffn * (1.0 + sc2)
        sum_dh = jnp.zeros((1, D_MODEL), F32)
        sum_dh_xn = jnp.zeros((1, D_MODEL), F32)
        for rows, dh2 in zip(part_rows, dhs):
            x2 = x2_ref[rows, :]
            r = lax.rsqrt(jnp.mean(x2 * x2, axis=-1, keepdims=True) + EPS)
            xn = x2 * r
            dh_xn = dh2 * xn
            sum_dh = sum_dh + jnp.sum(dh2, axis=0, keepdims=True)
            sum_dh_xn = sum_dh_xn + jnp.sum(dh_xn, axis=0, keepdims=True)
            dx2 = dx3_ref[rows, :].astype(F32) + r * (dh2 * gs - xn * jnp.mean(dh_xn * gs, axis=-1, keepdims=True))
            dx2_ref[rows, :] = dx2.astype(GRAD_STREAM)
        acc_ref[0:1, :] += sum_dh
        acc_ref[1:2, :] += sum_dh_xn * gffn
        acc_ref[2:3, :] += sum_dh_xn * (1.0 + sc2)

    tok = pl.BlockSpec((tm, D_MODEL), lambda i: (i, 0))
    return _call(
        body, "ffn_in_bwd", (t // tm,), [dgu, x2, dx3, vec, w_t],
        [pl.BlockSpec((tm, 2 * D_FF), lambda i: (i, 0)), tok, tok, _full((SUBLANES, D_MODEL)),
         _full((2 * D_FF, D_MODEL))],
        [jax.ShapeDtypeStruct((t, D_MODEL), GRAD_STREAM), jax.ShapeDtypeStruct((SUBLANES, D_MODEL), F32)],
        [tok, _full((SUBLANES, D_MODEL))], rider=rider)


def _mix_bwd(dx2, oproj, attn, z, vec, w_out, rider):
    t = dx2.shape[0]
    tm = min(TOKEN_TILE, t)
    nt = t // tm
    rev = lambda i: nt - 1 - i

    def body(dx2_ref, m_ref, a_ref, cb_ref, cc_ref, cx_ref, ga_ref, gc_ref, hc_ref, hx_ref,
             vec_ref, wo_ref, do_ref, da_ref, dr_ref, acc_ref, carry_ref):
        i = pl.program_id(0)

        @pl.when(i == 0)
        def _():
            acc_ref[...] = jnp.zeros_like(acc_ref)
            carry_ref[...] = jnp.zeros_like(carry_ref)

        ga1 = vec_ref[0:1, :]
        w0, w1, w2 = vec_ref[1:2, :], vec_ref[2:3, :], vec_ref[3:4, :]
        dx2 = dx2_ref[...].astype(F32)
        acc_ref[0:1, :] += jnp.sum(dx2 * m_ref[...].astype(F32), axis=0, keepdims=True)
        do = (dx2 * ga1).astype(BF16)
        do_ref[...] = do
        dm = lax.dot_general(do, wo_ref[...], NT_DIMS, preferred_element_type=F32)

        cc, cx, u, u1, u2 = _conv_inputs(cc_ref, cx_ref, hc_ref, hx_ref, i == nt - 1)
        cv = w0 * u2 + w1 * u1 + w2 * u
        cb = cb_ref[...].astype(F32)
        sa = _sigmoid(ga_ref[...].astype(F32))
        sc = _sigmoid(gc_ref[...].astype(F32))
        attn = a_ref[...].astype(F32)
        dattn = dm * sa
        da_ref[...] = dattn.astype(BF16)
        dconv = dm * sc
        dconv_b = dconv * cv
        dr_ref[:, 3 * D_MODEL:4 * D_MODEL] = (dattn * attn * (1.0 - sa)).astype(BF16)
        dr_ref[:, 4 * D_MODEL:5 * D_MODEL] = (dconv_b * cb * (1.0 - sc)).astype(BF16)
        dr_ref[:, 0:D_MODEL] = dconv_b.astype(BF16)
        dcv = dconv * cb
        acc_ref[1:2, :] += jnp.sum(dcv * u2, axis=0, keepdims=True)
        acc_ref[2:3, :] += jnp.sum(dcv * u1, axis=0, keepdims=True)
        acc_ref[3:4, :] += jnp.sum(dcv * u, axis=0, keepdims=True)
        nxt = carry_ref[...]
        du = w2 * dcv + w1 * _shift_up(dcv, nxt, 1) + w0 * _shift_up(dcv, nxt, 2)
        carry_ref[...] = dcv[0:SUBLANES, :]
        dr_ref[:, D_MODEL:2 * D_MODEL] = (du * cx).astype(BF16)
        dr_ref[:, 2 * D_MODEL:3 * D_MODEL] = (du * cc).astype(BF16)

    tok = pl.BlockSpec((tm, D_MODEL), lambda i: (rev(i), 0))
    return _call(
        body, "mix_bwd", (nt,), [dx2, oproj, attn, z, z, z, z, z, z, z, vec, w_out],
        [tok, tok, tok] + _z_specs(tm, rev) + [_full((SUBLANES, D_MODEL)), _full((D_MODEL, D_MODEL))],
        [jax.ShapeDtypeStruct((t, D_MODEL), BF16), jax.ShapeDtypeStruct((t, D_MODEL), BF16),
         jax.ShapeDtypeStruct((t, REST_WIDTH), BF16), jax.ShapeDtypeStruct((SUBLANES, D_MODEL), F32)],
        [tok, tok, pl.BlockSpec((tm, REST_WIDTH), lambda i: (rev(i), 0)), _full((SUBLANES, D_MODEL))],
        scratch=[pltpu.VMEM((SUBLANES, D_MODEL), F32)], rider=rider)


def _attn_bwd(z, dattn, attn, lse, sinks, rider):
    t = z.shape[0]
    tq = min(TOKEN_TILE, t)
    nblk = tq // WINDOW
    nt = t // tq

    def body(q_ref, kv_ref, do_ref, o_ref, lse_ref, sink_ref, dq_ref, dkv_ref, ds_ref, acc_ref, bias_ref):
        i = pl.program_id(0)

        @pl.when(i == 0)
        def _():
            acc_ref[...] = jnp.zeros_like(acc_ref)
            ds_ref[...] = jnp.zeros_like(ds_ref)
            _fill_window_bias(bias_ref)

        lane = lax.broadcasted_iota(jnp.int32, (1, LANES), 1)
        ind_row = lax.broadcasted_iota(jnp.int32, (SUBLANES, LANES), 0)
        ind_low = lax.broadcasted_iota(jnp.int32, (SUBLANES, LANES), 1) < HEAD_DIM
        indicator = jnp.where(jnp.logical_or(jnp.logical_and(ind_row == 0, ind_low),
                                             jnp.logical_and(ind_row == 1, jnp.logical_not(ind_low))),
                              1.0, 0.0).astype(BF16)
        low = lax.broadcasted_iota(jnp.int32, (2 * WINDOW, LANES), 1) < HEAD_DIM

        def both_heads(even, odd):
            picked = jnp.where(low, even, odd)
            return picked + jnp.concatenate([picked[:, HEAD_DIM:], picked[:, :HEAD_DIM]], axis=1)

        def window(b):
            row0 = pl.multiple_of(b * WINDOW, WINDOW)
            start = i * tq + b * WINDOW
            prev = pl.multiple_of(jnp.maximum(start - WINDOW, 0), WINDOW)
            cur = pl.multiple_of(start, WINDOW)
            kvw = jnp.concatenate([kv_ref[pl.ds(prev, WINDOW), :], kv_ref[pl.ds(cur, WINDOW), :]], axis=0)
            return (row0, prev, cur, _half_tiles(kvw[:, :KV_WIDTH]), _half_tiles(kvw[:, KV_WIDTH:]),
                    bias_ref[jnp.minimum(start, 1)])

        def block_group(bb, dsink):
            windows = [window(bb * ATTN_BWD_BLOCKS + n) for n in range(ATTN_BWD_BLOCKS)]
            dk_groups = [[] for _ in windows]
            dv_groups = [[] for _ in windows]
            for j in range(N_KV_HEADS):
                stacks, deltas, dq_ts = [], [], []
                for row0, _, _, _, _, _ in windows:
                    qst = _stack_pairs(q_ref, row0, j)
                    dost = _stack_pairs(do_ref, row0, j)
                    prod = dost.astype(F32) * _stack_pairs(o_ref, row0, j).astype(F32)
                    prod_hi = prod.astype(BF16)
                    prod_lo = (prod - prod_hi.astype(F32)).astype(BF16)
                    stacks.append((qst, dost))
                    deltas.append(lax.dot_general(indicator, prod_hi, NT_DIMS, preferred_element_type=F32)
                                  + lax.dot_general(indicator, prod_lo, NT_DIMS, preferred_element_type=F32))
                    dq_ts.append(jnp.zeros((LANES, STACK), F32))
                dk_par = [[] for _ in windows]
                dv_par = [[] for _ in windows]
                for parity in range(2):
                    heads = [j * GROUP + 2 * p + parity for p in range(PAIRS)]
                    sink = _per_pair_row([sink_ref[h] * LOG2E for h in heads])
                    for n, (row0, _, _, k_halves, v_halves, bias) in enumerate(windows):
                        qst, dost = stacks[n]
                        kk, vv = k_halves[j][parity], v_halves[j][parity]
                        s = lax.dot_general(kk, qst, NT_DIMS, preferred_element_type=F32) * SCORE_SCALE + bias
                        lse = jnp.concatenate([lse_ref[h:h + 1, pl.ds(row0, WINDOW)] for h in heads], axis=1)
                        p = jnp.exp2(s - lse)
                        dp = lax.dot_general(vv, dost, NT_DIMS, preferred_element_type=F32)
                        delta = deltas[n][parity:parity + 1, :]
                        dsb = (p * (dp - delta)).astype(BF16)
                        dq_ts[n] = dq_ts[n] + lax.dot_general(kk, dsb, TN_DIMS, preferred_element_type=F32)
                        dk_par[n].append(jnp.dot(dsb, qst, preferred_element_type=F32))
                        dv_par[n].append(jnp.dot(p.astype(BF16), dost, preferred_element_type=F32))
                        weighted = jnp.exp2(sink - lse) * delta
                        for pr, h in enumerate(heads):
                            dsink = dsink - jnp.where(
                                lane == h, jnp.sum(weighted[:, pr * WINDOW:(pr + 1) * WINDOW]), 0.0)
                for n, (row0, _, _, _, _, _) in enumerate(windows):
                    dq_st = jnp.transpose((dq_ts[n] * ATTN_SCALE).astype(BF16))
                    for pr in range(PAIRS):
                        dq_ref[pl.ds(row0, WINDOW), (j * PAIRS + pr) * LANES:(j * PAIRS + pr + 1) * LANES] = (
                            dq_st[pr * WINDOW:(pr + 1) * WINDOW, :])
                    dk_groups[n].append(both_heads(dk_par[n][0], dk_par[n][1]))
                    dv_groups[n].append(both_heads(dv_par[n][0], dv_par[n][1]))
            for n, (_, prev, cur, _, _, _) in enumerate(windows):
                blk = jnp.concatenate([jnp.where(low, dk_groups[n][0], dk_groups[n][1]) * ATTN_SCALE,
                                       jnp.where(low, dv_groups[n][0], dv_groups[n][1])], axis=1)
                acc_ref[pl.ds(prev, WINDOW), :] += blk[:WINDOW, :]
                acc_ref[pl.ds(cur, WINDOW), :] += blk[WINDOW:, :]
            return dsink

        dsink = lax.fori_loop(0, nblk // ATTN_BWD_BLOCKS, block_group, jnp.zeros((1, LANES), F32))
        ds_ref[0:1, :] += dsink

        @pl.when(i == nt - 1)
        def _():
            dkv_ref[...] = acc_ref[...].astype(BF16)

    tok = pl.BlockSpec((tq, D_MODEL), lambda i: (i, 0))
    return _call(
        body, "attn_bwd", (nt,), [z, z, dattn, attn, lse, sinks],
        [tok, pl.BlockSpec((t, 2 * KV_WIDTH), lambda i: (0, KV_COL // (2 * KV_WIDTH))), tok, tok,
         pl.BlockSpec((N_Q_HEADS, tq), lambda i: (0, i)), pl.BlockSpec(memory_space=pltpu.SMEM)],
        [jax.ShapeDtypeStruct((t, D_MODEL), BF16), jax.ShapeDtypeStruct((t, 2 * KV_WIDTH), BF16),
         jax.ShapeDtypeStruct((SUBLANES, LANES), F32)],
        [tok, _full((t, 2 * KV_WIDTH)), _full((SUBLANES, LANES))],
        scratch=[pltpu.VMEM((t, 2 * KV_WIDTH), F32), pltpu.VMEM((2, 2 * WINDOW, STACK), F32)], rider=rider)


def _inproj_bwd(dq, drest, dkv, x, dx2, vec, w_t, rider):
    t = x.shape[0]
    tm = min(TOKEN_TILE, t)

    def body(dq_ref, dr_ref, dkv_ref, x_ref, dx2_ref, vec_ref, w_ref, gx_ref, acc_ref, db_ref):
        @pl.when(pl.program_id(0) == 0)
        def _():
            acc_ref[...] = jnp.zeros_like(acc_ref)
            db_ref[...] = jnp.zeros_like(db_ref)

        g = vec_ref[0:1, :]
        sc1 = vec_ref[1:2, :]
        dqb, drb, dkvb = dq_ref[...], dr_ref[...], dkv_ref[...]
        dh = jnp.dot(dqb, w_ref[:REF_KV_COL, :], preferred_element_type=F32)
        dh = dh + jnp.dot(drb, w_ref[REF_REST_COL:, :], preferred_element_type=F32)
        dh = dh + jnp.dot(dkvb, w_ref[REF_KV_COL:REF_REST_COL, :], preferred_element_type=F32)
        db_ref[:, :REF_KV_COL] += jnp.sum(dqb.astype(F32), axis=0, keepdims=True)
        db_ref[:, REF_REST_COL:] += jnp.sum(drb.astype(F32), axis=0, keepdims=True)
        db_ref[:, REF_KV_COL:REF_REST_COL] += jnp.sum(dkvb.astype(F32), axis=0, keepdims=True)
        xf = x_ref[...]
        r = lax.rsqrt(jnp.mean(xf * xf, axis=-1, keepdims=True) + EPS)
        xn = xf * r
        gs = g * (1.0 + sc1)
        dh_xn = dh * xn
        sum_dh_xn = jnp.sum(dh_xn, axis=0, keepdims=True)
        acc_ref[0:1, :] += jnp.sum(dh, axis=0, keepdims=True)
        acc_ref[1:2, :] += sum_dh_xn * g
        acc_ref[2:3, :] += sum_dh_xn * (1.0 + sc1)
        gx_ref[...] = dx2_ref[...].astype(F32) + r * (dh * gs - xn * jnp.mean(dh_xn * gs, axis=-1, keepdims=True))

    tok = pl.BlockSpec((tm, D_MODEL), lambda i: (i, 0))
    return _call(
        body, "inproj_bwd", (t // tm,), [dq, drest, dkv, x, dx2, vec, w_t],
        [tok, pl.BlockSpec((tm, REST_WIDTH), lambda i: (i, 0)),
         pl.BlockSpec((tm, 2 * KV_WIDTH), lambda i: (i, 0)), tok, tok,
         _full((SUBLANES, D_MODEL)), _full((IN_WIDTH, D_MODEL))],
        [jax.ShapeDtypeStruct((t, D_MODEL), F32), jax.ShapeDtypeStruct((SUBLANES, D_MODEL), F32),
         jax.ShapeDtypeStruct((1, IN_WIDTH), F32)],
        [tok, _full((SUBLANES, D_MODEL)), _full((1, IN_WIDTH))], rider=rider)


def _weight_grad(b, a, name, bn, rows=None, row0=0, into=None, rider=None):
    t, n = b.shape
    m = a.shape[1]
    rows = n if rows is None else rows
    tk = min(TOKEN_TILE, t)
    for cand in (4 * TOKEN_TILE, 2 * TOKEN_TILE):
        if t % cand == 0 and 2 * cand * (bn + m) * 2 + bn * m * 4 <= WGRAD_VMEM:
            tk = cand
            break
    nk = t // tk
    block0 = row0 // bn

    def body(b_ref, a_ref, *rest):
        out_ref, acc_ref = rest[-2:]
        k = pl.program_id(1)

        @pl.when(k == 0)
        def _():
            acc_ref[...] = jnp.zeros_like(acc_ref)

        acc_ref[...] += lax.dot_general(b_ref[...], a_ref[...], TN_DIMS, preferred_element_type=F32)

        @pl.when(k == nk - 1)
        def _():
            out_ref[...] = acc_ref[...].astype(BF16)

    outs, routs = _call(
        body, name, (n // bn, nk), [b, a] + ([] if into is None else [into]),
        [pl.BlockSpec((tk, bn), lambda j, k: (k, j)), pl.BlockSpec((tk, m), lambda j, k: (k, 0))]
        + ([] if into is None else [ANY]),
        [jax.ShapeDtypeStruct((rows, m), BF16)], [pl.BlockSpec((bn, m), lambda j, k: (block0 + j, 0))],
        scratch=[pltpu.VMEM((bn, m), F32)], rider=rider, aliases=None if into is None else {2: 0})
    return outs[0], routs


def _to_rows(v):
    n = v.shape[0]
    padded = -(-n // (SUBLANES * LANES)) * SUBLANES * LANES
    return jnp.pad(v, (0, padded - n)).reshape(padded // LANES, LANES)


def _vec_rows(*rows):
    stacked = jnp.concatenate([r.reshape(1, D_MODEL) for r in rows], axis=0)
    return jnp.pad(stacked, ((0, SUBLANES - len(rows)), (0, 0)))


def kernel(x, c, w_ada, b_ada, g_mix, w_in, b_in, sinks, conv_w, w_out, g_ffn, w_ffn_in, w_ffn_out, g_final, loss_target, m_w_ada, m_b_ada, m_g_mix, m_w_in, m_b_in, m_sinks, m_conv_w, m_w_out, m_g_ffn, m_w_ffn_in, m_w_ffn_out, m_g_final, v_w_ada, v_b_ada, v_g_mix, v_w_in, v_b_in, v_sinks, v_conv_w, v_w_out, v_g_ffn, v_w_ffn_in, v_w_ffn_out, v_g_final):
    ix, iy, ic = _my_place()
    me = 4 * ix + 2 * iy + ic
    xs = x[0]
    target = loss_target[0]
    ada_cols = w_ada.shape[2]
    conv_cols = conv_w.shape[2]

    wt_in, wt_fi = jnp.transpose(w_in[0]), jnp.transpose(w_ffn_in[0])
    b_cols = lax.dynamic_slice_in_dim(b_ada, me * ada_cols, ada_cols, axis=1)
    g_in, (cast_fi, cast_out, cast_fo), first, mod_all = _gather_first_weight(
        wt_in, [wt_fi, w_out[0], w_ffn_out[0]], _to_rows(jnp.concatenate([c[0], conv_w[0].reshape(-1)])),
        w_ada[0], b_cols)
    first = first.reshape(N_DEV, -1)
    c_all = first[:, :D_MODEL]
    conv_full = jnp.transpose(first[:, D_MODEL:D_MODEL + 3 * conv_cols].reshape(N_DEV, 3, conv_cols), (1, 0, 2))
    conv_full = conv_full.reshape(3, D_MODEL)
    mod = lax.dynamic_index_in_dim(mod_all, me, axis=1, keepdims=False).reshape(N_MOD, D_MODEL)
    sh1, sc1, ga1, sh2, sc2, ga2 = [mod[i:i + 1] for i in range(N_MOD)]
    w_in_t = g_in.reshape(IN_WIDTH, D_MODEL)
    (z, h1), (g_fi, g_out) = _inproj_fwd(xs, _vec_rows(g_mix, sc1, sh1), w_in_t, b_in,
                                         _gather_rider([cast_fi, cast_out]))
    w_fi_t = g_fi.reshape(2 * D_FF, D_MODEL)
    w_out_full = g_out.reshape(D_MODEL, D_MODEL)
    (attn, lse), (g_fo,) = _attn_fwd(z, sinks[0], _gather_rider([cast_fo]))
    w_fo_full = g_fo.reshape(D_FF, D_MODEL)
    merged, x2, h2, oproj = _mix_fwd(
        xs, attn, z, _vec_rows(ga1, g_ffn, sc2, sh2, conv_full[0], conv_full[1], conv_full[2]), w_out_full)
    gu, act = _ffn_fwd(h2, w_fi_t)
    dx3, df, dgu, acc_l = _ffn_out_loss(act, gu, x2, target, _vec_rows(ga2, g_final), w_fo_full)

    gw_fo, _ = _weight_grad(act, df, "wgrad_ffn_out", D_FF)
    gw_fi, _ = _weight_grad(dgu, h2, "wgrad_ffn_in", D_FF)
    blocks_fo = gw_fo.reshape(N_DEV, D_FF // N_DEV, D_MODEL)
    blocks_fi = gw_fi.reshape(N_DEV, 2 * D_FF // N_DEV, D_MODEL)
    (dx2, acc_f), (sib_fo, sib_fi) = _ffn_in_bwd(dgu, x2, dx3, _vec_rows(g_ffn, sc2), w_fi_t,
                                                 _sibling_rider([blocks_fo, blocks_fi]))
    sums_fo, mine_fo = _sibling_sum(blocks_fo, sib_fo, "sibling_sum_ffn_out")
    sums_fi, mine_fi = _sibling_sum(blocks_fi, sib_fi, "sibling_sum_ffn_in")
    (dout, dattn, drest, acc_m), (ici_fo, ici_fi) = _mix_bwd(
        dx2, oproj, attn, z, _vec_rows(ga1, conv_full[0], conv_full[1], conv_full[2]), w_out_full,
        _chip_rider([sums_fo, sums_fi]))
    gw_out, _ = _weight_grad(merged, dout, "wgrad_out", D_MODEL)
    blocks_out = gw_out.reshape(N_DEV, D_MODEL // N_DEV, D_MODEL)
    (dq, dkv, dsink), (sib_out,) = _attn_bwd(z, dattn, attn, lse, sinks[0], _sibling_rider([blocks_out]))
    sums_out, mine_out = _sibling_sum(blocks_out, sib_out, "sibling_sum_out")
    gw_in, (ici_out,) = _weight_grad(drest, h1, "wgrad_in_rest", IN_CHUNK, rows=IN_WIDTH, row0=REF_REST_COL,
                                     rider=_chip_rider([sums_out]))
    gw_in, _ = _weight_grad(dq, h1, "wgrad_in_q", D_MODEL, rows=IN_WIDTH, row0=0, into=gw_in)
    gw_in, _ = _weight_grad(dkv, h1, "wgrad_in_kv", 2 * KV_WIDTH, rows=IN_WIDTH, row0=REF_KV_COL, into=gw_in)
    blocks_in = gw_in.reshape(N_DEV, IN_WIDTH // N_DEV, D_MODEL)
    (sib_in,) = _carry(_sibling_rider([blocks_in]), "sibling_w_in")
    sums_in, mine_in = _sibling_sum(blocks_in, sib_in, "sibling_sum_in")
    (grad_x, acc_i, db_in), (ici_in,) = _inproj_bwd(dq, drest, dkv, xs, dx2, _vec_rows(g_mix, sc1), w_in_t,
                                                    _chip_rider([sums_in]))

    widen = lambda vec: jnp.pad(vec, (0, -vec.shape[0] % D_MODEL))
    packed = jnp.concatenate([
        acc_i[0], acc_i[1], acc_m[0], acc_f[0], acc_f[1], acc_l[2],
        acc_i[2], widen(db_in[0]), acc_f[2], acc_l[1],
        acc_m[1], acc_m[2], acc_m[3], widen(dsink[0]), acc_l[0],
        jnp.zeros(((PACK_ROWS - PACK_SQERR - 1) * D_MODEL,), F32)]).reshape(PACK_ROWS, D_MODEL)
    packed_all = _small_allgather(packed, "gather_small")
    dmod_all = packed_all[:, PACK_DMOD:PACK_DMOD + N_MOD, :].reshape(N_DEV, N_MOD * D_MODEL)
    dmod_cols = lax.dynamic_slice_in_dim(dmod_all, me * ada_cols, ada_cols, axis=1)
    g_w_ada = _ada_weight_grad(c_all, dmod_cols)
    row_of = lambda a: a.reshape(1, -1)
    small, g_conv_full, loss = _small_finalize(packed_all, {
        "b_ada": (b_ada, m_b_ada, v_b_ada), "g_mix": (g_mix, m_g_mix, v_g_mix), "b_in": (b_in, m_b_in, v_b_in),
        "g_ffn": (g_ffn, m_g_ffn, v_g_ffn), "sinks": (sinks, m_sinks, v_sinks),
        "g_final": (row_of(g_final), row_of(m_g_final), row_of(v_g_final))})
    small["g_final"] = tuple(o.reshape(g_final.shape) for o in small["g_final"])
    g_conv = lax.dynamic_slice_in_dim(g_conv_full, me * conv_cols, conv_cols, axis=1)
    d_conv, nm_conv, nv_conv = _adamw(conv_w[0], g_conv, m_conv_w[0], v_conv_w[0], "adamw_conv_w")
    small["conv_w"] = (g_conv[None], d_conv[None], nm_conv[None], nv_conv[None])

    def reduced(mine, ici, w, m, v, name, transposed=False):
        turn = jnp.transpose if transposed else (lambda a: a)
        return tuple(turn(o)[None] for o in _chip_sum_adamw(mine, ici, turn(w[0]), turn(m[0]), turn(v[0]), name))

    d_ada, nm_ada, nv_ada = _adamw(w_ada[0], g_w_ada, m_w_ada[0], v_w_ada[0], "adamw_w_ada")
    res = {
        "w_ada": (g_w_ada[None], d_ada[None], nm_ada[None], nv_ada[None]),
        "w_in": reduced(mine_in, ici_in, w_in, m_w_in, v_w_in, "adamw_w_in", transposed=True),
        "w_out": reduced(mine_out, ici_out, w_out, m_w_out, v_w_out, "adamw_w_out"),
        "w_ffn_in": reduced(mine_fi, ici_fi, w_ffn_in, m_w_ffn_in, v_w_ffn_in, "adamw_w_ffn_in", transposed=True),
        "w_ffn_out": reduced(mine_fo, ici_fo, w_ffn_out, m_w_ffn_out, v_w_ffn_out, "adamw_w_ffn_out"),
    }
    res.update(small)
    order = ["w_ada", "b_ada", "g_mix", "w_in", "b_in", "sinks", "conv_w", "w_out", "g_ffn", "w_ffn_in", "w_ffn_out",
             "g_final"]
    outs = [loss.reshape(()), grad_x[None]]
    for k in range(4):
        outs += [res[n][k] for n in order]
    return tuple(outs)
```

```python
import functools
import math

import jax
import jax.numpy as jnp
from jax import lax
from jax.experimental import pallas as pl
from jax.experimental.pallas import tpu as pltpu

F32 = jnp.float32
BF16 = jnp.bfloat16
GRAD_STREAM = F32

D_MODEL = 1024
HEAD_DIM = 64
N_Q_HEADS = 16
N_KV_HEADS = 2
GROUP = 8
WINDOW = 128
KV_WIDTH = N_KV_HEADS * HEAD_DIM
D_FF = 2816
IN_WIDTH = 6400
N_MOD = 6
EPS = 1e-6
N_DEV = 8
REST_WIDTH = 5 * D_MODEL
KV_COL = D_MODEL + REST_WIDTH
ATTN_SCALE = HEAD_DIM ** -0.5

ADAM_LR = 0.001
ADAM_B1 = 0.9
ADAM_B2 = 0.999
ADAM_EPS = 1e-08
ADAM_WD = 0.01
ADAM_STEP = 10

LANES = 128
SUBLANES = 8
BF16_ROWS = 16
VMEM_LIMIT = 56 * 1024 * 1024
TOKEN_TILE = 512
FF_CHUNK = 256
ROW_PARTS = 2
WGRAD_VMEM = 40 * 1024 * 1024
MESH = pl.DeviceIdType.MESH
ANY = pl.BlockSpec(memory_space=pl.ANY)

NT_DIMS = (((1,), (1,)), ((), ()))
TN_DIMS = (((0,), (0,)), ((), ()))
CHIP_FLIPS = [(0, 0), (1, 0), (0, 1), (1, 1)]


def _full(shape):
    return pl.BlockSpec(shape, lambda *_: (0,) * len(shape))


def _my_place():
    return lax.axis_index("x"), lax.axis_index("y"), lax.axis_index("c")


def _flip(v, bit):
    return 1 - v if bit else v


def _sigmoid(v):
    return 1.0 / (1.0 + jnp.exp2(v * (-1.4426950408889634)))


class _Rider:
    def __init__(self, ins, out_shapes, sem_shapes, first=None, mid=None, last=None, ins_in_vmem=False):
        self.ins, self.out_shapes, self.sem_shapes = list(ins), list(out_shapes), list(sem_shapes)
        self.in_specs = [_full(a.shape) if ins_in_vmem else ANY for a in self.ins]
        self.hooks = [(when, fn) for when, fn in (("first", first), ("mid", mid), ("last", last)) if fn is not None]


def _call(body, name, grid, args, in_specs, out_shape, out_specs, scratch=(), rider=None, aliases=None):
    n_in, n_out, n_scr = len(args), len(out_shape), len(scratch)
    r_in = rider.ins if rider else []
    r_out = rider.out_shapes if rider else []
    r_sem = rider.sem_shapes if rider else []
    nsteps = math.prod(grid)

    def full_body(*refs):
        pos = 0
        groups = []
        for size in (n_in, len(r_in), n_out, len(r_out), n_scr, len(r_sem)):
            groups.append(refs[pos:pos + size])
            pos += size
        ins, rins, outs, routs, scr, rsems = groups
        step = pl.program_id(0)
        for axis in range(1, len(grid)):
            step = step * grid[axis] + pl.program_id(axis)
        at = {"first": 0, "mid": (3 * nsteps) // 4, "last": nsteps - 1}
        hooks = rider.hooks if rider else []
        for when, fn in hooks:
            if when != "last":
                pl.when(step == at[when])(functools.partial(fn, rins, routs, rsems))
        body(*ins, *outs, *scr)
        for when, fn in hooks:
            if when == "last":
                pl.when(step == at[when])(functools.partial(fn, rins, routs, rsems))

    outs = pl.pallas_call(
        full_body, name=name, grid=grid,
        out_shape=list(out_shape) + list(r_out),
        in_specs=list(in_specs) + (rider.in_specs if rider else []),
        out_specs=list(out_specs) + [ANY] * len(r_out),
        scratch_shapes=list(scratch) + list(r_sem),
        input_output_aliases=dict(aliases or {}),
        compiler_params=pltpu.CompilerParams(dimension_semantics=("arbitrary",) * len(grid),
                                             vmem_limit_bytes=VMEM_LIMIT),
    )(*args, *r_in)
    return list(outs[:n_out]), list(outs[n_out:])


def _gather_rider(shards):
    n = len(shards)

    def setup(outs, sems):
        x, y, c = _my_place()
        send_sems, recv_sems, _ = sems
        chips = [(1 - x, y), (x, 1 - y), (1 - x, 1 - y)]

        def block(w, place):
            return outs[w].at[4 * place[0] + 2 * place[1] + place[2]]

        def copy(w, k, place, to, src=None):
            return pltpu.make_async_remote_copy(
                src_ref=block(w, place) if src is None else src, dst_ref=block(w, place),
                send_sem=send_sems.at[w, k], recv_sem=recv_sems.at[w, k], device_id=to, device_id_type=MESH)

        return (x, y, c), (x, y, 1 - c), chips, block, copy

    def first(ins, outs, sems):
        me, sibling, chips, block, copy = setup(outs, sems)
        for w in range(n):
            pltpu.make_async_copy(ins[w], block(w, me), sems[2].at[w]).start()
            copy(w, 0, me, sibling, src=ins[w]).start()
            for j, chip in enumerate(chips):
                copy(w, 1 + j, me, (*chip, me[2]), src=ins[w]).start()

    def mid(ins, outs, sems):
        me, sibling, chips, block, copy = setup(outs, sems)
        for w in range(n):
            for j, chip in enumerate(chips):
                copy(w, 1 + j, (*chip, me[2]), me).wait_recv()
                copy(w, 4 + j, (*chip, me[2]), sibling).start()

    def last(ins, outs, sems):
        me, sibling, chips, block, copy = setup(outs, sems)
        for w in range(n):
            copy(w, 0, sibling, me).wait_recv()
            for j, chip in enumerate(chips):
                copy(w, 4 + j, (*chip, 1 - me[2]), me).wait_recv()
            copy(w, 0, me, sibling, src=ins[w]).wait_send()
            for j, chip in enumerate(chips):
                copy(w, 1 + j, me, (*chip, me[2]), src=ins[w]).wait_send()
                copy(w, 4 + j, (*chip, me[2]), sibling).wait_send()
            pltpu.make_async_copy(ins[w], block(w, me), sems[2].at[w]).wait()

    return _Rider(
        shards, [jax.ShapeDtypeStruct((N_DEV,) + s.shape, BF16) for s in shards],
        [pltpu.SemaphoreType.DMA((n, N_DEV - 1)), pltpu.SemaphoreType.DMA((n, N_DEV - 1)),
         pltpu.SemaphoreType.DMA((n,))],
        first=first, mid=mid, last=last, ins_in_vmem=True)


def _sibling_rider(gblocks):
    n = len(gblocks)

    def copies(ins, outs, sems):
        x, y, c = _my_place()
        send_sems, recv_sems = sems
        made = []
        for w in range(n):
            for f, (fx, fy) in enumerate(CHIP_FLIPS):
                chip = 4 * _flip(x, fx) + 2 * _flip(y, fy)
                made.append(pltpu.make_async_remote_copy(
                    src_ref=ins[w].at[chip + 1 - c], dst_ref=outs[w].at[f], send_sem=send_sems.at[w, f],
                    recv_sem=recv_sems.at[w, f], device_id=(x, y, 1 - c), device_id_type=MESH))
        return made

    def first(ins, outs, sems):
        for cp in copies(ins, outs, sems):
            cp.start()

    def last(ins, outs, sems):
        for cp in copies(ins, outs, sems):
            cp.wait_recv()
            cp.wait_send()

    return _Rider(gblocks, [jax.ShapeDtypeStruct((4,) + g.shape[1:], BF16) for g in gblocks],
                  [pltpu.SemaphoreType.DMA((n, 4))] * 2, first=first, last=last)


def _chip_rider(sums):
    n = len(sums)

    def copies(ins, outs, sems):
        x, y, c = _my_place()
        send_sems, recv_sems = sems
        made = []
        for w in range(n):
            for f in (1, 2, 3):
                fx, fy = CHIP_FLIPS[f]
                made.append(pltpu.make_async_remote_copy(
                    src_ref=ins[w].at[f - 1], dst_ref=outs[w].at[f - 1], send_sem=send_sems.at[w, f - 1],
                    recv_sem=recv_sems.at[w, f - 1], device_id=(_flip(x, fx), _flip(y, fy), c), device_id_type=MESH))
        return made

    def first(ins, outs, sems):
        for cp in copies(ins, outs, sems):
            cp.start()

    def last(ins, outs, sems):
        for cp in copies(ins, outs, sems):
            cp.wait_recv()
            cp.wait_send()

    return _Rider(sums, [jax.ShapeDtypeStruct(s.shape, BF16) for s in sums],
                  [pltpu.SemaphoreType.DMA((n, 3))] * 2, first=first, last=last)


def _push_to_all(v_ref, out_ref, send_sems, recv_sems, local_sem, wait=True):
    x, y, c = _my_place()
    me = 4 * x + 2 * y + c
    mine = pltpu.make_async_copy(v_ref, out_ref.at[me], local_sem)
    mine.start()
    sends = []
    for k in range(1, N_DEV):
        px, py, pc = _flip(x, k & 4), _flip(y, k & 2), _flip(c, k & 1)
        cp = pltpu.make_async_remote_copy(
            src_ref=v_ref, dst_ref=out_ref.at[me], send_sem=send_sems.at[k - 1], recv_sem=recv_sems.at[k - 1],
            device_id=(px, py, pc), device_id_type=MESH)
        cp.start()
        sends.append(cp)

    def finish():
        for k in range(1, N_DEV):
            px, py, pc = _flip(x, k & 4), _flip(y, k & 2), _flip(c, k & 1)
            pltpu.make_async_remote_copy(
                src_ref=v_ref, dst_ref=out_ref.at[4 * px + 2 * py + pc], send_sem=send_sems.at[k - 1],
                recv_sem=recv_sems.at[k - 1], device_id=(px, py, pc), device_id_type=MESH).wait_recv()
        for cp in sends:
            cp.wait_send()
        mine.wait()

    if wait:
        finish()
    return finish


def _small_allgather(v, name):
    def body(v_ref, out_ref, send_sems, recv_sems, local_sem):
        _push_to_all(v_ref, out_ref, send_sems, recv_sems, local_sem)

    return pl.pallas_call(
        body, name=name,
        out_shape=jax.ShapeDtypeStruct((N_DEV,) + v.shape, F32),
        in_specs=[pl.BlockSpec(memory_space=pltpu.VMEM)],
        out_specs=pl.BlockSpec(memory_space=pltpu.VMEM),
        scratch_shapes=[pltpu.SemaphoreType.DMA((N_DEV - 1,)), pltpu.SemaphoreType.DMA((N_DEV - 1,)),
                        pltpu.SemaphoreType.DMA],
        compiler_params=pltpu.CompilerParams(vmem_limit_bytes=VMEM_LIMIT),
    )(v)


def _gather_first_weight(shard, others, cond_rows, w_ada, b_cols):
    n = len(others)
    ada_cols = w_ada.shape[1]
    c_rows = D_MODEL // LANES

    def body(*refs):
        w_ref, other_refs = refs[0], refs[1:1 + n]
        cond_ref, wada_ref, bcols_ref = refs[1 + n:4 + n]
        out_ref, cast_refs = refs[4 + n], refs[5 + n:5 + 2 * n]
        cond_all_ref, mod_all_ref = refs[5 + 2 * n:7 + 2 * n]
        mine_ref, mod_ref, send_sems, recv_sems, local_sem, small_send, small_recv, small_local = refs[7 + 2 * n:]
        x, y, c = _my_place()
        me, sibling = (x, y, c), (x, y, 1 - c)
        xnb, ynb, diag = (1 - x, y), (x, 1 - y), (1 - x, 1 - y)
        half = shard.shape[0] // 2

        def block(place, part=None):
            ref = out_ref.at[4 * place[0] + 2 * place[1] + place[2]]
            return ref if part is None else ref.at[pl.ds(part * half, half)]

        def copy(k, place, to, part=None, src=None):
            return pltpu.make_async_remote_copy(
                src_ref=block(place, part) if src is None else src, dst_ref=block(place, part),
                send_sem=send_sems.at[k], recv_sem=recv_sems.at[k], device_id=to, device_id_type=MESH)

        finish_cond = _push_to_all(cond_ref, cond_all_ref, small_send.at[0], small_recv.at[0], small_local.at[0],
                                   wait=False)
        mine_ref[...] = w_ref[...].astype(BF16)
        local = pltpu.make_async_copy(mine_ref, block(me), local_sem)
        local.start()
        started = [copy(0, me, sibling, src=mine_ref), copy(1, me, (*xnb, c), src=mine_ref),
                   copy(2, me, (*ynb, c), src=mine_ref)]
        for cp in started:
            cp.start()
        finish_cond()
        mod = jnp.zeros((N_DEV, ada_cols), F32) + bcols_ref[...]
        for r in range(c_rows):
            cf = cond_all_ref[:, r, :]
            act = (cf * _sigmoid(cf)).astype(BF16)
            mod = mod + jnp.dot(act, wada_ref[r * LANES:(r + 1) * LANES, :].astype(BF16),
                                preferred_element_type=F32)
        mod_ref[...] = mod
        finish_mod = _push_to_all(mod_ref, mod_all_ref, small_send.at[1], small_recv.at[1], small_local.at[1],
                                  wait=False)
        for o_ref, c_ref in zip(other_refs, cast_refs):
            c_ref[...] = o_ref[...].astype(BF16)
        def start(cp):
            cp.start()
            started.append(cp)

        copy(1, (*xnb, c), me).wait_recv()
        start(copy(3, (*xnb, c), (*ynb, c), part=0))
        start(copy(5, (*xnb, c), sibling))
        copy(2, (*ynb, c), me).wait_recv()
        start(copy(4, (*ynb, c), (*xnb, c), part=1))
        start(copy(6, (*ynb, c), sibling))
        copy(3, (*diag, c), me, part=0).wait_recv()
        start(copy(7, (*diag, c), sibling, part=0))
        copy(4, (*diag, c), me, part=1).wait_recv()
        start(copy(8, (*diag, c), sibling, part=1))
        copy(0, sibling, me).wait_recv()
        copy(5, (*xnb, 1 - c), me).wait_recv()
        copy(6, (*ynb, 1 - c), me).wait_recv()
        copy(7, (*diag, 1 - c), me, part=0).wait_recv()
        copy(8, (*diag, 1 - c), me, part=1).wait_recv()
        finish_mod()
        for cp in started:
            cp.wait_send()
        local.wait()

    vmem = pl.BlockSpec(memory_space=pltpu.VMEM)
    outs = pl.pallas_call(
        body, name="gather_w_in",
        out_shape=[jax.ShapeDtypeStruct((N_DEV,) + shard.shape, BF16)]
        + [jax.ShapeDtypeStruct(o.shape, BF16) for o in others]
        + [jax.ShapeDtypeStruct((N_DEV,) + cond_rows.shape, F32), jax.ShapeDtypeStruct((N_DEV, N_DEV, ada_cols), F32)],
        in_specs=[vmem] * (4 + n),
        out_specs=[ANY] + [vmem] * (n + 2),
        scratch_shapes=[pltpu.VMEM(shard.shape, BF16), pltpu.VMEM((N_DEV, ada_cols), F32),
                        pltpu.SemaphoreType.DMA((9,)), pltpu.SemaphoreType.DMA((9,)),
                        pltpu.SemaphoreType.DMA,
                        pltpu.SemaphoreType.DMA((2, N_DEV - 1)), pltpu.SemaphoreType.DMA((2, N_DEV - 1)),
                        pltpu.SemaphoreType.DMA((2,))],
        compiler_params=pltpu.CompilerParams(vmem_limit_bytes=VMEM_LIMIT),
    )(shard, *others, cond_rows, w_ada, b_cols)
    return outs[0], list(outs[1:1 + n]), outs[1 + n], outs[2 + n]


def _carry(rider, name):
    def body(token_ref):
        token_ref[...] = jnp.zeros_like(token_ref)

    _, routs = _call(body, name, (1,), [], [], [jax.ShapeDtypeStruct((SUBLANES, LANES), F32)],
                     [_full((SUBLANES, LANES))], rider=rider)
    return routs


def _ada_weight_grad(c_all, dmod_cols):
    cols = dmod_cols.shape[1]

    def body(c_ref, d_ref, out_ref):
        cf = c_ref[...]
        act = (cf * _sigmoid(cf)).astype(BF16)
        out_ref[...] = lax.dot_general(act, d_ref[...].astype(BF16), TN_DIMS, preferred_element_type=F32)

    return pl.pallas_call(
        body, name="ada_weight_grad",
        out_shape=jax.ShapeDtypeStruct((D_MODEL, cols), F32),
        in_specs=[pl.BlockSpec(memory_space=pltpu.VMEM)] * 2,
        out_specs=pl.BlockSpec(memory_space=pltpu.VMEM),
        compiler_params=pltpu.CompilerParams(vmem_limit_bytes=VMEM_LIMIT),
    )(c_all, dmod_cols)


PACK_ROWS = 24
PACK_DMOD = 0
PACK_PARAMS = {"g_mix": (6, D_MODEL), "b_in": (7, IN_WIDTH), "g_ffn": (14, D_MODEL), "g_final": (15, D_MODEL),
               "sinks": (19, N_Q_HEADS)}
PACK_CONV = 16
PACK_SQERR = 20


def _small_finalize(packed_all, params):
    names = ["b_ada"] + list(PACK_PARAMS)
    layout = dict(PACK_PARAMS, b_ada=(PACK_DMOD, N_MOD * D_MODEL))
    n = len(names)

    def body(*refs):
        p_ref = refs[0]
        ins = refs[1:1 + 3 * n]
        outs = refs[1 + 3 * n:1 + 7 * n]
        conv_ref, loss_ref = refs[1 + 7 * n:]
        total = p_ref[0]
        for d in range(1, N_DEV):
            total = total + p_ref[d]
        for k, name in enumerate(names):
            row0, width = layout[name]
            w_ref, m_ref, v_ref = ins[3 * k:3 * k + 3]
            g_ref, d_ref, nm_ref, nv_ref = outs[4 * k:4 * k + 4]
            for chunk in range(-(-width // D_MODEL)):
                lo = chunk * D_MODEL
                hi = min(lo + D_MODEL, width)
                g = total[row0 + chunk:row0 + chunk + 1, :hi - lo]
                g_ref[:, lo:hi] = g
                d_ref[:, lo:hi], nm_ref[:, lo:hi], nv_ref[:, lo:hi] = _adamw_update(
                    w_ref[:, lo:hi], g, m_ref[:, lo:hi], v_ref[:, lo:hi])
        conv_ref[...] = total[PACK_CONV:PACK_CONV + 3, :]
        loss_ref[...] = (0.5 / D_MODEL) * jnp.sum(total[PACK_SQERR:PACK_SQERR + 1, :], keepdims=True)

    vmem = pl.BlockSpec(memory_space=pltpu.VMEM)
    flat = [a for name in names for a in params[name]]
    out_shape = [jax.ShapeDtypeStruct(params[name][0].shape, F32) for name in names for _ in range(4)]
    outs = pl.pallas_call(
        body, name="small_finalize",
        out_shape=out_shape + [jax.ShapeDtypeStruct((3, D_MODEL), F32), jax.ShapeDtypeStruct((1, 1), F32)],
        in_specs=[vmem] * (1 + 3 * n),
        out_specs=[vmem] * (4 * n + 2),
        compiler_params=pltpu.CompilerParams(vmem_limit_bytes=VMEM_LIMIT),
    )(packed_all, *flat)
    return {name: tuple(outs[4 * k:4 * k + 4]) for k, name in enumerate(names)}, outs[4 * n], outs[4 * n + 1]


def _row_tile(rows, multiple):
    for cand in range(min(rows, 256), 0, -1):
        if rows % cand == 0 and cand % multiple == 0:
            return cand
    return rows


def _adamw_update(w, g, m, v):
    c1 = 1.0 / (1.0 - ADAM_B1 ** ADAM_STEP)
    c2 = 1.0 / (1.0 - ADAM_B2 ** ADAM_STEP)
    nm = ADAM_B1 * m + (1.0 - ADAM_B1) * g
    nv = ADAM_B2 * v + (1.0 - ADAM_B2) * (g * g)
    delta = -ADAM_LR * ((nm * c1) / (jnp.sqrt(nv * c2) + ADAM_EPS) + ADAM_WD * w)
    return delta, nm, nv


def _adamw(w, g, m, v, name):
    rows, cols = w.shape
    tile = _row_tile(rows, SUBLANES)

    def body(w_ref, g_ref, m_ref, v_ref, d_ref, nm_ref, nv_ref):
        d_ref[...], nm_ref[...], nv_ref[...] = _adamw_update(w_ref[...], g_ref[...], m_ref[...], v_ref[...])

    spec = pl.BlockSpec((tile, cols), lambda i: (i, 0))
    outs, _ = _call(body, name, (rows // tile,), [w, g, m, v], [spec] * 4,
                    [jax.ShapeDtypeStruct((rows, cols), F32)] * 3, [spec] * 3)
    return outs


def _sibling_sum(gblocks, sib, name):
    _, r, cdim = gblocks.shape
    tile = _row_tile(r, BF16_ROWS)
    x, y, c = _my_place()
    table = jnp.stack([4 * _flip(x, fx) + 2 * _flip(y, fy) + c for fx, fy in CHIP_FLIPS]).astype(jnp.int32)

    def body(table_ref, own0, own1, own2, own3, sib_ref, sums_ref, mine_ref):
        mine_ref[...] = own0[...].astype(F32) + sib_ref[0].astype(F32)
        for f, own in ((1, own1), (2, own2), (3, own3)):
            sums_ref[f - 1] = (own[...].astype(F32) + sib_ref[f].astype(F32)).astype(BF16)

    own_specs = [pl.BlockSpec((None, tile, cdim), functools.partial(lambda i, tab, f: (tab[f], i, 0), f=f))
                 for f in range(4)]
    return pl.pallas_call(
        body, name=name,
        grid_spec=pltpu.PrefetchScalarGridSpec(
            num_scalar_prefetch=1, grid=(r // tile,),
            in_specs=own_specs + [pl.BlockSpec((4, tile, cdim), lambda i, tab: (0, i, 0))],
            out_specs=[pl.BlockSpec((3, tile, cdim), lambda i, tab: (0, i, 0)),
                       pl.BlockSpec((tile, cdim), lambda i, tab: (i, 0))]),
        out_shape=[jax.ShapeDtypeStruct((3, r, cdim), BF16), jax.ShapeDtypeStruct((r, cdim), F32)],
        compiler_params=pltpu.CompilerParams(dimension_semantics=("arbitrary",), vmem_limit_bytes=VMEM_LIMIT),
    )(table, gblocks, gblocks, gblocks, gblocks, sib)


def _chip_sum_adamw(mine, ici, w, m, v, name):
    r, cdim = mine.shape
    tile = _row_tile(r, BF16_ROWS)

    def body(mine_ref, ici_ref, w_ref, m_ref, v_ref, g_ref, d_ref, nm_ref, nv_ref):
        g = mine_ref[...]
        for f in range(3):
            g = g + ici_ref[f].astype(F32)
        g_ref[...] = g
        d_ref[...], nm_ref[...], nv_ref[...] = _adamw_update(w_ref[...], g, m_ref[...], v_ref[...])

    spec = pl.BlockSpec((tile, cdim), lambda i: (i, 0))
    outs, _ = _call(
        body, name, (r // tile,), [mine, ici, w, m, v],
        [spec, pl.BlockSpec((3, tile, cdim), lambda i: (0, i, 0)), spec, spec, spec],
        [jax.ShapeDtypeStruct((r, cdim), F32)] * 4, [spec] * 4)
    return outs


REF_KV_COL = D_MODEL
REF_REST_COL = D_MODEL + 2 * KV_WIDTH
IN_CHUNK = 1280
IN_PIECES = ([(0, 0, D_MODEL)]
             + [(D_MODEL + n * IN_CHUNK, REF_REST_COL + n * IN_CHUNK, IN_CHUNK) for n in range(REST_WIDTH // IN_CHUNK)]
             + [(KV_COL, REF_KV_COL, 2 * KV_WIDTH)])


def _inproj_fwd(x, vec, w_t, b_in, rider):
    t = x.shape[0]
    tm = min(TOKEN_TILE, t)

    def body(x_ref, vec_ref, w_ref, b_ref, z_ref, h_ref):
        xf = x_ref[...]
        r = lax.rsqrt(jnp.mean(xf * xf, axis=-1, keepdims=True) + EPS)
        h = (xf * r) * (vec_ref[0:1, :] * (1.0 + vec_ref[1:2, :])) + vec_ref[2:3, :]
        hb = h.astype(BF16)
        h_ref[...] = hb
        for mine, ref, width in IN_PIECES:
            zc = lax.dot_general(hb, w_ref[ref:ref + width, :], NT_DIMS, preferred_element_type=F32)
            z_ref[:, mine:mine + width] = (zc + b_ref[:, ref:ref + width]).astype(BF16)

    return _call(
        body, "inproj_fwd", (t // tm,), [x, vec, w_t, b_in],
        [pl.BlockSpec((tm, D_MODEL), lambda i: (i, 0)), _full((SUBLANES, D_MODEL)),
         _full((IN_WIDTH, D_MODEL)), _full((1, IN_WIDTH))],
        [jax.ShapeDtypeStruct((t, IN_WIDTH), BF16), jax.ShapeDtypeStruct((t, D_MODEL), BF16)],
        [pl.BlockSpec((tm, IN_WIDTH), lambda i: (i, 0)), pl.BlockSpec((tm, D_MODEL), lambda i: (i, 0))],
        rider=rider)


PAIRS = GROUP // 2
STACK = PAIRS * WINDOW


ATTN_BLOCKS = 4
ATTN_BWD_BLOCKS = 1
LOG2E = 1.4426950408889634
LN2 = 0.6931471805599453
SCORE_SCALE = ATTN_SCALE * LOG2E


def _fill_window_bias(bias_ref):
    shape = bias_ref.shape[1:]
    kj = lax.broadcasted_iota(jnp.int32, shape, 0)
    qi = jnp.bitwise_and(lax.broadcasted_iota(jnp.int32, shape, 1), WINDOW - 1)
    in_prev = jnp.logical_and(kj < WINDOW, kj > qi)
    in_cur = jnp.logical_and(kj >= WINDOW, (kj - WINDOW) <= qi)
    bias_ref[0] = jnp.where(in_cur, 0.0, -jnp.inf)
    bias_ref[1] = jnp.where(jnp.logical_or(in_prev, in_cur), 0.0, -jnp.inf)


def _half_tiles(tile):
    low = lax.broadcasted_iota(jnp.int32, tile.shape, 1) < HEAD_DIM
    swapped = jnp.concatenate([tile[:, HEAD_DIM:], tile[:, :HEAD_DIM]], axis=1)
    zero = jnp.zeros_like(tile)
    return ((jnp.where(low, tile, zero), jnp.where(low, zero, swapped)),
            (jnp.where(low, swapped, zero), jnp.where(low, zero, tile)))


def _stack_pairs(ref, row0, j):
    return jnp.concatenate(
        [ref[pl.ds(row0, WINDOW), (j * PAIRS + p) * LANES:(j * PAIRS + p + 1) * LANES] for p in range(PAIRS)], axis=0)


def _per_pair_row(values):
    pair = lax.broadcasted_iota(jnp.int32, (1, STACK), 1) // WINDOW
    row = jnp.full((1, STACK), values[PAIRS - 1], F32)
    for p in range(PAIRS - 2, -1, -1):
        row = jnp.where(pair == p, values[p], row)
    return row


def _attn_fwd(z, sinks, rider):
    t = z.shape[0]
    tq = min(TOKEN_TILE, t)
    nblk = tq // WINDOW

    def body(q_ref, kv_ref, sink_ref, o_ref, lse_ref, bias_ref):
        i = pl.program_id(0)

        @pl.when(i == 0)
        def _():
            _fill_window_bias(bias_ref)

        def window(b):
            row0 = pl.multiple_of(b * WINDOW, WINDOW)
            start = i * tq + b * WINDOW
            prev = pl.multiple_of(jnp.maximum(start - WINDOW, 0), WINDOW)
            cur = pl.multiple_of(start, WINDOW)
            kvw = jnp.concatenate([kv_ref[pl.ds(prev, WINDOW), :], kv_ref[pl.ds(cur, WINDOW), :]], axis=0)
            return row0, _half_tiles(kvw[:, :KV_WIDTH]), _half_tiles(kvw[:, KV_WIDTH:]), bias_ref[jnp.minimum(start, 1)]

        def block_group(bb, carry):
            windows = [window(bb * ATTN_BLOCKS + n) for n in range(ATTN_BLOCKS)]
            for j in range(N_KV_HEADS):
                for pr in range(PAIRS):
                    cols = slice((j * PAIRS + pr) * LANES, (j * PAIRS + pr + 1) * LANES)
                    o_ts = [jnp.zeros((LANES, WINDOW), F32) for _ in windows]
                    for parity in range(2):
                        h = j * GROUP + 2 * pr + parity
                        sink = sink_ref[h] * LOG2E
                        for n, (row0, k_halves, v_halves, bias) in enumerate(windows):
                            qp = q_ref[pl.ds(row0, WINDOW), cols]
                            s = lax.dot_general(k_halves[j][parity], qp, NT_DIMS, preferred_element_type=F32)
                            s = s * SCORE_SCALE + bias
                            m = jnp.maximum(jnp.max(s, axis=0, keepdims=True), sink)
                            p = jnp.exp2(s - m)
                            denom = jnp.sum(p, axis=0, keepdims=True) + jnp.exp2(sink - m)
                            pv = lax.dot_general(v_halves[j][parity], p.astype(BF16), TN_DIMS,
                                                 preferred_element_type=F32)
                            o_ts[n] = o_ts[n] + pv * (1.0 / denom)
                            lse_ref[h:h + 1, pl.ds(row0, WINDOW)] = m + jnp.log2(denom)
                    for n, (row0, _, _, _) in enumerate(windows):
                        o_ref[pl.ds(row0, WINDOW), cols] = jnp.transpose(o_ts[n].astype(BF16))
            return carry

        lax.fori_loop(0, nblk // ATTN_BLOCKS, block_group, 0)

    return _call(
        body, "attn_fwd", (t // tq,), [z, z, sinks],
        [pl.BlockSpec((tq, D_MODEL), lambda i: (i, 0)),
         pl.BlockSpec((t, 2 * KV_WIDTH), lambda i: (0, KV_COL // (2 * KV_WIDTH))),
         pl.BlockSpec(memory_space=pltpu.SMEM)],
        [jax.ShapeDtypeStruct((t, D_MODEL), BF16), jax.ShapeDtypeStruct((N_Q_HEADS, t), F32)],
        [pl.BlockSpec((tq, D_MODEL), lambda i: (i, 0)), pl.BlockSpec((N_Q_HEADS, tq), lambda i: (0, i))],
        scratch=[pltpu.VMEM((2, 2 * WINDOW, WINDOW), F32)], rider=rider)


HALO = BF16_ROWS


def _shift_down(u, uh, k):
    rolled = pltpu.roll(u, k, 0)
    row = lax.broadcasted_iota(jnp.int32, (SUBLANES, u.shape[1]), 0)
    top = rolled[:SUBLANES, :]
    for j in range(k):
        top = jnp.where(row == j, uh[HALO - k + j:HALO - k + j + 1, :], top)
    return jnp.concatenate([top, rolled[SUBLANES:, :]], axis=0)


def _shift_up(u, nxt, k):
    n = u.shape[0]
    rolled = pltpu.roll(u, n - k, 0)
    row = lax.broadcasted_iota(jnp.int32, (SUBLANES, u.shape[1]), 0)
    bottom = rolled[n - SUBLANES:, :]
    for j in range(k):
        bottom = jnp.where(row == SUBLANES - k + j, nxt[j:j + 1, :], bottom)
    return jnp.concatenate([rolled[:n - SUBLANES, :], bottom], axis=0)


def _conv_inputs(cc_ref, cx_ref, hc_ref, hx_ref, first_tile):
    cc = cc_ref[...].astype(F32)
    cx = cx_ref[...].astype(F32)
    u = cc * cx
    uh = jnp.where(first_tile, 0.0, hc_ref[...].astype(F32) * hx_ref[...].astype(F32))
    return cc, cx, u, _shift_down(u, uh, 1), _shift_down(u, uh, 2)


def _z_specs(tm, order):
    per_tile = tm // HALO
    cols = [pl.BlockSpec((tm, D_MODEL), functools.partial(lambda i, j: (order(i), j), j=j)) for j in range(1, 6)]
    halos = [pl.BlockSpec((HALO, D_MODEL),
                          functools.partial(lambda i, j: (jnp.maximum(order(i) * per_tile - 1, 0), j), j=j))
             for j in (2, 3)]
    return cols + halos


def _mix_fwd(x, attn, z, vec, w_out):
    t = x.shape[0]
    tm = min(TOKEN_TILE, t)

    def body(x_ref, a_ref, cb_ref, cc_ref, cx_ref, ga_ref, gc_ref, hc_ref, hx_ref, vec_ref, w_ref,
             m_ref, x2_ref, h2_ref, o_ref):
        i = pl.program_id(0)
        _, _, u, u1, u2 = _conv_inputs(cc_ref, cx_ref, hc_ref, hx_ref, i == 0)
        cv = vec_ref[4:5, :] * u2 + vec_ref[5:6, :] * u1 + vec_ref[6:7, :] * u
        conv = cb_ref[...].astype(F32) * cv
        merged = (_sigmoid(ga_ref[...].astype(F32)) * a_ref[...].astype(F32)
                  + _sigmoid(gc_ref[...].astype(F32)) * conv)
        mb = merged.astype(BF16)
        m_ref[...] = mb
        o = jnp.dot(mb, w_ref[...], preferred_element_type=F32)
        o_ref[...] = o.astype(BF16)
        x2 = x_ref[...] + vec_ref[0:1, :] * o
        x2_ref[...] = x2
        r = lax.rsqrt(jnp.mean(x2 * x2, axis=-1, keepdims=True) + EPS)
        h2 = (x2 * r) * (vec_ref[1:2, :] * (1.0 + vec_ref[2:3, :])) + vec_ref[3:4, :]
        h2_ref[...] = h2.astype(BF16)

    tok = pl.BlockSpec((tm, D_MODEL), lambda i: (i, 0))
    outs, _ = _call(
        body, "mix_fwd", (t // tm,), [x, attn, z, z, z, z, z, z, z, vec, w_out],
        [tok, tok] + _z_specs(tm, lambda i: i) + [_full((SUBLANES, D_MODEL)), _full((D_MODEL, D_MODEL))],
        [jax.ShapeDtypeStruct((t, D_MODEL), BF16), jax.ShapeDtypeStruct((t, D_MODEL), F32),
         jax.ShapeDtypeStruct((t, D_MODEL), BF16), jax.ShapeDtypeStruct((t, D_MODEL), BF16)],
        [tok, tok, tok, tok])
    return outs


def _ffn_fwd(h2, w_t):
    t = h2.shape[0]
    tm = min(TOKEN_TILE, t)

    def body(h_ref, w_ref, gu_ref, a_ref):
        hb = h_ref[...]
        for n in range(D_FF // FF_CHUNK):
            lo, hi = n * FF_CHUNK, (n + 1) * FF_CHUNK
            g = lax.dot_general(hb, w_ref[lo:hi, :], NT_DIMS, preferred_element_type=F32)
            u = lax.dot_general(hb, w_ref[D_FF + lo:D_FF + hi, :], NT_DIMS, preferred_element_type=F32)
            sg = _sigmoid(g)
            silu = g * sg
            gu_ref[:, lo:hi] = (u * (sg * (1.0 + g * (1.0 - sg)))).astype(BF16)
            gu_ref[:, D_FF + lo:D_FF + hi] = silu.astype(BF16)
            a_ref[:, lo:hi] = (silu * u).astype(BF16)

    outs, _ = _call(
        body, "ffn_fwd", (t // tm,), [h2, w_t],
        [pl.BlockSpec((tm, D_MODEL), lambda i: (i, 0)), _full((2 * D_FF, D_MODEL))],
        [jax.ShapeDtypeStruct((t, 2 * D_FF), BF16), jax.ShapeDtypeStruct((t, D_FF), BF16)],
        [pl.BlockSpec((tm, 2 * D_FF), lambda i: (i, 0)), pl.BlockSpec((tm, D_FF), lambda i: (i, 0))])
    return outs


def _ffn_out_loss(a, gu, x2, target, vec, w_ffn_out):
    t = a.shape[0]
    tm = min(TOKEN_TILE, t)

    def body(a_ref, gu_ref, x2_ref, t_ref, vec_ref, w_ref, dx3_ref, df_ref, dgu_ref, acc_ref):
        @pl.when(pl.program_id(0) == 0)
        def _():
            acc_ref[...] = jnp.zeros_like(acc_ref)

        ga2 = vec_ref[0:1, :]
        gf = vec_ref[1:2, :]
        parts = min(ROW_PARTS, tm // LANES)
        part_rows = [slice(n * (tm // parts), (n + 1) * (tm // parts)) for n in range(parts)]

        def head(rows, f):
            x3 = x2_ref[rows, :] + ga2 * f
            r = lax.rsqrt(jnp.mean(x3 * x3, axis=-1, keepdims=True) + EPS)
            xn = x3 * r
            err = xn * gf - t_ref[rows, :]
            dxn = err * (gf * (1.0 / D_MODEL))
            dx3 = r * (dxn - xn * jnp.mean(dxn * xn, axis=-1, keepdims=True))
            dx3_ref[rows, :] = dx3.astype(GRAD_STREAM)
            sums = (jnp.sum(err * err, axis=0, keepdims=True),
                    jnp.sum(err * xn, axis=0, keepdims=True) * (1.0 / D_MODEL),
                    jnp.sum(dx3 * f, axis=0, keepdims=True))
            df = (dx3 * ga2).astype(BF16)
            df_ref[rows, :] = df
            return df, sums

        def tail(rows, df):
            for n in range(D_FF // FF_CHUNK):
                lo, hi = n * FF_CHUNK, (n + 1) * FF_CHUNK
                da = lax.dot_general(df, w_ref[lo:hi, :], NT_DIMS, preferred_element_type=F32)
                dgu_ref[rows, lo:hi] = (da * gu_ref[rows, lo:hi].astype(F32)).astype(BF16)
                dgu_ref[rows, D_FF + lo:D_FF + hi] = (da * gu_ref[rows, D_FF + lo:D_FF + hi].astype(F32)).astype(BF16)

        fs = [jnp.dot(a_ref[rows, :], w_ref[...], preferred_element_type=F32) for rows in part_rows]
        heads = [head(rows, f) for rows, f in zip(part_rows, fs)]
        for rows, (df, _) in zip(part_rows, heads):
            tail(rows, df)
        for k in range(3):
            total = heads[0][1][k]
            for _, sums in heads[1:]:
                total = total + sums[k]
            acc_ref[k:k + 1, :] += total

    tok = pl.BlockSpec((tm, D_MODEL), lambda i: (i, 0))
    outs, _ = _call(
        body, "ffn_out_loss", (t // tm,), [a, gu, x2, target, vec, w_ffn_out],
        [pl.BlockSpec((tm, D_FF), lambda i: (i, 0)), pl.BlockSpec((tm, 2 * D_FF), lambda i: (i, 0)),
         tok, tok, _full((SUBLANES, D_MODEL)), _full((D_FF, D_MODEL))],
        [jax.ShapeDtypeStruct((t, D_MODEL), GRAD_STREAM), jax.ShapeDtypeStruct((t, D_MODEL), BF16),
         jax.ShapeDtypeStruct((t, 2 * D_FF), BF16), jax.ShapeDtypeStruct((SUBLANES, D_MODEL), F32)],
        [tok, tok, pl.BlockSpec((tm, 2 * D_FF), lambda i: (i, 0)), _full((SUBLANES, D_MODEL))])
    return outs


def _ffn_in_bwd(dgu, x2, dx3, vec, w_t, rider):
    t = x2.shape[0]
    tm = min(TOKEN_TILE, t)

    def body(dgu_ref, x2_ref, dx3_ref, vec_ref, wf_ref, dx2_ref, acc_ref):
        @pl.when(pl.program_id(0) == 0)
        def _():
            acc_ref[...] = jnp.zeros_like(acc_ref)

        gffn = vec_ref[0:1, :]
        sc2 = vec_ref[1:2, :]
        parts = min(ROW_PARTS, tm // LANES)
        part_rows = [slice(n * (tm // parts), (n + 1) * (tm // parts)) for n in range(parts)]
        dhs = [jnp.dot(dgu_ref[rows, :], wf_ref[...], preferred_element_type=F32) for rows in part_rows]
        gs = gffn * (1.0 + sc2)
        sum_dh = jnp.zeros((1, D_MODEL), F32)
        sum_dh_xn = jnp.zeros((1, D_MODEL), F32)
        for rows, dh2 in zip(part_rows, dhs):
            x2 = x2_ref[rows, :]
            r = lax.rsqrt(jnp.mean(x2 * x2, axis=-1, keepdims=True) + EPS)
            xn = x2 * r
            dh_xn = dh2 * xn
            sum_dh = sum_dh + jnp.sum(dh2, axis=0, keepdims=True)
            sum_dh_xn = sum_dh_xn + jnp.sum(dh_xn, axis=0, keepdims=True)
            dx2 = dx3_ref[rows, :].astype(F32) + r * (dh2 * gs - xn * jnp.mean(dh_xn * gs, axis=-1, keepdims=True))
            dx2_ref[rows, :] = dx2.astype(GRAD_STREAM)
        acc_ref[0:1, :] += sum_dh
        acc_ref[1:2, :] += sum_dh_xn * gffn
        acc_ref[2:3, :] += sum_dh_xn * (1.0 + sc2)

    tok = pl.BlockSpec((tm, D_MODEL), lambda i: (i, 0))
    return _call(
        body, "ffn_in_bwd", (t // tm,), [dgu, x2, dx3, vec, w_t],
        [pl.BlockSpec((tm, 2 * D_FF), lambda i: (i, 0)), tok, tok, _full((SUBLANES, D_MODEL)),
         _full((2 * D_FF, D_MODEL))],
        [jax.ShapeDtypeStruct((t, D_MODEL), GRAD_STREAM), jax.ShapeDtypeStruct((SUBLANES, D_MODEL), F32)],
        [tok, _full((SUBLANES, D_MODEL))], rider=rider)


def _mix_bwd(dx2, oproj, attn, z, vec, w_out, rider):
    t = dx2.shape[0]
    tm = min(TOKEN_TILE, t)
    nt = t // tm
    rev = lambda i: nt - 1 - i

    def body(dx2_ref, m_ref, a_ref, cb_ref, cc_ref, cx_ref, ga_ref, gc_ref, hc_ref, hx_ref,
             vec_ref, wo_ref, do_ref, da_ref, dr_ref, acc_ref, carry_ref):
        i = pl.program_id(0)

        @pl.when(i == 0)
        def _():
            acc_ref[...] = jnp.zeros_like(acc_ref)
            carry_ref[...] = jnp.zeros_like(carry_ref)

        ga1 = vec_ref[0:1, :]
        w0, w1, w2 = vec_ref[1:2, :], vec_ref[2:3, :], vec_ref[3:4, :]
        dx2 = dx2_ref[...].astype(F32)
        acc_ref[0:1, :] += jnp.sum(dx2 * m_ref[...].astype(F32), axis=0, keepdims=True)
        do = (dx2 * ga1).astype(BF16)
        do_ref[...] = do
        dm = lax.dot_general(do, wo_ref[...], NT_DIMS, preferred_element_type=F32)

        cc, cx, u, u1, u2 = _conv_inputs(cc_ref, cx_ref, hc_ref, hx_ref, i == nt - 1)
        cv = w0 * u2 + w1 * u1 + w2 * u
        cb = cb_ref[...].astype(F32)
        sa = _sigmoid(ga_ref[...].astype(F32))
        sc = _sigmoid(gc_ref[...].astype(F32))
        attn = a_ref[...].astype(F32)
        dattn = dm * sa
        da_ref[...] = dattn.astype(BF16)
        dconv = dm * sc
        dconv_b = dconv * cv
        dr_ref[:, 3 * D_MODEL:4 * D_MODEL] = (dattn * attn * (1.0 - sa)).astype(BF16)
        dr_ref[:, 4 * D_MODEL:5 * D_MODEL] = (dconv_b * cb * (1.0 - sc)).astype(BF16)
        dr_ref[:, 0:D_MODEL] = dconv_b.astype(BF16)
        dcv = dconv * cb
        acc_ref[1:2, :] += jnp.sum(dcv * u2, axis=0, keepdims=True)
        acc_ref[2:3, :] += jnp.sum(dcv * u1, axis=0, keepdims=True)
        acc_ref[3:4, :] += jnp.sum(dcv * u, axis=0, keepdims=True)
        nxt = carry_ref[...]
        du = w2 * dcv + w1 * _shift_up(dcv, nxt, 1) + w0 * _shift_up(dcv, nxt, 2)
        carry_ref[...] = dcv[0:SUBLANES, :]
        dr_ref[:, D_MODEL:2 * D_MODEL] = (du * cx).astype(BF16)
        dr_ref[:, 2 * D_MODEL:3 * D_MODEL] = (du * cc).astype(BF16)

    tok = pl.BlockSpec((tm, D_MODEL), lambda i: (rev(i), 0))
    return _call(
        body, "mix_bwd", (nt,), [dx2, oproj, attn, z, z, z, z, z, z, z, vec, w_out],
        [tok, tok, tok] + _z_specs(tm, rev) + [_full((SUBLANES, D_MODEL)), _full((D_MODEL, D_MODEL))],
        [jax.ShapeDtypeStruct((t, D_MODEL), BF16), jax.ShapeDtypeStruct((t, D_MODEL), BF16),
         jax.ShapeDtypeStruct((t, REST_WIDTH), BF16), jax.ShapeDtypeStruct((SUBLANES, D_MODEL), F32)],
        [tok, tok, pl.BlockSpec((tm, REST_WIDTH), lambda i: (rev(i), 0)), _full((SUBLANES, D_MODEL))],
        scratch=[pltpu.VMEM((SUBLANES, D_MODEL), F32)], rider=rider)


def _attn_bwd(z, dattn, attn, lse, sinks, rider):
    t = z.shape[0]
    tq = min(TOKEN_TILE, t)
    nblk = tq // WINDOW
    nt = t // tq

    def body(q_ref, kv_ref, do_ref, o_ref, lse_ref, sink_ref, dq_ref, dkv_ref, ds_ref, acc_ref, bias_ref):
        i = pl.program_id(0)

        @pl.when(i == 0)
        def _():
            acc_ref[...] = jnp.zeros_like(acc_ref)
            ds_ref[...] = jnp.zeros_like(ds_ref)
            _fill_window_bias(bias_ref)

        lane = lax.broadcasted_iota(jnp.int32, (1, LANES), 1)
        ind_row = lax.broadcasted_iota(jnp.int32, (SUBLANES, LANES), 0)
        ind_low = lax.broadcasted_iota(jnp.int32, (SUBLANES, LANES), 1) < HEAD_DIM
        indicator = jnp.where(jnp.logical_or(jnp.logical_and(ind_row == 0, ind_low),
                                             jnp.logical_and(ind_row == 1, jnp.logical_not(ind_low))),
                              1.0, 0.0).astype(BF16)
        low = lax.broadcasted_iota(jnp.int32, (2 * WINDOW, LANES), 1) < HEAD_DIM

        def both_heads(even, odd):
            picked = jnp.where(low, even, odd)
            return picked + jnp.concatenate([picked[:, HEAD_DIM:], picked[:, :HEAD_DIM]], axis=1)

        def window(b):
            row0 = pl.multiple_of(b * WINDOW, WINDOW)
            start = i * tq + b * WINDOW
            prev = pl.multiple_of(jnp.maximum(start - WINDOW, 0), WINDOW)
            cur = pl.multiple_of(start, WINDOW)
            kvw = jnp.concatenate([kv_ref[pl.ds(prev, WINDOW), :], kv_ref[pl.ds(cur, WINDOW), :]], axis=0)
            return (row0, prev, cur, _half_tiles(kvw[:, :KV_WIDTH]), _half_tiles(kvw[:, KV_WIDTH:]),
                    bias_ref[jnp.minimum(start, 1)])

        def block_group(bb, dsink):
            windows = [window(bb * ATTN_BWD_BLOCKS + n) for n in range(ATTN_BWD_BLOCKS)]
            dk_groups = [[] for _ in windows]
            dv_groups = [[] for _ in windows]
            for j in range(N_KV_HEADS):
                stacks, deltas, dq_ts = [], [], []
                for row0, _, _, _, _, _ in windows:
                    qst = _stack_pairs(q_ref, row0, j)
                    dost = _stack_pairs(do_ref, row0, j)
                    prod = dost.astype(F32) * _stack_pairs(o_ref, row0, j).astype(F32)
                    prod_hi = prod.astype(BF16)
                    prod_lo = (prod - prod_hi.astype(F32)).astype(BF16)
                    stacks.append((qst, dost))
                    deltas.append(lax.dot_general(indicator, prod_hi, NT_DIMS, preferred_element_type=F32)
                                  + lax.dot_general(indicator, prod_lo, NT_DIMS, preferred_element_type=F32))
                    dq_ts.append(jnp.zeros((LANES, STACK), F32))
                dk_par = [[] for _ in windows]
                dv_par = [[] for _ in windows]
                for parity in range(2):
                    heads = [j * GROUP + 2 * p + parity for p in range(PAIRS)]
                    sink = _per_pair_row([sink_ref[h] * LOG2E for h in heads])
                    for n, (row0, _, _, k_halves, v_halves, bias) in enumerate(windows):
                        qst, dost = stacks[n]
                        kk, vv = k_halves[j][parity], v_halves[j][parity]
                        s = lax.dot_general(kk, qst, NT_DIMS, preferred_element_type=F32) * SCORE_SCALE + bias
                        lse = jnp.concatenate([lse_ref[h:h + 1, pl.ds(row0, WINDOW)] for h in heads], axis=1)
                        p = jnp.exp2(s - lse)
                        dp = lax.dot_general(vv, dost, NT_DIMS, preferred_element_type=F32)
                        delta = deltas[n][parity:parity + 1, :]
                        dsb = (p * (dp - delta)).astype(BF16)
                        dq_ts[n] = dq_ts[n] + lax.dot_general(kk, dsb, TN_DIMS, preferred_element_type=F32)
                        dk_par[n].append(jnp.dot(dsb, qst, preferred_element_type=F32))
                        dv_par[n].append(jnp.dot(p.astype(BF16), dost, preferred_element_type=F32))
                        weighted = jnp.exp2(sink - lse) * delta
                        for pr, h in enumerate(heads):
                            dsink = dsink - jnp.where(
                                lane == h, jnp.sum(weighted[:, pr * WINDOW:(pr + 1) * WINDOW]), 0.0)
                for n, (row0, _, _, _, _, _) in enumerate(windows):
                    dq_st = jnp.transpose((dq_ts[n] * ATTN_SCALE).astype(BF16))
                    for pr in range(PAIRS):
                        dq_ref[pl.ds(row0, WINDOW), (j * PAIRS + pr) * LANES:(j * PAIRS + pr + 1) * LANES] = (
                            dq_st[pr * WINDOW:(pr + 1) * WINDOW, :])
                    dk_groups[n].append(both_heads(dk_par[n][0], dk_par[n][1]))
                    dv_groups[n].append(both_heads(dv_par[n][0], dv_par[n][1]))
            for n, (_, prev, cur, _, _, _) in enumerate(windows):
                blk = jnp.concatenate([jnp.where(low, dk_groups[n][0], dk_groups[n][1]) * ATTN_SCALE,
                                       jnp.where(low, dv_groups[n][0], dv_groups[n][1])], axis=1)
                acc_ref[pl.ds(prev, WINDOW), :] += blk[:WINDOW, :]
                acc_ref[pl.ds(cur, WINDOW), :] += blk[WINDOW:, :]
            return dsink

        dsink = lax.fori_loop(0, nblk // ATTN_BWD_BLOCKS, block_group, jnp.zeros((1, LANES), F32))
        ds_ref[0:1, :] += dsink

        @pl.when(i == nt - 1)
        def _():
            dkv_ref[...] = acc_ref[...].astype(BF16)

    tok = pl.BlockSpec((tq, D_MODEL), lambda i: (i, 0))
    return _call(
        body, "attn_bwd", (nt,), [z, z, dattn, attn, lse, sinks],
        [tok, pl.BlockSpec((t, 2 * KV_WIDTH), lambda i: (0, KV_COL // (2 * KV_WIDTH))), tok, tok,
         pl.BlockSpec((N_Q_HEADS, tq), lambda i: (0, i)), pl.BlockSpec(memory_space=pltpu.SMEM)],
        [jax.ShapeDtypeStruct((t, D_MODEL), BF16), jax.ShapeDtypeStruct((t, 2 * KV_WIDTH), BF16),
         jax.ShapeDtypeStruct((SUBLANES, LANES), F32)],
        [tok, _full((t, 2 * KV_WIDTH)), _full((SUBLANES, LANES))],
        scratch=[pltpu.VMEM((t, 2 * KV_WIDTH), F32), pltpu.VMEM((2, 2 * WINDOW, STACK), F32)], rider=rider)


def _inproj_bwd(dq, drest, dkv, x, dx2, vec, w_t, rider):
    t = x.shape[0]
    tm = min(TOKEN_TILE, t)

    def body(dq_ref, dr_ref, dkv_ref, x_ref, dx2_ref, vec_ref, w_ref, gx_ref, acc_ref, db_ref):
        @pl.when(pl.program_id(0) == 0)
        def _():
            acc_ref[...] = jnp.zeros_like(acc_ref)
            db_ref[...] = jnp.zeros_like(db_ref)

        g = vec_ref[0:1, :]
        sc1 = vec_ref[1:2, :]
        dqb, drb, dkvb = dq_ref[...], dr_ref[...], dkv_ref[...]
        dh = jnp.dot(dqb, w_ref[:REF_KV_COL, :], preferred_element_type=F32)
        dh = dh + jnp.dot(drb, w_ref[REF_REST_COL:, :], preferred_element_type=F32)
        dh = dh + jnp.dot(dkvb, w_ref[REF_KV_COL:REF_REST_COL, :], preferred_element_type=F32)
        db_ref[:, :REF_KV_COL] += jnp.sum(dqb.astype(F32), axis=0, keepdims=True)
        db_ref[:, REF_REST_COL:] += jnp.sum(drb.astype(F32), axis=0, keepdims=True)
        db_ref[:, REF_KV_COL:REF_REST_COL] += jnp.sum(dkvb.astype(F32), axis=0, keepdims=True)
        xf = x_ref[...]
        r = lax.rsqrt(jnp.mean(xf * xf, axis=-1, keepdims=True) + EPS)
        xn = xf * r
        gs = g * (1.0 + sc1)
        dh_xn = dh * xn
        sum_dh_xn = jnp.sum(dh_xn, axis=0, keepdims=True)
        acc_ref[0:1, :] += jnp.sum(dh, axis=0, keepdims=True)
        acc_ref[1:2, :] += sum_dh_xn * g
        acc_ref[2:3, :] += sum_dh_xn * (1.0 + sc1)
        gx_ref[...] = dx2_ref[...].astype(F32) + r * (dh * gs - xn * jnp.mean(dh_xn * gs, axis=-1, keepdims=True))

    tok = pl.BlockSpec((tm, D_MODEL), lambda i: (i, 0))
    return _call(
        body, "inproj_bwd", (t // tm,), [dq, drest, dkv, x, dx2, vec, w_t],
        [tok, pl.BlockSpec((tm, REST_WIDTH), lambda i: (i, 0)),
         pl.BlockSpec((tm, 2 * KV_WIDTH), lambda i: (i, 0)), tok, tok,
         _full((SUBLANES, D_MODEL)), _full((IN_WIDTH, D_MODEL))],
        [jax.ShapeDtypeStruct((t, D_MODEL), F32), jax.ShapeDtypeStruct((SUBLANES, D_MODEL), F32),
         jax.ShapeDtypeStruct((1, IN_WIDTH), F32)],
        [tok, _full((SUBLANES, D_MODEL)), _full((1, IN_WIDTH))], rider=rider)


def _weight_grad(b, a, name, bn, rows=None, row0=0, into=None, rider=None):
    t, n = b.shape
    m = a.shape[1]
    rows = n if rows is None else rows
    tk = min(TOKEN_TILE, t)
    for cand in (4 * TOKEN_TILE, 2 * TOKEN_TILE):
        if t % cand == 0 and 2 * cand * (bn + m) * 2 + bn * m * 4 <= WGRAD_VMEM:
            tk = cand
            break
    nk = t // tk
    block0 = row0 // bn

    def body(b_ref, a_ref, *rest):
        out_ref, acc_ref = rest[-2:]
        k = pl.program_id(1)

        @pl.when(k == 0)
        def _():
            acc_ref[...] = jnp.zeros_like(acc_ref)

        acc_ref[...] += lax.dot_general(b_ref[...], a_ref[...], TN_DIMS, preferred_element_type=F32)

        @pl.when(k == nk - 1)
        def _():
            out_ref[...] = acc_ref[...].astype(BF16)

    outs, routs = _call(
        body, name, (n // bn, nk), [b, a] + ([] if into is None else [into]),
        [pl.BlockSpec((tk, bn), lambda j, k: (k, j)), pl.BlockSpec((tk, m), lambda j, k: (k, 0))]
        + ([] if into is None else [ANY]),
        [jax.ShapeDtypeStruct((rows, m), BF16)], [pl.BlockSpec((bn, m), lambda j, k: (block0 + j, 0))],
        scratch=[pltpu.VMEM((bn, m), F32)], rider=rider, aliases=None if into is None else {2: 0})
    return outs[0], routs


def _to_rows(v):
    n = v.shape[0]
    padded = -(-n // (SUBLANES * LANES)) * SUBLANES * LANES
    return jnp.pad(v, (0, padded - n)).reshape(padded // LANES, LANES)


def _vec_rows(*rows):
    stacked = jnp.concatenate([r.reshape(1, D_MODEL) for r in rows], axis=0)
    return jnp.pad(stacked, ((0, SUBLANES - len(rows)), (0, 0)))


def kernel(x, c, w_ada, b_ada, g_mix, w_in, b_in, sinks, conv_w, w_out, g_ffn, w_ffn_in, w_ffn_out, g_final, loss_target, m_w_ada, m_b_ada, m_g_mix, m_w_in, m_b_in, m_sinks, m_conv_w, m_w_out, m_g_ffn, m_w_ffn_in, m_w_ffn_out, m_g_final, v_w_ada, v_b_ada, v_g_mix, v_w_in, v_b_in, v_sinks, v_conv_w, v_w_out, v_g_ffn, v_w_ffn_in, v_w_ffn_out, v_g_final):
    ix, iy, ic = _my_place()
    me = 4 * ix + 2 * iy + ic
    xs = x[0]
    target = loss_target[0]
    ada_cols = w_ada.shape[2]
    conv_cols = conv_w.shape[2]

    wt_in, wt_fi = jnp.transpose(w_in[0]), jnp.transpose(w_ffn_in[0])
    b_cols = lax.dynamic_slice_in_dim(b_ada, me * ada_cols, ada_cols, axis=1)
    g_in, (cast_fi, cast_out, cast_fo), first, mod_all = _gather_first_weight(
        wt_in, [wt_fi, w_out[0], w_ffn_out[0]], _to_rows(jnp.concatenate([c[0], conv_w[0].reshape(-1)])),
        w_ada[0], b_cols)
    first = first.reshape(N_DEV, -1)
    c_all = first[:, :D_MODEL]
    conv_full = jnp.transpose(first[:, D_MODEL:D_MODEL + 3 * conv_cols].reshape(N_DEV, 3, conv_cols), (1, 0, 2))
    conv_full = conv_full.reshape(3, D_MODEL)
    mod = lax.dynamic_index_in_dim(mod_all, me, axis=1, keepdims=False).reshape(N_MOD, D_MODEL)
    sh1, sc1, ga1, sh2, sc2, ga2 = [mod[i:i + 1] for i in range(N_MOD)]
    w_in_t = g_in.reshape(IN_WIDTH, D_MODEL)
    (z, h1), (g_fi, g_out) = _inproj_fwd(xs, _vec_rows(g_mix, sc1, sh1), w_in_t, b_in,
                                         _gather_rider([cast_fi, cast_out]))
    w_fi_t = g_fi.reshape(2 * D_FF, D_MODEL)
    w_out_full = g_out.reshape(D_MODEL, D_MODEL)
    (attn, lse), (g_fo,) = _attn_fwd(z, sinks[0], _gather_rider([cast_fo]))
    w_fo_full = g_fo.reshape(D_FF, D_MODEL)
    merged, x2, h2, oproj = _mix_fwd(
        xs, attn, z, _vec_rows(ga1, g_ffn, sc2, sh2, conv_full[0], conv_full[1], conv_full[2]), w_out_full)
    gu, act = _ffn_fwd(h2, w_fi_t)
    dx3, df, dgu, acc_l = _ffn_out_loss(act, gu, x2, target, _vec_rows(ga2, g_final), w_fo_full)

    gw_fo, _ = _weight_grad(act, df, "wgrad_ffn_out", D_FF)
    gw_fi, _ = _weight_grad(dgu, h2, "wgrad_ffn_in", D_FF)
    blocks_fo = gw_fo.reshape(N_DEV, D_FF // N_DEV, D_MODEL)
    blocks_fi = gw_fi.reshape(N_DEV, 2 * D_FF // N_DEV, D_MODEL)
    (dx2, acc_f), (sib_fo, sib_fi) = _ffn_in_bwd(dgu, x2, dx3, _vec_rows(g_ffn, sc2), w_fi_t,
                                                 _sibling_rider([blocks_fo, blocks_fi]))
    sums_fo, mine_fo = _sibling_sum(blocks_fo, sib_fo, "sibling_sum_ffn_out")
    sums_fi, mine_fi = _sibling_sum(blocks_fi, sib_fi, "sibling_sum_ffn_in")
    (dout, dattn, drest, acc_m), (ici_fo, ici_fi) = _mix_bwd(
        dx2, oproj, attn, z, _vec_rows(ga1, conv_full[0], conv_full[1], conv_full[2]), w_out_full,
        _chip_rider([sums_fo, sums_fi]))
    gw_out, _ = _weight_grad(merged, dout, "wgrad_out", D_MODEL)
    blocks_out = gw_out.reshape(N_DEV, D_MODEL // N_DEV, D_MODEL)
    (dq, dkv, dsink), (sib_out,) = _attn_bwd(z, dattn, attn, lse, sinks[0], _sibling_rider([blocks_out]))
    sums_out, mine_out = _sibling_sum(blocks_out, sib_out, "sibling_sum_out")
    gw_in, (ici_out,) = _weight_grad(drest, h1, "wgrad_in_rest", IN_CHUNK, rows=IN_WIDTH, row0=REF_REST_COL,
                                     rider=_chip_rider([sums_out]))
    gw_in, _ = _weight_grad(dq, h1, "wgrad_in_q", D_MODEL, rows=IN_WIDTH, row0=0, into=gw_in)
    gw_in, _ = _weight_grad(dkv, h1, "wgrad_in_kv", 2 * KV_WIDTH, rows=IN_WIDTH, row0=REF_KV_COL, into=gw_in)
    blocks_in = gw_in.reshape(N_DEV, IN_WIDTH // N_DEV, D_MODEL)
    (sib_in,) = _carry(_sibling_rider([blocks_in]), "sibling_w_in")
    sums_in, mine_in = _sibling_sum(blocks_in, sib_in, "sibling_sum_in")
    (grad_x, acc_i, db_in), (ici_in,) = _inproj_bwd(dq, drest, dkv, xs, dx2, _vec_rows(g_mix, sc1), w_in_t,
                                                    _chip_rider([sums_in]))

    widen = lambda vec: jnp.pad(vec, (0, -vec.shape[0] % D_MODEL))
    packed = jnp.concatenate([
        acc_i[0], acc_i[1], acc_m[0], acc_f[0], acc_f[1], acc_l[2],
        acc_i[2], widen(db_in[0]), acc_f[2], acc_l[1],
        acc_m[1], acc_m[2], acc_m[3], widen(dsink[0]), acc_l[0],
        jnp.zeros(((PACK_ROWS - PACK_SQERR - 1) * D_MODEL,), F32)]).reshape(PACK_ROWS, D_MODEL)
    packed_all = _small_allgather(packed, "gather_small")
    dmod_all = packed_all[:, PACK_DMOD:PACK_DMOD + N_MOD, :].reshape(N_DEV, N_MOD * D_MODEL)
    dmod_cols = lax.dynamic_slice_in_dim(dmod_all, me * ada_cols, ada_cols, axis=1)
    g_w_ada = _ada_weight_grad(c_all, dmod_cols)
    row_of = lambda a: a.reshape(1, -1)
    small, g_conv_full, loss = _small_finalize(packed_all, {
        "b_ada": (b_ada, m_b_ada, v_b_ada), "g_mix": (g_mix, m_g_mix, v_g_mix), "b_in": (b_in, m_b_in, v_b_in),
        "g_ffn": (g_ffn, m_g_ffn, v_g_ffn), "sinks": (sinks, m_sinks, v_sinks),
        "g_final": (row_of(g_final), row_of(m_g_final), row_of(v_g_final))})
    small["g_final"] = tuple(o.reshape(g_final.shape) for o in small["g_final"])
    g_conv = lax.dynamic_slice_in_dim(g_conv_full, me * conv_cols, conv_cols, axis=1)
    d_conv, nm_conv, nv_conv = _adamw(conv_w[0], g_conv, m_conv_w[0], v_conv_w[0], "adamw_conv_w")
    small["conv_w"] = (g_conv[None], d_conv[None], nm_conv[None], nv_conv[None])

    def reduced(mine, ici, w, m, v, name, transposed=False):
        turn = jnp.transpose if transposed else (lambda a: a)
        return tuple(turn(o)[None] for o in _chip_sum_adamw(mine, ici, turn(w[0]), turn(m[0]), turn(v[0]), name))

    d_ada, nm_ada, nv_ada = _adamw(w_ada[0], g_w_ada, m_w_ada[0], v_w_ada[0], "adamw_w_ada")
    res = {
        "w_ada": (g_w_ada[None], d_ada[None], nm_ada[None], nv_ada[None]),
        "w_in": reduced(mine_in, ici_in, w_in, m_w_in, v_w_in, "adamw_w_in", transposed=True),
        "w_out": reduced(mine_out, ici_out, w_out, m_w_out, v_w_out, "adamw_w_out"),
        "w_ffn_in": reduced(mine_fi, ici_fi, w_ffn_in, m_w_ffn_in, v_w_ffn_in, "adamw_w_ffn_in", transposed=True),
        "w_ffn_out": reduced(mine_fo, ici_fo, w_ffn_out, m_w_ffn_out, v_w_ffn_out, "adamw_w_ffn_out"),
    }
    res.update(small)
    order = ["w_ada", "b_ada", "g_mix", "w_in", "b_in", "sinks", "conv_w", "w_out", "g_ffn", "w_ffn_in", "w_ffn_out",
             "g_final"]
    outs = [loss.reshape(()), grad_x[None]]
    for k in range(4):
        outs += [res[n][k] for n in order]
    return tuple(outs)
```

```python
import functools
import math

import jax
import jax.numpy as jnp
from jax import lax
from jax.experimental import pallas as pl
from jax.experimental.pallas import tpu as pltpu

F32 = jnp.float32
BF16 = jnp.bfloat16
GRAD_STREAM = F32

D_MODEL = 1024
HEAD_DIM = 64
N_Q_HEADS = 16
N_KV_HEADS = 2
GROUP = 8
WINDOW = 128
KV_WIDTH = N_KV_HEADS * HEAD_DIM
D_FF = 2816
IN_WIDTH = 6400
N_MOD = 6
EPS = 1e-6
N_DEV = 8
REST_WIDTH = 5 * D_MODEL
KV_COL = D_MODEL + REST_WIDTH
ATTN_SCALE = HEAD_DIM ** -0.5

ADAM_LR = 0.001
ADAM_B1 = 0.9
ADAM_B2 = 0.999
ADAM_EPS = 1e-08
ADAM_WD = 0.01
ADAM_STEP = 10

LANES = 128
SUBLANES = 8
BF16_ROWS = 16
VMEM_LIMIT = 56 * 1024 * 1024
TOKEN_TILE = 512
ATTN_TILE = 1024
FF_CHUNK = 256
ROW_PARTS = 2
WGRAD_VMEM = 40 * 1024 * 1024
MESH = pl.DeviceIdType.MESH
ANY = pl.BlockSpec(memory_space=pl.ANY)

NT_DIMS = (((1,), (1,)), ((), ()))
TN_DIMS = (((0,), (0,)), ((), ()))
CHIP_FLIPS = [(0, 0), (1, 0), (0, 1), (1, 1)]


def _full(shape):
    return pl.BlockSpec(shape, lambda *_: (0,) * len(shape))


def _my_place():
    return lax.axis_index("x"), lax.axis_index("y"), lax.axis_index("c")


def _flip(v, bit):
    return 1 - v if bit else v


def _sigmoid(v):
    return 1.0 / (1.0 + jnp.exp2(v * (-1.4426950408889634)))


class _Rider:
    def __init__(self, ins, out_shapes, sem_shapes, first=None, mid=None, last=None, ins_in_vmem=False):
        self.ins, self.out_shapes, self.sem_shapes = list(ins), list(out_shapes), list(sem_shapes)
        self.in_specs = [_full(a.shape) if ins_in_vmem else ANY for a in self.ins]
        self.hooks = [(when, fn) for when, fn in (("first", first), ("mid", mid), ("last", last)) if fn is not None]


def _call(body, name, grid, args, in_specs, out_shape, out_specs, scratch=(), rider=None, aliases=None):
    n_in, n_out, n_scr = len(args), len(out_shape), len(scratch)
    r_in = rider.ins if rider else []
    r_out = rider.out_shapes if rider else []
    r_sem = rider.sem_shapes if rider else []
    nsteps = math.prod(grid)

    def full_body(*refs):
        pos = 0
        groups = []
        for size in (n_in, len(r_in), n_out, len(r_out), n_scr, len(r_sem)):
            groups.append(refs[pos:pos + size])
            pos += size
        ins, rins, outs, routs, scr, rsems = groups
        step = pl.program_id(0)
        for axis in range(1, len(grid)):
            step = step * grid[axis] + pl.program_id(axis)
        at = {"first": 0, "mid": (3 * nsteps) // 4, "last": nsteps - 1}
        hooks = rider.hooks if rider else []
        for when, fn in hooks:
            if when != "last":
                pl.when(step == at[when])(functools.partial(fn, rins, routs, rsems))
        body(*ins, *outs, *scr)
        for when, fn in hooks:
            if when == "last":
                pl.when(step == at[when])(functools.partial(fn, rins, routs, rsems))

    outs = pl.pallas_call(
        full_body, name=name, grid=grid,
        out_shape=list(out_shape) + list(r_out),
        in_specs=list(in_specs) + (rider.in_specs if rider else []),
        out_specs=list(out_specs) + [ANY] * len(r_out),
        scratch_shapes=list(scratch) + list(r_sem),
        input_output_aliases=dict(aliases or {}),
        compiler_params=pltpu.CompilerParams(dimension_semantics=("arbitrary",) * len(grid),
                                             vmem_limit_bytes=VMEM_LIMIT),
    )(*args, *r_in)
    return list(outs[:n_out]), list(outs[n_out:])


def _gather_rider(shards):
    n = len(shards)

    def setup(outs, sems):
        x, y, c = _my_place()
        send_sems, recv_sems, _ = sems
        chips = [(1 - x, y), (x, 1 - y), (1 - x, 1 - y)]

        def block(w, place):
            return outs[w].at[4 * place[0] + 2 * place[1] + place[2]]

        def copy(w, k, place, to, src=None):
            return pltpu.make_async_remote_copy(
                src_ref=block(w, place) if src is None else src, dst_ref=block(w, place),
                send_sem=send_sems.at[w, k], recv_sem=recv_sems.at[w, k], device_id=to, device_id_type=MESH)

        return (x, y, c), (x, y, 1 - c), chips, block, copy

    def first(ins, outs, sems):
        me, sibling, chips, block, copy = setup(outs, sems)
        for w in range(n):
            pltpu.make_async_copy(ins[w], block(w, me), sems[2].at[w]).start()
            copy(w, 0, me, sibling, src=ins[w]).start()
            for j, chip in enumerate(chips):
                copy(w, 1 + j, me, (*chip, me[2]), src=ins[w]).start()

    def mid(ins, outs, sems):
        me, sibling, chips, block, copy = setup(outs, sems)
        for w in range(n):
            for j, chip in enumerate(chips):
                copy(w, 1 + j, (*chip, me[2]), me).wait_recv()
                copy(w, 4 + j, (*chip, me[2]), sibling).start()

    def last(ins, outs, sems):
        me, sibling, chips, block, copy = setup(outs, sems)
        for w in range(n):
            copy(w, 0, sibling, me).wait_recv()
            for j, chip in enumerate(chips):
                copy(w, 4 + j, (*chip, 1 - me[2]), me).wait_recv()
            copy(w, 0, me, sibling, src=ins[w]).wait_send()
            for j, chip in enumerate(chips):
                copy(w, 1 + j, me, (*chip, me[2]), src=ins[w]).wait_send()
                copy(w, 4 + j, (*chip, me[2]), sibling).wait_send()
            pltpu.make_async_copy(ins[w], block(w, me), sems[2].at[w]).wait()

    return _Rider(
        shards, [jax.ShapeDtypeStruct((N_DEV,) + s.shape, BF16) for s in shards],
        [pltpu.SemaphoreType.DMA((n, N_DEV - 1)), pltpu.SemaphoreType.DMA((n, N_DEV - 1)),
         pltpu.SemaphoreType.DMA((n,))],
        first=first, mid=mid, last=last, ins_in_vmem=True)


def _sibling_rider(gblocks):
    n = len(gblocks)

    def copies(ins, outs, sems):
        x, y, c = _my_place()
        send_sems, recv_sems = sems
        made = []
        for w in range(n):
            for f, (fx, fy) in enumerate(CHIP_FLIPS):
                chip = 4 * _flip(x, fx) + 2 * _flip(y, fy)
                made.append(pltpu.make_async_remote_copy(
                    src_ref=ins[w].at[chip + 1 - c], dst_ref=outs[w].at[f], send_sem=send_sems.at[w, f],
                    recv_sem=recv_sems.at[w, f], device_id=(x, y, 1 - c), device_id_type=MESH))
        return made

    def first(ins, outs, sems):
        for cp in copies(ins, outs, sems):
            cp.start()

    def last(ins, outs, sems):
        for cp in copies(ins, outs, sems):
            cp.wait_recv()
            cp.wait_send()

    return _Rider(gblocks, [jax.ShapeDtypeStruct((4,) + g.shape[1:], BF16) for g in gblocks],
                  [pltpu.SemaphoreType.DMA((n, 4))] * 2, first=first, last=last)


def _chip_rider(sums):
    n = len(sums)

    def copies(ins, outs, sems):
        x, y, c = _my_place()
        send_sems, recv_sems = sems
        made = []
        for w in range(n):
            for f in (1, 2, 3):
                fx, fy = CHIP_FLIPS[f]
                made.append(pltpu.make_async_remote_copy(
                    src_ref=ins[w].at[f - 1], dst_ref=outs[w].at[f - 1], send_sem=send_sems.at[w, f - 1],
                    recv_sem=recv_sems.at[w, f - 1], device_id=(_flip(x, fx), _flip(y, fy), c), device_id_type=MESH))
        return made

    def first(ins, outs, sems):
        for cp in copies(ins, outs, sems):
            cp.start()

    def last(ins, outs, sems):
        for cp in copies(ins, outs, sems):
            cp.wait_recv()
            cp.wait_send()

    return _Rider(sums, [jax.ShapeDtypeStruct(s.shape, BF16) for s in sums],
                  [pltpu.SemaphoreType.DMA((n, 3))] * 2, first=first, last=last)


def _push_to_all(v_ref, out_ref, send_sems, recv_sems, local_sem, wait=True):
    x, y, c = _my_place()
    me = 4 * x + 2 * y + c
    mine = pltpu.make_async_copy(v_ref, out_ref.at[me], local_sem)
    mine.start()
    sends = []
    for k in range(1, N_DEV):
        px, py, pc = _flip(x, k & 4), _flip(y, k & 2), _flip(c, k & 1)
        cp = pltpu.make_async_remote_copy(
            src_ref=v_ref, dst_ref=out_ref.at[me], send_sem=send_sems.at[k - 1], recv_sem=recv_sems.at[k - 1],
            device_id=(px, py, pc), device_id_type=MESH)
        cp.start()
        sends.append(cp)

    def finish():
        for k in range(1, N_DEV):
            px, py, pc = _flip(x, k & 4), _flip(y, k & 2), _flip(c, k & 1)
            pltpu.make_async_remote_copy(
                src_ref=v_ref, dst_ref=out_ref.at[4 * px + 2 * py + pc], send_sem=send_sems.at[k - 1],
                recv_sem=recv_sems.at[k - 1], device_id=(px, py, pc), device_id_type=MESH).wait_recv()
        for cp in sends:
            cp.wait_send()
        mine.wait()

    if wait:
        finish()
    return finish


def _small_allgather(v, name):
    def body(v_ref, out_ref, send_sems, recv_sems, local_sem):
        _push_to_all(v_ref, out_ref, send_sems, recv_sems, local_sem)

    return pl.pallas_call(
        body, name=name,
        out_shape=jax.ShapeDtypeStruct((N_DEV,) + v.shape, F32),
        in_specs=[pl.BlockSpec(memory_space=pltpu.VMEM)],
        out_specs=pl.BlockSpec(memory_space=pltpu.VMEM),
        scratch_shapes=[pltpu.SemaphoreType.DMA((N_DEV - 1,)), pltpu.SemaphoreType.DMA((N_DEV - 1,)),
                        pltpu.SemaphoreType.DMA],
        compiler_params=pltpu.CompilerParams(vmem_limit_bytes=VMEM_LIMIT),
    )(v)


def _gather_first_weight(shard, others, cond_rows, w_ada, b_cols):
    n = len(others)
    ada_cols = w_ada.shape[1]
    c_rows = D_MODEL // LANES

    def body(*refs):
        w_ref, other_refs = refs[0], refs[1:1 + n]
        cond_ref, wada_ref, bcols_ref = refs[1 + n:4 + n]
        out_ref, cast_refs = refs[4 + n], refs[5 + n:5 + 2 * n]
        cond_all_ref, mod_all_ref = refs[5 + 2 * n:7 + 2 * n]
        mine_ref, mod_ref, send_sems, recv_sems, local_sem, small_send, small_recv, small_local = refs[7 + 2 * n:]
        x, y, c = _my_place()
        me, sibling = (x, y, c), (x, y, 1 - c)
        xnb, ynb, diag = (1 - x, y), (x, 1 - y), (1 - x, 1 - y)
        half = shard.shape[0] // 2

        def block(place, part=None):
            ref = out_ref.at[4 * place[0] + 2 * place[1] + place[2]]
            return ref if part is None else ref.at[pl.ds(part * half, half)]

        def copy(k, place, to, part=None, src=None):
            return pltpu.make_async_remote_copy(
                src_ref=block(place, part) if src is None else src, dst_ref=block(place, part),
                send_sem=send_sems.at[k], recv_sem=recv_sems.at[k], device_id=to, device_id_type=MESH)

        finish_cond = _push_to_all(cond_ref, cond_all_ref, small_send.at[0], small_recv.at[0], small_local.at[0],
                                   wait=False)
        mine_ref[...] = w_ref[...].astype(BF16)
        local = pltpu.make_async_copy(mine_ref, block(me), local_sem)
        local.start()
        started = [copy(0, me, sibling, src=mine_ref), copy(1, me, (*xnb, c), src=mine_ref),
                   copy(2, me, (*ynb, c), src=mine_ref)]
        for cp in started:
            cp.start()
        finish_cond()
        mod = jnp.zeros((N_DEV, ada_cols), F32) + bcols_ref[...]
        for r in range(c_rows):
            cf = cond_all_ref[:, r, :]
            act = (cf * _sigmoid(cf)).astype(BF16)
            mod = mod + jnp.dot(act, wada_ref[r * LANES:(r + 1) * LANES, :].astype(BF16),
                                preferred_element_type=F32)
        mod_ref[...] = mod
        finish_mod = _push_to_all(mod_ref, mod_all_ref, small_send.at[1], small_recv.at[1], small_local.at[1],
                                  wait=False)
        for o_ref, c_ref in zip(other_refs, cast_refs):
            c_ref[...] = o_ref[...].astype(BF16)
        def start(cp):
            cp.start()
            started.append(cp)

        copy(1, (*xnb, c), me).wait_recv()
        start(copy(3, (*xnb, c), (*ynb, c), part=0))
        start(copy(5, (*xnb, c), sibling))
        copy(2, (*ynb, c), me).wait_recv()
        start(copy(4, (*ynb, c), (*xnb, c), part=1))
        start(copy(6, (*ynb, c), sibling))
        copy(3, (*diag, c), me, part=0).wait_recv()
        start(copy(7, (*diag, c), sibling, part=0))
        copy(4, (*diag, c), me, part=1).wait_recv()
        start(copy(8, (*diag, c), sibling, part=1))
        copy(0, sibling, me).wait_recv()
        copy(5, (*xnb, 1 - c), me).wait_recv()
        copy(6, (*ynb, 1 - c), me).wait_recv()
        copy(7, (*diag, 1 - c), me, part=0).wait_recv()
        copy(8, (*diag, 1 - c), me, part=1).wait_recv()
        finish_mod()
        for cp in started:
            cp.wait_send()
        local.wait()

    vmem = pl.BlockSpec(memory_space=pltpu.VMEM)
    outs = pl.pallas_call(
        body, name="gather_w_in",
        out_shape=[jax.ShapeDtypeStruct((N_DEV,) + shard.shape, BF16)]
        + [jax.ShapeDtypeStruct(o.shape, BF16) for o in others]
        + [jax.ShapeDtypeStruct((N_DEV,) + cond_rows.shape, F32), jax.ShapeDtypeStruct((N_DEV, N_DEV, ada_cols), F32)],
        in_specs=[vmem] * (4 + n),
        out_specs=[ANY] + [vmem] * (n + 2),
        scratch_shapes=[pltpu.VMEM(shard.shape, BF16), pltpu.VMEM((N_DEV, ada_cols), F32),
                        pltpu.SemaphoreType.DMA((9,)), pltpu.SemaphoreType.DMA((9,)),
                        pltpu.SemaphoreType.DMA,
                        pltpu.SemaphoreType.DMA((2, N_DEV - 1)), pltpu.SemaphoreType.DMA((2, N_DEV - 1)),
                        pltpu.SemaphoreType.DMA((2,))],
        compiler_params=pltpu.CompilerParams(vmem_limit_bytes=VMEM_LIMIT),
    )(shard, *others, cond_rows, w_ada, b_cols)
    return outs[0], list(outs[1:1 + n]), outs[1 + n], outs[2 + n]


def _carry(rider, name):
    def body(token_ref):
        token_ref[...] = jnp.zeros_like(token_ref)

    _, routs = _call(body, name, (1,), [], [], [jax.ShapeDtypeStruct((SUBLANES, LANES), F32)],
                     [_full((SUBLANES, LANES))], rider=rider)
    return routs


def _ada_weight_grad(c_all, dmod_cols):
    cols = dmod_cols.shape[1]

    def body(c_ref, d_ref, out_ref):
        cf = c_ref[...]
        act = (cf * _sigmoid(cf)).astype(BF16)
        out_ref[...] = lax.dot_general(act, d_ref[...].astype(BF16), TN_DIMS, preferred_element_type=F32)

    return pl.pallas_call(
        body, name="ada_weight_grad",
        out_shape=jax.ShapeDtypeStruct((D_MODEL, cols), F32),
        in_specs=[pl.BlockSpec(memory_space=pltpu.VMEM)] * 2,
        out_specs=pl.BlockSpec(memory_space=pltpu.VMEM),
        compiler_params=pltpu.CompilerParams(vmem_limit_bytes=VMEM_LIMIT),
    )(c_all, dmod_cols)


PACK_ROWS = 24
PACK_DMOD = 0
PACK_PARAMS = {"g_mix": (6, D_MODEL), "b_in": (7, IN_WIDTH), "g_ffn": (14, D_MODEL), "g_final": (15, D_MODEL),
               "sinks": (19, N_Q_HEADS)}
PACK_CONV = 16
PACK_SQERR = 20


def _small_finalize(packed_all, params):
    names = ["b_ada"] + list(PACK_PARAMS)
    layout = dict(PACK_PARAMS, b_ada=(PACK_DMOD, N_MOD * D_MODEL))
    n = len(names)

    def body(*refs):
        p_ref = refs[0]
        ins = refs[1:1 + 3 * n]
        outs = refs[1 + 3 * n:1 + 7 * n]
        conv_ref, loss_ref = refs[1 + 7 * n:]
        total = p_ref[0]
        for d in range(1, N_DEV):
            total = total + p_ref[d]
        for k, name in enumerate(names):
            row0, width = layout[name]
            w_ref, m_ref, v_ref = ins[3 * k:3 * k + 3]
            g_ref, d_ref, nm_ref, nv_ref = outs[4 * k:4 * k + 4]
            for chunk in range(-(-width // D_MODEL)):
                lo = chunk * D_MODEL
                hi = min(lo + D_MODEL, width)
                g = total[row0 + chunk:row0 + chunk + 1, :hi - lo]
                g_ref[:, lo:hi] = g
                d_ref[:, lo:hi], nm_ref[:, lo:hi], nv_ref[:, lo:hi] = _adamw_update(
                    w_ref[:, lo:hi], g, m_ref[:, lo:hi], v_ref[:, lo:hi])
        conv_ref[...] = total[PACK_CONV:PACK_CONV + 3, :]
        loss_ref[...] = (0.5 / D_MODEL) * jnp.sum(total[PACK_SQERR:PACK_SQERR + 1, :], keepdims=True)

    vmem = pl.BlockSpec(memory_space=pltpu.VMEM)
    flat = [a for name in names for a in params[name]]
    out_shape = [jax.ShapeDtypeStruct(params[name][0].shape, F32) for name in names for _ in range(4)]
    outs = pl.pallas_call(
        body, name="small_finalize",
        out_shape=out_shape + [jax.ShapeDtypeStruct((3, D_MODEL), F32), jax.ShapeDtypeStruct((1, 1), F32)],
        in_specs=[vmem] * (1 + 3 * n),
        out_specs=[vmem] * (4 * n + 2),
        compiler_params=pltpu.CompilerParams(vmem_limit_bytes=VMEM_LIMIT),
    )(packed_all, *flat)
    return {name: tuple(outs[4 * k:4 * k + 4]) for k, name in enumerate(names)}, outs[4 * n], outs[4 * n + 1]


def _row_tile(rows, multiple):
    for cand in range(min(rows, 256), 0, -1):
        if rows % cand == 0 and cand % multiple == 0:
            return cand
    return rows


def _adamw_update(w, g, m, v):
    c1 = 1.0 / (1.0 - ADAM_B1 ** ADAM_STEP)
    c2 = 1.0 / (1.0 - ADAM_B2 ** ADAM_STEP)
    nm = ADAM_B1 * m + (1.0 - ADAM_B1) * g
    nv = ADAM_B2 * v + (1.0 - ADAM_B2) * (g * g)
    delta = -ADAM_LR * ((nm * c1) / (jnp.sqrt(nv * c2) + ADAM_EPS) + ADAM_WD * w)
    return delta, nm, nv


def _adamw(w, g, m, v, name):
    rows, cols = w.shape
    tile = _row_tile(rows, SUBLANES)

    def body(w_ref, g_ref, m_ref, v_ref, d_ref, nm_ref, nv_ref):
        d_ref[...], nm_ref[...], nv_ref[...] = _adamw_update(w_ref[...], g_ref[...], m_ref[...], v_ref[...])

    spec = pl.BlockSpec((tile, cols), lambda i: (i, 0))
    outs, _ = _call(body, name, (rows // tile,), [w, g, m, v], [spec] * 4,
                    [jax.ShapeDtypeStruct((rows, cols), F32)] * 3, [spec] * 3)
    return outs


def _sibling_sum(gblocks, sib, name):
    _, r, cdim = gblocks.shape
    tile = _row_tile(r, BF16_ROWS)
    x, y, c = _my_place()
    table = jnp.stack([4 * _flip(x, fx) + 2 * _flip(y, fy) + c for fx, fy in CHIP_FLIPS]).astype(jnp.int32)

    def body(table_ref, own0, own1, own2, own3, sib_ref, sums_ref, mine_ref):
        mine_ref[...] = own0[...].astype(F32) + sib_ref[0].astype(F32)
        for f, own in ((1, own1), (2, own2), (3, own3)):
            sums_ref[f - 1] = (own[...].astype(F32) + sib_ref[f].astype(F32)).astype(BF16)

    own_specs = [pl.BlockSpec((None, tile, cdim), functools.partial(lambda i, tab, f: (tab[f], i, 0), f=f))
                 for f in range(4)]
    return pl.pallas_call(
        body, name=name,
        grid_spec=pltpu.PrefetchScalarGridSpec(
            num_scalar_prefetch=1, grid=(r // tile,),
            in_specs=own_specs + [pl.BlockSpec((4, tile, cdim), lambda i, tab: (0, i, 0))],
            out_specs=[pl.BlockSpec((3, tile, cdim), lambda i, tab: (0, i, 0)),
                       pl.BlockSpec((tile, cdim), lambda i, tab: (i, 0))]),
        out_shape=[jax.ShapeDtypeStruct((3, r, cdim), BF16), jax.ShapeDtypeStruct((r, cdim), F32)],
        compiler_params=pltpu.CompilerParams(dimension_semantics=("arbitrary",), vmem_limit_bytes=VMEM_LIMIT),
    )(table, gblocks, gblocks, gblocks, gblocks, sib)


def _chip_sum_adamw(mine, ici, w, m, v, name):
    r, cdim = mine.shape
    tile = _row_tile(r, BF16_ROWS)

    def body(mine_ref, ici_ref, w_ref, m_ref, v_ref, g_ref, d_ref, nm_ref, nv_ref):
        g = mine_ref[...]
        for f in range(3):
            g = g + ici_ref[f].astype(F32)
        g_ref[...] = g
        d_ref[...], nm_ref[...], nv_ref[...] = _adamw_update(w_ref[...], g, m_ref[...], v_ref[...])

    spec = pl.BlockSpec((tile, cdim), lambda i: (i, 0))
    outs, _ = _call(
        body, name, (r // tile,), [mine, ici, w, m, v],
        [spec, pl.BlockSpec((3, tile, cdim), lambda i: (0, i, 0)), spec, spec, spec],
        [jax.ShapeDtypeStruct((r, cdim), F32)] * 4, [spec] * 4)
    return outs


REF_KV_COL = D_MODEL
REF_REST_COL = D_MODEL + 2 * KV_WIDTH
IN_CHUNK = 1280
IN_PIECES = ([(0, 0, D_MODEL)]
             + [(D_MODEL + n * IN_CHUNK, REF_REST_COL + n * IN_CHUNK, IN_CHUNK) for n in range(REST_WIDTH // IN_CHUNK)]
             + [(KV_COL, REF_KV_COL, 2 * KV_WIDTH)])


def _inproj_fwd(x, vec, w_t, b_in, rider):
    t = x.shape[0]
    tm = min(TOKEN_TILE, t)

    def body(x_ref, vec_ref, w_ref, b_ref, z_ref, h_ref):
        xf = x_ref[...]
        r = lax.rsqrt(jnp.mean(xf * xf, axis=-1, keepdims=True) + EPS)
        h = (xf * r) * (vec_ref[0:1, :] * (1.0 + vec_ref[1:2, :])) + vec_ref[2:3, :]
        hb = h.astype(BF16)
        h_ref[...] = hb
        for mine, ref, width in IN_PIECES:
            zc = lax.dot_general(hb, w_ref[ref:ref + width, :], NT_DIMS, preferred_element_type=F32)
            z_ref[:, mine:mine + width] = (zc + b_ref[:, ref:ref + width]).astype(BF16)

    return _call(
        body, "inproj_fwd", (t // tm,), [x, vec, w_t, b_in],
        [pl.BlockSpec((tm, D_MODEL), lambda i: (i, 0)), _full((SUBLANES, D_MODEL)),
         _full((IN_WIDTH, D_MODEL)), _full((1, IN_WIDTH))],
        [jax.ShapeDtypeStruct((t, IN_WIDTH), BF16), jax.ShapeDtypeStruct((t, D_MODEL), BF16)],
        [pl.BlockSpec((tm, IN_WIDTH), lambda i: (i, 0)), pl.BlockSpec((tm, D_MODEL), lambda i: (i, 0))],
        rider=rider)


PAIRS = GROUP // 2
STACK = PAIRS * WINDOW


ATTN_BLOCKS = 4
ATTN_BWD_BLOCKS = 1
LOG2E = 1.4426950408889634
LN2 = 0.6931471805599453
SCORE_SCALE = ATTN_SCALE * LOG2E


def _fill_window_bias(bias_ref):
    shape = bias_ref.shape[1:]
    kj = lax.broadcasted_iota(jnp.int32, shape, 0)
    qi = jnp.bitwise_and(lax.broadcasted_iota(jnp.int32, shape, 1), WINDOW - 1)
    in_prev = jnp.logical_and(kj < WINDOW, kj > qi)
    in_cur = jnp.logical_and(kj >= WINDOW, (kj - WINDOW) <= qi)
    bias_ref[0] = jnp.where(in_cur, 0.0, -jnp.inf)
    bias_ref[1] = jnp.where(jnp.logical_or(in_prev, in_cur), 0.0, -jnp.inf)


def _half_tiles(tile):
    low = lax.broadcasted_iota(jnp.int32, tile.shape, 1) < HEAD_DIM
    swapped = jnp.concatenate([tile[:, HEAD_DIM:], tile[:, :HEAD_DIM]], axis=1)
    zero = jnp.zeros_like(tile)
    return ((jnp.where(low, tile, zero), jnp.where(low, zero, swapped)),
            (jnp.where(low, swapped, zero), jnp.where(low, zero, tile)))


def _stack_pairs(ref, row0, j):
    return jnp.concatenate(
        [ref[pl.ds(row0, WINDOW), (j * PAIRS + p) * LANES:(j * PAIRS + p + 1) * LANES] for p in range(PAIRS)], axis=0)


def _per_pair_row(values):
    pair = lax.broadcasted_iota(jnp.int32, (1, STACK), 1) // WINDOW
    row = jnp.full((1, STACK), values[PAIRS - 1], F32)
    for p in range(PAIRS - 2, -1, -1):
        row = jnp.where(pair == p, values[p], row)
    return row


def _attn_fwd(z, sinks, rider):
    t = z.shape[0]
    tq = min(ATTN_TILE, t)
    nblk = tq // WINDOW

    def body(q_ref, kv_ref, sink_ref, o_ref, lse_ref, bias_ref):
        i = pl.program_id(0)

        @pl.when(i == 0)
        def _():
            _fill_window_bias(bias_ref)

        def window(b):
            row0 = pl.multiple_of(b * WINDOW, WINDOW)
            start = i * tq + b * WINDOW
            prev = pl.multiple_of(jnp.maximum(start - WINDOW, 0), WINDOW)
            cur = pl.multiple_of(start, WINDOW)
            kvw = jnp.concatenate([kv_ref[pl.ds(prev, WINDOW), :], kv_ref[pl.ds(cur, WINDOW), :]], axis=0)
            return row0, _half_tiles(kvw[:, :KV_WIDTH]), _half_tiles(kvw[:, KV_WIDTH:]), bias_ref[jnp.minimum(start, 1)]

        def block_group(bb, carry):
            windows = [window(bb * ATTN_BLOCKS + n) for n in range(ATTN_BLOCKS)]
            for j in range(N_KV_HEADS):
                for pr in range(PAIRS):
                    cols = slice((j * PAIRS + pr) * LANES, (j * PAIRS + pr + 1) * LANES)
                    o_ts = [jnp.zeros((LANES, WINDOW), F32) for _ in windows]
                    for parity in range(2):
                        h = j * GROUP + 2 * pr + parity
                        sink = sink_ref[h] * LOG2E
                        for n, (row0, k_halves, v_halves, bias) in enumerate(windows):
                            qp = q_ref[pl.ds(row0, WINDOW), cols]
                            s = lax.dot_general(k_halves[j][parity], qp, NT_DIMS, preferred_element_type=F32)
                            s = s * SCORE_SCALE + bias
                            m = jnp.maximum(jnp.max(s, axis=0, keepdims=True), sink)
                            p = jnp.exp2(s - m)
                            denom = jnp.sum(p, axis=0, keepdims=True) + jnp.exp2(sink - m)
                            pv = lax.dot_general(v_halves[j][parity], p.astype(BF16), TN_DIMS,
                                                 preferred_element_type=F32)
                            o_ts[n] = o_ts[n] + pv * (1.0 / denom)
                            lse_ref[h:h + 1, pl.ds(row0, WINDOW)] = m + jnp.log2(denom)
                    for n, (row0, _, _, _) in enumerate(windows):
                        o_ref[pl.ds(row0, WINDOW), cols] = jnp.transpose(o_ts[n].astype(BF16))
            return carry

        lax.fori_loop(0, nblk // ATTN_BLOCKS, block_group, 0)

    return _call(
        body, "attn_fwd", (t // tq,), [z, z, sinks],
        [pl.BlockSpec((tq, D_MODEL), lambda i: (i, 0)),
         pl.BlockSpec((t, 2 * KV_WIDTH), lambda i: (0, KV_COL // (2 * KV_WIDTH))),
         pl.BlockSpec(memory_space=pltpu.SMEM)],
        [jax.ShapeDtypeStruct((t, D_MODEL), BF16), jax.ShapeDtypeStruct((N_Q_HEADS, t), F32)],
        [pl.BlockSpec((tq, D_MODEL), lambda i: (i, 0)), pl.BlockSpec((N_Q_HEADS, tq), lambda i: (0, i))],
        scratch=[pltpu.VMEM((2, 2 * WINDOW, WINDOW), F32)], rider=rider)


HALO = BF16_ROWS


def _shift_down(u, uh, k):
    rolled = pltpu.roll(u, k, 0)
    row = lax.broadcasted_iota(jnp.int32, (SUBLANES, u.shape[1]), 0)
    top = rolled[:SUBLANES, :]
    for j in range(k):
        top = jnp.where(row == j, uh[HALO - k + j:HALO - k + j + 1, :], top)
    return jnp.concatenate([top, rolled[SUBLANES:, :]], axis=0)


def _shift_up(u, nxt, k):
    n = u.shape[0]
    rolled = pltpu.roll(u, n - k, 0)
    row = lax.broadcasted_iota(jnp.int32, (SUBLANES, u.shape[1]), 0)
    bottom = rolled[n - SUBLANES:, :]
    for j in range(k):
        bottom = jnp.where(row == SUBLANES - k + j, nxt[j:j + 1, :], bottom)
    return jnp.concatenate([rolled[:n - SUBLANES, :], bottom], axis=0)


def _conv_inputs(cc_ref, cx_ref, hc_ref, hx_ref, first_tile):
    cc = cc_ref[...].astype(F32)
    cx = cx_ref[...].astype(F32)
    u = cc * cx
    uh = jnp.where(first_tile, 0.0, hc_ref[...].astype(F32) * hx_ref[...].astype(F32))
    return cc, cx, u, _shift_down(u, uh, 1), _shift_down(u, uh, 2)


def _z_specs(tm, order):
    per_tile = tm // HALO
    cols = [pl.BlockSpec((tm, D_MODEL), functools.partial(lambda i, j: (order(i), j), j=j)) for j in range(1, 6)]
    halos = [pl.BlockSpec((HALO, D_MODEL),
                          functools.partial(lambda i, j: (jnp.maximum(order(i) * per_tile - 1, 0), j), j=j))
             for j in (2, 3)]
    return cols + halos


def _mix_fwd(x, attn, z, vec, w_out):
    t = x.shape[0]
    tm = min(TOKEN_TILE, t)

    def body(x_ref, a_ref, cb_ref, cc_ref, cx_ref, ga_ref, gc_ref, hc_ref, hx_ref, vec_ref, w_ref,
             m_ref, x2_ref, h2_ref, o_ref):
        i = pl.program_id(0)
        _, _, u, u1, u2 = _conv_inputs(cc_ref, cx_ref, hc_ref, hx_ref, i == 0)
        cv = vec_ref[4:5, :] * u2 + vec_ref[5:6, :] * u1 + vec_ref[6:7, :] * u
        conv = cb_ref[...].astype(F32) * cv
        merged = (_sigmoid(ga_ref[...].astype(F32)) * a_ref[...].astype(F32)
                  + _sigmoid(gc_ref[...].astype(F32)) * conv)
        mb = merged.astype(BF16)
        m_ref[...] = mb
        o = jnp.dot(mb, w_ref[...], preferred_element_type=F32)
        o_ref[...] = o.astype(BF16)
        x2 = x_ref[...] + vec_ref[0:1, :] * o
        x2_ref[...] = x2
        r = lax.rsqrt(jnp.mean(x2 * x2, axis=-1, keepdims=True) + EPS)
        h2 = (x2 * r) * (vec_ref[1:2, :] * (1.0 + vec_ref[2:3, :])) + vec_ref[3:4, :]
        h2_ref[...] = h2.astype(BF16)

    tok = pl.BlockSpec((tm, D_MODEL), lambda i: (i, 0))
    outs, _ = _call(
        body, "mix_fwd", (t // tm,), [x, attn, z, z, z, z, z, z, z, vec, w_out],
        [tok, tok] + _z_specs(tm, lambda i: i) + [_full((SUBLANES, D_MODEL)), _full((D_MODEL, D_MODEL))],
        [jax.ShapeDtypeStruct((t, D_MODEL), BF16), jax.ShapeDtypeStruct((t, D_MODEL), F32),
         jax.ShapeDtypeStruct((t, D_MODEL), BF16), jax.ShapeDtypeStruct((t, D_MODEL), BF16)],
        [tok, tok, tok, tok])
    return outs


def _ffn_fwd(h2, w_t):
    t = h2.shape[0]
    tm = min(TOKEN_TILE, t)

    def body(h_ref, w_ref, gu_ref, a_ref):
        hb = h_ref[...]
        for n in range(D_FF // FF_CHUNK):
            lo, hi = n * FF_CHUNK, (n + 1) * FF_CHUNK
            g = lax.dot_general(hb, w_ref[lo:hi, :], NT_DIMS, preferred_element_type=F32)
            u = lax.dot_general(hb, w_ref[D_FF + lo:D_FF + hi, :], NT_DIMS, preferred_element_type=F32)
            sg = _sigmoid(g)
            silu = g * sg
            gu_ref[:, lo:hi] = (u * (sg * (1.0 + g * (1.0 - sg)))).astype(BF16)
            gu_ref[:, D_FF + lo:D_FF + hi] = silu.astype(BF16)
            a_ref[:, lo:hi] = (silu * u).astype(BF16)

    outs, _ = _call(
        body, "ffn_fwd", (t // tm,), [h2, w_t],
        [pl.BlockSpec((tm, D_MODEL), lambda i: (i, 0)), _full((2 * D_FF, D_MODEL))],
        [jax.ShapeDtypeStruct((t, 2 * D_FF), BF16), jax.ShapeDtypeStruct((t, D_FF), BF16)],
        [pl.BlockSpec((tm, 2 * D_FF), lambda i: (i, 0)), pl.BlockSpec((tm, D_FF), lambda i: (i, 0))])
    return outs


def _ffn_out_loss(a, gu, x2, target, vec, w_ffn_out):
    t = a.shape[0]
    tm = min(TOKEN_TILE, t)

    def body(a_ref, gu_ref, x2_ref, t_ref, vec_ref, w_ref, dx3_ref, df_ref, dgu_ref, acc_ref):
        @pl.when(pl.program_id(0) == 0)
        def _():
            acc_ref[...] = jnp.zeros_like(acc_ref)

        ga2 = vec_ref[0:1, :]
        gf = vec_ref[1:2, :]
        parts = min(ROW_PARTS, tm // LANES)
        part_rows = [slice(n * (tm // parts), (n + 1) * (tm // parts)) for n in range(parts)]

        def head(rows, f):
            x3 = x2_ref[rows, :] + ga2 * f
            r = lax.rsqrt(jnp.mean(x3 * x3, axis=-1, keepdims=True) + EPS)
            xn = x3 * r
            err = xn * gf - t_ref[rows, :]
            dxn = err * (gf * (1.0 / D_MODEL))
            dx3 = r * (dxn - xn * jnp.mean(dxn * xn, axis=-1, keepdims=True))
            dx3_ref[rows, :] = dx3.astype(GRAD_STREAM)
            sums = (jnp.sum(err * err, axis=0, keepdims=True),
                    jnp.sum(err * xn, axis=0, keepdims=True) * (1.0 / D_MODEL),
                    jnp.sum(dx3 * f, axis=0, keepdims=True))
            df = (dx3 * ga2).astype(BF16)
            df_ref[rows, :] = df
            return df, sums

        def tail(rows, df):
            for n in range(D_FF // FF_CHUNK):
                lo, hi = n * FF_CHUNK, (n + 1) * FF_CHUNK
                da = lax.dot_general(df, w_ref[lo:hi, :], NT_DIMS, preferred_element_type=F32)
                dgu_ref[rows, lo:hi] = (da * gu_ref[rows, lo:hi].astype(F32)).astype(BF16)
                dgu_ref[rows, D_FF + lo:D_FF + hi] = (da * gu_ref[rows, D_FF + lo:D_FF + hi].astype(F32)).astype(BF16)

        fs = [jnp.dot(a_ref[rows, :], w_ref[...], preferred_element_type=F32) for rows in part_rows]
        heads = [head(rows, f) for rows, f in zip(part_rows, fs)]
        for rows, (df, _) in zip(part_rows, heads):
            tail(rows, df)
        for k in range(3):
            total = heads[0][1][k]
            for _, sums in heads[1:]:
                total = total + sums[k]
            acc_ref[k:k + 1, :] += total

    tok = pl.BlockSpec((tm, D_MODEL), lambda i: (i, 0))
    outs, _ = _call(
        body, "ffn_out_loss", (t // tm,), [a, gu, x2, target, vec, w_ffn_out],
        [pl.BlockSpec((tm, D_FF), lambda i: (i, 0)), pl.BlockSpec((tm, 2 * D_FF), lambda i: (i, 0)),
         tok, tok, _full((SUBLANES, D_MODEL)), _full((D_FF, D_MODEL))],
        [jax.ShapeDtypeStruct((t, D_MODEL), GRAD_STREAM), jax.ShapeDtypeStruct((t, D_MODEL), BF16),
         jax.ShapeDtypeStruct((t, 2 * D_FF), BF16), jax.ShapeDtypeStruct((SUBLANES, D_MODEL), F32)],
        [tok, tok, pl.BlockSpec((tm, 2 * D_FF), lambda i: (i, 0)), _full((SUBLANES, D_MODEL))])
    return outs


def _ffn_in_bwd(dgu, x2, dx3, vec, w_t, rider):
    t = x2.shape[0]
    tm = min(TOKEN_TILE, t)

    def body(dgu_ref, x2_ref, dx3_ref, vec_ref, wf_ref, dx2_ref, acc_ref):
        @pl.when(pl.program_id(0) == 0)
        def _():
            acc_ref[...] = jnp.zeros_like(acc_ref)

        gffn = vec_ref[0:1, :]
        sc2 = vec_ref[1:2, :]
        parts = min(ROW_PARTS, tm // LANES)
        part_rows = [slice(n * (tm // parts), (n + 1) * (tm // parts)) for n in range(parts)]
        dhs = [jnp.dot(dgu_ref[rows, :], wf_ref[...], preferred_element_type=F32) for rows in part_rows]
        gs = gffn * (1.0 + sc2)
        sum_dh = jnp.zeros((1, D_MODEL), F32)
        sum_dh_xn = jnp.zeros((1, D_MODEL), F32)
        for rows, dh2 in zip(part_rows, dhs):
            x2 = x2_ref[rows, :]
            r = lax.rsqrt(jnp.mean(x2 * x2, axis=-1, keepdims=True) + EPS)
            xn = x2 * r
            dh_xn = dh2 * xn
            sum_dh = sum_dh + jnp.sum(dh2, axis=0, keepdims=True)
            sum_dh_xn = sum_dh_xn + jnp.sum(dh_xn, axis=0, keepdims=True)
            dx2 = dx3_ref[rows, :].astype(F32) + r * (dh2 * gs - xn * jnp.mean(dh_xn * gs, axis=-1, keepdims=True))
            dx2_ref[rows, :] = dx2.astype(GRAD_STREAM)
        acc_ref[0:1, :] += sum_dh
        acc_ref[1:2, :] += sum_dh_xn * gffn
        acc_ref[2:3, :] += sum_dh_xn * (1.0 + sc2)

    tok = pl.BlockSpec((tm, D_MODEL), lambda i: (i, 0))
    return _call(
        body, "ffn_in_bwd", (t // tm,), [dgu, x2, dx3, vec, w_t],
        [pl.BlockSpec((tm, 2 * D_FF), lambda i: (i, 0)), tok, tok, _full((SUBLANES, D_MODEL)),
         _full((2 * D_FF, D_MODEL))],
        [jax.ShapeDtypeStruct((t, D_MODEL), GRAD_STREAM), jax.ShapeDtypeStruct((SUBLANES, D_MODEL), F32)],
        [tok, _full((SUBLANES, D_MODEL))], rider=rider)


def _mix_bwd(dx2, oproj, attn, z, vec, w_out, rider):
    t = dx2.shape[0]
    tm = min(TOKEN_TILE, t)
    nt = t // tm
    rev = lambda i: nt - 1 - i

    def body(dx2_ref, m_ref, a_ref, cb_ref, cc_ref, cx_ref, ga_ref, gc_ref, hc_ref, hx_ref,
             vec_ref, wo_ref, do_ref, da_ref, dr_ref, acc_ref, carry_ref):
        i = pl.program_id(0)

        @pl.when(i == 0)
        def _():
            acc_ref[...] = jnp.zeros_like(acc_ref)
            carry_ref[...] = jnp.zeros_like(carry_ref)

        ga1 = vec_ref[0:1, :]
        w0, w1, w2 = vec_ref[1:2, :], vec_ref[2:3, :], vec_ref[3:4, :]
        dx2 = dx2_ref[...].astype(F32)
        acc_ref[0:1, :] += jnp.sum(dx2 * m_ref[...].astype(F32), axis=0, keepdims=True)
        do = (dx2 * ga1).astype(BF16)
        do_ref[...] = do
        dm = lax.dot_general(do, wo_ref[...], NT_DIMS, preferred_element_type=F32)

        cc, cx, u, u1, u2 = _conv_inputs(cc_ref, cx_ref, hc_ref, hx_ref, i == nt - 1)
        cv = w0 * u2 + w1 * u1 + w2 * u
        cb = cb_ref[...].astype(F32)
        sa = _sigmoid(ga_ref[...].astype(F32))
        sc = _sigmoid(gc_ref[...].astype(F32))
        attn = a_ref[...].astype(F32)
        dattn = dm * sa
        da_ref[...] = dattn.astype(BF16)
        dconv = dm * sc
        dconv_b = dconv * cv
        dr_ref[:, 3 * D_MODEL:4 * D_MODEL] = (dattn * attn * (1.0 - sa)).astype(BF16)
        dr_ref[:, 4 * D_MODEL:5 * D_MODEL] = (dconv_b * cb * (1.0 - sc)).astype(BF16)
        dr_ref[:, 0:D_MODEL] = dconv_b.astype(BF16)
        dcv = dconv * cb
        acc_ref[1:2, :] += jnp.sum(dcv * u2, axis=0, keepdims=True)
        acc_ref[2:3, :] += jnp.sum(dcv * u1, axis=0, keepdims=True)
        acc_ref[3:4, :] += jnp.sum(dcv * u, axis=0, keepdims=True)
        nxt = carry_ref[...]
        du = w2 * dcv + w1 * _shift_up(dcv, nxt, 1) + w0 * _shift_up(dcv, nxt, 2)
        carry_ref[...] = dcv[0:SUBLANES, :]
        dr_ref[:, D_MODEL:2 * D_MODEL] = (du * cx).astype(BF16)
        dr_ref[:, 2 * D_MODEL:3 * D_MODEL] = (du * cc).astype(BF16)

    tok = pl.BlockSpec((tm, D_MODEL), lambda i: (rev(i), 0))
    return _call(
        body, "mix_bwd", (nt,), [dx2, oproj, attn, z, z, z, z, z, z, z, vec, w_out],
        [tok, tok, tok] + _z_specs(tm, rev) + [_full((SUBLANES, D_MODEL)), _full((D_MODEL, D_MODEL))],
        [jax.ShapeDtypeStruct((t, D_MODEL), BF16), jax.ShapeDtypeStruct((t, D_MODEL), BF16),
         jax.ShapeDtypeStruct((t, REST_WIDTH), BF16), jax.ShapeDtypeStruct((SUBLANES, D_MODEL), F32)],
        [tok, tok, pl.BlockSpec((tm, REST_WIDTH), lambda i: (rev(i), 0)), _full((SUBLANES, D_MODEL))],
        scratch=[pltpu.VMEM((SUBLANES, D_MODEL), F32)], rider=rider)


def _attn_bwd(z, dattn, attn, lse, sinks, rider):
    t = z.shape[0]
    tq = min(ATTN_TILE, t)
    nblk = tq // WINDOW
    nt = t // tq

    def body(q_ref, kv_ref, do_ref, o_ref, lse_ref, sink_ref, dq_ref, dkv_ref, ds_ref, acc_ref, bias_ref):
        i = pl.program_id(0)

        @pl.when(i == 0)
        def _():
            acc_ref[...] = jnp.zeros_like(acc_ref)
            ds_ref[...] = jnp.zeros_like(ds_ref)
            _fill_window_bias(bias_ref)

        lane = lax.broadcasted_iota(jnp.int32, (1, LANES), 1)
        ind_row = lax.broadcasted_iota(jnp.int32, (SUBLANES, LANES), 0)
        ind_low = lax.broadcasted_iota(jnp.int32, (SUBLANES, LANES), 1) < HEAD_DIM
        indicator = jnp.where(jnp.logical_or(jnp.logical_and(ind_row == 0, ind_low),
                                             jnp.logical_and(ind_row == 1, jnp.logical_not(ind_low))),
                              1.0, 0.0).astype(BF16)
        low = lax.broadcasted_iota(jnp.int32, (2 * WINDOW, LANES), 1) < HEAD_DIM

        def both_heads(even, odd):
            picked = jnp.where(low, even, odd)
            return picked + jnp.concatenate([picked[:, HEAD_DIM:], picked[:, :HEAD_DIM]], axis=1)

        def window(b):
            row0 = pl.multiple_of(b * WINDOW, WINDOW)
            start = i * tq + b * WINDOW
            prev = pl.multiple_of(jnp.maximum(start - WINDOW, 0), WINDOW)
            cur = pl.multiple_of(start, WINDOW)
            kvw = jnp.concatenate([kv_ref[pl.ds(prev, WINDOW), :], kv_ref[pl.ds(cur, WINDOW), :]], axis=0)
            return (row0, prev, cur, _half_tiles(kvw[:, :KV_WIDTH]), _half_tiles(kvw[:, KV_WIDTH:]),
                    bias_ref[jnp.minimum(start, 1)])

        def block_group(bb, dsink):
            windows = [window(bb * ATTN_BWD_BLOCKS + n) for n in range(ATTN_BWD_BLOCKS)]
            dk_groups = [[] for _ in windows]
            dv_groups = [[] for _ in windows]
            for j in range(N_KV_HEADS):
                stacks, deltas, dq_ts = [], [], []
                for row0, _, _, _, _, _ in windows:
                    qst = _stack_pairs(q_ref, row0, j)
                    dost = _stack_pairs(do_ref, row0, j)
                    prod = dost.astype(F32) * _stack_pairs(o_ref, row0, j).astype(F32)
                    prod_hi = prod.astype(BF16)
                    prod_lo = (prod - prod_hi.astype(F32)).astype(BF16)
                    stacks.append((qst, dost))
                    deltas.append(lax.dot_general(indicator, prod_hi, NT_DIMS, preferred_element_type=F32)
                                  + lax.dot_general(indicator, prod_lo, NT_DIMS, preferred_element_type=F32))
                    dq_ts.append(jnp.zeros((LANES, STACK), F32))
                dk_par = [[] for _ in windows]
                dv_par = [[] for _ in windows]
                for parity in range(2):
                    heads = [j * GROUP + 2 * p + parity for p in range(PAIRS)]
                    sink = _per_pair_row([sink_ref[h] * LOG2E for h in heads])
                    for n, (row0, _, _, k_halves, v_halves, bias) in enumerate(windows):
                        qst, dost = stacks[n]
                        kk, vv = k_halves[j][parity], v_halves[j][parity]
                        s = lax.dot_general(kk, qst, NT_DIMS, preferred_element_type=F32) * SCORE_SCALE + bias
                        lse = jnp.concatenate([lse_ref[h:h + 1, pl.ds(row0, WINDOW)] for h in heads], axis=1)
                        p = jnp.exp2(s - lse)
                        dp = lax.dot_general(vv, dost, NT_DIMS, preferred_element_type=F32)
                        delta = deltas[n][parity:parity + 1, :]
                        dsb = (p * (dp - delta)).astype(BF16)
                        dq_ts[n] = dq_ts[n] + lax.dot_general(kk, dsb, TN_DIMS, preferred_element_type=F32)
                        dk_par[n].append(jnp.dot(dsb, qst, preferred_element_type=F32))
                        dv_par[n].append(jnp.dot(p.astype(BF16), dost, preferred_element_type=F32))
                        weighted = jnp.exp2(sink - lse) * delta
                        for pr, h in enumerate(heads):
                            dsink = dsink - jnp.where(
                                lane == h, jnp.sum(weighted[:, pr * WINDOW:(pr + 1) * WINDOW]), 0.0)
                for n, (row0, _, _, _, _, _) in enumerate(windows):
                    dq_st = jnp.transpose((dq_ts[n] * ATTN_SCALE).astype(BF16))
                    for pr in range(PAIRS):
                        dq_ref[pl.ds(row0, WINDOW), (j * PAIRS + pr) * LANES:(j * PAIRS + pr + 1) * LANES] = (
                            dq_st[pr * WINDOW:(pr + 1) * WINDOW, :])
                    dk_groups[n].append(both_heads(dk_par[n][0], dk_par[n][1]))
                    dv_groups[n].append(both_heads(dv_par[n][0], dv_par[n][1]))
            for n, (_, prev, cur, _, _, _) in enumerate(windows):
                blk = jnp.concatenate([jnp.where(low, dk_groups[n][0], dk_groups[n][1]) * ATTN_SCALE,
                                       jnp.where(low, dv_groups[n][0], dv_groups[n][1])], axis=1)
                acc_ref[pl.ds(prev, WINDOW), :] += blk[:WINDOW, :]
                acc_ref[pl.ds(cur, WINDOW), :] += blk[WINDOW:, :]
            return dsink

        dsink = lax.fori_loop(0, nblk // ATTN_BWD_BLOCKS, block_group, jnp.zeros((1, LANES), F32))
        ds_ref[0:1, :] += dsink

        @pl.when(i == nt - 1)
        def _():
            dkv_ref[...] = acc_ref[...].astype(BF16)

    tok = pl.BlockSpec((tq, D_MODEL), lambda i: (i, 0))
    return _call(
        body, "attn_bwd", (nt,), [z, z, dattn, attn, lse, sinks],
        [tok, pl.BlockSpec((t, 2 * KV_WIDTH), lambda i: (0, KV_COL // (2 * KV_WIDTH))), tok, tok,
         pl.BlockSpec((N_Q_HEADS, tq), lambda i: (0, i)), pl.BlockSpec(memory_space=pltpu.SMEM)],
        [jax.ShapeDtypeStruct((t, D_MODEL), BF16), jax.ShapeDtypeStruct((t, 2 * KV_WIDTH), BF16),
         jax.ShapeDtypeStruct((SUBLANES, LANES), F32)],
        [tok, _full((t, 2 * KV_WIDTH)), _full((SUBLANES, LANES))],
        scratch=[pltpu.VMEM((t, 2 * KV_WIDTH), F32), pltpu.VMEM((2, 2 * WINDOW, STACK), F32)], rider=rider)


def _inproj_bwd(dq, drest, dkv, x, dx2, vec, w_t, rider):
    t = x.shape[0]
    tm = min(TOKEN_TILE, t)

    def body(dq_ref, dr_ref, dkv_ref, x_ref, dx2_ref, vec_ref, w_ref, gx_ref, acc_ref, db_ref):
        @pl.when(pl.program_id(0) == 0)
        def _():
            acc_ref[...] = jnp.zeros_like(acc_ref)
            db_ref[...] = jnp.zeros_like(db_ref)

        g = vec_ref[0:1, :]
        sc1 = vec_ref[1:2, :]
        dqb, drb, dkvb = dq_ref[...], dr_ref[...], dkv_ref[...]
        dh = jnp.dot(dqb, w_ref[:REF_KV_COL, :], preferred_element_type=F32)
        dh = dh + jnp.dot(drb, w_ref[REF_REST_COL:, :], preferred_element_type=F32)
        dh = dh + jnp.dot(dkvb, w_ref[REF_KV_COL:REF_REST_COL, :], preferred_element_type=F32)
        db_ref[:, :REF_KV_COL] += jnp.sum(dqb.astype(F32), axis=0, keepdims=True)
        db_ref[:, REF_REST_COL:] += jnp.sum(drb.astype(F32), axis=0, keepdims=True)
        db_ref[:, REF_KV_COL:REF_REST_COL] += jnp.sum(dkvb.astype(F32), axis=0, keepdims=True)
        xf = x_ref[...]
        r = lax.rsqrt(jnp.mean(xf * xf, axis=-1, keepdims=True) + EPS)
        xn = xf * r
        gs = g * (1.0 + sc1)
        dh_xn = dh * xn
        sum_dh_xn = jnp.sum(dh_xn, axis=0, keepdims=True)
        acc_ref[0:1, :] += jnp.sum(dh, axis=0, keepdims=True)
        acc_ref[1:2, :] += sum_dh_xn * g
        acc_ref[2:3, :] += sum_dh_xn * (1.0 + sc1)
        gx_ref[...] = dx2_ref[...].astype(F32) + r * (dh * gs - xn * jnp.mean(dh_xn * gs, axis=-1, keepdims=True))

    tok = pl.BlockSpec((tm, D_MODEL), lambda i: (i, 0))
    return _call(
        body, "inproj_bwd", (t // tm,), [dq, drest, dkv, x, dx2, vec, w_t],
        [tok, pl.BlockSpec((tm, REST_WIDTH), lambda i: (i, 0)),
         pl.BlockSpec((tm, 2 * KV_WIDTH), lambda i: (i, 0)), tok, tok,
         _full((SUBLANES, D_MODEL)), _full((IN_WIDTH, D_MODEL))],
        [jax.ShapeDtypeStruct((t, D_MODEL), F32), jax.ShapeDtypeStruct((SUBLANES, D_MODEL), F32),
         jax.ShapeDtypeStruct((1, IN_WIDTH), F32)],
        [tok, _full((SUBLANES, D_MODEL)), _full((1, IN_WIDTH))], rider=rider)


def _weight_grad(b, a, name, bn, rows=None, row0=0, into=None, rider=None):
    t, n = b.shape
    m = a.shape[1]
    rows = n if rows is None else rows
    tk = min(TOKEN_TILE, t)
    for cand in (4 * TOKEN_TILE, 2 * TOKEN_TILE):
        if t % cand == 0 and 2 * cand * (bn + m) * 2 + bn * m * 4 <= WGRAD_VMEM:
            tk = cand
            break
    nk = t // tk
    block0 = row0 // bn

    def body(b_ref, a_ref, *rest):
        out_ref, acc_ref = rest[-2:]
        k = pl.program_id(1)

        @pl.when(k == 0)
        def _():
            acc_ref[...] = jnp.zeros_like(acc_ref)

        acc_ref[...] += lax.dot_general(b_ref[...], a_ref[...], TN_DIMS, preferred_element_type=F32)

        @pl.when(k == nk - 1)
        def _():
            out_ref[...] = acc_ref[...].astype(BF16)

    outs, routs = _call(
        body, name, (n // bn, nk), [b, a] + ([] if into is None else [into]),
        [pl.BlockSpec((tk, bn), lambda j, k: (k, j)), pl.BlockSpec((tk, m), lambda j, k: (k, 0))]
        + ([] if into is None else [ANY]),
        [jax.ShapeDtypeStruct((rows, m), BF16)], [pl.BlockSpec((bn, m), lambda j, k: (block0 + j, 0))],
        scratch=[pltpu.VMEM((bn, m), F32)], rider=rider, aliases=None if into is None else {2: 0})
    return outs[0], routs


def _to_rows(v):
    n = v.shape[0]
    padded = -(-n // (SUBLANES * LANES)) * SUBLANES * LANES
    return jnp.pad(v, (0, padded - n)).reshape(padded // LANES, LANES)


def _vec_rows(*rows):
    stacked = jnp.concatenate([r.reshape(1, D_MODEL) for r in rows], axis=0)
    return jnp.pad(stacked, ((0, SUBLANES - len(rows)), (0, 0)))


def kernel(x, c, w_ada, b_ada, g_mix, w_in, b_in, sinks, conv_w, w_out, g_ffn, w_ffn_in, w_ffn_out, g_final, loss_target, m_w_ada, m_b_ada, m_g_mix, m_w_in, m_b_in, m_sinks, m_conv_w, m_w_out, m_g_ffn, m_w_ffn_in, m_w_ffn_out, m_g_final, v_w_ada, v_b_ada, v_g_mix, v_w_in, v_b_in, v_sinks, v_conv_w, v_w_out, v_g_ffn, v_w_ffn_in, v_w_ffn_out, v_g_final):
    ix, iy, ic = _my_place()
    me = 4 * ix + 2 * iy + ic
    xs = x[0]
    target = loss_target[0]
    ada_cols = w_ada.shape[2]
    conv_cols = conv_w.shape[2]

    wt_in, wt_fi = jnp.transpose(w_in[0]), jnp.transpose(w_ffn_in[0])
    b_cols = lax.dynamic_slice_in_dim(b_ada, me * ada_cols, ada_cols, axis=1)
    g_in, (cast_fi, cast_out, cast_fo), first, mod_all = _gather_first_weight(
        wt_in, [wt_fi, w_out[0], w_ffn_out[0]], _to_rows(jnp.concatenate([c[0], conv_w[0].reshape(-1)])),
        w_ada[0], b_cols)
    first = first.reshape(N_DEV, -1)
    c_all = first[:, :D_MODEL]
    conv_full = jnp.transpose(first[:, D_MODEL:D_MODEL + 3 * conv_cols].reshape(N_DEV, 3, conv_cols), (1, 0, 2))
    conv_full = conv_full.reshape(3, D_MODEL)
    mod = lax.dynamic_index_in_dim(mod_all, me, axis=1, keepdims=False).reshape(N_MOD, D_MODEL)
    sh1, sc1, ga1, sh2, sc2, ga2 = [mod[i:i + 1] for i in range(N_MOD)]
    w_in_t = g_in.reshape(IN_WIDTH, D_MODEL)
    (z, h1), (g_fi, g_out) = _inproj_fwd(xs, _vec_rows(g_mix, sc1, sh1), w_in_t, b_in,
                                         _gather_rider([cast_fi, cast_out]))
    w_fi_t = g_fi.reshape(2 * D_FF, D_MODEL)
    w_out_full = g_out.reshape(D_MODEL, D_MODEL)
    (attn, lse), (g_fo,) = _attn_fwd(z, sinks[0], _gather_rider([cast_fo]))
    w_fo_full = g_fo.reshape(D_FF, D_MODEL)
    merged, x2, h2, oproj = _mix_fwd(
        xs, attn, z, _vec_rows(ga1, g_ffn, sc2, sh2, conv_full[0], conv_full[1], conv_full[2]), w_out_full)
    gu, act = _ffn_fwd(h2, w_fi_t)
    dx3, df, dgu, acc_l = _ffn_out_loss(act, gu, x2, target, _vec_rows(ga2, g_final), w_fo_full)

    gw_fo, _ = _weight_grad(act, df, "wgrad_ffn_out", D_FF)
    gw_fi, _ = _weight_grad(dgu, h2, "wgrad_ffn_in", D_FF)
    blocks_fo = gw_fo.reshape(N_DEV, D_FF // N_DEV, D_MODEL)
    blocks_fi = gw_fi.reshape(N_DEV, 2 * D_FF // N_DEV, D_MODEL)
    (dx2, acc_f), (sib_fo, sib_fi) = _ffn_in_bwd(dgu, x2, dx3, _vec_rows(g_ffn, sc2), w_fi_t,
                                                 _sibling_rider([blocks_fo, blocks_fi]))
    sums_fo, mine_fo = _sibling_sum(blocks_fo, sib_fo, "sibling_sum_ffn_out")
    sums_fi, mine_fi = _sibling_sum(blocks_fi, sib_fi, "sibling_sum_ffn_in")
    (dout, dattn, drest, acc_m), (ici_fo, ici_fi) = _mix_bwd(
        dx2, oproj, attn, z, _vec_rows(ga1, conv_full[0], conv_full[1], conv_full[2]), w_out_full,
        _chip_rider([sums_fo, sums_fi]))
    gw_out, _ = _weight_grad(merged, dout, "wgrad_out", D_MODEL)
    blocks_out = gw_out.reshape(N_DEV, D_MODEL // N_DEV, D_MODEL)
    (dq, dkv, dsink), (sib_out,) = _attn_bwd(z, dattn, attn, lse, sinks[0], _sibling_rider([blocks_out]))
    sums_out, mine_out = _sibling_sum(blocks_out, sib_out, "sibling_sum_out")
    gw_in, (ici_out,) = _weight_grad(drest, h1, "wgrad_in_rest", IN_CHUNK, rows=IN_WIDTH, row0=REF_REST_COL,
                                     rider=_chip_rider([sums_out]))
    gw_in, _ = _weight_grad(dq, h1, "wgrad_in_q", D_MODEL, rows=IN_WIDTH, row0=0, into=gw_in)
    gw_in, _ = _weight_grad(dkv, h1, "wgrad_in_kv", 2 * KV_WIDTH, rows=IN_WIDTH, row0=REF_KV_COL, into=gw_in)
    blocks_in = gw_in.reshape(N_DEV, IN_WIDTH // N_DEV, D_MODEL)
    (sib_in,) = _carry(_sibling_rider([blocks_in]), "sibling_w_in")
    sums_in, mine_in = _sibling_sum(blocks_in, sib_in, "sibling_sum_in")
    (grad_x, acc_i, db_in), (ici_in,) = _inproj_bwd(dq, drest, dkv, xs, dx2, _vec_rows(g_mix, sc1), w_in_t,
                                                    _chip_rider([sums_in]))

    widen = lambda vec: jnp.pad(vec, (0, -vec.shape[0] % D_MODEL))
    packed = jnp.concatenate([
        acc_i[0], acc_i[1], acc_m[0], acc_f[0], acc_f[1], acc_l[2],
        acc_i[2], widen(db_in[0]), acc_f[2], acc_l[1],
        acc_m[1], acc_m[2], acc_m[3], widen(dsink[0]), acc_l[0],
        jnp.zeros(((PACK_ROWS - PACK_SQERR - 1) * D_MODEL,), F32)]).reshape(PACK_ROWS, D_MODEL)
    packed_all = _small_allgather(packed, "gather_small")
    dmod_all = packed_all[:, PACK_DMOD:PACK_DMOD + N_MOD, :].reshape(N_DEV, N_MOD * D_MODEL)
    dmod_cols = lax.dynamic_slice_in_dim(dmod_all, me * ada_cols, ada_cols, axis=1)
    g_w_ada = _ada_weight_grad(c_all, dmod_cols)
    row_of = lambda a: a.reshape(1, -1)
    small, g_conv_full, loss = _small_finalize(packed_all, {
        "b_ada": (b_ada, m_b_ada, v_b_ada), "g_mix": (g_mix, m_g_mix, v_g_mix), "b_in": (b_in, m_b_in, v_b_in),
        "g_ffn": (g_ffn, m_g_ffn, v_g_ffn), "sinks": (sinks, m_sinks, v_sinks),
        "g_final": (row_of(g_final), row_of(m_g_final), row_of(v_g_final))})
    small["g_final"] = tuple(o.reshape(g_final.shape) for o in small["g_final"])
    g_conv = lax.dynamic_slice_in_dim(g_conv_full, me * conv_cols, conv_cols, axis=1)
    d_conv, nm_conv, nv_conv = _adamw(conv_w[0], g_conv, m_conv_w[0], v_conv_w[0], "adamw_conv_w")
    small["conv_w"] = (g_conv[None], d_conv[None], nm_conv[None], nv_conv[None])

    def reduced(mine, ici, w, m, v, name, transposed=False):
        turn = jnp.transpose if transposed else (lambda a: a)
        return tuple(turn(o)[None] for o in _chip_sum_adamw(mine, ici, turn(w[0]), turn(m[0]), turn(v[0]), name))

    d_ada, nm_ada, nv_ada = _adamw(w_ada[0], g_w_ada, m_w_ada[0], v_w_ada[0], "adamw_w_ada")
    res = {
        "w_ada": (g_w_ada[None], d_ada[None], nm_ada[None], nv_ada[None]),
        "w_in": reduced(mine_in, ici_in, w_in, m_w_in, v_w_in, "adamw_w_in", transposed=True),
        "w_out": reduced(mine_out, ici_out, w_out, m_w_out, v_w_out, "adamw_w_out"),
        "w_ffn_in": reduced(mine_fi, ici_fi, w_ffn_in, m_w_ffn_in, v_w_ffn_in, "adamw_w_ffn_in", transposed=True),
        "w_ffn_out": reduced(mine_fo, ici_fo, w_ffn_out, m_w_ffn_out, v_w_ffn_out, "adamw_w_ffn_out"),
    }
    res.update(small)
    order = ["w_ada", "b_ada", "g_mix", "w_in", "b_in", "sinks", "conv_w", "w_out", "g_ffn", "w_ffn_in", "w_ffn_out",
             "g_final"]
    outs = [loss.reshape(()), grad_x[None]]
    for k in range(4):
        outs += [res[n][k] for n in order]
    return tuple(outs)
```

```python
import functools
import math

import jax
import jax.numpy as jnp
from jax import lax
from jax.experimental import pallas as pl
from jax.experimental.pallas import tpu as pltpu

F32 = jnp.float32
BF16 = jnp.bfloat16
GRAD_STREAM = F32

D_MODEL = 1024
HEAD_DIM = 64
N_Q_HEADS = 16
N_KV_HEADS = 2
GROUP = 8
WINDOW = 128
KV_WIDTH = N_KV_HEADS * HEAD_DIM
D_FF = 2816
IN_WIDTH = 6400
N_MOD = 6
EPS = 1e-6
N_DEV = 8
REST_WIDTH = 5 * D_MODEL
KV_COL = D_MODEL + REST_WIDTH
ATTN_SCALE = HEAD_DIM ** -0.5

ADAM_LR = 0.001
ADAM_B1 = 0.9
ADAM_B2 = 0.999
ADAM_EPS = 1e-08
ADAM_WD = 0.01
ADAM_STEP = 10

LANES = 128
SUBLANES = 8
BF16_ROWS = 16
VMEM_LIMIT = 56 * 1024 * 1024
TOKEN_TILE = 512
FF_CHUNK = 256
ROW_PARTS = 2
WGRAD_VMEM = 40 * 1024 * 1024
MESH = pl.DeviceIdType.MESH
ANY = pl.BlockSpec(memory_space=pl.ANY)

NT_DIMS = (((1,), (1,)), ((), ()))
TN_DIMS = (((0,), (0,)), ((), ()))
CHIP_FLIPS = [(0, 0), (1, 0), (0, 1), (1, 1)]


def _full(shape):
    return pl.BlockSpec(shape, lambda *_: (0,) * len(shape))


def _my_place():
    return lax.axis_index("x"), lax.axis_index("y"), lax.axis_index("c")


def _flip(v, bit):
    return 1 - v if bit else v


def _sigmoid(v):
    return 1.0 / (1.0 + jnp.exp2(v * (-1.4426950408889634)))


class _Rider:
    def __init__(self, ins, out_shapes, sem_shapes, first=None, mid=None, last=None, ins_in_vmem=False):
        self.ins, self.out_shapes, self.sem_shapes = list(ins), list(out_shapes), list(sem_shapes)
        self.in_specs = [_full(a.shape) if ins_in_vmem else ANY for a in self.ins]
        self.hooks = [(when, fn) for when, fn in (("first", first), ("mid", mid), ("last", last)) if fn is not None]


def _call(body, name, grid, args, in_specs, out_shape, out_specs, scratch=(), rider=None, aliases=None):
    n_in, n_out, n_scr = len(args), len(out_shape), len(scratch)
    r_in = rider.ins if rider else []
    r_out = rider.out_shapes if rider else []
    r_sem = rider.sem_shapes if rider else []
    nsteps = math.prod(grid)

    def full_body(*refs):
        pos = 0
        groups = []
        for size in (n_in, len(r_in), n_out, len(r_out), n_scr, len(r_sem)):
            groups.append(refs[pos:pos + size])
            pos += size
        ins, rins, outs, routs, scr, rsems = groups
        step = pl.program_id(0)
        for axis in range(1, len(grid)):
            step = step * grid[axis] + pl.program_id(axis)
        at = {"first": 0, "mid": (3 * nsteps) // 4, "last": nsteps - 1}
        hooks = rider.hooks if rider else []
        for when, fn in hooks:
            if when != "last":
                pl.when(step == at[when])(functools.partial(fn, rins, routs, rsems))
        body(*ins, *outs, *scr)
        for when, fn in hooks:
            if when == "last":
                pl.when(step == at[when])(functools.partial(fn, rins, routs, rsems))

    outs = pl.pallas_call(
        full_body, name=name, grid=grid,
        out_shape=list(out_shape) + list(r_out),
        in_specs=list(in_specs) + (rider.in_specs if rider else []),
        out_specs=list(out_specs) + [ANY] * len(r_out),
        scratch_shapes=list(scratch) + list(r_sem),
        input_output_aliases=dict(aliases or {}),
        compiler_params=pltpu.CompilerParams(dimension_semantics=("arbitrary",) * len(grid),
                                             vmem_limit_bytes=VMEM_LIMIT),
    )(*args, *r_in)
    return list(outs[:n_out]), list(outs[n_out:])


def _gather_rider(shards):
    n = len(shards)

    def setup(outs, sems):
        x, y, c = _my_place()
        send_sems, recv_sems, _ = sems
        chips = [(1 - x, y), (x, 1 - y), (1 - x, 1 - y)]

        def block(w, place):
            return outs[w].at[4 * place[0] + 2 * place[1] + place[2]]

        def copy(w, k, place, to, src=None):
            return pltpu.make_async_remote_copy(
                src_ref=block(w, place) if src is None else src, dst_ref=block(w, place),
                send_sem=send_sems.at[w, k], recv_sem=recv_sems.at[w, k], device_id=to, device_id_type=MESH)

        return (x, y, c), (x, y, 1 - c), chips, block, copy

    def first(ins, outs, sems):
        me, sibling, chips, block, copy = setup(outs, sems)
        for w in range(n):
            pltpu.make_async_copy(ins[w], block(w, me), sems[2].at[w]).start()
            copy(w, 0, me, sibling, src=ins[w]).start()
            for j, chip in enumerate(chips):
                copy(w, 1 + j, me, (*chip, me[2]), src=ins[w]).start()

    def mid(ins, outs, sems):
        me, sibling, chips, block, copy = setup(outs, sems)
        for w in range(n):
            for j, chip in enumerate(chips):
                copy(w, 1 + j, (*chip, me[2]), me).wait_recv()
                copy(w, 4 + j, (*chip, me[2]), sibling).start()

    def last(ins, outs, sems):
        me, sibling, chips, block, copy = setup(outs, sems)
        for w in range(n):
            copy(w, 0, sibling, me).wait_recv()
            for j, chip in enumerate(chips):
                copy(w, 4 + j, (*chip, 1 - me[2]), me).wait_recv()
            copy(w, 0, me, sibling, src=ins[w]).wait_send()
            for j, chip in enumerate(chips):
                copy(w, 1 + j, me, (*chip, me[2]), src=ins[w]).wait_send()
                copy(w, 4 + j, (*chip, me[2]), sibling).wait_send()
            pltpu.make_async_copy(ins[w], block(w, me), sems[2].at[w]).wait()

    return _Rider(
        shards, [jax.ShapeDtypeStruct((N_DEV,) + s.shape, BF16) for s in shards],
        [pltpu.SemaphoreType.DMA((n, N_DEV - 1)), pltpu.SemaphoreType.DMA((n, N_DEV - 1)),
         pltpu.SemaphoreType.DMA((n,))],
        first=first, mid=mid, last=last, ins_in_vmem=True)


def _sibling_rider(gblocks):
    n = len(gblocks)

    def copies(ins, outs, sems):
        x, y, c = _my_place()
        send_sems, recv_sems = sems
        made = []
        for w in range(n):
            for f, (fx, fy) in enumerate(CHIP_FLIPS):
                chip = 4 * _flip(x, fx) + 2 * _flip(y, fy)
                made.append(pltpu.make_async_remote_copy(
                    src_ref=ins[w].at[chip + 1 - c], dst_ref=outs[w].at[f], send_sem=send_sems.at[w, f],
                    recv_sem=recv_sems.at[w, f], device_id=(x, y, 1 - c), device_id_type=MESH))
        return made

    def first(ins, outs, sems):
        for cp in copies(ins, outs, sems):
            cp.start()

    def last(ins, outs, sems):
        for cp in copies(ins, outs, sems):
            cp.wait_recv()
            cp.wait_send()

    return _Rider(gblocks, [jax.ShapeDtypeStruct((4,) + g.shape[1:], BF16) for g in gblocks],
                  [pltpu.SemaphoreType.DMA((n, 4))] * 2, first=first, last=last)


def _chip_rider(sums):
    n = len(sums)

    def copies(ins, outs, sems):
        x, y, c = _my_place()
        send_sems, recv_sems = sems
        made = []
        for w in range(n):
            for f in (1, 2, 3):
                fx, fy = CHIP_FLIPS[f]
                made.append(pltpu.make_async_remote_copy(
                    src_ref=ins[w].at[f - 1], dst_ref=outs[w].at[f - 1], send_sem=send_sems.at[w, f - 1],
                    recv_sem=recv_sems.at[w, f - 1], device_id=(_flip(x, fx), _flip(y, fy), c), device_id_type=MESH))
        return made

    def first(ins, outs, sems):
        for cp in copies(ins, outs, sems):
            cp.start()

    def last(ins, outs, sems):
        for cp in copies(ins, outs, sems):
            cp.wait_recv()
            cp.wait_send()

    return _Rider(sums, [jax.ShapeDtypeStruct(s.shape, BF16) for s in sums],
                  [pltpu.SemaphoreType.DMA((n, 3))] * 2, first=first, last=last)


def _push_to_all(v_ref, out_ref, send_sems, recv_sems, local_sem, wait=True):
    x, y, c = _my_place()
    me = 4 * x + 2 * y + c
    mine = pltpu.make_async_copy(v_ref, out_ref.at[me], local_sem)
    mine.start()
    sends = []
    for k in range(1, N_DEV):
        px, py, pc = _flip(x, k & 4), _flip(y, k & 2), _flip(c, k & 1)
        cp = pltpu.make_async_remote_copy(
            src_ref=v_ref, dst_ref=out_ref.at[me], send_sem=send_sems.at[k - 1], recv_sem=recv_sems.at[k - 1],
            device_id=(px, py, pc), device_id_type=MESH)
        cp.start()
        sends.append(cp)

    def finish():
        for k in range(1, N_DEV):
            px, py, pc = _flip(x, k & 4), _flip(y, k & 2), _flip(c, k & 1)
            pltpu.make_async_remote_copy(
                src_ref=v_ref, dst_ref=out_ref.at[4 * px + 2 * py + pc], send_sem=send_sems.at[k - 1],
                recv_sem=recv_sems.at[k - 1], device_id=(px, py, pc), device_id_type=MESH).wait_recv()
        for cp in sends:
            cp.wait_send()
        mine.wait()

    if wait:
        finish()
    return finish


def _small_allgather(v, name):
    def body(v_ref, out_ref, send_sems, recv_sems, local_sem):
        _push_to_all(v_ref, out_ref, send_sems, recv_sems, local_sem)

    return pl.pallas_call(
        body, name=name,
        out_shape=jax.ShapeDtypeStruct((N_DEV,) + v.shape, F32),
        in_specs=[pl.BlockSpec(memory_space=pltpu.VMEM)],
        out_specs=pl.BlockSpec(memory_space=pltpu.VMEM),
        scratch_shapes=[pltpu.SemaphoreType.DMA((N_DEV - 1,)), pltpu.SemaphoreType.DMA((N_DEV - 1,)),
                        pltpu.SemaphoreType.DMA],
        compiler_params=pltpu.CompilerParams(vmem_limit_bytes=VMEM_LIMIT),
    )(v)


def _gather_first_weight(shard, others, cond_rows, w_ada, b_cols):
    n = len(others)
    ada_cols = w_ada.shape[1]
    c_rows = D_MODEL // LANES

    def body(*refs):
        w_ref, other_refs = refs[0], refs[1:1 + n]
        cond_ref, wada_ref, bcols_ref = refs[1 + n:4 + n]
        out_ref, cast_refs = refs[4 + n], refs[5 + n:5 + 2 * n]
        cond_all_ref, mod_all_ref = refs[5 + 2 * n:7 + 2 * n]
        mine_ref, mod_ref, send_sems, recv_sems, local_sem, small_send, small_recv, small_local = refs[7 + 2 * n:]
        x, y, c = _my_place()
        me, sibling = (x, y, c), (x, y, 1 - c)
        xnb, ynb, diag = (1 - x, y), (x, 1 - y), (1 - x, 1 - y)
        half = shard.shape[0] // 2

        def block(place, part=None):
            ref = out_ref.at[4 * place[0] + 2 * place[1] + place[2]]
            return ref if part is None else ref.at[pl.ds(part * half, half)]

        def copy(k, place, to, part=None, src=None):
            return pltpu.make_async_remote_copy(
                src_ref=block(place, part) if src is None else src, dst_ref=block(place, part),
                send_sem=send_sems.at[k], recv_sem=recv_sems.at[k], device_id=to, device_id_type=MESH)

        finish_cond = _push_to_all(cond_ref, cond_all_ref, small_send.at[0], small_recv.at[0], small_local.at[0],
                                   wait=False)
        mine_ref[...] = w_ref[...].astype(BF16)
        local = pltpu.make_async_copy(mine_ref, block(me), local_sem)
        local.start()
        started = [copy(0, me, sibling, src=mine_ref), copy(1, me, (*xnb, c), src=mine_ref),
                   copy(2, me, (*ynb, c), src=mine_ref)]
        for cp in started:
            cp.start()
        finish_cond()
        mod = jnp.zeros((N_DEV, ada_cols), F32) + bcols_ref[...]
        for r in range(c_rows):
            cf = cond_all_ref[:, r, :]
            act = (cf * _sigmoid(cf)).astype(BF16)
            mod = mod + jnp.dot(act, wada_ref[r * LANES:(r + 1) * LANES, :].astype(BF16),
                                preferred_element_type=F32)
        mod_ref[...] = mod
        finish_mod = _push_to_all(mod_ref, mod_all_ref, small_send.at[1], small_recv.at[1], small_local.at[1],
                                  wait=False)
        for o_ref, c_ref in zip(other_refs, cast_refs):
            c_ref[...] = o_ref[...].astype(BF16)
        def start(cp):
            cp.start()
            started.append(cp)

        copy(1, (*xnb, c), me).wait_recv()
        start(copy(3, (*xnb, c), (*ynb, c), part=0))
        start(copy(5, (*xnb, c), sibling))
        copy(2, (*ynb, c), me).wait_recv()
        start(copy(4, (*ynb, c), (*xnb, c), part=1))
        start(copy(6, (*ynb, c), sibling))
        copy(3, (*diag, c), me, part=0).wait_recv()
        start(copy(7, (*diag, c), sibling, part=0))
        copy(4, (*diag, c), me, part=1).wait_recv()
        start(copy(8, (*diag, c), sibling, part=1))
        copy(0, sibling, me).wait_recv()
        copy(5, (*xnb, 1 - c), me).wait_recv()
        copy(6, (*ynb, 1 - c), me).wait_recv()
        copy(7, (*diag, 1 - c), me, part=0).wait_recv()
        copy(8, (*diag, 1 - c), me, part=1).wait_recv()
        finish_mod()
        for cp in started:
            cp.wait_send()
        local.wait()

    vmem = pl.BlockSpec(memory_space=pltpu.VMEM)
    outs = pl.pallas_call(
        body, name="gather_w_in",
        out_shape=[jax.ShapeDtypeStruct((N_DEV,) + shard.shape, BF16)]
        + [jax.ShapeDtypeStruct(o.shape, BF16) for o in others]
        + [jax.ShapeDtypeStruct((N_DEV,) + cond_rows.shape, F32), jax.ShapeDtypeStruct((N_DEV, N_DEV, ada_cols), F32)],
        in_specs=[vmem] * (4 + n),
        out_specs=[ANY] + [vmem] * (n + 2),
        scratch_shapes=[pltpu.VMEM(shard.shape, BF16), pltpu.VMEM((N_DEV, ada_cols), F32),
                        pltpu.SemaphoreType.DMA((9,)), pltpu.SemaphoreType.DMA((9,)),
                        pltpu.SemaphoreType.DMA,
                        pltpu.SemaphoreType.DMA((2, N_DEV - 1)), pltpu.SemaphoreType.DMA((2, N_DEV - 1)),
                        pltpu.SemaphoreType.DMA((2,))],
        compiler_params=pltpu.CompilerParams(vmem_limit_bytes=VMEM_LIMIT),
    )(shard, *others, cond_rows, w_ada, b_cols)
    return outs[0], list(outs[1:1 + n]), outs[1 + n], outs[2 + n]


def _carry(rider, name):
    def body(token_ref):
        token_ref[...] = jnp.zeros_like(token_ref)

    _, routs = _call(body, name, (1,), [], [], [jax.ShapeDtypeStruct((SUBLANES, LANES), F32)],
                     [_full((SUBLANES, LANES))], rider=rider)
    return routs


def _ada_weight_grad(c_all, dmod_cols):
    cols = dmod_cols.shape[1]

    def body(c_ref, d_ref, out_ref):
        cf = c_ref[...]
        act = (cf * _sigmoid(cf)).astype(BF16)
        out_ref[...] = lax.dot_general(act, d_ref[...].astype(BF16), TN_DIMS, preferred_element_type=F32)

    return pl.pallas_call(
        body, name="ada_weight_grad",
        out_shape=jax.ShapeDtypeStruct((D_MODEL, cols), F32),
        in_specs=[pl.BlockSpec(memory_space=pltpu.VMEM)] * 2,
        out_specs=pl.BlockSpec(memory_space=pltpu.VMEM),
        compiler_params=pltpu.CompilerParams(vmem_limit_bytes=VMEM_LIMIT),
    )(c_all, dmod_cols)


PACK_ROWS = 24
PACK_DMOD = 0
PACK_PARAMS = {"g_mix": (6, D_MODEL), "b_in": (7, IN_WIDTH), "g_ffn": (14, D_MODEL), "g_final": (15, D_MODEL),
               "sinks": (19, N_Q_HEADS)}
PACK_CONV = 16
PACK_SQERR = 20


def _small_finalize(packed_all, params):
    names = ["b_ada"] + list(PACK_PARAMS)
    layout = dict(PACK_PARAMS, b_ada=(PACK_DMOD, N_MOD * D_MODEL))
    n = len(names)

    def body(*refs):
        p_ref = refs[0]
        ins = refs[1:1 + 3 * n]
        outs = refs[1 + 3 * n:1 + 7 * n]
        conv_ref, loss_ref = refs[1 + 7 * n:]
        total = p_ref[0]
        for d in range(1, N_DEV):
            total = total + p_ref[d]
        for k, name in enumerate(names):
            row0, width = layout[name]
            w_ref, m_ref, v_ref = ins[3 * k:3 * k + 3]
            g_ref, d_ref, nm_ref, nv_ref = outs[4 * k:4 * k + 4]
            for chunk in range(-(-width // D_MODEL)):
                lo = chunk * D_MODEL
                hi = min(lo + D_MODEL, width)
                g = total[row0 + chunk:row0 + chunk + 1, :hi - lo]
                g_ref[:, lo:hi] = g
                d_ref[:, lo:hi], nm_ref[:, lo:hi], nv_ref[:, lo:hi] = _adamw_update(
                    w_ref[:, lo:hi], g, m_ref[:, lo:hi], v_ref[:, lo:hi])
        conv_ref[...] = total[PACK_CONV:PACK_CONV + 3, :]
        loss_ref[...] = (0.5 / D_MODEL) * jnp.sum(total[PACK_SQERR:PACK_SQERR + 1, :], keepdims=True)

    vmem = pl.BlockSpec(memory_space=pltpu.VMEM)
    flat = [a for name in names for a in params[name]]
    out_shape = [jax.ShapeDtypeStruct(params[name][0].shape, F32) for name in names for _ in range(4)]
    outs = pl.pallas_call(
        body, name="small_finalize",
        out_shape=out_shape + [jax.ShapeDtypeStruct((3, D_MODEL), F32), jax.ShapeDtypeStruct((1, 1), F32)],
        in_specs=[vmem] * (1 + 3 * n),
        out_specs=[vmem] * (4 * n + 2),
        compiler_params=pltpu.CompilerParams(vmem_limit_bytes=VMEM_LIMIT),
    )(packed_all, *flat)
    return {name: tuple(outs[4 * k:4 * k + 4]) for k, name in enumerate(names)}, outs[4 * n], outs[4 * n + 1]


def _row_tile(rows, multiple):
    for cand in range(min(rows, 256), 0, -1):
        if rows % cand == 0 and cand % multiple == 0:
            return cand
    return rows


def _adamw_update(w, g, m, v):
    c1 = 1.0 / (1.0 - ADAM_B1 ** ADAM_STEP)
    c2 = 1.0 / (1.0 - ADAM_B2 ** ADAM_STEP)
    nm = ADAM_B1 * m + (1.0 - ADAM_B1) * g
    nv = ADAM_B2 * v + (1.0 - ADAM_B2) * (g * g)
    delta = -ADAM_LR * ((nm * c1) / (jnp.sqrt(nv * c2) + ADAM_EPS) + ADAM_WD * w)
    return delta, nm, nv


def _adamw(w, g, m, v, name):
    rows, cols = w.shape
    tile = _row_tile(rows, SUBLANES)

    def body(w_ref, g_ref, m_ref, v_ref, d_ref, nm_ref, nv_ref):
        d_ref[...], nm_ref[...], nv_ref[...] = _adamw_update(w_ref[...], g_ref[...], m_ref[...], v_ref[...])

    spec = pl.BlockSpec((tile, cols), lambda i: (i, 0))
    outs, _ = _call(body, name, (rows // tile,), [w, g, m, v], [spec] * 4,
                    [jax.ShapeDtypeStruct((rows, cols), F32)] * 3, [spec] * 3)
    return outs


def _sibling_sum(gblocks, sib, name):
    _, r, cdim = gblocks.shape
    tile = _row_tile(r, BF16_ROWS)
    x, y, c = _my_place()
    table = jnp.stack([4 * _flip(x, fx) + 2 * _flip(y, fy) + c for fx, fy in CHIP_FLIPS]).astype(jnp.int32)

    def body(table_ref, own0, own1, own2, own3, sib_ref, sums_ref, mine_ref):
        mine_ref[...] = own0[...].astype(F32) + sib_ref[0].astype(F32)
        for f, own in ((1, own1), (2, own2), (3, own3)):
            sums_ref[f - 1] = (own[...].astype(F32) + sib_ref[f].astype(F32)).astype(BF16)

    own_specs = [pl.BlockSpec((None, tile, cdim), functools.partial(lambda i, tab, f: (tab[f], i, 0), f=f))
                 for f in range(4)]
    return pl.pallas_call(
        body, name=name,
        grid_spec=pltpu.PrefetchScalarGridSpec(
            num_scalar_prefetch=1, grid=(r // tile,),
            in_specs=own_specs + [pl.BlockSpec((4, tile, cdim), lambda i, tab: (0, i, 0))],
            out_specs=[pl.BlockSpec((3, tile, cdim), lambda i, tab: (0, i, 0)),
                       pl.BlockSpec((tile, cdim), lambda i, tab: (i, 0))]),
        out_shape=[jax.ShapeDtypeStruct((3, r, cdim), BF16), jax.ShapeDtypeStruct((r, cdim), F32)],
        compiler_params=pltpu.CompilerParams(dimension_semantics=("arbitrary",), vmem_limit_bytes=VMEM_LIMIT),
    )(table, gblocks, gblocks, gblocks, gblocks, sib)


def _chip_sum_adamw(mine, ici, w, m, v, name):
    r, cdim = mine.shape
    tile = _row_tile(r, BF16_ROWS)

    def body(mine_ref, ici_ref, w_ref, m_ref, v_ref, g_ref, d_ref, nm_ref, nv_ref):
        g = mine_ref[...]
        for f in range(3):
            g = g + ici_ref[f].astype(F32)
        g_ref[...] = g
        d_ref[...], nm_ref[...], nv_ref[...] = _adamw_update(w_ref[...], g, m_ref[...], v_ref[...])

    spec = pl.BlockSpec((tile, cdim), lambda i: (i, 0))
    outs, _ = _call(
        body, name, (r // tile,), [mine, ici, w, m, v],
        [spec, pl.BlockSpec((3, tile, cdim), lambda i: (0, i, 0)), spec, spec, spec],
        [jax.ShapeDtypeStruct((r, cdim), F32)] * 4, [spec] * 4)
    return outs


REF_KV_COL = D_MODEL
REF_REST_COL = D_MODEL + 2 * KV_WIDTH
IN_CHUNK = 1280
IN_PIECES = ([(0, 0, D_MODEL)]
             + [(D_MODEL + n * IN_CHUNK, REF_REST_COL + n * IN_CHUNK, IN_CHUNK) for n in range(REST_WIDTH // IN_CHUNK)]
             + [(KV_COL, REF_KV_COL, 2 * KV_WIDTH)])


def _inproj_fwd(x, vec, w_t, b_in, rider):
    t = x.shape[0]
    tm = min(TOKEN_TILE, t)

    def body(x_ref, vec_ref, w_ref, b_ref, z_ref, h_ref):
        xf = x_ref[...]
        r = lax.rsqrt(jnp.mean(xf * xf, axis=-1, keepdims=True) + EPS)
        h = (xf * r) * (vec_ref[0:1, :] * (1.0 + vec_ref[1:2, :])) + vec_ref[2:3, :]
        hb = h.astype(BF16)
        h_ref[...] = hb
        for mine, ref, width in IN_PIECES:
            zc = lax.dot_general(hb, w_ref[ref:ref + width, :], NT_DIMS, preferred_element_type=F32)
            z_ref[:, mine:mine + width] = (zc + b_ref[:, ref:ref + width]).astype(BF16)

    return _call(
        body, "inproj_fwd", (t // tm,), [x, vec, w_t, b_in],
        [pl.BlockSpec((tm, D_MODEL), lambda i: (i, 0)), _full((SUBLANES, D_MODEL)),
         _full((IN_WIDTH, D_MODEL)), _full((1, IN_WIDTH))],
        [jax.ShapeDtypeStruct((t, IN_WIDTH), BF16), jax.ShapeDtypeStruct((t, D_MODEL), BF16)],
        [pl.BlockSpec((tm, IN_WIDTH), lambda i: (i, 0)), pl.BlockSpec((tm, D_MODEL), lambda i: (i, 0))],
        rider=rider)


PAIRS = GROUP // 2
STACK = PAIRS * WINDOW


ATTN_BLOCKS = 4
ATTN_BWD_BLOCKS = 1
LOG2E = 1.4426950408889634
LN2 = 0.6931471805599453
SCORE_SCALE = ATTN_SCALE * LOG2E


def _fill_window_bias(bias_ref):
    shape = bias_ref.shape[1:]
    kj = lax.broadcasted_iota(jnp.int32, shape, 0)
    qi = jnp.bitwise_and(lax.broadcasted_iota(jnp.int32, shape, 1), WINDOW - 1)
    in_prev = jnp.logical_and(kj < WINDOW, kj > qi)
    in_cur = jnp.logical_and(kj >= WINDOW, (kj - WINDOW) <= qi)
    bias_ref[0] = jnp.where(in_cur, 0.0, -jnp.inf)
    bias_ref[1] = jnp.where(jnp.logical_or(in_prev, in_cur), 0.0, -jnp.inf)


def _half_tiles(tile):
    low = lax.broadcasted_iota(jnp.int32, tile.shape, 1) < HEAD_DIM
    swapped = jnp.concatenate([tile[:, HEAD_DIM:], tile[:, :HEAD_DIM]], axis=1)
    zero = jnp.zeros_like(tile)
    return ((jnp.where(low, tile, zero), jnp.where(low, zero, swapped)),
            (jnp.where(low, swapped, zero), jnp.where(low, zero, tile)))


def _stack_pairs(ref, row0, j):
    return jnp.concatenate(
        [ref[pl.ds(row0, WINDOW), (j * PAIRS + p) * LANES:(j * PAIRS + p + 1) * LANES] for p in range(PAIRS)], axis=0)


def _per_pair_row(values):
    pair = lax.broadcasted_iota(jnp.int32, (1, STACK), 1) // WINDOW
    row = jnp.full((1, STACK), values[PAIRS - 1], F32)
    for p in range(PAIRS - 2, -1, -1):
        row = jnp.where(pair == p, values[p], row)
    return row


def _attn_fwd(z, sinks, rider):
    t = z.shape[0]
    tq = min(TOKEN_TILE, t)
    nblk = tq // WINDOW

    def body(q_ref, kv_ref, sink_ref, o_ref, lse_ref, bias_ref):
        i = pl.program_id(0)

        @pl.when(i == 0)
        def _():
            _fill_window_bias(bias_ref)

        def window(b):
            row0 = pl.multiple_of(b * WINDOW, WINDOW)
            start = i * tq + b * WINDOW
            prev = pl.multiple_of(jnp.maximum(start - WINDOW, 0), WINDOW)
            cur = pl.multiple_of(start, WINDOW)
            kvw = jnp.concatenate([kv_ref[pl.ds(prev, WINDOW), :], kv_ref[pl.ds(cur, WINDOW), :]], axis=0)
            return row0, _half_tiles(kvw[:, :KV_WIDTH]), _half_tiles(kvw[:, KV_WIDTH:]), bias_ref[jnp.minimum(start, 1)]

        def block_group(bb, carry):
            windows = [window(bb * ATTN_BLOCKS + n) for n in range(ATTN_BLOCKS)]
            for j in range(N_KV_HEADS):
                for pr in range(PAIRS):
                    cols = slice((j * PAIRS + pr) * LANES, (j * PAIRS + pr + 1) * LANES)
                    o_ts = [jnp.zeros((LANES, WINDOW), F32) for _ in windows]
                    for parity in range(2):
                        h = j * GROUP + 2 * pr + parity
                        sink = sink_ref[h] * LOG2E
                        for n, (row0, k_halves, v_halves, bias) in enumerate(windows):
                            qp = q_ref[pl.ds(row0, WINDOW), cols]
                            s = lax.dot_general(k_halves[j][parity], qp, NT_DIMS, preferred_element_type=F32)
                            s = s * SCORE_SCALE + bias
                            m = jnp.maximum(jnp.max(s, axis=0, keepdims=True), sink)
                            p = jnp.exp2(s - m)
                            denom = jnp.sum(p, axis=0, keepdims=True) + jnp.exp2(sink - m)
                            pv = lax.dot_general(v_halves[j][parity], p.astype(BF16), TN_DIMS,
                                                 preferred_element_type=F32)
                            o_ts[n] = o_ts[n] + pv * (1.0 / denom)
                            lse_ref[h:h + 1, pl.ds(row0, WINDOW)] = m + jnp.log2(denom)
                    for n, (row0, _, _, _) in enumerate(windows):
                        o_ref[pl.ds(row0, WINDOW), cols] = jnp.transpose(o_ts[n].astype(BF16))
            return carry

        lax.fori_loop(0, nblk // ATTN_BLOCKS, block_group, 0)

    return _call(
        body, "attn_fwd", (t // tq,), [z, z, sinks],
        [pl.BlockSpec((tq, D_MODEL), lambda i: (i, 0)),
         pl.BlockSpec((t, 2 * KV_WIDTH), lambda i: (0, KV_COL // (2 * KV_WIDTH))),
         pl.BlockSpec(memory_space=pltpu.SMEM)],
        [jax.ShapeDtypeStruct((t, D_MODEL), BF16), jax.ShapeDtypeStruct((N_Q_HEADS, t), F32)],
        [pl.BlockSpec((tq, D_MODEL), lambda i: (i, 0)), pl.BlockSpec((N_Q_HEADS, tq), lambda i: (0, i))],
        scratch=[pltpu.VMEM((2, 2 * WINDOW, WINDOW), F32)], rider=rider)


HALO = BF16_ROWS


def _shift_down(u, uh, k):
    rolled = pltpu.roll(u, k, 0)
    row = lax.broadcasted_iota(jnp.int32, (SUBLANES, u.shape[1]), 0)
    top = rolled[:SUBLANES, :]
    for j in range(k):
        top = jnp.where(row == j, uh[HALO - k + j:HALO - k + j + 1, :], top)
    return jnp.concatenate([top, rolled[SUBLANES:, :]], axis=0)


def _shift_up(u, nxt, k):
    n = u.shape[0]
    rolled = pltpu.roll(u, n - k, 0)
    row = lax.broadcasted_iota(jnp.int32, (SUBLANES, u.shape[1]), 0)
    bottom = rolled[n - SUBLANES:, :]
    for j in range(k):
        bottom = jnp.where(row == SUBLANES - k + j, nxt[j:j + 1, :], bottom)
    return jnp.concatenate([rolled[:n - SUBLANES, :], bottom], axis=0)


def _conv_inputs(cc_ref, cx_ref, hc_ref, hx_ref, first_tile):
    cc = cc_ref[...].astype(F32)
    cx = cx_ref[...].astype(F32)
    u = cc * cx
    uh = jnp.where(first_tile, 0.0, hc_ref[...].astype(F32) * hx_ref[...].astype(F32))
    return cc, cx, u, _shift_down(u, uh, 1), _shift_down(u, uh, 2)


def _z_specs(tm, order):
    per_tile = tm // HALO
    cols = [pl.BlockSpec((tm, D_MODEL), functools.partial(lambda i, j: (order(i), j), j=j)) for j in range(1, 6)]
    halos = [pl.BlockSpec((HALO, D_MODEL),
                          functools.partial(lambda i, j: (jnp.maximum(order(i) * per_tile - 1, 0), j), j=j))
             for j in (2, 3)]
    return cols + halos


def _mix_fwd(x, attn, z, vec, w_out):
    t = x.shape[0]
    tm = min(TOKEN_TILE, t)

    def body(x_ref, a_ref, cb_ref, cc_ref, cx_ref, ga_ref, gc_ref, hc_ref, hx_ref, vec_ref, w_ref,
             m_ref, x2_ref, h2_ref, o_ref):
        i = pl.program_id(0)
        _, _, u, u1, u2 = _conv_inputs(cc_ref, cx_ref, hc_ref, hx_ref, i == 0)
        cv = vec_ref[4:5, :] * u2 + vec_ref[5:6, :] * u1 + vec_ref[6:7, :] * u
        conv = cb_ref[...].astype(F32) * cv
        merged = (_sigmoid(ga_ref[...].astype(F32)) * a_ref[...].astype(F32)
                  + _sigmoid(gc_ref[...].astype(F32)) * conv)
        mb = merged.astype(BF16)
        m_ref[...] = mb
        o = jnp.dot(mb, w_ref[...], preferred_element_type=F32)
        o_ref[...] = o.astype(BF16)
        x2 = x_ref[...] + vec_ref[0:1, :] * o
        x2_ref[...] = x2
        r = lax.rsqrt(jnp.mean(x2 * x2, axis=-1, keepdims=True) + EPS)
        h2 = (x2 * r) * (vec_ref[1:2, :] * (1.0 + vec_ref[2:3, :])) + vec_ref[3:4, :]
        h2_ref[...] = h2.astype(BF16)

    tok = pl.BlockSpec((tm, D_MODEL), lambda i: (i, 0))
    outs, _ = _call(
        body, "mix_fwd", (t // tm,), [x, attn, z, z, z, z, z, z, z, vec, w_out],
        [tok, tok] + _z_specs(tm, lambda i: i) + [_full((SUBLANES, D_MODEL)), _full((D_MODEL, D_MODEL))],
        [jax.ShapeDtypeStruct((t, D_MODEL), BF16), jax.ShapeDtypeStruct((t, D_MODEL), F32),
         jax.ShapeDtypeStruct((t, D_MODEL), BF16), jax.ShapeDtypeStruct((t, D_MODEL), BF16)],
        [tok, tok, tok, tok])
    return outs


def _ffn_fwd(h2, w_t):
    t = h2.shape[0]
    tm = min(TOKEN_TILE, t)

    def body(h_ref, w_ref, gu_ref, a_ref):
        hb = h_ref[...]
        for n in range(D_FF // FF_CHUNK):
            lo, hi = n * FF_CHUNK, (n + 1) * FF_CHUNK
            g = lax.dot_general(hb, w_ref[lo:hi, :], NT_DIMS, preferred_element_type=F32)
            u = lax.dot_general(hb, w_ref[D_FF + lo:D_FF + hi, :], NT_DIMS, preferred_element_type=F32)
            sg = _sigmoid(g)
            silu = g * sg
            gu_ref[:, lo:hi] = (u * (sg + silu * (1.0 - sg))).astype(BF16)
            gu_ref[:, D_FF + lo:D_FF + hi] = silu.astype(BF16)
            a_ref[:, lo:hi] = (silu * u).astype(BF16)

    outs, _ = _call(
        body, "ffn_fwd", (t // tm,), [h2, w_t],
        [pl.BlockSpec((tm, D_MODEL), lambda i: (i, 0)), _full((2 * D_FF, D_MODEL))],
        [jax.ShapeDtypeStruct((t, 2 * D_FF), BF16), jax.ShapeDtypeStruct((t, D_FF), BF16)],
        [pl.BlockSpec((tm, 2 * D_FF), lambda i: (i, 0)), pl.BlockSpec((tm, D_FF), lambda i: (i, 0))])
    return outs


def _ffn_out_loss(a, gu, x2, target, vec, w_ffn_out):
    t = a.shape[0]
    tm = min(TOKEN_TILE, t)

    def body(a_ref, gu_ref, x2_ref, t_ref, vec_ref, w_ref, dx3_ref, df_ref, dgu_ref, acc_ref):
        @pl.when(pl.program_id(0) == 0)
        def _():
            acc_ref[...] = jnp.zeros_like(acc_ref)

        ga2 = vec_ref[0:1, :]
        gf = vec_ref[1:2, :]
        parts = min(ROW_PARTS, tm // LANES)
        part_rows = [slice(n * (tm // parts), (n + 1) * (tm // parts)) for n in range(parts)]

        def head(rows, f):
            x3 = x2_ref[rows, :] + ga2 * f
            r = lax.rsqrt(jnp.mean(x3 * x3, axis=-1, keepdims=True) + EPS)
            xn = x3 * r
            err = xn * gf - t_ref[rows, :]
            dxn = err * (gf * (1.0 / D_MODEL))
            dx3 = r * (dxn - xn * jnp.mean(dxn * xn, axis=-1, keepdims=True))
            dx3_ref[rows, :] = dx3.astype(GRAD_STREAM)
            sums = (jnp.sum(err * err, axis=0, keepdims=True),
                    jnp.sum(err * xn, axis=0, keepdims=True) * (1.0 / D_MODEL),
                    jnp.sum(dx3 * f, axis=0, keepdims=True))
            df = (dx3 * ga2).astype(BF16)
            df_ref[rows, :] = df
            return df, sums

        def tail(rows, df):
            for n in range(D_FF // FF_CHUNK):
                lo, hi = n * FF_CHUNK, (n + 1) * FF_CHUNK
                da = lax.dot_general(df, w_ref[lo:hi, :], NT_DIMS, preferred_element_type=F32)
                dgu_ref[rows, lo:hi] = (da * gu_ref[rows, lo:hi].astype(F32)).astype(BF16)
                dgu_ref[rows, D_FF + lo:D_FF + hi] = (da * gu_ref[rows, D_FF + lo:D_FF + hi].astype(F32)).astype(BF16)

        fs = [jnp.dot(a_ref[rows, :], w_ref[...], preferred_element_type=F32) for rows in part_rows]
        heads = [head(rows, f) for rows, f in zip(part_rows, fs)]
        for rows, (df, _) in zip(part_rows, heads):
            tail(rows, df)
        for k in range(3):
            total = heads[0][1][k]
            for _, sums in heads[1:]:
                total = total + sums[k]
            acc_ref[k:k + 1, :] += total

    tok = pl.BlockSpec((tm, D_MODEL), lambda i: (i, 0))
    outs, _ = _call(
        body, "ffn_out_loss", (t // tm,), [a, gu, x2, target, vec, w_ffn_out],
        [pl.BlockSpec((tm, D_FF), lambda i: (i, 0)), pl.BlockSpec((tm, 2 * D_FF), lambda i: (i, 0)),
         tok, tok, _full((SUBLANES, D_MODEL)), _full((D_FF, D_MODEL))],
        [jax.ShapeDtypeStruct((t, D_MODEL), GRAD_STREAM), jax.ShapeDtypeStruct((t, D_MODEL), BF16),
         jax.ShapeDtypeStruct((t, 2 * D_FF), BF16), jax.ShapeDtypeStruct((SUBLANES, D_MODEL), F32)],
        [tok, tok, pl.BlockSpec((tm, 2 * D_FF), lambda i: (i, 0)), _full((SUBLANES, D_MODEL))])
    return outs


def _ffn_in_bwd(dgu, x2, dx3, vec, w_t, rider):
    t = x2.shape[0]
    tm = min(TOKEN_TILE, t)

    def body(dgu_ref, x2_ref, dx3_ref, vec_ref, wf_ref, dx2_ref, acc_ref):
        @pl.when(pl.program_id(0) == 0)
        def _():
            acc_ref[...] = jnp.zeros_like(acc_ref)

        gffn = vec_ref[0:1, :]
        sc2 = vec_ref[1:2, :]
        parts = min(ROW_PARTS, tm // LANES)
        part_rows = [slice(n * (tm // parts), (n + 1) * (tm // parts)) for n in range(parts)]
        dhs = [jnp.dot(dgu_ref[rows, :], wf_ref[...], preferred_element_type=F32) for rows in part_rows]
        gs = gffn * (1.0 + sc2)
        sum_dh = jnp.zeros((1, D_MODEL), F32)
        sum_dh_xn = jnp.zeros((1, D_MODEL), F32)
        for rows, dh2 in zip(part_rows, dhs):
            x2 = x2_ref[rows, :]
            r = lax.rsqrt(jnp.mean(x2 * x2, axis=-1, keepdims=True) + EPS)
            xn = x2 * r
            dh_xn = dh2 * xn
            sum_dh = sum_dh + jnp.sum(dh2, axis=0, keepdims=True)
            sum_dh_xn = sum_dh_xn + jnp.sum(dh_xn, axis=0, keepdims=True)
            dx2 = dx3_ref[rows, :].astype(F32) + r * (dh2 * gs - xn * jnp.mean(dh_xn * gs, axis=-1, keepdims=True))
            dx2_ref[rows, :] = dx2.astype(GRAD_STREAM)
        acc_ref[0:1, :] += sum_dh
        acc_ref[1:2, :] += sum_dh_xn * gffn
        acc_ref[2:3, :] += sum_dh_xn * (1.0 + sc2)

    tok = pl.BlockSpec((tm, D_MODEL), lambda i: (i, 0))
    return _call(
        body, "ffn_in_bwd", (t // tm,), [dgu, x2, dx3, vec, w_t],
        [pl.BlockSpec((tm, 2 * D_FF), lambda i: (i, 0)), tok, tok, _full((SUBLANES, D_MODEL)),
         _full((2 * D_FF, D_MODEL))],
        [jax.ShapeDtypeStruct((t, D_MODEL), GRAD_STREAM), jax.ShapeDtypeStruct((SUBLANES, D_MODEL), F32)],
        [tok, _full((SUBLANES, D_MODEL))], rider=rider)


def _mix_bwd(dx2, oproj, attn, z, vec, w_out, rider):
    t = dx2.shape[0]
    tm = min(TOKEN_TILE, t)
    nt = t // tm
    rev = lambda i: nt - 1 - i

    def body(dx2_ref, m_ref, a_ref, cb_ref, cc_ref, cx_ref, ga_ref, gc_ref, hc_ref, hx_ref,
             vec_ref, wo_ref, do_ref, da_ref, dr_ref, acc_ref, carry_ref):
        i = pl.program_id(0)

        @pl.when(i == 0)
        def _():
            acc_ref[...] = jnp.zeros_like(acc_ref)
            carry_ref[...] = jnp.zeros_like(carry_ref)

        ga1 = vec_ref[0:1, :]
        w0, w1, w2 = vec_ref[1:2, :], vec_ref[2:3, :], vec_ref[3:4, :]
        dx2 = dx2_ref[...].astype(F32)
        acc_ref[0:1, :] += jnp.sum(dx2 * m_ref[...].astype(F32), axis=0, keepdims=True)
        do = (dx2 * ga1).astype(BF16)
        do_ref[...] = do
        dm = lax.dot_general(do, wo_ref[...], NT_DIMS, preferred_element_type=F32)

        cc, cx, u, u1, u2 = _conv_inputs(cc_ref, cx_ref, hc_ref, hx_ref, i == nt - 1)
        cv = w0 * u2 + w1 * u1 + w2 * u
        cb = cb_ref[...].astype(F32)
        sa = _sigmoid(ga_ref[...].astype(F32))
        sc = _sigmoid(gc_ref[...].astype(F32))
        attn = a_ref[...].astype(F32)
        dattn = dm * sa
        da_ref[...] = dattn.astype(BF16)
        dconv = dm * sc
        dconv_b = dconv * cv
        dr_ref[:, 3 * D_MODEL:4 * D_MODEL] = (dattn * attn * (1.0 - sa)).astype(BF16)
        dr_ref[:, 4 * D_MODEL:5 * D_MODEL] = (dconv_b * cb * (1.0 - sc)).astype(BF16)
        dr_ref[:, 0:D_MODEL] = dconv_b.astype(BF16)
        dcv = dconv * cb
        acc_ref[1:2, :] += jnp.sum(dcv * u2, axis=0, keepdims=True)
        acc_ref[2:3, :] += jnp.sum(dcv * u1, axis=0, keepdims=True)
        acc_ref[3:4, :] += jnp.sum(dcv * u, axis=0, keepdims=True)
        nxt = carry_ref[...]
        du = w2 * dcv + w1 * _shift_up(dcv, nxt, 1) + w0 * _shift_up(dcv, nxt, 2)
        carry_ref[...] = dcv[0:SUBLANES, :]
        dr_ref[:, D_MODEL:2 * D_MODEL] = (du * cx).astype(BF16)
        dr_ref[:, 2 * D_MODEL:3 * D_MODEL] = (du * cc).astype(BF16)

    tok = pl.BlockSpec((tm, D_MODEL), lambda i: (rev(i), 0))
    return _call(
        body, "mix_bwd", (nt,), [dx2, oproj, attn, z, z, z, z, z, z, z, vec, w_out],
        [tok, tok, tok] + _z_specs(tm, rev) + [_full((SUBLANES, D_MODEL)), _full((D_MODEL, D_MODEL))],
        [jax.ShapeDtypeStruct((t, D_MODEL), BF16), jax.ShapeDtypeStruct((t, D_MODEL), BF16),
         jax.ShapeDtypeStruct((t, REST_WIDTH), BF16), jax.ShapeDtypeStruct((SUBLANES, D_MODEL), F32)],
        [tok, tok, pl.BlockSpec((tm, REST_WIDTH), lambda i: (rev(i), 0)), _full((SUBLANES, D_MODEL))],
        scratch=[pltpu.VMEM((SUBLANES, D_MODEL), F32)], rider=rider)


def _attn_bwd(z, dattn, attn, lse, sinks, rider):
    t = z.shape[0]
    tq = min(TOKEN_TILE, t)
    nblk = tq // WINDOW
    nt = t // tq

    def body(q_ref, kv_ref, do_ref, o_ref, lse_ref, sink_ref, dq_ref, dkv_ref, ds_ref, acc_ref, bias_ref):
        i = pl.program_id(0)

        @pl.when(i == 0)
        def _():
            acc_ref[...] = jnp.zeros_like(acc_ref)
            ds_ref[...] = jnp.zeros_like(ds_ref)
            _fill_window_bias(bias_ref)

        lane = lax.broadcasted_iota(jnp.int32, (1, LANES), 1)
        ind_row = lax.broadcasted_iota(jnp.int32, (SUBLANES, LANES), 0)
        ind_low = lax.broadcasted_iota(jnp.int32, (SUBLANES, LANES), 1) < HEAD_DIM
        indicator = jnp.where(jnp.logical_or(jnp.logical_and(ind_row == 0, ind_low),
                                             jnp.logical_and(ind_row == 1, jnp.logical_not(ind_low))),
                              1.0, 0.0).astype(BF16)
        low = lax.broadcasted_iota(jnp.int32, (2 * WINDOW, LANES), 1) < HEAD_DIM

        def both_heads(even, odd):
            picked = jnp.where(low, even, odd)
            return picked + jnp.concatenate([picked[:, HEAD_DIM:], picked[:, :HEAD_DIM]], axis=1)

        def window(b):
            row0 = pl.multiple_of(b * WINDOW, WINDOW)
            start = i * tq + b * WINDOW
            prev = pl.multiple_of(jnp.maximum(start - WINDOW, 0), WINDOW)
            cur = pl.multiple_of(start, WINDOW)
            kvw = jnp.concatenate([kv_ref[pl.ds(prev, WINDOW), :], kv_ref[pl.ds(cur, WINDOW), :]], axis=0)
            return (row0, prev, cur, _half_tiles(kvw[:, :KV_WIDTH]), _half_tiles(kvw[:, KV_WIDTH:]),
                    bias_ref[jnp.minimum(start, 1)])

        def block_group(bb, dsink):
            windows = [window(bb * ATTN_BWD_BLOCKS + n) for n in range(ATTN_BWD_BLOCKS)]
            dk_groups = [[] for _ in windows]
            dv_groups = [[] for _ in windows]
            for j in range(N_KV_HEADS):
                stacks, deltas, dq_ts = [], [], []
                for row0, _, _, _, _, _ in windows:
                    qst = _stack_pairs(q_ref, row0, j)
                    dost = _stack_pairs(do_ref, row0, j)
                    prod = dost.astype(F32) * _stack_pairs(o_ref, row0, j).astype(F32)
                    prod_hi = prod.astype(BF16)
                    prod_lo = (prod - prod_hi.astype(F32)).astype(BF16)
                    stacks.append((qst, dost))
                    deltas.append(lax.dot_general(indicator, prod_hi, NT_DIMS, preferred_element_type=F32)
                                  + lax.dot_general(indicator, prod_lo, NT_DIMS, preferred_element_type=F32))
                    dq_ts.append(jnp.zeros((LANES, STACK), F32))
                dk_par = [[] for _ in windows]
                dv_par = [[] for _ in windows]
                for parity in range(2):
                    heads = [j * GROUP + 2 * p + parity for p in range(PAIRS)]
                    sink = _per_pair_row([sink_ref[h] * LOG2E for h in heads])
                    for n, (row0, _, _, k_halves, v_halves, bias) in enumerate(windows):
                        qst, dost = stacks[n]
                        kk, vv = k_halves[j][parity], v_halves[j][parity]
                        s = lax.dot_general(kk, qst, NT_DIMS, preferred_element_type=F32) * SCORE_SCALE + bias
                        lse = jnp.concatenate([lse_ref[h:h + 1, pl.ds(row0, WINDOW)] for h in heads], axis=1)
                        p = jnp.exp2(s - lse)
                        dp = lax.dot_general(vv, dost, NT_DIMS, preferred_element_type=F32)
                        delta = deltas[n][parity:parity + 1, :]
                        dsb = (p * (dp - delta)).astype(BF16)
                        dq_ts[n] = dq_ts[n] + lax.dot_general(kk, dsb, TN_DIMS, preferred_element_type=F32)
                        dk_par[n].append(jnp.dot(dsb, qst, preferred_element_type=F32))
                        dv_par[n].append(jnp.dot(p.astype(BF16), dost, preferred_element_type=F32))
                        weighted = jnp.exp2(sink - lse) * delta
                        for pr, h in enumerate(heads):
                            dsink = dsink - jnp.where(
                                lane == h, jnp.sum(weighted[:, pr * WINDOW:(pr + 1) * WINDOW]), 0.0)
                for n, (row0, _, _, _, _, _) in enumerate(windows):
                    dq_st = jnp.transpose((dq_ts[n] * ATTN_SCALE).astype(BF16))
                    for pr in range(PAIRS):
                        dq_ref[pl.ds(row0, WINDOW), (j * PAIRS + pr) * LANES:(j * PAIRS + pr + 1) * LANES] = (
                            dq_st[pr * WINDOW:(pr + 1) * WINDOW, :])
                    dk_groups[n].append(both_heads(dk_par[n][0], dk_par[n][1]))
                    dv_groups[n].append(both_heads(dv_par[n][0], dv_par[n][1]))
            for n, (_, prev, cur, _, _, _) in enumerate(windows):
                blk = jnp.concatenate([jnp.where(low, dk_groups[n][0], dk_groups[n][1]) * ATTN_SCALE,
                                       jnp.where(low, dv_groups[n][0], dv_groups[n][1])], axis=1)
                acc_ref[pl.ds(prev, WINDOW), :] += blk[:WINDOW, :]
                acc_ref[pl.ds(cur, WINDOW), :] += blk[WINDOW:, :]
            return dsink

        dsink = lax.fori_loop(0, nblk // ATTN_BWD_BLOCKS, block_group, jnp.zeros((1, LANES), F32))
        ds_ref[0:1, :] += dsink

        @pl.when(i == nt - 1)
        def _():
            dkv_ref[...] = acc_ref[...].astype(BF16)

    tok = pl.BlockSpec((tq, D_MODEL), lambda i: (i, 0))
    return _call(
        body, "attn_bwd", (nt,), [z, z, dattn, attn, lse, sinks],
        [tok, pl.BlockSpec((t, 2 * KV_WIDTH), lambda i: (0, KV_COL // (2 * KV_WIDTH))), tok, tok,
         pl.BlockSpec((N_Q_HEADS, tq), lambda i: (0, i)), pl.BlockSpec(memory_space=pltpu.SMEM)],
        [jax.ShapeDtypeStruct((t, D_MODEL), BF16), jax.ShapeDtypeStruct((t, 2 * KV_WIDTH), BF16),
         jax.ShapeDtypeStruct((SUBLANES, LANES), F32)],
        [tok, _full((t, 2 * KV_WIDTH)), _full((SUBLANES, LANES))],
        scratch=[pltpu.VMEM((t, 2 * KV_WIDTH), F32), pltpu.VMEM((2, 2 * WINDOW, STACK), F32)], rider=rider)


def _inproj_bwd(dq, drest, dkv, x, dx2, vec, w_t, rider):
    t = x.shape[0]
    tm = min(TOKEN_TILE, t)

    def body(dq_ref, dr_ref, dkv_ref, x_ref, dx2_ref, vec_ref, w_ref, gx_ref, acc_ref, db_ref):
        @pl.when(pl.program_id(0) == 0)
        def _():
            acc_ref[...] = jnp.zeros_like(acc_ref)
            db_ref[...] = jnp.zeros_like(db_ref)

        g = vec_ref[0:1, :]
        sc1 = vec_ref[1:2, :]
        dqb, drb, dkvb = dq_ref[...], dr_ref[...], dkv_ref[...]
        dh = jnp.dot(dqb, w_ref[:REF_KV_COL, :], preferred_element_type=F32)
        dh = dh + jnp.dot(drb, w_ref[REF_REST_COL:, :], preferred_element_type=F32)
        dh = dh + jnp.dot(dkvb, w_ref[REF_KV_COL:REF_REST_COL, :], preferred_element_type=F32)
        db_ref[:, :REF_KV_COL] += jnp.sum(dqb.astype(F32), axis=0, keepdims=True)
        db_ref[:, REF_REST_COL:] += jnp.sum(drb.astype(F32), axis=0, keepdims=True)
        db_ref[:, REF_KV_COL:REF_REST_COL] += jnp.sum(dkvb.astype(F32), axis=0, keepdims=True)
        xf = x_ref[...]
        r = lax.rsqrt(jnp.mean(xf * xf, axis=-1, keepdims=True) + EPS)
        xn = xf * r
        gs = g * (1.0 + sc1)
        dh_xn = dh * xn
        sum_dh_xn = jnp.sum(dh_xn, axis=0, keepdims=True)
        acc_ref[0:1, :] += jnp.sum(dh, axis=0, keepdims=True)
        acc_ref[1:2, :] += sum_dh_xn * g
        acc_ref[2:3, :] += sum_dh_xn * (1.0 + sc1)
        gx_ref[...] = dx2_ref[...].astype(F32) + r * (dh * gs - xn * jnp.mean(dh_xn * gs, axis=-1, keepdims=True))

    tok = pl.BlockSpec((tm, D_MODEL), lambda i: (i, 0))
    return _call(
        body, "inproj_bwd", (t // tm,), [dq, drest, dkv, x, dx2, vec, w_t],
        [tok, pl.BlockSpec((tm, REST_WIDTH), lambda i: (i, 0)),
         pl.BlockSpec((tm, 2 * KV_WIDTH), lambda i: (i, 0)), tok, tok,
         _full((SUBLANES, D_MODEL)), _full((IN_WIDTH, D_MODEL))],
        [jax.ShapeDtypeStruct((t, D_MODEL), F32), jax.ShapeDtypeStruct((SUBLANES, D_MODEL), F32),
         jax.ShapeDtypeStruct((1, IN_WIDTH), F32)],
        [tok, _full((SUBLANES, D_MODEL)), _full((1, IN_WIDTH))], rider=rider)


def _weight_grad(b, a, name, bn, rows=None, row0=0, into=None, rider=None):
    t, n = b.shape
    m = a.shape[1]
    rows = n if rows is None else rows
    tk = min(TOKEN_TILE, t)
    for cand in (4 * TOKEN_TILE, 2 * TOKEN_TILE):
        if t % cand == 0 and 2 * cand * (bn + m) * 2 + bn * m * 4 <= WGRAD_VMEM:
            tk = cand
            break
    nk = t // tk
    block0 = row0 // bn

    def body(b_ref, a_ref, *rest):
        out_ref, acc_ref = rest[-2:]
        k = pl.program_id(1)

        @pl.when(k == 0)
        def _():
            acc_ref[...] = jnp.zeros_like(acc_ref)

        acc_ref[...] += lax.dot_general(b_ref[...], a_ref[...], TN_DIMS, preferred_element_type=F32)

        @pl.when(k == nk - 1)
        def _():
            out_ref[...] = acc_ref[...].astype(BF16)

    outs, routs = _call(
        body, name, (n // bn, nk), [b, a] + ([] if into is None else [into]),
        [pl.BlockSpec((tk, bn), lambda j, k: (k, j)), pl.BlockSpec((tk, m), lambda j, k: (k, 0))]
        + ([] if into is None else [ANY]),
        [jax.ShapeDtypeStruct((rows, m), BF16)], [pl.BlockSpec((bn, m), lambda j, k: (block0 + j, 0))],
        scratch=[pltpu.VMEM((bn, m), F32)], rider=rider, aliases=None if into is None else {2: 0})
    return outs[0], routs


def _to_rows(v):
    n = v.shape[0]
    padded = -(-n // (SUBLANES * LANES)) * SUBLANES * LANES
    return jnp.pad(v, (0, padded - n)).reshape(padded // LANES, LANES)


def _vec_rows(*rows):
    stacked = jnp.concatenate([r.reshape(1, D_MODEL) for r in rows], axis=0)
    return jnp.pad(stacked, ((0, SUBLANES - len(rows)), (0, 0)))


def kernel(x, c, w_ada, b_ada, g_mix, w_in, b_in, sinks, conv_w, w_out, g_ffn, w_ffn_in, w_ffn_out, g_final, loss_target, m_w_ada, m_b_ada, m_g_mix, m_w_in, m_b_in, m_sinks, m_conv_w, m_w_out, m_g_ffn, m_w_ffn_in, m_w_ffn_out, m_g_final, v_w_ada, v_b_ada, v_g_mix, v_w_in, v_b_in, v_sinks, v_conv_w, v_w_out, v_g_ffn, v_w_ffn_in, v_w_ffn_out, v_g_final):
    ix, iy, ic = _my_place()
    me = 4 * ix + 2 * iy + ic
    xs = x[0]
    target = loss_target[0]
    ada_cols = w_ada.shape[2]
    conv_cols = conv_w.shape[2]

    wt_in, wt_fi = jnp.transpose(w_in[0]), jnp.transpose(w_ffn_in[0])
    b_cols = lax.dynamic_slice_in_dim(b_ada, me * ada_cols, ada_cols, axis=1)
    g_in, (cast_fi, cast_out, cast_fo), first, mod_all = _gather_first_weight(
        wt_in, [wt_fi, w_out[0], w_ffn_out[0]], _to_rows(jnp.concatenate([c[0], conv_w[0].reshape(-1)])),
        w_ada[0], b_cols)
    first = first.reshape(N_DEV, -1)
    c_all = first[:, :D_MODEL]
    conv_full = jnp.transpose(first[:, D_MODEL:D_MODEL + 3 * conv_cols].reshape(N_DEV, 3, conv_cols), (1, 0, 2))
    conv_full = conv_full.reshape(3, D_MODEL)
    mod = lax.dynamic_index_in_dim(mod_all, me, axis=1, keepdims=False).reshape(N_MOD, D_MODEL)
    sh1, sc1, ga1, sh2, sc2, ga2 = [mod[i:i + 1] for i in range(N_MOD)]
    w_in_t = g_in.reshape(IN_WIDTH, D_MODEL)
    (z, h1), (g_fi, g_out) = _inproj_fwd(xs, _vec_rows(g_mix, sc1, sh1), w_in_t, b_in,
                                         _gather_rider([cast_fi, cast_out]))
    w_fi_t = g_fi.reshape(2 * D_FF, D_MODEL)
    w_out_full = g_out.reshape(D_MODEL, D_MODEL)
    (attn, lse), (g_fo,) = _attn_fwd(z, sinks[0], _gather_rider([cast_fo]))
    w_fo_full = g_fo.reshape(D_FF, D_MODEL)
    merged, x2, h2, oproj = _mix_fwd(
        xs, attn, z, _vec_rows(ga1, g_ffn, sc2, sh2, conv_full[0], conv_full[1], conv_full[2]), w_out_full)
    gu, act = _ffn_fwd(h2, w_fi_t)
    dx3, df, dgu, acc_l = _ffn_out_loss(act, gu, x2, target, _vec_rows(ga2, g_final), w_fo_full)

    gw_fo, _ = _weight_grad(act, df, "wgrad_ffn_out", D_FF)
    gw_fi, _ = _weight_grad(dgu, h2, "wgrad_ffn_in", D_FF)
    blocks_fo = gw_fo.reshape(N_DEV, D_FF // N_DEV, D_MODEL)
    blocks_fi = gw_fi.reshape(N_DEV, 2 * D_FF // N_DEV, D_MODEL)
    (dx2, acc_f), (sib_fo, sib_fi) = _ffn_in_bwd(dgu, x2, dx3, _vec_rows(g_ffn, sc2), w_fi_t,
                                                 _sibling_rider([blocks_fo, blocks_fi]))
    sums_fo, mine_fo = _sibling_sum(blocks_fo, sib_fo, "sibling_sum_ffn_out")
    sums_fi, mine_fi = _sibling_sum(blocks_fi, sib_fi, "sibling_sum_ffn_in")
    (dout, dattn, drest, acc_m), (ici_fo, ici_fi) = _mix_bwd(
        dx2, oproj, attn, z, _vec_rows(ga1, conv_full[0], conv_full[1], conv_full[2]), w_out_full,
        _chip_rider([sums_fo, sums_fi]))
    gw_out, _ = _weight_grad(merged, dout, "wgrad_out", D_MODEL)
    blocks_out = gw_out.reshape(N_DEV, D_MODEL // N_DEV, D_MODEL)
    (dq, dkv, dsink), (sib_out,) = _attn_bwd(z, dattn, attn, lse, sinks[0], _sibling_rider([blocks_out]))
    sums_out, mine_out = _sibling_sum(blocks_out, sib_out, "sibling_sum_out")
    gw_in, (ici_out,) = _weight_grad(drest, h1, "wgrad_in_rest", IN_CHUNK, rows=IN_WIDTH, row0=REF_REST_COL,
                                     rider=_chip_rider([sums_out]))
    gw_in, _ = _weight_grad(dq, h1, "wgrad_in_q", D_MODEL, rows=IN_WIDTH, row0=0, into=gw_in)
    gw_in, _ = _weight_grad(dkv, h1, "wgrad_in_kv", 2 * KV_WIDTH, rows=IN_WIDTH, row0=REF_KV_COL, into=gw_in)
    blocks_in = gw_in.reshape(N_DEV, IN_WIDTH // N_DEV, D_MODEL)
    (sib_in,) = _carry(_sibling_rider([blocks_in]), "sibling_w_in")
    sums_in, mine_in = _sibling_sum(blocks_in, sib_in, "sibling_sum_in")
    (grad_x, acc_i, db_in), (ici_in,) = _inproj_bwd(dq, drest, dkv, xs, dx2, _vec_rows(g_mix, sc1), w_in_t,
                                                    _chip_rider([sums_in]))

    widen = lambda vec: jnp.pad(vec, (0, -vec.shape[0] % D_MODEL))
    packed = jnp.concatenate([
        acc_i[0], acc_i[1], acc_m[0], acc_f[0], acc_f[1], acc_l[2],
        acc_i[2], widen(db_in[0]), acc_f[2], acc_l[1],
        acc_m[1], acc_m[2], acc_m[3], widen(dsink[0]), acc_l[0],
        jnp.zeros(((PACK_ROWS - PACK_SQERR - 1) * D_MODEL,), F32)]).reshape(PACK_ROWS, D_MODEL)
    packed_all = _small_allgather(packed, "gather_small")
    dmod_all = packed_all[:, PACK_DMOD:PACK_DMOD + N_MOD, :].reshape(N_DEV, N_MOD * D_MODEL)
    dmod_cols = lax.dynamic_slice_in_dim(dmod_all, me * ada_cols, ada_cols, axis=1)
    g_w_ada = _ada_weight_grad(c_all, dmod_cols)
    row_of = lambda a: a.reshape(1, -1)
    small, g_conv_full, loss = _small_finalize(packed_all, {
        "b_ada": (b_ada, m_b_ada, v_b_ada), "g_mix": (g_mix, m_g_mix, v_g_mix), "b_in": (b_in, m_b_in, v_b_in),
        "g_ffn": (g_ffn, m_g_ffn, v_g_ffn), "sinks": (sinks, m_sinks, v_sinks),
        "g_final": (row_of(g_final), row_of(m_g_final), row_of(v_g_final))})
    small["g_final"] = tuple(o.reshape(g_final.shape) for o in small["g_final"])
    g_conv = lax.dynamic_slice_in_dim(g_conv_full, me * conv_cols, conv_cols, axis=1)
    d_conv, nm_conv, nv_conv = _adamw(conv_w[0], g_conv, m_conv_w[0], v_conv_w[0], "adamw_conv_w")
    small["conv_w"] = (g_conv[None], d_conv[None], nm_conv[None], nv_conv[None])

    def reduced(mine, ici, w, m, v, name, transposed=False):
        turn = jnp.transpose if transposed else (lambda a: a)
        return tuple(turn(o)[None] for o in _chip_sum_adamw(mine, ici, turn(w[0]), turn(m[0]), turn(v[0]), name))

    d_ada, nm_ada, nv_ada = _adamw(w_ada[0], g_w_ada, m_w_ada[0], v_w_ada[0], "adamw_w_ada")
    res = {
        "w_ada": (g_w_ada[None], d_ada[None], nm_ada[None], nv_ada[None]),
        "w_in": reduced(mine_in, ici_in, w_in, m_w_in, v_w_in, "adamw_w_in", transposed=True),
        "w_out": reduced(mine_out, ici_out, w_out, m_w_out, v_w_out, "adamw_w_out"),
        "w_ffn_in": reduced(mine_fi, ici_fi, w_ffn_in, m_w_ffn_in, v_w_ffn_in, "adamw_w_ffn_in", transposed=True),
        "w_ffn_out": reduced(mine_fo, ici_fo, w_ffn_out, m_w_ffn_out, v_w_ffn_out, "adamw_w_ffn_out"),
    }
    res.update(small)
    order = ["w_ada", "b_ada", "g_mix", "w_in", "b_in", "sinks", "conv_w", "w_out", "g_ffn", "w_ffn_in", "w_ffn_out",
             "g_final"]
    outs = [loss.reshape(()), grad_x[None]]
    for k in range(4):
        outs += [res[n][k] for n in order]
    return tuple(outs)
```

```python
import functools
import math

import jax
import jax.numpy as jnp
from jax import lax
from jax.experimental import pallas as pl
from jax.experimental.pallas import tpu as pltpu

F32 = jnp.float32
BF16 = jnp.bfloat16
GRAD_STREAM = F32

D_MODEL = 1024
HEAD_DIM = 64
N_Q_HEADS = 16
N_KV_HEADS = 2
GROUP = 8
WINDOW = 128
KV_WIDTH = N_KV_HEADS * HEAD_DIM
D_FF = 2816
IN_WIDTH = 6400
N_MOD = 6
EPS = 1e-6
N_DEV = 8
REST_WIDTH = 5 * D_MODEL
KV_COL = D_MODEL + REST_WIDTH
ATTN_SCALE = HEAD_DIM ** -0.5

ADAM_LR = 0.001
ADAM_B1 = 0.9
ADAM_B2 = 0.999
ADAM_EPS = 1e-08
ADAM_WD = 0.01
ADAM_STEP = 10

LANES = 128
SUBLANES = 8
BF16_ROWS = 16
VMEM_LIMIT = 56 * 1024 * 1024
TOKEN_TILE = 512
FF_CHUNK = 256
ROW_PARTS = 2
WGRAD_VMEM = 40 * 1024 * 1024
MESH = pl.DeviceIdType.MESH
ANY = pl.BlockSpec(memory_space=pl.ANY)

NT_DIMS = (((1,), (1,)), ((), ()))
TN_DIMS = (((0,), (0,)), ((), ()))
CHIP_FLIPS = [(0, 0), (1, 0), (0, 1), (1, 1)]


def _full(shape):
    return pl.BlockSpec(shape, lambda *_: (0,) * len(shape))


def _my_place():
    return lax.axis_index("x"), lax.axis_index("y"), lax.axis_index("c")


def _flip(v, bit):
    return 1 - v if bit else v


def _sigmoid(v):
    return 1.0 / (1.0 + jnp.exp2(v * (-1.4426950408889634)))


class _Rider:
    def __init__(self, ins, out_shapes, sem_shapes, first=None, mid=None, last=None, ins_in_vmem=False):
        self.ins, self.out_shapes, self.sem_shapes = list(ins), list(out_shapes), list(sem_shapes)
        self.in_specs = [_full(a.shape) if ins_in_vmem else ANY for a in self.ins]
        self.hooks = [(when, fn) for when, fn in (("first", first), ("mid", mid), ("last", last)) if fn is not None]


def _call(body, name, grid, args, in_specs, out_shape, out_specs, scratch=(), rider=None, aliases=None):
    n_in, n_out, n_scr = len(args), len(out_shape), len(scratch)
    r_in = rider.ins if rider else []
    r_out = rider.out_shapes if rider else []
    r_sem = rider.sem_shapes if rider else []
    nsteps = math.prod(grid)

    def full_body(*refs):
        pos = 0
        groups = []
        for size in (n_in, len(r_in), n_out, len(r_out), n_scr, len(r_sem)):
            groups.append(refs[pos:pos + size])
            pos += size
        ins, rins, outs, routs, scr, rsems = groups
        step = pl.program_id(0)
        for axis in range(1, len(grid)):
            step = step * grid[axis] + pl.program_id(axis)
        at = {"first": 0, "mid": (3 * nsteps) // 4, "last": nsteps - 1}
        hooks = rider.hooks if rider else []
        for when, fn in hooks:
            if when != "last":
                pl.when(step == at[when])(functools.partial(fn, rins, routs, rsems))
        body(*ins, *outs, *scr)
        for when, fn in hooks:
            if when == "last":
                pl.when(step == at[when])(functools.partial(fn, rins, routs, rsems))

    outs = pl.pallas_call(
        full_body, name=name, grid=grid,
        out_shape=list(out_shape) + list(r_out),
        in_specs=list(in_specs) + (rider.in_specs if rider else []),
        out_specs=list(out_specs) + [ANY] * len(r_out),
        scratch_shapes=list(scratch) + list(r_sem),
        input_output_aliases=dict(aliases or {}),
        compiler_params=pltpu.CompilerParams(dimension_semantics=("arbitrary",) * len(grid),
                                             vmem_limit_bytes=VMEM_LIMIT),
    )(*args, *r_in)
    return list(outs[:n_out]), list(outs[n_out:])


def _gather_rider(shards):
    n = len(shards)

    def setup(outs, sems):
        x, y, c = _my_place()
        send_sems, recv_sems, _ = sems
        chips = [(1 - x, y), (x, 1 - y), (1 - x, 1 - y)]

        def block(w, place):
            return outs[w].at[4 * place[0] + 2 * place[1] + place[2]]

        def copy(w, k, place, to, src=None):
            return pltpu.make_async_remote_copy(
                src_ref=block(w, place) if src is None else src, dst_ref=block(w, place),
                send_sem=send_sems.at[w, k], recv_sem=recv_sems.at[w, k], device_id=to, device_id_type=MESH)

        return (x, y, c), (x, y, 1 - c), chips, block, copy

    def first(ins, outs, sems):
        me, sibling, chips, block, copy = setup(outs, sems)
        for w in range(n):
            pltpu.make_async_copy(ins[w], block(w, me), sems[2].at[w]).start()
            copy(w, 0, me, sibling, src=ins[w]).start()
            for j, chip in enumerate(chips):
                copy(w, 1 + j, me, (*chip, me[2]), src=ins[w]).start()

    def mid(ins, outs, sems):
        me, sibling, chips, block, copy = setup(outs, sems)
        for w in range(n):
            for j, chip in enumerate(chips):
                copy(w, 1 + j, (*chip, me[2]), me).wait_recv()
                copy(w, 4 + j, (*chip, me[2]), sibling).start()

    def last(ins, outs, sems):
        me, sibling, chips, block, copy = setup(outs, sems)
        for w in range(n):
            copy(w, 0, sibling, me).wait_recv()
            for j, chip in enumerate(chips):
                copy(w, 4 + j, (*chip, 1 - me[2]), me).wait_recv()
            copy(w, 0, me, sibling, src=ins[w]).wait_send()
            for j, chip in enumerate(chips):
                copy(w, 1 + j, me, (*chip, me[2]), src=ins[w]).wait_send()
                copy(w, 4 + j, (*chip, me[2]), sibling).wait_send()
            pltpu.make_async_copy(ins[w], block(w, me), sems[2].at[w]).wait()

    return _Rider(
        shards, [jax.ShapeDtypeStruct((N_DEV,) + s.shape, BF16) for s in shards],
        [pltpu.SemaphoreType.DMA((n, N_DEV - 1)), pltpu.SemaphoreType.DMA((n, N_DEV - 1)),
         pltpu.SemaphoreType.DMA((n,))],
        first=first, mid=mid, last=last, ins_in_vmem=True)


def _sibling_rider(gblocks):
    n = len(gblocks)

    def copies(ins, outs, sems):
        x, y, c = _my_place()
        send_sems, recv_sems = sems
        made = []
        for w in range(n):
            for f, (fx, fy) in enumerate(CHIP_FLIPS):
                chip = 4 * _flip(x, fx) + 2 * _flip(y, fy)
                made.append(pltpu.make_async_remote_copy(
                    src_ref=ins[w].at[chip + 1 - c], dst_ref=outs[w].at[f], send_sem=send_sems.at[w, f],
                    recv_sem=recv_sems.at[w, f], device_id=(x, y, 1 - c), device_id_type=MESH))
        return made

    def first(ins, outs, sems):
        for cp in copies(ins, outs, sems):
            cp.start()

    def last(ins, outs, sems):
        for cp in copies(ins, outs, sems):
            cp.wait_recv()
            cp.wait_send()

    return _Rider(gblocks, [jax.ShapeDtypeStruct((4,) + g.shape[1:], BF16) for g in gblocks],
                  [pltpu.SemaphoreType.DMA((n, 4))] * 2, first=first, last=last)


def _chip_rider(sums):
    n = len(sums)

    def copies(ins, outs, sems):
        x, y, c = _my_place()
        send_sems, recv_sems = sems
        made = []
        for w in range(n):
            for f in (1, 2, 3):
                fx, fy = CHIP_FLIPS[f]
                made.append(pltpu.make_async_remote_copy(
                    src_ref=ins[w].at[f - 1], dst_ref=outs[w].at[f - 1], send_sem=send_sems.at[w, f - 1],
                    recv_sem=recv_sems.at[w, f - 1], device_id=(_flip(x, fx), _flip(y, fy), c), device_id_type=MESH))
        return made

    def first(ins, outs, sems):
        for cp in copies(ins, outs, sems):
            cp.start()

    def last(ins, outs, sems):
        for cp in copies(ins, outs, sems):
            cp.wait_recv()
            cp.wait_send()

    return _Rider(sums, [jax.ShapeDtypeStruct(s.shape, BF16) for s in sums],
                  [pltpu.SemaphoreType.DMA((n, 3))] * 2, first=first, last=last)


def _push_to_all(v_ref, out_ref, send_sems, recv_sems, local_sem, wait=True):
    x, y, c = _my_place()
    me = 4 * x + 2 * y + c
    mine = pltpu.make_async_copy(v_ref, out_ref.at[me], local_sem)
    mine.start()
    sends = []
    for k in range(1, N_DEV):
        px, py, pc = _flip(x, k & 4), _flip(y, k & 2), _flip(c, k & 1)
        cp = pltpu.make_async_remote_copy(
            src_ref=v_ref, dst_ref=out_ref.at[me], send_sem=send_sems.at[k - 1], recv_sem=recv_sems.at[k - 1],
            device_id=(px, py, pc), device_id_type=MESH)
        cp.start()
        sends.append(cp)

    def finish():
        for k in range(1, N_DEV):
            px, py, pc = _flip(x, k & 4), _flip(y, k & 2), _flip(c, k & 1)
            pltpu.make_async_remote_copy(
                src_ref=v_ref, dst_ref=out_ref.at[4 * px + 2 * py + pc], send_sem=send_sems.at[k - 1],
                recv_sem=recv_sems.at[k - 1], device_id=(px, py, pc), device_id_type=MESH).wait_recv()
        for cp in sends:
            cp.wait_send()
        mine.wait()

    if wait:
        finish()
    return finish


def _small_allgather(v, name):
    def body(v_ref, out_ref, send_sems, recv_sems, local_sem):
        _push_to_all(v_ref, out_ref, send_sems, recv_sems, local_sem)

    return pl.pallas_call(
        body, name=name,
        out_shape=jax.ShapeDtypeStruct((N_DEV,) + v.shape, F32),
        in_specs=[pl.BlockSpec(memory_space=pltpu.VMEM)],
        out_specs=pl.BlockSpec(memory_space=pltpu.VMEM),
        scratch_shapes=[pltpu.SemaphoreType.DMA((N_DEV - 1,)), pltpu.SemaphoreType.DMA((N_DEV - 1,)),
                        pltpu.SemaphoreType.DMA],
        compiler_params=pltpu.CompilerParams(vmem_limit_bytes=VMEM_LIMIT),
    )(v)


def _gather_first_weight(shard, others, cond_rows, w_ada, b_cols):
    n = len(others)
    ada_cols = w_ada.shape[1]
    c_rows = D_MODEL // LANES

    def body(*refs):
        w_ref, other_refs = refs[0], refs[1:1 + n]
        cond_ref, wada_ref, bcols_ref = refs[1 + n:4 + n]
        out_ref, cast_refs = refs[4 + n], refs[5 + n:5 + 2 * n]
        cond_all_ref, mod_all_ref = refs[5 + 2 * n:7 + 2 * n]
        mine_ref, mod_ref, send_sems, recv_sems, local_sem, small_send, small_recv, small_local = refs[7 + 2 * n:]
        x, y, c = _my_place()
        me, sibling = (x, y, c), (x, y, 1 - c)
        xnb, ynb, diag = (1 - x, y), (x, 1 - y), (1 - x, 1 - y)
        half = shard.shape[0] // 2

        def block(place, part=None):
            ref = out_ref.at[4 * place[0] + 2 * place[1] + place[2]]
            return ref if part is None else ref.at[pl.ds(part * half, half)]

        def copy(k, place, to, part=None, src=None):
            return pltpu.make_async_remote_copy(
                src_ref=block(place, part) if src is None else src, dst_ref=block(place, part),
                send_sem=send_sems.at[k], recv_sem=recv_sems.at[k], device_id=to, device_id_type=MESH)

        finish_cond = _push_to_all(cond_ref, cond_all_ref, small_send.at[0], small_recv.at[0], small_local.at[0],
                                   wait=False)
        mine_ref[...] = w_ref[...].astype(BF16)
        local = pltpu.make_async_copy(mine_ref, block(me), local_sem)
        local.start()
        started = [copy(0, me, sibling, src=mine_ref), copy(1, me, (*xnb, c), src=mine_ref),
                   copy(2, me, (*ynb, c), src=mine_ref)]
        for cp in started:
            cp.start()
        finish_cond()
        mod = jnp.zeros((N_DEV, ada_cols), F32) + bcols_ref[...]
        for r in range(c_rows):
            cf = cond_all_ref[:, r, :]
            act = (cf * _sigmoid(cf)).astype(BF16)
            mod = mod + jnp.dot(act, wada_ref[r * LANES:(r + 1) * LANES, :].astype(BF16),
                                preferred_element_type=F32)
        mod_ref[...] = mod
        finish_mod = _push_to_all(mod_ref, mod_all_ref, small_send.at[1], small_recv.at[1], small_local.at[1],
                                  wait=False)
        for o_ref, c_ref in zip(other_refs, cast_refs):
            c_ref[...] = o_ref[...].astype(BF16)
        def start(cp):
            cp.start()
            started.append(cp)

        copy(1, (*xnb, c), me).wait_recv()
        start(copy(3, (*xnb, c), (*ynb, c), part=0))
        start(copy(5, (*xnb, c), sibling))
        copy(2, (*ynb, c), me).wait_recv()
        start(copy(4, (*ynb, c), (*xnb, c), part=1))
        start(copy(6, (*ynb, c), sibling))
        copy(3, (*diag, c), me, part=0).wait_recv()
        start(copy(7, (*diag, c), sibling, part=0))
        copy(4, (*diag, c), me, part=1).wait_recv()
        start(copy(8, (*diag, c), sibling, part=1))
        copy(0, sibling, me).wait_recv()
        copy(5, (*xnb, 1 - c), me).wait_recv()
        copy(6, (*ynb, 1 - c), me).wait_recv()
        copy(7, (*diag, 1 - c), me, part=0).wait_recv()
        copy(8, (*diag, 1 - c), me, part=1).wait_recv()
        finish_mod()
        for cp in started:
            cp.wait_send()
        local.wait()

    vmem = pl.BlockSpec(memory_space=pltpu.VMEM)
    outs = pl.pallas_call(
        body, name="gather_w_in",
        out_shape=[jax.ShapeDtypeStruct((N_DEV,) + shard.shape, BF16)]
        + [jax.ShapeDtypeStruct(o.shape, BF16) for o in others]
        + [jax.ShapeDtypeStruct((N_DEV,) + cond_rows.shape, F32), jax.ShapeDtypeStruct((N_DEV, N_DEV, ada_cols), F32)],
        in_specs=[vmem] * (4 + n),
        out_specs=[ANY] + [vmem] * (n + 2),
        scratch_shapes=[pltpu.VMEM(shard.shape, BF16), pltpu.VMEM((N_DEV, ada_cols), F32),
                        pltpu.SemaphoreType.DMA((9,)), pltpu.SemaphoreType.DMA((9,)),
                        pltpu.SemaphoreType.DMA,
                        pltpu.SemaphoreType.DMA((2, N_DEV - 1)), pltpu.SemaphoreType.DMA((2, N_DEV - 1)),
                        pltpu.SemaphoreType.DMA((2,))],
        compiler_params=pltpu.CompilerParams(vmem_limit_bytes=VMEM_LIMIT),
    )(shard, *others, cond_rows, w_ada, b_cols)
    return outs[0], list(outs[1:1 + n]), outs[1 + n], outs[2 + n]


def _sibling_exchange_sum(gblocks, name):
    _, r, cdim = gblocks.shape

    def body(g_ref, sums_ref, mine_ref, own_buf, sib_buf, own_sems, send_sems, recv_sems):
        x, y, c = _my_place()
        pairs = []
        for f, (fx, fy) in enumerate(CHIP_FLIPS):
            chip = 4 * _flip(x, fx) + 2 * _flip(y, fy)
            own = pltpu.make_async_copy(g_ref.at[chip + c], own_buf.at[f], own_sems.at[f])
            own.start()
            remote = pltpu.make_async_remote_copy(
                src_ref=g_ref.at[chip + 1 - c], dst_ref=sib_buf.at[f], send_sem=send_sems.at[f],
                recv_sem=recv_sems.at[f], device_id=(x, y, 1 - c), device_id_type=MESH)
            remote.start()
            pairs.append((own, remote))
        for f in (1, 2, 3, 0):
            own, remote = pairs[f]
            own.wait()
            remote.wait_recv()
            total = own_buf[f].astype(F32) + sib_buf[f].astype(F32)
            if f == 0:
                mine_ref[...] = total
            else:
                sums_ref[f - 1] = total.astype(BF16)
        for _, remote in pairs:
            remote.wait_send()

    vmem = pl.BlockSpec(memory_space=pltpu.VMEM)
    return pl.pallas_call(
        body, name=name,
        out_shape=[jax.ShapeDtypeStruct((3, r, cdim), BF16), jax.ShapeDtypeStruct((r, cdim), F32)],
        in_specs=[ANY], out_specs=[vmem, vmem],
        scratch_shapes=[pltpu.VMEM((4, r, cdim), BF16), pltpu.VMEM((4, r, cdim), BF16),
                        pltpu.SemaphoreType.DMA((4,)), pltpu.SemaphoreType.DMA((4,)), pltpu.SemaphoreType.DMA((4,))],
        compiler_params=pltpu.CompilerParams(vmem_limit_bytes=VMEM_LIMIT),
    )(gblocks)


def _ada_weight_grad(c_all, dmod_cols):
    cols = dmod_cols.shape[1]

    def body(c_ref, d_ref, out_ref):
        cf = c_ref[...]
        act = (cf * _sigmoid(cf)).astype(BF16)
        out_ref[...] = lax.dot_general(act, d_ref[...].astype(BF16), TN_DIMS, preferred_element_type=F32)

    return pl.pallas_call(
        body, name="ada_weight_grad",
        out_shape=jax.ShapeDtypeStruct((D_MODEL, cols), F32),
        in_specs=[pl.BlockSpec(memory_space=pltpu.VMEM)] * 2,
        out_specs=pl.BlockSpec(memory_space=pltpu.VMEM),
        compiler_params=pltpu.CompilerParams(vmem_limit_bytes=VMEM_LIMIT),
    )(c_all, dmod_cols)


PACK_ROWS = 24
PACK_DMOD = 0
PACK_PARAMS = {"g_mix": (6, D_MODEL), "b_in": (7, IN_WIDTH), "g_ffn": (14, D_MODEL), "g_final": (15, D_MODEL),
               "sinks": (19, N_Q_HEADS)}
PACK_CONV = 16
PACK_SQERR = 20


def _small_finalize(packed_all, params):
    names = ["b_ada"] + list(PACK_PARAMS)
    layout = dict(PACK_PARAMS, b_ada=(PACK_DMOD, N_MOD * D_MODEL))
    n = len(names)

    def body(*refs):
        p_ref = refs[0]
        ins = refs[1:1 + 3 * n]
        outs = refs[1 + 3 * n:1 + 7 * n]
        conv_ref, loss_ref = refs[1 + 7 * n:]
        total = p_ref[0]
        for d in range(1, N_DEV):
            total = total + p_ref[d]
        for k, name in enumerate(names):
            row0, width = layout[name]
            w_ref, m_ref, v_ref = ins[3 * k:3 * k + 3]
            g_ref, d_ref, nm_ref, nv_ref = outs[4 * k:4 * k + 4]
            for chunk in range(-(-width // D_MODEL)):
                lo = chunk * D_MODEL
                hi = min(lo + D_MODEL, width)
                g = total[row0 + chunk:row0 + chunk + 1, :hi - lo]
                g_ref[:, lo:hi] = g
                d_ref[:, lo:hi], nm_ref[:, lo:hi], nv_ref[:, lo:hi] = _adamw_update(
                    w_ref[:, lo:hi], g, m_ref[:, lo:hi], v_ref[:, lo:hi])
        conv_ref[...] = total[PACK_CONV:PACK_CONV + 3, :]
        loss_ref[...] = (0.5 / D_MODEL) * jnp.sum(total[PACK_SQERR:PACK_SQERR + 1, :], keepdims=True)

    vmem = pl.BlockSpec(memory_space=pltpu.VMEM)
    flat = [a for name in names for a in params[name]]
    out_shape = [jax.ShapeDtypeStruct(params[name][0].shape, F32) for name in names for _ in range(4)]
    outs = pl.pallas_call(
        body, name="small_finalize",
        out_shape=out_shape + [jax.ShapeDtypeStruct((3, D_MODEL), F32), jax.ShapeDtypeStruct((1, 1), F32)],
        in_specs=[vmem] * (1 + 3 * n),
        out_specs=[vmem] * (4 * n + 2),
        compiler_params=pltpu.CompilerParams(vmem_limit_bytes=VMEM_LIMIT),
    )(packed_all, *flat)
    return {name: tuple(outs[4 * k:4 * k + 4]) for k, name in enumerate(names)}, outs[4 * n], outs[4 * n + 1]


def _row_tile(rows, multiple):
    for cand in range(min(rows, 256), 0, -1):
        if rows % cand == 0 and cand % multiple == 0:
            return cand
    return rows


def _adamw_update(w, g, m, v):
    c1 = 1.0 / (1.0 - ADAM_B1 ** ADAM_STEP)
    c2 = 1.0 / (1.0 - ADAM_B2 ** ADAM_STEP)
    nm = ADAM_B1 * m + (1.0 - ADAM_B1) * g
    nv = ADAM_B2 * v + (1.0 - ADAM_B2) * (g * g)
    delta = -ADAM_LR * ((nm * c1) / (jnp.sqrt(nv * c2) + ADAM_EPS) + ADAM_WD * w)
    return delta, nm, nv


def _adamw(w, g, m, v, name):
    rows, cols = w.shape
    tile = _row_tile(rows, SUBLANES)

    def body(w_ref, g_ref, m_ref, v_ref, d_ref, nm_ref, nv_ref):
        d_ref[...], nm_ref[...], nv_ref[...] = _adamw_update(w_ref[...], g_ref[...], m_ref[...], v_ref[...])

    spec = pl.BlockSpec((tile, cols), lambda i: (i, 0))
    outs, _ = _call(body, name, (rows // tile,), [w, g, m, v], [spec] * 4,
                    [jax.ShapeDtypeStruct((rows, cols), F32)] * 3, [spec] * 3)
    return outs


def _sibling_sum(gblocks, sib, name):
    _, r, cdim = gblocks.shape
    tile = _row_tile(r, BF16_ROWS)
    x, y, c = _my_place()
    table = jnp.stack([4 * _flip(x, fx) + 2 * _flip(y, fy) + c for fx, fy in CHIP_FLIPS]).astype(jnp.int32)

    def body(table_ref, own0, own1, own2, own3, sib_ref, sums_ref, mine_ref):
        mine_ref[...] = own0[...].astype(F32) + sib_ref[0].astype(F32)
        for f, own in ((1, own1), (2, own2), (3, own3)):
            sums_ref[f - 1] = (own[...].astype(F32) + sib_ref[f].astype(F32)).astype(BF16)

    own_specs = [pl.BlockSpec((None, tile, cdim), functools.partial(lambda i, tab, f: (tab[f], i, 0), f=f))
                 for f in range(4)]
    return pl.pallas_call(
        body, name=name,
        grid_spec=pltpu.PrefetchScalarGridSpec(
            num_scalar_prefetch=1, grid=(r // tile,),
            in_specs=own_specs + [pl.BlockSpec((4, tile, cdim), lambda i, tab: (0, i, 0))],
            out_specs=[pl.BlockSpec((3, tile, cdim), lambda i, tab: (0, i, 0)),
                       pl.BlockSpec((tile, cdim), lambda i, tab: (i, 0))]),
        out_shape=[jax.ShapeDtypeStruct((3, r, cdim), BF16), jax.ShapeDtypeStruct((r, cdim), F32)],
        compiler_params=pltpu.CompilerParams(dimension_semantics=("arbitrary",), vmem_limit_bytes=VMEM_LIMIT),
    )(table, gblocks, gblocks, gblocks, gblocks, sib)


def _chip_sum_adamw(mine, ici, w, m, v, name):
    r, cdim = mine.shape
    tile = _row_tile(r, BF16_ROWS)

    def body(mine_ref, ici_ref, w_ref, m_ref, v_ref, g_ref, d_ref, nm_ref, nv_ref):
        g = mine_ref[...]
        for f in range(3):
            g = g + ici_ref[f].astype(F32)
        g_ref[...] = g
        d_ref[...], nm_ref[...], nv_ref[...] = _adamw_update(w_ref[...], g, m_ref[...], v_ref[...])

    spec = pl.BlockSpec((tile, cdim), lambda i: (i, 0))
    outs, _ = _call(
        body, name, (r // tile,), [mine, ici, w, m, v],
        [spec, pl.BlockSpec((3, tile, cdim), lambda i: (0, i, 0)), spec, spec, spec],
        [jax.ShapeDtypeStruct((r, cdim), F32)] * 4, [spec] * 4)
    return outs


REF_KV_COL = D_MODEL
REF_REST_COL = D_MODEL + 2 * KV_WIDTH
IN_CHUNK = 1280
IN_PIECES = ([(0, 0, D_MODEL)]
             + [(D_MODEL + n * IN_CHUNK, REF_REST_COL + n * IN_CHUNK, IN_CHUNK) for n in range(REST_WIDTH // IN_CHUNK)]
             + [(KV_COL, REF_KV_COL, 2 * KV_WIDTH)])


def _inproj_fwd(x, vec, w_t, b_in, rider):
    t = x.shape[0]
    tm = min(TOKEN_TILE, t)

    def body(x_ref, vec_ref, w_ref, b_ref, z_ref, h_ref):
        xf = x_ref[...]
        r = lax.rsqrt(jnp.mean(xf * xf, axis=-1, keepdims=True) + EPS)
        h = (xf * r) * (vec_ref[0:1, :] * (1.0 + vec_ref[1:2, :])) + vec_ref[2:3, :]
        hb = h.astype(BF16)
        h_ref[...] = hb
        for mine, ref, width in IN_PIECES:
            zc = lax.dot_general(hb, w_ref[ref:ref + width, :], NT_DIMS, preferred_element_type=F32)
            z_ref[:, mine:mine + width] = (zc + b_ref[:, ref:ref + width]).astype(BF16)

    return _call(
        body, "inproj_fwd", (t // tm,), [x, vec, w_t, b_in],
        [pl.BlockSpec((tm, D_MODEL), lambda i: (i, 0)), _full((SUBLANES, D_MODEL)),
         _full((IN_WIDTH, D_MODEL)), _full((1, IN_WIDTH))],
        [jax.ShapeDtypeStruct((t, IN_WIDTH), BF16), jax.ShapeDtypeStruct((t, D_MODEL), BF16)],
        [pl.BlockSpec((tm, IN_WIDTH), lambda i: (i, 0)), pl.BlockSpec((tm, D_MODEL), lambda i: (i, 0))],
        rider=rider)


PAIRS = GROUP // 2
STACK = PAIRS * WINDOW


ATTN_BLOCKS = 4
ATTN_BWD_BLOCKS = 1
LOG2E = 1.4426950408889634
LN2 = 0.6931471805599453
SCORE_SCALE = ATTN_SCALE * LOG2E


def _fill_window_bias(bias_ref):
    shape = bias_ref.shape[1:]
    kj = lax.broadcasted_iota(jnp.int32, shape, 0)
    qi = jnp.bitwise_and(lax.broadcasted_iota(jnp.int32, shape, 1), WINDOW - 1)
    in_prev = jnp.logical_and(kj < WINDOW, kj > qi)
    in_cur = jnp.logical_and(kj >= WINDOW, (kj - WINDOW) <= qi)
    bias_ref[0] = jnp.where(in_cur, 0.0, -jnp.inf)
    bias_ref[1] = jnp.where(jnp.logical_or(in_prev, in_cur), 0.0, -jnp.inf)


def _half_tiles(tile):
    low = lax.broadcasted_iota(jnp.int32, tile.shape, 1) < HEAD_DIM
    swapped = jnp.concatenate([tile[:, HEAD_DIM:], tile[:, :HEAD_DIM]], axis=1)
    zero = jnp.zeros_like(tile)
    return ((jnp.where(low, tile, zero), jnp.where(low, zero, swapped)),
            (jnp.where(low, swapped, zero), jnp.where(low, zero, tile)))


def _stack_pairs(ref, row0, j):
    return jnp.concatenate(
        [ref[pl.ds(row0, WINDOW), (j * PAIRS + p) * LANES:(j * PAIRS + p + 1) * LANES] for p in range(PAIRS)], axis=0)


def _per_pair_row(values):
    pair = lax.broadcasted_iota(jnp.int32, (1, STACK), 1) // WINDOW
    row = jnp.full((1, STACK), values[PAIRS - 1], F32)
    for p in range(PAIRS - 2, -1, -1):
        row = jnp.where(pair == p, values[p], row)
    return row


def _attn_fwd(z, sinks, rider):
    t = z.shape[0]
    tq = min(TOKEN_TILE, t)
    nblk = tq // WINDOW

    def body(q_ref, kv_ref, sink_ref, o_ref, lse_ref, bias_ref):
        i = pl.program_id(0)

        @pl.when(i == 0)
        def _():
            _fill_window_bias(bias_ref)

        def window(b):
            row0 = pl.multiple_of(b * WINDOW, WINDOW)
            start = i * tq + b * WINDOW
            prev = pl.multiple_of(jnp.maximum(start - WINDOW, 0), WINDOW)
            cur = pl.multiple_of(start, WINDOW)
            kvw = jnp.concatenate([kv_ref[pl.ds(prev, WINDOW), :], kv_ref[pl.ds(cur, WINDOW), :]], axis=0)
            return row0, _half_tiles(kvw[:, :KV_WIDTH]), _half_tiles(kvw[:, KV_WIDTH:]), bias_ref[jnp.minimum(start, 1)]

        def block_group(bb, carry):
            windows = [window(bb * ATTN_BLOCKS + n) for n in range(ATTN_BLOCKS)]
            for j in range(N_KV_HEADS):
                for pr in range(PAIRS):
                    cols = slice((j * PAIRS + pr) * LANES, (j * PAIRS + pr + 1) * LANES)
                    o_ts = [jnp.zeros((LANES, WINDOW), F32) for _ in windows]
                    for parity in range(2):
                        h = j * GROUP + 2 * pr + parity
                        sink = sink_ref[h] * LOG2E
                        for n, (row0, k_halves, v_halves, bias) in enumerate(windows):
                            qp = q_ref[pl.ds(row0, WINDOW), cols]
                            s = lax.dot_general(k_halves[j][parity], qp, NT_DIMS, preferred_element_type=F32)
                            s = s * SCORE_SCALE + bias
                            m = jnp.maximum(jnp.max(s, axis=0, keepdims=True), sink)
                            p = jnp.exp2(s - m)
                            denom = jnp.sum(p, axis=0, keepdims=True) + jnp.exp2(sink - m)
                            pv = lax.dot_general(v_halves[j][parity], p.astype(BF16), TN_DIMS,
                                                 preferred_element_type=F32)
                            o_ts[n] = o_ts[n] + pv * (1.0 / denom)
                            lse_ref[h:h + 1, pl.ds(row0, WINDOW)] = m + jnp.log2(denom)
                    for n, (row0, _, _, _) in enumerate(windows):
                        o_ref[pl.ds(row0, WINDOW), cols] = jnp.transpose(o_ts[n].astype(BF16))
            return carry

        lax.fori_loop(0, nblk // ATTN_BLOCKS, block_group, 0)

    return _call(
        body, "attn_fwd", (t // tq,), [z, z, sinks],
        [pl.BlockSpec((tq, D_MODEL), lambda i: (i, 0)),
         pl.BlockSpec((t, 2 * KV_WIDTH), lambda i: (0, KV_COL // (2 * KV_WIDTH))),
         pl.BlockSpec(memory_space=pltpu.SMEM)],
        [jax.ShapeDtypeStruct((t, D_MODEL), BF16), jax.ShapeDtypeStruct((N_Q_HEADS, t), F32)],
        [pl.BlockSpec((tq, D_MODEL), lambda i: (i, 0)), pl.BlockSpec((N_Q_HEADS, tq), lambda i: (0, i))],
        scratch=[pltpu.VMEM((2, 2 * WINDOW, WINDOW), F32)], rider=rider)


HALO = BF16_ROWS


def _shift_down(u, uh, k):
    rolled = pltpu.roll(u, k, 0)
    row = lax.broadcasted_iota(jnp.int32, (SUBLANES, u.shape[1]), 0)
    top = rolled[:SUBLANES, :]
    for j in range(k):
        top = jnp.where(row == j, uh[HALO - k + j:HALO - k + j + 1, :], top)
    return jnp.concatenate([top, rolled[SUBLANES:, :]], axis=0)


def _shift_up(u, nxt, k):
    n = u.shape[0]
    rolled = pltpu.roll(u, n - k, 0)
    row = lax.broadcasted_iota(jnp.int32, (SUBLANES, u.shape[1]), 0)
    bottom = rolled[n - SUBLANES:, :]
    for j in range(k):
        bottom = jnp.where(row == SUBLANES - k + j, nxt[j:j + 1, :], bottom)
    return jnp.concatenate([rolled[:n - SUBLANES, :], bottom], axis=0)


def _conv_inputs(cc_ref, cx_ref, hc_ref, hx_ref, first_tile):
    cc = cc_ref[...].astype(F32)
    cx = cx_ref[...].astype(F32)
    u = cc * cx
    uh = jnp.where(first_tile, 0.0, hc_ref[...].astype(F32) * hx_ref[...].astype(F32))
    return cc, cx, u, _shift_down(u, uh, 1), _shift_down(u, uh, 2)


def _z_specs(tm, order):
    per_tile = tm // HALO
    cols = [pl.BlockSpec((tm, D_MODEL), functools.partial(lambda i, j: (order(i), j), j=j)) for j in range(1, 6)]
    halos = [pl.BlockSpec((HALO, D_MODEL),
                          functools.partial(lambda i, j: (jnp.maximum(order(i) * per_tile - 1, 0), j), j=j))
             for j in (2, 3)]
    return cols + halos


def _mix_fwd(x, attn, z, vec, w_out):
    t = x.shape[0]
    tm = min(TOKEN_TILE, t)

    def body(x_ref, a_ref, cb_ref, cc_ref, cx_ref, ga_ref, gc_ref, hc_ref, hx_ref, vec_ref, w_ref,
             m_ref, x2_ref, h2_ref, o_ref):
        i = pl.program_id(0)
        _, _, u, u1, u2 = _conv_inputs(cc_ref, cx_ref, hc_ref, hx_ref, i == 0)
        cv = vec_ref[4:5, :] * u2 + vec_ref[5:6, :] * u1 + vec_ref[6:7, :] * u
        conv = cb_ref[...].astype(F32) * cv
        merged = (_sigmoid(ga_ref[...].astype(F32)) * a_ref[...].astype(F32)
                  + _sigmoid(gc_ref[...].astype(F32)) * conv)
        mb = merged.astype(BF16)
        m_ref[...] = mb
        o = jnp.dot(mb, w_ref[...], preferred_element_type=F32)
        o_ref[...] = o.astype(BF16)
        x2 = x_ref[...] + vec_ref[0:1, :] * o
        x2_ref[...] = x2
        r = lax.rsqrt(jnp.mean(x2 * x2, axis=-1, keepdims=True) + EPS)
        h2 = (x2 * r) * (vec_ref[1:2, :] * (1.0 + vec_ref[2:3, :])) + vec_ref[3:4, :]
        h2_ref[...] = h2.astype(BF16)

    tok = pl.BlockSpec((tm, D_MODEL), lambda i: (i, 0))
    outs, _ = _call(
        body, "mix_fwd", (t // tm,), [x, attn, z, z, z, z, z, z, z, vec, w_out],
        [tok, tok] + _z_specs(tm, lambda i: i) + [_full((SUBLANES, D_MODEL)), _full((D_MODEL, D_MODEL))],
        [jax.ShapeDtypeStruct((t, D_MODEL), BF16), jax.ShapeDtypeStruct((t, D_MODEL), F32),
         jax.ShapeDtypeStruct((t, D_MODEL), BF16), jax.ShapeDtypeStruct((t, D_MODEL), BF16)],
        [tok, tok, tok, tok])
    return outs


def _ffn_fwd(h2, w_t):
    t = h2.shape[0]
    tm = min(TOKEN_TILE, t)

    def body(h_ref, w_ref, gu_ref, a_ref):
        hb = h_ref[...]
        for n in range(D_FF // FF_CHUNK):
            lo, hi = n * FF_CHUNK, (n + 1) * FF_CHUNK
            g = lax.dot_general(hb, w_ref[lo:hi, :], NT_DIMS, preferred_element_type=F32)
            u = lax.dot_general(hb, w_ref[D_FF + lo:D_FF + hi, :], NT_DIMS, preferred_element_type=F32)
            sg = _sigmoid(g)
            silu = g * sg
            gu_ref[:, lo:hi] = (u * (sg + silu * (1.0 - sg))).astype(BF16)
            gu_ref[:, D_FF + lo:D_FF + hi] = silu.astype(BF16)
            a_ref[:, lo:hi] = (silu * u).astype(BF16)

    outs, _ = _call(
        body, "ffn_fwd", (t // tm,), [h2, w_t],
        [pl.BlockSpec((tm, D_MODEL), lambda i: (i, 0)), _full((2 * D_FF, D_MODEL))],
        [jax.ShapeDtypeStruct((t, 2 * D_FF), BF16), jax.ShapeDtypeStruct((t, D_FF), BF16)],
        [pl.BlockSpec((tm, 2 * D_FF), lambda i: (i, 0)), pl.BlockSpec((tm, D_FF), lambda i: (i, 0))])
    return outs


def _ffn_out_loss(a, gu, x2, target, vec, w_ffn_out):
    t = a.shape[0]
    tm = min(TOKEN_TILE, t)

    def body(a_ref, gu_ref, x2_ref, t_ref, vec_ref, w_ref, dx3_ref, df_ref, dgu_ref, acc_ref):
        @pl.when(pl.program_id(0) == 0)
        def _():
            acc_ref[...] = jnp.zeros_like(acc_ref)

        ga2 = vec_ref[0:1, :]
        gf = vec_ref[1:2, :]
        parts = min(ROW_PARTS, tm // LANES)
        part_rows = [slice(n * (tm // parts), (n + 1) * (tm // parts)) for n in range(parts)]

        def head(rows, f):
            x3 = x2_ref[rows, :] + ga2 * f
            r = lax.rsqrt(jnp.mean(x3 * x3, axis=-1, keepdims=True) + EPS)
            xn = x3 * r
            err = xn * gf - t_ref[rows, :]
            dxn = err * (gf * (1.0 / D_MODEL))
            dx3 = r * (dxn - xn * jnp.mean(dxn * xn, axis=-1, keepdims=True))
            dx3_ref[rows, :] = dx3.astype(GRAD_STREAM)
            sums = (jnp.sum(err * err, axis=0, keepdims=True),
                    jnp.sum(err * xn, axis=0, keepdims=True) * (1.0 / D_MODEL),
                    jnp.sum(dx3 * f, axis=0, keepdims=True))
            df = (dx3 * ga2).astype(BF16)
            df_ref[rows, :] = df
            return df, sums

        def tail(rows, df):
            for n in range(D_FF // FF_CHUNK):
                lo, hi = n * FF_CHUNK, (n + 1) * FF_CHUNK
                da = lax.dot_general(df, w_ref[lo:hi, :], NT_DIMS, preferred_element_type=F32)
                dgu_ref[rows, lo:hi] = (da * gu_ref[rows, lo:hi].astype(F32)).astype(BF16)
                dgu_ref[rows, D_FF + lo:D_FF + hi] = (da * gu_ref[rows, D_FF + lo:D_FF + hi].astype(F32)).astype(BF16)

        fs = [jnp.dot(a_ref[rows, :], w_ref[...], preferred_element_type=F32) for rows in part_rows]
        heads = [head(rows, f) for rows, f in zip(part_rows, fs)]
        for rows, (df, _) in zip(part_rows, heads):
            tail(rows, df)
        for k in range(3):
            total = heads[0][1][k]
            for _, sums in heads[1:]:
                total = total + sums[k]
            acc_ref[k:k + 1, :] += total

    tok = pl.BlockSpec((tm, D_MODEL), lambda i: (i, 0))
    outs, _ = _call(
        body, "ffn_out_loss", (t // tm,), [a, gu, x2, target, vec, w_ffn_out],
        [pl.BlockSpec((tm, D_FF), lambda i: (i, 0)), pl.BlockSpec((tm, 2 * D_FF), lambda i: (i, 0)),
         tok, tok, _full((SUBLANES, D_MODEL)), _full((D_FF, D_MODEL))],
        [jax.ShapeDtypeStruct((t, D_MODEL), GRAD_STREAM), jax.ShapeDtypeStruct((t, D_MODEL), BF16),
         jax.ShapeDtypeStruct((t, 2 * D_FF), BF16), jax.ShapeDtypeStruct((SUBLANES, D_MODEL), F32)],
        [tok, tok, pl.BlockSpec((tm, 2 * D_FF), lambda i: (i, 0)), _full((SUBLANES, D_MODEL))])
    return outs


def _ffn_in_bwd(dgu, x2, dx3, vec, w_t, rider):
    t = x2.shape[0]
    tm = min(TOKEN_TILE, t)

    def body(dgu_ref, x2_ref, dx3_ref, vec_ref, wf_ref, dx2_ref, acc_ref):
        @pl.when(pl.program_id(0) == 0)
        def _():
            acc_ref[...] = jnp.zeros_like(acc_ref)

        gffn = vec_ref[0:1, :]
        sc2 = vec_ref[1:2, :]
        parts = min(ROW_PARTS, tm // LANES)
        part_rows = [slice(n * (tm // parts), (n + 1) * (tm // parts)) for n in range(parts)]
        dhs = [jnp.dot(dgu_ref[rows, :], wf_ref[...], preferred_element_type=F32) for rows in part_rows]
        gs = gffn * (1.0 + sc2)
        sum_dh = jnp.zeros((1, D_MODEL), F32)
        sum_dh_xn = jnp.zeros((1, D_MODEL), F32)
        for rows, dh2 in zip(part_rows, dhs):
            x2 = x2_ref[rows, :]
            r = lax.rsqrt(jnp.mean(x2 * x2, axis=-1, keepdims=True) + EPS)
            xn = x2 * r
            dh_xn = dh2 * xn
            sum_dh = sum_dh + jnp.sum(dh2, axis=0, keepdims=True)
            sum_dh_xn = sum_dh_xn + jnp.sum(dh_xn, axis=0, keepdims=True)
            dx2 = dx3_ref[rows, :].astype(F32) + r * (dh2 * gs - xn * jnp.mean(dh_xn * gs, axis=-1, keepdims=True))
            dx2_ref[rows, :] = dx2.astype(GRAD_STREAM)
        acc_ref[0:1, :] += sum_dh
        acc_ref[1:2, :] += sum_dh_xn * gffn
        acc_ref[2:3, :] += sum_dh_xn * (1.0 + sc2)

    tok = pl.BlockSpec((tm, D_MODEL), lambda i: (i, 0))
    return _call(
        body, "ffn_in_bwd", (t // tm,), [dgu, x2, dx3, vec, w_t],
        [pl.BlockSpec((tm, 2 * D_FF), lambda i: (i, 0)), tok, tok, _full((SUBLANES, D_MODEL)),
         _full((2 * D_FF, D_MODEL))],
        [jax.ShapeDtypeStruct((t, D_MODEL), GRAD_STREAM), jax.ShapeDtypeStruct((SUBLANES, D_MODEL), F32)],
        [tok, _full((SUBLANES, D_MODEL))], rider=rider)


def _mix_bwd(dx2, oproj, attn, z, vec, w_out, rider):
    t = dx2.shape[0]
    tm = min(TOKEN_TILE, t)
    nt = t // tm
    rev = lambda i: nt - 1 - i

    def body(dx2_ref, m_ref, a_ref, cb_ref, cc_ref, cx_ref, ga_ref, gc_ref, hc_ref, hx_ref,
             vec_ref, wo_ref, do_ref, da_ref, dr_ref, acc_ref, carry_ref):
        i = pl.program_id(0)

        @pl.when(i == 0)
        def _():
            acc_ref[...] = jnp.zeros_like(acc_ref)
            carry_ref[...] = jnp.zeros_like(carry_ref)

        ga1 = vec_ref[0:1, :]
        w0, w1, w2 = vec_ref[1:2, :], vec_ref[2:3, :], vec_ref[3:4, :]
        dx2 = dx2_ref[...].astype(F32)
        acc_ref[0:1, :] += jnp.sum(dx2 * m_ref[...].astype(F32), axis=0, keepdims=True)
        do = (dx2 * ga1).astype(BF16)
        do_ref[...] = do
        dm = lax.dot_general(do, wo_ref[...], NT_DIMS, preferred_element_type=F32)

        cc, cx, u, u1, u2 = _conv_inputs(cc_ref, cx_ref, hc_ref, hx_ref, i == nt - 1)
        cv = w0 * u2 + w1 * u1 + w2 * u
        cb = cb_ref[...].astype(F32)
        sa = _sigmoid(ga_ref[...].astype(F32))
        sc = _sigmoid(gc_ref[...].astype(F32))
        attn = a_ref[...].astype(F32)
        dattn = dm * sa
        da_ref[...] = dattn.astype(BF16)
        dconv = dm * sc
        dconv_b = dconv * cv
        dr_ref[:, 3 * D_MODEL:4 * D_MODEL] = (dattn * attn * (1.0 - sa)).astype(BF16)
        dr_ref[:, 4 * D_MODEL:5 * D_MODEL] = (dconv_b * cb * (1.0 - sc)).astype(BF16)
        dr_ref[:, 0:D_MODEL] = dconv_b.astype(BF16)
        dcv = dconv * cb
        acc_ref[1:2, :] += jnp.sum(dcv * u2, axis=0, keepdims=True)
        acc_ref[2:3, :] += jnp.sum(dcv * u1, axis=0, keepdims=True)
        acc_ref[3:4, :] += jnp.sum(dcv * u, axis=0, keepdims=True)
        nxt = carry_ref[...]
        du = w2 * dcv + w1 * _shift_up(dcv, nxt, 1) + w0 * _shift_up(dcv, nxt, 2)
        carry_ref[...] = dcv[0:SUBLANES, :]
        dr_ref[:, D_MODEL:2 * D_MODEL] = (du * cx).astype(BF16)
        dr_ref[:, 2 * D_MODEL:3 * D_MODEL] = (du * cc).astype(BF16)

    tok = pl.BlockSpec((tm, D_MODEL), lambda i: (rev(i), 0))
    return _call(
        body, "mix_bwd", (nt,), [dx2, oproj, attn, z, z, z, z, z, z, z, vec, w_out],
        [tok, tok, tok] + _z_specs(tm, rev) + [_full((SUBLANES, D_MODEL)), _full((D_MODEL, D_MODEL))],
        [jax.ShapeDtypeStruct((t, D_MODEL), BF16), jax.ShapeDtypeStruct((t, D_MODEL), BF16),
         jax.ShapeDtypeStruct((t, REST_WIDTH), BF16), jax.ShapeDtypeStruct((SUBLANES, D_MODEL), F32)],
        [tok, tok, pl.BlockSpec((tm, REST_WIDTH), lambda i: (rev(i), 0)), _full((SUBLANES, D_MODEL))],
        scratch=[pltpu.VMEM((SUBLANES, D_MODEL), F32)], rider=rider)


def _attn_bwd(z, dattn, attn, lse, sinks, rider):
    t = z.shape[0]
    tq = min(TOKEN_TILE, t)
    nblk = tq // WINDOW
    nt = t // tq

    def body(q_ref, kv_ref, do_ref, o_ref, lse_ref, sink_ref, dq_ref, dkv_ref, ds_ref, acc_ref, bias_ref):
        i = pl.program_id(0)

        @pl.when(i == 0)
        def _():
            acc_ref[...] = jnp.zeros_like(acc_ref)
            ds_ref[...] = jnp.zeros_like(ds_ref)
            _fill_window_bias(bias_ref)

        lane = lax.broadcasted_iota(jnp.int32, (1, LANES), 1)
        ind_row = lax.broadcasted_iota(jnp.int32, (SUBLANES, LANES), 0)
        ind_low = lax.broadcasted_iota(jnp.int32, (SUBLANES, LANES), 1) < HEAD_DIM
        indicator = jnp.where(jnp.logical_or(jnp.logical_and(ind_row == 0, ind_low),
                                             jnp.logical_and(ind_row == 1, jnp.logical_not(ind_low))),
                              1.0, 0.0).astype(BF16)
        low = lax.broadcasted_iota(jnp.int32, (2 * WINDOW, LANES), 1) < HEAD_DIM

        def both_heads(even, odd):
            picked = jnp.where(low, even, odd)
            return picked + jnp.concatenate([picked[:, HEAD_DIM:], picked[:, :HEAD_DIM]], axis=1)

        def window(b):
            row0 = pl.multiple_of(b * WINDOW, WINDOW)
            start = i * tq + b * WINDOW
            prev = pl.multiple_of(jnp.maximum(start - WINDOW, 0), WINDOW)
            cur = pl.multiple_of(start, WINDOW)
            kvw = jnp.concatenate([kv_ref[pl.ds(prev, WINDOW), :], kv_ref[pl.ds(cur, WINDOW), :]], axis=0)
            return (row0, prev, cur, _half_tiles(kvw[:, :KV_WIDTH]), _half_tiles(kvw[:, KV_WIDTH:]),
                    bias_ref[jnp.minimum(start, 1)])

        def block_group(bb, dsink):
            windows = [window(bb * ATTN_BWD_BLOCKS + n) for n in range(ATTN_BWD_BLOCKS)]
            dk_groups = [[] for _ in windows]
            dv_groups = [[] for _ in windows]
            for j in range(N_KV_HEADS):
                stacks, deltas, dq_ts = [], [], []
                for row0, _, _, _, _, _ in windows:
                    qst = _stack_pairs(q_ref, row0, j)
                    dost = _stack_pairs(do_ref, row0, j)
                    prod = dost.astype(F32) * _stack_pairs(o_ref, row0, j).astype(F32)
                    prod_hi = prod.astype(BF16)
                    prod_lo = (prod - prod_hi.astype(F32)).astype(BF16)
                    stacks.append((qst, dost))
                    deltas.append(lax.dot_general(indicator, prod_hi, NT_DIMS, preferred_element_type=F32)
                                  + lax.dot_general(indicator, prod_lo, NT_DIMS, preferred_element_type=F32))
                    dq_ts.append(jnp.zeros((LANES, STACK), F32))
                dk_par = [[] for _ in windows]
                dv_par = [[] for _ in windows]
                for parity in range(2):
                    heads = [j * GROUP + 2 * p + parity for p in range(PAIRS)]
                    sink = _per_pair_row([sink_ref[h] * LOG2E for h in heads])
                    for n, (row0, _, _, k_halves, v_halves, bias) in enumerate(windows):
                        qst, dost = stacks[n]
                        kk, vv = k_halves[j][parity], v_halves[j][parity]
                        s = lax.dot_general(kk, qst, NT_DIMS, preferred_element_type=F32) * SCORE_SCALE + bias
                        lse = jnp.concatenate([lse_ref[h:h + 1, pl.ds(row0, WINDOW)] for h in heads], axis=1)
                        p = jnp.exp2(s - lse)
                        dp = lax.dot_general(vv, dost, NT_DIMS, preferred_element_type=F32)
                        delta = deltas[n][parity:parity + 1, :]
                        dsb = (p * (dp - delta)).astype(BF16)
                        dq_ts[n] = dq_ts[n] + lax.dot_general(kk, dsb, TN_DIMS, preferred_element_type=F32)
                        dk_par[n].append(jnp.dot(dsb, qst, preferred_element_type=F32))
                        dv_par[n].append(jnp.dot(p.astype(BF16), dost, preferred_element_type=F32))
                        weighted = jnp.exp2(sink - lse) * delta
                        for pr, h in enumerate(heads):
                            dsink = dsink - jnp.where(
                                lane == h, jnp.sum(weighted[:, pr * WINDOW:(pr + 1) * WINDOW]), 0.0)
                for n, (row0, _, _, _, _, _) in enumerate(windows):
                    dq_st = jnp.transpose((dq_ts[n] * ATTN_SCALE).astype(BF16))
                    for pr in range(PAIRS):
                        dq_ref[pl.ds(row0, WINDOW), (j * PAIRS + pr) * LANES:(j * PAIRS + pr + 1) * LANES] = (
                            dq_st[pr * WINDOW:(pr + 1) * WINDOW, :])
                    dk_groups[n].append(both_heads(dk_par[n][0], dk_par[n][1]))
                    dv_groups[n].append(both_heads(dv_par[n][0], dv_par[n][1]))
            for n, (_, prev, cur, _, _, _) in enumerate(windows):
                blk = jnp.concatenate([jnp.where(low, dk_groups[n][0], dk_groups[n][1]) * ATTN_SCALE,
                                       jnp.where(low, dv_groups[n][0], dv_groups[n][1])], axis=1)
                acc_ref[pl.ds(prev, WINDOW), :] += blk[:WINDOW, :]
                acc_ref[pl.ds(cur, WINDOW), :] += blk[WINDOW:, :]
            return dsink

        dsink = lax.fori_loop(0, nblk // ATTN_BWD_BLOCKS, block_group, jnp.zeros((1, LANES), F32))
        ds_ref[0:1, :] += dsink

        @pl.when(i == nt - 1)
        def _():
            dkv_ref[...] = acc_ref[...].astype(BF16)

    tok = pl.BlockSpec((tq, D_MODEL), lambda i: (i, 0))
    return _call(
        body, "attn_bwd", (nt,), [z, z, dattn, attn, lse, sinks],
        [tok, pl.BlockSpec((t, 2 * KV_WIDTH), lambda i: (0, KV_COL // (2 * KV_WIDTH))), tok, tok,
         pl.BlockSpec((N_Q_HEADS, tq), lambda i: (0, i)), pl.BlockSpec(memory_space=pltpu.SMEM)],
        [jax.ShapeDtypeStruct((t, D_MODEL), BF16), jax.ShapeDtypeStruct((t, 2 * KV_WIDTH), BF16),
         jax.ShapeDtypeStruct((SUBLANES, LANES), F32)],
        [tok, _full((t, 2 * KV_WIDTH)), _full((SUBLANES, LANES))],
        scratch=[pltpu.VMEM((t, 2 * KV_WIDTH), F32), pltpu.VMEM((2, 2 * WINDOW, STACK), F32)], rider=rider)


def _inproj_bwd(dq, drest, dkv, x, dx2, vec, w_t, rider):
    t = x.shape[0]
    tm = min(TOKEN_TILE, t)

    def body(dq_ref, dr_ref, dkv_ref, x_ref, dx2_ref, vec_ref, w_ref, gx_ref, acc_ref, db_ref):
        @pl.when(pl.program_id(0) == 0)
        def _():
            acc_ref[...] = jnp.zeros_like(acc_ref)
            db_ref[...] = jnp.zeros_like(db_ref)

        g = vec_ref[0:1, :]
        sc1 = vec_ref[1:2, :]
        dqb, drb, dkvb = dq_ref[...], dr_ref[...], dkv_ref[...]
        dh = jnp.dot(dqb, w_ref[:REF_KV_COL, :], preferred_element_type=F32)
        dh = dh + jnp.dot(drb, w_ref[REF_REST_COL:, :], preferred_element_type=F32)
        dh = dh + jnp.dot(dkvb, w_ref[REF_KV_COL:REF_REST_COL, :], preferred_element_type=F32)
        db_ref[:, :REF_KV_COL] += jnp.sum(dqb.astype(F32), axis=0, keepdims=True)
        db_ref[:, REF_REST_COL:] += jnp.sum(drb.astype(F32), axis=0, keepdims=True)
        db_ref[:, REF_KV_COL:REF_REST_COL] += jnp.sum(dkvb.astype(F32), axis=0, keepdims=True)
        xf = x_ref[...]
        r = lax.rsqrt(jnp.mean(xf * xf, axis=-1, keepdims=True) + EPS)
        xn = xf * r
        gs = g * (1.0 + sc1)
        dh_xn = dh * xn
        sum_dh_xn = jnp.sum(dh_xn, axis=0, keepdims=True)
        acc_ref[0:1, :] += jnp.sum(dh, axis=0, keepdims=True)
        acc_ref[1:2, :] += sum_dh_xn * g
        acc_ref[2:3, :] += sum_dh_xn * (1.0 + sc1)
        gx_ref[...] = dx2_ref[...].astype(F32) + r * (dh * gs - xn * jnp.mean(dh_xn * gs, axis=-1, keepdims=True))

    tok = pl.BlockSpec((tm, D_MODEL), lambda i: (i, 0))
    return _call(
        body, "inproj_bwd", (t // tm,), [dq, drest, dkv, x, dx2, vec, w_t],
        [tok, pl.BlockSpec((tm, REST_WIDTH), lambda i: (i, 0)),
         pl.BlockSpec((tm, 2 * KV_WIDTH), lambda i: (i, 0)), tok, tok,
         _full((SUBLANES, D_MODEL)), _full((IN_WIDTH, D_MODEL))],
        [jax.ShapeDtypeStruct((t, D_MODEL), F32), jax.ShapeDtypeStruct((SUBLANES, D_MODEL), F32),
         jax.ShapeDtypeStruct((1, IN_WIDTH), F32)],
        [tok, _full((SUBLANES, D_MODEL)), _full((1, IN_WIDTH))], rider=rider)


def _weight_grad(b, a, name, bn, rows=None, row0=0, into=None, rider=None):
    t, n = b.shape
    m = a.shape[1]
    rows = n if rows is None else rows
    tk = min(TOKEN_TILE, t)
    for cand in (4 * TOKEN_TILE, 2 * TOKEN_TILE):
        if t % cand == 0 and 2 * cand * (bn + m) * 2 + bn * m * 4 <= WGRAD_VMEM:
            tk = cand
            break
    nk = t // tk
    block0 = row0 // bn

    def body(b_ref, a_ref, *rest):
        out_ref, acc_ref = rest[-2:]
        k = pl.program_id(1)

        @pl.when(k == 0)
        def _():
            acc_ref[...] = jnp.zeros_like(acc_ref)

        acc_ref[...] += lax.dot_general(b_ref[...], a_ref[...], TN_DIMS, preferred_element_type=F32)

        @pl.when(k == nk - 1)
        def _():
            out_ref[...] = acc_ref[...].astype(BF16)

    outs, routs = _call(
        body, name, (n // bn, nk), [b, a] + ([] if into is None else [into]),
        [pl.BlockSpec((tk, bn), lambda j, k: (k, j)), pl.BlockSpec((tk, m), lambda j, k: (k, 0))]
        + ([] if into is None else [ANY]),
        [jax.ShapeDtypeStruct((rows, m), BF16)], [pl.BlockSpec((bn, m), lambda j, k: (block0 + j, 0))],
        scratch=[pltpu.VMEM((bn, m), F32)], rider=rider, aliases=None if into is None else {2: 0})
    return outs[0], routs


def _to_rows(v):
    n = v.shape[0]
    padded = -(-n // (SUBLANES * LANES)) * SUBLANES * LANES
    return jnp.pad(v, (0, padded - n)).reshape(padded // LANES, LANES)


def _vec_rows(*rows):
    stacked = jnp.concatenate([r.reshape(1, D_MODEL) for r in rows], axis=0)
    return jnp.pad(stacked, ((0, SUBLANES - len(rows)), (0, 0)))


def kernel(x, c, w_ada, b_ada, g_mix, w_in, b_in, sinks, conv_w, w_out, g_ffn, w_ffn_in, w_ffn_out, g_final, loss_target, m_w_ada, m_b_ada, m_g_mix, m_w_in, m_b_in, m_sinks, m_conv_w, m_w_out, m_g_ffn, m_w_ffn_in, m_w_ffn_out, m_g_final, v_w_ada, v_b_ada, v_g_mix, v_w_in, v_b_in, v_sinks, v_conv_w, v_w_out, v_g_ffn, v_w_ffn_in, v_w_ffn_out, v_g_final):
    ix, iy, ic = _my_place()
    me = 4 * ix + 2 * iy + ic
    xs = x[0]
    target = loss_target[0]
    ada_cols = w_ada.shape[2]
    conv_cols = conv_w.shape[2]

    wt_in, wt_fi = jnp.transpose(w_in[0]), jnp.transpose(w_ffn_in[0])
    b_cols = lax.dynamic_slice_in_dim(b_ada, me * ada_cols, ada_cols, axis=1)
    g_in, (cast_fi, cast_out, cast_fo), first, mod_all = _gather_first_weight(
        wt_in, [wt_fi, w_out[0], w_ffn_out[0]], _to_rows(jnp.concatenate([c[0], conv_w[0].reshape(-1)])),
        w_ada[0], b_cols)
    first = first.reshape(N_DEV, -1)
    c_all = first[:, :D_MODEL]
    conv_full = jnp.transpose(first[:, D_MODEL:D_MODEL + 3 * conv_cols].reshape(N_DEV, 3, conv_cols), (1, 0, 2))
    conv_full = conv_full.reshape(3, D_MODEL)
    mod = lax.dynamic_index_in_dim(mod_all, me, axis=1, keepdims=False).reshape(N_MOD, D_MODEL)
    sh1, sc1, ga1, sh2, sc2, ga2 = [mod[i:i + 1] for i in range(N_MOD)]
    w_in_t = g_in.reshape(IN_WIDTH, D_MODEL)
    (z, h1), (g_fi, g_out) = _inproj_fwd(xs, _vec_rows(g_mix, sc1, sh1), w_in_t, b_in,
                                         _gather_rider([cast_fi, cast_out]))
    w_fi_t = g_fi.reshape(2 * D_FF, D_MODEL)
    w_out_full = g_out.reshape(D_MODEL, D_MODEL)
    (attn, lse), (g_fo,) = _attn_fwd(z, sinks[0], _gather_rider([cast_fo]))
    w_fo_full = g_fo.reshape(D_FF, D_MODEL)
    merged, x2, h2, oproj = _mix_fwd(
        xs, attn, z, _vec_rows(ga1, g_ffn, sc2, sh2, conv_full[0], conv_full[1], conv_full[2]), w_out_full)
    gu, act = _ffn_fwd(h2, w_fi_t)
    dx3, df, dgu, acc_l = _ffn_out_loss(act, gu, x2, target, _vec_rows(ga2, g_final), w_fo_full)

    gw_fo, _ = _weight_grad(act, df, "wgrad_ffn_out", D_FF)
    gw_fi, _ = _weight_grad(dgu, h2, "wgrad_ffn_in", D_FF)
    blocks_fo = gw_fo.reshape(N_DEV, D_FF // N_DEV, D_MODEL)
    blocks_fi = gw_fi.reshape(N_DEV, 2 * D_FF // N_DEV, D_MODEL)
    (dx2, acc_f), (sib_fo, sib_fi) = _ffn_in_bwd(dgu, x2, dx3, _vec_rows(g_ffn, sc2), w_fi_t,
                                                 _sibling_rider([blocks_fo, blocks_fi]))
    sums_fo, mine_fo = _sibling_sum(blocks_fo, sib_fo, "sibling_sum_ffn_out")
    sums_fi, mine_fi = _sibling_sum(blocks_fi, sib_fi, "sibling_sum_ffn_in")
    (dout, dattn, drest, acc_m), (ici_fo, ici_fi) = _mix_bwd(
        dx2, oproj, attn, z, _vec_rows(ga1, conv_full[0], conv_full[1], conv_full[2]), w_out_full,
        _chip_rider([sums_fo, sums_fi]))
    gw_out, _ = _weight_grad(merged, dout, "wgrad_out", D_MODEL)
    blocks_out = gw_out.reshape(N_DEV, D_MODEL // N_DEV, D_MODEL)
    (dq, dkv, dsink), (sib_out,) = _attn_bwd(z, dattn, attn, lse, sinks[0], _sibling_rider([blocks_out]))
    sums_out, mine_out = _sibling_sum(blocks_out, sib_out, "sibling_sum_out")
    gw_in, (ici_out,) = _weight_grad(drest, h1, "wgrad_in_rest", IN_CHUNK, rows=IN_WIDTH, row0=REF_REST_COL,
                                     rider=_chip_rider([sums_out]))
    gw_in, _ = _weight_grad(dq, h1, "wgrad_in_q", D_MODEL, rows=IN_WIDTH, row0=0, into=gw_in)
    gw_in, _ = _weight_grad(dkv, h1, "wgrad_in_kv", 2 * KV_WIDTH, rows=IN_WIDTH, row0=REF_KV_COL, into=gw_in)
    blocks_in = gw_in.reshape(N_DEV, IN_WIDTH // N_DEV, D_MODEL)
    sums_in, mine_in = _sibling_exchange_sum(blocks_in, "sibling_w_in")
    (grad_x, acc_i, db_in), (ici_in,) = _inproj_bwd(dq, drest, dkv, xs, dx2, _vec_rows(g_mix, sc1), w_in_t,
                                                    _chip_rider([sums_in]))

    widen = lambda vec: jnp.pad(vec, (0, -vec.shape[0] % D_MODEL))
    packed = jnp.concatenate([
        acc_i[0], acc_i[1], acc_m[0], acc_f[0], acc_f[1], acc_l[2],
        acc_i[2], widen(db_in[0]), acc_f[2], acc_l[1],
        acc_m[1], acc_m[2], acc_m[3], widen(dsink[0]), acc_l[0],
        jnp.zeros(((PACK_ROWS - PACK_SQERR - 1) * D_MODEL,), F32)]).reshape(PACK_ROWS, D_MODEL)
    packed_all = _small_allgather(packed, "gather_small")
    dmod_all = packed_all[:, PACK_DMOD:PACK_DMOD + N_MOD, :].reshape(N_DEV, N_MOD * D_MODEL)
    dmod_cols = lax.dynamic_slice_in_dim(dmod_all, me * ada_cols, ada_cols, axis=1)
    g_w_ada = _ada_weight_grad(c_all, dmod_cols)
    row_of = lambda a: a.reshape(1, -1)
    small, g_conv_full, loss = _small_finalize(packed_all, {
        "b_ada": (b_ada, m_b_ada, v_b_ada), "g_mix": (g_mix, m_g_mix, v_g_mix), "b_in": (b_in, m_b_in, v_b_in),
        "g_ffn": (g_ffn, m_g_ffn, v_g_ffn), "sinks": (sinks, m_sinks, v_sinks),
        "g_final": (row_of(g_final), row_of(m_g_final), row_of(v_g_final))})
    small["g_final"] = tuple(o.reshape(g_final.shape) for o in small["g_final"])
    g_conv = lax.dynamic_slice_in_dim(g_conv_full, me * conv_cols, conv_cols, axis=1)
    d_conv, nm_conv, nv_conv = _adamw(conv_w[0], g_conv, m_conv_w[0], v_conv_w[0], "adamw_conv_w")
    small["conv_w"] = (g_conv[None], d_conv[None], nm_conv[None], nv_conv[None])

    def reduced(mine, ici, w, m, v, name, transposed=False):
        turn = jnp.transpose if transposed else (lambda a: a)
        return tuple(turn(o)[None] for o in _chip_sum_adamw(mine, ici, turn(w[0]), turn(m[0]), turn(v[0]), name))

    d_ada, nm_ada, nv_ada = _adamw(w_ada[0], g_w_ada, m_w_ada[0], v_w_ada[0], "adamw_w_ada")
    res = {
        "w_ada": (g_w_ada[None], d_ada[None], nm_ada[None], nv_ada[None]),
        "w_in": reduced(mine_in, ici_in, w_in, m_w_in, v_w_in, "adamw_w_in", transposed=True),
        "w_out": reduced(mine_out, ici_out, w_out, m_w_out, v_w_out, "adamw_w_out"),
        "w_ffn_in": reduced(mine_fi, ici_fi, w_ffn_in, m_w_ffn_in, v_w_ffn_in, "adamw_w_ffn_in", transposed=True),
        "w_ffn_out": reduced(mine_fo, ici_fo, w_ffn_out, m_w_ffn_out, v_w_ffn_out, "adamw_w_ffn_out"),
    }
    res.update(small)
    order = ["w_ada", "b_ada", "g_mix", "w_in", "b_in", "sinks", "conv_w", "w_out", "g_ffn", "w_ffn_in", "w_ffn_out",
             "g_final"]
    outs = [loss.reshape(()), grad_x[None]]
    for k in range(4):
        outs += [res[n][k] for n in order]
    return tuple(outs)
```

```python
import functools
import math

import jax
import jax.numpy as jnp
from jax import lax
from jax.experimental import pallas as pl
from jax.experimental.pallas import tpu as pltpu

F32 = jnp.float32
BF16 = jnp.bfloat16
GRAD_STREAM = F32

D_MODEL = 1024
HEAD_DIM = 64
N_Q_HEADS = 16
N_KV_HEADS = 2
GROUP = 8
WINDOW = 128
KV_WIDTH = N_KV_HEADS * HEAD_DIM
D_FF = 2816
IN_WIDTH = 6400
N_MOD = 6
EPS = 1e-6
N_DEV = 8
REST_WIDTH = 5 * D_MODEL
KV_COL = D_MODEL + REST_WIDTH
ATTN_SCALE = HEAD_DIM ** -0.5

ADAM_LR = 0.001
ADAM_B1 = 0.9
ADAM_B2 = 0.999
ADAM_EPS = 1e-08
ADAM_WD = 0.01
ADAM_STEP = 10

LANES = 128
SUBLANES = 8
BF16_ROWS = 16
VMEM_LIMIT = 56 * 1024 * 1024
TOKEN_TILE = 512
FF_CHUNK = 256
ROW_PARTS = 2
MIN_STREAM_STEPS = 8
WGRAD_VMEM = 40 * 1024 * 1024
MESH = pl.DeviceIdType.MESH
ANY = pl.BlockSpec(memory_space=pl.ANY)

NT_DIMS = (((1,), (1,)), ((), ()))
TN_DIMS = (((0,), (0,)), ((), ()))
CHIP_FLIPS = [(0, 0), (1, 0), (0, 1), (1, 1)]


def _full(shape):
    return pl.BlockSpec(shape, lambda *_: (0,) * len(shape))


def _my_place():
    return lax.axis_index("x"), lax.axis_index("y"), lax.axis_index("c")


def _flip(v, bit):
    return 1 - v if bit else v


def _sigmoid(v):
    return 1.0 / (1.0 + jnp.exp2(v * (-1.4426950408889634)))


class _Rider:
    def __init__(self, ins, out_shapes, sem_shapes, first=None, mid=None, last=None, ins_in_vmem=False):
        self.ins, self.out_shapes, self.sem_shapes = list(ins), list(out_shapes), list(sem_shapes)
        self.in_specs = [_full(a.shape) if ins_in_vmem else ANY for a in self.ins]
        self.hooks = [(when, fn) for when, fn in (("first", first), ("mid", mid), ("last", last)) if fn is not None]


def _call(body, name, grid, args, in_specs, out_shape, out_specs, scratch=(), rider=None, aliases=None):
    n_in, n_out, n_scr = len(args), len(out_shape), len(scratch)
    r_in = rider.ins if rider else []
    r_out = rider.out_shapes if rider else []
    r_sem = rider.sem_shapes if rider else []
    nsteps = math.prod(grid)

    def full_body(*refs):
        pos = 0
        groups = []
        for size in (n_in, len(r_in), n_out, len(r_out), n_scr, len(r_sem)):
            groups.append(refs[pos:pos + size])
            pos += size
        ins, rins, outs, routs, scr, rsems = groups
        step = pl.program_id(0)
        for axis in range(1, len(grid)):
            step = step * grid[axis] + pl.program_id(axis)
        at = {"first": 0, "mid": (3 * nsteps) // 4, "last": nsteps - 1}
        hooks = rider.hooks if rider else []
        for when, fn in hooks:
            if when != "last":
                pl.when(step == at[when])(functools.partial(fn, rins, routs, rsems))
        body(*ins, *outs, *scr)
        for when, fn in hooks:
            if when == "last":
                pl.when(step == at[when])(functools.partial(fn, rins, routs, rsems))

    outs = pl.pallas_call(
        full_body, name=name, grid=grid,
        out_shape=list(out_shape) + list(r_out),
        in_specs=list(in_specs) + (rider.in_specs if rider else []),
        out_specs=list(out_specs) + [ANY] * len(r_out),
        scratch_shapes=list(scratch) + list(r_sem),
        input_output_aliases=dict(aliases or {}),
        compiler_params=pltpu.CompilerParams(dimension_semantics=("arbitrary",) * len(grid),
                                             vmem_limit_bytes=VMEM_LIMIT),
    )(*args, *r_in)
    return list(outs[:n_out]), list(outs[n_out:])


def _gather_rider(shards):
    n = len(shards)

    def setup(outs, sems):
        x, y, c = _my_place()
        send_sems, recv_sems, _ = sems
        chips = [(1 - x, y), (x, 1 - y), (1 - x, 1 - y)]

        def block(w, place):
            return outs[w].at[4 * place[0] + 2 * place[1] + place[2]]

        def copy(w, k, place, to, src=None):
            return pltpu.make_async_remote_copy(
                src_ref=block(w, place) if src is None else src, dst_ref=block(w, place),
                send_sem=send_sems.at[w, k], recv_sem=recv_sems.at[w, k], device_id=to, device_id_type=MESH)

        return (x, y, c), (x, y, 1 - c), chips, block, copy

    def first(ins, outs, sems):
        me, sibling, chips, block, copy = setup(outs, sems)
        for w in range(n):
            pltpu.make_async_copy(ins[w], block(w, me), sems[2].at[w]).start()
            copy(w, 0, me, sibling, src=ins[w]).start()
            for j, chip in enumerate(chips):
                copy(w, 1 + j, me, (*chip, me[2]), src=ins[w]).start()

    def mid(ins, outs, sems):
        me, sibling, chips, block, copy = setup(outs, sems)
        for w in range(n):
            for j, chip in enumerate(chips):
                copy(w, 1 + j, (*chip, me[2]), me).wait_recv()
                copy(w, 4 + j, (*chip, me[2]), sibling).start()

    def last(ins, outs, sems):
        me, sibling, chips, block, copy = setup(outs, sems)
        for w in range(n):
            copy(w, 0, sibling, me).wait_recv()
            for j, chip in enumerate(chips):
                copy(w, 4 + j, (*chip, 1 - me[2]), me).wait_recv()
            copy(w, 0, me, sibling, src=ins[w]).wait_send()
            for j, chip in enumerate(chips):
                copy(w, 1 + j, me, (*chip, me[2]), src=ins[w]).wait_send()
                copy(w, 4 + j, (*chip, me[2]), sibling).wait_send()
            pltpu.make_async_copy(ins[w], block(w, me), sems[2].at[w]).wait()

    return _Rider(
        shards, [jax.ShapeDtypeStruct((N_DEV,) + s.shape, BF16) for s in shards],
        [pltpu.SemaphoreType.DMA((n, N_DEV - 1)), pltpu.SemaphoreType.DMA((n, N_DEV - 1)),
         pltpu.SemaphoreType.DMA((n,))],
        first=first, mid=mid, last=last, ins_in_vmem=True)


def _sibling_rider(gblocks):
    n = len(gblocks)

    def copies(ins, outs, sems):
        x, y, c = _my_place()
        send_sems, recv_sems = sems
        made = []
        for w in range(n):
            for f, (fx, fy) in enumerate(CHIP_FLIPS):
                chip = 4 * _flip(x, fx) + 2 * _flip(y, fy)
                made.append(pltpu.make_async_remote_copy(
                    src_ref=ins[w].at[chip + 1 - c], dst_ref=outs[w].at[f], send_sem=send_sems.at[w, f],
                    recv_sem=recv_sems.at[w, f], device_id=(x, y, 1 - c), device_id_type=MESH))
        return made

    def first(ins, outs, sems):
        for cp in copies(ins, outs, sems):
            cp.start()

    def last(ins, outs, sems):
        for cp in copies(ins, outs, sems):
            cp.wait_recv()
            cp.wait_send()

    return _Rider(gblocks, [jax.ShapeDtypeStruct((4,) + g.shape[1:], BF16) for g in gblocks],
                  [pltpu.SemaphoreType.DMA((n, 4))] * 2, first=first, last=last)


def _chip_rider(sums):
    n = len(sums)

    def copies(ins, outs, sems):
        x, y, c = _my_place()
        send_sems, recv_sems = sems
        made = []
        for w in range(n):
            for f in (1, 2, 3):
                fx, fy = CHIP_FLIPS[f]
                made.append(pltpu.make_async_remote_copy(
                    src_ref=ins[w].at[f - 1], dst_ref=outs[w].at[f - 1], send_sem=send_sems.at[w, f - 1],
                    recv_sem=recv_sems.at[w, f - 1], device_id=(_flip(x, fx), _flip(y, fy), c), device_id_type=MESH))
        return made

    def first(ins, outs, sems):
        for cp in copies(ins, outs, sems):
            cp.start()

    def last(ins, outs, sems):
        for cp in copies(ins, outs, sems):
            cp.wait_recv()
            cp.wait_send()

    return _Rider(sums, [jax.ShapeDtypeStruct(s.shape, BF16) for s in sums],
                  [pltpu.SemaphoreType.DMA((n, 3))] * 2, first=first, last=last)


def _push_to_all(v_ref, out_ref, send_sems, recv_sems, local_sem, wait=True):
    x, y, c = _my_place()
    me = 4 * x + 2 * y + c
    mine = pltpu.make_async_copy(v_ref, out_ref.at[me], local_sem)
    mine.start()
    sends = []
    for k in range(1, N_DEV):
        px, py, pc = _flip(x, k & 4), _flip(y, k & 2), _flip(c, k & 1)
        cp = pltpu.make_async_remote_copy(
            src_ref=v_ref, dst_ref=out_ref.at[me], send_sem=send_sems.at[k - 1], recv_sem=recv_sems.at[k - 1],
            device_id=(px, py, pc), device_id_type=MESH)
        cp.start()
        sends.append(cp)

    def finish():
        for k in range(1, N_DEV):
            px, py, pc = _flip(x, k & 4), _flip(y, k & 2), _flip(c, k & 1)
            pltpu.make_async_remote_copy(
                src_ref=v_ref, dst_ref=out_ref.at[4 * px + 2 * py + pc], send_sem=send_sems.at[k - 1],
                recv_sem=recv_sems.at[k - 1], device_id=(px, py, pc), device_id_type=MESH).wait_recv()
        for cp in sends:
            cp.wait_send()
        mine.wait()

    if wait:
        finish()
    return finish


def _small_allgather(v, name):
    def body(v_ref, out_ref, send_sems, recv_sems, local_sem):
        _push_to_all(v_ref, out_ref, send_sems, recv_sems, local_sem)

    return pl.pallas_call(
        body, name=name,
        out_shape=jax.ShapeDtypeStruct((N_DEV,) + v.shape, F32),
        in_specs=[pl.BlockSpec(memory_space=pltpu.VMEM)],
        out_specs=pl.BlockSpec(memory_space=pltpu.VMEM),
        scratch_shapes=[pltpu.SemaphoreType.DMA((N_DEV - 1,)), pltpu.SemaphoreType.DMA((N_DEV - 1,)),
                        pltpu.SemaphoreType.DMA],
        compiler_params=pltpu.CompilerParams(vmem_limit_bytes=VMEM_LIMIT),
    )(v)


def _gather_first_weight(shard, others, cond_rows, w_ada, b_cols):
    n = len(others)
    ada_cols = w_ada.shape[1]
    c_rows = D_MODEL // LANES

    def body(*refs):
        w_ref, other_refs = refs[0], refs[1:1 + n]
        cond_ref, wada_ref, bcols_ref = refs[1 + n:4 + n]
        out_ref, cast_refs = refs[4 + n], refs[5 + n:5 + 2 * n]
        cond_all_ref, mod_all_ref = refs[5 + 2 * n:7 + 2 * n]
        mine_ref, mod_ref, send_sems, recv_sems, local_sem, small_send, small_recv, small_local = refs[7 + 2 * n:]
        x, y, c = _my_place()
        me, sibling = (x, y, c), (x, y, 1 - c)
        xnb, ynb, diag = (1 - x, y), (x, 1 - y), (1 - x, 1 - y)
        half = shard.shape[0] // 2

        def block(place, part=None):
            ref = out_ref.at[4 * place[0] + 2 * place[1] + place[2]]
            return ref if part is None else ref.at[pl.ds(part * half, half)]

        def copy(k, place, to, part=None, src=None):
            return pltpu.make_async_remote_copy(
                src_ref=block(place, part) if src is None else src, dst_ref=block(place, part),
                send_sem=send_sems.at[k], recv_sem=recv_sems.at[k], device_id=to, device_id_type=MESH)

        finish_cond = _push_to_all(cond_ref, cond_all_ref, small_send.at[0], small_recv.at[0], small_local.at[0],
                                   wait=False)
        mine_ref[...] = w_ref[...].astype(BF16)
        local = pltpu.make_async_copy(mine_ref, block(me), local_sem)
        local.start()
        started = [copy(0, me, sibling, src=mine_ref), copy(1, me, (*xnb, c), src=mine_ref),
                   copy(2, me, (*ynb, c), src=mine_ref)]
        for cp in started:
            cp.start()
        finish_cond()
        mod = jnp.zeros((N_DEV, ada_cols), F32) + bcols_ref[...]
        for r in range(c_rows):
            cf = cond_all_ref[:, r, :]
            act = (cf * _sigmoid(cf)).astype(BF16)
            mod = mod + jnp.dot(act, wada_ref[r * LANES:(r + 1) * LANES, :].astype(BF16),
                                preferred_element_type=F32)
        mod_ref[...] = mod
        finish_mod = _push_to_all(mod_ref, mod_all_ref, small_send.at[1], small_recv.at[1], small_local.at[1],
                                  wait=False)
        for o_ref, c_ref in zip(other_refs, cast_refs):
            c_ref[...] = o_ref[...].astype(BF16)
        def start(cp):
            cp.start()
            started.append(cp)

        copy(1, (*xnb, c), me).wait_recv()
        start(copy(3, (*xnb, c), (*ynb, c), part=0))
        start(copy(5, (*xnb, c), sibling))
        copy(2, (*ynb, c), me).wait_recv()
        start(copy(4, (*ynb, c), (*xnb, c), part=1))
        start(copy(6, (*ynb, c), sibling))
        copy(3, (*diag, c), me, part=0).wait_recv()
        start(copy(7, (*diag, c), sibling, part=0))
        copy(4, (*diag, c), me, part=1).wait_recv()
        start(copy(8, (*diag, c), sibling, part=1))
        copy(0, sibling, me).wait_recv()
        copy(5, (*xnb, 1 - c), me).wait_recv()
        copy(6, (*ynb, 1 - c), me).wait_recv()
        copy(7, (*diag, 1 - c), me, part=0).wait_recv()
        copy(8, (*diag, 1 - c), me, part=1).wait_recv()
        finish_mod()
        for cp in started:
            cp.wait_send()
        local.wait()

    vmem = pl.BlockSpec(memory_space=pltpu.VMEM)
    outs = pl.pallas_call(
        body, name="gather_w_in",
        out_shape=[jax.ShapeDtypeStruct((N_DEV,) + shard.shape, BF16)]
        + [jax.ShapeDtypeStruct(o.shape, BF16) for o in others]
        + [jax.ShapeDtypeStruct((N_DEV,) + cond_rows.shape, F32), jax.ShapeDtypeStruct((N_DEV, N_DEV, ada_cols), F32)],
        in_specs=[vmem] * (4 + n),
        out_specs=[ANY] + [vmem] * (n + 2),
        scratch_shapes=[pltpu.VMEM(shard.shape, BF16), pltpu.VMEM((N_DEV, ada_cols), F32),
                        pltpu.SemaphoreType.DMA((9,)), pltpu.SemaphoreType.DMA((9,)),
                        pltpu.SemaphoreType.DMA,
                        pltpu.SemaphoreType.DMA((2, N_DEV - 1)), pltpu.SemaphoreType.DMA((2, N_DEV - 1)),
                        pltpu.SemaphoreType.DMA((2,))],
        compiler_params=pltpu.CompilerParams(vmem_limit_bytes=VMEM_LIMIT),
    )(shard, *others, cond_rows, w_ada, b_cols)
    return outs[0], list(outs[1:1 + n]), outs[1 + n], outs[2 + n]


def _sibling_exchange_sum(gblocks, name):
    _, r, cdim = gblocks.shape

    def body(g_ref, sums_ref, mine_ref, own_buf, sib_buf, own_sems, send_sems, recv_sems):
        x, y, c = _my_place()
        pairs = []
        for f, (fx, fy) in enumerate(CHIP_FLIPS):
            chip = 4 * _flip(x, fx) + 2 * _flip(y, fy)
            own = pltpu.make_async_copy(g_ref.at[chip + c], own_buf.at[f], own_sems.at[f])
            own.start()
            remote = pltpu.make_async_remote_copy(
                src_ref=g_ref.at[chip + 1 - c], dst_ref=sib_buf.at[f], send_sem=send_sems.at[f],
                recv_sem=recv_sems.at[f], device_id=(x, y, 1 - c), device_id_type=MESH)
            remote.start()
            pairs.append((own, remote))
        for f in (1, 2, 3, 0):
            own, remote = pairs[f]
            own.wait()
            remote.wait_recv()
            total = own_buf[f].astype(F32) + sib_buf[f].astype(F32)
            if f == 0:
                mine_ref[...] = total
            else:
                sums_ref[f - 1] = total.astype(BF16)
        for _, remote in pairs:
            remote.wait_send()

    vmem = pl.BlockSpec(memory_space=pltpu.VMEM)
    return pl.pallas_call(
        body, name=name,
        out_shape=[jax.ShapeDtypeStruct((3, r, cdim), BF16), jax.ShapeDtypeStruct((r, cdim), F32)],
        in_specs=[ANY], out_specs=[vmem, vmem],
        scratch_shapes=[pltpu.VMEM((4, r, cdim), BF16), pltpu.VMEM((4, r, cdim), BF16),
                        pltpu.SemaphoreType.DMA((4,)), pltpu.SemaphoreType.DMA((4,)), pltpu.SemaphoreType.DMA((4,))],
        compiler_params=pltpu.CompilerParams(vmem_limit_bytes=VMEM_LIMIT),
    )(gblocks)


def _ada_weight_grad(c_all, dmod_cols):
    cols = dmod_cols.shape[1]

    def body(c_ref, d_ref, out_ref):
        cf = c_ref[...]
        act = (cf * _sigmoid(cf)).astype(BF16)
        out_ref[...] = lax.dot_general(act, d_ref[...].astype(BF16), TN_DIMS, preferred_element_type=F32)

    return pl.pallas_call(
        body, name="ada_weight_grad",
        out_shape=jax.ShapeDtypeStruct((D_MODEL, cols), F32),
        in_specs=[pl.BlockSpec(memory_space=pltpu.VMEM)] * 2,
        out_specs=pl.BlockSpec(memory_space=pltpu.VMEM),
        compiler_params=pltpu.CompilerParams(vmem_limit_bytes=VMEM_LIMIT),
    )(c_all, dmod_cols)


PACK_ROWS = 24
PACK_DMOD = 0
PACK_PARAMS = {"g_mix": (6, D_MODEL), "b_in": (7, IN_WIDTH), "g_ffn": (14, D_MODEL), "g_final": (15, D_MODEL),
               "sinks": (19, N_Q_HEADS)}
PACK_CONV = 16
PACK_SQERR = 20


def _small_finalize(packed_all, params):
    names = ["b_ada"] + list(PACK_PARAMS)
    layout = dict(PACK_PARAMS, b_ada=(PACK_DMOD, N_MOD * D_MODEL))
    n = len(names)

    def body(*refs):
        p_ref = refs[0]
        ins = refs[1:1 + 3 * n]
        outs = refs[1 + 3 * n:1 + 7 * n]
        conv_ref, loss_ref = refs[1 + 7 * n:]
        total = p_ref[0]
        for d in range(1, N_DEV):
            total = total + p_ref[d]
        for k, name in enumerate(names):
            row0, width = layout[name]
            w_ref, m_ref, v_ref = ins[3 * k:3 * k + 3]
            g_ref, d_ref, nm_ref, nv_ref = outs[4 * k:4 * k + 4]
            for chunk in range(-(-width // D_MODEL)):
                lo = chunk * D_MODEL
                hi = min(lo + D_MODEL, width)
                g = total[row0 + chunk:row0 + chunk + 1, :hi - lo]
                g_ref[:, lo:hi] = g
                d_ref[:, lo:hi], nm_ref[:, lo:hi], nv_ref[:, lo:hi] = _adamw_update(
                    w_ref[:, lo:hi], g, m_ref[:, lo:hi], v_ref[:, lo:hi])
        conv_ref[...] = total[PACK_CONV:PACK_CONV + 3, :]
        loss_ref[...] = (0.5 / D_MODEL) * jnp.sum(total[PACK_SQERR:PACK_SQERR + 1, :], keepdims=True)

    vmem = pl.BlockSpec(memory_space=pltpu.VMEM)
    flat = [a for name in names for a in params[name]]
    out_shape = [jax.ShapeDtypeStruct(params[name][0].shape, F32) for name in names for _ in range(4)]
    outs = pl.pallas_call(
        body, name="small_finalize",
        out_shape=out_shape + [jax.ShapeDtypeStruct((3, D_MODEL), F32), jax.ShapeDtypeStruct((1, 1), F32)],
        in_specs=[vmem] * (1 + 3 * n),
        out_specs=[vmem] * (4 * n + 2),
        compiler_params=pltpu.CompilerParams(vmem_limit_bytes=VMEM_LIMIT),
    )(packed_all, *flat)
    return {name: tuple(outs[4 * k:4 * k + 4]) for k, name in enumerate(names)}, outs[4 * n], outs[4 * n + 1]


def _row_tile(rows, multiple):
    for cand in range(rows // MIN_STREAM_STEPS, 0, -1):
        if rows % cand == 0 and cand % multiple == 0:
            return cand
    return rows


def _adamw_update(w, g, m, v):
    c1 = 1.0 / (1.0 - ADAM_B1 ** ADAM_STEP)
    c2 = 1.0 / (1.0 - ADAM_B2 ** ADAM_STEP)
    nm = ADAM_B1 * m + (1.0 - ADAM_B1) * g
    nv = ADAM_B2 * v + (1.0 - ADAM_B2) * (g * g)
    delta = -ADAM_LR * ((nm * c1) / (jnp.sqrt(nv * c2) + ADAM_EPS) + ADAM_WD * w)
    return delta, nm, nv


def _adamw(w, g, m, v, name):
    rows, cols = w.shape
    tile = _row_tile(rows, SUBLANES)

    def body(w_ref, g_ref, m_ref, v_ref, d_ref, nm_ref, nv_ref):
        d_ref[...], nm_ref[...], nv_ref[...] = _adamw_update(w_ref[...], g_ref[...], m_ref[...], v_ref[...])

    spec = pl.BlockSpec((tile, cols), lambda i: (i, 0))
    outs, _ = _call(body, name, (rows // tile,), [w, g, m, v], [spec] * 4,
                    [jax.ShapeDtypeStruct((rows, cols), F32)] * 3, [spec] * 3)
    return outs


def _sibling_sum(gblocks, sib, name):
    _, r, cdim = gblocks.shape
    tile = _row_tile(r, BF16_ROWS)
    x, y, c = _my_place()
    table = jnp.stack([4 * _flip(x, fx) + 2 * _flip(y, fy) + c for fx, fy in CHIP_FLIPS]).astype(jnp.int32)

    def body(table_ref, own0, own1, own2, own3, sib_ref, sums_ref, mine_ref):
        mine_ref[...] = own0[...].astype(F32) + sib_ref[0].astype(F32)
        for f, own in ((1, own1), (2, own2), (3, own3)):
            sums_ref[f - 1] = (own[...].astype(F32) + sib_ref[f].astype(F32)).astype(BF16)

    own_specs = [pl.BlockSpec((None, tile, cdim), functools.partial(lambda i, tab, f: (tab[f], i, 0), f=f))
                 for f in range(4)]
    return pl.pallas_call(
        body, name=name,
        grid_spec=pltpu.PrefetchScalarGridSpec(
            num_scalar_prefetch=1, grid=(r // tile,),
            in_specs=own_specs + [pl.BlockSpec((4, tile, cdim), lambda i, tab: (0, i, 0))],
            out_specs=[pl.BlockSpec((3, tile, cdim), lambda i, tab: (0, i, 0)),
                       pl.BlockSpec((tile, cdim), lambda i, tab: (i, 0))]),
        out_shape=[jax.ShapeDtypeStruct((3, r, cdim), BF16), jax.ShapeDtypeStruct((r, cdim), F32)],
        compiler_params=pltpu.CompilerParams(dimension_semantics=("arbitrary",), vmem_limit_bytes=VMEM_LIMIT),
    )(table, gblocks, gblocks, gblocks, gblocks, sib)


def _chip_sum_adamw(mine, ici, w, m, v, name):
    r, cdim = mine.shape
    tile = _row_tile(r, BF16_ROWS)

    def body(mine_ref, ici_ref, w_ref, m_ref, v_ref, g_ref, d_ref, nm_ref, nv_ref):
        g = mine_ref[...]
        for f in range(3):
            g = g + ici_ref[f].astype(F32)
        g_ref[...] = g
        d_ref[...], nm_ref[...], nv_ref[...] = _adamw_update(w_ref[...], g, m_ref[...], v_ref[...])

    spec = pl.BlockSpec((tile, cdim), lambda i: (i, 0))
    outs, _ = _call(
        body, name, (r // tile,), [mine, ici, w, m, v],
        [spec, pl.BlockSpec((3, tile, cdim), lambda i: (0, i, 0)), spec, spec, spec],
        [jax.ShapeDtypeStruct((r, cdim), F32)] * 4, [spec] * 4)
    return outs


REF_KV_COL = D_MODEL
REF_REST_COL = D_MODEL + 2 * KV_WIDTH
IN_CHUNK = 1280
IN_PIECES = ([(0, 0, D_MODEL)]
             + [(D_MODEL + n * IN_CHUNK, REF_REST_COL + n * IN_CHUNK, IN_CHUNK) for n in range(REST_WIDTH // IN_CHUNK)]
             + [(KV_COL, REF_KV_COL, 2 * KV_WIDTH)])


def _inproj_fwd(x, vec, w_t, b_in, rider):
    t = x.shape[0]
    tm = min(TOKEN_TILE, t)

    def body(x_ref, vec_ref, w_ref, b_ref, z_ref, h_ref):
        xf = x_ref[...]
        r = lax.rsqrt(jnp.mean(xf * xf, axis=-1, keepdims=True) + EPS)
        h = (xf * r) * (vec_ref[0:1, :] * (1.0 + vec_ref[1:2, :])) + vec_ref[2:3, :]
        hb = h.astype(BF16)
        h_ref[...] = hb
        for mine, ref, width in IN_PIECES:
            zc = lax.dot_general(hb, w_ref[ref:ref + width, :], NT_DIMS, preferred_element_type=F32)
            z_ref[:, mine:mine + width] = (zc + b_ref[:, ref:ref + width]).astype(BF16)

    return _call(
        body, "inproj_fwd", (t // tm,), [x, vec, w_t, b_in],
        [pl.BlockSpec((tm, D_MODEL), lambda i: (i, 0)), _full((SUBLANES, D_MODEL)),
         _full((IN_WIDTH, D_MODEL)), _full((1, IN_WIDTH))],
        [jax.ShapeDtypeStruct((t, IN_WIDTH), BF16), jax.ShapeDtypeStruct((t, D_MODEL), BF16)],
        [pl.BlockSpec((tm, IN_WIDTH), lambda i: (i, 0)), pl.BlockSpec((tm, D_MODEL), lambda i: (i, 0))],
        rider=rider)


PAIRS = GROUP // 2
STACK = PAIRS * WINDOW


ATTN_BLOCKS = 4
ATTN_BWD_BLOCKS = 1
LOG2E = 1.4426950408889634
LN2 = 0.6931471805599453
SCORE_SCALE = ATTN_SCALE * LOG2E


def _fill_window_bias(bias_ref):
    shape = bias_ref.shape[1:]
    kj = lax.broadcasted_iota(jnp.int32, shape, 0)
    qi = jnp.bitwise_and(lax.broadcasted_iota(jnp.int32, shape, 1), WINDOW - 1)
    in_prev = jnp.logical_and(kj < WINDOW, kj > qi)
    in_cur = jnp.logical_and(kj >= WINDOW, (kj - WINDOW) <= qi)
    bias_ref[0] = jnp.where(in_cur, 0.0, -jnp.inf)
    bias_ref[1] = jnp.where(jnp.logical_or(in_prev, in_cur), 0.0, -jnp.inf)


def _half_tiles(tile):
    low = lax.broadcasted_iota(jnp.int32, tile.shape, 1) < HEAD_DIM
    swapped = jnp.concatenate([tile[:, HEAD_DIM:], tile[:, :HEAD_DIM]], axis=1)
    zero = jnp.zeros_like(tile)
    return ((jnp.where(low, tile, zero), jnp.where(low, zero, swapped)),
            (jnp.where(low, swapped, zero), jnp.where(low, zero, tile)))


def _stack_pairs(ref, row0, j):
    return jnp.concatenate(
        [ref[pl.ds(row0, WINDOW), (j * PAIRS + p) * LANES:(j * PAIRS + p + 1) * LANES] for p in range(PAIRS)], axis=0)


def _per_pair_row(values):
    pair = lax.broadcasted_iota(jnp.int32, (1, STACK), 1) // WINDOW
    row = jnp.full((1, STACK), values[PAIRS - 1], F32)
    for p in range(PAIRS - 2, -1, -1):
        row = jnp.where(pair == p, values[p], row)
    return row


def _attn_fwd(z, sinks, rider):
    t = z.shape[0]
    tq = min(TOKEN_TILE, t)
    nblk = tq // WINDOW

    def body(q_ref, kv_ref, sink_ref, o_ref, lse_ref, bias_ref):
        i = pl.program_id(0)

        @pl.when(i == 0)
        def _():
            _fill_window_bias(bias_ref)

        def window(b):
            row0 = pl.multiple_of(b * WINDOW, WINDOW)
            start = i * tq + b * WINDOW
            prev = pl.multiple_of(jnp.maximum(start - WINDOW, 0), WINDOW)
            cur = pl.multiple_of(start, WINDOW)
            kvw = jnp.concatenate([kv_ref[pl.ds(prev, WINDOW), :], kv_ref[pl.ds(cur, WINDOW), :]], axis=0)
            return row0, _half_tiles(kvw[:, :KV_WIDTH]), _half_tiles(kvw[:, KV_WIDTH:]), bias_ref[jnp.minimum(start, 1)]

        def block_group(bb, carry):
            windows = [window(bb * ATTN_BLOCKS + n) for n in range(ATTN_BLOCKS)]
            for j in range(N_KV_HEADS):
                for pr in range(PAIRS):
                    cols = slice((j * PAIRS + pr) * LANES, (j * PAIRS + pr + 1) * LANES)
                    o_ts = [jnp.zeros((LANES, WINDOW), F32) for _ in windows]
                    for parity in range(2):
                        h = j * GROUP + 2 * pr + parity
                        sink = sink_ref[h] * LOG2E
                        for n, (row0, k_halves, v_halves, bias) in enumerate(windows):
                            qp = q_ref[pl.ds(row0, WINDOW), cols]
                            s = lax.dot_general(k_halves[j][parity], qp, NT_DIMS, preferred_element_type=F32)
                            s = s * SCORE_SCALE + bias
                            m = jnp.maximum(jnp.max(s, axis=0, keepdims=True), sink)
                            p = jnp.exp2(s - m)
                            denom = jnp.sum(p, axis=0, keepdims=True) + jnp.exp2(sink - m)
                            pv = lax.dot_general(v_halves[j][parity], p.astype(BF16), TN_DIMS,
                                                 preferred_element_type=F32)
                            o_ts[n] = o_ts[n] + pv * (1.0 / denom)
                            lse_ref[h:h + 1, pl.ds(row0, WINDOW)] = m + jnp.log2(denom)
                    for n, (row0, _, _, _) in enumerate(windows):
                        o_ref[pl.ds(row0, WINDOW), cols] = jnp.transpose(o_ts[n].astype(BF16))
            return carry

        lax.fori_loop(0, nblk // ATTN_BLOCKS, block_group, 0)

    return _call(
        body, "attn_fwd", (t // tq,), [z, z, sinks],
        [pl.BlockSpec((tq, D_MODEL), lambda i: (i, 0)),
         pl.BlockSpec((t, 2 * KV_WIDTH), lambda i: (0, KV_COL // (2 * KV_WIDTH))),
         pl.BlockSpec(memory_space=pltpu.SMEM)],
        [jax.ShapeDtypeStruct((t, D_MODEL), BF16), jax.ShapeDtypeStruct((N_Q_HEADS, t), F32)],
        [pl.BlockSpec((tq, D_MODEL), lambda i: (i, 0)), pl.BlockSpec((N_Q_HEADS, tq), lambda i: (0, i))],
        scratch=[pltpu.VMEM((2, 2 * WINDOW, WINDOW), F32)], rider=rider)


HALO = BF16_ROWS


def _shift_down(u, uh, k):
    rolled = pltpu.roll(u, k, 0)
    row = lax.broadcasted_iota(jnp.int32, (SUBLANES, u.shape[1]), 0)
    top = rolled[:SUBLANES, :]
    for j in range(k):
        top = jnp.where(row == j, uh[HALO - k + j:HALO - k + j + 1, :], top)
    return jnp.concatenate([top, rolled[SUBLANES:, :]], axis=0)


def _shift_up(u, nxt, k):
    n = u.shape[0]
    rolled = pltpu.roll(u, n - k, 0)
    row = lax.broadcasted_iota(jnp.int32, (SUBLANES, u.shape[1]), 0)
    bottom = rolled[n - SUBLANES:, :]
    for j in range(k):
        bottom = jnp.where(row == SUBLANES - k + j, nxt[j:j + 1, :], bottom)
    return jnp.concatenate([rolled[:n - SUBLANES, :], bottom], axis=0)


def _conv_inputs(cc_ref, cx_ref, hc_ref, hx_ref, first_tile):
    cc = cc_ref[...].astype(F32)
    cx = cx_ref[...].astype(F32)
    u = cc * cx
    uh = jnp.where(first_tile, 0.0, hc_ref[...].astype(F32) * hx_ref[...].astype(F32))
    return cc, cx, u, _shift_down(u, uh, 1), _shift_down(u, uh, 2)


def _z_specs(tm, order):
    per_tile = tm // HALO
    cols = [pl.BlockSpec((tm, D_MODEL), functools.partial(lambda i, j: (order(i), j), j=j)) for j in range(1, 6)]
    halos = [pl.BlockSpec((HALO, D_MODEL),
                          functools.partial(lambda i, j: (jnp.maximum(order(i) * per_tile - 1, 0), j), j=j))
             for j in (2, 3)]
    return cols + halos


def _mix_fwd(x, attn, z, vec, w_out):
    t = x.shape[0]
    tm = min(TOKEN_TILE, t)

    def body(x_ref, a_ref, cb_ref, cc_ref, cx_ref, ga_ref, gc_ref, hc_ref, hx_ref, vec_ref, w_ref,
             m_ref, x2_ref, h2_ref, o_ref):
        i = pl.program_id(0)
        _, _, u, u1, u2 = _conv_inputs(cc_ref, cx_ref, hc_ref, hx_ref, i == 0)
        cv = vec_ref[4:5, :] * u2 + vec_ref[5:6, :] * u1 + vec_ref[6:7, :] * u
        conv = cb_ref[...].astype(F32) * cv
        merged = (_sigmoid(ga_ref[...].astype(F32)) * a_ref[...].astype(F32)
                  + _sigmoid(gc_ref[...].astype(F32)) * conv)
        mb = merged.astype(BF16)
        m_ref[...] = mb
        o = jnp.dot(mb, w_ref[...], preferred_element_type=F32)
        o_ref[...] = o.astype(BF16)
        x2 = x_ref[...] + vec_ref[0:1, :] * o
        x2_ref[...] = x2
        r = lax.rsqrt(jnp.mean(x2 * x2, axis=-1, keepdims=True) + EPS)
        h2 = (x2 * r) * (vec_ref[1:2, :] * (1.0 + vec_ref[2:3, :])) + vec_ref[3:4, :]
        h2_ref[...] = h2.astype(BF16)

    tok = pl.BlockSpec((tm, D_MODEL), lambda i: (i, 0))
    outs, _ = _call(
        body, "mix_fwd", (t // tm,), [x, attn, z, z, z, z, z, z, z, vec, w_out],
        [tok, tok] + _z_specs(tm, lambda i: i) + [_full((SUBLANES, D_MODEL)), _full((D_MODEL, D_MODEL))],
        [jax.ShapeDtypeStruct((t, D_MODEL), BF16), jax.ShapeDtypeStruct((t, D_MODEL), F32),
         jax.ShapeDtypeStruct((t, D_MODEL), BF16), jax.ShapeDtypeStruct((t, D_MODEL), BF16)],
        [tok, tok, tok, tok])
    return outs


def _ffn_fwd(h2, w_t):
    t = h2.shape[0]
    tm = min(TOKEN_TILE, t)

    def body(h_ref, w_ref, gu_ref, a_ref):
        hb = h_ref[...]
        for n in range(D_FF // FF_CHUNK):
            lo, hi = n * FF_CHUNK, (n + 1) * FF_CHUNK
            g = lax.dot_general(hb, w_ref[lo:hi, :], NT_DIMS, preferred_element_type=F32)
            u = lax.dot_general(hb, w_ref[D_FF + lo:D_FF + hi, :], NT_DIMS, preferred_element_type=F32)
            sg = _sigmoid(g)
            silu = g * sg
            gu_ref[:, lo:hi] = (u * (sg + silu * (1.0 - sg))).astype(BF16)
            gu_ref[:, D_FF + lo:D_FF + hi] = silu.astype(BF16)
            a_ref[:, lo:hi] = (silu * u).astype(BF16)

    outs, _ = _call(
        body, "ffn_fwd", (t // tm,), [h2, w_t],
        [pl.BlockSpec((tm, D_MODEL), lambda i: (i, 0)), _full((2 * D_FF, D_MODEL))],
        [jax.ShapeDtypeStruct((t, 2 * D_FF), BF16), jax.ShapeDtypeStruct((t, D_FF), BF16)],
        [pl.BlockSpec((tm, 2 * D_FF), lambda i: (i, 0)), pl.BlockSpec((tm, D_FF), lambda i: (i, 0))])
    return outs


def _ffn_out_loss(a, gu, x2, target, vec, w_ffn_out):
    t = a.shape[0]
    tm = min(TOKEN_TILE, t)

    def body(a_ref, gu_ref, x2_ref, t_ref, vec_ref, w_ref, dx3_ref, df_ref, dgu_ref, acc_ref):
        @pl.when(pl.program_id(0) == 0)
        def _():
            acc_ref[...] = jnp.zeros_like(acc_ref)

        ga2 = vec_ref[0:1, :]
        gf = vec_ref[1:2, :]
        parts = min(ROW_PARTS, tm // LANES)
        part_rows = [slice(n * (tm // parts), (n + 1) * (tm // parts)) for n in range(parts)]

        def head(rows, f):
            x3 = x2_ref[rows, :] + ga2 * f
            r = lax.rsqrt(jnp.mean(x3 * x3, axis=-1, keepdims=True) + EPS)
            xn = x3 * r
            err = xn * gf - t_ref[rows, :]
            dxn = err * (gf * (1.0 / D_MODEL))
            dx3 = r * (dxn - xn * jnp.mean(dxn * xn, axis=-1, keepdims=True))
            dx3_ref[rows, :] = dx3.astype(GRAD_STREAM)
            sums = (jnp.sum(err * err, axis=0, keepdims=True),
                    jnp.sum(err * xn, axis=0, keepdims=True) * (1.0 / D_MODEL),
                    jnp.sum(dx3 * f, axis=0, keepdims=True))
            df = (dx3 * ga2).astype(BF16)
            df_ref[rows, :] = df
            return df, sums

        def tail(rows, df):
            for n in range(D_FF // FF_CHUNK):
                lo, hi = n * FF_CHUNK, (n + 1) * FF_CHUNK
                da = lax.dot_general(df, w_ref[lo:hi, :], NT_DIMS, preferred_element_type=F32)
                dgu_ref[rows, lo:hi] = (da * gu_ref[rows, lo:hi].astype(F32)).astype(BF16)
                dgu_ref[rows, D_FF + lo:D_FF + hi] = (da * gu_ref[rows, D_FF + lo:D_FF + hi].astype(F32)).astype(BF16)

        fs = [jnp.dot(a_ref[rows, :], w_ref[...], preferred_element_type=F32) for rows in part_rows]
        heads = [head(rows, f) for rows, f in zip(part_rows, fs)]
        for rows, (df, _) in zip(part_rows, heads):
            tail(rows, df)
        for k in range(3):
            total = heads[0][1][k]
            for _, sums in heads[1:]:
                total = total + sums[k]
            acc_ref[k:k + 1, :] += total

    tok = pl.BlockSpec((tm, D_MODEL), lambda i: (i, 0))
    outs, _ = _call(
        body, "ffn_out_loss", (t // tm,), [a, gu, x2, target, vec, w_ffn_out],
        [pl.BlockSpec((tm, D_FF), lambda i: (i, 0)), pl.BlockSpec((tm, 2 * D_FF), lambda i: (i, 0)),
         tok, tok, _full((SUBLANES, D_MODEL)), _full((D_FF, D_MODEL))],
        [jax.ShapeDtypeStruct((t, D_MODEL), GRAD_STREAM), jax.ShapeDtypeStruct((t, D_MODEL), BF16),
         jax.ShapeDtypeStruct((t, 2 * D_FF), BF16), jax.ShapeDtypeStruct((SUBLANES, D_MODEL), F32)],
        [tok, tok, pl.BlockSpec((tm, 2 * D_FF), lambda i: (i, 0)), _full((SUBLANES, D_MODEL))])
    return outs


def _ffn_in_bwd(dgu, x2, dx3, vec, w_t, rider):
    t = x2.shape[0]
    tm = min(TOKEN_TILE, t)

    def body(dgu_ref, x2_ref, dx3_ref, vec_ref, wf_ref, dx2_ref, acc_ref):
        @pl.when(pl.program_id(0) == 0)
        def _():
            acc_ref[...] = jnp.zeros_like(acc_ref)

        gffn = vec_ref[0:1, :]
        sc2 = vec_ref[1:2, :]
        parts = min(ROW_PARTS, tm // LANES)
        part_rows = [slice(n * (tm // parts), (n + 1) * (tm // parts)) for n in range(parts)]
        dhs = [jnp.dot(dgu_ref[rows, :], wf_ref[...], preferred_element_type=F32) for rows in part_rows]
        gs = gffn * (1.0 + sc2)
        sum_dh = jnp.zeros((1, D_MODEL), F32)
        sum_dh_xn = jnp.zeros((1, D_MODEL), F32)
        for rows, dh2 in zip(part_rows, dhs):
            x2 = x2_ref[rows, :]
            r = lax.rsqrt(jnp.mean(x2 * x2, axis=-1, keepdims=True) + EPS)
            xn = x2 * r
            dh_xn = dh2 * xn
            sum_dh = sum_dh + jnp.sum(dh2, axis=0, keepdims=True)
            sum_dh_xn = sum_dh_xn + jnp.sum(dh_xn, axis=0, keepdims=True)
            dx2 = dx3_ref[rows, :].astype(F32) + r * (dh2 * gs - xn * jnp.mean(dh_xn * gs, axis=-1, keepdims=True))
            dx2_ref[rows, :] = dx2.astype(GRAD_STREAM)
        acc_ref[0:1, :] += sum_dh
        acc_ref[1:2, :] += sum_dh_xn * gffn
        acc_ref[2:3, :] += sum_dh_xn * (1.0 + sc2)

    tok = pl.BlockSpec((tm, D_MODEL), lambda i: (i, 0))
    return _call(
        body, "ffn_in_bwd", (t // tm,), [dgu, x2, dx3, vec, w_t],
        [pl.BlockSpec((tm, 2 * D_FF), lambda i: (i, 0)), tok, tok, _full((SUBLANES, D_MODEL)),
         _full((2 * D_FF, D_MODEL))],
        [jax.ShapeDtypeStruct((t, D_MODEL), GRAD_STREAM), jax.ShapeDtypeStruct((SUBLANES, D_MODEL), F32)],
        [tok, _full((SUBLANES, D_MODEL))], rider=rider)


def _mix_bwd(dx2, oproj, attn, z, vec, w_out, rider):
    t = dx2.shape[0]
    tm = min(TOKEN_TILE, t)
    nt = t // tm
    rev = lambda i: nt - 1 - i

    def body(dx2_ref, m_ref, a_ref, cb_ref, cc_ref, cx_ref, ga_ref, gc_ref, hc_ref, hx_ref,
             vec_ref, wo_ref, do_ref, da_ref, dr_ref, acc_ref, carry_ref):
        i = pl.program_id(0)

        @pl.when(i == 0)
        def _():
            acc_ref[...] = jnp.zeros_like(acc_ref)
            carry_ref[...] = jnp.zeros_like(carry_ref)

        ga1 = vec_ref[0:1, :]
        w0, w1, w2 = vec_ref[1:2, :], vec_ref[2:3, :], vec_ref[3:4, :]
        dx2 = dx2_ref[...].astype(F32)
        acc_ref[0:1, :] += jnp.sum(dx2 * m_ref[...].astype(F32), axis=0, keepdims=True)
        do = (dx2 * ga1).astype(BF16)
        do_ref[...] = do
        dm = lax.dot_general(do, wo_ref[...], NT_DIMS, preferred_element_type=F32)

        cc, cx, u, u1, u2 = _conv_inputs(cc_ref, cx_ref, hc_ref, hx_ref, i == nt - 1)
        cv = w0 * u2 + w1 * u1 + w2 * u
        cb = cb_ref[...].astype(F32)
        sa = _sigmoid(ga_ref[...].astype(F32))
        sc = _sigmoid(gc_ref[...].astype(F32))
        attn = a_ref[...].astype(F32)
        dattn = dm * sa
        da_ref[...] = dattn.astype(BF16)
        dconv = dm * sc
        dconv_b = dconv * cv
        dr_ref[:, 3 * D_MODEL:4 * D_MODEL] = (dattn * attn * (1.0 - sa)).astype(BF16)
        dr_ref[:, 4 * D_MODEL:5 * D_MODEL] = (dconv_b * cb * (1.0 - sc)).astype(BF16)
        dr_ref[:, 0:D_MODEL] = dconv_b.astype(BF16)
        dcv = dconv * cb
        acc_ref[1:2, :] += jnp.sum(dcv * u2, axis=0, keepdims=True)
        acc_ref[2:3, :] += jnp.sum(dcv * u1, axis=0, keepdims=True)
        acc_ref[3:4, :] += jnp.sum(dcv * u, axis=0, keepdims=True)
        nxt = carry_ref[...]
        du = w2 * dcv + w1 * _shift_up(dcv, nxt, 1) + w0 * _shift_up(dcv, nxt, 2)
        carry_ref[...] = dcv[0:SUBLANES, :]
        dr_ref[:, D_MODEL:2 * D_MODEL] = (du * cx).astype(BF16)
        dr_ref[:, 2 * D_MODEL:3 * D_MODEL] = (du * cc).astype(BF16)

    tok = pl.BlockSpec((tm, D_MODEL), lambda i: (rev(i), 0))
    return _call(
        body, "mix_bwd", (nt,), [dx2, oproj, attn, z, z, z, z, z, z, z, vec, w_out],
        [tok, tok, tok] + _z_specs(tm, rev) + [_full((SUBLANES, D_MODEL)), _full((D_MODEL, D_MODEL))],
        [jax.ShapeDtypeStruct((t, D_MODEL), BF16), jax.ShapeDtypeStruct((t, D_MODEL), BF16),
         jax.ShapeDtypeStruct((t, REST_WIDTH), BF16), jax.ShapeDtypeStruct((SUBLANES, D_MODEL), F32)],
        [tok, tok, pl.BlockSpec((tm, REST_WIDTH), lambda i: (rev(i), 0)), _full((SUBLANES, D_MODEL))],
        scratch=[pltpu.VMEM((SUBLANES, D_MODEL), F32)], rider=rider)


def _attn_bwd(z, dattn, attn, lse, sinks, rider):
    t = z.shape[0]
    tq = min(TOKEN_TILE, t)
    nblk = tq // WINDOW
    nt = t // tq

    def body(q_ref, kv_ref, do_ref, o_ref, lse_ref, sink_ref, dq_ref, dkv_ref, ds_ref, acc_ref, bias_ref):
        i = pl.program_id(0)

        @pl.when(i == 0)
        def _():
            acc_ref[...] = jnp.zeros_like(acc_ref)
            ds_ref[...] = jnp.zeros_like(ds_ref)
            _fill_window_bias(bias_ref)

        lane = lax.broadcasted_iota(jnp.int32, (1, LANES), 1)
        ind_row = lax.broadcasted_iota(jnp.int32, (SUBLANES, LANES), 0)
        ind_low = lax.broadcasted_iota(jnp.int32, (SUBLANES, LANES), 1) < HEAD_DIM
        indicator = jnp.where(jnp.logical_or(jnp.logical_and(ind_row == 0, ind_low),
                                             jnp.logical_and(ind_row == 1, jnp.logical_not(ind_low))),
                              1.0, 0.0).astype(BF16)
        low = lax.broadcasted_iota(jnp.int32, (2 * WINDOW, LANES), 1) < HEAD_DIM

        def both_heads(even, odd):
            picked = jnp.where(low, even, odd)
            return picked + jnp.concatenate([picked[:, HEAD_DIM:], picked[:, :HEAD_DIM]], axis=1)

        def window(b):
            row0 = pl.multiple_of(b * WINDOW, WINDOW)
            start = i * tq + b * WINDOW
            prev = pl.multiple_of(jnp.maximum(start - WINDOW, 0), WINDOW)
            cur = pl.multiple_of(start, WINDOW)
            kvw = jnp.concatenate([kv_ref[pl.ds(prev, WINDOW), :], kv_ref[pl.ds(cur, WINDOW), :]], axis=0)
            return (row0, prev, cur, _half_tiles(kvw[:, :KV_WIDTH]), _half_tiles(kvw[:, KV_WIDTH:]),
                    bias_ref[jnp.minimum(start, 1)])

        def block_group(bb, dsink):
            windows = [window(bb * ATTN_BWD_BLOCKS + n) for n in range(ATTN_BWD_BLOCKS)]
            dk_groups = [[] for _ in windows]
            dv_groups = [[] for _ in windows]
            for j in range(N_KV_HEADS):
                stacks, deltas, dq_ts = [], [], []
                for row0, _, _, _, _, _ in windows:
                    qst = _stack_pairs(q_ref, row0, j)
                    dost = _stack_pairs(do_ref, row0, j)
                    prod = dost.astype(F32) * _stack_pairs(o_ref, row0, j).astype(F32)
                    prod_hi = prod.astype(BF16)
                    prod_lo = (prod - prod_hi.astype(F32)).astype(BF16)
                    stacks.append((qst, dost))
                    deltas.append(lax.dot_general(indicator, prod_hi, NT_DIMS, preferred_element_type=F32)
                                  + lax.dot_general(indicator, prod_lo, NT_DIMS, preferred_element_type=F32))
                    dq_ts.append(jnp.zeros((LANES, STACK), F32))
                dk_par = [[] for _ in windows]
                dv_par = [[] for _ in windows]
                for parity in range(2):
                    heads = [j * GROUP + 2 * p + parity for p in range(PAIRS)]
                    sink = _per_pair_row([sink_ref[h] * LOG2E for h in heads])
                    for n, (row0, _, _, k_halves, v_halves, bias) in enumerate(windows):
                        qst, dost = stacks[n]
                        kk, vv = k_halves[j][parity], v_halves[j][parity]
                        s = lax.dot_general(kk, qst, NT_DIMS, preferred_element_type=F32) * SCORE_SCALE + bias
                        lse = jnp.concatenate([lse_ref[h:h + 1, pl.ds(row0, WINDOW)] for h in heads], axis=1)
                        p = jnp.exp2(s - lse)
                        dp = lax.dot_general(vv, dost, NT_DIMS, preferred_element_type=F32)
                        delta = deltas[n][parity:parity + 1, :]
                        dsb = (p * (dp - delta)).astype(BF16)
                        dq_ts[n] = dq_ts[n] + lax.dot_general(kk, dsb, TN_DIMS, preferred_element_type=F32)
                        dk_par[n].append(jnp.dot(dsb, qst, preferred_element_type=F32))
                        dv_par[n].append(jnp.dot(p.astype(BF16), dost, preferred_element_type=F32))
                        weighted = jnp.exp2(sink - lse) * delta
                        for pr, h in enumerate(heads):
                            dsink = dsink - jnp.where(
                                lane == h, jnp.sum(weighted[:, pr * WINDOW:(pr + 1) * WINDOW]), 0.0)
                for n, (row0, _, _, _, _, _) in enumerate(windows):
                    dq_st = jnp.transpose((dq_ts[n] * ATTN_SCALE).astype(BF16))
                    for pr in range(PAIRS):
                        dq_ref[pl.ds(row0, WINDOW), (j * PAIRS + pr) * LANES:(j * PAIRS + pr + 1) * LANES] = (
                            dq_st[pr * WINDOW:(pr + 1) * WINDOW, :])
                    dk_groups[n].append(both_heads(dk_par[n][0], dk_par[n][1]))
                    dv_groups[n].append(both_heads(dv_par[n][0], dv_par[n][1]))
            for n, (_, prev, cur, _, _, _) in enumerate(windows):
                blk = jnp.concatenate([jnp.where(low, dk_groups[n][0], dk_groups[n][1]) * ATTN_SCALE,
                                       jnp.where(low, dv_groups[n][0], dv_groups[n][1])], axis=1)
                acc_ref[pl.ds(prev, WINDOW), :] += blk[:WINDOW, :]
                acc_ref[pl.ds(cur, WINDOW), :] += blk[WINDOW:, :]
            return dsink

        dsink = lax.fori_loop(0, nblk // ATTN_BWD_BLOCKS, block_group, jnp.zeros((1, LANES), F32))
        ds_ref[0:1, :] += dsink

        @pl.when(i == nt - 1)
        def _():
            dkv_ref[...] = acc_ref[...].astype(BF16)

    tok = pl.BlockSpec((tq, D_MODEL), lambda i: (i, 0))
    return _call(
        body, "attn_bwd", (nt,), [z, z, dattn, attn, lse, sinks],
        [tok, pl.BlockSpec((t, 2 * KV_WIDTH), lambda i: (0, KV_COL // (2 * KV_WIDTH))), tok, tok,
         pl.BlockSpec((N_Q_HEADS, tq), lambda i: (0, i)), pl.BlockSpec(memory_space=pltpu.SMEM)],
        [jax.ShapeDtypeStruct((t, D_MODEL), BF16), jax.ShapeDtypeStruct((t, 2 * KV_WIDTH), BF16),
         jax.ShapeDtypeStruct((SUBLANES, LANES), F32)],
        [tok, _full((t, 2 * KV_WIDTH)), _full((SUBLANES, LANES))],
        scratch=[pltpu.VMEM((t, 2 * KV_WIDTH), F32), pltpu.VMEM((2, 2 * WINDOW, STACK), F32)], rider=rider)


def _inproj_bwd(dq, drest, dkv, x, dx2, vec, w_t, rider):
    t = x.shape[0]
    tm = min(TOKEN_TILE, t)

    def body(dq_ref, dr_ref, dkv_ref, x_ref, dx2_ref, vec_ref, w_ref, gx_ref, acc_ref, db_ref):
        @pl.when(pl.program_id(0) == 0)
        def _():
            acc_ref[...] = jnp.zeros_like(acc_ref)
            db_ref[...] = jnp.zeros_like(db_ref)

        g = vec_ref[0:1, :]
        sc1 = vec_ref[1:2, :]
        dqb, drb, dkvb = dq_ref[...], dr_ref[...], dkv_ref[...]
        dh = jnp.dot(dqb, w_ref[:REF_KV_COL, :], preferred_element_type=F32)
        dh = dh + jnp.dot(drb, w_ref[REF_REST_COL:, :], preferred_element_type=F32)
        dh = dh + jnp.dot(dkvb, w_ref[REF_KV_COL:REF_REST_COL, :], preferred_element_type=F32)
        db_ref[:, :REF_KV_COL] += jnp.sum(dqb.astype(F32), axis=0, keepdims=True)
        db_ref[:, REF_REST_COL:] += jnp.sum(drb.astype(F32), axis=0, keepdims=True)
        db_ref[:, REF_KV_COL:REF_REST_COL] += jnp.sum(dkvb.astype(F32), axis=0, keepdims=True)
        xf = x_ref[...]
        r = lax.rsqrt(jnp.mean(xf * xf, axis=-1, keepdims=True) + EPS)
        xn = xf * r
        gs = g * (1.0 + sc1)
        dh_xn = dh * xn
        sum_dh_xn = jnp.sum(dh_xn, axis=0, keepdims=True)
        acc_ref[0:1, :] += jnp.sum(dh, axis=0, keepdims=True)
        acc_ref[1:2, :] += sum_dh_xn * g
        acc_ref[2:3, :] += sum_dh_xn * (1.0 + sc1)
        gx_ref[...] = dx2_ref[...].astype(F32) + r * (dh * gs - xn * jnp.mean(dh_xn * gs, axis=-1, keepdims=True))

    tok = pl.BlockSpec((tm, D_MODEL), lambda i: (i, 0))
    return _call(
        body, "inproj_bwd", (t // tm,), [dq, drest, dkv, x, dx2, vec, w_t],
        [tok, pl.BlockSpec((tm, REST_WIDTH), lambda i: (i, 0)),
         pl.BlockSpec((tm, 2 * KV_WIDTH), lambda i: (i, 0)), tok, tok,
         _full((SUBLANES, D_MODEL)), _full((IN_WIDTH, D_MODEL))],
        [jax.ShapeDtypeStruct((t, D_MODEL), F32), jax.ShapeDtypeStruct((SUBLANES, D_MODEL), F32),
         jax.ShapeDtypeStruct((1, IN_WIDTH), F32)],
        [tok, _full((SUBLANES, D_MODEL)), _full((1, IN_WIDTH))], rider=rider)


def _weight_grad(b, a, name, bn, rows=None, row0=0, into=None, rider=None):
    t, n = b.shape
    m = a.shape[1]
    rows = n if rows is None else rows
    tk = min(TOKEN_TILE, t)
    for cand in (4 * TOKEN_TILE, 2 * TOKEN_TILE):
        if t % cand == 0 and 2 * cand * (bn + m) * 2 + bn * m * 4 <= WGRAD_VMEM:
            tk = cand
            break
    nk = t // tk
    block0 = row0 // bn

    def body(b_ref, a_ref, *rest):
        out_ref, acc_ref = rest[-2:]
        k = pl.program_id(1)

        @pl.when(k == 0)
        def _():
            acc_ref[...] = jnp.zeros_like(acc_ref)

        acc_ref[...] += lax.dot_general(b_ref[...], a_ref[...], TN_DIMS, preferred_element_type=F32)

        @pl.when(k == nk - 1)
        def _():
            out_ref[...] = acc_ref[...].astype(BF16)

    outs, routs = _call(
        body, name, (n // bn, nk), [b, a] + ([] if into is None else [into]),
        [pl.BlockSpec((tk, bn), lambda j, k: (k, j)), pl.BlockSpec((tk, m), lambda j, k: (k, 0))]
        + ([] if into is None else [ANY]),
        [jax.ShapeDtypeStruct((rows, m), BF16)], [pl.BlockSpec((bn, m), lambda j, k: (block0 + j, 0))],
        scratch=[pltpu.VMEM((bn, m), F32)], rider=rider, aliases=None if into is None else {2: 0})
    return outs[0], routs


def _to_rows(v):
    n = v.shape[0]
    padded = -(-n // (SUBLANES * LANES)) * SUBLANES * LANES
    return jnp.pad(v, (0, padded - n)).reshape(padded // LANES, LANES)


def _vec_rows(*rows):
    stacked = jnp.concatenate([r.reshape(1, D_MODEL) for r in rows], axis=0)
    return jnp.pad(stacked, ((0, SUBLANES - len(rows)), (0, 0)))


def kernel(x, c, w_ada, b_ada, g_mix, w_in, b_in, sinks, conv_w, w_out, g_ffn, w_ffn_in, w_ffn_out, g_final, loss_target, m_w_ada, m_b_ada, m_g_mix, m_w_in, m_b_in, m_sinks, m_conv_w, m_w_out, m_g_ffn, m_w_ffn_in, m_w_ffn_out, m_g_final, v_w_ada, v_b_ada, v_g_mix, v_w_in, v_b_in, v_sinks, v_conv_w, v_w_out, v_g_ffn, v_w_ffn_in, v_w_ffn_out, v_g_final):
    ix, iy, ic = _my_place()
    me = 4 * ix + 2 * iy + ic
    xs = x[0]
    target = loss_target[0]
    ada_cols = w_ada.shape[2]
    conv_cols = conv_w.shape[2]

    wt_in, wt_fi = jnp.transpose(w_in[0]), jnp.transpose(w_ffn_in[0])
    b_cols = lax.dynamic_slice_in_dim(b_ada, me * ada_cols, ada_cols, axis=1)
    g_in, (cast_fi, cast_out, cast_fo), first, mod_all = _gather_first_weight(
        wt_in, [wt_fi, w_out[0], w_ffn_out[0]], _to_rows(jnp.concatenate([c[0], conv_w[0].reshape(-1)])),
        w_ada[0], b_cols)
    first = first.reshape(N_DEV, -1)
    c_all = first[:, :D_MODEL]
    conv_full = jnp.transpose(first[:, D_MODEL:D_MODEL + 3 * conv_cols].reshape(N_DEV, 3, conv_cols), (1, 0, 2))
    conv_full = conv_full.reshape(3, D_MODEL)
    mod = lax.dynamic_index_in_dim(mod_all, me, axis=1, keepdims=False).reshape(N_MOD, D_MODEL)
    sh1, sc1, ga1, sh2, sc2, ga2 = [mod[i:i + 1] for i in range(N_MOD)]
    w_in_t = g_in.reshape(IN_WIDTH, D_MODEL)
    (z, h1), (g_fi, g_out) = _inproj_fwd(xs, _vec_rows(g_mix, sc1, sh1), w_in_t, b_in,
                                         _gather_rider([cast_fi, cast_out]))
    w_fi_t = g_fi.reshape(2 * D_FF, D_MODEL)
    w_out_full = g_out.reshape(D_MODEL, D_MODEL)
    (attn, lse), (g_fo,) = _attn_fwd(z, sinks[0], _gather_rider([cast_fo]))
    w_fo_full = g_fo.reshape(D_FF, D_MODEL)
    merged, x2, h2, oproj = _mix_fwd(
        xs, attn, z, _vec_rows(ga1, g_ffn, sc2, sh2, conv_full[0], conv_full[1], conv_full[2]), w_out_full)
    gu, act = _ffn_fwd(h2, w_fi_t)
    dx3, df, dgu, acc_l = _ffn_out_loss(act, gu, x2, target, _vec_rows(ga2, g_final), w_fo_full)

    gw_fo, _ = _weight_grad(act, df, "wgrad_ffn_out", D_FF)
    gw_fi, _ = _weight_grad(dgu, h2, "wgrad_ffn_in", D_FF)
    blocks_fo = gw_fo.reshape(N_DEV, D_FF // N_DEV, D_MODEL)
    blocks_fi = gw_fi.reshape(N_DEV, 2 * D_FF // N_DEV, D_MODEL)
    (dx2, acc_f), (sib_fo, sib_fi) = _ffn_in_bwd(dgu, x2, dx3, _vec_rows(g_ffn, sc2), w_fi_t,
                                                 _sibling_rider([blocks_fo, blocks_fi]))
    sums_fo, mine_fo = _sibling_sum(blocks_fo, sib_fo, "sibling_sum_ffn_out")
    sums_fi, mine_fi = _sibling_sum(blocks_fi, sib_fi, "sibling_sum_ffn_in")
    (dout, dattn, drest, acc_m), (ici_fo, ici_fi) = _mix_bwd(
        dx2, oproj, attn, z, _vec_rows(ga1, conv_full[0], conv_full[1], conv_full[2]), w_out_full,
        _chip_rider([sums_fo, sums_fi]))
    gw_out, _ = _weight_grad(merged, dout, "wgrad_out", D_MODEL)
    blocks_out = gw_out.reshape(N_DEV, D_MODEL // N_DEV, D_MODEL)
    (dq, dkv, dsink), (sib_out,) = _attn_bwd(z, dattn, attn, lse, sinks[0], _sibling_rider([blocks_out]))
    sums_out, mine_out = _sibling_sum(blocks_out, sib_out, "sibling_sum_out")
    gw_in, (ici_out,) = _weight_grad(drest, h1, "wgrad_in_rest", IN_CHUNK, rows=IN_WIDTH, row0=REF_REST_COL,
                                     rider=_chip_rider([sums_out]))
    gw_in, _ = _weight_grad(dq, h1, "wgrad_in_q", D_MODEL, rows=IN_WIDTH, row0=0, into=gw_in)
    gw_in, _ = _weight_grad(dkv, h1, "wgrad_in_kv", 2 * KV_WIDTH, rows=IN_WIDTH, row0=REF_KV_COL, into=gw_in)
    blocks_in = gw_in.reshape(N_DEV, IN_WIDTH // N_DEV, D_MODEL)
    sums_in, mine_in = _sibling_exchange_sum(blocks_in, "sibling_w_in")
    (grad_x, acc_i, db_in), (ici_in,) = _inproj_bwd(dq, drest, dkv, xs, dx2, _vec_rows(g_mix, sc1), w_in_t,
                                                    _chip_rider([sums_in]))

    widen = lambda vec: jnp.pad(vec, (0, -vec.shape[0] % D_MODEL))
    packed = jnp.concatenate([
        acc_i[0], acc_i[1], acc_m[0], acc_f[0], acc_f[1], acc_l[2],
        acc_i[2], widen(db_in[0]), acc_f[2], acc_l[1],
        acc_m[1], acc_m[2], acc_m[3], widen(dsink[0]), acc_l[0],
        jnp.zeros(((PACK_ROWS - PACK_SQERR - 1) * D_MODEL,), F32)]).reshape(PACK_ROWS, D_MODEL)
    packed_all = _small_allgather(packed, "gather_small")
    dmod_all = packed_all[:, PACK_DMOD:PACK_DMOD + N_MOD, :].reshape(N_DEV, N_MOD * D_MODEL)
    dmod_cols = lax.dynamic_slice_in_dim(dmod_all, me * ada_cols, ada_cols, axis=1)
    g_w_ada = _ada_weight_grad(c_all, dmod_cols)
    row_of = lambda a: a.reshape(1, -1)
    small, g_conv_full, loss = _small_finalize(packed_all, {
        "b_ada": (b_ada, m_b_ada, v_b_ada), "g_mix": (g_mix, m_g_mix, v_g_mix), "b_in": (b_in, m_b_in, v_b_in),
        "g_ffn": (g_ffn, m_g_ffn, v_g_ffn), "sinks": (sinks, m_sinks, v_sinks),
        "g_final": (row_of(g_final), row_of(m_g_final), row_of(v_g_final))})
    small["g_final"] = tuple(o.reshape(g_final.shape) for o in small["g_final"])
    g_conv = lax.dynamic_slice_in_dim(g_conv_full, me * conv_cols, conv_cols, axis=1)
    d_conv, nm_conv, nv_conv = _adamw(conv_w[0], g_conv, m_conv_w[0], v_conv_w[0], "adamw_conv_w")
    small["conv_w"] = (g_conv[None], d_conv[None], nm_conv[None], nv_conv[None])

    def reduced(mine, ici, w, m, v, name, transposed=False):
        turn = jnp.transpose if transposed else (lambda a: a)
        return tuple(turn(o)[None] for o in _chip_sum_adamw(mine, ici, turn(w[0]), turn(m[0]), turn(v[0]), name))

    d_ada, nm_ada, nv_ada = _adamw(w_ada[0], g_w_ada, m_w_ada[0], v_w_ada[0], "adamw_w_ada")
    res = {
        "w_ada": (g_w_ada[None], d_ada[None], nm_ada[None], nv_ada[None]),
        "w_in": reduced(mine_in, ici_in, w_in, m_w_in, v_w_in, "adamw_w_in", transposed=True),
        "w_out": reduced(mine_out, ici_out, w_out, m_w_out, v_w_out, "adamw_w_out"),
        "w_ffn_in": reduced(mine_fi, ici_fi, w_ffn_in, m_w_ffn_in, v_w_ffn_in, "adamw_w_ffn_in", transposed=True),
        "w_ffn_out": reduced(mine_fo, ici_fo, w_ffn_out, m_w_ffn_out, v_w_ffn_out, "adamw_w_ffn_out"),
    }
    res.update(small)
    order = ["w_ada", "b_ada", "g_mix", "w_in", "b_in", "sinks", "conv_w", "w_out", "g_ffn", "w_ffn_in", "w_ffn_out",
             "g_final"]
    outs = [loss.reshape(()), grad_x[None]]
    for k in range(4):
        outs += [res[n][k] for n in order]
    return tuple(outs)
```

```python
import functools
import math

import jax
import jax.numpy as jnp
from jax import lax
from jax.experimental import pallas as pl
from jax.experimental.pallas import tpu as pltpu

F32 = jnp.float32
BF16 = jnp.bfloat16
GRAD_STREAM = F32

D_MODEL = 1024
HEAD_DIM = 64
N_Q_HEADS = 16
N_KV_HEADS = 2
GROUP = 8
WINDOW = 128
KV_WIDTH = N_KV_HEADS * HEAD_DIM
D_FF = 2816
IN_WIDTH = 6400
N_MOD = 6
EPS = 1e-6
N_DEV = 8
REST_WIDTH = 5 * D_MODEL
KV_COL = D_MODEL + REST_WIDTH
ATTN_SCALE = HEAD_DIM ** -0.5

ADAM_LR = 0.001
ADAM_B1 = 0.9
ADAM_B2 = 0.999
ADAM_EPS = 1e-08
ADAM_WD = 0.01
ADAM_STEP = 10

LANES = 128
SUBLANES = 8
BF16_ROWS = 16
VMEM_LIMIT = 56 * 1024 * 1024
TOKEN_TILE = 512
FF_CHUNK = 256
ROW_PARTS = 2
MIN_STREAM_STEPS = 2
WGRAD_VMEM = 40 * 1024 * 1024
MESH = pl.DeviceIdType.MESH
ANY = pl.BlockSpec(memory_space=pl.ANY)

NT_DIMS = (((1,), (1,)), ((), ()))
TN_DIMS = (((0,), (0,)), ((), ()))
CHIP_FLIPS = [(0, 0), (1, 0), (0, 1), (1, 1)]


def _full(shape):
    return pl.BlockSpec(shape, lambda *_: (0,) * len(shape))


def _my_place():
    return lax.axis_index("x"), lax.axis_index("y"), lax.axis_index("c")


def _flip(v, bit):
    return 1 - v if bit else v


def _sigmoid(v):
    return 1.0 / (1.0 + jnp.exp2(v * (-1.4426950408889634)))


class _Rider:
    def __init__(self, ins, out_shapes, sem_shapes, first=None, mid=None, last=None, ins_in_vmem=False):
        self.ins, self.out_shapes, self.sem_shapes = list(ins), list(out_shapes), list(sem_shapes)
        self.in_specs = [_full(a.shape) if ins_in_vmem else ANY for a in self.ins]
        self.hooks = [(when, fn) for when, fn in (("first", first), ("mid", mid), ("last", last)) if fn is not None]


def _call(body, name, grid, args, in_specs, out_shape, out_specs, scratch=(), rider=None, aliases=None):
    n_in, n_out, n_scr = len(args), len(out_shape), len(scratch)
    r_in = rider.ins if rider else []
    r_out = rider.out_shapes if rider else []
    r_sem = rider.sem_shapes if rider else []
    nsteps = math.prod(grid)

    def full_body(*refs):
        pos = 0
        groups = []
        for size in (n_in, len(r_in), n_out, len(r_out), n_scr, len(r_sem)):
            groups.append(refs[pos:pos + size])
            pos += size
        ins, rins, outs, routs, scr, rsems = groups
        step = pl.program_id(0)
        for axis in range(1, len(grid)):
            step = step * grid[axis] + pl.program_id(axis)
        at = {"first": 0, "mid": (3 * nsteps) // 4, "last": nsteps - 1}
        hooks = rider.hooks if rider else []
        for when, fn in hooks:
            if when != "last":
                pl.when(step == at[when])(functools.partial(fn, rins, routs, rsems))
        body(*ins, *outs, *scr)
        for when, fn in hooks:
            if when == "last":
                pl.when(step == at[when])(functools.partial(fn, rins, routs, rsems))

    outs = pl.pallas_call(
        full_body, name=name, grid=grid,
        out_shape=list(out_shape) + list(r_out),
        in_specs=list(in_specs) + (rider.in_specs if rider else []),
        out_specs=list(out_specs) + [ANY] * len(r_out),
        scratch_shapes=list(scratch) + list(r_sem),
        input_output_aliases=dict(aliases or {}),
        compiler_params=pltpu.CompilerParams(dimension_semantics=("arbitrary",) * len(grid),
                                             vmem_limit_bytes=VMEM_LIMIT),
    )(*args, *r_in)
    return list(outs[:n_out]), list(outs[n_out:])


def _gather_rider(shards):
    n = len(shards)

    def setup(outs, sems):
        x, y, c = _my_place()
        send_sems, recv_sems, _ = sems
        chips = [(1 - x, y), (x, 1 - y), (1 - x, 1 - y)]

        def block(w, place):
            return outs[w].at[4 * place[0] + 2 * place[1] + place[2]]

        def copy(w, k, place, to, src=None):
            return pltpu.make_async_remote_copy(
                src_ref=block(w, place) if src is None else src, dst_ref=block(w, place),
                send_sem=send_sems.at[w, k], recv_sem=recv_sems.at[w, k], device_id=to, device_id_type=MESH)

        return (x, y, c), (x, y, 1 - c), chips, block, copy

    def first(ins, outs, sems):
        me, sibling, chips, block, copy = setup(outs, sems)
        for w in range(n):
            pltpu.make_async_copy(ins[w], block(w, me), sems[2].at[w]).start()
            copy(w, 0, me, sibling, src=ins[w]).start()
            for j, chip in enumerate(chips):
                copy(w, 1 + j, me, (*chip, me[2]), src=ins[w]).start()

    def mid(ins, outs, sems):
        me, sibling, chips, block, copy = setup(outs, sems)
        for w in range(n):
            for j, chip in enumerate(chips):
                copy(w, 1 + j, (*chip, me[2]), me).wait_recv()
                copy(w, 4 + j, (*chip, me[2]), sibling).start()

    def last(ins, outs, sems):
        me, sibling, chips, block, copy = setup(outs, sems)
        for w in range(n):
            copy(w, 0, sibling, me).wait_recv()
            for j, chip in enumerate(chips):
                copy(w, 4 + j, (*chip, 1 - me[2]), me).wait_recv()
            copy(w, 0, me, sibling, src=ins[w]).wait_send()
            for j, chip in enumerate(chips):
                copy(w, 1 + j, me, (*chip, me[2]), src=ins[w]).wait_send()
                copy(w, 4 + j, (*chip, me[2]), sibling).wait_send()
            pltpu.make_async_copy(ins[w], block(w, me), sems[2].at[w]).wait()

    return _Rider(
        shards, [jax.ShapeDtypeStruct((N_DEV,) + s.shape, BF16) for s in shards],
        [pltpu.SemaphoreType.DMA((n, N_DEV - 1)), pltpu.SemaphoreType.DMA((n, N_DEV - 1)),
         pltpu.SemaphoreType.DMA((n,))],
        first=first, mid=mid, last=last, ins_in_vmem=True)


def _sibling_rider(gblocks):
    n = len(gblocks)

    def copies(ins, outs, sems):
        x, y, c = _my_place()
        send_sems, recv_sems = sems
        made = []
        for w in range(n):
            for f, (fx, fy) in enumerate(CHIP_FLIPS):
                chip = 4 * _flip(x, fx) + 2 * _flip(y, fy)
                made.append(pltpu.make_async_remote_copy(
                    src_ref=ins[w].at[chip + 1 - c], dst_ref=outs[w].at[f], send_sem=send_sems.at[w, f],
                    recv_sem=recv_sems.at[w, f], device_id=(x, y, 1 - c), device_id_type=MESH))
        return made

    def first(ins, outs, sems):
        for cp in copies(ins, outs, sems):
            cp.start()

    def last(ins, outs, sems):
        for cp in copies(ins, outs, sems):
            cp.wait_recv()
            cp.wait_send()

    return _Rider(gblocks, [jax.ShapeDtypeStruct((4,) + g.shape[1:], BF16) for g in gblocks],
                  [pltpu.SemaphoreType.DMA((n, 4))] * 2, first=first, last=last)


def _chip_rider(sums):
    n = len(sums)

    def copies(ins, outs, sems):
        x, y, c = _my_place()
        send_sems, recv_sems = sems
        made = []
        for w in range(n):
            for f in (1, 2, 3):
                fx, fy = CHIP_FLIPS[f]
                made.append(pltpu.make_async_remote_copy(
                    src_ref=ins[w].at[f - 1], dst_ref=outs[w].at[f - 1], send_sem=send_sems.at[w, f - 1],
                    recv_sem=recv_sems.at[w, f - 1], device_id=(_flip(x, fx), _flip(y, fy), c), device_id_type=MESH))
        return made

    def first(ins, outs, sems):
        for cp in copies(ins, outs, sems):
            cp.start()

    def last(ins, outs, sems):
        for cp in copies(ins, outs, sems):
            cp.wait_recv()
            cp.wait_send()

    return _Rider(sums, [jax.ShapeDtypeStruct(s.shape, BF16) for s in sums],
                  [pltpu.SemaphoreType.DMA((n, 3))] * 2, first=first, last=last)


def _push_to_all(v_ref, out_ref, send_sems, recv_sems, local_sem, wait=True):
    x, y, c = _my_place()
    me = 4 * x + 2 * y + c
    mine = pltpu.make_async_copy(v_ref, out_ref.at[me], local_sem)
    mine.start()
    sends = []
    for k in range(1, N_DEV):
        px, py, pc = _flip(x, k & 4), _flip(y, k & 2), _flip(c, k & 1)
        cp = pltpu.make_async_remote_copy(
            src_ref=v_ref, dst_ref=out_ref.at[me], send_sem=send_sems.at[k - 1], recv_sem=recv_sems.at[k - 1],
            device_id=(px, py, pc), device_id_type=MESH)
        cp.start()
        sends.append(cp)

    def finish():
        for k in range(1, N_DEV):
            px, py, pc = _flip(x, k & 4), _flip(y, k & 2), _flip(c, k & 1)
            pltpu.make_async_remote_copy(
                src_ref=v_ref, dst_ref=out_ref.at[4 * px + 2 * py + pc], send_sem=send_sems.at[k - 1],
                recv_sem=recv_sems.at[k - 1], device_id=(px, py, pc), device_id_type=MESH).wait_recv()
        for cp in sends:
            cp.wait_send()
        mine.wait()

    if wait:
        finish()
    return finish


def _small_allgather(v, name):
    def body(v_ref, out_ref, send_sems, recv_sems, local_sem):
        _push_to_all(v_ref, out_ref, send_sems, recv_sems, local_sem)

    return pl.pallas_call(
        body, name=name,
        out_shape=jax.ShapeDtypeStruct((N_DEV,) + v.shape, F32),
        in_specs=[pl.BlockSpec(memory_space=pltpu.VMEM)],
        out_specs=pl.BlockSpec(memory_space=pltpu.VMEM),
        scratch_shapes=[pltpu.SemaphoreType.DMA((N_DEV - 1,)), pltpu.SemaphoreType.DMA((N_DEV - 1,)),
                        pltpu.SemaphoreType.DMA],
        compiler_params=pltpu.CompilerParams(vmem_limit_bytes=VMEM_LIMIT),
    )(v)


def _gather_first_weight(shard, others, cond_rows, w_ada, b_cols):
    n = len(others)
    ada_cols = w_ada.shape[1]
    c_rows = D_MODEL // LANES

    def body(*refs):
        w_ref, other_refs = refs[0], refs[1:1 + n]
        cond_ref, wada_ref, bcols_ref = refs[1 + n:4 + n]
        out_ref, cast_refs = refs[4 + n], refs[5 + n:5 + 2 * n]
        cond_all_ref, mod_all_ref = refs[5 + 2 * n:7 + 2 * n]
        mine_ref, mod_ref, send_sems, recv_sems, local_sem, small_send, small_recv, small_local = refs[7 + 2 * n:]
        x, y, c = _my_place()
        me, sibling = (x, y, c), (x, y, 1 - c)
        xnb, ynb, diag = (1 - x, y), (x, 1 - y), (1 - x, 1 - y)
        half = shard.shape[0] // 2

        def block(place, part=None):
            ref = out_ref.at[4 * place[0] + 2 * place[1] + place[2]]
            return ref if part is None else ref.at[pl.ds(part * half, half)]

        def copy(k, place, to, part=None, src=None):
            return pltpu.make_async_remote_copy(
                src_ref=block(place, part) if src is None else src, dst_ref=block(place, part),
                send_sem=send_sems.at[k], recv_sem=recv_sems.at[k], device_id=to, device_id_type=MESH)

        finish_cond = _push_to_all(cond_ref, cond_all_ref, small_send.at[0], small_recv.at[0], small_local.at[0],
                                   wait=False)
        mine_ref[...] = w_ref[...].astype(BF16)
        local = pltpu.make_async_copy(mine_ref, block(me), local_sem)
        local.start()
        started = [copy(0, me, sibling, src=mine_ref), copy(1, me, (*xnb, c), src=mine_ref),
                   copy(2, me, (*ynb, c), src=mine_ref)]
        for cp in started:
            cp.start()
        finish_cond()
        mod = jnp.zeros((N_DEV, ada_cols), F32) + bcols_ref[...]
        for r in range(c_rows):
            cf = cond_all_ref[:, r, :]
            act = (cf * _sigmoid(cf)).astype(BF16)
            mod = mod + jnp.dot(act, wada_ref[r * LANES:(r + 1) * LANES, :].astype(BF16),
                                preferred_element_type=F32)
        mod_ref[...] = mod
        finish_mod = _push_to_all(mod_ref, mod_all_ref, small_send.at[1], small_recv.at[1], small_local.at[1],
                                  wait=False)
        for o_ref, c_ref in zip(other_refs, cast_refs):
            c_ref[...] = o_ref[...].astype(BF16)
        def start(cp):
            cp.start()
            started.append(cp)

        copy(1, (*xnb, c), me).wait_recv()
        start(copy(3, (*xnb, c), (*ynb, c), part=0))
        start(copy(5, (*xnb, c), sibling))
        copy(2, (*ynb, c), me).wait_recv()
        start(copy(4, (*ynb, c), (*xnb, c), part=1))
        start(copy(6, (*ynb, c), sibling))
        copy(3, (*diag, c), me, part=0).wait_recv()
        start(copy(7, (*diag, c), sibling, part=0))
        copy(4, (*diag, c), me, part=1).wait_recv()
        start(copy(8, (*diag, c), sibling, part=1))
        copy(0, sibling, me).wait_recv()
        copy(5, (*xnb, 1 - c), me).wait_recv()
        copy(6, (*ynb, 1 - c), me).wait_recv()
        copy(7, (*diag, 1 - c), me, part=0).wait_recv()
        copy(8, (*diag, 1 - c), me, part=1).wait_recv()
        finish_mod()
        for cp in started:
            cp.wait_send()
        local.wait()

    vmem = pl.BlockSpec(memory_space=pltpu.VMEM)
    outs = pl.pallas_call(
        body, name="gather_w_in",
        out_shape=[jax.ShapeDtypeStruct((N_DEV,) + shard.shape, BF16)]
        + [jax.ShapeDtypeStruct(o.shape, BF16) for o in others]
        + [jax.ShapeDtypeStruct((N_DEV,) + cond_rows.shape, F32), jax.ShapeDtypeStruct((N_DEV, N_DEV, ada_cols), F32)],
        in_specs=[vmem] * (4 + n),
        out_specs=[ANY] + [vmem] * (n + 2),
        scratch_shapes=[pltpu.VMEM(shard.shape, BF16), pltpu.VMEM((N_DEV, ada_cols), F32),
                        pltpu.SemaphoreType.DMA((9,)), pltpu.SemaphoreType.DMA((9,)),
                        pltpu.SemaphoreType.DMA,
                        pltpu.SemaphoreType.DMA((2, N_DEV - 1)), pltpu.SemaphoreType.DMA((2, N_DEV - 1)),
                        pltpu.SemaphoreType.DMA((2,))],
        compiler_params=pltpu.CompilerParams(vmem_limit_bytes=VMEM_LIMIT),
    )(shard, *others, cond_rows, w_ada, b_cols)
    return outs[0], list(outs[1:1 + n]), outs[1 + n], outs[2 + n]


def _sibling_exchange_sum(gblocks, name):
    _, r, cdim = gblocks.shape

    def body(g_ref, sums_ref, mine_ref, own_buf, sib_buf, own_sems, send_sems, recv_sems):
        x, y, c = _my_place()
        pairs = []
        for f, (fx, fy) in enumerate(CHIP_FLIPS):
            chip = 4 * _flip(x, fx) + 2 * _flip(y, fy)
            own = pltpu.make_async_copy(g_ref.at[chip + c], own_buf.at[f], own_sems.at[f])
            own.start()
            remote = pltpu.make_async_remote_copy(
                src_ref=g_ref.at[chip + 1 - c], dst_ref=sib_buf.at[f], send_sem=send_sems.at[f],
                recv_sem=recv_sems.at[f], device_id=(x, y, 1 - c), device_id_type=MESH)
            remote.start()
            pairs.append((own, remote))
        for f in (1, 2, 3, 0):
            own, remote = pairs[f]
            own.wait()
            remote.wait_recv()
            total = own_buf[f].astype(F32) + sib_buf[f].astype(F32)
            if f == 0:
                mine_ref[...] = total
            else:
                sums_ref[f - 1] = total.astype(BF16)
        for _, remote in pairs:
            remote.wait_send()

    vmem = pl.BlockSpec(memory_space=pltpu.VMEM)
    return pl.pallas_call(
        body, name=name,
        out_shape=[jax.ShapeDtypeStruct((3, r, cdim), BF16), jax.ShapeDtypeStruct((r, cdim), F32)],
        in_specs=[ANY], out_specs=[vmem, vmem],
        scratch_shapes=[pltpu.VMEM((4, r, cdim), BF16), pltpu.VMEM((4, r, cdim), BF16),
                        pltpu.SemaphoreType.DMA((4,)), pltpu.SemaphoreType.DMA((4,)), pltpu.SemaphoreType.DMA((4,))],
        compiler_params=pltpu.CompilerParams(vmem_limit_bytes=VMEM_LIMIT),
    )(gblocks)


def _ada_weight_grad(c_all, dmod_cols):
    cols = dmod_cols.shape[1]

    def body(c_ref, d_ref, out_ref):
        cf = c_ref[...]
        act = (cf * _sigmoid(cf)).astype(BF16)
        out_ref[...] = lax.dot_general(act, d_ref[...].astype(BF16), TN_DIMS, preferred_element_type=F32)

    return pl.pallas_call(
        body, name="ada_weight_grad",
        out_shape=jax.ShapeDtypeStruct((D_MODEL, cols), F32),
        in_specs=[pl.BlockSpec(memory_space=pltpu.VMEM)] * 2,
        out_specs=pl.BlockSpec(memory_space=pltpu.VMEM),
        compiler_params=pltpu.CompilerParams(vmem_limit_bytes=VMEM_LIMIT),
    )(c_all, dmod_cols)


PACK_ROWS = 24
PACK_DMOD = 0
PACK_PARAMS = {"g_mix": (6, D_MODEL), "b_in": (7, IN_WIDTH), "g_ffn": (14, D_MODEL), "g_final": (15, D_MODEL),
               "sinks": (19, N_Q_HEADS)}
PACK_CONV = 16
PACK_SQERR = 20


def _small_finalize(packed_all, params):
    names = ["b_ada"] + list(PACK_PARAMS)
    layout = dict(PACK_PARAMS, b_ada=(PACK_DMOD, N_MOD * D_MODEL))
    n = len(names)

    def body(*refs):
        p_ref = refs[0]
        ins = refs[1:1 + 3 * n]
        outs = refs[1 + 3 * n:1 + 7 * n]
        conv_ref, loss_ref = refs[1 + 7 * n:]
        total = p_ref[0]
        for d in range(1, N_DEV):
            total = total + p_ref[d]
        for k, name in enumerate(names):
            row0, width = layout[name]
            w_ref, m_ref, v_ref = ins[3 * k:3 * k + 3]
            g_ref, d_ref, nm_ref, nv_ref = outs[4 * k:4 * k + 4]
            for chunk in range(-(-width // D_MODEL)):
                lo = chunk * D_MODEL
                hi = min(lo + D_MODEL, width)
                g = total[row0 + chunk:row0 + chunk + 1, :hi - lo]
                g_ref[:, lo:hi] = g
                d_ref[:, lo:hi], nm_ref[:, lo:hi], nv_ref[:, lo:hi] = _adamw_update(
                    w_ref[:, lo:hi], g, m_ref[:, lo:hi], v_ref[:, lo:hi])
        conv_ref[...] = total[PACK_CONV:PACK_CONV + 3, :]
        loss_ref[...] = (0.5 / D_MODEL) * jnp.sum(total[PACK_SQERR:PACK_SQERR + 1, :], keepdims=True)

    vmem = pl.BlockSpec(memory_space=pltpu.VMEM)
    flat = [a for name in names for a in params[name]]
    out_shape = [jax.ShapeDtypeStruct(params[name][0].shape, F32) for name in names for _ in range(4)]
    outs = pl.pallas_call(
        body, name="small_finalize",
        out_shape=out_shape + [jax.ShapeDtypeStruct((3, D_MODEL), F32), jax.ShapeDtypeStruct((1, 1), F32)],
        in_specs=[vmem] * (1 + 3 * n),
        out_specs=[vmem] * (4 * n + 2),
        compiler_params=pltpu.CompilerParams(vmem_limit_bytes=VMEM_LIMIT),
    )(packed_all, *flat)
    return {name: tuple(outs[4 * k:4 * k + 4]) for k, name in enumerate(names)}, outs[4 * n], outs[4 * n + 1]


def _row_tile(rows, multiple):
    for cand in range(rows // MIN_STREAM_STEPS, 0, -1):
        if rows % cand == 0 and cand % multiple == 0:
            return cand
    return rows


def _adamw_update(w, g, m, v):
    c1 = 1.0 / (1.0 - ADAM_B1 ** ADAM_STEP)
    c2 = 1.0 / (1.0 - ADAM_B2 ** ADAM_STEP)
    nm = ADAM_B1 * m + (1.0 - ADAM_B1) * g
    nv = ADAM_B2 * v + (1.0 - ADAM_B2) * (g * g)
    delta = -ADAM_LR * ((nm * c1) / (jnp.sqrt(nv * c2) + ADAM_EPS) + ADAM_WD * w)
    return delta, nm, nv


def _adamw(w, g, m, v, name):
    rows, cols = w.shape
    tile = _row_tile(rows, SUBLANES)

    def body(w_ref, g_ref, m_ref, v_ref, d_ref, nm_ref, nv_ref):
        d_ref[...], nm_ref[...], nv_ref[...] = _adamw_update(w_ref[...], g_ref[...], m_ref[...], v_ref[...])

    spec = pl.BlockSpec((tile, cols), lambda i: (i, 0))
    outs, _ = _call(body, name, (rows // tile,), [w, g, m, v], [spec] * 4,
                    [jax.ShapeDtypeStruct((rows, cols), F32)] * 3, [spec] * 3)
    return outs


def _sibling_sum(gblocks, sib, name):
    _, r, cdim = gblocks.shape
    tile = _row_tile(r, BF16_ROWS)
    x, y, c = _my_place()
    table = jnp.stack([4 * _flip(x, fx) + 2 * _flip(y, fy) + c for fx, fy in CHIP_FLIPS]).astype(jnp.int32)

    def body(table_ref, own0, own1, own2, own3, sib_ref, sums_ref, mine_ref):
        mine_ref[...] = own0[...].astype(F32) + sib_ref[0].astype(F32)
        for f, own in ((1, own1), (2, own2), (3, own3)):
            sums_ref[f - 1] = (own[...].astype(F32) + sib_ref[f].astype(F32)).astype(BF16)

    own_specs = [pl.BlockSpec((None, tile, cdim), functools.partial(lambda i, tab, f: (tab[f], i, 0), f=f))
                 for f in range(4)]
    return pl.pallas_call(
        body, name=name,
        grid_spec=pltpu.PrefetchScalarGridSpec(
            num_scalar_prefetch=1, grid=(r // tile,),
            in_specs=own_specs + [pl.BlockSpec((4, tile, cdim), lambda i, tab: (0, i, 0))],
            out_specs=[pl.BlockSpec((3, tile, cdim), lambda i, tab: (0, i, 0)),
                       pl.BlockSpec((tile, cdim), lambda i, tab: (i, 0))]),
        out_shape=[jax.ShapeDtypeStruct((3, r, cdim), BF16), jax.ShapeDtypeStruct((r, cdim), F32)],
        compiler_params=pltpu.CompilerParams(dimension_semantics=("arbitrary",), vmem_limit_bytes=VMEM_LIMIT),
    )(table, gblocks, gblocks, gblocks, gblocks, sib)


def _chip_sum_adamw(mine, ici, w, m, v, name):
    r, cdim = mine.shape
    tile = _row_tile(r, BF16_ROWS)

    def body(mine_ref, ici_ref, w_ref, m_ref, v_ref, g_ref, d_ref, nm_ref, nv_ref):
        g = mine_ref[...]
        for f in range(3):
            g = g + ici_ref[f].astype(F32)
        g_ref[...] = g
        d_ref[...], nm_ref[...], nv_ref[...] = _adamw_update(w_ref[...], g, m_ref[...], v_ref[...])

    spec = pl.BlockSpec((tile, cdim), lambda i: (i, 0))
    outs, _ = _call(
        body, name, (r // tile,), [mine, ici, w, m, v],
        [spec, pl.BlockSpec((3, tile, cdim), lambda i: (0, i, 0)), spec, spec, spec],
        [jax.ShapeDtypeStruct((r, cdim), F32)] * 4, [spec] * 4)
    return outs


REF_KV_COL = D_MODEL
REF_REST_COL = D_MODEL + 2 * KV_WIDTH
IN_CHUNK = 1280
IN_PIECES = ([(0, 0, D_MODEL)]
             + [(D_MODEL + n * IN_CHUNK, REF_REST_COL + n * IN_CHUNK, IN_CHUNK) for n in range(REST_WIDTH // IN_CHUNK)]
             + [(KV_COL, REF_KV_COL, 2 * KV_WIDTH)])


def _inproj_fwd(x, vec, w_t, b_in, rider):
    t = x.shape[0]
    tm = min(TOKEN_TILE, t)

    def body(x_ref, vec_ref, w_ref, b_ref, z_ref, h_ref):
        xf = x_ref[...]
        r = lax.rsqrt(jnp.mean(xf * xf, axis=-1, keepdims=True) + EPS)
        h = (xf * r) * (vec_ref[0:1, :] * (1.0 + vec_ref[1:2, :])) + vec_ref[2:3, :]
        hb = h.astype(BF16)
        h_ref[...] = hb
        for mine, ref, width in IN_PIECES:
            zc = lax.dot_general(hb, w_ref[ref:ref + width, :], NT_DIMS, preferred_element_type=F32)
            z_ref[:, mine:mine + width] = (zc + b_ref[:, ref:ref + width]).astype(BF16)

    return _call(
        body, "inproj_fwd", (t // tm,), [x, vec, w_t, b_in],
        [pl.BlockSpec((tm, D_MODEL), lambda i: (i, 0)), _full((SUBLANES, D_MODEL)),
         _full((IN_WIDTH, D_MODEL)), _full((1, IN_WIDTH))],
        [jax.ShapeDtypeStruct((t, IN_WIDTH), BF16), jax.ShapeDtypeStruct((t, D_MODEL), BF16)],
        [pl.BlockSpec((tm, IN_WIDTH), lambda i: (i, 0)), pl.BlockSpec((tm, D_MODEL), lambda i: (i, 0))],
        rider=rider)


PAIRS = GROUP // 2
STACK = PAIRS * WINDOW


ATTN_BLOCKS = 4
ATTN_BWD_BLOCKS = 1
LOG2E = 1.4426950408889634
LN2 = 0.6931471805599453
SCORE_SCALE = ATTN_SCALE * LOG2E


def _fill_window_bias(bias_ref):
    shape = bias_ref.shape[1:]
    kj = lax.broadcasted_iota(jnp.int32, shape, 0)
    qi = jnp.bitwise_and(lax.broadcasted_iota(jnp.int32, shape, 1), WINDOW - 1)
    in_prev = jnp.logical_and(kj < WINDOW, kj > qi)
    in_cur = jnp.logical_and(kj >= WINDOW, (kj - WINDOW) <= qi)
    bias_ref[0] = jnp.where(in_cur, 0.0, -jnp.inf)
    bias_ref[1] = jnp.where(jnp.logical_or(in_prev, in_cur), 0.0, -jnp.inf)


def _half_tiles(tile):
    low = lax.broadcasted_iota(jnp.int32, tile.shape, 1) < HEAD_DIM
    swapped = jnp.concatenate([tile[:, HEAD_DIM:], tile[:, :HEAD_DIM]], axis=1)
    zero = jnp.zeros_like(tile)
    return ((jnp.where(low, tile, zero), jnp.where(low, zero, swapped)),
            (jnp.where(low, swapped, zero), jnp.where(low, zero, tile)))


def _stack_pairs(ref, row0, j):
    return jnp.concatenate(
        [ref[pl.ds(row0, WINDOW), (j * PAIRS + p) * LANES:(j * PAIRS + p + 1) * LANES] for p in range(PAIRS)], axis=0)


def _per_pair_row(values):
    pair = lax.broadcasted_iota(jnp.int32, (1, STACK), 1) // WINDOW
    row = jnp.full((1, STACK), values[PAIRS - 1], F32)
    for p in range(PAIRS - 2, -1, -1):
        row = jnp.where(pair == p, values[p], row)
    return row


def _attn_fwd(z, sinks, rider):
    t = z.shape[0]
    tq = min(TOKEN_TILE, t)
    nblk = tq // WINDOW

    def body(q_ref, kv_ref, sink_ref, o_ref, lse_ref, bias_ref):
        i = pl.program_id(0)

        @pl.when(i == 0)
        def _():
            _fill_window_bias(bias_ref)

        def window(b):
            row0 = pl.multiple_of(b * WINDOW, WINDOW)
            start = i * tq + b * WINDOW
            prev = pl.multiple_of(jnp.maximum(start - WINDOW, 0), WINDOW)
            cur = pl.multiple_of(start, WINDOW)
            kvw = jnp.concatenate([kv_ref[pl.ds(prev, WINDOW), :], kv_ref[pl.ds(cur, WINDOW), :]], axis=0)
            return row0, _half_tiles(kvw[:, :KV_WIDTH]), _half_tiles(kvw[:, KV_WIDTH:]), bias_ref[jnp.minimum(start, 1)]

        def block_group(bb, carry):
            windows = [window(bb * ATTN_BLOCKS + n) for n in range(ATTN_BLOCKS)]
            for j in range(N_KV_HEADS):
                for pr in range(PAIRS):
                    cols = slice((j * PAIRS + pr) * LANES, (j * PAIRS + pr + 1) * LANES)
                    o_ts = [jnp.zeros((LANES, WINDOW), F32) for _ in windows]
                    for parity in range(2):
                        h = j * GROUP + 2 * pr + parity
                        sink = sink_ref[h] * LOG2E
                        for n, (row0, k_halves, v_halves, bias) in enumerate(windows):
                            qp = q_ref[pl.ds(row0, WINDOW), cols]
                            s = lax.dot_general(k_halves[j][parity], qp, NT_DIMS, preferred_element_type=F32)
                            s = s * SCORE_SCALE + bias
                            m = jnp.maximum(jnp.max(s, axis=0, keepdims=True), sink)
                            p = jnp.exp2(s - m)
                            denom = jnp.sum(p, axis=0, keepdims=True) + jnp.exp2(sink - m)
                            pv = lax.dot_general(v_halves[j][parity], p.astype(BF16), TN_DIMS,
                                                 preferred_element_type=F32)
                            o_ts[n] = o_ts[n] + pv * (1.0 / denom)
                            lse_ref[h:h + 1, pl.ds(row0, WINDOW)] = m + jnp.log2(denom)
                    for n, (row0, _, _, _) in enumerate(windows):
                        o_ref[pl.ds(row0, WINDOW), cols] = jnp.transpose(o_ts[n].astype(BF16))
            return carry

        lax.fori_loop(0, nblk // ATTN_BLOCKS, block_group, 0)

    return _call(
        body, "attn_fwd", (t // tq,), [z, z, sinks],
        [pl.BlockSpec((tq, D_MODEL), lambda i: (i, 0)),
         pl.BlockSpec((t, 2 * KV_WIDTH), lambda i: (0, KV_COL // (2 * KV_WIDTH))),
         pl.BlockSpec(memory_space=pltpu.SMEM)],
        [jax.ShapeDtypeStruct((t, D_MODEL), BF16), jax.ShapeDtypeStruct((N_Q_HEADS, t), F32)],
        [pl.BlockSpec((tq, D_MODEL), lambda i: (i, 0)), pl.BlockSpec((N_Q_HEADS, tq), lambda i: (0, i))],
        scratch=[pltpu.VMEM((2, 2 * WINDOW, WINDOW), F32)], rider=rider)


HALO = BF16_ROWS


def _shift_down(u, uh, k):
    rolled = pltpu.roll(u, k, 0)
    row = lax.broadcasted_iota(jnp.int32, (SUBLANES, u.shape[1]), 0)
    top = rolled[:SUBLANES, :]
    for j in range(k):
        top = jnp.where(row == j, uh[HALO - k + j:HALO - k + j + 1, :], top)
    return jnp.concatenate([top, rolled[SUBLANES:, :]], axis=0)


def _shift_up(u, nxt, k):
    n = u.shape[0]
    rolled = pltpu.roll(u, n - k, 0)
    row = lax.broadcasted_iota(jnp.int32, (SUBLANES, u.shape[1]), 0)
    bottom = rolled[n - SUBLANES:, :]
    for j in range(k):
        bottom = jnp.where(row == SUBLANES - k + j, nxt[j:j + 1, :], bottom)
    return jnp.concatenate([rolled[:n - SUBLANES, :], bottom], axis=0)


def _conv_inputs(cc_ref, cx_ref, hc_ref, hx_ref, first_tile):
    cc = cc_ref[...].astype(F32)
    cx = cx_ref[...].astype(F32)
    u = cc * cx
    uh = jnp.where(first_tile, 0.0, hc_ref[...].astype(F32) * hx_ref[...].astype(F32))
    return cc, cx, u, _shift_down(u, uh, 1), _shift_down(u, uh, 2)


def _z_specs(tm, order):
    per_tile = tm // HALO
    cols = [pl.BlockSpec((tm, D_MODEL), functools.partial(lambda i, j: (order(i), j), j=j)) for j in range(1, 6)]
    halos = [pl.BlockSpec((HALO, D_MODEL),
                          functools.partial(lambda i, j: (jnp.maximum(order(i) * per_tile - 1, 0), j), j=j))
             for j in (2, 3)]
    return cols + halos


def _mix_fwd(x, attn, z, vec, w_out):
    t = x.shape[0]
    tm = min(TOKEN_TILE, t)

    def body(x_ref, a_ref, cb_ref, cc_ref, cx_ref, ga_ref, gc_ref, hc_ref, hx_ref, vec_ref, w_ref,
             m_ref, x2_ref, h2_ref, o_ref):
        i = pl.program_id(0)
        _, _, u, u1, u2 = _conv_inputs(cc_ref, cx_ref, hc_ref, hx_ref, i == 0)
        cv = vec_ref[4:5, :] * u2 + vec_ref[5:6, :] * u1 + vec_ref[6:7, :] * u
        conv = cb_ref[...].astype(F32) * cv
        merged = (_sigmoid(ga_ref[...].astype(F32)) * a_ref[...].astype(F32)
                  + _sigmoid(gc_ref[...].astype(F32)) * conv)
        mb = merged.astype(BF16)
        m_ref[...] = mb
        o = jnp.dot(mb, w_ref[...], preferred_element_type=F32)
        o_ref[...] = o.astype(BF16)
        x2 = x_ref[...] + vec_ref[0:1, :] * o
        x2_ref[...] = x2
        r = lax.rsqrt(jnp.mean(x2 * x2, axis=-1, keepdims=True) + EPS)
        h2 = (x2 * r) * (vec_ref[1:2, :] * (1.0 + vec_ref[2:3, :])) + vec_ref[3:4, :]
        h2_ref[...] = h2.astype(BF16)

    tok = pl.BlockSpec((tm, D_MODEL), lambda i: (i, 0))
    outs, _ = _call(
        body, "mix_fwd", (t // tm,), [x, attn, z, z, z, z, z, z, z, vec, w_out],
        [tok, tok] + _z_specs(tm, lambda i: i) + [_full((SUBLANES, D_MODEL)), _full((D_MODEL, D_MODEL))],
        [jax.ShapeDtypeStruct((t, D_MODEL), BF16), jax.ShapeDtypeStruct((t, D_MODEL), F32),
         jax.ShapeDtypeStruct((t, D_MODEL), BF16), jax.ShapeDtypeStruct((t, D_MODEL), BF16)],
        [tok, tok, tok, tok])
    return outs


def _ffn_fwd(h2, w_t):
    t = h2.shape[0]
    tm = min(TOKEN_TILE, t)

    def body(h_ref, w_ref, gu_ref, a_ref):
        hb = h_ref[...]
        for n in range(D_FF // FF_CHUNK):
            lo, hi = n * FF_CHUNK, (n + 1) * FF_CHUNK
            g = lax.dot_general(hb, w_ref[lo:hi, :], NT_DIMS, preferred_element_type=F32)
            u = lax.dot_general(hb, w_ref[D_FF + lo:D_FF + hi, :], NT_DIMS, preferred_element_type=F32)
            sg = _sigmoid(g)
            silu = g * sg
            gu_ref[:, lo:hi] = (u * (sg + silu * (1.0 - sg))).astype(BF16)
            gu_ref[:, D_FF + lo:D_FF + hi] = silu.astype(BF16)
            a_ref[:, lo:hi] = (silu * u).astype(BF16)

    outs, _ = _call(
        body, "ffn_fwd", (t // tm,), [h2, w_t],
        [pl.BlockSpec((tm, D_MODEL), lambda i: (i, 0)), _full((2 * D_FF, D_MODEL))],
        [jax.ShapeDtypeStruct((t, 2 * D_FF), BF16), jax.ShapeDtypeStruct((t, D_FF), BF16)],
        [pl.BlockSpec((tm, 2 * D_FF), lambda i: (i, 0)), pl.BlockSpec((tm, D_FF), lambda i: (i, 0))])
    return outs


def _ffn_out_loss(a, gu, x2, target, vec, w_ffn_out):
    t = a.shape[0]
    tm = min(TOKEN_TILE, t)

    def body(a_ref, gu_ref, x2_ref, t_ref, vec_ref, w_ref, dx3_ref, df_ref, dgu_ref, acc_ref):
        @pl.when(pl.program_id(0) == 0)
        def _():
            acc_ref[...] = jnp.zeros_like(acc_ref)

        ga2 = vec_ref[0:1, :]
        gf = vec_ref[1:2, :]
        parts = min(ROW_PARTS, tm // LANES)
        part_rows = [slice(n * (tm // parts), (n + 1) * (tm // parts)) for n in range(parts)]

        def head(rows, f):
            x3 = x2_ref[rows, :] + ga2 * f
            r = lax.rsqrt(jnp.mean(x3 * x3, axis=-1, keepdims=True) + EPS)
            xn = x3 * r
            err = xn * gf - t_ref[rows, :]
            dxn = err * (gf * (1.0 / D_MODEL))
            dx3 = r * (dxn - xn * jnp.mean(dxn * xn, axis=-1, keepdims=True))
            dx3_ref[rows, :] = dx3.astype(GRAD_STREAM)
            sums = (jnp.sum(err * err, axis=0, keepdims=True),
                    jnp.sum(err * xn, axis=0, keepdims=True) * (1.0 / D_MODEL),
                    jnp.sum(dx3 * f, axis=0, keepdims=True))
            df = (dx3 * ga2).astype(BF16)
            df_ref[rows, :] = df
            return df, sums

        def tail(rows, df):
            for n in range(D_FF // FF_CHUNK):
                lo, hi = n * FF_CHUNK, (n + 1) * FF_CHUNK
                da = lax.dot_general(df, w_ref[lo:hi, :], NT_DIMS, preferred_element_type=F32)
                dgu_ref[rows, lo:hi] = (da * gu_ref[rows, lo:hi].astype(F32)).astype(BF16)
                dgu_ref[rows, D_FF + lo:D_FF + hi] = (da * gu_ref[rows, D_FF + lo:D_FF + hi].astype(F32)).astype(BF16)

        fs = [jnp.dot(a_ref[rows, :], w_ref[...], preferred_element_type=F32) for rows in part_rows]
        heads = [head(rows, f) for rows, f in zip(part_rows, fs)]
        for rows, (df, _) in zip(part_rows, heads):
            tail(rows, df)
        for k in range(3):
            total = heads[0][1][k]
            for _, sums in heads[1:]:
                total = total + sums[k]
            acc_ref[k:k + 1, :] += total

    tok = pl.BlockSpec((tm, D_MODEL), lambda i: (i, 0))
    outs, _ = _call(
        body, "ffn_out_loss", (t // tm,), [a, gu, x2, target, vec, w_ffn_out],
        [pl.BlockSpec((tm, D_FF), lambda i: (i, 0)), pl.BlockSpec((tm, 2 * D_FF), lambda i: (i, 0)),
         tok, tok, _full((SUBLANES, D_MODEL)), _full((D_FF, D_MODEL))],
        [jax.ShapeDtypeStruct((t, D_MODEL), GRAD_STREAM), jax.ShapeDtypeStruct((t, D_MODEL), BF16),
         jax.ShapeDtypeStruct((t, 2 * D_FF), BF16), jax.ShapeDtypeStruct((SUBLANES, D_MODEL), F32)],
        [tok, tok, pl.BlockSpec((tm, 2 * D_FF), lambda i: (i, 0)), _full((SUBLANES, D_MODEL))])
    return outs


def _ffn_in_bwd(dgu, x2, dx3, vec, w_t, rider):
    t = x2.shape[0]
    tm = min(TOKEN_TILE, t)

    def body(dgu_ref, x2_ref, dx3_ref, vec_ref, wf_ref, dx2_ref, acc_ref):
        @pl.when(pl.program_id(0) == 0)
        def _():
            acc_ref[...] = jnp.zeros_like(acc_ref)

        gffn = vec_ref[0:1, :]
        sc2 = vec_ref[1:2, :]
        parts = min(ROW_PARTS, tm // LANES)
        part_rows = [slice(n * (tm // parts), (n + 1) * (tm // parts)) for n in range(parts)]
        dhs = [jnp.dot(dgu_ref[rows, :], wf_ref[...], preferred_element_type=F32) for rows in part_rows]
        gs = gffn * (1.0 + sc2)
        sum_dh = jnp.zeros((1, D_MODEL), F32)
        sum_dh_xn = jnp.zeros((1, D_MODEL), F32)
        for rows, dh2 in zip(part_rows, dhs):
            x2 = x2_ref[rows, :]
            r = lax.rsqrt(jnp.mean(x2 * x2, axis=-1, keepdims=True) + EPS)
            xn = x2 * r
            dh_xn = dh2 * xn
            sum_dh = sum_dh + jnp.sum(dh2, axis=0, keepdims=True)
            sum_dh_xn = sum_dh_xn + jnp.sum(dh_xn, axis=0, keepdims=True)
            dx2 = dx3_ref[rows, :].astype(F32) + r * (dh2 * gs - xn * jnp.mean(dh_xn * gs, axis=-1, keepdims=True))
            dx2_ref[rows, :] = dx2.astype(GRAD_STREAM)
        acc_ref[0:1, :] += sum_dh
        acc_ref[1:2, :] += sum_dh_xn * gffn
        acc_ref[2:3, :] += sum_dh_xn * (1.0 + sc2)

    tok = pl.BlockSpec((tm, D_MODEL), lambda i: (i, 0))
    return _call(
        body, "ffn_in_bwd", (t // tm,), [dgu, x2, dx3, vec, w_t],
        [pl.BlockSpec((tm, 2 * D_FF), lambda i: (i, 0)), tok, tok, _full((SUBLANES, D_MODEL)),
         _full((2 * D_FF, D_MODEL))],
        [jax.ShapeDtypeStruct((t, D_MODEL), GRAD_STREAM), jax.ShapeDtypeStruct((SUBLANES, D_MODEL), F32)],
        [tok, _full((SUBLANES, D_MODEL))], rider=rider)


def _mix_bwd(dx2, oproj, attn, z, vec, w_out, rider):
    t = dx2.shape[0]
    tm = min(TOKEN_TILE, t)
    nt = t // tm
    rev = lambda i: nt - 1 - i

    def body(dx2_ref, m_ref, a_ref, cb_ref, cc_ref, cx_ref, ga_ref, gc_ref, hc_ref, hx_ref,
             vec_ref, wo_ref, do_ref, da_ref, dr_ref, acc_ref, carry_ref):
        i = pl.program_id(0)

        @pl.when(i == 0)
        def _():
            acc_ref[...] = jnp.zeros_like(acc_ref)
            carry_ref[...] = jnp.zeros_like(carry_ref)

        ga1 = vec_ref[0:1, :]
        w0, w1, w2 = vec_ref[1:2, :], vec_ref[2:3, :], vec_ref[3:4, :]
        dx2 = dx2_ref[...].astype(F32)
        acc_ref[0:1, :] += jnp.sum(dx2 * m_ref[...].astype(F32), axis=0, keepdims=True)
        do = (dx2 * ga1).astype(BF16)
        do_ref[...] = do
        dm = lax.dot_general(do, wo_ref[...], NT_DIMS, preferred_element_type=F32)

        cc, cx, u, u1, u2 = _conv_inputs(cc_ref, cx_ref, hc_ref, hx_ref, i == nt - 1)
        cv = w0 * u2 + w1 * u1 + w2 * u
        cb = cb_ref[...].astype(F32)
        sa = _sigmoid(ga_ref[...].astype(F32))
        sc = _sigmoid(gc_ref[...].astype(F32))
        attn = a_ref[...].astype(F32)
        dattn = dm * sa
        da_ref[...] = dattn.astype(BF16)
        dconv = dm * sc
        dconv_b = dconv * cv
        dr_ref[:, 3 * D_MODEL:4 * D_MODEL] = (dattn * attn * (1.0 - sa)).astype(BF16)
        dr_ref[:, 4 * D_MODEL:5 * D_MODEL] = (dconv_b * cb * (1.0 - sc)).astype(BF16)
        dr_ref[:, 0:D_MODEL] = dconv_b.astype(BF16)
        dcv = dconv * cb
        acc_ref[1:2, :] += jnp.sum(dcv * u2, axis=0, keepdims=True)
        acc_ref[2:3, :] += jnp.sum(dcv * u1, axis=0, keepdims=True)
        acc_ref[3:4, :] += jnp.sum(dcv * u, axis=0, keepdims=True)
        nxt = carry_ref[...]
        du = w2 * dcv + w1 * _shift_up(dcv, nxt, 1) + w0 * _shift_up(dcv, nxt, 2)
        carry_ref[...] = dcv[0:SUBLANES, :]
        dr_ref[:, D_MODEL:2 * D_MODEL] = (du * cx).astype(BF16)
        dr_ref[:, 2 * D_MODEL:3 * D_MODEL] = (du * cc).astype(BF16)

    tok = pl.BlockSpec((tm, D_MODEL), lambda i: (rev(i), 0))
    return _call(
        body, "mix_bwd", (nt,), [dx2, oproj, attn, z, z, z, z, z, z, z, vec, w_out],
        [tok, tok, tok] + _z_specs(tm, rev) + [_full((SUBLANES, D_MODEL)), _full((D_MODEL, D_MODEL))],
        [jax.ShapeDtypeStruct((t, D_MODEL), BF16), jax.ShapeDtypeStruct((t, D_MODEL), BF16),
         jax.ShapeDtypeStruct((t, REST_WIDTH), BF16), jax.ShapeDtypeStruct((SUBLANES, D_MODEL), F32)],
        [tok, tok, pl.BlockSpec((tm, REST_WIDTH), lambda i: (rev(i), 0)), _full((SUBLANES, D_MODEL))],
        scratch=[pltpu.VMEM((SUBLANES, D_MODEL), F32)], rider=rider)


def _attn_bwd(z, dattn, attn, lse, sinks, rider):
    t = z.shape[0]
    tq = min(TOKEN_TILE, t)
    nblk = tq // WINDOW
    nt = t // tq

    def body(q_ref, kv_ref, do_ref, o_ref, lse_ref, sink_ref, dq_ref, dkv_ref, ds_ref, acc_ref, bias_ref):
        i = pl.program_id(0)

        @pl.when(i == 0)
        def _():
            acc_ref[...] = jnp.zeros_like(acc_ref)
            ds_ref[...] = jnp.zeros_like(ds_ref)
            _fill_window_bias(bias_ref)

        lane = lax.broadcasted_iota(jnp.int32, (1, LANES), 1)
        ind_row = lax.broadcasted_iota(jnp.int32, (SUBLANES, LANES), 0)
        ind_low = lax.broadcasted_iota(jnp.int32, (SUBLANES, LANES), 1) < HEAD_DIM
        indicator = jnp.where(jnp.logical_or(jnp.logical_and(ind_row == 0, ind_low),
                                             jnp.logical_and(ind_row == 1, jnp.logical_not(ind_low))),
                              1.0, 0.0).astype(BF16)
        low = lax.broadcasted_iota(jnp.int32, (2 * WINDOW, LANES), 1) < HEAD_DIM

        def both_heads(even, odd):
            picked = jnp.where(low, even, odd)
            return picked + jnp.concatenate([picked[:, HEAD_DIM:], picked[:, :HEAD_DIM]], axis=1)

        def window(b):
            row0 = pl.multiple_of(b * WINDOW, WINDOW)
            start = i * tq + b * WINDOW
            prev = pl.multiple_of(jnp.maximum(start - WINDOW, 0), WINDOW)
            cur = pl.multiple_of(start, WINDOW)
            kvw = jnp.concatenate([kv_ref[pl.ds(prev, WINDOW), :], kv_ref[pl.ds(cur, WINDOW), :]], axis=0)
            return (row0, prev, cur, _half_tiles(kvw[:, :KV_WIDTH]), _half_tiles(kvw[:, KV_WIDTH:]),
                    bias_ref[jnp.minimum(start, 1)])

        def block_group(bb, dsink):
            windows = [window(bb * ATTN_BWD_BLOCKS + n) for n in range(ATTN_BWD_BLOCKS)]
            dk_groups = [[] for _ in windows]
            dv_groups = [[] for _ in windows]
            for j in range(N_KV_HEADS):
                stacks, deltas, dq_ts = [], [], []
                for row0, _, _, _, _, _ in windows:
                    qst = _stack_pairs(q_ref, row0, j)
                    dost = _stack_pairs(do_ref, row0, j)
                    prod = dost.astype(F32) * _stack_pairs(o_ref, row0, j).astype(F32)
                    prod_hi = prod.astype(BF16)
                    prod_lo = (prod - prod_hi.astype(F32)).astype(BF16)
                    stacks.append((qst, dost))
                    deltas.append(lax.dot_general(indicator, prod_hi, NT_DIMS, preferred_element_type=F32)
                                  + lax.dot_general(indicator, prod_lo, NT_DIMS, preferred_element_type=F32))
                    dq_ts.append(jnp.zeros((LANES, STACK), F32))
                dk_par = [[] for _ in windows]
                dv_par = [[] for _ in windows]
                for parity in range(2):
                    heads = [j * GROUP + 2 * p + parity for p in range(PAIRS)]
                    sink = _per_pair_row([sink_ref[h] * LOG2E for h in heads])
                    for n, (row0, _, _, k_halves, v_halves, bias) in enumerate(windows):
                        qst, dost = stacks[n]
                        kk, vv = k_halves[j][parity], v_halves[j][parity]
                        s = lax.dot_general(kk, qst, NT_DIMS, preferred_element_type=F32) * SCORE_SCALE + bias
                        lse = jnp.concatenate([lse_ref[h:h + 1, pl.ds(row0, WINDOW)] for h in heads], axis=1)
                        p = jnp.exp2(s - lse)
                        dp = lax.dot_general(vv, dost, NT_DIMS, preferred_element_type=F32)
                        delta = deltas[n][parity:parity + 1, :]
                        dsb = (p * (dp - delta)).astype(BF16)
                        dq_ts[n] = dq_ts[n] + lax.dot_general(kk, dsb, TN_DIMS, preferred_element_type=F32)
                        dk_par[n].append(jnp.dot(dsb, qst, preferred_element_type=F32))
                        dv_par[n].append(jnp.dot(p.astype(BF16), dost, preferred_element_type=F32))
                        weighted = jnp.exp2(sink - lse) * delta
                        for pr, h in enumerate(heads):
                            dsink = dsink - jnp.where(
                                lane == h, jnp.sum(weighted[:, pr * WINDOW:(pr + 1) * WINDOW]), 0.0)
                for n, (row0, _, _, _, _, _) in enumerate(windows):
                    dq_st = jnp.transpose((dq_ts[n] * ATTN_SCALE).astype(BF16))
                    for pr in range(PAIRS):
                        dq_ref[pl.ds(row0, WINDOW), (j * PAIRS + pr) * LANES:(j * PAIRS + pr + 1) * LANES] = (
                            dq_st[pr * WINDOW:(pr + 1) * WINDOW, :])
                    dk_groups[n].append(both_heads(dk_par[n][0], dk_par[n][1]))
                    dv_groups[n].append(both_heads(dv_par[n][0], dv_par[n][1]))
            for n, (_, prev, cur, _, _, _) in enumerate(windows):
                blk = jnp.concatenate([jnp.where(low, dk_groups[n][0], dk_groups[n][1]) * ATTN_SCALE,
                                       jnp.where(low, dv_groups[n][0], dv_groups[n][1])], axis=1)
                acc_ref[pl.ds(prev, WINDOW), :] += blk[:WINDOW, :]
                acc_ref[pl.ds(cur, WINDOW), :] += blk[WINDOW:, :]
            return dsink

        dsink = lax.fori_loop(0, nblk // ATTN_BWD_BLOCKS, block_group, jnp.zeros((1, LANES), F32))
        ds_ref[0:1, :] += dsink

        @pl.when(i == nt - 1)
        def _():
            dkv_ref[...] = acc_ref[...].astype(BF16)

    tok = pl.BlockSpec((tq, D_MODEL), lambda i: (i, 0))
    return _call(
        body, "attn_bwd", (nt,), [z, z, dattn, attn, lse, sinks],
        [tok, pl.BlockSpec((t, 2 * KV_WIDTH), lambda i: (0, KV_COL // (2 * KV_WIDTH))), tok, tok,
         pl.BlockSpec((N_Q_HEADS, tq), lambda i: (0, i)), pl.BlockSpec(memory_space=pltpu.SMEM)],
        [jax.ShapeDtypeStruct((t, D_MODEL), BF16), jax.ShapeDtypeStruct((t, 2 * KV_WIDTH), BF16),
         jax.ShapeDtypeStruct((SUBLANES, LANES), F32)],
        [tok, _full((t, 2 * KV_WIDTH)), _full((SUBLANES, LANES))],
        scratch=[pltpu.VMEM((t, 2 * KV_WIDTH), F32), pltpu.VMEM((2, 2 * WINDOW, STACK), F32)], rider=rider)


def _inproj_bwd(dq, drest, dkv, x, dx2, vec, w_t, rider):
    t = x.shape[0]
    tm = min(TOKEN_TILE, t)

    def body(dq_ref, dr_ref, dkv_ref, x_ref, dx2_ref, vec_ref, w_ref, gx_ref, acc_ref, db_ref):
        @pl.when(pl.program_id(0) == 0)
        def _():
            acc_ref[...] = jnp.zeros_like(acc_ref)
            db_ref[...] = jnp.zeros_like(db_ref)

        g = vec_ref[0:1, :]
        sc1 = vec_ref[1:2, :]
        dqb, drb, dkvb = dq_ref[...], dr_ref[...], dkv_ref[...]
        dh = jnp.dot(dqb, w_ref[:REF_KV_COL, :], preferred_element_type=F32)
        dh = dh + jnp.dot(drb, w_ref[REF_REST_COL:, :], preferred_element_type=F32)
        dh = dh + jnp.dot(dkvb, w_ref[REF_KV_COL:REF_REST_COL, :], preferred_element_type=F32)
        db_ref[:, :REF_KV_COL] += jnp.sum(dqb.astype(F32), axis=0, keepdims=True)
        db_ref[:, REF_REST_COL:] += jnp.sum(drb.astype(F32), axis=0, keepdims=True)
        db_ref[:, REF_KV_COL:REF_REST_COL] += jnp.sum(dkvb.astype(F32), axis=0, keepdims=True)
        xf = x_ref[...]
        r = lax.rsqrt(jnp.mean(xf * xf, axis=-1, keepdims=True) + EPS)
        xn = xf * r
        gs = g * (1.0 + sc1)
        dh_xn = dh * xn
        sum_dh_xn = jnp.sum(dh_xn, axis=0, keepdims=True)
        acc_ref[0:1, :] += jnp.sum(dh, axis=0, keepdims=True)
        acc_ref[1:2, :] += sum_dh_xn * g
        acc_ref[2:3, :] += sum_dh_xn * (1.0 + sc1)
        gx_ref[...] = dx2_ref[...].astype(F32) + r * (dh * gs - xn * jnp.mean(dh_xn * gs, axis=-1, keepdims=True))

    tok = pl.BlockSpec((tm, D_MODEL), lambda i: (i, 0))
    return _call(
        body, "inproj_bwd", (t // tm,), [dq, drest, dkv, x, dx2, vec, w_t],
        [tok, pl.BlockSpec((tm, REST_WIDTH), lambda i: (i, 0)),
         pl.BlockSpec((tm, 2 * KV_WIDTH), lambda i: (i, 0)), tok, tok,
         _full((SUBLANES, D_MODEL)), _full((IN_WIDTH, D_MODEL))],
        [jax.ShapeDtypeStruct((t, D_MODEL), F32), jax.ShapeDtypeStruct((SUBLANES, D_MODEL), F32),
         jax.ShapeDtypeStruct((1, IN_WIDTH), F32)],
        [tok, _full((SUBLANES, D_MODEL)), _full((1, IN_WIDTH))], rider=rider)


def _weight_grad(b, a, name, bn, rows=None, row0=0, into=None, rider=None):
    t, n = b.shape
    m = a.shape[1]
    rows = n if rows is None else rows
    tk = min(TOKEN_TILE, t)
    for cand in (4 * TOKEN_TILE, 2 * TOKEN_TILE):
        if t % cand == 0 and 2 * cand * (bn + m) * 2 + bn * m * 4 <= WGRAD_VMEM:
            tk = cand
            break
    nk = t // tk
    block0 = row0 // bn

    def body(b_ref, a_ref, *rest):
        out_ref, acc_ref = rest[-2:]
        k = pl.program_id(1)

        @pl.when(k == 0)
        def _():
            acc_ref[...] = jnp.zeros_like(acc_ref)

        acc_ref[...] += lax.dot_general(b_ref[...], a_ref[...], TN_DIMS, preferred_element_type=F32)

        @pl.when(k == nk - 1)
        def _():
            out_ref[...] = acc_ref[...].astype(BF16)

    outs, routs = _call(
        body, name, (n // bn, nk), [b, a] + ([] if into is None else [into]),
        [pl.BlockSpec((tk, bn), lambda j, k: (k, j)), pl.BlockSpec((tk, m), lambda j, k: (k, 0))]
        + ([] if into is None else [ANY]),
        [jax.ShapeDtypeStruct((rows, m), BF16)], [pl.BlockSpec((bn, m), lambda j, k: (block0 + j, 0))],
        scratch=[pltpu.VMEM((bn, m), F32)], rider=rider, aliases=None if into is None else {2: 0})
    return outs[0], routs


def _to_rows(v):
    n = v.shape[0]
    padded = -(-n // (SUBLANES * LANES)) * SUBLANES * LANES
    return jnp.pad(v, (0, padded - n)).reshape(padded // LANES, LANES)


def _vec_rows(*rows):
    stacked = jnp.concatenate([r.reshape(1, D_MODEL) for r in rows], axis=0)
    return jnp.pad(stacked, ((0, SUBLANES - len(rows)), (0, 0)))


def kernel(x, c, w_ada, b_ada, g_mix, w_in, b_in, sinks, conv_w, w_out, g_ffn, w_ffn_in, w_ffn_out, g_final, loss_target, m_w_ada, m_b_ada, m_g_mix, m_w_in, m_b_in, m_sinks, m_conv_w, m_w_out, m_g_ffn, m_w_ffn_in, m_w_ffn_out, m_g_final, v_w_ada, v_b_ada, v_g_mix, v_w_in, v_b_in, v_sinks, v_conv_w, v_w_out, v_g_ffn, v_w_ffn_in, v_w_ffn_out, v_g_final):
    ix, iy, ic = _my_place()
    me = 4 * ix + 2 * iy + ic
    xs = x[0]
    target = loss_target[0]
    ada_cols = w_ada.shape[2]
    conv_cols = conv_w.shape[2]

    wt_in, wt_fi = jnp.transpose(w_in[0]), jnp.transpose(w_ffn_in[0])
    b_cols = lax.dynamic_slice_in_dim(b_ada, me * ada_cols, ada_cols, axis=1)
    g_in, (cast_fi, cast_out, cast_fo), first, mod_all = _gather_first_weight(
        wt_in, [wt_fi, w_out[0], w_ffn_out[0]], _to_rows(jnp.concatenate([c[0], conv_w[0].reshape(-1)])),
        w_ada[0], b_cols)
    first = first.reshape(N_DEV, -1)
    c_all = first[:, :D_MODEL]
    conv_full = jnp.transpose(first[:, D_MODEL:D_MODEL + 3 * conv_cols].reshape(N_DEV, 3, conv_cols), (1, 0, 2))
    conv_full = conv_full.reshape(3, D_MODEL)
    mod = lax.dynamic_index_in_dim(mod_all, me, axis=1, keepdims=False).reshape(N_MOD, D_MODEL)
    sh1, sc1, ga1, sh2, sc2, ga2 = [mod[i:i + 1] for i in range(N_MOD)]
    w_in_t = g_in.reshape(IN_WIDTH, D_MODEL)
    (z, h1), (g_fi, g_out) = _inproj_fwd(xs, _vec_rows(g_mix, sc1, sh1), w_in_t, b_in,
                                         _gather_rider([cast_fi, cast_out]))
    w_fi_t = g_fi.reshape(2 * D_FF, D_MODEL)
    w_out_full = g_out.reshape(D_MODEL, D_MODEL)
    (attn, lse), (g_fo,) = _attn_fwd(z, sinks[0], _gather_rider([cast_fo]))
    w_fo_full = g_fo.reshape(D_FF, D_MODEL)
    merged, x2, h2, oproj = _mix_fwd(
        xs, attn, z, _vec_rows(ga1, g_ffn, sc2, sh2, conv_full[0], conv_full[1], conv_full[2]), w_out_full)
    gu, act = _ffn_fwd(h2, w_fi_t)
    dx3, df, dgu, acc_l = _ffn_out_loss(act, gu, x2, target, _vec_rows(ga2, g_final), w_fo_full)

    gw_fo, _ = _weight_grad(act, df, "wgrad_ffn_out", D_FF)
    gw_fi, _ = _weight_grad(dgu, h2, "wgrad_ffn_in", D_FF)
    blocks_fo = gw_fo.reshape(N_DEV, D_FF // N_DEV, D_MODEL)
    blocks_fi = gw_fi.reshape(N_DEV, 2 * D_FF // N_DEV, D_MODEL)
    (dx2, acc_f), (sib_fo, sib_fi) = _ffn_in_bwd(dgu, x2, dx3, _vec_rows(g_ffn, sc2), w_fi_t,
                                                 _sibling_rider([blocks_fo, blocks_fi]))
    sums_fo, mine_fo = _sibling_sum(blocks_fo, sib_fo, "sibling_sum_ffn_out")
    sums_fi, mine_fi = _sibling_sum(blocks_fi, sib_fi, "sibling_sum_ffn_in")
    (dout, dattn, drest, acc_m), (ici_fo, ici_fi) = _mix_bwd(
        dx2, oproj, attn, z, _vec_rows(ga1, conv_full[0], conv_full[1], conv_full[2]), w_out_full,
        _chip_rider([sums_fo, sums_fi]))
    gw_out, _ = _weight_grad(merged, dout, "wgrad_out", D_MODEL)
    blocks_out = gw_out.reshape(N_DEV, D_MODEL // N_DEV, D_MODEL)
    (dq, dkv, dsink), (sib_out,) = _attn_bwd(z, dattn, attn, lse, sinks[0], _sibling_rider([blocks_out]))
    sums_out, mine_out = _sibling_sum(blocks_out, sib_out, "sibling_sum_out")
    gw_in, (ici_out,) = _weight_grad(drest, h1, "wgrad_in_rest", IN_CHUNK, rows=IN_WIDTH, row0=REF_REST_COL,
                                     rider=_chip_rider([sums_out]))
    gw_in, _ = _weight_grad(dq, h1, "wgrad_in_q", D_MODEL, rows=IN_WIDTH, row0=0, into=gw_in)
    gw_in, _ = _weight_grad(dkv, h1, "wgrad_in_kv", 2 * KV_WIDTH, rows=IN_WIDTH, row0=REF_KV_COL, into=gw_in)
    blocks_in = gw_in.reshape(N_DEV, IN_WIDTH // N_DEV, D_MODEL)
    sums_in, mine_in = _sibling_exchange_sum(blocks_in, "sibling_w_in")
    (grad_x, acc_i, db_in), (ici_in,) = _inproj_bwd(dq, drest, dkv, xs, dx2, _vec_rows(g_mix, sc1), w_in_t,
                                                    _chip_rider([sums_in]))

    widen = lambda vec: jnp.pad(vec, (0, -vec.shape[0] % D_MODEL))
    packed = jnp.concatenate([
        acc_i[0], acc_i[1], acc_m[0], acc_f[0], acc_f[1], acc_l[2],
        acc_i[2], widen(db_in[0]), acc_f[2], acc_l[1],
        acc_m[1], acc_m[2], acc_m[3], widen(dsink[0]), acc_l[0],
        jnp.zeros(((PACK_ROWS - PACK_SQERR - 1) * D_MODEL,), F32)]).reshape(PACK_ROWS, D_MODEL)
    packed_all = _small_allgather(packed, "gather_small")
    dmod_all = packed_all[:, PACK_DMOD:PACK_DMOD + N_MOD, :].reshape(N_DEV, N_MOD * D_MODEL)
    dmod_cols = lax.dynamic_slice_in_dim(dmod_all, me * ada_cols, ada_cols, axis=1)
    g_w_ada = _ada_weight_grad(c_all, dmod_cols)
    row_of = lambda a: a.reshape(1, -1)
    small, g_conv_full, loss = _small_finalize(packed_all, {
        "b_ada": (b_ada, m_b_ada, v_b_ada), "g_mix": (g_mix, m_g_mix, v_g_mix), "b_in": (b_in, m_b_in, v_b_in),
        "g_ffn": (g_ffn, m_g_ffn, v_g_ffn), "sinks": (sinks, m_sinks, v_sinks),
        "g_final": (row_of(g_final), row_of(m_g_final), row_of(v_g_final))})
    small["g_final"] = tuple(o.reshape(g_final.shape) for o in small["g_final"])
    g_conv = lax.dynamic_slice_in_dim(g_conv_full, me * conv_cols, conv_cols, axis=1)
    d_conv, nm_conv, nv_conv = _adamw(conv_w[0], g_conv, m_conv_w[0], v_conv_w[0], "adamw_conv_w")
    small["conv_w"] = (g_conv[None], d_conv[None], nm_conv[None], nv_conv[None])

    def reduced(mine, ici, w, m, v, name, transposed=False):
        turn = jnp.transpose if transposed else (lambda a: a)
        return tuple(turn(o)[None] for o in _chip_sum_adamw(mine, ici, turn(w[0]), turn(m[0]), turn(v[0]), name))

    d_ada, nm_ada, nv_ada = _adamw(w_ada[0], g_w_ada, m_w_ada[0], v_w_ada[0], "adamw_w_ada")
    res = {
        "w_ada": (g_w_ada[None], d_ada[None], nm_ada[None], nv_ada[None]),
        "w_in": reduced(mine_in, ici_in, w_in, m_w_in, v_w_in, "adamw_w_in", transposed=True),
        "w_out": reduced(mine_out, ici_out, w_out, m_w_out, v_w_out, "adamw_w_out"),
        "w_ffn_in": reduced(mine_fi, ici_fi, w_ffn_in, m_w_ffn_in, v_w_ffn_in, "adamw_w_ffn_in", transposed=True),
        "w_ffn_out": reduced(mine_fo, ici_fo, w_ffn_out, m_w_ffn_out, v_w_ffn_out, "adamw_w_ffn_out"),
    }
    res.update(small)
    order = ["w_ada", "b_ada", "g_mix", "w_in", "b_in", "sinks", "conv_w", "w_out", "g_ffn", "w_ffn_in", "w_ffn_out",
             "g_final"]
    outs = [loss.reshape(()), grad_x[None]]
    for k in range(4):
        outs += [res[n][k] for n in order]
    return tuple(outs)
```

```python
import functools
import math

import jax
import jax.numpy as jnp
from jax import lax
from jax.experimental import pallas as pl
from jax.experimental.pallas import tpu as pltpu

F32 = jnp.float32
BF16 = jnp.bfloat16
GRAD_STREAM = F32

D_MODEL = 1024
HEAD_DIM = 64
N_Q_HEADS = 16
N_KV_HEADS = 2
GROUP = 8
WINDOW = 128
KV_WIDTH = N_KV_HEADS * HEAD_DIM
D_FF = 2816
IN_WIDTH = 6400
N_MOD = 6
EPS = 1e-6
N_DEV = 8
REST_WIDTH = 5 * D_MODEL
KV_COL = D_MODEL + REST_WIDTH
ATTN_SCALE = HEAD_DIM ** -0.5

ADAM_LR = 0.001
ADAM_B1 = 0.9
ADAM_B2 = 0.999
ADAM_EPS = 1e-08
ADAM_WD = 0.01
ADAM_STEP = 10

LANES = 128
SUBLANES = 8
BF16_ROWS = 16
VMEM_LIMIT = 56 * 1024 * 1024
TOKEN_TILE = 512
FF_CHUNK = 256
ROW_PARTS = 2
MIN_STREAM_STEPS = 2
WGRAD_VMEM = 40 * 1024 * 1024
MESH = pl.DeviceIdType.MESH
ANY = pl.BlockSpec(memory_space=pl.ANY)

NT_DIMS = (((1,), (1,)), ((), ()))
TN_DIMS = (((0,), (0,)), ((), ()))
CHIP_FLIPS = [(0, 0), (1, 0), (0, 1), (1, 1)]


def _full(shape):
    return pl.BlockSpec(shape, lambda *_: (0,) * len(shape))


def _my_place():
    return lax.axis_index("x"), lax.axis_index("y"), lax.axis_index("c")


def _flip(v, bit):
    return 1 - v if bit else v


def _sigmoid(v):
    return 1.0 / (1.0 + jnp.exp2(v * (-1.4426950408889634)))


class _Rider:
    def __init__(self, ins, out_shapes, sem_shapes, first=None, mid=None, last=None, ins_in_vmem=False):
        self.ins, self.out_shapes, self.sem_shapes = list(ins), list(out_shapes), list(sem_shapes)
        self.in_specs = [_full(a.shape) if ins_in_vmem else ANY for a in self.ins]
        self.hooks = [(when, fn) for when, fn in (("first", first), ("mid", mid), ("last", last)) if fn is not None]


def _call(body, name, grid, args, in_specs, out_shape, out_specs, scratch=(), rider=None, aliases=None):
    n_in, n_out, n_scr = len(args), len(out_shape), len(scratch)
    r_in = rider.ins if rider else []
    r_out = rider.out_shapes if rider else []
    r_sem = rider.sem_shapes if rider else []
    nsteps = math.prod(grid)

    def full_body(*refs):
        pos = 0
        groups = []
        for size in (n_in, len(r_in), n_out, len(r_out), n_scr, len(r_sem)):
            groups.append(refs[pos:pos + size])
            pos += size
        ins, rins, outs, routs, scr, rsems = groups
        step = pl.program_id(0)
        for axis in range(1, len(grid)):
            step = step * grid[axis] + pl.program_id(axis)
        at = {"first": 0, "mid": (3 * nsteps) // 4, "last": nsteps - 1}
        hooks = rider.hooks if rider else []
        for when, fn in hooks:
            if when != "last":
                pl.when(step == at[when])(functools.partial(fn, rins, routs, rsems))
        body(*ins, *outs, *scr)
        for when, fn in hooks:
            if when == "last":
                pl.when(step == at[when])(functools.partial(fn, rins, routs, rsems))

    outs = pl.pallas_call(
        full_body, name=name, grid=grid,
        out_shape=list(out_shape) + list(r_out),
        in_specs=list(in_specs) + (rider.in_specs if rider else []),
        out_specs=list(out_specs) + [ANY] * len(r_out),
        scratch_shapes=list(scratch) + list(r_sem),
        input_output_aliases=dict(aliases or {}),
        compiler_params=pltpu.CompilerParams(dimension_semantics=("arbitrary",) * len(grid),
                                             vmem_limit_bytes=VMEM_LIMIT),
    )(*args, *r_in)
    return list(outs[:n_out]), list(outs[n_out:])


def _gather_rider(shards):
    n = len(shards)

    def setup(outs, sems):
        x, y, c = _my_place()
        send_sems, recv_sems, _ = sems
        chips = [(1 - x, y), (x, 1 - y), (1 - x, 1 - y)]

        def block(w, place):
            return outs[w].at[4 * place[0] + 2 * place[1] + place[2]]

        def copy(w, k, place, to, src=None):
            return pltpu.make_async_remote_copy(
                src_ref=block(w, place) if src is None else src, dst_ref=block(w, place),
                send_sem=send_sems.at[w, k], recv_sem=recv_sems.at[w, k], device_id=to, device_id_type=MESH)

        return (x, y, c), (x, y, 1 - c), chips, block, copy

    def first(ins, outs, sems):
        me, sibling, chips, block, copy = setup(outs, sems)
        for w in range(n):
            pltpu.make_async_copy(ins[w], block(w, me), sems[2].at[w]).start()
            copy(w, 0, me, sibling, src=ins[w]).start()
            for j, chip in enumerate(chips):
                copy(w, 1 + j, me, (*chip, me[2]), src=ins[w]).start()

    def mid(ins, outs, sems):
        me, sibling, chips, block, copy = setup(outs, sems)
        for w in range(n):
            for j, chip in enumerate(chips):
                copy(w, 1 + j, (*chip, me[2]), me).wait_recv()
                copy(w, 4 + j, (*chip, me[2]), sibling).start()

    def last(ins, outs, sems):
        me, sibling, chips, block, copy = setup(outs, sems)
        for w in range(n):
            copy(w, 0, sibling, me).wait_recv()
            for j, chip in enumerate(chips):
                copy(w, 4 + j, (*chip, 1 - me[2]), me).wait_recv()
            copy(w, 0, me, sibling, src=ins[w]).wait_send()
            for j, chip in enumerate(chips):
                copy(w, 1 + j, me, (*chip, me[2]), src=ins[w]).wait_send()
                copy(w, 4 + j, (*chip, me[2]), sibling).wait_send()
            pltpu.make_async_copy(ins[w], block(w, me), sems[2].at[w]).wait()

    return _Rider(
        shards, [jax.ShapeDtypeStruct((N_DEV,) + s.shape, BF16) for s in shards],
        [pltpu.SemaphoreType.DMA((n, N_DEV - 1)), pltpu.SemaphoreType.DMA((n, N_DEV - 1)),
         pltpu.SemaphoreType.DMA((n,))],
        first=first, mid=mid, last=last, ins_in_vmem=True)


def _sibling_rider(gblocks):
    n = len(gblocks)

    def copies(ins, outs, sems):
        x, y, c = _my_place()
        send_sems, recv_sems = sems
        made = []
        for w in range(n):
            for f, (fx, fy) in enumerate(CHIP_FLIPS):
                chip = 4 * _flip(x, fx) + 2 * _flip(y, fy)
                made.append(pltpu.make_async_remote_copy(
                    src_ref=ins[w].at[chip + 1 - c], dst_ref=outs[w].at[f], send_sem=send_sems.at[w, f],
                    recv_sem=recv_sems.at[w, f], device_id=(x, y, 1 - c), device_id_type=MESH))
        return made

    def first(ins, outs, sems):
        for cp in copies(ins, outs, sems):
            cp.start()

    def last(ins, outs, sems):
        for cp in copies(ins, outs, sems):
            cp.wait_recv()
            cp.wait_send()

    return _Rider(gblocks, [jax.ShapeDtypeStruct((4,) + g.shape[1:], BF16) for g in gblocks],
                  [pltpu.SemaphoreType.DMA((n, 4))] * 2, first=first, last=last)


def _chip_rider(sums):
    n = len(sums)

    def copies(ins, outs, sems):
        x, y, c = _my_place()
        send_sems, recv_sems = sems
        made = []
        for w in range(n):
            for f in (1, 2, 3):
                fx, fy = CHIP_FLIPS[f]
                made.append(pltpu.make_async_remote_copy(
                    src_ref=ins[w].at[f - 1], dst_ref=outs[w].at[f - 1], send_sem=send_sems.at[w, f - 1],
                    recv_sem=recv_sems.at[w, f - 1], device_id=(_flip(x, fx), _flip(y, fy), c), device_id_type=MESH))
        return made

    def first(ins, outs, sems):
        for cp in copies(ins, outs, sems):
            cp.start()

    def last(ins, outs, sems):
        for cp in copies(ins, outs, sems):
            cp.wait_recv()
            cp.wait_send()

    return _Rider(sums, [jax.ShapeDtypeStruct(s.shape, BF16) for s in sums],
                  [pltpu.SemaphoreType.DMA((n, 3))] * 2, first=first, last=last)


def _push_to_all(v_ref, out_ref, send_sems, recv_sems, local_sem, wait=True):
    x, y, c = _my_place()
    me = 4 * x + 2 * y + c
    mine = pltpu.make_async_copy(v_ref, out_ref.at[me], local_sem)
    mine.start()
    sends = []
    for k in range(1, N_DEV):
        px, py, pc = _flip(x, k & 4), _flip(y, k & 2), _flip(c, k & 1)
        cp = pltpu.make_async_remote_copy(
            src_ref=v_ref, dst_ref=out_ref.at[me], send_sem=send_sems.at[k - 1], recv_sem=recv_sems.at[k - 1],
            device_id=(px, py, pc), device_id_type=MESH)
        cp.start()
        sends.append(cp)

    def finish():
        for k in range(1, N_DEV):
            px, py, pc = _flip(x, k & 4), _flip(y, k & 2), _flip(c, k & 1)
            pltpu.make_async_remote_copy(
                src_ref=v_ref, dst_ref=out_ref.at[4 * px + 2 * py + pc], send_sem=send_sems.at[k - 1],
                recv_sem=recv_sems.at[k - 1], device_id=(px, py, pc), device_id_type=MESH).wait_recv()
        for cp in sends:
            cp.wait_send()
        mine.wait()

    if wait:
        finish()
    return finish


def _small_allgather(v, name):
    def body(v_ref, out_ref, send_sems, recv_sems, local_sem):
        _push_to_all(v_ref, out_ref, send_sems, recv_sems, local_sem)

    return pl.pallas_call(
        body, name=name,
        out_shape=jax.ShapeDtypeStruct((N_DEV,) + v.shape, F32),
        in_specs=[pl.BlockSpec(memory_space=pltpu.VMEM)],
        out_specs=pl.BlockSpec(memory_space=pltpu.VMEM),
        scratch_shapes=[pltpu.SemaphoreType.DMA((N_DEV - 1,)), pltpu.SemaphoreType.DMA((N_DEV - 1,)),
                        pltpu.SemaphoreType.DMA],
        compiler_params=pltpu.CompilerParams(vmem_limit_bytes=VMEM_LIMIT),
    )(v)


def _gather_first_weight(shard, others, cond_rows, w_ada, b_cols):
    n = len(others)
    ada_cols = w_ada.shape[1]
    c_rows = D_MODEL // LANES

    def body(*refs):
        w_ref, other_refs = refs[0], refs[1:1 + n]
        cond_ref, wada_ref, bcols_ref = refs[1 + n:4 + n]
        out_ref, cast_refs = refs[4 + n], refs[5 + n:5 + 2 * n]
        cond_all_ref, mod_all_ref = refs[5 + 2 * n:7 + 2 * n]
        mine_ref, mod_ref, send_sems, recv_sems, local_sem, small_send, small_recv, small_local = refs[7 + 2 * n:]
        x, y, c = _my_place()
        me, sibling = (x, y, c), (x, y, 1 - c)
        xnb, ynb, diag = (1 - x, y), (x, 1 - y), (1 - x, 1 - y)
        half = shard.shape[0] // 2

        def block(place, part=None):
            ref = out_ref.at[4 * place[0] + 2 * place[1] + place[2]]
            return ref if part is None else ref.at[pl.ds(part * half, half)]

        def copy(k, place, to, part=None, src=None):
            return pltpu.make_async_remote_copy(
                src_ref=block(place, part) if src is None else src, dst_ref=block(place, part),
                send_sem=send_sems.at[k], recv_sem=recv_sems.at[k], device_id=to, device_id_type=MESH)

        finish_cond = _push_to_all(cond_ref, cond_all_ref, small_send.at[0], small_recv.at[0], small_local.at[0],
                                   wait=False)
        mine_ref[...] = w_ref[...].astype(BF16)
        local = pltpu.make_async_copy(mine_ref, block(me), local_sem)
        local.start()
        started = [copy(0, me, sibling, src=mine_ref), copy(1, me, (*xnb, c), src=mine_ref),
                   copy(2, me, (*ynb, c), src=mine_ref)]
        for cp in started:
            cp.start()
        finish_cond()
        mod = jnp.zeros((N_DEV, ada_cols), F32) + bcols_ref[...]
        for r in range(c_rows):
            cf = cond_all_ref[:, r, :]
            act = (cf * _sigmoid(cf)).astype(BF16)
            mod = mod + jnp.dot(act, wada_ref[r * LANES:(r + 1) * LANES, :].astype(BF16),
                                preferred_element_type=F32)
        mod_ref[...] = mod
        finish_mod = _push_to_all(mod_ref, mod_all_ref, small_send.at[1], small_recv.at[1], small_local.at[1],
                                  wait=False)
        for o_ref, c_ref in zip(other_refs, cast_refs):
            c_ref[...] = o_ref[...].astype(BF16)
        def start(cp):
            cp.start()
            started.append(cp)

        copy(1, (*xnb, c), me).wait_recv()
        start(copy(3, (*xnb, c), (*ynb, c), part=0))
        start(copy(5, (*xnb, c), sibling))
        copy(2, (*ynb, c), me).wait_recv()
        start(copy(4, (*ynb, c), (*xnb, c), part=1))
        start(copy(6, (*ynb, c), sibling))
        copy(3, (*diag, c), me, part=0).wait_recv()
        start(copy(7, (*diag, c), sibling, part=0))
        copy(4, (*diag, c), me, part=1).wait_recv()
        start(copy(8, (*diag, c), sibling, part=1))
        copy(0, sibling, me).wait_recv()
        copy(5, (*xnb, 1 - c), me).wait_recv()
        copy(6, (*ynb, 1 - c), me).wait_recv()
        copy(7, (*diag, 1 - c), me, part=0).wait_recv()
        copy(8, (*diag, 1 - c), me, part=1).wait_recv()
        finish_mod()
        for cp in started:
            cp.wait_send()
        local.wait()

    vmem = pl.BlockSpec(memory_space=pltpu.VMEM)
    outs = pl.pallas_call(
        body, name="gather_w_in",
        out_shape=[jax.ShapeDtypeStruct((N_DEV,) + shard.shape, BF16)]
        + [jax.ShapeDtypeStruct(o.shape, BF16) for o in others]
        + [jax.ShapeDtypeStruct((N_DEV,) + cond_rows.shape, F32), jax.ShapeDtypeStruct((N_DEV, N_DEV, ada_cols), F32)],
        in_specs=[vmem] * (4 + n),
        out_specs=[ANY] + [vmem] * (n + 2),
        scratch_shapes=[pltpu.VMEM(shard.shape, BF16), pltpu.VMEM((N_DEV, ada_cols), F32),
                        pltpu.SemaphoreType.DMA((9,)), pltpu.SemaphoreType.DMA((9,)),
                        pltpu.SemaphoreType.DMA,
                        pltpu.SemaphoreType.DMA((2, N_DEV - 1)), pltpu.SemaphoreType.DMA((2, N_DEV - 1)),
                        pltpu.SemaphoreType.DMA((2,))],
        compiler_params=pltpu.CompilerParams(vmem_limit_bytes=VMEM_LIMIT),
    )(shard, *others, cond_rows, w_ada, b_cols)
    return outs[0], list(outs[1:1 + n]), outs[1 + n], outs[2 + n]


def _sibling_exchange_sum(gblocks, name):
    n = len(gblocks)

    def body(*refs):
        g_refs, out_refs = refs[:n], refs[n:3 * n]
        bufs = refs[3 * n:5 * n]
        own_sems, send_sems, recv_sems = refs[5 * n:]
        x, y, c = _my_place()
        pairs = []
        for w in range(n):
            own_buf, sib_buf = bufs[2 * w], bufs[2 * w + 1]
            for f, (fx, fy) in enumerate(CHIP_FLIPS):
                chip = 4 * _flip(x, fx) + 2 * _flip(y, fy)
                own = pltpu.make_async_copy(g_refs[w].at[chip + c], own_buf.at[f], own_sems.at[w, f])
                own.start()
                remote = pltpu.make_async_remote_copy(
                    src_ref=g_refs[w].at[chip + 1 - c], dst_ref=sib_buf.at[f], send_sem=send_sems.at[w, f],
                    recv_sem=recv_sems.at[w, f], device_id=(x, y, 1 - c), device_id_type=MESH)
                remote.start()
                pairs.append((own, remote))
        for w in range(n):
            own_buf, sib_buf = bufs[2 * w], bufs[2 * w + 1]
            sums_ref, mine_ref = out_refs[2 * w], out_refs[2 * w + 1]
            for f in (1, 2, 3, 0):
                own, remote = pairs[4 * w + f]
                own.wait()
                remote.wait_recv()
                total = own_buf[f].astype(F32) + sib_buf[f].astype(F32)
                if f == 0:
                    mine_ref[...] = total
                else:
                    sums_ref[f - 1] = total.astype(BF16)
        for _, remote in pairs:
            remote.wait_send()

    vmem = pl.BlockSpec(memory_space=pltpu.VMEM)
    out_shape, scratch = [], []
    for g in gblocks:
        out_shape += [jax.ShapeDtypeStruct((3,) + g.shape[1:], BF16), jax.ShapeDtypeStruct(g.shape[1:], F32)]
        scratch += [pltpu.VMEM((4,) + g.shape[1:], BF16)] * 2
    outs = pl.pallas_call(
        body, name=name, out_shape=out_shape,
        in_specs=[ANY] * n, out_specs=[vmem] * (2 * n),
        scratch_shapes=scratch + [pltpu.SemaphoreType.DMA((n, 4))] * 3,
        compiler_params=pltpu.CompilerParams(vmem_limit_bytes=VMEM_LIMIT),
    )(*gblocks)
    return [(outs[2 * w], outs[2 * w + 1]) for w in range(n)]


def _ada_weight_grad(c_all, dmod_cols):
    cols = dmod_cols.shape[1]

    def body(c_ref, d_ref, out_ref):
        cf = c_ref[...]
        act = (cf * _sigmoid(cf)).astype(BF16)
        out_ref[...] = lax.dot_general(act, d_ref[...].astype(BF16), TN_DIMS, preferred_element_type=F32)

    return pl.pallas_call(
        body, name="ada_weight_grad",
        out_shape=jax.ShapeDtypeStruct((D_MODEL, cols), F32),
        in_specs=[pl.BlockSpec(memory_space=pltpu.VMEM)] * 2,
        out_specs=pl.BlockSpec(memory_space=pltpu.VMEM),
        compiler_params=pltpu.CompilerParams(vmem_limit_bytes=VMEM_LIMIT),
    )(c_all, dmod_cols)


PACK_ROWS = 24
PACK_DMOD = 0
PACK_PARAMS = {"g_mix": (6, D_MODEL), "b_in": (7, IN_WIDTH), "g_ffn": (14, D_MODEL), "g_final": (15, D_MODEL),
               "sinks": (19, N_Q_HEADS)}
PACK_CONV = 16
PACK_SQERR = 20


def _small_finalize(packed_all, params):
    names = ["b_ada"] + list(PACK_PARAMS)
    layout = dict(PACK_PARAMS, b_ada=(PACK_DMOD, N_MOD * D_MODEL))
    n = len(names)

    def body(*refs):
        p_ref = refs[0]
        ins = refs[1:1 + 3 * n]
        outs = refs[1 + 3 * n:1 + 7 * n]
        conv_ref, loss_ref = refs[1 + 7 * n:]
        total = p_ref[0]
        for d in range(1, N_DEV):
            total = total + p_ref[d]
        for k, name in enumerate(names):
            row0, width = layout[name]
            w_ref, m_ref, v_ref = ins[3 * k:3 * k + 3]
            g_ref, d_ref, nm_ref, nv_ref = outs[4 * k:4 * k + 4]
            for chunk in range(-(-width // D_MODEL)):
                lo = chunk * D_MODEL
                hi = min(lo + D_MODEL, width)
                g = total[row0 + chunk:row0 + chunk + 1, :hi - lo]
                g_ref[:, lo:hi] = g
                d_ref[:, lo:hi], nm_ref[:, lo:hi], nv_ref[:, lo:hi] = _adamw_update(
                    w_ref[:, lo:hi], g, m_ref[:, lo:hi], v_ref[:, lo:hi])
        conv_ref[...] = total[PACK_CONV:PACK_CONV + 3, :]
        loss_ref[...] = (0.5 / D_MODEL) * jnp.sum(total[PACK_SQERR:PACK_SQERR + 1, :], keepdims=True)

    vmem = pl.BlockSpec(memory_space=pltpu.VMEM)
    flat = [a for name in names for a in params[name]]
    out_shape = [jax.ShapeDtypeStruct(params[name][0].shape, F32) for name in names for _ in range(4)]
    outs = pl.pallas_call(
        body, name="small_finalize",
        out_shape=out_shape + [jax.ShapeDtypeStruct((3, D_MODEL), F32), jax.ShapeDtypeStruct((1, 1), F32)],
        in_specs=[vmem] * (1 + 3 * n),
        out_specs=[vmem] * (4 * n + 2),
        compiler_params=pltpu.CompilerParams(vmem_limit_bytes=VMEM_LIMIT),
    )(packed_all, *flat)
    return {name: tuple(outs[4 * k:4 * k + 4]) for k, name in enumerate(names)}, outs[4 * n], outs[4 * n + 1]


def _row_tile(rows, multiple):
    for cand in range(rows // MIN_STREAM_STEPS, 0, -1):
        if rows % cand == 0 and cand % multiple == 0:
            return cand
    return rows


def _adamw_update(w, g, m, v):
    c1 = 1.0 / (1.0 - ADAM_B1 ** ADAM_STEP)
    c2 = 1.0 / (1.0 - ADAM_B2 ** ADAM_STEP)
    nm = ADAM_B1 * m + (1.0 - ADAM_B1) * g
    nv = ADAM_B2 * v + (1.0 - ADAM_B2) * (g * g)
    delta = -ADAM_LR * ((nm * c1) / (jnp.sqrt(nv * c2) + ADAM_EPS) + ADAM_WD * w)
    return delta, nm, nv


def _adamw(w, g, m, v, name):
    rows, cols = w.shape
    tile = _row_tile(rows, SUBLANES)

    def body(w_ref, g_ref, m_ref, v_ref, d_ref, nm_ref, nv_ref):
        d_ref[...], nm_ref[...], nv_ref[...] = _adamw_update(w_ref[...], g_ref[...], m_ref[...], v_ref[...])

    spec = pl.BlockSpec((tile, cols), lambda i: (i, 0))
    outs, _ = _call(body, name, (rows // tile,), [w, g, m, v], [spec] * 4,
                    [jax.ShapeDtypeStruct((rows, cols), F32)] * 3, [spec] * 3)
    return outs


def _sibling_sum(gblocks, sib, name):
    _, r, cdim = gblocks.shape
    tile = _row_tile(r, BF16_ROWS)
    x, y, c = _my_place()
    table = jnp.stack([4 * _flip(x, fx) + 2 * _flip(y, fy) + c for fx, fy in CHIP_FLIPS]).astype(jnp.int32)

    def body(table_ref, own0, own1, own2, own3, sib_ref, sums_ref, mine_ref):
        mine_ref[...] = own0[...].astype(F32) + sib_ref[0].astype(F32)
        for f, own in ((1, own1), (2, own2), (3, own3)):
            sums_ref[f - 1] = (own[...].astype(F32) + sib_ref[f].astype(F32)).astype(BF16)

    own_specs = [pl.BlockSpec((None, tile, cdim), functools.partial(lambda i, tab, f: (tab[f], i, 0), f=f))
                 for f in range(4)]
    return pl.pallas_call(
        body, name=name,
        grid_spec=pltpu.PrefetchScalarGridSpec(
            num_scalar_prefetch=1, grid=(r // tile,),
            in_specs=own_specs + [pl.BlockSpec((4, tile, cdim), lambda i, tab: (0, i, 0))],
            out_specs=[pl.BlockSpec((3, tile, cdim), lambda i, tab: (0, i, 0)),
                       pl.BlockSpec((tile, cdim), lambda i, tab: (i, 0))]),
        out_shape=[jax.ShapeDtypeStruct((3, r, cdim), BF16), jax.ShapeDtypeStruct((r, cdim), F32)],
        compiler_params=pltpu.CompilerParams(dimension_semantics=("arbitrary",), vmem_limit_bytes=VMEM_LIMIT),
    )(table, gblocks, gblocks, gblocks, gblocks, sib)


def _chip_sum_adamw(mine, ici, w, m, v, name):
    r, cdim = mine.shape
    tile = _row_tile(r, BF16_ROWS)

    def body(mine_ref, ici_ref, w_ref, m_ref, v_ref, g_ref, d_ref, nm_ref, nv_ref):
        g = mine_ref[...]
        for f in range(3):
            g = g + ici_ref[f].astype(F32)
        g_ref[...] = g
        d_ref[...], nm_ref[...], nv_ref[...] = _adamw_update(w_ref[...], g, m_ref[...], v_ref[...])

    spec = pl.BlockSpec((tile, cdim), lambda i: (i, 0))
    outs, _ = _call(
        body, name, (r // tile,), [mine, ici, w, m, v],
        [spec, pl.BlockSpec((3, tile, cdim), lambda i: (0, i, 0)), spec, spec, spec],
        [jax.ShapeDtypeStruct((r, cdim), F32)] * 4, [spec] * 4)
    return outs


REF_KV_COL = D_MODEL
REF_REST_COL = D_MODEL + 2 * KV_WIDTH
IN_CHUNK = 1280
IN_PIECES = ([(0, 0, D_MODEL)]
             + [(D_MODEL + n * IN_CHUNK, REF_REST_COL + n * IN_CHUNK, IN_CHUNK) for n in range(REST_WIDTH // IN_CHUNK)]
             + [(KV_COL, REF_KV_COL, 2 * KV_WIDTH)])


def _inproj_fwd(x, vec, w_t, b_in, rider):
    t = x.shape[0]
    tm = min(TOKEN_TILE, t)

    def body(x_ref, vec_ref, w_ref, b_ref, z_ref, h_ref):
        xf = x_ref[...]
        r = lax.rsqrt(jnp.mean(xf * xf, axis=-1, keepdims=True) + EPS)
        h = (xf * r) * (vec_ref[0:1, :] * (1.0 + vec_ref[1:2, :])) + vec_ref[2:3, :]
        hb = h.astype(BF16)
        h_ref[...] = hb
        for mine, ref, width in IN_PIECES:
            zc = lax.dot_general(hb, w_ref[ref:ref + width, :], NT_DIMS, preferred_element_type=F32)
            z_ref[:, mine:mine + width] = (zc + b_ref[:, ref:ref + width]).astype(BF16)

    return _call(
        body, "inproj_fwd", (t // tm,), [x, vec, w_t, b_in],
        [pl.BlockSpec((tm, D_MODEL), lambda i: (i, 0)), _full((SUBLANES, D_MODEL)),
         _full((IN_WIDTH, D_MODEL)), _full((1, IN_WIDTH))],
        [jax.ShapeDtypeStruct((t, IN_WIDTH), BF16), jax.ShapeDtypeStruct((t, D_MODEL), BF16)],
        [pl.BlockSpec((tm, IN_WIDTH), lambda i: (i, 0)), pl.BlockSpec((tm, D_MODEL), lambda i: (i, 0))],
        rider=rider)


PAIRS = GROUP // 2
STACK = PAIRS * WINDOW


ATTN_BLOCKS = 4
ATTN_BWD_BLOCKS = 1
LOG2E = 1.4426950408889634
LN2 = 0.6931471805599453
SCORE_SCALE = ATTN_SCALE * LOG2E


def _fill_window_bias(bias_ref):
    shape = bias_ref.shape[1:]
    kj = lax.broadcasted_iota(jnp.int32, shape, 0)
    qi = jnp.bitwise_and(lax.broadcasted_iota(jnp.int32, shape, 1), WINDOW - 1)
    in_prev = jnp.logical_and(kj < WINDOW, kj > qi)
    in_cur = jnp.logical_and(kj >= WINDOW, (kj - WINDOW) <= qi)
    bias_ref[0] = jnp.where(in_cur, 0.0, -jnp.inf)
    bias_ref[1] = jnp.where(jnp.logical_or(in_prev, in_cur), 0.0, -jnp.inf)


def _half_tiles(tile):
    low = lax.broadcasted_iota(jnp.int32, tile.shape, 1) < HEAD_DIM
    swapped = jnp.concatenate([tile[:, HEAD_DIM:], tile[:, :HEAD_DIM]], axis=1)
    zero = jnp.zeros_like(tile)
    return ((jnp.where(low, tile, zero), jnp.where(low, zero, swapped)),
            (jnp.where(low, swapped, zero), jnp.where(low, zero, tile)))


def _stack_pairs(ref, row0, j):
    return jnp.concatenate(
        [ref[pl.ds(row0, WINDOW), (j * PAIRS + p) * LANES:(j * PAIRS + p + 1) * LANES] for p in range(PAIRS)], axis=0)


def _per_pair_row(values):
    pair = lax.broadcasted_iota(jnp.int32, (1, STACK), 1) // WINDOW
    row = jnp.full((1, STACK), values[PAIRS - 1], F32)
    for p in range(PAIRS - 2, -1, -1):
        row = jnp.where(pair == p, values[p], row)
    return row


def _attn_fwd(z, sinks, rider):
    t = z.shape[0]
    tq = min(TOKEN_TILE, t)
    nblk = tq // WINDOW

    def body(q_ref, kv_ref, sink_ref, o_ref, lse_ref, bias_ref):
        i = pl.program_id(0)

        @pl.when(i == 0)
        def _():
            _fill_window_bias(bias_ref)

        def window(b):
            row0 = pl.multiple_of(b * WINDOW, WINDOW)
            start = i * tq + b * WINDOW
            prev = pl.multiple_of(jnp.maximum(start - WINDOW, 0), WINDOW)
            cur = pl.multiple_of(start, WINDOW)
            kvw = jnp.concatenate([kv_ref[pl.ds(prev, WINDOW), :], kv_ref[pl.ds(cur, WINDOW), :]], axis=0)
            return row0, _half_tiles(kvw[:, :KV_WIDTH]), _half_tiles(kvw[:, KV_WIDTH:]), bias_ref[jnp.minimum(start, 1)]

        def block_group(bb, carry):
            windows = [window(bb * ATTN_BLOCKS + n) for n in range(ATTN_BLOCKS)]
            for j in range(N_KV_HEADS):
                for pr in range(PAIRS):
                    cols = slice((j * PAIRS + pr) * LANES, (j * PAIRS + pr + 1) * LANES)
                    o_ts = [jnp.zeros((LANES, WINDOW), F32) for _ in windows]
                    for parity in range(2):
                        h = j * GROUP + 2 * pr + parity
                        sink = sink_ref[h] * LOG2E
                        for n, (row0, k_halves, v_halves, bias) in enumerate(windows):
                            qp = q_ref[pl.ds(row0, WINDOW), cols]
                            s = lax.dot_general(k_halves[j][parity], qp, NT_DIMS, preferred_element_type=F32)
                            s = s * SCORE_SCALE + bias
                            m = jnp.maximum(jnp.max(s, axis=0, keepdims=True), sink)
                            p = jnp.exp2(s - m)
                            denom = jnp.sum(p, axis=0, keepdims=True) + jnp.exp2(sink - m)
                            pv = lax.dot_general(v_halves[j][parity], p.astype(BF16), TN_DIMS,
                                                 preferred_element_type=F32)
                            o_ts[n] = o_ts[n] + pv * (1.0 / denom)
                            lse_ref[h:h + 1, pl.ds(row0, WINDOW)] = m + jnp.log2(denom)
                    for n, (row0, _, _, _) in enumerate(windows):
                        o_ref[pl.ds(row0, WINDOW), cols] = jnp.transpose(o_ts[n].astype(BF16))
            return carry

        lax.fori_loop(0, nblk // ATTN_BLOCKS, block_group, 0)

    return _call(
        body, "attn_fwd", (t // tq,), [z, z, sinks],
        [pl.BlockSpec((tq, D_MODEL), lambda i: (i, 0)),
         pl.BlockSpec((t, 2 * KV_WIDTH), lambda i: (0, KV_COL // (2 * KV_WIDTH))),
         pl.BlockSpec(memory_space=pltpu.SMEM)],
        [jax.ShapeDtypeStruct((t, D_MODEL), BF16), jax.ShapeDtypeStruct((N_Q_HEADS, t), F32)],
        [pl.BlockSpec((tq, D_MODEL), lambda i: (i, 0)), pl.BlockSpec((N_Q_HEADS, tq), lambda i: (0, i))],
        scratch=[pltpu.VMEM((2, 2 * WINDOW, WINDOW), F32)], rider=rider)


HALO = BF16_ROWS


def _shift_down(u, uh, k):
    rolled = pltpu.roll(u, k, 0)
    row = lax.broadcasted_iota(jnp.int32, (SUBLANES, u.shape[1]), 0)
    top = rolled[:SUBLANES, :]
    for j in range(k):
        top = jnp.where(row == j, uh[HALO - k + j:HALO - k + j + 1, :], top)
    return jnp.concatenate([top, rolled[SUBLANES:, :]], axis=0)


def _shift_up(u, nxt, k):
    n = u.shape[0]
    rolled = pltpu.roll(u, n - k, 0)
    row = lax.broadcasted_iota(jnp.int32, (SUBLANES, u.shape[1]), 0)
    bottom = rolled[n - SUBLANES:, :]
    for j in range(k):
        bottom = jnp.where(row == SUBLANES - k + j, nxt[j:j + 1, :], bottom)
    return jnp.concatenate([rolled[:n - SUBLANES, :], bottom], axis=0)


def _conv_inputs(cc_ref, cx_ref, hc_ref, hx_ref, first_tile):
    cc = cc_ref[...].astype(F32)
    cx = cx_ref[...].astype(F32)
    u = cc * cx
    uh = jnp.where(first_tile, 0.0, hc_ref[...].astype(F32) * hx_ref[...].astype(F32))
    return cc, cx, u, _shift_down(u, uh, 1), _shift_down(u, uh, 2)


def _z_specs(tm, order):
    per_tile = tm // HALO
    cols = [pl.BlockSpec((tm, D_MODEL), functools.partial(lambda i, j: (order(i), j), j=j)) for j in range(1, 6)]
    halos = [pl.BlockSpec((HALO, D_MODEL),
                          functools.partial(lambda i, j: (jnp.maximum(order(i) * per_tile - 1, 0), j), j=j))
             for j in (2, 3)]
    return cols + halos


def _mix_fwd(x, attn, z, vec, w_out):
    t = x.shape[0]
    tm = min(TOKEN_TILE, t)

    def body(x_ref, a_ref, cb_ref, cc_ref, cx_ref, ga_ref, gc_ref, hc_ref, hx_ref, vec_ref, w_ref,
             m_ref, x2_ref, h2_ref, o_ref):
        i = pl.program_id(0)
        _, _, u, u1, u2 = _conv_inputs(cc_ref, cx_ref, hc_ref, hx_ref, i == 0)
        cv = vec_ref[4:5, :] * u2 + vec_ref[5:6, :] * u1 + vec_ref[6:7, :] * u
        conv = cb_ref[...].astype(F32) * cv
        merged = (_sigmoid(ga_ref[...].astype(F32)) * a_ref[...].astype(F32)
                  + _sigmoid(gc_ref[...].astype(F32)) * conv)
        mb = merged.astype(BF16)
        m_ref[...] = mb
        o = jnp.dot(mb, w_ref[...], preferred_element_type=F32)
        o_ref[...] = o.astype(BF16)
        x2 = x_ref[...] + vec_ref[0:1, :] * o
        x2_ref[...] = x2
        r = lax.rsqrt(jnp.mean(x2 * x2, axis=-1, keepdims=True) + EPS)
        h2 = (x2 * r) * (vec_ref[1:2, :] * (1.0 + vec_ref[2:3, :])) + vec_ref[3:4, :]
        h2_ref[...] = h2.astype(BF16)

    tok = pl.BlockSpec((tm, D_MODEL), lambda i: (i, 0))
    outs, _ = _call(
        body, "mix_fwd", (t // tm,), [x, attn, z, z, z, z, z, z, z, vec, w_out],
        [tok, tok] + _z_specs(tm, lambda i: i) + [_full((SUBLANES, D_MODEL)), _full((D_MODEL, D_MODEL))],
        [jax.ShapeDtypeStruct((t, D_MODEL), BF16), jax.ShapeDtypeStruct((t, D_MODEL), F32),
         jax.ShapeDtypeStruct((t, D_MODEL), BF16), jax.ShapeDtypeStruct((t, D_MODEL), BF16)],
        [tok, tok, tok, tok])
    return outs


def _ffn_fwd(h2, w_t):
    t = h2.shape[0]
    tm = min(TOKEN_TILE, t)

    def body(h_ref, w_ref, gu_ref, a_ref):
        hb = h_ref[...]
        for n in range(D_FF // FF_CHUNK):
            lo, hi = n * FF_CHUNK, (n + 1) * FF_CHUNK
            g = lax.dot_general(hb, w_ref[lo:hi, :], NT_DIMS, preferred_element_type=F32)
            u = lax.dot_general(hb, w_ref[D_FF + lo:D_FF + hi, :], NT_DIMS, preferred_element_type=F32)
            sg = _sigmoid(g)
            silu = g * sg
            gu_ref[:, lo:hi] = (u * (sg + silu * (1.0 - sg))).astype(BF16)
            gu_ref[:, D_FF + lo:D_FF + hi] = silu.astype(BF16)
            a_ref[:, lo:hi] = (silu * u).astype(BF16)

    outs, _ = _call(
        body, "ffn_fwd", (t // tm,), [h2, w_t],
        [pl.BlockSpec((tm, D_MODEL), lambda i: (i, 0)), _full((2 * D_FF, D_MODEL))],
        [jax.ShapeDtypeStruct((t, 2 * D_FF), BF16), jax.ShapeDtypeStruct((t, D_FF), BF16)],
        [pl.BlockSpec((tm, 2 * D_FF), lambda i: (i, 0)), pl.BlockSpec((tm, D_FF), lambda i: (i, 0))])
    return outs


def _ffn_out_loss(a, gu, x2, target, vec, w_ffn_out):
    t = a.shape[0]
    tm = min(TOKEN_TILE, t)

    def body(a_ref, gu_ref, x2_ref, t_ref, vec_ref, w_ref, dx3_ref, df_ref, dgu_ref, acc_ref):
        @pl.when(pl.program_id(0) == 0)
        def _():
            acc_ref[...] = jnp.zeros_like(acc_ref)

        ga2 = vec_ref[0:1, :]
        gf = vec_ref[1:2, :]
        parts = min(ROW_PARTS, tm // LANES)
        part_rows = [slice(n * (tm // parts), (n + 1) * (tm // parts)) for n in range(parts)]

        def head(rows, f):
            x3 = x2_ref[rows, :] + ga2 * f
            r = lax.rsqrt(jnp.mean(x3 * x3, axis=-1, keepdims=True) + EPS)
            xn = x3 * r
            err = xn * gf - t_ref[rows, :]
            dxn = err * (gf * (1.0 / D_MODEL))
            dx3 = r * (dxn - xn * jnp.mean(dxn * xn, axis=-1, keepdims=True))
            dx3_ref[rows, :] = dx3.astype(GRAD_STREAM)
            sums = (jnp.sum(err * err, axis=0, keepdims=True),
                    jnp.sum(err * xn, axis=0, keepdims=True) * (1.0 / D_MODEL),
                    jnp.sum(dx3 * f, axis=0, keepdims=True))
            df = (dx3 * ga2).astype(BF16)
            df_ref[rows, :] = df
            return df, sums

        def tail(rows, df):
            for n in range(D_FF // FF_CHUNK):
                lo, hi = n * FF_CHUNK, (n + 1) * FF_CHUNK
                da = lax.dot_general(df, w_ref[lo:hi, :], NT_DIMS, preferred_element_type=F32)
                dgu_ref[rows, lo:hi] = (da * gu_ref[rows, lo:hi].astype(F32)).astype(BF16)
                dgu_ref[rows, D_FF + lo:D_FF + hi] = (da * gu_ref[rows, D_FF + lo:D_FF + hi].astype(F32)).astype(BF16)

        fs = [jnp.dot(a_ref[rows, :], w_ref[...], preferred_element_type=F32) for rows in part_rows]
        heads = [head(rows, f) for rows, f in zip(part_rows, fs)]
        for rows, (df, _) in zip(part_rows, heads):
            tail(rows, df)
        for k in range(3):
            total = heads[0][1][k]
            for _, sums in heads[1:]:
                total = total + sums[k]
            acc_ref[k:k + 1, :] += total

    tok = pl.BlockSpec((tm, D_MODEL), lambda i: (i, 0))
    outs, _ = _call(
        body, "ffn_out_loss", (t // tm,), [a, gu, x2, target, vec, w_ffn_out],
        [pl.BlockSpec((tm, D_FF), lambda i: (i, 0)), pl.BlockSpec((tm, 2 * D_FF), lambda i: (i, 0)),
         tok, tok, _full((SUBLANES, D_MODEL)), _full((D_FF, D_MODEL))],
        [jax.ShapeDtypeStruct((t, D_MODEL), GRAD_STREAM), jax.ShapeDtypeStruct((t, D_MODEL), BF16),
         jax.ShapeDtypeStruct((t, 2 * D_FF), BF16), jax.ShapeDtypeStruct((SUBLANES, D_MODEL), F32)],
        [tok, tok, pl.BlockSpec((tm, 2 * D_FF), lambda i: (i, 0)), _full((SUBLANES, D_MODEL))])
    return outs


def _ffn_in_bwd(dgu, x2, dx3, vec, w_t, rider):
    t = x2.shape[0]
    tm = min(TOKEN_TILE, t)

    def body(dgu_ref, x2_ref, dx3_ref, vec_ref, wf_ref, dx2_ref, acc_ref):
        @pl.when(pl.program_id(0) == 0)
        def _():
            acc_ref[...] = jnp.zeros_like(acc_ref)

        gffn = vec_ref[0:1, :]
        sc2 = vec_ref[1:2, :]
        parts = min(ROW_PARTS, tm // LANES)
        part_rows = [slice(n * (tm // parts), (n + 1) * (tm // parts)) for n in range(parts)]
        dhs = [jnp.dot(dgu_ref[rows, :], wf_ref[...], preferred_element_type=F32) for rows in part_rows]
        gs = gffn * (1.0 + sc2)
        sum_dh = jnp.zeros((1, D_MODEL), F32)
        sum_dh_xn = jnp.zeros((1, D_MODEL), F32)
        for rows, dh2 in zip(part_rows, dhs):
            x2 = x2_ref[rows, :]
            r = lax.rsqrt(jnp.mean(x2 * x2, axis=-1, keepdims=True) + EPS)
            xn = x2 * r
            dh_xn = dh2 * xn
            sum_dh = sum_dh + jnp.sum(dh2, axis=0, keepdims=True)
            sum_dh_xn = sum_dh_xn + jnp.sum(dh_xn, axis=0, keepdims=True)
            dx2 = dx3_ref[rows, :].astype(F32) + r * (dh2 * gs - xn * jnp.mean(dh_xn * gs, axis=-1, keepdims=True))
            dx2_ref[rows, :] = dx2.astype(GRAD_STREAM)
        acc_ref[0:1, :] += sum_dh
        acc_ref[1:2, :] += sum_dh_xn * gffn
        acc_ref[2:3, :] += sum_dh_xn * (1.0 + sc2)

    tok = pl.BlockSpec((tm, D_MODEL), lambda i: (i, 0))
    return _call(
        body, "ffn_in_bwd", (t // tm,), [dgu, x2, dx3, vec, w_t],
        [pl.BlockSpec((tm, 2 * D_FF), lambda i: (i, 0)), tok, tok, _full((SUBLANES, D_MODEL)),
         _full((2 * D_FF, D_MODEL))],
        [jax.ShapeDtypeStruct((t, D_MODEL), GRAD_STREAM), jax.ShapeDtypeStruct((SUBLANES, D_MODEL), F32)],
        [tok, _full((SUBLANES, D_MODEL))], rider=rider)


def _mix_bwd(dx2, oproj, attn, z, vec, w_out, rider):
    t = dx2.shape[0]
    tm = min(TOKEN_TILE, t)
    nt = t // tm
    rev = lambda i: nt - 1 - i

    def body(dx2_ref, m_ref, a_ref, cb_ref, cc_ref, cx_ref, ga_ref, gc_ref, hc_ref, hx_ref,
             vec_ref, wo_ref, do_ref, da_ref, dr_ref, acc_ref, carry_ref):
        i = pl.program_id(0)

        @pl.when(i == 0)
        def _():
            acc_ref[...] = jnp.zeros_like(acc_ref)
            carry_ref[...] = jnp.zeros_like(carry_ref)

        ga1 = vec_ref[0:1, :]
        w0, w1, w2 = vec_ref[1:2, :], vec_ref[2:3, :], vec_ref[3:4, :]
        dx2 = dx2_ref[...].astype(F32)
        acc_ref[0:1, :] += jnp.sum(dx2 * m_ref[...].astype(F32), axis=0, keepdims=True)
        do = (dx2 * ga1).astype(BF16)
        do_ref[...] = do
        dm = lax.dot_general(do, wo_ref[...], NT_DIMS, preferred_element_type=F32)

        cc, cx, u, u1, u2 = _conv_inputs(cc_ref, cx_ref, hc_ref, hx_ref, i == nt - 1)
        cv = w0 * u2 + w1 * u1 + w2 * u
        cb = cb_ref[...].astype(F32)
        sa = _sigmoid(ga_ref[...].astype(F32))
        sc = _sigmoid(gc_ref[...].astype(F32))
        attn = a_ref[...].astype(F32)
        dattn = dm * sa
        da_ref[...] = dattn.astype(BF16)
        dconv = dm * sc
        dconv_b = dconv * cv
        dr_ref[:, 3 * D_MODEL:4 * D_MODEL] = (dattn * attn * (1.0 - sa)).astype(BF16)
        dr_ref[:, 4 * D_MODEL:5 * D_MODEL] = (dconv_b * cb * (1.0 - sc)).astype(BF16)
        dr_ref[:, 0:D_MODEL] = dconv_b.astype(BF16)
        dcv = dconv * cb
        acc_ref[1:2, :] += jnp.sum(dcv * u2, axis=0, keepdims=True)
        acc_ref[2:3, :] += jnp.sum(dcv * u1, axis=0, keepdims=True)
        acc_ref[3:4, :] += jnp.sum(dcv * u, axis=0, keepdims=True)
        nxt = carry_ref[...]
        du = w2 * dcv + w1 * _shift_up(dcv, nxt, 1) + w0 * _shift_up(dcv, nxt, 2)
        carry_ref[...] = dcv[0:SUBLANES, :]
        dr_ref[:, D_MODEL:2 * D_MODEL] = (du * cx).astype(BF16)
        dr_ref[:, 2 * D_MODEL:3 * D_MODEL] = (du * cc).astype(BF16)

    tok = pl.BlockSpec((tm, D_MODEL), lambda i: (rev(i), 0))
    return _call(
        body, "mix_bwd", (nt,), [dx2, oproj, attn, z, z, z, z, z, z, z, vec, w_out],
        [tok, tok, tok] + _z_specs(tm, rev) + [_full((SUBLANES, D_MODEL)), _full((D_MODEL, D_MODEL))],
        [jax.ShapeDtypeStruct((t, D_MODEL), BF16), jax.ShapeDtypeStruct((t, D_MODEL), BF16),
         jax.ShapeDtypeStruct((t, REST_WIDTH), BF16), jax.ShapeDtypeStruct((SUBLANES, D_MODEL), F32)],
        [tok, tok, pl.BlockSpec((tm, REST_WIDTH), lambda i: (rev(i), 0)), _full((SUBLANES, D_MODEL))],
        scratch=[pltpu.VMEM((SUBLANES, D_MODEL), F32)], rider=rider)


def _attn_bwd(z, dattn, attn, lse, sinks, rider):
    t = z.shape[0]
    tq = min(TOKEN_TILE, t)
    nblk = tq // WINDOW
    nt = t // tq

    def body(q_ref, kv_ref, do_ref, o_ref, lse_ref, sink_ref, dq_ref, dkv_ref, ds_ref, acc_ref, bias_ref):
        i = pl.program_id(0)

        @pl.when(i == 0)
        def _():
            acc_ref[...] = jnp.zeros_like(acc_ref)
            ds_ref[...] = jnp.zeros_like(ds_ref)
            _fill_window_bias(bias_ref)

        lane = lax.broadcasted_iota(jnp.int32, (1, LANES), 1)
        ind_row = lax.broadcasted_iota(jnp.int32, (SUBLANES, LANES), 0)
        ind_low = lax.broadcasted_iota(jnp.int32, (SUBLANES, LANES), 1) < HEAD_DIM
        indicator = jnp.where(jnp.logical_or(jnp.logical_and(ind_row == 0, ind_low),
                                             jnp.logical_and(ind_row == 1, jnp.logical_not(ind_low))),
                              1.0, 0.0).astype(BF16)
        low = lax.broadcasted_iota(jnp.int32, (2 * WINDOW, LANES), 1) < HEAD_DIM

        def both_heads(even, odd):
            picked = jnp.where(low, even, odd)
            return picked + jnp.concatenate([picked[:, HEAD_DIM:], picked[:, :HEAD_DIM]], axis=1)

        def window(b):
            row0 = pl.multiple_of(b * WINDOW, WINDOW)
            start = i * tq + b * WINDOW
            prev = pl.multiple_of(jnp.maximum(start - WINDOW, 0), WINDOW)
            cur = pl.multiple_of(start, WINDOW)
            kvw = jnp.concatenate([kv_ref[pl.ds(prev, WINDOW), :], kv_ref[pl.ds(cur, WINDOW), :]], axis=0)
            return (row0, prev, cur, _half_tiles(kvw[:, :KV_WIDTH]), _half_tiles(kvw[:, KV_WIDTH:]),
                    bias_ref[jnp.minimum(start, 1)])

        def block_group(bb, dsink):
            windows = [window(bb * ATTN_BWD_BLOCKS + n) for n in range(ATTN_BWD_BLOCKS)]
            dk_groups = [[] for _ in windows]
            dv_groups = [[] for _ in windows]
            for j in range(N_KV_HEADS):
                stacks, deltas, dq_ts = [], [], []
                for row0, _, _, _, _, _ in windows:
                    qst = _stack_pairs(q_ref, row0, j)
                    dost = _stack_pairs(do_ref, row0, j)
                    prod = dost.astype(F32) * _stack_pairs(o_ref, row0, j).astype(F32)
                    prod_hi = prod.astype(BF16)
                    prod_lo = (prod - prod_hi.astype(F32)).astype(BF16)
                    stacks.append((qst, dost))
                    deltas.append(lax.dot_general(indicator, prod_hi, NT_DIMS, preferred_element_type=F32)
                                  + lax.dot_general(indicator, prod_lo, NT_DIMS, preferred_element_type=F32))
                    dq_ts.append(jnp.zeros((LANES, STACK), F32))
                dk_par = [[] for _ in windows]
                dv_par = [[] for _ in windows]
                for parity in range(2):
                    heads = [j * GROUP + 2 * p + parity for p in range(PAIRS)]
                    sink = _per_pair_row([sink_ref[h] * LOG2E for h in heads])
                    for n, (row0, _, _, k_halves, v_halves, bias) in enumerate(windows):
                        qst, dost = stacks[n]
                        kk, vv = k_halves[j][parity], v_halves[j][parity]
                        s = lax.dot_general(kk, qst, NT_DIMS, preferred_element_type=F32) * SCORE_SCALE + bias
                        lse = jnp.concatenate([lse_ref[h:h + 1, pl.ds(row0, WINDOW)] for h in heads], axis=1)
                        p = jnp.exp2(s - lse)
                        dp = lax.dot_general(vv, dost, NT_DIMS, preferred_element_type=F32)
                        delta = deltas[n][parity:parity + 1, :]
                        dsb = (p * (dp - delta)).astype(BF16)
                        dq_ts[n] = dq_ts[n] + lax.dot_general(kk, dsb, TN_DIMS, preferred_element_type=F32)
                        dk_par[n].append(jnp.dot(dsb, qst, preferred_element_type=F32))
                        dv_par[n].append(jnp.dot(p.astype(BF16), dost, preferred_element_type=F32))
                        weighted = jnp.exp2(sink - lse) * delta
                        for pr, h in enumerate(heads):
                            dsink = dsink - jnp.where(
                                lane == h, jnp.sum(weighted[:, pr * WINDOW:(pr + 1) * WINDOW]), 0.0)
                for n, (row0, _, _, _, _, _) in enumerate(windows):
                    dq_st = jnp.transpose((dq_ts[n] * ATTN_SCALE).astype(BF16))
                    for pr in range(PAIRS):
                        dq_ref[pl.ds(row0, WINDOW), (j * PAIRS + pr) * LANES:(j * PAIRS + pr + 1) * LANES] = (
                            dq_st[pr * WINDOW:(pr + 1) * WINDOW, :])
                    dk_groups[n].append(both_heads(dk_par[n][0], dk_par[n][1]))
                    dv_groups[n].append(both_heads(dv_par[n][0], dv_par[n][1]))
            for n, (_, prev, cur, _, _, _) in enumerate(windows):
                blk = jnp.concatenate([jnp.where(low, dk_groups[n][0], dk_groups[n][1]) * ATTN_SCALE,
                                       jnp.where(low, dv_groups[n][0], dv_groups[n][1])], axis=1)
                acc_ref[pl.ds(prev, WINDOW), :] += blk[:WINDOW, :]
                acc_ref[pl.ds(cur, WINDOW), :] += blk[WINDOW:, :]
            return dsink

        dsink = lax.fori_loop(0, nblk // ATTN_BWD_BLOCKS, block_group, jnp.zeros((1, LANES), F32))
        ds_ref[0:1, :] += dsink

        @pl.when(i == nt - 1)
        def _():
            dkv_ref[...] = acc_ref[...].astype(BF16)

    tok = pl.BlockSpec((tq, D_MODEL), lambda i: (i, 0))
    return _call(
        body, "attn_bwd", (nt,), [z, z, dattn, attn, lse, sinks],
        [tok, pl.BlockSpec((t, 2 * KV_WIDTH), lambda i: (0, KV_COL // (2 * KV_WIDTH))), tok, tok,
         pl.BlockSpec((N_Q_HEADS, tq), lambda i: (0, i)), pl.BlockSpec(memory_space=pltpu.SMEM)],
        [jax.ShapeDtypeStruct((t, D_MODEL), BF16), jax.ShapeDtypeStruct((t, 2 * KV_WIDTH), BF16),
         jax.ShapeDtypeStruct((SUBLANES, LANES), F32)],
        [tok, _full((t, 2 * KV_WIDTH)), _full((SUBLANES, LANES))],
        scratch=[pltpu.VMEM((t, 2 * KV_WIDTH), F32), pltpu.VMEM((2, 2 * WINDOW, STACK), F32)], rider=rider)


def _inproj_bwd(dq, drest, dkv, x, dx2, vec, w_t, rider):
    t = x.shape[0]
    tm = min(TOKEN_TILE, t)

    def body(dq_ref, dr_ref, dkv_ref, x_ref, dx2_ref, vec_ref, w_ref, gx_ref, acc_ref, db_ref):
        @pl.when(pl.program_id(0) == 0)
        def _():
            acc_ref[...] = jnp.zeros_like(acc_ref)
            db_ref[...] = jnp.zeros_like(db_ref)

        g = vec_ref[0:1, :]
        sc1 = vec_ref[1:2, :]
        dqb, drb, dkvb = dq_ref[...], dr_ref[...], dkv_ref[...]
        dh = jnp.dot(dqb, w_ref[:REF_KV_COL, :], preferred_element_type=F32)
        dh = dh + jnp.dot(drb, w_ref[REF_REST_COL:, :], preferred_element_type=F32)
        dh = dh + jnp.dot(dkvb, w_ref[REF_KV_COL:REF_REST_COL, :], preferred_element_type=F32)
        db_ref[:, :REF_KV_COL] += jnp.sum(dqb.astype(F32), axis=0, keepdims=True)
        db_ref[:, REF_REST_COL:] += jnp.sum(drb.astype(F32), axis=0, keepdims=True)
        db_ref[:, REF_KV_COL:REF_REST_COL] += jnp.sum(dkvb.astype(F32), axis=0, keepdims=True)
        xf = x_ref[...]
        r = lax.rsqrt(jnp.mean(xf * xf, axis=-1, keepdims=True) + EPS)
        xn = xf * r
        gs = g * (1.0 + sc1)
        dh_xn = dh * xn
        sum_dh_xn = jnp.sum(dh_xn, axis=0, keepdims=True)
        acc_ref[0:1, :] += jnp.sum(dh, axis=0, keepdims=True)
        acc_ref[1:2, :] += sum_dh_xn * g
        acc_ref[2:3, :] += sum_dh_xn * (1.0 + sc1)
        gx_ref[...] = dx2_ref[...].astype(F32) + r * (dh * gs - xn * jnp.mean(dh_xn * gs, axis=-1, keepdims=True))

    tok = pl.BlockSpec((tm, D_MODEL), lambda i: (i, 0))
    return _call(
        body, "inproj_bwd", (t // tm,), [dq, drest, dkv, x, dx2, vec, w_t],
        [tok, pl.BlockSpec((tm, REST_WIDTH), lambda i: (i, 0)),
         pl.BlockSpec((tm, 2 * KV_WIDTH), lambda i: (i, 0)), tok, tok,
         _full((SUBLANES, D_MODEL)), _full((IN_WIDTH, D_MODEL))],
        [jax.ShapeDtypeStruct((t, D_MODEL), F32), jax.ShapeDtypeStruct((SUBLANES, D_MODEL), F32),
         jax.ShapeDtypeStruct((1, IN_WIDTH), F32)],
        [tok, _full((SUBLANES, D_MODEL)), _full((1, IN_WIDTH))], rider=rider)


def _weight_grad(b, a, name, bn, rows=None, row0=0, into=None, rider=None):
    t, n = b.shape
    m = a.shape[1]
    rows = n if rows is None else rows
    tk = min(TOKEN_TILE, t)
    for cand in (4 * TOKEN_TILE, 2 * TOKEN_TILE):
        if t % cand == 0 and 2 * cand * (bn + m) * 2 + bn * m * 4 <= WGRAD_VMEM:
            tk = cand
            break
    nk = t // tk
    block0 = row0 // bn

    def body(b_ref, a_ref, *rest):
        out_ref, acc_ref = rest[-2:]
        k = pl.program_id(1)

        @pl.when(k == 0)
        def _():
            acc_ref[...] = jnp.zeros_like(acc_ref)

        acc_ref[...] += lax.dot_general(b_ref[...], a_ref[...], TN_DIMS, preferred_element_type=F32)

        @pl.when(k == nk - 1)
        def _():
            out_ref[...] = acc_ref[...].astype(BF16)

    outs, routs = _call(
        body, name, (n // bn, nk), [b, a] + ([] if into is None else [into]),
        [pl.BlockSpec((tk, bn), lambda j, k: (k, j)), pl.BlockSpec((tk, m), lambda j, k: (k, 0))]
        + ([] if into is None else [ANY]),
        [jax.ShapeDtypeStruct((rows, m), BF16)], [pl.BlockSpec((bn, m), lambda j, k: (block0 + j, 0))],
        scratch=[pltpu.VMEM((bn, m), F32)], rider=rider, aliases=None if into is None else {2: 0})
    return outs[0], routs


def _to_rows(v):
    n = v.shape[0]
    padded = -(-n // (SUBLANES * LANES)) * SUBLANES * LANES
    return jnp.pad(v, (0, padded - n)).reshape(padded // LANES, LANES)


def _vec_rows(*rows):
    stacked = jnp.concatenate([r.reshape(1, D_MODEL) for r in rows], axis=0)
    return jnp.pad(stacked, ((0, SUBLANES - len(rows)), (0, 0)))


def kernel(x, c, w_ada, b_ada, g_mix, w_in, b_in, sinks, conv_w, w_out, g_ffn, w_ffn_in, w_ffn_out, g_final, loss_target, m_w_ada, m_b_ada, m_g_mix, m_w_in, m_b_in, m_sinks, m_conv_w, m_w_out, m_g_ffn, m_w_ffn_in, m_w_ffn_out, m_g_final, v_w_ada, v_b_ada, v_g_mix, v_w_in, v_b_in, v_sinks, v_conv_w, v_w_out, v_g_ffn, v_w_ffn_in, v_w_ffn_out, v_g_final):
    ix, iy, ic = _my_place()
    me = 4 * ix + 2 * iy + ic
    xs = x[0]
    target = loss_target[0]
    ada_cols = w_ada.shape[2]
    conv_cols = conv_w.shape[2]

    wt_in, wt_fi = jnp.transpose(w_in[0]), jnp.transpose(w_ffn_in[0])
    b_cols = lax.dynamic_slice_in_dim(b_ada, me * ada_cols, ada_cols, axis=1)
    g_in, (cast_fi, cast_out, cast_fo), first, mod_all = _gather_first_weight(
        wt_in, [wt_fi, w_out[0], w_ffn_out[0]], _to_rows(jnp.concatenate([c[0], conv_w[0].reshape(-1)])),
        w_ada[0], b_cols)
    first = first.reshape(N_DEV, -1)
    c_all = first[:, :D_MODEL]
    conv_full = jnp.transpose(first[:, D_MODEL:D_MODEL + 3 * conv_cols].reshape(N_DEV, 3, conv_cols), (1, 0, 2))
    conv_full = conv_full.reshape(3, D_MODEL)
    mod = lax.dynamic_index_in_dim(mod_all, me, axis=1, keepdims=False).reshape(N_MOD, D_MODEL)
    sh1, sc1, ga1, sh2, sc2, ga2 = [mod[i:i + 1] for i in range(N_MOD)]
    w_in_t = g_in.reshape(IN_WIDTH, D_MODEL)
    (z, h1), (g_fi, g_out) = _inproj_fwd(xs, _vec_rows(g_mix, sc1, sh1), w_in_t, b_in,
                                         _gather_rider([cast_fi, cast_out]))
    w_fi_t = g_fi.reshape(2 * D_FF, D_MODEL)
    w_out_full = g_out.reshape(D_MODEL, D_MODEL)
    (attn, lse), (g_fo,) = _attn_fwd(z, sinks[0], _gather_rider([cast_fo]))
    w_fo_full = g_fo.reshape(D_FF, D_MODEL)
    merged, x2, h2, oproj = _mix_fwd(
        xs, attn, z, _vec_rows(ga1, g_ffn, sc2, sh2, conv_full[0], conv_full[1], conv_full[2]), w_out_full)
    gu, act = _ffn_fwd(h2, w_fi_t)
    dx3, df, dgu, acc_l = _ffn_out_loss(act, gu, x2, target, _vec_rows(ga2, g_final), w_fo_full)

    gw_fo, _ = _weight_grad(act, df, "wgrad_ffn_out", D_FF)
    gw_fi, _ = _weight_grad(dgu, h2, "wgrad_ffn_in", D_FF)
    blocks_fo = gw_fo.reshape(N_DEV, D_FF // N_DEV, D_MODEL)
    blocks_fi = gw_fi.reshape(N_DEV, 2 * D_FF // N_DEV, D_MODEL)
    (dx2, acc_f), (sib_fo, sib_fi) = _ffn_in_bwd(dgu, x2, dx3, _vec_rows(g_ffn, sc2), w_fi_t,
                                                 _sibling_rider([blocks_fo, blocks_fi]))
    sums_fo, mine_fo = _sibling_sum(blocks_fo, sib_fo, "sibling_sum_ffn_out")
    sums_fi, mine_fi = _sibling_sum(blocks_fi, sib_fi, "sibling_sum_ffn_in")
    (dout, dattn, drest, acc_m), (ici_fo, ici_fi) = _mix_bwd(
        dx2, oproj, attn, z, _vec_rows(ga1, conv_full[0], conv_full[1], conv_full[2]), w_out_full,
        _chip_rider([sums_fo, sums_fi]))
    gw_out, _ = _weight_grad(merged, dout, "wgrad_out", D_MODEL)
    blocks_out = gw_out.reshape(N_DEV, D_MODEL // N_DEV, D_MODEL)
    (dq, dkv, dsink), _ = _attn_bwd(z, dattn, attn, lse, sinks[0], None)
    gw_in, _ = _weight_grad(drest, h1, "wgrad_in_rest", IN_CHUNK, rows=IN_WIDTH, row0=REF_REST_COL)
    gw_in, _ = _weight_grad(dq, h1, "wgrad_in_q", D_MODEL, rows=IN_WIDTH, row0=0, into=gw_in)
    gw_in, _ = _weight_grad(dkv, h1, "wgrad_in_kv", 2 * KV_WIDTH, rows=IN_WIDTH, row0=REF_KV_COL, into=gw_in)
    blocks_in = gw_in.reshape(N_DEV, IN_WIDTH // N_DEV, D_MODEL)
    (sums_in, mine_in), (sums_out, mine_out) = _sibling_exchange_sum([blocks_in, blocks_out], "sibling_w_in_out")
    (grad_x, acc_i, db_in), (ici_in, ici_out) = _inproj_bwd(dq, drest, dkv, xs, dx2, _vec_rows(g_mix, sc1), w_in_t,
                                                            _chip_rider([sums_in, sums_out]))

    widen = lambda vec: jnp.pad(vec, (0, -vec.shape[0] % D_MODEL))
    packed = jnp.concatenate([
        acc_i[0], acc_i[1], acc_m[0], acc_f[0], acc_f[1], acc_l[2],
        acc_i[2], widen(db_in[0]), acc_f[2], acc_l[1],
        acc_m[1], acc_m[2], acc_m[3], widen(dsink[0]), acc_l[0],
        jnp.zeros(((PACK_ROWS - PACK_SQERR - 1) * D_MODEL,), F32)]).reshape(PACK_ROWS, D_MODEL)
    packed_all = _small_allgather(packed, "gather_small")
    dmod_all = packed_all[:, PACK_DMOD:PACK_DMOD + N_MOD, :].reshape(N_DEV, N_MOD * D_MODEL)
    dmod_cols = lax.dynamic_slice_in_dim(dmod_all, me * ada_cols, ada_cols, axis=1)
    g_w_ada = _ada_weight_grad(c_all, dmod_cols)
    row_of = lambda a: a.reshape(1, -1)
    small, g_conv_full, loss = _small_finalize(packed_all, {
        "b_ada": (b_ada, m_b_ada, v_b_ada), "g_mix": (g_mix, m_g_mix, v_g_mix), "b_in": (b_in, m_b_in, v_b_in),
        "g_ffn": (g_ffn, m_g_ffn, v_g_ffn), "sinks": (sinks, m_sinks, v_sinks),
        "g_final": (row_of(g_final), row_of(m_g_final), row_of(v_g_final))})
    small["g_final"] = tuple(o.reshape(g_final.shape) for o in small["g_final"])
    g_conv = lax.dynamic_slice_in_dim(g_conv_full, me * conv_cols, conv_cols, axis=1)
    d_conv, nm_conv, nv_conv = _adamw(conv_w[0], g_conv, m_conv_w[0], v_conv_w[0], "adamw_conv_w")
    small["conv_w"] = (g_conv[None], d_conv[None], nm_conv[None], nv_conv[None])

    def reduced(mine, ici, w, m, v, name, transposed=False):
        turn = jnp.transpose if transposed else (lambda a: a)
        return tuple(turn(o)[None] for o in _chip_sum_adamw(mine, ici, turn(w[0]), turn(m[0]), turn(v[0]), name))

    d_ada, nm_ada, nv_ada = _adamw(w_ada[0], g_w_ada, m_w_ada[0], v_w_ada[0], "adamw_w_ada")
    res = {
        "w_ada": (g_w_ada[None], d_ada[None], nm_ada[None], nv_ada[None]),
        "w_in": reduced(mine_in, ici_in, w_in, m_w_in, v_w_in, "adamw_w_in", transposed=True),
        "w_out": reduced(mine_out, ici_out, w_out, m_w_out, v_w_out, "adamw_w_out"),
        "w_ffn_in": reduced(mine_fi, ici_fi, w_ffn_in, m_w_ffn_in, v_w_ffn_in, "adamw_w_ffn_in", transposed=True),
        "w_ffn_out": reduced(mine_fo, ici_fo, w_ffn_out, m_w_ffn_out, v_w_ffn_out, "adamw_w_ffn_out"),
    }
    res.update(small)
    order = ["w_ada", "b_ada", "g_mix", "w_in", "b_in", "sinks", "conv_w", "w_out", "g_ffn", "w_ffn_in", "w_ffn_out",
             "g_final"]
    outs = [loss.reshape(()), grad_x[None]]
    for k in range(4):
        outs += [res[n][k] for n in order]
    return tuple(outs)
```

```python
import functools
import math

import jax
import jax.numpy as jnp
from jax import lax
from jax.experimental import pallas as pl
from jax.experimental.pallas import tpu as pltpu

F32 = jnp.float32
BF16 = jnp.bfloat16
GRAD_STREAM = F32

D_MODEL = 1024
HEAD_DIM = 64
N_Q_HEADS = 16
N_KV_HEADS = 2
GROUP = 8
WINDOW = 128
KV_WIDTH = N_KV_HEADS * HEAD_DIM
D_FF = 2816
IN_WIDTH = 6400
N_MOD = 6
EPS = 1e-6
N_DEV = 8
REST_WIDTH = 5 * D_MODEL
KV_COL = D_MODEL + REST_WIDTH
ATTN_SCALE = HEAD_DIM ** -0.5

ADAM_LR = 0.001
ADAM_B1 = 0.9
ADAM_B2 = 0.999
ADAM_EPS = 1e-08
ADAM_WD = 0.01
ADAM_STEP = 10

LANES = 128
SUBLANES = 8
BF16_ROWS = 16
VMEM_LIMIT = 56 * 1024 * 1024
TOKEN_TILE = 512
FF_CHUNK = 256
ROW_PARTS = 2
MIN_STREAM_STEPS = 2
WGRAD_VMEM = 40 * 1024 * 1024
MESH = pl.DeviceIdType.MESH
ANY = pl.BlockSpec(memory_space=pl.ANY)

NT_DIMS = (((1,), (1,)), ((), ()))
TN_DIMS = (((0,), (0,)), ((), ()))
CHIP_FLIPS = [(0, 0), (1, 0), (0, 1), (1, 1)]


def _full(shape):
    return pl.BlockSpec(shape, lambda *_: (0,) * len(shape))


def _my_place():
    return lax.axis_index("x"), lax.axis_index("y"), lax.axis_index("c")


def _flip(v, bit):
    return 1 - v if bit else v


def _sigmoid(v):
    return 1.0 / (1.0 + jnp.exp2(v * (-1.4426950408889634)))


class _Rider:
    def __init__(self, ins, out_shapes, sem_shapes, first=None, mid=None, last=None, ins_in_vmem=False):
        self.ins, self.out_shapes, self.sem_shapes = list(ins), list(out_shapes), list(sem_shapes)
        self.in_specs = [_full(a.shape) if ins_in_vmem else ANY for a in self.ins]
        self.hooks = [(when, fn) for when, fn in (("first", first), ("mid", mid), ("last", last)) if fn is not None]


def _call(body, name, grid, args, in_specs, out_shape, out_specs, scratch=(), rider=None, aliases=None):
    n_in, n_out, n_scr = len(args), len(out_shape), len(scratch)
    r_in = rider.ins if rider else []
    r_out = rider.out_shapes if rider else []
    r_sem = rider.sem_shapes if rider else []
    nsteps = math.prod(grid)

    def full_body(*refs):
        pos = 0
        groups = []
        for size in (n_in, len(r_in), n_out, len(r_out), n_scr, len(r_sem)):
            groups.append(refs[pos:pos + size])
            pos += size
        ins, rins, outs, routs, scr, rsems = groups
        step = pl.program_id(0)
        for axis in range(1, len(grid)):
            step = step * grid[axis] + pl.program_id(axis)
        at = {"first": 0, "mid": (13 * nsteps) // 16, "last": nsteps - 1}
        hooks = rider.hooks if rider else []
        for when, fn in hooks:
            if when != "last":
                pl.when(step == at[when])(functools.partial(fn, rins, routs, rsems))
        body(*ins, *outs, *scr)
        for when, fn in hooks:
            if when == "last":
                pl.when(step == at[when])(functools.partial(fn, rins, routs, rsems))

    outs = pl.pallas_call(
        full_body, name=name, grid=grid,
        out_shape=list(out_shape) + list(r_out),
        in_specs=list(in_specs) + (rider.in_specs if rider else []),
        out_specs=list(out_specs) + [ANY] * len(r_out),
        scratch_shapes=list(scratch) + list(r_sem),
        input_output_aliases=dict(aliases or {}),
        compiler_params=pltpu.CompilerParams(dimension_semantics=("arbitrary",) * len(grid),
                                             vmem_limit_bytes=VMEM_LIMIT),
    )(*args, *r_in)
    return list(outs[:n_out]), list(outs[n_out:])


def _gather_rider(shards):
    n = len(shards)

    def setup(outs, sems):
        x, y, c = _my_place()
        send_sems, recv_sems, _ = sems
        chips = [(1 - x, y), (x, 1 - y), (1 - x, 1 - y)]

        def block(w, place):
            return outs[w].at[4 * place[0] + 2 * place[1] + place[2]]

        def copy(w, k, place, to, src=None):
            return pltpu.make_async_remote_copy(
                src_ref=block(w, place) if src is None else src, dst_ref=block(w, place),
                send_sem=send_sems.at[w, k], recv_sem=recv_sems.at[w, k], device_id=to, device_id_type=MESH)

        return (x, y, c), (x, y, 1 - c), chips, block, copy

    def first(ins, outs, sems):
        me, sibling, chips, block, copy = setup(outs, sems)
        for w in range(n):
            pltpu.make_async_copy(ins[w], block(w, me), sems[2].at[w]).start()
            copy(w, 0, me, sibling, src=ins[w]).start()
            for j, chip in enumerate(chips):
                copy(w, 1 + j, me, (*chip, me[2]), src=ins[w]).start()

    def mid(ins, outs, sems):
        me, sibling, chips, block, copy = setup(outs, sems)
        for w in range(n):
            for j, chip in enumerate(chips):
                copy(w, 1 + j, (*chip, me[2]), me).wait_recv()
                copy(w, 4 + j, (*chip, me[2]), sibling).start()

    def last(ins, outs, sems):
        me, sibling, chips, block, copy = setup(outs, sems)
        for w in range(n):
            copy(w, 0, sibling, me).wait_recv()
            for j, chip in enumerate(chips):
                copy(w, 4 + j, (*chip, 1 - me[2]), me).wait_recv()
            copy(w, 0, me, sibling, src=ins[w]).wait_send()
            for j, chip in enumerate(chips):
                copy(w, 1 + j, me, (*chip, me[2]), src=ins[w]).wait_send()
                copy(w, 4 + j, (*chip, me[2]), sibling).wait_send()
            pltpu.make_async_copy(ins[w], block(w, me), sems[2].at[w]).wait()

    return _Rider(
        shards, [jax.ShapeDtypeStruct((N_DEV,) + s.shape, BF16) for s in shards],
        [pltpu.SemaphoreType.DMA((n, N_DEV - 1)), pltpu.SemaphoreType.DMA((n, N_DEV - 1)),
         pltpu.SemaphoreType.DMA((n,))],
        first=first, mid=mid, last=last, ins_in_vmem=True)


def _sibling_rider(gblocks):
    n = len(gblocks)

    def copies(ins, outs, sems):
        x, y, c = _my_place()
        send_sems, recv_sems = sems
        made = []
        for w in range(n):
            for f, (fx, fy) in enumerate(CHIP_FLIPS):
                chip = 4 * _flip(x, fx) + 2 * _flip(y, fy)
                made.append(pltpu.make_async_remote_copy(
                    src_ref=ins[w].at[chip + 1 - c], dst_ref=outs[w].at[f], send_sem=send_sems.at[w, f],
                    recv_sem=recv_sems.at[w, f], device_id=(x, y, 1 - c), device_id_type=MESH))
        return made

    def first(ins, outs, sems):
        for cp in copies(ins, outs, sems):
            cp.start()

    def last(ins, outs, sems):
        for cp in copies(ins, outs, sems):
            cp.wait_recv()
            cp.wait_send()

    return _Rider(gblocks, [jax.ShapeDtypeStruct((4,) + g.shape[1:], BF16) for g in gblocks],
                  [pltpu.SemaphoreType.DMA((n, 4))] * 2, first=first, last=last)


def _chip_rider(sums):
    n = len(sums)

    def copies(ins, outs, sems):
        x, y, c = _my_place()
        send_sems, recv_sems = sems
        made = []
        for w in range(n):
            for f in (1, 2, 3):
                fx, fy = CHIP_FLIPS[f]
                made.append(pltpu.make_async_remote_copy(
                    src_ref=ins[w].at[f - 1], dst_ref=outs[w].at[f - 1], send_sem=send_sems.at[w, f - 1],
                    recv_sem=recv_sems.at[w, f - 1], device_id=(_flip(x, fx), _flip(y, fy), c), device_id_type=MESH))
        return made

    def first(ins, outs, sems):
        for cp in copies(ins, outs, sems):
            cp.start()

    def last(ins, outs, sems):
        for cp in copies(ins, outs, sems):
            cp.wait_recv()
            cp.wait_send()

    return _Rider(sums, [jax.ShapeDtypeStruct(s.shape, BF16) for s in sums],
                  [pltpu.SemaphoreType.DMA((n, 3))] * 2, first=first, last=last)


def _push_to_all(v_ref, out_ref, send_sems, recv_sems, local_sem, wait=True):
    x, y, c = _my_place()
    me = 4 * x + 2 * y + c
    mine = pltpu.make_async_copy(v_ref, out_ref.at[me], local_sem)
    mine.start()
    sends = []
    for k in range(1, N_DEV):
        px, py, pc = _flip(x, k & 4), _flip(y, k & 2), _flip(c, k & 1)
        cp = pltpu.make_async_remote_copy(
            src_ref=v_ref, dst_ref=out_ref.at[me], send_sem=send_sems.at[k - 1], recv_sem=recv_sems.at[k - 1],
            device_id=(px, py, pc), device_id_type=MESH)
        cp.start()
        sends.append(cp)

    def finish():
        for k in range(1, N_DEV):
            px, py, pc = _flip(x, k & 4), _flip(y, k & 2), _flip(c, k & 1)
            pltpu.make_async_remote_copy(
                src_ref=v_ref, dst_ref=out_ref.at[4 * px + 2 * py + pc], send_sem=send_sems.at[k - 1],
                recv_sem=recv_sems.at[k - 1], device_id=(px, py, pc), device_id_type=MESH).wait_recv()
        for cp in sends:
            cp.wait_send()
        mine.wait()

    if wait:
        finish()
    return finish


def _small_allgather(v, name):
    def body(v_ref, out_ref, send_sems, recv_sems, local_sem):
        _push_to_all(v_ref, out_ref, send_sems, recv_sems, local_sem)

    return pl.pallas_call(
        body, name=name,
        out_shape=jax.ShapeDtypeStruct((N_DEV,) + v.shape, F32),
        in_specs=[pl.BlockSpec(memory_space=pltpu.VMEM)],
        out_specs=pl.BlockSpec(memory_space=pltpu.VMEM),
        scratch_shapes=[pltpu.SemaphoreType.DMA((N_DEV - 1,)), pltpu.SemaphoreType.DMA((N_DEV - 1,)),
                        pltpu.SemaphoreType.DMA],
        compiler_params=pltpu.CompilerParams(vmem_limit_bytes=VMEM_LIMIT),
    )(v)


def _gather_first_weight(shard, others, cond_rows, w_ada, b_cols):
    n = len(others)
    ada_cols = w_ada.shape[1]
    c_rows = D_MODEL // LANES

    def body(*refs):
        w_ref, other_refs = refs[0], refs[1:1 + n]
        cond_ref, wada_ref, bcols_ref = refs[1 + n:4 + n]
        out_ref, cast_refs = refs[4 + n], refs[5 + n:5 + 2 * n]
        cond_all_ref, mod_all_ref = refs[5 + 2 * n:7 + 2 * n]
        mine_ref, mod_ref, send_sems, recv_sems, local_sem, small_send, small_recv, small_local = refs[7 + 2 * n:]
        x, y, c = _my_place()
        me, sibling = (x, y, c), (x, y, 1 - c)
        xnb, ynb, diag = (1 - x, y), (x, 1 - y), (1 - x, 1 - y)
        half = shard.shape[0] // 2

        def block(place, part=None):
            ref = out_ref.at[4 * place[0] + 2 * place[1] + place[2]]
            return ref if part is None else ref.at[pl.ds(part * half, half)]

        def copy(k, place, to, part=None, src=None):
            return pltpu.make_async_remote_copy(
                src_ref=block(place, part) if src is None else src, dst_ref=block(place, part),
                send_sem=send_sems.at[k], recv_sem=recv_sems.at[k], device_id=to, device_id_type=MESH)

        finish_cond = _push_to_all(cond_ref, cond_all_ref, small_send.at[0], small_recv.at[0], small_local.at[0],
                                   wait=False)
        mine_ref[...] = w_ref[...].astype(BF16)
        local = pltpu.make_async_copy(mine_ref, block(me), local_sem)
        local.start()
        started = [copy(0, me, sibling, src=mine_ref), copy(1, me, (*xnb, c), src=mine_ref),
                   copy(2, me, (*ynb, c), src=mine_ref)]
        for cp in started:
            cp.start()
        finish_cond()
        mod = jnp.zeros((N_DEV, ada_cols), F32) + bcols_ref[...]
        for r in range(c_rows):
            cf = cond_all_ref[:, r, :]
            act = (cf * _sigmoid(cf)).astype(BF16)
            mod = mod + jnp.dot(act, wada_ref[r * LANES:(r + 1) * LANES, :].astype(BF16),
                                preferred_element_type=F32)
        mod_ref[...] = mod
        finish_mod = _push_to_all(mod_ref, mod_all_ref, small_send.at[1], small_recv.at[1], small_local.at[1],
                                  wait=False)
        for o_ref, c_ref in zip(other_refs, cast_refs):
            c_ref[...] = o_ref[...].astype(BF16)
        def start(cp):
            cp.start()
            started.append(cp)

        copy(1, (*xnb, c), me).wait_recv()
        start(copy(3, (*xnb, c), (*ynb, c), part=0))
        start(copy(5, (*xnb, c), sibling))
        copy(2, (*ynb, c), me).wait_recv()
        start(copy(4, (*ynb, c), (*xnb, c), part=1))
        start(copy(6, (*ynb, c), sibling))
        copy(3, (*diag, c), me, part=0).wait_recv()
        start(copy(7, (*diag, c), sibling, part=0))
        copy(4, (*diag, c), me, part=1).wait_recv()
        start(copy(8, (*diag, c), sibling, part=1))
        copy(0, sibling, me).wait_recv()
        copy(5, (*xnb, 1 - c), me).wait_recv()
        copy(6, (*ynb, 1 - c), me).wait_recv()
        copy(7, (*diag, 1 - c), me, part=0).wait_recv()
        copy(8, (*diag, 1 - c), me, part=1).wait_recv()
        finish_mod()
        for cp in started:
            cp.wait_send()
        local.wait()

    vmem = pl.BlockSpec(memory_space=pltpu.VMEM)
    outs = pl.pallas_call(
        body, name="gather_w_in",
        out_shape=[jax.ShapeDtypeStruct((N_DEV,) + shard.shape, BF16)]
        + [jax.ShapeDtypeStruct(o.shape, BF16) for o in others]
        + [jax.ShapeDtypeStruct((N_DEV,) + cond_rows.shape, F32), jax.ShapeDtypeStruct((N_DEV, N_DEV, ada_cols), F32)],
        in_specs=[vmem] * (4 + n),
        out_specs=[ANY] + [vmem] * (n + 2),
        scratch_shapes=[pltpu.VMEM(shard.shape, BF16), pltpu.VMEM((N_DEV, ada_cols), F32),
                        pltpu.SemaphoreType.DMA((9,)), pltpu.SemaphoreType.DMA((9,)),
                        pltpu.SemaphoreType.DMA,
                        pltpu.SemaphoreType.DMA((2, N_DEV - 1)), pltpu.SemaphoreType.DMA((2, N_DEV - 1)),
                        pltpu.SemaphoreType.DMA((2,))],
        compiler_params=pltpu.CompilerParams(vmem_limit_bytes=VMEM_LIMIT),
    )(shard, *others, cond_rows, w_ada, b_cols)
    return outs[0], list(outs[1:1 + n]), outs[1 + n], outs[2 + n]


def _sibling_exchange_sum(gblocks, name):
    n = len(gblocks)

    def body(*refs):
        g_refs, out_refs = refs[:n], refs[n:3 * n]
        bufs = refs[3 * n:5 * n]
        own_sems, send_sems, recv_sems = refs[5 * n:]
        x, y, c = _my_place()
        pairs = []
        for w in range(n):
            own_buf, sib_buf = bufs[2 * w], bufs[2 * w + 1]
            for f, (fx, fy) in enumerate(CHIP_FLIPS):
                chip = 4 * _flip(x, fx) + 2 * _flip(y, fy)
                own = pltpu.make_async_copy(g_refs[w].at[chip + c], own_buf.at[f], own_sems.at[w, f])
                own.start()
                remote = pltpu.make_async_remote_copy(
                    src_ref=g_refs[w].at[chip + 1 - c], dst_ref=sib_buf.at[f], send_sem=send_sems.at[w, f],
                    recv_sem=recv_sems.at[w, f], device_id=(x, y, 1 - c), device_id_type=MESH)
                remote.start()
                pairs.append((own, remote))
        for w in range(n):
            own_buf, sib_buf = bufs[2 * w], bufs[2 * w + 1]
            sums_ref, mine_ref = out_refs[2 * w], out_refs[2 * w + 1]
            for f in (1, 2, 3, 0):
                own, remote = pairs[4 * w + f]
                own.wait()
                remote.wait_recv()
                total = own_buf[f].astype(F32) + sib_buf[f].astype(F32)
                if f == 0:
                    mine_ref[...] = total
                else:
                    sums_ref[f - 1] = total.astype(BF16)
        for _, remote in pairs:
            remote.wait_send()

    vmem = pl.BlockSpec(memory_space=pltpu.VMEM)
    out_shape, scratch = [], []
    for g in gblocks:
        out_shape += [jax.ShapeDtypeStruct((3,) + g.shape[1:], BF16), jax.ShapeDtypeStruct(g.shape[1:], F32)]
        scratch += [pltpu.VMEM((4,) + g.shape[1:], BF16)] * 2
    outs = pl.pallas_call(
        body, name=name, out_shape=out_shape,
        in_specs=[ANY] * n, out_specs=[vmem] * (2 * n),
        scratch_shapes=scratch + [pltpu.SemaphoreType.DMA((n, 4))] * 3,
        compiler_params=pltpu.CompilerParams(vmem_limit_bytes=VMEM_LIMIT),
    )(*gblocks)
    return [(outs[2 * w], outs[2 * w + 1]) for w in range(n)]


def _ada_weight_grad(c_all, dmod_cols):
    cols = dmod_cols.shape[1]

    def body(c_ref, d_ref, out_ref):
        cf = c_ref[...]
        act = (cf * _sigmoid(cf)).astype(BF16)
        out_ref[...] = lax.dot_general(act, d_ref[...].astype(BF16), TN_DIMS, preferred_element_type=F32)

    return pl.pallas_call(
        body, name="ada_weight_grad",
        out_shape=jax.ShapeDtypeStruct((D_MODEL, cols), F32),
        in_specs=[pl.BlockSpec(memory_space=pltpu.VMEM)] * 2,
        out_specs=pl.BlockSpec(memory_space=pltpu.VMEM),
        compiler_params=pltpu.CompilerParams(vmem_limit_bytes=VMEM_LIMIT),
    )(c_all, dmod_cols)


PACK_ROWS = 24
PACK_DMOD = 0
PACK_PARAMS = {"g_mix": (6, D_MODEL), "b_in": (7, IN_WIDTH), "g_ffn": (14, D_MODEL), "g_final": (15, D_MODEL),
               "sinks": (19, N_Q_HEADS)}
PACK_CONV = 16
PACK_SQERR = 20


def _small_finalize(packed_all, params):
    names = ["b_ada"] + list(PACK_PARAMS)
    layout = dict(PACK_PARAMS, b_ada=(PACK_DMOD, N_MOD * D_MODEL))
    n = len(names)

    def body(*refs):
        p_ref = refs[0]
        ins = refs[1:1 + 3 * n]
        outs = refs[1 + 3 * n:1 + 7 * n]
        conv_ref, loss_ref = refs[1 + 7 * n:]
        total = p_ref[0]
        for d in range(1, N_DEV):
            total = total + p_ref[d]
        for k, name in enumerate(names):
            row0, width = layout[name]
            w_ref, m_ref, v_ref = ins[3 * k:3 * k + 3]
            g_ref, d_ref, nm_ref, nv_ref = outs[4 * k:4 * k + 4]
            for chunk in range(-(-width // D_MODEL)):
                lo = chunk * D_MODEL
                hi = min(lo + D_MODEL, width)
                g = total[row0 + chunk:row0 + chunk + 1, :hi - lo]
                g_ref[:, lo:hi] = g
                d_ref[:, lo:hi], nm_ref[:, lo:hi], nv_ref[:, lo:hi] = _adamw_update(
                    w_ref[:, lo:hi], g, m_ref[:, lo:hi], v_ref[:, lo:hi])
        conv_ref[...] = total[PACK_CONV:PACK_CONV + 3, :]
        loss_ref[...] = (0.5 / D_MODEL) * jnp.sum(total[PACK_SQERR:PACK_SQERR + 1, :], keepdims=True)

    vmem = pl.BlockSpec(memory_space=pltpu.VMEM)
    flat = [a for name in names for a in params[name]]
    out_shape = [jax.ShapeDtypeStruct(params[name][0].shape, F32) for name in names for _ in range(4)]
    outs = pl.pallas_call(
        body, name="small_finalize",
        out_shape=out_shape + [jax.ShapeDtypeStruct((3, D_MODEL), F32), jax.ShapeDtypeStruct((1, 1), F32)],
        in_specs=[vmem] * (1 + 3 * n),
        out_specs=[vmem] * (4 * n + 2),
        compiler_params=pltpu.CompilerParams(vmem_limit_bytes=VMEM_LIMIT),
    )(packed_all, *flat)
    return {name: tuple(outs[4 * k:4 * k + 4]) for k, name in enumerate(names)}, outs[4 * n], outs[4 * n + 1]


def _row_tile(rows, multiple):
    for cand in range(rows // MIN_STREAM_STEPS, 0, -1):
        if rows % cand == 0 and cand % multiple == 0:
            return cand
    return rows


def _adamw_update(w, g, m, v):
    c1 = 1.0 / (1.0 - ADAM_B1 ** ADAM_STEP)
    c2 = 1.0 / (1.0 - ADAM_B2 ** ADAM_STEP)
    nm = ADAM_B1 * m + (1.0 - ADAM_B1) * g
    nv = ADAM_B2 * v + (1.0 - ADAM_B2) * (g * g)
    delta = -ADAM_LR * ((nm * c1) / (jnp.sqrt(nv * c2) + ADAM_EPS) + ADAM_WD * w)
    return delta, nm, nv


def _adamw(w, g, m, v, name):
    rows, cols = w.shape
    tile = _row_tile(rows, SUBLANES)

    def body(w_ref, g_ref, m_ref, v_ref, d_ref, nm_ref, nv_ref):
        d_ref[...], nm_ref[...], nv_ref[...] = _adamw_update(w_ref[...], g_ref[...], m_ref[...], v_ref[...])

    spec = pl.BlockSpec((tile, cols), lambda i: (i, 0))
    outs, _ = _call(body, name, (rows // tile,), [w, g, m, v], [spec] * 4,
                    [jax.ShapeDtypeStruct((rows, cols), F32)] * 3, [spec] * 3)
    return outs


def _sibling_sum(gblocks, sib, name):
    _, r, cdim = gblocks.shape
    tile = _row_tile(r, BF16_ROWS)
    x, y, c = _my_place()
    table = jnp.stack([4 * _flip(x, fx) + 2 * _flip(y, fy) + c for fx, fy in CHIP_FLIPS]).astype(jnp.int32)

    def body(table_ref, own0, own1, own2, own3, sib_ref, sums_ref, mine_ref):
        mine_ref[...] = own0[...].astype(F32) + sib_ref[0].astype(F32)
        for f, own in ((1, own1), (2, own2), (3, own3)):
            sums_ref[f - 1] = (own[...].astype(F32) + sib_ref[f].astype(F32)).astype(BF16)

    own_specs = [pl.BlockSpec((None, tile, cdim), functools.partial(lambda i, tab, f: (tab[f], i, 0), f=f))
                 for f in range(4)]
    return pl.pallas_call(
        body, name=name,
        grid_spec=pltpu.PrefetchScalarGridSpec(
            num_scalar_prefetch=1, grid=(r // tile,),
            in_specs=own_specs + [pl.BlockSpec((4, tile, cdim), lambda i, tab: (0, i, 0))],
            out_specs=[pl.BlockSpec((3, tile, cdim), lambda i, tab: (0, i, 0)),
                       pl.BlockSpec((tile, cdim), lambda i, tab: (i, 0))]),
        out_shape=[jax.ShapeDtypeStruct((3, r, cdim), BF16), jax.ShapeDtypeStruct((r, cdim), F32)],
        compiler_params=pltpu.CompilerParams(dimension_semantics=("arbitrary",), vmem_limit_bytes=VMEM_LIMIT),
    )(table, gblocks, gblocks, gblocks, gblocks, sib)


def _chip_sum_adamw(mine, ici, w, m, v, name):
    r, cdim = mine.shape
    tile = _row_tile(r, BF16_ROWS)

    def body(mine_ref, ici_ref, w_ref, m_ref, v_ref, g_ref, d_ref, nm_ref, nv_ref):
        g = mine_ref[...]
        for f in range(3):
            g = g + ici_ref[f].astype(F32)
        g_ref[...] = g
        d_ref[...], nm_ref[...], nv_ref[...] = _adamw_update(w_ref[...], g, m_ref[...], v_ref[...])

    spec = pl.BlockSpec((tile, cdim), lambda i: (i, 0))
    outs, _ = _call(
        body, name, (r // tile,), [mine, ici, w, m, v],
        [spec, pl.BlockSpec((3, tile, cdim), lambda i: (0, i, 0)), spec, spec, spec],
        [jax.ShapeDtypeStruct((r, cdim), F32)] * 4, [spec] * 4)
    return outs


REF_KV_COL = D_MODEL
REF_REST_COL = D_MODEL + 2 * KV_WIDTH
IN_CHUNK = 1280
IN_PIECES = ([(0, 0, D_MODEL)]
             + [(D_MODEL + n * IN_CHUNK, REF_REST_COL + n * IN_CHUNK, IN_CHUNK) for n in range(REST_WIDTH // IN_CHUNK)]
             + [(KV_COL, REF_KV_COL, 2 * KV_WIDTH)])


def _inproj_fwd(x, vec, w_t, b_in, rider):
    t = x.shape[0]
    tm = min(TOKEN_TILE, t)

    def body(x_ref, vec_ref, w_ref, b_ref, z_ref, h_ref):
        xf = x_ref[...]
        r = lax.rsqrt(jnp.mean(xf * xf, axis=-1, keepdims=True) + EPS)
        h = (xf * r) * (vec_ref[0:1, :] * (1.0 + vec_ref[1:2, :])) + vec_ref[2:3, :]
        hb = h.astype(BF16)
        h_ref[...] = hb
        for mine, ref, width in IN_PIECES:
            zc = lax.dot_general(hb, w_ref[ref:ref + width, :], NT_DIMS, preferred_element_type=F32)
            z_ref[:, mine:mine + width] = (zc + b_ref[:, ref:ref + width]).astype(BF16)

    return _call(
        body, "inproj_fwd", (t // tm,), [x, vec, w_t, b_in],
        [pl.BlockSpec((tm, D_MODEL), lambda i: (i, 0)), _full((SUBLANES, D_MODEL)),
         _full((IN_WIDTH, D_MODEL)), _full((1, IN_WIDTH))],
        [jax.ShapeDtypeStruct((t, IN_WIDTH), BF16), jax.ShapeDtypeStruct((t, D_MODEL), BF16)],
        [pl.BlockSpec((tm, IN_WIDTH), lambda i: (i, 0)), pl.BlockSpec((tm, D_MODEL), lambda i: (i, 0))],
        rider=rider)


PAIRS = GROUP // 2
STACK = PAIRS * WINDOW


ATTN_BLOCKS = 4
ATTN_BWD_BLOCKS = 1
LOG2E = 1.4426950408889634
LN2 = 0.6931471805599453
SCORE_SCALE = ATTN_SCALE * LOG2E


def _fill_window_bias(bias_ref):
    shape = bias_ref.shape[1:]
    kj = lax.broadcasted_iota(jnp.int32, shape, 0)
    qi = jnp.bitwise_and(lax.broadcasted_iota(jnp.int32, shape, 1), WINDOW - 1)
    in_prev = jnp.logical_and(kj < WINDOW, kj > qi)
    in_cur = jnp.logical_and(kj >= WINDOW, (kj - WINDOW) <= qi)
    bias_ref[0] = jnp.where(in_cur, 0.0, -jnp.inf)
    bias_ref[1] = jnp.where(jnp.logical_or(in_prev, in_cur), 0.0, -jnp.inf)


def _half_tiles(tile):
    low = lax.broadcasted_iota(jnp.int32, tile.shape, 1) < HEAD_DIM
    swapped = jnp.concatenate([tile[:, HEAD_DIM:], tile[:, :HEAD_DIM]], axis=1)
    zero = jnp.zeros_like(tile)
    return ((jnp.where(low, tile, zero), jnp.where(low, zero, swapped)),
            (jnp.where(low, swapped, zero), jnp.where(low, zero, tile)))


def _stack_pairs(ref, row0, j):
    return jnp.concatenate(
        [ref[pl.ds(row0, WINDOW), (j * PAIRS + p) * LANES:(j * PAIRS + p + 1) * LANES] for p in range(PAIRS)], axis=0)


def _per_pair_row(values):
    pair = lax.broadcasted_iota(jnp.int32, (1, STACK), 1) // WINDOW
    row = jnp.full((1, STACK), values[PAIRS - 1], F32)
    for p in range(PAIRS - 2, -1, -1):
        row = jnp.where(pair == p, values[p], row)
    return row


def _attn_fwd(z, sinks, rider):
    t = z.shape[0]
    tq = min(TOKEN_TILE, t)
    nblk = tq // WINDOW

    def body(q_ref, kv_ref, sink_ref, o_ref, lse_ref, bias_ref):
        i = pl.program_id(0)

        @pl.when(i == 0)
        def _():
            _fill_window_bias(bias_ref)

        def window(b):
            row0 = pl.multiple_of(b * WINDOW, WINDOW)
            start = i * tq + b * WINDOW
            prev = pl.multiple_of(jnp.maximum(start - WINDOW, 0), WINDOW)
            cur = pl.multiple_of(start, WINDOW)
            kvw = jnp.concatenate([kv_ref[pl.ds(prev, WINDOW), :], kv_ref[pl.ds(cur, WINDOW), :]], axis=0)
            return row0, _half_tiles(kvw[:, :KV_WIDTH]), _half_tiles(kvw[:, KV_WIDTH:]), bias_ref[jnp.minimum(start, 1)]

        def block_group(bb, carry):
            windows = [window(bb * ATTN_BLOCKS + n) for n in range(ATTN_BLOCKS)]
            for j in range(N_KV_HEADS):
                for pr in range(PAIRS):
                    cols = slice((j * PAIRS + pr) * LANES, (j * PAIRS + pr + 1) * LANES)
                    o_ts = [jnp.zeros((LANES, WINDOW), F32) for _ in windows]
                    for parity in range(2):
                        h = j * GROUP + 2 * pr + parity
                        sink = sink_ref[h] * LOG2E
                        for n, (row0, k_halves, v_halves, bias) in enumerate(windows):
                            qp = q_ref[pl.ds(row0, WINDOW), cols]
                            s = lax.dot_general(k_halves[j][parity], qp, NT_DIMS, preferred_element_type=F32)
                            s = s * SCORE_SCALE + bias
                            m = jnp.maximum(jnp.max(s, axis=0, keepdims=True), sink)
                            p = jnp.exp2(s - m)
                            denom = jnp.sum(p, axis=0, keepdims=True) + jnp.exp2(sink - m)
                            pv = lax.dot_general(v_halves[j][parity], p.astype(BF16), TN_DIMS,
                                                 preferred_element_type=F32)
                            o_ts[n] = o_ts[n] + pv * (1.0 / denom)
                            lse_ref[h:h + 1, pl.ds(row0, WINDOW)] = m + jnp.log2(denom)
                    for n, (row0, _, _, _) in enumerate(windows):
                        o_ref[pl.ds(row0, WINDOW), cols] = jnp.transpose(o_ts[n].astype(BF16))
            return carry

        lax.fori_loop(0, nblk // ATTN_BLOCKS, block_group, 0)

    return _call(
        body, "attn_fwd", (t // tq,), [z, z, sinks],
        [pl.BlockSpec((tq, D_MODEL), lambda i: (i, 0)),
         pl.BlockSpec((t, 2 * KV_WIDTH), lambda i: (0, KV_COL // (2 * KV_WIDTH))),
         pl.BlockSpec(memory_space=pltpu.SMEM)],
        [jax.ShapeDtypeStruct((t, D_MODEL), BF16), jax.ShapeDtypeStruct((N_Q_HEADS, t), F32)],
        [pl.BlockSpec((tq, D_MODEL), lambda i: (i, 0)), pl.BlockSpec((N_Q_HEADS, tq), lambda i: (0, i))],
        scratch=[pltpu.VMEM((2, 2 * WINDOW, WINDOW), F32)], rider=rider)


HALO = BF16_ROWS


def _shift_down(u, uh, k):
    rolled = pltpu.roll(u, k, 0)
    row = lax.broadcasted_iota(jnp.int32, (SUBLANES, u.shape[1]), 0)
    top = rolled[:SUBLANES, :]
    for j in range(k):
        top = jnp.where(row == j, uh[HALO - k + j:HALO - k + j + 1, :], top)
    return jnp.concatenate([top, rolled[SUBLANES:, :]], axis=0)


def _shift_up(u, nxt, k):
    n = u.shape[0]
    rolled = pltpu.roll(u, n - k, 0)
    row = lax.broadcasted_iota(jnp.int32, (SUBLANES, u.shape[1]), 0)
    bottom = rolled[n - SUBLANES:, :]
    for j in range(k):
        bottom = jnp.where(row == SUBLANES - k + j, nxt[j:j + 1, :], bottom)
    return jnp.concatenate([rolled[:n - SUBLANES, :], bottom], axis=0)


def _conv_inputs(cc_ref, cx_ref, hc_ref, hx_ref, first_tile):
    cc = cc_ref[...].astype(F32)
    cx = cx_ref[...].astype(F32)
    u = cc * cx
    uh = jnp.where(first_tile, 0.0, hc_ref[...].astype(F32) * hx_ref[...].astype(F32))
    return cc, cx, u, _shift_down(u, uh, 1), _shift_down(u, uh, 2)


def _z_specs(tm, order):
    per_tile = tm // HALO
    cols = [pl.BlockSpec((tm, D_MODEL), functools.partial(lambda i, j: (order(i), j), j=j)) for j in range(1, 6)]
    halos = [pl.BlockSpec((HALO, D_MODEL),
                          functools.partial(lambda i, j: (jnp.maximum(order(i) * per_tile - 1, 0), j), j=j))
             for j in (2, 3)]
    return cols + halos


def _mix_fwd(x, attn, z, vec, w_out):
    t = x.shape[0]
    tm = min(TOKEN_TILE, t)

    def body(x_ref, a_ref, cb_ref, cc_ref, cx_ref, ga_ref, gc_ref, hc_ref, hx_ref, vec_ref, w_ref,
             m_ref, x2_ref, h2_ref, o_ref):
        i = pl.program_id(0)
        _, _, u, u1, u2 = _conv_inputs(cc_ref, cx_ref, hc_ref, hx_ref, i == 0)
        cv = vec_ref[4:5, :] * u2 + vec_ref[5:6, :] * u1 + vec_ref[6:7, :] * u
        conv = cb_ref[...].astype(F32) * cv
        merged = (_sigmoid(ga_ref[...].astype(F32)) * a_ref[...].astype(F32)
                  + _sigmoid(gc_ref[...].astype(F32)) * conv)
        mb = merged.astype(BF16)
        m_ref[...] = mb
        o = jnp.dot(mb, w_ref[...], preferred_element_type=F32)
        o_ref[...] = o.astype(BF16)
        x2 = x_ref[...] + vec_ref[0:1, :] * o
        x2_ref[...] = x2
        r = lax.rsqrt(jnp.mean(x2 * x2, axis=-1, keepdims=True) + EPS)
        h2 = (x2 * r) * (vec_ref[1:2, :] * (1.0 + vec_ref[2:3, :])) + vec_ref[3:4, :]
        h2_ref[...] = h2.astype(BF16)

    tok = pl.BlockSpec((tm, D_MODEL), lambda i: (i, 0))
    outs, _ = _call(
        body, "mix_fwd", (t // tm,), [x, attn, z, z, z, z, z, z, z, vec, w_out],
        [tok, tok] + _z_specs(tm, lambda i: i) + [_full((SUBLANES, D_MODEL)), _full((D_MODEL, D_MODEL))],
        [jax.ShapeDtypeStruct((t, D_MODEL), BF16), jax.ShapeDtypeStruct((t, D_MODEL), F32),
         jax.ShapeDtypeStruct((t, D_MODEL), BF16), jax.ShapeDtypeStruct((t, D_MODEL), BF16)],
        [tok, tok, tok, tok])
    return outs


def _ffn_fwd(h2, w_t):
    t = h2.shape[0]
    tm = min(TOKEN_TILE, t)

    def body(h_ref, w_ref, gu_ref, a_ref):
        hb = h_ref[...]
        for n in range(D_FF // FF_CHUNK):
            lo, hi = n * FF_CHUNK, (n + 1) * FF_CHUNK
            g = lax.dot_general(hb, w_ref[lo:hi, :], NT_DIMS, preferred_element_type=F32)
            u = lax.dot_general(hb, w_ref[D_FF + lo:D_FF + hi, :], NT_DIMS, preferred_element_type=F32)
            sg = _sigmoid(g)
            silu = g * sg
            gu_ref[:, lo:hi] = (u * (sg + silu * (1.0 - sg))).astype(BF16)
            gu_ref[:, D_FF + lo:D_FF + hi] = silu.astype(BF16)
            a_ref[:, lo:hi] = (silu * u).astype(BF16)

    outs, _ = _call(
        body, "ffn_fwd", (t // tm,), [h2, w_t],
        [pl.BlockSpec((tm, D_MODEL), lambda i: (i, 0)), _full((2 * D_FF, D_MODEL))],
        [jax.ShapeDtypeStruct((t, 2 * D_FF), BF16), jax.ShapeDtypeStruct((t, D_FF), BF16)],
        [pl.BlockSpec((tm, 2 * D_FF), lambda i: (i, 0)), pl.BlockSpec((tm, D_FF), lambda i: (i, 0))])
    return outs


def _ffn_out_loss(a, gu, x2, target, vec, w_ffn_out):
    t = a.shape[0]
    tm = min(TOKEN_TILE, t)

    def body(a_ref, gu_ref, x2_ref, t_ref, vec_ref, w_ref, dx3_ref, df_ref, dgu_ref, acc_ref):
        @pl.when(pl.program_id(0) == 0)
        def _():
            acc_ref[...] = jnp.zeros_like(acc_ref)

        ga2 = vec_ref[0:1, :]
        gf = vec_ref[1:2, :]
        parts = min(ROW_PARTS, tm // LANES)
        part_rows = [slice(n * (tm // parts), (n + 1) * (tm // parts)) for n in range(parts)]

        def head(rows, f):
            x3 = x2_ref[rows, :] + ga2 * f
            r = lax.rsqrt(jnp.mean(x3 * x3, axis=-1, keepdims=True) + EPS)
            xn = x3 * r
            err = xn * gf - t_ref[rows, :]
            dxn = err * (gf * (1.0 / D_MODEL))
            dx3 = r * (dxn - xn * jnp.mean(dxn * xn, axis=-1, keepdims=True))
            dx3_ref[rows, :] = dx3.astype(GRAD_STREAM)
            sums = (jnp.sum(err * err, axis=0, keepdims=True),
                    jnp.sum(err * xn, axis=0, keepdims=True) * (1.0 / D_MODEL),
                    jnp.sum(dx3 * f, axis=0, keepdims=True))
            df = (dx3 * ga2).astype(BF16)
            df_ref[rows, :] = df
            return df, sums

        def tail(rows, df):
            for n in range(D_FF // FF_CHUNK):
                lo, hi = n * FF_CHUNK, (n + 1) * FF_CHUNK
                da = lax.dot_general(df, w_ref[lo:hi, :], NT_DIMS, preferred_element_type=F32)
                dgu_ref[rows, lo:hi] = (da * gu_ref[rows, lo:hi].astype(F32)).astype(BF16)
                dgu_ref[rows, D_FF + lo:D_FF + hi] = (da * gu_ref[rows, D_FF + lo:D_FF + hi].astype(F32)).astype(BF16)

        fs = [jnp.dot(a_ref[rows, :], w_ref[...], preferred_element_type=F32) for rows in part_rows]
        heads = [head(rows, f) for rows, f in zip(part_rows, fs)]
        for rows, (df, _) in zip(part_rows, heads):
            tail(rows, df)
        for k in range(3):
            total = heads[0][1][k]
            for _, sums in heads[1:]:
                total = total + sums[k]
            acc_ref[k:k + 1, :] += total

    tok = pl.BlockSpec((tm, D_MODEL), lambda i: (i, 0))
    outs, _ = _call(
        body, "ffn_out_loss", (t // tm,), [a, gu, x2, target, vec, w_ffn_out],
        [pl.BlockSpec((tm, D_FF), lambda i: (i, 0)), pl.BlockSpec((tm, 2 * D_FF), lambda i: (i, 0)),
         tok, tok, _full((SUBLANES, D_MODEL)), _full((D_FF, D_MODEL))],
        [jax.ShapeDtypeStruct((t, D_MODEL), GRAD_STREAM), jax.ShapeDtypeStruct((t, D_MODEL), BF16),
         jax.ShapeDtypeStruct((t, 2 * D_FF), BF16), jax.ShapeDtypeStruct((SUBLANES, D_MODEL), F32)],
        [tok, tok, pl.BlockSpec((tm, 2 * D_FF), lambda i: (i, 0)), _full((SUBLANES, D_MODEL))])
    return outs


def _ffn_in_bwd(dgu, x2, dx3, vec, w_t, rider):
    t = x2.shape[0]
    tm = min(TOKEN_TILE, t)

    def body(dgu_ref, x2_ref, dx3_ref, vec_ref, wf_ref, dx2_ref, acc_ref):
        @pl.when(pl.program_id(0) == 0)
        def _():
            acc_ref[...] = jnp.zeros_like(acc_ref)

        gffn = vec_ref[0:1, :]
        sc2 = vec_ref[1:2, :]
        parts = min(ROW_PARTS, tm // LANES)
        part_rows = [slice(n * (tm // parts), (n + 1) * (tm // parts)) for n in range(parts)]
        dhs = [jnp.dot(dgu_ref[rows, :], wf_ref[...], preferred_element_type=F32) for rows in part_rows]
        gs = gffn * (1.0 + sc2)
        sum_dh = jnp.zeros((1, D_MODEL), F32)
        sum_dh_xn = jnp.zeros((1, D_MODEL), F32)
        for rows, dh2 in zip(part_rows, dhs):
            x2 = x2_ref[rows, :]
            r = lax.rsqrt(jnp.mean(x2 * x2, axis=-1, keepdims=True) + EPS)
            xn = x2 * r
            dh_xn = dh2 * xn
            sum_dh = sum_dh + jnp.sum(dh2, axis=0, keepdims=True)
            sum_dh_xn = sum_dh_xn + jnp.sum(dh_xn, axis=0, keepdims=True)
            dx2 = dx3_ref[rows, :].astype(F32) + r * (dh2 * gs - xn * jnp.mean(dh_xn * gs, axis=-1, keepdims=True))
            dx2_ref[rows, :] = dx2.astype(GRAD_STREAM)
        acc_ref[0:1, :] += sum_dh
        acc_ref[1:2, :] += sum_dh_xn * gffn
        acc_ref[2:3, :] += sum_dh_xn * (1.0 + sc2)

    tok = pl.BlockSpec((tm, D_MODEL), lambda i: (i, 0))
    return _call(
        body, "ffn_in_bwd", (t // tm,), [dgu, x2, dx3, vec, w_t],
        [pl.BlockSpec((tm, 2 * D_FF), lambda i: (i, 0)), tok, tok, _full((SUBLANES, D_MODEL)),
         _full((2 * D_FF, D_MODEL))],
        [jax.ShapeDtypeStruct((t, D_MODEL), GRAD_STREAM), jax.ShapeDtypeStruct((SUBLANES, D_MODEL), F32)],
        [tok, _full((SUBLANES, D_MODEL))], rider=rider)


def _mix_bwd(dx2, oproj, attn, z, vec, w_out, rider):
    t = dx2.shape[0]
    tm = min(TOKEN_TILE, t)
    nt = t // tm
    rev = lambda i: nt - 1 - i

    def body(dx2_ref, m_ref, a_ref, cb_ref, cc_ref, cx_ref, ga_ref, gc_ref, hc_ref, hx_ref,
             vec_ref, wo_ref, do_ref, da_ref, dr_ref, acc_ref, carry_ref):
        i = pl.program_id(0)

        @pl.when(i == 0)
        def _():
            acc_ref[...] = jnp.zeros_like(acc_ref)
            carry_ref[...] = jnp.zeros_like(carry_ref)

        ga1 = vec_ref[0:1, :]
        w0, w1, w2 = vec_ref[1:2, :], vec_ref[2:3, :], vec_ref[3:4, :]
        dx2 = dx2_ref[...].astype(F32)
        acc_ref[0:1, :] += jnp.sum(dx2 * m_ref[...].astype(F32), axis=0, keepdims=True)
        do = (dx2 * ga1).astype(BF16)
        do_ref[...] = do
        dm = lax.dot_general(do, wo_ref[...], NT_DIMS, preferred_element_type=F32)

        cc, cx, u, u1, u2 = _conv_inputs(cc_ref, cx_ref, hc_ref, hx_ref, i == nt - 1)
        cv = w0 * u2 + w1 * u1 + w2 * u
        cb = cb_ref[...].astype(F32)
        sa = _sigmoid(ga_ref[...].astype(F32))
        sc = _sigmoid(gc_ref[...].astype(F32))
        attn = a_ref[...].astype(F32)
        dattn = dm * sa
        da_ref[...] = dattn.astype(BF16)
        dconv = dm * sc
        dconv_b = dconv * cv
        dr_ref[:, 3 * D_MODEL:4 * D_MODEL] = (dattn * attn * (1.0 - sa)).astype(BF16)
        dr_ref[:, 4 * D_MODEL:5 * D_MODEL] = (dconv_b * cb * (1.0 - sc)).astype(BF16)
        dr_ref[:, 0:D_MODEL] = dconv_b.astype(BF16)
        dcv = dconv * cb
        acc_ref[1:2, :] += jnp.sum(dcv * u2, axis=0, keepdims=True)
        acc_ref[2:3, :] += jnp.sum(dcv * u1, axis=0, keepdims=True)
        acc_ref[3:4, :] += jnp.sum(dcv * u, axis=0, keepdims=True)
        nxt = carry_ref[...]
        du = w2 * dcv + w1 * _shift_up(dcv, nxt, 1) + w0 * _shift_up(dcv, nxt, 2)
        carry_ref[...] = dcv[0:SUBLANES, :]
        dr_ref[:, D_MODEL:2 * D_MODEL] = (du * cx).astype(BF16)
        dr_ref[:, 2 * D_MODEL:3 * D_MODEL] = (du * cc).astype(BF16)

    tok = pl.BlockSpec((tm, D_MODEL), lambda i: (rev(i), 0))
    return _call(
        body, "mix_bwd", (nt,), [dx2, oproj, attn, z, z, z, z, z, z, z, vec, w_out],
        [tok, tok, tok] + _z_specs(tm, rev) + [_full((SUBLANES, D_MODEL)), _full((D_MODEL, D_MODEL))],
        [jax.ShapeDtypeStruct((t, D_MODEL), BF16), jax.ShapeDtypeStruct((t, D_MODEL), BF16),
         jax.ShapeDtypeStruct((t, REST_WIDTH), BF16), jax.ShapeDtypeStruct((SUBLANES, D_MODEL), F32)],
        [tok, tok, pl.BlockSpec((tm, REST_WIDTH), lambda i: (rev(i), 0)), _full((SUBLANES, D_MODEL))],
        scratch=[pltpu.VMEM((SUBLANES, D_MODEL), F32)], rider=rider)


def _attn_bwd(z, dattn, attn, lse, sinks, rider):
    t = z.shape[0]
    tq = min(TOKEN_TILE, t)
    nblk = tq // WINDOW
    nt = t // tq

    def body(q_ref, kv_ref, do_ref, o_ref, lse_ref, sink_ref, dq_ref, dkv_ref, ds_ref, acc_ref, bias_ref):
        i = pl.program_id(0)

        @pl.when(i == 0)
        def _():
            acc_ref[...] = jnp.zeros_like(acc_ref)
            ds_ref[...] = jnp.zeros_like(ds_ref)
            _fill_window_bias(bias_ref)

        lane = lax.broadcasted_iota(jnp.int32, (1, LANES), 1)
        ind_row = lax.broadcasted_iota(jnp.int32, (SUBLANES, LANES), 0)
        ind_low = lax.broadcasted_iota(jnp.int32, (SUBLANES, LANES), 1) < HEAD_DIM
        indicator = jnp.where(jnp.logical_or(jnp.logical_and(ind_row == 0, ind_low),
                                             jnp.logical_and(ind_row == 1, jnp.logical_not(ind_low))),
                              1.0, 0.0).astype(BF16)
        low = lax.broadcasted_iota(jnp.int32, (2 * WINDOW, LANES), 1) < HEAD_DIM

        def both_heads(even, odd):
            picked = jnp.where(low, even, odd)
            return picked + jnp.concatenate([picked[:, HEAD_DIM:], picked[:, :HEAD_DIM]], axis=1)

        def window(b):
            row0 = pl.multiple_of(b * WINDOW, WINDOW)
            start = i * tq + b * WINDOW
            prev = pl.multiple_of(jnp.maximum(start - WINDOW, 0), WINDOW)
            cur = pl.multiple_of(start, WINDOW)
            kvw = jnp.concatenate([kv_ref[pl.ds(prev, WINDOW), :], kv_ref[pl.ds(cur, WINDOW), :]], axis=0)
            return (row0, prev, cur, _half_tiles(kvw[:, :KV_WIDTH]), _half_tiles(kvw[:, KV_WIDTH:]),
                    bias_ref[jnp.minimum(start, 1)])

        def block_group(bb, dsink):
            windows = [window(bb * ATTN_BWD_BLOCKS + n) for n in range(ATTN_BWD_BLOCKS)]
            dk_groups = [[] for _ in windows]
            dv_groups = [[] for _ in windows]
            for j in range(N_KV_HEADS):
                stacks, deltas, dq_ts = [], [], []
                for row0, _, _, _, _, _ in windows:
                    qst = _stack_pairs(q_ref, row0, j)
                    dost = _stack_pairs(do_ref, row0, j)
                    prod = dost.astype(F32) * _stack_pairs(o_ref, row0, j).astype(F32)
                    prod_hi = prod.astype(BF16)
                    prod_lo = (prod - prod_hi.astype(F32)).astype(BF16)
                    stacks.append((qst, dost))
                    deltas.append(lax.dot_general(indicator, prod_hi, NT_DIMS, preferred_element_type=F32)
                                  + lax.dot_general(indicator, prod_lo, NT_DIMS, preferred_element_type=F32))
                    dq_ts.append(jnp.zeros((LANES, STACK), F32))
                dk_par = [[] for _ in windows]
                dv_par = [[] for _ in windows]
                for parity in range(2):
                    heads = [j * GROUP + 2 * p + parity for p in range(PAIRS)]
                    sink = _per_pair_row([sink_ref[h] * LOG2E for h in heads])
                    for n, (row0, _, _, k_halves, v_halves, bias) in enumerate(windows):
                        qst, dost = stacks[n]
                        kk, vv = k_halves[j][parity], v_halves[j][parity]
                        s = lax.dot_general(kk, qst, NT_DIMS, preferred_element_type=F32) * SCORE_SCALE + bias
                        lse = jnp.concatenate([lse_ref[h:h + 1, pl.ds(row0, WINDOW)] for h in heads], axis=1)
                        p = jnp.exp2(s - lse)
                        dp = lax.dot_general(vv, dost, NT_DIMS, preferred_element_type=F32)
                        delta = deltas[n][parity:parity + 1, :]
                        dsb = (p * (dp - delta)).astype(BF16)
                        dq_ts[n] = dq_ts[n] + lax.dot_general(kk, dsb, TN_DIMS, preferred_element_type=F32)
                        dk_par[n].append(jnp.dot(dsb, qst, preferred_element_type=F32))
                        dv_par[n].append(jnp.dot(p.astype(BF16), dost, preferred_element_type=F32))
                        weighted = jnp.exp2(sink - lse) * delta
                        for pr, h in enumerate(heads):
                            dsink = dsink - jnp.where(
                                lane == h, jnp.sum(weighted[:, pr * WINDOW:(pr + 1) * WINDOW]), 0.0)
                for n, (row0, _, _, _, _, _) in enumerate(windows):
                    dq_st = jnp.transpose((dq_ts[n] * ATTN_SCALE).astype(BF16))
                    for pr in range(PAIRS):
                        dq_ref[pl.ds(row0, WINDOW), (j * PAIRS + pr) * LANES:(j * PAIRS + pr + 1) * LANES] = (
                            dq_st[pr * WINDOW:(pr + 1) * WINDOW, :])
                    dk_groups[n].append(both_heads(dk_par[n][0], dk_par[n][1]))
                    dv_groups[n].append(both_heads(dv_par[n][0], dv_par[n][1]))
            for n, (_, prev, cur, _, _, _) in enumerate(windows):
                blk = jnp.concatenate([jnp.where(low, dk_groups[n][0], dk_groups[n][1]) * ATTN_SCALE,
                                       jnp.where(low, dv_groups[n][0], dv_groups[n][1])], axis=1)
                acc_ref[pl.ds(prev, WINDOW), :] += blk[:WINDOW, :]
                acc_ref[pl.ds(cur, WINDOW), :] += blk[WINDOW:, :]
            return dsink

        dsink = lax.fori_loop(0, nblk // ATTN_BWD_BLOCKS, block_group, jnp.zeros((1, LANES), F32))
        ds_ref[0:1, :] += dsink

        @pl.when(i == nt - 1)
        def _():
            dkv_ref[...] = acc_ref[...].astype(BF16)

    tok = pl.BlockSpec((tq, D_MODEL), lambda i: (i, 0))
    return _call(
        body, "attn_bwd", (nt,), [z, z, dattn, attn, lse, sinks],
        [tok, pl.BlockSpec((t, 2 * KV_WIDTH), lambda i: (0, KV_COL // (2 * KV_WIDTH))), tok, tok,
         pl.BlockSpec((N_Q_HEADS, tq), lambda i: (0, i)), pl.BlockSpec(memory_space=pltpu.SMEM)],
        [jax.ShapeDtypeStruct((t, D_MODEL), BF16), jax.ShapeDtypeStruct((t, 2 * KV_WIDTH), BF16),
         jax.ShapeDtypeStruct((SUBLANES, LANES), F32)],
        [tok, _full((t, 2 * KV_WIDTH)), _full((SUBLANES, LANES))],
        scratch=[pltpu.VMEM((t, 2 * KV_WIDTH), F32), pltpu.VMEM((2, 2 * WINDOW, STACK), F32)], rider=rider)


def _inproj_bwd(dq, drest, dkv, x, dx2, vec, w_t, rider):
    t = x.shape[0]
    tm = min(TOKEN_TILE, t)

    def body(dq_ref, dr_ref, dkv_ref, x_ref, dx2_ref, vec_ref, w_ref, gx_ref, acc_ref, db_ref):
        @pl.when(pl.program_id(0) == 0)
        def _():
            acc_ref[...] = jnp.zeros_like(acc_ref)
            db_ref[...] = jnp.zeros_like(db_ref)

        g = vec_ref[0:1, :]
        sc1 = vec_ref[1:2, :]
        dqb, drb, dkvb = dq_ref[...], dr_ref[...], dkv_ref[...]
        dh = jnp.dot(dqb, w_ref[:REF_KV_COL, :], preferred_element_type=F32)
        dh = dh + jnp.dot(drb, w_ref[REF_REST_COL:, :], preferred_element_type=F32)
        dh = dh + jnp.dot(dkvb, w_ref[REF_KV_COL:REF_REST_COL, :], preferred_element_type=F32)
        db_ref[:, :REF_KV_COL] += jnp.sum(dqb.astype(F32), axis=0, keepdims=True)
        db_ref[:, REF_REST_COL:] += jnp.sum(drb.astype(F32), axis=0, keepdims=True)
        db_ref[:, REF_KV_COL:REF_REST_COL] += jnp.sum(dkvb.astype(F32), axis=0, keepdims=True)
        xf = x_ref[...]
        r = lax.rsqrt(jnp.mean(xf * xf, axis=-1, keepdims=True) + EPS)
        xn = xf * r
        gs = g * (1.0 + sc1)
        dh_xn = dh * xn
        sum_dh_xn = jnp.sum(dh_xn, axis=0, keepdims=True)
        acc_ref[0:1, :] += jnp.sum(dh, axis=0, keepdims=True)
        acc_ref[1:2, :] += sum_dh_xn * g
        acc_ref[2:3, :] += sum_dh_xn * (1.0 + sc1)
        gx_ref[...] = dx2_ref[...].astype(F32) + r * (dh * gs - xn * jnp.mean(dh_xn * gs, axis=-1, keepdims=True))

    tok = pl.BlockSpec((tm, D_MODEL), lambda i: (i, 0))
    return _call(
        body, "inproj_bwd", (t // tm,), [dq, drest, dkv, x, dx2, vec, w_t],
        [tok, pl.BlockSpec((tm, REST_WIDTH), lambda i: (i, 0)),
         pl.BlockSpec((tm, 2 * KV_WIDTH), lambda i: (i, 0)), tok, tok,
         _full((SUBLANES, D_MODEL)), _full((IN_WIDTH, D_MODEL))],
        [jax.ShapeDtypeStruct((t, D_MODEL), F32), jax.ShapeDtypeStruct((SUBLANES, D_MODEL), F32),
         jax.ShapeDtypeStruct((1, IN_WIDTH), F32)],
        [tok, _full((SUBLANES, D_MODEL)), _full((1, IN_WIDTH))], rider=rider)


def _weight_grad(b, a, name, bn, rows=None, row0=0, into=None, rider=None):
    t, n = b.shape
    m = a.shape[1]
    rows = n if rows is None else rows
    tk = min(TOKEN_TILE, t)
    for cand in (4 * TOKEN_TILE, 2 * TOKEN_TILE):
        if t % cand == 0 and 2 * cand * (bn + m) * 2 + bn * m * 4 <= WGRAD_VMEM:
            tk = cand
            break
    nk = t // tk
    block0 = row0 // bn

    def body(b_ref, a_ref, *rest):
        out_ref, acc_ref = rest[-2:]
        k = pl.program_id(1)

        @pl.when(k == 0)
        def _():
            acc_ref[...] = jnp.zeros_like(acc_ref)

        acc_ref[...] += lax.dot_general(b_ref[...], a_ref[...], TN_DIMS, preferred_element_type=F32)

        @pl.when(k == nk - 1)
        def _():
            out_ref[...] = acc_ref[...].astype(BF16)

    outs, routs = _call(
        body, name, (n // bn, nk), [b, a] + ([] if into is None else [into]),
        [pl.BlockSpec((tk, bn), lambda j, k: (k, j)), pl.BlockSpec((tk, m), lambda j, k: (k, 0))]
        + ([] if into is None else [ANY]),
        [jax.ShapeDtypeStruct((rows, m), BF16)], [pl.BlockSpec((bn, m), lambda j, k: (block0 + j, 0))],
        scratch=[pltpu.VMEM((bn, m), F32)], rider=rider, aliases=None if into is None else {2: 0})
    return outs[0], routs


def _to_rows(v):
    n = v.shape[0]
    padded = -(-n // (SUBLANES * LANES)) * SUBLANES * LANES
    return jnp.pad(v, (0, padded - n)).reshape(padded // LANES, LANES)


def _vec_rows(*rows):
    stacked = jnp.concatenate([r.reshape(1, D_MODEL) for r in rows], axis=0)
    return jnp.pad(stacked, ((0, SUBLANES - len(rows)), (0, 0)))


def kernel(x, c, w_ada, b_ada, g_mix, w_in, b_in, sinks, conv_w, w_out, g_ffn, w_ffn_in, w_ffn_out, g_final, loss_target, m_w_ada, m_b_ada, m_g_mix, m_w_in, m_b_in, m_sinks, m_conv_w, m_w_out, m_g_ffn, m_w_ffn_in, m_w_ffn_out, m_g_final, v_w_ada, v_b_ada, v_g_mix, v_w_in, v_b_in, v_sinks, v_conv_w, v_w_out, v_g_ffn, v_w_ffn_in, v_w_ffn_out, v_g_final):
    ix, iy, ic = _my_place()
    me = 4 * ix + 2 * iy + ic
    xs = x[0]
    target = loss_target[0]
    ada_cols = w_ada.shape[2]
    conv_cols = conv_w.shape[2]

    wt_in, wt_fi = jnp.transpose(w_in[0]), jnp.transpose(w_ffn_in[0])
    b_cols = lax.dynamic_slice_in_dim(b_ada, me * ada_cols, ada_cols, axis=1)
    g_in, (cast_fi, cast_out, cast_fo), first, mod_all = _gather_first_weight(
        wt_in, [wt_fi, w_out[0], w_ffn_out[0]], _to_rows(jnp.concatenate([c[0], conv_w[0].reshape(-1)])),
        w_ada[0], b_cols)
    first = first.reshape(N_DEV, -1)
    c_all = first[:, :D_MODEL]
    conv_full = jnp.transpose(first[:, D_MODEL:D_MODEL + 3 * conv_cols].reshape(N_DEV, 3, conv_cols), (1, 0, 2))
    conv_full = conv_full.reshape(3, D_MODEL)
    mod = lax.dynamic_index_in_dim(mod_all, me, axis=1, keepdims=False).reshape(N_MOD, D_MODEL)
    sh1, sc1, ga1, sh2, sc2, ga2 = [mod[i:i + 1] for i in range(N_MOD)]
    w_in_t = g_in.reshape(IN_WIDTH, D_MODEL)
    (z, h1), (g_out, g_fi, g_fo) = _inproj_fwd(xs, _vec_rows(g_mix, sc1, sh1), w_in_t, b_in,
                                               _gather_rider([cast_out, cast_fi, cast_fo]))
    w_fi_t = g_fi.reshape(2 * D_FF, D_MODEL)
    w_out_full = g_out.reshape(D_MODEL, D_MODEL)
    (attn, lse), _ = _attn_fwd(z, sinks[0], None)
    w_fo_full = g_fo.reshape(D_FF, D_MODEL)
    merged, x2, h2, oproj = _mix_fwd(
        xs, attn, z, _vec_rows(ga1, g_ffn, sc2, sh2, conv_full[0], conv_full[1], conv_full[2]), w_out_full)
    gu, act = _ffn_fwd(h2, w_fi_t)
    dx3, df, dgu, acc_l = _ffn_out_loss(act, gu, x2, target, _vec_rows(ga2, g_final), w_fo_full)

    gw_fo, _ = _weight_grad(act, df, "wgrad_ffn_out", D_FF)
    gw_fi, _ = _weight_grad(dgu, h2, "wgrad_ffn_in", D_FF)
    blocks_fo = gw_fo.reshape(N_DEV, D_FF // N_DEV, D_MODEL)
    blocks_fi = gw_fi.reshape(N_DEV, 2 * D_FF // N_DEV, D_MODEL)
    (dx2, acc_f), (sib_fo, sib_fi) = _ffn_in_bwd(dgu, x2, dx3, _vec_rows(g_ffn, sc2), w_fi_t,
                                                 _sibling_rider([blocks_fo, blocks_fi]))
    sums_fo, mine_fo = _sibling_sum(blocks_fo, sib_fo, "sibling_sum_ffn_out")
    sums_fi, mine_fi = _sibling_sum(blocks_fi, sib_fi, "sibling_sum_ffn_in")
    (dout, dattn, drest, acc_m), (ici_fo, ici_fi) = _mix_bwd(
        dx2, oproj, attn, z, _vec_rows(ga1, conv_full[0], conv_full[1], conv_full[2]), w_out_full,
        _chip_rider([sums_fo, sums_fi]))
    gw_out, _ = _weight_grad(merged, dout, "wgrad_out", D_MODEL)
    blocks_out = gw_out.reshape(N_DEV, D_MODEL // N_DEV, D_MODEL)
    (dq, dkv, dsink), _ = _attn_bwd(z, dattn, attn, lse, sinks[0], None)
    gw_in, _ = _weight_grad(drest, h1, "wgrad_in_rest", IN_CHUNK, rows=IN_WIDTH, row0=REF_REST_COL)
    gw_in, _ = _weight_grad(dq, h1, "wgrad_in_q", D_MODEL, rows=IN_WIDTH, row0=0, into=gw_in)
    gw_in, _ = _weight_grad(dkv, h1, "wgrad_in_kv", 2 * KV_WIDTH, rows=IN_WIDTH, row0=REF_KV_COL, into=gw_in)
    blocks_in = gw_in.reshape(N_DEV, IN_WIDTH // N_DEV, D_MODEL)
    (sums_in, mine_in), (sums_out, mine_out) = _sibling_exchange_sum([blocks_in, blocks_out], "sibling_w_in_out")
    (grad_x, acc_i, db_in), (ici_in, ici_out) = _inproj_bwd(dq, drest, dkv, xs, dx2, _vec_rows(g_mix, sc1), w_in_t,
                                                            _chip_rider([sums_in, sums_out]))

    widen = lambda vec: jnp.pad(vec, (0, -vec.shape[0] % D_MODEL))
    packed = jnp.concatenate([
        acc_i[0], acc_i[1], acc_m[0], acc_f[0], acc_f[1], acc_l[2],
        acc_i[2], widen(db_in[0]), acc_f[2], acc_l[1],
        acc_m[1], acc_m[2], acc_m[3], widen(dsink[0]), acc_l[0],
        jnp.zeros(((PACK_ROWS - PACK_SQERR - 1) * D_MODEL,), F32)]).reshape(PACK_ROWS, D_MODEL)
    packed_all = _small_allgather(packed, "gather_small")
    dmod_all = packed_all[:, PACK_DMOD:PACK_DMOD + N_MOD, :].reshape(N_DEV, N_MOD * D_MODEL)
    dmod_cols = lax.dynamic_slice_in_dim(dmod_all, me * ada_cols, ada_cols, axis=1)
    g_w_ada = _ada_weight_grad(c_all, dmod_cols)
    row_of = lambda a: a.reshape(1, -1)
    small, g_conv_full, loss = _small_finalize(packed_all, {
        "b_ada": (b_ada, m_b_ada, v_b_ada), "g_mix": (g_mix, m_g_mix, v_g_mix), "b_in": (b_in, m_b_in, v_b_in),
        "g_ffn": (g_ffn, m_g_ffn, v_g_ffn), "sinks": (sinks, m_sinks, v_sinks),
        "g_final": (row_of(g_final), row_of(m_g_final), row_of(v_g_final))})
    small["g_final"] = tuple(o.reshape(g_final.shape) for o in small["g_final"])
    g_conv = lax.dynamic_slice_in_dim(g_conv_full, me * conv_cols, conv_cols, axis=1)
    d_conv, nm_conv, nv_conv = _adamw(conv_w[0], g_conv, m_conv_w[0], v_conv_w[0], "adamw_conv_w")
    small["conv_w"] = (g_conv[None], d_conv[None], nm_conv[None], nv_conv[None])

    def reduced(mine, ici, w, m, v, name, transposed=False):
        turn = jnp.transpose if transposed else (lambda a: a)
        return tuple(turn(o)[None] for o in _chip_sum_adamw(mine, ici, turn(w[0]), turn(m[0]), turn(v[0]), name))

    d_ada, nm_ada, nv_ada = _adamw(w_ada[0], g_w_ada, m_w_ada[0], v_w_ada[0], "adamw_w_ada")
    res = {
        "w_ada": (g_w_ada[None], d_ada[None], nm_ada[None], nv_ada[None]),
        "w_in": reduced(mine_in, ici_in, w_in, m_w_in, v_w_in, "adamw_w_in", transposed=True),
        "w_out": reduced(mine_out, ici_out, w_out, m_w_out, v_w_out, "adamw_w_out"),
        "w_ffn_in": reduced(mine_fi, ici_fi, w_ffn_in, m_w_ffn_in, v_w_ffn_in, "adamw_w_ffn_in", transposed=True),
        "w_ffn_out": reduced(mine_fo, ici_fo, w_ffn_out, m_w_ffn_out, v_w_ffn_out, "adamw_w_ffn_out"),
    }
    res.update(small)
    order = ["w_ada", "b_ada", "g_mix", "w_in", "b_in", "sinks", "conv_w", "w_out", "g_ffn", "w_ffn_in", "w_ffn_out",
             "g_final"]
    outs = [loss.reshape(()), grad_x[None]]
    for k in range(4):
        outs += [res[n][k] for n in order]
    return tuple(outs)
```

```python
import functools
import math

import jax
import jax.numpy as jnp
from jax import lax
from jax.experimental import pallas as pl
from jax.experimental.pallas import tpu as pltpu

F32 = jnp.float32
BF16 = jnp.bfloat16
GRAD_STREAM = F32

D_MODEL = 1024
HEAD_DIM = 64
N_Q_HEADS = 16
N_KV_HEADS = 2
GROUP = 8
WINDOW = 128
KV_WIDTH = N_KV_HEADS * HEAD_DIM
D_FF = 2816
IN_WIDTH = 6400
N_MOD = 6
EPS = 1e-6
N_DEV = 8
REST_WIDTH = 5 * D_MODEL
KV_COL = D_MODEL + REST_WIDTH
ATTN_SCALE = HEAD_DIM ** -0.5

ADAM_LR = 0.001
ADAM_B1 = 0.9
ADAM_B2 = 0.999
ADAM_EPS = 1e-08
ADAM_WD = 0.01
ADAM_STEP = 10

LANES = 128
SUBLANES = 8
BF16_ROWS = 16
VMEM_LIMIT = 56 * 1024 * 1024
TOKEN_TILE = 512
FF_CHUNK = 256
ROW_PARTS = 2
MIN_STREAM_STEPS = 2
WGRAD_VMEM = 40 * 1024 * 1024
MESH = pl.DeviceIdType.MESH
ANY = pl.BlockSpec(memory_space=pl.ANY)

NT_DIMS = (((1,), (1,)), ((), ()))
TN_DIMS = (((0,), (0,)), ((), ()))
CHIP_FLIPS = [(0, 0), (1, 0), (0, 1), (1, 1)]


def _full(shape):
    return pl.BlockSpec(shape, lambda *_: (0,) * len(shape))


def _my_place():
    return lax.axis_index("x"), lax.axis_index("y"), lax.axis_index("c")


def _flip(v, bit):
    return 1 - v if bit else v


def _sigmoid(v):
    return 1.0 / (1.0 + jnp.exp2(v * (-1.4426950408889634)))


class _Rider:
    def __init__(self, ins, out_shapes, sem_shapes, first=None, mid=None, last=None, ins_in_vmem=False):
        self.ins, self.out_shapes, self.sem_shapes = list(ins), list(out_shapes), list(sem_shapes)
        self.in_specs = [_full(a.shape) if ins_in_vmem else ANY for a in self.ins]
        self.hooks = [(when, fn) for when, fn in (("first", first), ("mid", mid), ("last", last)) if fn is not None]


def _call(body, name, grid, args, in_specs, out_shape, out_specs, scratch=(), rider=None, aliases=None):
    n_in, n_out, n_scr = len(args), len(out_shape), len(scratch)
    r_in = rider.ins if rider else []
    r_out = rider.out_shapes if rider else []
    r_sem = rider.sem_shapes if rider else []
    nsteps = math.prod(grid)

    def full_body(*refs):
        pos = 0
        groups = []
        for size in (n_in, len(r_in), n_out, len(r_out), n_scr, len(r_sem)):
            groups.append(refs[pos:pos + size])
            pos += size
        ins, rins, outs, routs, scr, rsems = groups
        step = pl.program_id(0)
        for axis in range(1, len(grid)):
            step = step * grid[axis] + pl.program_id(axis)
        at = {"first": 0, "mid": (3 * nsteps) // 4, "last": nsteps - 1}
        hooks = rider.hooks if rider else []
        for when, fn in hooks:
            if when != "last":
                pl.when(step == at[when])(functools.partial(fn, rins, routs, rsems))
        body(*ins, *outs, *scr)
        for when, fn in hooks:
            if when == "last":
                pl.when(step == at[when])(functools.partial(fn, rins, routs, rsems))

    outs = pl.pallas_call(
        full_body, name=name, grid=grid,
        out_shape=list(out_shape) + list(r_out),
        in_specs=list(in_specs) + (rider.in_specs if rider else []),
        out_specs=list(out_specs) + [ANY] * len(r_out),
        scratch_shapes=list(scratch) + list(r_sem),
        input_output_aliases=dict(aliases or {}),
        compiler_params=pltpu.CompilerParams(dimension_semantics=("arbitrary",) * len(grid),
                                             vmem_limit_bytes=VMEM_LIMIT),
    )(*args, *r_in)
    return list(outs[:n_out]), list(outs[n_out:])


def _gather_rider(shards):
    n = len(shards)

    def setup(outs, sems):
        x, y, c = _my_place()
        send_sems, recv_sems, _ = sems
        chips = [(1 - x, y), (x, 1 - y), (1 - x, 1 - y)]

        def block(w, place):
            return outs[w].at[4 * place[0] + 2 * place[1] + place[2]]

        def copy(w, k, place, to, src=None):
            return pltpu.make_async_remote_copy(
                src_ref=block(w, place) if src is None else src, dst_ref=block(w, place),
                send_sem=send_sems.at[w, k], recv_sem=recv_sems.at[w, k], device_id=to, device_id_type=MESH)

        return (x, y, c), (x, y, 1 - c), chips, block, copy

    def first(ins, outs, sems):
        me, sibling, chips, block, copy = setup(outs, sems)
        for w in range(n):
            pltpu.make_async_copy(ins[w], block(w, me), sems[2].at[w]).start()
            copy(w, 0, me, sibling, src=ins[w]).start()
            for j, chip in enumerate(chips):
                copy(w, 1 + j, me, (*chip, me[2]), src=ins[w]).start()

    def mid(ins, outs, sems):
        me, sibling, chips, block, copy = setup(outs, sems)
        for w in range(n):
            for j, chip in enumerate(chips):
                copy(w, 1 + j, (*chip, me[2]), me).wait_recv()
                copy(w, 4 + j, (*chip, me[2]), sibling).start()

    def last(ins, outs, sems):
        me, sibling, chips, block, copy = setup(outs, sems)
        for w in range(n):
            copy(w, 0, sibling, me).wait_recv()
            for j, chip in enumerate(chips):
                copy(w, 4 + j, (*chip, 1 - me[2]), me).wait_recv()
            copy(w, 0, me, sibling, src=ins[w]).wait_send()
            for j, chip in enumerate(chips):
                copy(w, 1 + j, me, (*chip, me[2]), src=ins[w]).wait_send()
                copy(w, 4 + j, (*chip, me[2]), sibling).wait_send()
            pltpu.make_async_copy(ins[w], block(w, me), sems[2].at[w]).wait()

    return _Rider(
        shards, [jax.ShapeDtypeStruct((N_DEV,) + s.shape, BF16) for s in shards],
        [pltpu.SemaphoreType.DMA((n, N_DEV - 1)), pltpu.SemaphoreType.DMA((n, N_DEV - 1)),
         pltpu.SemaphoreType.DMA((n,))],
        first=first, mid=mid, last=last, ins_in_vmem=True)


def _sibling_rider(gblocks):
    n = len(gblocks)

    def copies(ins, outs, sems):
        x, y, c = _my_place()
        send_sems, recv_sems = sems
        made = []
        for w in range(n):
            for f, (fx, fy) in enumerate(CHIP_FLIPS):
                chip = 4 * _flip(x, fx) + 2 * _flip(y, fy)
                made.append(pltpu.make_async_remote_copy(
                    src_ref=ins[w].at[chip + 1 - c], dst_ref=outs[w].at[f], send_sem=send_sems.at[w, f],
                    recv_sem=recv_sems.at[w, f], device_id=(x, y, 1 - c), device_id_type=MESH))
        return made

    def first(ins, outs, sems):
        for cp in copies(ins, outs, sems):
            cp.start()

    def last(ins, outs, sems):
        for cp in copies(ins, outs, sems):
            cp.wait_recv()
            cp.wait_send()

    return _Rider(gblocks, [jax.ShapeDtypeStruct((4,) + g.shape[1:], BF16) for g in gblocks],
                  [pltpu.SemaphoreType.DMA((n, 4))] * 2, first=first, last=last)


def _chip_rider(sums):
    n = len(sums)

    def copies(ins, outs, sems):
        x, y, c = _my_place()
        send_sems, recv_sems = sems
        made = []
        for w in range(n):
            for f in (1, 2, 3):
                fx, fy = CHIP_FLIPS[f]
                made.append(pltpu.make_async_remote_copy(
                    src_ref=ins[w].at[f - 1], dst_ref=outs[w].at[f - 1], send_sem=send_sems.at[w, f - 1],
                    recv_sem=recv_sems.at[w, f - 1], device_id=(_flip(x, fx), _flip(y, fy), c), device_id_type=MESH))
        return made

    def first(ins, outs, sems):
        for cp in copies(ins, outs, sems):
            cp.start()

    def last(ins, outs, sems):
        for cp in copies(ins, outs, sems):
            cp.wait_recv()
            cp.wait_send()

    return _Rider(sums, [jax.ShapeDtypeStruct(s.shape, BF16) for s in sums],
                  [pltpu.SemaphoreType.DMA((n, 3))] * 2, first=first, last=last)


def _push_to_all(v_ref, out_ref, send_sems, recv_sems, local_sem, wait=True):
    x, y, c = _my_place()
    me = 4 * x + 2 * y + c
    mine = pltpu.make_async_copy(v_ref, out_ref.at[me], local_sem)
    mine.start()
    sends = []
    for k in range(1, N_DEV):
        px, py, pc = _flip(x, k & 4), _flip(y, k & 2), _flip(c, k & 1)
        cp = pltpu.make_async_remote_copy(
            src_ref=v_ref, dst_ref=out_ref.at[me], send_sem=send_sems.at[k - 1], recv_sem=recv_sems.at[k - 1],
            device_id=(px, py, pc), device_id_type=MESH)
        cp.start()
        sends.append(cp)

    def finish():
        for k in range(1, N_DEV):
            px, py, pc = _flip(x, k & 4), _flip(y, k & 2), _flip(c, k & 1)
            pltpu.make_async_remote_copy(
                src_ref=v_ref, dst_ref=out_ref.at[4 * px + 2 * py + pc], send_sem=send_sems.at[k - 1],
                recv_sem=recv_sems.at[k - 1], device_id=(px, py, pc), device_id_type=MESH).wait_recv()
        for cp in sends:
            cp.wait_send()
        mine.wait()

    if wait:
        finish()
    return finish


def _small_allgather(v, name):
    def body(v_ref, out_ref, send_sems, recv_sems, local_sem):
        _push_to_all(v_ref, out_ref, send_sems, recv_sems, local_sem)

    return pl.pallas_call(
        body, name=name,
        out_shape=jax.ShapeDtypeStruct((N_DEV,) + v.shape, F32),
        in_specs=[pl.BlockSpec(memory_space=pltpu.VMEM)],
        out_specs=pl.BlockSpec(memory_space=pltpu.VMEM),
        scratch_shapes=[pltpu.SemaphoreType.DMA((N_DEV - 1,)), pltpu.SemaphoreType.DMA((N_DEV - 1,)),
                        pltpu.SemaphoreType.DMA],
        compiler_params=pltpu.CompilerParams(vmem_limit_bytes=VMEM_LIMIT),
    )(v)


def _gather_first_weight(shard, others, cond_rows, w_ada, b_cols):
    n = len(others)
    ada_cols = w_ada.shape[1]
    c_rows = D_MODEL // LANES

    def body(*refs):
        w_ref, other_refs = refs[0], refs[1:1 + n]
        cond_ref, wada_ref, bcols_ref = refs[1 + n:4 + n]
        out_ref, cast_refs = refs[4 + n], refs[5 + n:5 + 2 * n]
        cond_all_ref, mod_all_ref = refs[5 + 2 * n:7 + 2 * n]
        mine_ref, mod_ref, send_sems, recv_sems, local_sem, small_send, small_recv, small_local = refs[7 + 2 * n:]
        x, y, c = _my_place()
        me, sibling = (x, y, c), (x, y, 1 - c)
        xnb, ynb, diag = (1 - x, y), (x, 1 - y), (1 - x, 1 - y)
        half = shard.shape[0] // 2

        def block(place, part=None):
            ref = out_ref.at[4 * place[0] + 2 * place[1] + place[2]]
            return ref if part is None else ref.at[pl.ds(part * half, half)]

        def copy(k, place, to, part=None, src=None):
            return pltpu.make_async_remote_copy(
                src_ref=block(place, part) if src is None else src, dst_ref=block(place, part),
                send_sem=send_sems.at[k], recv_sem=recv_sems.at[k], device_id=to, device_id_type=MESH)

        finish_cond = _push_to_all(cond_ref, cond_all_ref, small_send.at[0], small_recv.at[0], small_local.at[0],
                                   wait=False)
        mine_ref[...] = w_ref[...].astype(BF16)
        local = pltpu.make_async_copy(mine_ref, block(me), local_sem)
        local.start()
        started = [copy(0, me, sibling, src=mine_ref), copy(1, me, (*xnb, c), src=mine_ref),
                   copy(2, me, (*ynb, c), src=mine_ref)]
        for cp in started:
            cp.start()
        finish_cond()
        mod = jnp.zeros((N_DEV, ada_cols), F32) + bcols_ref[...]
        for r in range(c_rows):
            cf = cond_all_ref[:, r, :]
            act = (cf * _sigmoid(cf)).astype(BF16)
            mod = mod + jnp.dot(act, wada_ref[r * LANES:(r + 1) * LANES, :].astype(BF16),
                                preferred_element_type=F32)
        mod_ref[...] = mod
        finish_mod = _push_to_all(mod_ref, mod_all_ref, small_send.at[1], small_recv.at[1], small_local.at[1],
                                  wait=False)
        for o_ref, c_ref in zip(other_refs, cast_refs):
            c_ref[...] = o_ref[...].astype(BF16)
        def start(cp):
            cp.start()
            started.append(cp)

        copy(1, (*xnb, c), me).wait_recv()
        start(copy(3, (*xnb, c), (*ynb, c), part=0))
        start(copy(5, (*xnb, c), sibling))
        copy(2, (*ynb, c), me).wait_recv()
        start(copy(4, (*ynb, c), (*xnb, c), part=1))
        start(copy(6, (*ynb, c), sibling))
        copy(3, (*diag, c), me, part=0).wait_recv()
        start(copy(7, (*diag, c), sibling, part=0))
        copy(4, (*diag, c), me, part=1).wait_recv()
        start(copy(8, (*diag, c), sibling, part=1))
        copy(0, sibling, me).wait_recv()
        copy(5, (*xnb, 1 - c), me).wait_recv()
        copy(6, (*ynb, 1 - c), me).wait_recv()
        copy(7, (*diag, 1 - c), me, part=0).wait_recv()
        copy(8, (*diag, 1 - c), me, part=1).wait_recv()
        finish_mod()
        for cp in started:
            cp.wait_send()
        local.wait()

    vmem = pl.BlockSpec(memory_space=pltpu.VMEM)
    outs = pl.pallas_call(
        body, name="gather_w_in",
        out_shape=[jax.ShapeDtypeStruct((N_DEV,) + shard.shape, BF16)]
        + [jax.ShapeDtypeStruct(o.shape, BF16) for o in others]
        + [jax.ShapeDtypeStruct((N_DEV,) + cond_rows.shape, F32), jax.ShapeDtypeStruct((N_DEV, N_DEV, ada_cols), F32)],
        in_specs=[vmem] * (4 + n),
        out_specs=[ANY] + [vmem] * (n + 2),
        scratch_shapes=[pltpu.VMEM(shard.shape, BF16), pltpu.VMEM((N_DEV, ada_cols), F32),
                        pltpu.SemaphoreType.DMA((9,)), pltpu.SemaphoreType.DMA((9,)),
                        pltpu.SemaphoreType.DMA,
                        pltpu.SemaphoreType.DMA((2, N_DEV - 1)), pltpu.SemaphoreType.DMA((2, N_DEV - 1)),
                        pltpu.SemaphoreType.DMA((2,))],
        compiler_params=pltpu.CompilerParams(vmem_limit_bytes=VMEM_LIMIT),
    )(shard, *others, cond_rows, w_ada, b_cols)
    return outs[0], list(outs[1:1 + n]), outs[1 + n], outs[2 + n]


def _sibling_exchange_sum(gblocks, name):
    n = len(gblocks)

    def body(*refs):
        g_refs, out_refs = refs[:n], refs[n:3 * n]
        bufs = refs[3 * n:5 * n]
        own_sems, send_sems, recv_sems = refs[5 * n:]
        x, y, c = _my_place()
        pairs = []
        for w in range(n):
            own_buf, sib_buf = bufs[2 * w], bufs[2 * w + 1]
            for f, (fx, fy) in enumerate(CHIP_FLIPS):
                chip = 4 * _flip(x, fx) + 2 * _flip(y, fy)
                own = pltpu.make_async_copy(g_refs[w].at[chip + c], own_buf.at[f], own_sems.at[w, f])
                own.start()
                remote = pltpu.make_async_remote_copy(
                    src_ref=g_refs[w].at[chip + 1 - c], dst_ref=sib_buf.at[f], send_sem=send_sems.at[w, f],
                    recv_sem=recv_sems.at[w, f], device_id=(x, y, 1 - c), device_id_type=MESH)
                remote.start()
                pairs.append((own, remote))
        for w in range(n):
            own_buf, sib_buf = bufs[2 * w], bufs[2 * w + 1]
            sums_ref, mine_ref = out_refs[2 * w], out_refs[2 * w + 1]
            for f in (1, 2, 3, 0):
                own, remote = pairs[4 * w + f]
                own.wait()
                remote.wait_recv()
                total = own_buf[f].astype(F32) + sib_buf[f].astype(F32)
                if f == 0:
                    mine_ref[...] = total
                else:
                    sums_ref[f - 1] = total.astype(BF16)
        for _, remote in pairs:
            remote.wait_send()

    vmem = pl.BlockSpec(memory_space=pltpu.VMEM)
    out_shape, scratch = [], []
    for g in gblocks:
        out_shape += [jax.ShapeDtypeStruct((3,) + g.shape[1:], BF16), jax.ShapeDtypeStruct(g.shape[1:], F32)]
        scratch += [pltpu.VMEM((4,) + g.shape[1:], BF16)] * 2
    outs = pl.pallas_call(
        body, name=name, out_shape=out_shape,
        in_specs=[ANY] * n, out_specs=[vmem] * (2 * n),
        scratch_shapes=scratch + [pltpu.SemaphoreType.DMA((n, 4))] * 3,
        compiler_params=pltpu.CompilerParams(vmem_limit_bytes=VMEM_LIMIT),
    )(*gblocks)
    return [(outs[2 * w], outs[2 * w + 1]) for w in range(n)]


def _ada_weight_grad(c_all, dmod_cols):
    cols = dmod_cols.shape[1]

    def body(c_ref, d_ref, out_ref):
        cf = c_ref[...]
        act = (cf * _sigmoid(cf)).astype(BF16)
        out_ref[...] = lax.dot_general(act, d_ref[...].astype(BF16), TN_DIMS, preferred_element_type=F32)

    return pl.pallas_call(
        body, name="ada_weight_grad",
        out_shape=jax.ShapeDtypeStruct((D_MODEL, cols), F32),
        in_specs=[pl.BlockSpec(memory_space=pltpu.VMEM)] * 2,
        out_specs=pl.BlockSpec(memory_space=pltpu.VMEM),
        compiler_params=pltpu.CompilerParams(vmem_limit_bytes=VMEM_LIMIT),
    )(c_all, dmod_cols)


PACK_ROWS = 24
PACK_DMOD = 0
PACK_PARAMS = {"g_mix": (6, D_MODEL), "b_in": (7, IN_WIDTH), "g_ffn": (14, D_MODEL), "g_final": (15, D_MODEL),
               "sinks": (19, N_Q_HEADS)}
PACK_CONV = 16
PACK_SQERR = 20


def _small_finalize(packed_all, params):
    names = ["b_ada"] + list(PACK_PARAMS)
    layout = dict(PACK_PARAMS, b_ada=(PACK_DMOD, N_MOD * D_MODEL))
    n = len(names)

    def body(*refs):
        p_ref = refs[0]
        ins = refs[1:1 + 3 * n]
        outs = refs[1 + 3 * n:1 + 7 * n]
        conv_ref, loss_ref = refs[1 + 7 * n:]
        total = p_ref[0]
        for d in range(1, N_DEV):
            total = total + p_ref[d]
        for k, name in enumerate(names):
            row0, width = layout[name]
            w_ref, m_ref, v_ref = ins[3 * k:3 * k + 3]
            g_ref, d_ref, nm_ref, nv_ref = outs[4 * k:4 * k + 4]
            for chunk in range(-(-width // D_MODEL)):
                lo = chunk * D_MODEL
                hi = min(lo + D_MODEL, width)
                g = total[row0 + chunk:row0 + chunk + 1, :hi - lo]
                g_ref[:, lo:hi] = g
                d_ref[:, lo:hi], nm_ref[:, lo:hi], nv_ref[:, lo:hi] = _adamw_update(
                    w_ref[:, lo:hi], g, m_ref[:, lo:hi], v_ref[:, lo:hi])
        conv_ref[...] = total[PACK_CONV:PACK_CONV + 3, :]
        loss_ref[...] = (0.5 / D_MODEL) * jnp.sum(total[PACK_SQERR:PACK_SQERR + 1, :], keepdims=True)

    vmem = pl.BlockSpec(memory_space=pltpu.VMEM)
    flat = [a for name in names for a in params[name]]
    out_shape = [jax.ShapeDtypeStruct(params[name][0].shape, F32) for name in names for _ in range(4)]
    outs = pl.pallas_call(
        body, name="small_finalize",
        out_shape=out_shape + [jax.ShapeDtypeStruct((3, D_MODEL), F32), jax.ShapeDtypeStruct((1, 1), F32)],
        in_specs=[vmem] * (1 + 3 * n),
        out_specs=[vmem] * (4 * n + 2),
        compiler_params=pltpu.CompilerParams(vmem_limit_bytes=VMEM_LIMIT),
    )(packed_all, *flat)
    return {name: tuple(outs[4 * k:4 * k + 4]) for k, name in enumerate(names)}, outs[4 * n], outs[4 * n + 1]


def _row_tile(rows, multiple):
    for cand in range(rows // MIN_STREAM_STEPS, 0, -1):
        if rows % cand == 0 and cand % multiple == 0:
            return cand
    return rows


def _adamw_update(w, g, m, v):
    c1 = 1.0 / (1.0 - ADAM_B1 ** ADAM_STEP)
    c2 = 1.0 / (1.0 - ADAM_B2 ** ADAM_STEP)
    nm = ADAM_B1 * m + (1.0 - ADAM_B1) * g
    nv = ADAM_B2 * v + (1.0 - ADAM_B2) * (g * g)
    delta = -ADAM_LR * ((nm * c1) / (jnp.sqrt(nv * c2) + ADAM_EPS) + ADAM_WD * w)
    return delta, nm, nv


def _adamw(w, g, m, v, name):
    rows, cols = w.shape
    tile = _row_tile(rows, SUBLANES)

    def body(w_ref, g_ref, m_ref, v_ref, d_ref, nm_ref, nv_ref):
        d_ref[...], nm_ref[...], nv_ref[...] = _adamw_update(w_ref[...], g_ref[...], m_ref[...], v_ref[...])

    spec = pl.BlockSpec((tile, cols), lambda i: (i, 0))
    outs, _ = _call(body, name, (rows // tile,), [w, g, m, v], [spec] * 4,
                    [jax.ShapeDtypeStruct((rows, cols), F32)] * 3, [spec] * 3)
    return outs


def _sibling_sum(gblocks, sib, name):
    _, r, cdim = gblocks.shape
    tile = _row_tile(r, BF16_ROWS)
    x, y, c = _my_place()
    table = jnp.stack([4 * _flip(x, fx) + 2 * _flip(y, fy) + c for fx, fy in CHIP_FLIPS]).astype(jnp.int32)

    def body(table_ref, own0, own1, own2, own3, sib_ref, sums_ref, mine_ref):
        mine_ref[...] = own0[...].astype(F32) + sib_ref[0].astype(F32)
        for f, own in ((1, own1), (2, own2), (3, own3)):
            sums_ref[f - 1] = (own[...].astype(F32) + sib_ref[f].astype(F32)).astype(BF16)

    own_specs = [pl.BlockSpec((None, tile, cdim), functools.partial(lambda i, tab, f: (tab[f], i, 0), f=f))
                 for f in range(4)]
    return pl.pallas_call(
        body, name=name,
        grid_spec=pltpu.PrefetchScalarGridSpec(
            num_scalar_prefetch=1, grid=(r // tile,),
            in_specs=own_specs + [pl.BlockSpec((4, tile, cdim), lambda i, tab: (0, i, 0))],
            out_specs=[pl.BlockSpec((3, tile, cdim), lambda i, tab: (0, i, 0)),
                       pl.BlockSpec((tile, cdim), lambda i, tab: (i, 0))]),
        out_shape=[jax.ShapeDtypeStruct((3, r, cdim), BF16), jax.ShapeDtypeStruct((r, cdim), F32)],
        compiler_params=pltpu.CompilerParams(dimension_semantics=("arbitrary",), vmem_limit_bytes=VMEM_LIMIT),
    )(table, gblocks, gblocks, gblocks, gblocks, sib)


def _chip_sum_adamw(mine, ici, w, m, v, name):
    r, cdim = mine.shape
    tile = _row_tile(r, BF16_ROWS)

    def body(mine_ref, ici_ref, w_ref, m_ref, v_ref, g_ref, d_ref, nm_ref, nv_ref):
        g = mine_ref[...]
        for f in range(3):
            g = g + ici_ref[f].astype(F32)
        g_ref[...] = g
        d_ref[...], nm_ref[...], nv_ref[...] = _adamw_update(w_ref[...], g, m_ref[...], v_ref[...])

    spec = pl.BlockSpec((tile, cdim), lambda i: (i, 0))
    outs, _ = _call(
        body, name, (r // tile,), [mine, ici, w, m, v],
        [spec, pl.BlockSpec((3, tile, cdim), lambda i: (0, i, 0)), spec, spec, spec],
        [jax.ShapeDtypeStruct((r, cdim), F32)] * 4, [spec] * 4)
    return outs


REF_KV_COL = D_MODEL
REF_REST_COL = D_MODEL + 2 * KV_WIDTH
IN_CHUNK = 1280
IN_PIECES = ([(0, 0, D_MODEL)]
             + [(D_MODEL + n * IN_CHUNK, REF_REST_COL + n * IN_CHUNK, IN_CHUNK) for n in range(REST_WIDTH // IN_CHUNK)]
             + [(KV_COL, REF_KV_COL, 2 * KV_WIDTH)])


def _inproj_fwd(x, vec, w_t, b_in, rider):
    t = x.shape[0]
    tm = min(TOKEN_TILE, t)

    def body(x_ref, vec_ref, w_ref, b_ref, z_ref, h_ref):
        xf = x_ref[...]
        r = lax.rsqrt(jnp.mean(xf * xf, axis=-1, keepdims=True) + EPS)
        h = (xf * r) * (vec_ref[0:1, :] * (1.0 + vec_ref[1:2, :])) + vec_ref[2:3, :]
        hb = h.astype(BF16)
        h_ref[...] = hb
        for mine, ref, width in IN_PIECES:
            zc = lax.dot_general(hb, w_ref[ref:ref + width, :], NT_DIMS, preferred_element_type=F32)
            z_ref[:, mine:mine + width] = (zc + b_ref[:, ref:ref + width]).astype(BF16)

    return _call(
        body, "inproj_fwd", (t // tm,), [x, vec, w_t, b_in],
        [pl.BlockSpec((tm, D_MODEL), lambda i: (i, 0)), _full((SUBLANES, D_MODEL)),
         _full((IN_WIDTH, D_MODEL)), _full((1, IN_WIDTH))],
        [jax.ShapeDtypeStruct((t, IN_WIDTH), BF16), jax.ShapeDtypeStruct((t, D_MODEL), BF16)],
        [pl.BlockSpec((tm, IN_WIDTH), lambda i: (i, 0)), pl.BlockSpec((tm, D_MODEL), lambda i: (i, 0))],
        rider=rider)


PAIRS = GROUP // 2
STACK = PAIRS * WINDOW


ATTN_BLOCKS = 4
ATTN_BWD_BLOCKS = 1
LOG2E = 1.4426950408889634
LN2 = 0.6931471805599453
SCORE_SCALE = ATTN_SCALE * LOG2E


def _fill_window_bias(bias_ref):
    shape = bias_ref.shape[1:]
    kj = lax.broadcasted_iota(jnp.int32, shape, 0)
    qi = jnp.bitwise_and(lax.broadcasted_iota(jnp.int32, shape, 1), WINDOW - 1)
    in_prev = jnp.logical_and(kj < WINDOW, kj > qi)
    in_cur = jnp.logical_and(kj >= WINDOW, (kj - WINDOW) <= qi)
    bias_ref[0] = jnp.where(in_cur, 0.0, -jnp.inf)
    bias_ref[1] = jnp.where(jnp.logical_or(in_prev, in_cur), 0.0, -jnp.inf)


def _half_tiles(tile):
    low = lax.broadcasted_iota(jnp.int32, tile.shape, 1) < HEAD_DIM
    swapped = jnp.concatenate([tile[:, HEAD_DIM:], tile[:, :HEAD_DIM]], axis=1)
    zero = jnp.zeros_like(tile)
    return ((jnp.where(low, tile, zero), jnp.where(low, zero, swapped)),
            (jnp.where(low, swapped, zero), jnp.where(low, zero, tile)))


def _stack_pairs(ref, row0, j):
    return jnp.concatenate(
        [ref[pl.ds(row0, WINDOW), (j * PAIRS + p) * LANES:(j * PAIRS + p + 1) * LANES] for p in range(PAIRS)], axis=0)


def _per_pair_row(values):
    pair = lax.broadcasted_iota(jnp.int32, (1, STACK), 1) // WINDOW
    row = jnp.full((1, STACK), values[PAIRS - 1], F32)
    for p in range(PAIRS - 2, -1, -1):
        row = jnp.where(pair == p, values[p], row)
    return row


def _attn_fwd(z, sinks, rider):
    t = z.shape[0]
    tq = min(TOKEN_TILE, t)
    nblk = tq // WINDOW

    def body(q_ref, kv_ref, sink_ref, o_ref, lse_ref, bias_ref):
        i = pl.program_id(0)

        @pl.when(i == 0)
        def _():
            _fill_window_bias(bias_ref)

        def window(b):
            row0 = pl.multiple_of(b * WINDOW, WINDOW)
            start = i * tq + b * WINDOW
            prev = pl.multiple_of(jnp.maximum(start - WINDOW, 0), WINDOW)
            cur = pl.multiple_of(start, WINDOW)
            kvw = jnp.concatenate([kv_ref[pl.ds(prev, WINDOW), :], kv_ref[pl.ds(cur, WINDOW), :]], axis=0)
            return row0, _half_tiles(kvw[:, :KV_WIDTH]), _half_tiles(kvw[:, KV_WIDTH:]), bias_ref[jnp.minimum(start, 1)]

        def block_group(bb, carry):
            windows = [window(bb * ATTN_BLOCKS + n) for n in range(ATTN_BLOCKS)]
            for j in range(N_KV_HEADS):
                for pr in range(PAIRS):
                    cols = slice((j * PAIRS + pr) * LANES, (j * PAIRS + pr + 1) * LANES)
                    o_ts = [jnp.zeros((LANES, WINDOW), F32) for _ in windows]
                    for parity in range(2):
                        h = j * GROUP + 2 * pr + parity
                        sink = sink_ref[h] * LOG2E
                        for n, (row0, k_halves, v_halves, bias) in enumerate(windows):
                            qp = q_ref[pl.ds(row0, WINDOW), cols]
                            s = lax.dot_general(k_halves[j][parity], qp, NT_DIMS, preferred_element_type=F32)
                            s = s * SCORE_SCALE + bias
                            m = jnp.maximum(jnp.max(s, axis=0, keepdims=True), sink)
                            p = jnp.exp2(s - m)
                            denom = jnp.sum(p, axis=0, keepdims=True) + jnp.exp2(sink - m)
                            pv = lax.dot_general(v_halves[j][parity], p.astype(BF16), TN_DIMS,
                                                 preferred_element_type=F32)
                            o_ts[n] = o_ts[n] + pv * (1.0 / denom)
                            lse_ref[h:h + 1, pl.ds(row0, WINDOW)] = m + jnp.log2(denom)
                    for n, (row0, _, _, _) in enumerate(windows):
                        o_ref[pl.ds(row0, WINDOW), cols] = jnp.transpose(o_ts[n].astype(BF16))
            return carry

        lax.fori_loop(0, nblk // ATTN_BLOCKS, block_group, 0)

    return _call(
        body, "attn_fwd", (t // tq,), [z, z, sinks],
        [pl.BlockSpec((tq, D_MODEL), lambda i: (i, 0)),
         pl.BlockSpec((t, 2 * KV_WIDTH), lambda i: (0, KV_COL // (2 * KV_WIDTH))),
         pl.BlockSpec(memory_space=pltpu.SMEM)],
        [jax.ShapeDtypeStruct((t, D_MODEL), BF16), jax.ShapeDtypeStruct((N_Q_HEADS, t), F32)],
        [pl.BlockSpec((tq, D_MODEL), lambda i: (i, 0)), pl.BlockSpec((N_Q_HEADS, tq), lambda i: (0, i))],
        scratch=[pltpu.VMEM((2, 2 * WINDOW, WINDOW), F32)], rider=rider)


HALO = BF16_ROWS


def _shift_down(u, uh, k):
    rolled = pltpu.roll(u, k, 0)
    row = lax.broadcasted_iota(jnp.int32, (SUBLANES, u.shape[1]), 0)
    top = rolled[:SUBLANES, :]
    for j in range(k):
        top = jnp.where(row == j, uh[HALO - k + j:HALO - k + j + 1, :], top)
    return jnp.concatenate([top, rolled[SUBLANES:, :]], axis=0)


def _shift_up(u, nxt, k):
    n = u.shape[0]
    rolled = pltpu.roll(u, n - k, 0)
    row = lax.broadcasted_iota(jnp.int32, (SUBLANES, u.shape[1]), 0)
    bottom = rolled[n - SUBLANES:, :]
    for j in range(k):
        bottom = jnp.where(row == SUBLANES - k + j, nxt[j:j + 1, :], bottom)
    return jnp.concatenate([rolled[:n - SUBLANES, :], bottom], axis=0)


def _conv_inputs(cc_ref, cx_ref, hc_ref, hx_ref, first_tile):
    cc = cc_ref[...].astype(F32)
    cx = cx_ref[...].astype(F32)
    u = cc * cx
    uh = jnp.where(first_tile, 0.0, hc_ref[...].astype(F32) * hx_ref[...].astype(F32))
    return cc, cx, u, _shift_down(u, uh, 1), _shift_down(u, uh, 2)


def _z_specs(tm, order):
    per_tile = tm // HALO
    cols = [pl.BlockSpec((tm, D_MODEL), functools.partial(lambda i, j: (order(i), j), j=j)) for j in range(1, 6)]
    halos = [pl.BlockSpec((HALO, D_MODEL),
                          functools.partial(lambda i, j: (jnp.maximum(order(i) * per_tile - 1, 0), j), j=j))
             for j in (2, 3)]
    return cols + halos


def _mix_fwd(x, attn, z, vec, w_out):
    t = x.shape[0]
    tm = min(TOKEN_TILE, t)

    def body(x_ref, a_ref, cb_ref, cc_ref, cx_ref, ga_ref, gc_ref, hc_ref, hx_ref, vec_ref, w_ref,
             m_ref, x2_ref, h2_ref, o_ref):
        i = pl.program_id(0)
        _, _, u, u1, u2 = _conv_inputs(cc_ref, cx_ref, hc_ref, hx_ref, i == 0)
        cv = vec_ref[4:5, :] * u2 + vec_ref[5:6, :] * u1 + vec_ref[6:7, :] * u
        conv = cb_ref[...].astype(F32) * cv
        merged = (_sigmoid(ga_ref[...].astype(F32)) * a_ref[...].astype(F32)
                  + _sigmoid(gc_ref[...].astype(F32)) * conv)
        mb = merged.astype(BF16)
        m_ref[...] = mb
        o = jnp.dot(mb, w_ref[...], preferred_element_type=F32)
        o_ref[...] = o.astype(BF16)
        x2 = x_ref[...] + vec_ref[0:1, :] * o
        x2_ref[...] = x2
        r = lax.rsqrt(jnp.mean(x2 * x2, axis=-1, keepdims=True) + EPS)
        h2 = (x2 * r) * (vec_ref[1:2, :] * (1.0 + vec_ref[2:3, :])) + vec_ref[3:4, :]
        h2_ref[...] = h2.astype(BF16)

    tok = pl.BlockSpec((tm, D_MODEL), lambda i: (i, 0))
    outs, _ = _call(
        body, "mix_fwd", (t // tm,), [x, attn, z, z, z, z, z, z, z, vec, w_out],
        [tok, tok] + _z_specs(tm, lambda i: i) + [_full((SUBLANES, D_MODEL)), _full((D_MODEL, D_MODEL))],
        [jax.ShapeDtypeStruct((t, D_MODEL), BF16), jax.ShapeDtypeStruct((t, D_MODEL), F32),
         jax.ShapeDtypeStruct((t, D_MODEL), BF16), jax.ShapeDtypeStruct((t, D_MODEL), BF16)],
        [tok, tok, tok, tok])
    return outs


def _ffn_fwd(h2, w_t):
    t = h2.shape[0]
    tm = min(TOKEN_TILE, t)

    def body(h_ref, w_ref, gu_ref, a_ref):
        hb = h_ref[...]
        for n in range(D_FF // FF_CHUNK):
            lo, hi = n * FF_CHUNK, (n + 1) * FF_CHUNK
            g = lax.dot_general(hb, w_ref[lo:hi, :], NT_DIMS, preferred_element_type=F32)
            u = lax.dot_general(hb, w_ref[D_FF + lo:D_FF + hi, :], NT_DIMS, preferred_element_type=F32)
            sg = _sigmoid(g)
            silu = g * sg
            gu_ref[:, lo:hi] = (u * (sg + silu * (1.0 - sg))).astype(BF16)
            gu_ref[:, D_FF + lo:D_FF + hi] = silu.astype(BF16)
            a_ref[:, lo:hi] = (silu * u).astype(BF16)

    outs, _ = _call(
        body, "ffn_fwd", (t // tm,), [h2, w_t],
        [pl.BlockSpec((tm, D_MODEL), lambda i: (i, 0)), _full((2 * D_FF, D_MODEL))],
        [jax.ShapeDtypeStruct((t, 2 * D_FF), BF16), jax.ShapeDtypeStruct((t, D_FF), BF16)],
        [pl.BlockSpec((tm, 2 * D_FF), lambda i: (i, 0)), pl.BlockSpec((tm, D_FF), lambda i: (i, 0))])
    return outs


def _ffn_out_loss(a, gu, x2, target, vec, w_ffn_out):
    t = a.shape[0]
    tm = min(TOKEN_TILE, t)

    def body(a_ref, gu_ref, x2_ref, t_ref, vec_ref, w_ref, dx3_ref, df_ref, dgu_ref, acc_ref):
        @pl.when(pl.program_id(0) == 0)
        def _():
            acc_ref[...] = jnp.zeros_like(acc_ref)

        ga2 = vec_ref[0:1, :]
        gf = vec_ref[1:2, :]
        parts = min(ROW_PARTS, tm // LANES)
        part_rows = [slice(n * (tm // parts), (n + 1) * (tm // parts)) for n in range(parts)]

        def head(rows, f):
            x3 = x2_ref[rows, :] + ga2 * f
            r = lax.rsqrt(jnp.mean(x3 * x3, axis=-1, keepdims=True) + EPS)
            xn = x3 * r
            err = xn * gf - t_ref[rows, :]
            dxn = err * (gf * (1.0 / D_MODEL))
            dx3 = r * (dxn - xn * jnp.mean(dxn * xn, axis=-1, keepdims=True))
            dx3_ref[rows, :] = dx3.astype(GRAD_STREAM)
            sums = (jnp.sum(err * err, axis=0, keepdims=True),
                    jnp.sum(err * xn, axis=0, keepdims=True) * (1.0 / D_MODEL),
                    jnp.sum(dx3 * f, axis=0, keepdims=True))
            df = (dx3 * ga2).astype(BF16)
            df_ref[rows, :] = df
            return df, sums

        def tail(rows, df):
            for n in range(D_FF // FF_CHUNK):
                lo, hi = n * FF_CHUNK, (n + 1) * FF_CHUNK
                da = lax.dot_general(df, w_ref[lo:hi, :], NT_DIMS, preferred_element_type=F32)
                dgu_ref[rows, lo:hi] = (da * gu_ref[rows, lo:hi].astype(F32)).astype(BF16)
                dgu_ref[rows, D_FF + lo:D_FF + hi] = (da * gu_ref[rows, D_FF + lo:D_FF + hi].astype(F32)).astype(BF16)

        fs = [jnp.dot(a_ref[rows, :], w_ref[...], preferred_element_type=F32) for rows in part_rows]
        heads = [head(rows, f) for rows, f in zip(part_rows, fs)]
        for rows, (df, _) in zip(part_rows, heads):
            tail(rows, df)
        for k in range(3):
            total = heads[0][1][k]
            for _, sums in heads[1:]:
                total = total + sums[k]
            acc_ref[k:k + 1, :] += total

    tok = pl.BlockSpec((tm, D_MODEL), lambda i: (i, 0))
    outs, _ = _call(
        body, "ffn_out_loss", (t // tm,), [a, gu, x2, target, vec, w_ffn_out],
        [pl.BlockSpec((tm, D_FF), lambda i: (i, 0)), pl.BlockSpec((tm, 2 * D_FF), lambda i: (i, 0)),
         tok, tok, _full((SUBLANES, D_MODEL)), _full((D_FF, D_MODEL))],
        [jax.ShapeDtypeStruct((t, D_MODEL), GRAD_STREAM), jax.ShapeDtypeStruct((t, D_MODEL), BF16),
         jax.ShapeDtypeStruct((t, 2 * D_FF), BF16), jax.ShapeDtypeStruct((SUBLANES, D_MODEL), F32)],
        [tok, tok, pl.BlockSpec((tm, 2 * D_FF), lambda i: (i, 0)), _full((SUBLANES, D_MODEL))])
    return outs


def _ffn_in_bwd(dgu, x2, dx3, vec, w_t, rider):
    t = x2.shape[0]
    tm = min(TOKEN_TILE, t)

    def body(dgu_ref, x2_ref, dx3_ref, vec_ref, wf_ref, dx2_ref, acc_ref):
        @pl.when(pl.program_id(0) == 0)
        def _():
            acc_ref[...] = jnp.zeros_like(acc_ref)

        gffn = vec_ref[0:1, :]
        sc2 = vec_ref[1:2, :]
        parts = min(ROW_PARTS, tm // LANES)
        part_rows = [slice(n * (tm // parts), (n + 1) * (tm // parts)) for n in range(parts)]
        dhs = [jnp.dot(dgu_ref[rows, :], wf_ref[...], preferred_element_type=F32) for rows in part_rows]
        gs = gffn * (1.0 + sc2)
        sum_dh = jnp.zeros((1, D_MODEL), F32)
        sum_dh_xn = jnp.zeros((1, D_MODEL), F32)
        for rows, dh2 in zip(part_rows, dhs):
            x2 = x2_ref[rows, :]
            r = lax.rsqrt(jnp.mean(x2 * x2, axis=-1, keepdims=True) + EPS)
            xn = x2 * r
            dh_xn = dh2 * xn
            sum_dh = sum_dh + jnp.sum(dh2, axis=0, keepdims=True)
            sum_dh_xn = sum_dh_xn + jnp.sum(dh_xn, axis=0, keepdims=True)
            dx2 = dx3_ref[rows, :].astype(F32) + r * (dh2 * gs - xn * jnp.mean(dh_xn * gs, axis=-1, keepdims=True))
            dx2_ref[rows, :] = dx2.astype(GRAD_STREAM)
        acc_ref[0:1, :] += sum_dh
        acc_ref[1:2, :] += sum_dh_xn * gffn
        acc_ref[2:3, :] += sum_dh_xn * (1.0 + sc2)

    tok = pl.BlockSpec((tm, D_MODEL), lambda i: (i, 0))
    return _call(
        body, "ffn_in_bwd", (t // tm,), [dgu, x2, dx3, vec, w_t],
        [pl.BlockSpec((tm, 2 * D_FF), lambda i: (i, 0)), tok, tok, _full((SUBLANES, D_MODEL)),
         _full((2 * D_FF, D_MODEL))],
        [jax.ShapeDtypeStruct((t, D_MODEL), GRAD_STREAM), jax.ShapeDtypeStruct((SUBLANES, D_MODEL), F32)],
        [tok, _full((SUBLANES, D_MODEL))], rider=rider)


def _mix_bwd(dx2, oproj, attn, z, vec, w_out, rider):
    t = dx2.shape[0]
    tm = min(TOKEN_TILE, t)
    nt = t // tm
    rev = lambda i: nt - 1 - i

    def body(dx2_ref, m_ref, a_ref, cb_ref, cc_ref, cx_ref, ga_ref, gc_ref, hc_ref, hx_ref,
             vec_ref, wo_ref, do_ref, da_ref, dr_ref, acc_ref, carry_ref):
        i = pl.program_id(0)

        @pl.when(i == 0)
        def _():
            acc_ref[...] = jnp.zeros_like(acc_ref)
            carry_ref[...] = jnp.zeros_like(carry_ref)

        ga1 = vec_ref[0:1, :]
        w0, w1, w2 = vec_ref[1:2, :], vec_ref[2:3, :], vec_ref[3:4, :]
        dx2 = dx2_ref[...].astype(F32)
        acc_ref[0:1, :] += jnp.sum(dx2 * m_ref[...].astype(F32), axis=0, keepdims=True)
        do = (dx2 * ga1).astype(BF16)
        do_ref[...] = do
        dm = lax.dot_general(do, wo_ref[...], NT_DIMS, preferred_element_type=F32)

        cc, cx, u, u1, u2 = _conv_inputs(cc_ref, cx_ref, hc_ref, hx_ref, i == nt - 1)
        cv = w0 * u2 + w1 * u1 + w2 * u
        cb = cb_ref[...].astype(F32)
        sa = _sigmoid(ga_ref[...].astype(F32))
        sc = _sigmoid(gc_ref[...].astype(F32))
        attn = a_ref[...].astype(F32)
        dattn = dm * sa
        da_ref[...] = dattn.astype(BF16)
        dconv = dm * sc
        dconv_b = dconv * cv
        dr_ref[:, 3 * D_MODEL:4 * D_MODEL] = (dattn * attn * (1.0 - sa)).astype(BF16)
        dr_ref[:, 4 * D_MODEL:5 * D_MODEL] = (dconv_b * cb * (1.0 - sc)).astype(BF16)
        dr_ref[:, 0:D_MODEL] = dconv_b.astype(BF16)
        dcv = dconv * cb
        acc_ref[1:2, :] += jnp.sum(dcv * u2, axis=0, keepdims=True)
        acc_ref[2:3, :] += jnp.sum(dcv * u1, axis=0, keepdims=True)
        acc_ref[3:4, :] += jnp.sum(dcv * u, axis=0, keepdims=True)
        nxt = carry_ref[...]
        du = w2 * dcv + w1 * _shift_up(dcv, nxt, 1) + w0 * _shift_up(dcv, nxt, 2)
        carry_ref[...] = dcv[0:SUBLANES, :]
        dr_ref[:, D_MODEL:2 * D_MODEL] = (du * cx).astype(BF16)
        dr_ref[:, 2 * D_MODEL:3 * D_MODEL] = (du * cc).astype(BF16)

    tok = pl.BlockSpec((tm, D_MODEL), lambda i: (rev(i), 0))
    return _call(
        body, "mix_bwd", (nt,), [dx2, oproj, attn, z, z, z, z, z, z, z, vec, w_out],
        [tok, tok, tok] + _z_specs(tm, rev) + [_full((SUBLANES, D_MODEL)), _full((D_MODEL, D_MODEL))],
        [jax.ShapeDtypeStruct((t, D_MODEL), BF16), jax.ShapeDtypeStruct((t, D_MODEL), BF16),
         jax.ShapeDtypeStruct((t, REST_WIDTH), BF16), jax.ShapeDtypeStruct((SUBLANES, D_MODEL), F32)],
        [tok, tok, pl.BlockSpec((tm, REST_WIDTH), lambda i: (rev(i), 0)), _full((SUBLANES, D_MODEL))],
        scratch=[pltpu.VMEM((SUBLANES, D_MODEL), F32)], rider=rider)


def _attn_bwd(z, dattn, attn, lse, sinks, rider):
    t = z.shape[0]
    tq = min(TOKEN_TILE, t)
    nblk = tq // WINDOW
    nt = t // tq

    def body(q_ref, kv_ref, do_ref, o_ref, lse_ref, sink_ref, dq_ref, dkv_ref, ds_ref, acc_ref, bias_ref):
        i = pl.program_id(0)

        @pl.when(i == 0)
        def _():
            acc_ref[...] = jnp.zeros_like(acc_ref)
            ds_ref[...] = jnp.zeros_like(ds_ref)
            _fill_window_bias(bias_ref)

        lane = lax.broadcasted_iota(jnp.int32, (1, LANES), 1)
        ind_row = lax.broadcasted_iota(jnp.int32, (SUBLANES, LANES), 0)
        ind_low = lax.broadcasted_iota(jnp.int32, (SUBLANES, LANES), 1) < HEAD_DIM
        indicator = jnp.where(jnp.logical_or(jnp.logical_and(ind_row == 0, ind_low),
                                             jnp.logical_and(ind_row == 1, jnp.logical_not(ind_low))),
                              1.0, 0.0).astype(BF16)
        low = lax.broadcasted_iota(jnp.int32, (2 * WINDOW, LANES), 1) < HEAD_DIM

        def both_heads(even, odd):
            picked = jnp.where(low, even, odd)
            return picked + jnp.concatenate([picked[:, HEAD_DIM:], picked[:, :HEAD_DIM]], axis=1)

        def window(b):
            row0 = pl.multiple_of(b * WINDOW, WINDOW)
            start = i * tq + b * WINDOW
            prev = pl.multiple_of(jnp.maximum(start - WINDOW, 0), WINDOW)
            cur = pl.multiple_of(start, WINDOW)
            kvw = jnp.concatenate([kv_ref[pl.ds(prev, WINDOW), :], kv_ref[pl.ds(cur, WINDOW), :]], axis=0)
            return (row0, prev, cur, _half_tiles(kvw[:, :KV_WIDTH]), _half_tiles(kvw[:, KV_WIDTH:]),
                    bias_ref[jnp.minimum(start, 1)])

        def block_group(bb, dsink):
            windows = [window(bb * ATTN_BWD_BLOCKS + n) for n in range(ATTN_BWD_BLOCKS)]
            dk_groups = [[] for _ in windows]
            dv_groups = [[] for _ in windows]
            for j in range(N_KV_HEADS):
                stacks, deltas, dq_ts = [], [], []
                for row0, _, _, _, _, _ in windows:
                    qst = _stack_pairs(q_ref, row0, j)
                    dost = _stack_pairs(do_ref, row0, j)
                    prod = dost.astype(F32) * _stack_pairs(o_ref, row0, j).astype(F32)
                    prod_hi = prod.astype(BF16)
                    prod_lo = (prod - prod_hi.astype(F32)).astype(BF16)
                    stacks.append((qst, dost))
                    deltas.append(lax.dot_general(indicator, prod_hi, NT_DIMS, preferred_element_type=F32)
                                  + lax.dot_general(indicator, prod_lo, NT_DIMS, preferred_element_type=F32))
                    dq_ts.append(jnp.zeros((LANES, STACK), F32))
                dk_par = [[] for _ in windows]
                dv_par = [[] for _ in windows]
                for parity in range(2):
                    heads = [j * GROUP + 2 * p + parity for p in range(PAIRS)]
                    sink = _per_pair_row([sink_ref[h] * LOG2E for h in heads])
                    for n, (row0, _, _, k_halves, v_halves, bias) in enumerate(windows):
                        qst, dost = stacks[n]
                        kk, vv = k_halves[j][parity], v_halves[j][parity]
                        s = lax.dot_general(kk, qst, NT_DIMS, preferred_element_type=F32) * SCORE_SCALE + bias
                        lse = jnp.concatenate([lse_ref[h:h + 1, pl.ds(row0, WINDOW)] for h in heads], axis=1)
                        p = jnp.exp2(s - lse)
                        dp = lax.dot_general(vv, dost, NT_DIMS, preferred_element_type=F32)
                        delta = deltas[n][parity:parity + 1, :]
                        dsb = (p * (dp - delta)).astype(BF16)
                        dq_ts[n] = dq_ts[n] + lax.dot_general(kk, dsb, TN_DIMS, preferred_element_type=F32)
                        dk_par[n].append(jnp.dot(dsb, qst, preferred_element_type=F32))
                        dv_par[n].append(jnp.dot(p.astype(BF16), dost, preferred_element_type=F32))
                        weighted = jnp.exp2(sink - lse) * delta
                        for pr, h in enumerate(heads):
                            dsink = dsink - jnp.where(
                                lane == h, jnp.sum(weighted[:, pr * WINDOW:(pr + 1) * WINDOW]), 0.0)
                for n, (row0, _, _, _, _, _) in enumerate(windows):
                    dq_st = jnp.transpose((dq_ts[n] * ATTN_SCALE).astype(BF16))
                    for pr in range(PAIRS):
                        dq_ref[pl.ds(row0, WINDOW), (j * PAIRS + pr) * LANES:(j * PAIRS + pr + 1) * LANES] = (
                            dq_st[pr * WINDOW:(pr + 1) * WINDOW, :])
                    dk_groups[n].append(both_heads(dk_par[n][0], dk_par[n][1]))
                    dv_groups[n].append(both_heads(dv_par[n][0], dv_par[n][1]))
            for n, (_, prev, cur, _, _, _) in enumerate(windows):
                blk = jnp.concatenate([jnp.where(low, dk_groups[n][0], dk_groups[n][1]) * ATTN_SCALE,
                                       jnp.where(low, dv_groups[n][0], dv_groups[n][1])], axis=1)
                acc_ref[pl.ds(prev, WINDOW), :] += blk[:WINDOW, :]
                acc_ref[pl.ds(cur, WINDOW), :] += blk[WINDOW:, :]
            return dsink

        dsink = lax.fori_loop(0, nblk // ATTN_BWD_BLOCKS, block_group, jnp.zeros((1, LANES), F32))
        ds_ref[0:1, :] += dsink

        @pl.when(i == nt - 1)
        def _():
            dkv_ref[...] = acc_ref[...].astype(BF16)

    tok = pl.BlockSpec((tq, D_MODEL), lambda i: (i, 0))
    return _call(
        body, "attn_bwd", (nt,), [z, z, dattn, attn, lse, sinks],
        [tok, pl.BlockSpec((t, 2 * KV_WIDTH), lambda i: (0, KV_COL // (2 * KV_WIDTH))), tok, tok,
         pl.BlockSpec((N_Q_HEADS, tq), lambda i: (0, i)), pl.BlockSpec(memory_space=pltpu.SMEM)],
        [jax.ShapeDtypeStruct((t, D_MODEL), BF16), jax.ShapeDtypeStruct((t, 2 * KV_WIDTH), BF16),
         jax.ShapeDtypeStruct((SUBLANES, LANES), F32)],
        [tok, _full((t, 2 * KV_WIDTH)), _full((SUBLANES, LANES))],
        scratch=[pltpu.VMEM((t, 2 * KV_WIDTH), F32), pltpu.VMEM((2, 2 * WINDOW, STACK), F32)], rider=rider)


def _inproj_bwd(dq, drest, dkv, x, dx2, vec, w_t, rider):
    t = x.shape[0]
    tm = min(TOKEN_TILE, t)

    def body(dq_ref, dr_ref, dkv_ref, x_ref, dx2_ref, vec_ref, w_ref, gx_ref, acc_ref, db_ref):
        @pl.when(pl.program_id(0) == 0)
        def _():
            acc_ref[...] = jnp.zeros_like(acc_ref)
            db_ref[...] = jnp.zeros_like(db_ref)

        g = vec_ref[0:1, :]
        sc1 = vec_ref[1:2, :]
        dqb, drb, dkvb = dq_ref[...], dr_ref[...], dkv_ref[...]
        dh = jnp.dot(dqb, w_ref[:REF_KV_COL, :], preferred_element_type=F32)
        dh = dh + jnp.dot(drb, w_ref[REF_REST_COL:, :], preferred_element_type=F32)
        dh = dh + jnp.dot(dkvb, w_ref[REF_KV_COL:REF_REST_COL, :], preferred_element_type=F32)
        db_ref[:, :REF_KV_COL] += jnp.sum(dqb.astype(F32), axis=0, keepdims=True)
        db_ref[:, REF_REST_COL:] += jnp.sum(drb.astype(F32), axis=0, keepdims=True)
        db_ref[:, REF_KV_COL:REF_REST_COL] += jnp.sum(dkvb.astype(F32), axis=0, keepdims=True)
        xf = x_ref[...]
        r = lax.rsqrt(jnp.mean(xf * xf, axis=-1, keepdims=True) + EPS)
        xn = xf * r
        gs = g * (1.0 + sc1)
        dh_xn = dh * xn
        sum_dh_xn = jnp.sum(dh_xn, axis=0, keepdims=True)
        acc_ref[0:1, :] += jnp.sum(dh, axis=0, keepdims=True)
        acc_ref[1:2, :] += sum_dh_xn * g
        acc_ref[2:3, :] += sum_dh_xn * (1.0 + sc1)
        gx_ref[...] = dx2_ref[...].astype(F32) + r * (dh * gs - xn * jnp.mean(dh_xn * gs, axis=-1, keepdims=True))

    tok = pl.BlockSpec((tm, D_MODEL), lambda i: (i, 0))
    return _call(
        body, "inproj_bwd", (t // tm,), [dq, drest, dkv, x, dx2, vec, w_t],
        [tok, pl.BlockSpec((tm, REST_WIDTH), lambda i: (i, 0)),
         pl.BlockSpec((tm, 2 * KV_WIDTH), lambda i: (i, 0)), tok, tok,
         _full((SUBLANES, D_MODEL)), _full((IN_WIDTH, D_MODEL))],
        [jax.ShapeDtypeStruct((t, D_MODEL), F32), jax.ShapeDtypeStruct((SUBLANES, D_MODEL), F32),
         jax.ShapeDtypeStruct((1, IN_WIDTH), F32)],
        [tok, _full((SUBLANES, D_MODEL)), _full((1, IN_WIDTH))], rider=rider)


def _weight_grad(b, a, name, bn, rows=None, row0=0, into=None, rider=None):
    pieces = list(b) if isinstance(b, (list, tuple)) else [b]
    widths = [p.shape[1] for p in pieces]
    t, n = pieces[0].shape[0], sum(widths)
    assert len(pieces) == 1 or bn == n
    m = a.shape[1]
    rows = n if rows is None else rows
    tk = min(TOKEN_TILE, t)
    for cand in (4 * TOKEN_TILE, 2 * TOKEN_TILE):
        if t % cand == 0 and 2 * cand * (bn + m) * 2 + bn * m * 4 <= WGRAD_VMEM:
            tk = cand
            break
    nk = t // tk
    block0 = row0 // bn

    npieces = len(pieces)

    def body(*refs):
        b_refs, a_ref = refs[:npieces], refs[npieces]
        out_ref, acc_ref = refs[-2:]
        k = pl.program_id(1)

        @pl.when(k == 0)
        def _():
            acc_ref[...] = jnp.zeros_like(acc_ref)

        if npieces == 1:
            acc_ref[...] += lax.dot_general(b_refs[0][...], a_ref[...], TN_DIMS, preferred_element_type=F32)
        else:
            lo = 0
            for b_ref, width in zip(b_refs, widths):
                acc_ref[lo:lo + width, :] += lax.dot_general(b_ref[...], a_ref[...], TN_DIMS,
                                                             preferred_element_type=F32)
                lo += width

        @pl.when(k == nk - 1)
        def _():
            out_ref[...] = acc_ref[...].astype(BF16)

    if npieces == 1:
        b_specs = [pl.BlockSpec((tk, bn), lambda j, k: (k, j))]
    else:
        b_specs = [pl.BlockSpec((tk, width), lambda j, k: (k, 0)) for width in widths]
    outs, routs = _call(
        body, name, (n // bn, nk), pieces + [a] + ([] if into is None else [into]),
        b_specs + [pl.BlockSpec((tk, m), lambda j, k: (k, 0))] + ([] if into is None else [ANY]),
        [jax.ShapeDtypeStruct((rows, m), BF16)], [pl.BlockSpec((bn, m), lambda j, k: (block0 + j, 0))],
        scratch=[pltpu.VMEM((bn, m), F32)], rider=rider, aliases=None if into is None else {npieces + 1: 0})
    return outs[0], routs


def _to_rows(v):
    n = v.shape[0]
    padded = -(-n // (SUBLANES * LANES)) * SUBLANES * LANES
    return jnp.pad(v, (0, padded - n)).reshape(padded // LANES, LANES)


def _vec_rows(*rows):
    stacked = jnp.concatenate([r.reshape(1, D_MODEL) for r in rows], axis=0)
    return jnp.pad(stacked, ((0, SUBLANES - len(rows)), (0, 0)))


def kernel(x, c, w_ada, b_ada, g_mix, w_in, b_in, sinks, conv_w, w_out, g_ffn, w_ffn_in, w_ffn_out, g_final, loss_target, m_w_ada, m_b_ada, m_g_mix, m_w_in, m_b_in, m_sinks, m_conv_w, m_w_out, m_g_ffn, m_w_ffn_in, m_w_ffn_out, m_g_final, v_w_ada, v_b_ada, v_g_mix, v_w_in, v_b_in, v_sinks, v_conv_w, v_w_out, v_g_ffn, v_w_ffn_in, v_w_ffn_out, v_g_final):
    ix, iy, ic = _my_place()
    me = 4 * ix + 2 * iy + ic
    xs = x[0]
    target = loss_target[0]
    ada_cols = w_ada.shape[2]
    conv_cols = conv_w.shape[2]

    wt_in, wt_fi = jnp.transpose(w_in[0]), jnp.transpose(w_ffn_in[0])
    b_cols = lax.dynamic_slice_in_dim(b_ada, me * ada_cols, ada_cols, axis=1)
    g_in, (cast_fi, cast_out, cast_fo), first, mod_all = _gather_first_weight(
        wt_in, [wt_fi, w_out[0], w_ffn_out[0]], _to_rows(jnp.concatenate([c[0], conv_w[0].reshape(-1)])),
        w_ada[0], b_cols)
    first = first.reshape(N_DEV, -1)
    c_all = first[:, :D_MODEL]
    conv_full = jnp.transpose(first[:, D_MODEL:D_MODEL + 3 * conv_cols].reshape(N_DEV, 3, conv_cols), (1, 0, 2))
    conv_full = conv_full.reshape(3, D_MODEL)
    mod = lax.dynamic_index_in_dim(mod_all, me, axis=1, keepdims=False).reshape(N_MOD, D_MODEL)
    sh1, sc1, ga1, sh2, sc2, ga2 = [mod[i:i + 1] for i in range(N_MOD)]
    w_in_t = g_in.reshape(IN_WIDTH, D_MODEL)
    (z, h1), (g_fi, g_out) = _inproj_fwd(xs, _vec_rows(g_mix, sc1, sh1), w_in_t, b_in,
                                         _gather_rider([cast_fi, cast_out]))
    w_fi_t = g_fi.reshape(2 * D_FF, D_MODEL)
    w_out_full = g_out.reshape(D_MODEL, D_MODEL)
    (attn, lse), (g_fo,) = _attn_fwd(z, sinks[0], _gather_rider([cast_fo]))
    w_fo_full = g_fo.reshape(D_FF, D_MODEL)
    merged, x2, h2, oproj = _mix_fwd(
        xs, attn, z, _vec_rows(ga1, g_ffn, sc2, sh2, conv_full[0], conv_full[1], conv_full[2]), w_out_full)
    gu, act = _ffn_fwd(h2, w_fi_t)
    dx3, df, dgu, acc_l = _ffn_out_loss(act, gu, x2, target, _vec_rows(ga2, g_final), w_fo_full)

    gw_fo, _ = _weight_grad(act, df, "wgrad_ffn_out", D_FF)
    gw_fi, _ = _weight_grad(dgu, h2, "wgrad_ffn_in", D_FF)
    blocks_fo = gw_fo.reshape(N_DEV, D_FF // N_DEV, D_MODEL)
    blocks_fi = gw_fi.reshape(N_DEV, 2 * D_FF // N_DEV, D_MODEL)
    (dx2, acc_f), (sib_fo, sib_fi) = _ffn_in_bwd(dgu, x2, dx3, _vec_rows(g_ffn, sc2), w_fi_t,
                                                 _sibling_rider([blocks_fo, blocks_fi]))
    sums_fo, mine_fo = _sibling_sum(blocks_fo, sib_fo, "sibling_sum_ffn_out")
    sums_fi, mine_fi = _sibling_sum(blocks_fi, sib_fi, "sibling_sum_ffn_in")
    (dout, dattn, drest, acc_m), (ici_fo, ici_fi) = _mix_bwd(
        dx2, oproj, attn, z, _vec_rows(ga1, conv_full[0], conv_full[1], conv_full[2]), w_out_full,
        _chip_rider([sums_fo, sums_fi]))
    gw_out, _ = _weight_grad(merged, dout, "wgrad_out", D_MODEL)
    blocks_out = gw_out.reshape(N_DEV, D_MODEL // N_DEV, D_MODEL)
    (dq, dkv, dsink), _ = _attn_bwd(z, dattn, attn, lse, sinks[0], None)
    gw_in, _ = _weight_grad(drest, h1, "wgrad_in_rest", IN_CHUNK, rows=IN_WIDTH, row0=REF_REST_COL)
    gw_in, _ = _weight_grad([dq, dkv], h1, "wgrad_in_qkv", REF_REST_COL, rows=IN_WIDTH, row0=0, into=gw_in)
    blocks_in = gw_in.reshape(N_DEV, IN_WIDTH // N_DEV, D_MODEL)
    (sums_in, mine_in), (sums_out, mine_out) = _sibling_exchange_sum([blocks_in, blocks_out], "sibling_w_in_out")
    (grad_x, acc_i, db_in), (ici_in, ici_out) = _inproj_bwd(dq, drest, dkv, xs, dx2, _vec_rows(g_mix, sc1), w_in_t,
                                                            _chip_rider([sums_in, sums_out]))

    widen = lambda vec: jnp.pad(vec, (0, -vec.shape[0] % D_MODEL))
    packed = jnp.concatenate([
        acc_i[0], acc_i[1], acc_m[0], acc_f[0], acc_f[1], acc_l[2],
        acc_i[2], widen(db_in[0]), acc_f[2], acc_l[1],
        acc_m[1], acc_m[2], acc_m[3], widen(dsink[0]), acc_l[0],
        jnp.zeros(((PACK_ROWS - PACK_SQERR - 1) * D_MODEL,), F32)]).reshape(PACK_ROWS, D_MODEL)
    packed_all = _small_allgather(packed, "gather_small")
    dmod_all = packed_all[:, PACK_DMOD:PACK_DMOD + N_MOD, :].reshape(N_DEV, N_MOD * D_MODEL)
    dmod_cols = lax.dynamic_slice_in_dim(dmod_all, me * ada_cols, ada_cols, axis=1)
    g_w_ada = _ada_weight_grad(c_all, dmod_cols)
    row_of = lambda a: a.reshape(1, -1)
    small, g_conv_full, loss = _small_finalize(packed_all, {
        "b_ada": (b_ada, m_b_ada, v_b_ada), "g_mix": (g_mix, m_g_mix, v_g_mix), "b_in": (b_in, m_b_in, v_b_in),
        "g_ffn": (g_ffn, m_g_ffn, v_g_ffn), "sinks": (sinks, m_sinks, v_sinks),
        "g_final": (row_of(g_final), row_of(m_g_final), row_of(v_g_final))})
    small["g_final"] = tuple(o.reshape(g_final.shape) for o in small["g_final"])
    g_conv = lax.dynamic_slice_in_dim(g_conv_full, me * conv_cols, conv_cols, axis=1)
    d_conv, nm_conv, nv_conv = _adamw(conv_w[0], g_conv, m_conv_w[0], v_conv_w[0], "adamw_conv_w")
    small["conv_w"] = (g_conv[None], d_conv[None], nm_conv[None], nv_conv[None])

    def reduced(mine, ici, w, m, v, name, transposed=False):
        turn = jnp.transpose if transposed else (lambda a: a)
        return tuple(turn(o)[None] for o in _chip_sum_adamw(mine, ici, turn(w[0]), turn(m[0]), turn(v[0]), name))

    d_ada, nm_ada, nv_ada = _adamw(w_ada[0], g_w_ada, m_w_ada[0], v_w_ada[0], "adamw_w_ada")
    res = {
        "w_ada": (g_w_ada[None], d_ada[None], nm_ada[None], nv_ada[None]),
        "w_in": reduced(mine_in, ici_in, w_in, m_w_in, v_w_in, "adamw_w_in", transposed=True),
        "w_out": reduced(mine_out, ici_out, w_out, m_w_out, v_w_out, "adamw_w_out"),
        "w_ffn_in": reduced(mine_fi, ici_fi, w_ffn_in, m_w_ffn_in, v_w_ffn_in, "adamw_w_ffn_in", transposed=True),
        "w_ffn_out": reduced(mine_fo, ici_fo, w_ffn_out, m_w_ffn_out, v_w_ffn_out, "adamw_w_ffn_out"),
    }
    res.update(small)
    order = ["w_ada", "b_ada", "g_mix", "w_in", "b_in", "sinks", "conv_w", "w_out", "g_ffn", "w_ffn_in", "w_ffn_out",
             "g_final"]
    outs = [loss.reshape(()), grad_x[None]]
    for k in range(4):
        outs += [res[n][k] for n in order]
    return tuple(outs)
```

```python
import functools
import math

import jax
import jax.numpy as jnp
from jax import lax
from jax.experimental import pallas as pl
from jax.experimental.pallas import tpu as pltpu

F32 = jnp.float32
BF16 = jnp.bfloat16
GRAD_STREAM = F32

D_MODEL = 1024
HEAD_DIM = 64
N_Q_HEADS = 16
N_KV_HEADS = 2
GROUP = 8
WINDOW = 128
KV_WIDTH = N_KV_HEADS * HEAD_DIM
D_FF = 2816
IN_WIDTH = 6400
N_MOD = 6
EPS = 1e-6
N_DEV = 8
REST_WIDTH = 5 * D_MODEL
KV_COL = D_MODEL + REST_WIDTH
ATTN_SCALE = HEAD_DIM ** -0.5

ADAM_LR = 0.001
ADAM_B1 = 0.9
ADAM_B2 = 0.999
ADAM_EPS = 1e-08
ADAM_WD = 0.01
ADAM_STEP = 10

LANES = 128
SUBLANES = 8
BF16_ROWS = 16
VMEM_LIMIT = 56 * 1024 * 1024
TOKEN_TILE = 512
FF_CHUNK = 256
ROW_PARTS = 2
MIN_STREAM_STEPS = 2
WGRAD_VMEM = 40 * 1024 * 1024
MESH = pl.DeviceIdType.MESH
ANY = pl.BlockSpec(memory_space=pl.ANY)

NT_DIMS = (((1,), (1,)), ((), ()))
TN_DIMS = (((0,), (0,)), ((), ()))
CHIP_FLIPS = [(0, 0), (1, 0), (0, 1), (1, 1)]


def _full(shape):
    return pl.BlockSpec(shape, lambda *_: (0,) * len(shape))


def _my_place():
    return lax.axis_index("x"), lax.axis_index("y"), lax.axis_index("c")


def _flip(v, bit):
    return 1 - v if bit else v


def _sigmoid(v):
    return 1.0 / (1.0 + jnp.exp2(v * (-1.4426950408889634)))


class _Rider:
    def __init__(self, ins, out_shapes, sem_shapes, first=None, mid=None, last=None, ins_in_vmem=False):
        self.ins, self.out_shapes, self.sem_shapes = list(ins), list(out_shapes), list(sem_shapes)
        self.in_specs = [_full(a.shape) if ins_in_vmem else ANY for a in self.ins]
        self.hooks = [(when, fn) for when, fn in (("first", first), ("mid", mid), ("last", last)) if fn is not None]


def _call(body, name, grid, args, in_specs, out_shape, out_specs, scratch=(), rider=None, aliases=None):
    n_in, n_out, n_scr = len(args), len(out_shape), len(scratch)
    r_in = rider.ins if rider else []
    r_out = rider.out_shapes if rider else []
    r_sem = rider.sem_shapes if rider else []
    nsteps = math.prod(grid)

    def full_body(*refs):
        pos = 0
        groups = []
        for size in (n_in, len(r_in), n_out, len(r_out), n_scr, len(r_sem)):
            groups.append(refs[pos:pos + size])
            pos += size
        ins, rins, outs, routs, scr, rsems = groups
        step = pl.program_id(0)
        for axis in range(1, len(grid)):
            step = step * grid[axis] + pl.program_id(axis)
        at = {"first": 0, "mid": (3 * nsteps) // 4, "last": nsteps - 1}
        hooks = rider.hooks if rider else []
        for when, fn in hooks:
            if when != "last":
                pl.when(step == at[when])(functools.partial(fn, rins, routs, rsems))
        body(*ins, *outs, *scr)
        for when, fn in hooks:
            if when == "last":
                pl.when(step == at[when])(functools.partial(fn, rins, routs, rsems))

    outs = pl.pallas_call(
        full_body, name=name, grid=grid,
        out_shape=list(out_shape) + list(r_out),
        in_specs=list(in_specs) + (rider.in_specs if rider else []),
        out_specs=list(out_specs) + [ANY] * len(r_out),
        scratch_shapes=list(scratch) + list(r_sem),
        input_output_aliases=dict(aliases or {}),
        compiler_params=pltpu.CompilerParams(dimension_semantics=("arbitrary",) * len(grid),
                                             vmem_limit_bytes=VMEM_LIMIT),
    )(*args, *r_in)
    return list(outs[:n_out]), list(outs[n_out:])


def _gather_rider(shards):
    n = len(shards)

    def setup(outs, sems):
        x, y, c = _my_place()
        send_sems, recv_sems, _ = sems
        chips = [(1 - x, y), (x, 1 - y), (1 - x, 1 - y)]

        def block(w, place):
            return outs[w].at[4 * place[0] + 2 * place[1] + place[2]]

        def copy(w, k, place, to, src=None):
            return pltpu.make_async_remote_copy(
                src_ref=block(w, place) if src is None else src, dst_ref=block(w, place),
                send_sem=send_sems.at[w, k], recv_sem=recv_sems.at[w, k], device_id=to, device_id_type=MESH)

        return (x, y, c), (x, y, 1 - c), chips, block, copy

    def first(ins, outs, sems):
        me, sibling, chips, block, copy = setup(outs, sems)
        for w in range(n):
            pltpu.make_async_copy(ins[w], block(w, me), sems[2].at[w]).start()
            copy(w, 0, me, sibling, src=ins[w]).start()
            for j, chip in enumerate(chips):
                copy(w, 1 + j, me, (*chip, me[2]), src=ins[w]).start()

    def mid(ins, outs, sems):
        me, sibling, chips, block, copy = setup(outs, sems)
        for w in range(n):
            for j, chip in enumerate(chips):
                copy(w, 1 + j, (*chip, me[2]), me).wait_recv()
                copy(w, 4 + j, (*chip, me[2]), sibling).start()

    def last(ins, outs, sems):
        me, sibling, chips, block, copy = setup(outs, sems)
        for w in range(n):
            copy(w, 0, sibling, me).wait_recv()
            for j, chip in enumerate(chips):
                copy(w, 4 + j, (*chip, 1 - me[2]), me).wait_recv()
            copy(w, 0, me, sibling, src=ins[w]).wait_send()
            for j, chip in enumerate(chips):
                copy(w, 1 + j, me, (*chip, me[2]), src=ins[w]).wait_send()
                copy(w, 4 + j, (*chip, me[2]), sibling).wait_send()
            pltpu.make_async_copy(ins[w], block(w, me), sems[2].at[w]).wait()

    return _Rider(
        shards, [jax.ShapeDtypeStruct((N_DEV,) + s.shape, BF16) for s in shards],
        [pltpu.SemaphoreType.DMA((n, N_DEV - 1)), pltpu.SemaphoreType.DMA((n, N_DEV - 1)),
         pltpu.SemaphoreType.DMA((n,))],
        first=first, mid=mid, last=last, ins_in_vmem=True)


def _sibling_rider(gblocks):
    n = len(gblocks)

    def copies(ins, outs, sems):
        x, y, c = _my_place()
        send_sems, recv_sems = sems
        made = []
        for w in range(n):
            for f, (fx, fy) in enumerate(CHIP_FLIPS):
                chip = 4 * _flip(x, fx) + 2 * _flip(y, fy)
                made.append(pltpu.make_async_remote_copy(
                    src_ref=ins[w].at[chip + 1 - c], dst_ref=outs[w].at[f], send_sem=send_sems.at[w, f],
                    recv_sem=recv_sems.at[w, f], device_id=(x, y, 1 - c), device_id_type=MESH))
        return made

    def first(ins, outs, sems):
        for cp in copies(ins, outs, sems):
            cp.start()

    def last(ins, outs, sems):
        for cp in copies(ins, outs, sems):
            cp.wait_recv()
            cp.wait_send()

    return _Rider(gblocks, [jax.ShapeDtypeStruct((4,) + g.shape[1:], BF16) for g in gblocks],
                  [pltpu.SemaphoreType.DMA((n, 4))] * 2, first=first, last=last)


def _chip_rider(sums):
    n = len(sums)

    def copies(ins, outs, sems):
        x, y, c = _my_place()
        send_sems, recv_sems = sems
        made = []
        for w in range(n):
            for f in (1, 2, 3):
                fx, fy = CHIP_FLIPS[f]
                made.append(pltpu.make_async_remote_copy(
                    src_ref=ins[w].at[f - 1], dst_ref=outs[w].at[f - 1], send_sem=send_sems.at[w, f - 1],
                    recv_sem=recv_sems.at[w, f - 1], device_id=(_flip(x, fx), _flip(y, fy), c), device_id_type=MESH))
        return made

    def first(ins, outs, sems):
        for cp in copies(ins, outs, sems):
            cp.start()

    def last(ins, outs, sems):
        for cp in copies(ins, outs, sems):
            cp.wait_recv()
            cp.wait_send()

    return _Rider(sums, [jax.ShapeDtypeStruct(s.shape, BF16) for s in sums],
                  [pltpu.SemaphoreType.DMA((n, 3))] * 2, first=first, last=last)


def _push_to_all(v_ref, out_ref, send_sems, recv_sems, local_sem, wait=True):
    x, y, c = _my_place()
    me = 4 * x + 2 * y + c
    mine = pltpu.make_async_copy(v_ref, out_ref.at[me], local_sem)
    mine.start()
    sends = []
    for k in range(1, N_DEV):
        px, py, pc = _flip(x, k & 4), _flip(y, k & 2), _flip(c, k & 1)
        cp = pltpu.make_async_remote_copy(
            src_ref=v_ref, dst_ref=out_ref.at[me], send_sem=send_sems.at[k - 1], recv_sem=recv_sems.at[k - 1],
            device_id=(px, py, pc), device_id_type=MESH)
        cp.start()
        sends.append(cp)

    def finish():
        for k in range(1, N_DEV):
            px, py, pc = _flip(x, k & 4), _flip(y, k & 2), _flip(c, k & 1)
            pltpu.make_async_remote_copy(
                src_ref=v_ref, dst_ref=out_ref.at[4 * px + 2 * py + pc], send_sem=send_sems.at[k - 1],
                recv_sem=recv_sems.at[k - 1], device_id=(px, py, pc), device_id_type=MESH).wait_recv()
        for cp in sends:
            cp.wait_send()
        mine.wait()

    if wait:
        finish()
    return finish


def _small_allgather(v, name):
    def body(v_ref, out_ref, send_sems, recv_sems, local_sem):
        _push_to_all(v_ref, out_ref, send_sems, recv_sems, local_sem)

    return pl.pallas_call(
        body, name=name,
        out_shape=jax.ShapeDtypeStruct((N_DEV,) + v.shape, F32),
        in_specs=[pl.BlockSpec(memory_space=pltpu.VMEM)],
        out_specs=pl.BlockSpec(memory_space=pltpu.VMEM),
        scratch_shapes=[pltpu.SemaphoreType.DMA((N_DEV - 1,)), pltpu.SemaphoreType.DMA((N_DEV - 1,)),
                        pltpu.SemaphoreType.DMA],
        compiler_params=pltpu.CompilerParams(vmem_limit_bytes=VMEM_LIMIT),
    )(v)


def _gather_first_weight(shard, others, cond_rows, w_ada, b_cols):
    n = len(others)
    ada_cols = w_ada.shape[1]
    c_rows = D_MODEL // LANES

    def body(*refs):
        w_ref, other_refs = refs[0], refs[1:1 + n]
        cond_ref, wada_ref, bcols_ref = refs[1 + n:4 + n]
        out_ref, cast_refs = refs[4 + n], refs[5 + n:5 + 2 * n]
        cond_all_ref, mod_all_ref = refs[5 + 2 * n:7 + 2 * n]
        mine_ref, mod_ref, send_sems, recv_sems, local_sem, small_send, small_recv, small_local = refs[7 + 2 * n:]
        x, y, c = _my_place()
        me, sibling = (x, y, c), (x, y, 1 - c)
        xnb, ynb, diag = (1 - x, y), (x, 1 - y), (1 - x, 1 - y)
        half = shard.shape[0] // 2

        def block(place, part=None):
            ref = out_ref.at[4 * place[0] + 2 * place[1] + place[2]]
            return ref if part is None else ref.at[pl.ds(part * half, half)]

        def copy(k, place, to, part=None, src=None):
            return pltpu.make_async_remote_copy(
                src_ref=block(place, part) if src is None else src, dst_ref=block(place, part),
                send_sem=send_sems.at[k], recv_sem=recv_sems.at[k], device_id=to, device_id_type=MESH)

        finish_cond = _push_to_all(cond_ref, cond_all_ref, small_send.at[0], small_recv.at[0], small_local.at[0],
                                   wait=False)
        mine_ref[...] = w_ref[...].astype(BF16)
        local = pltpu.make_async_copy(mine_ref, block(me), local_sem)
        local.start()
        started = [copy(0, me, sibling, src=mine_ref), copy(1, me, (*xnb, c), src=mine_ref),
                   copy(2, me, (*ynb, c), src=mine_ref)]
        for cp in started:
            cp.start()
        finish_cond()
        mod = jnp.zeros((N_DEV, ada_cols), F32) + bcols_ref[...]
        for r in range(c_rows):
            cf = cond_all_ref[:, r, :]
            act = (cf * _sigmoid(cf)).astype(BF16)
            mod = mod + jnp.dot(act, wada_ref[r * LANES:(r + 1) * LANES, :].astype(BF16),
                                preferred_element_type=F32)
        mod_ref[...] = mod
        finish_mod = _push_to_all(mod_ref, mod_all_ref, small_send.at[1], small_recv.at[1], small_local.at[1],
                                  wait=False)
        for o_ref, c_ref in zip(other_refs, cast_refs):
            c_ref[...] = o_ref[...].astype(BF16)
        def start(cp):
            cp.start()
            started.append(cp)

        copy(1, (*xnb, c), me).wait_recv()
        start(copy(3, (*xnb, c), (*ynb, c), part=0))
        start(copy(5, (*xnb, c), sibling))
        copy(2, (*ynb, c), me).wait_recv()
        start(copy(4, (*ynb, c), (*xnb, c), part=1))
        start(copy(6, (*ynb, c), sibling))
        copy(3, (*diag, c), me, part=0).wait_recv()
        start(copy(7, (*diag, c), sibling, part=0))
        copy(4, (*diag, c), me, part=1).wait_recv()
        start(copy(8, (*diag, c), sibling, part=1))
        copy(0, sibling, me).wait_recv()
        copy(5, (*xnb, 1 - c), me).wait_recv()
        copy(6, (*ynb, 1 - c), me).wait_recv()
        copy(7, (*diag, 1 - c), me, part=0).wait_recv()
        copy(8, (*diag, 1 - c), me, part=1).wait_recv()
        finish_mod()
        for cp in started:
            cp.wait_send()
        local.wait()

    vmem = pl.BlockSpec(memory_space=pltpu.VMEM)
    outs = pl.pallas_call(
        body, name="gather_w_in",
        out_shape=[jax.ShapeDtypeStruct((N_DEV,) + shard.shape, BF16)]
        + [jax.ShapeDtypeStruct(o.shape, BF16) for o in others]
        + [jax.ShapeDtypeStruct((N_DEV,) + cond_rows.shape, F32), jax.ShapeDtypeStruct((N_DEV, N_DEV, ada_cols), F32)],
        in_specs=[vmem] * (4 + n),
        out_specs=[ANY] + [vmem] * (n + 2),
        scratch_shapes=[pltpu.VMEM(shard.shape, BF16), pltpu.VMEM((N_DEV, ada_cols), F32),
                        pltpu.SemaphoreType.DMA((9,)), pltpu.SemaphoreType.DMA((9,)),
                        pltpu.SemaphoreType.DMA,
                        pltpu.SemaphoreType.DMA((2, N_DEV - 1)), pltpu.SemaphoreType.DMA((2, N_DEV - 1)),
                        pltpu.SemaphoreType.DMA((2,))],
        compiler_params=pltpu.CompilerParams(vmem_limit_bytes=VMEM_LIMIT),
    )(shard, *others, cond_rows, w_ada, b_cols)
    return outs[0], list(outs[1:1 + n]), outs[1 + n], outs[2 + n]


def _sibling_exchange_sum(gblocks, name):
    n = len(gblocks)

    def body(*refs):
        g_refs, out_refs = refs[:n], refs[n:3 * n]
        bufs = refs[3 * n:5 * n]
        own_sems, send_sems, recv_sems = refs[5 * n:]
        x, y, c = _my_place()
        pairs = []
        for w in range(n):
            own_buf, sib_buf = bufs[2 * w], bufs[2 * w + 1]
            for f, (fx, fy) in enumerate(CHIP_FLIPS):
                chip = 4 * _flip(x, fx) + 2 * _flip(y, fy)
                own = pltpu.make_async_copy(g_refs[w].at[chip + c], own_buf.at[f], own_sems.at[w, f])
                own.start()
                remote = pltpu.make_async_remote_copy(
                    src_ref=g_refs[w].at[chip + 1 - c], dst_ref=sib_buf.at[f], send_sem=send_sems.at[w, f],
                    recv_sem=recv_sems.at[w, f], device_id=(x, y, 1 - c), device_id_type=MESH)
                remote.start()
                pairs.append((own, remote))
        for w in range(n):
            own_buf, sib_buf = bufs[2 * w], bufs[2 * w + 1]
            sums_ref, mine_ref = out_refs[2 * w], out_refs[2 * w + 1]
            for f in (1, 2, 3, 0):
                own, remote = pairs[4 * w + f]
                own.wait()
                remote.wait_recv()
                total = own_buf[f].astype(F32) + sib_buf[f].astype(F32)
                if f == 0:
                    mine_ref[...] = total
                else:
                    sums_ref[f - 1] = total.astype(BF16)
        for _, remote in pairs:
            remote.wait_send()

    vmem = pl.BlockSpec(memory_space=pltpu.VMEM)
    out_shape, scratch = [], []
    for g in gblocks:
        out_shape += [jax.ShapeDtypeStruct((3,) + g.shape[1:], BF16), jax.ShapeDtypeStruct(g.shape[1:], F32)]
        scratch += [pltpu.VMEM((4,) + g.shape[1:], BF16)] * 2
    outs = pl.pallas_call(
        body, name=name, out_shape=out_shape,
        in_specs=[ANY] * n, out_specs=[vmem] * (2 * n),
        scratch_shapes=scratch + [pltpu.SemaphoreType.DMA((n, 4))] * 3,
        compiler_params=pltpu.CompilerParams(vmem_limit_bytes=VMEM_LIMIT),
    )(*gblocks)
    return [(outs[2 * w], outs[2 * w + 1]) for w in range(n)]


def _ada_weight_grad(c_all, dmod_cols):
    cols = dmod_cols.shape[1]

    def body(c_ref, d_ref, out_ref):
        cf = c_ref[...]
        act = (cf * _sigmoid(cf)).astype(BF16)
        out_ref[...] = lax.dot_general(act, d_ref[...].astype(BF16), TN_DIMS, preferred_element_type=F32)

    return pl.pallas_call(
        body, name="ada_weight_grad",
        out_shape=jax.ShapeDtypeStruct((D_MODEL, cols), F32),
        in_specs=[pl.BlockSpec(memory_space=pltpu.VMEM)] * 2,
        out_specs=pl.BlockSpec(memory_space=pltpu.VMEM),
        compiler_params=pltpu.CompilerParams(vmem_limit_bytes=VMEM_LIMIT),
    )(c_all, dmod_cols)


PACK_ROWS = 24
PACK_DMOD = 0
PACK_PARAMS = {"g_mix": (6, D_MODEL), "b_in": (7, IN_WIDTH), "g_ffn": (14, D_MODEL), "g_final": (15, D_MODEL),
               "sinks": (19, N_Q_HEADS)}
PACK_CONV = 16
PACK_SQERR = 20


def _small_finalize(packed_all, params):
    names = ["b_ada"] + list(PACK_PARAMS)
    layout = dict(PACK_PARAMS, b_ada=(PACK_DMOD, N_MOD * D_MODEL))
    n = len(names)

    def body(*refs):
        p_ref = refs[0]
        ins = refs[1:1 + 3 * n]
        outs = refs[1 + 3 * n:1 + 7 * n]
        conv_ref, loss_ref = refs[1 + 7 * n:]
        total = p_ref[0]
        for d in range(1, N_DEV):
            total = total + p_ref[d]
        for k, name in enumerate(names):
            row0, width = layout[name]
            w_ref, m_ref, v_ref = ins[3 * k:3 * k + 3]
            g_ref, d_ref, nm_ref, nv_ref = outs[4 * k:4 * k + 4]
            for chunk in range(-(-width // D_MODEL)):
                lo = chunk * D_MODEL
                hi = min(lo + D_MODEL, width)
                g = total[row0 + chunk:row0 + chunk + 1, :hi - lo]
                g_ref[:, lo:hi] = g
                d_ref[:, lo:hi], nm_ref[:, lo:hi], nv_ref[:, lo:hi] = _adamw_update(
                    w_ref[:, lo:hi], g, m_ref[:, lo:hi], v_ref[:, lo:hi])
        conv_ref[...] = total[PACK_CONV:PACK_CONV + 3, :]
        loss_ref[...] = (0.5 / D_MODEL) * jnp.sum(total[PACK_SQERR:PACK_SQERR + 1, :], keepdims=True)

    vmem = pl.BlockSpec(memory_space=pltpu.VMEM)
    flat = [a for name in names for a in params[name]]
    out_shape = [jax.ShapeDtypeStruct(params[name][0].shape, F32) for name in names for _ in range(4)]
    outs = pl.pallas_call(
        body, name="small_finalize",
        out_shape=out_shape + [jax.ShapeDtypeStruct((3, D_MODEL), F32), jax.ShapeDtypeStruct((1, 1), F32)],
        in_specs=[vmem] * (1 + 3 * n),
        out_specs=[vmem] * (4 * n + 2),
        compiler_params=pltpu.CompilerParams(vmem_limit_bytes=VMEM_LIMIT),
    )(packed_all, *flat)
    return {name: tuple(outs[4 * k:4 * k + 4]) for k, name in enumerate(names)}, outs[4 * n], outs[4 * n + 1]


def _row_tile(rows, multiple):
    for cand in range(rows // MIN_STREAM_STEPS, 0, -1):
        if rows % cand == 0 and cand % multiple == 0:
            return cand
    return rows


def _adamw_update(w, g, m, v):
    c1 = 1.0 / (1.0 - ADAM_B1 ** ADAM_STEP)
    c2 = 1.0 / (1.0 - ADAM_B2 ** ADAM_STEP)
    nm = ADAM_B1 * m + (1.0 - ADAM_B1) * g
    nv = ADAM_B2 * v + (1.0 - ADAM_B2) * (g * g)
    delta = -ADAM_LR * ((nm * c1) / (jnp.sqrt(nv * c2) + ADAM_EPS) + ADAM_WD * w)
    return delta, nm, nv


def _adamw(w, g, m, v, name):
    rows, cols = w.shape
    tile = _row_tile(rows, SUBLANES)

    def body(w_ref, g_ref, m_ref, v_ref, d_ref, nm_ref, nv_ref):
        d_ref[...], nm_ref[...], nv_ref[...] = _adamw_update(w_ref[...], g_ref[...], m_ref[...], v_ref[...])

    spec = pl.BlockSpec((tile, cols), lambda i: (i, 0))
    outs, _ = _call(body, name, (rows // tile,), [w, g, m, v], [spec] * 4,
                    [jax.ShapeDtypeStruct((rows, cols), F32)] * 3, [spec] * 3)
    return outs


def _sibling_sum(gblocks, sib, name):
    _, r, cdim = gblocks.shape
    tile = _row_tile(r, BF16_ROWS)
    x, y, c = _my_place()
    table = jnp.stack([4 * _flip(x, fx) + 2 * _flip(y, fy) + c for fx, fy in CHIP_FLIPS]).astype(jnp.int32)

    def body(table_ref, own0, own1, own2, own3, sib_ref, sums_ref, mine_ref):
        mine_ref[...] = own0[...].astype(F32) + sib_ref[0].astype(F32)
        for f, own in ((1, own1), (2, own2), (3, own3)):
            sums_ref[f - 1] = (own[...].astype(F32) + sib_ref[f].astype(F32)).astype(BF16)

    own_specs = [pl.BlockSpec((None, tile, cdim), functools.partial(lambda i, tab, f: (tab[f], i, 0), f=f))
                 for f in range(4)]
    return pl.pallas_call(
        body, name=name,
        grid_spec=pltpu.PrefetchScalarGridSpec(
            num_scalar_prefetch=1, grid=(r // tile,),
            in_specs=own_specs + [pl.BlockSpec((4, tile, cdim), lambda i, tab: (0, i, 0))],
            out_specs=[pl.BlockSpec((3, tile, cdim), lambda i, tab: (0, i, 0)),
                       pl.BlockSpec((tile, cdim), lambda i, tab: (i, 0))]),
        out_shape=[jax.ShapeDtypeStruct((3, r, cdim), BF16), jax.ShapeDtypeStruct((r, cdim), F32)],
        compiler_params=pltpu.CompilerParams(dimension_semantics=("arbitrary",), vmem_limit_bytes=VMEM_LIMIT),
    )(table, gblocks, gblocks, gblocks, gblocks, sib)


def _chip_sum_adamw(mine, ici, w, m, v, name):
    r, cdim = mine.shape
    tile = _row_tile(r, BF16_ROWS)

    def body(mine_ref, ici_ref, w_ref, m_ref, v_ref, g_ref, d_ref, nm_ref, nv_ref):
        g = mine_ref[...]
        for f in range(3):
            g = g + ici_ref[f].astype(F32)
        g_ref[...] = g
        d_ref[...], nm_ref[...], nv_ref[...] = _adamw_update(w_ref[...], g, m_ref[...], v_ref[...])

    spec = pl.BlockSpec((tile, cdim), lambda i: (i, 0))
    outs, _ = _call(
        body, name, (r // tile,), [mine, ici, w, m, v],
        [spec, pl.BlockSpec((3, tile, cdim), lambda i: (0, i, 0)), spec, spec, spec],
        [jax.ShapeDtypeStruct((r, cdim), F32)] * 4, [spec] * 4)
    return outs


REF_KV_COL = D_MODEL
REF_REST_COL = D_MODEL + 2 * KV_WIDTH
IN_CHUNK = 1280
IN_PIECES = ([(0, 0, D_MODEL)]
             + [(D_MODEL + n * IN_CHUNK, REF_REST_COL + n * IN_CHUNK, IN_CHUNK) for n in range(REST_WIDTH // IN_CHUNK)]
             + [(KV_COL, REF_KV_COL, 2 * KV_WIDTH)])


def _inproj_fwd(x, vec, w_t, b_in, rider):
    t = x.shape[0]
    tm = min(TOKEN_TILE, t)

    def body(x_ref, vec_ref, w_ref, b_ref, z_ref, h_ref):
        xf = x_ref[...]
        r = lax.rsqrt(jnp.mean(xf * xf, axis=-1, keepdims=True) + EPS)
        h = (xf * r) * (vec_ref[0:1, :] * (1.0 + vec_ref[1:2, :])) + vec_ref[2:3, :]
        hb = h.astype(BF16)
        h_ref[...] = hb
        for mine, ref, width in IN_PIECES:
            zc = lax.dot_general(hb, w_ref[ref:ref + width, :], NT_DIMS, preferred_element_type=F32)
            z_ref[:, mine:mine + width] = (zc + b_ref[:, ref:ref + width]).astype(BF16)

    return _call(
        body, "inproj_fwd", (t // tm,), [x, vec, w_t, b_in],
        [pl.BlockSpec((tm, D_MODEL), lambda i: (i, 0)), _full((SUBLANES, D_MODEL)),
         _full((IN_WIDTH, D_MODEL)), _full((1, IN_WIDTH))],
        [jax.ShapeDtypeStruct((t, IN_WIDTH), BF16), jax.ShapeDtypeStruct((t, D_MODEL), BF16)],
        [pl.BlockSpec((tm, IN_WIDTH), lambda i: (i, 0)), pl.BlockSpec((tm, D_MODEL), lambda i: (i, 0))],
        rider=rider)


PAIRS = GROUP // 2
STACK = PAIRS * WINDOW


ATTN_BLOCKS = 4
ATTN_BWD_BLOCKS = 1
LOG2E = 1.4426950408889634
LN2 = 0.6931471805599453
SCORE_SCALE = ATTN_SCALE * LOG2E


def _fill_window_bias(bias_ref):
    shape = bias_ref.shape[1:]
    kj = lax.broadcasted_iota(jnp.int32, shape, 0)
    qi = jnp.bitwise_and(lax.broadcasted_iota(jnp.int32, shape, 1), WINDOW - 1)
    in_prev = jnp.logical_and(kj < WINDOW, kj > qi)
    in_cur = jnp.logical_and(kj >= WINDOW, (kj - WINDOW) <= qi)
    bias_ref[0] = jnp.where(in_cur, 0.0, -jnp.inf)
    bias_ref[1] = jnp.where(jnp.logical_or(in_prev, in_cur), 0.0, -jnp.inf)


def _half_tiles(tile):
    low = lax.broadcasted_iota(jnp.int32, tile.shape, 1) < HEAD_DIM
    swapped = jnp.concatenate([tile[:, HEAD_DIM:], tile[:, :HEAD_DIM]], axis=1)
    zero = jnp.zeros_like(tile)
    return ((jnp.where(low, tile, zero), jnp.where(low, zero, swapped)),
            (jnp.where(low, swapped, zero), jnp.where(low, zero, tile)))


def _stack_pairs(ref, row0, j):
    return jnp.concatenate(
        [ref[pl.ds(row0, WINDOW), (j * PAIRS + p) * LANES:(j * PAIRS + p + 1) * LANES] for p in range(PAIRS)], axis=0)


def _per_pair_row(values):
    pair = lax.broadcasted_iota(jnp.int32, (1, STACK), 1) // WINDOW
    row = jnp.full((1, STACK), values[PAIRS - 1], F32)
    for p in range(PAIRS - 2, -1, -1):
        row = jnp.where(pair == p, values[p], row)
    return row


def _attn_fwd(z, sinks, rider):
    t = z.shape[0]
    tq = min(TOKEN_TILE, t)
    nblk = tq // WINDOW

    def body(q_ref, kv_ref, sink_ref, o_ref, lse_ref, bias_ref):
        i = pl.program_id(0)

        @pl.when(i == 0)
        def _():
            _fill_window_bias(bias_ref)

        def window(b):
            row0 = pl.multiple_of(b * WINDOW, WINDOW)
            start = i * tq + b * WINDOW
            prev = pl.multiple_of(jnp.maximum(start - WINDOW, 0), WINDOW)
            cur = pl.multiple_of(start, WINDOW)
            kvw = jnp.concatenate([kv_ref[pl.ds(prev, WINDOW), :], kv_ref[pl.ds(cur, WINDOW), :]], axis=0)
            return row0, _half_tiles(kvw[:, :KV_WIDTH]), _half_tiles(kvw[:, KV_WIDTH:]), bias_ref[jnp.minimum(start, 1)]

        def block_group(bb, carry):
            windows = [window(bb * ATTN_BLOCKS + n) for n in range(ATTN_BLOCKS)]
            for j in range(N_KV_HEADS):
                for pr in range(PAIRS):
                    cols = slice((j * PAIRS + pr) * LANES, (j * PAIRS + pr + 1) * LANES)
                    o_ts = [jnp.zeros((LANES, WINDOW), F32) for _ in windows]
                    for parity in range(2):
                        h = j * GROUP + 2 * pr + parity
                        sink = sink_ref[h] * LOG2E
                        for n, (row0, k_halves, v_halves, bias) in enumerate(windows):
                            qp = q_ref[pl.ds(row0, WINDOW), cols]
                            s = lax.dot_general(k_halves[j][parity], qp, NT_DIMS, preferred_element_type=F32)
                            s = s * SCORE_SCALE + bias
                            m = jnp.maximum(jnp.max(s, axis=0, keepdims=True), sink)
                            p = jnp.exp2(s - m)
                            denom = jnp.sum(p, axis=0, keepdims=True) + jnp.exp2(sink - m)
                            pv = lax.dot_general(v_halves[j][parity], p.astype(BF16), TN_DIMS,
                                                 preferred_element_type=F32)
                            o_ts[n] = o_ts[n] + pv * (1.0 / denom)
                            lse_ref[h:h + 1, pl.ds(row0, WINDOW)] = m + jnp.log2(denom)
                    for n, (row0, _, _, _) in enumerate(windows):
                        o_ref[pl.ds(row0, WINDOW), cols] = jnp.transpose(o_ts[n].astype(BF16))
            return carry

        lax.fori_loop(0, nblk // ATTN_BLOCKS, block_group, 0)

    return _call(
        body, "attn_fwd", (t // tq,), [z, z, sinks],
        [pl.BlockSpec((tq, D_MODEL), lambda i: (i, 0)),
         pl.BlockSpec((t, 2 * KV_WIDTH), lambda i: (0, KV_COL // (2 * KV_WIDTH))),
         pl.BlockSpec(memory_space=pltpu.SMEM)],
        [jax.ShapeDtypeStruct((t, D_MODEL), BF16), jax.ShapeDtypeStruct((N_Q_HEADS, t), F32)],
        [pl.BlockSpec((tq, D_MODEL), lambda i: (i, 0)), pl.BlockSpec((N_Q_HEADS, tq), lambda i: (0, i))],
        scratch=[pltpu.VMEM((2, 2 * WINDOW, WINDOW), F32)], rider=rider)


HALO = BF16_ROWS


def _shift_down(u, uh, k):
    rolled = pltpu.roll(u, k, 0)
    row = lax.broadcasted_iota(jnp.int32, (SUBLANES, u.shape[1]), 0)
    top = rolled[:SUBLANES, :]
    for j in range(k):
        top = jnp.where(row == j, uh[HALO - k + j:HALO - k + j + 1, :], top)
    return jnp.concatenate([top, rolled[SUBLANES:, :]], axis=0)


def _shift_up(u, nxt, k):
    n = u.shape[0]
    rolled = pltpu.roll(u, n - k, 0)
    row = lax.broadcasted_iota(jnp.int32, (SUBLANES, u.shape[1]), 0)
    bottom = rolled[n - SUBLANES:, :]
    for j in range(k):
        bottom = jnp.where(row == SUBLANES - k + j, nxt[j:j + 1, :], bottom)
    return jnp.concatenate([rolled[:n - SUBLANES, :], bottom], axis=0)


def _conv_inputs(cc_ref, cx_ref, hc_ref, hx_ref, first_tile):
    cc = cc_ref[...].astype(F32)
    cx = cx_ref[...].astype(F32)
    u = cc * cx
    uh = jnp.where(first_tile, 0.0, hc_ref[...].astype(F32) * hx_ref[...].astype(F32))
    return cc, cx, u, _shift_down(u, uh, 1), _shift_down(u, uh, 2)


def _z_specs(tm, order):
    per_tile = tm // HALO
    cols = [pl.BlockSpec((tm, D_MODEL), functools.partial(lambda i, j: (order(i), j), j=j)) for j in range(1, 6)]
    halos = [pl.BlockSpec((HALO, D_MODEL),
                          functools.partial(lambda i, j: (jnp.maximum(order(i) * per_tile - 1, 0), j), j=j))
             for j in (2, 3)]
    return cols + halos


def _mix_fwd(x, attn, z, vec, w_out):
    t = x.shape[0]
    tm = min(TOKEN_TILE, t)

    def body(x_ref, a_ref, cb_ref, cc_ref, cx_ref, ga_ref, gc_ref, hc_ref, hx_ref, vec_ref, w_ref,
             m_ref, x2_ref, h2_ref, o_ref):
        i = pl.program_id(0)
        _, _, u, u1, u2 = _conv_inputs(cc_ref, cx_ref, hc_ref, hx_ref, i == 0)
        cv = vec_ref[4:5, :] * u2 + vec_ref[5:6, :] * u1 + vec_ref[6:7, :] * u
        conv = cb_ref[...].astype(F32) * cv
        merged = (_sigmoid(ga_ref[...].astype(F32)) * a_ref[...].astype(F32)
                  + _sigmoid(gc_ref[...].astype(F32)) * conv)
        mb = merged.astype(BF16)
        m_ref[...] = mb
        o = jnp.dot(mb, w_ref[...], preferred_element_type=F32)
        o_ref[...] = o.astype(BF16)
        x2 = x_ref[...] + vec_ref[0:1, :] * o
        x2_ref[...] = x2
        r = lax.rsqrt(jnp.mean(x2 * x2, axis=-1, keepdims=True) + EPS)
        h2 = (x2 * r) * (vec_ref[1:2, :] * (1.0 + vec_ref[2:3, :])) + vec_ref[3:4, :]
        h2_ref[...] = h2.astype(BF16)

    tok = pl.BlockSpec((tm, D_MODEL), lambda i: (i, 0))
    outs, _ = _call(
        body, "mix_fwd", (t // tm,), [x, attn, z, z, z, z, z, z, z, vec, w_out],
        [tok, tok] + _z_specs(tm, lambda i: i) + [_full((SUBLANES, D_MODEL)), _full((D_MODEL, D_MODEL))],
        [jax.ShapeDtypeStruct((t, D_MODEL), BF16), jax.ShapeDtypeStruct((t, D_MODEL), F32),
         jax.ShapeDtypeStruct((t, D_MODEL), BF16), jax.ShapeDtypeStruct((t, D_MODEL), BF16)],
        [tok, tok, tok, tok])
    return outs


def _ffn_fwd(h2, w_t):
    t = h2.shape[0]
    tm = min(TOKEN_TILE, t)

    def body(h_ref, w_ref, gu_ref, a_ref):
        hb = h_ref[...]
        for n in range(D_FF // FF_CHUNK):
            lo, hi = n * FF_CHUNK, (n + 1) * FF_CHUNK
            g = lax.dot_general(hb, w_ref[lo:hi, :], NT_DIMS, preferred_element_type=F32)
            u = lax.dot_general(hb, w_ref[D_FF + lo:D_FF + hi, :], NT_DIMS, preferred_element_type=F32)
            sg = _sigmoid(g)
            silu = g * sg
            gu_ref[:, lo:hi] = (u * (sg + silu * (1.0 - sg))).astype(BF16)
            gu_ref[:, D_FF + lo:D_FF + hi] = silu.astype(BF16)
            a_ref[:, lo:hi] = (silu * u).astype(BF16)

    outs, _ = _call(
        body, "ffn_fwd", (t // tm,), [h2, w_t],
        [pl.BlockSpec((tm, D_MODEL), lambda i: (i, 0)), _full((2 * D_FF, D_MODEL))],
        [jax.ShapeDtypeStruct((t, 2 * D_FF), BF16), jax.ShapeDtypeStruct((t, D_FF), BF16)],
        [pl.BlockSpec((tm, 2 * D_FF), lambda i: (i, 0)), pl.BlockSpec((tm, D_FF), lambda i: (i, 0))])
    return outs


def _ffn_out_loss(a, gu, x2, target, vec, w_ffn_out):
    t = a.shape[0]
    tm = min(TOKEN_TILE, t)

    def body(a_ref, gu_ref, x2_ref, t_ref, vec_ref, w_ref, dx3_ref, df_ref, dgu_ref, acc_ref):
        @pl.when(pl.program_id(0) == 0)
        def _():
            acc_ref[...] = jnp.zeros_like(acc_ref)

        ga2 = vec_ref[0:1, :]
        gf = vec_ref[1:2, :]
        parts = min(ROW_PARTS, tm // LANES)
        part_rows = [slice(n * (tm // parts), (n + 1) * (tm // parts)) for n in range(parts)]

        def head(rows, f):
            x3 = x2_ref[rows, :] + ga2 * f
            r = lax.rsqrt(jnp.mean(x3 * x3, axis=-1, keepdims=True) + EPS)
            xn = x3 * r
            err = xn * gf - t_ref[rows, :]
            dxn = err * (gf * (1.0 / D_MODEL))
            dx3 = r * (dxn - xn * jnp.mean(dxn * xn, axis=-1, keepdims=True))
            dx3_ref[rows, :] = dx3.astype(GRAD_STREAM)
            sums = (jnp.sum(err * err, axis=0, keepdims=True),
                    jnp.sum(err * xn, axis=0, keepdims=True) * (1.0 / D_MODEL),
                    jnp.sum(dx3 * f, axis=0, keepdims=True))
            df = (dx3 * ga2).astype(BF16)
            df_ref[rows, :] = df
            return df, sums

        def tail(rows, df):
            for n in range(D_FF // FF_CHUNK):
                lo, hi = n * FF_CHUNK, (n + 1) * FF_CHUNK
                da = lax.dot_general(df, w_ref[lo:hi, :], NT_DIMS, preferred_element_type=F32)
                dgu_ref[rows, lo:hi] = (da * gu_ref[rows, lo:hi].astype(F32)).astype(BF16)
                dgu_ref[rows, D_FF + lo:D_FF + hi] = (da * gu_ref[rows, D_FF + lo:D_FF + hi].astype(F32)).astype(BF16)

        fs = [jnp.dot(a_ref[rows, :], w_ref[...], preferred_element_type=F32) for rows in part_rows]
        heads = [head(rows, f) for rows, f in zip(part_rows, fs)]
        for rows, (df, _) in zip(part_rows, heads):
            tail(rows, df)
        for k in range(3):
            total = heads[0][1][k]
            for _, sums in heads[1:]:
                total = total + sums[k]
            acc_ref[k:k + 1, :] += total

    tok = pl.BlockSpec((tm, D_MODEL), lambda i: (i, 0))
    outs, _ = _call(
        body, "ffn_out_loss", (t // tm,), [a, gu, x2, target, vec, w_ffn_out],
        [pl.BlockSpec((tm, D_FF), lambda i: (i, 0)), pl.BlockSpec((tm, 2 * D_FF), lambda i: (i, 0)),
         tok, tok, _full((SUBLANES, D_MODEL)), _full((D_FF, D_MODEL))],
        [jax.ShapeDtypeStruct((t, D_MODEL), GRAD_STREAM), jax.ShapeDtypeStruct((t, D_MODEL), BF16),
         jax.ShapeDtypeStruct((t, 2 * D_FF), BF16), jax.ShapeDtypeStruct((SUBLANES, D_MODEL), F32)],
        [tok, tok, pl.BlockSpec((tm, 2 * D_FF), lambda i: (i, 0)), _full((SUBLANES, D_MODEL))])
    return outs


def _ffn_in_bwd(dgu, x2, dx3, vec, w_t, rider):
    t = x2.shape[0]
    tm = min(TOKEN_TILE, t)

    def body(dgu_ref, x2_ref, dx3_ref, vec_ref, wf_ref, dx2_ref, acc_ref):
        @pl.when(pl.program_id(0) == 0)
        def _():
            acc_ref[...] = jnp.zeros_like(acc_ref)

        gffn = vec_ref[0:1, :]
        sc2 = vec_ref[1:2, :]
        parts = min(ROW_PARTS, tm // LANES)
        part_rows = [slice(n * (tm // parts), (n + 1) * (tm // parts)) for n in range(parts)]
        dhs = [jnp.dot(dgu_ref[rows, :], wf_ref[...], preferred_element_type=F32) for rows in part_rows]
        gs = gffn * (1.0 + sc2)
        sum_dh = jnp.zeros((1, D_MODEL), F32)
        sum_dh_xn = jnp.zeros((1, D_MODEL), F32)
        for rows, dh2 in zip(part_rows, dhs):
            x2 = x2_ref[rows, :]
            r = lax.rsqrt(jnp.mean(x2 * x2, axis=-1, keepdims=True) + EPS)
            xn = x2 * r
            dh_xn = dh2 * xn
            sum_dh = sum_dh + jnp.sum(dh2, axis=0, keepdims=True)
            sum_dh_xn = sum_dh_xn + jnp.sum(dh_xn, axis=0, keepdims=True)
            dx2 = dx3_ref[rows, :].astype(F32) + r * (dh2 * gs - xn * jnp.mean(dh_xn * gs, axis=-1, keepdims=True))
            dx2_ref[rows, :] = dx2.astype(GRAD_STREAM)
        acc_ref[0:1, :] += sum_dh
        acc_ref[1:2, :] += sum_dh_xn * gffn
        acc_ref[2:3, :] += sum_dh_xn * (1.0 + sc2)

    tok = pl.BlockSpec((tm, D_MODEL), lambda i: (i, 0))
    return _call(
        body, "ffn_in_bwd", (t // tm,), [dgu, x2, dx3, vec, w_t],
        [pl.BlockSpec((tm, 2 * D_FF), lambda i: (i, 0)), tok, tok, _full((SUBLANES, D_MODEL)),
         _full((2 * D_FF, D_MODEL))],
        [jax.ShapeDtypeStruct((t, D_MODEL), GRAD_STREAM), jax.ShapeDtypeStruct((SUBLANES, D_MODEL), F32)],
        [tok, _full((SUBLANES, D_MODEL))], rider=rider)


def _mix_bwd(dx2, oproj, attn, z, vec, w_out, rider):
    t = dx2.shape[0]
    tm = min(TOKEN_TILE, t)
    nt = t // tm
    rev = lambda i: nt - 1 - i

    def body(dx2_ref, m_ref, a_ref, cb_ref, cc_ref, cx_ref, ga_ref, gc_ref, hc_ref, hx_ref,
             vec_ref, wo_ref, do_ref, da_ref, dr_ref, acc_ref, carry_ref):
        i = pl.program_id(0)

        @pl.when(i == 0)
        def _():
            acc_ref[...] = jnp.zeros_like(acc_ref)
            carry_ref[...] = jnp.zeros_like(carry_ref)

        ga1 = vec_ref[0:1, :]
        w0, w1, w2 = vec_ref[1:2, :], vec_ref[2:3, :], vec_ref[3:4, :]
        dx2 = dx2_ref[...].astype(F32)
        acc_ref[0:1, :] += jnp.sum(dx2 * m_ref[...].astype(F32), axis=0, keepdims=True)
        do = (dx2 * ga1).astype(BF16)
        do_ref[...] = do
        dm = lax.dot_general(do, wo_ref[...], NT_DIMS, preferred_element_type=F32)

        cc, cx, u, u1, u2 = _conv_inputs(cc_ref, cx_ref, hc_ref, hx_ref, i == nt - 1)
        cv = w0 * u2 + w1 * u1 + w2 * u
        cb = cb_ref[...].astype(F32)
        sa = _sigmoid(ga_ref[...].astype(F32))
        sc = _sigmoid(gc_ref[...].astype(F32))
        attn = a_ref[...].astype(F32)
        dattn = dm * sa
        da_ref[...] = dattn.astype(BF16)
        dconv = dm * sc
        dconv_b = dconv * cv
        dr_ref[:, 3 * D_MODEL:4 * D_MODEL] = (dattn * attn * (1.0 - sa)).astype(BF16)
        dr_ref[:, 4 * D_MODEL:5 * D_MODEL] = (dconv_b * cb * (1.0 - sc)).astype(BF16)
        dr_ref[:, 0:D_MODEL] = dconv_b.astype(BF16)
        dcv = dconv * cb
        acc_ref[1:2, :] += jnp.sum(dcv * u2, axis=0, keepdims=True)
        acc_ref[2:3, :] += jnp.sum(dcv * u1, axis=0, keepdims=True)
        acc_ref[3:4, :] += jnp.sum(dcv * u, axis=0, keepdims=True)
        nxt = carry_ref[...]
        du = w2 * dcv + w1 * _shift_up(dcv, nxt, 1) + w0 * _shift_up(dcv, nxt, 2)
        carry_ref[...] = dcv[0:SUBLANES, :]
        dr_ref[:, D_MODEL:2 * D_MODEL] = (du * cx).astype(BF16)
        dr_ref[:, 2 * D_MODEL:3 * D_MODEL] = (du * cc).astype(BF16)

    tok = pl.BlockSpec((tm, D_MODEL), lambda i: (rev(i), 0))
    return _call(
        body, "mix_bwd", (nt,), [dx2, oproj, attn, z, z, z, z, z, z, z, vec, w_out],
        [tok, tok, tok] + _z_specs(tm, rev) + [_full((SUBLANES, D_MODEL)), _full((D_MODEL, D_MODEL))],
        [jax.ShapeDtypeStruct((t, D_MODEL), BF16), jax.ShapeDtypeStruct((t, D_MODEL), BF16),
         jax.ShapeDtypeStruct((t, REST_WIDTH), BF16), jax.ShapeDtypeStruct((SUBLANES, D_MODEL), F32)],
        [tok, tok, pl.BlockSpec((tm, REST_WIDTH), lambda i: (rev(i), 0)), _full((SUBLANES, D_MODEL))],
        scratch=[pltpu.VMEM((SUBLANES, D_MODEL), F32)], rider=rider)


def _attn_bwd(z, dattn, attn, lse, sinks, rider):
    t = z.shape[0]
    tq = min(TOKEN_TILE, t)
    nblk = tq // WINDOW
    nt = t // tq

    def body(q_ref, kv_ref, do_ref, o_ref, lse_ref, sink_ref, dq_ref, dkv_ref, ds_ref, acc_ref, bias_ref):
        i = pl.program_id(0)

        @pl.when(i == 0)
        def _():
            acc_ref[...] = jnp.zeros_like(acc_ref)
            ds_ref[...] = jnp.zeros_like(ds_ref)
            _fill_window_bias(bias_ref)

        lane = lax.broadcasted_iota(jnp.int32, (1, LANES), 1)
        ind_row = lax.broadcasted_iota(jnp.int32, (SUBLANES, LANES), 0)
        ind_low = lax.broadcasted_iota(jnp.int32, (SUBLANES, LANES), 1) < HEAD_DIM
        indicator = jnp.where(jnp.logical_or(jnp.logical_and(ind_row == 0, ind_low),
                                             jnp.logical_and(ind_row == 1, jnp.logical_not(ind_low))),
                              1.0, 0.0).astype(BF16)
        low = lax.broadcasted_iota(jnp.int32, (2 * WINDOW, LANES), 1) < HEAD_DIM

        def both_heads(even, odd):
            picked = jnp.where(low, even, odd)
            return picked + jnp.concatenate([picked[:, HEAD_DIM:], picked[:, :HEAD_DIM]], axis=1)

        def window(b):
            row0 = pl.multiple_of(b * WINDOW, WINDOW)
            start = i * tq + b * WINDOW
            prev = pl.multiple_of(jnp.maximum(start - WINDOW, 0), WINDOW)
            cur = pl.multiple_of(start, WINDOW)
            kvw = jnp.concatenate([kv_ref[pl.ds(prev, WINDOW), :], kv_ref[pl.ds(cur, WINDOW), :]], axis=0)
            return (row0, prev, cur, _half_tiles(kvw[:, :KV_WIDTH]), _half_tiles(kvw[:, KV_WIDTH:]),
                    bias_ref[jnp.minimum(start, 1)])

        def block_group(bb, dsink):
            windows = [window(bb * ATTN_BWD_BLOCKS + n) for n in range(ATTN_BWD_BLOCKS)]
            dk_groups = [[] for _ in windows]
            dv_groups = [[] for _ in windows]
            for j in range(N_KV_HEADS):
                stacks, deltas, dq_ts = [], [], []
                for row0, _, _, _, _, _ in windows:
                    qst = _stack_pairs(q_ref, row0, j)
                    dost = _stack_pairs(do_ref, row0, j)
                    prod = dost.astype(F32) * _stack_pairs(o_ref, row0, j).astype(F32)
                    prod_hi = prod.astype(BF16)
                    prod_lo = (prod - prod_hi.astype(F32)).astype(BF16)
                    stacks.append((qst, dost))
                    deltas.append(lax.dot_general(indicator, prod_hi, NT_DIMS, preferred_element_type=F32)
                                  + lax.dot_general(indicator, prod_lo, NT_DIMS, preferred_element_type=F32))
                    dq_ts.append(jnp.zeros((LANES, STACK), F32))
                dk_par = [[] for _ in windows]
                dv_par = [[] for _ in windows]
                for parity in range(2):
                    heads = [j * GROUP + 2 * p + parity for p in range(PAIRS)]
                    sink = _per_pair_row([sink_ref[h] * LOG2E for h in heads])
                    for n, (row0, _, _, k_halves, v_halves, bias) in enumerate(windows):
                        qst, dost = stacks[n]
                        kk, vv = k_halves[j][parity], v_halves[j][parity]
                        s = lax.dot_general(kk, qst, NT_DIMS, preferred_element_type=F32) * SCORE_SCALE + bias
                        lse = jnp.concatenate([lse_ref[h:h + 1, pl.ds(row0, WINDOW)] for h in heads], axis=1)
                        p = jnp.exp2(s - lse)
                        dp = lax.dot_general(vv, dost, NT_DIMS, preferred_element_type=F32)
                        delta = deltas[n][parity:parity + 1, :]
                        dsb = (p * (dp - delta)).astype(BF16)
                        dq_ts[n] = dq_ts[n] + lax.dot_general(kk, dsb, TN_DIMS, preferred_element_type=F32)
                        dk_par[n].append(jnp.dot(dsb, qst, preferred_element_type=F32))
                        dv_par[n].append(jnp.dot(p.astype(BF16), dost, preferred_element_type=F32))
                        weighted = jnp.exp2(sink - lse) * delta
                        for pr, h in enumerate(heads):
                            dsink = dsink - jnp.where(
                                lane == h, jnp.sum(weighted[:, pr * WINDOW:(pr + 1) * WINDOW]), 0.0)
                for n, (row0, _, _, _, _, _) in enumerate(windows):
                    dq_st = jnp.transpose((dq_ts[n] * ATTN_SCALE).astype(BF16))
                    for pr in range(PAIRS):
                        dq_ref[pl.ds(row0, WINDOW), (j * PAIRS + pr) * LANES:(j * PAIRS + pr + 1) * LANES] = (
                            dq_st[pr * WINDOW:(pr + 1) * WINDOW, :])
                    dk_groups[n].append(both_heads(dk_par[n][0], dk_par[n][1]))
                    dv_groups[n].append(both_heads(dv_par[n][0], dv_par[n][1]))
            for n, (_, prev, cur, _, _, _) in enumerate(windows):
                blk = jnp.concatenate([jnp.where(low, dk_groups[n][0], dk_groups[n][1]) * ATTN_SCALE,
                                       jnp.where(low, dv_groups[n][0], dv_groups[n][1])], axis=1)
                acc_ref[pl.ds(prev, WINDOW), :] += blk[:WINDOW, :]
                acc_ref[pl.ds(cur, WINDOW), :] += blk[WINDOW:, :]
            return dsink

        dsink = lax.fori_loop(0, nblk // ATTN_BWD_BLOCKS, block_group, jnp.zeros((1, LANES), F32))
        ds_ref[0:1, :] += dsink

        @pl.when(i == nt - 1)
        def _():
            dkv_ref[...] = acc_ref[...].astype(BF16)

    tok = pl.BlockSpec((tq, D_MODEL), lambda i: (i, 0))
    return _call(
        body, "attn_bwd", (nt,), [z, z, dattn, attn, lse, sinks],
        [tok, pl.BlockSpec((t, 2 * KV_WIDTH), lambda i: (0, KV_COL // (2 * KV_WIDTH))), tok, tok,
         pl.BlockSpec((N_Q_HEADS, tq), lambda i: (0, i)), pl.BlockSpec(memory_space=pltpu.SMEM)],
        [jax.ShapeDtypeStruct((t, D_MODEL), BF16), jax.ShapeDtypeStruct((t, 2 * KV_WIDTH), BF16),
         jax.ShapeDtypeStruct((SUBLANES, LANES), F32)],
        [tok, _full((t, 2 * KV_WIDTH)), _full((SUBLANES, LANES))],
        scratch=[pltpu.VMEM((t, 2 * KV_WIDTH), F32), pltpu.VMEM((2, 2 * WINDOW, STACK), F32)], rider=rider)


def _inproj_bwd(dq, drest, dkv, x, dx2, vec, w_t, rider):
    t = x.shape[0]
    tm = min(TOKEN_TILE, t)

    def body(dq_ref, dr_ref, dkv_ref, x_ref, dx2_ref, vec_ref, w_ref, gx_ref, acc_ref, db_ref):
        @pl.when(pl.program_id(0) == 0)
        def _():
            acc_ref[...] = jnp.zeros_like(acc_ref)
            db_ref[...] = jnp.zeros_like(db_ref)

        g = vec_ref[0:1, :]
        sc1 = vec_ref[1:2, :]
        dqb, drb, dkvb = dq_ref[...], dr_ref[...], dkv_ref[...]
        dh = jnp.dot(dqb, w_ref[:REF_KV_COL, :], preferred_element_type=F32)
        dh = dh + jnp.dot(drb, w_ref[REF_REST_COL:, :], preferred_element_type=F32)
        dh = dh + jnp.dot(dkvb, w_ref[REF_KV_COL:REF_REST_COL, :], preferred_element_type=F32)
        db_ref[:, :REF_KV_COL] += jnp.sum(dqb.astype(F32), axis=0, keepdims=True)
        db_ref[:, REF_REST_COL:] += jnp.sum(drb.astype(F32), axis=0, keepdims=True)
        db_ref[:, REF_KV_COL:REF_REST_COL] += jnp.sum(dkvb.astype(F32), axis=0, keepdims=True)
        xf = x_ref[...]
        r = lax.rsqrt(jnp.mean(xf * xf, axis=-1, keepdims=True) + EPS)
        xn = xf * r
        gs = g * (1.0 + sc1)
        dh_xn = dh * xn
        sum_dh_xn = jnp.sum(dh_xn, axis=0, keepdims=True)
        acc_ref[0:1, :] += jnp.sum(dh, axis=0, keepdims=True)
        acc_ref[1:2, :] += sum_dh_xn * g
        acc_ref[2:3, :] += sum_dh_xn * (1.0 + sc1)
        gx_ref[...] = dx2_ref[...].astype(F32) + r * (dh * gs - xn * jnp.mean(dh_xn * gs, axis=-1, keepdims=True))

    tok = pl.BlockSpec((tm, D_MODEL), lambda i: (i, 0))
    return _call(
        body, "inproj_bwd", (t // tm,), [dq, drest, dkv, x, dx2, vec, w_t],
        [tok, pl.BlockSpec((tm, REST_WIDTH), lambda i: (i, 0)),
         pl.BlockSpec((tm, 2 * KV_WIDTH), lambda i: (i, 0)), tok, tok,
         _full((SUBLANES, D_MODEL)), _full((IN_WIDTH, D_MODEL))],
        [jax.ShapeDtypeStruct((t, D_MODEL), F32), jax.ShapeDtypeStruct((SUBLANES, D_MODEL), F32),
         jax.ShapeDtypeStruct((1, IN_WIDTH), F32)],
        [tok, _full((SUBLANES, D_MODEL)), _full((1, IN_WIDTH))], rider=rider)


def _weight_grad(b, a, name, bn, rows=None, row0=0, into=None, rider=None):
    pieces = list(b) if isinstance(b, (list, tuple)) else [b]
    widths = [p.shape[1] for p in pieces]
    t, n = pieces[0].shape[0], sum(widths)
    assert len(pieces) == 1 or bn == n
    m = a.shape[1]
    rows = n if rows is None else rows
    tk = min(TOKEN_TILE, t)
    for cand in (4 * TOKEN_TILE, 2 * TOKEN_TILE):
        if t % cand == 0 and 2 * cand * (bn + m) * 2 + bn * m * 4 <= WGRAD_VMEM:
            tk = cand
            break
    nk = t // tk
    block0 = row0 // bn

    npieces = len(pieces)

    def body(*refs):
        b_refs, a_ref = refs[:npieces], refs[npieces]
        out_ref, acc_ref = refs[-2:]
        k = pl.program_id(1)

        @pl.when(k == 0)
        def _():
            acc_ref[...] = jnp.zeros_like(acc_ref)

        if npieces == 1:
            acc_ref[...] += lax.dot_general(b_refs[0][...], a_ref[...], TN_DIMS, preferred_element_type=F32)
        else:
            lo = 0
            for b_ref, width in zip(b_refs, widths):
                acc_ref[lo:lo + width, :] += lax.dot_general(b_ref[...], a_ref[...], TN_DIMS,
                                                             preferred_element_type=F32)
                lo += width

        @pl.when(k == nk - 1)
        def _():
            out_ref[...] = acc_ref[...].astype(BF16)

    if npieces == 1:
        b_specs = [pl.BlockSpec((tk, bn), lambda j, k: (k, j))]
    else:
        b_specs = [pl.BlockSpec((tk, width), lambda j, k: (k, 0)) for width in widths]
    outs, routs = _call(
        body, name, (n // bn, nk), pieces + [a] + ([] if into is None else [into]),
        b_specs + [pl.BlockSpec((tk, m), lambda j, k: (k, 0))] + ([] if into is None else [ANY]),
        [jax.ShapeDtypeStruct((rows, m), BF16)], [pl.BlockSpec((bn, m), lambda j, k: (block0 + j, 0))],
        scratch=[pltpu.VMEM((bn, m), F32)], rider=rider, aliases=None if into is None else {npieces + 1: 0})
    return outs[0], routs


def _weight_grad_in(dq, dkv, drest, h1):
    t = h1.shape[0]
    tk = min(4 * TOKEN_TILE, t)
    nk = t // tk
    nblocks = IN_WIDTH // IN_CHUNK

    def body(dq_ref, dkv_ref, dr_ref, a_ref, out_ref, acc_ref):
        j, k = pl.program_id(0), pl.program_id(1)

        @pl.when(k == 0)
        def _():
            acc_ref[...] = jnp.zeros_like(acc_ref)

        @pl.when(j == 0)
        def _():
            acc_ref[:D_MODEL, :] += lax.dot_general(dq_ref[...], a_ref[...], TN_DIMS, preferred_element_type=F32)
            acc_ref[D_MODEL:, :] += lax.dot_general(dkv_ref[...], a_ref[...], TN_DIMS, preferred_element_type=F32)

        @pl.when(j > 0)
        def _():
            acc_ref[...] += lax.dot_general(dr_ref[...], a_ref[...], TN_DIMS, preferred_element_type=F32)

        @pl.when(k == nk - 1)
        def _():
            out_ref[...] = acc_ref[...].astype(BF16)

    first = lambda j, k: (jnp.where(j == 0, k, 0), 0)
    outs, _ = _call(
        body, "wgrad_in", (nblocks, nk), [dq, dkv, drest, h1],
        [pl.BlockSpec((tk, D_MODEL), first), pl.BlockSpec((tk, 2 * KV_WIDTH), first),
         pl.BlockSpec((tk, IN_CHUNK), lambda j, k: (jnp.where(j == 0, 0, k), jnp.maximum(j - 1, 0))),
         pl.BlockSpec((tk, D_MODEL), lambda j, k: (k, 0))],
        [jax.ShapeDtypeStruct((IN_WIDTH, D_MODEL), BF16)], [pl.BlockSpec((IN_CHUNK, D_MODEL), lambda j, k: (j, 0))],
        scratch=[pltpu.VMEM((IN_CHUNK, D_MODEL), F32)])
    return outs[0]


def _to_rows(v):
    n = v.shape[0]
    padded = -(-n // (SUBLANES * LANES)) * SUBLANES * LANES
    return jnp.pad(v, (0, padded - n)).reshape(padded // LANES, LANES)


def _vec_rows(*rows):
    stacked = jnp.concatenate([r.reshape(1, D_MODEL) for r in rows], axis=0)
    return jnp.pad(stacked, ((0, SUBLANES - len(rows)), (0, 0)))


def kernel(x, c, w_ada, b_ada, g_mix, w_in, b_in, sinks, conv_w, w_out, g_ffn, w_ffn_in, w_ffn_out, g_final, loss_target, m_w_ada, m_b_ada, m_g_mix, m_w_in, m_b_in, m_sinks, m_conv_w, m_w_out, m_g_ffn, m_w_ffn_in, m_w_ffn_out, m_g_final, v_w_ada, v_b_ada, v_g_mix, v_w_in, v_b_in, v_sinks, v_conv_w, v_w_out, v_g_ffn, v_w_ffn_in, v_w_ffn_out, v_g_final):
    ix, iy, ic = _my_place()
    me = 4 * ix + 2 * iy + ic
    xs = x[0]
    target = loss_target[0]
    ada_cols = w_ada.shape[2]
    conv_cols = conv_w.shape[2]

    wt_in, wt_fi = jnp.transpose(w_in[0]), jnp.transpose(w_ffn_in[0])
    b_cols = lax.dynamic_slice_in_dim(b_ada, me * ada_cols, ada_cols, axis=1)
    g_in, (cast_fi, cast_out, cast_fo), first, mod_all = _gather_first_weight(
        wt_in, [wt_fi, w_out[0], w_ffn_out[0]], _to_rows(jnp.concatenate([c[0], conv_w[0].reshape(-1)])),
        w_ada[0], b_cols)
    first = first.reshape(N_DEV, -1)
    c_all = first[:, :D_MODEL]
    conv_full = jnp.transpose(first[:, D_MODEL:D_MODEL + 3 * conv_cols].reshape(N_DEV, 3, conv_cols), (1, 0, 2))
    conv_full = conv_full.reshape(3, D_MODEL)
    mod = lax.dynamic_index_in_dim(mod_all, me, axis=1, keepdims=False).reshape(N_MOD, D_MODEL)
    sh1, sc1, ga1, sh2, sc2, ga2 = [mod[i:i + 1] for i in range(N_MOD)]
    w_in_t = g_in.reshape(IN_WIDTH, D_MODEL)
    (z, h1), (g_fi, g_out) = _inproj_fwd(xs, _vec_rows(g_mix, sc1, sh1), w_in_t, b_in,
                                         _gather_rider([cast_fi, cast_out]))
    w_fi_t = g_fi.reshape(2 * D_FF, D_MODEL)
    w_out_full = g_out.reshape(D_MODEL, D_MODEL)
    (attn, lse), (g_fo,) = _attn_fwd(z, sinks[0], _gather_rider([cast_fo]))
    w_fo_full = g_fo.reshape(D_FF, D_MODEL)
    merged, x2, h2, oproj = _mix_fwd(
        xs, attn, z, _vec_rows(ga1, g_ffn, sc2, sh2, conv_full[0], conv_full[1], conv_full[2]), w_out_full)
    gu, act = _ffn_fwd(h2, w_fi_t)
    dx3, df, dgu, acc_l = _ffn_out_loss(act, gu, x2, target, _vec_rows(ga2, g_final), w_fo_full)

    gw_fo, _ = _weight_grad(act, df, "wgrad_ffn_out", D_FF)
    gw_fi, _ = _weight_grad(dgu, h2, "wgrad_ffn_in", D_FF)
    blocks_fo = gw_fo.reshape(N_DEV, D_FF // N_DEV, D_MODEL)
    blocks_fi = gw_fi.reshape(N_DEV, 2 * D_FF // N_DEV, D_MODEL)
    (dx2, acc_f), (sib_fo, sib_fi) = _ffn_in_bwd(dgu, x2, dx3, _vec_rows(g_ffn, sc2), w_fi_t,
                                                 _sibling_rider([blocks_fo, blocks_fi]))
    sums_fo, mine_fo = _sibling_sum(blocks_fo, sib_fo, "sibling_sum_ffn_out")
    sums_fi, mine_fi = _sibling_sum(blocks_fi, sib_fi, "sibling_sum_ffn_in")
    (dout, dattn, drest, acc_m), (ici_fo, ici_fi) = _mix_bwd(
        dx2, oproj, attn, z, _vec_rows(ga1, conv_full[0], conv_full[1], conv_full[2]), w_out_full,
        _chip_rider([sums_fo, sums_fi]))
    gw_out, _ = _weight_grad(merged, dout, "wgrad_out", D_MODEL)
    blocks_out = gw_out.reshape(N_DEV, D_MODEL // N_DEV, D_MODEL)
    (dq, dkv, dsink), _ = _attn_bwd(z, dattn, attn, lse, sinks[0], None)
    blocks_in = _weight_grad_in(dq, dkv, drest, h1).reshape(N_DEV, IN_WIDTH // N_DEV, D_MODEL)
    (sums_in, mine_in), (sums_out, mine_out) = _sibling_exchange_sum([blocks_in, blocks_out], "sibling_w_in_out")
    (grad_x, acc_i, db_in), (ici_in, ici_out) = _inproj_bwd(dq, drest, dkv, xs, dx2, _vec_rows(g_mix, sc1), w_in_t,
                                                            _chip_rider([sums_in, sums_out]))

    widen = lambda vec: jnp.pad(vec, (0, -vec.shape[0] % D_MODEL))
    packed = jnp.concatenate([
        acc_i[0], acc_i[1], acc_m[0], acc_f[0], acc_f[1], acc_l[2],
        acc_i[2], widen(db_in[0]), acc_f[2], acc_l[1],
        acc_m[1], acc_m[2], acc_m[3], widen(dsink[0]), acc_l[0],
        jnp.zeros(((PACK_ROWS - PACK_SQERR - 1) * D_MODEL,), F32)]).reshape(PACK_ROWS, D_MODEL)
    packed_all = _small_allgather(packed, "gather_small")
    dmod_all = packed_all[:, PACK_DMOD:PACK_DMOD + N_MOD, :].reshape(N_DEV, N_MOD * D_MODEL)
    dmod_cols = lax.dynamic_slice_in_dim(dmod_all, me * ada_cols, ada_cols, axis=1)
    g_w_ada = _ada_weight_grad(c_all, dmod_cols)
    row_of = lambda a: a.reshape(1, -1)
    small, g_conv_full, loss = _small_finalize(packed_all, {
        "b_ada": (b_ada, m_b_ada, v_b_ada), "g_mix": (g_mix, m_g_mix, v_g_mix), "b_in": (b_in, m_b_in, v_b_in),
        "g_ffn": (g_ffn, m_g_ffn, v_g_ffn), "sinks": (sinks, m_sinks, v_sinks),
        "g_final": (row_of(g_final), row_of(m_g_final), row_of(v_g_final))})
    small["g_final"] = tuple(o.reshape(g_final.shape) for o in small["g_final"])
    g_conv = lax.dynamic_slice_in_dim(g_conv_full, me * conv_cols, conv_cols, axis=1)
    d_conv, nm_conv, nv_conv = _adamw(conv_w[0], g_conv, m_conv_w[0], v_conv_w[0], "adamw_conv_w")
    small["conv_w"] = (g_conv[None], d_conv[None], nm_conv[None], nv_conv[None])

    def reduced(mine, ici, w, m, v, name, transposed=False):
        turn = jnp.transpose if transposed else (lambda a: a)
        return tuple(turn(o)[None] for o in _chip_sum_adamw(mine, ici, turn(w[0]), turn(m[0]), turn(v[0]), name))

    d_ada, nm_ada, nv_ada = _adamw(w_ada[0], g_w_ada, m_w_ada[0], v_w_ada[0], "adamw_w_ada")
    res = {
        "w_ada": (g_w_ada[None], d_ada[None], nm_ada[None], nv_ada[None]),
        "w_in": reduced(mine_in, ici_in, w_in, m_w_in, v_w_in, "adamw_w_in", transposed=True),
        "w_out": reduced(mine_out, ici_out, w_out, m_w_out, v_w_out, "adamw_w_out"),
        "w_ffn_in": reduced(mine_fi, ici_fi, w_ffn_in, m_w_ffn_in, v_w_ffn_in, "adamw_w_ffn_in", transposed=True),
        "w_ffn_out": reduced(mine_fo, ici_fo, w_ffn_out, m_w_ffn_out, v_w_ffn_out, "adamw_w_ffn_out"),
    }
    res.update(small)
    order = ["w_ada", "b_ada", "g_mix", "w_in", "b_in", "sinks", "conv_w", "w_out", "g_ffn", "w_ffn_in", "w_ffn_out",
             "g_final"]
    outs = [loss.reshape(()), grad_x[None]]
    for k in range(4):
        outs += [res[n][k] for n in order]
    return tuple(outs)
```

```python
import functools
import math

import jax
import jax.numpy as jnp
from jax import lax
from jax.experimental import pallas as pl
from jax.experimental.pallas import tpu as pltpu

F32 = jnp.float32
BF16 = jnp.bfloat16
GRAD_STREAM = F32

D_MODEL = 1024
HEAD_DIM = 64
N_Q_HEADS = 16
N_KV_HEADS = 2
GROUP = 8
WINDOW = 128
KV_WIDTH = N_KV_HEADS * HEAD_DIM
D_FF = 2816
IN_WIDTH = 6400
N_MOD = 6
EPS = 1e-6
N_DEV = 8
REST_WIDTH = 5 * D_MODEL
KV_COL = D_MODEL + REST_WIDTH
ATTN_SCALE = HEAD_DIM ** -0.5

ADAM_LR = 0.001
ADAM_B1 = 0.9
ADAM_B2 = 0.999
ADAM_EPS = 1e-08
ADAM_WD = 0.01
ADAM_STEP = 10

LANES = 128
SUBLANES = 8
BF16_ROWS = 16
VMEM_LIMIT = 56 * 1024 * 1024
TOKEN_TILE = 512
FF_CHUNK = 256
ROW_PARTS = 2
MIN_STREAM_STEPS = 2
WGRAD_VMEM = 40 * 1024 * 1024
MESH = pl.DeviceIdType.MESH
ANY = pl.BlockSpec(memory_space=pl.ANY)

NT_DIMS = (((1,), (1,)), ((), ()))
TN_DIMS = (((0,), (0,)), ((), ()))
CHIP_FLIPS = [(0, 0), (1, 0), (0, 1), (1, 1)]


def _full(shape):
    return pl.BlockSpec(shape, lambda *_: (0,) * len(shape))


def _my_place():
    return lax.axis_index("x"), lax.axis_index("y"), lax.axis_index("c")


def _flip(v, bit):
    return 1 - v if bit else v


def _sigmoid(v):
    return 1.0 / (1.0 + jnp.exp2(v * (-1.4426950408889634)))


class _Rider:
    def __init__(self, ins, out_shapes, sem_shapes, first=None, mid=None, last=None, ins_in_vmem=False):
        self.ins, self.out_shapes, self.sem_shapes = list(ins), list(out_shapes), list(sem_shapes)
        self.in_specs = [_full(a.shape) if ins_in_vmem else ANY for a in self.ins]
        self.hooks = [(when, fn) for when, fn in (("first", first), ("mid", mid), ("last", last)) if fn is not None]


def _call(body, name, grid, args, in_specs, out_shape, out_specs, scratch=(), rider=None, aliases=None):
    n_in, n_out, n_scr = len(args), len(out_shape), len(scratch)
    r_in = rider.ins if rider else []
    r_out = rider.out_shapes if rider else []
    r_sem = rider.sem_shapes if rider else []
    nsteps = math.prod(grid)

    def full_body(*refs):
        pos = 0
        groups = []
        for size in (n_in, len(r_in), n_out, len(r_out), n_scr, len(r_sem)):
            groups.append(refs[pos:pos + size])
            pos += size
        ins, rins, outs, routs, scr, rsems = groups
        step = pl.program_id(0)
        for axis in range(1, len(grid)):
            step = step * grid[axis] + pl.program_id(axis)
        at = {"first": 0, "mid": (3 * nsteps) // 4, "last": nsteps - 1}
        hooks = rider.hooks if rider else []
        for when, fn in hooks:
            if when != "last":
                pl.when(step == at[when])(functools.partial(fn, rins, routs, rsems))
        body(*ins, *outs, *scr)
        for when, fn in hooks:
            if when == "last":
                pl.when(step == at[when])(functools.partial(fn, rins, routs, rsems))

    outs = pl.pallas_call(
        full_body, name=name, grid=grid,
        out_shape=list(out_shape) + list(r_out),
        in_specs=list(in_specs) + (rider.in_specs if rider else []),
        out_specs=list(out_specs) + [ANY] * len(r_out),
        scratch_shapes=list(scratch) + list(r_sem),
        input_output_aliases=dict(aliases or {}),
        compiler_params=pltpu.CompilerParams(dimension_semantics=("arbitrary",) * len(grid),
                                             vmem_limit_bytes=VMEM_LIMIT),
    )(*args, *r_in)
    return list(outs[:n_out]), list(outs[n_out:])


def _gather_rider(shards):
    n = len(shards)

    def setup(outs, sems):
        x, y, c = _my_place()
        send_sems, recv_sems, _ = sems
        chips = [(1 - x, y), (x, 1 - y), (1 - x, 1 - y)]

        def block(w, place):
            return outs[w].at[4 * place[0] + 2 * place[1] + place[2]]

        def copy(w, k, place, to, src=None):
            return pltpu.make_async_remote_copy(
                src_ref=block(w, place) if src is None else src, dst_ref=block(w, place),
                send_sem=send_sems.at[w, k], recv_sem=recv_sems.at[w, k], device_id=to, device_id_type=MESH)

        return (x, y, c), (x, y, 1 - c), chips, block, copy

    def first(ins, outs, sems):
        me, sibling, chips, block, copy = setup(outs, sems)
        for w in range(n):
            pltpu.make_async_copy(ins[w], block(w, me), sems[2].at[w]).start()
            copy(w, 0, me, sibling, src=ins[w]).start()
            for j, chip in enumerate(chips):
                copy(w, 1 + j, me, (*chip, me[2]), src=ins[w]).start()

    def mid(ins, outs, sems):
        me, sibling, chips, block, copy = setup(outs, sems)
        for w in range(n):
            for j, chip in enumerate(chips):
                copy(w, 1 + j, (*chip, me[2]), me).wait_recv()
                copy(w, 4 + j, (*chip, me[2]), sibling).start()

    def last(ins, outs, sems):
        me, sibling, chips, block, copy = setup(outs, sems)
        for w in range(n):
            copy(w, 0, sibling, me).wait_recv()
            for j, chip in enumerate(chips):
                copy(w, 4 + j, (*chip, 1 - me[2]), me).wait_recv()
            copy(w, 0, me, sibling, src=ins[w]).wait_send()
            for j, chip in enumerate(chips):
                copy(w, 1 + j, me, (*chip, me[2]), src=ins[w]).wait_send()
                copy(w, 4 + j, (*chip, me[2]), sibling).wait_send()
            pltpu.make_async_copy(ins[w], block(w, me), sems[2].at[w]).wait()

    return _Rider(
        shards, [jax.ShapeDtypeStruct((N_DEV,) + s.shape, BF16) for s in shards],
        [pltpu.SemaphoreType.DMA((n, N_DEV - 1)), pltpu.SemaphoreType.DMA((n, N_DEV - 1)),
         pltpu.SemaphoreType.DMA((n,))],
        first=first, mid=mid, last=last, ins_in_vmem=True)


def _sibling_rider(gblocks):
    n = len(gblocks)

    def copies(ins, outs, sems):
        x, y, c = _my_place()
        send_sems, recv_sems = sems
        made = []
        for w in range(n):
            for f, (fx, fy) in enumerate(CHIP_FLIPS):
                chip = 4 * _flip(x, fx) + 2 * _flip(y, fy)
                made.append(pltpu.make_async_remote_copy(
                    src_ref=ins[w].at[chip + 1 - c], dst_ref=outs[w].at[f], send_sem=send_sems.at[w, f],
                    recv_sem=recv_sems.at[w, f], device_id=(x, y, 1 - c), device_id_type=MESH))
        return made

    def first(ins, outs, sems):
        for cp in copies(ins, outs, sems):
            cp.start()

    def last(ins, outs, sems):
        for cp in copies(ins, outs, sems):
            cp.wait_recv()
            cp.wait_send()

    return _Rider(gblocks, [jax.ShapeDtypeStruct((4,) + g.shape[1:], BF16) for g in gblocks],
                  [pltpu.SemaphoreType.DMA((n, 4))] * 2, first=first, last=last)


def _chip_rider(sums):
    n = len(sums)

    def copies(ins, outs, sems):
        x, y, c = _my_place()
        send_sems, recv_sems = sems
        made = []
        for w in range(n):
            for f in (1, 2, 3):
                fx, fy = CHIP_FLIPS[f]
                made.append(pltpu.make_async_remote_copy(
                    src_ref=ins[w].at[f - 1], dst_ref=outs[w].at[f - 1], send_sem=send_sems.at[w, f - 1],
                    recv_sem=recv_sems.at[w, f - 1], device_id=(_flip(x, fx), _flip(y, fy), c), device_id_type=MESH))
        return made

    def first(ins, outs, sems):
        for cp in copies(ins, outs, sems):
            cp.start()

    def last(ins, outs, sems):
        for cp in copies(ins, outs, sems):
            cp.wait_recv()
            cp.wait_send()

    return _Rider(sums, [jax.ShapeDtypeStruct(s.shape, BF16) for s in sums],
                  [pltpu.SemaphoreType.DMA((n, 3))] * 2, first=first, last=last)


def _push_to_all(v_ref, out_ref, send_sems, recv_sems, local_sem, wait=True):
    x, y, c = _my_place()
    me = 4 * x + 2 * y + c
    mine = pltpu.make_async_copy(v_ref, out_ref.at[me], local_sem)
    mine.start()
    sends = []
    for k in range(1, N_DEV):
        px, py, pc = _flip(x, k & 4), _flip(y, k & 2), _flip(c, k & 1)
        cp = pltpu.make_async_remote_copy(
            src_ref=v_ref, dst_ref=out_ref.at[me], send_sem=send_sems.at[k - 1], recv_sem=recv_sems.at[k - 1],
            device_id=(px, py, pc), device_id_type=MESH)
        cp.start()
        sends.append(cp)

    def finish():
        for k in range(1, N_DEV):
            px, py, pc = _flip(x, k & 4), _flip(y, k & 2), _flip(c, k & 1)
            pltpu.make_async_remote_copy(
                src_ref=v_ref, dst_ref=out_ref.at[4 * px + 2 * py + pc], send_sem=send_sems.at[k - 1],
                recv_sem=recv_sems.at[k - 1], device_id=(px, py, pc), device_id_type=MESH).wait_recv()
        for cp in sends:
            cp.wait_send()
        mine.wait()

    if wait:
        finish()
    return finish


def _small_allgather(v, name):
    def body(v_ref, out_ref, send_sems, recv_sems, local_sem):
        _push_to_all(v_ref, out_ref, send_sems, recv_sems, local_sem)

    return pl.pallas_call(
        body, name=name,
        out_shape=jax.ShapeDtypeStruct((N_DEV,) + v.shape, F32),
        in_specs=[pl.BlockSpec(memory_space=pltpu.VMEM)],
        out_specs=pl.BlockSpec(memory_space=pltpu.VMEM),
        scratch_shapes=[pltpu.SemaphoreType.DMA((N_DEV - 1,)), pltpu.SemaphoreType.DMA((N_DEV - 1,)),
                        pltpu.SemaphoreType.DMA],
        compiler_params=pltpu.CompilerParams(vmem_limit_bytes=VMEM_LIMIT),
    )(v)


def _gather_first_weight(shard, others, cond_rows, w_ada, b_cols):
    n = len(others)
    ada_cols = w_ada.shape[1]
    c_rows = D_MODEL // LANES

    def body(*refs):
        w_ref, other_refs = refs[0], refs[1:1 + n]
        cond_ref, wada_ref, bcols_ref = refs[1 + n:4 + n]
        out_ref, cast_refs = refs[4 + n], refs[5 + n:5 + 2 * n]
        cond_all_ref, mod_all_ref = refs[5 + 2 * n:7 + 2 * n]
        mine_ref, mod_ref, send_sems, recv_sems, local_sem, small_send, small_recv, small_local = refs[7 + 2 * n:]
        x, y, c = _my_place()
        me, sibling = (x, y, c), (x, y, 1 - c)
        xnb, ynb, diag = (1 - x, y), (x, 1 - y), (1 - x, 1 - y)
        half = shard.shape[0] // 2

        def block(place, part=None):
            ref = out_ref.at[4 * place[0] + 2 * place[1] + place[2]]
            return ref if part is None else ref.at[pl.ds(part * half, half)]

        def copy(k, place, to, part=None, src=None):
            return pltpu.make_async_remote_copy(
                src_ref=block(place, part) if src is None else src, dst_ref=block(place, part),
                send_sem=send_sems.at[k], recv_sem=recv_sems.at[k], device_id=to, device_id_type=MESH)

        finish_cond = _push_to_all(cond_ref, cond_all_ref, small_send.at[0], small_recv.at[0], small_local.at[0],
                                   wait=False)
        mine_ref[...] = w_ref[...].astype(BF16)
        local = pltpu.make_async_copy(mine_ref, block(me), local_sem)
        local.start()
        started = [copy(0, me, sibling, src=mine_ref), copy(1, me, (*xnb, c), src=mine_ref),
                   copy(2, me, (*ynb, c), src=mine_ref)]
        for cp in started:
            cp.start()
        finish_cond()
        mod = jnp.zeros((N_DEV, ada_cols), F32) + bcols_ref[...]
        for r in range(c_rows):
            cf = cond_all_ref[:, r, :]
            act = (cf * _sigmoid(cf)).astype(BF16)
            mod = mod + jnp.dot(act, wada_ref[r * LANES:(r + 1) * LANES, :].astype(BF16),
                                preferred_element_type=F32)
        mod_ref[...] = mod
        finish_mod = _push_to_all(mod_ref, mod_all_ref, small_send.at[1], small_recv.at[1], small_local.at[1],
                                  wait=False)
        for o_ref, c_ref in zip(other_refs, cast_refs):
            c_ref[...] = o_ref[...].astype(BF16)
        def start(cp):
            cp.start()
            started.append(cp)

        copy(1, (*xnb, c), me).wait_recv()
        start(copy(3, (*xnb, c), (*ynb, c), part=0))
        start(copy(5, (*xnb, c), sibling))
        copy(2, (*ynb, c), me).wait_recv()
        start(copy(4, (*ynb, c), (*xnb, c), part=1))
        start(copy(6, (*ynb, c), sibling))
        copy(3, (*diag, c), me, part=0).wait_recv()
        start(copy(7, (*diag, c), sibling, part=0))
        copy(4, (*diag, c), me, part=1).wait_recv()
        start(copy(8, (*diag, c), sibling, part=1))
        copy(0, sibling, me).wait_recv()
        copy(5, (*xnb, 1 - c), me).wait_recv()
        copy(6, (*ynb, 1 - c), me).wait_recv()
        copy(7, (*diag, 1 - c), me, part=0).wait_recv()
        copy(8, (*diag, 1 - c), me, part=1).wait_recv()
        finish_mod()
        for cp in started:
            cp.wait_send()
        local.wait()

    vmem = pl.BlockSpec(memory_space=pltpu.VMEM)
    outs = pl.pallas_call(
        body, name="gather_w_in",
        out_shape=[jax.ShapeDtypeStruct((N_DEV,) + shard.shape, BF16)]
        + [jax.ShapeDtypeStruct(o.shape, BF16) for o in others]
        + [jax.ShapeDtypeStruct((N_DEV,) + cond_rows.shape, F32), jax.ShapeDtypeStruct((N_DEV, N_DEV, ada_cols), F32)],
        in_specs=[vmem] * (4 + n),
        out_specs=[ANY] + [vmem] * (n + 2),
        scratch_shapes=[pltpu.VMEM(shard.shape, BF16), pltpu.VMEM((N_DEV, ada_cols), F32),
                        pltpu.SemaphoreType.DMA((9,)), pltpu.SemaphoreType.DMA((9,)),
                        pltpu.SemaphoreType.DMA,
                        pltpu.SemaphoreType.DMA((2, N_DEV - 1)), pltpu.SemaphoreType.DMA((2, N_DEV - 1)),
                        pltpu.SemaphoreType.DMA((2,))],
        compiler_params=pltpu.CompilerParams(vmem_limit_bytes=VMEM_LIMIT),
    )(shard, *others, cond_rows, w_ada, b_cols)
    return outs[0], list(outs[1:1 + n]), outs[1 + n], outs[2 + n]


def _sibling_exchange_sum(gblocks, name):
    n = len(gblocks)

    def body(*refs):
        g_refs, out_refs = refs[:n], refs[n:3 * n]
        bufs = refs[3 * n:5 * n]
        own_sems, send_sems, recv_sems = refs[5 * n:]
        x, y, c = _my_place()
        pairs = []
        for w in range(n):
            own_buf, sib_buf = bufs[2 * w], bufs[2 * w + 1]
            for f, (fx, fy) in enumerate(CHIP_FLIPS):
                chip = 4 * _flip(x, fx) + 2 * _flip(y, fy)
                own = pltpu.make_async_copy(g_refs[w].at[chip + c], own_buf.at[f], own_sems.at[w, f])
                own.start()
                remote = pltpu.make_async_remote_copy(
                    src_ref=g_refs[w].at[chip + 1 - c], dst_ref=sib_buf.at[f], send_sem=send_sems.at[w, f],
                    recv_sem=recv_sems.at[w, f], device_id=(x, y, 1 - c), device_id_type=MESH)
                remote.start()
                pairs.append((own, remote))
        for w in range(n):
            own_buf, sib_buf = bufs[2 * w], bufs[2 * w + 1]
            sums_ref, mine_ref = out_refs[2 * w], out_refs[2 * w + 1]
            for f in (1, 2, 3, 0):
                own, remote = pairs[4 * w + f]
                own.wait()
                remote.wait_recv()
                total = own_buf[f].astype(F32) + sib_buf[f].astype(F32)
                if f == 0:
                    mine_ref[...] = total
                else:
                    sums_ref[f - 1] = total.astype(BF16)
        for _, remote in pairs:
            remote.wait_send()

    vmem = pl.BlockSpec(memory_space=pltpu.VMEM)
    out_shape, scratch = [], []
    for g in gblocks:
        out_shape += [jax.ShapeDtypeStruct((3,) + g.shape[1:], BF16), jax.ShapeDtypeStruct(g.shape[1:], F32)]
        scratch += [pltpu.VMEM((4,) + g.shape[1:], BF16)] * 2
    outs = pl.pallas_call(
        body, name=name, out_shape=out_shape,
        in_specs=[ANY] * n, out_specs=[vmem] * (2 * n),
        scratch_shapes=scratch + [pltpu.SemaphoreType.DMA((n, 4))] * 3,
        compiler_params=pltpu.CompilerParams(vmem_limit_bytes=VMEM_LIMIT),
    )(*gblocks)
    return [(outs[2 * w], outs[2 * w + 1]) for w in range(n)]


def _ada_grad_adamw(c_all, dmod_cols, w, m, v):
    rows, cols = w.shape
    tile = _row_tile(rows, LANES)

    def body(c_ref, d_ref, w_ref, m_ref, v_ref, g_ref, dl_ref, nm_ref, nv_ref):
        cf = c_ref[...]
        act = (cf * _sigmoid(cf)).astype(BF16)
        g = lax.dot_general(act, d_ref[...].astype(BF16), TN_DIMS, preferred_element_type=F32)
        g_ref[...] = g
        dl_ref[...], nm_ref[...], nv_ref[...] = _adamw_update(w_ref[...], g, m_ref[...], v_ref[...])

    spec = pl.BlockSpec((tile, cols), lambda i: (i, 0))
    outs, _ = _call(
        body, "ada_grad_adamw", (rows // tile,), [c_all, dmod_cols, w, m, v],
        [pl.BlockSpec((N_DEV, tile), lambda i: (0, i)), _full((N_DEV, cols)), spec, spec, spec],
        [jax.ShapeDtypeStruct((rows, cols), F32)] * 4, [spec] * 4)
    return outs


PACK_ROWS = 24
PACK_DMOD = 0
PACK_PARAMS = {"g_mix": (6, D_MODEL), "b_in": (7, IN_WIDTH), "g_ffn": (14, D_MODEL), "g_final": (15, D_MODEL),
               "sinks": (19, N_Q_HEADS)}
PACK_CONV = 16
PACK_SQERR = 20


def _small_finalize(packed_all, params):
    names = ["b_ada"] + list(PACK_PARAMS)
    layout = dict(PACK_PARAMS, b_ada=(PACK_DMOD, N_MOD * D_MODEL))
    n = len(names)

    def body(*refs):
        p_ref = refs[0]
        ins = refs[1:1 + 3 * n]
        outs = refs[1 + 3 * n:1 + 7 * n]
        conv_ref, loss_ref = refs[1 + 7 * n:]
        total = p_ref[0]
        for d in range(1, N_DEV):
            total = total + p_ref[d]
        for k, name in enumerate(names):
            row0, width = layout[name]
            w_ref, m_ref, v_ref = ins[3 * k:3 * k + 3]
            g_ref, d_ref, nm_ref, nv_ref = outs[4 * k:4 * k + 4]
            for chunk in range(-(-width // D_MODEL)):
                lo = chunk * D_MODEL
                hi = min(lo + D_MODEL, width)
                g = total[row0 + chunk:row0 + chunk + 1, :hi - lo]
                g_ref[:, lo:hi] = g
                d_ref[:, lo:hi], nm_ref[:, lo:hi], nv_ref[:, lo:hi] = _adamw_update(
                    w_ref[:, lo:hi], g, m_ref[:, lo:hi], v_ref[:, lo:hi])
        conv_ref[...] = total[PACK_CONV:PACK_CONV + 3, :]
        loss_ref[...] = (0.5 / D_MODEL) * jnp.sum(total[PACK_SQERR:PACK_SQERR + 1, :], keepdims=True)

    vmem = pl.BlockSpec(memory_space=pltpu.VMEM)
    flat = [a for name in names for a in params[name]]
    out_shape = [jax.ShapeDtypeStruct(params[name][0].shape, F32) for name in names for _ in range(4)]
    outs = pl.pallas_call(
        body, name="small_finalize",
        out_shape=out_shape + [jax.ShapeDtypeStruct((3, D_MODEL), F32), jax.ShapeDtypeStruct((1, 1), F32)],
        in_specs=[vmem] * (1 + 3 * n),
        out_specs=[vmem] * (4 * n + 2),
        compiler_params=pltpu.CompilerParams(vmem_limit_bytes=VMEM_LIMIT),
    )(packed_all, *flat)
    return {name: tuple(outs[4 * k:4 * k + 4]) for k, name in enumerate(names)}, outs[4 * n], outs[4 * n + 1]


def _row_tile(rows, multiple):
    for cand in range(rows // MIN_STREAM_STEPS, 0, -1):
        if rows % cand == 0 and cand % multiple == 0:
            return cand
    return rows


def _adamw_update(w, g, m, v):
    c1 = 1.0 / (1.0 - ADAM_B1 ** ADAM_STEP)
    c2 = 1.0 / (1.0 - ADAM_B2 ** ADAM_STEP)
    nm = ADAM_B1 * m + (1.0 - ADAM_B1) * g
    nv = ADAM_B2 * v + (1.0 - ADAM_B2) * (g * g)
    delta = -ADAM_LR * ((nm * c1) / (jnp.sqrt(nv * c2) + ADAM_EPS) + ADAM_WD * w)
    return delta, nm, nv


def _adamw(w, g, m, v, name):
    rows, cols = w.shape
    tile = _row_tile(rows, SUBLANES)

    def body(w_ref, g_ref, m_ref, v_ref, d_ref, nm_ref, nv_ref):
        d_ref[...], nm_ref[...], nv_ref[...] = _adamw_update(w_ref[...], g_ref[...], m_ref[...], v_ref[...])

    spec = pl.BlockSpec((tile, cols), lambda i: (i, 0))
    outs, _ = _call(body, name, (rows // tile,), [w, g, m, v], [spec] * 4,
                    [jax.ShapeDtypeStruct((rows, cols), F32)] * 3, [spec] * 3)
    return outs


def _sibling_sum(gblocks, sib, name):
    _, r, cdim = gblocks.shape
    tile = _row_tile(r, BF16_ROWS)
    x, y, c = _my_place()
    table = jnp.stack([4 * _flip(x, fx) + 2 * _flip(y, fy) + c for fx, fy in CHIP_FLIPS]).astype(jnp.int32)

    def body(table_ref, own0, own1, own2, own3, sib_ref, sums_ref, mine_ref):
        mine_ref[...] = own0[...].astype(F32) + sib_ref[0].astype(F32)
        for f, own in ((1, own1), (2, own2), (3, own3)):
            sums_ref[f - 1] = (own[...].astype(F32) + sib_ref[f].astype(F32)).astype(BF16)

    own_specs = [pl.BlockSpec((None, tile, cdim), functools.partial(lambda i, tab, f: (tab[f], i, 0), f=f))
                 for f in range(4)]
    return pl.pallas_call(
        body, name=name,
        grid_spec=pltpu.PrefetchScalarGridSpec(
            num_scalar_prefetch=1, grid=(r // tile,),
            in_specs=own_specs + [pl.BlockSpec((4, tile, cdim), lambda i, tab: (0, i, 0))],
            out_specs=[pl.BlockSpec((3, tile, cdim), lambda i, tab: (0, i, 0)),
                       pl.BlockSpec((tile, cdim), lambda i, tab: (i, 0))]),
        out_shape=[jax.ShapeDtypeStruct((3, r, cdim), BF16), jax.ShapeDtypeStruct((r, cdim), F32)],
        compiler_params=pltpu.CompilerParams(dimension_semantics=("arbitrary",), vmem_limit_bytes=VMEM_LIMIT),
    )(table, gblocks, gblocks, gblocks, gblocks, sib)


def _chip_sum_adamw(mine, ici, w, m, v, name):
    r, cdim = mine.shape
    tile = _row_tile(r, BF16_ROWS)

    def body(mine_ref, ici_ref, w_ref, m_ref, v_ref, g_ref, d_ref, nm_ref, nv_ref):
        g = mine_ref[...]
        for f in range(3):
            g = g + ici_ref[f].astype(F32)
        g_ref[...] = g
        d_ref[...], nm_ref[...], nv_ref[...] = _adamw_update(w_ref[...], g, m_ref[...], v_ref[...])

    spec = pl.BlockSpec((tile, cdim), lambda i: (i, 0))
    outs, _ = _call(
        body, name, (r // tile,), [mine, ici, w, m, v],
        [spec, pl.BlockSpec((3, tile, cdim), lambda i: (0, i, 0)), spec, spec, spec],
        [jax.ShapeDtypeStruct((r, cdim), F32)] * 4, [spec] * 4)
    return outs


REF_KV_COL = D_MODEL
REF_REST_COL = D_MODEL + 2 * KV_WIDTH
IN_CHUNK = 1280
IN_PIECES = ([(0, 0, D_MODEL)]
             + [(D_MODEL + n * IN_CHUNK, REF_REST_COL + n * IN_CHUNK, IN_CHUNK) for n in range(REST_WIDTH // IN_CHUNK)]
             + [(KV_COL, REF_KV_COL, 2 * KV_WIDTH)])


def _inproj_fwd(x, vec, w_t, b_in, rider):
    t = x.shape[0]
    tm = min(TOKEN_TILE, t)

    def body(x_ref, vec_ref, w_ref, b_ref, z_ref, h_ref):
        xf = x_ref[...]
        r = lax.rsqrt(jnp.mean(xf * xf, axis=-1, keepdims=True) + EPS)
        h = (xf * r) * (vec_ref[0:1, :] * (1.0 + vec_ref[1:2, :])) + vec_ref[2:3, :]
        hb = h.astype(BF16)
        h_ref[...] = hb
        for mine, ref, width in IN_PIECES:
            zc = lax.dot_general(hb, w_ref[ref:ref + width, :], NT_DIMS, preferred_element_type=F32)
            z_ref[:, mine:mine + width] = (zc + b_ref[:, ref:ref + width]).astype(BF16)

    return _call(
        body, "inproj_fwd", (t // tm,), [x, vec, w_t, b_in],
        [pl.BlockSpec((tm, D_MODEL), lambda i: (i, 0)), _full((SUBLANES, D_MODEL)),
         _full((IN_WIDTH, D_MODEL)), _full((1, IN_WIDTH))],
        [jax.ShapeDtypeStruct((t, IN_WIDTH), BF16), jax.ShapeDtypeStruct((t, D_MODEL), BF16)],
        [pl.BlockSpec((tm, IN_WIDTH), lambda i: (i, 0)), pl.BlockSpec((tm, D_MODEL), lambda i: (i, 0))],
        rider=rider)


PAIRS = GROUP // 2
STACK = PAIRS * WINDOW


ATTN_BLOCKS = 4
ATTN_BWD_BLOCKS = 1
LOG2E = 1.4426950408889634
LN2 = 0.6931471805599453
SCORE_SCALE = ATTN_SCALE * LOG2E


def _fill_window_bias(bias_ref):
    shape = bias_ref.shape[1:]
    kj = lax.broadcasted_iota(jnp.int32, shape, 0)
    qi = jnp.bitwise_and(lax.broadcasted_iota(jnp.int32, shape, 1), WINDOW - 1)
    in_prev = jnp.logical_and(kj < WINDOW, kj > qi)
    in_cur = jnp.logical_and(kj >= WINDOW, (kj - WINDOW) <= qi)
    bias_ref[0] = jnp.where(in_cur, 0.0, -jnp.inf)
    bias_ref[1] = jnp.where(jnp.logical_or(in_prev, in_cur), 0.0, -jnp.inf)


def _half_tiles(tile):
    low = lax.broadcasted_iota(jnp.int32, tile.shape, 1) < HEAD_DIM
    swapped = jnp.concatenate([tile[:, HEAD_DIM:], tile[:, :HEAD_DIM]], axis=1)
    zero = jnp.zeros_like(tile)
    return ((jnp.where(low, tile, zero), jnp.where(low, zero, swapped)),
            (jnp.where(low, swapped, zero), jnp.where(low, zero, tile)))


def _stack_pairs(ref, row0, j):
    return jnp.concatenate(
        [ref[pl.ds(row0, WINDOW), (j * PAIRS + p) * LANES:(j * PAIRS + p + 1) * LANES] for p in range(PAIRS)], axis=0)


def _per_pair_row(values):
    pair = lax.broadcasted_iota(jnp.int32, (1, STACK), 1) // WINDOW
    row = jnp.full((1, STACK), values[PAIRS - 1], F32)
    for p in range(PAIRS - 2, -1, -1):
        row = jnp.where(pair == p, values[p], row)
    return row


def _attn_fwd(z, sinks, rider):
    t = z.shape[0]
    tq = min(TOKEN_TILE, t)
    nblk = tq // WINDOW

    def body(q_ref, kv_ref, sink_ref, o_ref, lse_ref, bias_ref):
        i = pl.program_id(0)

        @pl.when(i == 0)
        def _():
            _fill_window_bias(bias_ref)

        def window(b):
            row0 = pl.multiple_of(b * WINDOW, WINDOW)
            start = i * tq + b * WINDOW
            prev = pl.multiple_of(jnp.maximum(start - WINDOW, 0), WINDOW)
            cur = pl.multiple_of(start, WINDOW)
            kvw = jnp.concatenate([kv_ref[pl.ds(prev, WINDOW), :], kv_ref[pl.ds(cur, WINDOW), :]], axis=0)
            return row0, _half_tiles(kvw[:, :KV_WIDTH]), _half_tiles(kvw[:, KV_WIDTH:]), bias_ref[jnp.minimum(start, 1)]

        def block_group(bb, carry):
            windows = [window(bb * ATTN_BLOCKS + n) for n in range(ATTN_BLOCKS)]
            for j in range(N_KV_HEADS):
                for pr in range(PAIRS):
                    cols = slice((j * PAIRS + pr) * LANES, (j * PAIRS + pr + 1) * LANES)
                    o_ts = [jnp.zeros((LANES, WINDOW), F32) for _ in windows]
                    for parity in range(2):
                        h = j * GROUP + 2 * pr + parity
                        sink = sink_ref[h] * LOG2E
                        for n, (row0, k_halves, v_halves, bias) in enumerate(windows):
                            qp = q_ref[pl.ds(row0, WINDOW), cols]
                            s = lax.dot_general(k_halves[j][parity], qp, NT_DIMS, preferred_element_type=F32)
                            s = s * SCORE_SCALE + bias
                            m = jnp.maximum(jnp.max(s, axis=0, keepdims=True), sink)
                            p = jnp.exp2(s - m)
                            denom = jnp.sum(p, axis=0, keepdims=True) + jnp.exp2(sink - m)
                            pv = lax.dot_general(v_halves[j][parity], p.astype(BF16), TN_DIMS,
                                                 preferred_element_type=F32)
                            o_ts[n] = o_ts[n] + pv * (1.0 / denom)
                            lse_ref[h:h + 1, pl.ds(row0, WINDOW)] = m + jnp.log2(denom)
                    for n, (row0, _, _, _) in enumerate(windows):
                        o_ref[pl.ds(row0, WINDOW), cols] = jnp.transpose(o_ts[n].astype(BF16))
            return carry

        lax.fori_loop(0, nblk // ATTN_BLOCKS, block_group, 0)

    return _call(
        body, "attn_fwd", (t // tq,), [z, z, sinks],
        [pl.BlockSpec((tq, D_MODEL), lambda i: (i, 0)),
         pl.BlockSpec((t, 2 * KV_WIDTH), lambda i: (0, KV_COL // (2 * KV_WIDTH))),
         pl.BlockSpec(memory_space=pltpu.SMEM)],
        [jax.ShapeDtypeStruct((t, D_MODEL), BF16), jax.ShapeDtypeStruct((N_Q_HEADS, t), F32)],
        [pl.BlockSpec((tq, D_MODEL), lambda i: (i, 0)), pl.BlockSpec((N_Q_HEADS, tq), lambda i: (0, i))],
        scratch=[pltpu.VMEM((2, 2 * WINDOW, WINDOW), F32)], rider=rider)


HALO = BF16_ROWS


def _shift_down(u, uh, k):
    rolled = pltpu.roll(u, k, 0)
    row = lax.broadcasted_iota(jnp.int32, (SUBLANES, u.shape[1]), 0)
    top = rolled[:SUBLANES, :]
    for j in range(k):
        top = jnp.where(row == j, uh[HALO - k + j:HALO - k + j + 1, :], top)
    return jnp.concatenate([top, rolled[SUBLANES:, :]], axis=0)


def _shift_up(u, nxt, k):
    n = u.shape[0]
    rolled = pltpu.roll(u, n - k, 0)
    row = lax.broadcasted_iota(jnp.int32, (SUBLANES, u.shape[1]), 0)
    bottom = rolled[n - SUBLANES:, :]
    for j in range(k):
        bottom = jnp.where(row == SUBLANES - k + j, nxt[j:j + 1, :], bottom)
    return jnp.concatenate([rolled[:n - SUBLANES, :], bottom], axis=0)


def _conv_inputs(cc_ref, cx_ref, hc_ref, hx_ref, first_tile):
    cc = cc_ref[...].astype(F32)
    cx = cx_ref[...].astype(F32)
    u = cc * cx
    uh = jnp.where(first_tile, 0.0, hc_ref[...].astype(F32) * hx_ref[...].astype(F32))
    return cc, cx, u, _shift_down(u, uh, 1), _shift_down(u, uh, 2)


def _z_specs(tm, order):
    per_tile = tm // HALO
    cols = [pl.BlockSpec((tm, D_MODEL), functools.partial(lambda i, j: (order(i), j), j=j)) for j in range(1, 6)]
    halos = [pl.BlockSpec((HALO, D_MODEL),
                          functools.partial(lambda i, j: (jnp.maximum(order(i) * per_tile - 1, 0), j), j=j))
             for j in (2, 3)]
    return cols + halos


def _mix_fwd(x, attn, z, vec, w_out):
    t = x.shape[0]
    tm = min(TOKEN_TILE, t)

    def body(x_ref, a_ref, cb_ref, cc_ref, cx_ref, ga_ref, gc_ref, hc_ref, hx_ref, vec_ref, w_ref,
             m_ref, x2_ref, h2_ref, o_ref):
        i = pl.program_id(0)
        _, _, u, u1, u2 = _conv_inputs(cc_ref, cx_ref, hc_ref, hx_ref, i == 0)
        cv = vec_ref[4:5, :] * u2 + vec_ref[5:6, :] * u1 + vec_ref[6:7, :] * u
        conv = cb_ref[...].astype(F32) * cv
        merged = (_sigmoid(ga_ref[...].astype(F32)) * a_ref[...].astype(F32)
                  + _sigmoid(gc_ref[...].astype(F32)) * conv)
        mb = merged.astype(BF16)
        m_ref[...] = mb
        o = jnp.dot(mb, w_ref[...], preferred_element_type=F32)
        o_ref[...] = o.astype(BF16)
        x2 = x_ref[...] + vec_ref[0:1, :] * o
        x2_ref[...] = x2
        r = lax.rsqrt(jnp.mean(x2 * x2, axis=-1, keepdims=True) + EPS)
        h2 = (x2 * r) * (vec_ref[1:2, :] * (1.0 + vec_ref[2:3, :])) + vec_ref[3:4, :]
        h2_ref[...] = h2.astype(BF16)

    tok = pl.BlockSpec((tm, D_MODEL), lambda i: (i, 0))
    outs, _ = _call(
        body, "mix_fwd", (t // tm,), [x, attn, z, z, z, z, z, z, z, vec, w_out],
        [tok, tok] + _z_specs(tm, lambda i: i) + [_full((SUBLANES, D_MODEL)), _full((D_MODEL, D_MODEL))],
        [jax.ShapeDtypeStruct((t, D_MODEL), BF16), jax.ShapeDtypeStruct((t, D_MODEL), F32),
         jax.ShapeDtypeStruct((t, D_MODEL), BF16), jax.ShapeDtypeStruct((t, D_MODEL), BF16)],
        [tok, tok, tok, tok])
    return outs


def _ffn_fwd(h2, w_t):
    t = h2.shape[0]
    tm = min(TOKEN_TILE, t)

    def body(h_ref, w_ref, gu_ref, a_ref):
        hb = h_ref[...]
        for n in range(D_FF // FF_CHUNK):
            lo, hi = n * FF_CHUNK, (n + 1) * FF_CHUNK
            g = lax.dot_general(hb, w_ref[lo:hi, :], NT_DIMS, preferred_element_type=F32)
            u = lax.dot_general(hb, w_ref[D_FF + lo:D_FF + hi, :], NT_DIMS, preferred_element_type=F32)
            sg = _sigmoid(g)
            silu = g * sg
            gu_ref[:, lo:hi] = (u * (sg + silu * (1.0 - sg))).astype(BF16)
            gu_ref[:, D_FF + lo:D_FF + hi] = silu.astype(BF16)
            a_ref[:, lo:hi] = (silu * u).astype(BF16)

    outs, _ = _call(
        body, "ffn_fwd", (t // tm,), [h2, w_t],
        [pl.BlockSpec((tm, D_MODEL), lambda i: (i, 0)), _full((2 * D_FF, D_MODEL))],
        [jax.ShapeDtypeStruct((t, 2 * D_FF), BF16), jax.ShapeDtypeStruct((t, D_FF), BF16)],
        [pl.BlockSpec((tm, 2 * D_FF), lambda i: (i, 0)), pl.BlockSpec((tm, D_FF), lambda i: (i, 0))])
    return outs


def _ffn_out_loss(a, gu, x2, target, vec, w_ffn_out):
    t = a.shape[0]
    tm = min(TOKEN_TILE, t)

    def body(a_ref, gu_ref, x2_ref, t_ref, vec_ref, w_ref, dx3_ref, df_ref, dgu_ref, acc_ref):
        @pl.when(pl.program_id(0) == 0)
        def _():
            acc_ref[...] = jnp.zeros_like(acc_ref)

        ga2 = vec_ref[0:1, :]
        gf = vec_ref[1:2, :]
        parts = min(ROW_PARTS, tm // LANES)
        part_rows = [slice(n * (tm // parts), (n + 1) * (tm // parts)) for n in range(parts)]

        def head(rows, f):
            x3 = x2_ref[rows, :] + ga2 * f
            r = lax.rsqrt(jnp.mean(x3 * x3, axis=-1, keepdims=True) + EPS)
            xn = x3 * r
            err = xn * gf - t_ref[rows, :]
            dxn = err * (gf * (1.0 / D_MODEL))
            dx3 = r * (dxn - xn * jnp.mean(dxn * xn, axis=-1, keepdims=True))
            dx3_ref[rows, :] = dx3.astype(GRAD_STREAM)
            sums = (jnp.sum(err * err, axis=0, keepdims=True),
                    jnp.sum(err * xn, axis=0, keepdims=True) * (1.0 / D_MODEL),
                    jnp.sum(dx3 * f, axis=0, keepdims=True))
            df = (dx3 * ga2).astype(BF16)
            df_ref[rows, :] = df
            return df, sums

        def tail(rows, df):
            for n in range(D_FF // FF_CHUNK):
                lo, hi = n * FF_CHUNK, (n + 1) * FF_CHUNK
                da = lax.dot_general(df, w_ref[lo:hi, :], NT_DIMS, preferred_element_type=F32)
                dgu_ref[rows, lo:hi] = (da * gu_ref[rows, lo:hi].astype(F32)).astype(BF16)
                dgu_ref[rows, D_FF + lo:D_FF + hi] = (da * gu_ref[rows, D_FF + lo:D_FF + hi].astype(F32)).astype(BF16)

        fs = [jnp.dot(a_ref[rows, :], w_ref[...], preferred_element_type=F32) for rows in part_rows]
        heads = [head(rows, f) for rows, f in zip(part_rows, fs)]
        for rows, (df, _) in zip(part_rows, heads):
            tail(rows, df)
        for k in range(3):
            total = heads[0][1][k]
            for _, sums in heads[1:]:
                total = total + sums[k]
            acc_ref[k:k + 1, :] += total

    tok = pl.BlockSpec((tm, D_MODEL), lambda i: (i, 0))
    outs, _ = _call(
        body, "ffn_out_loss", (t // tm,), [a, gu, x2, target, vec, w_ffn_out],
        [pl.BlockSpec((tm, D_FF), lambda i: (i, 0)), pl.BlockSpec((tm, 2 * D_FF), lambda i: (i, 0)),
         tok, tok, _full((SUBLANES, D_MODEL)), _full((D_FF, D_MODEL))],
        [jax.ShapeDtypeStruct((t, D_MODEL), GRAD_STREAM), jax.ShapeDtypeStruct((t, D_MODEL), BF16),
         jax.ShapeDtypeStruct((t, 2 * D_FF), BF16), jax.ShapeDtypeStruct((SUBLANES, D_MODEL), F32)],
        [tok, tok, pl.BlockSpec((tm, 2 * D_FF), lambda i: (i, 0)), _full((SUBLANES, D_MODEL))])
    return outs


def _ffn_in_bwd(dgu, x2, dx3, vec, w_t, rider):
    t = x2.shape[0]
    tm = min(TOKEN_TILE, t)

    def body(dgu_ref, x2_ref, dx3_ref, vec_ref, wf_ref, dx2_ref, acc_ref):
        @pl.when(pl.program_id(0) == 0)
        def _():
            acc_ref[...] = jnp.zeros_like(acc_ref)

        gffn = vec_ref[0:1, :]
        sc2 = vec_ref[1:2, :]
        parts = min(ROW_PARTS, tm // LANES)
        part_rows = [slice(n * (tm // parts), (n + 1) * (tm // parts)) for n in range(parts)]
        dhs = [jnp.dot(dgu_ref[rows, :], wf_ref[...], preferred_element_type=F32) for rows in part_rows]
        gs = gffn * (1.0 + sc2)
        sum_dh = jnp.zeros((1, D_MODEL), F32)
        sum_dh_xn = jnp.zeros((1, D_MODEL), F32)
        for rows, dh2 in zip(part_rows, dhs):
            x2 = x2_ref[rows, :]
            r = lax.rsqrt(jnp.mean(x2 * x2, axis=-1, keepdims=True) + EPS)
            xn = x2 * r
            dh_xn = dh2 * xn
            sum_dh = sum_dh + jnp.sum(dh2, axis=0, keepdims=True)
            sum_dh_xn = sum_dh_xn + jnp.sum(dh_xn, axis=0, keepdims=True)
            dx2 = dx3_ref[rows, :].astype(F32) + r * (dh2 * gs - xn * jnp.mean(dh_xn * gs, axis=-1, keepdims=True))
            dx2_ref[rows, :] = dx2.astype(GRAD_STREAM)
        acc_ref[0:1, :] += sum_dh
        acc_ref[1:2, :] += sum_dh_xn * gffn
        acc_ref[2:3, :] += sum_dh_xn * (1.0 + sc2)

    tok = pl.BlockSpec((tm, D_MODEL), lambda i: (i, 0))
    return _call(
        body, "ffn_in_bwd", (t // tm,), [dgu, x2, dx3, vec, w_t],
        [pl.BlockSpec((tm, 2 * D_FF), lambda i: (i, 0)), tok, tok, _full((SUBLANES, D_MODEL)),
         _full((2 * D_FF, D_MODEL))],
        [jax.ShapeDtypeStruct((t, D_MODEL), GRAD_STREAM), jax.ShapeDtypeStruct((SUBLANES, D_MODEL), F32)],
        [tok, _full((SUBLANES, D_MODEL))], rider=rider)


def _mix_bwd(dx2, oproj, attn, z, vec, w_out, rider):
    t = dx2.shape[0]
    tm = min(TOKEN_TILE, t)
    nt = t // tm
    rev = lambda i: nt - 1 - i

    def body(dx2_ref, m_ref, a_ref, cb_ref, cc_ref, cx_ref, ga_ref, gc_ref, hc_ref, hx_ref,
             vec_ref, wo_ref, do_ref, da_ref, dr_ref, acc_ref, carry_ref):
        i = pl.program_id(0)

        @pl.when(i == 0)
        def _():
            acc_ref[...] = jnp.zeros_like(acc_ref)
            carry_ref[...] = jnp.zeros_like(carry_ref)

        ga1 = vec_ref[0:1, :]
        w0, w1, w2 = vec_ref[1:2, :], vec_ref[2:3, :], vec_ref[3:4, :]
        dx2 = dx2_ref[...].astype(F32)
        acc_ref[0:1, :] += jnp.sum(dx2 * m_ref[...].astype(F32), axis=0, keepdims=True)
        do = (dx2 * ga1).astype(BF16)
        do_ref[...] = do
        dm = lax.dot_general(do, wo_ref[...], NT_DIMS, preferred_element_type=F32)

        cc, cx, u, u1, u2 = _conv_inputs(cc_ref, cx_ref, hc_ref, hx_ref, i == nt - 1)
        cv = w0 * u2 + w1 * u1 + w2 * u
        cb = cb_ref[...].astype(F32)
        sa = _sigmoid(ga_ref[...].astype(F32))
        sc = _sigmoid(gc_ref[...].astype(F32))
        attn = a_ref[...].astype(F32)
        dattn = dm * sa
        da_ref[...] = dattn.astype(BF16)
        dconv = dm * sc
        dconv_b = dconv * cv
        dr_ref[:, 3 * D_MODEL:4 * D_MODEL] = (dattn * attn * (1.0 - sa)).astype(BF16)
        dr_ref[:, 4 * D_MODEL:5 * D_MODEL] = (dconv_b * cb * (1.0 - sc)).astype(BF16)
        dr_ref[:, 0:D_MODEL] = dconv_b.astype(BF16)
        dcv = dconv * cb
        acc_ref[1:2, :] += jnp.sum(dcv * u2, axis=0, keepdims=True)
        acc_ref[2:3, :] += jnp.sum(dcv * u1, axis=0, keepdims=True)
        acc_ref[3:4, :] += jnp.sum(dcv * u, axis=0, keepdims=True)
        nxt = carry_ref[...]
        du = w2 * dcv + w1 * _shift_up(dcv, nxt, 1) + w0 * _shift_up(dcv, nxt, 2)
        carry_ref[...] = dcv[0:SUBLANES, :]
        dr_ref[:, D_MODEL:2 * D_MODEL] = (du * cx).astype(BF16)
        dr_ref[:, 2 * D_MODEL:3 * D_MODEL] = (du * cc).astype(BF16)

    tok = pl.BlockSpec((tm, D_MODEL), lambda i: (rev(i), 0))
    return _call(
        body, "mix_bwd", (nt,), [dx2, oproj, attn, z, z, z, z, z, z, z, vec, w_out],
        [tok, tok, tok] + _z_specs(tm, rev) + [_full((SUBLANES, D_MODEL)), _full((D_MODEL, D_MODEL))],
        [jax.ShapeDtypeStruct((t, D_MODEL), BF16), jax.ShapeDtypeStruct((t, D_MODEL), BF16),
         jax.ShapeDtypeStruct((t, REST_WIDTH), BF16), jax.ShapeDtypeStruct((SUBLANES, D_MODEL), F32)],
        [tok, tok, pl.BlockSpec((tm, REST_WIDTH), lambda i: (rev(i), 0)), _full((SUBLANES, D_MODEL))],
        scratch=[pltpu.VMEM((SUBLANES, D_MODEL), F32)], rider=rider)


def _attn_bwd(z, dattn, attn, lse, sinks, rider):
    t = z.shape[0]
    tq = min(TOKEN_TILE, t)
    nblk = tq // WINDOW
    nt = t // tq

    def body(q_ref, kv_ref, do_ref, o_ref, lse_ref, sink_ref, dq_ref, dkv_ref, ds_ref, acc_ref, bias_ref):
        i = pl.program_id(0)

        @pl.when(i == 0)
        def _():
            acc_ref[...] = jnp.zeros_like(acc_ref)
            ds_ref[...] = jnp.zeros_like(ds_ref)
            _fill_window_bias(bias_ref)

        lane = lax.broadcasted_iota(jnp.int32, (1, LANES), 1)
        ind_row = lax.broadcasted_iota(jnp.int32, (SUBLANES, LANES), 0)
        ind_low = lax.broadcasted_iota(jnp.int32, (SUBLANES, LANES), 1) < HEAD_DIM
        indicator = jnp.where(jnp.logical_or(jnp.logical_and(ind_row == 0, ind_low),
                                             jnp.logical_and(ind_row == 1, jnp.logical_not(ind_low))),
                              1.0, 0.0).astype(BF16)
        low = lax.broadcasted_iota(jnp.int32, (2 * WINDOW, LANES), 1) < HEAD_DIM

        def both_heads(even, odd):
            picked = jnp.where(low, even, odd)
            return picked + jnp.concatenate([picked[:, HEAD_DIM:], picked[:, :HEAD_DIM]], axis=1)

        def window(b):
            row0 = pl.multiple_of(b * WINDOW, WINDOW)
            start = i * tq + b * WINDOW
            prev = pl.multiple_of(jnp.maximum(start - WINDOW, 0), WINDOW)
            cur = pl.multiple_of(start, WINDOW)
            kvw = jnp.concatenate([kv_ref[pl.ds(prev, WINDOW), :], kv_ref[pl.ds(cur, WINDOW), :]], axis=0)
            return (row0, prev, cur, _half_tiles(kvw[:, :KV_WIDTH]), _half_tiles(kvw[:, KV_WIDTH:]),
                    bias_ref[jnp.minimum(start, 1)])

        def block_group(bb, dsink):
            windows = [window(bb * ATTN_BWD_BLOCKS + n) for n in range(ATTN_BWD_BLOCKS)]
            dk_groups = [[] for _ in windows]
            dv_groups = [[] for _ in windows]
            for j in range(N_KV_HEADS):
                stacks, deltas, dq_ts = [], [], []
                for row0, _, _, _, _, _ in windows:
                    qst = _stack_pairs(q_ref, row0, j)
                    dost = _stack_pairs(do_ref, row0, j)
                    prod = dost.astype(F32) * _stack_pairs(o_ref, row0, j).astype(F32)
                    prod_hi = prod.astype(BF16)
                    prod_lo = (prod - prod_hi.astype(F32)).astype(BF16)
                    stacks.append((qst, dost))
                    deltas.append(lax.dot_general(indicator, prod_hi, NT_DIMS, preferred_element_type=F32)
                                  + lax.dot_general(indicator, prod_lo, NT_DIMS, preferred_element_type=F32))
                    dq_ts.append(jnp.zeros((LANES, STACK), F32))
                dk_par = [[] for _ in windows]
                dv_par = [[] for _ in windows]
                for parity in range(2):
                    heads = [j * GROUP + 2 * p + parity for p in range(PAIRS)]
                    sink = _per_pair_row([sink_ref[h] * LOG2E for h in heads])
                    for n, (row0, _, _, k_halves, v_halves, bias) in enumerate(windows):
                        qst, dost = stacks[n]
                        kk, vv = k_halves[j][parity], v_halves[j][parity]
                        s = lax.dot_general(kk, qst, NT_DIMS, preferred_element_type=F32) * SCORE_SCALE + bias
                        lse = jnp.concatenate([lse_ref[h:h + 1, pl.ds(row0, WINDOW)] for h in heads], axis=1)
                        p = jnp.exp2(s - lse)
                        dp = lax.dot_general(vv, dost, NT_DIMS, preferred_element_type=F32)
                        delta = deltas[n][parity:parity + 1, :]
                        dsb = (p * (dp - delta)).astype(BF16)
                        dq_ts[n] = dq_ts[n] + lax.dot_general(kk, dsb, TN_DIMS, preferred_element_type=F32)
                        dk_par[n].append(jnp.dot(dsb, qst, preferred_element_type=F32))
                        dv_par[n].append(jnp.dot(p.astype(BF16), dost, preferred_element_type=F32))
                        weighted = jnp.exp2(sink - lse) * delta
                        for pr, h in enumerate(heads):
                            dsink = dsink - jnp.where(
                                lane == h, jnp.sum(weighted[:, pr * WINDOW:(pr + 1) * WINDOW]), 0.0)
                for n, (row0, _, _, _, _, _) in enumerate(windows):
                    dq_st = jnp.transpose((dq_ts[n] * ATTN_SCALE).astype(BF16))
                    for pr in range(PAIRS):
                        dq_ref[pl.ds(row0, WINDOW), (j * PAIRS + pr) * LANES:(j * PAIRS + pr + 1) * LANES] = (
                            dq_st[pr * WINDOW:(pr + 1) * WINDOW, :])
                    dk_groups[n].append(both_heads(dk_par[n][0], dk_par[n][1]))
                    dv_groups[n].append(both_heads(dv_par[n][0], dv_par[n][1]))
            for n, (_, prev, cur, _, _, _) in enumerate(windows):
                blk = jnp.concatenate([jnp.where(low, dk_groups[n][0], dk_groups[n][1]) * ATTN_SCALE,
                                       jnp.where(low, dv_groups[n][0], dv_groups[n][1])], axis=1)
                acc_ref[pl.ds(prev, WINDOW), :] += blk[:WINDOW, :]
                acc_ref[pl.ds(cur, WINDOW), :] += blk[WINDOW:, :]
            return dsink

        dsink = lax.fori_loop(0, nblk // ATTN_BWD_BLOCKS, block_group, jnp.zeros((1, LANES), F32))
        ds_ref[0:1, :] += dsink

        @pl.when(i == nt - 1)
        def _():
            dkv_ref[...] = acc_ref[...].astype(BF16)

    tok = pl.BlockSpec((tq, D_MODEL), lambda i: (i, 0))
    return _call(
        body, "attn_bwd", (nt,), [z, z, dattn, attn, lse, sinks],
        [tok, pl.BlockSpec((t, 2 * KV_WIDTH), lambda i: (0, KV_COL // (2 * KV_WIDTH))), tok, tok,
         pl.BlockSpec((N_Q_HEADS, tq), lambda i: (0, i)), pl.BlockSpec(memory_space=pltpu.SMEM)],
        [jax.ShapeDtypeStruct((t, D_MODEL), BF16), jax.ShapeDtypeStruct((t, 2 * KV_WIDTH), BF16),
         jax.ShapeDtypeStruct((SUBLANES, LANES), F32)],
        [tok, _full((t, 2 * KV_WIDTH)), _full((SUBLANES, LANES))],
        scratch=[pltpu.VMEM((t, 2 * KV_WIDTH), F32), pltpu.VMEM((2, 2 * WINDOW, STACK), F32)], rider=rider)


def _inproj_bwd(dq, drest, dkv, x, dx2, vec, w_t, rider):
    t = x.shape[0]
    tm = min(TOKEN_TILE, t)

    def body(dq_ref, dr_ref, dkv_ref, x_ref, dx2_ref, vec_ref, w_ref, gx_ref, acc_ref, db_ref):
        @pl.when(pl.program_id(0) == 0)
        def _():
            acc_ref[...] = jnp.zeros_like(acc_ref)
            db_ref[...] = jnp.zeros_like(db_ref)

        g = vec_ref[0:1, :]
        sc1 = vec_ref[1:2, :]
        dqb, drb, dkvb = dq_ref[...], dr_ref[...], dkv_ref[...]
        dh = jnp.dot(dqb, w_ref[:REF_KV_COL, :], preferred_element_type=F32)
        dh = dh + jnp.dot(drb, w_ref[REF_REST_COL:, :], preferred_element_type=F32)
        dh = dh + jnp.dot(dkvb, w_ref[REF_KV_COL:REF_REST_COL, :], preferred_element_type=F32)
        db_ref[:, :REF_KV_COL] += jnp.sum(dqb.astype(F32), axis=0, keepdims=True)
        db_ref[:, REF_REST_COL:] += jnp.sum(drb.astype(F32), axis=0, keepdims=True)
        db_ref[:, REF_KV_COL:REF_REST_COL] += jnp.sum(dkvb.astype(F32), axis=0, keepdims=True)
        xf = x_ref[...]
        r = lax.rsqrt(jnp.mean(xf * xf, axis=-1, keepdims=True) + EPS)
        xn = xf * r
        gs = g * (1.0 + sc1)
        dh_xn = dh * xn
        sum_dh_xn = jnp.sum(dh_xn, axis=0, keepdims=True)
        acc_ref[0:1, :] += jnp.sum(dh, axis=0, keepdims=True)
        acc_ref[1:2, :] += sum_dh_xn * g
        acc_ref[2:3, :] += sum_dh_xn * (1.0 + sc1)
        gx_ref[...] = dx2_ref[...].astype(F32) + r * (dh * gs - xn * jnp.mean(dh_xn * gs, axis=-1, keepdims=True))

    tok = pl.BlockSpec((tm, D_MODEL), lambda i: (i, 0))
    return _call(
        body, "inproj_bwd", (t // tm,), [dq, drest, dkv, x, dx2, vec, w_t],
        [tok, pl.BlockSpec((tm, REST_WIDTH), lambda i: (i, 0)),
         pl.BlockSpec((tm, 2 * KV_WIDTH), lambda i: (i, 0)), tok, tok,
         _full((SUBLANES, D_MODEL)), _full((IN_WIDTH, D_MODEL))],
        [jax.ShapeDtypeStruct((t, D_MODEL), F32), jax.ShapeDtypeStruct((SUBLANES, D_MODEL), F32),
         jax.ShapeDtypeStruct((1, IN_WIDTH), F32)],
        [tok, _full((SUBLANES, D_MODEL)), _full((1, IN_WIDTH))], rider=rider)


def _weight_grad(b, a, name, bn, rows=None, row0=0, into=None, rider=None):
    pieces = list(b) if isinstance(b, (list, tuple)) else [b]
    widths = [p.shape[1] for p in pieces]
    t, n = pieces[0].shape[0], sum(widths)
    assert len(pieces) == 1 or bn == n
    m = a.shape[1]
    rows = n if rows is None else rows
    tk = min(TOKEN_TILE, t)
    for cand in (4 * TOKEN_TILE, 2 * TOKEN_TILE):
        if t % cand == 0 and 2 * cand * (bn + m) * 2 + bn * m * 4 <= WGRAD_VMEM:
            tk = cand
            break
    nk = t // tk
    block0 = row0 // bn

    npieces = len(pieces)

    def body(*refs):
        b_refs, a_ref = refs[:npieces], refs[npieces]
        out_ref, acc_ref = refs[-2:]
        k = pl.program_id(1)

        @pl.when(k == 0)
        def _():
            acc_ref[...] = jnp.zeros_like(acc_ref)

        if npieces == 1:
            acc_ref[...] += lax.dot_general(b_refs[0][...], a_ref[...], TN_DIMS, preferred_element_type=F32)
        else:
            lo = 0
            for b_ref, width in zip(b_refs, widths):
                acc_ref[lo:lo + width, :] += lax.dot_general(b_ref[...], a_ref[...], TN_DIMS,
                                                             preferred_element_type=F32)
                lo += width

        @pl.when(k == nk - 1)
        def _():
            out_ref[...] = acc_ref[...].astype(BF16)

    if npieces == 1:
        b_specs = [pl.BlockSpec((tk, bn), lambda j, k: (k, j))]
    else:
        b_specs = [pl.BlockSpec((tk, width), lambda j, k: (k, 0)) for width in widths]
    outs, routs = _call(
        body, name, (n // bn, nk), pieces + [a] + ([] if into is None else [into]),
        b_specs + [pl.BlockSpec((tk, m), lambda j, k: (k, 0))] + ([] if into is None else [ANY]),
        [jax.ShapeDtypeStruct((rows, m), BF16)], [pl.BlockSpec((bn, m), lambda j, k: (block0 + j, 0))],
        scratch=[pltpu.VMEM((bn, m), F32)], rider=rider, aliases=None if into is None else {npieces + 1: 0})
    return outs[0], routs


def _weight_grad_in(dq, dkv, drest, h1):
    t = h1.shape[0]
    tk = min(4 * TOKEN_TILE, t)
    nk = t // tk
    nblocks = IN_WIDTH // IN_CHUNK

    def body(dq_ref, dkv_ref, dr_ref, a_ref, out_ref, acc_ref):
        j, k = pl.program_id(0), pl.program_id(1)

        @pl.when(k == 0)
        def _():
            acc_ref[...] = jnp.zeros_like(acc_ref)

        @pl.when(j == 0)
        def _():
            acc_ref[:D_MODEL, :] += lax.dot_general(dq_ref[...], a_ref[...], TN_DIMS, preferred_element_type=F32)
            acc_ref[D_MODEL:, :] += lax.dot_general(dkv_ref[...], a_ref[...], TN_DIMS, preferred_element_type=F32)

        @pl.when(j > 0)
        def _():
            acc_ref[...] += lax.dot_general(dr_ref[...], a_ref[...], TN_DIMS, preferred_element_type=F32)

        @pl.when(k == nk - 1)
        def _():
            out_ref[...] = acc_ref[...].astype(BF16)

    first = lambda j, k: (jnp.where(j == 0, k, 0), 0)
    outs, _ = _call(
        body, "wgrad_in", (nblocks, nk), [dq, dkv, drest, h1],
        [pl.BlockSpec((tk, D_MODEL), first), pl.BlockSpec((tk, 2 * KV_WIDTH), first),
         pl.BlockSpec((tk, IN_CHUNK), lambda j, k: (jnp.where(j == 0, 0, k), jnp.maximum(j - 1, 0))),
         pl.BlockSpec((tk, D_MODEL), lambda j, k: (k, 0))],
        [jax.ShapeDtypeStruct((IN_WIDTH, D_MODEL), BF16)], [pl.BlockSpec((IN_CHUNK, D_MODEL), lambda j, k: (j, 0))],
        scratch=[pltpu.VMEM((IN_CHUNK, D_MODEL), F32)])
    return outs[0]


def _to_rows(v):
    n = v.shape[0]
    padded = -(-n // (SUBLANES * LANES)) * SUBLANES * LANES
    return jnp.pad(v, (0, padded - n)).reshape(padded // LANES, LANES)


def _vec_rows(*rows):
    stacked = jnp.concatenate([r.reshape(1, D_MODEL) for r in rows], axis=0)
    return jnp.pad(stacked, ((0, SUBLANES - len(rows)), (0, 0)))


def kernel(x, c, w_ada, b_ada, g_mix, w_in, b_in, sinks, conv_w, w_out, g_ffn, w_ffn_in, w_ffn_out, g_final, loss_target, m_w_ada, m_b_ada, m_g_mix, m_w_in, m_b_in, m_sinks, m_conv_w, m_w_out, m_g_ffn, m_w_ffn_in, m_w_ffn_out, m_g_final, v_w_ada, v_b_ada, v_g_mix, v_w_in, v_b_in, v_sinks, v_conv_w, v_w_out, v_g_ffn, v_w_ffn_in, v_w_ffn_out, v_g_final):
    ix, iy, ic = _my_place()
    me = 4 * ix + 2 * iy + ic
    xs = x[0]
    target = loss_target[0]
    ada_cols = w_ada.shape[2]
    conv_cols = conv_w.shape[2]

    wt_in, wt_fi = jnp.transpose(w_in[0]), jnp.transpose(w_ffn_in[0])
    b_cols = lax.dynamic_slice_in_dim(b_ada, me * ada_cols, ada_cols, axis=1)
    g_in, (cast_fi, cast_out, cast_fo), first, mod_all = _gather_first_weight(
        wt_in, [wt_fi, w_out[0], w_ffn_out[0]], _to_rows(jnp.concatenate([c[0], conv_w[0].reshape(-1)])),
        w_ada[0], b_cols)
    first = first.reshape(N_DEV, -1)
    c_all = first[:, :D_MODEL]
    conv_full = jnp.transpose(first[:, D_MODEL:D_MODEL + 3 * conv_cols].reshape(N_DEV, 3, conv_cols), (1, 0, 2))
    conv_full = conv_full.reshape(3, D_MODEL)
    mod = lax.dynamic_index_in_dim(mod_all, me, axis=1, keepdims=False).reshape(N_MOD, D_MODEL)
    sh1, sc1, ga1, sh2, sc2, ga2 = [mod[i:i + 1] for i in range(N_MOD)]
    w_in_t = g_in.reshape(IN_WIDTH, D_MODEL)
    (z, h1), (g_fi, g_out) = _inproj_fwd(xs, _vec_rows(g_mix, sc1, sh1), w_in_t, b_in,
                                         _gather_rider([cast_fi, cast_out]))
    w_fi_t = g_fi.reshape(2 * D_FF, D_MODEL)
    w_out_full = g_out.reshape(D_MODEL, D_MODEL)
    (attn, lse), (g_fo,) = _attn_fwd(z, sinks[0], _gather_rider([cast_fo]))
    w_fo_full = g_fo.reshape(D_FF, D_MODEL)
    merged, x2, h2, oproj = _mix_fwd(
        xs, attn, z, _vec_rows(ga1, g_ffn, sc2, sh2, conv_full[0], conv_full[1], conv_full[2]), w_out_full)
    gu, act = _ffn_fwd(h2, w_fi_t)
    dx3, df, dgu, acc_l = _ffn_out_loss(act, gu, x2, target, _vec_rows(ga2, g_final), w_fo_full)

    gw_fo, _ = _weight_grad(act, df, "wgrad_ffn_out", D_FF)
    gw_fi, _ = _weight_grad(dgu, h2, "wgrad_ffn_in", D_FF)
    blocks_fo = gw_fo.reshape(N_DEV, D_FF // N_DEV, D_MODEL)
    blocks_fi = gw_fi.reshape(N_DEV, 2 * D_FF // N_DEV, D_MODEL)
    (dx2, acc_f), (sib_fo, sib_fi) = _ffn_in_bwd(dgu, x2, dx3, _vec_rows(g_ffn, sc2), w_fi_t,
                                                 _sibling_rider([blocks_fo, blocks_fi]))
    sums_fo, mine_fo = _sibling_sum(blocks_fo, sib_fo, "sibling_sum_ffn_out")
    sums_fi, mine_fi = _sibling_sum(blocks_fi, sib_fi, "sibling_sum_ffn_in")
    (dout, dattn, drest, acc_m), (ici_fo, ici_fi) = _mix_bwd(
        dx2, oproj, attn, z, _vec_rows(ga1, conv_full[0], conv_full[1], conv_full[2]), w_out_full,
        _chip_rider([sums_fo, sums_fi]))
    gw_out, _ = _weight_grad(merged, dout, "wgrad_out", D_MODEL)
    blocks_out = gw_out.reshape(N_DEV, D_MODEL // N_DEV, D_MODEL)
    (dq, dkv, dsink), _ = _attn_bwd(z, dattn, attn, lse, sinks[0], None)
    blocks_in = _weight_grad_in(dq, dkv, drest, h1).reshape(N_DEV, IN_WIDTH // N_DEV, D_MODEL)
    (sums_in, mine_in), (sums_out, mine_out) = _sibling_exchange_sum([blocks_in, blocks_out], "sibling_w_in_out")
    (grad_x, acc_i, db_in), (ici_in, ici_out) = _inproj_bwd(dq, drest, dkv, xs, dx2, _vec_rows(g_mix, sc1), w_in_t,
                                                            _chip_rider([sums_in, sums_out]))

    widen = lambda vec: jnp.pad(vec, (0, -vec.shape[0] % D_MODEL))
    packed = jnp.concatenate([
        acc_i[0], acc_i[1], acc_m[0], acc_f[0], acc_f[1], acc_l[2],
        acc_i[2], widen(db_in[0]), acc_f[2], acc_l[1],
        acc_m[1], acc_m[2], acc_m[3], widen(dsink[0]), acc_l[0],
        jnp.zeros(((PACK_ROWS - PACK_SQERR - 1) * D_MODEL,), F32)]).reshape(PACK_ROWS, D_MODEL)
    packed_all = _small_allgather(packed, "gather_small")
    dmod_all = packed_all[:, PACK_DMOD:PACK_DMOD + N_MOD, :].reshape(N_DEV, N_MOD * D_MODEL)
    dmod_cols = lax.dynamic_slice_in_dim(dmod_all, me * ada_cols, ada_cols, axis=1)
    g_w_ada, d_ada, nm_ada, nv_ada = _ada_grad_adamw(c_all, dmod_cols, w_ada[0], m_w_ada[0], v_w_ada[0])
    row_of = lambda a: a.reshape(1, -1)
    small, g_conv_full, loss = _small_finalize(packed_all, {
        "b_ada": (b_ada, m_b_ada, v_b_ada), "g_mix": (g_mix, m_g_mix, v_g_mix), "b_in": (b_in, m_b_in, v_b_in),
        "g_ffn": (g_ffn, m_g_ffn, v_g_ffn), "sinks": (sinks, m_sinks, v_sinks),
        "g_final": (row_of(g_final), row_of(m_g_final), row_of(v_g_final))})
    small["g_final"] = tuple(o.reshape(g_final.shape) for o in small["g_final"])
    g_conv = lax.dynamic_slice_in_dim(g_conv_full, me * conv_cols, conv_cols, axis=1)
    d_conv, nm_conv, nv_conv = _adamw(conv_w[0], g_conv, m_conv_w[0], v_conv_w[0], "adamw_conv_w")
    small["conv_w"] = (g_conv[None], d_conv[None], nm_conv[None], nv_conv[None])

    def reduced(mine, ici, w, m, v, name, transposed=False):
        turn = jnp.transpose if transposed else (lambda a: a)
        return tuple(turn(o)[None] for o in _chip_sum_adamw(mine, ici, turn(w[0]), turn(m[0]), turn(v[0]), name))

    res = {
        "w_ada": (g_w_ada[None], d_ada[None], nm_ada[None], nv_ada[None]),
        "w_in": reduced(mine_in, ici_in, w_in, m_w_in, v_w_in, "adamw_w_in", transposed=True),
        "w_out": reduced(mine_out, ici_out, w_out, m_w_out, v_w_out, "adamw_w_out"),
        "w_ffn_in": reduced(mine_fi, ici_fi, w_ffn_in, m_w_ffn_in, v_w_ffn_in, "adamw_w_ffn_in", transposed=True),
        "w_ffn_out": reduced(mine_fo, ici_fo, w_ffn_out, m_w_ffn_out, v_w_ffn_out, "adamw_w_ffn_out"),
    }
    res.update(small)
    order = ["w_ada", "b_ada", "g_mix", "w_in", "b_in", "sinks", "conv_w", "w_out", "g_ffn", "w_ffn_in", "w_ffn_out",
             "g_final"]
    outs = [loss.reshape(()), grad_x[None]]
    for k in range(4):
        outs += [res[n][k] for n in order]
    return tuple(outs)
```

```python
import functools
import math

import jax
import jax.numpy as jnp
from jax import lax
from jax.experimental import pallas as pl
from jax.experimental.pallas import tpu as pltpu

F32 = jnp.float32
BF16 = jnp.bfloat16
GRAD_STREAM = F32

D_MODEL = 1024
HEAD_DIM = 64
N_Q_HEADS = 16
N_KV_HEADS = 2
GROUP = 8
WINDOW = 128
KV_WIDTH = N_KV_HEADS * HEAD_DIM
D_FF = 2816
IN_WIDTH = 6400
N_MOD = 6
EPS = 1e-6
N_DEV = 8
REST_WIDTH = 5 * D_MODEL
KV_COL = D_MODEL + REST_WIDTH
ATTN_SCALE = HEAD_DIM ** -0.5

ADAM_LR = 0.001
ADAM_B1 = 0.9
ADAM_B2 = 0.999
ADAM_EPS = 1e-08
ADAM_WD = 0.01
ADAM_STEP = 10

LANES = 128
SUBLANES = 8
BF16_ROWS = 16
VMEM_LIMIT = 56 * 1024 * 1024
TOKEN_TILE = 512
FF_CHUNK = 256
ROW_PARTS = 2
MIN_STREAM_STEPS = 2
WGRAD_VMEM = 40 * 1024 * 1024
MESH = pl.DeviceIdType.MESH
ANY = pl.BlockSpec(memory_space=pl.ANY)

NT_DIMS = (((1,), (1,)), ((), ()))
TN_DIMS = (((0,), (0,)), ((), ()))
CHIP_FLIPS = [(0, 0), (1, 0), (0, 1), (1, 1)]


def _full(shape):
    return pl.BlockSpec(shape, lambda *_: (0,) * len(shape))


def _my_place():
    return lax.axis_index("x"), lax.axis_index("y"), lax.axis_index("c")


def _flip(v, bit):
    return 1 - v if bit else v


def _sigmoid(v):
    return 1.0 / (1.0 + jnp.exp2(v * (-1.4426950408889634)))


class _Rider:
    def __init__(self, ins, out_shapes, sem_shapes, first=None, mid=None, last=None, ins_in_vmem=False):
        self.ins, self.out_shapes, self.sem_shapes = list(ins), list(out_shapes), list(sem_shapes)
        self.in_specs = [_full(a.shape) if ins_in_vmem else ANY for a in self.ins]
        self.hooks = [(when, fn) for when, fn in (("first", first), ("mid", mid), ("last", last)) if fn is not None]


def _call(body, name, grid, args, in_specs, out_shape, out_specs, scratch=(), rider=None, aliases=None):
    n_in, n_out, n_scr = len(args), len(out_shape), len(scratch)
    r_in = rider.ins if rider else []
    r_out = rider.out_shapes if rider else []
    r_sem = rider.sem_shapes if rider else []
    nsteps = math.prod(grid)

    def full_body(*refs):
        pos = 0
        groups = []
        for size in (n_in, len(r_in), n_out, len(r_out), n_scr, len(r_sem)):
            groups.append(refs[pos:pos + size])
            pos += size
        ins, rins, outs, routs, scr, rsems = groups
        step = pl.program_id(0)
        for axis in range(1, len(grid)):
            step = step * grid[axis] + pl.program_id(axis)
        at = {"first": 0, "mid": (3 * nsteps) // 4, "last": nsteps - 1}
        hooks = rider.hooks if rider else []
        for when, fn in hooks:
            if when != "last":
                pl.when(step == at[when])(functools.partial(fn, rins, routs, rsems))
        body(*ins, *outs, *scr)
        for when, fn in hooks:
            if when == "last":
                pl.when(step == at[when])(functools.partial(fn, rins, routs, rsems))

    outs = pl.pallas_call(
        full_body, name=name, grid=grid,
        out_shape=list(out_shape) + list(r_out),
        in_specs=list(in_specs) + (rider.in_specs if rider else []),
        out_specs=list(out_specs) + [ANY] * len(r_out),
        scratch_shapes=list(scratch) + list(r_sem),
        input_output_aliases=dict(aliases or {}),
        compiler_params=pltpu.CompilerParams(dimension_semantics=("arbitrary",) * len(grid),
                                             vmem_limit_bytes=VMEM_LIMIT),
    )(*args, *r_in)
    return list(outs[:n_out]), list(outs[n_out:])


def _gather_rider(shards):
    n = len(shards)

    def setup(outs, sems):
        x, y, c = _my_place()
        send_sems, recv_sems, _ = sems
        chips = [(1 - x, y), (x, 1 - y), (1 - x, 1 - y)]

        def block(w, place):
            return outs[w].at[4 * place[0] + 2 * place[1] + place[2]]

        def copy(w, k, place, to, src=None):
            return pltpu.make_async_remote_copy(
                src_ref=block(w, place) if src is None else src, dst_ref=block(w, place),
                send_sem=send_sems.at[w, k], recv_sem=recv_sems.at[w, k], device_id=to, device_id_type=MESH)

        return (x, y, c), (x, y, 1 - c), chips, block, copy

    def first(ins, outs, sems):
        me, sibling, chips, block, copy = setup(outs, sems)
        for w in range(n):
            pltpu.make_async_copy(ins[w], block(w, me), sems[2].at[w]).start()
            copy(w, 0, me, sibling, src=ins[w]).start()
            for j, chip in enumerate(chips):
                copy(w, 1 + j, me, (*chip, me[2]), src=ins[w]).start()

    def mid(ins, outs, sems):
        me, sibling, chips, block, copy = setup(outs, sems)
        for w in range(n):
            for j, chip in enumerate(chips):
                copy(w, 1 + j, (*chip, me[2]), me).wait_recv()
                copy(w, 4 + j, (*chip, me[2]), sibling).start()

    def last(ins, outs, sems):
        me, sibling, chips, block, copy = setup(outs, sems)
        for w in range(n):
            copy(w, 0, sibling, me).wait_recv()
            for j, chip in enumerate(chips):
                copy(w, 4 + j, (*chip, 1 - me[2]), me).wait_recv()
            copy(w, 0, me, sibling, src=ins[w]).wait_send()
            for j, chip in enumerate(chips):
                copy(w, 1 + j, me, (*chip, me[2]), src=ins[w]).wait_send()
                copy(w, 4 + j, (*chip, me[2]), sibling).wait_send()
            pltpu.make_async_copy(ins[w], block(w, me), sems[2].at[w]).wait()

    return _Rider(
        shards, [jax.ShapeDtypeStruct((N_DEV,) + s.shape, BF16) for s in shards],
        [pltpu.SemaphoreType.DMA((n, N_DEV - 1)), pltpu.SemaphoreType.DMA((n, N_DEV - 1)),
         pltpu.SemaphoreType.DMA((n,))],
        first=first, mid=mid, last=last, ins_in_vmem=True)


def _sibling_rider(gblocks):
    n = len(gblocks)

    def copies(ins, outs, sems):
        x, y, c = _my_place()
        send_sems, recv_sems = sems
        made = []
        for w in range(n):
            for f, (fx, fy) in enumerate(CHIP_FLIPS):
                chip = 4 * _flip(x, fx) + 2 * _flip(y, fy)
                made.append(pltpu.make_async_remote_copy(
                    src_ref=ins[w].at[chip + 1 - c], dst_ref=outs[w].at[f], send_sem=send_sems.at[w, f],
                    recv_sem=recv_sems.at[w, f], device_id=(x, y, 1 - c), device_id_type=MESH))
        return made

    def first(ins, outs, sems):
        for cp in copies(ins, outs, sems):
            cp.start()

    def last(ins, outs, sems):
        for cp in copies(ins, outs, sems):
            cp.wait_recv()
            cp.wait_send()

    return _Rider(gblocks, [jax.ShapeDtypeStruct((4,) + g.shape[1:], BF16) for g in gblocks],
                  [pltpu.SemaphoreType.DMA((n, 4))] * 2, first=first, last=last)


def _chip_rider(sums):
    n = len(sums)

    def copies(ins, outs, sems):
        x, y, c = _my_place()
        send_sems, recv_sems = sems
        made = []
        for w in range(n):
            for f in (1, 2, 3):
                fx, fy = CHIP_FLIPS[f]
                made.append(pltpu.make_async_remote_copy(
                    src_ref=ins[w].at[f - 1], dst_ref=outs[w].at[f - 1], send_sem=send_sems.at[w, f - 1],
                    recv_sem=recv_sems.at[w, f - 1], device_id=(_flip(x, fx), _flip(y, fy), c), device_id_type=MESH))
        return made

    def first(ins, outs, sems):
        for cp in copies(ins, outs, sems):
            cp.start()

    def last(ins, outs, sems):
        for cp in copies(ins, outs, sems):
            cp.wait_recv()
            cp.wait_send()

    return _Rider(sums, [jax.ShapeDtypeStruct(s.shape, BF16) for s in sums],
                  [pltpu.SemaphoreType.DMA((n, 3))] * 2, first=first, last=last)


def _push_to_all(v_ref, out_ref, send_sems, recv_sems, local_sem, wait=True):
    x, y, c = _my_place()
    me = 4 * x + 2 * y + c
    mine = pltpu.make_async_copy(v_ref, out_ref.at[me], local_sem)
    mine.start()
    sends = []
    for k in range(1, N_DEV):
        px, py, pc = _flip(x, k & 4), _flip(y, k & 2), _flip(c, k & 1)
        cp = pltpu.make_async_remote_copy(
            src_ref=v_ref, dst_ref=out_ref.at[me], send_sem=send_sems.at[k - 1], recv_sem=recv_sems.at[k - 1],
            device_id=(px, py, pc), device_id_type=MESH)
        cp.start()
        sends.append(cp)

    def finish():
        for k in range(1, N_DEV):
            px, py, pc = _flip(x, k & 4), _flip(y, k & 2), _flip(c, k & 1)
            pltpu.make_async_remote_copy(
                src_ref=v_ref, dst_ref=out_ref.at[4 * px + 2 * py + pc], send_sem=send_sems.at[k - 1],
                recv_sem=recv_sems.at[k - 1], device_id=(px, py, pc), device_id_type=MESH).wait_recv()
        for cp in sends:
            cp.wait_send()
        mine.wait()

    if wait:
        finish()
    return finish


def _small_allgather(v, name):
    def body(v_ref, out_ref, send_sems, recv_sems, local_sem):
        _push_to_all(v_ref, out_ref, send_sems, recv_sems, local_sem)

    return pl.pallas_call(
        body, name=name,
        out_shape=jax.ShapeDtypeStruct((N_DEV,) + v.shape, F32),
        in_specs=[pl.BlockSpec(memory_space=pltpu.VMEM)],
        out_specs=pl.BlockSpec(memory_space=pltpu.VMEM),
        scratch_shapes=[pltpu.SemaphoreType.DMA((N_DEV - 1,)), pltpu.SemaphoreType.DMA((N_DEV - 1,)),
                        pltpu.SemaphoreType.DMA],
        compiler_params=pltpu.CompilerParams(vmem_limit_bytes=VMEM_LIMIT),
    )(v)


def _gather_first_weight(shard, others, cond_rows, w_ada, b_cols):
    n = len(others)
    ada_cols = w_ada.shape[1]
    c_rows = D_MODEL // LANES

    def body(*refs):
        w_ref, other_refs = refs[0], refs[1:1 + n]
        cond_ref, wada_ref, bcols_ref = refs[1 + n:4 + n]
        out_ref, cast_refs = refs[4 + n], refs[5 + n:5 + 2 * n]
        cond_all_ref, mod_all_ref = refs[5 + 2 * n:7 + 2 * n]
        mine_ref, mod_ref, send_sems, recv_sems, local_sem, small_send, small_recv, small_local = refs[7 + 2 * n:]
        x, y, c = _my_place()
        me, sibling = (x, y, c), (x, y, 1 - c)
        xnb, ynb, diag = (1 - x, y), (x, 1 - y), (1 - x, 1 - y)
        half = shard.shape[0] // 2

        def block(place, part=None):
            ref = out_ref.at[4 * place[0] + 2 * place[1] + place[2]]
            return ref if part is None else ref.at[pl.ds(part * half, half)]

        def copy(k, place, to, part=None, src=None):
            return pltpu.make_async_remote_copy(
                src_ref=block(place, part) if src is None else src, dst_ref=block(place, part),
                send_sem=send_sems.at[k], recv_sem=recv_sems.at[k], device_id=to, device_id_type=MESH)

        finish_cond = _push_to_all(cond_ref, cond_all_ref, small_send.at[0], small_recv.at[0], small_local.at[0],
                                   wait=False)
        mine_ref[...] = w_ref[...].astype(BF16)
        local = pltpu.make_async_copy(mine_ref, block(me), local_sem)
        local.start()
        started = [copy(0, me, sibling, src=mine_ref), copy(1, me, (*xnb, c), src=mine_ref),
                   copy(2, me, (*ynb, c), src=mine_ref)]
        for cp in started:
            cp.start()
        finish_cond()
        mod = jnp.zeros((N_DEV, ada_cols), F32) + bcols_ref[...]
        for r in range(c_rows):
            cf = cond_all_ref[:, r, :]
            act = (cf * _sigmoid(cf)).astype(BF16)
            mod = mod + jnp.dot(act, wada_ref[r * LANES:(r + 1) * LANES, :].astype(BF16),
                                preferred_element_type=F32)
        mod_ref[...] = mod
        finish_mod = _push_to_all(mod_ref, mod_all_ref, small_send.at[1], small_recv.at[1], small_local.at[1],
                                  wait=False)
        for o_ref, c_ref in zip(other_refs, cast_refs):
            c_ref[...] = o_ref[...].astype(BF16)
        def start(cp):
            cp.start()
            started.append(cp)

        copy(1, (*xnb, c), me).wait_recv()
        start(copy(3, (*xnb, c), (*ynb, c), part=0))
        start(copy(5, (*xnb, c), sibling))
        copy(2, (*ynb, c), me).wait_recv()
        start(copy(4, (*ynb, c), (*xnb, c), part=1))
        start(copy(6, (*ynb, c), sibling))
        copy(3, (*diag, c), me, part=0).wait_recv()
        start(copy(7, (*diag, c), sibling, part=0))
        copy(4, (*diag, c), me, part=1).wait_recv()
        start(copy(8, (*diag, c), sibling, part=1))
        copy(0, sibling, me).wait_recv()
        copy(5, (*xnb, 1 - c), me).wait_recv()
        copy(6, (*ynb, 1 - c), me).wait_recv()
        copy(7, (*diag, 1 - c), me, part=0).wait_recv()
        copy(8, (*diag, 1 - c), me, part=1).wait_recv()
        finish_mod()
        for cp in started:
            cp.wait_send()
        local.wait()

    vmem = pl.BlockSpec(memory_space=pltpu.VMEM)
    outs = pl.pallas_call(
        body, name="gather_w_in",
        out_shape=[jax.ShapeDtypeStruct((N_DEV,) + shard.shape, BF16)]
        + [jax.ShapeDtypeStruct(o.shape, BF16) for o in others]
        + [jax.ShapeDtypeStruct((N_DEV,) + cond_rows.shape, F32), jax.ShapeDtypeStruct((N_DEV, N_DEV, ada_cols), F32)],
        in_specs=[vmem] * (4 + n),
        out_specs=[ANY] + [vmem] * (n + 2),
        scratch_shapes=[pltpu.VMEM(shard.shape, BF16), pltpu.VMEM((N_DEV, ada_cols), F32),
                        pltpu.SemaphoreType.DMA((9,)), pltpu.SemaphoreType.DMA((9,)),
                        pltpu.SemaphoreType.DMA,
                        pltpu.SemaphoreType.DMA((2, N_DEV - 1)), pltpu.SemaphoreType.DMA((2, N_DEV - 1)),
                        pltpu.SemaphoreType.DMA((2,))],
        compiler_params=pltpu.CompilerParams(vmem_limit_bytes=VMEM_LIMIT),
    )(shard, *others, cond_rows, w_ada, b_cols)
    return outs[0], list(outs[1:1 + n]), outs[1 + n], outs[2 + n]


def _sibling_exchange_sum(gblocks, name):
    n = len(gblocks)

    def body(*refs):
        g_refs, out_refs = refs[:n], refs[n:3 * n]
        bufs = refs[3 * n:5 * n]
        own_sems, send_sems, recv_sems = refs[5 * n:]
        x, y, c = _my_place()
        pairs = []
        for w in range(n):
            own_buf, sib_buf = bufs[2 * w], bufs[2 * w + 1]
            for f, (fx, fy) in enumerate(CHIP_FLIPS):
                chip = 4 * _flip(x, fx) + 2 * _flip(y, fy)
                own = pltpu.make_async_copy(g_refs[w].at[chip + c], own_buf.at[f], own_sems.at[w, f])
                own.start()
                remote = pltpu.make_async_remote_copy(
                    src_ref=g_refs[w].at[chip + 1 - c], dst_ref=sib_buf.at[f], send_sem=send_sems.at[w, f],
                    recv_sem=recv_sems.at[w, f], device_id=(x, y, 1 - c), device_id_type=MESH)
                remote.start()
                pairs.append((own, remote))
        for w in range(n):
            own_buf, sib_buf = bufs[2 * w], bufs[2 * w + 1]
            sums_ref, mine_ref = out_refs[2 * w], out_refs[2 * w + 1]
            for f in (1, 2, 3, 0):
                own, remote = pairs[4 * w + f]
                own.wait()
                remote.wait_recv()
                total = own_buf[f].astype(F32) + sib_buf[f].astype(F32)
                if f == 0:
                    mine_ref[...] = total
                else:
                    sums_ref[f - 1] = total.astype(BF16)
        for _, remote in pairs:
            remote.wait_send()

    vmem = pl.BlockSpec(memory_space=pltpu.VMEM)
    out_shape, scratch = [], []
    for g in gblocks:
        out_shape += [jax.ShapeDtypeStruct((3,) + g.shape[1:], BF16), jax.ShapeDtypeStruct(g.shape[1:], F32)]
        scratch += [pltpu.VMEM((4,) + g.shape[1:], BF16)] * 2
    outs = pl.pallas_call(
        body, name=name, out_shape=out_shape,
        in_specs=[ANY] * n, out_specs=[vmem] * (2 * n),
        scratch_shapes=scratch + [pltpu.SemaphoreType.DMA((n, 4))] * 3,
        compiler_params=pltpu.CompilerParams(vmem_limit_bytes=VMEM_LIMIT),
    )(*gblocks)
    return [(outs[2 * w], outs[2 * w + 1]) for w in range(n)]


def _ada_weight_grad(c_all, dmod_cols):
    cols = dmod_cols.shape[1]

    def body(c_ref, d_ref, out_ref):
        cf = c_ref[...]
        act = (cf * _sigmoid(cf)).astype(BF16)
        out_ref[...] = lax.dot_general(act, d_ref[...].astype(BF16), TN_DIMS, preferred_element_type=F32)

    return pl.pallas_call(
        body, name="ada_weight_grad",
        out_shape=jax.ShapeDtypeStruct((D_MODEL, cols), F32),
        in_specs=[pl.BlockSpec(memory_space=pltpu.VMEM)] * 2,
        out_specs=pl.BlockSpec(memory_space=pltpu.VMEM),
        compiler_params=pltpu.CompilerParams(vmem_limit_bytes=VMEM_LIMIT),
    )(c_all, dmod_cols)


PACK_ROWS = 24
PACK_DMOD = 0
PACK_PARAMS = {"g_mix": (6, D_MODEL), "b_in": (7, IN_WIDTH), "g_ffn": (14, D_MODEL), "g_final": (15, D_MODEL),
               "sinks": (19, N_Q_HEADS)}
PACK_CONV = 16
PACK_SQERR = 20


def _small_finalize(packed_all, params):
    names = ["b_ada"] + list(PACK_PARAMS)
    layout = dict(PACK_PARAMS, b_ada=(PACK_DMOD, N_MOD * D_MODEL))
    n = len(names)

    def body(*refs):
        p_ref = refs[0]
        ins = refs[1:1 + 3 * n]
        outs = refs[1 + 3 * n:1 + 7 * n]
        conv_ref, loss_ref = refs[1 + 7 * n:]
        total = p_ref[0]
        for d in range(1, N_DEV):
            total = total + p_ref[d]
        for k, name in enumerate(names):
            row0, width = layout[name]
            w_ref, m_ref, v_ref = ins[3 * k:3 * k + 3]
            g_ref, d_ref, nm_ref, nv_ref = outs[4 * k:4 * k + 4]
            for chunk in range(-(-width // D_MODEL)):
                lo = chunk * D_MODEL
                hi = min(lo + D_MODEL, width)
                g = total[row0 + chunk:row0 + chunk + 1, :hi - lo]
                g_ref[:, lo:hi] = g
                d_ref[:, lo:hi], nm_ref[:, lo:hi], nv_ref[:, lo:hi] = _adamw_update(
                    w_ref[:, lo:hi], g, m_ref[:, lo:hi], v_ref[:, lo:hi])
        conv_ref[...] = total[PACK_CONV:PACK_CONV + 3, :]
        loss_ref[...] = (0.5 / D_MODEL) * jnp.sum(total[PACK_SQERR:PACK_SQERR + 1, :], keepdims=True)

    vmem = pl.BlockSpec(memory_space=pltpu.VMEM)
    flat = [a for name in names for a in params[name]]
    out_shape = [jax.ShapeDtypeStruct(params[name][0].shape, F32) for name in names for _ in range(4)]
    outs = pl.pallas_call(
        body, name="small_finalize",
        out_shape=out_shape + [jax.ShapeDtypeStruct((3, D_MODEL), F32), jax.ShapeDtypeStruct((1, 1), F32)],
        in_specs=[vmem] * (1 + 3 * n),
        out_specs=[vmem] * (4 * n + 2),
        compiler_params=pltpu.CompilerParams(vmem_limit_bytes=VMEM_LIMIT),
    )(packed_all, *flat)
    return {name: tuple(outs[4 * k:4 * k + 4]) for k, name in enumerate(names)}, outs[4 * n], outs[4 * n + 1]


def _row_tile(rows, multiple):
    for cand in range(rows // MIN_STREAM_STEPS, 0, -1):
        if rows % cand == 0 and cand % multiple == 0:
            return cand
    return rows


def _adamw_update(w, g, m, v):
    c1 = 1.0 / (1.0 - ADAM_B1 ** ADAM_STEP)
    c2 = 1.0 / (1.0 - ADAM_B2 ** ADAM_STEP)
    nm = ADAM_B1 * m + (1.0 - ADAM_B1) * g
    nv = ADAM_B2 * v + (1.0 - ADAM_B2) * (g * g)
    delta = -ADAM_LR * ((nm * c1) / (jnp.sqrt(nv * c2) + ADAM_EPS) + ADAM_WD * w)
    return delta, nm, nv


def _adamw(w, g, m, v, name):
    rows, cols = w.shape
    tile = _row_tile(rows, SUBLANES)

    def body(w_ref, g_ref, m_ref, v_ref, d_ref, nm_ref, nv_ref):
        d_ref[...], nm_ref[...], nv_ref[...] = _adamw_update(w_ref[...], g_ref[...], m_ref[...], v_ref[...])

    spec = pl.BlockSpec((tile, cols), lambda i: (i, 0))
    outs, _ = _call(body, name, (rows // tile,), [w, g, m, v], [spec] * 4,
                    [jax.ShapeDtypeStruct((rows, cols), F32)] * 3, [spec] * 3)
    return outs


def _sibling_sum(gblocks, sib, name):
    _, r, cdim = gblocks.shape
    tile = _row_tile(r, BF16_ROWS)
    x, y, c = _my_place()
    table = jnp.stack([4 * _flip(x, fx) + 2 * _flip(y, fy) + c for fx, fy in CHIP_FLIPS]).astype(jnp.int32)

    def body(table_ref, own0, own1, own2, own3, sib_ref, sums_ref, mine_ref):
        mine_ref[...] = own0[...].astype(F32) + sib_ref[0].astype(F32)
        for f, own in ((1, own1), (2, own2), (3, own3)):
            sums_ref[f - 1] = (own[...].astype(F32) + sib_ref[f].astype(F32)).astype(BF16)

    own_specs = [pl.BlockSpec((None, tile, cdim), functools.partial(lambda i, tab, f: (tab[f], i, 0), f=f))
                 for f in range(4)]
    return pl.pallas_call(
        body, name=name,
        grid_spec=pltpu.PrefetchScalarGridSpec(
            num_scalar_prefetch=1, grid=(r // tile,),
            in_specs=own_specs + [pl.BlockSpec((4, tile, cdim), lambda i, tab: (0, i, 0))],
            out_specs=[pl.BlockSpec((3, tile, cdim), lambda i, tab: (0, i, 0)),
                       pl.BlockSpec((tile, cdim), lambda i, tab: (i, 0))]),
        out_shape=[jax.ShapeDtypeStruct((3, r, cdim), BF16), jax.ShapeDtypeStruct((r, cdim), F32)],
        compiler_params=pltpu.CompilerParams(dimension_semantics=("arbitrary",), vmem_limit_bytes=VMEM_LIMIT),
    )(table, gblocks, gblocks, gblocks, gblocks, sib)


def _chip_sum_adamw(mine, ici, w, m, v, name):
    r, cdim = mine.shape
    tile = _row_tile(r, BF16_ROWS)

    def body(mine_ref, ici_ref, w_ref, m_ref, v_ref, g_ref, d_ref, nm_ref, nv_ref):
        g = mine_ref[...]
        for f in range(3):
            g = g + ici_ref[f].astype(F32)
        g_ref[...] = g
        d_ref[...], nm_ref[...], nv_ref[...] = _adamw_update(w_ref[...], g, m_ref[...], v_ref[...])

    spec = pl.BlockSpec((tile, cdim), lambda i: (i, 0))
    outs, _ = _call(
        body, name, (r // tile,), [mine, ici, w, m, v],
        [spec, pl.BlockSpec((3, tile, cdim), lambda i: (0, i, 0)), spec, spec, spec],
        [jax.ShapeDtypeStruct((r, cdim), F32)] * 4, [spec] * 4)
    return outs


REF_KV_COL = D_MODEL
REF_REST_COL = D_MODEL + 2 * KV_WIDTH
IN_CHUNK = 1280
IN_PIECES = ([(0, 0, D_MODEL)]
             + [(D_MODEL + n * IN_CHUNK, REF_REST_COL + n * IN_CHUNK, IN_CHUNK) for n in range(REST_WIDTH // IN_CHUNK)]
             + [(KV_COL, REF_KV_COL, 2 * KV_WIDTH)])


def _inproj_fwd(x, vec, w_t, b_in, rider):
    t = x.shape[0]
    tm = min(TOKEN_TILE, t)

    def body(x_ref, vec_ref, w_ref, b_ref, z_ref, h_ref):
        xf = x_ref[...]
        r = lax.rsqrt(jnp.mean(xf * xf, axis=-1, keepdims=True) + EPS)
        h = (xf * r) * (vec_ref[0:1, :] * (1.0 + vec_ref[1:2, :])) + vec_ref[2:3, :]
        hb = h.astype(BF16)
        h_ref[...] = hb
        for mine, ref, width in IN_PIECES:
            zc = lax.dot_general(hb, w_ref[ref:ref + width, :], NT_DIMS, preferred_element_type=F32)
            z_ref[:, mine:mine + width] = (zc + b_ref[:, ref:ref + width]).astype(BF16)

    return _call(
        body, "inproj_fwd", (t // tm,), [x, vec, w_t, b_in],
        [pl.BlockSpec((tm, D_MODEL), lambda i: (i, 0)), _full((SUBLANES, D_MODEL)),
         _full((IN_WIDTH, D_MODEL)), _full((1, IN_WIDTH))],
        [jax.ShapeDtypeStruct((t, IN_WIDTH), BF16), jax.ShapeDtypeStruct((t, D_MODEL), BF16)],
        [pl.BlockSpec((tm, IN_WIDTH), lambda i: (i, 0)), pl.BlockSpec((tm, D_MODEL), lambda i: (i, 0))],
        rider=rider)


PAIRS = GROUP // 2
STACK = PAIRS * WINDOW


ATTN_BLOCKS = 4
ATTN_BWD_BLOCKS = 2
LOG2E = 1.4426950408889634
LN2 = 0.6931471805599453
SCORE_SCALE = ATTN_SCALE * LOG2E


def _fill_window_bias(bias_ref):
    shape = bias_ref.shape[1:]
    kj = lax.broadcasted_iota(jnp.int32, shape, 0)
    qi = jnp.bitwise_and(lax.broadcasted_iota(jnp.int32, shape, 1), WINDOW - 1)
    in_prev = jnp.logical_and(kj < WINDOW, kj > qi)
    in_cur = jnp.logical_and(kj >= WINDOW, (kj - WINDOW) <= qi)
    bias_ref[0] = jnp.where(in_cur, 0.0, -jnp.inf)
    bias_ref[1] = jnp.where(jnp.logical_or(in_prev, in_cur), 0.0, -jnp.inf)


def _half_tiles(tile):
    low = lax.broadcasted_iota(jnp.int32, tile.shape, 1) < HEAD_DIM
    swapped = jnp.concatenate([tile[:, HEAD_DIM:], tile[:, :HEAD_DIM]], axis=1)
    zero = jnp.zeros_like(tile)
    return ((jnp.where(low, tile, zero), jnp.where(low, zero, swapped)),
            (jnp.where(low, swapped, zero), jnp.where(low, zero, tile)))


def _stack_pairs(ref, row0, j):
    return jnp.concatenate(
        [ref[pl.ds(row0, WINDOW), (j * PAIRS + p) * LANES:(j * PAIRS + p + 1) * LANES] for p in range(PAIRS)], axis=0)


def _per_pair_row(values):
    pair = lax.broadcasted_iota(jnp.int32, (1, STACK), 1) // WINDOW
    row = jnp.full((1, STACK), values[PAIRS - 1], F32)
    for p in range(PAIRS - 2, -1, -1):
        row = jnp.where(pair == p, values[p], row)
    return row


def _attn_fwd(z, sinks, rider):
    t = z.shape[0]
    tq = min(TOKEN_TILE, t)
    nblk = tq // WINDOW

    def body(q_ref, kv_ref, sink_ref, o_ref, lse_ref, bias_ref):
        i = pl.program_id(0)

        @pl.when(i == 0)
        def _():
            _fill_window_bias(bias_ref)

        def window(b):
            row0 = pl.multiple_of(b * WINDOW, WINDOW)
            start = i * tq + b * WINDOW
            prev = pl.multiple_of(jnp.maximum(start - WINDOW, 0), WINDOW)
            cur = pl.multiple_of(start, WINDOW)
            kvw = jnp.concatenate([kv_ref[pl.ds(prev, WINDOW), :], kv_ref[pl.ds(cur, WINDOW), :]], axis=0)
            return row0, _half_tiles(kvw[:, :KV_WIDTH]), _half_tiles(kvw[:, KV_WIDTH:]), bias_ref[jnp.minimum(start, 1)]

        def block_group(bb, carry):
            windows = [window(bb * ATTN_BLOCKS + n) for n in range(ATTN_BLOCKS)]
            for j in range(N_KV_HEADS):
                for pr in range(PAIRS):
                    cols = slice((j * PAIRS + pr) * LANES, (j * PAIRS + pr + 1) * LANES)
                    o_ts = [jnp.zeros((LANES, WINDOW), F32) for _ in windows]
                    for parity in range(2):
                        h = j * GROUP + 2 * pr + parity
                        sink = sink_ref[h] * LOG2E
                        for n, (row0, k_halves, v_halves, bias) in enumerate(windows):
                            qp = q_ref[pl.ds(row0, WINDOW), cols]
                            s = lax.dot_general(k_halves[j][parity], qp, NT_DIMS, preferred_element_type=F32)
                            s = s * SCORE_SCALE + bias
                            m = jnp.maximum(jnp.max(s, axis=0, keepdims=True), sink)
                            p = jnp.exp2(s - m)
                            denom = jnp.sum(p, axis=0, keepdims=True) + jnp.exp2(sink - m)
                            pv = lax.dot_general(v_halves[j][parity], p.astype(BF16), TN_DIMS,
                                                 preferred_element_type=F32)
                            o_ts[n] = o_ts[n] + pv * (1.0 / denom)
                            lse_ref[h:h + 1, pl.ds(row0, WINDOW)] = m + jnp.log2(denom)
                    for n, (row0, _, _, _) in enumerate(windows):
                        o_ref[pl.ds(row0, WINDOW), cols] = jnp.transpose(o_ts[n].astype(BF16))
            return carry

        lax.fori_loop(0, nblk // ATTN_BLOCKS, block_group, 0)

    return _call(
        body, "attn_fwd", (t // tq,), [z, z, sinks],
        [pl.BlockSpec((tq, D_MODEL), lambda i: (i, 0)),
         pl.BlockSpec((t, 2 * KV_WIDTH), lambda i: (0, KV_COL // (2 * KV_WIDTH))),
         pl.BlockSpec(memory_space=pltpu.SMEM)],
        [jax.ShapeDtypeStruct((t, D_MODEL), BF16), jax.ShapeDtypeStruct((N_Q_HEADS, t), F32)],
        [pl.BlockSpec((tq, D_MODEL), lambda i: (i, 0)), pl.BlockSpec((N_Q_HEADS, tq), lambda i: (0, i))],
        scratch=[pltpu.VMEM((2, 2 * WINDOW, WINDOW), F32)], rider=rider)


HALO = BF16_ROWS


def _shift_down(u, uh, k):
    rolled = pltpu.roll(u, k, 0)
    row = lax.broadcasted_iota(jnp.int32, (SUBLANES, u.shape[1]), 0)
    top = rolled[:SUBLANES, :]
    for j in range(k):
        top = jnp.where(row == j, uh[HALO - k + j:HALO - k + j + 1, :], top)
    return jnp.concatenate([top, rolled[SUBLANES:, :]], axis=0)


def _shift_up(u, nxt, k):
    n = u.shape[0]
    rolled = pltpu.roll(u, n - k, 0)
    row = lax.broadcasted_iota(jnp.int32, (SUBLANES, u.shape[1]), 0)
    bottom = rolled[n - SUBLANES:, :]
    for j in range(k):
        bottom = jnp.where(row == SUBLANES - k + j, nxt[j:j + 1, :], bottom)
    return jnp.concatenate([rolled[:n - SUBLANES, :], bottom], axis=0)


def _conv_inputs(cc_ref, cx_ref, hc_ref, hx_ref, first_tile):
    cc = cc_ref[...].astype(F32)
    cx = cx_ref[...].astype(F32)
    u = cc * cx
    uh = jnp.where(first_tile, 0.0, hc_ref[...].astype(F32) * hx_ref[...].astype(F32))
    return cc, cx, u, _shift_down(u, uh, 1), _shift_down(u, uh, 2)


def _z_specs(tm, order):
    per_tile = tm // HALO
    cols = [pl.BlockSpec((tm, D_MODEL), functools.partial(lambda i, j: (order(i), j), j=j)) for j in range(1, 6)]
    halos = [pl.BlockSpec((HALO, D_MODEL),
                          functools.partial(lambda i, j: (jnp.maximum(order(i) * per_tile - 1, 0), j), j=j))
             for j in (2, 3)]
    return cols + halos


def _mix_fwd(x, attn, z, vec, w_out):
    t = x.shape[0]
    tm = min(TOKEN_TILE, t)

    def body(x_ref, a_ref, cb_ref, cc_ref, cx_ref, ga_ref, gc_ref, hc_ref, hx_ref, vec_ref, w_ref,
             m_ref, x2_ref, h2_ref, o_ref):
        i = pl.program_id(0)
        _, _, u, u1, u2 = _conv_inputs(cc_ref, cx_ref, hc_ref, hx_ref, i == 0)
        cv = vec_ref[4:5, :] * u2 + vec_ref[5:6, :] * u1 + vec_ref[6:7, :] * u
        conv = cb_ref[...].astype(F32) * cv
        merged = (_sigmoid(ga_ref[...].astype(F32)) * a_ref[...].astype(F32)
                  + _sigmoid(gc_ref[...].astype(F32)) * conv)
        mb = merged.astype(BF16)
        m_ref[...] = mb
        o = jnp.dot(mb, w_ref[...], preferred_element_type=F32)
        o_ref[...] = o.astype(BF16)
        x2 = x_ref[...] + vec_ref[0:1, :] * o
        x2_ref[...] = x2
        r = lax.rsqrt(jnp.mean(x2 * x2, axis=-1, keepdims=True) + EPS)
        h2 = (x2 * r) * (vec_ref[1:2, :] * (1.0 + vec_ref[2:3, :])) + vec_ref[3:4, :]
        h2_ref[...] = h2.astype(BF16)

    tok = pl.BlockSpec((tm, D_MODEL), lambda i: (i, 0))
    outs, _ = _call(
        body, "mix_fwd", (t // tm,), [x, attn, z, z, z, z, z, z, z, vec, w_out],
        [tok, tok] + _z_specs(tm, lambda i: i) + [_full((SUBLANES, D_MODEL)), _full((D_MODEL, D_MODEL))],
        [jax.ShapeDtypeStruct((t, D_MODEL), BF16), jax.ShapeDtypeStruct((t, D_MODEL), F32),
         jax.ShapeDtypeStruct((t, D_MODEL), BF16), jax.ShapeDtypeStruct((t, D_MODEL), BF16)],
        [tok, tok, tok, tok])
    return outs


def _ffn_fwd(h2, w_t):
    t = h2.shape[0]
    tm = min(TOKEN_TILE, t)

    def body(h_ref, w_ref, gu_ref, a_ref):
        hb = h_ref[...]
        for n in range(D_FF // FF_CHUNK):
            lo, hi = n * FF_CHUNK, (n + 1) * FF_CHUNK
            g = lax.dot_general(hb, w_ref[lo:hi, :], NT_DIMS, preferred_element_type=F32)
            u = lax.dot_general(hb, w_ref[D_FF + lo:D_FF + hi, :], NT_DIMS, preferred_element_type=F32)
            sg = _sigmoid(g)
            silu = g * sg
            gu_ref[:, lo:hi] = (u * (sg + silu * (1.0 - sg))).astype(BF16)
            gu_ref[:, D_FF + lo:D_FF + hi] = silu.astype(BF16)
            a_ref[:, lo:hi] = (silu * u).astype(BF16)

    outs, _ = _call(
        body, "ffn_fwd", (t // tm,), [h2, w_t],
        [pl.BlockSpec((tm, D_MODEL), lambda i: (i, 0)), _full((2 * D_FF, D_MODEL))],
        [jax.ShapeDtypeStruct((t, 2 * D_FF), BF16), jax.ShapeDtypeStruct((t, D_FF), BF16)],
        [pl.BlockSpec((tm, 2 * D_FF), lambda i: (i, 0)), pl.BlockSpec((tm, D_FF), lambda i: (i, 0))])
    return outs


def _ffn_out_loss(a, gu, x2, target, vec, w_ffn_out):
    t = a.shape[0]
    tm = min(TOKEN_TILE, t)

    def body(a_ref, gu_ref, x2_ref, t_ref, vec_ref, w_ref, dx3_ref, df_ref, dgu_ref, acc_ref):
        @pl.when(pl.program_id(0) == 0)
        def _():
            acc_ref[...] = jnp.zeros_like(acc_ref)

        ga2 = vec_ref[0:1, :]
        gf = vec_ref[1:2, :]
        parts = min(ROW_PARTS, tm // LANES)
        part_rows = [slice(n * (tm // parts), (n + 1) * (tm // parts)) for n in range(parts)]

        def head(rows, f):
            x3 = x2_ref[rows, :] + ga2 * f
            r = lax.rsqrt(jnp.mean(x3 * x3, axis=-1, keepdims=True) + EPS)
            xn = x3 * r
            err = xn * gf - t_ref[rows, :]
            dxn = err * (gf * (1.0 / D_MODEL))
            dx3 = r * (dxn - xn * jnp.mean(dxn * xn, axis=-1, keepdims=True))
            dx3_ref[rows, :] = dx3.astype(GRAD_STREAM)
            sums = (jnp.sum(err * err, axis=0, keepdims=True),
                    jnp.sum(err * xn, axis=0, keepdims=True) * (1.0 / D_MODEL),
                    jnp.sum(dx3 * f, axis=0, keepdims=True))
            df = (dx3 * ga2).astype(BF16)
            df_ref[rows, :] = df
            return df, sums

        def tail(rows, df):
            for n in range(D_FF // FF_CHUNK):
                lo, hi = n * FF_CHUNK, (n + 1) * FF_CHUNK
                da = lax.dot_general(df, w_ref[lo:hi, :], NT_DIMS, preferred_element_type=F32)
                dgu_ref[rows, lo:hi] = (da * gu_ref[rows, lo:hi].astype(F32)).astype(BF16)
                dgu_ref[rows, D_FF + lo:D_FF + hi] = (da * gu_ref[rows, D_FF + lo:D_FF + hi].astype(F32)).astype(BF16)

        fs = [jnp.dot(a_ref[rows, :], w_ref[...], preferred_element_type=F32) for rows in part_rows]
        heads = [head(rows, f) for rows, f in zip(part_rows, fs)]
        for rows, (df, _) in zip(part_rows, heads):
            tail(rows, df)
        for k in range(3):
            total = heads[0][1][k]
            for _, sums in heads[1:]:
                total = total + sums[k]
            acc_ref[k:k + 1, :] += total

    tok = pl.BlockSpec((tm, D_MODEL), lambda i: (i, 0))
    outs, _ = _call(
        body, "ffn_out_loss", (t // tm,), [a, gu, x2, target, vec, w_ffn_out],
        [pl.BlockSpec((tm, D_FF), lambda i: (i, 0)), pl.BlockSpec((tm, 2 * D_FF), lambda i: (i, 0)),
         tok, tok, _full((SUBLANES, D_MODEL)), _full((D_FF, D_MODEL))],
        [jax.ShapeDtypeStruct((t, D_MODEL), GRAD_STREAM), jax.ShapeDtypeStruct((t, D_MODEL), BF16),
         jax.ShapeDtypeStruct((t, 2 * D_FF), BF16), jax.ShapeDtypeStruct((SUBLANES, D_MODEL), F32)],
        [tok, tok, pl.BlockSpec((tm, 2 * D_FF), lambda i: (i, 0)), _full((SUBLANES, D_MODEL))])
    return outs


def _ffn_in_bwd(dgu, x2, dx3, vec, w_t, rider):
    t = x2.shape[0]
    tm = min(TOKEN_TILE, t)

    def body(dgu_ref, x2_ref, dx3_ref, vec_ref, wf_ref, dx2_ref, acc_ref):
        @pl.when(pl.program_id(0) == 0)
        def _():
            acc_ref[...] = jnp.zeros_like(acc_ref)

        gffn = vec_ref[0:1, :]
        sc2 = vec_ref[1:2, :]
        parts = min(ROW_PARTS, tm // LANES)
        part_rows = [slice(n * (tm // parts), (n + 1) * (tm // parts)) for n in range(parts)]
        dhs = [jnp.dot(dgu_ref[rows, :], wf_ref[...], preferred_element_type=F32) for rows in part_rows]
        gs = gffn * (1.0 + sc2)
        sum_dh = jnp.zeros((1, D_MODEL), F32)
        sum_dh_xn = jnp.zeros((1, D_MODEL), F32)
        for rows, dh2 in zip(part_rows, dhs):
            x2 = x2_ref[rows, :]
            r = lax.rsqrt(jnp.mean(x2 * x2, axis=-1, keepdims=True) + EPS)
            xn = x2 * r
            dh_xn = dh2 * xn
            sum_dh = sum_dh + jnp.sum(dh2, axis=0, keepdims=True)
            sum_dh_xn = sum_dh_xn + jnp.sum(dh_xn, axis=0, keepdims=True)
            dx2 = dx3_ref[rows, :].astype(F32) + r * (dh2 * gs - xn * jnp.mean(dh_xn * gs, axis=-1, keepdims=True))
            dx2_ref[rows, :] = dx2.astype(GRAD_STREAM)
        acc_ref[0:1, :] += sum_dh
        acc_ref[1:2, :] += sum_dh_xn * gffn
        acc_ref[2:3, :] += sum_dh_xn * (1.0 + sc2)

    tok = pl.BlockSpec((tm, D_MODEL), lambda i: (i, 0))
    return _call(
        body, "ffn_in_bwd", (t // tm,), [dgu, x2, dx3, vec, w_t],
        [pl.BlockSpec((tm, 2 * D_FF), lambda i: (i, 0)), tok, tok, _full((SUBLANES, D_MODEL)),
         _full((2 * D_FF, D_MODEL))],
        [jax.ShapeDtypeStruct((t, D_MODEL), GRAD_STREAM), jax.ShapeDtypeStruct((SUBLANES, D_MODEL), F32)],
        [tok, _full((SUBLANES, D_MODEL))], rider=rider)


def _mix_bwd(dx2, oproj, attn, z, vec, w_out, rider):
    t = dx2.shape[0]
    tm = min(TOKEN_TILE, t)
    nt = t // tm
    rev = lambda i: nt - 1 - i

    def body(dx2_ref, m_ref, a_ref, cb_ref, cc_ref, cx_ref, ga_ref, gc_ref, hc_ref, hx_ref,
             vec_ref, wo_ref, do_ref, da_ref, dr_ref, acc_ref, carry_ref):
        i = pl.program_id(0)

        @pl.when(i == 0)
        def _():
            acc_ref[...] = jnp.zeros_like(acc_ref)
            carry_ref[...] = jnp.zeros_like(carry_ref)

        ga1 = vec_ref[0:1, :]
        w0, w1, w2 = vec_ref[1:2, :], vec_ref[2:3, :], vec_ref[3:4, :]
        dx2 = dx2_ref[...].astype(F32)
        acc_ref[0:1, :] += jnp.sum(dx2 * m_ref[...].astype(F32), axis=0, keepdims=True)
        do = (dx2 * ga1).astype(BF16)
        do_ref[...] = do
        dm = lax.dot_general(do, wo_ref[...], NT_DIMS, preferred_element_type=F32)

        cc, cx, u, u1, u2 = _conv_inputs(cc_ref, cx_ref, hc_ref, hx_ref, i == nt - 1)
        cv = w0 * u2 + w1 * u1 + w2 * u
        cb = cb_ref[...].astype(F32)
        sa = _sigmoid(ga_ref[...].astype(F32))
        sc = _sigmoid(gc_ref[...].astype(F32))
        attn = a_ref[...].astype(F32)
        dattn = dm * sa
        da_ref[...] = dattn.astype(BF16)
        dconv = dm * sc
        dconv_b = dconv * cv
        dr_ref[:, 3 * D_MODEL:4 * D_MODEL] = (dattn * attn * (1.0 - sa)).astype(BF16)
        dr_ref[:, 4 * D_MODEL:5 * D_MODEL] = (dconv_b * cb * (1.0 - sc)).astype(BF16)
        dr_ref[:, 0:D_MODEL] = dconv_b.astype(BF16)
        dcv = dconv * cb
        acc_ref[1:2, :] += jnp.sum(dcv * u2, axis=0, keepdims=True)
        acc_ref[2:3, :] += jnp.sum(dcv * u1, axis=0, keepdims=True)
        acc_ref[3:4, :] += jnp.sum(dcv * u, axis=0, keepdims=True)
        nxt = carry_ref[...]
        du = w2 * dcv + w1 * _shift_up(dcv, nxt, 1) + w0 * _shift_up(dcv, nxt, 2)
        carry_ref[...] = dcv[0:SUBLANES, :]
        dr_ref[:, D_MODEL:2 * D_MODEL] = (du * cx).astype(BF16)
        dr_ref[:, 2 * D_MODEL:3 * D_MODEL] = (du * cc).astype(BF16)

    tok = pl.BlockSpec((tm, D_MODEL), lambda i: (rev(i), 0))
    return _call(
        body, "mix_bwd", (nt,), [dx2, oproj, attn, z, z, z, z, z, z, z, vec, w_out],
        [tok, tok, tok] + _z_specs(tm, rev) + [_full((SUBLANES, D_MODEL)), _full((D_MODEL, D_MODEL))],
        [jax.ShapeDtypeStruct((t, D_MODEL), BF16), jax.ShapeDtypeStruct((t, D_MODEL), BF16),
         jax.ShapeDtypeStruct((t, REST_WIDTH), BF16), jax.ShapeDtypeStruct((SUBLANES, D_MODEL), F32)],
        [tok, tok, pl.BlockSpec((tm, REST_WIDTH), lambda i: (rev(i), 0)), _full((SUBLANES, D_MODEL))],
        scratch=[pltpu.VMEM((SUBLANES, D_MODEL), F32)], rider=rider)


def _attn_bwd(z, dattn, attn, lse, sinks, rider):
    t = z.shape[0]
    tq = min(TOKEN_TILE, t)
    nblk = tq // WINDOW
    nt = t // tq

    def body(q_ref, kv_ref, do_ref, o_ref, lse_ref, sink_ref, dq_ref, dkv_ref, ds_ref, acc_ref, bias_ref):
        i = pl.program_id(0)

        @pl.when(i == 0)
        def _():
            acc_ref[...] = jnp.zeros_like(acc_ref)
            ds_ref[...] = jnp.zeros_like(ds_ref)
            _fill_window_bias(bias_ref)

        lane = lax.broadcasted_iota(jnp.int32, (1, LANES), 1)
        ind_row = lax.broadcasted_iota(jnp.int32, (SUBLANES, LANES), 0)
        ind_low = lax.broadcasted_iota(jnp.int32, (SUBLANES, LANES), 1) < HEAD_DIM
        indicator = jnp.where(jnp.logical_or(jnp.logical_and(ind_row == 0, ind_low),
                                             jnp.logical_and(ind_row == 1, jnp.logical_not(ind_low))),
                              1.0, 0.0).astype(BF16)
        low = lax.broadcasted_iota(jnp.int32, (2 * WINDOW, LANES), 1) < HEAD_DIM

        def both_heads(even, odd):
            picked = jnp.where(low, even, odd)
            return picked + jnp.concatenate([picked[:, HEAD_DIM:], picked[:, :HEAD_DIM]], axis=1)

        def window(b):
            row0 = pl.multiple_of(b * WINDOW, WINDOW)
            start = i * tq + b * WINDOW
            prev = pl.multiple_of(jnp.maximum(start - WINDOW, 0), WINDOW)
            cur = pl.multiple_of(start, WINDOW)
            kvw = jnp.concatenate([kv_ref[pl.ds(prev, WINDOW), :], kv_ref[pl.ds(cur, WINDOW), :]], axis=0)
            return (row0, prev, cur, _half_tiles(kvw[:, :KV_WIDTH]), _half_tiles(kvw[:, KV_WIDTH:]),
                    bias_ref[jnp.minimum(start, 1)])

        def block_group(bb, dsink):
            windows = [window(bb * ATTN_BWD_BLOCKS + n) for n in range(ATTN_BWD_BLOCKS)]
            dk_groups = [[] for _ in windows]
            dv_groups = [[] for _ in windows]
            for j in range(N_KV_HEADS):
                stacks, deltas, dq_ts = [], [], []
                for row0, _, _, _, _, _ in windows:
                    qst = _stack_pairs(q_ref, row0, j)
                    dost = _stack_pairs(do_ref, row0, j)
                    prod = dost.astype(F32) * _stack_pairs(o_ref, row0, j).astype(F32)
                    prod_hi = prod.astype(BF16)
                    prod_lo = (prod - prod_hi.astype(F32)).astype(BF16)
                    stacks.append((qst, dost))
                    deltas.append(lax.dot_general(indicator, prod_hi, NT_DIMS, preferred_element_type=F32)
                                  + lax.dot_general(indicator, prod_lo, NT_DIMS, preferred_element_type=F32))
                    dq_ts.append(jnp.zeros((LANES, STACK), F32))
                dk_par = [[] for _ in windows]
                dv_par = [[] for _ in windows]
                for parity in range(2):
                    heads = [j * GROUP + 2 * p + parity for p in range(PAIRS)]
                    sink = _per_pair_row([sink_ref[h] * LOG2E for h in heads])
                    for n, (row0, _, _, k_halves, v_halves, bias) in enumerate(windows):
                        qst, dost = stacks[n]
                        kk, vv = k_halves[j][parity], v_halves[j][parity]
                        s = lax.dot_general(kk, qst, NT_DIMS, preferred_element_type=F32) * SCORE_SCALE + bias
                        lse = jnp.concatenate([lse_ref[h:h + 1, pl.ds(row0, WINDOW)] for h in heads], axis=1)
                        p = jnp.exp2(s - lse)
                        dp = lax.dot_general(vv, dost, NT_DIMS, preferred_element_type=F32)
                        delta = deltas[n][parity:parity + 1, :]
                        dsb = (p * (dp - delta)).astype(BF16)
                        dq_ts[n] = dq_ts[n] + lax.dot_general(kk, dsb, TN_DIMS, preferred_element_type=F32)
                        dk_par[n].append(jnp.dot(dsb, qst, preferred_element_type=F32))
                        dv_par[n].append(jnp.dot(p.astype(BF16), dost, preferred_element_type=F32))
                        weighted = jnp.exp2(sink - lse) * delta
                        for pr, h in enumerate(heads):
                            dsink = dsink - jnp.where(
                                lane == h, jnp.sum(weighted[:, pr * WINDOW:(pr + 1) * WINDOW]), 0.0)
                for n, (row0, _, _, _, _, _) in enumerate(windows):
                    dq_st = jnp.transpose((dq_ts[n] * ATTN_SCALE).astype(BF16))
                    for pr in range(PAIRS):
                        dq_ref[pl.ds(row0, WINDOW), (j * PAIRS + pr) * LANES:(j * PAIRS + pr + 1) * LANES] = (
                            dq_st[pr * WINDOW:(pr + 1) * WINDOW, :])
                    dk_groups[n].append(both_heads(dk_par[n][0], dk_par[n][1]))
                    dv_groups[n].append(both_heads(dv_par[n][0], dv_par[n][1]))
            for n, (_, prev, cur, _, _, _) in enumerate(windows):
                blk = jnp.concatenate([jnp.where(low, dk_groups[n][0], dk_groups[n][1]) * ATTN_SCALE,
                                       jnp.where(low, dv_groups[n][0], dv_groups[n][1])], axis=1)
                acc_ref[pl.ds(prev, WINDOW), :] += blk[:WINDOW, :]
                acc_ref[pl.ds(cur, WINDOW), :] += blk[WINDOW:, :]
            return dsink

        dsink = lax.fori_loop(0, nblk // ATTN_BWD_BLOCKS, block_group, jnp.zeros((1, LANES), F32))
        ds_ref[0:1, :] += dsink

        @pl.when(i == nt - 1)
        def _():
            dkv_ref[...] = acc_ref[...].astype(BF16)

    tok = pl.BlockSpec((tq, D_MODEL), lambda i: (i, 0))
    return _call(
        body, "attn_bwd", (nt,), [z, z, dattn, attn, lse, sinks],
        [tok, pl.BlockSpec((t, 2 * KV_WIDTH), lambda i: (0, KV_COL // (2 * KV_WIDTH))), tok, tok,
         pl.BlockSpec((N_Q_HEADS, tq), lambda i: (0, i)), pl.BlockSpec(memory_space=pltpu.SMEM)],
        [jax.ShapeDtypeStruct((t, D_MODEL), BF16), jax.ShapeDtypeStruct((t, 2 * KV_WIDTH), BF16),
         jax.ShapeDtypeStruct((SUBLANES, LANES), F32)],
        [tok, _full((t, 2 * KV_WIDTH)), _full((SUBLANES, LANES))],
        scratch=[pltpu.VMEM((t, 2 * KV_WIDTH), F32), pltpu.VMEM((2, 2 * WINDOW, STACK), F32)], rider=rider)


def _inproj_bwd(dq, drest, dkv, x, dx2, vec, w_t, rider):
    t = x.shape[0]
    tm = min(TOKEN_TILE, t)

    def body(dq_ref, dr_ref, dkv_ref, x_ref, dx2_ref, vec_ref, w_ref, gx_ref, acc_ref, db_ref):
        @pl.when(pl.program_id(0) == 0)
        def _():
            acc_ref[...] = jnp.zeros_like(acc_ref)
            db_ref[...] = jnp.zeros_like(db_ref)

        g = vec_ref[0:1, :]
        sc1 = vec_ref[1:2, :]
        dqb, drb, dkvb = dq_ref[...], dr_ref[...], dkv_ref[...]
        dh = jnp.dot(dqb, w_ref[:REF_KV_COL, :], preferred_element_type=F32)
        dh = dh + jnp.dot(drb, w_ref[REF_REST_COL:, :], preferred_element_type=F32)
        dh = dh + jnp.dot(dkvb, w_ref[REF_KV_COL:REF_REST_COL, :], preferred_element_type=F32)
        db_ref[:, :REF_KV_COL] += jnp.sum(dqb.astype(F32), axis=0, keepdims=True)
        db_ref[:, REF_REST_COL:] += jnp.sum(drb.astype(F32), axis=0, keepdims=True)
        db_ref[:, REF_KV_COL:REF_REST_COL] += jnp.sum(dkvb.astype(F32), axis=0, keepdims=True)
        xf = x_ref[...]
        r = lax.rsqrt(jnp.mean(xf * xf, axis=-1, keepdims=True) + EPS)
        xn = xf * r
        gs = g * (1.0 + sc1)
        dh_xn = dh * xn
        sum_dh_xn = jnp.sum(dh_xn, axis=0, keepdims=True)
        acc_ref[0:1, :] += jnp.sum(dh, axis=0, keepdims=True)
        acc_ref[1:2, :] += sum_dh_xn * g
        acc_ref[2:3, :] += sum_dh_xn * (1.0 + sc1)
        gx_ref[...] = dx2_ref[...].astype(F32) + r * (dh * gs - xn * jnp.mean(dh_xn * gs, axis=-1, keepdims=True))

    tok = pl.BlockSpec((tm, D_MODEL), lambda i: (i, 0))
    return _call(
        body, "inproj_bwd", (t // tm,), [dq, drest, dkv, x, dx2, vec, w_t],
        [tok, pl.BlockSpec((tm, REST_WIDTH), lambda i: (i, 0)),
         pl.BlockSpec((tm, 2 * KV_WIDTH), lambda i: (i, 0)), tok, tok,
         _full((SUBLANES, D_MODEL)), _full((IN_WIDTH, D_MODEL))],
        [jax.ShapeDtypeStruct((t, D_MODEL), F32), jax.ShapeDtypeStruct((SUBLANES, D_MODEL), F32),
         jax.ShapeDtypeStruct((1, IN_WIDTH), F32)],
        [tok, _full((SUBLANES, D_MODEL)), _full((1, IN_WIDTH))], rider=rider)


def _weight_grad(b, a, name, bn, rows=None, row0=0, into=None, rider=None):
    pieces = list(b) if isinstance(b, (list, tuple)) else [b]
    widths = [p.shape[1] for p in pieces]
    t, n = pieces[0].shape[0], sum(widths)
    assert len(pieces) == 1 or bn == n
    m = a.shape[1]
    rows = n if rows is None else rows
    tk = min(TOKEN_TILE, t)
    for cand in (4 * TOKEN_TILE, 2 * TOKEN_TILE):
        if t % cand == 0 and 2 * cand * (bn + m) * 2 + bn * m * 4 <= WGRAD_VMEM:
            tk = cand
            break
    nk = t // tk
    block0 = row0 // bn

    npieces = len(pieces)

    def body(*refs):
        b_refs, a_ref = refs[:npieces], refs[npieces]
        out_ref, acc_ref = refs[-2:]
        k = pl.program_id(1)

        @pl.when(k == 0)
        def _():
            acc_ref[...] = jnp.zeros_like(acc_ref)

        if npieces == 1:
            acc_ref[...] += lax.dot_general(b_refs[0][...], a_ref[...], TN_DIMS, preferred_element_type=F32)
        else:
            lo = 0
            for b_ref, width in zip(b_refs, widths):
                acc_ref[lo:lo + width, :] += lax.dot_general(b_ref[...], a_ref[...], TN_DIMS,
                                                             preferred_element_type=F32)
                lo += width

        @pl.when(k == nk - 1)
        def _():
            out_ref[...] = acc_ref[...].astype(BF16)

    if npieces == 1:
        b_specs = [pl.BlockSpec((tk, bn), lambda j, k: (k, j))]
    else:
        b_specs = [pl.BlockSpec((tk, width), lambda j, k: (k, 0)) for width in widths]
    outs, routs = _call(
        body, name, (n // bn, nk), pieces + [a] + ([] if into is None else [into]),
        b_specs + [pl.BlockSpec((tk, m), lambda j, k: (k, 0))] + ([] if into is None else [ANY]),
        [jax.ShapeDtypeStruct((rows, m), BF16)], [pl.BlockSpec((bn, m), lambda j, k: (block0 + j, 0))],
        scratch=[pltpu.VMEM((bn, m), F32)], rider=rider, aliases=None if into is None else {npieces + 1: 0})
    return outs[0], routs


def _weight_grad_in(dq, dkv, drest, h1):
    t = h1.shape[0]
    tk = min(4 * TOKEN_TILE, t)
    nk = t // tk
    nblocks = IN_WIDTH // IN_CHUNK

    def body(dq_ref, dkv_ref, dr_ref, a_ref, out_ref, acc_ref):
        j, k = pl.program_id(0), pl.program_id(1)

        @pl.when(k == 0)
        def _():
            acc_ref[...] = jnp.zeros_like(acc_ref)

        @pl.when(j == 0)
        def _():
            acc_ref[:D_MODEL, :] += lax.dot_general(dq_ref[...], a_ref[...], TN_DIMS, preferred_element_type=F32)
            acc_ref[D_MODEL:, :] += lax.dot_general(dkv_ref[...], a_ref[...], TN_DIMS, preferred_element_type=F32)

        @pl.when(j > 0)
        def _():
            acc_ref[...] += lax.dot_general(dr_ref[...], a_ref[...], TN_DIMS, preferred_element_type=F32)

        @pl.when(k == nk - 1)
        def _():
            out_ref[...] = acc_ref[...].astype(BF16)

    first = lambda j, k: (jnp.where(j == 0, k, 0), 0)
    outs, _ = _call(
        body, "wgrad_in", (nblocks, nk), [dq, dkv, drest, h1],
        [pl.BlockSpec((tk, D_MODEL), first), pl.BlockSpec((tk, 2 * KV_WIDTH), first),
         pl.BlockSpec((tk, IN_CHUNK), lambda j, k: (jnp.where(j == 0, 0, k), jnp.maximum(j - 1, 0))),
         pl.BlockSpec((tk, D_MODEL), lambda j, k: (k, 0))],
        [jax.ShapeDtypeStruct((IN_WIDTH, D_MODEL), BF16)], [pl.BlockSpec((IN_CHUNK, D_MODEL), lambda j, k: (j, 0))],
        scratch=[pltpu.VMEM((IN_CHUNK, D_MODEL), F32)])
    return outs[0]


def _to_rows(v):
    n = v.shape[0]
    padded = -(-n // (SUBLANES * LANES)) * SUBLANES * LANES
    return jnp.pad(v, (0, padded - n)).reshape(padded // LANES, LANES)


def _vec_rows(*rows):
    stacked = jnp.concatenate([r.reshape(1, D_MODEL) for r in rows], axis=0)
    return jnp.pad(stacked, ((0, SUBLANES - len(rows)), (0, 0)))


def kernel(x, c, w_ada, b_ada, g_mix, w_in, b_in, sinks, conv_w, w_out, g_ffn, w_ffn_in, w_ffn_out, g_final, loss_target, m_w_ada, m_b_ada, m_g_mix, m_w_in, m_b_in, m_sinks, m_conv_w, m_w_out, m_g_ffn, m_w_ffn_in, m_w_ffn_out, m_g_final, v_w_ada, v_b_ada, v_g_mix, v_w_in, v_b_in, v_sinks, v_conv_w, v_w_out, v_g_ffn, v_w_ffn_in, v_w_ffn_out, v_g_final):
    ix, iy, ic = _my_place()
    me = 4 * ix + 2 * iy + ic
    xs = x[0]
    target = loss_target[0]
    ada_cols = w_ada.shape[2]
    conv_cols = conv_w.shape[2]

    wt_in, wt_fi = jnp.transpose(w_in[0]), jnp.transpose(w_ffn_in[0])
    b_cols = lax.dynamic_slice_in_dim(b_ada, me * ada_cols, ada_cols, axis=1)
    g_in, (cast_fi, cast_out, cast_fo), first, mod_all = _gather_first_weight(
        wt_in, [wt_fi, w_out[0], w_ffn_out[0]], _to_rows(jnp.concatenate([c[0], conv_w[0].reshape(-1)])),
        w_ada[0], b_cols)
    first = first.reshape(N_DEV, -1)
    c_all = first[:, :D_MODEL]
    conv_full = jnp.transpose(first[:, D_MODEL:D_MODEL + 3 * conv_cols].reshape(N_DEV, 3, conv_cols), (1, 0, 2))
    conv_full = conv_full.reshape(3, D_MODEL)
    mod = lax.dynamic_index_in_dim(mod_all, me, axis=1, keepdims=False).reshape(N_MOD, D_MODEL)
    sh1, sc1, ga1, sh2, sc2, ga2 = [mod[i:i + 1] for i in range(N_MOD)]
    w_in_t = g_in.reshape(IN_WIDTH, D_MODEL)
    (z, h1), (g_fi, g_out) = _inproj_fwd(xs, _vec_rows(g_mix, sc1, sh1), w_in_t, b_in,
                                         _gather_rider([cast_fi, cast_out]))
    w_fi_t = g_fi.reshape(2 * D_FF, D_MODEL)
    w_out_full = g_out.reshape(D_MODEL, D_MODEL)
    (attn, lse), (g_fo,) = _attn_fwd(z, sinks[0], _gather_rider([cast_fo]))
    w_fo_full = g_fo.reshape(D_FF, D_MODEL)
    merged, x2, h2, oproj = _mix_fwd(
        xs, attn, z, _vec_rows(ga1, g_ffn, sc2, sh2, conv_full[0], conv_full[1], conv_full[2]), w_out_full)
    gu, act = _ffn_fwd(h2, w_fi_t)
    dx3, df, dgu, acc_l = _ffn_out_loss(act, gu, x2, target, _vec_rows(ga2, g_final), w_fo_full)

    gw_fo, _ = _weight_grad(act, df, "wgrad_ffn_out", D_FF)
    gw_fi, _ = _weight_grad(dgu, h2, "wgrad_ffn_in", D_FF)
    blocks_fo = gw_fo.reshape(N_DEV, D_FF // N_DEV, D_MODEL)
    blocks_fi = gw_fi.reshape(N_DEV, 2 * D_FF // N_DEV, D_MODEL)
    (dx2, acc_f), (sib_fo, sib_fi) = _ffn_in_bwd(dgu, x2, dx3, _vec_rows(g_ffn, sc2), w_fi_t,
                                                 _sibling_rider([blocks_fo, blocks_fi]))
    sums_fo, mine_fo = _sibling_sum(blocks_fo, sib_fo, "sibling_sum_ffn_out")
    sums_fi, mine_fi = _sibling_sum(blocks_fi, sib_fi, "sibling_sum_ffn_in")
    (dout, dattn, drest, acc_m), (ici_fo, ici_fi) = _mix_bwd(
        dx2, oproj, attn, z, _vec_rows(ga1, conv_full[0], conv_full[1], conv_full[2]), w_out_full,
        _chip_rider([sums_fo, sums_fi]))
    gw_out, _ = _weight_grad(merged, dout, "wgrad_out", D_MODEL)
    blocks_out = gw_out.reshape(N_DEV, D_MODEL // N_DEV, D_MODEL)
    (dq, dkv, dsink), _ = _attn_bwd(z, dattn, attn, lse, sinks[0], None)
    blocks_in = _weight_grad_in(dq, dkv, drest, h1).reshape(N_DEV, IN_WIDTH // N_DEV, D_MODEL)
    (sums_in, mine_in), (sums_out, mine_out) = _sibling_exchange_sum([blocks_in, blocks_out], "sibling_w_in_out")
    (grad_x, acc_i, db_in), (ici_in, ici_out) = _inproj_bwd(dq, drest, dkv, xs, dx2, _vec_rows(g_mix, sc1), w_in_t,
                                                            _chip_rider([sums_in, sums_out]))

    widen = lambda vec: jnp.pad(vec, (0, -vec.shape[0] % D_MODEL))
    packed = jnp.concatenate([
        acc_i[0], acc_i[1], acc_m[0], acc_f[0], acc_f[1], acc_l[2],
        acc_i[2], widen(db_in[0]), acc_f[2], acc_l[1],
        acc_m[1], acc_m[2], acc_m[3], widen(dsink[0]), acc_l[0],
        jnp.zeros(((PACK_ROWS - PACK_SQERR - 1) * D_MODEL,), F32)]).reshape(PACK_ROWS, D_MODEL)
    packed_all = _small_allgather(packed, "gather_small")
    dmod_all = packed_all[:, PACK_DMOD:PACK_DMOD + N_MOD, :].reshape(N_DEV, N_MOD * D_MODEL)
    dmod_cols = lax.dynamic_slice_in_dim(dmod_all, me * ada_cols, ada_cols, axis=1)
    g_w_ada = _ada_weight_grad(c_all, dmod_cols)
    row_of = lambda a: a.reshape(1, -1)
    small, g_conv_full, loss = _small_finalize(packed_all, {
        "b_ada": (b_ada, m_b_ada, v_b_ada), "g_mix": (g_mix, m_g_mix, v_g_mix), "b_in": (b_in, m_b_in, v_b_in),
        "g_ffn": (g_ffn, m_g_ffn, v_g_ffn), "sinks": (sinks, m_sinks, v_sinks),
        "g_final": (row_of(g_final), row_of(m_g_final), row_of(v_g_final))})
    small["g_final"] = tuple(o.reshape(g_final.shape) for o in small["g_final"])
    g_conv = lax.dynamic_slice_in_dim(g_conv_full, me * conv_cols, conv_cols, axis=1)
    d_conv, nm_conv, nv_conv = _adamw(conv_w[0], g_conv, m_conv_w[0], v_conv_w[0], "adamw_conv_w")
    small["conv_w"] = (g_conv[None], d_conv[None], nm_conv[None], nv_conv[None])

    def reduced(mine, ici, w, m, v, name, transposed=False):
        turn = jnp.transpose if transposed else (lambda a: a)
        return tuple(turn(o)[None] for o in _chip_sum_adamw(mine, ici, turn(w[0]), turn(m[0]), turn(v[0]), name))

    d_ada, nm_ada, nv_ada = _adamw(w_ada[0], g_w_ada, m_w_ada[0], v_w_ada[0], "adamw_w_ada")
    res = {
        "w_ada": (g_w_ada[None], d_ada[None], nm_ada[None], nv_ada[None]),
        "w_in": reduced(mine_in, ici_in, w_in, m_w_in, v_w_in, "adamw_w_in", transposed=True),
        "w_out": reduced(mine_out, ici_out, w_out, m_w_out, v_w_out, "adamw_w_out"),
        "w_ffn_in": reduced(mine_fi, ici_fi, w_ffn_in, m_w_ffn_in, v_w_ffn_in, "adamw_w_ffn_in", transposed=True),
        "w_ffn_out": reduced(mine_fo, ici_fo, w_ffn_out, m_w_ffn_out, v_w_ffn_out, "adamw_w_ffn_out"),
    }
    res.update(small)
    order = ["w_ada", "b_ada", "g_mix", "w_in", "b_in", "sinks", "conv_w", "w_out", "g_ffn", "w_ffn_in", "w_ffn_out",
             "g_final"]
    outs = [loss.reshape(()), grad_x[None]]
    for k in range(4):
        outs += [res[n][k] for n in order]
    return tuple(outs)
```
